```python
import math
import jax, jax.numpy as jnp
from jax import lax
import numpy as np

D_MODEL = 1024
BATCH = 8
SEQ = 4096
DEPTH = 1

CHUNK = 64
MIX_W = D_MODEL
FOX_W = MIX_W // 2
LRU_W = MIX_W - FOX_W
FOX_HEADS = 8
FOX_HD = FOX_W // FOX_HEADS
LRU_BLOCKS = 8
LRU_BW = LRU_W // LRU_BLOCKS
LRU_C = 8.0
CONV_K = 4
D_FF = 4 * D_MODEL
Q_BLOCK = 128
LN_EPS = 1e-5
DN_ALPHA = (2.0 * DEPTH) ** 0.25
DN_BETA = (8.0 * DEPTH) ** -0.25

Q_OFF = 0
K_OFF = Q_OFF + FOX_W
V_OFF = K_OFF + FOX_W
LX_OFF = V_OFF + FOX_W
LG_OFF = LX_OFF + LRU_W
FG_OFF = LG_OFF + LRU_W
IN_COLS = FG_OFF + FOX_HEADS

kernel_name = "fox_rglru_macaron_deepnorm_block"


def layer_norm(x, g, b):
    xf = x.astype(jnp.float32)
    mu = jnp.mean(xf, axis=-1, keepdims=True)
    var = jnp.mean(jnp.square(xf - mu), axis=-1, keepdims=True)
    y = (xf - mu) * lax.rsqrt(var + LN_EPS) * g.astype(jnp.float32) + b.astype(jnp.float32)
    return y.astype(x.dtype)


def swiglu(x, w_gate, w_up, w_down):
    return (jax.nn.silu(x @ w_gate) * (x @ w_up)) @ w_down


def forgetting_attention(q, k, v, fg_logit):
    seq = q.shape[1]
    scale = 1.0 / math.sqrt(FOX_HD)
    cum = jnp.cumsum(jax.nn.log_sigmoid(fg_logit.astype(jnp.float32)), axis=1)
    cum = cum.transpose(0, 2, 1)
    qh = q.transpose(0, 2, 1, 3)
    kh = k.transpose(0, 2, 1, 3)
    vh = v.transpose(0, 2, 1, 3)
    outs = []
    for i in range(seq // Q_BLOCK):
        q0, q1 = i * Q_BLOCK, (i + 1) * Q_BLOCK
        s = jnp.einsum('bhqd,bhkd->bhqk', qh[:, :, q0:q1], kh[:, :, :q1],
                       preferred_element_type=jnp.float32) * scale
        s = s + cum[:, :, q0:q1, None] - cum[:, :, None, :q1]
        mask = jnp.arange(q0, q1)[:, None] >= jnp.arange(q1)[None, :]
        s = jnp.where(mask, s, -1e30)
        p = jax.nn.softmax(s, axis=-1).astype(vh.dtype)
        outs.append(jnp.einsum('bhqk,bhkd->bqhd', p, vh[:, :, :q1]))
    return jnp.concatenate(outs, axis=1)


def causal_depthwise_conv(u, w, b):
    y = lax.conv_general_dilated(u, w[:, None, :], window_strides=(1,), padding=[(CONV_K - 1, 0)],
                                 dimension_numbers=('NWC', 'WIO', 'NWC'),
                                 feature_group_count=u.shape[-1])
    return y + b


def _lin_rec_combine(c1, c2):
    a1, b1 = c1
    a2, b2 = c2
    return a1 * a2, a2 * b1 + b2


def rg_lru(u, wa, ba, wx, bx, lam):
    bsz, seq, width = u.shape
    ub = u.reshape(bsz, seq, LRU_BLOCKS, LRU_BW)
    r = jax.nn.sigmoid(jnp.einsum('bshi,hij->bshj', ub, wa) + ba).reshape(bsz, seq, width)
    gi = jax.nn.sigmoid(jnp.einsum('bshi,hij->bshj', ub, wx) + bx).reshape(bsz, seq, width)
    log_a = -LRU_C * r.astype(jnp.float32) * jax.nn.softplus(-lam.astype(jnp.float32))
    a = jnp.exp(log_a)
    bterm = jnp.sqrt(-jnp.expm1(2.0 * log_a)) * (gi * u).astype(jnp.float32)
    _, h = lax.associative_scan(_lin_rec_combine, (a, bterm), axis=1)
    return h.astype(u.dtype)


def _fwd_setup_inputs(seed: int = 0) -> dict:
    key = jax.random.key(seed)
    ks = iter(jax.random.split(key, 32))
    f32 = jnp.float32

    def nrm(shape, scale):
        return jax.random.normal(next(ks), shape, f32) * scale

    d_in, d_ff = D_MODEL ** -0.5, D_FF ** -0.5
    x = jax.random.normal(next(ks), (BATCH, SEQ, D_MODEL), f32)
    w_in = nrm((DEPTH, D_MODEL, IN_COLS), d_in)
    w_in = w_in.at[:, :, V_OFF:V_OFF + FOX_W].multiply(DN_BETA)
    a0 = jax.random.uniform(next(ks), (DEPTH, LRU_W), f32, 0.9, 0.999)
    p = a0 ** (1.0 / LRU_C)
    lru_lambda = jnp.log(p) - jnp.log1p(-p)
    return {
        "x": x,
        "ffn1_w_gate": nrm((DEPTH, D_MODEL, D_FF), d_in),
        "ffn1_w_up": nrm((DEPTH, D_MODEL, D_FF), d_in),
        "ffn1_w_down": nrm((DEPTH, D_FF, D_MODEL), d_ff * DN_BETA),
        "ln1_g": 1.0 + nrm((DEPTH, D_MODEL), 0.02),
        "ln1_b": nrm((DEPTH, D_MODEL), 0.02),
        "w_in": w_in,
        "b_forget": 3.0 + nrm((DEPTH, FOX_HEADS), 0.1),
        "conv_w": nrm((DEPTH, CONV_K, LRU_W), CONV_K ** -0.5),
        "conv_b": nrm((DEPTH, LRU_W), 0.02),
        "rg_wa": nrm((DEPTH, LRU_BLOCKS, LRU_BW, LRU_BW), LRU_BW ** -0.5),
        "rg_ba": nrm((DEPTH, LRU_BLOCKS, LRU_BW), 0.02),
        "rg_wx": nrm((DEPTH, LRU_BLOCKS, LRU_BW, LRU_BW), LRU_BW ** -0.5),
        "rg_bx": nrm((DEPTH, LRU_BLOCKS, LRU_BW), 0.02),
        "lru_lambda": lru_lambda,
        "w_out": nrm((DEPTH, MIX_W, D_MODEL), (MIX_W ** -0.5) * DN_BETA),
        "ln2_g": 1.0 + nrm((DEPTH, D_MODEL), 0.02),
        "ln2_b": nrm((DEPTH, D_MODEL), 0.02),
        "ffn2_w_gate": nrm((DEPTH, D_MODEL, D_FF), d_in),
        "ffn2_w_up": nrm((DEPTH, D_MODEL, D_FF), d_in),
        "ffn2_w_down": nrm((DEPTH, D_FF, D_MODEL), d_ff * DN_BETA),
        "ln3_g": 1.0 + nrm((DEPTH, D_MODEL), 0.02),
        "ln3_b": nrm((DEPTH, D_MODEL), 0.02),
    }


def _fwd_reference(x, ffn1_w_gate, ffn1_w_up, ffn1_w_down, ln1_g, ln1_b, w_in, b_forget,
              conv_w, conv_b, rg_wa, rg_ba, rg_wx, rg_bx, lru_lambda, w_out,
              ln2_g, ln2_b, ffn2_w_gate, ffn2_w_up, ffn2_w_down, ln3_g, ln3_b):
    bsz, seq, _ = x.shape
    for l in range(DEPTH):
        x = layer_norm(DN_ALPHA * x + 0.5 * swiglu(x, ffn1_w_gate[l], ffn1_w_up[l], ffn1_w_down[l]),
                       ln1_g[l], ln1_b[l])
        z = x @ w_in[l]
        q = z[..., Q_OFF:Q_OFF + FOX_W].reshape(bsz, seq, FOX_HEADS, FOX_HD)
        k = z[..., K_OFF:K_OFF + FOX_W].reshape(bsz, seq, FOX_HEADS, FOX_HD)
        v = z[..., V_OFF:V_OFF + FOX_W].reshape(bsz, seq, FOX_HEADS, FOX_HD)
        fg = z[..., FG_OFF:FG_OFF + FOX_HEADS] + b_forget[l]
        fox = forgetting_attention(q, k, v, fg).reshape(bsz, seq, FOX_W)
        u = causal_depthwise_conv(z[..., LX_OFF:LX_OFF + LRU_W], conv_w[l], conv_b[l])
        rec = rg_lru(u, rg_wa[l], rg_ba[l], rg_wx[l], rg_bx[l], lru_lambda[l])
        lru = jax.nn.gelu(z[..., LG_OFF:LG_OFF + LRU_W]) * rec
        mix = jnp.concatenate([fox, lru], axis=-1) @ w_out[l]
        x = layer_norm(DN_ALPHA * x + mix, ln2_g[l], ln2_b[l])
        x = layer_norm(DN_ALPHA * x + 0.5 * swiglu(x, ffn2_w_gate[l], ffn2_w_up[l], ffn2_w_down[l]),
                       ln3_g[l], ln3_b[l])
    return x


import jax as _jax
import jax.numpy as _jnp

TWIN_FORMAT = 'train_step'
FWD_PARAMS = ['x', 'ffn1_w_gate', 'ffn1_w_up', 'ffn1_w_down', 'ln1_g', 'ln1_b', 'w_in', 'b_forget', 'conv_w', 'conv_b', 'rg_wa', 'rg_ba', 'rg_wx', 'rg_bx', 'lru_lambda', 'w_out', 'ln2_g', 'ln2_b', 'ffn2_w_gate', 'ffn2_w_up', 'ffn2_w_down', 'ln3_g', 'ln3_b']
TWIN_WEIGHTS = ['ffn1_w_gate', 'ffn1_w_up', 'ffn1_w_down', 'ln1_g', 'ln1_b', 'w_in', 'b_forget', 'conv_w', 'conv_b', 'rg_wa', 'rg_ba', 'rg_wx', 'rg_bx', 'lru_lambda', 'w_out', 'ln2_g', 'ln2_b', 'ffn2_w_gate', 'ffn2_w_up', 'ffn2_w_down', 'ln3_g', 'ln3_b']
TWIN_DIFF_INPUT = 'x'
TWIN_INPUTS = ['x', 'ffn1_w_gate', 'ffn1_w_up', 'ffn1_w_down', 'ln1_g', 'ln1_b', 'w_in', 'b_forget', 'conv_w', 'conv_b', 'rg_wa', 'rg_ba', 'rg_wx', 'rg_bx', 'lru_lambda', 'w_out', 'ln2_g', 'ln2_b', 'ffn2_w_gate', 'ffn2_w_up', 'ffn2_w_down', 'ln3_g', 'ln3_b', 'loss_target', 'm_ffn1_w_gate', 'm_ffn1_w_up', 'm_ffn1_w_down', 'm_ln1_g', 'm_ln1_b', 'm_w_in', 'm_b_forget', 'm_conv_w', 'm_conv_b', 'm_rg_wa', 'm_rg_ba', 'm_rg_wx', 'm_rg_bx', 'm_lru_lambda', 'm_w_out', 'm_ln2_g', 'm_ln2_b', 'm_ffn2_w_gate', 'm_ffn2_w_up', 'm_ffn2_w_down', 'm_ln3_g', 'm_ln3_b', 'v_ffn1_w_gate', 'v_ffn1_w_up', 'v_ffn1_w_down', 'v_ln1_g', 'v_ln1_b', 'v_w_in', 'v_b_forget', 'v_conv_w', 'v_conv_b', 'v_rg_wa', 'v_rg_ba', 'v_rg_wx', 'v_rg_bx', 'v_lru_lambda', 'v_w_out', 'v_ln2_g', 'v_ln2_b', 'v_ffn2_w_gate', 'v_ffn2_w_up', 'v_ffn2_w_down', 'v_ln3_g', 'v_ln3_b']
TWIN_OUTPUTS = ['loss', 'grad_x', 'grad_ffn1_w_gate', 'grad_ffn1_w_up', 'grad_ffn1_w_down', 'grad_ln1_g', 'grad_ln1_b', 'grad_w_in', 'grad_b_forget', 'grad_conv_w', 'grad_conv_b', 'grad_rg_wa', 'grad_rg_ba', 'grad_rg_wx', 'grad_rg_bx', 'grad_lru_lambda', 'grad_w_out', 'grad_ln2_g', 'grad_ln2_b', 'grad_ffn2_w_gate', 'grad_ffn2_w_up', 'grad_ffn2_w_down', 'grad_ln3_g', 'grad_ln3_b', 'delta_ffn1_w_gate', 'delta_ffn1_w_up', 'delta_ffn1_w_down', 'delta_ln1_g', 'delta_ln1_b', 'delta_w_in', 'delta_b_forget', 'delta_conv_w', 'delta_conv_b', 'delta_rg_wa', 'delta_rg_ba', 'delta_rg_wx', 'delta_rg_bx', 'delta_lru_lambda', 'delta_w_out', 'delta_ln2_g', 'delta_ln2_b', 'delta_ffn2_w_gate', 'delta_ffn2_w_up', 'delta_ffn2_w_down', 'delta_ln3_g', 'delta_ln3_b', 'new_m_ffn1_w_gate', 'new_m_ffn1_w_up', 'new_m_ffn1_w_down', 'new_m_ln1_g', 'new_m_ln1_b', 'new_m_w_in', 'new_m_b_forget', 'new_m_conv_w', 'new_m_conv_b', 'new_m_rg_wa', 'new_m_rg_ba', 'new_m_rg_wx', 'new_m_rg_bx', 'new_m_lru_lambda', 'new_m_w_out', 'new_m_ln2_g', 'new_m_ln2_b', 'new_m_ffn2_w_gate', 'new_m_ffn2_w_up', 'new_m_ffn2_w_down', 'new_m_ln3_g', 'new_m_ln3_b', 'new_v_ffn1_w_gate', 'new_v_ffn1_w_up', 'new_v_ffn1_w_down', 'new_v_ln1_g', 'new_v_ln1_b', 'new_v_w_in', 'new_v_b_forget', 'new_v_conv_w', 'new_v_conv_b', 'new_v_rg_wa', 'new_v_rg_ba', 'new_v_rg_wx', 'new_v_rg_bx', 'new_v_lru_lambda', 'new_v_w_out', 'new_v_ln2_g', 'new_v_ln2_b', 'new_v_ffn2_w_gate', 'new_v_ffn2_w_up', 'new_v_ffn2_w_down', 'new_v_ln3_g', 'new_v_ln3_b']
TWIN_LEAF_KINDS = {'loss': 'loss', 'grad_x': 'grad_x', 'grad_ffn1_w_gate': 'grad_w', 'grad_ffn1_w_up': 'grad_w', 'grad_ffn1_w_down': 'grad_w', 'grad_ln1_g': 'grad_w', 'grad_ln1_b': 'grad_w', 'grad_w_in': 'grad_w', 'grad_b_forget': 'grad_w', 'grad_conv_w': 'grad_w', 'grad_conv_b': 'grad_w', 'grad_rg_wa': 'grad_w', 'grad_rg_ba': 'grad_w', 'grad_rg_wx': 'grad_w', 'grad_rg_bx': 'grad_w', 'grad_lru_lambda': 'grad_w', 'grad_w_out': 'grad_w', 'grad_ln2_g': 'grad_w', 'grad_ln2_b': 'grad_w', 'grad_ffn2_w_gate': 'grad_w', 'grad_ffn2_w_up': 'grad_w', 'grad_ffn2_w_down': 'grad_w', 'grad_ln3_g': 'grad_w', 'grad_ln3_b': 'grad_w', 'delta_ffn1_w_gate': 'delta_w', 'delta_ffn1_w_up': 'delta_w', 'delta_ffn1_w_down': 'delta_w', 'delta_ln1_g': 'delta_w', 'delta_ln1_b': 'delta_w', 'delta_w_in': 'delta_w', 'delta_b_forget': 'delta_w', 'delta_conv_w': 'delta_w', 'delta_conv_b': 'delta_w', 'delta_rg_wa': 'delta_w', 'delta_rg_ba': 'delta_w', 'delta_rg_wx': 'delta_w', 'delta_rg_bx': 'delta_w', 'delta_lru_lambda': 'delta_w', 'delta_w_out': 'delta_w', 'delta_ln2_g': 'delta_w', 'delta_ln2_b': 'delta_w', 'delta_ffn2_w_gate': 'delta_w', 'delta_ffn2_w_up': 'delta_w', 'delta_ffn2_w_down': 'delta_w', 'delta_ln3_g': 'delta_w', 'delta_ln3_b': 'delta_w', 'new_m_ffn1_w_gate': 'new_m', 'new_m_ffn1_w_up': 'new_m', 'new_m_ffn1_w_down': 'new_m', 'new_m_ln1_g': 'new_m', 'new_m_ln1_b': 'new_m', 'new_m_w_in': 'new_m', 'new_m_b_forget': 'new_m', 'new_m_conv_w': 'new_m', 'new_m_conv_b': 'new_m', 'new_m_rg_wa': 'new_m', 'new_m_rg_ba': 'new_m', 'new_m_rg_wx': 'new_m', 'new_m_rg_bx': 'new_m', 'new_m_lru_lambda': 'new_m', 'new_m_w_out': 'new_m', 'new_m_ln2_g': 'new_m', 'new_m_ln2_b': 'new_m', 'new_m_ffn2_w_gate': 'new_m', 'new_m_ffn2_w_up': 'new_m', 'new_m_ffn2_w_down': 'new_m', 'new_m_ln3_g': 'new_m', 'new_m_ln3_b': 'new_m', 'new_v_ffn1_w_gate': 'new_v', 'new_v_ffn1_w_up': 'new_v', 'new_v_ffn1_w_down': 'new_v', 'new_v_ln1_g': 'new_v', 'new_v_ln1_b': 'new_v', 'new_v_w_in': 'new_v', 'new_v_b_forget': 'new_v', 'new_v_conv_w': 'new_v', 'new_v_conv_b': 'new_v', 'new_v_rg_wa': 'new_v', 'new_v_rg_ba': 'new_v', 'new_v_rg_wx': 'new_v', 'new_v_rg_bx': 'new_v', 'new_v_lru_lambda': 'new_v', 'new_v_w_out': 'new_v', 'new_v_ln2_g': 'new_v', 'new_v_ln2_b': 'new_v', 'new_v_ffn2_w_gate': 'new_v', 'new_v_ffn2_w_up': 'new_v', 'new_v_ffn2_w_down': 'new_v', 'new_v_ln3_g': 'new_v', 'new_v_ln3_b': 'new_v'}


def _forward(args):
    return _fwd_reference(*[args[k] for k in FWD_PARAMS])


def _output_shape():
    out = _jax.eval_shape(lambda: _forward(_fwd_setup_inputs(0)))
    return out.shape, out.dtype

N_MICROBATCH = 1
ADAM_LR = 0.001
ADAM_B1 = 0.9
ADAM_B2 = 0.999
ADAM_EPS = 1e-08
ADAM_WD = 0.01
ADAM_STEP = 10
PER_EXAMPLE_BATCH_AXIS = {'x': 0, 'loss_target': 0}
SHARED_INPUTS = []
_WEIGHT_DTYPES = {'ffn1_w_gate': _jnp.float32, 'ffn1_w_up': _jnp.float32, 'ffn1_w_down': _jnp.float32, 'ln1_g': _jnp.float32, 'ln1_b': _jnp.float32, 'w_in': _jnp.float32, 'b_forget': _jnp.float32, 'conv_w': _jnp.float32, 'conv_b': _jnp.float32, 'rg_wa': _jnp.float32, 'rg_ba': _jnp.float32, 'rg_wx': _jnp.float32, 'rg_bx': _jnp.float32, 'lru_lambda': _jnp.float32, 'w_out': _jnp.float32, 'ln2_g': _jnp.float32, 'ln2_b': _jnp.float32, 'ffn2_w_gate': _jnp.float32, 'ffn2_w_up': _jnp.float32, 'ffn2_w_down': _jnp.float32, 'ln3_g': _jnp.float32, 'ln3_b': _jnp.float32}
MOMENT_SCALE = {'ffn1_w_gate': 1.384486e-02, 'ffn1_w_up': 1.343928e-02, 'ffn1_w_down': 4.519867e-02, 'ln1_g': 9.881288e-01, 'ln1_b': 6.475922e-01, 'w_in': 3.180859e-02, 'b_forget': 5.310360e-02, 'conv_w': 5.057978e-02, 'conv_b': 5.323688e-01, 'rg_wa': 1.760342e-02, 'rg_ba': 1.731445e-02, 'rg_wx': 3.219249e-02, 'rg_bx': 1.365626e-02, 'lru_lambda': 3.206210e-02, 'w_out': 6.716455e-02, 'ln2_g': 1.055128e+00, 'ln2_b': 4.540827e-01, 'ffn2_w_gate': 1.351676e-02, 'ffn2_w_up': 1.307578e-02, 'ffn2_w_down': 4.400092e-02, 'ln3_g': 3.205627e+01, 'ln3_b': 1.674518e+00}


def _to_microbatches(a, axis):
    t = _jnp.moveaxis(a, axis, 0)
    t = t.reshape((N_MICROBATCH, t.shape[0] // N_MICROBATCH) + t.shape[1:])
    return _jnp.moveaxis(t, 1, axis + 1)


def setup_inputs(seed: int = 0) -> dict:
    inp = _fwd_setup_inputs(seed)
    key = _jax.random.fold_in(_jax.random.key(seed), 7919)
    shape, _ = _output_shape()
    out = dict(inp)
    out["loss_target"] = _jax.random.normal(_jax.random.fold_in(key, 0), shape, _jnp.float32)
    for i, name in enumerate(TWIN_WEIGHTS):
        w = inp[name].astype(_jnp.float32)
        if MOMENT_SCALE is None:
            s = _jnp.sqrt(_jnp.mean(_jnp.square(w)) + 1e-30)
        else:
            s = MOMENT_SCALE[name]
        km, kv = _jax.random.split(_jax.random.fold_in(key, i + 1))
        out[name] = w
        out["m_" + name] = s * _jax.random.normal(km, w.shape, _jnp.float32)
        out["v_" + name] = (s * s) * _jax.random.uniform(kv, w.shape, _jnp.float32, 0.5, 1.5)
    if N_MICROBATCH > 1:
        for name, axis in PER_EXAMPLE_BATCH_AXIS.items():
            out[name] = _to_microbatches(out[name], axis)
    return {'x': out['x'], 'ffn1_w_gate': out['ffn1_w_gate'], 'ffn1_w_up': out['ffn1_w_up'], 'ffn1_w_down': out['ffn1_w_down'], 'ln1_g': out['ln1_g'], 'ln1_b': out['ln1_b'], 'w_in': out['w_in'], 'b_forget': out['b_forget'], 'conv_w': out['conv_w'], 'conv_b': out['conv_b'], 'rg_wa': out['rg_wa'], 'rg_ba': out['rg_ba'], 'rg_wx': out['rg_wx'], 'rg_bx': out['rg_bx'], 'lru_lambda': out['lru_lambda'], 'w_out': out['w_out'], 'ln2_g': out['ln2_g'], 'ln2_b': out['ln2_b'], 'ffn2_w_gate': out['ffn2_w_gate'], 'ffn2_w_up': out['ffn2_w_up'], 'ffn2_w_down': out['ffn2_w_down'], 'ln3_g': out['ln3_g'], 'ln3_b': out['ln3_b'], 'loss_target': out['loss_target'], 'm_ffn1_w_gate': out['m_ffn1_w_gate'], 'm_ffn1_w_up': out['m_ffn1_w_up'], 'm_ffn1_w_down': out['m_ffn1_w_down'], 'm_ln1_g': out['m_ln1_g'], 'm_ln1_b': out['m_ln1_b'], 'm_w_in': out['m_w_in'], 'm_b_forget': out['m_b_forget'], 'm_conv_w': out['m_conv_w'], 'm_conv_b': out['m_conv_b'], 'm_rg_wa': out['m_rg_wa'], 'm_rg_ba': out['m_rg_ba'], 'm_rg_wx': out['m_rg_wx'], 'm_rg_bx': out['m_rg_bx'], 'm_lru_lambda': out['m_lru_lambda'], 'm_w_out': out['m_w_out'], 'm_ln2_g': out['m_ln2_g'], 'm_ln2_b': out['m_ln2_b'], 'm_ffn2_w_gate': out['m_ffn2_w_gate'], 'm_ffn2_w_up': out['m_ffn2_w_up'], 'm_ffn2_w_down': out['m_ffn2_w_down'], 'm_ln3_g': out['m_ln3_g'], 'm_ln3_b': out['m_ln3_b'], 'v_ffn1_w_gate': out['v_ffn1_w_gate'], 'v_ffn1_w_up': out['v_ffn1_w_up'], 'v_ffn1_w_down': out['v_ffn1_w_down'], 'v_ln1_g': out['v_ln1_g'], 'v_ln1_b': out['v_ln1_b'], 'v_w_in': out['v_w_in'], 'v_b_forget': out['v_b_forget'], 'v_conv_w': out['v_conv_w'], 'v_conv_b': out['v_conv_b'], 'v_rg_wa': out['v_rg_wa'], 'v_rg_ba': out['v_rg_ba'], 'v_rg_wx': out['v_rg_wx'], 'v_rg_bx': out['v_rg_bx'], 'v_lru_lambda': out['v_lru_lambda'], 'v_w_out': out['v_w_out'], 'v_ln2_g': out['v_ln2_g'], 'v_ln2_b': out['v_ln2_b'], 'v_ffn2_w_gate': out['v_ffn2_w_gate'], 'v_ffn2_w_up': out['v_ffn2_w_up'], 'v_ffn2_w_down': out['v_ffn2_w_down'], 'v_ln3_g': out['v_ln3_g'], 'v_ln3_b': out['v_ln3_b']}


def _loss(weights, diff, rest, loss_target):
    with _jax.named_scope("forward"):
        args = {**rest, TWIN_DIFF_INPUT: diff, **{k: w.astype(_WEIGHT_DTYPES[k]) for k, w in weights.items()}}
        y = _forward(args)
    with _jax.named_scope("loss_head"):
        err = _jnp.square(y.astype(_jnp.float32) - loss_target)
        return 0.5 * _jnp.sum(_jnp.mean(err, axis=-1)) if err.ndim else 0.5 * err


def _adamw(w, g, m, v):
    m = ADAM_B1 * m + (1.0 - ADAM_B1) * g
    v = ADAM_B2 * v + (1.0 - ADAM_B2) * _jnp.square(g)
    m_hat = m / (1.0 - ADAM_B1 ** ADAM_STEP)
    v_hat = v / (1.0 - ADAM_B2 ** ADAM_STEP)
    delta = -ADAM_LR * (m_hat / (_jnp.sqrt(v_hat) + ADAM_EPS) + ADAM_WD * w)
    return delta, m, v


def reference(x, ffn1_w_gate, ffn1_w_up, ffn1_w_down, ln1_g, ln1_b, w_in, b_forget, conv_w, conv_b, rg_wa, rg_ba, rg_wx, rg_bx, lru_lambda, w_out, ln2_g, ln2_b, ffn2_w_gate, ffn2_w_up, ffn2_w_down, ln3_g, ln3_b, loss_target, m_ffn1_w_gate, m_ffn1_w_up, m_ffn1_w_down, m_ln1_g, m_ln1_b, m_w_in, m_b_forget, m_conv_w, m_conv_b, m_rg_wa, m_rg_ba, m_rg_wx, m_rg_bx, m_lru_lambda, m_w_out, m_ln2_g, m_ln2_b, m_ffn2_w_gate, m_ffn2_w_up, m_ffn2_w_down, m_ln3_g, m_ln3_b, v_ffn1_w_gate, v_ffn1_w_up, v_ffn1_w_down, v_ln1_g, v_ln1_b, v_w_in, v_b_forget, v_conv_w, v_conv_b, v_rg_wa, v_rg_ba, v_rg_wx, v_rg_bx, v_lru_lambda, v_w_out, v_ln2_g, v_ln2_b, v_ffn2_w_gate, v_ffn2_w_up, v_ffn2_w_down, v_ln3_g, v_ln3_b):
    given = dict(x=x, ffn1_w_gate=ffn1_w_gate, ffn1_w_up=ffn1_w_up, ffn1_w_down=ffn1_w_down, ln1_g=ln1_g, ln1_b=ln1_b, w_in=w_in, b_forget=b_forget, conv_w=conv_w, conv_b=conv_b, rg_wa=rg_wa, rg_ba=rg_ba, rg_wx=rg_wx, rg_bx=rg_bx, lru_lambda=lru_lambda, w_out=w_out, ln2_g=ln2_g, ln2_b=ln2_b, ffn2_w_gate=ffn2_w_gate, ffn2_w_up=ffn2_w_up, ffn2_w_down=ffn2_w_down, ln3_g=ln3_g, ln3_b=ln3_b, loss_target=loss_target, m_ffn1_w_gate=m_ffn1_w_gate, m_ffn1_w_up=m_ffn1_w_up, m_ffn1_w_down=m_ffn1_w_down, m_ln1_g=m_ln1_g, m_ln1_b=m_ln1_b, m_w_in=m_w_in, m_b_forget=m_b_forget, m_conv_w=m_conv_w, m_conv_b=m_conv_b, m_rg_wa=m_rg_wa, m_rg_ba=m_rg_ba, m_rg_wx=m_rg_wx, m_rg_bx=m_rg_bx, m_lru_lambda=m_lru_lambda, m_w_out=m_w_out, m_ln2_g=m_ln2_g, m_ln2_b=m_ln2_b, m_ffn2_w_gate=m_ffn2_w_gate, m_ffn2_w_up=m_ffn2_w_up, m_ffn2_w_down=m_ffn2_w_down, m_ln3_g=m_ln3_g, m_ln3_b=m_ln3_b, v_ffn1_w_gate=v_ffn1_w_gate, v_ffn1_w_up=v_ffn1_w_up, v_ffn1_w_down=v_ffn1_w_down, v_ln1_g=v_ln1_g, v_ln1_b=v_ln1_b, v_w_in=v_w_in, v_b_forget=v_b_forget, v_conv_w=v_conv_w, v_conv_b=v_conv_b, v_rg_wa=v_rg_wa, v_rg_ba=v_rg_ba, v_rg_wx=v_rg_wx, v_rg_bx=v_rg_bx, v_lru_lambda=v_lru_lambda, v_w_out=v_w_out, v_ln2_g=v_ln2_g, v_ln2_b=v_ln2_b, v_ffn2_w_gate=v_ffn2_w_gate, v_ffn2_w_up=v_ffn2_w_up, v_ffn2_w_down=v_ffn2_w_down, v_ln3_g=v_ln3_g, v_ln3_b=v_ln3_b)
    weights = {n: given[n] for n in TWIN_WEIGHTS}
    shared = {n: given[n] for n in SHARED_INPUTS}
    per_example = {n: given[n] for n in ['x']}
    grad_fn = _jax.value_and_grad(_loss, argnums=(0, 1))

    def one_microbatch(ex, loss_target):
        ex = dict(ex)
        diff = ex.pop(TWIN_DIFF_INPUT)
        return grad_fn(weights, diff, {**shared, **ex}, loss_target)

    if N_MICROBATCH == 1:
        loss, (grad_w, grad_x) = one_microbatch(per_example, given["loss_target"])
    else:
        def body(carry, xs):
            loss_sum, grad_sum = carry
            l_k, (gw_k, gx_k) = one_microbatch(xs[0], xs[1])
            with _jax.named_scope("update"):
                return (loss_sum + l_k, _jax.tree.map(_jnp.add, grad_sum, gw_k)), gx_k

        init = (_jnp.zeros((), _jnp.float32), _jax.tree.map(_jnp.zeros_like, weights))
        (loss, grad_w), grad_x = _jax.lax.scan(body, init, (per_example, given["loss_target"]))
    with _jax.named_scope("update"):
        delta_w, new_m, new_v = {}, {}, {}
        for n in TWIN_WEIGHTS:
            delta_w[n], new_m[n], new_v[n] = _adamw(weights[n], grad_w[n], given["m_" + n], given["v_" + n])
    return (loss, grad_x, *[grad_w[n] for n in TWIN_WEIGHTS], *[delta_w[n] for n in TWIN_WEIGHTS],
            *[new_m[n] for n in TWIN_WEIGHTS], *[new_v[n] for n in TWIN_WEIGHTS])
```

```python
import functools
import math

import jax
import jax.numpy as jnp
from jax import lax
from jax.experimental import pallas as pl
from jax.experimental.pallas import tpu as pltpu

f32 = jnp.float32
MXU_DTYPE = jnp.bfloat16
GRAD_DTYPE = jnp.bfloat16

D_MODEL = 1024
D_FF = 4096
N_SHARD = 4
N_DEV = 8
FOX_W = 512
LRU_W = 512
HEADS = 8
HEAD_DIM = 64
CONV_K = 4
IN_COLS = 2568
IN_SHARD = IN_COLS // N_SHARD
QKV_W = 3 * FOX_W
Z_PAD = 2688
LANES = 128
LN_EPS = 1e-5
DN_ALPHA = 2.0 ** 0.25
LRU_C = 8.0
NEG_BIG = -1e30
VMEM_LIMIT = 56 * 1024 * 1024

ADAM_LR = 0.001
ADAM_B1 = 0.9
ADAM_B2 = 0.999
ADAM_EPS = 1e-08
ADAM_WD = 0.01
ADAM_STEP = 10


def _pcall(body, **kw):
    return pl.pallas_call(body, **kw)


def _params(n_grid, vmem=VMEM_LIMIT):
    return pltpu.CompilerParams(dimension_semantics=("arbitrary",) * n_grid, vmem_limit_bytes=vmem)


def _dot(a, b):
    return jnp.dot(a, b, preferred_element_type=f32)


def _dot_nt(a, b):
    return lax.dot_general(a, b, (((1,), (1,)), ((), ())), preferred_element_type=f32)


def _dot_tn(a, b):
    return lax.dot_general(a, b, (((0,), (0,)), ((), ())), preferred_element_type=f32)


def _sigmoid(x):
    return 1.0 / (1.0 + jnp.exp(-x))


def _layer_norm_stats(y):
    mu = jnp.mean(y, axis=-1, keepdims=True)
    yc = y - mu
    var = jnp.mean(yc * yc, axis=-1, keepdims=True)
    rstd = lax.rsqrt(var + LN_EPS)
    return yc * rstd, rstd


def _ln_backward(dy, xhat, rstd, gamma):
    dxhat = dy * gamma
    m1 = jnp.mean(dxhat, axis=-1, keepdims=True)
    m2 = jnp.mean(dxhat * xhat, axis=-1, keepdims=True)
    dyp = rstd * (dxhat - m1 - xhat * m2)
    return dyp, jnp.sum(dy * xhat, axis=0, keepdims=True), jnp.sum(dy, axis=0, keepdims=True)


def _ffn_fwd(x, wg, wu, wd, ln_g, ln_b, *, name, tm=1024, tf=512):
    T = x.shape[0]
    tm = min(tm, T)
    fs = D_FF // N_SHARD
    cpf = fs // tf
    nf = D_FF // tf
    nt = T // tm

    def body(x_ref, wg_ref, wu_ref, wd_ref, g_ref, b_ref,
             xb_ref, gact_ref, uact_ref, xhat_ref, xn_ref, rstd_ref, acc_ref):
        f = pl.program_id(1)

        @pl.when(f == 0)
        def _():
            xb_ref[...] = x_ref[...].astype(MXU_DTYPE)
            acc_ref[...] = jnp.zeros_like(acc_ref)

        xb = xb_ref[...]
        g = _dot(xb, wg_ref[...])
        u = _dot(xb, wu_ref[...])
        h = (g * _sigmoid(g)) * u
        gact_ref[...] = g.astype(gact_ref.dtype)
        uact_ref[...] = u.astype(uact_ref.dtype)
        acc_ref[...] += _dot(h.astype(MXU_DTYPE), wd_ref[...])

        @pl.when(f == nf - 1)
        def _():
            y = DN_ALPHA * x_ref[...] + 0.5 * acc_ref[...]
            xhat, rstd = _layer_norm_stats(y)
            xhat_ref[...] = xhat
            xn_ref[...] = (xhat * g_ref[...] + b_ref[...]).astype(xn_ref.dtype)
            rstd_ref[...] = jnp.broadcast_to(rstd, rstd_ref.shape)

    row = lambda i, f: (i, 0)
    return _pcall(
        body, name=name, grid=(nt, nf),
        in_specs=[
            pl.BlockSpec((tm, D_MODEL), row),
            pl.BlockSpec((None, D_MODEL, tf), lambda i, f: (f // cpf, 0, f % cpf)),
            pl.BlockSpec((None, D_MODEL, tf), lambda i, f: (f // cpf, 0, f % cpf)),
            pl.BlockSpec((None, tf, D_MODEL), lambda i, f: (f // cpf, f % cpf, 0)),
            pl.BlockSpec((1, D_MODEL), lambda i, f: (0, 0)),
            pl.BlockSpec((1, D_MODEL), lambda i, f: (0, 0)),
        ],
        out_specs=[
            pl.BlockSpec((tm, D_MODEL), row),
            pl.BlockSpec((tm, tf), lambda i, f: (i, f)),
            pl.BlockSpec((tm, tf), lambda i, f: (i, f)),
            pl.BlockSpec((tm, D_MODEL), row),
            pl.BlockSpec((tm, D_MODEL), row),
            pl.BlockSpec((tm, LANES), row),
        ],
        out_shape=[
            jax.ShapeDtypeStruct((T, D_MODEL), MXU_DTYPE),
            jax.ShapeDtypeStruct((T, D_FF), MXU_DTYPE),
            jax.ShapeDtypeStruct((T, D_FF), MXU_DTYPE),
            jax.ShapeDtypeStruct((T, D_MODEL), f32),
            jax.ShapeDtypeStruct((T, D_MODEL), MXU_DTYPE),
            jax.ShapeDtypeStruct((T, LANES), f32),
        ],
        scratch_shapes=[pltpu.VMEM((tm, D_MODEL), f32)],
        compiler_params=_params(2),
    )(x, wg, wu, wd, ln_g, ln_b)


def _ffn_bwd(dyp, xb, gact, uact, wg, wu, wd, *, name, tm=512, tf=512):
    T = dyp.shape[0]
    tm = min(tm, T)
    fs = D_FF // N_SHARD
    cpf = fs // tf
    nf = D_FF // tf
    nt = T // tm

    def body(dyp_ref, xb_ref, g_ref, u_ref, wg_ref, wu_ref, wd_ref,
             dx_hbm, dwg_ref, dwu_ref, dwd_ref, dx_sc, dwg_sc, dwu_sc, dwd_sc, sem):
        f = pl.program_id(0)
        i = pl.program_id(1)
        rows = pl.ds(pl.multiple_of(i * tm, tm), tm)
        dyp_t = dyp_ref[...]
        dy = (0.5 * dyp_t).astype(MXU_DTYPE)

        @pl.when(i == 0)
        def _():
            dwg_sc[...] = jnp.zeros_like(dwg_sc)
            dwu_sc[...] = jnp.zeros_like(dwu_sc)
            dwd_sc[...] = jnp.zeros_like(dwd_sc)

        @pl.when(f == 0)
        def _():
            dx_sc[rows, :] = DN_ALPHA * dyp_t

        g = g_ref[...].astype(f32)
        u = u_ref[...].astype(f32)
        sig = _sigmoid(g)
        silu = g * sig
        dh = _dot_nt(dy, wd_ref[...])
        dg = (dh * u * (sig * (1.0 + g * (1.0 - sig)))).astype(MXU_DTYPE)
        du = (dh * silu).astype(MXU_DTYPE)
        hb = (silu * u).astype(MXU_DTYPE)
        dx_sc[rows, :] += _dot_nt(dg, wg_ref[...]) + _dot_nt(du, wu_ref[...])
        xb_t = xb_ref[...]
        dwg_sc[...] += _dot_tn(xb_t, dg)
        dwu_sc[...] += _dot_tn(xb_t, du)
        dwd_sc[...] += _dot_tn(hb, dy)

        @pl.when(i == nt - 1)
        def _():
            dwg_ref[...] = dwg_sc[...].astype(dwg_ref.dtype)
            dwu_ref[...] = dwu_sc[...].astype(dwu_ref.dtype)
            dwd_ref[...] = dwd_sc[...].astype(dwd_ref.dtype)

        @pl.when(jnp.logical_and(f == nf - 1, i == nt - 1))
        def _():
            cp = pltpu.make_async_copy(dx_sc, dx_hbm, sem)
            cp.start()
            cp.wait()

    row = lambda f, i: (i, 0)
    return _pcall(
        body, name=name, grid=(nf, nt),
        in_specs=[
            pl.BlockSpec((tm, D_MODEL), row),
            pl.BlockSpec((tm, D_MODEL), row),
            pl.BlockSpec((tm, tf), lambda f, i: (i, f)),
            pl.BlockSpec((tm, tf), lambda f, i: (i, f)),
            pl.BlockSpec((None, D_MODEL, tf), lambda f, i: (f // cpf, 0, f % cpf)),
            pl.BlockSpec((None, D_MODEL, tf), lambda f, i: (f // cpf, 0, f % cpf)),
            pl.BlockSpec((None, tf, D_MODEL), lambda f, i: (f // cpf, f % cpf, 0)),
        ],
        out_specs=[
            pl.BlockSpec(memory_space=pl.ANY),
            pl.BlockSpec((None, D_MODEL, tf), lambda f, i: (f // cpf, 0, f % cpf)),
            pl.BlockSpec((None, D_MODEL, tf), lambda f, i: (f // cpf, 0, f % cpf)),
            pl.BlockSpec((None, tf, D_MODEL), lambda f, i: (f // cpf, f % cpf, 0)),
        ],
        out_shape=[
            jax.ShapeDtypeStruct((T, D_MODEL), f32),
            jax.ShapeDtypeStruct((N_SHARD, D_MODEL, fs), GRAD_DTYPE),
            jax.ShapeDtypeStruct((N_SHARD, D_MODEL, fs), GRAD_DTYPE),
            jax.ShapeDtypeStruct((N_SHARD, fs, D_MODEL), GRAD_DTYPE),
        ],
        scratch_shapes=[pltpu.VMEM((T, D_MODEL), f32), pltpu.VMEM((D_MODEL, tf), f32),
                        pltpu.VMEM((D_MODEL, tf), f32), pltpu.VMEM((tf, D_MODEL), f32),
                        pltpu.SemaphoreType.DMA],
        compiler_params=_params(2),
    )(dyp, xb, gact, uact, wg, wu, wd)


def _loss_ln_bwd(xhat, rstd, ln_g, ln_b, target, *, name, tm=512):
    T = xhat.shape[0]
    tm = min(tm, T)
    nt = T // tm

    def body(xhat_ref, rstd_ref, g_ref, b_ref, t_ref, dyp_ref, dg_ref, db_ref, loss_ref):
        i = pl.program_id(0)

        @pl.when(i == 0)
        def _():
            dg_ref[...] = jnp.zeros_like(dg_ref)
            db_ref[...] = jnp.zeros_like(db_ref)
            loss_ref[...] = jnp.zeros_like(loss_ref)

        xhat_t = xhat_ref[...]
        gamma = g_ref[...]
        err = xhat_t * gamma + b_ref[...] - t_ref[...]
        sq = jnp.sum(jnp.sum(err * err, axis=0, keepdims=True), axis=1, keepdims=True)
        loss_ref[...] += jnp.broadcast_to(sq * (0.5 / D_MODEL), loss_ref.shape)
        dy = err * (1.0 / D_MODEL)
        dyp, dgam, dbeta = _ln_backward(dy, xhat_t, rstd_ref[:, 0:1], gamma)
        dyp_ref[...] = dyp
        dg_ref[...] += dgam
        db_ref[...] += dbeta

    row = lambda i: (i, 0)
    const = lambda i: (0, 0)
    return _pcall(
        body, name=name, grid=(nt,),
        in_specs=[pl.BlockSpec((tm, D_MODEL), row), pl.BlockSpec((tm, LANES), row),
                  pl.BlockSpec((1, D_MODEL), const), pl.BlockSpec((1, D_MODEL), const),
                  pl.BlockSpec((tm, D_MODEL), row)],
        out_specs=[pl.BlockSpec((tm, D_MODEL), row), pl.BlockSpec((1, D_MODEL), const),
                   pl.BlockSpec((1, D_MODEL), const), pl.BlockSpec((1, LANES), const)],
        out_shape=[jax.ShapeDtypeStruct((T, D_MODEL), f32), jax.ShapeDtypeStruct((1, D_MODEL), f32),
                   jax.ShapeDtypeStruct((1, D_MODEL), f32), jax.ShapeDtypeStruct((1, LANES), f32)],
        compiler_params=_params(1),
    )(xhat, rstd, ln_g, ln_b, target)


def _ln_bwd(dy, xhat, rstd, ln_g, *, name, tm=512):
    T = xhat.shape[0]
    tm = min(tm, T)
    nt = T // tm

    def body(dy_ref, xhat_ref, rstd_ref, g_ref, dyp_ref, dg_ref, db_ref):
        i = pl.program_id(0)

        @pl.when(i == 0)
        def _():
            dg_ref[...] = jnp.zeros_like(dg_ref)
            db_ref[...] = jnp.zeros_like(db_ref)

        dyp, dgam, dbeta = _ln_backward(dy_ref[...], xhat_ref[...], rstd_ref[:, 0:1], g_ref[...])
        dyp_ref[...] = dyp
        dg_ref[...] += dgam
        db_ref[...] += dbeta

    row = lambda i: (i, 0)
    const = lambda i: (0, 0)
    return _pcall(
        body, name=name, grid=(nt,),
        in_specs=[pl.BlockSpec((tm, D_MODEL), row), pl.BlockSpec((tm, D_MODEL), row),
                  pl.BlockSpec((tm, LANES), row), pl.BlockSpec((1, D_MODEL), const)],
        out_specs=[pl.BlockSpec((tm, D_MODEL), row), pl.BlockSpec((1, D_MODEL), const),
                   pl.BlockSpec((1, D_MODEL), const)],
        out_shape=[jax.ShapeDtypeStruct((T, D_MODEL), f32), jax.ShapeDtypeStruct((1, D_MODEL), f32),
                   jax.ShapeDtypeStruct((1, D_MODEL), f32)],
        compiler_params=_params(1),
    )(dy, xhat, rstd, ln_g)


def _proj_in(xn, wp, bfp, *, name, tm=512):
    T = xn.shape[0]
    tm = min(tm, T)
    nt = T // tm

    def body(x_ref, w_ref, b_ref, qkv_ref, lxg_ref, fg_ref):
        z = _dot(x_ref[...], w_ref[...])
        qkv_ref[...] = z[:, :QKV_W].astype(qkv_ref.dtype)
        lxg_ref[...] = z[:, QKV_W:QKV_W + 2 * LRU_W]
        fg_ref[...] = z[:, QKV_W + 2 * LRU_W:] + b_ref[...]

    row = lambda i: (i, 0)
    const = lambda i: (0, 0)
    return _pcall(
        body, name=name, grid=(nt,),
        in_specs=[pl.BlockSpec((tm, D_MODEL), row), pl.BlockSpec((D_MODEL, Z_PAD), const),
                  pl.BlockSpec((1, LANES), const)],
        out_specs=[pl.BlockSpec((tm, QKV_W), row), pl.BlockSpec((tm, 2 * LRU_W), row),
                   pl.BlockSpec((tm, LANES), row)],
        out_shape=[jax.ShapeDtypeStruct((T, QKV_W), MXU_DTYPE), jax.ShapeDtypeStruct((T, 2 * LRU_W), f32),
                   jax.ShapeDtypeStruct((T, LANES), f32)],
        compiler_params=_params(1),
    )(xn, wp, bfp)


def _proj_in_bwd(dq, dk, dv, dlxg, dfg, xn, dyp, wp, *, name, tm=512):
    T = xn.shape[0]
    tm = min(tm, T)
    nt = T // tm

    def body(dq_ref, dk_ref, dv_ref, dl_ref, dfg_ref, x_ref, dyp_ref, w_ref, dx_ref, dw_hbm, dw_sc, sem):
        i = pl.program_id(0)

        @pl.when(i == 0)
        def _():
            dw_sc[...] = jnp.zeros_like(dw_sc)

        dz = jnp.concatenate(
            [dq_ref[...].astype(MXU_DTYPE), dk_ref[...].astype(MXU_DTYPE), dv_ref[...].astype(MXU_DTYPE),
             dl_ref[...].astype(MXU_DTYPE), dfg_ref[...].astype(MXU_DTYPE)], axis=1)
        dx_ref[...] = DN_ALPHA * dyp_ref[...] + _dot_nt(dz, w_ref[...])
        dw_sc[...] += _dot_tn(x_ref[...], dz)

        @pl.when(i == nt - 1)
        def _():
            dw_sc[:, :FOX_W] = dw_sc[:, :FOX_W] * (1.0 / math.sqrt(HEAD_DIM))
            cp = pltpu.make_async_copy(dw_sc, dw_hbm, sem)
            cp.start()
            cp.wait()

    row = lambda i: (i, 0)
    const = lambda i: (0, 0)
    return _pcall(
        body, name=name, grid=(nt,),
        in_specs=[pl.BlockSpec((tm, FOX_W), row), pl.BlockSpec((tm, FOX_W), row), pl.BlockSpec((tm, FOX_W), row),
                  pl.BlockSpec((tm, 2 * LRU_W), row), pl.BlockSpec((tm, LANES), row),
                  pl.BlockSpec((tm, D_MODEL), row), pl.BlockSpec((tm, D_MODEL), row),
                  pl.BlockSpec((D_MODEL, Z_PAD), const)],
        out_specs=[pl.BlockSpec((tm, D_MODEL), row), pl.BlockSpec(memory_space=pl.ANY)],
        out_shape=[jax.ShapeDtypeStruct((T, D_MODEL), f32), jax.ShapeDtypeStruct((D_MODEL, Z_PAD), f32)],
        scratch_shapes=[pltpu.VMEM((D_MODEL, Z_PAD), f32), pltpu.SemaphoreType.DMA],
        compiler_params=_params(1),
    )(dq, dk, dv, dlxg, dfg, xn, dyp, wp)


def _split3(x):
    hi = x.astype(jnp.bfloat16)
    r1 = x - hi.astype(f32)
    mid = r1.astype(jnp.bfloat16)
    lo = (r1 - mid.astype(f32)).astype(jnp.bfloat16)
    return hi, mid, lo


def _tri_dot(tri, x):
    hi, mid, lo = _split3(x)
    return _dot(tri, hi) + _dot(tri, mid) + _dot(tri, lo)


def _fox_prep(fgb, *, name, tm=512):
    T = fgb.shape[0]
    tm = min(tm, T)
    nt = T // tm

    def body(fg_ref, crep_ref, ct_ref, carry):
        i = pl.program_id(0)

        @pl.when(i == 0)
        def _():
            carry[...] = jnp.zeros_like(carry)

        x = fg_ref[...]
        ls = jnp.minimum(x, 0.0) - jnp.log(1.0 + jnp.exp(-jnp.abs(x)))
        r = lax.broadcasted_iota(jnp.int32, (tm, tm), 0)
        c = lax.broadcasted_iota(jnp.int32, (tm, tm), 1)
        tri = jnp.where(r >= c, 1.0, 0.0).astype(jnp.bfloat16)
        cum = _tri_dot(tri, ls) + carry[0:1, :]
        carry[...] = jnp.broadcast_to(cum[tm - 1:tm, :], carry.shape)
        for h in range(HEADS):
            crep_ref[h] = jnp.broadcast_to(cum[:, h:h + 1], (tm, LANES))
        ct_ref[...] = cum.T[:HEADS, :]

    return _pcall(
        body, name=name, grid=(nt,),
        in_specs=[pl.BlockSpec((tm, LANES), lambda i: (i, 0))],
        out_specs=[pl.BlockSpec((HEADS, tm, LANES), lambda i: (0, i, 0)),
                   pl.BlockSpec((HEADS, tm), lambda i: (0, i))],
        out_shape=[jax.ShapeDtypeStruct((HEADS, T, LANES), f32), jax.ShapeDtypeStruct((HEADS, T), f32)],
        scratch_shapes=[pltpu.VMEM((8, LANES), f32)],
        compiler_params=_params(1),
    )(fgb)


def _head_mask(shape, h):
    lane = lax.broadcasted_iota(jnp.int32, shape, 1)
    return (lane < HEAD_DIM) if h == 0 else (lane >= HEAD_DIM)


def _causal_mask(qi, ki, tq, tk):
    r = lax.broadcasted_iota(jnp.int32, (tq, tk), 0)
    c = lax.broadcasted_iota(jnp.int32, (tq, tk), 1)
    return jnp.logical_and(qi == ki, c > r)


def _fox_fwd(qkv, crep, ct, *, name, tq=512):
    T = qkv.shape[0]
    tq = min(tq, T)
    tk = tq
    nq = T // tq
    rep = tk // LANES

    def body(q_ref, k_ref, v_ref, cq_ref, ct_ref, o_ref, lse_ref, m_sc, l_sc, acc_sc):
        j = pl.program_id(0)
        qi = pl.program_id(1)
        ki = pl.program_id(2)

        @pl.when(ki == 0)
        def _():
            m_sc[...] = jnp.full_like(m_sc, NEG_BIG)
            l_sc[...] = jnp.zeros_like(l_sc)
            acc_sc[...] = jnp.zeros_like(acc_sc)

        @pl.when(ki <= qi)
        def _():
            q2 = q_ref[...]
            k2 = k_ref[...]
            v2 = v_ref[...]
            future = _causal_mask(qi, ki, tq, tk)
            for h in range(2):
                qh = jnp.where(_head_mask(q2.shape, h), q2, jnp.zeros_like(q2))
                ck = ct_ref[pl.ds(2 * j + h, 1), :]
                s = _dot_nt(qh, k2) + (jnp.tile(cq_ref[h], (1, rep)) - ck)
                s = jnp.where(future, NEG_BIG, s)
                m_prev = m_sc[h]
                m_new = jnp.maximum(m_prev, jnp.max(s, axis=1, keepdims=True))
                p = jnp.exp(s - jnp.tile(m_new, (1, rep)))
                alpha = jnp.exp(m_prev - m_new)
                l_sc[h] = alpha * l_sc[h] + jnp.sum(p, axis=1, keepdims=True)
                acc_sc[h] = alpha * acc_sc[h] + _dot(p.astype(MXU_DTYPE), v2)
                m_sc[h] = m_new

        @pl.when(ki == qi)
        def _():
            o0 = acc_sc[0] / l_sc[0]
            o1 = acc_sc[1] / l_sc[1]
            o_ref[...] = jnp.where(_head_mask(o0.shape, 0), o0, o1)
            for h in range(2):
                lse_ref[h] = m_sc[h] + jnp.log(l_sc[h])

    kv = lambda j, qi, ki: jnp.minimum(ki, qi)
    return _pcall(
        body, name=name, grid=(HEADS // 2, nq, nq),
        in_specs=[
            pl.BlockSpec((tq, LANES), lambda j, qi, ki: (qi, j)),
            pl.BlockSpec((tk, LANES), lambda j, qi, ki: (kv(j, qi, ki), 4 + j)),
            pl.BlockSpec((tk, LANES), lambda j, qi, ki: (kv(j, qi, ki), 8 + j)),
            pl.BlockSpec((2, tq, LANES), lambda j, qi, ki: (j, qi, 0)),
            pl.BlockSpec((HEADS, tk), lambda j, qi, ki: (0, kv(j, qi, ki))),
        ],
        out_specs=[pl.BlockSpec((tq, LANES), lambda j, qi, ki: (qi, j)),
                   pl.BlockSpec((2, tq, LANES), lambda j, qi, ki: (j, qi, 0))],
        out_shape=[jax.ShapeDtypeStruct((T, FOX_W), f32), jax.ShapeDtypeStruct((HEADS, T, LANES), f32)],
        scratch_shapes=[pltpu.VMEM((2, tq, LANES), f32)] * 3,
        compiler_params=_params(3),
    )(qkv, qkv, qkv, crep, ct)


def _fox_bwd_prep(do, o, *, name, tm=512):
    T = o.shape[0]
    tm = min(tm, T)
    nt = T // tm

    def body(do_ref, o_ref, d_ref):
        prod = do_ref[...].astype(f32) * o_ref[...]
        for j in range(HEADS // 2):
            pj = prod[:, j * LANES:(j + 1) * LANES]
            for h in range(2):
                dsum = jnp.sum(jnp.where(_head_mask(pj.shape, h), pj, 0.0), axis=1, keepdims=True)
                d_ref[2 * j + h] = jnp.broadcast_to(dsum, (tm, LANES))

    return _pcall(
        body, name=name, grid=(nt,),
        in_specs=[pl.BlockSpec((tm, FOX_W), lambda i: (i, 0)), pl.BlockSpec((tm, FOX_W), lambda i: (i, 0))],
        out_specs=[pl.BlockSpec((HEADS, tm, LANES), lambda i: (0, i, 0))],
        out_shape=[jax.ShapeDtypeStruct((HEADS, T, LANES), f32)],
        compiler_params=_params(1),
    )(do, o)[0]


def _fox_bwd(qkv, do, crep, ct, lse, drep, *, name, tq=512):
    T = qkv.shape[0]
    tq = min(tq, T)
    tk = tq
    nq = T // tq
    rep = tk // LANES

    def body(q_ref, k_ref, v_ref, do_ref, cq_ref, ct_ref, lse_ref, d_ref,
             dq_ref, drow_ref, dk_ref, dv_ref, dcol_ref, dk_sc, dv_sc):
        j = pl.program_id(0)
        ki = pl.program_id(1)
        qi = pl.program_id(2)
        rows = pl.ds(pl.multiple_of(qi * tq, tq), tq)

        @pl.when(qi == 0)
        def _():
            dk_sc[...] = jnp.zeros_like(dk_sc)
            dv_sc[...] = jnp.zeros_like(dv_sc)

        @pl.when(jnp.logical_and(ki == 0, qi == 0))
        def _():
            dq_ref[...] = jnp.zeros_like(dq_ref)
            drow_ref[...] = jnp.zeros_like(drow_ref)

        @pl.when(qi >= ki)
        def _():
            q2 = q_ref[...]
            k2 = k_ref[...]
            v2 = v_ref[...]
            do2 = do_ref[...]
            future = _causal_mask(qi, ki, tq, tk)
            dq_acc = jnp.zeros((tq, LANES), f32)
            drow_acc = jnp.zeros((tq, LANES), f32)
            for h in range(2):
                hm = _head_mask(q2.shape, h)
                qh = jnp.where(hm, q2, jnp.zeros_like(q2))
                ck = ct_ref[pl.ds(2 * j + h, 1), :]
                s = _dot_nt(qh, k2) + (jnp.tile(cq_ref[h], (1, rep)) - ck)
                p = jnp.exp(s - jnp.tile(lse_ref[h], (1, rep)))
                p = jnp.where(future, 0.0, p)
                doh = jnp.where(hm, do2, jnp.zeros_like(do2))
                dp = _dot_nt(doh, v2)
                ds = (p * (dp - jnp.tile(d_ref[h], (1, rep)))).astype(MXU_DTYPE)
                dv_sc[h] += _dot_tn(p.astype(MXU_DTYPE), do2)
                q_ones = jnp.where(hm, q2, jnp.ones_like(q2))
                dk_sc[h] += _dot_tn(ds, q_ones)
                dq_full = _dot(ds, jnp.where(hm, k2, jnp.ones_like(k2)))
                dq_acc = dq_acc + jnp.where(hm, dq_full, 0.0)
                drow_acc = drow_acc + jnp.where(hm, 0.0, dq_full)
            dq_ref[rows, :] += dq_acc
            drow_ref[rows, :] += drow_acc

        @pl.when(qi == nq - 1)
        def _():
            hm0 = _head_mask((tk, LANES), 0)
            dk_ref[...] = jnp.where(hm0, dk_sc[0], dk_sc[1])
            dcol_ref[...] = jnp.where(hm0, dk_sc[1], dk_sc[0])
            dv_ref[...] = jnp.where(hm0, dv_sc[0], dv_sc[1])

    qb = lambda j, ki, qi: jnp.maximum(qi, ki)
    return _pcall(
        body, name=name, grid=(HEADS // 2, nq, nq),
        in_specs=[
            pl.BlockSpec((tq, LANES), lambda j, ki, qi: (qb(j, ki, qi), j)),
            pl.BlockSpec((tk, LANES), lambda j, ki, qi: (ki, 4 + j)),
            pl.BlockSpec((tk, LANES), lambda j, ki, qi: (ki, 8 + j)),
            pl.BlockSpec((tq, LANES), lambda j, ki, qi: (qb(j, ki, qi), j)),
            pl.BlockSpec((2, tq, LANES), lambda j, ki, qi: (j, qb(j, ki, qi), 0)),
            pl.BlockSpec((HEADS, tk), lambda j, ki, qi: (0, ki)),
            pl.BlockSpec((2, tq, LANES), lambda j, ki, qi: (j, qb(j, ki, qi), 0)),
            pl.BlockSpec((2, tq, LANES), lambda j, ki, qi: (j, qb(j, ki, qi), 0)),
        ],
        out_specs=[
            pl.BlockSpec((T, LANES), lambda j, ki, qi: (0, j)),
            pl.BlockSpec((T, LANES), lambda j, ki, qi: (0, j)),
            pl.BlockSpec((tk, LANES), lambda j, ki, qi: (ki, j)),
            pl.BlockSpec((tk, LANES), lambda j, ki, qi: (ki, j)),
            pl.BlockSpec((tk, LANES), lambda j, ki, qi: (ki, j)),
        ],
        out_shape=[jax.ShapeDtypeStruct((T, FOX_W), f32)] * 5,
        scratch_shapes=[pltpu.VMEM((2, tk, LANES), f32)] * 2,
        compiler_params=_params(3),
    )(qkv, qkv, qkv, do, crep, ct, lse, drep)


def _fox_bwd_post(drow, dcol, fgb, *, name, tm=512):
    T = fgb.shape[0]
    tm = min(tm, T)
    nt = T // tm

    def body(drow_ref, dcol_ref, fg_ref, dfg_ref, dbf_ref, carry):
        i = pl.program_id(0)

        @pl.when(i == 0)
        def _():
            carry[...] = jnp.zeros_like(carry)
            dbf_ref[...] = jnp.zeros_like(dbf_ref)

        dcol_t = drow_ref[...] - dcol_ref[...]
        lane = lax.broadcasted_iota(jnp.int32, (tm, LANES), 1)
        dc = jnp.zeros((tm, LANES), f32)
        for h in range(HEADS):
            src = (h // 2) * LANES + (HEAD_DIM if h % 2 == 0 else 0)
            dc = jnp.where(lane == h, jnp.broadcast_to(dcol_t[:, src:src + 1], (tm, LANES)), dc)
        r = lax.broadcasted_iota(jnp.int32, (tm, tm), 0)
        c = lax.broadcasted_iota(jnp.int32, (tm, tm), 1)
        tri = jnp.where(c >= r, 1.0, 0.0).astype(jnp.bfloat16)
        dls = _tri_dot(tri, dc) + carry[0:1, :]
        carry[...] = jnp.broadcast_to(dls[0:1, :], carry.shape)
        dfg = dls * _sigmoid(-fg_ref[...])
        dfg_ref[...] = dfg
        dbf_ref[...] += jnp.sum(dfg, axis=0, keepdims=True)

    rev = lambda i: (nt - 1 - i, 0)
    return _pcall(
        body, name=name, grid=(nt,),
        in_specs=[pl.BlockSpec((tm, FOX_W), rev), pl.BlockSpec((tm, FOX_W), rev), pl.BlockSpec((tm, LANES), rev)],
        out_specs=[pl.BlockSpec((tm, LANES), rev), pl.BlockSpec((1, LANES), lambda i: (0, 0))],
        out_shape=[jax.ShapeDtypeStruct((T, LANES), f32), jax.ShapeDtypeStruct((1, LANES), f32)],
        scratch_shapes=[pltpu.VMEM((8, LANES), f32)],
        compiler_params=_params(1),
    )(drow, dcol, fgb)


GELU_C = math.sqrt(2.0 / math.pi)
GELU_A = 0.044715


def _gelu(x):
    t = jnp.tanh(GELU_C * (x + GELU_A * x * x * x))
    return 0.5 * x * (1.0 + t), t


def _gelu_grad(x, t):
    return 0.5 * (1.0 + t) + 0.5 * x * (1.0 - t * t) * GELU_C * (1.0 + 3.0 * GELU_A * x * x)


def _expm1(x):
    e = jnp.exp(x)
    safe = jnp.where(e == 1.0, x, (e - 1.0) * x / jnp.log(jnp.where(e == 1.0, 0.5, e)))
    return jnp.where(x < -0.5, e - 1.0, safe)


def _lru_gates(u, wab_ref, bab_ref, lam_ref):
    pre = _dot(u.astype(MXU_DTYPE), wab_ref[...]) + bab_ref[...]
    r = _sigmoid(pre[:, :LRU_W])
    gi = _sigmoid(pre[:, LRU_W:])
    lam = lam_ref[...]
    sp = jnp.maximum(-lam, 0.0) + jnp.log(1.0 + jnp.exp(-jnp.abs(lam)))
    log_a = -LRU_C * r * sp
    a = jnp.exp(log_a)
    s = jnp.sqrt(-_expm1(2.0 * log_a))
    return r, gi, sp, a, s


def _lru_fwd(lxg, conv_w, conv_b, wab, bab, lam, *, name, tc=512):
    T = lxg.shape[0]
    tc = min(tc, T)
    nc = T // tc

    def body(lx_ref, lg_ref, cw_ref, cb_ref, wab_ref, bab_ref, lam_ref,
             out_ref, u_ref, hs_ref, ext, a_sc, b_sc, h_sc):
        i = pl.program_id(0)

        @pl.when(i == 0)
        def _():
            ext[0:8, :] = jnp.zeros((8, LRU_W), f32)
            h_sc[...] = jnp.zeros_like(h_sc)

        ext[8:, :] = lx_ref[...]
        u = cb_ref[...] + cw_ref[0:1, :] * ext[pl.ds(5, tc), :]
        for k in range(1, CONV_K):
            u = u + cw_ref[k:k + 1, :] * ext[pl.ds(5 + k, tc), :]
        ext[0:8, :] = ext[tc:tc + 8, :]
        u_ref[...] = u
        r, gi, sp, a, s = _lru_gates(u, wab_ref, bab_ref, lam_ref)
        a_sc[...] = a
        b_sc[...] = s * (gi * u)

        def step(t, h):
            h = a_sc[pl.ds(t, 1), :] * h + b_sc[pl.ds(t, 1), :]
            hs_ref[pl.ds(t, 1), :] = h
            return h

        h = lax.fori_loop(0, tc, step, h_sc[0:1, :], unroll=8)
        h_sc[...] = jnp.broadcast_to(h, h_sc.shape)
        gel, _ = _gelu(lg_ref[...])
        out_ref[...] = gel * hs_ref[...]

    row = lambda i: (i, 0)
    const = lambda i: (0, 0)
    return _pcall(
        body, name=name, grid=(nc,),
        in_specs=[pl.BlockSpec((tc, LRU_W), row), pl.BlockSpec((tc, LRU_W), lambda i: (i, 1)),
                  pl.BlockSpec((CONV_K, LRU_W), const), pl.BlockSpec((1, LRU_W), const),
                  pl.BlockSpec((LRU_W, 2 * LRU_W), const), pl.BlockSpec((1, 2 * LRU_W), const),
                  pl.BlockSpec((1, LRU_W), const)],
        out_specs=[pl.BlockSpec((tc, LRU_W), row)] * 3,
        out_shape=[jax.ShapeDtypeStruct((T, LRU_W), f32)] * 3,
        scratch_shapes=[pltpu.VMEM((tc + 8, LRU_W), f32), pltpu.VMEM((tc, LRU_W), f32),
                        pltpu.VMEM((tc, LRU_W), f32), pltpu.VMEM((8, LRU_W), f32)],
        compiler_params=_params(1),
    )(lxg, lxg, conv_w, conv_b, wab, bab, lam)


def _lru_bwd(dlru, lxg, u, hs, conv_w, wab, bab, lam, *, name, tc=512):
    T = lxg.shape[0]
    tc = min(tc, T)
    nc = T // tc
    bp = tc // 8

    def body(dl_ref, lx_ref, lxp_ref, lg_ref, u_ref, hs_ref, hsp_ref, cw_ref, wab_ref, bab_ref, lam_ref,
             dlxg_ref, dwab_ref, dbab_ref, dcw_ref, dcb_ref, dlam_ref,
             dh_sc, a_sc, ext, du_ext, carry):
        i = pl.program_id(0)
        first_chunk = i == nc - 1

        @pl.when(i == 0)
        def _():
            dwab_ref[...] = jnp.zeros_like(dwab_ref)
            dbab_ref[...] = jnp.zeros_like(dbab_ref)
            dcw_ref[...] = jnp.zeros_like(dcw_ref)
            dcb_ref[...] = jnp.zeros_like(dcb_ref)
            dlam_ref[...] = jnp.zeros_like(dlam_ref)
            carry[...] = jnp.zeros_like(carry)
            du_ext[tc:tc + 8, :] = jnp.zeros((8, LRU_W), f32)

        lg = lg_ref[...]
        gel, th = _gelu(lg)
        dl = dl_ref[...]
        hs = hs_ref[...]
        dlg = dl * hs * _gelu_grad(lg, th)
        u = u_ref[...]
        r, gi, sp, a, s = _lru_gates(u, wab_ref, bab_ref, lam_ref)
        a_sc[...] = a
        dh_sc[...] = dl * gel

        def step(k, c):
            t = tc - 1 - k
            dh = dh_sc[pl.ds(t, 1), :] + c
            dh_sc[pl.ds(t, 1), :] = dh
            return a_sc[pl.ds(t, 1), :] * dh

        c = lax.fori_loop(0, tc, step, carry[0:1, :], unroll=8)
        carry[...] = jnp.broadcast_to(c, carry.shape)

        ext[0:8, :] = jnp.where(first_chunk, 0.0, hsp_ref[...])
        ext[8:, :] = hs
        hprev = ext[pl.ds(7, tc), :]
        dh = dh_sc[...]
        da = dh * hprev
        giu = gi * u
        dla = da * a - (dh * giu) * (a * a / s)
        dgi = dh * s * u
        du = dh * s * gi
        dr = dla * (-LRU_C * sp)
        dlam_ref[...] += jnp.sum(dla * (-LRU_C * r), axis=0, keepdims=True) * (-_sigmoid(-lam_ref[...]))
        dpre = jnp.concatenate([dr * r * (1.0 - r), dgi * gi * (1.0 - gi)], axis=1)
        dpre_b = dpre.astype(MXU_DTYPE)
        du = du + _dot_nt(dpre_b, wab_ref[...])
        dwab_ref[...] += _dot_tn(u.astype(MXU_DTYPE), dpre_b)
        dbab_ref[...] += jnp.sum(dpre, axis=0, keepdims=True)
        dcb_ref[...] += jnp.sum(du, axis=0, keepdims=True)

        du_ext[0:tc, :] = du
        dlx = cw_ref[0:1, :] * du_ext[pl.ds(3, tc), :]
        for k in range(1, CONV_K):
            dlx = dlx + cw_ref[k:k + 1, :] * du_ext[pl.ds(3 - k, tc), :]
        du_ext[tc:tc + 8, :] = du_ext[0:8, :]
        ext[0:8, :] = jnp.where(first_chunk, 0.0, lxp_ref[...])
        ext[8:, :] = lx_ref[...]
        for k in range(CONV_K):
            dcw_ref[k:k + 1, :] += jnp.sum(du * ext[pl.ds(5 + k, tc), :], axis=0, keepdims=True)
        dlxg_ref[:, :LRU_W] = dlx.astype(dlxg_ref.dtype)
        dlxg_ref[:, LRU_W:] = dlg.astype(dlxg_ref.dtype)

    rev = lambda i: (nc - 1 - i, 0)
    prev8 = lambda i: (jnp.maximum((nc - 1 - i) * bp - 1, 0), 0)
    const = lambda i: (0, 0)
    return _pcall(
        body, name=name, grid=(nc,),
        in_specs=[
            pl.BlockSpec((tc, LRU_W), rev),
            pl.BlockSpec((tc, LRU_W), rev),
            pl.BlockSpec((8, LRU_W), prev8),
            pl.BlockSpec((tc, LRU_W), lambda i: (nc - 1 - i, 1)),
            pl.BlockSpec((tc, LRU_W), rev),
            pl.BlockSpec((tc, LRU_W), rev),
            pl.BlockSpec((8, LRU_W), prev8),
            pl.BlockSpec((CONV_K, LRU_W), const),
            pl.BlockSpec((LRU_W, 2 * LRU_W), const),
            pl.BlockSpec((1, 2 * LRU_W), const),
            pl.BlockSpec((1, LRU_W), const),
        ],
        out_specs=[
            pl.BlockSpec((tc, 2 * LRU_W), rev),
            pl.BlockSpec((LRU_W, 2 * LRU_W), const),
            pl.BlockSpec((1, 2 * LRU_W), const),
            pl.BlockSpec((8, LRU_W), const),
            pl.BlockSpec((1, LRU_W), const),
            pl.BlockSpec((1, LRU_W), const),
        ],
        out_shape=[
            jax.ShapeDtypeStruct((T, 2 * LRU_W), MXU_DTYPE),
            jax.ShapeDtypeStruct((LRU_W, 2 * LRU_W), f32),
            jax.ShapeDtypeStruct((1, 2 * LRU_W), f32),
            jax.ShapeDtypeStruct((8, LRU_W), f32),
            jax.ShapeDtypeStruct((1, LRU_W), f32),
            jax.ShapeDtypeStruct((1, LRU_W), f32),
        ],
        scratch_shapes=[pltpu.VMEM((tc, LRU_W), f32), pltpu.VMEM((tc, LRU_W), f32),
                        pltpu.VMEM((tc + 8, LRU_W), f32), pltpu.VMEM((tc + 8, LRU_W), f32),
                        pltpu.VMEM((8, LRU_W), f32)],
        compiler_params=_params(1),
    )(dlru, lxg, lxg, lxg, u, hs, hs, conv_w, wab, bab, lam)


def _mix_out(fox, lru, wo, xhat1, g1, b1, g2, b2, *, name, tm=512):
    T = fox.shape[0]
    tm = min(tm, T)
    nt = T // tm

    def body(fox_ref, lru_ref, wo_ref, xh_ref, g1_ref, b1_ref, g2_ref, b2_ref, xhat_ref, xn_ref, rstd_ref):
        mix = _dot(fox_ref[...].astype(MXU_DTYPE), wo_ref[:FOX_W, :])
        mix = mix + _dot(lru_ref[...].astype(MXU_DTYPE), wo_ref[FOX_W:, :])
        x1 = xh_ref[...] * g1_ref[...] + b1_ref[...]
        xhat, rstd = _layer_norm_stats(DN_ALPHA * x1 + mix)
        xhat_ref[...] = xhat
        xn_ref[...] = xhat * g2_ref[...] + b2_ref[...]
        rstd_ref[...] = jnp.broadcast_to(rstd, rstd_ref.shape)

    row = lambda i: (i, 0)
    const = lambda i: (0, 0)
    vec = pl.BlockSpec((1, D_MODEL), const)
    return _pcall(
        body, name=name, grid=(nt,),
        in_specs=[pl.BlockSpec((tm, FOX_W), row), pl.BlockSpec((tm, LRU_W), row),
                  pl.BlockSpec((D_MODEL, D_MODEL), const), pl.BlockSpec((tm, D_MODEL), row), vec, vec, vec, vec],
        out_specs=[pl.BlockSpec((tm, D_MODEL), row), pl.BlockSpec((tm, D_MODEL), row),
                   pl.BlockSpec((tm, LANES), row)],
        out_shape=[jax.ShapeDtypeStruct((T, D_MODEL), f32), jax.ShapeDtypeStruct((T, D_MODEL), f32),
                   jax.ShapeDtypeStruct((T, LANES), f32)],
        compiler_params=_params(1),
    )(fox, lru, wo, xhat1, g1, b1, g2, b2)


def _mix_out_bwd(dyp, fox, lru, wo, *, name, tm=512):
    T = fox.shape[0]
    tm = min(tm, T)
    nt = T // tm

    def body(dyp_ref, fox_ref, lru_ref, wo_ref, dfox_ref, dlru_ref, dwo_ref):
        i = pl.program_id(0)

        @pl.when(i == 0)
        def _():
            dwo_ref[...] = jnp.zeros_like(dwo_ref)

        dmix = dyp_ref[...].astype(MXU_DTYPE)
        dcat = _dot_nt(dmix, wo_ref[...])
        dfox_ref[...] = dcat[:, :FOX_W].astype(dfox_ref.dtype)
        dlru_ref[...] = dcat[:, FOX_W:]
        dwo_ref[:FOX_W, :] += _dot_tn(fox_ref[...].astype(MXU_DTYPE), dmix)
        dwo_ref[FOX_W:, :] += _dot_tn(lru_ref[...].astype(MXU_DTYPE), dmix)

    row = lambda i: (i, 0)
    const = lambda i: (0, 0)
    return _pcall(
        body, name=name, grid=(nt,),
        in_specs=[pl.BlockSpec((tm, D_MODEL), row), pl.BlockSpec((tm, FOX_W), row), pl.BlockSpec((tm, LRU_W), row),
                  pl.BlockSpec((D_MODEL, D_MODEL), const)],
        out_specs=[pl.BlockSpec((tm, FOX_W), row), pl.BlockSpec((tm, LRU_W), row),
                   pl.BlockSpec((D_MODEL, D_MODEL), const)],
        out_shape=[jax.ShapeDtypeStruct((T, FOX_W), MXU_DTYPE), jax.ShapeDtypeStruct((T, LRU_W), f32),
                   jax.ShapeDtypeStruct((D_MODEL, D_MODEL), f32)],
        compiler_params=_params(1),
    )(dyp, fox, lru, wo)


def make_wp(w_in):
    scale = jnp.concatenate([jnp.full((FOX_W,), 1.0 / math.sqrt(HEAD_DIM), w_in.dtype),
                             jnp.ones((IN_COLS - FOX_W,), w_in.dtype)])
    return jnp.pad(w_in * scale[None, :], ((0, 0), (0, Z_PAD - IN_COLS)))


def _block_diag(w):
    eye = jnp.eye(HEADS, dtype=w.dtype)
    return jnp.einsum("hij,hg->higj", w, eye).reshape(LRU_W, LRU_W)


def _block_diag_extract(m):
    m4 = m.reshape(HEADS, HEAD_DIM, HEADS, HEAD_DIM)
    return jnp.stack([m4[h, :, h, :] for h in range(HEADS)])


def _local_step(x, target, w):
    wp = w["wp"]
    bfp = w["bfp"]
    wab = jnp.concatenate([_block_diag(w["rg_wa"]), _block_diag(w["rg_wx"])], axis=1).astype(MXU_DTYPE)
    bab = jnp.concatenate([w["rg_ba"].reshape(1, LRU_W), w["rg_bx"].reshape(1, LRU_W)], axis=1)

    xb0, g1a, u1a, xhat1, xn1, rstd1 = _ffn_fwd(x, w["f1g"], w["f1u"], w["f1d"], w["ln1_g"], w["ln1_b"], name="ffn1_fwd")
    qkv, lxg, fgb = _proj_in(xn1, wp, bfp, name="proj_in")
    crep, ct = _fox_prep(fgb, name="fox_prep")
    fox, lse = _fox_fwd(qkv, crep, ct, name="fox_fwd")
    lru, uconv, hs = _lru_fwd(lxg, w["conv_w"], w["conv_b"], wab, bab, w["lam"], name="lru_fwd")
    xhat2, x2, rstd2 = _mix_out(fox, lru, w["wo"], xhat1, w["ln1_g"], w["ln1_b"], w["ln2_g"], w["ln2_b"], name="mix_out")
    xb2, g2a, u2a, xhat3, _, rstd3 = _ffn_fwd(x2, w["f2g"], w["f2u"], w["f2d"], w["ln3_g"], w["ln3_b"], name="ffn2_fwd")

    dy3p, dln3g, dln3b, loss = _loss_ln_bwd(xhat3, rstd3, w["ln3_g"], w["ln3_b"], target, name="loss_ln3_bwd")
    dx2, df2g, df2u, df2d = _ffn_bwd(dy3p, xb2, g2a, u2a, w["f2g"], w["f2u"], w["f2d"], name="ffn2_bwd")
    dy2p, dln2g, dln2b = _ln_bwd(dx2, xhat2, rstd2, w["ln2_g"], name="ln2_bwd")
    dfox, dlru, dwo = _mix_out_bwd(dy2p, fox, lru, w["wo"], name="mix_out_bwd")
    dlxg, dwab, dbab, dcw, dcb, dlam = _lru_bwd(dlru, lxg, uconv, hs, w["conv_w"], wab, bab, w["lam"], name="lru_bwd")
    drep = _fox_bwd_prep(dfox, fox, name="fox_bwd_prep")
    dq, drow, dk, dv, dcol = _fox_bwd(qkv, dfox, crep, ct, lse, drep, name="fox_bwd")
    dfg, dbf = _fox_bwd_post(drow, dcol, fgb, name="fox_bwd_post")
    dx1, dwp = _proj_in_bwd(dq, dk, dv, dlxg, dfg, xn1, dy2p, wp, name="proj_in_bwd")
    dy1p, dln1g, dln1b = _ln_bwd(dx1, xhat1, rstd1, w["ln1_g"], name="ln1_bwd")
    dx, df1g, df1u, df1d = _ffn_bwd(dy1p, xb0, g1a, u1a, w["f1g"], w["f1u"], w["f1d"], name="ffn1_bwd")

    grads = dict(
        f1g=df1g, f1u=df1u, f1d=df1d, f2g=df2g, f2u=df2u, f2d=df2d, wp=dwp, wo=dwo,
        ln1_g=dln1g, ln1_b=dln1b, ln2_g=dln2g, ln2_b=dln2b, ln3_g=dln3g, ln3_b=dln3b,
        b_forget=dbf[:, :HEADS], conv_w=dcw[:CONV_K], conv_b=dcb,
        rg_wa=_block_diag_extract(dwab[:, :LRU_W]), rg_wx=_block_diag_extract(dwab[:, LRU_W:]),
        rg_ba=dbab[:, :LRU_W].reshape(HEADS, HEAD_DIM), rg_bx=dbab[:, LRU_W:].reshape(HEADS, HEAD_DIM),
        lam=dlam,
    )
    return loss, dx, grads


MESH = pl.DeviceIdType.MESH
HBM_SPEC = pl.BlockSpec(memory_space=pl.ANY)
VMEM_SPEC = pl.BlockSpec(memory_space=pltpu.VMEM)


def _position():
    return lax.axis_index("x"), lax.axis_index("y"), lax.axis_index("c")


def _other_chips(x, y):
    return [(1 - x, y), (x, 1 - y), (1 - x, 1 - y)]


def _all_gather_bf16(shards, *, name):
    n = len(shards)

    def body(*refs):
        ins, outs, stages = refs[:n], refs[n:2 * n], refs[2 * n:3 * n]
        send_sems, recv_sems, local_sems = refs[3 * n:]
        x, y, c = _position()
        me, sibling = (x, y, c), (x, y, 1 - c)
        chips = _other_chips(x, y)

        def rows(k, px, py, pc):
            r = shards[k].shape[0]
            m = r // 2
            return outs[k].at[pl.ds(pl.multiple_of((2 * px + py) * r + pc * m, 16), m), :]

        def copy(k, idx, block, to, src=None):
            return pltpu.make_async_remote_copy(
                src_ref=rows(k, *block) if src is None else src, dst_ref=rows(k, *block),
                send_sem=send_sems.at[7 * k + idx], recv_sem=recv_sems.at[7 * k + idx],
                device_id=to, device_id_type=MESH)

        started = []
        mine = []
        for k in range(n):
            m = shards[k].shape[0] // 2
            stages[k][...] = ins[k][pl.ds(pl.multiple_of(c * m, 16), m), :].astype(stages[k].dtype)
            cp = pltpu.make_async_copy(stages[k], rows(k, *me), local_sems.at[k])
            cp.start()
            mine.append(cp)
            first = [copy(k, 0, me, sibling, src=stages[k])]
            first += [copy(k, 1 + j, me, (*chip, c), src=stages[k]) for j, chip in enumerate(chips)]
            for cp in first:
                cp.start()
            started += first
        for k in range(n):
            for j, chip in enumerate(chips):
                copy(k, 1 + j, (*chip, c), me).wait_recv()
                fwd = copy(k, 4 + j, (*chip, c), sibling)
                fwd.start()
                started.append(fwd)
        for k in range(n):
            copy(k, 0, sibling, me).wait_recv()
            for j, chip in enumerate(chips):
                copy(k, 4 + j, (*chip, 1 - c), me).wait_recv()
        for cp in started:
            cp.wait_send()
        for cp in mine:
            cp.wait()

    return _pcall(
        body, name=name,
        in_specs=[VMEM_SPEC] * n, out_specs=[HBM_SPEC] * n,
        out_shape=[jax.ShapeDtypeStruct((N_SHARD * s.shape[0], s.shape[1]), MXU_DTYPE) for s in shards],
        scratch_shapes=[pltpu.VMEM((s.shape[0] // 2, s.shape[1]), MXU_DTYPE) for s in shards]
        + [pltpu.SemaphoreType.DMA((7 * n,)), pltpu.SemaphoreType.DMA((7 * n,)), pltpu.SemaphoreType.DMA((n,))],
        compiler_params=pltpu.CompilerParams(vmem_limit_bytes=VMEM_LIMIT),
    )(*shards)


def _swap_halves(gs, *, name):
    n = len(gs)

    def body(*refs):
        ins, outs = refs[:n], refs[n:2 * n]
        send_sems, recv_sems = refs[2 * n:]
        x, y, c = _position()
        cps = []
        for k in range(n):
            m = gs[k].shape[1] // 2
            src = ins[k].at[:, pl.ds(pl.multiple_of((1 - c) * m, 16), m), :]
            cp = pltpu.make_async_remote_copy(src_ref=src, dst_ref=outs[k], send_sem=send_sems.at[k],
                                              recv_sem=recv_sems.at[k], device_id=(x, y, 1 - c), device_id_type=MESH)
            cp.start()
            cps.append(cp)
        for cp in cps:
            cp.wait()

    return _pcall(
        body, name=name, in_specs=[HBM_SPEC] * n, out_specs=[HBM_SPEC] * n,
        out_shape=[jax.ShapeDtypeStruct((g.shape[0], g.shape[1] // 2, g.shape[2]), g.dtype) for g in gs],
        scratch_shapes=[pltpu.SemaphoreType.DMA((n,)), pltpu.SemaphoreType.DMA((n,))],
    )(*gs)


def _add_halves(gs, recvs, *, name, tm=256):
    n = len(gs)
    _, r, cdim = gs[0].shape
    m = r // 2
    tm = min(tm, m)
    nb = m // tm
    c_idx = lax.axis_index("c").astype(jnp.int32).reshape(1)

    def body(c_ref, *refs):
        for k in range(n):
            refs[2 * n + k][...] = (refs[k][...].astype(f32) + refs[n + k][...].astype(f32)).astype(refs[2 * n + k].dtype)

    mine = pl.BlockSpec((None, tm, cdim), lambda j, i, c_ref: (j, c_ref[0] * nb + i, 0))
    half = pl.BlockSpec((None, tm, cdim), lambda j, i, c_ref: (j, i, 0))
    return _pcall(
        body, name=name,
        grid_spec=pltpu.PrefetchScalarGridSpec(
            num_scalar_prefetch=1, grid=(N_SHARD, nb),
            in_specs=[mine] * n + [half] * n, out_specs=[half] * n),
        out_shape=[jax.ShapeDtypeStruct((N_SHARD, m, cdim), g.dtype) for g in gs],
        compiler_params=_params(2),
    )(c_idx, *gs, *recvs)


def _scatter_partials(ps, *, name):
    n = len(ps)

    def body(*refs):
        ins, outs = refs[:n], refs[n:2 * n]
        send_sems, recv_sems, local_sems = refs[2 * n:]
        x, y, c = _position()
        me_chip = 2 * x + y
        chips = _other_chips(x, y)
        cps = []
        for k in range(n):
            cp = pltpu.make_async_copy(ins[k].at[me_chip], outs[k].at[me_chip], local_sems.at[k])
            cp.start()
            cps.append(cp)
            for j, (px, py) in enumerate(chips):
                cp = pltpu.make_async_remote_copy(
                    src_ref=ins[k].at[2 * px + py], dst_ref=outs[k].at[me_chip],
                    send_sem=send_sems.at[3 * k + j], recv_sem=recv_sems.at[3 * k + j],
                    device_id=(px, py, c), device_id_type=MESH)
                cp.start()
                cps.append(cp)
        for cp in cps:
            cp.wait()

    return _pcall(
        body, name=name, in_specs=[HBM_SPEC] * n, out_specs=[HBM_SPEC] * n,
        out_shape=[jax.ShapeDtypeStruct(p.shape, p.dtype) for p in ps],
        scratch_shapes=[pltpu.SemaphoreType.DMA((3 * n,)), pltpu.SemaphoreType.DMA((3 * n,)),
                        pltpu.SemaphoreType.DMA((n,))],
    )(*ps)


def _sum_slabs(qs, *, name, tm=128):
    n = len(qs)
    _, m, cdim = qs[0].shape
    tm = min(tm, m)

    def body(*refs):
        for k in range(n):
            q = refs[k]
            refs[n + k][...] = ((q[0].astype(f32) + q[1].astype(f32)) + q[2].astype(f32)) + q[3].astype(f32)

    return _pcall(
        body, name=name, grid=(m // tm,),
        in_specs=[pl.BlockSpec((N_SHARD, tm, cdim), lambda i: (0, i, 0))] * n,
        out_specs=[pl.BlockSpec((tm, cdim), lambda i: (i, 0))] * n,
        out_shape=[jax.ShapeDtypeStruct((m, cdim), f32) for _ in qs],
        compiler_params=_params(1),
    )(*qs)


def _join_halves(fs, *, name):
    n = len(fs)

    def body(*refs):
        ins, outs = refs[:n], refs[n:2 * n]
        send_sems, recv_sems, local_sems = refs[2 * n:]
        x, y, c = _position()
        cps = []
        for k in range(n):
            m = fs[k].shape[0]
            dst = outs[k].at[pl.ds(pl.multiple_of(c * m, 8), m), :]
            lc = pltpu.make_async_copy(ins[k], dst, local_sems.at[k])
            lc.start()
            cps.append(lc)
            cp = pltpu.make_async_remote_copy(src_ref=ins[k], dst_ref=dst, send_sem=send_sems.at[k],
                                              recv_sem=recv_sems.at[k], device_id=(x, y, 1 - c), device_id_type=MESH)
            cp.start()
            cps.append(cp)
        for cp in cps:
            cp.wait()

    return _pcall(
        body, name=name, in_specs=[HBM_SPEC] * n, out_specs=[HBM_SPEC] * n,
        out_shape=[jax.ShapeDtypeStruct((2 * f.shape[0], f.shape[1]), f.dtype) for f in fs],
        scratch_shapes=[pltpu.SemaphoreType.DMA((n,)), pltpu.SemaphoreType.DMA((n,)), pltpu.SemaphoreType.DMA((n,))],
    )(*fs)


def _all_reduce_small(v, *, name):
    r = v.shape[0]

    def body(v_ref, out_ref, buf, send_sems, recv_sems, local_sem):
        x, y, c = _position()
        me, sibling = (x, y, c), (x, y, 1 - c)
        chips = _other_chips(x, y)

        def rows(px, py, pc):
            return buf.at[pl.ds(pl.multiple_of((4 * px + 2 * py + pc) * r, 8), r), :]

        def copy(k, block, to, src=None):
            return pltpu.make_async_remote_copy(
                src_ref=rows(*block) if src is None else src, dst_ref=rows(*block),
                send_sem=send_sems.at[k], recv_sem=recv_sems.at[k], device_id=to, device_id_type=MESH)

        mine = pltpu.make_async_copy(v_ref, rows(*me), local_sem)
        mine.start()
        first = [copy(0, me, sibling, src=v_ref)]
        first += [copy(1 + j, me, (*chip, c), src=v_ref) for j, chip in enumerate(chips)]
        for cp in first:
            cp.start()
        passed = [copy(4 + j, (*chip, c), sibling) for j, chip in enumerate(chips)]
        for j, chip in enumerate(chips):
            copy(1 + j, (*chip, c), me).wait_recv()
            passed[j].start()
        copy(0, sibling, me).wait_recv()
        for j, chip in enumerate(chips):
            copy(4 + j, (*chip, 1 - c), me).wait_recv()
        for cp in first + passed:
            cp.wait_send()
        mine.wait()
        acc = buf[0:r, :]
        for d in range(1, N_DEV):
            acc = acc + buf[d * r:(d + 1) * r, :]
        out_ref[...] = acc

    return _pcall(
        body, name=name, in_specs=[VMEM_SPEC], out_specs=VMEM_SPEC,
        out_shape=jax.ShapeDtypeStruct((r, LANES), f32),
        scratch_shapes=[pltpu.VMEM((N_DEV * r, LANES), f32), pltpu.SemaphoreType.DMA((7,)),
                        pltpu.SemaphoreType.DMA((7,)), pltpu.SemaphoreType.DMA],
    )(v)


def _reduce_to_owner(gs, tag):
    recvs = _swap_halves(gs, name=f"rs_swap_{tag}")
    ps = _add_halves(gs, recvs, name=f"rs_add_{tag}")
    qs = _scatter_partials(ps, name=f"rs_scatter_{tag}")
    fs = _sum_slabs(qs, name=f"rs_sum_{tag}")
    return _join_halves(fs, name=f"rs_join_{tag}")


def _adamw(gs, ws, ms, vs, *, name, tm=256):
    n = len(gs)
    r, cdim = gs[0].shape
    tm = r if tm is None else min(tm, r)
    assert r % tm == 0, (r, tm)
    c1 = 1.0 / (1.0 - ADAM_B1 ** ADAM_STEP)
    c2 = 1.0 / (1.0 - ADAM_B2 ** ADAM_STEP)

    def body(*refs):
        for k in range(n):
            g = refs[k][...]
            w = refs[n + k][...]
            m = ADAM_B1 * refs[2 * n + k][...] + (1.0 - ADAM_B1) * g
            v = ADAM_B2 * refs[3 * n + k][...] + (1.0 - ADAM_B2) * (g * g)
            refs[4 * n + k][...] = -ADAM_LR * ((m * c1) / (jnp.sqrt(v * c2) + ADAM_EPS) + ADAM_WD * w)
            refs[5 * n + k][...] = m
            refs[6 * n + k][...] = v

    spec = pl.BlockSpec((tm, cdim), lambda i: (i, 0))
    outs = _pcall(
        body, name=name, grid=(r // tm,), in_specs=[spec] * (4 * n), out_specs=[spec] * (3 * n),
        out_shape=[jax.ShapeDtypeStruct((r, cdim), f32)] * (3 * n),
        compiler_params=_params(1),
    )(*gs, *ws, *ms, *vs)
    return outs[:n], outs[n:2 * n], outs[2 * n:]


BIG = ["ffn1_w_gate", "ffn1_w_up", "ffn1_w_down", "ffn2_w_gate", "ffn2_w_up", "ffn2_w_down"]
SMALL = ["ln1_g", "ln1_b", "b_forget", "conv_w", "conv_b", "rg_wa", "rg_ba", "rg_wx", "rg_bx", "lru_lambda",
         "ln2_g", "ln2_b", "ln3_g", "ln3_b"]
WEIGHTS = ["ffn1_w_gate", "ffn1_w_up", "ffn1_w_down", "ln1_g", "ln1_b", "w_in", "b_forget", "conv_w", "conv_b",
           "rg_wa", "rg_ba", "rg_wx", "rg_bx", "lru_lambda", "w_out", "ln2_g", "ln2_b",
           "ffn2_w_gate", "ffn2_w_up", "ffn2_w_down", "ln3_g", "ln3_b"]


def _pack_small(parts):
    rows = []
    for n in SMALL:
        flat = parts[n].reshape(-1)
        pad = (-flat.shape[0]) % LANES
        rows.append(jnp.pad(flat, (0, pad)).reshape(-1, LANES))
    packed = jnp.concatenate(rows, axis=0)
    return jnp.pad(packed, ((0, (-packed.shape[0]) % 8), (0, 0)))


def _unpack_small(packed, shapes):
    out, r0 = {}, 0
    for n in SMALL:
        size = math.prod(shapes[n])
        nr = -(-size // LANES)
        out[n] = packed[r0:r0 + nr].reshape(-1)[:size].reshape(shapes[n])
        r0 += nr
    return out


def kernel(x, ffn1_w_gate, ffn1_w_up, ffn1_w_down, ln1_g, ln1_b, w_in, b_forget, conv_w, conv_b, rg_wa, rg_ba, rg_wx, rg_bx, lru_lambda, w_out, ln2_g, ln2_b, ffn2_w_gate, ffn2_w_up, ffn2_w_down, ln3_g, ln3_b, loss_target, m_ffn1_w_gate, m_ffn1_w_up, m_ffn1_w_down, m_ln1_g, m_ln1_b, m_w_in, m_b_forget, m_conv_w, m_conv_b, m_rg_wa, m_rg_ba, m_rg_wx, m_rg_bx, m_lru_lambda, m_w_out, m_ln2_g, m_ln2_b, m_ffn2_w_gate, m_ffn2_w_up, m_ffn2_w_down, m_ln3_g, m_ln3_b, v_ffn1_w_gate, v_ffn1_w_up, v_ffn1_w_down, v_ln1_g, v_ln1_b, v_w_in, v_b_forget, v_conv_w, v_conv_b, v_rg_wa, v_rg_ba, v_rg_wx, v_rg_bx, v_lru_lambda, v_w_out, v_ln2_g, v_ln2_b, v_ffn2_w_gate, v_ffn2_w_up, v_ffn2_w_down, v_ln3_g, v_ln3_b):
    args = dict(locals())
    w = {n: args[n] for n in WEIGHTS}
    mom = {n: args["m_" + n] for n in WEIGHTS}
    var = {n: args["v_" + n] for n in WEIGHTS}
    chip = 2 * lax.axis_index("x") + lax.axis_index("y")

    g1 = _all_gather_bf16([w[n][0] for n in BIG[:3]], name="ag_ffn1")
    g2 = _all_gather_bf16([w[n][0] for n in BIG[3:]] + [w["w_in"][0], w["w_out"][0]], name="ag_rest")
    fs = D_FF // N_SHARD
    w_in_full = g2[3].reshape(N_SHARD, D_MODEL, IN_SHARD).transpose(1, 0, 2).reshape(D_MODEL, IN_COLS)
    full = dict(
        f1g=g1[0].reshape(N_SHARD, D_MODEL, fs), f1u=g1[1].reshape(N_SHARD, D_MODEL, fs),
        f1d=g1[2].reshape(N_SHARD, fs, D_MODEL),
        f2g=g2[0].reshape(N_SHARD, D_MODEL, fs), f2u=g2[1].reshape(N_SHARD, D_MODEL, fs),
        f2d=g2[2].reshape(N_SHARD, fs, D_MODEL),
        wp=make_wp(w_in_full), bfp=jnp.pad(b_forget, ((0, 0), (0, LANES - HEADS))), wo=g2[4],
        ln1_g=ln1_g, ln1_b=ln1_b, ln2_g=ln2_g, ln2_b=ln2_b, ln3_g=ln3_g, ln3_b=ln3_b,
        conv_b=conv_b, rg_wa=rg_wa[0], rg_wx=rg_wx[0], rg_ba=rg_ba[0], rg_bx=rg_bx[0], lam=lru_lambda,
    )
    cw_place = lax.dynamic_update_slice(jnp.zeros((8, LRU_W), f32), conv_w[0] * 0.5, (0, chip * (LRU_W // N_SHARD)))
    full["conv_w"] = _all_reduce_small(cw_place.reshape(-1, LANES), name="ag_conv_w").reshape(8, LRU_W)[:CONV_K]

    loss_rep, dx, g = _local_step(x[0], loss_target[0], full)
    loss = lax.psum(loss_rep[0, 0], ("x", "y", "c"))

    red_ffn = _reduce_to_owner([g["f1g"], g["f1u"], g["f1d"], g["f2g"], g["f2u"], g["f2d"]], "ffn")
    gwin = g["wp"][:, :IN_COLS].reshape(D_MODEL, N_SHARD, IN_SHARD).transpose(1, 0, 2).astype(GRAD_DTYPE)
    red_win = _reduce_to_owner([gwin], "w_in")
    red_wo = _reduce_to_owner([g["wo"].reshape(N_SHARD, D_MODEL // N_SHARD, D_MODEL).astype(GRAD_DTYPE)], "w_out")
    grads = dict(zip(BIG, red_ffn))
    grads["w_in"] = red_win[0]
    grads["w_out"] = red_wo[0]

    small_shapes = {n: w[n].shape for n in SMALL}
    small_shapes["conv_w"] = (1, CONV_K, LRU_W)
    gsmall = dict(ln1_g=g["ln1_g"], ln1_b=g["ln1_b"], ln2_g=g["ln2_g"], ln2_b=g["ln2_b"], ln3_g=g["ln3_g"],
                  ln3_b=g["ln3_b"], b_forget=g["b_forget"], conv_w=g["conv_w"], conv_b=g["conv_b"],
                  rg_wa=g["rg_wa"], rg_wx=g["rg_wx"], rg_ba=g["rg_ba"], rg_bx=g["rg_bx"], lru_lambda=g["lam"])
    gs_red = _unpack_small(_all_reduce_small(_pack_small(gsmall), name="ar_small"), small_shapes)
    gs_red["conv_w"] = lax.dynamic_slice(gs_red["conv_w"], (0, 0, chip * (LRU_W // N_SHARD)),
                                         (1, CONV_K, LRU_W // N_SHARD))
    grads.update(gs_red)

    delta, new_m, new_v = {}, {}, {}
    d, nm, nv = _adamw([grads[n] for n in BIG], [w[n][0] for n in BIG], [mom[n][0] for n in BIG],
                       [var[n][0] for n in BIG], name="adamw_ffn", tm=64)
    for i, n in enumerate(BIG):
        delta[n], new_m[n], new_v[n] = d[i], nm[i], nv[i]
    for n in ("w_in", "w_out"):
        d, nm, nv = _adamw([grads[n]], [w[n][0]], [mom[n][0]], [var[n][0]], name="adamw_" + n)
        delta[n], new_m[n], new_v[n] = d[0], nm[0], nv[0]
    shard_shapes = {n: w[n].shape for n in SMALL}
    d, nm, nv = _adamw([_pack_small({n: grads[n] for n in SMALL})], [_pack_small({n: w[n] for n in SMALL})],
                       [_pack_small({n: mom[n] for n in SMALL})], [_pack_small({n: var[n] for n in SMALL})],
                       name="adamw_small", tm=None)
    for dst, packed in ((delta, d[0]), (new_m, nm[0]), (new_v, nv[0])):
        dst.update(_unpack_small(packed, shard_shapes))

    def shaped(tree, n):
        return tree[n].reshape(w[n].shape)

    return (loss, dx[None], *[shaped(grads, n) for n in WEIGHTS], *[shaped(delta, n) for n in WEIGHTS],
            *[shaped(new_m, n) for n in WEIGHTS], *[shaped(new_v, n) for n in WEIGHTS])
```

```python
import functools
import math

import jax
import jax.numpy as jnp
from jax import lax
from jax.experimental import pallas as pl
from jax.experimental.pallas import tpu as pltpu

f32 = jnp.float32
MXU_DTYPE = jnp.bfloat16
GRAD_DTYPE = jnp.bfloat16

D_MODEL = 1024
D_FF = 4096
N_SHARD = 4
N_DEV = 8
FOX_W = 512
LRU_W = 512
HEADS = 8
HEAD_DIM = 64
CONV_K = 4
IN_COLS = 2568
IN_SHARD = IN_COLS // N_SHARD
QKV_W = 3 * FOX_W
Z_PAD = 2688
LANES = 128
LN_EPS = 1e-5
DN_ALPHA = 2.0 ** 0.25
LRU_C = 8.0
NEG_BIG = -1e30
VMEM_LIMIT = 56 * 1024 * 1024

ADAM_LR = 0.001
ADAM_B1 = 0.9
ADAM_B2 = 0.999
ADAM_EPS = 1e-08
ADAM_WD = 0.01
ADAM_STEP = 10


def _pcall(body, **kw):
    return pl.pallas_call(body, **kw)


def _params(n_grid, vmem=VMEM_LIMIT):
    return pltpu.CompilerParams(dimension_semantics=("arbitrary",) * n_grid, vmem_limit_bytes=vmem)


def _dot(a, b):
    return jnp.dot(a, b, preferred_element_type=f32)


def _dot_nt(a, b):
    return lax.dot_general(a, b, (((1,), (1,)), ((), ())), preferred_element_type=f32)


def _dot_tn(a, b):
    return lax.dot_general(a, b, (((0,), (0,)), ((), ())), preferred_element_type=f32)


def _sigmoid(x):
    return 1.0 / (1.0 + jnp.exp(-x))


def _layer_norm_stats(y):
    mu = jnp.mean(y, axis=-1, keepdims=True)
    yc = y - mu
    var = jnp.mean(yc * yc, axis=-1, keepdims=True)
    rstd = lax.rsqrt(var + LN_EPS)
    return yc * rstd, rstd


def _ln_backward(dy, xhat, rstd, gamma):
    dxhat = dy * gamma
    m1 = jnp.mean(dxhat, axis=-1, keepdims=True)
    m2 = jnp.mean(dxhat * xhat, axis=-1, keepdims=True)
    dyp = rstd * (dxhat - m1 - xhat * m2)
    return dyp, jnp.sum(dy * xhat, axis=0, keepdims=True), jnp.sum(dy, axis=0, keepdims=True)


def _ffn_fwd(x, wg, wu, wd, ln_g, ln_b, *, name, tm=1024, tf=512):
    T = x.shape[0]
    tm = min(tm, T)
    fs = D_FF // N_SHARD
    cpf = fs // tf
    nf = D_FF // tf
    nt = T // tm

    def body(x_ref, wg_ref, wu_ref, wd_ref, g_ref, b_ref,
             xb_ref, gact_ref, uact_ref, xhat_ref, xn_ref, rstd_ref, acc_ref):
        f = pl.program_id(1)

        @pl.when(f == 0)
        def _():
            xb_ref[...] = x_ref[...].astype(MXU_DTYPE)
            acc_ref[...] = jnp.zeros_like(acc_ref)

        xb = xb_ref[...]
        g = _dot(xb, wg_ref[...])
        u = _dot(xb, wu_ref[...])
        h = (g * _sigmoid(g)) * u
        gact_ref[...] = g.astype(gact_ref.dtype)
        uact_ref[...] = u.astype(uact_ref.dtype)
        acc_ref[...] += _dot(h.astype(MXU_DTYPE), wd_ref[...])

        @pl.when(f == nf - 1)
        def _():
            y = DN_ALPHA * x_ref[...] + 0.5 * acc_ref[...]
            xhat, rstd = _layer_norm_stats(y)
            xhat_ref[...] = xhat
            xn_ref[...] = (xhat * g_ref[...] + b_ref[...]).astype(xn_ref.dtype)
            rstd_ref[...] = jnp.broadcast_to(rstd, rstd_ref.shape)

    row = lambda i, f: (i, 0)
    return _pcall(
        body, name=name, grid=(nt, nf),
        in_specs=[
            pl.BlockSpec((tm, D_MODEL), row),
            pl.BlockSpec((None, D_MODEL, tf), lambda i, f: (f // cpf, 0, f % cpf)),
            pl.BlockSpec((None, D_MODEL, tf), lambda i, f: (f // cpf, 0, f % cpf)),
            pl.BlockSpec((None, tf, D_MODEL), lambda i, f: (f // cpf, f % cpf, 0)),
            pl.BlockSpec((1, D_MODEL), lambda i, f: (0, 0)),
            pl.BlockSpec((1, D_MODEL), lambda i, f: (0, 0)),
        ],
        out_specs=[
            pl.BlockSpec((tm, D_MODEL), row),
            pl.BlockSpec((tm, tf), lambda i, f: (i, f)),
            pl.BlockSpec((tm, tf), lambda i, f: (i, f)),
            pl.BlockSpec((tm, D_MODEL), row),
            pl.BlockSpec((tm, D_MODEL), row),
            pl.BlockSpec((tm, LANES), row),
        ],
        out_shape=[
            jax.ShapeDtypeStruct((T, D_MODEL), MXU_DTYPE),
            jax.ShapeDtypeStruct((T, D_FF), MXU_DTYPE),
            jax.ShapeDtypeStruct((T, D_FF), MXU_DTYPE),
            jax.ShapeDtypeStruct((T, D_MODEL), f32),
            jax.ShapeDtypeStruct((T, D_MODEL), MXU_DTYPE),
            jax.ShapeDtypeStruct((T, LANES), f32),
        ],
        scratch_shapes=[pltpu.VMEM((tm, D_MODEL), f32)],
        compiler_params=_params(2),
    )(x, wg, wu, wd, ln_g, ln_b)


def _ffn_bwd(dyp, xb, gact, uact, wg, wu, wd, *, name, tm=512, tf=512):
    T = dyp.shape[0]
    tm = min(tm, T)
    fs = D_FF // N_SHARD
    cpf = fs // tf
    nf = D_FF // tf
    nt = T // tm

    def body(dyp_ref, xb_ref, g_ref, u_ref, wg_ref, wu_ref, wd_ref,
             dx_hbm, dwg_ref, dwu_ref, dwd_ref, dx_sc, dwg_sc, dwu_sc, dwd_sc, sem):
        f = pl.program_id(0)
        i = pl.program_id(1)
        rows = pl.ds(pl.multiple_of(i * tm, tm), tm)
        dyp_t = dyp_ref[...]
        dy = (0.5 * dyp_t).astype(MXU_DTYPE)

        @pl.when(i == 0)
        def _():
            dwg_sc[...] = jnp.zeros_like(dwg_sc)
            dwu_sc[...] = jnp.zeros_like(dwu_sc)
            dwd_sc[...] = jnp.zeros_like(dwd_sc)

        @pl.when(f == 0)
        def _():
            dx_sc[rows, :] = DN_ALPHA * dyp_t

        g = g_ref[...].astype(f32)
        u = u_ref[...].astype(f32)
        sig = _sigmoid(g)
        silu = g * sig
        dh = _dot_nt(dy, wd_ref[...])
        dg = (dh * u * (sig * (1.0 + g * (1.0 - sig)))).astype(MXU_DTYPE)
        du = (dh * silu).astype(MXU_DTYPE)
        hb = (silu * u).astype(MXU_DTYPE)
        dx_sc[rows, :] += _dot_nt(dg, wg_ref[...]) + _dot_nt(du, wu_ref[...])
        xb_t = xb_ref[...]
        dwg_sc[...] += _dot_tn(xb_t, dg)
        dwu_sc[...] += _dot_tn(xb_t, du)
        dwd_sc[...] += _dot_tn(hb, dy)

        @pl.when(i == nt - 1)
        def _():
            dwg_ref[...] = dwg_sc[...].astype(dwg_ref.dtype)
            dwu_ref[...] = dwu_sc[...].astype(dwu_ref.dtype)
            dwd_ref[...] = dwd_sc[...].astype(dwd_ref.dtype)

        @pl.when(jnp.logical_and(f == nf - 1, i == nt - 1))
        def _():
            cp = pltpu.make_async_copy(dx_sc, dx_hbm, sem)
            cp.start()
            cp.wait()

    row = lambda f, i: (i, 0)
    return _pcall(
        body, name=name, grid=(nf, nt),
        in_specs=[
            pl.BlockSpec((tm, D_MODEL), row),
            pl.BlockSpec((tm, D_MODEL), row),
            pl.BlockSpec((tm, tf), lambda f, i: (i, f)),
            pl.BlockSpec((tm, tf), lambda f, i: (i, f)),
            pl.BlockSpec((None, D_MODEL, tf), lambda f, i: (f // cpf, 0, f % cpf)),
            pl.BlockSpec((None, D_MODEL, tf), lambda f, i: (f // cpf, 0, f % cpf)),
            pl.BlockSpec((None, tf, D_MODEL), lambda f, i: (f // cpf, f % cpf, 0)),
        ],
        out_specs=[
            pl.BlockSpec(memory_space=pl.ANY),
            pl.BlockSpec((None, D_MODEL, tf), lambda f, i: (f // cpf, 0, f % cpf)),
            pl.BlockSpec((None, D_MODEL, tf), lambda f, i: (f // cpf, 0, f % cpf)),
            pl.BlockSpec((None, tf, D_MODEL), lambda f, i: (f // cpf, f % cpf, 0)),
        ],
        out_shape=[
            jax.ShapeDtypeStruct((T, D_MODEL), f32),
            jax.ShapeDtypeStruct((N_SHARD, D_MODEL, fs), GRAD_DTYPE),
            jax.ShapeDtypeStruct((N_SHARD, D_MODEL, fs), GRAD_DTYPE),
            jax.ShapeDtypeStruct((N_SHARD, fs, D_MODEL), GRAD_DTYPE),
        ],
        scratch_shapes=[pltpu.VMEM((T, D_MODEL), f32), pltpu.VMEM((D_MODEL, tf), f32),
                        pltpu.VMEM((D_MODEL, tf), f32), pltpu.VMEM((tf, D_MODEL), f32),
                        pltpu.SemaphoreType.DMA],
        compiler_params=_params(2),
    )(dyp, xb, gact, uact, wg, wu, wd)


def _loss_ln_bwd(xhat, rstd, ln_g, ln_b, target, *, name, tm=512):
    T = xhat.shape[0]
    tm = min(tm, T)
    nt = T // tm

    def body(xhat_ref, rstd_ref, g_ref, b_ref, t_ref, dyp_ref, dg_ref, db_ref, loss_ref):
        i = pl.program_id(0)

        @pl.when(i == 0)
        def _():
            dg_ref[...] = jnp.zeros_like(dg_ref)
            db_ref[...] = jnp.zeros_like(db_ref)
            loss_ref[...] = jnp.zeros_like(loss_ref)

        xhat_t = xhat_ref[...]
        gamma = g_ref[...]
        err = xhat_t * gamma + b_ref[...] - t_ref[...]
        sq = jnp.sum(jnp.sum(err * err, axis=0, keepdims=True), axis=1, keepdims=True)
        loss_ref[...] += jnp.broadcast_to(sq * (0.5 / D_MODEL), loss_ref.shape)
        dy = err * (1.0 / D_MODEL)
        dyp, dgam, dbeta = _ln_backward(dy, xhat_t, rstd_ref[:, 0:1], gamma)
        dyp_ref[...] = dyp
        dg_ref[...] += dgam
        db_ref[...] += dbeta

    row = lambda i: (i, 0)
    const = lambda i: (0, 0)
    return _pcall(
        body, name=name, grid=(nt,),
        in_specs=[pl.BlockSpec((tm, D_MODEL), row), pl.BlockSpec((tm, LANES), row),
                  pl.BlockSpec((1, D_MODEL), const), pl.BlockSpec((1, D_MODEL), const),
                  pl.BlockSpec((tm, D_MODEL), row)],
        out_specs=[pl.BlockSpec((tm, D_MODEL), row), pl.BlockSpec((1, D_MODEL), const),
                   pl.BlockSpec((1, D_MODEL), const), pl.BlockSpec((1, LANES), const)],
        out_shape=[jax.ShapeDtypeStruct((T, D_MODEL), f32), jax.ShapeDtypeStruct((1, D_MODEL), f32),
                   jax.ShapeDtypeStruct((1, D_MODEL), f32), jax.ShapeDtypeStruct((1, LANES), f32)],
        compiler_params=_params(1),
    )(xhat, rstd, ln_g, ln_b, target)


def _ln_bwd(dy, xhat, rstd, ln_g, *, name, tm=512):
    T = xhat.shape[0]
    tm = min(tm, T)
    nt = T // tm

    def body(dy_ref, xhat_ref, rstd_ref, g_ref, dyp_ref, dg_ref, db_ref):
        i = pl.program_id(0)

        @pl.when(i == 0)
        def _():
            dg_ref[...] = jnp.zeros_like(dg_ref)
            db_ref[...] = jnp.zeros_like(db_ref)

        dyp, dgam, dbeta = _ln_backward(dy_ref[...], xhat_ref[...], rstd_ref[:, 0:1], g_ref[...])
        dyp_ref[...] = dyp
        dg_ref[...] += dgam
        db_ref[...] += dbeta

    row = lambda i: (i, 0)
    const = lambda i: (0, 0)
    return _pcall(
        body, name=name, grid=(nt,),
        in_specs=[pl.BlockSpec((tm, D_MODEL), row), pl.BlockSpec((tm, D_MODEL), row),
                  pl.BlockSpec((tm, LANES), row), pl.BlockSpec((1, D_MODEL), const)],
        out_specs=[pl.BlockSpec((tm, D_MODEL), row), pl.BlockSpec((1, D_MODEL), const),
                   pl.BlockSpec((1, D_MODEL), const)],
        out_shape=[jax.ShapeDtypeStruct((T, D_MODEL), f32), jax.ShapeDtypeStruct((1, D_MODEL), f32),
                   jax.ShapeDtypeStruct((1, D_MODEL), f32)],
        compiler_params=_params(1),
    )(dy, xhat, rstd, ln_g)


def _proj_in(xn, wp, bfp, *, name, tm=512):
    T = xn.shape[0]
    tm = min(tm, T)
    nt = T // tm

    def body(x_ref, w_ref, b_ref, qkv_ref, lxg_ref, fg_ref):
        z = _dot(x_ref[...], w_ref[...])
        qkv_ref[...] = z[:, :QKV_W].astype(qkv_ref.dtype)
        lxg_ref[...] = z[:, QKV_W:QKV_W + 2 * LRU_W]
        fg_ref[...] = z[:, QKV_W + 2 * LRU_W:] + b_ref[...]

    row = lambda i: (i, 0)
    const = lambda i: (0, 0)
    return _pcall(
        body, name=name, grid=(nt,),
        in_specs=[pl.BlockSpec((tm, D_MODEL), row), pl.BlockSpec((D_MODEL, Z_PAD), const),
                  pl.BlockSpec((1, LANES), const)],
        out_specs=[pl.BlockSpec((tm, QKV_W), row), pl.BlockSpec((tm, 2 * LRU_W), row),
                   pl.BlockSpec((tm, LANES), row)],
        out_shape=[jax.ShapeDtypeStruct((T, QKV_W), MXU_DTYPE), jax.ShapeDtypeStruct((T, 2 * LRU_W), f32),
                   jax.ShapeDtypeStruct((T, LANES), f32)],
        compiler_params=_params(1),
    )(xn, wp, bfp)


def _proj_in_bwd(dq, dk, dv, dlxg, dfg, xn, dyp, wp, *, name, tm=512):
    T = xn.shape[0]
    tm = min(tm, T)
    nt = T // tm

    def body(dq_ref, dk_ref, dv_ref, dl_ref, dfg_ref, x_ref, dyp_ref, w_ref, dx_ref, dw_hbm, dw_sc, sem):
        i = pl.program_id(0)

        @pl.when(i == 0)
        def _():
            dw_sc[...] = jnp.zeros_like(dw_sc)

        dz = jnp.concatenate(
            [dq_ref[...].astype(MXU_DTYPE), dk_ref[...].astype(MXU_DTYPE), dv_ref[...].astype(MXU_DTYPE),
             dl_ref[...].astype(MXU_DTYPE), dfg_ref[...].astype(MXU_DTYPE)], axis=1)
        dx_ref[...] = DN_ALPHA * dyp_ref[...] + _dot_nt(dz, w_ref[...])
        dw_sc[...] += _dot_tn(x_ref[...], dz)

        @pl.when(i == nt - 1)
        def _():
            dw_sc[:, :FOX_W] = dw_sc[:, :FOX_W] * (1.0 / math.sqrt(HEAD_DIM))
            cp = pltpu.make_async_copy(dw_sc, dw_hbm, sem)
            cp.start()
            cp.wait()

    row = lambda i: (i, 0)
    const = lambda i: (0, 0)
    return _pcall(
        body, name=name, grid=(nt,),
        in_specs=[pl.BlockSpec((tm, FOX_W), row), pl.BlockSpec((tm, FOX_W), row), pl.BlockSpec((tm, FOX_W), row),
                  pl.BlockSpec((tm, 2 * LRU_W), row), pl.BlockSpec((tm, LANES), row),
                  pl.BlockSpec((tm, D_MODEL), row), pl.BlockSpec((tm, D_MODEL), row),
                  pl.BlockSpec((D_MODEL, Z_PAD), const)],
        out_specs=[pl.BlockSpec((tm, D_MODEL), row), pl.BlockSpec(memory_space=pl.ANY)],
        out_shape=[jax.ShapeDtypeStruct((T, D_MODEL), f32), jax.ShapeDtypeStruct((D_MODEL, Z_PAD), f32)],
        scratch_shapes=[pltpu.VMEM((D_MODEL, Z_PAD), f32), pltpu.SemaphoreType.DMA],
        compiler_params=_params(1),
    )(dq, dk, dv, dlxg, dfg, xn, dyp, wp)


def _split3(x):
    hi = x.astype(jnp.bfloat16)
    r1 = x - hi.astype(f32)
    mid = r1.astype(jnp.bfloat16)
    lo = (r1 - mid.astype(f32)).astype(jnp.bfloat16)
    return hi, mid, lo


def _tri_dot(tri, x):
    hi, mid, lo = _split3(x)
    return _dot(tri, hi) + _dot(tri, mid) + _dot(tri, lo)


def _fox_prep(fgb, *, name, tm=512):
    T = fgb.shape[0]
    tm = min(tm, T)
    nt = T // tm

    def body(fg_ref, crep_ref, ct_ref, carry):
        i = pl.program_id(0)

        @pl.when(i == 0)
        def _():
            carry[...] = jnp.zeros_like(carry)

        x = fg_ref[...]
        ls = jnp.minimum(x, 0.0) - jnp.log(1.0 + jnp.exp(-jnp.abs(x)))
        r = lax.broadcasted_iota(jnp.int32, (tm, tm), 0)
        c = lax.broadcasted_iota(jnp.int32, (tm, tm), 1)
        tri = jnp.where(r >= c, 1.0, 0.0).astype(jnp.bfloat16)
        cum = _tri_dot(tri, ls) + carry[0:1, :]
        carry[...] = jnp.broadcast_to(cum[tm - 1:tm, :], carry.shape)
        for h in range(HEADS):
            crep_ref[h] = jnp.broadcast_to(cum[:, h:h + 1], (tm, LANES))
        ct_ref[...] = cum.T[:HEADS, :]

    return _pcall(
        body, name=name, grid=(nt,),
        in_specs=[pl.BlockSpec((tm, LANES), lambda i: (i, 0))],
        out_specs=[pl.BlockSpec((HEADS, tm, LANES), lambda i: (0, i, 0)),
                   pl.BlockSpec((HEADS, tm), lambda i: (0, i))],
        out_shape=[jax.ShapeDtypeStruct((HEADS, T, LANES), f32), jax.ShapeDtypeStruct((HEADS, T), f32)],
        scratch_shapes=[pltpu.VMEM((8, LANES), f32)],
        compiler_params=_params(1),
    )(fgb)


def _head_mask(shape, h):
    lane = lax.broadcasted_iota(jnp.int32, shape, 1)
    return (lane < HEAD_DIM) if h == 0 else (lane >= HEAD_DIM)


def _causal_mask(qi, ki, tq, tk):
    r = lax.broadcasted_iota(jnp.int32, (tq, tk), 0)
    c = lax.broadcasted_iota(jnp.int32, (tq, tk), 1)
    return jnp.logical_and(qi == ki, c > r)


def _fox_fwd(qkv, crep, ct, *, name, tq=512):
    T = qkv.shape[0]
    tq = min(tq, T)
    tk = tq
    nq = T // tq
    rep = tk // LANES

    def body(q_ref, k_ref, v_ref, cq_ref, ct_ref, o_ref, lse_ref, m_sc, l_sc, acc_sc):
        j = pl.program_id(0)
        qi = pl.program_id(1)
        ki = pl.program_id(2)

        @pl.when(ki == 0)
        def _():
            m_sc[...] = jnp.full_like(m_sc, NEG_BIG)
            l_sc[...] = jnp.zeros_like(l_sc)
            acc_sc[...] = jnp.zeros_like(acc_sc)

        @pl.when(ki <= qi)
        def _():
            q2 = q_ref[...]
            k2 = k_ref[...]
            v2 = v_ref[...]
            future = _causal_mask(qi, ki, tq, tk)
            for h in range(2):
                qh = jnp.where(_head_mask(q2.shape, h), q2, jnp.zeros_like(q2))
                ck = ct_ref[pl.ds(2 * j + h, 1), :]
                s = _dot_nt(qh, k2) + (jnp.tile(cq_ref[h], (1, rep)) - ck)
                s = jnp.where(future, NEG_BIG, s)
                m_prev = m_sc[h]
                m_new = jnp.maximum(m_prev, jnp.max(s, axis=1, keepdims=True))
                p = jnp.exp(s - jnp.tile(m_new, (1, rep)))
                alpha = jnp.exp(m_prev - m_new)
                l_sc[h] = alpha * l_sc[h] + jnp.sum(p, axis=1, keepdims=True)
                acc_sc[h] = alpha * acc_sc[h] + _dot(p.astype(MXU_DTYPE), v2)
                m_sc[h] = m_new

        @pl.when(ki == qi)
        def _():
            o0 = acc_sc[0] / l_sc[0]
            o1 = acc_sc[1] / l_sc[1]
            o_ref[...] = jnp.where(_head_mask(o0.shape, 0), o0, o1)
            for h in range(2):
                lse_ref[h] = m_sc[h] + jnp.log(l_sc[h])

    kv = lambda j, qi, ki: jnp.minimum(ki, qi)
    return _pcall(
        body, name=name, grid=(HEADS // 2, nq, nq),
        in_specs=[
            pl.BlockSpec((tq, LANES), lambda j, qi, ki: (qi, j)),
            pl.BlockSpec((tk, LANES), lambda j, qi, ki: (kv(j, qi, ki), 4 + j)),
            pl.BlockSpec((tk, LANES), lambda j, qi, ki: (kv(j, qi, ki), 8 + j)),
            pl.BlockSpec((2, tq, LANES), lambda j, qi, ki: (j, qi, 0)),
            pl.BlockSpec((HEADS, tk), lambda j, qi, ki: (0, kv(j, qi, ki))),
        ],
        out_specs=[pl.BlockSpec((tq, LANES), lambda j, qi, ki: (qi, j)),
                   pl.BlockSpec((2, tq, LANES), lambda j, qi, ki: (j, qi, 0))],
        out_shape=[jax.ShapeDtypeStruct((T, FOX_W), f32), jax.ShapeDtypeStruct((HEADS, T, LANES), f32)],
        scratch_shapes=[pltpu.VMEM((2, tq, LANES), f32)] * 3,
        compiler_params=_params(3),
    )(qkv, qkv, qkv, crep, ct)


def _fox_bwd_prep(do, o, *, name, tm=512):
    T = o.shape[0]
    tm = min(tm, T)
    nt = T // tm

    def body(do_ref, o_ref, d_ref):
        prod = do_ref[...].astype(f32) * o_ref[...]
        for j in range(HEADS // 2):
            pj = prod[:, j * LANES:(j + 1) * LANES]
            for h in range(2):
                dsum = jnp.sum(jnp.where(_head_mask(pj.shape, h), pj, 0.0), axis=1, keepdims=True)
                d_ref[2 * j + h] = jnp.broadcast_to(dsum, (tm, LANES))

    return _pcall(
        body, name=name, grid=(nt,),
        in_specs=[pl.BlockSpec((tm, FOX_W), lambda i: (i, 0)), pl.BlockSpec((tm, FOX_W), lambda i: (i, 0))],
        out_specs=[pl.BlockSpec((HEADS, tm, LANES), lambda i: (0, i, 0))],
        out_shape=[jax.ShapeDtypeStruct((HEADS, T, LANES), f32)],
        compiler_params=_params(1),
    )(do, o)[0]


def _fox_bwd(qkv, do, crep, ct, lse, drep, *, name, tq=512):
    T = qkv.shape[0]
    tq = min(tq, T)
    tk = tq
    nq = T // tq
    rep = tk // LANES

    def body(q_ref, k_ref, v_ref, do_ref, cq_ref, ct_ref, lse_ref, d_ref,
             dq_ref, drow_ref, dk_ref, dv_ref, dcol_ref, dk_sc, dv_sc):
        j = pl.program_id(0)
        ki = pl.program_id(1)
        qi = pl.program_id(2)
        rows = pl.ds(pl.multiple_of(qi * tq, tq), tq)

        @pl.when(qi == 0)
        def _():
            dk_sc[...] = jnp.zeros_like(dk_sc)
            dv_sc[...] = jnp.zeros_like(dv_sc)

        @pl.when(jnp.logical_and(ki == 0, qi == 0))
        def _():
            dq_ref[...] = jnp.zeros_like(dq_ref)
            drow_ref[...] = jnp.zeros_like(drow_ref)

        @pl.when(qi >= ki)
        def _():
            q2 = q_ref[...]
            k2 = k_ref[...]
            v2 = v_ref[...]
            do2 = do_ref[...]
            future = _causal_mask(qi, ki, tq, tk)
            dq_acc = jnp.zeros((tq, LANES), f32)
            drow_acc = jnp.zeros((tq, LANES), f32)
            for h in range(2):
                hm = _head_mask(q2.shape, h)
                qh = jnp.where(hm, q2, jnp.zeros_like(q2))
                ck = ct_ref[pl.ds(2 * j + h, 1), :]
                s = _dot_nt(qh, k2) + (jnp.tile(cq_ref[h], (1, rep)) - ck)
                p = jnp.exp(s - jnp.tile(lse_ref[h], (1, rep)))
                p = jnp.where(future, 0.0, p)
                doh = jnp.where(hm, do2, jnp.zeros_like(do2))
                dp = _dot_nt(doh, v2)
                ds = (p * (dp - jnp.tile(d_ref[h], (1, rep)))).astype(MXU_DTYPE)
                dv_sc[h] += _dot_tn(p.astype(MXU_DTYPE), do2)
                q_ones = jnp.where(hm, q2, jnp.ones_like(q2))
                dk_sc[h] += _dot_tn(ds, q_ones)
                dq_full = _dot(ds, jnp.where(hm, k2, jnp.ones_like(k2)))
                dq_acc = dq_acc + jnp.where(hm, dq_full, 0.0)
                drow_acc = drow_acc + jnp.where(hm, 0.0, dq_full)
            dq_ref[rows, :] += dq_acc
            drow_ref[rows, :] += drow_acc

        @pl.when(qi == nq - 1)
        def _():
            hm0 = _head_mask((tk, LANES), 0)
            dk_ref[...] = jnp.where(hm0, dk_sc[0], dk_sc[1])
            dcol_ref[...] = jnp.where(hm0, dk_sc[1], dk_sc[0])
            dv_ref[...] = jnp.where(hm0, dv_sc[0], dv_sc[1])

    qb = lambda j, ki, qi: jnp.maximum(qi, ki)
    return _pcall(
        body, name=name, grid=(HEADS // 2, nq, nq),
        in_specs=[
            pl.BlockSpec((tq, LANES), lambda j, ki, qi: (qb(j, ki, qi), j)),
            pl.BlockSpec((tk, LANES), lambda j, ki, qi: (ki, 4 + j)),
            pl.BlockSpec((tk, LANES), lambda j, ki, qi: (ki, 8 + j)),
            pl.BlockSpec((tq, LANES), lambda j, ki, qi: (qb(j, ki, qi), j)),
            pl.BlockSpec((2, tq, LANES), lambda j, ki, qi: (j, qb(j, ki, qi), 0)),
            pl.BlockSpec((HEADS, tk), lambda j, ki, qi: (0, ki)),
            pl.BlockSpec((2, tq, LANES), lambda j, ki, qi: (j, qb(j, ki, qi), 0)),
            pl.BlockSpec((2, tq, LANES), lambda j, ki, qi: (j, qb(j, ki, qi), 0)),
        ],
        out_specs=[
            pl.BlockSpec((T, LANES), lambda j, ki, qi: (0, j)),
            pl.BlockSpec((T, LANES), lambda j, ki, qi: (0, j)),
            pl.BlockSpec((tk, LANES), lambda j, ki, qi: (ki, j)),
            pl.BlockSpec((tk, LANES), lambda j, ki, qi: (ki, j)),
            pl.BlockSpec((tk, LANES), lambda j, ki, qi: (ki, j)),
        ],
        out_shape=[jax.ShapeDtypeStruct((T, FOX_W), f32)] * 5,
        scratch_shapes=[pltpu.VMEM((2, tk, LANES), f32)] * 2,
        compiler_params=_params(3),
    )(qkv, qkv, qkv, do, crep, ct, lse, drep)


def _fox_bwd_post(drow, dcol, fgb, *, name, tm=512):
    T = fgb.shape[0]
    tm = min(tm, T)
    nt = T // tm

    def body(drow_ref, dcol_ref, fg_ref, dfg_ref, dbf_ref, carry):
        i = pl.program_id(0)

        @pl.when(i == 0)
        def _():
            carry[...] = jnp.zeros_like(carry)
            dbf_ref[...] = jnp.zeros_like(dbf_ref)

        dcol_t = drow_ref[...] - dcol_ref[...]
        lane = lax.broadcasted_iota(jnp.int32, (tm, LANES), 1)
        dc = jnp.zeros((tm, LANES), f32)
        for h in range(HEADS):
            src = (h // 2) * LANES + (HEAD_DIM if h % 2 == 0 else 0)
            dc = jnp.where(lane == h, jnp.broadcast_to(dcol_t[:, src:src + 1], (tm, LANES)), dc)
        r = lax.broadcasted_iota(jnp.int32, (tm, tm), 0)
        c = lax.broadcasted_iota(jnp.int32, (tm, tm), 1)
        tri = jnp.where(c >= r, 1.0, 0.0).astype(jnp.bfloat16)
        dls = _tri_dot(tri, dc) + carry[0:1, :]
        carry[...] = jnp.broadcast_to(dls[0:1, :], carry.shape)
        dfg = dls * _sigmoid(-fg_ref[...])
        dfg_ref[...] = dfg
        dbf_ref[...] += jnp.sum(dfg, axis=0, keepdims=True)

    rev = lambda i: (nt - 1 - i, 0)
    return _pcall(
        body, name=name, grid=(nt,),
        in_specs=[pl.BlockSpec((tm, FOX_W), rev), pl.BlockSpec((tm, FOX_W), rev), pl.BlockSpec((tm, LANES), rev)],
        out_specs=[pl.BlockSpec((tm, LANES), rev), pl.BlockSpec((1, LANES), lambda i: (0, 0))],
        out_shape=[jax.ShapeDtypeStruct((T, LANES), f32), jax.ShapeDtypeStruct((1, LANES), f32)],
        scratch_shapes=[pltpu.VMEM((8, LANES), f32)],
        compiler_params=_params(1),
    )(drow, dcol, fgb)


GELU_C = math.sqrt(2.0 / math.pi)
GELU_A = 0.044715


def _gelu(x):
    t = jnp.tanh(GELU_C * (x + GELU_A * x * x * x))
    return 0.5 * x * (1.0 + t), t


def _gelu_grad(x, t):
    return 0.5 * (1.0 + t) + 0.5 * x * (1.0 - t * t) * GELU_C * (1.0 + 3.0 * GELU_A * x * x)


def _expm1(x):
    e = jnp.exp(x)
    safe = jnp.where(e == 1.0, x, (e - 1.0) * x / jnp.log(jnp.where(e == 1.0, 0.5, e)))
    return jnp.where(x < -0.5, e - 1.0, safe)


def _lru_gates(u, wab_ref, bab_ref, lam_ref):
    pre = _dot(u.astype(MXU_DTYPE), wab_ref[...]) + bab_ref[...]
    r = _sigmoid(pre[:, :LRU_W])
    gi = _sigmoid(pre[:, LRU_W:])
    lam = lam_ref[...]
    sp = jnp.maximum(-lam, 0.0) + jnp.log(1.0 + jnp.exp(-jnp.abs(lam)))
    log_a = -LRU_C * r * sp
    a = jnp.exp(log_a)
    s = jnp.sqrt(-_expm1(2.0 * log_a))
    return r, gi, sp, a, s


def _lru_fwd(lxg, conv_w, conv_b, wab, bab, lam, *, name, tc=512):
    T = lxg.shape[0]
    tc = min(tc, T)
    nc = T // tc

    def body(lx_ref, lg_ref, cw_ref, cb_ref, wab_ref, bab_ref, lam_ref,
             out_ref, u_ref, hs_ref, ext, a_sc, b_sc, h_sc):
        i = pl.program_id(0)

        @pl.when(i == 0)
        def _():
            ext[0:8, :] = jnp.zeros((8, LRU_W), f32)
            h_sc[...] = jnp.zeros_like(h_sc)

        ext[8:, :] = lx_ref[...]
        u = cb_ref[...] + cw_ref[0:1, :] * ext[pl.ds(5, tc), :]
        for k in range(1, CONV_K):
            u = u + cw_ref[k:k + 1, :] * ext[pl.ds(5 + k, tc), :]
        ext[0:8, :] = ext[tc:tc + 8, :]
        u_ref[...] = u
        r, gi, sp, a, s = _lru_gates(u, wab_ref, bab_ref, lam_ref)
        a_sc[...] = a
        b_sc[...] = s * (gi * u)

        def step(t, h):
            h = a_sc[pl.ds(t, 1), :] * h + b_sc[pl.ds(t, 1), :]
            hs_ref[pl.ds(t, 1), :] = h
            return h

        h = lax.fori_loop(0, tc, step, h_sc[0:1, :], unroll=8)
        h_sc[...] = jnp.broadcast_to(h, h_sc.shape)
        gel, _ = _gelu(lg_ref[...])
        out_ref[...] = gel * hs_ref[...]

    row = lambda i: (i, 0)
    const = lambda i: (0, 0)
    return _pcall(
        body, name=name, grid=(nc,),
        in_specs=[pl.BlockSpec((tc, LRU_W), row), pl.BlockSpec((tc, LRU_W), lambda i: (i, 1)),
                  pl.BlockSpec((CONV_K, LRU_W), const), pl.BlockSpec((1, LRU_W), const),
                  pl.BlockSpec((LRU_W, 2 * LRU_W), const), pl.BlockSpec((1, 2 * LRU_W), const),
                  pl.BlockSpec((1, LRU_W), const)],
        out_specs=[pl.BlockSpec((tc, LRU_W), row)] * 3,
        out_shape=[jax.ShapeDtypeStruct((T, LRU_W), f32)] * 3,
        scratch_shapes=[pltpu.VMEM((tc + 8, LRU_W), f32), pltpu.VMEM((tc, LRU_W), f32),
                        pltpu.VMEM((tc, LRU_W), f32), pltpu.VMEM((8, LRU_W), f32)],
        compiler_params=_params(1),
    )(lxg, lxg, conv_w, conv_b, wab, bab, lam)


def _lru_bwd(dlru, lxg, u, hs, conv_w, wab, bab, lam, *, name, tc=512):
    T = lxg.shape[0]
    tc = min(tc, T)
    nc = T // tc
    bp = tc // 8

    def body(dl_ref, lx_ref, lxp_ref, lg_ref, u_ref, hs_ref, hsp_ref, cw_ref, wab_ref, bab_ref, lam_ref,
             dlxg_ref, dwab_ref, dbab_ref, dcw_ref, dcb_ref, dlam_ref,
             dh_sc, a_sc, ext, du_ext, carry):
        i = pl.program_id(0)
        first_chunk = i == nc - 1

        @pl.when(i == 0)
        def _():
            dwab_ref[...] = jnp.zeros_like(dwab_ref)
            dbab_ref[...] = jnp.zeros_like(dbab_ref)
            dcw_ref[...] = jnp.zeros_like(dcw_ref)
            dcb_ref[...] = jnp.zeros_like(dcb_ref)
            dlam_ref[...] = jnp.zeros_like(dlam_ref)
            carry[...] = jnp.zeros_like(carry)
            du_ext[tc:tc + 8, :] = jnp.zeros((8, LRU_W), f32)

        lg = lg_ref[...]
        gel, th = _gelu(lg)
        dl = dl_ref[...]
        hs = hs_ref[...]
        dlg = dl * hs * _gelu_grad(lg, th)
        u = u_ref[...]
        r, gi, sp, a, s = _lru_gates(u, wab_ref, bab_ref, lam_ref)
        a_sc[...] = a
        dh_sc[...] = dl * gel

        def step(k, c):
            t = tc - 1 - k
            dh = dh_sc[pl.ds(t, 1), :] + c
            dh_sc[pl.ds(t, 1), :] = dh
            return a_sc[pl.ds(t, 1), :] * dh

        c = lax.fori_loop(0, tc, step, carry[0:1, :], unroll=8)
        carry[...] = jnp.broadcast_to(c, carry.shape)

        ext[0:8, :] = jnp.where(first_chunk, 0.0, hsp_ref[...])
        ext[8:, :] = hs
        hprev = ext[pl.ds(7, tc), :]
        dh = dh_sc[...]
        da = dh * hprev
        giu = gi * u
        dla = da * a - (dh * giu) * (a * a / s)
        dgi = dh * s * u
        du = dh * s * gi
        dr = dla * (-LRU_C * sp)
        dlam_ref[...] += jnp.sum(dla * (-LRU_C * r), axis=0, keepdims=True) * (-_sigmoid(-lam_ref[...]))
        dpre = jnp.concatenate([dr * r * (1.0 - r), dgi * gi * (1.0 - gi)], axis=1)
        dpre_b = dpre.astype(MXU_DTYPE)
        du = du + _dot_nt(dpre_b, wab_ref[...])
        dwab_ref[...] += _dot_tn(u.astype(MXU_DTYPE), dpre_b)
        dbab_ref[...] += jnp.sum(dpre, axis=0, keepdims=True)
        dcb_ref[...] += jnp.sum(du, axis=0, keepdims=True)

        du_ext[0:tc, :] = du
        dlx = cw_ref[0:1, :] * du_ext[pl.ds(3, tc), :]
        for k in range(1, CONV_K):
            dlx = dlx + cw_ref[k:k + 1, :] * du_ext[pl.ds(3 - k, tc), :]
        du_ext[tc:tc + 8, :] = du_ext[0:8, :]
        ext[0:8, :] = jnp.where(first_chunk, 0.0, lxp_ref[...])
        ext[8:, :] = lx_ref[...]
        for k in range(CONV_K):
            dcw_ref[k:k + 1, :] += jnp.sum(du * ext[pl.ds(5 + k, tc), :], axis=0, keepdims=True)
        dlxg_ref[:, :LRU_W] = dlx.astype(dlxg_ref.dtype)
        dlxg_ref[:, LRU_W:] = dlg.astype(dlxg_ref.dtype)

    rev = lambda i: (nc - 1 - i, 0)
    prev8 = lambda i: (jnp.maximum((nc - 1 - i) * bp - 1, 0), 0)
    const = lambda i: (0, 0)
    return _pcall(
        body, name=name, grid=(nc,),
        in_specs=[
            pl.BlockSpec((tc, LRU_W), rev),
            pl.BlockSpec((tc, LRU_W), rev),
            pl.BlockSpec((8, LRU_W), prev8),
            pl.BlockSpec((tc, LRU_W), lambda i: (nc - 1 - i, 1)),
            pl.BlockSpec((tc, LRU_W), rev),
            pl.BlockSpec((tc, LRU_W), rev),
            pl.BlockSpec((8, LRU_W), prev8),
            pl.BlockSpec((CONV_K, LRU_W), const),
            pl.BlockSpec((LRU_W, 2 * LRU_W), const),
            pl.BlockSpec((1, 2 * LRU_W), const),
            pl.BlockSpec((1, LRU_W), const),
        ],
        out_specs=[
            pl.BlockSpec((tc, 2 * LRU_W), rev),
            pl.BlockSpec((LRU_W, 2 * LRU_W), const),
            pl.BlockSpec((1, 2 * LRU_W), const),
            pl.BlockSpec((8, LRU_W), const),
            pl.BlockSpec((1, LRU_W), const),
            pl.BlockSpec((1, LRU_W), const),
        ],
        out_shape=[
            jax.ShapeDtypeStruct((T, 2 * LRU_W), MXU_DTYPE),
            jax.ShapeDtypeStruct((LRU_W, 2 * LRU_W), f32),
            jax.ShapeDtypeStruct((1, 2 * LRU_W), f32),
            jax.ShapeDtypeStruct((8, LRU_W), f32),
            jax.ShapeDtypeStruct((1, LRU_W), f32),
            jax.ShapeDtypeStruct((1, LRU_W), f32),
        ],
        scratch_shapes=[pltpu.VMEM((tc, LRU_W), f32), pltpu.VMEM((tc, LRU_W), f32),
                        pltpu.VMEM((tc + 8, LRU_W), f32), pltpu.VMEM((tc + 8, LRU_W), f32),
                        pltpu.VMEM((8, LRU_W), f32)],
        compiler_params=_params(1),
    )(dlru, lxg, lxg, lxg, u, hs, hs, conv_w, wab, bab, lam)


def _mix_out(fox, lru, wo, xhat1, g1, b1, g2, b2, *, name, tm=512):
    T = fox.shape[0]
    tm = min(tm, T)
    nt = T // tm

    def body(fox_ref, lru_ref, wo_ref, xh_ref, g1_ref, b1_ref, g2_ref, b2_ref, xhat_ref, xn_ref, rstd_ref):
        mix = _dot(fox_ref[...].astype(MXU_DTYPE), wo_ref[:FOX_W, :])
        mix = mix + _dot(lru_ref[...].astype(MXU_DTYPE), wo_ref[FOX_W:, :])
        x1 = xh_ref[...] * g1_ref[...] + b1_ref[...]
        xhat, rstd = _layer_norm_stats(DN_ALPHA * x1 + mix)
        xhat_ref[...] = xhat
        xn_ref[...] = xhat * g2_ref[...] + b2_ref[...]
        rstd_ref[...] = jnp.broadcast_to(rstd, rstd_ref.shape)

    row = lambda i: (i, 0)
    const = lambda i: (0, 0)
    vec = pl.BlockSpec((1, D_MODEL), const)
    return _pcall(
        body, name=name, grid=(nt,),
        in_specs=[pl.BlockSpec((tm, FOX_W), row), pl.BlockSpec((tm, LRU_W), row),
                  pl.BlockSpec((D_MODEL, D_MODEL), const), pl.BlockSpec((tm, D_MODEL), row), vec, vec, vec, vec],
        out_specs=[pl.BlockSpec((tm, D_MODEL), row), pl.BlockSpec((tm, D_MODEL), row),
                   pl.BlockSpec((tm, LANES), row)],
        out_shape=[jax.ShapeDtypeStruct((T, D_MODEL), f32), jax.ShapeDtypeStruct((T, D_MODEL), f32),
                   jax.ShapeDtypeStruct((T, LANES), f32)],
        compiler_params=_params(1),
    )(fox, lru, wo, xhat1, g1, b1, g2, b2)


def _mix_out_bwd(dyp, fox, lru, wo, *, name, tm=512):
    T = fox.shape[0]
    tm = min(tm, T)
    nt = T // tm

    def body(dyp_ref, fox_ref, lru_ref, wo_ref, dfox_ref, dlru_ref, dwo_ref):
        i = pl.program_id(0)

        @pl.when(i == 0)
        def _():
            dwo_ref[...] = jnp.zeros_like(dwo_ref)

        dmix = dyp_ref[...].astype(MXU_DTYPE)
        dcat = _dot_nt(dmix, wo_ref[...])
        dfox_ref[...] = dcat[:, :FOX_W].astype(dfox_ref.dtype)
        dlru_ref[...] = dcat[:, FOX_W:]
        dwo_ref[:FOX_W, :] += _dot_tn(fox_ref[...].astype(MXU_DTYPE), dmix)
        dwo_ref[FOX_W:, :] += _dot_tn(lru_ref[...].astype(MXU_DTYPE), dmix)

    row = lambda i: (i, 0)
    const = lambda i: (0, 0)
    return _pcall(
        body, name=name, grid=(nt,),
        in_specs=[pl.BlockSpec((tm, D_MODEL), row), pl.BlockSpec((tm, FOX_W), row), pl.BlockSpec((tm, LRU_W), row),
                  pl.BlockSpec((D_MODEL, D_MODEL), const)],
        out_specs=[pl.BlockSpec((tm, FOX_W), row), pl.BlockSpec((tm, LRU_W), row),
                   pl.BlockSpec((D_MODEL, D_MODEL), const)],
        out_shape=[jax.ShapeDtypeStruct((T, FOX_W), MXU_DTYPE), jax.ShapeDtypeStruct((T, LRU_W), f32),
                   jax.ShapeDtypeStruct((D_MODEL, D_MODEL), f32)],
        compiler_params=_params(1),
    )(dyp, fox, lru, wo)


def make_wp(w_in):
    scale = jnp.concatenate([jnp.full((FOX_W,), 1.0 / math.sqrt(HEAD_DIM), w_in.dtype),
                             jnp.ones((IN_COLS - FOX_W,), w_in.dtype)])
    return jnp.pad(w_in * scale[None, :], ((0, 0), (0, Z_PAD - IN_COLS)))


def _block_diag(w):
    eye = jnp.eye(HEADS, dtype=w.dtype)
    return jnp.einsum("hij,hg->higj", w, eye).reshape(LRU_W, LRU_W)


def _block_diag_extract(m):
    m4 = m.reshape(HEADS, HEAD_DIM, HEADS, HEAD_DIM)
    return jnp.stack([m4[h, :, h, :] for h in range(HEADS)])


def _local_step(x, target, w):
    wp = w["wp"]
    bfp = w["bfp"]
    wab = jnp.concatenate([_block_diag(w["rg_wa"]), _block_diag(w["rg_wx"])], axis=1).astype(MXU_DTYPE)
    bab = jnp.concatenate([w["rg_ba"].reshape(1, LRU_W), w["rg_bx"].reshape(1, LRU_W)], axis=1)

    xb0, g1a, u1a, xhat1, xn1, rstd1 = _ffn_fwd(x, w["f1g"], w["f1u"], w["f1d"], w["ln1_g"], w["ln1_b"], name="ffn1_fwd")
    qkv, lxg, fgb = _proj_in(xn1, wp, bfp, name="proj_in")
    crep, ct = _fox_prep(fgb, name="fox_prep")
    fox, lse = _fox_fwd(qkv, crep, ct, name="fox_fwd")
    lru, uconv, hs = _lru_fwd(lxg, w["conv_w"], w["conv_b"], wab, bab, w["lam"], name="lru_fwd")
    xhat2, x2, rstd2 = _mix_out(fox, lru, w["wo"], xhat1, w["ln1_g"], w["ln1_b"], w["ln2_g"], w["ln2_b"], name="mix_out")
    xb2, g2a, u2a, xhat3, _, rstd3 = _ffn_fwd(x2, w["f2g"], w["f2u"], w["f2d"], w["ln3_g"], w["ln3_b"], name="ffn2_fwd")

    dy3p, dln3g, dln3b, loss = _loss_ln_bwd(xhat3, rstd3, w["ln3_g"], w["ln3_b"], target, name="loss_ln3_bwd")
    dx2, df2g, df2u, df2d = _ffn_bwd(dy3p, xb2, g2a, u2a, w["f2g"], w["f2u"], w["f2d"], name="ffn2_bwd")
    dy2p, dln2g, dln2b = _ln_bwd(dx2, xhat2, rstd2, w["ln2_g"], name="ln2_bwd")
    dfox, dlru, dwo = _mix_out_bwd(dy2p, fox, lru, w["wo"], name="mix_out_bwd")
    dlxg, dwab, dbab, dcw, dcb, dlam = _lru_bwd(dlru, lxg, uconv, hs, w["conv_w"], wab, bab, w["lam"], name="lru_bwd")
    drep = _fox_bwd_prep(dfox, fox, name="fox_bwd_prep")
    dq, drow, dk, dv, dcol = _fox_bwd(qkv, dfox, crep, ct, lse, drep, name="fox_bwd")
    dfg, dbf = _fox_bwd_post(drow, dcol, fgb, name="fox_bwd_post")
    dx1, dwp = _proj_in_bwd(dq, dk, dv, dlxg, dfg, xn1, dy2p, wp, name="proj_in_bwd")
    dy1p, dln1g, dln1b = _ln_bwd(dx1, xhat1, rstd1, w["ln1_g"], name="ln1_bwd")
    dx, df1g, df1u, df1d = _ffn_bwd(dy1p, xb0, g1a, u1a, w["f1g"], w["f1u"], w["f1d"], name="ffn1_bwd")

    grads = dict(
        f1g=df1g, f1u=df1u, f1d=df1d, f2g=df2g, f2u=df2u, f2d=df2d, wp=dwp, wo=dwo,
        ln1_g=dln1g, ln1_b=dln1b, ln2_g=dln2g, ln2_b=dln2b, ln3_g=dln3g, ln3_b=dln3b,
        b_forget=dbf[:, :HEADS], conv_w=dcw[:CONV_K], conv_b=dcb,
        rg_wa=_block_diag_extract(dwab[:, :LRU_W]), rg_wx=_block_diag_extract(dwab[:, LRU_W:]),
        rg_ba=dbab[:, :LRU_W].reshape(HEADS, HEAD_DIM), rg_bx=dbab[:, LRU_W:].reshape(HEADS, HEAD_DIM),
        lam=dlam,
    )
    return loss, dx, grads


MESH = pl.DeviceIdType.MESH
HBM_SPEC = pl.BlockSpec(memory_space=pl.ANY)
VMEM_SPEC = pl.BlockSpec(memory_space=pltpu.VMEM)


def _position():
    return lax.axis_index("x"), lax.axis_index("y"), lax.axis_index("c")


def _other_chips(x, y):
    return [(1 - x, y), (x, 1 - y), (1 - x, 1 - y)]


def _all_gather_bf16(shards, *, name):
    n = len(shards)

    def body(*refs):
        ins, outs, stages = refs[:n], refs[n:2 * n], refs[2 * n:3 * n]
        send_sems, recv_sems, local_sems = refs[3 * n:]
        x, y, c = _position()
        me, sibling = (x, y, c), (x, y, 1 - c)
        chips = _other_chips(x, y)

        def rows(k, px, py, pc):
            r = shards[k].shape[0]
            m = r // 2
            return outs[k].at[pl.ds(pl.multiple_of((2 * px + py) * r + pc * m, 16), m), :]

        def copy(k, idx, block, to, src=None):
            return pltpu.make_async_remote_copy(
                src_ref=rows(k, *block) if src is None else src, dst_ref=rows(k, *block),
                send_sem=send_sems.at[7 * k + idx], recv_sem=recv_sems.at[7 * k + idx],
                device_id=to, device_id_type=MESH)

        started = []
        mine = []
        for k in range(n):
            m = shards[k].shape[0] // 2
            stages[k][...] = ins[k][pl.ds(pl.multiple_of(c * m, 16), m), :].astype(stages[k].dtype)
            cp = pltpu.make_async_copy(stages[k], rows(k, *me), local_sems.at[k])
            cp.start()
            mine.append(cp)
            first = [copy(k, 0, me, sibling, src=stages[k])]
            first += [copy(k, 1 + j, me, (*chip, c), src=stages[k]) for j, chip in enumerate(chips)]
            for cp in first:
                cp.start()
            started += first
        for k in range(n):
            for j, chip in enumerate(chips):
                copy(k, 1 + j, (*chip, c), me).wait_recv()
                fwd = copy(k, 4 + j, (*chip, c), sibling)
                fwd.start()
                started.append(fwd)
        for k in range(n):
            copy(k, 0, sibling, me).wait_recv()
            for j, chip in enumerate(chips):
                copy(k, 4 + j, (*chip, 1 - c), me).wait_recv()
        for cp in started:
            cp.wait_send()
        for cp in mine:
            cp.wait()

    return _pcall(
        body, name=name,
        in_specs=[VMEM_SPEC] * n, out_specs=[HBM_SPEC] * n,
        out_shape=[jax.ShapeDtypeStruct((N_SHARD * s.shape[0], s.shape[1]), MXU_DTYPE) for s in shards],
        scratch_shapes=[pltpu.VMEM((s.shape[0] // 2, s.shape[1]), MXU_DTYPE) for s in shards]
        + [pltpu.SemaphoreType.DMA((7 * n,)), pltpu.SemaphoreType.DMA((7 * n,)), pltpu.SemaphoreType.DMA((n,))],
        compiler_params=pltpu.CompilerParams(vmem_limit_bytes=VMEM_LIMIT),
    )(*shards)


def _swap_halves(gs, *, name):
    n = len(gs)

    def body(*refs):
        ins, outs = refs[:n], refs[n:2 * n]
        send_sems, recv_sems = refs[2 * n:]
        x, y, c = _position()
        cps = []
        for k in range(n):
            m = gs[k].shape[1] // 2
            src = ins[k].at[:, pl.ds(pl.multiple_of((1 - c) * m, 16), m), :]
            cp = pltpu.make_async_remote_copy(src_ref=src, dst_ref=outs[k], send_sem=send_sems.at[k],
                                              recv_sem=recv_sems.at[k], device_id=(x, y, 1 - c), device_id_type=MESH)
            cp.start()
            cps.append(cp)
        for cp in cps:
            cp.wait()

    return _pcall(
        body, name=name, in_specs=[HBM_SPEC] * n, out_specs=[HBM_SPEC] * n,
        out_shape=[jax.ShapeDtypeStruct((g.shape[0], g.shape[1] // 2, g.shape[2]), g.dtype) for g in gs],
        scratch_shapes=[pltpu.SemaphoreType.DMA((n,)), pltpu.SemaphoreType.DMA((n,))],
    )(*gs)


def _add_halves(gs, recvs, *, name, tm=256):
    n = len(gs)
    _, r, cdim = gs[0].shape
    m = r // 2
    tm = min(tm, m)
    nb = m // tm
    c_idx = lax.axis_index("c").astype(jnp.int32).reshape(1)

    def body(c_ref, *refs):
        for k in range(n):
            refs[2 * n + k][...] = (refs[k][...].astype(f32) + refs[n + k][...].astype(f32)).astype(refs[2 * n + k].dtype)

    mine = pl.BlockSpec((None, tm, cdim), lambda j, i, c_ref: (j, c_ref[0] * nb + i, 0))
    half = pl.BlockSpec((None, tm, cdim), lambda j, i, c_ref: (j, i, 0))
    return _pcall(
        body, name=name,
        grid_spec=pltpu.PrefetchScalarGridSpec(
            num_scalar_prefetch=1, grid=(N_SHARD, nb),
            in_specs=[mine] * n + [half] * n, out_specs=[half] * n),
        out_shape=[jax.ShapeDtypeStruct((N_SHARD, m, cdim), g.dtype) for g in gs],
        compiler_params=_params(2),
    )(c_idx, *gs, *recvs)


def _scatter_partials(ps, *, name):
    n = len(ps)

    def body(*refs):
        ins, outs = refs[:n], refs[n:2 * n]
        send_sems, recv_sems = refs[2 * n:]
        x, y, c = _position()
        me_chip = 2 * x + y
        cps = []
        for k in range(n):
            for j, (px, py) in enumerate(_other_chips(x, y)):
                cp = pltpu.make_async_remote_copy(
                    src_ref=ins[k].at[2 * px + py], dst_ref=outs[k].at[me_chip],
                    send_sem=send_sems.at[3 * k + j], recv_sem=recv_sems.at[3 * k + j],
                    device_id=(px, py, c), device_id_type=MESH)
                cp.start()
                cps.append(cp)
        for cp in cps:
            cp.wait()

    return _pcall(
        body, name=name, in_specs=[HBM_SPEC] * n, out_specs=[HBM_SPEC] * n,
        out_shape=[jax.ShapeDtypeStruct(p.shape, p.dtype) for p in ps],
        scratch_shapes=[pltpu.SemaphoreType.DMA((3 * n,)), pltpu.SemaphoreType.DMA((3 * n,))],
    )(*ps)


def _sum_slabs(ps, qs, *, name, tm=128):
    n = len(qs)
    _, m, cdim = qs[0].shape
    tm = min(tm, m)
    nb = m // tm
    assert m % tm == 0, (m, tm)
    where = jnp.stack([2 * lax.axis_index("x") + lax.axis_index("y"), lax.axis_index("c")]).astype(jnp.int32)

    def body(w_ref, *refs):
        for k in range(n):
            own, q1, q2, q3 = (refs[4 * k + t][...].astype(f32) for t in range(4))
            refs[4 * n + k][...] = ((own + q1) + q2) + q3

    def slab(flip):
        return pl.BlockSpec((None, tm, cdim), lambda i, w_ref: (jnp.bitwise_xor(w_ref[0], flip), i, 0))

    operands = []
    for p, q in zip(ps, qs):
        operands += [p, q, q, q]
    return _pcall(
        body, name=name,
        grid_spec=pltpu.PrefetchScalarGridSpec(
            num_scalar_prefetch=1, grid=(nb,),
            in_specs=[slab(0), slab(2), slab(1), slab(3)] * n,
            out_specs=[pl.BlockSpec((tm, cdim), lambda i, w_ref: (w_ref[1] * nb + i, 0))] * n),
        out_shape=[jax.ShapeDtypeStruct((2 * m, cdim), f32) for _ in qs],
        compiler_params=_params(1),
    )(where, *operands)


def _join_halves(fs, *, name):
    n = len(fs)

    def body(*refs):
        outs = refs[n:2 * n]
        send_sems, recv_sems = refs[2 * n:]
        x, y, c = _position()
        cps = []
        for k in range(n):
            m = fs[k].shape[0] // 2
            half = outs[k].at[pl.ds(pl.multiple_of(c * m, 8), m), :]
            cp = pltpu.make_async_remote_copy(src_ref=half, dst_ref=half, send_sem=send_sems.at[k],
                                              recv_sem=recv_sems.at[k], device_id=(x, y, 1 - c), device_id_type=MESH)
            cp.start()
            cps.append(cp)
        for cp in cps:
            cp.wait()

    return _pcall(
        body, name=name, in_specs=[HBM_SPEC] * n, out_specs=[HBM_SPEC] * n,
        out_shape=[jax.ShapeDtypeStruct(f.shape, f.dtype) for f in fs],
        input_output_aliases={k: k for k in range(n)},
        scratch_shapes=[pltpu.SemaphoreType.DMA((n,)), pltpu.SemaphoreType.DMA((n,))],
    )(*fs)


def _all_reduce_small(v, *, name):
    r = v.shape[0]

    def body(v_ref, out_ref, buf, send_sems, recv_sems, local_sem):
        x, y, c = _position()
        me, sibling = (x, y, c), (x, y, 1 - c)
        chips = _other_chips(x, y)

        def rows(px, py, pc):
            return buf.at[pl.ds(pl.multiple_of((4 * px + 2 * py + pc) * r, 8), r), :]

        def copy(k, block, to, src=None):
            return pltpu.make_async_remote_copy(
                src_ref=rows(*block) if src is None else src, dst_ref=rows(*block),
                send_sem=send_sems.at[k], recv_sem=recv_sems.at[k], device_id=to, device_id_type=MESH)

        mine = pltpu.make_async_copy(v_ref, rows(*me), local_sem)
        mine.start()
        first = [copy(0, me, sibling, src=v_ref)]
        first += [copy(1 + j, me, (*chip, c), src=v_ref) for j, chip in enumerate(chips)]
        for cp in first:
            cp.start()
        passed = [copy(4 + j, (*chip, c), sibling) for j, chip in enumerate(chips)]
        for j, chip in enumerate(chips):
            copy(1 + j, (*chip, c), me).wait_recv()
            passed[j].start()
        copy(0, sibling, me).wait_recv()
        for j, chip in enumerate(chips):
            copy(4 + j, (*chip, 1 - c), me).wait_recv()
        for cp in first + passed:
            cp.wait_send()
        mine.wait()
        acc = buf[0:r, :]
        for d in range(1, N_DEV):
            acc = acc + buf[d * r:(d + 1) * r, :]
        out_ref[...] = acc

    return _pcall(
        body, name=name, in_specs=[VMEM_SPEC], out_specs=VMEM_SPEC,
        out_shape=jax.ShapeDtypeStruct((r, LANES), f32),
        scratch_shapes=[pltpu.VMEM((N_DEV * r, LANES), f32), pltpu.SemaphoreType.DMA((7,)),
                        pltpu.SemaphoreType.DMA((7,)), pltpu.SemaphoreType.DMA],
    )(v)


def _reduce_to_owner(gs, tag):
    recvs = _swap_halves(gs, name=f"rs_swap_{tag}")
    ps = _add_halves(gs, recvs, name=f"rs_add_{tag}")
    qs = _scatter_partials(ps, name=f"rs_scatter_{tag}")
    fs = _sum_slabs(ps, qs, name=f"rs_sum_{tag}")
    return _join_halves(fs, name=f"rs_join_{tag}")


def _adamw(gs, ws, ms, vs, *, name, tm=256):
    n = len(gs)
    r, cdim = gs[0].shape
    tm = r if tm is None else min(tm, r)
    assert r % tm == 0, (r, tm)
    c1 = 1.0 / (1.0 - ADAM_B1 ** ADAM_STEP)
    c2 = 1.0 / (1.0 - ADAM_B2 ** ADAM_STEP)

    def body(*refs):
        for k in range(n):
            g = refs[k][...]
            w = refs[n + k][...]
            m = ADAM_B1 * refs[2 * n + k][...] + (1.0 - ADAM_B1) * g
            v = ADAM_B2 * refs[3 * n + k][...] + (1.0 - ADAM_B2) * (g * g)
            refs[4 * n + k][...] = -ADAM_LR * ((m * c1) / (jnp.sqrt(v * c2) + ADAM_EPS) + ADAM_WD * w)
            refs[5 * n + k][...] = m
            refs[6 * n + k][...] = v

    spec = pl.BlockSpec((tm, cdim), lambda i: (i, 0))
    outs = _pcall(
        body, name=name, grid=(r // tm,), in_specs=[spec] * (4 * n), out_specs=[spec] * (3 * n),
        out_shape=[jax.ShapeDtypeStruct((r, cdim), f32)] * (3 * n),
        compiler_params=_params(1),
    )(*gs, *ws, *ms, *vs)
    return outs[:n], outs[n:2 * n], outs[2 * n:]


BIG = ["ffn1_w_gate", "ffn1_w_up", "ffn1_w_down", "ffn2_w_gate", "ffn2_w_up", "ffn2_w_down"]
SMALL = ["ln1_g", "ln1_b", "b_forget", "conv_w", "conv_b", "rg_wa", "rg_ba", "rg_wx", "rg_bx", "lru_lambda",
         "ln2_g", "ln2_b", "ln3_g", "ln3_b"]
WEIGHTS = ["ffn1_w_gate", "ffn1_w_up", "ffn1_w_down", "ln1_g", "ln1_b", "w_in", "b_forget", "conv_w", "conv_b",
           "rg_wa", "rg_ba", "rg_wx", "rg_bx", "lru_lambda", "w_out", "ln2_g", "ln2_b",
           "ffn2_w_gate", "ffn2_w_up", "ffn2_w_down", "ln3_g", "ln3_b"]


def _pack_small(parts):
    rows = []
    for n in SMALL:
        flat = parts[n].reshape(-1)
        pad = (-flat.shape[0]) % LANES
        rows.append(jnp.pad(flat, (0, pad)).reshape(-1, LANES))
    packed = jnp.concatenate(rows, axis=0)
    return jnp.pad(packed, ((0, (-packed.shape[0]) % 8), (0, 0)))


def _unpack_small(packed, shapes):
    out, r0 = {}, 0
    for n in SMALL:
        size = math.prod(shapes[n])
        nr = -(-size // LANES)
        out[n] = packed[r0:r0 + nr].reshape(-1)[:size].reshape(shapes[n])
        r0 += nr
    return out


def kernel(x, ffn1_w_gate, ffn1_w_up, ffn1_w_down, ln1_g, ln1_b, w_in, b_forget, conv_w, conv_b, rg_wa, rg_ba, rg_wx, rg_bx, lru_lambda, w_out, ln2_g, ln2_b, ffn2_w_gate, ffn2_w_up, ffn2_w_down, ln3_g, ln3_b, loss_target, m_ffn1_w_gate, m_ffn1_w_up, m_ffn1_w_down, m_ln1_g, m_ln1_b, m_w_in, m_b_forget, m_conv_w, m_conv_b, m_rg_wa, m_rg_ba, m_rg_wx, m_rg_bx, m_lru_lambda, m_w_out, m_ln2_g, m_ln2_b, m_ffn2_w_gate, m_ffn2_w_up, m_ffn2_w_down, m_ln3_g, m_ln3_b, v_ffn1_w_gate, v_ffn1_w_up, v_ffn1_w_down, v_ln1_g, v_ln1_b, v_w_in, v_b_forget, v_conv_w, v_conv_b, v_rg_wa, v_rg_ba, v_rg_wx, v_rg_bx, v_lru_lambda, v_w_out, v_ln2_g, v_ln2_b, v_ffn2_w_gate, v_ffn2_w_up, v_ffn2_w_down, v_ln3_g, v_ln3_b):
    args = dict(locals())
    w = {n: args[n] for n in WEIGHTS}
    mom = {n: args["m_" + n] for n in WEIGHTS}
    var = {n: args["v_" + n] for n in WEIGHTS}
    chip = 2 * lax.axis_index("x") + lax.axis_index("y")

    g1 = _all_gather_bf16([w[n][0] for n in BIG[:3]], name="ag_ffn1")
    g2 = _all_gather_bf16([w[n][0] for n in BIG[3:]] + [w["w_in"][0], w["w_out"][0]], name="ag_rest")
    fs = D_FF // N_SHARD
    w_in_full = g2[3].reshape(N_SHARD, D_MODEL, IN_SHARD).transpose(1, 0, 2).reshape(D_MODEL, IN_COLS)
    full = dict(
        f1g=g1[0].reshape(N_SHARD, D_MODEL, fs), f1u=g1[1].reshape(N_SHARD, D_MODEL, fs),
        f1d=g1[2].reshape(N_SHARD, fs, D_MODEL),
        f2g=g2[0].reshape(N_SHARD, D_MODEL, fs), f2u=g2[1].reshape(N_SHARD, D_MODEL, fs),
        f2d=g2[2].reshape(N_SHARD, fs, D_MODEL),
        wp=make_wp(w_in_full), bfp=jnp.pad(b_forget, ((0, 0), (0, LANES - HEADS))), wo=g2[4],
        ln1_g=ln1_g, ln1_b=ln1_b, ln2_g=ln2_g, ln2_b=ln2_b, ln3_g=ln3_g, ln3_b=ln3_b,
        conv_b=conv_b, rg_wa=rg_wa[0], rg_wx=rg_wx[0], rg_ba=rg_ba[0], rg_bx=rg_bx[0], lam=lru_lambda,
    )
    cw_place = lax.dynamic_update_slice(jnp.zeros((8, LRU_W), f32), conv_w[0] * 0.5, (0, chip * (LRU_W // N_SHARD)))
    full["conv_w"] = _all_reduce_small(cw_place.reshape(-1, LANES), name="ag_conv_w").reshape(8, LRU_W)[:CONV_K]

    loss_rep, dx, g = _local_step(x[0], loss_target[0], full)
    loss = lax.psum(loss_rep[0, 0], ("x", "y", "c"))

    red_ffn = _reduce_to_owner([g["f1g"], g["f1u"], g["f1d"], g["f2g"], g["f2u"], g["f2d"]], "ffn")
    gwin = g["wp"][:, :IN_COLS].reshape(D_MODEL, N_SHARD, IN_SHARD).transpose(1, 0, 2).astype(GRAD_DTYPE)
    red_win = _reduce_to_owner([gwin], "w_in")
    red_wo = _reduce_to_owner([g["wo"].reshape(N_SHARD, D_MODEL // N_SHARD, D_MODEL).astype(GRAD_DTYPE)], "w_out")
    grads = dict(zip(BIG, red_ffn))
    grads["w_in"] = red_win[0]
    grads["w_out"] = red_wo[0]

    small_shapes = {n: w[n].shape for n in SMALL}
    small_shapes["conv_w"] = (1, CONV_K, LRU_W)
    gsmall = dict(ln1_g=g["ln1_g"], ln1_b=g["ln1_b"], ln2_g=g["ln2_g"], ln2_b=g["ln2_b"], ln3_g=g["ln3_g"],
                  ln3_b=g["ln3_b"], b_forget=g["b_forget"], conv_w=g["conv_w"], conv_b=g["conv_b"],
                  rg_wa=g["rg_wa"], rg_wx=g["rg_wx"], rg_ba=g["rg_ba"], rg_bx=g["rg_bx"], lru_lambda=g["lam"])
    gs_red = _unpack_small(_all_reduce_small(_pack_small(gsmall), name="ar_small"), small_shapes)
    gs_red["conv_w"] = lax.dynamic_slice(gs_red["conv_w"], (0, 0, chip * (LRU_W // N_SHARD)),
                                         (1, CONV_K, LRU_W // N_SHARD))
    grads.update(gs_red)

    delta, new_m, new_v = {}, {}, {}
    d, nm, nv = _adamw([grads[n] for n in BIG], [w[n][0] for n in BIG], [mom[n][0] for n in BIG],
                       [var[n][0] for n in BIG], name="adamw_ffn", tm=64)
    for i, n in enumerate(BIG):
        delta[n], new_m[n], new_v[n] = d[i], nm[i], nv[i]
    for n in ("w_in", "w_out"):
        d, nm, nv = _adamw([grads[n]], [w[n][0]], [mom[n][0]], [var[n][0]], name="adamw_" + n)
        delta[n], new_m[n], new_v[n] = d[0], nm[0], nv[0]
    shard_shapes = {n: w[n].shape for n in SMALL}
    d, nm, nv = _adamw([_pack_small({n: grads[n] for n in SMALL})], [_pack_small({n: w[n] for n in SMALL})],
                       [_pack_small({n: mom[n] for n in SMALL})], [_pack_small({n: var[n] for n in SMALL})],
                       name="adamw_small", tm=None)
    for dst, packed in ((delta, d[0]), (new_m, nm[0]), (new_v, nv[0])):
        dst.update(_unpack_small(packed, shard_shapes))

    def shaped(tree, n):
        return tree[n].reshape(w[n].shape)

    return (loss, dx[None], *[shaped(grads, n) for n in WEIGHTS], *[shaped(delta, n) for n in WEIGHTS],
            *[shaped(new_m, n) for n in WEIGHTS], *[shaped(new_v, n) for n in WEIGHTS])
```

```python
import functools
import math

import jax
import jax.numpy as jnp
from jax import lax
from jax.experimental import pallas as pl
from jax.experimental.pallas import tpu as pltpu

f32 = jnp.float32
MXU_DTYPE = jnp.bfloat16
GRAD_DTYPE = jnp.bfloat16

D_MODEL = 1024
D_FF = 4096
N_SHARD = 4
N_DEV = 8
FOX_W = 512
LRU_W = 512
HEADS = 8
HEAD_DIM = 64
CONV_K = 4
IN_COLS = 2568
IN_SHARD = IN_COLS // N_SHARD
QKV_W = 3 * FOX_W
Z_PAD = 2688
LANES = 128
LN_EPS = 1e-5
DN_ALPHA = 2.0 ** 0.25
LRU_C = 8.0
NEG_BIG = -1e30
VMEM_LIMIT = 56 * 1024 * 1024

ADAM_LR = 0.001
ADAM_B1 = 0.9
ADAM_B2 = 0.999
ADAM_EPS = 1e-08
ADAM_WD = 0.01
ADAM_STEP = 10


def _pcall(body, **kw):
    return pl.pallas_call(body, **kw)


def _params(n_grid, vmem=VMEM_LIMIT):
    return pltpu.CompilerParams(dimension_semantics=("arbitrary",) * n_grid, vmem_limit_bytes=vmem)


def _dot(a, b):
    return jnp.dot(a, b, preferred_element_type=f32)


def _dot_nt(a, b):
    return lax.dot_general(a, b, (((1,), (1,)), ((), ())), preferred_element_type=f32)


def _dot_tn(a, b):
    return lax.dot_general(a, b, (((0,), (0,)), ((), ())), preferred_element_type=f32)


def _sigmoid(x):
    return 1.0 / (1.0 + jnp.exp(-x))


def _layer_norm_stats(y):
    mu = jnp.mean(y, axis=-1, keepdims=True)
    yc = y - mu
    var = jnp.mean(yc * yc, axis=-1, keepdims=True)
    rstd = lax.rsqrt(var + LN_EPS)
    return yc * rstd, rstd


def _ln_backward(dy, xhat, rstd, gamma):
    dxhat = dy * gamma
    m1 = jnp.mean(dxhat, axis=-1, keepdims=True)
    m2 = jnp.mean(dxhat * xhat, axis=-1, keepdims=True)
    dyp = rstd * (dxhat - m1 - xhat * m2)
    return dyp, jnp.sum(dy * xhat, axis=0, keepdims=True), jnp.sum(dy, axis=0, keepdims=True)


def _ffn_fwd(x, wg, wu, wd, ln_g, ln_b, *, name, tm=1024, tf=512):
    T = x.shape[0]
    tm = min(tm, T)
    fs = D_FF // N_SHARD
    cpf = fs // tf
    nf = D_FF // tf
    nt = T // tm

    def body(x_ref, wg_ref, wu_ref, wd_ref, g_ref, b_ref,
             xb_ref, gact_ref, uact_ref, xhat_ref, xn_ref, rstd_ref, acc_ref):
        f = pl.program_id(1)

        @pl.when(f == 0)
        def _():
            xb_ref[...] = x_ref[...].astype(MXU_DTYPE)
            acc_ref[...] = jnp.zeros_like(acc_ref)

        xb = xb_ref[...]
        g = _dot(xb, wg_ref[...])
        u = _dot(xb, wu_ref[...])
        h = (g * _sigmoid(g)) * u
        gact_ref[...] = g.astype(gact_ref.dtype)
        uact_ref[...] = u.astype(uact_ref.dtype)
        acc_ref[...] += _dot(h.astype(MXU_DTYPE), wd_ref[...])

        @pl.when(f == nf - 1)
        def _():
            y = DN_ALPHA * x_ref[...] + 0.5 * acc_ref[...]
            xhat, rstd = _layer_norm_stats(y)
            xhat_ref[...] = xhat
            xn_ref[...] = (xhat * g_ref[...] + b_ref[...]).astype(xn_ref.dtype)
            rstd_ref[...] = jnp.broadcast_to(rstd, rstd_ref.shape)

    row = lambda i, f: (i, 0)
    return _pcall(
        body, name=name, grid=(nt, nf),
        in_specs=[
            pl.BlockSpec((tm, D_MODEL), row),
            pl.BlockSpec((None, D_MODEL, tf), lambda i, f: (f // cpf, 0, f % cpf)),
            pl.BlockSpec((None, D_MODEL, tf), lambda i, f: (f // cpf, 0, f % cpf)),
            pl.BlockSpec((None, tf, D_MODEL), lambda i, f: (f // cpf, f % cpf, 0)),
            pl.BlockSpec((1, D_MODEL), lambda i, f: (0, 0)),
            pl.BlockSpec((1, D_MODEL), lambda i, f: (0, 0)),
        ],
        out_specs=[
            pl.BlockSpec((tm, D_MODEL), row),
            pl.BlockSpec((tm, tf), lambda i, f: (i, f)),
            pl.BlockSpec((tm, tf), lambda i, f: (i, f)),
            pl.BlockSpec((tm, D_MODEL), row),
            pl.BlockSpec((tm, D_MODEL), row),
            pl.BlockSpec((tm, LANES), row),
        ],
        out_shape=[
            jax.ShapeDtypeStruct((T, D_MODEL), MXU_DTYPE),
            jax.ShapeDtypeStruct((T, D_FF), MXU_DTYPE),
            jax.ShapeDtypeStruct((T, D_FF), MXU_DTYPE),
            jax.ShapeDtypeStruct((T, D_MODEL), f32),
            jax.ShapeDtypeStruct((T, D_MODEL), MXU_DTYPE),
            jax.ShapeDtypeStruct((T, LANES), f32),
        ],
        scratch_shapes=[pltpu.VMEM((tm, D_MODEL), f32)],
        compiler_params=_params(2),
    )(x, wg, wu, wd, ln_g, ln_b)


def _ffn_bwd(dyp, xb, gact, uact, wg, wu, wd, *, name, tm=512, tf=512):
    T = dyp.shape[0]
    tm = min(tm, T)
    fs = D_FF // N_SHARD
    cpf = fs // tf
    nf = D_FF // tf
    nt = T // tm

    def body(dyp_ref, xb_ref, g_ref, u_ref, wg_ref, wu_ref, wd_ref,
             dx_hbm, dwg_ref, dwu_ref, dwd_ref, dx_sc, dwg_sc, dwu_sc, dwd_sc, sem):
        f = pl.program_id(0)
        i = pl.program_id(1)
        rows = pl.ds(pl.multiple_of(i * tm, tm), tm)
        dyp_t = dyp_ref[...]
        dy = (0.5 * dyp_t).astype(MXU_DTYPE)

        @pl.when(i == 0)
        def _():
            dwg_sc[...] = jnp.zeros_like(dwg_sc)
            dwu_sc[...] = jnp.zeros_like(dwu_sc)
            dwd_sc[...] = jnp.zeros_like(dwd_sc)

        @pl.when(f == 0)
        def _():
            dx_sc[rows, :] = DN_ALPHA * dyp_t

        g = g_ref[...].astype(f32)
        u = u_ref[...].astype(f32)
        sig = _sigmoid(g)
        silu = g * sig
        dh = _dot_nt(dy, wd_ref[...])
        dg = (dh * u * (sig * (1.0 + g * (1.0 - sig)))).astype(MXU_DTYPE)
        du = (dh * silu).astype(MXU_DTYPE)
        hb = (silu * u).astype(MXU_DTYPE)
        dx_sc[rows, :] += _dot_nt(dg, wg_ref[...]) + _dot_nt(du, wu_ref[...])
        xb_t = xb_ref[...]
        dwg_sc[...] += _dot_tn(xb_t, dg)
        dwu_sc[...] += _dot_tn(xb_t, du)
        dwd_sc[...] += _dot_tn(hb, dy)

        @pl.when(i == nt - 1)
        def _():
            dwg_ref[...] = dwg_sc[...].astype(dwg_ref.dtype)
            dwu_ref[...] = dwu_sc[...].astype(dwu_ref.dtype)
            dwd_ref[...] = dwd_sc[...].astype(dwd_ref.dtype)

        @pl.when(jnp.logical_and(f == nf - 1, i == nt - 1))
        def _():
            cp = pltpu.make_async_copy(dx_sc, dx_hbm, sem)
            cp.start()
            cp.wait()

    row = lambda f, i: (i, 0)
    return _pcall(
        body, name=name, grid=(nf, nt),
        in_specs=[
            pl.BlockSpec((tm, D_MODEL), row),
            pl.BlockSpec((tm, D_MODEL), row),
            pl.BlockSpec((tm, tf), lambda f, i: (i, f)),
            pl.BlockSpec((tm, tf), lambda f, i: (i, f)),
            pl.BlockSpec((None, D_MODEL, tf), lambda f, i: (f // cpf, 0, f % cpf)),
            pl.BlockSpec((None, D_MODEL, tf), lambda f, i: (f // cpf, 0, f % cpf)),
            pl.BlockSpec((None, tf, D_MODEL), lambda f, i: (f // cpf, f % cpf, 0)),
        ],
        out_specs=[
            pl.BlockSpec(memory_space=pl.ANY),
            pl.BlockSpec((None, D_MODEL, tf), lambda f, i: (f // cpf, 0, f % cpf)),
            pl.BlockSpec((None, D_MODEL, tf), lambda f, i: (f // cpf, 0, f % cpf)),
            pl.BlockSpec((None, tf, D_MODEL), lambda f, i: (f // cpf, f % cpf, 0)),
        ],
        out_shape=[
            jax.ShapeDtypeStruct((T, D_MODEL), f32),
            jax.ShapeDtypeStruct((N_SHARD, D_MODEL, fs), GRAD_DTYPE),
            jax.ShapeDtypeStruct((N_SHARD, D_MODEL, fs), GRAD_DTYPE),
            jax.ShapeDtypeStruct((N_SHARD, fs, D_MODEL), GRAD_DTYPE),
        ],
        scratch_shapes=[pltpu.VMEM((T, D_MODEL), f32), pltpu.VMEM((D_MODEL, tf), f32),
                        pltpu.VMEM((D_MODEL, tf), f32), pltpu.VMEM((tf, D_MODEL), f32),
                        pltpu.SemaphoreType.DMA],
        compiler_params=_params(2),
    )(dyp, xb, gact, uact, wg, wu, wd)


def _loss_ln_bwd(xhat, rstd, ln_g, ln_b, target, *, name, tm=512):
    T = xhat.shape[0]
    tm = min(tm, T)
    nt = T // tm

    def body(xhat_ref, rstd_ref, g_ref, b_ref, t_ref, dyp_ref, dg_ref, db_ref, loss_ref):
        i = pl.program_id(0)

        @pl.when(i == 0)
        def _():
            dg_ref[...] = jnp.zeros_like(dg_ref)
            db_ref[...] = jnp.zeros_like(db_ref)
            loss_ref[...] = jnp.zeros_like(loss_ref)

        xhat_t = xhat_ref[...]
        gamma = g_ref[...]
        err = xhat_t * gamma + b_ref[...] - t_ref[...]
        sq = jnp.sum(jnp.sum(err * err, axis=0, keepdims=True), axis=1, keepdims=True)
        loss_ref[...] += jnp.broadcast_to(sq * (0.5 / D_MODEL), loss_ref.shape)
        dy = err * (1.0 / D_MODEL)
        dyp, dgam, dbeta = _ln_backward(dy, xhat_t, rstd_ref[:, 0:1], gamma)
        dyp_ref[...] = dyp
        dg_ref[...] += dgam
        db_ref[...] += dbeta

    row = lambda i: (i, 0)
    const = lambda i: (0, 0)
    return _pcall(
        body, name=name, grid=(nt,),
        in_specs=[pl.BlockSpec((tm, D_MODEL), row), pl.BlockSpec((tm, LANES), row),
                  pl.BlockSpec((1, D_MODEL), const), pl.BlockSpec((1, D_MODEL), const),
                  pl.BlockSpec((tm, D_MODEL), row)],
        out_specs=[pl.BlockSpec((tm, D_MODEL), row), pl.BlockSpec((1, D_MODEL), const),
                   pl.BlockSpec((1, D_MODEL), const), pl.BlockSpec((1, LANES), const)],
        out_shape=[jax.ShapeDtypeStruct((T, D_MODEL), f32), jax.ShapeDtypeStruct((1, D_MODEL), f32),
                   jax.ShapeDtypeStruct((1, D_MODEL), f32), jax.ShapeDtypeStruct((1, LANES), f32)],
        compiler_params=_params(1),
    )(xhat, rstd, ln_g, ln_b, target)


def _ln_bwd(dy, xhat, rstd, ln_g, *, name, tm=512):
    T = xhat.shape[0]
    tm = min(tm, T)
    nt = T // tm

    def body(dy_ref, xhat_ref, rstd_ref, g_ref, dyp_ref, dg_ref, db_ref):
        i = pl.program_id(0)

        @pl.when(i == 0)
        def _():
            dg_ref[...] = jnp.zeros_like(dg_ref)
            db_ref[...] = jnp.zeros_like(db_ref)

        dyp, dgam, dbeta = _ln_backward(dy_ref[...], xhat_ref[...], rstd_ref[:, 0:1], g_ref[...])
        dyp_ref[...] = dyp
        dg_ref[...] += dgam
        db_ref[...] += dbeta

    row = lambda i: (i, 0)
    const = lambda i: (0, 0)
    return _pcall(
        body, name=name, grid=(nt,),
        in_specs=[pl.BlockSpec((tm, D_MODEL), row), pl.BlockSpec((tm, D_MODEL), row),
                  pl.BlockSpec((tm, LANES), row), pl.BlockSpec((1, D_MODEL), const)],
        out_specs=[pl.BlockSpec((tm, D_MODEL), row), pl.BlockSpec((1, D_MODEL), const),
                   pl.BlockSpec((1, D_MODEL), const)],
        out_shape=[jax.ShapeDtypeStruct((T, D_MODEL), f32), jax.ShapeDtypeStruct((1, D_MODEL), f32),
                   jax.ShapeDtypeStruct((1, D_MODEL), f32)],
        compiler_params=_params(1),
    )(dy, xhat, rstd, ln_g)


def _proj_in(xn, wp, bfp, *, name, tm=512):
    T = xn.shape[0]
    tm = min(tm, T)
    nt = T // tm

    def body(x_ref, w_ref, b_ref, qkv_ref, lxg_ref, fg_ref):
        z = _dot(x_ref[...], w_ref[...])
        qkv_ref[...] = z[:, :QKV_W].astype(qkv_ref.dtype)
        lxg_ref[...] = z[:, QKV_W:QKV_W + 2 * LRU_W]
        fg_ref[...] = z[:, QKV_W + 2 * LRU_W:] + b_ref[...]

    row = lambda i: (i, 0)
    const = lambda i: (0, 0)
    return _pcall(
        body, name=name, grid=(nt,),
        in_specs=[pl.BlockSpec((tm, D_MODEL), row), pl.BlockSpec((D_MODEL, Z_PAD), const),
                  pl.BlockSpec((1, LANES), const)],
        out_specs=[pl.BlockSpec((tm, QKV_W), row), pl.BlockSpec((tm, 2 * LRU_W), row),
                   pl.BlockSpec((tm, LANES), row)],
        out_shape=[jax.ShapeDtypeStruct((T, QKV_W), MXU_DTYPE), jax.ShapeDtypeStruct((T, 2 * LRU_W), f32),
                   jax.ShapeDtypeStruct((T, LANES), f32)],
        compiler_params=_params(1),
    )(xn, wp, bfp)


def _proj_in_bwd(dq, dk, dv, dlxg, dfg, xn, dyp, wp, *, name, tm=512):
    T = xn.shape[0]
    tm = min(tm, T)
    nt = T // tm

    def body(dq_ref, dk_ref, dv_ref, dl_ref, dfg_ref, x_ref, dyp_ref, w_ref, dx_ref, dw_hbm, dw_sc, sem):
        i = pl.program_id(0)

        @pl.when(i == 0)
        def _():
            dw_sc[...] = jnp.zeros_like(dw_sc)

        dz = jnp.concatenate(
            [dq_ref[...].astype(MXU_DTYPE), dk_ref[...].astype(MXU_DTYPE), dv_ref[...].astype(MXU_DTYPE),
             dl_ref[...].astype(MXU_DTYPE), dfg_ref[...].astype(MXU_DTYPE)], axis=1)
        dx_ref[...] = DN_ALPHA * dyp_ref[...] + _dot_nt(dz, w_ref[...])
        dw_sc[...] += _dot_tn(x_ref[...], dz)

        @pl.when(i == nt - 1)
        def _():
            dw_sc[:, :FOX_W] = dw_sc[:, :FOX_W] * (1.0 / math.sqrt(HEAD_DIM))
            cp = pltpu.make_async_copy(dw_sc, dw_hbm, sem)
            cp.start()
            cp.wait()

    row = lambda i: (i, 0)
    const = lambda i: (0, 0)
    return _pcall(
        body, name=name, grid=(nt,),
        in_specs=[pl.BlockSpec((tm, FOX_W), row), pl.BlockSpec((tm, FOX_W), row), pl.BlockSpec((tm, FOX_W), row),
                  pl.BlockSpec((tm, 2 * LRU_W), row), pl.BlockSpec((tm, LANES), row),
                  pl.BlockSpec((tm, D_MODEL), row), pl.BlockSpec((tm, D_MODEL), row),
                  pl.BlockSpec((D_MODEL, Z_PAD), const)],
        out_specs=[pl.BlockSpec((tm, D_MODEL), row), pl.BlockSpec(memory_space=pl.ANY)],
        out_shape=[jax.ShapeDtypeStruct((T, D_MODEL), f32), jax.ShapeDtypeStruct((D_MODEL, Z_PAD), f32)],
        scratch_shapes=[pltpu.VMEM((D_MODEL, Z_PAD), f32), pltpu.SemaphoreType.DMA],
        compiler_params=_params(1),
    )(dq, dk, dv, dlxg, dfg, xn, dyp, wp)


def _split3(x):
    hi = x.astype(jnp.bfloat16)
    r1 = x - hi.astype(f32)
    mid = r1.astype(jnp.bfloat16)
    lo = (r1 - mid.astype(f32)).astype(jnp.bfloat16)
    return hi, mid, lo


def _tri_dot(tri, x):
    hi, mid, lo = _split3(x)
    return _dot(tri, hi) + _dot(tri, mid) + _dot(tri, lo)


def _fox_prep(fgb, *, name, tm=512):
    T = fgb.shape[0]
    tm = min(tm, T)
    nt = T // tm

    def body(fg_ref, crep_ref, ct_ref, carry):
        i = pl.program_id(0)

        @pl.when(i == 0)
        def _():
            carry[...] = jnp.zeros_like(carry)

        x = fg_ref[...]
        ls = jnp.minimum(x, 0.0) - jnp.log(1.0 + jnp.exp(-jnp.abs(x)))
        r = lax.broadcasted_iota(jnp.int32, (tm, tm), 0)
        c = lax.broadcasted_iota(jnp.int32, (tm, tm), 1)
        tri = jnp.where(r >= c, 1.0, 0.0).astype(jnp.bfloat16)
        cum = _tri_dot(tri, ls) + carry[0:1, :]
        carry[...] = jnp.broadcast_to(cum[tm - 1:tm, :], carry.shape)
        for h in range(HEADS):
            crep_ref[h] = jnp.broadcast_to(cum[:, h:h + 1], (tm, LANES))
        ct_ref[...] = cum.T[:HEADS, :]

    return _pcall(
        body, name=name, grid=(nt,),
        in_specs=[pl.BlockSpec((tm, LANES), lambda i: (i, 0))],
        out_specs=[pl.BlockSpec((HEADS, tm, LANES), lambda i: (0, i, 0)),
                   pl.BlockSpec((HEADS, tm), lambda i: (0, i))],
        out_shape=[jax.ShapeDtypeStruct((HEADS, T, LANES), f32), jax.ShapeDtypeStruct((HEADS, T), f32)],
        scratch_shapes=[pltpu.VMEM((8, LANES), f32)],
        compiler_params=_params(1),
    )(fgb)


def _head_mask(shape, h):
    lane = lax.broadcasted_iota(jnp.int32, shape, 1)
    return (lane < HEAD_DIM) if h == 0 else (lane >= HEAD_DIM)


def _causal_mask(qi, ki, tq, tk):
    r = lax.broadcasted_iota(jnp.int32, (tq, tk), 0)
    c = lax.broadcasted_iota(jnp.int32, (tq, tk), 1)
    return jnp.logical_and(qi == ki, c > r)


def _fox_fwd(qkv, crep, ct, *, name, tq=512):
    T = qkv.shape[0]
    tq = min(tq, T)
    tk = tq
    nq = T // tq
    rep = tk // LANES

    def body(q_ref, k_ref, v_ref, cq_ref, ct_ref, o_ref, lse_ref, m_sc, l_sc, acc_sc):
        j = pl.program_id(0)
        qi = pl.program_id(1)
        ki = pl.program_id(2)

        @pl.when(ki == 0)
        def _():
            m_sc[...] = jnp.full_like(m_sc, NEG_BIG)
            l_sc[...] = jnp.zeros_like(l_sc)
            acc_sc[...] = jnp.zeros_like(acc_sc)

        @pl.when(ki <= qi)
        def _():
            q2 = q_ref[...]
            k2 = k_ref[...]
            v2 = v_ref[...]
            future = _causal_mask(qi, ki, tq, tk)
            for h in range(2):
                qh = jnp.where(_head_mask(q2.shape, h), q2, jnp.zeros_like(q2))
                ck = ct_ref[pl.ds(2 * j + h, 1), :]
                s = _dot_nt(qh, k2) + (jnp.tile(cq_ref[h], (1, rep)) - ck)
                s = jnp.where(future, NEG_BIG, s)
                m_prev = m_sc[h]
                m_new = jnp.maximum(m_prev, jnp.max(s, axis=1, keepdims=True))
                p = jnp.exp(s - jnp.tile(m_new, (1, rep)))
                alpha = jnp.exp(m_prev - m_new)
                l_sc[h] = alpha * l_sc[h] + jnp.sum(p, axis=1, keepdims=True)
                acc_sc[h] = alpha * acc_sc[h] + _dot(p.astype(MXU_DTYPE), v2)
                m_sc[h] = m_new

        @pl.when(ki == qi)
        def _():
            o0 = acc_sc[0] / l_sc[0]
            o1 = acc_sc[1] / l_sc[1]
            o_ref[...] = jnp.where(_head_mask(o0.shape, 0), o0, o1)
            for h in range(2):
                lse_ref[h] = m_sc[h] + jnp.log(l_sc[h])

    kv = lambda j, qi, ki: jnp.minimum(ki, qi)
    return _pcall(
        body, name=name, grid=(HEADS // 2, nq, nq),
        in_specs=[
            pl.BlockSpec((tq, LANES), lambda j, qi, ki: (qi, j)),
            pl.BlockSpec((tk, LANES), lambda j, qi, ki: (kv(j, qi, ki), 4 + j)),
            pl.BlockSpec((tk, LANES), lambda j, qi, ki: (kv(j, qi, ki), 8 + j)),
            pl.BlockSpec((2, tq, LANES), lambda j, qi, ki: (j, qi, 0)),
            pl.BlockSpec((HEADS, tk), lambda j, qi, ki: (0, kv(j, qi, ki))),
        ],
        out_specs=[pl.BlockSpec((tq, LANES), lambda j, qi, ki: (qi, j)),
                   pl.BlockSpec((2, tq, LANES), lambda j, qi, ki: (j, qi, 0))],
        out_shape=[jax.ShapeDtypeStruct((T, FOX_W), f32), jax.ShapeDtypeStruct((HEADS, T, LANES), f32)],
        scratch_shapes=[pltpu.VMEM((2, tq, LANES), f32)] * 3,
        compiler_params=_params(3),
    )(qkv, qkv, qkv, crep, ct)


def _fox_bwd_prep(do, o, *, name, tm=512):
    T = o.shape[0]
    tm = min(tm, T)
    nt = T // tm

    def body(do_ref, o_ref, d_ref):
        prod = do_ref[...].astype(f32) * o_ref[...]
        for j in range(HEADS // 2):
            pj = prod[:, j * LANES:(j + 1) * LANES]
            for h in range(2):
                dsum = jnp.sum(jnp.where(_head_mask(pj.shape, h), pj, 0.0), axis=1, keepdims=True)
                d_ref[2 * j + h] = jnp.broadcast_to(dsum, (tm, LANES))

    return _pcall(
        body, name=name, grid=(nt,),
        in_specs=[pl.BlockSpec((tm, FOX_W), lambda i: (i, 0)), pl.BlockSpec((tm, FOX_W), lambda i: (i, 0))],
        out_specs=[pl.BlockSpec((HEADS, tm, LANES), lambda i: (0, i, 0))],
        out_shape=[jax.ShapeDtypeStruct((HEADS, T, LANES), f32)],
        compiler_params=_params(1),
    )(do, o)[0]


def _fox_bwd(qkv, do, crep, ct, lse, drep, *, name, tq=512):
    T = qkv.shape[0]
    tq = min(tq, T)
    tk = tq
    nq = T // tq
    rep = tk // LANES

    def body(q_ref, k_ref, v_ref, do_ref, cq_ref, ct_ref, lse_ref, d_ref,
             dq_ref, drow_ref, dk_ref, dv_ref, dcol_ref, dk_sc, dv_sc):
        j = pl.program_id(0)
        ki = pl.program_id(1)
        qi = pl.program_id(2)
        rows = pl.ds(pl.multiple_of(qi * tq, tq), tq)

        @pl.when(qi == 0)
        def _():
            dk_sc[...] = jnp.zeros_like(dk_sc)
            dv_sc[...] = jnp.zeros_like(dv_sc)

        @pl.when(jnp.logical_and(ki == 0, qi == 0))
        def _():
            dq_ref[...] = jnp.zeros_like(dq_ref)
            drow_ref[...] = jnp.zeros_like(drow_ref)

        @pl.when(qi >= ki)
        def _():
            q2 = q_ref[...]
            k2 = k_ref[...]
            v2 = v_ref[...]
            do2 = do_ref[...]
            future = _causal_mask(qi, ki, tq, tk)
            dq_acc = jnp.zeros((tq, LANES), f32)
            drow_acc = jnp.zeros((tq, LANES), f32)
            for h in range(2):
                hm = _head_mask(q2.shape, h)
                qh = jnp.where(hm, q2, jnp.zeros_like(q2))
                ck = ct_ref[pl.ds(2 * j + h, 1), :]
                s = _dot_nt(qh, k2) + (jnp.tile(cq_ref[h], (1, rep)) - ck)
                p = jnp.exp(s - jnp.tile(lse_ref[h], (1, rep)))
                p = jnp.where(future, 0.0, p)
                doh = jnp.where(hm, do2, jnp.zeros_like(do2))
                dp = _dot_nt(doh, v2)
                ds = (p * (dp - jnp.tile(d_ref[h], (1, rep)))).astype(MXU_DTYPE)
                dv_sc[h] += _dot_tn(p.astype(MXU_DTYPE), do2)
                q_ones = jnp.where(hm, q2, jnp.ones_like(q2))
                dk_sc[h] += _dot_tn(ds, q_ones)
                dq_full = _dot(ds, jnp.where(hm, k2, jnp.ones_like(k2)))
                dq_acc = dq_acc + jnp.where(hm, dq_full, 0.0)
                drow_acc = drow_acc + jnp.where(hm, 0.0, dq_full)
            dq_ref[rows, :] += dq_acc
            drow_ref[rows, :] += drow_acc

        @pl.when(qi == nq - 1)
        def _():
            hm0 = _head_mask((tk, LANES), 0)
            dk_ref[...] = jnp.where(hm0, dk_sc[0], dk_sc[1])
            dcol_ref[...] = jnp.where(hm0, dk_sc[1], dk_sc[0])
            dv_ref[...] = jnp.where(hm0, dv_sc[0], dv_sc[1])

    qb = lambda j, ki, qi: jnp.maximum(qi, ki)
    return _pcall(
        body, name=name, grid=(HEADS // 2, nq, nq),
        in_specs=[
            pl.BlockSpec((tq, LANES), lambda j, ki, qi: (qb(j, ki, qi), j)),
            pl.BlockSpec((tk, LANES), lambda j, ki, qi: (ki, 4 + j)),
            pl.BlockSpec((tk, LANES), lambda j, ki, qi: (ki, 8 + j)),
            pl.BlockSpec((tq, LANES), lambda j, ki, qi: (qb(j, ki, qi), j)),
            pl.BlockSpec((2, tq, LANES), lambda j, ki, qi: (j, qb(j, ki, qi), 0)),
            pl.BlockSpec((HEADS, tk), lambda j, ki, qi: (0, ki)),
            pl.BlockSpec((2, tq, LANES), lambda j, ki, qi: (j, qb(j, ki, qi), 0)),
            pl.BlockSpec((2, tq, LANES), lambda j, ki, qi: (j, qb(j, ki, qi), 0)),
        ],
        out_specs=[
            pl.BlockSpec((T, LANES), lambda j, ki, qi: (0, j)),
            pl.BlockSpec((T, LANES), lambda j, ki, qi: (0, j)),
            pl.BlockSpec((tk, LANES), lambda j, ki, qi: (ki, j)),
            pl.BlockSpec((tk, LANES), lambda j, ki, qi: (ki, j)),
            pl.BlockSpec((tk, LANES), lambda j, ki, qi: (ki, j)),
        ],
        out_shape=[jax.ShapeDtypeStruct((T, FOX_W), f32)] * 5,
        scratch_shapes=[pltpu.VMEM((2, tk, LANES), f32)] * 2,
        compiler_params=_params(3),
    )(qkv, qkv, qkv, do, crep, ct, lse, drep)


def _fox_bwd_post(drow, dcol, fgb, *, name, tm=512):
    T = fgb.shape[0]
    tm = min(tm, T)
    nt = T // tm

    def body(drow_ref, dcol_ref, fg_ref, dfg_ref, dbf_ref, carry):
        i = pl.program_id(0)

        @pl.when(i == 0)
        def _():
            carry[...] = jnp.zeros_like(carry)
            dbf_ref[...] = jnp.zeros_like(dbf_ref)

        dcol_t = drow_ref[...] - dcol_ref[...]
        lane = lax.broadcasted_iota(jnp.int32, (tm, LANES), 1)
        dc = jnp.zeros((tm, LANES), f32)
        for h in range(HEADS):
            src = (h // 2) * LANES + (HEAD_DIM if h % 2 == 0 else 0)
            dc = jnp.where(lane == h, jnp.broadcast_to(dcol_t[:, src:src + 1], (tm, LANES)), dc)
        r = lax.broadcasted_iota(jnp.int32, (tm, tm), 0)
        c = lax.broadcasted_iota(jnp.int32, (tm, tm), 1)
        tri = jnp.where(c >= r, 1.0, 0.0).astype(jnp.bfloat16)
        dls = _tri_dot(tri, dc) + carry[0:1, :]
        carry[...] = jnp.broadcast_to(dls[0:1, :], carry.shape)
        dfg = dls * _sigmoid(-fg_ref[...])
        dfg_ref[...] = dfg
        dbf_ref[...] += jnp.sum(dfg, axis=0, keepdims=True)

    rev = lambda i: (nt - 1 - i, 0)
    return _pcall(
        body, name=name, grid=(nt,),
        in_specs=[pl.BlockSpec((tm, FOX_W), rev), pl.BlockSpec((tm, FOX_W), rev), pl.BlockSpec((tm, LANES), rev)],
        out_specs=[pl.BlockSpec((tm, LANES), rev), pl.BlockSpec((1, LANES), lambda i: (0, 0))],
        out_shape=[jax.ShapeDtypeStruct((T, LANES), f32), jax.ShapeDtypeStruct((1, LANES), f32)],
        scratch_shapes=[pltpu.VMEM((8, LANES), f32)],
        compiler_params=_params(1),
    )(drow, dcol, fgb)


GELU_C = math.sqrt(2.0 / math.pi)
GELU_A = 0.044715


def _gelu(x):
    t = jnp.tanh(GELU_C * (x + GELU_A * x * x * x))
    return 0.5 * x * (1.0 + t), t


def _gelu_grad(x, t):
    return 0.5 * (1.0 + t) + 0.5 * x * (1.0 - t * t) * GELU_C * (1.0 + 3.0 * GELU_A * x * x)


def _expm1(x):
    e = jnp.exp(x)
    safe = jnp.where(e == 1.0, x, (e - 1.0) * x / jnp.log(jnp.where(e == 1.0, 0.5, e)))
    return jnp.where(x < -0.5, e - 1.0, safe)


def _lru_gates(u, wab_ref, bab_ref, lam_ref):
    pre = _dot(u.astype(MXU_DTYPE), wab_ref[...]) + bab_ref[...]
    r = _sigmoid(pre[:, :LRU_W])
    gi = _sigmoid(pre[:, LRU_W:])
    lam = lam_ref[...]
    sp = jnp.maximum(-lam, 0.0) + jnp.log(1.0 + jnp.exp(-jnp.abs(lam)))
    log_a = -LRU_C * r * sp
    a = jnp.exp(log_a)
    s = jnp.sqrt(-_expm1(2.0 * log_a))
    return r, gi, sp, a, s


def _lru_fwd(lxg, conv_w, conv_b, wab, bab, lam, *, name, tc=512):
    T = lxg.shape[0]
    tc = min(tc, T)
    nc = T // tc

    def body(lx_ref, lg_ref, cw_ref, cb_ref, wab_ref, bab_ref, lam_ref,
             out_ref, u_ref, hs_ref, ext, a_sc, b_sc, h_sc):
        i = pl.program_id(0)

        @pl.when(i == 0)
        def _():
            ext[0:8, :] = jnp.zeros((8, LRU_W), f32)
            h_sc[...] = jnp.zeros_like(h_sc)

        ext[8:, :] = lx_ref[...]
        u = cb_ref[...] + cw_ref[0:1, :] * ext[pl.ds(5, tc), :]
        for k in range(1, CONV_K):
            u = u + cw_ref[k:k + 1, :] * ext[pl.ds(5 + k, tc), :]
        ext[0:8, :] = ext[tc:tc + 8, :]
        u_ref[...] = u
        r, gi, sp, a, s = _lru_gates(u, wab_ref, bab_ref, lam_ref)
        a_sc[...] = a
        b_sc[...] = s * (gi * u)

        def step(t, h):
            h = a_sc[pl.ds(t, 1), :] * h + b_sc[pl.ds(t, 1), :]
            hs_ref[pl.ds(t, 1), :] = h
            return h

        h = lax.fori_loop(0, tc, step, h_sc[0:1, :], unroll=8)
        h_sc[...] = jnp.broadcast_to(h, h_sc.shape)
        gel, _ = _gelu(lg_ref[...])
        out_ref[...] = gel * hs_ref[...]

    row = lambda i: (i, 0)
    const = lambda i: (0, 0)
    return _pcall(
        body, name=name, grid=(nc,),
        in_specs=[pl.BlockSpec((tc, LRU_W), row), pl.BlockSpec((tc, LRU_W), lambda i: (i, 1)),
                  pl.BlockSpec((CONV_K, LRU_W), const), pl.BlockSpec((1, LRU_W), const),
                  pl.BlockSpec((LRU_W, 2 * LRU_W), const), pl.BlockSpec((1, 2 * LRU_W), const),
                  pl.BlockSpec((1, LRU_W), const)],
        out_specs=[pl.BlockSpec((tc, LRU_W), row)] * 3,
        out_shape=[jax.ShapeDtypeStruct((T, LRU_W), f32)] * 3,
        scratch_shapes=[pltpu.VMEM((tc + 8, LRU_W), f32), pltpu.VMEM((tc, LRU_W), f32),
                        pltpu.VMEM((tc, LRU_W), f32), pltpu.VMEM((8, LRU_W), f32)],
        compiler_params=_params(1),
    )(lxg, lxg, conv_w, conv_b, wab, bab, lam)


def _lru_bwd(dlru, lxg, u, hs, conv_w, wab, bab, lam, *, name, tc=512):
    T = lxg.shape[0]
    tc = min(tc, T)
    nc = T // tc
    bp = tc // 8

    def body(dl_ref, lx_ref, lxp_ref, lg_ref, u_ref, hs_ref, hsp_ref, cw_ref, wab_ref, bab_ref, lam_ref,
             dlxg_ref, dwab_ref, dbab_ref, dcw_ref, dcb_ref, dlam_ref,
             dh_sc, a_sc, ext, du_ext, carry):
        i = pl.program_id(0)
        first_chunk = i == nc - 1

        @pl.when(i == 0)
        def _():
            dwab_ref[...] = jnp.zeros_like(dwab_ref)
            dbab_ref[...] = jnp.zeros_like(dbab_ref)
            dcw_ref[...] = jnp.zeros_like(dcw_ref)
            dcb_ref[...] = jnp.zeros_like(dcb_ref)
            dlam_ref[...] = jnp.zeros_like(dlam_ref)
            carry[...] = jnp.zeros_like(carry)
            du_ext[tc:tc + 8, :] = jnp.zeros((8, LRU_W), f32)

        lg = lg_ref[...]
        gel, th = _gelu(lg)
        dl = dl_ref[...]
        hs = hs_ref[...]
        dlg = dl * hs * _gelu_grad(lg, th)
        u = u_ref[...]
        r, gi, sp, a, s = _lru_gates(u, wab_ref, bab_ref, lam_ref)
        a_sc[...] = a
        dh_sc[...] = dl * gel

        def step(k, c):
            t = tc - 1 - k
            dh = dh_sc[pl.ds(t, 1), :] + c
            dh_sc[pl.ds(t, 1), :] = dh
            return a_sc[pl.ds(t, 1), :] * dh

        c = lax.fori_loop(0, tc, step, carry[0:1, :], unroll=8)
        carry[...] = jnp.broadcast_to(c, carry.shape)

        ext[0:8, :] = jnp.where(first_chunk, 0.0, hsp_ref[...])
        ext[8:, :] = hs
        hprev = ext[pl.ds(7, tc), :]
        dh = dh_sc[...]
        da = dh * hprev
        giu = gi * u
        dla = da * a - (dh * giu) * (a * a / s)
        dgi = dh * s * u
        du = dh * s * gi
        dr = dla * (-LRU_C * sp)
        dlam_ref[...] += jnp.sum(dla * (-LRU_C * r), axis=0, keepdims=True) * (-_sigmoid(-lam_ref[...]))
        dpre = jnp.concatenate([dr * r * (1.0 - r), dgi * gi * (1.0 - gi)], axis=1)
        dpre_b = dpre.astype(MXU_DTYPE)
        du = du + _dot_nt(dpre_b, wab_ref[...])
        dwab_ref[...] += _dot_tn(u.astype(MXU_DTYPE), dpre_b)
        dbab_ref[...] += jnp.sum(dpre, axis=0, keepdims=True)
        dcb_ref[...] += jnp.sum(du, axis=0, keepdims=True)

        du_ext[0:tc, :] = du
        dlx = cw_ref[0:1, :] * du_ext[pl.ds(3, tc), :]
        for k in range(1, CONV_K):
            dlx = dlx + cw_ref[k:k + 1, :] * du_ext[pl.ds(3 - k, tc), :]
        du_ext[tc:tc + 8, :] = du_ext[0:8, :]
        ext[0:8, :] = jnp.where(first_chunk, 0.0, lxp_ref[...])
        ext[8:, :] = lx_ref[...]
        for k in range(CONV_K):
            dcw_ref[k:k + 1, :] += jnp.sum(du * ext[pl.ds(5 + k, tc), :], axis=0, keepdims=True)
        dlxg_ref[:, :LRU_W] = dlx.astype(dlxg_ref.dtype)
        dlxg_ref[:, LRU_W:] = dlg.astype(dlxg_ref.dtype)

    rev = lambda i: (nc - 1 - i, 0)
    prev8 = lambda i: (jnp.maximum((nc - 1 - i) * bp - 1, 0), 0)
    const = lambda i: (0, 0)
    return _pcall(
        body, name=name, grid=(nc,),
        in_specs=[
            pl.BlockSpec((tc, LRU_W), rev),
            pl.BlockSpec((tc, LRU_W), rev),
            pl.BlockSpec((8, LRU_W), prev8),
            pl.BlockSpec((tc, LRU_W), lambda i: (nc - 1 - i, 1)),
            pl.BlockSpec((tc, LRU_W), rev),
            pl.BlockSpec((tc, LRU_W), rev),
            pl.BlockSpec((8, LRU_W), prev8),
            pl.BlockSpec((CONV_K, LRU_W), const),
            pl.BlockSpec((LRU_W, 2 * LRU_W), const),
            pl.BlockSpec((1, 2 * LRU_W), const),
            pl.BlockSpec((1, LRU_W), const),
        ],
        out_specs=[
            pl.BlockSpec((tc, 2 * LRU_W), rev),
            pl.BlockSpec((LRU_W, 2 * LRU_W), const),
            pl.BlockSpec((1, 2 * LRU_W), const),
            pl.BlockSpec((8, LRU_W), const),
            pl.BlockSpec((1, LRU_W), const),
            pl.BlockSpec((1, LRU_W), const),
        ],
        out_shape=[
            jax.ShapeDtypeStruct((T, 2 * LRU_W), MXU_DTYPE),
            jax.ShapeDtypeStruct((LRU_W, 2 * LRU_W), f32),
            jax.ShapeDtypeStruct((1, 2 * LRU_W), f32),
            jax.ShapeDtypeStruct((8, LRU_W), f32),
            jax.ShapeDtypeStruct((1, LRU_W), f32),
            jax.ShapeDtypeStruct((1, LRU_W), f32),
        ],
        scratch_shapes=[pltpu.VMEM((tc, LRU_W), f32), pltpu.VMEM((tc, LRU_W), f32),
                        pltpu.VMEM((tc + 8, LRU_W), f32), pltpu.VMEM((tc + 8, LRU_W), f32),
                        pltpu.VMEM((8, LRU_W), f32)],
        compiler_params=_params(1),
    )(dlru, lxg, lxg, lxg, u, hs, hs, conv_w, wab, bab, lam)


def _mix_out(fox, lru, wo, xhat1, g1, b1, g2, b2, *, name, tm=512):
    T = fox.shape[0]
    tm = min(tm, T)
    nt = T // tm

    def body(fox_ref, lru_ref, wo_ref, xh_ref, g1_ref, b1_ref, g2_ref, b2_ref, xhat_ref, xn_ref, rstd_ref):
        mix = _dot(fox_ref[...].astype(MXU_DTYPE), wo_ref[:FOX_W, :])
        mix = mix + _dot(lru_ref[...].astype(MXU_DTYPE), wo_ref[FOX_W:, :])
        x1 = xh_ref[...] * g1_ref[...] + b1_ref[...]
        xhat, rstd = _layer_norm_stats(DN_ALPHA * x1 + mix)
        xhat_ref[...] = xhat
        xn_ref[...] = xhat * g2_ref[...] + b2_ref[...]
        rstd_ref[...] = jnp.broadcast_to(rstd, rstd_ref.shape)

    row = lambda i: (i, 0)
    const = lambda i: (0, 0)
    vec = pl.BlockSpec((1, D_MODEL), const)
    return _pcall(
        body, name=name, grid=(nt,),
        in_specs=[pl.BlockSpec((tm, FOX_W), row), pl.BlockSpec((tm, LRU_W), row),
                  pl.BlockSpec((D_MODEL, D_MODEL), const), pl.BlockSpec((tm, D_MODEL), row), vec, vec, vec, vec],
        out_specs=[pl.BlockSpec((tm, D_MODEL), row), pl.BlockSpec((tm, D_MODEL), row),
                   pl.BlockSpec((tm, LANES), row)],
        out_shape=[jax.ShapeDtypeStruct((T, D_MODEL), f32), jax.ShapeDtypeStruct((T, D_MODEL), f32),
                   jax.ShapeDtypeStruct((T, LANES), f32)],
        compiler_params=_params(1),
    )(fox, lru, wo, xhat1, g1, b1, g2, b2)


def _mix_out_bwd(dyp, fox, lru, wo, *, name, tm=512):
    T = fox.shape[0]
    tm = min(tm, T)
    nt = T // tm

    def body(dyp_ref, fox_ref, lru_ref, wo_ref, dfox_ref, dlru_ref, dwo_ref):
        i = pl.program_id(0)

        @pl.when(i == 0)
        def _():
            dwo_ref[...] = jnp.zeros_like(dwo_ref)

        dmix = dyp_ref[...].astype(MXU_DTYPE)
        dcat = _dot_nt(dmix, wo_ref[...])
        dfox_ref[...] = dcat[:, :FOX_W].astype(dfox_ref.dtype)
        dlru_ref[...] = dcat[:, FOX_W:]
        dwo_ref[:FOX_W, :] += _dot_tn(fox_ref[...].astype(MXU_DTYPE), dmix)
        dwo_ref[FOX_W:, :] += _dot_tn(lru_ref[...].astype(MXU_DTYPE), dmix)

    row = lambda i: (i, 0)
    const = lambda i: (0, 0)
    return _pcall(
        body, name=name, grid=(nt,),
        in_specs=[pl.BlockSpec((tm, D_MODEL), row), pl.BlockSpec((tm, FOX_W), row), pl.BlockSpec((tm, LRU_W), row),
                  pl.BlockSpec((D_MODEL, D_MODEL), const)],
        out_specs=[pl.BlockSpec((tm, FOX_W), row), pl.BlockSpec((tm, LRU_W), row),
                   pl.BlockSpec((D_MODEL, D_MODEL), const)],
        out_shape=[jax.ShapeDtypeStruct((T, FOX_W), MXU_DTYPE), jax.ShapeDtypeStruct((T, LRU_W), f32),
                   jax.ShapeDtypeStruct((D_MODEL, D_MODEL), f32)],
        compiler_params=_params(1),
    )(dyp, fox, lru, wo)


def make_wp(w_in):
    scale = jnp.concatenate([jnp.full((FOX_W,), 1.0 / math.sqrt(HEAD_DIM), w_in.dtype),
                             jnp.ones((IN_COLS - FOX_W,), w_in.dtype)])
    return jnp.pad(w_in * scale[None, :], ((0, 0), (0, Z_PAD - IN_COLS)))


def _block_diag(w):
    eye = jnp.eye(HEADS, dtype=w.dtype)
    return jnp.einsum("hij,hg->higj", w, eye).reshape(LRU_W, LRU_W)


def _block_diag_extract(m):
    m4 = m.reshape(HEADS, HEAD_DIM, HEADS, HEAD_DIM)
    return jnp.stack([m4[h, :, h, :] for h in range(HEADS)])


class _NoOverlap:
    def start_token(self):
        return None

    def after_attention(self, after):
        return None

    def ffn2_weights(self, w, after):
        return w["f2g"], w["f2u"], w["f2d"]

    def ffn2_grads(self, grads):
        return None

    def before_ffn1_bwd(self, after):
        return None


def _tied(a, token):
    return a if token is None else a + token[0, 0]


def _local_step(x, target, w, hooks=None):
    hooks = hooks or _NoOverlap()
    wp = w["wp"]
    bfp = w["bfp"]
    wab = jnp.concatenate([_block_diag(w["rg_wa"]), _block_diag(w["rg_wx"])], axis=1).astype(MXU_DTYPE)
    bab = jnp.concatenate([w["rg_ba"].reshape(1, LRU_W), w["rg_bx"].reshape(1, LRU_W)], axis=1)

    xb0, g1a, u1a, xhat1, xn1, rstd1 = _ffn_fwd(x, w["f1g"], w["f1u"], w["f1d"], w["ln1_g"],
                                                _tied(w["ln1_b"], hooks.start_token()), name="ffn1_fwd")
    qkv, lxg, fgb = _proj_in(xn1, wp, bfp, name="proj_in")
    crep, ct = _fox_prep(fgb, name="fox_prep")
    fox, lse = _fox_fwd(qkv, crep, ct, name="fox_fwd")
    token = hooks.after_attention(lse)
    lru, uconv, hs = _lru_fwd(lxg, w["conv_w"], _tied(w["conv_b"], token), wab, bab, w["lam"], name="lru_fwd")
    xhat2, x2, rstd2 = _mix_out(fox, lru, w["wo"], xhat1, w["ln1_g"], w["ln1_b"], w["ln2_g"], w["ln2_b"], name="mix_out")
    f2g, f2u, f2d = hooks.ffn2_weights(w, rstd2)
    xb2, g2a, u2a, xhat3, _, rstd3 = _ffn_fwd(x2, f2g, f2u, f2d, w["ln3_g"], w["ln3_b"], name="ffn2_fwd")

    dy3p, dln3g, dln3b, loss = _loss_ln_bwd(xhat3, rstd3, w["ln3_g"], w["ln3_b"], target, name="loss_ln3_bwd")
    dx2, df2g, df2u, df2d = _ffn_bwd(dy3p, xb2, g2a, u2a, f2g, f2u, f2d, name="ffn2_bwd")
    token = hooks.ffn2_grads([df2g, df2u, df2d])
    dy2p, dln2g, dln2b = _ln_bwd(dx2, xhat2, rstd2, _tied(w["ln2_g"], token), name="ln2_bwd")
    dfox, dlru, dwo = _mix_out_bwd(dy2p, fox, lru, w["wo"], name="mix_out_bwd")
    dlxg, dwab, dbab, dcw, dcb, dlam = _lru_bwd(dlru, lxg, uconv, hs, w["conv_w"], wab, bab, w["lam"], name="lru_bwd")
    drep = _fox_bwd_prep(dfox, fox, name="fox_bwd_prep")
    dq, drow, dk, dv, dcol = _fox_bwd(qkv, dfox, crep, ct, lse, drep, name="fox_bwd")
    dfg, dbf = _fox_bwd_post(drow, dcol, fgb, name="fox_bwd_post")
    dx1, dwp = _proj_in_bwd(dq, dk, dv, dlxg, dfg, xn1, dy2p, wp, name="proj_in_bwd")
    dy1p, dln1g, dln1b = _ln_bwd(dx1, xhat1, rstd1, w["ln1_g"], name="ln1_bwd")
    hooks.before_ffn1_bwd(dln1b)
    dx, df1g, df1u, df1d = _ffn_bwd(dy1p, xb0, g1a, u1a, w["f1g"], w["f1u"], w["f1d"], name="ffn1_bwd")

    grads = dict(
        f1g=df1g, f1u=df1u, f1d=df1d, f2g=df2g, f2u=df2u, f2d=df2d, wp=dwp, wo=dwo,
        ln1_g=dln1g, ln1_b=dln1b, ln2_g=dln2g, ln2_b=dln2b, ln3_g=dln3g, ln3_b=dln3b,
        b_forget=dbf[:, :HEADS], conv_w=dcw[:CONV_K], conv_b=dcb,
        rg_wa=_block_diag_extract(dwab[:, :LRU_W]), rg_wx=_block_diag_extract(dwab[:, LRU_W:]),
        rg_ba=dbab[:, :LRU_W].reshape(HEADS, HEAD_DIM), rg_bx=dbab[:, LRU_W:].reshape(HEADS, HEAD_DIM),
        lam=dlam,
    )
    return loss, dx, grads


MESH = pl.DeviceIdType.MESH
HBM_SPEC = pl.BlockSpec(memory_space=pl.ANY)
VMEM_SPEC = pl.BlockSpec(memory_space=pltpu.VMEM)


def _position():
    return lax.axis_index("x"), lax.axis_index("y"), lax.axis_index("c")


def _other_chips(x, y):
    return [(1 - x, y), (x, 1 - y), (1 - x, 1 - y)]


def _all_gather_bf16(shards, *, name):
    n = len(shards)

    def body(*refs):
        ins, outs, stages = refs[:n], refs[n:2 * n], refs[2 * n:3 * n]
        send_sems, recv_sems, local_sems = refs[3 * n:]
        x, y, c = _position()
        me, sibling = (x, y, c), (x, y, 1 - c)
        chips = _other_chips(x, y)

        def rows(k, px, py, pc):
            r = shards[k].shape[0]
            m = r // 2
            return outs[k].at[pl.ds(pl.multiple_of((2 * px + py) * r + pc * m, 16), m), :]

        def copy(k, idx, block, to, src=None):
            return pltpu.make_async_remote_copy(
                src_ref=rows(k, *block) if src is None else src, dst_ref=rows(k, *block),
                send_sem=send_sems.at[7 * k + idx], recv_sem=recv_sems.at[7 * k + idx],
                device_id=to, device_id_type=MESH)

        started = []
        mine = []
        for k in range(n):
            m = shards[k].shape[0] // 2
            stages[k][...] = ins[k][pl.ds(pl.multiple_of(c * m, 16), m), :].astype(stages[k].dtype)
            cp = pltpu.make_async_copy(stages[k], rows(k, *me), local_sems.at[k])
            cp.start()
            mine.append(cp)
            first = [copy(k, 0, me, sibling, src=stages[k])]
            first += [copy(k, 1 + j, me, (*chip, c), src=stages[k]) for j, chip in enumerate(chips)]
            for cp in first:
                cp.start()
            started += first
        for k in range(n):
            for j, chip in enumerate(chips):
                copy(k, 1 + j, (*chip, c), me).wait_recv()
                fwd = copy(k, 4 + j, (*chip, c), sibling)
                fwd.start()
                started.append(fwd)
        for k in range(n):
            copy(k, 0, sibling, me).wait_recv()
            for j, chip in enumerate(chips):
                copy(k, 4 + j, (*chip, 1 - c), me).wait_recv()
        for cp in started:
            cp.wait_send()
        for cp in mine:
            cp.wait()

    return _pcall(
        body, name=name,
        in_specs=[VMEM_SPEC] * n, out_specs=[HBM_SPEC] * n,
        out_shape=[jax.ShapeDtypeStruct((N_SHARD * s.shape[0], s.shape[1]), MXU_DTYPE) for s in shards],
        scratch_shapes=[pltpu.VMEM((s.shape[0] // 2, s.shape[1]), MXU_DTYPE) for s in shards]
        + [pltpu.SemaphoreType.DMA((7 * n,)), pltpu.SemaphoreType.DMA((7 * n,)), pltpu.SemaphoreType.DMA((n,))],
        compiler_params=pltpu.CompilerParams(vmem_limit_bytes=VMEM_LIMIT),
    )(*shards)


def _swap_halves(gs, *, name):
    n = len(gs)

    def body(*refs):
        ins, outs = refs[:n], refs[n:2 * n]
        send_sems, recv_sems = refs[2 * n:]
        x, y, c = _position()
        cps = []
        for k in range(n):
            m = gs[k].shape[1] // 2
            src = ins[k].at[:, pl.ds(pl.multiple_of((1 - c) * m, 16), m), :]
            cp = pltpu.make_async_remote_copy(src_ref=src, dst_ref=outs[k], send_sem=send_sems.at[k],
                                              recv_sem=recv_sems.at[k], device_id=(x, y, 1 - c), device_id_type=MESH)
            cp.start()
            cps.append(cp)
        for cp in cps:
            cp.wait()

    return _pcall(
        body, name=name, in_specs=[HBM_SPEC] * n, out_specs=[HBM_SPEC] * n,
        out_shape=[jax.ShapeDtypeStruct((g.shape[0], g.shape[1] // 2, g.shape[2]), g.dtype) for g in gs],
        scratch_shapes=[pltpu.SemaphoreType.DMA((n,)), pltpu.SemaphoreType.DMA((n,))],
    )(*gs)


def _add_halves(gs, recvs, *, name, tm=256):
    n = len(gs)
    _, r, cdim = gs[0].shape
    m = r // 2
    tm = min(tm, m)
    nb = m // tm
    c_idx = lax.axis_index("c").astype(jnp.int32).reshape(1)

    def body(c_ref, *refs):
        for k in range(n):
            refs[2 * n + k][...] = (refs[k][...].astype(f32) + refs[n + k][...].astype(f32)).astype(refs[2 * n + k].dtype)

    mine = pl.BlockSpec((None, tm, cdim), lambda j, i, c_ref: (j, c_ref[0] * nb + i, 0))
    half = pl.BlockSpec((None, tm, cdim), lambda j, i, c_ref: (j, i, 0))
    return _pcall(
        body, name=name,
        grid_spec=pltpu.PrefetchScalarGridSpec(
            num_scalar_prefetch=1, grid=(N_SHARD, nb),
            in_specs=[mine] * n + [half] * n, out_specs=[half] * n),
        out_shape=[jax.ShapeDtypeStruct((N_SHARD, m, cdim), g.dtype) for g in gs],
        compiler_params=_params(2),
    )(c_idx, *gs, *recvs)


def _scatter_partials(ps, *, name):
    n = len(ps)

    def body(*refs):
        ins, outs = refs[:n], refs[n:2 * n]
        send_sems, recv_sems = refs[2 * n:]
        x, y, c = _position()
        me_chip = 2 * x + y
        cps = []
        for k in range(n):
            for j, (px, py) in enumerate(_other_chips(x, y)):
                cp = pltpu.make_async_remote_copy(
                    src_ref=ins[k].at[2 * px + py], dst_ref=outs[k].at[me_chip],
                    send_sem=send_sems.at[3 * k + j], recv_sem=recv_sems.at[3 * k + j],
                    device_id=(px, py, c), device_id_type=MESH)
                cp.start()
                cps.append(cp)
        for cp in cps:
            cp.wait()

    return _pcall(
        body, name=name, in_specs=[HBM_SPEC] * n, out_specs=[HBM_SPEC] * n,
        out_shape=[jax.ShapeDtypeStruct(p.shape, p.dtype) for p in ps],
        scratch_shapes=[pltpu.SemaphoreType.DMA((3 * n,)), pltpu.SemaphoreType.DMA((3 * n,))],
    )(*ps)


def _sum_slabs(ps, qs, *, name, tm=128):
    n = len(qs)
    _, m, cdim = qs[0].shape
    tm = min(tm, m)
    nb = m // tm
    assert m % tm == 0, (m, tm)
    where = jnp.stack([2 * lax.axis_index("x") + lax.axis_index("y"), lax.axis_index("c")]).astype(jnp.int32)

    def body(w_ref, *refs):
        for k in range(n):
            own, q1, q2, q3 = (refs[4 * k + t][...].astype(f32) for t in range(4))
            refs[4 * n + k][...] = ((own + q1) + q2) + q3

    def slab(flip):
        return pl.BlockSpec((None, tm, cdim), lambda i, w_ref: (jnp.bitwise_xor(w_ref[0], flip), i, 0))

    operands = []
    for p, q in zip(ps, qs):
        operands += [p, q, q, q]
    return _pcall(
        body, name=name,
        grid_spec=pltpu.PrefetchScalarGridSpec(
            num_scalar_prefetch=1, grid=(nb,),
            in_specs=[slab(0), slab(2), slab(1), slab(3)] * n,
            out_specs=[pl.BlockSpec((tm, cdim), lambda i, w_ref: (w_ref[1] * nb + i, 0))] * n),
        out_shape=[jax.ShapeDtypeStruct((2 * m, cdim), f32) for _ in qs],
        compiler_params=_params(1),
    )(where, *operands)


def _join_halves(fs, *, name):
    n = len(fs)

    def body(*refs):
        outs = refs[n:2 * n]
        send_sems, recv_sems = refs[2 * n:]
        x, y, c = _position()
        cps = []
        for k in range(n):
            m = fs[k].shape[0] // 2
            half = outs[k].at[pl.ds(pl.multiple_of(c * m, 8), m), :]
            cp = pltpu.make_async_remote_copy(src_ref=half, dst_ref=half, send_sem=send_sems.at[k],
                                              recv_sem=recv_sems.at[k], device_id=(x, y, 1 - c), device_id_type=MESH)
            cp.start()
            cps.append(cp)
        for cp in cps:
            cp.wait()

    return _pcall(
        body, name=name, in_specs=[HBM_SPEC] * n, out_specs=[HBM_SPEC] * n,
        out_shape=[jax.ShapeDtypeStruct(f.shape, f.dtype) for f in fs],
        input_output_aliases={k: k for k in range(n)},
        scratch_shapes=[pltpu.SemaphoreType.DMA((n,)), pltpu.SemaphoreType.DMA((n,))],
    )(*fs)


def _all_reduce_small(v, after=None, *, name):
    r = v.shape[0]
    extra = [] if after is None else [after]

    def body(v_ref, *refs):
        out_ref, buf, send_sems, recv_sems, local_sem = refs[len(extra):]
        x, y, c = _position()
        me, sibling = (x, y, c), (x, y, 1 - c)
        chips = _other_chips(x, y)

        def rows(px, py, pc):
            return buf.at[pl.ds(pl.multiple_of((4 * px + 2 * py + pc) * r, 8), r), :]

        def copy(k, block, to, src=None):
            return pltpu.make_async_remote_copy(
                src_ref=rows(*block) if src is None else src, dst_ref=rows(*block),
                send_sem=send_sems.at[k], recv_sem=recv_sems.at[k], device_id=to, device_id_type=MESH)

        mine = pltpu.make_async_copy(v_ref, rows(*me), local_sem)
        mine.start()
        first = [copy(0, me, sibling, src=v_ref)]
        first += [copy(1 + j, me, (*chip, c), src=v_ref) for j, chip in enumerate(chips)]
        for cp in first:
            cp.start()
        passed = [copy(4 + j, (*chip, c), sibling) for j, chip in enumerate(chips)]
        for j, chip in enumerate(chips):
            copy(1 + j, (*chip, c), me).wait_recv()
            passed[j].start()
        copy(0, sibling, me).wait_recv()
        for j, chip in enumerate(chips):
            copy(4 + j, (*chip, 1 - c), me).wait_recv()
        for cp in first + passed:
            cp.wait_send()
        mine.wait()
        acc = buf[0:r, :]
        for d in range(1, N_DEV):
            acc = acc + buf[d * r:(d + 1) * r, :]
        out_ref[...] = acc

    return _pcall(
        body, name=name, in_specs=[VMEM_SPEC] + [HBM_SPEC] * len(extra), out_specs=VMEM_SPEC,
        out_shape=jax.ShapeDtypeStruct((r, LANES), f32),
        scratch_shapes=[pltpu.VMEM((N_DEV * r, LANES), f32), pltpu.SemaphoreType.DMA((7,)),
                        pltpu.SemaphoreType.DMA((7,)), pltpu.SemaphoreType.DMA],
    )(v, *extra)


SEM_SPEC = pl.BlockSpec(memory_space=pltpu.SEMAPHORE)
HBM_ONLY = pl.BlockSpec(memory_space=pltpu.HBM)
EFFECT = pltpu.SideEffectType.DATAFLOW_SIDE_EFFECTING


def _split_start(bufs, copies_fn, n_sems, *, name):
    n = len(bufs)

    def body(*refs):
        send_sems, recv_sems = refs[n], refs[n + 1]
        thru = refs[n + 2:2 * n + 2]
        token = refs[2 * n + 2]
        for cp in copies_fn(thru, send_sems, recv_sems):
            cp.start()
        token[...] = jnp.zeros_like(token)

    outs = _pcall(
        body, name=name,
        out_shape=(pltpu.SemaphoreType.DMA((n_sems,)), pltpu.SemaphoreType.DMA((n_sems,)),
                   *[pltpu.HBM(b.shape, b.dtype) for b in bufs], jax.ShapeDtypeStruct((8, LANES), f32)),
        in_specs=[HBM_ONLY] * n,
        out_specs=(SEM_SPEC, SEM_SPEC, *[HBM_ONLY] * n, VMEM_SPEC),
        input_output_aliases={k: 2 + k for k in range(n)},
        compiler_params=pltpu.CompilerParams(has_side_effects=EFFECT),
    )(*[pltpu.with_memory_space_constraint(b, pltpu.HBM) for b in bufs])
    return outs[0], outs[1], list(outs[2:2 + n]), outs[2 + n]


def _split_wait(thru, send_sems, recv_sems, after, copies_fn, *, name):
    n = len(thru)

    def body(*refs):
        for cp in copies_fn(refs[:n], refs[n], refs[n + 1]):
            cp.wait_send()
            cp.wait_recv()

    return list(_pcall(
        body, name=name,
        out_shape=tuple(pltpu.HBM(b.shape, b.dtype) for b in thru),
        in_specs=[HBM_ONLY] * n + [SEM_SPEC, SEM_SPEC, HBM_SPEC],
        out_specs=tuple([HBM_ONLY] * n),
        input_output_aliases={k: k for k in range(n)},
        compiler_params=pltpu.CompilerParams(has_side_effects=EFFECT),
    )(*thru, send_sems, recv_sems, after))


def _scatter_copies(n):
    def copies(bufs, send_sems, recv_sems):
        x, y, c = _position()
        me_chip = 2 * x + y
        cps = []
        for k in range(n):
            for j, (px, py) in enumerate(_other_chips(x, y)):
                cps.append(pltpu.make_async_remote_copy(
                    src_ref=bufs[k].at[2 * px + py], dst_ref=bufs[n + k].at[me_chip],
                    send_sem=send_sems.at[3 * k + j], recv_sem=recv_sems.at[3 * k + j],
                    device_id=(px, py, c), device_id_type=MESH))
        return cps
    return copies


def _block_rows(buf, px, py, pc):
    m = buf.shape[0] // N_DEV
    return buf.at[pl.ds(pl.multiple_of((4 * px + 2 * py + pc) * m, 16), m), :]


def _gather_ici_copies(n):
    def copies(bufs, send_sems, recv_sems):
        x, y, c = _position()
        cps = []
        for k in range(n):
            rows = _block_rows(bufs[k], x, y, c)
            targets = [(x, y, 1 - c)] + [(px, py, c) for px, py in _other_chips(x, y)]
            for j, to in enumerate(targets):
                cps.append(pltpu.make_async_remote_copy(
                    src_ref=rows, dst_ref=rows, send_sem=send_sems.at[4 * k + j], recv_sem=recv_sems.at[4 * k + j],
                    device_id=to, device_id_type=MESH))
        return cps
    return copies


def _gather_d2d_copies(n):
    def copies(bufs, send_sems, recv_sems):
        x, y, c = _position()
        cps = []
        for k in range(n):
            for j, (px, py) in enumerate(_other_chips(x, y)):
                rows = _block_rows(bufs[k], px, py, c)
                cps.append(pltpu.make_async_remote_copy(
                    src_ref=rows, dst_ref=rows, send_sem=send_sems.at[3 * k + j], recv_sem=recv_sems.at[3 * k + j],
                    device_id=(x, y, 1 - c), device_id_type=MESH))
        return cps
    return copies


def _cast_halves(shards, after, *, name, tm=256):
    n = len(shards)
    r, cdim = shards[0].shape
    m = r // 2
    tm = min(tm, m)
    nb = m // tm
    assert m % tm == 0, (m, tm)
    where = jnp.stack([2 * lax.axis_index("x") + lax.axis_index("y"), lax.axis_index("c")]).astype(jnp.int32)

    def body(w_ref, *refs):
        for k in range(n):
            refs[n + 1 + k][...] = refs[k][...].astype(refs[n + 1 + k].dtype)

    return _pcall(
        body, name=name,
        grid_spec=pltpu.PrefetchScalarGridSpec(
            num_scalar_prefetch=1, grid=(nb,),
            in_specs=[pl.BlockSpec((tm, cdim), lambda i, w_ref: (w_ref[1] * nb + i, 0))] * n + [HBM_SPEC],
            out_specs=[pl.BlockSpec((tm, cdim), lambda i, w_ref: ((2 * w_ref[0] + w_ref[1]) * nb + i, 0))] * n),
        out_shape=[jax.ShapeDtypeStruct((N_SHARD * r, cdim), MXU_DTYPE) for _ in shards],
        compiler_params=_params(1),
    )(where, *shards, after)


class _Overlap(_NoOverlap):
    def __init__(self, ffn2_shards, after):
        halves = _cast_halves(ffn2_shards, after, name="ag2_cast")
        self.n = len(halves)
        self.ici = _split_start(halves, _gather_ici_copies(self.n), 4 * self.n, name="ag2_ici_start")
        self.reduced = None

    def start_token(self):
        return self.ici[3]

    def after_attention(self, after):
        send_sems, recv_sems, thru, _ = self.ici
        landed = _split_wait(thru, send_sems, recv_sems, after, _gather_ici_copies(self.n), name="ag2_ici_wait")
        self.d2d = _split_start(landed, _gather_d2d_copies(self.n), 3 * self.n, name="ag2_d2d_start")
        return self.d2d[3]

    def ffn2_weights(self, w, after):
        send_sems, recv_sems, thru, _ = self.d2d
        full = _split_wait(thru, send_sems, recv_sems, after, _gather_d2d_copies(self.n), name="ag2_d2d_wait")
        fs = D_FF // N_SHARD
        return (full[0].reshape(N_SHARD, D_MODEL, fs), full[1].reshape(N_SHARD, D_MODEL, fs),
                full[2].reshape(N_SHARD, fs, D_MODEL))

    def ffn2_grads(self, grads):
        recvs = _swap_halves(grads, name="rs_swap_ffn2")
        ps = _add_halves(grads, recvs, name="rs_add_ffn2")
        lands = [lax.empty(p.shape, p.dtype) for p in ps]
        self.scatter = _split_start(list(ps) + lands, _scatter_copies(len(ps)), 3 * len(ps), name="rs_scatter_ffn2_start")
        return self.scatter[3]

    def before_ffn1_bwd(self, after):
        send_sems, recv_sems, thru, _ = self.scatter
        n = len(thru) // 2
        done = _split_wait(thru, send_sems, recv_sems, after, _scatter_copies(n), name="rs_scatter_ffn2_wait")
        self.reduced = list(_sum_slabs(done[:n], done[n:], name="rs_sum_ffn2"))


def _adamw(gs, ws, ms, vs, *, name, tm=256):
    n = len(gs)
    r, cdim = gs[0].shape
    tm = r if tm is None else min(tm, r)
    assert r % tm == 0, (r, tm)
    c1 = 1.0 / (1.0 - ADAM_B1 ** ADAM_STEP)
    c2 = 1.0 / (1.0 - ADAM_B2 ** ADAM_STEP)

    def body(*refs):
        for k in range(n):
            g = refs[k][...]
            w = refs[n + k][...]
            m = ADAM_B1 * refs[2 * n + k][...] + (1.0 - ADAM_B1) * g
            v = ADAM_B2 * refs[3 * n + k][...] + (1.0 - ADAM_B2) * (g * g)
            refs[4 * n + k][...] = -ADAM_LR * ((m * c1) / (jnp.sqrt(v * c2) + ADAM_EPS) + ADAM_WD * w)
            refs[5 * n + k][...] = m
            refs[6 * n + k][...] = v

    spec = pl.BlockSpec((tm, cdim), lambda i: (i, 0))
    outs = _pcall(
        body, name=name, grid=(r // tm,), in_specs=[spec] * (4 * n), out_specs=[spec] * (3 * n),
        out_shape=[jax.ShapeDtypeStruct((r, cdim), f32)] * (3 * n),
        compiler_params=_params(1),
    )(*gs, *ws, *ms, *vs)
    return outs[:n], outs[n:2 * n], outs[2 * n:]


BIG = ["ffn1_w_gate", "ffn1_w_up", "ffn1_w_down", "ffn2_w_gate", "ffn2_w_up", "ffn2_w_down"]
SMALL = ["ln1_g", "ln1_b", "b_forget", "conv_w", "conv_b", "rg_wa", "rg_ba", "rg_wx", "rg_bx", "lru_lambda",
         "ln2_g", "ln2_b", "ln3_g", "ln3_b"]
WEIGHTS = ["ffn1_w_gate", "ffn1_w_up", "ffn1_w_down", "ln1_g", "ln1_b", "w_in", "b_forget", "conv_w", "conv_b",
           "rg_wa", "rg_ba", "rg_wx", "rg_bx", "lru_lambda", "w_out", "ln2_g", "ln2_b",
           "ffn2_w_gate", "ffn2_w_up", "ffn2_w_down", "ln3_g", "ln3_b"]


def _pack_small(parts):
    rows = []
    for n in SMALL:
        flat = parts[n].reshape(-1)
        pad = (-flat.shape[0]) % LANES
        rows.append(jnp.pad(flat, (0, pad)).reshape(-1, LANES))
    packed = jnp.concatenate(rows, axis=0)
    return jnp.pad(packed, ((0, (-packed.shape[0]) % 8), (0, 0)))


def _unpack_small(packed, shapes):
    out, r0 = {}, 0
    for n in SMALL:
        size = math.prod(shapes[n])
        nr = -(-size // LANES)
        out[n] = packed[r0:r0 + nr].reshape(-1)[:size].reshape(shapes[n])
        r0 += nr
    return out


def kernel(x, ffn1_w_gate, ffn1_w_up, ffn1_w_down, ln1_g, ln1_b, w_in, b_forget, conv_w, conv_b, rg_wa, rg_ba, rg_wx, rg_bx, lru_lambda, w_out, ln2_g, ln2_b, ffn2_w_gate, ffn2_w_up, ffn2_w_down, ln3_g, ln3_b, loss_target, m_ffn1_w_gate, m_ffn1_w_up, m_ffn1_w_down, m_ln1_g, m_ln1_b, m_w_in, m_b_forget, m_conv_w, m_conv_b, m_rg_wa, m_rg_ba, m_rg_wx, m_rg_bx, m_lru_lambda, m_w_out, m_ln2_g, m_ln2_b, m_ffn2_w_gate, m_ffn2_w_up, m_ffn2_w_down, m_ln3_g, m_ln3_b, v_ffn1_w_gate, v_ffn1_w_up, v_ffn1_w_down, v_ln1_g, v_ln1_b, v_w_in, v_b_forget, v_conv_w, v_conv_b, v_rg_wa, v_rg_ba, v_rg_wx, v_rg_bx, v_lru_lambda, v_w_out, v_ln2_g, v_ln2_b, v_ffn2_w_gate, v_ffn2_w_up, v_ffn2_w_down, v_ln3_g, v_ln3_b):
    args = dict(locals())
    w = {n: args[n] for n in WEIGHTS}
    mom = {n: args["m_" + n] for n in WEIGHTS}
    var = {n: args["v_" + n] for n in WEIGHTS}
    chip = 2 * lax.axis_index("x") + lax.axis_index("y")

    g1 = _all_gather_bf16([w[n][0] for n in BIG[:3]] + [w["w_in"][0], w["w_out"][0]], name="ag_first")
    fs = D_FF // N_SHARD
    w_in_full = g1[3].reshape(N_SHARD, D_MODEL, IN_SHARD).transpose(1, 0, 2).reshape(D_MODEL, IN_COLS)
    full = dict(
        f1g=g1[0].reshape(N_SHARD, D_MODEL, fs), f1u=g1[1].reshape(N_SHARD, D_MODEL, fs),
        f1d=g1[2].reshape(N_SHARD, fs, D_MODEL),
        wp=make_wp(w_in_full), bfp=jnp.pad(b_forget, ((0, 0), (0, LANES - HEADS))), wo=g1[4],
        ln1_g=ln1_g, ln1_b=ln1_b, ln2_g=ln2_g, ln2_b=ln2_b, ln3_g=ln3_g, ln3_b=ln3_b,
        conv_b=conv_b, rg_wa=rg_wa[0], rg_wx=rg_wx[0], rg_ba=rg_ba[0], rg_bx=rg_bx[0], lam=lru_lambda,
    )
    cw_place = lax.dynamic_update_slice(jnp.zeros((8, LRU_W), f32), conv_w[0] * 0.5, (0, chip * (LRU_W // N_SHARD)))
    cw_full = _all_reduce_small(cw_place.reshape(-1, LANES), g1[0], name="ag_conv_w")
    full["conv_w"] = cw_full.reshape(8, LRU_W)[:CONV_K]

    hooks = _Overlap([w[n][0] for n in BIG[3:]], cw_full)
    loss_rep, dx, g = _local_step(x[0], loss_target[0], full, hooks)
    loss = lax.psum(loss_rep[0, 0], ("x", "y", "c"))

    gwin = g["wp"][:, :IN_COLS].reshape(D_MODEL, N_SHARD, IN_SHARD).transpose(1, 0, 2).astype(GRAD_DTYPE)
    gwo = g["wo"].reshape(N_SHARD, D_MODEL // N_SHARD, D_MODEL).astype(GRAD_DTYPE)
    sums = []
    for tag, gs in (("ffn1", [g["f1g"], g["f1u"], g["f1d"]]), ("w_in", [gwin]), ("w_out", [gwo])):
        recvs = _swap_halves(gs, name=f"rs_swap_{tag}")
        ps = _add_halves(gs, recvs, name=f"rs_add_{tag}")
        qs = _scatter_partials(ps, name=f"rs_scatter_{tag}")
        sums += _sum_slabs(ps, qs, name=f"rs_sum_{tag}")
    red = _join_halves(sums[:3] + hooks.reduced + sums[3:], name="rs_join")
    grads = dict(zip(BIG + ["w_in", "w_out"], red))

    small_shapes = {n: w[n].shape for n in SMALL}
    small_shapes["conv_w"] = (1, CONV_K, LRU_W)
    gsmall = dict(ln1_g=g["ln1_g"], ln1_b=g["ln1_b"], ln2_g=g["ln2_g"], ln2_b=g["ln2_b"], ln3_g=g["ln3_g"],
                  ln3_b=g["ln3_b"], b_forget=g["b_forget"], conv_w=g["conv_w"], conv_b=g["conv_b"],
                  rg_wa=g["rg_wa"], rg_wx=g["rg_wx"], rg_ba=g["rg_ba"], rg_bx=g["rg_bx"], lru_lambda=g["lam"])
    gs_red = _unpack_small(_all_reduce_small(_pack_small(gsmall), name="ar_small"), small_shapes)
    gs_red["conv_w"] = lax.dynamic_slice(gs_red["conv_w"], (0, 0, chip * (LRU_W // N_SHARD)),
                                         (1, CONV_K, LRU_W // N_SHARD))
    grads.update(gs_red)

    delta, new_m, new_v = {}, {}, {}
    d, nm, nv = _adamw([grads[n] for n in BIG], [w[n][0] for n in BIG], [mom[n][0] for n in BIG],
                       [var[n][0] for n in BIG], name="adamw_ffn", tm=64)
    for i, n in enumerate(BIG):
        delta[n], new_m[n], new_v[n] = d[i], nm[i], nv[i]
    for n in ("w_in", "w_out"):
        d, nm, nv = _adamw([grads[n]], [w[n][0]], [mom[n][0]], [var[n][0]], name="adamw_" + n)
        delta[n], new_m[n], new_v[n] = d[0], nm[0], nv[0]
    shard_shapes = {n: w[n].shape for n in SMALL}
    d, nm, nv = _adamw([_pack_small({n: grads[n] for n in SMALL})], [_pack_small({n: w[n] for n in SMALL})],
                       [_pack_small({n: mom[n] for n in SMALL})], [_pack_small({n: var[n] for n in SMALL})],
                       name="adamw_small", tm=None)
    for dst, packed in ((delta, d[0]), (new_m, nm[0]), (new_v, nv[0])):
        dst.update(_unpack_small(packed, shard_shapes))

    def shaped(tree, n):
        return tree[n].reshape(w[n].shape)

    return (loss, dx[None], *[shaped(grads, n) for n in WEIGHTS], *[shaped(delta, n) for n in WEIGHTS],
            *[shaped(new_m, n) for n in WEIGHTS], *[shaped(new_v, n) for n in WEIGHTS])
```

```python
import functools
import math

import jax
import jax.numpy as jnp
from jax import lax
from jax.experimental import pallas as pl
from jax.experimental.pallas import tpu as pltpu

f32 = jnp.float32
MXU_DTYPE = jnp.bfloat16
GRAD_DTYPE = jnp.bfloat16

D_MODEL = 1024
D_FF = 4096
N_SHARD = 4
N_DEV = 8
FOX_W = 512
LRU_W = 512
HEADS = 8
HEAD_DIM = 64
CONV_K = 4
IN_COLS = 2568
IN_SHARD = IN_COLS // N_SHARD
QKV_W = 3 * FOX_W
Z_PAD = 2688
LANES = 128
LN_EPS = 1e-5
DN_ALPHA = 2.0 ** 0.25
LRU_C = 8.0
NEG_BIG = -1e30
VMEM_LIMIT = 56 * 1024 * 1024

ADAM_LR = 0.001
ADAM_B1 = 0.9
ADAM_B2 = 0.999
ADAM_EPS = 1e-08
ADAM_WD = 0.01
ADAM_STEP = 10


def _pcall(body, **kw):
    return pl.pallas_call(body, **kw)


def _params(n_grid, vmem=VMEM_LIMIT):
    return pltpu.CompilerParams(dimension_semantics=("arbitrary",) * n_grid, vmem_limit_bytes=vmem)


def _dot(a, b):
    return jnp.dot(a, b, preferred_element_type=f32)


def _dot_nt(a, b):
    return lax.dot_general(a, b, (((1,), (1,)), ((), ())), preferred_element_type=f32)


def _dot_tn(a, b):
    return lax.dot_general(a, b, (((0,), (0,)), ((), ())), preferred_element_type=f32)


def _sigmoid(x):
    return 1.0 / (1.0 + jnp.exp(-x))


def _layer_norm_stats(y):
    mu = jnp.mean(y, axis=-1, keepdims=True)
    yc = y - mu
    var = jnp.mean(yc * yc, axis=-1, keepdims=True)
    rstd = lax.rsqrt(var + LN_EPS)
    return yc * rstd, rstd


def _ln_backward(dy, xhat, rstd, gamma):
    dxhat = dy * gamma
    m1 = jnp.mean(dxhat, axis=-1, keepdims=True)
    m2 = jnp.mean(dxhat * xhat, axis=-1, keepdims=True)
    dyp = rstd * (dxhat - m1 - xhat * m2)
    return dyp, jnp.sum(dy * xhat, axis=0, keepdims=True), jnp.sum(dy, axis=0, keepdims=True)


def _ffn_fwd(x, wg, wu, wd, ln_g, ln_b, *, name, tm=1024, tf=512):
    T = x.shape[0]
    tm = min(tm, T)
    fs = D_FF // N_SHARD
    cpf = fs // tf
    nf = D_FF // tf
    nt = T // tm

    def body(x_ref, wg_ref, wu_ref, wd_ref, g_ref, b_ref,
             xb_ref, gact_ref, uact_ref, xhat_ref, xn_ref, rstd_ref, acc_ref):
        f = pl.program_id(1)

        @pl.when(f == 0)
        def _():
            xb_ref[...] = x_ref[...].astype(MXU_DTYPE)
            acc_ref[...] = jnp.zeros_like(acc_ref)

        xb = xb_ref[...]
        g = _dot(xb, wg_ref[...])
        u = _dot(xb, wu_ref[...])
        h = (g * _sigmoid(g)) * u
        gact_ref[...] = g.astype(gact_ref.dtype)
        uact_ref[...] = u.astype(uact_ref.dtype)
        acc_ref[...] += _dot(h.astype(MXU_DTYPE), wd_ref[...])

        @pl.when(f == nf - 1)
        def _():
            y = DN_ALPHA * x_ref[...] + 0.5 * acc_ref[...]
            xhat, rstd = _layer_norm_stats(y)
            xhat_ref[...] = xhat
            xn_ref[...] = (xhat * g_ref[...] + b_ref[...]).astype(xn_ref.dtype)
            rstd_ref[...] = jnp.broadcast_to(rstd, rstd_ref.shape)

    row = lambda i, f: (i, 0)
    return _pcall(
        body, name=name, grid=(nt, nf),
        in_specs=[
            pl.BlockSpec((tm, D_MODEL), row),
            pl.BlockSpec((None, D_MODEL, tf), lambda i, f: (f // cpf, 0, f % cpf)),
            pl.BlockSpec((None, D_MODEL, tf), lambda i, f: (f // cpf, 0, f % cpf)),
            pl.BlockSpec((None, tf, D_MODEL), lambda i, f: (f // cpf, f % cpf, 0)),
            pl.BlockSpec((1, D_MODEL), lambda i, f: (0, 0)),
            pl.BlockSpec((1, D_MODEL), lambda i, f: (0, 0)),
        ],
        out_specs=[
            pl.BlockSpec((tm, D_MODEL), row),
            pl.BlockSpec((tm, tf), lambda i, f: (i, f)),
            pl.BlockSpec((tm, tf), lambda i, f: (i, f)),
            pl.BlockSpec((tm, D_MODEL), row),
            pl.BlockSpec((tm, D_MODEL), row),
            pl.BlockSpec((tm, LANES), row),
        ],
        out_shape=[
            jax.ShapeDtypeStruct((T, D_MODEL), MXU_DTYPE),
            jax.ShapeDtypeStruct((T, D_FF), MXU_DTYPE),
            jax.ShapeDtypeStruct((T, D_FF), MXU_DTYPE),
            jax.ShapeDtypeStruct((T, D_MODEL), f32),
            jax.ShapeDtypeStruct((T, D_MODEL), MXU_DTYPE),
            jax.ShapeDtypeStruct((T, LANES), f32),
        ],
        scratch_shapes=[pltpu.VMEM((tm, D_MODEL), f32)],
        compiler_params=_params(2),
    )(x, wg, wu, wd, ln_g, ln_b)


def _ffn_bwd(dyp, xb, gact, uact, wg, wu, wd, *, name, tm=512, tf=512):
    T = dyp.shape[0]
    tm = min(tm, T)
    fs = D_FF // N_SHARD
    cpf = fs // tf
    nf = D_FF // tf
    nt = T // tm

    def body(dyp_ref, xb_ref, g_ref, u_ref, wg_ref, wu_ref, wd_ref,
             dx_hbm, dwg_ref, dwu_ref, dwd_ref, dx_sc, dwg_sc, dwu_sc, dwd_sc, sem):
        f = pl.program_id(0)
        i = pl.program_id(1)
        rows = pl.ds(pl.multiple_of(i * tm, tm), tm)
        dyp_t = dyp_ref[...]
        dy = (0.5 * dyp_t).astype(MXU_DTYPE)

        @pl.when(i == 0)
        def _():
            dwg_sc[...] = jnp.zeros_like(dwg_sc)
            dwu_sc[...] = jnp.zeros_like(dwu_sc)
            dwd_sc[...] = jnp.zeros_like(dwd_sc)

        @pl.when(f == 0)
        def _():
            dx_sc[rows, :] = DN_ALPHA * dyp_t

        g = g_ref[...].astype(f32)
        u = u_ref[...].astype(f32)
        sig = _sigmoid(g)
        silu = g * sig
        dh = _dot_nt(dy, wd_ref[...])
        dg = (dh * u * (sig * (1.0 + g * (1.0 - sig)))).astype(MXU_DTYPE)
        du = (dh * silu).astype(MXU_DTYPE)
        hb = (silu * u).astype(MXU_DTYPE)
        dx_sc[rows, :] += _dot_nt(dg, wg_ref[...]) + _dot_nt(du, wu_ref[...])
        xb_t = xb_ref[...]
        dwg_sc[...] += _dot_tn(xb_t, dg)
        dwu_sc[...] += _dot_tn(xb_t, du)
        dwd_sc[...] += _dot_tn(hb, dy)

        @pl.when(i == nt - 1)
        def _():
            dwg_ref[...] = dwg_sc[...].astype(dwg_ref.dtype)
            dwu_ref[...] = dwu_sc[...].astype(dwu_ref.dtype)
            dwd_ref[...] = dwd_sc[...].astype(dwd_ref.dtype)

        @pl.when(jnp.logical_and(f == nf - 1, i == nt - 1))
        def _():
            cp = pltpu.make_async_copy(dx_sc, dx_hbm, sem)
            cp.start()
            cp.wait()

    row = lambda f, i: (i, 0)
    return _pcall(
        body, name=name, grid=(nf, nt),
        in_specs=[
            pl.BlockSpec((tm, D_MODEL), row),
            pl.BlockSpec((tm, D_MODEL), row),
            pl.BlockSpec((tm, tf), lambda f, i: (i, f)),
            pl.BlockSpec((tm, tf), lambda f, i: (i, f)),
            pl.BlockSpec((None, D_MODEL, tf), lambda f, i: (f // cpf, 0, f % cpf)),
            pl.BlockSpec((None, D_MODEL, tf), lambda f, i: (f // cpf, 0, f % cpf)),
            pl.BlockSpec((None, tf, D_MODEL), lambda f, i: (f // cpf, f % cpf, 0)),
        ],
        out_specs=[
            pl.BlockSpec(memory_space=pl.ANY),
            pl.BlockSpec((None, D_MODEL, tf), lambda f, i: (f // cpf, 0, f % cpf)),
            pl.BlockSpec((None, D_MODEL, tf), lambda f, i: (f // cpf, 0, f % cpf)),
            pl.BlockSpec((None, tf, D_MODEL), lambda f, i: (f // cpf, f % cpf, 0)),
        ],
        out_shape=[
            jax.ShapeDtypeStruct((T, D_MODEL), f32),
            jax.ShapeDtypeStruct((N_SHARD, D_MODEL, fs), GRAD_DTYPE),
            jax.ShapeDtypeStruct((N_SHARD, D_MODEL, fs), GRAD_DTYPE),
            jax.ShapeDtypeStruct((N_SHARD, fs, D_MODEL), GRAD_DTYPE),
        ],
        scratch_shapes=[pltpu.VMEM((T, D_MODEL), f32), pltpu.VMEM((D_MODEL, tf), f32),
                        pltpu.VMEM((D_MODEL, tf), f32), pltpu.VMEM((tf, D_MODEL), f32),
                        pltpu.SemaphoreType.DMA],
        compiler_params=_params(2),
    )(dyp, xb, gact, uact, wg, wu, wd)


def _loss_ln_bwd(xhat, rstd, ln_g, ln_b, target, *, name, tm=512):
    T = xhat.shape[0]
    tm = min(tm, T)
    nt = T // tm

    def body(xhat_ref, rstd_ref, g_ref, b_ref, t_ref, dyp_ref, dg_ref, db_ref, loss_ref):
        i = pl.program_id(0)

        @pl.when(i == 0)
        def _():
            dg_ref[...] = jnp.zeros_like(dg_ref)
            db_ref[...] = jnp.zeros_like(db_ref)
            loss_ref[...] = jnp.zeros_like(loss_ref)

        xhat_t = xhat_ref[...]
        gamma = g_ref[...]
        err = xhat_t * gamma + b_ref[...] - t_ref[...]
        sq = jnp.sum(jnp.sum(err * err, axis=0, keepdims=True), axis=1, keepdims=True)
        loss_ref[...] += jnp.broadcast_to(sq * (0.5 / D_MODEL), loss_ref.shape)
        dy = err * (1.0 / D_MODEL)
        dyp, dgam, dbeta = _ln_backward(dy, xhat_t, rstd_ref[:, 0:1], gamma)
        dyp_ref[...] = dyp
        dg_ref[...] += dgam
        db_ref[...] += dbeta

    row = lambda i: (i, 0)
    const = lambda i: (0, 0)
    return _pcall(
        body, name=name, grid=(nt,),
        in_specs=[pl.BlockSpec((tm, D_MODEL), row), pl.BlockSpec((tm, LANES), row),
                  pl.BlockSpec((1, D_MODEL), const), pl.BlockSpec((1, D_MODEL), const),
                  pl.BlockSpec((tm, D_MODEL), row)],
        out_specs=[pl.BlockSpec((tm, D_MODEL), row), pl.BlockSpec((1, D_MODEL), const),
                   pl.BlockSpec((1, D_MODEL), const), pl.BlockSpec((1, LANES), const)],
        out_shape=[jax.ShapeDtypeStruct((T, D_MODEL), f32), jax.ShapeDtypeStruct((1, D_MODEL), f32),
                   jax.ShapeDtypeStruct((1, D_MODEL), f32), jax.ShapeDtypeStruct((1, LANES), f32)],
        compiler_params=_params(1),
    )(xhat, rstd, ln_g, ln_b, target)


def _ln_bwd(dy, xhat, rstd, ln_g, *, name, tm=512):
    T = xhat.shape[0]
    tm = min(tm, T)
    nt = T // tm

    def body(dy_ref, xhat_ref, rstd_ref, g_ref, dyp_ref, dg_ref, db_ref):
        i = pl.program_id(0)

        @pl.when(i == 0)
        def _():
            dg_ref[...] = jnp.zeros_like(dg_ref)
            db_ref[...] = jnp.zeros_like(db_ref)

        dyp, dgam, dbeta = _ln_backward(dy_ref[...], xhat_ref[...], rstd_ref[:, 0:1], g_ref[...])
        dyp_ref[...] = dyp
        dg_ref[...] += dgam
        db_ref[...] += dbeta

    row = lambda i: (i, 0)
    const = lambda i: (0, 0)
    return _pcall(
        body, name=name, grid=(nt,),
        in_specs=[pl.BlockSpec((tm, D_MODEL), row), pl.BlockSpec((tm, D_MODEL), row),
                  pl.BlockSpec((tm, LANES), row), pl.BlockSpec((1, D_MODEL), const)],
        out_specs=[pl.BlockSpec((tm, D_MODEL), row), pl.BlockSpec((1, D_MODEL), const),
                   pl.BlockSpec((1, D_MODEL), const)],
        out_shape=[jax.ShapeDtypeStruct((T, D_MODEL), f32), jax.ShapeDtypeStruct((1, D_MODEL), f32),
                   jax.ShapeDtypeStruct((1, D_MODEL), f32)],
        compiler_params=_params(1),
    )(dy, xhat, rstd, ln_g)


def _proj_in(xn, wp, bfp, *, name, tm=512):
    T = xn.shape[0]
    tm = min(tm, T)
    nt = T // tm

    def body(x_ref, w_ref, b_ref, qkv_ref, lxg_ref, fg_ref):
        z = _dot(x_ref[...], w_ref[...])
        qkv_ref[...] = z[:, :QKV_W].astype(qkv_ref.dtype)
        lxg_ref[...] = z[:, QKV_W:QKV_W + 2 * LRU_W]
        fg_ref[...] = z[:, QKV_W + 2 * LRU_W:] + b_ref[...]

    row = lambda i: (i, 0)
    const = lambda i: (0, 0)
    return _pcall(
        body, name=name, grid=(nt,),
        in_specs=[pl.BlockSpec((tm, D_MODEL), row), pl.BlockSpec((D_MODEL, Z_PAD), const),
                  pl.BlockSpec((1, LANES), const)],
        out_specs=[pl.BlockSpec((tm, QKV_W), row), pl.BlockSpec((tm, 2 * LRU_W), row),
                   pl.BlockSpec((tm, LANES), row)],
        out_shape=[jax.ShapeDtypeStruct((T, QKV_W), MXU_DTYPE), jax.ShapeDtypeStruct((T, 2 * LRU_W), f32),
                   jax.ShapeDtypeStruct((T, LANES), f32)],
        compiler_params=_params(1),
    )(xn, wp, bfp)


def _proj_in_bwd(dq, dk, dv, dlxg, dfg, xn, dyp, wp, *, name, tm=512):
    T = xn.shape[0]
    tm = min(tm, T)
    nt = T // tm

    def body(dq_ref, dk_ref, dv_ref, dl_ref, dfg_ref, x_ref, dyp_ref, w_ref, dx_ref, dw_hbm, dw_sc, sem):
        i = pl.program_id(0)

        @pl.when(i == 0)
        def _():
            dw_sc[...] = jnp.zeros_like(dw_sc)

        dz = jnp.concatenate(
            [dq_ref[...].astype(MXU_DTYPE), dk_ref[...].astype(MXU_DTYPE), dv_ref[...].astype(MXU_DTYPE),
             dl_ref[...].astype(MXU_DTYPE), dfg_ref[...].astype(MXU_DTYPE)], axis=1)
        dx_ref[...] = DN_ALPHA * dyp_ref[...] + _dot_nt(dz, w_ref[...])
        dw_sc[...] += _dot_tn(x_ref[...], dz)

        @pl.when(i == nt - 1)
        def _():
            dw_sc[:, :FOX_W] = dw_sc[:, :FOX_W] * (1.0 / math.sqrt(HEAD_DIM))
            cp = pltpu.make_async_copy(dw_sc, dw_hbm, sem)
            cp.start()
            cp.wait()

    row = lambda i: (i, 0)
    const = lambda i: (0, 0)
    return _pcall(
        body, name=name, grid=(nt,),
        in_specs=[pl.BlockSpec((tm, FOX_W), row), pl.BlockSpec((tm, FOX_W), row), pl.BlockSpec((tm, FOX_W), row),
                  pl.BlockSpec((tm, 2 * LRU_W), row), pl.BlockSpec((tm, LANES), row),
                  pl.BlockSpec((tm, D_MODEL), row), pl.BlockSpec((tm, D_MODEL), row),
                  pl.BlockSpec((D_MODEL, Z_PAD), const)],
        out_specs=[pl.BlockSpec((tm, D_MODEL), row), pl.BlockSpec(memory_space=pl.ANY)],
        out_shape=[jax.ShapeDtypeStruct((T, D_MODEL), f32), jax.ShapeDtypeStruct((D_MODEL, Z_PAD), f32)],
        scratch_shapes=[pltpu.VMEM((D_MODEL, Z_PAD), f32), pltpu.SemaphoreType.DMA],
        compiler_params=_params(1),
    )(dq, dk, dv, dlxg, dfg, xn, dyp, wp)


def _split3(x):
    hi = x.astype(jnp.bfloat16)
    r1 = x - hi.astype(f32)
    mid = r1.astype(jnp.bfloat16)
    lo = (r1 - mid.astype(f32)).astype(jnp.bfloat16)
    return hi, mid, lo


def _tri_dot(tri, x):
    hi, mid, lo = _split3(x)
    return _dot(tri, hi) + _dot(tri, mid) + _dot(tri, lo)


def _fox_prep(fgb, *, name, tm=512):
    T = fgb.shape[0]
    tm = min(tm, T)
    nt = T // tm

    def body(fg_ref, crep_ref, ct_ref, carry):
        i = pl.program_id(0)

        @pl.when(i == 0)
        def _():
            carry[...] = jnp.zeros_like(carry)

        x = fg_ref[...]
        ls = jnp.minimum(x, 0.0) - jnp.log(1.0 + jnp.exp(-jnp.abs(x)))
        r = lax.broadcasted_iota(jnp.int32, (tm, tm), 0)
        c = lax.broadcasted_iota(jnp.int32, (tm, tm), 1)
        tri = jnp.where(r >= c, 1.0, 0.0).astype(jnp.bfloat16)
        cum = _tri_dot(tri, ls) + carry[0:1, :]
        carry[...] = jnp.broadcast_to(cum[tm - 1:tm, :], carry.shape)
        for h in range(HEADS):
            crep_ref[h] = jnp.broadcast_to(cum[:, h:h + 1], (tm, LANES))
        ct_ref[...] = cum.T[:HEADS, :]

    return _pcall(
        body, name=name, grid=(nt,),
        in_specs=[pl.BlockSpec((tm, LANES), lambda i: (i, 0))],
        out_specs=[pl.BlockSpec((HEADS, tm, LANES), lambda i: (0, i, 0)),
                   pl.BlockSpec((HEADS, tm), lambda i: (0, i))],
        out_shape=[jax.ShapeDtypeStruct((HEADS, T, LANES), f32), jax.ShapeDtypeStruct((HEADS, T), f32)],
        scratch_shapes=[pltpu.VMEM((8, LANES), f32)],
        compiler_params=_params(1),
    )(fgb)


def _head_mask(shape, h):
    lane = lax.broadcasted_iota(jnp.int32, shape, 1)
    return (lane < HEAD_DIM) if h == 0 else (lane >= HEAD_DIM)


def _causal_mask(qi, ki, tq, tk):
    r = lax.broadcasted_iota(jnp.int32, (tq, tk), 0)
    c = lax.broadcasted_iota(jnp.int32, (tq, tk), 1)
    return jnp.logical_and(qi == ki, c > r)


def _fox_fwd(qkv, crep, ct, *, name, tq=512):
    T = qkv.shape[0]
    tq = min(tq, T)
    tk = tq
    nq = T // tq
    rep = tk // LANES

    def body(q_ref, k_ref, v_ref, cq_ref, ct_ref, o_ref, lse_ref, m_sc, l_sc, acc_sc):
        j = pl.program_id(0)
        qi = pl.program_id(1)
        ki = pl.program_id(2)

        @pl.when(ki == 0)
        def _():
            m_sc[...] = jnp.full_like(m_sc, NEG_BIG)
            l_sc[...] = jnp.zeros_like(l_sc)
            acc_sc[...] = jnp.zeros_like(acc_sc)

        @pl.when(ki <= qi)
        def _():
            q2 = q_ref[...]
            k2 = k_ref[...]
            v2 = v_ref[...]
            future = _causal_mask(qi, ki, tq, tk)
            for h in range(2):
                qh = jnp.where(_head_mask(q2.shape, h), q2, jnp.zeros_like(q2))
                ck = ct_ref[pl.ds(2 * j + h, 1), :]
                s = _dot_nt(qh, k2) + (jnp.tile(cq_ref[h], (1, rep)) - ck)
                s = jnp.where(future, NEG_BIG, s)
                m_prev = m_sc[h]
                m_new = jnp.maximum(m_prev, jnp.max(s, axis=1, keepdims=True))
                p = jnp.exp(s - jnp.tile(m_new, (1, rep)))
                alpha = jnp.exp(m_prev - m_new)
                l_sc[h] = alpha * l_sc[h] + jnp.sum(p, axis=1, keepdims=True)
                acc_sc[h] = alpha * acc_sc[h] + _dot(p.astype(MXU_DTYPE), v2)
                m_sc[h] = m_new

        @pl.when(ki == qi)
        def _():
            o0 = acc_sc[0] / l_sc[0]
            o1 = acc_sc[1] / l_sc[1]
            o_ref[...] = jnp.where(_head_mask(o0.shape, 0), o0, o1)
            for h in range(2):
                lse_ref[h] = m_sc[h] + jnp.log(l_sc[h])

    kv = lambda j, qi, ki: jnp.minimum(ki, qi)
    return _pcall(
        body, name=name, grid=(HEADS // 2, nq, nq),
        in_specs=[
            pl.BlockSpec((tq, LANES), lambda j, qi, ki: (qi, j)),
            pl.BlockSpec((tk, LANES), lambda j, qi, ki: (kv(j, qi, ki), 4 + j)),
            pl.BlockSpec((tk, LANES), lambda j, qi, ki: (kv(j, qi, ki), 8 + j)),
            pl.BlockSpec((2, tq, LANES), lambda j, qi, ki: (j, qi, 0)),
            pl.BlockSpec((HEADS, tk), lambda j, qi, ki: (0, kv(j, qi, ki))),
        ],
        out_specs=[pl.BlockSpec((tq, LANES), lambda j, qi, ki: (qi, j)),
                   pl.BlockSpec((2, tq, LANES), lambda j, qi, ki: (j, qi, 0))],
        out_shape=[jax.ShapeDtypeStruct((T, FOX_W), f32), jax.ShapeDtypeStruct((HEADS, T, LANES), f32)],
        scratch_shapes=[pltpu.VMEM((2, tq, LANES), f32)] * 3,
        compiler_params=_params(3),
    )(qkv, qkv, qkv, crep, ct)


def _fox_bwd_prep(do, o, *, name, tm=512):
    T = o.shape[0]
    tm = min(tm, T)
    nt = T // tm

    def body(do_ref, o_ref, d_ref):
        prod = do_ref[...].astype(f32) * o_ref[...]
        for j in range(HEADS // 2):
            pj = prod[:, j * LANES:(j + 1) * LANES]
            for h in range(2):
                dsum = jnp.sum(jnp.where(_head_mask(pj.shape, h), pj, 0.0), axis=1, keepdims=True)
                d_ref[2 * j + h] = jnp.broadcast_to(dsum, (tm, LANES))

    return _pcall(
        body, name=name, grid=(nt,),
        in_specs=[pl.BlockSpec((tm, FOX_W), lambda i: (i, 0)), pl.BlockSpec((tm, FOX_W), lambda i: (i, 0))],
        out_specs=[pl.BlockSpec((HEADS, tm, LANES), lambda i: (0, i, 0))],
        out_shape=[jax.ShapeDtypeStruct((HEADS, T, LANES), f32)],
        compiler_params=_params(1),
    )(do, o)[0]


def _fox_bwd(qkv, do, crep, ct, lse, drep, *, name, tq=512):
    T = qkv.shape[0]
    tq = min(tq, T)
    tk = tq
    nq = T // tq
    rep = tk // LANES

    def body(q_ref, k_ref, v_ref, do_ref, cq_ref, ct_ref, lse_ref, d_ref,
             dq_ref, drow_ref, dk_ref, dv_ref, dcol_ref, dk_sc, dv_sc):
        j = pl.program_id(0)
        ki = pl.program_id(1)
        qi = pl.program_id(2)
        rows = pl.ds(pl.multiple_of(qi * tq, tq), tq)

        @pl.when(qi == 0)
        def _():
            dk_sc[...] = jnp.zeros_like(dk_sc)
            dv_sc[...] = jnp.zeros_like(dv_sc)

        @pl.when(jnp.logical_and(ki == 0, qi == 0))
        def _():
            dq_ref[...] = jnp.zeros_like(dq_ref)
            drow_ref[...] = jnp.zeros_like(drow_ref)

        @pl.when(qi >= ki)
        def _():
            q2 = q_ref[...]
            k2 = k_ref[...]
            v2 = v_ref[...]
            do2 = do_ref[...]
            future = _causal_mask(qi, ki, tq, tk)
            dq_acc = jnp.zeros((tq, LANES), f32)
            drow_acc = jnp.zeros((tq, LANES), f32)
            for h in range(2):
                hm = _head_mask(q2.shape, h)
                qh = jnp.where(hm, q2, jnp.zeros_like(q2))
                ck = ct_ref[pl.ds(2 * j + h, 1), :]
                s = _dot_nt(qh, k2) + (jnp.tile(cq_ref[h], (1, rep)) - ck)
                p = jnp.exp(s - jnp.tile(lse_ref[h], (1, rep)))
                p = jnp.where(future, 0.0, p)
                doh = jnp.where(hm, do2, jnp.zeros_like(do2))
                dp = _dot_nt(doh, v2)
                ds = (p * (dp - jnp.tile(d_ref[h], (1, rep)))).astype(MXU_DTYPE)
                dv_sc[h] += _dot_tn(p.astype(MXU_DTYPE), do2)
                q_ones = jnp.where(hm, q2, jnp.ones_like(q2))
                dk_sc[h] += _dot_tn(ds, q_ones)
                dq_full = _dot(ds, jnp.where(hm, k2, jnp.ones_like(k2)))
                dq_acc = dq_acc + jnp.where(hm, dq_full, 0.0)
                drow_acc = drow_acc + jnp.where(hm, 0.0, dq_full)
            dq_ref[rows, :] += dq_acc
            drow_ref[rows, :] += drow_acc

        @pl.when(qi == nq - 1)
        def _():
            hm0 = _head_mask((tk, LANES), 0)
            dk_ref[...] = jnp.where(hm0, dk_sc[0], dk_sc[1])
            dcol_ref[...] = jnp.where(hm0, dk_sc[1], dk_sc[0])
            dv_ref[...] = jnp.where(hm0, dv_sc[0], dv_sc[1])

    qb = lambda j, ki, qi: jnp.maximum(qi, ki)
    return _pcall(
        body, name=name, grid=(HEADS // 2, nq, nq),
        in_specs=[
            pl.BlockSpec((tq, LANES), lambda j, ki, qi: (qb(j, ki, qi), j)),
            pl.BlockSpec((tk, LANES), lambda j, ki, qi: (ki, 4 + j)),
            pl.BlockSpec((tk, LANES), lambda j, ki, qi: (ki, 8 + j)),
            pl.BlockSpec((tq, LANES), lambda j, ki, qi: (qb(j, ki, qi), j)),
            pl.BlockSpec((2, tq, LANES), lambda j, ki, qi: (j, qb(j, ki, qi), 0)),
            pl.BlockSpec((HEADS, tk), lambda j, ki, qi: (0, ki)),
            pl.BlockSpec((2, tq, LANES), lambda j, ki, qi: (j, qb(j, ki, qi), 0)),
            pl.BlockSpec((2, tq, LANES), lambda j, ki, qi: (j, qb(j, ki, qi), 0)),
        ],
        out_specs=[
            pl.BlockSpec((T, LANES), lambda j, ki, qi: (0, j)),
            pl.BlockSpec((T, LANES), lambda j, ki, qi: (0, j)),
            pl.BlockSpec((tk, LANES), lambda j, ki, qi: (ki, j)),
            pl.BlockSpec((tk, LANES), lambda j, ki, qi: (ki, j)),
            pl.BlockSpec((tk, LANES), lambda j, ki, qi: (ki, j)),
        ],
        out_shape=[jax.ShapeDtypeStruct((T, FOX_W), f32)] * 5,
        scratch_shapes=[pltpu.VMEM((2, tk, LANES), f32)] * 2,
        compiler_params=_params(3),
    )(qkv, qkv, qkv, do, crep, ct, lse, drep)


def _fox_bwd_post(drow, dcol, fgb, *, name, tm=512):
    T = fgb.shape[0]
    tm = min(tm, T)
    nt = T // tm

    def body(drow_ref, dcol_ref, fg_ref, dfg_ref, dbf_ref, carry):
        i = pl.program_id(0)

        @pl.when(i == 0)
        def _():
            carry[...] = jnp.zeros_like(carry)
            dbf_ref[...] = jnp.zeros_like(dbf_ref)

        dcol_t = drow_ref[...] - dcol_ref[...]
        lane = lax.broadcasted_iota(jnp.int32, (tm, LANES), 1)
        dc = jnp.zeros((tm, LANES), f32)
        for h in range(HEADS):
            src = (h // 2) * LANES + (HEAD_DIM if h % 2 == 0 else 0)
            dc = jnp.where(lane == h, jnp.broadcast_to(dcol_t[:, src:src + 1], (tm, LANES)), dc)
        r = lax.broadcasted_iota(jnp.int32, (tm, tm), 0)
        c = lax.broadcasted_iota(jnp.int32, (tm, tm), 1)
        tri = jnp.where(c >= r, 1.0, 0.0).astype(jnp.bfloat16)
        dls = _tri_dot(tri, dc) + carry[0:1, :]
        carry[...] = jnp.broadcast_to(dls[0:1, :], carry.shape)
        dfg = dls * _sigmoid(-fg_ref[...])
        dfg_ref[...] = dfg
        dbf_ref[...] += jnp.sum(dfg, axis=0, keepdims=True)

    rev = lambda i: (nt - 1 - i, 0)
    return _pcall(
        body, name=name, grid=(nt,),
        in_specs=[pl.BlockSpec((tm, FOX_W), rev), pl.BlockSpec((tm, FOX_W), rev), pl.BlockSpec((tm, LANES), rev)],
        out_specs=[pl.BlockSpec((tm, LANES), rev), pl.BlockSpec((1, LANES), lambda i: (0, 0))],
        out_shape=[jax.ShapeDtypeStruct((T, LANES), f32), jax.ShapeDtypeStruct((1, LANES), f32)],
        scratch_shapes=[pltpu.VMEM((8, LANES), f32)],
        compiler_params=_params(1),
    )(drow, dcol, fgb)


GELU_C = math.sqrt(2.0 / math.pi)
GELU_A = 0.044715


def _gelu(x):
    t = jnp.tanh(GELU_C * (x + GELU_A * x * x * x))
    return 0.5 * x * (1.0 + t), t


def _gelu_grad(x, t):
    return 0.5 * (1.0 + t) + 0.5 * x * (1.0 - t * t) * GELU_C * (1.0 + 3.0 * GELU_A * x * x)


def _expm1(x):
    e = jnp.exp(x)
    safe = jnp.where(e == 1.0, x, (e - 1.0) * x / jnp.log(jnp.where(e == 1.0, 0.5, e)))
    return jnp.where(x < -0.5, e - 1.0, safe)


def _lru_gates(u, wab_ref, bab_ref, lam_ref):
    pre = _dot(u.astype(MXU_DTYPE), wab_ref[...]) + bab_ref[...]
    r = _sigmoid(pre[:, :LRU_W])
    gi = _sigmoid(pre[:, LRU_W:])
    lam = lam_ref[...]
    sp = jnp.maximum(-lam, 0.0) + jnp.log(1.0 + jnp.exp(-jnp.abs(lam)))
    log_a = -LRU_C * r * sp
    a = jnp.exp(log_a)
    s = jnp.sqrt(-_expm1(2.0 * log_a))
    return r, gi, sp, a, s


def _lru_fwd(lxg, conv_w, conv_b, wab, bab, lam, *, name, tc=512):
    T = lxg.shape[0]
    tc = min(tc, T)
    nc = T // tc

    def body(lx_ref, lg_ref, cw_ref, cb_ref, wab_ref, bab_ref, lam_ref,
             out_ref, u_ref, hs_ref, ext, a_sc, b_sc, h_sc):
        i = pl.program_id(0)

        @pl.when(i == 0)
        def _():
            ext[0:8, :] = jnp.zeros((8, LRU_W), f32)
            h_sc[...] = jnp.zeros_like(h_sc)

        ext[8:, :] = lx_ref[...]
        u = cb_ref[...] + cw_ref[0:1, :] * ext[pl.ds(5, tc), :]
        for k in range(1, CONV_K):
            u = u + cw_ref[k:k + 1, :] * ext[pl.ds(5 + k, tc), :]
        ext[0:8, :] = ext[tc:tc + 8, :]
        u_ref[...] = u
        r, gi, sp, a, s = _lru_gates(u, wab_ref, bab_ref, lam_ref)
        a_sc[...] = a
        b_sc[...] = s * (gi * u)

        def step(t, h):
            h = a_sc[pl.ds(t, 1), :] * h + b_sc[pl.ds(t, 1), :]
            hs_ref[pl.ds(t, 1), :] = h
            return h

        h = lax.fori_loop(0, tc, step, h_sc[0:1, :], unroll=8)
        h_sc[...] = jnp.broadcast_to(h, h_sc.shape)
        gel, _ = _gelu(lg_ref[...])
        out_ref[...] = gel * hs_ref[...]

    row = lambda i: (i, 0)
    const = lambda i: (0, 0)
    return _pcall(
        body, name=name, grid=(nc,),
        in_specs=[pl.BlockSpec((tc, LRU_W), row), pl.BlockSpec((tc, LRU_W), lambda i: (i, 1)),
                  pl.BlockSpec((CONV_K, LRU_W), const), pl.BlockSpec((1, LRU_W), const),
                  pl.BlockSpec((LRU_W, 2 * LRU_W), const), pl.BlockSpec((1, 2 * LRU_W), const),
                  pl.BlockSpec((1, LRU_W), const)],
        out_specs=[pl.BlockSpec((tc, LRU_W), row)] * 3,
        out_shape=[jax.ShapeDtypeStruct((T, LRU_W), f32)] * 3,
        scratch_shapes=[pltpu.VMEM((tc + 8, LRU_W), f32), pltpu.VMEM((tc, LRU_W), f32),
                        pltpu.VMEM((tc, LRU_W), f32), pltpu.VMEM((8, LRU_W), f32)],
        compiler_params=_params(1),
    )(lxg, lxg, conv_w, conv_b, wab, bab, lam)


def _lru_bwd(dlru, lxg, u, hs, conv_w, wab, bab, lam, *, name, tc=512):
    T = lxg.shape[0]
    tc = min(tc, T)
    nc = T // tc
    bp = tc // 8

    def body(dl_ref, lx_ref, lxp_ref, lg_ref, u_ref, hs_ref, hsp_ref, cw_ref, wab_ref, bab_ref, lam_ref,
             dlxg_ref, dwab_ref, dbab_ref, dcw_ref, dcb_ref, dlam_ref,
             dh_sc, a_sc, ext, du_ext, carry):
        i = pl.program_id(0)
        first_chunk = i == nc - 1

        @pl.when(i == 0)
        def _():
            dwab_ref[...] = jnp.zeros_like(dwab_ref)
            dbab_ref[...] = jnp.zeros_like(dbab_ref)
            dcw_ref[...] = jnp.zeros_like(dcw_ref)
            dcb_ref[...] = jnp.zeros_like(dcb_ref)
            dlam_ref[...] = jnp.zeros_like(dlam_ref)
            carry[...] = jnp.zeros_like(carry)
            du_ext[tc:tc + 8, :] = jnp.zeros((8, LRU_W), f32)

        lg = lg_ref[...]
        gel, th = _gelu(lg)
        dl = dl_ref[...]
        hs = hs_ref[...]
        dlg = dl * hs * _gelu_grad(lg, th)
        u = u_ref[...]
        r, gi, sp, a, s = _lru_gates(u, wab_ref, bab_ref, lam_ref)
        a_sc[...] = a
        dh_sc[...] = dl * gel

        def step(k, c):
            t = tc - 1 - k
            dh = dh_sc[pl.ds(t, 1), :] + c
            dh_sc[pl.ds(t, 1), :] = dh
            return a_sc[pl.ds(t, 1), :] * dh

        c = lax.fori_loop(0, tc, step, carry[0:1, :], unroll=8)
        carry[...] = jnp.broadcast_to(c, carry.shape)

        ext[0:8, :] = jnp.where(first_chunk, 0.0, hsp_ref[...])
        ext[8:, :] = hs
        hprev = ext[pl.ds(7, tc), :]
        dh = dh_sc[...]
        da = dh * hprev
        giu = gi * u
        dla = da * a - (dh * giu) * (a * a / s)
        dgi = dh * s * u
        du = dh * s * gi
        dr = dla * (-LRU_C * sp)
        dlam_ref[...] += jnp.sum(dla * (-LRU_C * r), axis=0, keepdims=True) * (-_sigmoid(-lam_ref[...]))
        dpre = jnp.concatenate([dr * r * (1.0 - r), dgi * gi * (1.0 - gi)], axis=1)
        dpre_b = dpre.astype(MXU_DTYPE)
        du = du + _dot_nt(dpre_b, wab_ref[...])
        dwab_ref[...] += _dot_tn(u.astype(MXU_DTYPE), dpre_b)
        dbab_ref[...] += jnp.sum(dpre, axis=0, keepdims=True)
        dcb_ref[...] += jnp.sum(du, axis=0, keepdims=True)

        du_ext[0:tc, :] = du
        dlx = cw_ref[0:1, :] * du_ext[pl.ds(3, tc), :]
        for k in range(1, CONV_K):
            dlx = dlx + cw_ref[k:k + 1, :] * du_ext[pl.ds(3 - k, tc), :]
        du_ext[tc:tc + 8, :] = du_ext[0:8, :]
        ext[0:8, :] = jnp.where(first_chunk, 0.0, lxp_ref[...])
        ext[8:, :] = lx_ref[...]
        for k in range(CONV_K):
            dcw_ref[k:k + 1, :] += jnp.sum(du * ext[pl.ds(5 + k, tc), :], axis=0, keepdims=True)
        dlxg_ref[:, :LRU_W] = dlx.astype(dlxg_ref.dtype)
        dlxg_ref[:, LRU_W:] = dlg.astype(dlxg_ref.dtype)

    rev = lambda i: (nc - 1 - i, 0)
    prev8 = lambda i: (jnp.maximum((nc - 1 - i) * bp - 1, 0), 0)
    const = lambda i: (0, 0)
    return _pcall(
        body, name=name, grid=(nc,),
        in_specs=[
            pl.BlockSpec((tc, LRU_W), rev),
            pl.BlockSpec((tc, LRU_W), rev),
            pl.BlockSpec((8, LRU_W), prev8),
            pl.BlockSpec((tc, LRU_W), lambda i: (nc - 1 - i, 1)),
            pl.BlockSpec((tc, LRU_W), rev),
            pl.BlockSpec((tc, LRU_W), rev),
            pl.BlockSpec((8, LRU_W), prev8),
            pl.BlockSpec((CONV_K, LRU_W), const),
            pl.BlockSpec((LRU_W, 2 * LRU_W), const),
            pl.BlockSpec((1, 2 * LRU_W), const),
            pl.BlockSpec((1, LRU_W), const),
        ],
        out_specs=[
            pl.BlockSpec((tc, 2 * LRU_W), rev),
            pl.BlockSpec((LRU_W, 2 * LRU_W), const),
            pl.BlockSpec((1, 2 * LRU_W), const),
            pl.BlockSpec((8, LRU_W), const),
            pl.BlockSpec((1, LRU_W), const),
            pl.BlockSpec((1, LRU_W), const),
        ],
        out_shape=[
            jax.ShapeDtypeStruct((T, 2 * LRU_W), MXU_DTYPE),
            jax.ShapeDtypeStruct((LRU_W, 2 * LRU_W), f32),
            jax.ShapeDtypeStruct((1, 2 * LRU_W), f32),
            jax.ShapeDtypeStruct((8, LRU_W), f32),
            jax.ShapeDtypeStruct((1, LRU_W), f32),
            jax.ShapeDtypeStruct((1, LRU_W), f32),
        ],
        scratch_shapes=[pltpu.VMEM((tc, LRU_W), f32), pltpu.VMEM((tc, LRU_W), f32),
                        pltpu.VMEM((tc + 8, LRU_W), f32), pltpu.VMEM((tc + 8, LRU_W), f32),
                        pltpu.VMEM((8, LRU_W), f32)],
        compiler_params=_params(1),
    )(dlru, lxg, lxg, lxg, u, hs, hs, conv_w, wab, bab, lam)


def _mix_out(fox, lru, wo, xhat1, g1, b1, g2, b2, *, name, tm=512):
    T = fox.shape[0]
    tm = min(tm, T)
    nt = T // tm

    def body(fox_ref, lru_ref, wo_ref, xh_ref, g1_ref, b1_ref, g2_ref, b2_ref, xhat_ref, xn_ref, rstd_ref):
        mix = _dot(fox_ref[...].astype(MXU_DTYPE), wo_ref[:FOX_W, :])
        mix = mix + _dot(lru_ref[...].astype(MXU_DTYPE), wo_ref[FOX_W:, :])
        x1 = xh_ref[...] * g1_ref[...] + b1_ref[...]
        xhat, rstd = _layer_norm_stats(DN_ALPHA * x1 + mix)
        xhat_ref[...] = xhat
        xn_ref[...] = xhat * g2_ref[...] + b2_ref[...]
        rstd_ref[...] = jnp.broadcast_to(rstd, rstd_ref.shape)

    row = lambda i: (i, 0)
    const = lambda i: (0, 0)
    vec = pl.BlockSpec((1, D_MODEL), const)
    return _pcall(
        body, name=name, grid=(nt,),
        in_specs=[pl.BlockSpec((tm, FOX_W), row), pl.BlockSpec((tm, LRU_W), row),
                  pl.BlockSpec((D_MODEL, D_MODEL), const), pl.BlockSpec((tm, D_MODEL), row), vec, vec, vec, vec],
        out_specs=[pl.BlockSpec((tm, D_MODEL), row), pl.BlockSpec((tm, D_MODEL), row),
                   pl.BlockSpec((tm, LANES), row)],
        out_shape=[jax.ShapeDtypeStruct((T, D_MODEL), f32), jax.ShapeDtypeStruct((T, D_MODEL), f32),
                   jax.ShapeDtypeStruct((T, LANES), f32)],
        compiler_params=_params(1),
    )(fox, lru, wo, xhat1, g1, b1, g2, b2)


def _mix_out_bwd(dyp, fox, lru, wo, *, name, tm=512):
    T = fox.shape[0]
    tm = min(tm, T)
    nt = T // tm

    def body(dyp_ref, fox_ref, lru_ref, wo_ref, dfox_ref, dlru_ref, dwo_ref):
        i = pl.program_id(0)

        @pl.when(i == 0)
        def _():
            dwo_ref[...] = jnp.zeros_like(dwo_ref)

        dmix = dyp_ref[...].astype(MXU_DTYPE)
        dcat = _dot_nt(dmix, wo_ref[...])
        dfox_ref[...] = dcat[:, :FOX_W].astype(dfox_ref.dtype)
        dlru_ref[...] = dcat[:, FOX_W:]
        dwo_ref[:FOX_W, :] += _dot_tn(fox_ref[...].astype(MXU_DTYPE), dmix)
        dwo_ref[FOX_W:, :] += _dot_tn(lru_ref[...].astype(MXU_DTYPE), dmix)

    row = lambda i: (i, 0)
    const = lambda i: (0, 0)
    return _pcall(
        body, name=name, grid=(nt,),
        in_specs=[pl.BlockSpec((tm, D_MODEL), row), pl.BlockSpec((tm, FOX_W), row), pl.BlockSpec((tm, LRU_W), row),
                  pl.BlockSpec((D_MODEL, D_MODEL), const)],
        out_specs=[pl.BlockSpec((tm, FOX_W), row), pl.BlockSpec((tm, LRU_W), row),
                   pl.BlockSpec((D_MODEL, D_MODEL), const)],
        out_shape=[jax.ShapeDtypeStruct((T, FOX_W), MXU_DTYPE), jax.ShapeDtypeStruct((T, LRU_W), f32),
                   jax.ShapeDtypeStruct((D_MODEL, D_MODEL), f32)],
        compiler_params=_params(1),
    )(dyp, fox, lru, wo)


def make_wp(w_in):
    scale = jnp.concatenate([jnp.full((FOX_W,), 1.0 / math.sqrt(HEAD_DIM), w_in.dtype),
                             jnp.ones((IN_COLS - FOX_W,), w_in.dtype)])
    return jnp.pad(w_in * scale[None, :], ((0, 0), (0, Z_PAD - IN_COLS)))


def _block_diag(w):
    eye = jnp.eye(HEADS, dtype=w.dtype)
    return jnp.einsum("hij,hg->higj", w, eye).reshape(LRU_W, LRU_W)


def _block_diag_extract(m):
    m4 = m.reshape(HEADS, HEAD_DIM, HEADS, HEAD_DIM)
    return jnp.stack([m4[h, :, h, :] for h in range(HEADS)])


class _NoOverlap:
    def start_token(self):
        return None

    def after_attention(self, after):
        return None

    def ffn2_weights(self, w, after):
        return w["f2g"], w["f2u"], w["f2d"]

    def ffn2_grads(self, grads):
        return None

    def mixer_grads(self, dwp, dwo):
        return None

    def before_ffn1_bwd(self, after):
        return None


def _tied(a, token):
    return a if token is None else a + token[0, 0]


def _local_step(x, target, w, hooks=None):
    hooks = hooks or _NoOverlap()
    wp = w["wp"]
    bfp = w["bfp"]
    wab = jnp.concatenate([_block_diag(w["rg_wa"]), _block_diag(w["rg_wx"])], axis=1).astype(MXU_DTYPE)
    bab = jnp.concatenate([w["rg_ba"].reshape(1, LRU_W), w["rg_bx"].reshape(1, LRU_W)], axis=1)

    xb0, g1a, u1a, xhat1, xn1, rstd1 = _ffn_fwd(x, w["f1g"], w["f1u"], w["f1d"], w["ln1_g"],
                                                _tied(w["ln1_b"], hooks.start_token()), name="ffn1_fwd")
    qkv, lxg, fgb = _proj_in(xn1, wp, bfp, name="proj_in")
    crep, ct = _fox_prep(fgb, name="fox_prep")
    fox, lse = _fox_fwd(qkv, crep, ct, name="fox_fwd")
    token = hooks.after_attention([lse])
    lru, uconv, hs = _lru_fwd(lxg, w["conv_w"], _tied(w["conv_b"], token), wab, bab, w["lam"], name="lru_fwd")
    xhat2, x2, rstd2 = _mix_out(fox, lru, w["wo"], xhat1, w["ln1_g"], w["ln1_b"], w["ln2_g"], w["ln2_b"], name="mix_out")
    f2g, f2u, f2d = hooks.ffn2_weights(w, [rstd2])
    xb2, g2a, u2a, xhat3, _, rstd3 = _ffn_fwd(x2, f2g, f2u, f2d, w["ln3_g"], w["ln3_b"], name="ffn2_fwd")

    dy3p, dln3g, dln3b, loss = _loss_ln_bwd(xhat3, rstd3, w["ln3_g"], w["ln3_b"], target, name="loss_ln3_bwd")
    dx2, df2g, df2u, df2d = _ffn_bwd(dy3p, xb2, g2a, u2a, f2g, f2u, f2d, name="ffn2_bwd")
    token = hooks.ffn2_grads([df2g, df2u, df2d])
    dy2p, dln2g, dln2b = _ln_bwd(dx2, xhat2, rstd2, _tied(w["ln2_g"], token), name="ln2_bwd")
    dfox, dlru, dwo = _mix_out_bwd(dy2p, fox, lru, w["wo"], name="mix_out_bwd")
    dlxg, dwab, dbab, dcw, dcb, dlam = _lru_bwd(dlru, lxg, uconv, hs, w["conv_w"], wab, bab, w["lam"], name="lru_bwd")
    drep = _fox_bwd_prep(dfox, fox, name="fox_bwd_prep")
    dq, drow, dk, dv, dcol = _fox_bwd(qkv, dfox, crep, ct, lse, drep, name="fox_bwd")
    dfg, dbf = _fox_bwd_post(drow, dcol, fgb, name="fox_bwd_post")
    dx1, dwp = _proj_in_bwd(dq, dk, dv, dlxg, dfg, xn1, dy2p, wp, name="proj_in_bwd")
    token = hooks.mixer_grads(dwp, dwo)
    dy1p, dln1g, dln1b = _ln_bwd(dx1, xhat1, rstd1, _tied(w["ln1_g"], token), name="ln1_bwd")
    hooks.before_ffn1_bwd([dln1b])
    dx, df1g, df1u, df1d = _ffn_bwd(dy1p, xb0, g1a, u1a, w["f1g"], w["f1u"], w["f1d"], name="ffn1_bwd")

    grads = dict(
        f1g=df1g, f1u=df1u, f1d=df1d, f2g=df2g, f2u=df2u, f2d=df2d, wp=dwp, wo=dwo,
        ln1_g=dln1g, ln1_b=dln1b, ln2_g=dln2g, ln2_b=dln2b, ln3_g=dln3g, ln3_b=dln3b,
        b_forget=dbf[:, :HEADS], conv_w=dcw[:CONV_K], conv_b=dcb,
        rg_wa=_block_diag_extract(dwab[:, :LRU_W]), rg_wx=_block_diag_extract(dwab[:, LRU_W:]),
        rg_ba=dbab[:, :LRU_W].reshape(HEADS, HEAD_DIM), rg_bx=dbab[:, LRU_W:].reshape(HEADS, HEAD_DIM),
        lam=dlam,
    )
    return loss, dx, grads


MESH = pl.DeviceIdType.MESH
HBM_SPEC = pl.BlockSpec(memory_space=pl.ANY)
VMEM_SPEC = pl.BlockSpec(memory_space=pltpu.VMEM)


def _position():
    return lax.axis_index("x"), lax.axis_index("y"), lax.axis_index("c")


def _other_chips(x, y):
    return [(1 - x, y), (x, 1 - y), (1 - x, 1 - y)]


def _all_gather_bf16(shards, *, name):
    n = len(shards)

    def body(*refs):
        ins, outs, stages = refs[:n], refs[n:2 * n], refs[2 * n:3 * n]
        send_sems, recv_sems, local_sems = refs[3 * n:]
        x, y, c = _position()
        me, sibling = (x, y, c), (x, y, 1 - c)
        chips = _other_chips(x, y)

        def rows(k, px, py, pc):
            r = shards[k].shape[0]
            m = r // 2
            return outs[k].at[pl.ds(pl.multiple_of((2 * px + py) * r + pc * m, 16), m), :]

        def copy(k, idx, block, to, src=None):
            return pltpu.make_async_remote_copy(
                src_ref=rows(k, *block) if src is None else src, dst_ref=rows(k, *block),
                send_sem=send_sems.at[7 * k + idx], recv_sem=recv_sems.at[7 * k + idx],
                device_id=to, device_id_type=MESH)

        started = []
        mine = []
        for k in range(n):
            m = shards[k].shape[0] // 2
            stages[k][...] = ins[k][pl.ds(pl.multiple_of(c * m, 16), m), :].astype(stages[k].dtype)
            cp = pltpu.make_async_copy(stages[k], rows(k, *me), local_sems.at[k])
            cp.start()
            mine.append(cp)
            first = [copy(k, 0, me, sibling, src=stages[k])]
            first += [copy(k, 1 + j, me, (*chip, c), src=stages[k]) for j, chip in enumerate(chips)]
            for cp in first:
                cp.start()
            started += first
        for k in range(n):
            for j, chip in enumerate(chips):
                copy(k, 1 + j, (*chip, c), me).wait_recv()
                fwd = copy(k, 4 + j, (*chip, c), sibling)
                fwd.start()
                started.append(fwd)
        for k in range(n):
            copy(k, 0, sibling, me).wait_recv()
            for j, chip in enumerate(chips):
                copy(k, 4 + j, (*chip, 1 - c), me).wait_recv()
        for cp in started:
            cp.wait_send()
        for cp in mine:
            cp.wait()

    return _pcall(
        body, name=name,
        in_specs=[VMEM_SPEC] * n, out_specs=[HBM_SPEC] * n,
        out_shape=[jax.ShapeDtypeStruct((N_SHARD * s.shape[0], s.shape[1]), MXU_DTYPE) for s in shards],
        scratch_shapes=[pltpu.VMEM((s.shape[0] // 2, s.shape[1]), MXU_DTYPE) for s in shards]
        + [pltpu.SemaphoreType.DMA((7 * n,)), pltpu.SemaphoreType.DMA((7 * n,)), pltpu.SemaphoreType.DMA((n,))],
        compiler_params=pltpu.CompilerParams(vmem_limit_bytes=VMEM_LIMIT),
    )(*shards)


def _swap_halves(gs, *, name):
    n = len(gs)

    def body(*refs):
        ins, outs = refs[:n], refs[n:2 * n]
        send_sems, recv_sems = refs[2 * n:]
        x, y, c = _position()
        cps = []
        for k in range(n):
            m = gs[k].shape[1] // 2
            src = ins[k].at[:, pl.ds(pl.multiple_of((1 - c) * m, 16), m), :]
            cp = pltpu.make_async_remote_copy(src_ref=src, dst_ref=outs[k], send_sem=send_sems.at[k],
                                              recv_sem=recv_sems.at[k], device_id=(x, y, 1 - c), device_id_type=MESH)
            cp.start()
            cps.append(cp)
        for cp in cps:
            cp.wait()

    return _pcall(
        body, name=name, in_specs=[HBM_SPEC] * n, out_specs=[HBM_SPEC] * n,
        out_shape=[jax.ShapeDtypeStruct((g.shape[0], g.shape[1] // 2, g.shape[2]), g.dtype) for g in gs],
        scratch_shapes=[pltpu.SemaphoreType.DMA((n,)), pltpu.SemaphoreType.DMA((n,))],
    )(*gs)


def _add_halves(gs, recvs, *, name, tm=256):
    n = len(gs)
    _, r, cdim = gs[0].shape
    m = r // 2
    tm = min(tm, m)
    nb = m // tm
    c_idx = lax.axis_index("c").astype(jnp.int32).reshape(1)

    def body(c_ref, *refs):
        for k in range(n):
            refs[2 * n + k][...] = (refs[k][...].astype(f32) + refs[n + k][...].astype(f32)).astype(refs[2 * n + k].dtype)

    mine = pl.BlockSpec((None, tm, cdim), lambda j, i, c_ref: (j, c_ref[0] * nb + i, 0))
    half = pl.BlockSpec((None, tm, cdim), lambda j, i, c_ref: (j, i, 0))
    return _pcall(
        body, name=name,
        grid_spec=pltpu.PrefetchScalarGridSpec(
            num_scalar_prefetch=1, grid=(N_SHARD, nb),
            in_specs=[mine] * n + [half] * n, out_specs=[half] * n),
        out_shape=[jax.ShapeDtypeStruct((N_SHARD, m, cdim), g.dtype) for g in gs],
        compiler_params=_params(2),
    )(c_idx, *gs, *recvs)


def _scatter_partials(ps, *, name):
    n = len(ps)

    def body(*refs):
        ins, outs = refs[:n], refs[n:2 * n]
        send_sems, recv_sems = refs[2 * n:]
        x, y, c = _position()
        me_chip = 2 * x + y
        cps = []
        for k in range(n):
            for j, (px, py) in enumerate(_other_chips(x, y)):
                cp = pltpu.make_async_remote_copy(
                    src_ref=ins[k].at[2 * px + py], dst_ref=outs[k].at[me_chip],
                    send_sem=send_sems.at[3 * k + j], recv_sem=recv_sems.at[3 * k + j],
                    device_id=(px, py, c), device_id_type=MESH)
                cp.start()
                cps.append(cp)
        for cp in cps:
            cp.wait()

    return _pcall(
        body, name=name, in_specs=[HBM_SPEC] * n, out_specs=[HBM_SPEC] * n,
        out_shape=[jax.ShapeDtypeStruct(p.shape, p.dtype) for p in ps],
        scratch_shapes=[pltpu.SemaphoreType.DMA((3 * n,)), pltpu.SemaphoreType.DMA((3 * n,))],
    )(*ps)


def _sum_slabs(ps, qs, *, name, tm=128):
    n = len(qs)
    _, m, cdim = qs[0].shape
    tm = min(tm, m)
    nb = m // tm
    assert m % tm == 0, (m, tm)
    where = jnp.stack([2 * lax.axis_index("x") + lax.axis_index("y"), lax.axis_index("c")]).astype(jnp.int32)

    def body(w_ref, *refs):
        for k in range(n):
            own, q1, q2, q3 = (refs[4 * k + t][...].astype(f32) for t in range(4))
            refs[4 * n + k][...] = ((own + q1) + q2) + q3

    def slab(flip):
        return pl.BlockSpec((None, tm, cdim), lambda i, w_ref: (jnp.bitwise_xor(w_ref[0], flip), i, 0))

    operands = []
    for p, q in zip(ps, qs):
        operands += [p, q, q, q]
    return _pcall(
        body, name=name,
        grid_spec=pltpu.PrefetchScalarGridSpec(
            num_scalar_prefetch=1, grid=(nb,),
            in_specs=[slab(0), slab(2), slab(1), slab(3)] * n,
            out_specs=[pl.BlockSpec((tm, cdim), lambda i, w_ref: (w_ref[1] * nb + i, 0))] * n),
        out_shape=[jax.ShapeDtypeStruct((2 * m, cdim), f32) for _ in qs],
        compiler_params=_params(1),
    )(where, *operands)


def _join_halves(fs, *, name):
    n = len(fs)

    def body(*refs):
        outs = refs[n:2 * n]
        send_sems, recv_sems = refs[2 * n:]
        x, y, c = _position()
        cps = []
        for k in range(n):
            m = fs[k].shape[0] // 2
            half = outs[k].at[pl.ds(pl.multiple_of(c * m, 8), m), :]
            cp = pltpu.make_async_remote_copy(src_ref=half, dst_ref=half, send_sem=send_sems.at[k],
                                              recv_sem=recv_sems.at[k], device_id=(x, y, 1 - c), device_id_type=MESH)
            cp.start()
            cps.append(cp)
        for cp in cps:
            cp.wait()

    return _pcall(
        body, name=name, in_specs=[HBM_SPEC] * n, out_specs=[HBM_SPEC] * n,
        out_shape=[jax.ShapeDtypeStruct(f.shape, f.dtype) for f in fs],
        input_output_aliases={k: k for k in range(n)},
        scratch_shapes=[pltpu.SemaphoreType.DMA((n,)), pltpu.SemaphoreType.DMA((n,))],
    )(*fs)


def _all_reduce_small(v, after=None, *, name):
    r = v.shape[0]
    extra = [] if after is None else [after]

    def body(v_ref, *refs):
        out_ref, buf, send_sems, recv_sems, local_sem = refs[len(extra):]
        x, y, c = _position()
        me, sibling = (x, y, c), (x, y, 1 - c)
        chips = _other_chips(x, y)

        def rows(px, py, pc):
            return buf.at[pl.ds(pl.multiple_of((4 * px + 2 * py + pc) * r, 8), r), :]

        def copy(k, block, to, src=None):
            return pltpu.make_async_remote_copy(
                src_ref=rows(*block) if src is None else src, dst_ref=rows(*block),
                send_sem=send_sems.at[k], recv_sem=recv_sems.at[k], device_id=to, device_id_type=MESH)

        mine = pltpu.make_async_copy(v_ref, rows(*me), local_sem)
        mine.start()
        first = [copy(0, me, sibling, src=v_ref)]
        first += [copy(1 + j, me, (*chip, c), src=v_ref) for j, chip in enumerate(chips)]
        for cp in first:
            cp.start()
        passed = [copy(4 + j, (*chip, c), sibling) for j, chip in enumerate(chips)]
        for j, chip in enumerate(chips):
            copy(1 + j, (*chip, c), me).wait_recv()
            passed[j].start()
        copy(0, sibling, me).wait_recv()
        for j, chip in enumerate(chips):
            copy(4 + j, (*chip, 1 - c), me).wait_recv()
        for cp in first + passed:
            cp.wait_send()
        mine.wait()
        acc = buf[0:r, :]
        for d in range(1, N_DEV):
            acc = acc + buf[d * r:(d + 1) * r, :]
        out_ref[...] = acc

    return _pcall(
        body, name=name, in_specs=[VMEM_SPEC] + [HBM_SPEC] * len(extra), out_specs=VMEM_SPEC,
        out_shape=jax.ShapeDtypeStruct((r, LANES), f32),
        scratch_shapes=[pltpu.VMEM((N_DEV * r, LANES), f32), pltpu.SemaphoreType.DMA((7,)),
                        pltpu.SemaphoreType.DMA((7,)), pltpu.SemaphoreType.DMA],
    )(v, *extra)


SEM_SPEC = pl.BlockSpec(memory_space=pltpu.SEMAPHORE)
HBM_ONLY = pl.BlockSpec(memory_space=pltpu.HBM)
EFFECT = pltpu.SideEffectType.DATAFLOW_SIDE_EFFECTING


def _split_start(bufs, copies_fn, n_sems, *, name):
    n = len(bufs)

    def body(*refs):
        send_sems, recv_sems = refs[n], refs[n + 1]
        thru = refs[n + 2:2 * n + 2]
        token = refs[2 * n + 2]
        for cp in copies_fn(thru, send_sems, recv_sems):
            cp.start()
        token[...] = jnp.zeros_like(token)

    outs = _pcall(
        body, name=name,
        out_shape=(pltpu.SemaphoreType.DMA((n_sems,)), pltpu.SemaphoreType.DMA((n_sems,)),
                   *[pltpu.HBM(b.shape, b.dtype) for b in bufs], jax.ShapeDtypeStruct((8, LANES), f32)),
        in_specs=[HBM_ONLY] * n,
        out_specs=(SEM_SPEC, SEM_SPEC, *[HBM_ONLY] * n, VMEM_SPEC),
        input_output_aliases={k: 2 + k for k in range(n)},
        compiler_params=pltpu.CompilerParams(has_side_effects=EFFECT),
    )(*[pltpu.with_memory_space_constraint(b, pltpu.HBM) for b in bufs])
    return outs[0], outs[1], list(outs[2:2 + n]), outs[2 + n]


def _split_wait(thru, send_sems, recv_sems, after, copies_fn, *, name):
    n = len(thru)

    def body(*refs):
        for cp in copies_fn(refs[:n], refs[n], refs[n + 1]):
            cp.wait_send()
            cp.wait_recv()

    return list(_pcall(
        body, name=name,
        out_shape=tuple(pltpu.HBM(b.shape, b.dtype) for b in thru),
        in_specs=[HBM_ONLY] * n + [SEM_SPEC, SEM_SPEC] + [HBM_SPEC] * len(after),
        out_specs=tuple([HBM_ONLY] * n),
        input_output_aliases={k: k for k in range(n)},
        compiler_params=pltpu.CompilerParams(has_side_effects=EFFECT),
    )(*thru, send_sems, recv_sems, *after))


def _scatter_copies(n):
    def copies(bufs, send_sems, recv_sems):
        x, y, c = _position()
        me_chip = 2 * x + y
        cps = []
        for k in range(n):
            for j, (px, py) in enumerate(_other_chips(x, y)):
                cps.append(pltpu.make_async_remote_copy(
                    src_ref=bufs[k].at[2 * px + py], dst_ref=bufs[n + k].at[me_chip],
                    send_sem=send_sems.at[3 * k + j], recv_sem=recv_sems.at[3 * k + j],
                    device_id=(px, py, c), device_id_type=MESH))
        return cps
    return copies


def _block_rows(buf, px, py, pc):
    m = buf.shape[0] // N_DEV
    return buf.at[pl.ds(pl.multiple_of((4 * px + 2 * py + pc) * m, 16), m), :]


def _gather_ici_copies(n):
    def copies(bufs, send_sems, recv_sems):
        x, y, c = _position()
        cps = []
        for k in range(n):
            rows = _block_rows(bufs[k], x, y, c)
            targets = [(x, y, 1 - c)] + [(px, py, c) for px, py in _other_chips(x, y)]
            for j, to in enumerate(targets):
                cps.append(pltpu.make_async_remote_copy(
                    src_ref=rows, dst_ref=rows, send_sem=send_sems.at[4 * k + j], recv_sem=recv_sems.at[4 * k + j],
                    device_id=to, device_id_type=MESH))
        return cps
    return copies


def _gather_d2d_copies(n):
    def copies(bufs, send_sems, recv_sems):
        x, y, c = _position()
        cps = []
        for k in range(n):
            for j, (px, py) in enumerate(_other_chips(x, y)):
                rows = _block_rows(bufs[k], px, py, c)
                cps.append(pltpu.make_async_remote_copy(
                    src_ref=rows, dst_ref=rows, send_sem=send_sems.at[3 * k + j], recv_sem=recv_sems.at[3 * k + j],
                    device_id=(x, y, 1 - c), device_id_type=MESH))
        return cps
    return copies


def _cast_halves(shards, after, *, name, tm=256):
    n = len(shards)
    r, cdim = shards[0].shape
    m = r // 2
    tm = min(tm, m)
    nb = m // tm
    assert m % tm == 0, (m, tm)
    where = jnp.stack([2 * lax.axis_index("x") + lax.axis_index("y"), lax.axis_index("c")]).astype(jnp.int32)

    def body(w_ref, *refs):
        for k in range(n):
            refs[n + 1 + k][...] = refs[k][...].astype(refs[n + 1 + k].dtype)

    return _pcall(
        body, name=name,
        grid_spec=pltpu.PrefetchScalarGridSpec(
            num_scalar_prefetch=1, grid=(nb,),
            in_specs=[pl.BlockSpec((tm, cdim), lambda i, w_ref: (w_ref[1] * nb + i, 0))] * n + [HBM_SPEC],
            out_specs=[pl.BlockSpec((tm, cdim), lambda i, w_ref: ((2 * w_ref[0] + w_ref[1]) * nb + i, 0))] * n),
        out_shape=[jax.ShapeDtypeStruct((N_SHARD * r, cdim), MXU_DTYPE) for _ in shards],
        compiler_params=_params(1),
    )(where, *shards, after)


class _Overlap(_NoOverlap):
    def __init__(self, ffn2_shards, after):
        halves = _cast_halves(ffn2_shards, after, name="ag2_cast")
        self.n = len(halves)
        self.ici = _split_start(halves, _gather_ici_copies(self.n), 4 * self.n, name="ag2_ici_start")
        self.reduced = None

    def start_token(self):
        return self.ici[3]

    def after_attention(self, after):
        send_sems, recv_sems, thru, _ = self.ici
        landed = _split_wait(thru, send_sems, recv_sems, after, _gather_ici_copies(self.n), name="ag2_ici_wait")
        self.d2d = _split_start(landed, _gather_d2d_copies(self.n), 3 * self.n, name="ag2_d2d_start")
        return self.d2d[3]

    def ffn2_weights(self, w, after):
        send_sems, recv_sems, thru, _ = self.d2d
        full = _split_wait(thru, send_sems, recv_sems, after, _gather_d2d_copies(self.n), name="ag2_d2d_wait")
        fs = D_FF // N_SHARD
        return (full[0].reshape(N_SHARD, D_MODEL, fs), full[1].reshape(N_SHARD, D_MODEL, fs),
                full[2].reshape(N_SHARD, fs, D_MODEL))

    def ffn2_grads(self, grads):
        recvs = _swap_halves(grads, name="rs_swap_ffn2")
        ps = _add_halves(grads, recvs, name="rs_add_ffn2")
        lands = [lax.empty(p.shape, p.dtype) for p in ps]
        self.scatter = _split_start(list(ps) + lands, _scatter_copies(len(ps)), 3 * len(ps), name="rs_scatter_ffn2_start")
        return self.scatter[3]

    def mixer_grads(self, dwp, dwo):
        gwin = dwp[:, :IN_COLS].reshape(D_MODEL, N_SHARD, IN_SHARD).transpose(1, 0, 2).astype(GRAD_DTYPE)
        gwo = dwo.reshape(N_SHARD, D_MODEL // N_SHARD, D_MODEL).astype(GRAD_DTYPE)
        recvs = _swap_halves([gwin, gwo], name="rs_swap_mix")
        ps = [_add_halves([g], [r], name=f"rs_add_{tag}")[0] for g, r, tag in zip([gwin, gwo], recvs, ["w_in", "w_out"])]
        lands = [lax.empty(p.shape, p.dtype) for p in ps]
        self.scatter_mix = _split_start(ps + lands, _scatter_copies(2), 6, name="rs_scatter_mix_start")
        return self.scatter_mix[3]

    def mixer_reduced(self, after):
        send_sems, recv_sems, thru, _ = self.scatter_mix
        done = _split_wait(thru, send_sems, recv_sems, after, _scatter_copies(2), name="rs_scatter_mix_wait")
        return [_sum_slabs([done[k]], [done[2 + k]], name=f"rs_sum_{tag}")[0] for k, tag in enumerate(["w_in", "w_out"])]

    def before_ffn1_bwd(self, after):
        send_sems, recv_sems, thru, _ = self.scatter
        n = len(thru) // 2
        done = _split_wait(thru, send_sems, recv_sems, after, _scatter_copies(n), name="rs_scatter_ffn2_wait")
        self.reduced = list(_sum_slabs(done[:n], done[n:], name="rs_sum_ffn2"))


def _adamw(gs, ws, ms, vs, *, name, tm=256):
    n = len(gs)
    r, cdim = gs[0].shape
    tm = r if tm is None else min(tm, r)
    assert r % tm == 0, (r, tm)
    c1 = 1.0 / (1.0 - ADAM_B1 ** ADAM_STEP)
    c2 = 1.0 / (1.0 - ADAM_B2 ** ADAM_STEP)

    def body(*refs):
        for k in range(n):
            g = refs[k][...]
            w = refs[n + k][...]
            m = ADAM_B1 * refs[2 * n + k][...] + (1.0 - ADAM_B1) * g
            v = ADAM_B2 * refs[3 * n + k][...] + (1.0 - ADAM_B2) * (g * g)
            refs[4 * n + k][...] = -ADAM_LR * ((m * c1) / (jnp.sqrt(v * c2) + ADAM_EPS) + ADAM_WD * w)
            refs[5 * n + k][...] = m
            refs[6 * n + k][...] = v

    spec = pl.BlockSpec((tm, cdim), lambda i: (i, 0))
    outs = _pcall(
        body, name=name, grid=(r // tm,), in_specs=[spec] * (4 * n), out_specs=[spec] * (3 * n),
        out_shape=[jax.ShapeDtypeStruct((r, cdim), f32)] * (3 * n),
        compiler_params=_params(1),
    )(*gs, *ws, *ms, *vs)
    return outs[:n], outs[n:2 * n], outs[2 * n:]


BIG = ["ffn1_w_gate", "ffn1_w_up", "ffn1_w_down", "ffn2_w_gate", "ffn2_w_up", "ffn2_w_down"]
SMALL = ["ln1_g", "ln1_b", "b_forget", "conv_w", "conv_b", "rg_wa", "rg_ba", "rg_wx", "rg_bx", "lru_lambda",
         "ln2_g", "ln2_b", "ln3_g", "ln3_b"]
WEIGHTS = ["ffn1_w_gate", "ffn1_w_up", "ffn1_w_down", "ln1_g", "ln1_b", "w_in", "b_forget", "conv_w", "conv_b",
           "rg_wa", "rg_ba", "rg_wx", "rg_bx", "lru_lambda", "w_out", "ln2_g", "ln2_b",
           "ffn2_w_gate", "ffn2_w_up", "ffn2_w_down", "ln3_g", "ln3_b"]


def _pack_small(parts):
    rows = []
    for n in SMALL:
        flat = parts[n].reshape(-1)
        pad = (-flat.shape[0]) % LANES
        rows.append(jnp.pad(flat, (0, pad)).reshape(-1, LANES))
    packed = jnp.concatenate(rows, axis=0)
    return jnp.pad(packed, ((0, (-packed.shape[0]) % 8), (0, 0)))


def _unpack_small(packed, shapes):
    out, r0 = {}, 0
    for n in SMALL:
        size = math.prod(shapes[n])
        nr = -(-size // LANES)
        out[n] = packed[r0:r0 + nr].reshape(-1)[:size].reshape(shapes[n])
        r0 += nr
    return out


def kernel(x, ffn1_w_gate, ffn1_w_up, ffn1_w_down, ln1_g, ln1_b, w_in, b_forget, conv_w, conv_b, rg_wa, rg_ba, rg_wx, rg_bx, lru_lambda, w_out, ln2_g, ln2_b, ffn2_w_gate, ffn2_w_up, ffn2_w_down, ln3_g, ln3_b, loss_target, m_ffn1_w_gate, m_ffn1_w_up, m_ffn1_w_down, m_ln1_g, m_ln1_b, m_w_in, m_b_forget, m_conv_w, m_conv_b, m_rg_wa, m_rg_ba, m_rg_wx, m_rg_bx, m_lru_lambda, m_w_out, m_ln2_g, m_ln2_b, m_ffn2_w_gate, m_ffn2_w_up, m_ffn2_w_down, m_ln3_g, m_ln3_b, v_ffn1_w_gate, v_ffn1_w_up, v_ffn1_w_down, v_ln1_g, v_ln1_b, v_w_in, v_b_forget, v_conv_w, v_conv_b, v_rg_wa, v_rg_ba, v_rg_wx, v_rg_bx, v_lru_lambda, v_w_out, v_ln2_g, v_ln2_b, v_ffn2_w_gate, v_ffn2_w_up, v_ffn2_w_down, v_ln3_g, v_ln3_b):
    args = dict(locals())
    w = {n: args[n] for n in WEIGHTS}
    mom = {n: args["m_" + n] for n in WEIGHTS}
    var = {n: args["v_" + n] for n in WEIGHTS}
    chip = 2 * lax.axis_index("x") + lax.axis_index("y")

    g1 = _all_gather_bf16([w[n][0] for n in BIG[:3]] + [w["w_in"][0], w["w_out"][0]], name="ag_first")
    fs = D_FF // N_SHARD
    w_in_full = g1[3].reshape(N_SHARD, D_MODEL, IN_SHARD).transpose(1, 0, 2).reshape(D_MODEL, IN_COLS)
    full = dict(
        f1g=g1[0].reshape(N_SHARD, D_MODEL, fs), f1u=g1[1].reshape(N_SHARD, D_MODEL, fs),
        f1d=g1[2].reshape(N_SHARD, fs, D_MODEL),
        wp=make_wp(w_in_full), bfp=jnp.pad(b_forget, ((0, 0), (0, LANES - HEADS))), wo=g1[4],
        ln1_g=ln1_g, ln1_b=ln1_b, ln2_g=ln2_g, ln2_b=ln2_b, ln3_g=ln3_g, ln3_b=ln3_b,
        conv_b=conv_b, rg_wa=rg_wa[0], rg_wx=rg_wx[0], rg_ba=rg_ba[0], rg_bx=rg_bx[0], lam=lru_lambda,
    )
    cw_place = lax.dynamic_update_slice(jnp.zeros((8, LRU_W), f32), conv_w[0] * 0.5, (0, chip * (LRU_W // N_SHARD)))
    cw_full = _all_reduce_small(cw_place.reshape(-1, LANES), g1[0], name="ag_conv_w")
    full["conv_w"] = cw_full.reshape(8, LRU_W)[:CONV_K]

    hooks = _Overlap([w[n][0] for n in BIG[3:]], cw_full)
    loss_rep, dx, g = _local_step(x[0], loss_target[0], full, hooks)
    loss = lax.psum(loss_rep[0, 0], ("x", "y", "c"))

    gs1 = [g["f1g"], g["f1u"], g["f1d"]]
    ps1 = _add_halves(gs1, _swap_halves(gs1, name="rs_swap_ffn1"), name="rs_add_ffn1")
    lands = [lax.empty(p.shape, p.dtype) for p in ps1]
    send1, recv1, thru1, token1 = _split_start(list(ps1) + lands, _scatter_copies(3), 9, name="rs_scatter_ffn1_start")
    red = _join_halves(hooks.reduced + hooks.mixer_reduced([token1]), name="rs_join_rest")
    grads = dict(zip(BIG[3:] + ["w_in", "w_out"], red))

    small_shapes = {n: w[n].shape for n in SMALL}
    small_shapes["conv_w"] = (1, CONV_K, LRU_W)
    gsmall = dict(ln1_g=g["ln1_g"], ln1_b=g["ln1_b"], ln2_g=g["ln2_g"], ln2_b=g["ln2_b"], ln3_g=g["ln3_g"],
                  ln3_b=g["ln3_b"], b_forget=g["b_forget"], conv_w=g["conv_w"], conv_b=g["conv_b"],
                  rg_wa=g["rg_wa"], rg_wx=g["rg_wx"], rg_ba=g["rg_ba"], rg_bx=g["rg_bx"], lru_lambda=g["lam"])
    gs_red = _unpack_small(_all_reduce_small(_pack_small(gsmall), red[0], name="ar_small"), small_shapes)
    gs_red["conv_w"] = lax.dynamic_slice(gs_red["conv_w"], (0, 0, chip * (LRU_W // N_SHARD)),
                                         (1, CONV_K, LRU_W // N_SHARD))
    grads.update(gs_red)

    delta, new_m, new_v = {}, {}, {}

    def adamw(names, name, **kw):
        d, nm, nv = _adamw([grads[n] for n in names], [w[n][0] for n in names], [mom[n][0] for n in names],
                           [var[n][0] for n in names], name=name, **kw)
        for i, n in enumerate(names):
            delta[n], new_m[n], new_v[n] = d[i], nm[i], nv[i]

    adamw(BIG[3:], "adamw_ffn2", tm=128)
    adamw(["w_in"], "adamw_w_in")
    adamw(["w_out"], "adamw_w_out")
    shard_shapes = {n: w[n].shape for n in SMALL}
    d, nm, nv = _adamw([_pack_small({n: grads[n] for n in SMALL})], [_pack_small({n: w[n] for n in SMALL})],
                       [_pack_small({n: mom[n] for n in SMALL})], [_pack_small({n: var[n] for n in SMALL})],
                       name="adamw_small", tm=None)
    for dst, packed in ((delta, d[0]), (new_m, nm[0]), (new_v, nv[0])):
        dst.update(_unpack_small(packed, shard_shapes))

    worked = [new_v["ffn2_w_down"], new_v["w_in"], new_v["w_out"], nv[0]]
    done1 = _split_wait(thru1, send1, recv1, worked, _scatter_copies(3), name="rs_scatter_ffn1_wait")
    red1 = _join_halves(list(_sum_slabs(done1[:3], done1[3:], name="rs_sum_ffn1")), name="rs_join_ffn1")
    grads.update(zip(BIG[:3], red1))
    adamw(BIG[:3], "adamw_ffn1", tm=128)

    def shaped(tree, n):
        return tree[n].reshape(w[n].shape)

    return (loss, dx[None], *[shaped(grads, n) for n in WEIGHTS], *[shaped(delta, n) for n in WEIGHTS],
            *[shaped(new_m, n) for n in WEIGHTS], *[shaped(new_v, n) for n in WEIGHTS])
```

```python
import functools
import math

import jax
import jax.numpy as jnp
from jax import lax
from jax.experimental import pallas as pl
from jax.experimental.pallas import tpu as pltpu

f32 = jnp.float32
MXU_DTYPE = jnp.bfloat16
GRAD_DTYPE = jnp.bfloat16

D_MODEL = 1024
D_FF = 4096
N_SHARD = 4
N_DEV = 8
FOX_W = 512
LRU_W = 512
HEADS = 8
HEAD_DIM = 64
CONV_K = 4
IN_COLS = 2568
IN_SHARD = IN_COLS // N_SHARD
QKV_W = 3 * FOX_W
Z_PAD = 2688
LANES = 128
LN_EPS = 1e-5
DN_ALPHA = 2.0 ** 0.25
LRU_C = 8.0
NEG_BIG = -1e30
VMEM_LIMIT = 56 * 1024 * 1024

ADAM_LR = 0.001
ADAM_B1 = 0.9
ADAM_B2 = 0.999
ADAM_EPS = 1e-08
ADAM_WD = 0.01
ADAM_STEP = 10


def _pcall(body, **kw):
    return pl.pallas_call(body, **kw)


def _params(n_grid, vmem=VMEM_LIMIT):
    return pltpu.CompilerParams(dimension_semantics=("arbitrary",) * n_grid, vmem_limit_bytes=vmem)


def _dot(a, b):
    return jnp.dot(a, b, preferred_element_type=f32)


def _dot_nt(a, b):
    return lax.dot_general(a, b, (((1,), (1,)), ((), ())), preferred_element_type=f32)


def _dot_tn(a, b):
    return lax.dot_general(a, b, (((0,), (0,)), ((), ())), preferred_element_type=f32)


def _sigmoid(x):
    return 1.0 / (1.0 + jnp.exp(-x))


def _layer_norm_stats(y):
    mu = jnp.mean(y, axis=-1, keepdims=True)
    yc = y - mu
    var = jnp.mean(yc * yc, axis=-1, keepdims=True)
    rstd = lax.rsqrt(var + LN_EPS)
    return yc * rstd, rstd


def _ln_backward(dy, xhat, rstd, gamma):
    dxhat = dy * gamma
    m1 = jnp.mean(dxhat, axis=-1, keepdims=True)
    m2 = jnp.mean(dxhat * xhat, axis=-1, keepdims=True)
    dyp = rstd * (dxhat - m1 - xhat * m2)
    return dyp, jnp.sum(dy * xhat, axis=0, keepdims=True), jnp.sum(dy, axis=0, keepdims=True)


def _ffn_fwd(x, wg, wu, wd, ln_g, ln_b, *, name, tm=1024, tf=512):
    T = x.shape[0]
    tm = min(tm, T)
    fs = D_FF // N_SHARD
    cpf = fs // tf
    nf = D_FF // tf
    nt = T // tm

    def body(x_ref, wg_ref, wu_ref, wd_ref, g_ref, b_ref,
             xb_ref, gact_ref, uact_ref, xhat_ref, xn_ref, rstd_ref, acc_ref):
        f = pl.program_id(1)

        @pl.when(f == 0)
        def _():
            xb_ref[...] = x_ref[...].astype(MXU_DTYPE)
            acc_ref[...] = jnp.zeros_like(acc_ref)

        xb = xb_ref[...]
        g = _dot(xb, wg_ref[...])
        u = _dot(xb, wu_ref[...])
        h = (g * _sigmoid(g)) * u
        gact_ref[...] = g.astype(gact_ref.dtype)
        uact_ref[...] = u.astype(uact_ref.dtype)
        acc_ref[...] += _dot(h.astype(MXU_DTYPE), wd_ref[...])

        @pl.when(f == nf - 1)
        def _():
            y = DN_ALPHA * x_ref[...] + 0.5 * acc_ref[...]
            xhat, rstd = _layer_norm_stats(y)
            xhat_ref[...] = xhat
            xn_ref[...] = (xhat * g_ref[...] + b_ref[...]).astype(xn_ref.dtype)
            rstd_ref[...] = jnp.broadcast_to(rstd, rstd_ref.shape)

    row = lambda i, f: (i, 0)
    return _pcall(
        body, name=name, grid=(nt, nf),
        in_specs=[
            pl.BlockSpec((tm, D_MODEL), row),
            pl.BlockSpec((None, D_MODEL, tf), lambda i, f: (f // cpf, 0, f % cpf)),
            pl.BlockSpec((None, D_MODEL, tf), lambda i, f: (f // cpf, 0, f % cpf)),
            pl.BlockSpec((None, tf, D_MODEL), lambda i, f: (f // cpf, f % cpf, 0)),
            pl.BlockSpec((1, D_MODEL), lambda i, f: (0, 0)),
            pl.BlockSpec((1, D_MODEL), lambda i, f: (0, 0)),
        ],
        out_specs=[
            pl.BlockSpec((tm, D_MODEL), row),
            pl.BlockSpec((tm, tf), lambda i, f: (i, f)),
            pl.BlockSpec((tm, tf), lambda i, f: (i, f)),
            pl.BlockSpec((tm, D_MODEL), row),
            pl.BlockSpec((tm, D_MODEL), row),
            pl.BlockSpec((tm, LANES), row),
        ],
        out_shape=[
            jax.ShapeDtypeStruct((T, D_MODEL), MXU_DTYPE),
            jax.ShapeDtypeStruct((T, D_FF), MXU_DTYPE),
            jax.ShapeDtypeStruct((T, D_FF), MXU_DTYPE),
            jax.ShapeDtypeStruct((T, D_MODEL), f32),
            jax.ShapeDtypeStruct((T, D_MODEL), MXU_DTYPE),
            jax.ShapeDtypeStruct((T, LANES), f32),
        ],
        scratch_shapes=[pltpu.VMEM((tm, D_MODEL), f32)],
        compiler_params=_params(2),
    )(x, wg, wu, wd, ln_g, ln_b)


def _ffn_bwd(dyp, xb, gact, uact, wg, wu, wd, after=None, *, name, tm=512, tf=512):
    T = dyp.shape[0]
    tm = min(tm, T)
    fs = D_FF // N_SHARD
    cpf = fs // tf
    nf = D_FF // tf
    nt = T // tm
    extra = [] if after is None else [after]

    def body(dyp_ref, xb_ref, g_ref, u_ref, wg_ref, wu_ref, wd_ref, *refs):
        dx_hbm, dwg_ref, dwu_ref, dwd_ref, dx_sc, dwg_sc, dwu_sc, dwd_sc, sem = refs[len(extra):]
        f = pl.program_id(0)
        i = pl.program_id(1)
        rows = pl.ds(pl.multiple_of(i * tm, tm), tm)
        dyp_t = dyp_ref[...]
        dy = (0.5 * dyp_t).astype(MXU_DTYPE)

        @pl.when(i == 0)
        def _():
            dwg_sc[...] = jnp.zeros_like(dwg_sc)
            dwu_sc[...] = jnp.zeros_like(dwu_sc)
            dwd_sc[...] = jnp.zeros_like(dwd_sc)

        @pl.when(f == 0)
        def _():
            dx_sc[rows, :] = DN_ALPHA * dyp_t

        g = g_ref[...].astype(f32)
        u = u_ref[...].astype(f32)
        sig = _sigmoid(g)
        silu = g * sig
        dh = _dot_nt(dy, wd_ref[...])
        dg = (dh * u * (sig * (1.0 + g * (1.0 - sig)))).astype(MXU_DTYPE)
        du = (dh * silu).astype(MXU_DTYPE)
        hb = (silu * u).astype(MXU_DTYPE)
        dx_sc[rows, :] += _dot_nt(dg, wg_ref[...]) + _dot_nt(du, wu_ref[...])
        xb_t = xb_ref[...]
        dwg_sc[...] += _dot_tn(xb_t, dg)
        dwu_sc[...] += _dot_tn(xb_t, du)
        dwd_sc[...] += _dot_tn(hb, dy)

        @pl.when(i == nt - 1)
        def _():
            dwg_ref[...] = dwg_sc[...].astype(dwg_ref.dtype)
            dwu_ref[...] = dwu_sc[...].astype(dwu_ref.dtype)
            dwd_ref[...] = dwd_sc[...].astype(dwd_ref.dtype)

        @pl.when(jnp.logical_and(f == nf - 1, i == nt - 1))
        def _():
            cp = pltpu.make_async_copy(dx_sc, dx_hbm, sem)
            cp.start()
            cp.wait()

    row = lambda f, i: (i, 0)
    return _pcall(
        body, name=name, grid=(nf, nt),
        in_specs=[
            pl.BlockSpec((tm, D_MODEL), row),
            pl.BlockSpec((tm, D_MODEL), row),
            pl.BlockSpec((tm, tf), lambda f, i: (i, f)),
            pl.BlockSpec((tm, tf), lambda f, i: (i, f)),
            pl.BlockSpec((None, D_MODEL, tf), lambda f, i: (f // cpf, 0, f % cpf)),
            pl.BlockSpec((None, D_MODEL, tf), lambda f, i: (f // cpf, 0, f % cpf)),
            pl.BlockSpec((None, tf, D_MODEL), lambda f, i: (f // cpf, f % cpf, 0)),
        ] + [pl.BlockSpec(memory_space=pl.ANY)] * len(extra),
        out_specs=[
            pl.BlockSpec(memory_space=pl.ANY),
            pl.BlockSpec((None, D_MODEL, tf), lambda f, i: (f // cpf, 0, f % cpf)),
            pl.BlockSpec((None, D_MODEL, tf), lambda f, i: (f // cpf, 0, f % cpf)),
            pl.BlockSpec((None, tf, D_MODEL), lambda f, i: (f // cpf, f % cpf, 0)),
        ],
        out_shape=[
            jax.ShapeDtypeStruct((T, D_MODEL), f32),
            jax.ShapeDtypeStruct((N_SHARD, D_MODEL, fs), GRAD_DTYPE),
            jax.ShapeDtypeStruct((N_SHARD, D_MODEL, fs), GRAD_DTYPE),
            jax.ShapeDtypeStruct((N_SHARD, fs, D_MODEL), GRAD_DTYPE),
        ],
        scratch_shapes=[pltpu.VMEM((T, D_MODEL), f32), pltpu.VMEM((D_MODEL, tf), f32),
                        pltpu.VMEM((D_MODEL, tf), f32), pltpu.VMEM((tf, D_MODEL), f32),
                        pltpu.SemaphoreType.DMA],
        compiler_params=_params(2),
    )(dyp, xb, gact, uact, wg, wu, wd, *extra)


def _loss_ln_bwd(xhat, rstd, ln_g, ln_b, target, *, name, tm=512):
    T = xhat.shape[0]
    tm = min(tm, T)
    nt = T // tm

    def body(xhat_ref, rstd_ref, g_ref, b_ref, t_ref, dyp_ref, dg_ref, db_ref, loss_ref):
        i = pl.program_id(0)

        @pl.when(i == 0)
        def _():
            dg_ref[...] = jnp.zeros_like(dg_ref)
            db_ref[...] = jnp.zeros_like(db_ref)
            loss_ref[...] = jnp.zeros_like(loss_ref)

        xhat_t = xhat_ref[...]
        gamma = g_ref[...]
        err = xhat_t * gamma + b_ref[...] - t_ref[...]
        sq = jnp.sum(jnp.sum(err * err, axis=0, keepdims=True), axis=1, keepdims=True)
        loss_ref[...] += jnp.broadcast_to(sq * (0.5 / D_MODEL), loss_ref.shape)
        dy = err * (1.0 / D_MODEL)
        dyp, dgam, dbeta = _ln_backward(dy, xhat_t, rstd_ref[:, 0:1], gamma)
        dyp_ref[...] = dyp
        dg_ref[...] += dgam
        db_ref[...] += dbeta

    row = lambda i: (i, 0)
    const = lambda i: (0, 0)
    return _pcall(
        body, name=name, grid=(nt,),
        in_specs=[pl.BlockSpec((tm, D_MODEL), row), pl.BlockSpec((tm, LANES), row),
                  pl.BlockSpec((1, D_MODEL), const), pl.BlockSpec((1, D_MODEL), const),
                  pl.BlockSpec((tm, D_MODEL), row)],
        out_specs=[pl.BlockSpec((tm, D_MODEL), row), pl.BlockSpec((1, D_MODEL), const),
                   pl.BlockSpec((1, D_MODEL), const), pl.BlockSpec((1, LANES), const)],
        out_shape=[jax.ShapeDtypeStruct((T, D_MODEL), f32), jax.ShapeDtypeStruct((1, D_MODEL), f32),
                   jax.ShapeDtypeStruct((1, D_MODEL), f32), jax.ShapeDtypeStruct((1, LANES), f32)],
        compiler_params=_params(1),
    )(xhat, rstd, ln_g, ln_b, target)


def _ln_bwd(dy, xhat, rstd, ln_g, *, name, tm=512):
    T = xhat.shape[0]
    tm = min(tm, T)
    nt = T // tm

    def body(dy_ref, xhat_ref, rstd_ref, g_ref, dyp_ref, dg_ref, db_ref):
        i = pl.program_id(0)

        @pl.when(i == 0)
        def _():
            dg_ref[...] = jnp.zeros_like(dg_ref)
            db_ref[...] = jnp.zeros_like(db_ref)

        dyp, dgam, dbeta = _ln_backward(dy_ref[...], xhat_ref[...], rstd_ref[:, 0:1], g_ref[...])
        dyp_ref[...] = dyp
        dg_ref[...] += dgam
        db_ref[...] += dbeta

    row = lambda i: (i, 0)
    const = lambda i: (0, 0)
    return _pcall(
        body, name=name, grid=(nt,),
        in_specs=[pl.BlockSpec((tm, D_MODEL), row), pl.BlockSpec((tm, D_MODEL), row),
                  pl.BlockSpec((tm, LANES), row), pl.BlockSpec((1, D_MODEL), const)],
        out_specs=[pl.BlockSpec((tm, D_MODEL), row), pl.BlockSpec((1, D_MODEL), const),
                   pl.BlockSpec((1, D_MODEL), const)],
        out_shape=[jax.ShapeDtypeStruct((T, D_MODEL), f32), jax.ShapeDtypeStruct((1, D_MODEL), f32),
                   jax.ShapeDtypeStruct((1, D_MODEL), f32)],
        compiler_params=_params(1),
    )(dy, xhat, rstd, ln_g)


def _proj_in(xn, wp, bfp, *, name, tm=512):
    T = xn.shape[0]
    tm = min(tm, T)
    nt = T // tm

    def body(x_ref, w_ref, b_ref, qkv_ref, lxg_ref, fg_ref):
        z = _dot(x_ref[...], w_ref[...])
        qkv_ref[...] = z[:, :QKV_W].astype(qkv_ref.dtype)
        lxg_ref[...] = z[:, QKV_W:QKV_W + 2 * LRU_W]
        fg_ref[...] = z[:, QKV_W + 2 * LRU_W:] + b_ref[...]

    row = lambda i: (i, 0)
    const = lambda i: (0, 0)
    return _pcall(
        body, name=name, grid=(nt,),
        in_specs=[pl.BlockSpec((tm, D_MODEL), row), pl.BlockSpec((D_MODEL, Z_PAD), const),
                  pl.BlockSpec((1, LANES), const)],
        out_specs=[pl.BlockSpec((tm, QKV_W), row), pl.BlockSpec((tm, 2 * LRU_W), row),
                   pl.BlockSpec((tm, LANES), row)],
        out_shape=[jax.ShapeDtypeStruct((T, QKV_W), MXU_DTYPE), jax.ShapeDtypeStruct((T, 2 * LRU_W), f32),
                   jax.ShapeDtypeStruct((T, LANES), f32)],
        compiler_params=_params(1),
    )(xn, wp, bfp)


def _proj_in_bwd(dq, dk, dv, dlxg, dfg, xn, dyp, wp, *, name, tm=512):
    T = xn.shape[0]
    tm = min(tm, T)
    nt = T // tm

    def body(dq_ref, dk_ref, dv_ref, dl_ref, dfg_ref, x_ref, dyp_ref, w_ref, dx_ref, dw_hbm, dw_sc, sem):
        i = pl.program_id(0)

        @pl.when(i == 0)
        def _():
            dw_sc[...] = jnp.zeros_like(dw_sc)

        dz = jnp.concatenate(
            [dq_ref[...].astype(MXU_DTYPE), dk_ref[...].astype(MXU_DTYPE), dv_ref[...].astype(MXU_DTYPE),
             dl_ref[...].astype(MXU_DTYPE), dfg_ref[...].astype(MXU_DTYPE)], axis=1)
        dx_ref[...] = DN_ALPHA * dyp_ref[...] + _dot_nt(dz, w_ref[...])
        dw_sc[...] += _dot_tn(x_ref[...], dz)

        @pl.when(i == nt - 1)
        def _():
            dw_sc[:, :FOX_W] = dw_sc[:, :FOX_W] * (1.0 / math.sqrt(HEAD_DIM))
            cp = pltpu.make_async_copy(dw_sc, dw_hbm, sem)
            cp.start()
            cp.wait()

    row = lambda i: (i, 0)
    const = lambda i: (0, 0)
    return _pcall(
        body, name=name, grid=(nt,),
        in_specs=[pl.BlockSpec((tm, FOX_W), row), pl.BlockSpec((tm, FOX_W), row), pl.BlockSpec((tm, FOX_W), row),
                  pl.BlockSpec((tm, 2 * LRU_W), row), pl.BlockSpec((tm, LANES), row),
                  pl.BlockSpec((tm, D_MODEL), row), pl.BlockSpec((tm, D_MODEL), row),
                  pl.BlockSpec((D_MODEL, Z_PAD), const)],
        out_specs=[pl.BlockSpec((tm, D_MODEL), row), pl.BlockSpec(memory_space=pl.ANY)],
        out_shape=[jax.ShapeDtypeStruct((T, D_MODEL), f32), jax.ShapeDtypeStruct((D_MODEL, Z_PAD), f32)],
        scratch_shapes=[pltpu.VMEM((D_MODEL, Z_PAD), f32), pltpu.SemaphoreType.DMA],
        compiler_params=_params(1),
    )(dq, dk, dv, dlxg, dfg, xn, dyp, wp)


def _split3(x):
    hi = x.astype(jnp.bfloat16)
    r1 = x - hi.astype(f32)
    mid = r1.astype(jnp.bfloat16)
    lo = (r1 - mid.astype(f32)).astype(jnp.bfloat16)
    return hi, mid, lo


def _tri_dot(tri, x):
    hi, mid, lo = _split3(x)
    return _dot(tri, hi) + _dot(tri, mid) + _dot(tri, lo)


def _fox_prep(fgb, *, name, tm=512):
    T = fgb.shape[0]
    tm = min(tm, T)
    nt = T // tm

    def body(fg_ref, crep_ref, ct_ref, carry):
        i = pl.program_id(0)

        @pl.when(i == 0)
        def _():
            carry[...] = jnp.zeros_like(carry)

        x = fg_ref[...]
        ls = jnp.minimum(x, 0.0) - jnp.log(1.0 + jnp.exp(-jnp.abs(x)))
        r = lax.broadcasted_iota(jnp.int32, (tm, tm), 0)
        c = lax.broadcasted_iota(jnp.int32, (tm, tm), 1)
        tri = jnp.where(r >= c, 1.0, 0.0).astype(jnp.bfloat16)
        cum = _tri_dot(tri, ls) + carry[0:1, :]
        carry[...] = jnp.broadcast_to(cum[tm - 1:tm, :], carry.shape)
        for h in range(HEADS):
            crep_ref[h] = jnp.broadcast_to(cum[:, h:h + 1], (tm, LANES))
        ct_ref[...] = cum.T[:HEADS, :]

    return _pcall(
        body, name=name, grid=(nt,),
        in_specs=[pl.BlockSpec((tm, LANES), lambda i: (i, 0))],
        out_specs=[pl.BlockSpec((HEADS, tm, LANES), lambda i: (0, i, 0)),
                   pl.BlockSpec((HEADS, tm), lambda i: (0, i))],
        out_shape=[jax.ShapeDtypeStruct((HEADS, T, LANES), f32), jax.ShapeDtypeStruct((HEADS, T), f32)],
        scratch_shapes=[pltpu.VMEM((8, LANES), f32)],
        compiler_params=_params(1),
    )(fgb)


def _head_mask(shape, h):
    lane = lax.broadcasted_iota(jnp.int32, shape, 1)
    return (lane < HEAD_DIM) if h == 0 else (lane >= HEAD_DIM)


def _causal_mask(qi, ki, tq, tk):
    r = lax.broadcasted_iota(jnp.int32, (tq, tk), 0)
    c = lax.broadcasted_iota(jnp.int32, (tq, tk), 1)
    return jnp.logical_and(qi == ki, c > r)


def _fox_fwd(qkv, crep, ct, *, name, tq=512):
    T = qkv.shape[0]
    tq = min(tq, T)
    tk = tq
    nq = T // tq
    rep = tk // LANES

    def body(q_ref, k_ref, v_ref, cq_ref, ct_ref, o_ref, lse_ref, m_sc, l_sc, acc_sc):
        j = pl.program_id(0)
        qi = pl.program_id(1)
        ki = pl.program_id(2)

        @pl.when(ki == 0)
        def _():
            m_sc[...] = jnp.full_like(m_sc, NEG_BIG)
            l_sc[...] = jnp.zeros_like(l_sc)
            acc_sc[...] = jnp.zeros_like(acc_sc)

        @pl.when(ki <= qi)
        def _():
            q2 = q_ref[...]
            k2 = k_ref[...]
            v2 = v_ref[...]
            future = _causal_mask(qi, ki, tq, tk)
            for h in range(2):
                qh = jnp.where(_head_mask(q2.shape, h), q2, jnp.zeros_like(q2))
                ck = ct_ref[pl.ds(2 * j + h, 1), :]
                s = _dot_nt(qh, k2) + (jnp.tile(cq_ref[h], (1, rep)) - ck)
                s = jnp.where(future, NEG_BIG, s)
                m_prev = m_sc[h]
                m_new = jnp.maximum(m_prev, jnp.max(s, axis=1, keepdims=True))
                p = jnp.exp(s - jnp.tile(m_new, (1, rep)))
                alpha = jnp.exp(m_prev - m_new)
                l_sc[h] = alpha * l_sc[h] + jnp.sum(p, axis=1, keepdims=True)
                acc_sc[h] = alpha * acc_sc[h] + _dot(p.astype(MXU_DTYPE), v2)
                m_sc[h] = m_new

        @pl.when(ki == qi)
        def _():
            o0 = acc_sc[0] / l_sc[0]
            o1 = acc_sc[1] / l_sc[1]
            o_ref[...] = jnp.where(_head_mask(o0.shape, 0), o0, o1)
            for h in range(2):
                lse_ref[h] = m_sc[h] + jnp.log(l_sc[h])

    kv = lambda j, qi, ki: jnp.minimum(ki, qi)
    return _pcall(
        body, name=name, grid=(HEADS // 2, nq, nq),
        in_specs=[
            pl.BlockSpec((tq, LANES), lambda j, qi, ki: (qi, j)),
            pl.BlockSpec((tk, LANES), lambda j, qi, ki: (kv(j, qi, ki), 4 + j)),
            pl.BlockSpec((tk, LANES), lambda j, qi, ki: (kv(j, qi, ki), 8 + j)),
            pl.BlockSpec((2, tq, LANES), lambda j, qi, ki: (j, qi, 0)),
            pl.BlockSpec((HEADS, tk), lambda j, qi, ki: (0, kv(j, qi, ki))),
        ],
        out_specs=[pl.BlockSpec((tq, LANES), lambda j, qi, ki: (qi, j)),
                   pl.BlockSpec((2, tq, LANES), lambda j, qi, ki: (j, qi, 0))],
        out_shape=[jax.ShapeDtypeStruct((T, FOX_W), f32), jax.ShapeDtypeStruct((HEADS, T, LANES), f32)],
        scratch_shapes=[pltpu.VMEM((2, tq, LANES), f32)] * 3,
        compiler_params=_params(3),
    )(qkv, qkv, qkv, crep, ct)


def _fox_bwd_prep(do, o, *, name, tm=512):
    T = o.shape[0]
    tm = min(tm, T)
    nt = T // tm

    def body(do_ref, o_ref, d_ref):
        prod = do_ref[...].astype(f32) * o_ref[...]
        for j in range(HEADS // 2):
            pj = prod[:, j * LANES:(j + 1) * LANES]
            for h in range(2):
                dsum = jnp.sum(jnp.where(_head_mask(pj.shape, h), pj, 0.0), axis=1, keepdims=True)
                d_ref[2 * j + h] = jnp.broadcast_to(dsum, (tm, LANES))

    return _pcall(
        body, name=name, grid=(nt,),
        in_specs=[pl.BlockSpec((tm, FOX_W), lambda i: (i, 0)), pl.BlockSpec((tm, FOX_W), lambda i: (i, 0))],
        out_specs=[pl.BlockSpec((HEADS, tm, LANES), lambda i: (0, i, 0))],
        out_shape=[jax.ShapeDtypeStruct((HEADS, T, LANES), f32)],
        compiler_params=_params(1),
    )(do, o)[0]


def _fox_bwd(qkv, do, crep, ct, lse, drep, *, name, tq=512):
    T = qkv.shape[0]
    tq = min(tq, T)
    tk = tq
    nq = T // tq
    rep = tk // LANES

    def body(q_ref, k_ref, v_ref, do_ref, cq_ref, ct_ref, lse_ref, d_ref,
             dq_ref, drow_ref, dk_ref, dv_ref, dcol_ref, dk_sc, dv_sc):
        j = pl.program_id(0)
        ki = pl.program_id(1)
        qi = pl.program_id(2)
        rows = pl.ds(pl.multiple_of(qi * tq, tq), tq)

        @pl.when(qi == 0)
        def _():
            dk_sc[...] = jnp.zeros_like(dk_sc)
            dv_sc[...] = jnp.zeros_like(dv_sc)

        @pl.when(jnp.logical_and(ki == 0, qi == 0))
        def _():
            dq_ref[...] = jnp.zeros_like(dq_ref)
            drow_ref[...] = jnp.zeros_like(drow_ref)

        @pl.when(qi >= ki)
        def _():
            q2 = q_ref[...]
            k2 = k_ref[...]
            v2 = v_ref[...]
            do2 = do_ref[...]
            future = _causal_mask(qi, ki, tq, tk)
            dq_acc = jnp.zeros((tq, LANES), f32)
            drow_acc = jnp.zeros((tq, LANES), f32)
            for h in range(2):
                hm = _head_mask(q2.shape, h)
                qh = jnp.where(hm, q2, jnp.zeros_like(q2))
                ck = ct_ref[pl.ds(2 * j + h, 1), :]
                s = _dot_nt(qh, k2) + (jnp.tile(cq_ref[h], (1, rep)) - ck)
                p = jnp.exp(s - jnp.tile(lse_ref[h], (1, rep)))
                p = jnp.where(future, 0.0, p)
                doh = jnp.where(hm, do2, jnp.zeros_like(do2))
                dp = _dot_nt(doh, v2)
                ds = (p * (dp - jnp.tile(d_ref[h], (1, rep)))).astype(MXU_DTYPE)
                dv_sc[h] += _dot_tn(p.astype(MXU_DTYPE), do2)
                q_ones = jnp.where(hm, q2, jnp.ones_like(q2))
                dk_sc[h] += _dot_tn(ds, q_ones)
                dq_full = _dot(ds, jnp.where(hm, k2, jnp.ones_like(k2)))
                dq_acc = dq_acc + jnp.where(hm, dq_full, 0.0)
                drow_acc = drow_acc + jnp.where(hm, 0.0, dq_full)
            dq_ref[rows, :] += dq_acc
            drow_ref[rows, :] += drow_acc

        @pl.when(qi == nq - 1)
        def _():
            hm0 = _head_mask((tk, LANES), 0)
            dk_ref[...] = jnp.where(hm0, dk_sc[0], dk_sc[1])
            dcol_ref[...] = jnp.where(hm0, dk_sc[1], dk_sc[0])
            dv_ref[...] = jnp.where(hm0, dv_sc[0], dv_sc[1])

    qb = lambda j, ki, qi: jnp.maximum(qi, ki)
    return _pcall(
        body, name=name, grid=(HEADS // 2, nq, nq),
        in_specs=[
            pl.BlockSpec((tq, LANES), lambda j, ki, qi: (qb(j, ki, qi), j)),
            pl.BlockSpec((tk, LANES), lambda j, ki, qi: (ki, 4 + j)),
            pl.BlockSpec((tk, LANES), lambda j, ki, qi: (ki, 8 + j)),
            pl.BlockSpec((tq, LANES), lambda j, ki, qi: (qb(j, ki, qi), j)),
            pl.BlockSpec((2, tq, LANES), lambda j, ki, qi: (j, qb(j, ki, qi), 0)),
            pl.BlockSpec((HEADS, tk), lambda j, ki, qi: (0, ki)),
            pl.BlockSpec((2, tq, LANES), lambda j, ki, qi: (j, qb(j, ki, qi), 0)),
            pl.BlockSpec((2, tq, LANES), lambda j, ki, qi: (j, qb(j, ki, qi), 0)),
        ],
        out_specs=[
            pl.BlockSpec((T, LANES), lambda j, ki, qi: (0, j)),
            pl.BlockSpec((T, LANES), lambda j, ki, qi: (0, j)),
            pl.BlockSpec((tk, LANES), lambda j, ki, qi: (ki, j)),
            pl.BlockSpec((tk, LANES), lambda j, ki, qi: (ki, j)),
            pl.BlockSpec((tk, LANES), lambda j, ki, qi: (ki, j)),
        ],
        out_shape=[jax.ShapeDtypeStruct((T, FOX_W), f32)] * 5,
        scratch_shapes=[pltpu.VMEM((2, tk, LANES), f32)] * 2,
        compiler_params=_params(3),
    )(qkv, qkv, qkv, do, crep, ct, lse, drep)


def _fox_bwd_post(drow, dcol, fgb, *, name, tm=512):
    T = fgb.shape[0]
    tm = min(tm, T)
    nt = T // tm

    def body(drow_ref, dcol_ref, fg_ref, dfg_ref, dbf_ref, carry):
        i = pl.program_id(0)

        @pl.when(i == 0)
        def _():
            carry[...] = jnp.zeros_like(carry)
            dbf_ref[...] = jnp.zeros_like(dbf_ref)

        dcol_t = drow_ref[...] - dcol_ref[...]
        lane = lax.broadcasted_iota(jnp.int32, (tm, LANES), 1)
        dc = jnp.zeros((tm, LANES), f32)
        for h in range(HEADS):
            src = (h // 2) * LANES + (HEAD_DIM if h % 2 == 0 else 0)
            dc = jnp.where(lane == h, jnp.broadcast_to(dcol_t[:, src:src + 1], (tm, LANES)), dc)
        r = lax.broadcasted_iota(jnp.int32, (tm, tm), 0)
        c = lax.broadcasted_iota(jnp.int32, (tm, tm), 1)
        tri = jnp.where(c >= r, 1.0, 0.0).astype(jnp.bfloat16)
        dls = _tri_dot(tri, dc) + carry[0:1, :]
        carry[...] = jnp.broadcast_to(dls[0:1, :], carry.shape)
        dfg = dls * _sigmoid(-fg_ref[...])
        dfg_ref[...] = dfg
        dbf_ref[...] += jnp.sum(dfg, axis=0, keepdims=True)

    rev = lambda i: (nt - 1 - i, 0)
    return _pcall(
        body, name=name, grid=(nt,),
        in_specs=[pl.BlockSpec((tm, FOX_W), rev), pl.BlockSpec((tm, FOX_W), rev), pl.BlockSpec((tm, LANES), rev)],
        out_specs=[pl.BlockSpec((tm, LANES), rev), pl.BlockSpec((1, LANES), lambda i: (0, 0))],
        out_shape=[jax.ShapeDtypeStruct((T, LANES), f32), jax.ShapeDtypeStruct((1, LANES), f32)],
        scratch_shapes=[pltpu.VMEM((8, LANES), f32)],
        compiler_params=_params(1),
    )(drow, dcol, fgb)


GELU_C = math.sqrt(2.0 / math.pi)
GELU_A = 0.044715


def _gelu(x):
    t = jnp.tanh(GELU_C * (x + GELU_A * x * x * x))
    return 0.5 * x * (1.0 + t), t


def _gelu_grad(x, t):
    return 0.5 * (1.0 + t) + 0.5 * x * (1.0 - t * t) * GELU_C * (1.0 + 3.0 * GELU_A * x * x)


def _expm1(x):
    e = jnp.exp(x)
    safe = jnp.where(e == 1.0, x, (e - 1.0) * x / jnp.log(jnp.where(e == 1.0, 0.5, e)))
    return jnp.where(x < -0.5, e - 1.0, safe)


def _lru_gates(u, wab_ref, bab_ref, lam_ref):
    pre = _dot(u.astype(MXU_DTYPE), wab_ref[...]) + bab_ref[...]
    r = _sigmoid(pre[:, :LRU_W])
    gi = _sigmoid(pre[:, LRU_W:])
    lam = lam_ref[...]
    sp = jnp.maximum(-lam, 0.0) + jnp.log(1.0 + jnp.exp(-jnp.abs(lam)))
    log_a = -LRU_C * r * sp
    a = jnp.exp(log_a)
    s = jnp.sqrt(-_expm1(2.0 * log_a))
    return r, gi, sp, a, s


def _lru_fwd(lxg, conv_w, conv_b, wab, bab, lam, *, name, tc=512):
    T = lxg.shape[0]
    tc = min(tc, T)
    nc = T // tc

    def body(lx_ref, lg_ref, cw_ref, cb_ref, wab_ref, bab_ref, lam_ref,
             out_ref, u_ref, hs_ref, ext, a_sc, b_sc, h_sc):
        i = pl.program_id(0)

        @pl.when(i == 0)
        def _():
            ext[0:8, :] = jnp.zeros((8, LRU_W), f32)
            h_sc[...] = jnp.zeros_like(h_sc)

        ext[8:, :] = lx_ref[...]
        u = cb_ref[...] + cw_ref[0:1, :] * ext[pl.ds(5, tc), :]
        for k in range(1, CONV_K):
            u = u + cw_ref[k:k + 1, :] * ext[pl.ds(5 + k, tc), :]
        ext[0:8, :] = ext[tc:tc + 8, :]
        u_ref[...] = u
        r, gi, sp, a, s = _lru_gates(u, wab_ref, bab_ref, lam_ref)
        a_sc[...] = a
        b_sc[...] = s * (gi * u)

        def step(t, h):
            h = a_sc[pl.ds(t, 1), :] * h + b_sc[pl.ds(t, 1), :]
            hs_ref[pl.ds(t, 1), :] = h
            return h

        h = lax.fori_loop(0, tc, step, h_sc[0:1, :], unroll=8)
        h_sc[...] = jnp.broadcast_to(h, h_sc.shape)
        gel, _ = _gelu(lg_ref[...])
        out_ref[...] = gel * hs_ref[...]

    row = lambda i: (i, 0)
    const = lambda i: (0, 0)
    return _pcall(
        body, name=name, grid=(nc,),
        in_specs=[pl.BlockSpec((tc, LRU_W), row), pl.BlockSpec((tc, LRU_W), lambda i: (i, 1)),
                  pl.BlockSpec((CONV_K, LRU_W), const), pl.BlockSpec((1, LRU_W), const),
                  pl.BlockSpec((LRU_W, 2 * LRU_W), const), pl.BlockSpec((1, 2 * LRU_W), const),
                  pl.BlockSpec((1, LRU_W), const)],
        out_specs=[pl.BlockSpec((tc, LRU_W), row)] * 3,
        out_shape=[jax.ShapeDtypeStruct((T, LRU_W), f32)] * 3,
        scratch_shapes=[pltpu.VMEM((tc + 8, LRU_W), f32), pltpu.VMEM((tc, LRU_W), f32),
                        pltpu.VMEM((tc, LRU_W), f32), pltpu.VMEM((8, LRU_W), f32)],
        compiler_params=_params(1),
    )(lxg, lxg, conv_w, conv_b, wab, bab, lam)


def _lru_bwd(dlru, lxg, u, hs, conv_w, wab, bab, lam, *, name, tc=512):
    T = lxg.shape[0]
    tc = min(tc, T)
    nc = T // tc
    bp = tc // 8

    def body(dl_ref, lx_ref, lxp_ref, lg_ref, u_ref, hs_ref, hsp_ref, cw_ref, wab_ref, bab_ref, lam_ref,
             dlxg_ref, dwab_ref, dbab_ref, dcw_ref, dcb_ref, dlam_ref,
             dh_sc, a_sc, ext, du_ext, carry):
        i = pl.program_id(0)
        first_chunk = i == nc - 1

        @pl.when(i == 0)
        def _():
            dwab_ref[...] = jnp.zeros_like(dwab_ref)
            dbab_ref[...] = jnp.zeros_like(dbab_ref)
            dcw_ref[...] = jnp.zeros_like(dcw_ref)
            dcb_ref[...] = jnp.zeros_like(dcb_ref)
            dlam_ref[...] = jnp.zeros_like(dlam_ref)
            carry[...] = jnp.zeros_like(carry)
            du_ext[tc:tc + 8, :] = jnp.zeros((8, LRU_W), f32)

        lg = lg_ref[...]
        gel, th = _gelu(lg)
        dl = dl_ref[...]
        hs = hs_ref[...]
        dlg = dl * hs * _gelu_grad(lg, th)
        u = u_ref[...]
        r, gi, sp, a, s = _lru_gates(u, wab_ref, bab_ref, lam_ref)
        a_sc[...] = a
        dh_sc[...] = dl * gel

        def step(k, c):
            t = tc - 1 - k
            dh = dh_sc[pl.ds(t, 1), :] + c
            dh_sc[pl.ds(t, 1), :] = dh
            return a_sc[pl.ds(t, 1), :] * dh

        c = lax.fori_loop(0, tc, step, carry[0:1, :], unroll=8)
        carry[...] = jnp.broadcast_to(c, carry.shape)

        ext[0:8, :] = jnp.where(first_chunk, 0.0, hsp_ref[...])
        ext[8:, :] = hs
        hprev = ext[pl.ds(7, tc), :]
        dh = dh_sc[...]
        da = dh * hprev
        giu = gi * u
        dla = da * a - (dh * giu) * (a * a / s)
        dgi = dh * s * u
        du = dh * s * gi
        dr = dla * (-LRU_C * sp)
        dlam_ref[...] += jnp.sum(dla * (-LRU_C * r), axis=0, keepdims=True) * (-_sigmoid(-lam_ref[...]))
        dpre = jnp.concatenate([dr * r * (1.0 - r), dgi * gi * (1.0 - gi)], axis=1)
        dpre_b = dpre.astype(MXU_DTYPE)
        du = du + _dot_nt(dpre_b, wab_ref[...])
        dwab_ref[...] += _dot_tn(u.astype(MXU_DTYPE), dpre_b)
        dbab_ref[...] += jnp.sum(dpre, axis=0, keepdims=True)
        dcb_ref[...] += jnp.sum(du, axis=0, keepdims=True)

        du_ext[0:tc, :] = du
        dlx = cw_ref[0:1, :] * du_ext[pl.ds(3, tc), :]
        for k in range(1, CONV_K):
            dlx = dlx + cw_ref[k:k + 1, :] * du_ext[pl.ds(3 - k, tc), :]
        du_ext[tc:tc + 8, :] = du_ext[0:8, :]
        ext[0:8, :] = jnp.where(first_chunk, 0.0, lxp_ref[...])
        ext[8:, :] = lx_ref[...]
        for k in range(CONV_K):
            dcw_ref[k:k + 1, :] += jnp.sum(du * ext[pl.ds(5 + k, tc), :], axis=0, keepdims=True)
        dlxg_ref[:, :LRU_W] = dlx.astype(dlxg_ref.dtype)
        dlxg_ref[:, LRU_W:] = dlg.astype(dlxg_ref.dtype)

    rev = lambda i: (nc - 1 - i, 0)
    prev8 = lambda i: (jnp.maximum((nc - 1 - i) * bp - 1, 0), 0)
    const = lambda i: (0, 0)
    return _pcall(
        body, name=name, grid=(nc,),
        in_specs=[
            pl.BlockSpec((tc, LRU_W), rev),
            pl.BlockSpec((tc, LRU_W), rev),
            pl.BlockSpec((8, LRU_W), prev8),
            pl.BlockSpec((tc, LRU_W), lambda i: (nc - 1 - i, 1)),
            pl.BlockSpec((tc, LRU_W), rev),
            pl.BlockSpec((tc, LRU_W), rev),
            pl.BlockSpec((8, LRU_W), prev8),
            pl.BlockSpec((CONV_K, LRU_W), const),
            pl.BlockSpec((LRU_W, 2 * LRU_W), const),
            pl.BlockSpec((1, 2 * LRU_W), const),
            pl.BlockSpec((1, LRU_W), const),
        ],
        out_specs=[
            pl.BlockSpec((tc, 2 * LRU_W), rev),
            pl.BlockSpec((LRU_W, 2 * LRU_W), const),
            pl.BlockSpec((1, 2 * LRU_W), const),
            pl.BlockSpec((8, LRU_W), const),
            pl.BlockSpec((1, LRU_W), const),
            pl.BlockSpec((1, LRU_W), const),
        ],
        out_shape=[
            jax.ShapeDtypeStruct((T, 2 * LRU_W), MXU_DTYPE),
            jax.ShapeDtypeStruct((LRU_W, 2 * LRU_W), f32),
            jax.ShapeDtypeStruct((1, 2 * LRU_W), f32),
            jax.ShapeDtypeStruct((8, LRU_W), f32),
            jax.ShapeDtypeStruct((1, LRU_W), f32),
            jax.ShapeDtypeStruct((1, LRU_W), f32),
        ],
        scratch_shapes=[pltpu.VMEM((tc, LRU_W), f32), pltpu.VMEM((tc, LRU_W), f32),
                        pltpu.VMEM((tc + 8, LRU_W), f32), pltpu.VMEM((tc + 8, LRU_W), f32),
                        pltpu.VMEM((8, LRU_W), f32)],
        compiler_params=_params(1),
    )(dlru, lxg, lxg, lxg, u, hs, hs, conv_w, wab, bab, lam)


def _mix_out(fox, lru, wo, xhat1, g1, b1, g2, b2, *, name, tm=512):
    T = fox.shape[0]
    tm = min(tm, T)
    nt = T // tm

    def body(fox_ref, lru_ref, wo_ref, xh_ref, g1_ref, b1_ref, g2_ref, b2_ref, xhat_ref, xn_ref, rstd_ref):
        mix = _dot(fox_ref[...].astype(MXU_DTYPE), wo_ref[:FOX_W, :])
        mix = mix + _dot(lru_ref[...].astype(MXU_DTYPE), wo_ref[FOX_W:, :])
        x1 = xh_ref[...] * g1_ref[...] + b1_ref[...]
        xhat, rstd = _layer_norm_stats(DN_ALPHA * x1 + mix)
        xhat_ref[...] = xhat
        xn_ref[...] = xhat * g2_ref[...] + b2_ref[...]
        rstd_ref[...] = jnp.broadcast_to(rstd, rstd_ref.shape)

    row = lambda i: (i, 0)
    const = lambda i: (0, 0)
    vec = pl.BlockSpec((1, D_MODEL), const)
    return _pcall(
        body, name=name, grid=(nt,),
        in_specs=[pl.BlockSpec((tm, FOX_W), row), pl.BlockSpec((tm, LRU_W), row),
                  pl.BlockSpec((D_MODEL, D_MODEL), const), pl.BlockSpec((tm, D_MODEL), row), vec, vec, vec, vec],
        out_specs=[pl.BlockSpec((tm, D_MODEL), row), pl.BlockSpec((tm, D_MODEL), row),
                   pl.BlockSpec((tm, LANES), row)],
        out_shape=[jax.ShapeDtypeStruct((T, D_MODEL), f32), jax.ShapeDtypeStruct((T, D_MODEL), f32),
                   jax.ShapeDtypeStruct((T, LANES), f32)],
        compiler_params=_params(1),
    )(fox, lru, wo, xhat1, g1, b1, g2, b2)


def _mix_out_bwd(dyp, fox, lru, wo, *, name, tm=512):
    T = fox.shape[0]
    tm = min(tm, T)
    nt = T // tm

    def body(dyp_ref, fox_ref, lru_ref, wo_ref, dfox_ref, dlru_ref, dwo_ref):
        i = pl.program_id(0)

        @pl.when(i == 0)
        def _():
            dwo_ref[...] = jnp.zeros_like(dwo_ref)

        dmix = dyp_ref[...].astype(MXU_DTYPE)
        dcat = _dot_nt(dmix, wo_ref[...])
        dfox_ref[...] = dcat[:, :FOX_W].astype(dfox_ref.dtype)
        dlru_ref[...] = dcat[:, FOX_W:]
        dwo_ref[:FOX_W, :] += _dot_tn(fox_ref[...].astype(MXU_DTYPE), dmix)
        dwo_ref[FOX_W:, :] += _dot_tn(lru_ref[...].astype(MXU_DTYPE), dmix)

    row = lambda i: (i, 0)
    const = lambda i: (0, 0)
    return _pcall(
        body, name=name, grid=(nt,),
        in_specs=[pl.BlockSpec((tm, D_MODEL), row), pl.BlockSpec((tm, FOX_W), row), pl.BlockSpec((tm, LRU_W), row),
                  pl.BlockSpec((D_MODEL, D_MODEL), const)],
        out_specs=[pl.BlockSpec((tm, FOX_W), row), pl.BlockSpec((tm, LRU_W), row),
                   pl.BlockSpec((D_MODEL, D_MODEL), const)],
        out_shape=[jax.ShapeDtypeStruct((T, FOX_W), MXU_DTYPE), jax.ShapeDtypeStruct((T, LRU_W), f32),
                   jax.ShapeDtypeStruct((D_MODEL, D_MODEL), f32)],
        compiler_params=_params(1),
    )(dyp, fox, lru, wo)


def make_wp(w_in):
    scale = jnp.concatenate([jnp.full((FOX_W,), 1.0 / math.sqrt(HEAD_DIM), w_in.dtype),
                             jnp.ones((IN_COLS - FOX_W,), w_in.dtype)])
    return jnp.pad(w_in * scale[None, :], ((0, 0), (0, Z_PAD - IN_COLS)))


def _block_diag(w):
    eye = jnp.eye(HEADS, dtype=w.dtype)
    return jnp.einsum("hij,hg->higj", w, eye).reshape(LRU_W, LRU_W)


def _block_diag_extract(m):
    m4 = m.reshape(HEADS, HEAD_DIM, HEADS, HEAD_DIM)
    return jnp.stack([m4[h, :, h, :] for h in range(HEADS)])


class _NoOverlap:
    def start_token(self):
        return None

    def after_attention(self, after):
        return None

    def ffn2_weights(self, w, after):
        return w["f2g"], w["f2u"], w["f2d"]

    def ffn2_grads(self, grads):
        return None

    def mixer_grads(self, dwp, dwo, small):
        return None

    def before_ffn1_bwd(self, after):
        return None


def _tied(a, token):
    return a if token is None else a + token[0, 0]


def _local_step(x, target, w, hooks=None):
    hooks = hooks or _NoOverlap()
    wp = w["wp"]
    bfp = w["bfp"]
    wab = jnp.concatenate([_block_diag(w["rg_wa"]), _block_diag(w["rg_wx"])], axis=1).astype(MXU_DTYPE)
    bab = jnp.concatenate([w["rg_ba"].reshape(1, LRU_W), w["rg_bx"].reshape(1, LRU_W)], axis=1)

    xb0, g1a, u1a, xhat1, xn1, rstd1 = _ffn_fwd(x, w["f1g"], w["f1u"], w["f1d"], w["ln1_g"],
                                                _tied(w["ln1_b"], hooks.start_token()), name="ffn1_fwd")
    qkv, lxg, fgb = _proj_in(xn1, wp, bfp, name="proj_in")
    crep, ct = _fox_prep(fgb, name="fox_prep")
    fox, lse = _fox_fwd(qkv, crep, ct, name="fox_fwd")
    token = hooks.after_attention([lse])
    lru, uconv, hs = _lru_fwd(lxg, w["conv_w"], _tied(w["conv_b"], token), wab, bab, w["lam"], name="lru_fwd")
    xhat2, x2, rstd2 = _mix_out(fox, lru, w["wo"], xhat1, w["ln1_g"], w["ln1_b"], w["ln2_g"], w["ln2_b"], name="mix_out")
    f2g, f2u, f2d = hooks.ffn2_weights(w, [rstd2])
    xb2, g2a, u2a, xhat3, _, rstd3 = _ffn_fwd(x2, f2g, f2u, f2d, w["ln3_g"], w["ln3_b"], name="ffn2_fwd")

    dy3p, dln3g, dln3b, loss = _loss_ln_bwd(xhat3, rstd3, w["ln3_g"], w["ln3_b"], target, name="loss_ln3_bwd")
    dx2, df2g, df2u, df2d = _ffn_bwd(dy3p, xb2, g2a, u2a, f2g, f2u, f2d, name="ffn2_bwd")
    token = hooks.ffn2_grads([df2g, df2u, df2d])
    dy2p, dln2g, dln2b = _ln_bwd(dx2, xhat2, rstd2, _tied(w["ln2_g"], token), name="ln2_bwd")
    dfox, dlru, dwo = _mix_out_bwd(dy2p, fox, lru, w["wo"], name="mix_out_bwd")
    dlxg, dwab, dbab, dcw, dcb, dlam = _lru_bwd(dlru, lxg, uconv, hs, w["conv_w"], wab, bab, w["lam"], name="lru_bwd")
    drep = _fox_bwd_prep(dfox, fox, name="fox_bwd_prep")
    dq, drow, dk, dv, dcol = _fox_bwd(qkv, dfox, crep, ct, lse, drep, name="fox_bwd")
    dfg, dbf = _fox_bwd_post(drow, dcol, fgb, name="fox_bwd_post")
    dx1, dwp = _proj_in_bwd(dq, dk, dv, dlxg, dfg, xn1, dy2p, wp, name="proj_in_bwd")
    dy1p, dln1g, dln1b = _ln_bwd(dx1, xhat1, rstd1, w["ln1_g"], name="ln1_bwd")
    small = dict(
        ln1_g=dln1g, ln1_b=dln1b, ln2_g=dln2g, ln2_b=dln2b, ln3_g=dln3g, ln3_b=dln3b,
        b_forget=dbf[:, :HEADS], conv_w=dcw[:CONV_K], conv_b=dcb,
        rg_wa=_block_diag_extract(dwab[:, :LRU_W]), rg_wx=_block_diag_extract(dwab[:, LRU_W:]),
        rg_ba=dbab[:, :LRU_W].reshape(HEADS, HEAD_DIM), rg_bx=dbab[:, LRU_W:].reshape(HEADS, HEAD_DIM),
        lru_lambda=dlam,
    )
    hooks.before_ffn1_bwd([dln1b])
    token = hooks.mixer_grads(dwp, dwo, small)
    dx, df1g, df1u, df1d = _ffn_bwd(dy1p, xb0, g1a, u1a, w["f1g"], w["f1u"], w["f1d"], token, name="ffn1_bwd")

    grads = dict(f1g=df1g, f1u=df1u, f1d=df1d, f2g=df2g, f2u=df2u, f2d=df2d, wp=dwp, wo=dwo, **small)
    return loss, dx, grads


MESH = pl.DeviceIdType.MESH
HBM_SPEC = pl.BlockSpec(memory_space=pl.ANY)
VMEM_SPEC = pl.BlockSpec(memory_space=pltpu.VMEM)


def _position():
    return lax.axis_index("x"), lax.axis_index("y"), lax.axis_index("c")


def _other_chips(x, y):
    return [(1 - x, y), (x, 1 - y), (1 - x, 1 - y)]


def _all_gather_bf16(shards, *, name):
    n = len(shards)

    def body(*refs):
        ins, outs, stages = refs[:n], refs[n:2 * n], refs[2 * n:3 * n]
        send_sems, recv_sems, local_sems = refs[3 * n:]
        x, y, c = _position()
        me, sibling = (x, y, c), (x, y, 1 - c)
        chips = _other_chips(x, y)

        def rows(k, px, py, pc):
            r = shards[k].shape[0]
            m = r // 2
            return outs[k].at[pl.ds(pl.multiple_of((2 * px + py) * r + pc * m, 16), m), :]

        def copy(k, idx, block, to, src=None):
            return pltpu.make_async_remote_copy(
                src_ref=rows(k, *block) if src is None else src, dst_ref=rows(k, *block),
                send_sem=send_sems.at[7 * k + idx], recv_sem=recv_sems.at[7 * k + idx],
                device_id=to, device_id_type=MESH)

        started = []
        mine = []
        for k in range(n):
            m = shards[k].shape[0] // 2
            stages[k][...] = ins[k][pl.ds(pl.multiple_of(c * m, 16), m), :].astype(stages[k].dtype)
            cp = pltpu.make_async_copy(stages[k], rows(k, *me), local_sems.at[k])
            cp.start()
            mine.append(cp)
            first = [copy(k, 0, me, sibling, src=stages[k])]
            first += [copy(k, 1 + j, me, (*chip, c), src=stages[k]) for j, chip in enumerate(chips)]
            for cp in first:
                cp.start()
            started += first
        for k in range(n):
            for j, chip in enumerate(chips):
                copy(k, 1 + j, (*chip, c), me).wait_recv()
                fwd = copy(k, 4 + j, (*chip, c), sibling)
                fwd.start()
                started.append(fwd)
        for k in range(n):
            copy(k, 0, sibling, me).wait_recv()
            for j, chip in enumerate(chips):
                copy(k, 4 + j, (*chip, 1 - c), me).wait_recv()
        for cp in started:
            cp.wait_send()
        for cp in mine:
            cp.wait()

    return _pcall(
        body, name=name,
        in_specs=[VMEM_SPEC] * n, out_specs=[HBM_SPEC] * n,
        out_shape=[jax.ShapeDtypeStruct((N_SHARD * s.shape[0], s.shape[1]), MXU_DTYPE) for s in shards],
        scratch_shapes=[pltpu.VMEM((s.shape[0] // 2, s.shape[1]), MXU_DTYPE) for s in shards]
        + [pltpu.SemaphoreType.DMA((7 * n,)), pltpu.SemaphoreType.DMA((7 * n,)), pltpu.SemaphoreType.DMA((n,))],
        compiler_params=pltpu.CompilerParams(vmem_limit_bytes=VMEM_LIMIT),
    )(*shards)


def _swap_halves(gs, *, name):
    n = len(gs)

    def body(*refs):
        ins, outs = refs[:n], refs[n:2 * n]
        send_sems, recv_sems = refs[2 * n:]
        x, y, c = _position()
        cps = []
        for k in range(n):
            m = gs[k].shape[1] // 2
            src = ins[k].at[:, pl.ds(pl.multiple_of((1 - c) * m, 16), m), :]
            cp = pltpu.make_async_remote_copy(src_ref=src, dst_ref=outs[k], send_sem=send_sems.at[k],
                                              recv_sem=recv_sems.at[k], device_id=(x, y, 1 - c), device_id_type=MESH)
            cp.start()
            cps.append(cp)
        for cp in cps:
            cp.wait()

    return _pcall(
        body, name=name, in_specs=[HBM_SPEC] * n, out_specs=[HBM_SPEC] * n,
        out_shape=[jax.ShapeDtypeStruct((g.shape[0], g.shape[1] // 2, g.shape[2]), g.dtype) for g in gs],
        scratch_shapes=[pltpu.SemaphoreType.DMA((n,)), pltpu.SemaphoreType.DMA((n,))],
    )(*gs)


def _add_halves(gs, recvs, *, name, tm=256):
    n = len(gs)
    _, r, cdim = gs[0].shape
    m = r // 2
    tm = min(tm, m)
    nb = m // tm
    c_idx = lax.axis_index("c").astype(jnp.int32).reshape(1)

    def body(c_ref, *refs):
        for k in range(n):
            refs[2 * n + k][...] = (refs[k][...].astype(f32) + refs[n + k][...].astype(f32)).astype(refs[2 * n + k].dtype)

    mine = pl.BlockSpec((None, tm, cdim), lambda j, i, c_ref: (j, c_ref[0] * nb + i, 0))
    half = pl.BlockSpec((None, tm, cdim), lambda j, i, c_ref: (j, i, 0))
    return _pcall(
        body, name=name,
        grid_spec=pltpu.PrefetchScalarGridSpec(
            num_scalar_prefetch=1, grid=(N_SHARD, nb),
            in_specs=[mine] * n + [half] * n, out_specs=[half] * n),
        out_shape=[jax.ShapeDtypeStruct((N_SHARD, m, cdim), g.dtype) for g in gs],
        compiler_params=_params(2),
    )(c_idx, *gs, *recvs)


def _scatter_partials(ps, *, name):
    n = len(ps)

    def body(*refs):
        ins, outs = refs[:n], refs[n:2 * n]
        send_sems, recv_sems = refs[2 * n:]
        x, y, c = _position()
        me_chip = 2 * x + y
        cps = []
        for k in range(n):
            for j, (px, py) in enumerate(_other_chips(x, y)):
                cp = pltpu.make_async_remote_copy(
                    src_ref=ins[k].at[2 * px + py], dst_ref=outs[k].at[me_chip],
                    send_sem=send_sems.at[3 * k + j], recv_sem=recv_sems.at[3 * k + j],
                    device_id=(px, py, c), device_id_type=MESH)
                cp.start()
                cps.append(cp)
        for cp in cps:
            cp.wait()

    return _pcall(
        body, name=name, in_specs=[HBM_SPEC] * n, out_specs=[HBM_SPEC] * n,
        out_shape=[jax.ShapeDtypeStruct(p.shape, p.dtype) for p in ps],
        scratch_shapes=[pltpu.SemaphoreType.DMA((3 * n,)), pltpu.SemaphoreType.DMA((3 * n,))],
    )(*ps)


def _sum_slabs(ps, qs, *, name, tm=128):
    n = len(qs)
    _, m, cdim = qs[0].shape
    tm = min(tm, m)
    nb = m // tm
    assert m % tm == 0, (m, tm)
    where = jnp.stack([2 * lax.axis_index("x") + lax.axis_index("y"), lax.axis_index("c")]).astype(jnp.int32)

    def body(w_ref, *refs):
        for k in range(n):
            own, q1, q2, q3 = (refs[4 * k + t][...].astype(f32) for t in range(4))
            refs[4 * n + k][...] = ((own + q1) + q2) + q3

    def slab(flip):
        return pl.BlockSpec((None, tm, cdim), lambda i, w_ref: (jnp.bitwise_xor(w_ref[0], flip), i, 0))

    operands = []
    for p, q in zip(ps, qs):
        operands += [p, q, q, q]
    return _pcall(
        body, name=name,
        grid_spec=pltpu.PrefetchScalarGridSpec(
            num_scalar_prefetch=1, grid=(nb,),
            in_specs=[slab(0), slab(2), slab(1), slab(3)] * n,
            out_specs=[pl.BlockSpec((tm, cdim), lambda i, w_ref: (w_ref[1] * nb + i, 0))] * n),
        out_shape=[jax.ShapeDtypeStruct((2 * m, cdim), f32) for _ in qs],
        compiler_params=_params(1),
    )(where, *operands)


def _join_halves(fs, *, name):
    n = len(fs)

    def body(*refs):
        outs = refs[n:2 * n]
        send_sems, recv_sems = refs[2 * n:]
        x, y, c = _position()
        cps = []
        for k in range(n):
            m = fs[k].shape[0] // 2
            half = outs[k].at[pl.ds(pl.multiple_of(c * m, 8), m), :]
            cp = pltpu.make_async_remote_copy(src_ref=half, dst_ref=half, send_sem=send_sems.at[k],
                                              recv_sem=recv_sems.at[k], device_id=(x, y, 1 - c), device_id_type=MESH)
            cp.start()
            cps.append(cp)
        for cp in cps:
            cp.wait()

    return _pcall(
        body, name=name, in_specs=[HBM_SPEC] * n, out_specs=[HBM_SPEC] * n,
        out_shape=[jax.ShapeDtypeStruct(f.shape, f.dtype) for f in fs],
        input_output_aliases={k: k for k in range(n)},
        scratch_shapes=[pltpu.SemaphoreType.DMA((n,)), pltpu.SemaphoreType.DMA((n,))],
    )(*fs)


def _all_reduce_small(v, after=None, *, name):
    r = v.shape[0]
    extra = [] if after is None else [after]

    def body(v_ref, *refs):
        out_ref, buf, send_sems, recv_sems, local_sem = refs[len(extra):]
        x, y, c = _position()
        me, sibling = (x, y, c), (x, y, 1 - c)
        chips = _other_chips(x, y)

        def rows(px, py, pc):
            return buf.at[pl.ds(pl.multiple_of((4 * px + 2 * py + pc) * r, 8), r), :]

        def copy(k, block, to, src=None):
            return pltpu.make_async_remote_copy(
                src_ref=rows(*block) if src is None else src, dst_ref=rows(*block),
                send_sem=send_sems.at[k], recv_sem=recv_sems.at[k], device_id=to, device_id_type=MESH)

        mine = pltpu.make_async_copy(v_ref, rows(*me), local_sem)
        mine.start()
        first = [copy(0, me, sibling, src=v_ref)]
        first += [copy(1 + j, me, (*chip, c), src=v_ref) for j, chip in enumerate(chips)]
        for cp in first:
            cp.start()
        passed = [copy(4 + j, (*chip, c), sibling) for j, chip in enumerate(chips)]
        for j, chip in enumerate(chips):
            copy(1 + j, (*chip, c), me).wait_recv()
            passed[j].start()
        copy(0, sibling, me).wait_recv()
        for j, chip in enumerate(chips):
            copy(4 + j, (*chip, 1 - c), me).wait_recv()
        for cp in first + passed:
            cp.wait_send()
        mine.wait()
        acc = buf[0:r, :]
        for d in range(1, N_DEV):
            acc = acc + buf[d * r:(d + 1) * r, :]
        out_ref[...] = acc

    return _pcall(
        body, name=name, in_specs=[VMEM_SPEC] + [HBM_SPEC] * len(extra), out_specs=VMEM_SPEC,
        out_shape=jax.ShapeDtypeStruct((r, LANES), f32),
        scratch_shapes=[pltpu.VMEM((N_DEV * r, LANES), f32), pltpu.SemaphoreType.DMA((7,)),
                        pltpu.SemaphoreType.DMA((7,)), pltpu.SemaphoreType.DMA],
    )(v, *extra)


SEM_SPEC = pl.BlockSpec(memory_space=pltpu.SEMAPHORE)
HBM_ONLY = pl.BlockSpec(memory_space=pltpu.HBM)
EFFECT = pltpu.SideEffectType.DATAFLOW_SIDE_EFFECTING


def _split_start(bufs, copies_fn, n_sems, *, name):
    n = len(bufs)

    def body(*refs):
        send_sems, recv_sems = refs[n], refs[n + 1]
        thru = refs[n + 2:2 * n + 2]
        token = refs[2 * n + 2]
        for cp in copies_fn(thru, send_sems, recv_sems):
            cp.start()
        token[...] = jnp.zeros_like(token)

    outs = _pcall(
        body, name=name,
        out_shape=(pltpu.SemaphoreType.DMA((n_sems,)), pltpu.SemaphoreType.DMA((n_sems,)),
                   *[pltpu.HBM(b.shape, b.dtype) for b in bufs], jax.ShapeDtypeStruct((8, LANES), f32)),
        in_specs=[HBM_ONLY] * n,
        out_specs=(SEM_SPEC, SEM_SPEC, *[HBM_ONLY] * n, VMEM_SPEC),
        input_output_aliases={k: 2 + k for k in range(n)},
        compiler_params=pltpu.CompilerParams(has_side_effects=EFFECT),
    )(*[pltpu.with_memory_space_constraint(b, pltpu.HBM) for b in bufs])
    return outs[0], outs[1], list(outs[2:2 + n]), outs[2 + n]


def _split_wait(thru, send_sems, recv_sems, after, copies_fn, *, name):
    n = len(thru)

    def body(*refs):
        for cp in copies_fn(refs[:n], refs[n], refs[n + 1]):
            cp.wait_send()
            cp.wait_recv()

    return list(_pcall(
        body, name=name,
        out_shape=tuple(pltpu.HBM(b.shape, b.dtype) for b in thru),
        in_specs=[HBM_ONLY] * n + [SEM_SPEC, SEM_SPEC] + [HBM_SPEC] * len(after),
        out_specs=tuple([HBM_ONLY] * n),
        input_output_aliases={k: k for k in range(n)},
        compiler_params=pltpu.CompilerParams(has_side_effects=EFFECT),
    )(*thru, send_sems, recv_sems, *after))


def _scatter_copies(n):
    def copies(bufs, send_sems, recv_sems):
        x, y, c = _position()
        me_chip = 2 * x + y
        cps = []
        for k in range(n):
            for j, (px, py) in enumerate(_other_chips(x, y)):
                cps.append(pltpu.make_async_remote_copy(
                    src_ref=bufs[k].at[2 * px + py], dst_ref=bufs[n + k].at[me_chip],
                    send_sem=send_sems.at[3 * k + j], recv_sem=recv_sems.at[3 * k + j],
                    device_id=(px, py, c), device_id_type=MESH))
        return cps
    return copies


def _block_rows(buf, px, py, pc):
    m = buf.shape[0] // N_DEV
    return buf.at[pl.ds(pl.multiple_of((4 * px + 2 * py + pc) * m, 16), m), :]


def _gather_ici_copies(n):
    def copies(bufs, send_sems, recv_sems):
        x, y, c = _position()
        cps = []
        for k in range(n):
            rows = _block_rows(bufs[k], x, y, c)
            targets = [(x, y, 1 - c)] + [(px, py, c) for px, py in _other_chips(x, y)]
            for j, to in enumerate(targets):
                cps.append(pltpu.make_async_remote_copy(
                    src_ref=rows, dst_ref=rows, send_sem=send_sems.at[4 * k + j], recv_sem=recv_sems.at[4 * k + j],
                    device_id=to, device_id_type=MESH))
        return cps
    return copies


def _gather_d2d_copies(n):
    def copies(bufs, send_sems, recv_sems):
        x, y, c = _position()
        cps = []
        for k in range(n):
            for j, (px, py) in enumerate(_other_chips(x, y)):
                rows = _block_rows(bufs[k], px, py, c)
                cps.append(pltpu.make_async_remote_copy(
                    src_ref=rows, dst_ref=rows, send_sem=send_sems.at[3 * k + j], recv_sem=recv_sems.at[3 * k + j],
                    device_id=(x, y, 1 - c), device_id_type=MESH))
        return cps
    return copies


def _cast_halves(shards, after, *, name, tm=256):
    n = len(shards)
    r, cdim = shards[0].shape
    m = r // 2
    tm = min(tm, m)
    nb = m // tm
    assert m % tm == 0, (m, tm)
    where = jnp.stack([2 * lax.axis_index("x") + lax.axis_index("y"), lax.axis_index("c")]).astype(jnp.int32)

    def body(w_ref, *refs):
        for k in range(n):
            refs[n + 1 + k][...] = refs[k][...].astype(refs[n + 1 + k].dtype)

    return _pcall(
        body, name=name,
        grid_spec=pltpu.PrefetchScalarGridSpec(
            num_scalar_prefetch=1, grid=(nb,),
            in_specs=[pl.BlockSpec((tm, cdim), lambda i, w_ref: (w_ref[1] * nb + i, 0))] * n + [HBM_SPEC],
            out_specs=[pl.BlockSpec((tm, cdim), lambda i, w_ref: ((2 * w_ref[0] + w_ref[1]) * nb + i, 0))] * n),
        out_shape=[jax.ShapeDtypeStruct((N_SHARD * r, cdim), MXU_DTYPE) for _ in shards],
        compiler_params=_params(1),
    )(where, *shards, after)


class _Overlap(_NoOverlap):
    def __init__(self, ffn2_shards, after):
        halves = _cast_halves(ffn2_shards, after, name="ag2_cast")
        self.n = len(halves)
        self.ici = _split_start(halves, _gather_ici_copies(self.n), 4 * self.n, name="ag2_ici_start")
        self.reduced = None

    def start_token(self):
        return self.ici[3]

    def after_attention(self, after):
        send_sems, recv_sems, thru, _ = self.ici
        landed = _split_wait(thru, send_sems, recv_sems, after, _gather_ici_copies(self.n), name="ag2_ici_wait")
        self.d2d = _split_start(landed, _gather_d2d_copies(self.n), 3 * self.n, name="ag2_d2d_start")
        return self.d2d[3]

    def ffn2_weights(self, w, after):
        send_sems, recv_sems, thru, _ = self.d2d
        full = _split_wait(thru, send_sems, recv_sems, after, _gather_d2d_copies(self.n), name="ag2_d2d_wait")
        fs = D_FF // N_SHARD
        return (full[0].reshape(N_SHARD, D_MODEL, fs), full[1].reshape(N_SHARD, D_MODEL, fs),
                full[2].reshape(N_SHARD, fs, D_MODEL))

    def ffn2_grads(self, grads):
        recvs = _swap_halves(grads, name="rs_swap_ffn2")
        ps = _add_halves(grads, recvs, name="rs_add_ffn2")
        lands = [lax.empty(p.shape, p.dtype) for p in ps]
        self.scatter = _split_start(list(ps) + lands, _scatter_copies(len(ps)), 3 * len(ps), name="rs_scatter_ffn2_start")
        return self.scatter[3]

    def mixer_grads(self, dwp, dwo, small):
        self.small_sum = _all_reduce_small(_pack_small(small), name="ar_small")
        gwin = dwp[:, :IN_COLS].reshape(D_MODEL, N_SHARD, IN_SHARD).transpose(1, 0, 2).astype(GRAD_DTYPE)
        gwo = dwo.reshape(N_SHARD, D_MODEL // N_SHARD, D_MODEL).astype(GRAD_DTYPE)
        recvs = _swap_halves([gwin, gwo], name="rs_swap_mix")
        ps = [_add_halves([g], [r], name=f"rs_add_{tag}")[0] for g, r, tag in zip([gwin, gwo], recvs, ["w_in", "w_out"])]
        lands = [lax.empty(p.shape, p.dtype) for p in ps]
        self.scatter_mix = _split_start(ps + lands, _scatter_copies(2), 6, name="rs_scatter_mix_start")
        return self.scatter_mix[3]

    def mixer_reduced(self, after):
        send_sems, recv_sems, thru, _ = self.scatter_mix
        done = _split_wait(thru, send_sems, recv_sems, after, _scatter_copies(2), name="rs_scatter_mix_wait")
        return [_sum_slabs([done[k]], [done[2 + k]], name=f"rs_sum_{tag}")[0] for k, tag in enumerate(["w_in", "w_out"])]

    def before_ffn1_bwd(self, after):
        send_sems, recv_sems, thru, _ = self.scatter
        n = len(thru) // 2
        done = _split_wait(thru, send_sems, recv_sems, after, _scatter_copies(n), name="rs_scatter_ffn2_wait")
        self.reduced = list(_sum_slabs(done[:n], done[n:], name="rs_sum_ffn2"))


def _adamw(gs, ws, ms, vs, *, name, tm=256):
    n = len(gs)
    r, cdim = gs[0].shape
    tm = r if tm is None else min(tm, r)
    assert r % tm == 0, (r, tm)
    c1 = 1.0 / (1.0 - ADAM_B1 ** ADAM_STEP)
    c2 = 1.0 / (1.0 - ADAM_B2 ** ADAM_STEP)

    def body(*refs):
        for k in range(n):
            g = refs[k][...]
            w = refs[n + k][...]
            m = ADAM_B1 * refs[2 * n + k][...] + (1.0 - ADAM_B1) * g
            v = ADAM_B2 * refs[3 * n + k][...] + (1.0 - ADAM_B2) * (g * g)
            refs[4 * n + k][...] = -ADAM_LR * ((m * c1) / (jnp.sqrt(v * c2) + ADAM_EPS) + ADAM_WD * w)
            refs[5 * n + k][...] = m
            refs[6 * n + k][...] = v

    spec = pl.BlockSpec((tm, cdim), lambda i: (i, 0))
    outs = _pcall(
        body, name=name, grid=(r // tm,), in_specs=[spec] * (4 * n), out_specs=[spec] * (3 * n),
        out_shape=[jax.ShapeDtypeStruct((r, cdim), f32)] * (3 * n),
        compiler_params=_params(1),
    )(*gs, *ws, *ms, *vs)
    return outs[:n], outs[n:2 * n], outs[2 * n:]


BIG = ["ffn1_w_gate", "ffn1_w_up", "ffn1_w_down", "ffn2_w_gate", "ffn2_w_up", "ffn2_w_down"]
SMALL = ["ln1_g", "ln1_b", "b_forget", "conv_w", "conv_b", "rg_wa", "rg_ba", "rg_wx", "rg_bx", "lru_lambda",
         "ln2_g", "ln2_b", "ln3_g", "ln3_b"]
WEIGHTS = ["ffn1_w_gate", "ffn1_w_up", "ffn1_w_down", "ln1_g", "ln1_b", "w_in", "b_forget", "conv_w", "conv_b",
           "rg_wa", "rg_ba", "rg_wx", "rg_bx", "lru_lambda", "w_out", "ln2_g", "ln2_b",
           "ffn2_w_gate", "ffn2_w_up", "ffn2_w_down", "ln3_g", "ln3_b"]


def _pack_small(parts):
    rows = []
    for n in SMALL:
        flat = parts[n].reshape(-1)
        pad = (-flat.shape[0]) % LANES
        rows.append(jnp.pad(flat, (0, pad)).reshape(-1, LANES))
    packed = jnp.concatenate(rows, axis=0)
    return jnp.pad(packed, ((0, (-packed.shape[0]) % 8), (0, 0)))


def _unpack_small(packed, shapes):
    out, r0 = {}, 0
    for n in SMALL:
        size = math.prod(shapes[n])
        nr = -(-size // LANES)
        out[n] = packed[r0:r0 + nr].reshape(-1)[:size].reshape(shapes[n])
        r0 += nr
    return out


def kernel(x, ffn1_w_gate, ffn1_w_up, ffn1_w_down, ln1_g, ln1_b, w_in, b_forget, conv_w, conv_b, rg_wa, rg_ba, rg_wx, rg_bx, lru_lambda, w_out, ln2_g, ln2_b, ffn2_w_gate, ffn2_w_up, ffn2_w_down, ln3_g, ln3_b, loss_target, m_ffn1_w_gate, m_ffn1_w_up, m_ffn1_w_down, m_ln1_g, m_ln1_b, m_w_in, m_b_forget, m_conv_w, m_conv_b, m_rg_wa, m_rg_ba, m_rg_wx, m_rg_bx, m_lru_lambda, m_w_out, m_ln2_g, m_ln2_b, m_ffn2_w_gate, m_ffn2_w_up, m_ffn2_w_down, m_ln3_g, m_ln3_b, v_ffn1_w_gate, v_ffn1_w_up, v_ffn1_w_down, v_ln1_g, v_ln1_b, v_w_in, v_b_forget, v_conv_w, v_conv_b, v_rg_wa, v_rg_ba, v_rg_wx, v_rg_bx, v_lru_lambda, v_w_out, v_ln2_g, v_ln2_b, v_ffn2_w_gate, v_ffn2_w_up, v_ffn2_w_down, v_ln3_g, v_ln3_b):
    args = dict(locals())
    w = {n: args[n] for n in WEIGHTS}
    mom = {n: args["m_" + n] for n in WEIGHTS}
    var = {n: args["v_" + n] for n in WEIGHTS}
    chip = 2 * lax.axis_index("x") + lax.axis_index("y")

    g1 = _all_gather_bf16([w[n][0] for n in BIG[:3]] + [w["w_in"][0], w["w_out"][0]], name="ag_first")
    fs = D_FF // N_SHARD
    w_in_full = g1[3].reshape(N_SHARD, D_MODEL, IN_SHARD).transpose(1, 0, 2).reshape(D_MODEL, IN_COLS)
    full = dict(
        f1g=g1[0].reshape(N_SHARD, D_MODEL, fs), f1u=g1[1].reshape(N_SHARD, D_MODEL, fs),
        f1d=g1[2].reshape(N_SHARD, fs, D_MODEL),
        wp=make_wp(w_in_full), bfp=jnp.pad(b_forget, ((0, 0), (0, LANES - HEADS))), wo=g1[4],
        ln1_g=ln1_g, ln1_b=ln1_b, ln2_g=ln2_g, ln2_b=ln2_b, ln3_g=ln3_g, ln3_b=ln3_b,
        conv_b=conv_b, rg_wa=rg_wa[0], rg_wx=rg_wx[0], rg_ba=rg_ba[0], rg_bx=rg_bx[0], lam=lru_lambda,
    )
    cw_place = lax.dynamic_update_slice(jnp.zeros((8, LRU_W), f32), conv_w[0] * 0.5, (0, chip * (LRU_W // N_SHARD)))
    cw_full = _all_reduce_small(cw_place.reshape(-1, LANES), g1[0], name="ag_conv_w")
    full["conv_w"] = cw_full.reshape(8, LRU_W)[:CONV_K]

    hooks = _Overlap([w[n][0] for n in BIG[3:]], cw_full)
    loss_rep, dx, g = _local_step(x[0], loss_target[0], full, hooks)
    loss = lax.psum(loss_rep[0, 0], ("x", "y", "c"))

    gs1 = [g["f1g"], g["f1u"], g["f1d"]]
    ps1 = _add_halves(gs1, _swap_halves(gs1, name="rs_swap_ffn1"), name="rs_add_ffn1")
    lands = [lax.empty(p.shape, p.dtype) for p in ps1]
    send1, recv1, thru1, token1 = _split_start(list(ps1) + lands, _scatter_copies(3), 9, name="rs_scatter_ffn1_start")
    red = _join_halves(hooks.reduced + hooks.mixer_reduced([token1]), name="rs_join_rest")
    grads = dict(zip(BIG[3:] + ["w_in", "w_out"], red))

    small_shapes = {n: w[n].shape for n in SMALL}
    small_shapes["conv_w"] = (1, CONV_K, LRU_W)
    gs_red = _unpack_small(hooks.small_sum, small_shapes)
    gs_red["conv_w"] = lax.dynamic_slice(gs_red["conv_w"], (0, 0, chip * (LRU_W // N_SHARD)),
                                         (1, CONV_K, LRU_W // N_SHARD))
    grads.update(gs_red)

    delta, new_m, new_v = {}, {}, {}

    def adamw(names, name, **kw):
        d, nm, nv = _adamw([grads[n] for n in names], [w[n][0] for n in names], [mom[n][0] for n in names],
                           [var[n][0] for n in names], name=name, **kw)
        for i, n in enumerate(names):
            delta[n], new_m[n], new_v[n] = d[i], nm[i], nv[i]

    adamw(BIG[3:], "adamw_ffn2", tm=128)
    adamw(["w_in"], "adamw_w_in")
    adamw(["w_out"], "adamw_w_out")
    shard_shapes = {n: w[n].shape for n in SMALL}
    d, nm, nv = _adamw([_pack_small({n: grads[n] for n in SMALL})], [_pack_small({n: w[n] for n in SMALL})],
                       [_pack_small({n: mom[n] for n in SMALL})], [_pack_small({n: var[n] for n in SMALL})],
                       name="adamw_small", tm=None)
    for dst, packed in ((delta, d[0]), (new_m, nm[0]), (new_v, nv[0])):
        dst.update(_unpack_small(packed, shard_shapes))

    worked = [new_v["ffn2_w_down"], new_v["w_in"], new_v["w_out"], nv[0]]
    done1 = _split_wait(thru1, send1, recv1, worked, _scatter_copies(3), name="rs_scatter_ffn1_wait")
    red1 = _join_halves(list(_sum_slabs(done1[:3], done1[3:], name="rs_sum_ffn1")), name="rs_join_ffn1")
    grads.update(zip(BIG[:3], red1))
    adamw(BIG[:3], "adamw_ffn1", tm=128)

    def shaped(tree, n):
        return tree[n].reshape(w[n].shape)

    return (loss, dx[None], *[shaped(grads, n) for n in WEIGHTS], *[shaped(delta, n) for n in WEIGHTS],
            *[shaped(new_m, n) for n in WEIGHTS], *[shaped(new_v, n) for n in WEIGHTS])
```

```python
import functools
import math

import jax
import jax.numpy as jnp
from jax import lax
from jax.experimental import pallas as pl
from jax.experimental.pallas import tpu as pltpu

f32 = jnp.float32
MXU_DTYPE = jnp.bfloat16
GRAD_DTYPE = jnp.bfloat16

D_MODEL = 1024
D_FF = 4096
N_SHARD = 4
N_DEV = 8
FOX_W = 512
LRU_W = 512
HEADS = 8
HEAD_DIM = 64
CONV_K = 4
IN_COLS = 2568
IN_SHARD = IN_COLS // N_SHARD
QKV_W = 3 * FOX_W
Z_PAD = 2688
LANES = 128
LN_EPS = 1e-5
DN_ALPHA = 2.0 ** 0.25
LRU_C = 8.0
NEG_BIG = -1e30
VMEM_LIMIT = 56 * 1024 * 1024

ADAM_LR = 0.001
ADAM_B1 = 0.9
ADAM_B2 = 0.999
ADAM_EPS = 1e-08
ADAM_WD = 0.01
ADAM_STEP = 10


def _pcall(body, **kw):
    return pl.pallas_call(body, **kw)


def _params(n_grid, vmem=VMEM_LIMIT):
    return pltpu.CompilerParams(dimension_semantics=("arbitrary",) * n_grid, vmem_limit_bytes=vmem)


def _dot(a, b):
    return jnp.dot(a, b, preferred_element_type=f32)


def _dot_nt(a, b):
    return lax.dot_general(a, b, (((1,), (1,)), ((), ())), preferred_element_type=f32)


def _dot_tn(a, b):
    return lax.dot_general(a, b, (((0,), (0,)), ((), ())), preferred_element_type=f32)


def _sigmoid(x):
    return 1.0 / (1.0 + jnp.exp(-x))


def _layer_norm_stats(y):
    mu = jnp.mean(y, axis=-1, keepdims=True)
    yc = y - mu
    var = jnp.mean(yc * yc, axis=-1, keepdims=True)
    rstd = lax.rsqrt(var + LN_EPS)
    return yc * rstd, rstd


def _ln_backward(dy, xhat, rstd, gamma):
    dxhat = dy * gamma
    m1 = jnp.mean(dxhat, axis=-1, keepdims=True)
    m2 = jnp.mean(dxhat * xhat, axis=-1, keepdims=True)
    dyp = rstd * (dxhat - m1 - xhat * m2)
    return dyp, jnp.sum(dy * xhat, axis=0, keepdims=True), jnp.sum(dy, axis=0, keepdims=True)


def _ffn_fwd(x, wg, wu, wd, ln_g, ln_b, *, name, tm=1024, tf=512):
    T = x.shape[0]
    tm = min(tm, T)
    fs = D_FF // N_SHARD
    cpf = fs // tf
    nf = D_FF // tf
    nt = T // tm

    def body(x_ref, wg_ref, wu_ref, wd_ref, g_ref, b_ref,
             xb_ref, gact_ref, uact_ref, xhat_ref, xn_ref, rstd_ref, acc_ref):
        f = pl.program_id(1)

        @pl.when(f == 0)
        def _():
            xb_ref[...] = x_ref[...].astype(MXU_DTYPE)
            acc_ref[...] = jnp.zeros_like(acc_ref)

        xb = xb_ref[...]
        g = _dot(xb, wg_ref[...])
        u = _dot(xb, wu_ref[...])
        h = (g * _sigmoid(g)) * u
        gact_ref[...] = g.astype(gact_ref.dtype)
        uact_ref[...] = u.astype(uact_ref.dtype)
        acc_ref[...] += _dot(h.astype(MXU_DTYPE), wd_ref[...])

        @pl.when(f == nf - 1)
        def _():
            y = DN_ALPHA * x_ref[...] + 0.5 * acc_ref[...]
            xhat, rstd = _layer_norm_stats(y)
            xhat_ref[...] = xhat
            xn_ref[...] = (xhat * g_ref[...] + b_ref[...]).astype(xn_ref.dtype)
            rstd_ref[...] = jnp.broadcast_to(rstd, rstd_ref.shape)

    row = lambda i, f: (i, 0)
    return _pcall(
        body, name=name, grid=(nt, nf),
        in_specs=[
            pl.BlockSpec((tm, D_MODEL), row),
            pl.BlockSpec((None, D_MODEL, tf), lambda i, f: (f // cpf, 0, f % cpf)),
            pl.BlockSpec((None, D_MODEL, tf), lambda i, f: (f // cpf, 0, f % cpf)),
            pl.BlockSpec((None, tf, D_MODEL), lambda i, f: (f // cpf, f % cpf, 0)),
            pl.BlockSpec((1, D_MODEL), lambda i, f: (0, 0)),
            pl.BlockSpec((1, D_MODEL), lambda i, f: (0, 0)),
        ],
        out_specs=[
            pl.BlockSpec((tm, D_MODEL), row),
            pl.BlockSpec((tm, tf), lambda i, f: (i, f)),
            pl.BlockSpec((tm, tf), lambda i, f: (i, f)),
            pl.BlockSpec((tm, D_MODEL), row),
            pl.BlockSpec((tm, D_MODEL), row),
            pl.BlockSpec((tm, LANES), row),
        ],
        out_shape=[
            jax.ShapeDtypeStruct((T, D_MODEL), MXU_DTYPE),
            jax.ShapeDtypeStruct((T, D_FF), MXU_DTYPE),
            jax.ShapeDtypeStruct((T, D_FF), MXU_DTYPE),
            jax.ShapeDtypeStruct((T, D_MODEL), f32),
            jax.ShapeDtypeStruct((T, D_MODEL), MXU_DTYPE),
            jax.ShapeDtypeStruct((T, LANES), f32),
        ],
        scratch_shapes=[pltpu.VMEM((tm, D_MODEL), f32)],
        compiler_params=_params(2),
    )(x, wg, wu, wd, ln_g, ln_b)


def _ffn_bwd(dyp, xb, gact, uact, wg, wu, wd, after=None, *, name, tm=512, tf=512):
    T = dyp.shape[0]
    tm = min(tm, T)
    fs = D_FF // N_SHARD
    cpf = fs // tf
    nf = D_FF // tf
    nt = T // tm
    extra = [] if after is None else [after]

    def body(dyp_ref, xb_ref, g_ref, u_ref, wg_ref, wu_ref, wd_ref, *refs):
        dx_hbm, dwg_ref, dwu_ref, dwd_ref, dx_sc, dwg_sc, dwu_sc, dwd_sc, sem = refs[len(extra):]
        f = pl.program_id(0)
        i = pl.program_id(1)
        rows = pl.ds(pl.multiple_of(i * tm, tm), tm)
        dyp_t = dyp_ref[...]
        dy = (0.5 * dyp_t).astype(MXU_DTYPE)

        @pl.when(i == 0)
        def _():
            dwg_sc[...] = jnp.zeros_like(dwg_sc)
            dwu_sc[...] = jnp.zeros_like(dwu_sc)
            dwd_sc[...] = jnp.zeros_like(dwd_sc)

        @pl.when(f == 0)
        def _():
            dx_sc[rows, :] = DN_ALPHA * dyp_t

        g = g_ref[...].astype(f32)
        u = u_ref[...].astype(f32)
        sig = _sigmoid(g)
        silu = g * sig
        dh = _dot_nt(dy, wd_ref[...])
        dg = (dh * u * (sig * (1.0 + g * (1.0 - sig)))).astype(MXU_DTYPE)
        du = (dh * silu).astype(MXU_DTYPE)
        hb = (silu * u).astype(MXU_DTYPE)
        dx_sc[rows, :] += _dot_nt(dg, wg_ref[...]) + _dot_nt(du, wu_ref[...])
        xb_t = xb_ref[...]
        dwg_sc[...] += _dot_tn(xb_t, dg)
        dwu_sc[...] += _dot_tn(xb_t, du)
        dwd_sc[...] += _dot_tn(hb, dy)

        @pl.when(i == nt - 1)
        def _():
            dwg_ref[...] = dwg_sc[...].astype(dwg_ref.dtype)
            dwu_ref[...] = dwu_sc[...].astype(dwu_ref.dtype)
            dwd_ref[...] = dwd_sc[...].astype(dwd_ref.dtype)

        @pl.when(jnp.logical_and(f == nf - 1, i == nt - 1))
        def _():
            cp = pltpu.make_async_copy(dx_sc, dx_hbm, sem)
            cp.start()
            cp.wait()

    row = lambda f, i: (i, 0)
    return _pcall(
        body, name=name, grid=(nf, nt),
        in_specs=[
            pl.BlockSpec((tm, D_MODEL), row),
            pl.BlockSpec((tm, D_MODEL), row),
            pl.BlockSpec((tm, tf), lambda f, i: (i, f)),
            pl.BlockSpec((tm, tf), lambda f, i: (i, f)),
            pl.BlockSpec((None, D_MODEL, tf), lambda f, i: (f // cpf, 0, f % cpf)),
            pl.BlockSpec((None, D_MODEL, tf), lambda f, i: (f // cpf, 0, f % cpf)),
            pl.BlockSpec((None, tf, D_MODEL), lambda f, i: (f // cpf, f % cpf, 0)),
        ] + [pl.BlockSpec(memory_space=pl.ANY)] * len(extra),
        out_specs=[
            pl.BlockSpec(memory_space=pl.ANY),
            pl.BlockSpec((None, D_MODEL, tf), lambda f, i: (f // cpf, 0, f % cpf)),
            pl.BlockSpec((None, D_MODEL, tf), lambda f, i: (f // cpf, 0, f % cpf)),
            pl.BlockSpec((None, tf, D_MODEL), lambda f, i: (f // cpf, f % cpf, 0)),
        ],
        out_shape=[
            jax.ShapeDtypeStruct((T, D_MODEL), f32),
            jax.ShapeDtypeStruct((N_SHARD, D_MODEL, fs), GRAD_DTYPE),
            jax.ShapeDtypeStruct((N_SHARD, D_MODEL, fs), GRAD_DTYPE),
            jax.ShapeDtypeStruct((N_SHARD, fs, D_MODEL), GRAD_DTYPE),
        ],
        scratch_shapes=[pltpu.VMEM((T, D_MODEL), f32), pltpu.VMEM((D_MODEL, tf), f32),
                        pltpu.VMEM((D_MODEL, tf), f32), pltpu.VMEM((tf, D_MODEL), f32),
                        pltpu.SemaphoreType.DMA],
        compiler_params=_params(2),
    )(dyp, xb, gact, uact, wg, wu, wd, *extra)


def _loss_ln_bwd(xhat, rstd, ln_g, ln_b, target, *, name, tm=512):
    T = xhat.shape[0]
    tm = min(tm, T)
    nt = T // tm

    def body(xhat_ref, rstd_ref, g_ref, b_ref, t_ref, dyp_ref, dg_ref, db_ref, loss_ref):
        i = pl.program_id(0)

        @pl.when(i == 0)
        def _():
            dg_ref[...] = jnp.zeros_like(dg_ref)
            db_ref[...] = jnp.zeros_like(db_ref)
            loss_ref[...] = jnp.zeros_like(loss_ref)

        xhat_t = xhat_ref[...]
        gamma = g_ref[...]
        err = xhat_t * gamma + b_ref[...] - t_ref[...]
        sq = jnp.sum(jnp.sum(err * err, axis=0, keepdims=True), axis=1, keepdims=True)
        loss_ref[...] += jnp.broadcast_to(sq * (0.5 / D_MODEL), loss_ref.shape)
        dy = err * (1.0 / D_MODEL)
        dyp, dgam, dbeta = _ln_backward(dy, xhat_t, rstd_ref[:, 0:1], gamma)
        dyp_ref[...] = dyp
        dg_ref[...] += dgam
        db_ref[...] += dbeta

    row = lambda i: (i, 0)
    const = lambda i: (0, 0)
    return _pcall(
        body, name=name, grid=(nt,),
        in_specs=[pl.BlockSpec((tm, D_MODEL), row), pl.BlockSpec((tm, LANES), row),
                  pl.BlockSpec((1, D_MODEL), const), pl.BlockSpec((1, D_MODEL), const),
                  pl.BlockSpec((tm, D_MODEL), row)],
        out_specs=[pl.BlockSpec((tm, D_MODEL), row), pl.BlockSpec((1, D_MODEL), const),
                   pl.BlockSpec((1, D_MODEL), const), pl.BlockSpec((1, LANES), const)],
        out_shape=[jax.ShapeDtypeStruct((T, D_MODEL), f32), jax.ShapeDtypeStruct((1, D_MODEL), f32),
                   jax.ShapeDtypeStruct((1, D_MODEL), f32), jax.ShapeDtypeStruct((1, LANES), f32)],
        compiler_params=_params(1),
    )(xhat, rstd, ln_g, ln_b, target)


def _ln_bwd(dy, xhat, rstd, ln_g, *, name, tm=512):
    T = xhat.shape[0]
    tm = min(tm, T)
    nt = T // tm

    def body(dy_ref, xhat_ref, rstd_ref, g_ref, dyp_ref, dg_ref, db_ref):
        i = pl.program_id(0)

        @pl.when(i == 0)
        def _():
            dg_ref[...] = jnp.zeros_like(dg_ref)
            db_ref[...] = jnp.zeros_like(db_ref)

        dyp, dgam, dbeta = _ln_backward(dy_ref[...], xhat_ref[...], rstd_ref[:, 0:1], g_ref[...])
        dyp_ref[...] = dyp
        dg_ref[...] += dgam
        db_ref[...] += dbeta

    row = lambda i: (i, 0)
    const = lambda i: (0, 0)
    return _pcall(
        body, name=name, grid=(nt,),
        in_specs=[pl.BlockSpec((tm, D_MODEL), row), pl.BlockSpec((tm, D_MODEL), row),
                  pl.BlockSpec((tm, LANES), row), pl.BlockSpec((1, D_MODEL), const)],
        out_specs=[pl.BlockSpec((tm, D_MODEL), row), pl.BlockSpec((1, D_MODEL), const),
                   pl.BlockSpec((1, D_MODEL), const)],
        out_shape=[jax.ShapeDtypeStruct((T, D_MODEL), f32), jax.ShapeDtypeStruct((1, D_MODEL), f32),
                   jax.ShapeDtypeStruct((1, D_MODEL), f32)],
        compiler_params=_params(1),
    )(dy, xhat, rstd, ln_g)


def _proj_in(xn, wp, bfp, *, name, tm=512):
    T = xn.shape[0]
    tm = min(tm, T)
    nt = T // tm

    def body(x_ref, w_ref, b_ref, qkv_ref, lxg_ref, fg_ref):
        z = _dot(x_ref[...], w_ref[...])
        qkv_ref[...] = z[:, :QKV_W].astype(qkv_ref.dtype)
        lxg_ref[...] = z[:, QKV_W:QKV_W + 2 * LRU_W]
        fg_ref[...] = z[:, QKV_W + 2 * LRU_W:] + b_ref[...]

    row = lambda i: (i, 0)
    const = lambda i: (0, 0)
    return _pcall(
        body, name=name, grid=(nt,),
        in_specs=[pl.BlockSpec((tm, D_MODEL), row), pl.BlockSpec((D_MODEL, Z_PAD), const),
                  pl.BlockSpec((1, LANES), const)],
        out_specs=[pl.BlockSpec((tm, QKV_W), row), pl.BlockSpec((tm, 2 * LRU_W), row),
                   pl.BlockSpec((tm, LANES), row)],
        out_shape=[jax.ShapeDtypeStruct((T, QKV_W), MXU_DTYPE), jax.ShapeDtypeStruct((T, 2 * LRU_W), f32),
                   jax.ShapeDtypeStruct((T, LANES), f32)],
        compiler_params=_params(1),
    )(xn, wp, bfp)


def _proj_in_bwd(dqa, dka, dva, dlxg, dfg, xn, dyp, wp, *, name, tm=512):
    T = xn.shape[0]
    tm = min(tm, T)
    nt = T // tm

    def body(dq_ref, dk_ref, dv_ref, dl_ref, dfg_ref, x_ref, dyp_ref, w_ref, dx_ref, dw_hbm, dw_sc, sem):
        i = pl.program_id(0)

        @pl.when(i == 0)
        def _():
            dw_sc[...] = jnp.zeros_like(dw_sc)

        low = _low_lanes((tm, LANES))

        def packed(ref):
            pairs = [jnp.where(low, ref[:, (2 * j) * LANES:(2 * j + 1) * LANES],
                               _swap_lane_halves(ref[:, (2 * j + 1) * LANES:(2 * j + 2) * LANES]))
                     for j in range(HEADS // 2)]
            return jnp.concatenate(pairs, axis=1).astype(MXU_DTYPE)

        dz = jnp.concatenate(
            [packed(dq_ref), packed(dk_ref), packed(dv_ref),
             dl_ref[...].astype(MXU_DTYPE), dfg_ref[...].astype(MXU_DTYPE)], axis=1)
        dx_ref[...] = DN_ALPHA * dyp_ref[...] + _dot_nt(dz, w_ref[...])
        dw_sc[...] += _dot_tn(x_ref[...], dz)

        @pl.when(i == nt - 1)
        def _():
            dw_sc[:, :FOX_W] = dw_sc[:, :FOX_W] * (1.0 / math.sqrt(HEAD_DIM))
            cp = pltpu.make_async_copy(dw_sc, dw_hbm, sem)
            cp.start()
            cp.wait()

    row = lambda i: (i, 0)
    const = lambda i: (0, 0)
    return _pcall(
        body, name=name, grid=(nt,),
        in_specs=[pl.BlockSpec((tm, HEADS * LANES), row), pl.BlockSpec((tm, HEADS * LANES), row),
                  pl.BlockSpec((tm, HEADS * LANES), row),
                  pl.BlockSpec((tm, 2 * LRU_W), row), pl.BlockSpec((tm, LANES), row),
                  pl.BlockSpec((tm, D_MODEL), row), pl.BlockSpec((tm, D_MODEL), row),
                  pl.BlockSpec((D_MODEL, Z_PAD), const)],
        out_specs=[pl.BlockSpec((tm, D_MODEL), row), pl.BlockSpec(memory_space=pl.ANY)],
        out_shape=[jax.ShapeDtypeStruct((T, D_MODEL), f32), jax.ShapeDtypeStruct((D_MODEL, Z_PAD), f32)],
        scratch_shapes=[pltpu.VMEM((D_MODEL, Z_PAD), f32), pltpu.SemaphoreType.DMA],
        compiler_params=_params(1),
    )(dqa, dka, dva, dlxg, dfg, xn, dyp, wp)


def _split3(x):
    hi = x.astype(jnp.bfloat16)
    r1 = x - hi.astype(f32)
    mid = r1.astype(jnp.bfloat16)
    lo = (r1 - mid.astype(f32)).astype(jnp.bfloat16)
    return hi, mid, lo


def _tri_dot(tri, x):
    hi, mid, lo = _split3(x)
    return _dot(tri, hi) + _dot(tri, mid) + _dot(tri, lo)


FOX_PAD = HEADS * LANES
AUX = HEAD_DIM


def _low_lanes(shape):
    return lax.broadcasted_iota(jnp.int32, shape, 1) < HEAD_DIM


def _swap_lane_halves(x):
    return pltpu.roll(x, HEAD_DIM, 1)


def _fox_prep(qkv, fgb, *, name, tm=512):
    T = fgb.shape[0]
    tm = min(tm, T)
    nt = T // tm

    def body(qkv_ref, fg_ref, qa_ref, ka_ref, va_ref, carry):
        i = pl.program_id(0)

        @pl.when(i == 0)
        def _():
            carry[...] = jnp.zeros_like(carry)

        x = fg_ref[...]
        ls = jnp.minimum(x, 0.0) - jnp.log(1.0 + jnp.exp(-jnp.abs(x)))
        r = lax.broadcasted_iota(jnp.int32, (tm, tm), 0)
        c = lax.broadcasted_iota(jnp.int32, (tm, tm), 1)
        tri = jnp.where(r >= c, 1.0, 0.0).astype(jnp.bfloat16)
        cum = _tri_dot(tri, ls) + carry[0:1, :]
        carry[...] = jnp.broadcast_to(cum[tm - 1:tm, :], carry.shape)

        lane = lax.broadcasted_iota(jnp.int32, (tm, LANES), 1)
        low = lane < HEAD_DIM
        ones_q = jnp.where(jnp.logical_and(lane >= AUX + 3, lane < AUX + 6), 1.0, 0.0)
        ones_k = jnp.where(jnp.logical_and(lane >= AUX, lane < AUX + 3), 1.0, 0.0)
        for j in range(HEADS // 2):
            pair = [qkv_ref[:, t * FOX_W + j * LANES:t * FOX_W + (j + 1) * LANES].astype(f32) for t in range(3)]
            for odd in range(2):
                h = 2 * j + odd
                q, k, v = [_swap_lane_halves(a) if odd else a for a in pair]
                hi, mid, lo = [a.astype(f32) for a in _split3(jnp.broadcast_to(cum[:, h:h + 1], (tm, LANES)))]
                aux_q = jnp.where(lane == AUX, hi, jnp.where(lane == AUX + 1, mid, jnp.where(lane == AUX + 2, lo, ones_q)))
                aux_k = jnp.where(lane == AUX + 3, -hi,
                                  jnp.where(lane == AUX + 4, -mid, jnp.where(lane == AUX + 5, -lo, ones_k)))
                blk = slice(h * LANES, (h + 1) * LANES)
                qa_ref[:, blk] = jnp.where(low, q, aux_q).astype(qa_ref.dtype)
                ka_ref[:, blk] = jnp.where(low, k, aux_k).astype(ka_ref.dtype)
                va_ref[:, blk] = jnp.where(low, v, 1.0).astype(va_ref.dtype)

    row = lambda i: (i, 0)
    return _pcall(
        body, name=name, grid=(nt,),
        in_specs=[pl.BlockSpec((tm, QKV_W), row), pl.BlockSpec((tm, LANES), row)],
        out_specs=[pl.BlockSpec((tm, FOX_PAD), row)] * 3,
        out_shape=[jax.ShapeDtypeStruct((T, FOX_PAD), MXU_DTYPE)] * 3,
        scratch_shapes=[pltpu.VMEM((8, LANES), f32)],
        compiler_params=_params(1),
    )(qkv, fgb)


def _future_keys(tq, tk):
    r = lax.broadcasted_iota(jnp.int32, (tq, tk), 0)
    c = lax.broadcasted_iota(jnp.int32, (tq, tk), 1)
    return c > r


def _causal_steps(nq, key_major):
    if key_major:
        pairs = [(qi, ki) for ki in range(nq) for qi in range(ki, nq)]
    else:
        pairs = [(qi, ki) for qi in range(nq) for ki in range(qi + 1)]
    return (jnp.asarray([p[0] for p in pairs], jnp.int32), jnp.asarray([p[1] for p in pairs], jnp.int32))


def _fox_fwd(qa, ka, va, *, name, tq=512):
    T = qa.shape[0]
    tq = min(tq, T)
    tk = tq
    nq = T // tq
    rep = tk // LANES
    qi_tab, ki_tab = _causal_steps(nq, key_major=False)

    def body(qi_ref, ki_ref, qa_ref, ka_ref, va_ref, o_ref, lse_ref, m_sc, acc_sc):
        t = pl.program_id(1)
        qi = qi_ref[t]
        ki = ki_ref[t]

        @pl.when(ki == 0)
        def _():
            m_sc[...] = jnp.full_like(m_sc, NEG_BIG)
            acc_sc[...] = jnp.zeros_like(acc_sc)

        def tile(diagonal):
            for h in range(2):
                blk = slice(h * LANES, (h + 1) * LANES)
                s = _dot_nt(qa_ref[:, blk], ka_ref[:, blk])
                if diagonal:
                    s = jnp.where(_future_keys(tq, tk), NEG_BIG, s)
                m_prev = m_sc[h]
                m_new = jnp.maximum(m_prev, jnp.max(s, axis=1, keepdims=True))
                p = jnp.exp(s - jnp.tile(m_new, (1, rep)))
                acc_sc[h] = jnp.exp(m_prev - m_new) * acc_sc[h] + _dot(p.astype(MXU_DTYPE), va_ref[:, blk])
                m_sc[h] = m_new

        @pl.when(ki < qi)
        def _():
            tile(False)

        @pl.when(ki == qi)
        def _():
            tile(True)
            low = _low_lanes((tq, LANES))
            outs = []
            for h in range(2):
                acc = acc_sc[h]
                den = _swap_lane_halves(acc)
                outs.append(acc / den)
                lse_ref[h] = m_sc[h] + jnp.log(jnp.where(low, den, acc))
            o_ref[...] = jnp.where(low, outs[0], _swap_lane_halves(outs[1]))

    pair = 2 * LANES
    return _pcall(
        body, name=name,
        grid_spec=pltpu.PrefetchScalarGridSpec(
            num_scalar_prefetch=2, grid=(HEADS // 2, qi_tab.shape[0]),
            in_specs=[
                pl.BlockSpec((tq, pair), lambda j, t, qi_ref, ki_ref: (qi_ref[t], j)),
                pl.BlockSpec((tk, pair), lambda j, t, qi_ref, ki_ref: (ki_ref[t], j)),
                pl.BlockSpec((tk, pair), lambda j, t, qi_ref, ki_ref: (ki_ref[t], j)),
            ],
            out_specs=[pl.BlockSpec((tq, LANES), lambda j, t, qi_ref, ki_ref: (qi_ref[t], j)),
                       pl.BlockSpec((2, tq, LANES), lambda j, t, qi_ref, ki_ref: (j, qi_ref[t], 0))],
            scratch_shapes=[pltpu.VMEM((2, tq, LANES), f32)] * 2),
        out_shape=[jax.ShapeDtypeStruct((T, FOX_W), f32), jax.ShapeDtypeStruct((HEADS, T, LANES), f32)],
        compiler_params=_params(2),
    )(qi_tab, ki_tab, qa, ka, va)


def _fox_bwd_prep(do, o, *, name, tm=512):
    T = o.shape[0]
    tm = min(tm, T)
    nt = T // tm

    def body(do_ref, o_ref, d_ref, doa_ref):
        low = _low_lanes((tm, LANES))
        for j in range(HEADS // 2):
            do2 = do_ref[:, j * LANES:(j + 1) * LANES].astype(f32)
            prod = do2 * o_ref[:, j * LANES:(j + 1) * LANES]
            for odd in range(2):
                h = 2 * j + odd
                mine = jnp.where(low, _swap_lane_halves(prod) if odd else prod, 0.0)
                d_ref[h] = jnp.broadcast_to(jnp.sum(mine, axis=1, keepdims=True), (tm, LANES))
                doh = jnp.where(low, _swap_lane_halves(do2) if odd else do2, 0.0)
                doa_ref[:, h * LANES:(h + 1) * LANES] = doh.astype(doa_ref.dtype)

    return _pcall(
        body, name=name, grid=(nt,),
        in_specs=[pl.BlockSpec((tm, FOX_W), lambda i: (i, 0)), pl.BlockSpec((tm, FOX_W), lambda i: (i, 0))],
        out_specs=[pl.BlockSpec((HEADS, tm, LANES), lambda i: (0, i, 0)), pl.BlockSpec((tm, FOX_PAD), lambda i: (i, 0))],
        out_shape=[jax.ShapeDtypeStruct((HEADS, T, LANES), f32), jax.ShapeDtypeStruct((T, FOX_PAD), MXU_DTYPE)],
        compiler_params=_params(1),
    )(do, o)


def _fox_bwd(qa, ka, va, doa, lse, drep, *, name, tq=512):
    T = qa.shape[0]
    tq = min(tq, T)
    tk = tq
    nq = T // tq
    rep = tk // LANES
    qi_tab, ki_tab = _causal_steps(nq, key_major=True)

    def body(qi_ref, ki_ref, qa_ref, ka_ref, va_ref, doa_ref, lse_ref, d_ref, dqa_ref, dka_ref, dva_ref, dk_sc, dv_sc):
        t = pl.program_id(1)
        qi = qi_ref[t]
        ki = ki_ref[t]
        rows = pl.ds(pl.multiple_of(qi * tq, tq), tq)

        @pl.when(t == 0)
        def _():
            dqa_ref[...] = jnp.zeros_like(dqa_ref)

        @pl.when(qi == ki)
        def _():
            dk_sc[...] = jnp.zeros_like(dk_sc)
            dv_sc[...] = jnp.zeros_like(dv_sc)

        def tile(diagonal):
            for h in range(2):
                blk = slice(h * LANES, (h + 1) * LANES)
                qh, kh, doh = qa_ref[:, blk], ka_ref[:, blk], doa_ref[:, blk]
                p = jnp.exp(_dot_nt(qh, kh) - jnp.tile(lse_ref[h], (1, rep)))
                if diagonal:
                    p = jnp.where(_future_keys(tq, tk), 0.0, p)
                dp = _dot_nt(doh, va_ref[:, blk])
                ds = (p * (dp - jnp.tile(d_ref[h], (1, rep)))).astype(MXU_DTYPE)
                dv_sc[h] += _dot_tn(p.astype(MXU_DTYPE), doh)
                dk_sc[h] += _dot_tn(ds, qh)
                dqa_ref[rows, blk] += _dot(ds, kh)

        @pl.when(qi > ki)
        def _():
            tile(False)

        @pl.when(qi == ki)
        def _():
            tile(True)

        @pl.when(qi == nq - 1)
        def _():
            for h in range(2):
                blk = slice(h * LANES, (h + 1) * LANES)
                dka_ref[:, blk] = dk_sc[h]
                dva_ref[:, blk] = dv_sc[h]

    pair = 2 * LANES
    q_blk = lambda j, t, qi_ref, ki_ref: (qi_ref[t], j)
    k_blk = lambda j, t, qi_ref, ki_ref: (ki_ref[t], j)
    stat = pl.BlockSpec((2, tq, LANES), lambda j, t, qi_ref, ki_ref: (j, qi_ref[t], 0))
    return _pcall(
        body, name=name,
        grid_spec=pltpu.PrefetchScalarGridSpec(
            num_scalar_prefetch=2, grid=(HEADS // 2, qi_tab.shape[0]),
            in_specs=[pl.BlockSpec((tq, pair), q_blk), pl.BlockSpec((tk, pair), k_blk), pl.BlockSpec((tk, pair), k_blk),
                      pl.BlockSpec((tq, pair), q_blk), stat, stat],
            out_specs=[pl.BlockSpec((T, pair), lambda j, t, qi_ref, ki_ref: (0, j)),
                       pl.BlockSpec((tk, pair), k_blk), pl.BlockSpec((tk, pair), k_blk)],
            scratch_shapes=[pltpu.VMEM((2, tk, LANES), f32)] * 2),
        out_shape=[jax.ShapeDtypeStruct((T, FOX_PAD), f32)] * 3,
        compiler_params=_params(2),
    )(qi_tab, ki_tab, qa, ka, va, doa, lse, drep)


def _fox_bwd_post(dqa, dka, fgb, *, name, tm=512):
    T = fgb.shape[0]
    tm = min(tm, T)
    nt = T // tm

    def body(dqa_ref, dka_ref, fg_ref, dfg_ref, dbf_ref, carry):
        i = pl.program_id(0)

        @pl.when(i == 0)
        def _():
            carry[...] = jnp.zeros_like(carry)
            dbf_ref[...] = jnp.zeros_like(dbf_ref)

        lane = lax.broadcasted_iota(jnp.int32, (tm, LANES), 1)
        dc = jnp.zeros((tm, LANES), f32)
        for h in range(HEADS):
            row_sum = dqa_ref[:, h * LANES + AUX:h * LANES + AUX + 1]
            col_sum = dka_ref[:, h * LANES + AUX + 3:h * LANES + AUX + 4]
            dc = jnp.where(lane == h, jnp.broadcast_to(row_sum - col_sum, (tm, LANES)), dc)
        r = lax.broadcasted_iota(jnp.int32, (tm, tm), 0)
        c = lax.broadcasted_iota(jnp.int32, (tm, tm), 1)
        tri = jnp.where(c >= r, 1.0, 0.0).astype(jnp.bfloat16)
        dls = _tri_dot(tri, dc) + carry[0:1, :]
        carry[...] = jnp.broadcast_to(dls[0:1, :], carry.shape)
        dfg = dls * _sigmoid(-fg_ref[...])
        dfg_ref[...] = dfg
        dbf_ref[...] += jnp.sum(dfg, axis=0, keepdims=True)

    rev = lambda i: (nt - 1 - i, 0)
    return _pcall(
        body, name=name, grid=(nt,),
        in_specs=[pl.BlockSpec((tm, FOX_PAD), rev), pl.BlockSpec((tm, FOX_PAD), rev), pl.BlockSpec((tm, LANES), rev)],
        out_specs=[pl.BlockSpec((tm, LANES), rev), pl.BlockSpec((1, LANES), lambda i: (0, 0))],
        out_shape=[jax.ShapeDtypeStruct((T, LANES), f32), jax.ShapeDtypeStruct((1, LANES), f32)],
        scratch_shapes=[pltpu.VMEM((8, LANES), f32)],
        compiler_params=_params(1),
    )(dqa, dka, fgb)


GELU_C = math.sqrt(2.0 / math.pi)
GELU_A = 0.044715


def _gelu(x):
    t = jnp.tanh(GELU_C * (x + GELU_A * x * x * x))
    return 0.5 * x * (1.0 + t), t


def _gelu_grad(x, t):
    return 0.5 * (1.0 + t) + 0.5 * x * (1.0 - t * t) * GELU_C * (1.0 + 3.0 * GELU_A * x * x)


def _expm1(x):
    e = jnp.exp(x)
    safe = jnp.where(e == 1.0, x, (e - 1.0) * x / jnp.log(jnp.where(e == 1.0, 0.5, e)))
    return jnp.where(x < -0.5, e - 1.0, safe)


def _lru_gates(u, wab_ref, bab_ref, lam_ref):
    pre = _dot(u.astype(MXU_DTYPE), wab_ref[...]) + bab_ref[...]
    r = _sigmoid(pre[:, :LRU_W])
    gi = _sigmoid(pre[:, LRU_W:])
    lam = lam_ref[...]
    sp = jnp.maximum(-lam, 0.0) + jnp.log(1.0 + jnp.exp(-jnp.abs(lam)))
    log_a = -LRU_C * r * sp
    a = jnp.exp(log_a)
    s = jnp.sqrt(-_expm1(2.0 * log_a))
    return r, gi, sp, a, s


def _lru_fwd(lxg, conv_w, conv_b, wab, bab, lam, *, name, tc=512):
    T = lxg.shape[0]
    tc = min(tc, T)
    nc = T // tc

    def body(lx_ref, lg_ref, cw_ref, cb_ref, wab_ref, bab_ref, lam_ref,
             out_ref, u_ref, hs_ref, ext, a_sc, b_sc, h_sc):
        i = pl.program_id(0)

        @pl.when(i == 0)
        def _():
            ext[0:8, :] = jnp.zeros((8, LRU_W), f32)
            h_sc[...] = jnp.zeros_like(h_sc)

        ext[8:, :] = lx_ref[...]
        u = cb_ref[...] + cw_ref[0:1, :] * ext[pl.ds(5, tc), :]
        for k in range(1, CONV_K):
            u = u + cw_ref[k:k + 1, :] * ext[pl.ds(5 + k, tc), :]
        ext[0:8, :] = ext[tc:tc + 8, :]
        u_ref[...] = u
        r, gi, sp, a, s = _lru_gates(u, wab_ref, bab_ref, lam_ref)
        a_sc[...] = a
        b_sc[...] = s * (gi * u)

        def step(t, h):
            h = a_sc[pl.ds(t, 1), :] * h + b_sc[pl.ds(t, 1), :]
            hs_ref[pl.ds(t, 1), :] = h
            return h

        h = lax.fori_loop(0, tc, step, h_sc[0:1, :], unroll=8)
        h_sc[...] = jnp.broadcast_to(h, h_sc.shape)
        gel, _ = _gelu(lg_ref[...])
        out_ref[...] = gel * hs_ref[...]

    row = lambda i: (i, 0)
    const = lambda i: (0, 0)
    return _pcall(
        body, name=name, grid=(nc,),
        in_specs=[pl.BlockSpec((tc, LRU_W), row), pl.BlockSpec((tc, LRU_W), lambda i: (i, 1)),
                  pl.BlockSpec((CONV_K, LRU_W), const), pl.BlockSpec((1, LRU_W), const),
                  pl.BlockSpec((LRU_W, 2 * LRU_W), const), pl.BlockSpec((1, 2 * LRU_W), const),
                  pl.BlockSpec((1, LRU_W), const)],
        out_specs=[pl.BlockSpec((tc, LRU_W), row)] * 3,
        out_shape=[jax.ShapeDtypeStruct((T, LRU_W), f32)] * 3,
        scratch_shapes=[pltpu.VMEM((tc + 8, LRU_W), f32), pltpu.VMEM((tc, LRU_W), f32),
                        pltpu.VMEM((tc, LRU_W), f32), pltpu.VMEM((8, LRU_W), f32)],
        compiler_params=_params(1),
    )(lxg, lxg, conv_w, conv_b, wab, bab, lam)


def _lru_bwd(dlru, lxg, u, hs, conv_w, wab, bab, lam, *, name, tc=512):
    T = lxg.shape[0]
    tc = min(tc, T)
    nc = T // tc
    bp = tc // 8

    def body(dl_ref, lx_ref, lxp_ref, lg_ref, u_ref, hs_ref, hsp_ref, cw_ref, wab_ref, bab_ref, lam_ref,
             dlxg_ref, dwab_ref, dbab_ref, dcw_ref, dcb_ref, dlam_ref,
             dh_sc, a_sc, ext, du_ext, carry):
        i = pl.program_id(0)
        first_chunk = i == nc - 1

        @pl.when(i == 0)
        def _():
            dwab_ref[...] = jnp.zeros_like(dwab_ref)
            dbab_ref[...] = jnp.zeros_like(dbab_ref)
            dcw_ref[...] = jnp.zeros_like(dcw_ref)
            dcb_ref[...] = jnp.zeros_like(dcb_ref)
            dlam_ref[...] = jnp.zeros_like(dlam_ref)
            carry[...] = jnp.zeros_like(carry)
            du_ext[tc:tc + 8, :] = jnp.zeros((8, LRU_W), f32)

        lg = lg_ref[...]
        gel, th = _gelu(lg)
        dl = dl_ref[...]
        hs = hs_ref[...]
        dlg = dl * hs * _gelu_grad(lg, th)
        u = u_ref[...]
        r, gi, sp, a, s = _lru_gates(u, wab_ref, bab_ref, lam_ref)
        a_sc[...] = a
        dh_sc[...] = dl * gel

        def step(k, c):
            t = tc - 1 - k
            dh = dh_sc[pl.ds(t, 1), :] + c
            dh_sc[pl.ds(t, 1), :] = dh
            return a_sc[pl.ds(t, 1), :] * dh

        c = lax.fori_loop(0, tc, step, carry[0:1, :], unroll=8)
        carry[...] = jnp.broadcast_to(c, carry.shape)

        ext[0:8, :] = jnp.where(first_chunk, 0.0, hsp_ref[...])
        ext[8:, :] = hs
        hprev = ext[pl.ds(7, tc), :]
        dh = dh_sc[...]
        da = dh * hprev
        giu = gi * u
        dla = da * a - (dh * giu) * (a * a / s)
        dgi = dh * s * u
        du = dh * s * gi
        dr = dla * (-LRU_C * sp)
        dlam_ref[...] += jnp.sum(dla * (-LRU_C * r), axis=0, keepdims=True) * (-_sigmoid(-lam_ref[...]))
        dpre = jnp.concatenate([dr * r * (1.0 - r), dgi * gi * (1.0 - gi)], axis=1)
        dpre_b = dpre.astype(MXU_DTYPE)
        du = du + _dot_nt(dpre_b, wab_ref[...])
        dwab_ref[...] += _dot_tn(u.astype(MXU_DTYPE), dpre_b)
        dbab_ref[...] += jnp.sum(dpre, axis=0, keepdims=True)
        dcb_ref[...] += jnp.sum(du, axis=0, keepdims=True)

        du_ext[0:tc, :] = du
        dlx = cw_ref[0:1, :] * du_ext[pl.ds(3, tc), :]
        for k in range(1, CONV_K):
            dlx = dlx + cw_ref[k:k + 1, :] * du_ext[pl.ds(3 - k, tc), :]
        du_ext[tc:tc + 8, :] = du_ext[0:8, :]
        ext[0:8, :] = jnp.where(first_chunk, 0.0, lxp_ref[...])
        ext[8:, :] = lx_ref[...]
        for k in range(CONV_K):
            dcw_ref[k:k + 1, :] += jnp.sum(du * ext[pl.ds(5 + k, tc), :], axis=0, keepdims=True)
        dlxg_ref[:, :LRU_W] = dlx.astype(dlxg_ref.dtype)
        dlxg_ref[:, LRU_W:] = dlg.astype(dlxg_ref.dtype)

    rev = lambda i: (nc - 1 - i, 0)
    prev8 = lambda i: (jnp.maximum((nc - 1 - i) * bp - 1, 0), 0)
    const = lambda i: (0, 0)
    return _pcall(
        body, name=name, grid=(nc,),
        in_specs=[
            pl.BlockSpec((tc, LRU_W), rev),
            pl.BlockSpec((tc, LRU_W), rev),
            pl.BlockSpec((8, LRU_W), prev8),
            pl.BlockSpec((tc, LRU_W), lambda i: (nc - 1 - i, 1)),
            pl.BlockSpec((tc, LRU_W), rev),
            pl.BlockSpec((tc, LRU_W), rev),
            pl.BlockSpec((8, LRU_W), prev8),
            pl.BlockSpec((CONV_K, LRU_W), const),
            pl.BlockSpec((LRU_W, 2 * LRU_W), const),
            pl.BlockSpec((1, 2 * LRU_W), const),
            pl.BlockSpec((1, LRU_W), const),
        ],
        out_specs=[
            pl.BlockSpec((tc, 2 * LRU_W), rev),
            pl.BlockSpec((LRU_W, 2 * LRU_W), const),
            pl.BlockSpec((1, 2 * LRU_W), const),
            pl.BlockSpec((8, LRU_W), const),
            pl.BlockSpec((1, LRU_W), const),
            pl.BlockSpec((1, LRU_W), const),
        ],
        out_shape=[
            jax.ShapeDtypeStruct((T, 2 * LRU_W), MXU_DTYPE),
            jax.ShapeDtypeStruct((LRU_W, 2 * LRU_W), f32),
            jax.ShapeDtypeStruct((1, 2 * LRU_W), f32),
            jax.ShapeDtypeStruct((8, LRU_W), f32),
            jax.ShapeDtypeStruct((1, LRU_W), f32),
            jax.ShapeDtypeStruct((1, LRU_W), f32),
        ],
        scratch_shapes=[pltpu.VMEM((tc, LRU_W), f32), pltpu.VMEM((tc, LRU_W), f32),
                        pltpu.VMEM((tc + 8, LRU_W), f32), pltpu.VMEM((tc + 8, LRU_W), f32),
                        pltpu.VMEM((8, LRU_W), f32)],
        compiler_params=_params(1),
    )(dlru, lxg, lxg, lxg, u, hs, hs, conv_w, wab, bab, lam)


def _mix_out(fox, lru, wo, xhat1, g1, b1, g2, b2, *, name, tm=512):
    T = fox.shape[0]
    tm = min(tm, T)
    nt = T // tm

    def body(fox_ref, lru_ref, wo_ref, xh_ref, g1_ref, b1_ref, g2_ref, b2_ref, xhat_ref, xn_ref, rstd_ref):
        mix = _dot(fox_ref[...].astype(MXU_DTYPE), wo_ref[:FOX_W, :])
        mix = mix + _dot(lru_ref[...].astype(MXU_DTYPE), wo_ref[FOX_W:, :])
        x1 = xh_ref[...] * g1_ref[...] + b1_ref[...]
        xhat, rstd = _layer_norm_stats(DN_ALPHA * x1 + mix)
        xhat_ref[...] = xhat
        xn_ref[...] = xhat * g2_ref[...] + b2_ref[...]
        rstd_ref[...] = jnp.broadcast_to(rstd, rstd_ref.shape)

    row = lambda i: (i, 0)
    const = lambda i: (0, 0)
    vec = pl.BlockSpec((1, D_MODEL), const)
    return _pcall(
        body, name=name, grid=(nt,),
        in_specs=[pl.BlockSpec((tm, FOX_W), row), pl.BlockSpec((tm, LRU_W), row),
                  pl.BlockSpec((D_MODEL, D_MODEL), const), pl.BlockSpec((tm, D_MODEL), row), vec, vec, vec, vec],
        out_specs=[pl.BlockSpec((tm, D_MODEL), row), pl.BlockSpec((tm, D_MODEL), row),
                   pl.BlockSpec((tm, LANES), row)],
        out_shape=[jax.ShapeDtypeStruct((T, D_MODEL), f32), jax.ShapeDtypeStruct((T, D_MODEL), f32),
                   jax.ShapeDtypeStruct((T, LANES), f32)],
        compiler_params=_params(1),
    )(fox, lru, wo, xhat1, g1, b1, g2, b2)


def _mix_out_bwd(dyp, fox, lru, wo, *, name, tm=512):
    T = fox.shape[0]
    tm = min(tm, T)
    nt = T // tm

    def body(dyp_ref, fox_ref, lru_ref, wo_ref, dfox_ref, dlru_ref, dwo_ref):
        i = pl.program_id(0)

        @pl.when(i == 0)
        def _():
            dwo_ref[...] = jnp.zeros_like(dwo_ref)

        dmix = dyp_ref[...].astype(MXU_DTYPE)
        dcat = _dot_nt(dmix, wo_ref[...])
        dfox_ref[...] = dcat[:, :FOX_W].astype(dfox_ref.dtype)
        dlru_ref[...] = dcat[:, FOX_W:]
        dwo_ref[:FOX_W, :] += _dot_tn(fox_ref[...].astype(MXU_DTYPE), dmix)
        dwo_ref[FOX_W:, :] += _dot_tn(lru_ref[...].astype(MXU_DTYPE), dmix)

    row = lambda i: (i, 0)
    const = lambda i: (0, 0)
    return _pcall(
        body, name=name, grid=(nt,),
        in_specs=[pl.BlockSpec((tm, D_MODEL), row), pl.BlockSpec((tm, FOX_W), row), pl.BlockSpec((tm, LRU_W), row),
                  pl.BlockSpec((D_MODEL, D_MODEL), const)],
        out_specs=[pl.BlockSpec((tm, FOX_W), row), pl.BlockSpec((tm, LRU_W), row),
                   pl.BlockSpec((D_MODEL, D_MODEL), const)],
        out_shape=[jax.ShapeDtypeStruct((T, FOX_W), MXU_DTYPE), jax.ShapeDtypeStruct((T, LRU_W), f32),
                   jax.ShapeDtypeStruct((D_MODEL, D_MODEL), f32)],
        compiler_params=_params(1),
    )(dyp, fox, lru, wo)


def make_wp(w_in):
    scale = jnp.concatenate([jnp.full((FOX_W,), 1.0 / math.sqrt(HEAD_DIM), w_in.dtype),
                             jnp.ones((IN_COLS - FOX_W,), w_in.dtype)])
    return jnp.pad(w_in * scale[None, :], ((0, 0), (0, Z_PAD - IN_COLS)))


def _block_diag(w):
    eye = jnp.eye(HEADS, dtype=w.dtype)
    return jnp.einsum("hij,hg->higj", w, eye).reshape(LRU_W, LRU_W)


def _block_diag_extract(m):
    m4 = m.reshape(HEADS, HEAD_DIM, HEADS, HEAD_DIM)
    return jnp.stack([m4[h, :, h, :] for h in range(HEADS)])


class _NoOverlap:
    def start_token(self):
        return None

    def after_attention(self, after):
        return None

    def ffn2_weights(self, w, after):
        return w["f2g"], w["f2u"], w["f2d"]

    def ffn2_grads(self, grads):
        return None

    def mixer_grads(self, dwp, dwo, small):
        return None

    def before_ffn1_bwd(self, after):
        return None


def _tied(a, token):
    return a if token is None else a + token[0, 0]


def _local_step(x, target, w, hooks=None):
    hooks = hooks or _NoOverlap()
    wp = w["wp"]
    bfp = w["bfp"]
    wab = jnp.concatenate([_block_diag(w["rg_wa"]), _block_diag(w["rg_wx"])], axis=1).astype(MXU_DTYPE)
    bab = jnp.concatenate([w["rg_ba"].reshape(1, LRU_W), w["rg_bx"].reshape(1, LRU_W)], axis=1)

    xb0, g1a, u1a, xhat1, xn1, rstd1 = _ffn_fwd(x, w["f1g"], w["f1u"], w["f1d"], w["ln1_g"],
                                                _tied(w["ln1_b"], hooks.start_token()), name="ffn1_fwd")
    qkv, lxg, fgb = _proj_in(xn1, wp, bfp, name="proj_in")
    qa, ka, va = _fox_prep(qkv, fgb, name="fox_prep")
    fox, lse = _fox_fwd(qa, ka, va, name="fox_fwd")
    token = hooks.after_attention([lse])
    lru, uconv, hs = _lru_fwd(lxg, w["conv_w"], _tied(w["conv_b"], token), wab, bab, w["lam"], name="lru_fwd")
    xhat2, x2, rstd2 = _mix_out(fox, lru, w["wo"], xhat1, w["ln1_g"], w["ln1_b"], w["ln2_g"], w["ln2_b"], name="mix_out")
    f2g, f2u, f2d = hooks.ffn2_weights(w, [rstd2])
    xb2, g2a, u2a, xhat3, _, rstd3 = _ffn_fwd(x2, f2g, f2u, f2d, w["ln3_g"], w["ln3_b"], name="ffn2_fwd")

    dy3p, dln3g, dln3b, loss = _loss_ln_bwd(xhat3, rstd3, w["ln3_g"], w["ln3_b"], target, name="loss_ln3_bwd")
    dx2, df2g, df2u, df2d = _ffn_bwd(dy3p, xb2, g2a, u2a, f2g, f2u, f2d, name="ffn2_bwd")
    token = hooks.ffn2_grads([df2g, df2u, df2d])
    dy2p, dln2g, dln2b = _ln_bwd(dx2, xhat2, rstd2, _tied(w["ln2_g"], token), name="ln2_bwd")
    dfox, dlru, dwo = _mix_out_bwd(dy2p, fox, lru, w["wo"], name="mix_out_bwd")
    dlxg, dwab, dbab, dcw, dcb, dlam = _lru_bwd(dlru, lxg, uconv, hs, w["conv_w"], wab, bab, w["lam"], name="lru_bwd")
    drep, doa = _fox_bwd_prep(dfox, fox, name="fox_bwd_prep")
    dqa, dka, dva = _fox_bwd(qa, ka, va, doa, lse, drep, name="fox_bwd")
    dfg, dbf = _fox_bwd_post(dqa, dka, fgb, name="fox_bwd_post")
    dx1, dwp = _proj_in_bwd(dqa, dka, dva, dlxg, dfg, xn1, dy2p, wp, name="proj_in_bwd")
    dy1p, dln1g, dln1b = _ln_bwd(dx1, xhat1, rstd1, w["ln1_g"], name="ln1_bwd")
    small = dict(
        ln1_g=dln1g, ln1_b=dln1b, ln2_g=dln2g, ln2_b=dln2b, ln3_g=dln3g, ln3_b=dln3b,
        b_forget=dbf[:, :HEADS], conv_w=dcw[:CONV_K], conv_b=dcb,
        rg_wa=_block_diag_extract(dwab[:, :LRU_W]), rg_wx=_block_diag_extract(dwab[:, LRU_W:]),
        rg_ba=dbab[:, :LRU_W].reshape(HEADS, HEAD_DIM), rg_bx=dbab[:, LRU_W:].reshape(HEADS, HEAD_DIM),
        lru_lambda=dlam,
    )
    hooks.before_ffn1_bwd([dln1b])
    token = hooks.mixer_grads(dwp, dwo, small)
    dx, df1g, df1u, df1d = _ffn_bwd(dy1p, xb0, g1a, u1a, w["f1g"], w["f1u"], w["f1d"], token, name="ffn1_bwd")

    grads = dict(f1g=df1g, f1u=df1u, f1d=df1d, f2g=df2g, f2u=df2u, f2d=df2d, wp=dwp, wo=dwo, **small)
    return loss, dx, grads


MESH = pl.DeviceIdType.MESH
HBM_SPEC = pl.BlockSpec(memory_space=pl.ANY)
VMEM_SPEC = pl.BlockSpec(memory_space=pltpu.VMEM)


def _position():
    return lax.axis_index("x"), lax.axis_index("y"), lax.axis_index("c")


def _other_chips(x, y):
    return [(1 - x, y), (x, 1 - y), (1 - x, 1 - y)]


def _all_gather_bf16(shards, *, name):
    n = len(shards)

    def body(*refs):
        ins, outs, stages = refs[:n], refs[n:2 * n], refs[2 * n:3 * n]
        send_sems, recv_sems, local_sems = refs[3 * n:]
        x, y, c = _position()
        me, sibling = (x, y, c), (x, y, 1 - c)
        chips = _other_chips(x, y)

        def rows(k, px, py, pc):
            r = shards[k].shape[0]
            m = r // 2
            return outs[k].at[pl.ds(pl.multiple_of((2 * px + py) * r + pc * m, 16), m), :]

        def copy(k, idx, block, to, src=None):
            return pltpu.make_async_remote_copy(
                src_ref=rows(k, *block) if src is None else src, dst_ref=rows(k, *block),
                send_sem=send_sems.at[7 * k + idx], recv_sem=recv_sems.at[7 * k + idx],
                device_id=to, device_id_type=MESH)

        started = []
        mine = []
        for k in range(n):
            m = shards[k].shape[0] // 2
            stages[k][...] = ins[k][pl.ds(pl.multiple_of(c * m, 16), m), :].astype(stages[k].dtype)
            cp = pltpu.make_async_copy(stages[k], rows(k, *me), local_sems.at[k])
            cp.start()
            mine.append(cp)
            first = [copy(k, 0, me, sibling, src=stages[k])]
            first += [copy(k, 1 + j, me, (*chip, c), src=stages[k]) for j, chip in enumerate(chips)]
            for cp in first:
                cp.start()
            started += first
        for k in range(n):
            for j, chip in enumerate(chips):
                copy(k, 1 + j, (*chip, c), me).wait_recv()
                fwd = copy(k, 4 + j, (*chip, c), sibling)
                fwd.start()
                started.append(fwd)
        for k in range(n):
            copy(k, 0, sibling, me).wait_recv()
            for j, chip in enumerate(chips):
                copy(k, 4 + j, (*chip, 1 - c), me).wait_recv()
        for cp in started:
            cp.wait_send()
        for cp in mine:
            cp.wait()

    return _pcall(
        body, name=name,
        in_specs=[VMEM_SPEC] * n, out_specs=[HBM_SPEC] * n,
        out_shape=[jax.ShapeDtypeStruct((N_SHARD * s.shape[0], s.shape[1]), MXU_DTYPE) for s in shards],
        scratch_shapes=[pltpu.VMEM((s.shape[0] // 2, s.shape[1]), MXU_DTYPE) for s in shards]
        + [pltpu.SemaphoreType.DMA((7 * n,)), pltpu.SemaphoreType.DMA((7 * n,)), pltpu.SemaphoreType.DMA((n,))],
        compiler_params=pltpu.CompilerParams(vmem_limit_bytes=VMEM_LIMIT),
    )(*shards)


def _swap_halves(gs, *, name):
    n = len(gs)

    def body(*refs):
        ins, outs = refs[:n], refs[n:2 * n]
        send_sems, recv_sems = refs[2 * n:]
        x, y, c = _position()
        cps = []
        for k in range(n):
            m = gs[k].shape[1] // 2
            src = ins[k].at[:, pl.ds(pl.multiple_of((1 - c) * m, 16), m), :]
            cp = pltpu.make_async_remote_copy(src_ref=src, dst_ref=outs[k], send_sem=send_sems.at[k],
                                              recv_sem=recv_sems.at[k], device_id=(x, y, 1 - c), device_id_type=MESH)
            cp.start()
            cps.append(cp)
        for cp in cps:
            cp.wait()

    return _pcall(
        body, name=name, in_specs=[HBM_SPEC] * n, out_specs=[HBM_SPEC] * n,
        out_shape=[jax.ShapeDtypeStruct((g.shape[0], g.shape[1] // 2, g.shape[2]), g.dtype) for g in gs],
        scratch_shapes=[pltpu.SemaphoreType.DMA((n,)), pltpu.SemaphoreType.DMA((n,))],
    )(*gs)


def _add_halves(gs, recvs, *, name, tm=256):
    n = len(gs)
    _, r, cdim = gs[0].shape
    m = r // 2
    tm = min(tm, m)
    nb = m // tm
    c_idx = lax.axis_index("c").astype(jnp.int32).reshape(1)

    def body(c_ref, *refs):
        for k in range(n):
            refs[2 * n + k][...] = (refs[k][...].astype(f32) + refs[n + k][...].astype(f32)).astype(refs[2 * n + k].dtype)

    mine = pl.BlockSpec((None, tm, cdim), lambda j, i, c_ref: (j, c_ref[0] * nb + i, 0))
    half = pl.BlockSpec((None, tm, cdim), lambda j, i, c_ref: (j, i, 0))
    return _pcall(
        body, name=name,
        grid_spec=pltpu.PrefetchScalarGridSpec(
            num_scalar_prefetch=1, grid=(N_SHARD, nb),
            in_specs=[mine] * n + [half] * n, out_specs=[half] * n),
        out_shape=[jax.ShapeDtypeStruct((N_SHARD, m, cdim), g.dtype) for g in gs],
        compiler_params=_params(2),
    )(c_idx, *gs, *recvs)


def _scatter_partials(ps, *, name):
    n = len(ps)

    def body(*refs):
        ins, outs = refs[:n], refs[n:2 * n]
        send_sems, recv_sems = refs[2 * n:]
        x, y, c = _position()
        me_chip = 2 * x + y
        cps = []
        for k in range(n):
            for j, (px, py) in enumerate(_other_chips(x, y)):
                cp = pltpu.make_async_remote_copy(
                    src_ref=ins[k].at[2 * px + py], dst_ref=outs[k].at[me_chip],
                    send_sem=send_sems.at[3 * k + j], recv_sem=recv_sems.at[3 * k + j],
                    device_id=(px, py, c), device_id_type=MESH)
                cp.start()
                cps.append(cp)
        for cp in cps:
            cp.wait()

    return _pcall(
        body, name=name, in_specs=[HBM_SPEC] * n, out_specs=[HBM_SPEC] * n,
        out_shape=[jax.ShapeDtypeStruct(p.shape, p.dtype) for p in ps],
        scratch_shapes=[pltpu.SemaphoreType.DMA((3 * n,)), pltpu.SemaphoreType.DMA((3 * n,))],
    )(*ps)


def _sum_slabs(ps, qs, *, name, tm=128):
    n = len(qs)
    _, m, cdim = qs[0].shape
    tm = min(tm, m)
    nb = m // tm
    assert m % tm == 0, (m, tm)
    where = jnp.stack([2 * lax.axis_index("x") + lax.axis_index("y"), lax.axis_index("c")]).astype(jnp.int32)

    def body(w_ref, *refs):
        for k in range(n):
            own, q1, q2, q3 = (refs[4 * k + t][...].astype(f32) for t in range(4))
            refs[4 * n + k][...] = ((own + q1) + q2) + q3

    def slab(flip):
        return pl.BlockSpec((None, tm, cdim), lambda i, w_ref: (jnp.bitwise_xor(w_ref[0], flip), i, 0))

    operands = []
    for p, q in zip(ps, qs):
        operands += [p, q, q, q]
    return _pcall(
        body, name=name,
        grid_spec=pltpu.PrefetchScalarGridSpec(
            num_scalar_prefetch=1, grid=(nb,),
            in_specs=[slab(0), slab(2), slab(1), slab(3)] * n,
            out_specs=[pl.BlockSpec((tm, cdim), lambda i, w_ref: (w_ref[1] * nb + i, 0))] * n),
        out_shape=[jax.ShapeDtypeStruct((2 * m, cdim), f32) for _ in qs],
        compiler_params=_params(1),
    )(where, *operands)


def _join_halves(fs, *, name):
    n = len(fs)

    def body(*refs):
        outs = refs[n:2 * n]
        send_sems, recv_sems = refs[2 * n:]
        x, y, c = _position()
        cps = []
        for k in range(n):
            m = fs[k].shape[0] // 2
            half = outs[k].at[pl.ds(pl.multiple_of(c * m, 8), m), :]
            cp = pltpu.make_async_remote_copy(src_ref=half, dst_ref=half, send_sem=send_sems.at[k],
                                              recv_sem=recv_sems.at[k], device_id=(x, y, 1 - c), device_id_type=MESH)
            cp.start()
            cps.append(cp)
        for cp in cps:
            cp.wait()

    return _pcall(
        body, name=name, in_specs=[HBM_SPEC] * n, out_specs=[HBM_SPEC] * n,
        out_shape=[jax.ShapeDtypeStruct(f.shape, f.dtype) for f in fs],
        input_output_aliases={k: k for k in range(n)},
        scratch_shapes=[pltpu.SemaphoreType.DMA((n,)), pltpu.SemaphoreType.DMA((n,))],
    )(*fs)


def _all_reduce_small(v, after=None, *, name):
    r = v.shape[0]
    extra = [] if after is None else [after]

    def body(v_ref, *refs):
        out_ref, buf, send_sems, recv_sems, local_sem = refs[len(extra):]
        x, y, c = _position()
        me, sibling = (x, y, c), (x, y, 1 - c)
        chips = _other_chips(x, y)

        def rows(px, py, pc):
            return buf.at[pl.ds(pl.multiple_of((4 * px + 2 * py + pc) * r, 8), r), :]

        def copy(k, block, to, src=None):
            return pltpu.make_async_remote_copy(
                src_ref=rows(*block) if src is None else src, dst_ref=rows(*block),
                send_sem=send_sems.at[k], recv_sem=recv_sems.at[k], device_id=to, device_id_type=MESH)

        mine = pltpu.make_async_copy(v_ref, rows(*me), local_sem)
        mine.start()
        first = [copy(0, me, sibling, src=v_ref)]
        first += [copy(1 + j, me, (*chip, c), src=v_ref) for j, chip in enumerate(chips)]
        for cp in first:
            cp.start()
        passed = [copy(4 + j, (*chip, c), sibling) for j, chip in enumerate(chips)]
        for j, chip in enumerate(chips):
            copy(1 + j, (*chip, c), me).wait_recv()
            passed[j].start()
        copy(0, sibling, me).wait_recv()
        for j, chip in enumerate(chips):
            copy(4 + j, (*chip, 1 - c), me).wait_recv()
        for cp in first + passed:
            cp.wait_send()
        mine.wait()
        acc = buf[0:r, :]
        for d in range(1, N_DEV):
            acc = acc + buf[d * r:(d + 1) * r, :]
        out_ref[...] = acc

    return _pcall(
        body, name=name, in_specs=[VMEM_SPEC] + [HBM_SPEC] * len(extra), out_specs=VMEM_SPEC,
        out_shape=jax.ShapeDtypeStruct((r, LANES), f32),
        scratch_shapes=[pltpu.VMEM((N_DEV * r, LANES), f32), pltpu.SemaphoreType.DMA((7,)),
                        pltpu.SemaphoreType.DMA((7,)), pltpu.SemaphoreType.DMA],
    )(v, *extra)


SEM_SPEC = pl.BlockSpec(memory_space=pltpu.SEMAPHORE)
HBM_ONLY = pl.BlockSpec(memory_space=pltpu.HBM)
EFFECT = pltpu.SideEffectType.DATAFLOW_SIDE_EFFECTING


def _split_start(bufs, copies_fn, n_sems, *, name):
    n = len(bufs)

    def body(*refs):
        send_sems, recv_sems = refs[n], refs[n + 1]
        thru = refs[n + 2:2 * n + 2]
        token = refs[2 * n + 2]
        for cp in copies_fn(thru, send_sems, recv_sems):
            cp.start()
        token[...] = jnp.zeros_like(token)

    outs = _pcall(
        body, name=name,
        out_shape=(pltpu.SemaphoreType.DMA((n_sems,)), pltpu.SemaphoreType.DMA((n_sems,)),
                   *[pltpu.HBM(b.shape, b.dtype) for b in bufs], jax.ShapeDtypeStruct((8, LANES), f32)),
        in_specs=[HBM_ONLY] * n,
        out_specs=(SEM_SPEC, SEM_SPEC, *[HBM_ONLY] * n, VMEM_SPEC),
        input_output_aliases={k: 2 + k for k in range(n)},
        compiler_params=pltpu.CompilerParams(has_side_effects=EFFECT),
    )(*[pltpu.with_memory_space_constraint(b, pltpu.HBM) for b in bufs])
    return outs[0], outs[1], list(outs[2:2 + n]), outs[2 + n]


def _split_wait(thru, send_sems, recv_sems, after, copies_fn, *, name):
    n = len(thru)

    def body(*refs):
        for cp in copies_fn(refs[:n], refs[n], refs[n + 1]):
            cp.wait_send()
            cp.wait_recv()

    return list(_pcall(
        body, name=name,
        out_shape=tuple(pltpu.HBM(b.shape, b.dtype) for b in thru),
        in_specs=[HBM_ONLY] * n + [SEM_SPEC, SEM_SPEC] + [HBM_SPEC] * len(after),
        out_specs=tuple([HBM_ONLY] * n),
        input_output_aliases={k: k for k in range(n)},
        compiler_params=pltpu.CompilerParams(has_side_effects=EFFECT),
    )(*thru, send_sems, recv_sems, *after))


def _scatter_copies(n):
    def copies(bufs, send_sems, recv_sems):
        x, y, c = _position()
        me_chip = 2 * x + y
        cps = []
        for k in range(n):
            for j, (px, py) in enumerate(_other_chips(x, y)):
                cps.append(pltpu.make_async_remote_copy(
                    src_ref=bufs[k].at[2 * px + py], dst_ref=bufs[n + k].at[me_chip],
                    send_sem=send_sems.at[3 * k + j], recv_sem=recv_sems.at[3 * k + j],
                    device_id=(px, py, c), device_id_type=MESH))
        return cps
    return copies


def _block_rows(buf, px, py, pc):
    m = buf.shape[0] // N_DEV
    return buf.at[pl.ds(pl.multiple_of((4 * px + 2 * py + pc) * m, 16), m), :]


def _gather_ici_copies(n):
    def copies(bufs, send_sems, recv_sems):
        x, y, c = _position()
        cps = []
        for k in range(n):
            rows = _block_rows(bufs[k], x, y, c)
            targets = [(x, y, 1 - c)] + [(px, py, c) for px, py in _other_chips(x, y)]
            for j, to in enumerate(targets):
                cps.append(pltpu.make_async_remote_copy(
                    src_ref=rows, dst_ref=rows, send_sem=send_sems.at[4 * k + j], recv_sem=recv_sems.at[4 * k + j],
                    device_id=to, device_id_type=MESH))
        return cps
    return copies


def _gather_d2d_copies(n):
    def copies(bufs, send_sems, recv_sems):
        x, y, c = _position()
        cps = []
        for k in range(n):
            for j, (px, py) in enumerate(_other_chips(x, y)):
                rows = _block_rows(bufs[k], px, py, c)
                cps.append(pltpu.make_async_remote_copy(
                    src_ref=rows, dst_ref=rows, send_sem=send_sems.at[3 * k + j], recv_sem=recv_sems.at[3 * k + j],
                    device_id=(x, y, 1 - c), device_id_type=MESH))
        return cps
    return copies


def _cast_halves(shards, after, *, name, tm=256):
    n = len(shards)
    r, cdim = shards[0].shape
    m = r // 2
    tm = min(tm, m)
    nb = m // tm
    assert m % tm == 0, (m, tm)
    where = jnp.stack([2 * lax.axis_index("x") + lax.axis_index("y"), lax.axis_index("c")]).astype(jnp.int32)

    def body(w_ref, *refs):
        for k in range(n):
            refs[n + 1 + k][...] = refs[k][...].astype(refs[n + 1 + k].dtype)

    return _pcall(
        body, name=name,
        grid_spec=pltpu.PrefetchScalarGridSpec(
            num_scalar_prefetch=1, grid=(nb,),
            in_specs=[pl.BlockSpec((tm, cdim), lambda i, w_ref: (w_ref[1] * nb + i, 0))] * n + [HBM_SPEC],
            out_specs=[pl.BlockSpec((tm, cdim), lambda i, w_ref: ((2 * w_ref[0] + w_ref[1]) * nb + i, 0))] * n),
        out_shape=[jax.ShapeDtypeStruct((N_SHARD * r, cdim), MXU_DTYPE) for _ in shards],
        compiler_params=_params(1),
    )(where, *shards, after)


class _Overlap(_NoOverlap):
    def __init__(self, ffn2_shards, after):
        halves = _cast_halves(ffn2_shards, after, name="ag2_cast")
        self.n = len(halves)
        self.ici = _split_start(halves, _gather_ici_copies(self.n), 4 * self.n, name="ag2_ici_start")
        self.reduced = None

    def start_token(self):
        return self.ici[3]

    def after_attention(self, after):
        send_sems, recv_sems, thru, _ = self.ici
        landed = _split_wait(thru, send_sems, recv_sems, after, _gather_ici_copies(self.n), name="ag2_ici_wait")
        self.d2d = _split_start(landed, _gather_d2d_copies(self.n), 3 * self.n, name="ag2_d2d_start")
        return self.d2d[3]

    def ffn2_weights(self, w, after):
        send_sems, recv_sems, thru, _ = self.d2d
        full = _split_wait(thru, send_sems, recv_sems, after, _gather_d2d_copies(self.n), name="ag2_d2d_wait")
        fs = D_FF // N_SHARD
        return (full[0].reshape(N_SHARD, D_MODEL, fs), full[1].reshape(N_SHARD, D_MODEL, fs),
                full[2].reshape(N_SHARD, fs, D_MODEL))

    def ffn2_grads(self, grads):
        recvs = _swap_halves(grads, name="rs_swap_ffn2")
        ps = _add_halves(grads, recvs, name="rs_add_ffn2")
        lands = [lax.empty(p.shape, p.dtype) for p in ps]
        self.scatter = _split_start(list(ps) + lands, _scatter_copies(len(ps)), 3 * len(ps), name="rs_scatter_ffn2_start")
        return self.scatter[3]

    def mixer_grads(self, dwp, dwo, small):
        self.small_sum = _all_reduce_small(_pack_small(small), name="ar_small")
        gwin = dwp[:, :IN_COLS].reshape(D_MODEL, N_SHARD, IN_SHARD).transpose(1, 0, 2).astype(GRAD_DTYPE)
        gwo = dwo.reshape(N_SHARD, D_MODEL // N_SHARD, D_MODEL).astype(GRAD_DTYPE)
        recvs = _swap_halves([gwin, gwo], name="rs_swap_mix")
        ps = [_add_halves([g], [r], name=f"rs_add_{tag}")[0] for g, r, tag in zip([gwin, gwo], recvs, ["w_in", "w_out"])]
        lands = [lax.empty(p.shape, p.dtype) for p in ps]
        self.scatter_mix = _split_start(ps + lands, _scatter_copies(2), 6, name="rs_scatter_mix_start")
        return self.scatter_mix[3]

    def mixer_reduced(self, after):
        send_sems, recv_sems, thru, _ = self.scatter_mix
        done = _split_wait(thru, send_sems, recv_sems, after, _scatter_copies(2), name="rs_scatter_mix_wait")
        return [_sum_slabs([done[k]], [done[2 + k]], name=f"rs_sum_{tag}")[0] for k, tag in enumerate(["w_in", "w_out"])]

    def before_ffn1_bwd(self, after):
        send_sems, recv_sems, thru, _ = self.scatter
        n = len(thru) // 2
        done = _split_wait(thru, send_sems, recv_sems, after, _scatter_copies(n), name="rs_scatter_ffn2_wait")
        self.reduced = list(_sum_slabs(done[:n], done[n:], name="rs_sum_ffn2"))


def _adamw(gs, ws, ms, vs, *, name, tm=256):
    n = len(gs)
    r, cdim = gs[0].shape
    tm = r if tm is None else min(tm, r)
    assert r % tm == 0, (r, tm)
    c1 = 1.0 / (1.0 - ADAM_B1 ** ADAM_STEP)
    c2 = 1.0 / (1.0 - ADAM_B2 ** ADAM_STEP)

    def body(*refs):
        for k in range(n):
            g = refs[k][...]
            w = refs[n + k][...]
            m = ADAM_B1 * refs[2 * n + k][...] + (1.0 - ADAM_B1) * g
            v = ADAM_B2 * refs[3 * n + k][...] + (1.0 - ADAM_B2) * (g * g)
            refs[4 * n + k][...] = -ADAM_LR * ((m * c1) / (jnp.sqrt(v * c2) + ADAM_EPS) + ADAM_WD * w)
            refs[5 * n + k][...] = m
            refs[6 * n + k][...] = v

    spec = pl.BlockSpec((tm, cdim), lambda i: (i, 0))
    outs = _pcall(
        body, name=name, grid=(r // tm,), in_specs=[spec] * (4 * n), out_specs=[spec] * (3 * n),
        out_shape=[jax.ShapeDtypeStruct((r, cdim), f32)] * (3 * n),
        compiler_params=_params(1),
    )(*gs, *ws, *ms, *vs)
    return outs[:n], outs[n:2 * n], outs[2 * n:]


BIG = ["ffn1_w_gate", "ffn1_w_up", "ffn1_w_down", "ffn2_w_gate", "ffn2_w_up", "ffn2_w_down"]
SMALL = ["ln1_g", "ln1_b", "b_forget", "conv_w", "conv_b", "rg_wa", "rg_ba", "rg_wx", "rg_bx", "lru_lambda",
         "ln2_g", "ln2_b", "ln3_g", "ln3_b"]
WEIGHTS = ["ffn1_w_gate", "ffn1_w_up", "ffn1_w_down", "ln1_g", "ln1_b", "w_in", "b_forget", "conv_w", "conv_b",
           "rg_wa", "rg_ba", "rg_wx", "rg_bx", "lru_lambda", "w_out", "ln2_g", "ln2_b",
           "ffn2_w_gate", "ffn2_w_up", "ffn2_w_down", "ln3_g", "ln3_b"]


def _pack_small(parts):
    rows = []
    for n in SMALL:
        flat = parts[n].reshape(-1)
        pad = (-flat.shape[0]) % LANES
        rows.append(jnp.pad(flat, (0, pad)).reshape(-1, LANES))
    packed = jnp.concatenate(rows, axis=0)
    return jnp.pad(packed, ((0, (-packed.shape[0]) % 8), (0, 0)))


def _unpack_small(packed, shapes):
    out, r0 = {}, 0
    for n in SMALL:
        size = math.prod(shapes[n])
        nr = -(-size // LANES)
        out[n] = packed[r0:r0 + nr].reshape(-1)[:size].reshape(shapes[n])
        r0 += nr
    return out


def kernel(x, ffn1_w_gate, ffn1_w_up, ffn1_w_down, ln1_g, ln1_b, w_in, b_forget, conv_w, conv_b, rg_wa, rg_ba, rg_wx, rg_bx, lru_lambda, w_out, ln2_g, ln2_b, ffn2_w_gate, ffn2_w_up, ffn2_w_down, ln3_g, ln3_b, loss_target, m_ffn1_w_gate, m_ffn1_w_up, m_ffn1_w_down, m_ln1_g, m_ln1_b, m_w_in, m_b_forget, m_conv_w, m_conv_b, m_rg_wa, m_rg_ba, m_rg_wx, m_rg_bx, m_lru_lambda, m_w_out, m_ln2_g, m_ln2_b, m_ffn2_w_gate, m_ffn2_w_up, m_ffn2_w_down, m_ln3_g, m_ln3_b, v_ffn1_w_gate, v_ffn1_w_up, v_ffn1_w_down, v_ln1_g, v_ln1_b, v_w_in, v_b_forget, v_conv_w, v_conv_b, v_rg_wa, v_rg_ba, v_rg_wx, v_rg_bx, v_lru_lambda, v_w_out, v_ln2_g, v_ln2_b, v_ffn2_w_gate, v_ffn2_w_up, v_ffn2_w_down, v_ln3_g, v_ln3_b):
    args = dict(locals())
    w = {n: args[n] for n in WEIGHTS}
    mom = {n: args["m_" + n] for n in WEIGHTS}
    var = {n: args["v_" + n] for n in WEIGHTS}
    chip = 2 * lax.axis_index("x") + lax.axis_index("y")

    g1 = _all_gather_bf16([w[n][0] for n in BIG[:3]] + [w["w_in"][0], w["w_out"][0]], name="ag_first")
    fs = D_FF // N_SHARD
    w_in_full = g1[3].reshape(N_SHARD, D_MODEL, IN_SHARD).transpose(1, 0, 2).reshape(D_MODEL, IN_COLS)
    full = dict(
        f1g=g1[0].reshape(N_SHARD, D_MODEL, fs), f1u=g1[1].reshape(N_SHARD, D_MODEL, fs),
        f1d=g1[2].reshape(N_SHARD, fs, D_MODEL),
        wp=make_wp(w_in_full), bfp=jnp.pad(b_forget, ((0, 0), (0, LANES - HEADS))), wo=g1[4],
        ln1_g=ln1_g, ln1_b=ln1_b, ln2_g=ln2_g, ln2_b=ln2_b, ln3_g=ln3_g, ln3_b=ln3_b,
        conv_b=conv_b, rg_wa=rg_wa[0], rg_wx=rg_wx[0], rg_ba=rg_ba[0], rg_bx=rg_bx[0], lam=lru_lambda,
    )
    cw_place = lax.dynamic_update_slice(jnp.zeros((8, LRU_W), f32), conv_w[0] * 0.5, (0, chip * (LRU_W // N_SHARD)))
    cw_full = _all_reduce_small(cw_place.reshape(-1, LANES), g1[0], name="ag_conv_w")
    full["conv_w"] = cw_full.reshape(8, LRU_W)[:CONV_K]

    hooks = _Overlap([w[n][0] for n in BIG[3:]], cw_full)
    loss_rep, dx, g = _local_step(x[0], loss_target[0], full, hooks)
    loss = lax.psum(loss_rep[0, 0], ("x", "y", "c"))

    gs1 = [g["f1g"], g["f1u"], g["f1d"]]
    ps1 = _add_halves(gs1, _swap_halves(gs1, name="rs_swap_ffn1"), name="rs_add_ffn1")
    lands = [lax.empty(p.shape, p.dtype) for p in ps1]
    send1, recv1, thru1, token1 = _split_start(list(ps1) + lands, _scatter_copies(3), 9, name="rs_scatter_ffn1_start")
    red = _join_halves(hooks.reduced + hooks.mixer_reduced([token1]), name="rs_join_rest")
    grads = dict(zip(BIG[3:] + ["w_in", "w_out"], red))

    small_shapes = {n: w[n].shape for n in SMALL}
    small_shapes["conv_w"] = (1, CONV_K, LRU_W)
    gs_red = _unpack_small(hooks.small_sum, small_shapes)
    gs_red["conv_w"] = lax.dynamic_slice(gs_red["conv_w"], (0, 0, chip * (LRU_W // N_SHARD)),
                                         (1, CONV_K, LRU_W // N_SHARD))
    grads.update(gs_red)

    delta, new_m, new_v = {}, {}, {}

    def adamw(names, name, **kw):
        d, nm, nv = _adamw([grads[n] for n in names], [w[n][0] for n in names], [mom[n][0] for n in names],
                           [var[n][0] for n in names], name=name, **kw)
        for i, n in enumerate(names):
            delta[n], new_m[n], new_v[n] = d[i], nm[i], nv[i]

    adamw(BIG[3:], "adamw_ffn2", tm=128)
    adamw(["w_in"], "adamw_w_in")
    adamw(["w_out"], "adamw_w_out")
    shard_shapes = {n: w[n].shape for n in SMALL}
    d, nm, nv = _adamw([_pack_small({n: grads[n] for n in SMALL})], [_pack_small({n: w[n] for n in SMALL})],
                       [_pack_small({n: mom[n] for n in SMALL})], [_pack_small({n: var[n] for n in SMALL})],
                       name="adamw_small", tm=None)
    for dst, packed in ((delta, d[0]), (new_m, nm[0]), (new_v, nv[0])):
        dst.update(_unpack_small(packed, shard_shapes))

    worked = [new_v["ffn2_w_down"], new_v["w_in"], new_v["w_out"], nv[0]]
    done1 = _split_wait(thru1, send1, recv1, worked, _scatter_copies(3), name="rs_scatter_ffn1_wait")
    red1 = _join_halves(list(_sum_slabs(done1[:3], done1[3:], name="rs_sum_ffn1")), name="rs_join_ffn1")
    grads.update(zip(BIG[:3], red1))
    adamw(BIG[:3], "adamw_ffn1", tm=128)

    def shaped(tree, n):
        return tree[n].reshape(w[n].shape)

    return (loss, dx[None], *[shaped(grads, n) for n in WEIGHTS], *[shaped(delta, n) for n in WEIGHTS],
            *[shaped(new_m, n) for n in WEIGHTS], *[shaped(new_v, n) for n in WEIGHTS])
```

```python
import functools
import math

import jax
import jax.numpy as jnp
from jax import lax
from jax.experimental import pallas as pl
from jax.experimental.pallas import tpu as pltpu

f32 = jnp.float32
MXU_DTYPE = jnp.bfloat16
GRAD_DTYPE = jnp.bfloat16

D_MODEL = 1024
D_FF = 4096
N_SHARD = 4
N_DEV = 8
FOX_W = 512
LRU_W = 512
HEADS = 8
HEAD_DIM = 64
CONV_K = 4
IN_COLS = 2568
IN_SHARD = IN_COLS // N_SHARD
QKV_W = 3 * FOX_W
Z_PAD = 2688
LANES = 128
LN_EPS = 1e-5
DN_ALPHA = 2.0 ** 0.25
LRU_C = 8.0
NEG_BIG = -1e30
VMEM_LIMIT = 56 * 1024 * 1024

ADAM_LR = 0.001
ADAM_B1 = 0.9
ADAM_B2 = 0.999
ADAM_EPS = 1e-08
ADAM_WD = 0.01
ADAM_STEP = 10


def _pcall(body, **kw):
    return pl.pallas_call(body, **kw)


def _params(n_grid, vmem=VMEM_LIMIT):
    return pltpu.CompilerParams(dimension_semantics=("arbitrary",) * n_grid, vmem_limit_bytes=vmem)


def _dot(a, b):
    return jnp.dot(a, b, preferred_element_type=f32)


def _dot_nt(a, b):
    return lax.dot_general(a, b, (((1,), (1,)), ((), ())), preferred_element_type=f32)


def _dot_tn(a, b):
    return lax.dot_general(a, b, (((0,), (0,)), ((), ())), preferred_element_type=f32)


def _sigmoid(x):
    return 1.0 / (1.0 + jnp.exp(-x))


def _layer_norm_stats(y):
    mu = jnp.mean(y, axis=-1, keepdims=True)
    yc = y - mu
    var = jnp.mean(yc * yc, axis=-1, keepdims=True)
    rstd = lax.rsqrt(var + LN_EPS)
    return yc * rstd, rstd


def _ln_backward(dy, xhat, rstd, gamma):
    dxhat = dy * gamma
    m1 = jnp.mean(dxhat, axis=-1, keepdims=True)
    m2 = jnp.mean(dxhat * xhat, axis=-1, keepdims=True)
    dyp = rstd * (dxhat - m1 - xhat * m2)
    return dyp, jnp.sum(dy * xhat, axis=0, keepdims=True), jnp.sum(dy, axis=0, keepdims=True)


def _ffn_fwd(x, wg, wu, wd, ln_g, ln_b, *, name, tm=1024, tf=512):
    T = x.shape[0]
    tm = min(tm, T)
    fs = D_FF // N_SHARD
    cpf = fs // tf
    nf = D_FF // tf
    nt = T // tm

    def body(x_ref, wg_ref, wu_ref, wd_ref, g_ref, b_ref,
             xb_ref, gact_ref, uact_ref, xhat_ref, xn_ref, rstd_ref, acc_ref):
        f = pl.program_id(1)

        @pl.when(f == 0)
        def _():
            xb_ref[...] = x_ref[...].astype(MXU_DTYPE)
            acc_ref[...] = jnp.zeros_like(acc_ref)

        xb = xb_ref[...]
        g = _dot(xb, wg_ref[...])
        u = _dot(xb, wu_ref[...])
        h = (g * _sigmoid(g)) * u
        gact_ref[...] = g.astype(gact_ref.dtype)
        uact_ref[...] = u.astype(uact_ref.dtype)
        acc_ref[...] += _dot(h.astype(MXU_DTYPE), wd_ref[...])

        @pl.when(f == nf - 1)
        def _():
            y = DN_ALPHA * x_ref[...] + 0.5 * acc_ref[...]
            xhat, rstd = _layer_norm_stats(y)
            xhat_ref[...] = xhat
            xn_ref[...] = (xhat * g_ref[...] + b_ref[...]).astype(xn_ref.dtype)
            rstd_ref[...] = jnp.broadcast_to(rstd, rstd_ref.shape)

    row = lambda i, f: (i, 0)
    return _pcall(
        body, name=name, grid=(nt, nf),
        in_specs=[
            pl.BlockSpec((tm, D_MODEL), row),
            pl.BlockSpec((None, D_MODEL, tf), lambda i, f: (f // cpf, 0, f % cpf)),
            pl.BlockSpec((None, D_MODEL, tf), lambda i, f: (f // cpf, 0, f % cpf)),
            pl.BlockSpec((None, tf, D_MODEL), lambda i, f: (f // cpf, f % cpf, 0)),
            pl.BlockSpec((1, D_MODEL), lambda i, f: (0, 0)),
            pl.BlockSpec((1, D_MODEL), lambda i, f: (0, 0)),
        ],
        out_specs=[
            pl.BlockSpec((tm, D_MODEL), row),
            pl.BlockSpec((tm, tf), lambda i, f: (i, f)),
            pl.BlockSpec((tm, tf), lambda i, f: (i, f)),
            pl.BlockSpec((tm, D_MODEL), row),
            pl.BlockSpec((tm, D_MODEL), row),
            pl.BlockSpec((tm, LANES), row),
        ],
        out_shape=[
            jax.ShapeDtypeStruct((T, D_MODEL), MXU_DTYPE),
            jax.ShapeDtypeStruct((T, D_FF), MXU_DTYPE),
            jax.ShapeDtypeStruct((T, D_FF), MXU_DTYPE),
            jax.ShapeDtypeStruct((T, D_MODEL), f32),
            jax.ShapeDtypeStruct((T, D_MODEL), MXU_DTYPE),
            jax.ShapeDtypeStruct((T, LANES), f32),
        ],
        scratch_shapes=[pltpu.VMEM((tm, D_MODEL), f32)],
        compiler_params=_params(2),
    )(x, wg, wu, wd, ln_g, ln_b)


def _ffn_bwd(dyp, xb, gact, uact, wg, wu, wd, after=None, *, name, tm=512, tf=512):
    T = dyp.shape[0]
    tm = min(tm, T)
    fs = D_FF // N_SHARD
    cpf = fs // tf
    nf = D_FF // tf
    nt = T // tm
    n_tiles = nf * nt
    assert n_tiles % 2 == 0
    extra = [] if after is None else [after]

    def body(dyp_ref, g_ref, u_ref, wd_ref, xb_ref, wg_ref, wu_ref, *refs):
        (dx_hbm, dwg_ref, dwu_ref, dwd_ref, dx_sc, dwg_sc, dwu_sc, dwd_sc,
         dg0, du0, hb0, dy0, dg1, du1, hb1, dy1, sem) = refs[len(extra):]
        slots = ((dg0, du0, hb0, dy0), (dg1, du1, hb1, dy1))
        s = pl.program_id(0)
        a_tok = lax.rem(jnp.minimum(s, n_tiles - 1), nt)
        b = jnp.maximum(s - 1, 0)
        b_tok = lax.rem(b, nt)
        rows_a = pl.ds(pl.multiple_of(a_tok * tm, tm), tm)
        rows_b = pl.ds(pl.multiple_of(b_tok * tm, tm), tm)

        def stage_a(slot):
            dg_s, du_s, hb_s, dy_s = slots[slot]
            dy = (0.5 * dyp_ref[...]).astype(MXU_DTYPE)
            g = g_ref[...].astype(f32)
            u = u_ref[...].astype(f32)
            sig = _sigmoid(g)
            silu = g * sig
            dh = _dot_nt(dy, wd_ref[...])
            dg_s[...] = (dh * u * (sig * (1.0 + g * (1.0 - sig)))).astype(MXU_DTYPE)
            du_s[...] = (dh * silu).astype(MXU_DTYPE)
            hb_s[...] = (silu * u).astype(MXU_DTYPE)
            dy_s[...] = dy

        def stage_b(slot):
            dg_s, du_s, hb_s, dy_s = slots[slot]
            dg = dg_s[...]
            du = du_s[...]
            dx_sc[rows_b, :] += _dot_nt(dg, wg_ref[...]) + _dot_nt(du, wu_ref[...])
            xb_t = xb_ref[...]
            dwg_sc[...] += _dot_tn(xb_t, dg)
            dwu_sc[...] += _dot_tn(xb_t, du)
            dwd_sc[...] += _dot_tn(hb_s[...], dy_s[...])

        @pl.when(s < nt)
        def _():
            dx_sc[rows_a, :] = DN_ALPHA * dyp_ref[...]

        @pl.when(jnp.logical_and(s >= 1, b_tok == 0))
        def _():
            dwg_sc[...] = jnp.zeros_like(dwg_sc)
            dwu_sc[...] = jnp.zeros_like(dwu_sc)
            dwd_sc[...] = jnp.zeros_like(dwd_sc)

        even = lax.rem(s, 2) == 0
        middle = jnp.logical_and(s >= 1, s < n_tiles)

        @pl.when(s == 0)
        def _():
            stage_a(0)

        @pl.when(jnp.logical_and(middle, even))
        def _():
            stage_a(0)
            stage_b(1)

        @pl.when(jnp.logical_and(middle, jnp.logical_not(even)))
        def _():
            stage_a(1)
            stage_b(0)

        @pl.when(s == n_tiles)
        def _():
            stage_b(1)

        @pl.when(jnp.logical_and(s >= 1, b_tok == nt - 1))
        def _():
            dwg_ref[...] = dwg_sc[...].astype(dwg_ref.dtype)
            dwu_ref[...] = dwu_sc[...].astype(dwu_ref.dtype)
            dwd_ref[...] = dwd_sc[...].astype(dwd_ref.dtype)

        @pl.when(s == n_tiles)
        def _():
            cp = pltpu.make_async_copy(dx_sc, dx_hbm, sem)
            cp.start()
            cp.wait()

    a_of = lambda s: jnp.minimum(s, n_tiles - 1)
    b_of = lambda s: jnp.maximum(s - 1, 0)
    cols = lambda f: (f // cpf, 0, f % cpf)
    rows_of = lambda f: (f // cpf, f % cpf, 0)
    return _pcall(
        body, name=name, grid=(n_tiles + 1,),
        in_specs=[
            pl.BlockSpec((tm, D_MODEL), lambda s: (a_of(s) % nt, 0)),
            pl.BlockSpec((tm, tf), lambda s: (a_of(s) % nt, a_of(s) // nt)),
            pl.BlockSpec((tm, tf), lambda s: (a_of(s) % nt, a_of(s) // nt)),
            pl.BlockSpec((None, tf, D_MODEL), lambda s: rows_of(a_of(s) // nt)),
            pl.BlockSpec((tm, D_MODEL), lambda s: (b_of(s) % nt, 0)),
            pl.BlockSpec((None, D_MODEL, tf), lambda s: cols(b_of(s) // nt)),
            pl.BlockSpec((None, D_MODEL, tf), lambda s: cols(b_of(s) // nt)),
        ] + [pl.BlockSpec(memory_space=pl.ANY)] * len(extra),
        out_specs=[
            pl.BlockSpec(memory_space=pl.ANY),
            pl.BlockSpec((None, D_MODEL, tf), lambda s: cols(b_of(s) // nt)),
            pl.BlockSpec((None, D_MODEL, tf), lambda s: cols(b_of(s) // nt)),
            pl.BlockSpec((None, tf, D_MODEL), lambda s: rows_of(b_of(s) // nt)),
        ],
        out_shape=[
            jax.ShapeDtypeStruct((T, D_MODEL), f32),
            jax.ShapeDtypeStruct((N_SHARD, D_MODEL, fs), GRAD_DTYPE),
            jax.ShapeDtypeStruct((N_SHARD, D_MODEL, fs), GRAD_DTYPE),
            jax.ShapeDtypeStruct((N_SHARD, fs, D_MODEL), GRAD_DTYPE),
        ],
        scratch_shapes=[pltpu.VMEM((T, D_MODEL), f32), pltpu.VMEM((D_MODEL, tf), f32),
                        pltpu.VMEM((D_MODEL, tf), f32), pltpu.VMEM((tf, D_MODEL), f32)]
        + [pltpu.VMEM((tm, tf), MXU_DTYPE), pltpu.VMEM((tm, tf), MXU_DTYPE), pltpu.VMEM((tm, tf), MXU_DTYPE),
           pltpu.VMEM((tm, D_MODEL), MXU_DTYPE)] * 2
        + [pltpu.SemaphoreType.DMA],
        compiler_params=_params(1),
    )(dyp, gact, uact, wd, xb, wg, wu, *extra)


def _loss_ln_bwd(xhat, rstd, ln_g, ln_b, target, *, name, tm=512):
    T = xhat.shape[0]
    tm = min(tm, T)
    nt = T // tm

    def body(xhat_ref, rstd_ref, g_ref, b_ref, t_ref, dyp_ref, dg_ref, db_ref, loss_ref):
        i = pl.program_id(0)

        @pl.when(i == 0)
        def _():
            dg_ref[...] = jnp.zeros_like(dg_ref)
            db_ref[...] = jnp.zeros_like(db_ref)
            loss_ref[...] = jnp.zeros_like(loss_ref)

        xhat_t = xhat_ref[...]
        gamma = g_ref[...]
        err = xhat_t * gamma + b_ref[...] - t_ref[...]
        sq = jnp.sum(jnp.sum(err * err, axis=0, keepdims=True), axis=1, keepdims=True)
        loss_ref[...] += jnp.broadcast_to(sq * (0.5 / D_MODEL), loss_ref.shape)
        dy = err * (1.0 / D_MODEL)
        dyp, dgam, dbeta = _ln_backward(dy, xhat_t, rstd_ref[:, 0:1], gamma)
        dyp_ref[...] = dyp
        dg_ref[...] += dgam
        db_ref[...] += dbeta

    row = lambda i: (i, 0)
    const = lambda i: (0, 0)
    return _pcall(
        body, name=name, grid=(nt,),
        in_specs=[pl.BlockSpec((tm, D_MODEL), row), pl.BlockSpec((tm, LANES), row),
                  pl.BlockSpec((1, D_MODEL), const), pl.BlockSpec((1, D_MODEL), const),
                  pl.BlockSpec((tm, D_MODEL), row)],
        out_specs=[pl.BlockSpec((tm, D_MODEL), row), pl.BlockSpec((1, D_MODEL), const),
                   pl.BlockSpec((1, D_MODEL), const), pl.BlockSpec((1, LANES), const)],
        out_shape=[jax.ShapeDtypeStruct((T, D_MODEL), f32), jax.ShapeDtypeStruct((1, D_MODEL), f32),
                   jax.ShapeDtypeStruct((1, D_MODEL), f32), jax.ShapeDtypeStruct((1, LANES), f32)],
        compiler_params=_params(1),
    )(xhat, rstd, ln_g, ln_b, target)


def _ln_bwd(dy, xhat, rstd, ln_g, *, name, tm=512):
    T = xhat.shape[0]
    tm = min(tm, T)
    nt = T // tm

    def body(dy_ref, xhat_ref, rstd_ref, g_ref, dyp_ref, dg_ref, db_ref):
        i = pl.program_id(0)

        @pl.when(i == 0)
        def _():
            dg_ref[...] = jnp.zeros_like(dg_ref)
            db_ref[...] = jnp.zeros_like(db_ref)

        dyp, dgam, dbeta = _ln_backward(dy_ref[...], xhat_ref[...], rstd_ref[:, 0:1], g_ref[...])
        dyp_ref[...] = dyp
        dg_ref[...] += dgam
        db_ref[...] += dbeta

    row = lambda i: (i, 0)
    const = lambda i: (0, 0)
    return _pcall(
        body, name=name, grid=(nt,),
        in_specs=[pl.BlockSpec((tm, D_MODEL), row), pl.BlockSpec((tm, D_MODEL), row),
                  pl.BlockSpec((tm, LANES), row), pl.BlockSpec((1, D_MODEL), const)],
        out_specs=[pl.BlockSpec((tm, D_MODEL), row), pl.BlockSpec((1, D_MODEL), const),
                   pl.BlockSpec((1, D_MODEL), const)],
        out_shape=[jax.ShapeDtypeStruct((T, D_MODEL), f32), jax.ShapeDtypeStruct((1, D_MODEL), f32),
                   jax.ShapeDtypeStruct((1, D_MODEL), f32)],
        compiler_params=_params(1),
    )(dy, xhat, rstd, ln_g)


def _proj_in(xn, wp, bfp, *, name, tm=512):
    T = xn.shape[0]
    tm = min(tm, T)
    nt = T // tm

    def body(x_ref, w_ref, b_ref, qkv_ref, lxg_ref, fg_ref):
        z = _dot(x_ref[...], w_ref[...])
        qkv_ref[...] = z[:, :QKV_W].astype(qkv_ref.dtype)
        lxg_ref[...] = z[:, QKV_W:QKV_W + 2 * LRU_W]
        fg_ref[...] = z[:, QKV_W + 2 * LRU_W:] + b_ref[...]

    row = lambda i: (i, 0)
    const = lambda i: (0, 0)
    return _pcall(
        body, name=name, grid=(nt,),
        in_specs=[pl.BlockSpec((tm, D_MODEL), row), pl.BlockSpec((D_MODEL, Z_PAD), const),
                  pl.BlockSpec((1, LANES), const)],
        out_specs=[pl.BlockSpec((tm, QKV_W), row), pl.BlockSpec((tm, 2 * LRU_W), row),
                   pl.BlockSpec((tm, LANES), row)],
        out_shape=[jax.ShapeDtypeStruct((T, QKV_W), MXU_DTYPE), jax.ShapeDtypeStruct((T, 2 * LRU_W), f32),
                   jax.ShapeDtypeStruct((T, LANES), f32)],
        compiler_params=_params(1),
    )(xn, wp, bfp)


def _proj_in_bwd(dqa, dka, dva, dlxg, dfg, xn, dyp, wp, *, name, tm=512):
    T = xn.shape[0]
    tm = min(tm, T)
    nt = T // tm

    def body(dq_ref, dk_ref, dv_ref, dl_ref, dfg_ref, x_ref, dyp_ref, w_ref, dx_ref, dw_hbm, dw_sc, sem):
        i = pl.program_id(0)

        @pl.when(i == 0)
        def _():
            dw_sc[...] = jnp.zeros_like(dw_sc)

        low = _low_lanes((tm, LANES))

        def packed(ref):
            pairs = [jnp.where(low, ref[:, (2 * j) * LANES:(2 * j + 1) * LANES],
                               _swap_lane_halves(ref[:, (2 * j + 1) * LANES:(2 * j + 2) * LANES]))
                     for j in range(HEADS // 2)]
            return jnp.concatenate(pairs, axis=1).astype(MXU_DTYPE)

        dz = jnp.concatenate(
            [packed(dq_ref), packed(dk_ref), packed(dv_ref),
             dl_ref[...].astype(MXU_DTYPE), dfg_ref[...].astype(MXU_DTYPE)], axis=1)
        dx_ref[...] = DN_ALPHA * dyp_ref[...] + _dot_nt(dz, w_ref[...])
        dw_sc[...] += _dot_tn(x_ref[...], dz)

        @pl.when(i == nt - 1)
        def _():
            dw_sc[:, :FOX_W] = dw_sc[:, :FOX_W] * (1.0 / math.sqrt(HEAD_DIM))
            cp = pltpu.make_async_copy(dw_sc, dw_hbm, sem)
            cp.start()
            cp.wait()

    row = lambda i: (i, 0)
    const = lambda i: (0, 0)
    return _pcall(
        body, name=name, grid=(nt,),
        in_specs=[pl.BlockSpec((tm, HEADS * LANES), row), pl.BlockSpec((tm, HEADS * LANES), row),
                  pl.BlockSpec((tm, HEADS * LANES), row),
                  pl.BlockSpec((tm, 2 * LRU_W), row), pl.BlockSpec((tm, LANES), row),
                  pl.BlockSpec((tm, D_MODEL), row), pl.BlockSpec((tm, D_MODEL), row),
                  pl.BlockSpec((D_MODEL, Z_PAD), const)],
        out_specs=[pl.BlockSpec((tm, D_MODEL), row), pl.BlockSpec(memory_space=pl.ANY)],
        out_shape=[jax.ShapeDtypeStruct((T, D_MODEL), f32), jax.ShapeDtypeStruct((D_MODEL, Z_PAD), f32)],
        scratch_shapes=[pltpu.VMEM((D_MODEL, Z_PAD), f32), pltpu.SemaphoreType.DMA],
        compiler_params=_params(1),
    )(dqa, dka, dva, dlxg, dfg, xn, dyp, wp)


def _split3(x):
    hi = x.astype(jnp.bfloat16)
    r1 = x - hi.astype(f32)
    mid = r1.astype(jnp.bfloat16)
    lo = (r1 - mid.astype(f32)).astype(jnp.bfloat16)
    return hi, mid, lo


def _tri_dot(tri, x):
    hi, mid, lo = _split3(x)
    return _dot(tri, hi) + _dot(tri, mid) + _dot(tri, lo)


FOX_PAD = HEADS * LANES
AUX = HEAD_DIM


def _low_lanes(shape):
    return lax.broadcasted_iota(jnp.int32, shape, 1) < HEAD_DIM


def _swap_lane_halves(x):
    return pltpu.roll(x, HEAD_DIM, 1)


def _fox_prep(qkv, fgb, *, name, tm=512):
    T = fgb.shape[0]
    tm = min(tm, T)
    nt = T // tm

    def body(qkv_ref, fg_ref, qa_ref, ka_ref, va_ref, carry):
        i = pl.program_id(0)

        @pl.when(i == 0)
        def _():
            carry[...] = jnp.zeros_like(carry)

        x = fg_ref[...]
        ls = jnp.minimum(x, 0.0) - jnp.log(1.0 + jnp.exp(-jnp.abs(x)))
        r = lax.broadcasted_iota(jnp.int32, (tm, tm), 0)
        c = lax.broadcasted_iota(jnp.int32, (tm, tm), 1)
        tri = jnp.where(r >= c, 1.0, 0.0).astype(jnp.bfloat16)
        cum = _tri_dot(tri, ls) + carry[0:1, :]
        carry[...] = jnp.broadcast_to(cum[tm - 1:tm, :], carry.shape)

        lane = lax.broadcasted_iota(jnp.int32, (tm, LANES), 1)
        low = lane < HEAD_DIM
        ones_q = jnp.where(jnp.logical_and(lane >= AUX + 3, lane < AUX + 6), 1.0, 0.0)
        ones_k = jnp.where(jnp.logical_and(lane >= AUX, lane < AUX + 3), 1.0, 0.0)
        for j in range(HEADS // 2):
            pair = [qkv_ref[:, t * FOX_W + j * LANES:t * FOX_W + (j + 1) * LANES].astype(f32) for t in range(3)]
            for odd in range(2):
                h = 2 * j + odd
                q, k, v = [_swap_lane_halves(a) if odd else a for a in pair]
                hi, mid, lo = [a.astype(f32) for a in _split3(jnp.broadcast_to(cum[:, h:h + 1], (tm, LANES)))]
                aux_q = jnp.where(lane == AUX, hi, jnp.where(lane == AUX + 1, mid, jnp.where(lane == AUX + 2, lo, ones_q)))
                aux_k = jnp.where(lane == AUX + 3, -hi,
                                  jnp.where(lane == AUX + 4, -mid, jnp.where(lane == AUX + 5, -lo, ones_k)))
                blk = slice(h * LANES, (h + 1) * LANES)
                qa_ref[:, blk] = jnp.where(low, q, aux_q).astype(qa_ref.dtype)
                ka_ref[:, blk] = jnp.where(low, k, aux_k).astype(ka_ref.dtype)
                va_ref[:, blk] = jnp.where(low, v, 1.0).astype(va_ref.dtype)

    row = lambda i: (i, 0)
    return _pcall(
        body, name=name, grid=(nt,),
        in_specs=[pl.BlockSpec((tm, QKV_W), row), pl.BlockSpec((tm, LANES), row)],
        out_specs=[pl.BlockSpec((tm, FOX_PAD), row)] * 3,
        out_shape=[jax.ShapeDtypeStruct((T, FOX_PAD), MXU_DTYPE)] * 3,
        scratch_shapes=[pltpu.VMEM((8, LANES), f32)],
        compiler_params=_params(1),
    )(qkv, fgb)


def _future_keys(tq, tk):
    r = lax.broadcasted_iota(jnp.int32, (tq, tk), 0)
    c = lax.broadcasted_iota(jnp.int32, (tq, tk), 1)
    return c > r


def _causal_steps(nq, key_major):
    if key_major:
        pairs = [(qi, ki) for ki in range(nq) for qi in range(ki, nq)]
    else:
        pairs = [(qi, ki) for qi in range(nq) for ki in range(qi + 1)]
    return (jnp.asarray([p[0] for p in pairs], jnp.int32), jnp.asarray([p[1] for p in pairs], jnp.int32))


def _fox_fwd(qa, ka, va, *, name, tq=512):
    T = qa.shape[0]
    tq = min(tq, T)
    tk = tq
    nq = T // tq
    rep = tk // LANES
    qi_tab, ki_tab = _causal_steps(nq, key_major=False)

    def body(qi_ref, ki_ref, qa_ref, ka_ref, va_ref, o_ref, lse_ref, m_sc, acc_sc):
        t = pl.program_id(1)
        qi = qi_ref[t]
        ki = ki_ref[t]

        @pl.when(ki == 0)
        def _():
            m_sc[...] = jnp.full_like(m_sc, NEG_BIG)
            acc_sc[...] = jnp.zeros_like(acc_sc)

        def tile(diagonal):
            for h in range(2):
                blk = slice(h * LANES, (h + 1) * LANES)
                s = _dot_nt(qa_ref[:, blk], ka_ref[:, blk])
                if diagonal:
                    s = jnp.where(_future_keys(tq, tk), NEG_BIG, s)
                m_prev = m_sc[h]
                m_new = jnp.maximum(m_prev, jnp.max(s, axis=1, keepdims=True))
                p = jnp.exp(s - jnp.tile(m_new, (1, rep)))
                acc_sc[h] = jnp.exp(m_prev - m_new) * acc_sc[h] + _dot(p.astype(MXU_DTYPE), va_ref[:, blk])
                m_sc[h] = m_new

        @pl.when(ki < qi)
        def _():
            tile(False)

        @pl.when(ki == qi)
        def _():
            tile(True)
            low = _low_lanes((tq, LANES))
            outs = []
            for h in range(2):
                acc = acc_sc[h]
                den = _swap_lane_halves(acc)
                outs.append(acc / den)
                lse_ref[h] = m_sc[h] + jnp.log(jnp.where(low, den, acc))
            o_ref[...] = jnp.where(low, outs[0], _swap_lane_halves(outs[1]))

    pair = 2 * LANES
    return _pcall(
        body, name=name,
        grid_spec=pltpu.PrefetchScalarGridSpec(
            num_scalar_prefetch=2, grid=(HEADS // 2, qi_tab.shape[0]),
            in_specs=[
                pl.BlockSpec((tq, pair), lambda j, t, qi_ref, ki_ref: (qi_ref[t], j)),
                pl.BlockSpec((tk, pair), lambda j, t, qi_ref, ki_ref: (ki_ref[t], j)),
                pl.BlockSpec((tk, pair), lambda j, t, qi_ref, ki_ref: (ki_ref[t], j)),
            ],
            out_specs=[pl.BlockSpec((tq, LANES), lambda j, t, qi_ref, ki_ref: (qi_ref[t], j)),
                       pl.BlockSpec((2, tq, LANES), lambda j, t, qi_ref, ki_ref: (j, qi_ref[t], 0))],
            scratch_shapes=[pltpu.VMEM((2, tq, LANES), f32)] * 2),
        out_shape=[jax.ShapeDtypeStruct((T, FOX_W), f32), jax.ShapeDtypeStruct((HEADS, T, LANES), f32)],
        compiler_params=_params(2),
    )(qi_tab, ki_tab, qa, ka, va)


def _fox_bwd_prep(do, o, *, name, tm=512):
    T = o.shape[0]
    tm = min(tm, T)
    nt = T // tm

    def body(do_ref, o_ref, d_ref, doa_ref):
        low = _low_lanes((tm, LANES))
        for j in range(HEADS // 2):
            do2 = do_ref[:, j * LANES:(j + 1) * LANES].astype(f32)
            prod = do2 * o_ref[:, j * LANES:(j + 1) * LANES]
            for odd in range(2):
                h = 2 * j + odd
                mine = jnp.where(low, _swap_lane_halves(prod) if odd else prod, 0.0)
                d_ref[h] = jnp.broadcast_to(jnp.sum(mine, axis=1, keepdims=True), (tm, LANES))
                doh = jnp.where(low, _swap_lane_halves(do2) if odd else do2, 0.0)
                doa_ref[:, h * LANES:(h + 1) * LANES] = doh.astype(doa_ref.dtype)

    return _pcall(
        body, name=name, grid=(nt,),
        in_specs=[pl.BlockSpec((tm, FOX_W), lambda i: (i, 0)), pl.BlockSpec((tm, FOX_W), lambda i: (i, 0))],
        out_specs=[pl.BlockSpec((HEADS, tm, LANES), lambda i: (0, i, 0)), pl.BlockSpec((tm, FOX_PAD), lambda i: (i, 0))],
        out_shape=[jax.ShapeDtypeStruct((HEADS, T, LANES), f32), jax.ShapeDtypeStruct((T, FOX_PAD), MXU_DTYPE)],
        compiler_params=_params(1),
    )(do, o)


def _fox_bwd(qa, ka, va, doa, lse, drep, *, name, tq=512):
    T = qa.shape[0]
    tq = min(tq, T)
    tk = tq
    nq = T // tq
    rep = tk // LANES
    qi_tab, ki_tab = _causal_steps(nq, key_major=True)

    def body(qi_ref, ki_ref, qa_ref, ka_ref, va_ref, doa_ref, lse_ref, d_ref, dqa_ref, dka_ref, dva_ref, dk_sc, dv_sc):
        t = pl.program_id(1)
        qi = qi_ref[t]
        ki = ki_ref[t]
        rows = pl.ds(pl.multiple_of(qi * tq, tq), tq)

        @pl.when(t == 0)
        def _():
            dqa_ref[...] = jnp.zeros_like(dqa_ref)

        @pl.when(qi == ki)
        def _():
            dk_sc[...] = jnp.zeros_like(dk_sc)
            dv_sc[...] = jnp.zeros_like(dv_sc)

        def tile(diagonal):
            for h in range(2):
                blk = slice(h * LANES, (h + 1) * LANES)
                qh, kh, doh = qa_ref[:, blk], ka_ref[:, blk], doa_ref[:, blk]
                p = jnp.exp(_dot_nt(qh, kh) - jnp.tile(lse_ref[h], (1, rep)))
                if diagonal:
                    p = jnp.where(_future_keys(tq, tk), 0.0, p)
                dp = _dot_nt(doh, va_ref[:, blk])
                ds = (p * (dp - jnp.tile(d_ref[h], (1, rep)))).astype(MXU_DTYPE)
                dv_sc[h] += _dot_tn(p.astype(MXU_DTYPE), doh)
                dk_sc[h] += _dot_tn(ds, qh)
                dqa_ref[rows, blk] += _dot(ds, kh)

        @pl.when(qi > ki)
        def _():
            tile(False)

        @pl.when(qi == ki)
        def _():
            tile(True)

        @pl.when(qi == nq - 1)
        def _():
            for h in range(2):
                blk = slice(h * LANES, (h + 1) * LANES)
                dka_ref[:, blk] = dk_sc[h]
                dva_ref[:, blk] = dv_sc[h]

    pair = 2 * LANES
    q_blk = lambda j, t, qi_ref, ki_ref: (qi_ref[t], j)
    k_blk = lambda j, t, qi_ref, ki_ref: (ki_ref[t], j)
    stat = pl.BlockSpec((2, tq, LANES), lambda j, t, qi_ref, ki_ref: (j, qi_ref[t], 0))
    return _pcall(
        body, name=name,
        grid_spec=pltpu.PrefetchScalarGridSpec(
            num_scalar_prefetch=2, grid=(HEADS // 2, qi_tab.shape[0]),
            in_specs=[pl.BlockSpec((tq, pair), q_blk), pl.BlockSpec((tk, pair), k_blk), pl.BlockSpec((tk, pair), k_blk),
                      pl.BlockSpec((tq, pair), q_blk), stat, stat],
            out_specs=[pl.BlockSpec((T, pair), lambda j, t, qi_ref, ki_ref: (0, j)),
                       pl.BlockSpec((tk, pair), k_blk), pl.BlockSpec((tk, pair), k_blk)],
            scratch_shapes=[pltpu.VMEM((2, tk, LANES), f32)] * 2),
        out_shape=[jax.ShapeDtypeStruct((T, FOX_PAD), f32)] * 3,
        compiler_params=_params(2),
    )(qi_tab, ki_tab, qa, ka, va, doa, lse, drep)


def _fox_bwd_post(dqa, dka, fgb, *, name, tm=512):
    T = fgb.shape[0]
    tm = min(tm, T)
    nt = T // tm

    def body(dqa_ref, dka_ref, fg_ref, dfg_ref, dbf_ref, carry):
        i = pl.program_id(0)

        @pl.when(i == 0)
        def _():
            carry[...] = jnp.zeros_like(carry)
            dbf_ref[...] = jnp.zeros_like(dbf_ref)

        lane = lax.broadcasted_iota(jnp.int32, (tm, LANES), 1)
        dc = jnp.zeros((tm, LANES), f32)
        for h in range(HEADS):
            row_sum = dqa_ref[:, h * LANES + AUX:h * LANES + AUX + 1]
            col_sum = dka_ref[:, h * LANES + AUX + 3:h * LANES + AUX + 4]
            dc = jnp.where(lane == h, jnp.broadcast_to(row_sum - col_sum, (tm, LANES)), dc)
        r = lax.broadcasted_iota(jnp.int32, (tm, tm), 0)
        c = lax.broadcasted_iota(jnp.int32, (tm, tm), 1)
        tri = jnp.where(c >= r, 1.0, 0.0).astype(jnp.bfloat16)
        dls = _tri_dot(tri, dc) + carry[0:1, :]
        carry[...] = jnp.broadcast_to(dls[0:1, :], carry.shape)
        dfg = dls * _sigmoid(-fg_ref[...])
        dfg_ref[...] = dfg
        dbf_ref[...] += jnp.sum(dfg, axis=0, keepdims=True)

    rev = lambda i: (nt - 1 - i, 0)
    return _pcall(
        body, name=name, grid=(nt,),
        in_specs=[pl.BlockSpec((tm, FOX_PAD), rev), pl.BlockSpec((tm, FOX_PAD), rev), pl.BlockSpec((tm, LANES), rev)],
        out_specs=[pl.BlockSpec((tm, LANES), rev), pl.BlockSpec((1, LANES), lambda i: (0, 0))],
        out_shape=[jax.ShapeDtypeStruct((T, LANES), f32), jax.ShapeDtypeStruct((1, LANES), f32)],
        scratch_shapes=[pltpu.VMEM((8, LANES), f32)],
        compiler_params=_params(1),
    )(dqa, dka, fgb)


GELU_C = math.sqrt(2.0 / math.pi)
GELU_A = 0.044715


def _gelu(x):
    t = jnp.tanh(GELU_C * (x + GELU_A * x * x * x))
    return 0.5 * x * (1.0 + t), t


def _gelu_grad(x, t):
    return 0.5 * (1.0 + t) + 0.5 * x * (1.0 - t * t) * GELU_C * (1.0 + 3.0 * GELU_A * x * x)


def _expm1(x):
    e = jnp.exp(x)
    safe = jnp.where(e == 1.0, x, (e - 1.0) * x / jnp.log(jnp.where(e == 1.0, 0.5, e)))
    return jnp.where(x < -0.5, e - 1.0, safe)


def _lru_gates(u, wab_ref, bab_ref, lam_ref):
    pre = _dot(u.astype(MXU_DTYPE), wab_ref[...]) + bab_ref[...]
    r = _sigmoid(pre[:, :LRU_W])
    gi = _sigmoid(pre[:, LRU_W:])
    lam = lam_ref[...]
    sp = jnp.maximum(-lam, 0.0) + jnp.log(1.0 + jnp.exp(-jnp.abs(lam)))
    log_a = -LRU_C * r * sp
    a = jnp.exp(log_a)
    s = jnp.sqrt(-_expm1(2.0 * log_a))
    return r, gi, sp, a, s


def _lru_fwd(lxg, conv_w, conv_b, wab, bab, lam, *, name, tc=512):
    T = lxg.shape[0]
    tc = min(tc, T)
    nc = T // tc

    def body(lx_ref, lg_ref, cw_ref, cb_ref, wab_ref, bab_ref, lam_ref,
             out_ref, u_ref, hs_ref, ext, a_sc, b_sc, h_sc):
        i = pl.program_id(0)

        @pl.when(i == 0)
        def _():
            ext[0:8, :] = jnp.zeros((8, LRU_W), f32)
            h_sc[...] = jnp.zeros_like(h_sc)

        ext[8:, :] = lx_ref[...]
        u = cb_ref[...] + cw_ref[0:1, :] * ext[pl.ds(5, tc), :]
        for k in range(1, CONV_K):
            u = u + cw_ref[k:k + 1, :] * ext[pl.ds(5 + k, tc), :]
        ext[0:8, :] = ext[tc:tc + 8, :]
        u_ref[...] = u
        r, gi, sp, a, s = _lru_gates(u, wab_ref, bab_ref, lam_ref)
        a_sc[...] = a
        b_sc[...] = s * (gi * u)

        def step(t, h):
            h = a_sc[pl.ds(t, 1), :] * h + b_sc[pl.ds(t, 1), :]
            hs_ref[pl.ds(t, 1), :] = h
            return h

        h = lax.fori_loop(0, tc, step, h_sc[0:1, :], unroll=8)
        h_sc[...] = jnp.broadcast_to(h, h_sc.shape)
        gel, _ = _gelu(lg_ref[...])
        out_ref[...] = gel * hs_ref[...]

    row = lambda i: (i, 0)
    const = lambda i: (0, 0)
    return _pcall(
        body, name=name, grid=(nc,),
        in_specs=[pl.BlockSpec((tc, LRU_W), row), pl.BlockSpec((tc, LRU_W), lambda i: (i, 1)),
                  pl.BlockSpec((CONV_K, LRU_W), const), pl.BlockSpec((1, LRU_W), const),
                  pl.BlockSpec((LRU_W, 2 * LRU_W), const), pl.BlockSpec((1, 2 * LRU_W), const),
                  pl.BlockSpec((1, LRU_W), const)],
        out_specs=[pl.BlockSpec((tc, LRU_W), row)] * 3,
        out_shape=[jax.ShapeDtypeStruct((T, LRU_W), f32)] * 3,
        scratch_shapes=[pltpu.VMEM((tc + 8, LRU_W), f32), pltpu.VMEM((tc, LRU_W), f32),
                        pltpu.VMEM((tc, LRU_W), f32), pltpu.VMEM((8, LRU_W), f32)],
        compiler_params=_params(1),
    )(lxg, lxg, conv_w, conv_b, wab, bab, lam)


def _lru_bwd(dlru, lxg, u, hs, conv_w, wab, bab, lam, *, name, tc=512):
    T = lxg.shape[0]
    tc = min(tc, T)
    nc = T // tc
    bp = tc // 8

    def body(dl_ref, lx_ref, lxp_ref, lg_ref, u_ref, hs_ref, hsp_ref, cw_ref, wab_ref, bab_ref, lam_ref,
             dlxg_ref, dwab_ref, dbab_ref, dcw_ref, dcb_ref, dlam_ref,
             dh_sc, a_sc, ext, du_ext, carry):
        i = pl.program_id(0)
        first_chunk = i == nc - 1

        @pl.when(i == 0)
        def _():
            dwab_ref[...] = jnp.zeros_like(dwab_ref)
            dbab_ref[...] = jnp.zeros_like(dbab_ref)
            dcw_ref[...] = jnp.zeros_like(dcw_ref)
            dcb_ref[...] = jnp.zeros_like(dcb_ref)
            dlam_ref[...] = jnp.zeros_like(dlam_ref)
            carry[...] = jnp.zeros_like(carry)
            du_ext[tc:tc + 8, :] = jnp.zeros((8, LRU_W), f32)

        lg = lg_ref[...]
        gel, th = _gelu(lg)
        dl = dl_ref[...]
        hs = hs_ref[...]
        dlg = dl * hs * _gelu_grad(lg, th)
        u = u_ref[...]
        r, gi, sp, a, s = _lru_gates(u, wab_ref, bab_ref, lam_ref)
        a_sc[...] = a
        dh_sc[...] = dl * gel

        def step(k, c):
            t = tc - 1 - k
            dh = dh_sc[pl.ds(t, 1), :] + c
            dh_sc[pl.ds(t, 1), :] = dh
            return a_sc[pl.ds(t, 1), :] * dh

        c = lax.fori_loop(0, tc, step, carry[0:1, :], unroll=8)
        carry[...] = jnp.broadcast_to(c, carry.shape)

        ext[0:8, :] = jnp.where(first_chunk, 0.0, hsp_ref[...])
        ext[8:, :] = hs
        hprev = ext[pl.ds(7, tc), :]
        dh = dh_sc[...]
        da = dh * hprev
        giu = gi * u
        dla = da * a - (dh * giu) * (a * a / s)
        dgi = dh * s * u
        du = dh * s * gi
        dr = dla * (-LRU_C * sp)
        dlam_ref[...] += jnp.sum(dla * (-LRU_C * r), axis=0, keepdims=True) * (-_sigmoid(-lam_ref[...]))
        dpre = jnp.concatenate([dr * r * (1.0 - r), dgi * gi * (1.0 - gi)], axis=1)
        dpre_b = dpre.astype(MXU_DTYPE)
        du = du + _dot_nt(dpre_b, wab_ref[...])
        dwab_ref[...] += _dot_tn(u.astype(MXU_DTYPE), dpre_b)
        dbab_ref[...] += jnp.sum(dpre, axis=0, keepdims=True)
        dcb_ref[...] += jnp.sum(du, axis=0, keepdims=True)

        du_ext[0:tc, :] = du
        dlx = cw_ref[0:1, :] * du_ext[pl.ds(3, tc), :]
        for k in range(1, CONV_K):
            dlx = dlx + cw_ref[k:k + 1, :] * du_ext[pl.ds(3 - k, tc), :]
        du_ext[tc:tc + 8, :] = du_ext[0:8, :]
        ext[0:8, :] = jnp.where(first_chunk, 0.0, lxp_ref[...])
        ext[8:, :] = lx_ref[...]
        for k in range(CONV_K):
            dcw_ref[k:k + 1, :] += jnp.sum(du * ext[pl.ds(5 + k, tc), :], axis=0, keepdims=True)
        dlxg_ref[:, :LRU_W] = dlx.astype(dlxg_ref.dtype)
        dlxg_ref[:, LRU_W:] = dlg.astype(dlxg_ref.dtype)

    rev = lambda i: (nc - 1 - i, 0)
    prev8 = lambda i: (jnp.maximum((nc - 1 - i) * bp - 1, 0), 0)
    const = lambda i: (0, 0)
    return _pcall(
        body, name=name, grid=(nc,),
        in_specs=[
            pl.BlockSpec((tc, LRU_W), rev),
            pl.BlockSpec((tc, LRU_W), rev),
            pl.BlockSpec((8, LRU_W), prev8),
            pl.BlockSpec((tc, LRU_W), lambda i: (nc - 1 - i, 1)),
            pl.BlockSpec((tc, LRU_W), rev),
            pl.BlockSpec((tc, LRU_W), rev),
            pl.BlockSpec((8, LRU_W), prev8),
            pl.BlockSpec((CONV_K, LRU_W), const),
            pl.BlockSpec((LRU_W, 2 * LRU_W), const),
            pl.BlockSpec((1, 2 * LRU_W), const),
            pl.BlockSpec((1, LRU_W), const),
        ],
        out_specs=[
            pl.BlockSpec((tc, 2 * LRU_W), rev),
            pl.BlockSpec((LRU_W, 2 * LRU_W), const),
            pl.BlockSpec((1, 2 * LRU_W), const),
            pl.BlockSpec((8, LRU_W), const),
            pl.BlockSpec((1, LRU_W), const),
            pl.BlockSpec((1, LRU_W), const),
        ],
        out_shape=[
            jax.ShapeDtypeStruct((T, 2 * LRU_W), MXU_DTYPE),
            jax.ShapeDtypeStruct((LRU_W, 2 * LRU_W), f32),
            jax.ShapeDtypeStruct((1, 2 * LRU_W), f32),
            jax.ShapeDtypeStruct((8, LRU_W), f32),
            jax.ShapeDtypeStruct((1, LRU_W), f32),
            jax.ShapeDtypeStruct((1, LRU_W), f32),
        ],
        scratch_shapes=[pltpu.VMEM((tc, LRU_W), f32), pltpu.VMEM((tc, LRU_W), f32),
                        pltpu.VMEM((tc + 8, LRU_W), f32), pltpu.VMEM((tc + 8, LRU_W), f32),
                        pltpu.VMEM((8, LRU_W), f32)],
        compiler_params=_params(1),
    )(dlru, lxg, lxg, lxg, u, hs, hs, conv_w, wab, bab, lam)


def _mix_out(fox, lru, wo, xhat1, g1, b1, g2, b2, *, name, tm=512):
    T = fox.shape[0]
    tm = min(tm, T)
    nt = T // tm

    def body(fox_ref, lru_ref, wo_ref, xh_ref, g1_ref, b1_ref, g2_ref, b2_ref, xhat_ref, xn_ref, rstd_ref):
        mix = _dot(fox_ref[...].astype(MXU_DTYPE), wo_ref[:FOX_W, :])
        mix = mix + _dot(lru_ref[...].astype(MXU_DTYPE), wo_ref[FOX_W:, :])
        x1 = xh_ref[...] * g1_ref[...] + b1_ref[...]
        xhat, rstd = _layer_norm_stats(DN_ALPHA * x1 + mix)
        xhat_ref[...] = xhat
        xn_ref[...] = xhat * g2_ref[...] + b2_ref[...]
        rstd_ref[...] = jnp.broadcast_to(rstd, rstd_ref.shape)

    row = lambda i: (i, 0)
    const = lambda i: (0, 0)
    vec = pl.BlockSpec((1, D_MODEL), const)
    return _pcall(
        body, name=name, grid=(nt,),
        in_specs=[pl.BlockSpec((tm, FOX_W), row), pl.BlockSpec((tm, LRU_W), row),
                  pl.BlockSpec((D_MODEL, D_MODEL), const), pl.BlockSpec((tm, D_MODEL), row), vec, vec, vec, vec],
        out_specs=[pl.BlockSpec((tm, D_MODEL), row), pl.BlockSpec((tm, D_MODEL), row),
                   pl.BlockSpec((tm, LANES), row)],
        out_shape=[jax.ShapeDtypeStruct((T, D_MODEL), f32), jax.ShapeDtypeStruct((T, D_MODEL), f32),
                   jax.ShapeDtypeStruct((T, LANES), f32)],
        compiler_params=_params(1),
    )(fox, lru, wo, xhat1, g1, b1, g2, b2)


def _mix_out_bwd(dyp, fox, lru, wo, *, name, tm=512):
    T = fox.shape[0]
    tm = min(tm, T)
    nt = T // tm

    def body(dyp_ref, fox_ref, lru_ref, wo_ref, dfox_ref, dlru_ref, dwo_ref):
        i = pl.program_id(0)

        @pl.when(i == 0)
        def _():
            dwo_ref[...] = jnp.zeros_like(dwo_ref)

        dmix = dyp_ref[...].astype(MXU_DTYPE)
        dcat = _dot_nt(dmix, wo_ref[...])
        dfox_ref[...] = dcat[:, :FOX_W].astype(dfox_ref.dtype)
        dlru_ref[...] = dcat[:, FOX_W:]
        dwo_ref[:FOX_W, :] += _dot_tn(fox_ref[...].astype(MXU_DTYPE), dmix)
        dwo_ref[FOX_W:, :] += _dot_tn(lru_ref[...].astype(MXU_DTYPE), dmix)

    row = lambda i: (i, 0)
    const = lambda i: (0, 0)
    return _pcall(
        body, name=name, grid=(nt,),
        in_specs=[pl.BlockSpec((tm, D_MODEL), row), pl.BlockSpec((tm, FOX_W), row), pl.BlockSpec((tm, LRU_W), row),
                  pl.BlockSpec((D_MODEL, D_MODEL), const)],
        out_specs=[pl.BlockSpec((tm, FOX_W), row), pl.BlockSpec((tm, LRU_W), row),
                   pl.BlockSpec((D_MODEL, D_MODEL), const)],
        out_shape=[jax.ShapeDtypeStruct((T, FOX_W), MXU_DTYPE), jax.ShapeDtypeStruct((T, LRU_W), f32),
                   jax.ShapeDtypeStruct((D_MODEL, D_MODEL), f32)],
        compiler_params=_params(1),
    )(dyp, fox, lru, wo)


def make_wp(w_in):
    scale = jnp.concatenate([jnp.full((FOX_W,), 1.0 / math.sqrt(HEAD_DIM), w_in.dtype),
                             jnp.ones((IN_COLS - FOX_W,), w_in.dtype)])
    return jnp.pad(w_in * scale[None, :], ((0, 0), (0, Z_PAD - IN_COLS)))


def _block_diag(w):
    eye = jnp.eye(HEADS, dtype=w.dtype)
    return jnp.einsum("hij,hg->higj", w, eye).reshape(LRU_W, LRU_W)


def _block_diag_extract(m):
    m4 = m.reshape(HEADS, HEAD_DIM, HEADS, HEAD_DIM)
    return jnp.stack([m4[h, :, h, :] for h in range(HEADS)])


class _NoOverlap:
    def start_token(self):
        return None

    def after_attention(self, after):
        return None

    def ffn2_weights(self, w, after):
        return w["f2g"], w["f2u"], w["f2d"]

    def ffn2_grads(self, grads):
        return None

    def mixer_grads(self, dwp, dwo, small):
        return None

    def before_ffn1_bwd(self, after):
        return None


def _tied(a, token):
    return a if token is None else a + token[0, 0]


def _local_step(x, target, w, hooks=None):
    hooks = hooks or _NoOverlap()
    wp = w["wp"]
    bfp = w["bfp"]
    wab = jnp.concatenate([_block_diag(w["rg_wa"]), _block_diag(w["rg_wx"])], axis=1).astype(MXU_DTYPE)
    bab = jnp.concatenate([w["rg_ba"].reshape(1, LRU_W), w["rg_bx"].reshape(1, LRU_W)], axis=1)

    xb0, g1a, u1a, xhat1, xn1, rstd1 = _ffn_fwd(x, w["f1g"], w["f1u"], w["f1d"], w["ln1_g"],
                                                _tied(w["ln1_b"], hooks.start_token()), name="ffn1_fwd")
    qkv, lxg, fgb = _proj_in(xn1, wp, bfp, name="proj_in")
    qa, ka, va = _fox_prep(qkv, fgb, name="fox_prep")
    fox, lse = _fox_fwd(qa, ka, va, name="fox_fwd")
    token = hooks.after_attention([lse])
    lru, uconv, hs = _lru_fwd(lxg, w["conv_w"], _tied(w["conv_b"], token), wab, bab, w["lam"], name="lru_fwd")
    xhat2, x2, rstd2 = _mix_out(fox, lru, w["wo"], xhat1, w["ln1_g"], w["ln1_b"], w["ln2_g"], w["ln2_b"], name="mix_out")
    f2g, f2u, f2d = hooks.ffn2_weights(w, [rstd2])
    xb2, g2a, u2a, xhat3, _, rstd3 = _ffn_fwd(x2, f2g, f2u, f2d, w["ln3_g"], w["ln3_b"], name="ffn2_fwd")

    dy3p, dln3g, dln3b, loss = _loss_ln_bwd(xhat3, rstd3, w["ln3_g"], w["ln3_b"], target, name="loss_ln3_bwd")
    dx2, df2g, df2u, df2d = _ffn_bwd(dy3p, xb2, g2a, u2a, f2g, f2u, f2d, name="ffn2_bwd")
    token = hooks.ffn2_grads([df2g, df2u, df2d])
    dy2p, dln2g, dln2b = _ln_bwd(dx2, xhat2, rstd2, _tied(w["ln2_g"], token), name="ln2_bwd")
    dfox, dlru, dwo = _mix_out_bwd(dy2p, fox, lru, w["wo"], name="mix_out_bwd")
    dlxg, dwab, dbab, dcw, dcb, dlam = _lru_bwd(dlru, lxg, uconv, hs, w["conv_w"], wab, bab, w["lam"], name="lru_bwd")
    drep, doa = _fox_bwd_prep(dfox, fox, name="fox_bwd_prep")
    dqa, dka, dva = _fox_bwd(qa, ka, va, doa, lse, drep, name="fox_bwd")
    dfg, dbf = _fox_bwd_post(dqa, dka, fgb, name="fox_bwd_post")
    dx1, dwp = _proj_in_bwd(dqa, dka, dva, dlxg, dfg, xn1, dy2p, wp, name="proj_in_bwd")
    dy1p, dln1g, dln1b = _ln_bwd(dx1, xhat1, rstd1, w["ln1_g"], name="ln1_bwd")
    small = dict(
        ln1_g=dln1g, ln1_b=dln1b, ln2_g=dln2g, ln2_b=dln2b, ln3_g=dln3g, ln3_b=dln3b,
        b_forget=dbf[:, :HEADS], conv_w=dcw[:CONV_K], conv_b=dcb,
        rg_wa=_block_diag_extract(dwab[:, :LRU_W]), rg_wx=_block_diag_extract(dwab[:, LRU_W:]),
        rg_ba=dbab[:, :LRU_W].reshape(HEADS, HEAD_DIM), rg_bx=dbab[:, LRU_W:].reshape(HEADS, HEAD_DIM),
        lru_lambda=dlam,
    )
    hooks.before_ffn1_bwd([dln1b])
    token = hooks.mixer_grads(dwp, dwo, small)
    dx, df1g, df1u, df1d = _ffn_bwd(dy1p, xb0, g1a, u1a, w["f1g"], w["f1u"], w["f1d"], token, name="ffn1_bwd")

    grads = dict(f1g=df1g, f1u=df1u, f1d=df1d, f2g=df2g, f2u=df2u, f2d=df2d, wp=dwp, wo=dwo, **small)
    return loss, dx, grads


MESH = pl.DeviceIdType.MESH
HBM_SPEC = pl.BlockSpec(memory_space=pl.ANY)
VMEM_SPEC = pl.BlockSpec(memory_space=pltpu.VMEM)


def _position():
    return lax.axis_index("x"), lax.axis_index("y"), lax.axis_index("c")


def _other_chips(x, y):
    return [(1 - x, y), (x, 1 - y), (1 - x, 1 - y)]


def _all_gather_bf16(shards, *, name):
    n = len(shards)

    def body(*refs):
        ins, outs, stages = refs[:n], refs[n:2 * n], refs[2 * n:3 * n]
        send_sems, recv_sems, local_sems = refs[3 * n:]
        x, y, c = _position()
        me, sibling = (x, y, c), (x, y, 1 - c)
        chips = _other_chips(x, y)

        def rows(k, px, py, pc):
            r = shards[k].shape[0]
            m = r // 2
            return outs[k].at[pl.ds(pl.multiple_of((2 * px + py) * r + pc * m, 16), m), :]

        def copy(k, idx, block, to, src=None):
            return pltpu.make_async_remote_copy(
                src_ref=rows(k, *block) if src is None else src, dst_ref=rows(k, *block),
                send_sem=send_sems.at[7 * k + idx], recv_sem=recv_sems.at[7 * k + idx],
                device_id=to, device_id_type=MESH)

        started = []
        mine = []
        for k in range(n):
            m = shards[k].shape[0] // 2
            stages[k][...] = ins[k][pl.ds(pl.multiple_of(c * m, 16), m), :].astype(stages[k].dtype)
            cp = pltpu.make_async_copy(stages[k], rows(k, *me), local_sems.at[k])
            cp.start()
            mine.append(cp)
            first = [copy(k, 0, me, sibling, src=stages[k])]
            first += [copy(k, 1 + j, me, (*chip, c), src=stages[k]) for j, chip in enumerate(chips)]
            for cp in first:
                cp.start()
            started += first
        for k in range(n):
            for j, chip in enumerate(chips):
                copy(k, 1 + j, (*chip, c), me).wait_recv()
                fwd = copy(k, 4 + j, (*chip, c), sibling)
                fwd.start()
                started.append(fwd)
        for k in range(n):
            copy(k, 0, sibling, me).wait_recv()
            for j, chip in enumerate(chips):
                copy(k, 4 + j, (*chip, 1 - c), me).wait_recv()
        for cp in started:
            cp.wait_send()
        for cp in mine:
            cp.wait()

    return _pcall(
        body, name=name,
        in_specs=[VMEM_SPEC] * n, out_specs=[HBM_SPEC] * n,
        out_shape=[jax.ShapeDtypeStruct((N_SHARD * s.shape[0], s.shape[1]), MXU_DTYPE) for s in shards],
        scratch_shapes=[pltpu.VMEM((s.shape[0] // 2, s.shape[1]), MXU_DTYPE) for s in shards]
        + [pltpu.SemaphoreType.DMA((7 * n,)), pltpu.SemaphoreType.DMA((7 * n,)), pltpu.SemaphoreType.DMA((n,))],
        compiler_params=pltpu.CompilerParams(vmem_limit_bytes=VMEM_LIMIT),
    )(*shards)


def _swap_halves(gs, *, name):
    n = len(gs)

    def body(*refs):
        ins, outs = refs[:n], refs[n:2 * n]
        send_sems, recv_sems = refs[2 * n:]
        x, y, c = _position()
        cps = []
        for k in range(n):
            m = gs[k].shape[1] // 2
            src = ins[k].at[:, pl.ds(pl.multiple_of((1 - c) * m, 16), m), :]
            cp = pltpu.make_async_remote_copy(src_ref=src, dst_ref=outs[k], send_sem=send_sems.at[k],
                                              recv_sem=recv_sems.at[k], device_id=(x, y, 1 - c), device_id_type=MESH)
            cp.start()
            cps.append(cp)
        for cp in cps:
            cp.wait()

    return _pcall(
        body, name=name, in_specs=[HBM_SPEC] * n, out_specs=[HBM_SPEC] * n,
        out_shape=[jax.ShapeDtypeStruct((g.shape[0], g.shape[1] // 2, g.shape[2]), g.dtype) for g in gs],
        scratch_shapes=[pltpu.SemaphoreType.DMA((n,)), pltpu.SemaphoreType.DMA((n,))],
    )(*gs)


def _add_halves(gs, recvs, *, name, tm=256):
    n = len(gs)
    _, r, cdim = gs[0].shape
    m = r // 2
    tm = min(tm, m)
    nb = m // tm
    c_idx = lax.axis_index("c").astype(jnp.int32).reshape(1)

    def body(c_ref, *refs):
        for k in range(n):
            refs[2 * n + k][...] = (refs[k][...].astype(f32) + refs[n + k][...].astype(f32)).astype(refs[2 * n + k].dtype)

    mine = pl.BlockSpec((None, tm, cdim), lambda j, i, c_ref: (j, c_ref[0] * nb + i, 0))
    half = pl.BlockSpec((None, tm, cdim), lambda j, i, c_ref: (j, i, 0))
    return _pcall(
        body, name=name,
        grid_spec=pltpu.PrefetchScalarGridSpec(
            num_scalar_prefetch=1, grid=(N_SHARD, nb),
            in_specs=[mine] * n + [half] * n, out_specs=[half] * n),
        out_shape=[jax.ShapeDtypeStruct((N_SHARD, m, cdim), g.dtype) for g in gs],
        compiler_params=_params(2),
    )(c_idx, *gs, *recvs)


def _scatter_partials(ps, *, name):
    n = len(ps)

    def body(*refs):
        ins, outs = refs[:n], refs[n:2 * n]
        send_sems, recv_sems = refs[2 * n:]
        x, y, c = _position()
        me_chip = 2 * x + y
        cps = []
        for k in range(n):
            for j, (px, py) in enumerate(_other_chips(x, y)):
                cp = pltpu.make_async_remote_copy(
                    src_ref=ins[k].at[2 * px + py], dst_ref=outs[k].at[me_chip],
                    send_sem=send_sems.at[3 * k + j], recv_sem=recv_sems.at[3 * k + j],
                    device_id=(px, py, c), device_id_type=MESH)
                cp.start()
                cps.append(cp)
        for cp in cps:
            cp.wait()

    return _pcall(
        body, name=name, in_specs=[HBM_SPEC] * n, out_specs=[HBM_SPEC] * n,
        out_shape=[jax.ShapeDtypeStruct(p.shape, p.dtype) for p in ps],
        scratch_shapes=[pltpu.SemaphoreType.DMA((3 * n,)), pltpu.SemaphoreType.DMA((3 * n,))],
    )(*ps)


def _sum_slabs(ps, qs, *, name, tm=128):
    n = len(qs)
    _, m, cdim = qs[0].shape
    tm = min(tm, m)
    nb = m // tm
    assert m % tm == 0, (m, tm)
    where = jnp.stack([2 * lax.axis_index("x") + lax.axis_index("y"), lax.axis_index("c")]).astype(jnp.int32)

    def body(w_ref, *refs):
        for k in range(n):
            own, q1, q2, q3 = (refs[4 * k + t][...].astype(f32) for t in range(4))
            refs[4 * n + k][...] = ((own + q1) + q2) + q3

    def slab(flip):
        return pl.BlockSpec((None, tm, cdim), lambda i, w_ref: (jnp.bitwise_xor(w_ref[0], flip), i, 0))

    operands = []
    for p, q in zip(ps, qs):
        operands += [p, q, q, q]
    return _pcall(
        body, name=name,
        grid_spec=pltpu.PrefetchScalarGridSpec(
            num_scalar_prefetch=1, grid=(nb,),
            in_specs=[slab(0), slab(2), slab(1), slab(3)] * n,
            out_specs=[pl.BlockSpec((tm, cdim), lambda i, w_ref: (w_ref[1] * nb + i, 0))] * n),
        out_shape=[jax.ShapeDtypeStruct((2 * m, cdim), f32) for _ in qs],
        compiler_params=_params(1),
    )(where, *operands)


def _join_halves(fs, *, name):
    n = len(fs)

    def body(*refs):
        outs = refs[n:2 * n]
        send_sems, recv_sems = refs[2 * n:]
        x, y, c = _position()
        cps = []
        for k in range(n):
            m = fs[k].shape[0] // 2
            half = outs[k].at[pl.ds(pl.multiple_of(c * m, 8), m), :]
            cp = pltpu.make_async_remote_copy(src_ref=half, dst_ref=half, send_sem=send_sems.at[k],
                                              recv_sem=recv_sems.at[k], device_id=(x, y, 1 - c), device_id_type=MESH)
            cp.start()
            cps.append(cp)
        for cp in cps:
            cp.wait()

    return _pcall(
        body, name=name, in_specs=[HBM_SPEC] * n, out_specs=[HBM_SPEC] * n,
        out_shape=[jax.ShapeDtypeStruct(f.shape, f.dtype) for f in fs],
        input_output_aliases={k: k for k in range(n)},
        scratch_shapes=[pltpu.SemaphoreType.DMA((n,)), pltpu.SemaphoreType.DMA((n,))],
    )(*fs)


def _all_reduce_small(v, after=None, *, name):
    r = v.shape[0]
    extra = [] if after is None else [after]

    def body(v_ref, *refs):
        out_ref, buf, send_sems, recv_sems, local_sem = refs[len(extra):]
        x, y, c = _position()
        me, sibling = (x, y, c), (x, y, 1 - c)
        chips = _other_chips(x, y)

        def rows(px, py, pc):
            return buf.at[pl.ds(pl.multiple_of((4 * px + 2 * py + pc) * r, 8), r), :]

        def copy(k, block, to, src=None):
            return pltpu.make_async_remote_copy(
                src_ref=rows(*block) if src is None else src, dst_ref=rows(*block),
                send_sem=send_sems.at[k], recv_sem=recv_sems.at[k], device_id=to, device_id_type=MESH)

        mine = pltpu.make_async_copy(v_ref, rows(*me), local_sem)
        mine.start()
        first = [copy(0, me, sibling, src=v_ref)]
        first += [copy(1 + j, me, (*chip, c), src=v_ref) for j, chip in enumerate(chips)]
        for cp in first:
            cp.start()
        passed = [copy(4 + j, (*chip, c), sibling) for j, chip in enumerate(chips)]
        for j, chip in enumerate(chips):
            copy(1 + j, (*chip, c), me).wait_recv()
            passed[j].start()
        copy(0, sibling, me).wait_recv()
        for j, chip in enumerate(chips):
            copy(4 + j, (*chip, 1 - c), me).wait_recv()
        for cp in first + passed:
            cp.wait_send()
        mine.wait()
        acc = buf[0:r, :]
        for d in range(1, N_DEV):
            acc = acc + buf[d * r:(d + 1) * r, :]
        out_ref[...] = acc

    return _pcall(
        body, name=name, in_specs=[VMEM_SPEC] + [HBM_SPEC] * len(extra), out_specs=VMEM_SPEC,
        out_shape=jax.ShapeDtypeStruct((r, LANES), f32),
        scratch_shapes=[pltpu.VMEM((N_DEV * r, LANES), f32), pltpu.SemaphoreType.DMA((7,)),
                        pltpu.SemaphoreType.DMA((7,)), pltpu.SemaphoreType.DMA],
    )(v, *extra)


SEM_SPEC = pl.BlockSpec(memory_space=pltpu.SEMAPHORE)
HBM_ONLY = pl.BlockSpec(memory_space=pltpu.HBM)
EFFECT = pltpu.SideEffectType.DATAFLOW_SIDE_EFFECTING


def _split_start(bufs, copies_fn, n_sems, *, name):
    n = len(bufs)

    def body(*refs):
        send_sems, recv_sems = refs[n], refs[n + 1]
        thru = refs[n + 2:2 * n + 2]
        token = refs[2 * n + 2]
        for cp in copies_fn(thru, send_sems, recv_sems):
            cp.start()
        token[...] = jnp.zeros_like(token)

    outs = _pcall(
        body, name=name,
        out_shape=(pltpu.SemaphoreType.DMA((n_sems,)), pltpu.SemaphoreType.DMA((n_sems,)),
                   *[pltpu.HBM(b.shape, b.dtype) for b in bufs], jax.ShapeDtypeStruct((8, LANES), f32)),
        in_specs=[HBM_ONLY] * n,
        out_specs=(SEM_SPEC, SEM_SPEC, *[HBM_ONLY] * n, VMEM_SPEC),
        input_output_aliases={k: 2 + k for k in range(n)},
        compiler_params=pltpu.CompilerParams(has_side_effects=EFFECT),
    )(*[pltpu.with_memory_space_constraint(b, pltpu.HBM) for b in bufs])
    return outs[0], outs[1], list(outs[2:2 + n]), outs[2 + n]


def _split_wait(thru, send_sems, recv_sems, after, copies_fn, *, name):
    n = len(thru)

    def body(*refs):
        for cp in copies_fn(refs[:n], refs[n], refs[n + 1]):
            cp.wait_send()
            cp.wait_recv()

    return list(_pcall(
        body, name=name,
        out_shape=tuple(pltpu.HBM(b.shape, b.dtype) for b in thru),
        in_specs=[HBM_ONLY] * n + [SEM_SPEC, SEM_SPEC] + [HBM_SPEC] * len(after),
        out_specs=tuple([HBM_ONLY] * n),
        input_output_aliases={k: k for k in range(n)},
        compiler_params=pltpu.CompilerParams(has_side_effects=EFFECT),
    )(*thru, send_sems, recv_sems, *after))


def _scatter_copies(n):
    def copies(bufs, send_sems, recv_sems):
        x, y, c = _position()
        me_chip = 2 * x + y
        cps = []
        for k in range(n):
            for j, (px, py) in enumerate(_other_chips(x, y)):
                cps.append(pltpu.make_async_remote_copy(
                    src_ref=bufs[k].at[2 * px + py], dst_ref=bufs[n + k].at[me_chip],
                    send_sem=send_sems.at[3 * k + j], recv_sem=recv_sems.at[3 * k + j],
                    device_id=(px, py, c), device_id_type=MESH))
        return cps
    return copies


def _block_rows(buf, px, py, pc):
    m = buf.shape[0] // N_DEV
    return buf.at[pl.ds(pl.multiple_of((4 * px + 2 * py + pc) * m, 16), m), :]


def _gather_ici_copies(n):
    def copies(bufs, send_sems, recv_sems):
        x, y, c = _position()
        cps = []
        for k in range(n):
            rows = _block_rows(bufs[k], x, y, c)
            targets = [(x, y, 1 - c)] + [(px, py, c) for px, py in _other_chips(x, y)]
            for j, to in enumerate(targets):
                cps.append(pltpu.make_async_remote_copy(
                    src_ref=rows, dst_ref=rows, send_sem=send_sems.at[4 * k + j], recv_sem=recv_sems.at[4 * k + j],
                    device_id=to, device_id_type=MESH))
        return cps
    return copies


def _gather_d2d_copies(n):
    def copies(bufs, send_sems, recv_sems):
        x, y, c = _position()
        cps = []
        for k in range(n):
            for j, (px, py) in enumerate(_other_chips(x, y)):
                rows = _block_rows(bufs[k], px, py, c)
                cps.append(pltpu.make_async_remote_copy(
                    src_ref=rows, dst_ref=rows, send_sem=send_sems.at[3 * k + j], recv_sem=recv_sems.at[3 * k + j],
                    device_id=(x, y, 1 - c), device_id_type=MESH))
        return cps
    return copies


def _cast_halves(shards, after, *, name, tm=256):
    n = len(shards)
    r, cdim = shards[0].shape
    m = r // 2
    tm = min(tm, m)
    nb = m // tm
    assert m % tm == 0, (m, tm)
    where = jnp.stack([2 * lax.axis_index("x") + lax.axis_index("y"), lax.axis_index("c")]).astype(jnp.int32)

    def body(w_ref, *refs):
        for k in range(n):
            refs[n + 1 + k][...] = refs[k][...].astype(refs[n + 1 + k].dtype)

    return _pcall(
        body, name=name,
        grid_spec=pltpu.PrefetchScalarGridSpec(
            num_scalar_prefetch=1, grid=(nb,),
            in_specs=[pl.BlockSpec((tm, cdim), lambda i, w_ref: (w_ref[1] * nb + i, 0))] * n + [HBM_SPEC],
            out_specs=[pl.BlockSpec((tm, cdim), lambda i, w_ref: ((2 * w_ref[0] + w_ref[1]) * nb + i, 0))] * n),
        out_shape=[jax.ShapeDtypeStruct((N_SHARD * r, cdim), MXU_DTYPE) for _ in shards],
        compiler_params=_params(1),
    )(where, *shards, after)


class _Overlap(_NoOverlap):
    def __init__(self, ffn2_shards, after):
        halves = _cast_halves(ffn2_shards, after, name="ag2_cast")
        self.n = len(halves)
        self.ici = _split_start(halves, _gather_ici_copies(self.n), 4 * self.n, name="ag2_ici_start")
        self.reduced = None

    def start_token(self):
        return self.ici[3]

    def after_attention(self, after):
        send_sems, recv_sems, thru, _ = self.ici
        landed = _split_wait(thru, send_sems, recv_sems, after, _gather_ici_copies(self.n), name="ag2_ici_wait")
        self.d2d = _split_start(landed, _gather_d2d_copies(self.n), 3 * self.n, name="ag2_d2d_start")
        return self.d2d[3]

    def ffn2_weights(self, w, after):
        send_sems, recv_sems, thru, _ = self.d2d
        full = _split_wait(thru, send_sems, recv_sems, after, _gather_d2d_copies(self.n), name="ag2_d2d_wait")
        fs = D_FF // N_SHARD
        return (full[0].reshape(N_SHARD, D_MODEL, fs), full[1].reshape(N_SHARD, D_MODEL, fs),
                full[2].reshape(N_SHARD, fs, D_MODEL))

    def ffn2_grads(self, grads):
        recvs = _swap_halves(grads, name="rs_swap_ffn2")
        ps = _add_halves(grads, recvs, name="rs_add_ffn2")
        lands = [lax.empty(p.shape, p.dtype) for p in ps]
        self.scatter = _split_start(list(ps) + lands, _scatter_copies(len(ps)), 3 * len(ps), name="rs_scatter_ffn2_start")
        return self.scatter[3]

    def mixer_grads(self, dwp, dwo, small):
        self.small_sum = _all_reduce_small(_pack_small(small), name="ar_small")
        gwin = dwp[:, :IN_COLS].reshape(D_MODEL, N_SHARD, IN_SHARD).transpose(1, 0, 2).astype(GRAD_DTYPE)
        gwo = dwo.reshape(N_SHARD, D_MODEL // N_SHARD, D_MODEL).astype(GRAD_DTYPE)
        recvs = _swap_halves([gwin, gwo], name="rs_swap_mix")
        ps = [_add_halves([g], [r], name=f"rs_add_{tag}")[0] for g, r, tag in zip([gwin, gwo], recvs, ["w_in", "w_out"])]
        lands = [lax.empty(p.shape, p.dtype) for p in ps]
        self.scatter_mix = _split_start(ps + lands, _scatter_copies(2), 6, name="rs_scatter_mix_start")
        return self.scatter_mix[3]

    def mixer_reduced(self, after):
        send_sems, recv_sems, thru, _ = self.scatter_mix
        done = _split_wait(thru, send_sems, recv_sems, after, _scatter_copies(2), name="rs_scatter_mix_wait")
        return [_sum_slabs([done[k]], [done[2 + k]], name=f"rs_sum_{tag}")[0] for k, tag in enumerate(["w_in", "w_out"])]

    def before_ffn1_bwd(self, after):
        send_sems, recv_sems, thru, _ = self.scatter
        n = len(thru) // 2
        done = _split_wait(thru, send_sems, recv_sems, after, _scatter_copies(n), name="rs_scatter_ffn2_wait")
        self.reduced = list(_sum_slabs(done[:n], done[n:], name="rs_sum_ffn2"))


def _adamw(gs, ws, ms, vs, *, name, tm=256):
    n = len(gs)
    r, cdim = gs[0].shape
    tm = r if tm is None else min(tm, r)
    assert r % tm == 0, (r, tm)
    c1 = 1.0 / (1.0 - ADAM_B1 ** ADAM_STEP)
    c2 = 1.0 / (1.0 - ADAM_B2 ** ADAM_STEP)

    def body(*refs):
        for k in range(n):
            g = refs[k][...]
            w = refs[n + k][...]
            m = ADAM_B1 * refs[2 * n + k][...] + (1.0 - ADAM_B1) * g
            v = ADAM_B2 * refs[3 * n + k][...] + (1.0 - ADAM_B2) * (g * g)
            refs[4 * n + k][...] = -ADAM_LR * ((m * c1) / (jnp.sqrt(v * c2) + ADAM_EPS) + ADAM_WD * w)
            refs[5 * n + k][...] = m
            refs[6 * n + k][...] = v

    spec = pl.BlockSpec((tm, cdim), lambda i: (i, 0))
    outs = _pcall(
        body, name=name, grid=(r // tm,), in_specs=[spec] * (4 * n), out_specs=[spec] * (3 * n),
        out_shape=[jax.ShapeDtypeStruct((r, cdim), f32)] * (3 * n),
        compiler_params=_params(1),
    )(*gs, *ws, *ms, *vs)
    return outs[:n], outs[n:2 * n], outs[2 * n:]


BIG = ["ffn1_w_gate", "ffn1_w_up", "ffn1_w_down", "ffn2_w_gate", "ffn2_w_up", "ffn2_w_down"]
SMALL = ["ln1_g", "ln1_b", "b_forget", "conv_w", "conv_b", "rg_wa", "rg_ba", "rg_wx", "rg_bx", "lru_lambda",
         "ln2_g", "ln2_b", "ln3_g", "ln3_b"]
WEIGHTS = ["ffn1_w_gate", "ffn1_w_up", "ffn1_w_down", "ln1_g", "ln1_b", "w_in", "b_forget", "conv_w", "conv_b",
           "rg_wa", "rg_ba", "rg_wx", "rg_bx", "lru_lambda", "w_out", "ln2_g", "ln2_b",
           "ffn2_w_gate", "ffn2_w_up", "ffn2_w_down", "ln3_g", "ln3_b"]


def _pack_small(parts):
    rows = []
    for n in SMALL:
        flat = parts[n].reshape(-1)
        pad = (-flat.shape[0]) % LANES
        rows.append(jnp.pad(flat, (0, pad)).reshape(-1, LANES))
    packed = jnp.concatenate(rows, axis=0)
    return jnp.pad(packed, ((0, (-packed.shape[0]) % 8), (0, 0)))


def _unpack_small(packed, shapes):
    out, r0 = {}, 0
    for n in SMALL:
        size = math.prod(shapes[n])
        nr = -(-size // LANES)
        out[n] = packed[r0:r0 + nr].reshape(-1)[:size].reshape(shapes[n])
        r0 += nr
    return out


def kernel(x, ffn1_w_gate, ffn1_w_up, ffn1_w_down, ln1_g, ln1_b, w_in, b_forget, conv_w, conv_b, rg_wa, rg_ba, rg_wx, rg_bx, lru_lambda, w_out, ln2_g, ln2_b, ffn2_w_gate, ffn2_w_up, ffn2_w_down, ln3_g, ln3_b, loss_target, m_ffn1_w_gate, m_ffn1_w_up, m_ffn1_w_down, m_ln1_g, m_ln1_b, m_w_in, m_b_forget, m_conv_w, m_conv_b, m_rg_wa, m_rg_ba, m_rg_wx, m_rg_bx, m_lru_lambda, m_w_out, m_ln2_g, m_ln2_b, m_ffn2_w_gate, m_ffn2_w_up, m_ffn2_w_down, m_ln3_g, m_ln3_b, v_ffn1_w_gate, v_ffn1_w_up, v_ffn1_w_down, v_ln1_g, v_ln1_b, v_w_in, v_b_forget, v_conv_w, v_conv_b, v_rg_wa, v_rg_ba, v_rg_wx, v_rg_bx, v_lru_lambda, v_w_out, v_ln2_g, v_ln2_b, v_ffn2_w_gate, v_ffn2_w_up, v_ffn2_w_down, v_ln3_g, v_ln3_b):
    args = dict(locals())
    w = {n: args[n] for n in WEIGHTS}
    mom = {n: args["m_" + n] for n in WEIGHTS}
    var = {n: args["v_" + n] for n in WEIGHTS}
    chip = 2 * lax.axis_index("x") + lax.axis_index("y")

    g1 = _all_gather_bf16([w[n][0] for n in BIG[:3]] + [w["w_in"][0], w["w_out"][0]], name="ag_first")
    fs = D_FF // N_SHARD
    w_in_full = g1[3].reshape(N_SHARD, D_MODEL, IN_SHARD).transpose(1, 0, 2).reshape(D_MODEL, IN_COLS)
    full = dict(
        f1g=g1[0].reshape(N_SHARD, D_MODEL, fs), f1u=g1[1].reshape(N_SHARD, D_MODEL, fs),
        f1d=g1[2].reshape(N_SHARD, fs, D_MODEL),
        wp=make_wp(w_in_full), bfp=jnp.pad(b_forget, ((0, 0), (0, LANES - HEADS))), wo=g1[4],
        ln1_g=ln1_g, ln1_b=ln1_b, ln2_g=ln2_g, ln2_b=ln2_b, ln3_g=ln3_g, ln3_b=ln3_b,
        conv_b=conv_b, rg_wa=rg_wa[0], rg_wx=rg_wx[0], rg_ba=rg_ba[0], rg_bx=rg_bx[0], lam=lru_lambda,
    )
    cw_place = lax.dynamic_update_slice(jnp.zeros((8, LRU_W), f32), conv_w[0] * 0.5, (0, chip * (LRU_W // N_SHARD)))
    cw_full = _all_reduce_small(cw_place.reshape(-1, LANES), g1[0], name="ag_conv_w")
    full["conv_w"] = cw_full.reshape(8, LRU_W)[:CONV_K]

    hooks = _Overlap([w[n][0] for n in BIG[3:]], cw_full)
    loss_rep, dx, g = _local_step(x[0], loss_target[0], full, hooks)
    loss = lax.psum(loss_rep[0, 0], ("x", "y", "c"))

    gs1 = [g["f1g"], g["f1u"], g["f1d"]]
    ps1 = _add_halves(gs1, _swap_halves(gs1, name="rs_swap_ffn1"), name="rs_add_ffn1")
    lands = [lax.empty(p.shape, p.dtype) for p in ps1]
    send1, recv1, thru1, token1 = _split_start(list(ps1) + lands, _scatter_copies(3), 9, name="rs_scatter_ffn1_start")
    red = _join_halves(hooks.reduced + hooks.mixer_reduced([token1]), name="rs_join_rest")
    grads = dict(zip(BIG[3:] + ["w_in", "w_out"], red))

    small_shapes = {n: w[n].shape for n in SMALL}
    small_shapes["conv_w"] = (1, CONV_K, LRU_W)
    gs_red = _unpack_small(hooks.small_sum, small_shapes)
    gs_red["conv_w"] = lax.dynamic_slice(gs_red["conv_w"], (0, 0, chip * (LRU_W // N_SHARD)),
                                         (1, CONV_K, LRU_W // N_SHARD))
    grads.update(gs_red)

    delta, new_m, new_v = {}, {}, {}

    def adamw(names, name, **kw):
        d, nm, nv = _adamw([grads[n] for n in names], [w[n][0] for n in names], [mom[n][0] for n in names],
                           [var[n][0] for n in names], name=name, **kw)
        for i, n in enumerate(names):
            delta[n], new_m[n], new_v[n] = d[i], nm[i], nv[i]

    adamw(BIG[3:], "adamw_ffn2", tm=128)
    adamw(["w_in"], "adamw_w_in")
    adamw(["w_out"], "adamw_w_out")
    shard_shapes = {n: w[n].shape for n in SMALL}
    d, nm, nv = _adamw([_pack_small({n: grads[n] for n in SMALL})], [_pack_small({n: w[n] for n in SMALL})],
                       [_pack_small({n: mom[n] for n in SMALL})], [_pack_small({n: var[n] for n in SMALL})],
                       name="adamw_small", tm=None)
    for dst, packed in ((delta, d[0]), (new_m, nm[0]), (new_v, nv[0])):
        dst.update(_unpack_small(packed, shard_shapes))

    worked = [new_v["ffn2_w_down"], new_v["w_in"], new_v["w_out"], nv[0]]
    done1 = _split_wait(thru1, send1, recv1, worked, _scatter_copies(3), name="rs_scatter_ffn1_wait")
    red1 = _join_halves(list(_sum_slabs(done1[:3], done1[3:], name="rs_sum_ffn1")), name="rs_join_ffn1")
    grads.update(zip(BIG[:3], red1))
    adamw(BIG[:3], "adamw_ffn1", tm=128)

    def shaped(tree, n):
        return tree[n].reshape(w[n].shape)

    return (loss, dx[None], *[shaped(grads, n) for n in WEIGHTS], *[shaped(delta, n) for n in WEIGHTS],
            *[shaped(new_m, n) for n in WEIGHTS], *[shaped(new_v, n) for n in WEIGHTS])
```

```python
import functools
import math

import jax
import jax.numpy as jnp
from jax import lax
from jax.experimental import pallas as pl
from jax.experimental.pallas import tpu as pltpu

f32 = jnp.float32
MXU_DTYPE = jnp.bfloat16
GRAD_DTYPE = jnp.bfloat16

D_MODEL = 1024
D_FF = 4096
N_SHARD = 4
N_DEV = 8
FOX_W = 512
LRU_W = 512
HEADS = 8
HEAD_DIM = 64
CONV_K = 4
IN_COLS = 2568
IN_SHARD = IN_COLS // N_SHARD
QKV_W = 3 * FOX_W
Z_PAD = 2688
LANES = 128
LN_EPS = 1e-5
DN_ALPHA = 2.0 ** 0.25
LRU_C = 8.0
NEG_BIG = -1e30
VMEM_LIMIT = 56 * 1024 * 1024

ADAM_LR = 0.001
ADAM_B1 = 0.9
ADAM_B2 = 0.999
ADAM_EPS = 1e-08
ADAM_WD = 0.01
ADAM_STEP = 10


def _pcall(body, **kw):
    return pl.pallas_call(body, **kw)


def _params(n_grid, vmem=VMEM_LIMIT):
    return pltpu.CompilerParams(dimension_semantics=("arbitrary",) * n_grid, vmem_limit_bytes=vmem)


def _dot(a, b):
    return jnp.dot(a, b, preferred_element_type=f32)


def _dot_nt(a, b):
    return lax.dot_general(a, b, (((1,), (1,)), ((), ())), preferred_element_type=f32)


def _dot_tn(a, b):
    return lax.dot_general(a, b, (((0,), (0,)), ((), ())), preferred_element_type=f32)


def _sigmoid(x):
    return 1.0 / (1.0 + jnp.exp(-x))


def _layer_norm_stats(y):
    mu = jnp.mean(y, axis=-1, keepdims=True)
    yc = y - mu
    var = jnp.mean(yc * yc, axis=-1, keepdims=True)
    rstd = lax.rsqrt(var + LN_EPS)
    return yc * rstd, rstd


def _ln_backward(dy, xhat, rstd, gamma):
    dxhat = dy * gamma
    m1 = jnp.mean(dxhat, axis=-1, keepdims=True)
    m2 = jnp.mean(dxhat * xhat, axis=-1, keepdims=True)
    dyp = rstd * (dxhat - m1 - xhat * m2)
    return dyp, jnp.sum(dy * xhat, axis=0, keepdims=True), jnp.sum(dy, axis=0, keepdims=True)


def _ffn_fwd(x, wg, wu, wd, ln_g, ln_b, *, name, tm=1024, tf=512):
    T = x.shape[0]
    tm = min(tm, T)
    fs = D_FF // N_SHARD
    cpf = fs // tf
    nf = D_FF // tf
    nt = T // tm

    def body(x_ref, wg_ref, wu_ref, wd_ref, g_ref, b_ref,
             xb_ref, gact_ref, uact_ref, xhat_ref, xn_ref, rstd_ref, acc_ref):
        f = pl.program_id(1)

        @pl.when(f == 0)
        def _():
            xb_ref[...] = x_ref[...].astype(MXU_DTYPE)
            acc_ref[...] = jnp.zeros_like(acc_ref)

        xb = xb_ref[...]
        g = _dot(xb, wg_ref[...])
        u = _dot(xb, wu_ref[...])
        h = (g * _sigmoid(g)) * u
        gact_ref[...] = g.astype(gact_ref.dtype)
        uact_ref[...] = u.astype(uact_ref.dtype)
        acc_ref[...] += _dot(h.astype(MXU_DTYPE), wd_ref[...])

        @pl.when(f == nf - 1)
        def _():
            y = DN_ALPHA * x_ref[...] + 0.5 * acc_ref[...]
            xhat, rstd = _layer_norm_stats(y)
            xhat_ref[...] = xhat
            xn_ref[...] = (xhat * g_ref[...] + b_ref[...]).astype(xn_ref.dtype)
            rstd_ref[...] = jnp.broadcast_to(rstd, rstd_ref.shape)

    row = lambda i, f: (i, 0)
    return _pcall(
        body, name=name, grid=(nt, nf),
        in_specs=[
            pl.BlockSpec((tm, D_MODEL), row),
            pl.BlockSpec((None, D_MODEL, tf), lambda i, f: (f // cpf, 0, f % cpf)),
            pl.BlockSpec((None, D_MODEL, tf), lambda i, f: (f // cpf, 0, f % cpf)),
            pl.BlockSpec((None, tf, D_MODEL), lambda i, f: (f // cpf, f % cpf, 0)),
            pl.BlockSpec((1, D_MODEL), lambda i, f: (0, 0)),
            pl.BlockSpec((1, D_MODEL), lambda i, f: (0, 0)),
        ],
        out_specs=[
            pl.BlockSpec((tm, D_MODEL), row),
            pl.BlockSpec((tm, tf), lambda i, f: (i, f)),
            pl.BlockSpec((tm, tf), lambda i, f: (i, f)),
            pl.BlockSpec((tm, D_MODEL), row),
            pl.BlockSpec((tm, D_MODEL), row),
            pl.BlockSpec((tm, LANES), row),
        ],
        out_shape=[
            jax.ShapeDtypeStruct((T, D_MODEL), MXU_DTYPE),
            jax.ShapeDtypeStruct((T, D_FF), MXU_DTYPE),
            jax.ShapeDtypeStruct((T, D_FF), MXU_DTYPE),
            jax.ShapeDtypeStruct((T, D_MODEL), f32),
            jax.ShapeDtypeStruct((T, D_MODEL), MXU_DTYPE),
            jax.ShapeDtypeStruct((T, LANES), f32),
        ],
        scratch_shapes=[pltpu.VMEM((tm, D_MODEL), f32)],
        compiler_params=_params(2),
    )(x, wg, wu, wd, ln_g, ln_b)


def _ffn_bwd(dyp, xb, gact, uact, wg, wu, wd, after=None, *, name, tm=512, tf=512):
    T = dyp.shape[0]
    tm = min(tm, T)
    fs = D_FF // N_SHARD
    cpf = fs // tf
    nf = D_FF // tf
    nt = T // tm
    extra = [] if after is None else [after]

    def body(dyp_ref, xb_ref, g_ref, u_ref, wg_ref, wu_ref, wd_ref, *refs):
        dx_hbm, dwg_ref, dwu_ref, dwd_ref, dx_sc, dwg_sc, dwu_sc, dwd_sc, sem = refs[len(extra):]
        f = pl.program_id(0)
        i = pl.program_id(1)
        rows = pl.ds(pl.multiple_of(i * tm, tm), tm)
        dyp_t = dyp_ref[...]
        dy = (0.5 * dyp_t).astype(MXU_DTYPE)

        @pl.when(i == 0)
        def _():
            dwg_sc[...] = jnp.zeros_like(dwg_sc)
            dwu_sc[...] = jnp.zeros_like(dwu_sc)
            dwd_sc[...] = jnp.zeros_like(dwd_sc)

        @pl.when(f == 0)
        def _():
            dx_sc[rows, :] = DN_ALPHA * dyp_t

        g = g_ref[...].astype(f32)
        u = u_ref[...].astype(f32)
        sig = _sigmoid(g)
        silu = g * sig
        dh = _dot_nt(dy, wd_ref[...])
        dg = (dh * u * (sig * (1.0 + g * (1.0 - sig)))).astype(MXU_DTYPE)
        du = (dh * silu).astype(MXU_DTYPE)
        hb = (silu * u).astype(MXU_DTYPE)
        dx_sc[rows, :] += _dot_nt(dg, wg_ref[...]) + _dot_nt(du, wu_ref[...])
        xb_t = xb_ref[...]
        dwg_sc[...] += _dot_tn(xb_t, dg)
        dwu_sc[...] += _dot_tn(xb_t, du)
        dwd_sc[...] += _dot_tn(hb, dy)

        @pl.when(i == nt - 1)
        def _():
            dwg_ref[...] = dwg_sc[...].astype(dwg_ref.dtype)
            dwu_ref[...] = dwu_sc[...].astype(dwu_ref.dtype)
            dwd_ref[...] = dwd_sc[...].astype(dwd_ref.dtype)

        @pl.when(jnp.logical_and(f == nf - 1, i == nt - 1))
        def _():
            cp = pltpu.make_async_copy(dx_sc, dx_hbm, sem)
            cp.start()
            cp.wait()

    row = lambda f, i: (i, 0)
    return _pcall(
        body, name=name, grid=(nf, nt),
        in_specs=[
            pl.BlockSpec((tm, D_MODEL), row),
            pl.BlockSpec((tm, D_MODEL), row),
            pl.BlockSpec((tm, tf), lambda f, i: (i, f)),
            pl.BlockSpec((tm, tf), lambda f, i: (i, f)),
            pl.BlockSpec((None, D_MODEL, tf), lambda f, i: (f // cpf, 0, f % cpf)),
            pl.BlockSpec((None, D_MODEL, tf), lambda f, i: (f // cpf, 0, f % cpf)),
            pl.BlockSpec((None, tf, D_MODEL), lambda f, i: (f // cpf, f % cpf, 0)),
        ] + [pl.BlockSpec(memory_space=pl.ANY)] * len(extra),
        out_specs=[
            pl.BlockSpec(memory_space=pl.ANY),
            pl.BlockSpec((None, D_MODEL, tf), lambda f, i: (f // cpf, 0, f % cpf)),
            pl.BlockSpec((None, D_MODEL, tf), lambda f, i: (f // cpf, 0, f % cpf)),
            pl.BlockSpec((None, tf, D_MODEL), lambda f, i: (f // cpf, f % cpf, 0)),
        ],
        out_shape=[
            jax.ShapeDtypeStruct((T, D_MODEL), f32),
            jax.ShapeDtypeStruct((N_SHARD, D_MODEL, fs), GRAD_DTYPE),
            jax.ShapeDtypeStruct((N_SHARD, D_MODEL, fs), GRAD_DTYPE),
            jax.ShapeDtypeStruct((N_SHARD, fs, D_MODEL), GRAD_DTYPE),
        ],
        scratch_shapes=[pltpu.VMEM((T, D_MODEL), f32), pltpu.VMEM((D_MODEL, tf), f32),
                        pltpu.VMEM((D_MODEL, tf), f32), pltpu.VMEM((tf, D_MODEL), f32),
                        pltpu.SemaphoreType.DMA],
        compiler_params=_params(2),
    )(dyp, xb, gact, uact, wg, wu, wd, *extra)


def _loss_ln_bwd(xhat, rstd, ln_g, ln_b, target, *, name, tm=512):
    T = xhat.shape[0]
    tm = min(tm, T)
    nt = T // tm

    def body(xhat_ref, rstd_ref, g_ref, b_ref, t_ref, dyp_ref, dg_ref, db_ref, loss_ref):
        i = pl.program_id(0)

        @pl.when(i == 0)
        def _():
            dg_ref[...] = jnp.zeros_like(dg_ref)
            db_ref[...] = jnp.zeros_like(db_ref)
            loss_ref[...] = jnp.zeros_like(loss_ref)

        xhat_t = xhat_ref[...]
        gamma = g_ref[...]
        err = xhat_t * gamma + b_ref[...] - t_ref[...]
        sq = jnp.sum(jnp.sum(err * err, axis=0, keepdims=True), axis=1, keepdims=True)
        loss_ref[...] += jnp.broadcast_to(sq * (0.5 / D_MODEL), loss_ref.shape)
        dy = err * (1.0 / D_MODEL)
        dyp, dgam, dbeta = _ln_backward(dy, xhat_t, rstd_ref[:, 0:1], gamma)
        dyp_ref[...] = dyp
        dg_ref[...] += dgam
        db_ref[...] += dbeta

    row = lambda i: (i, 0)
    const = lambda i: (0, 0)
    return _pcall(
        body, name=name, grid=(nt,),
        in_specs=[pl.BlockSpec((tm, D_MODEL), row), pl.BlockSpec((tm, LANES), row),
                  pl.BlockSpec((1, D_MODEL), const), pl.BlockSpec((1, D_MODEL), const),
                  pl.BlockSpec((tm, D_MODEL), row)],
        out_specs=[pl.BlockSpec((tm, D_MODEL), row), pl.BlockSpec((1, D_MODEL), const),
                   pl.BlockSpec((1, D_MODEL), const), pl.BlockSpec((1, LANES), const)],
        out_shape=[jax.ShapeDtypeStruct((T, D_MODEL), f32), jax.ShapeDtypeStruct((1, D_MODEL), f32),
                   jax.ShapeDtypeStruct((1, D_MODEL), f32), jax.ShapeDtypeStruct((1, LANES), f32)],
        compiler_params=_params(1),
    )(xhat, rstd, ln_g, ln_b, target)


def _ln_bwd(dy, xhat, rstd, ln_g, *, name, tm=512):
    T = xhat.shape[0]
    tm = min(tm, T)
    nt = T // tm

    def body(dy_ref, xhat_ref, rstd_ref, g_ref, dyp_ref, dg_ref, db_ref):
        i = pl.program_id(0)

        @pl.when(i == 0)
        def _():
            dg_ref[...] = jnp.zeros_like(dg_ref)
            db_ref[...] = jnp.zeros_like(db_ref)

        dyp, dgam, dbeta = _ln_backward(dy_ref[...], xhat_ref[...], rstd_ref[:, 0:1], g_ref[...])
        dyp_ref[...] = dyp
        dg_ref[...] += dgam
        db_ref[...] += dbeta

    row = lambda i: (i, 0)
    const = lambda i: (0, 0)
    return _pcall(
        body, name=name, grid=(nt,),
        in_specs=[pl.BlockSpec((tm, D_MODEL), row), pl.BlockSpec((tm, D_MODEL), row),
                  pl.BlockSpec((tm, LANES), row), pl.BlockSpec((1, D_MODEL), const)],
        out_specs=[pl.BlockSpec((tm, D_MODEL), row), pl.BlockSpec((1, D_MODEL), const),
                   pl.BlockSpec((1, D_MODEL), const)],
        out_shape=[jax.ShapeDtypeStruct((T, D_MODEL), f32), jax.ShapeDtypeStruct((1, D_MODEL), f32),
                   jax.ShapeDtypeStruct((1, D_MODEL), f32)],
        compiler_params=_params(1),
    )(dy, xhat, rstd, ln_g)


def _proj_in(xn, wp, bfp, *, name, tm=512):
    T = xn.shape[0]
    tm = min(tm, T)
    nt = T // tm

    def body(x_ref, w_ref, b_ref, qkv_ref, lxg_ref, fg_ref):
        z = _dot(x_ref[...], w_ref[...])
        qkv_ref[...] = z[:, :QKV_W].astype(qkv_ref.dtype)
        lxg_ref[...] = z[:, QKV_W:QKV_W + 2 * LRU_W]
        fg_ref[...] = z[:, QKV_W + 2 * LRU_W:] + b_ref[...]

    row = lambda i: (i, 0)
    const = lambda i: (0, 0)
    return _pcall(
        body, name=name, grid=(nt,),
        in_specs=[pl.BlockSpec((tm, D_MODEL), row), pl.BlockSpec((D_MODEL, Z_PAD), const),
                  pl.BlockSpec((1, LANES), const)],
        out_specs=[pl.BlockSpec((tm, QKV_W), row), pl.BlockSpec((tm, 2 * LRU_W), row),
                   pl.BlockSpec((tm, LANES), row)],
        out_shape=[jax.ShapeDtypeStruct((T, QKV_W), MXU_DTYPE), jax.ShapeDtypeStruct((T, 2 * LRU_W), f32),
                   jax.ShapeDtypeStruct((T, LANES), f32)],
        compiler_params=_params(1),
    )(xn, wp, bfp)


def _proj_in_bwd(dqa, dka, dva, dlxg, dfg, xn, dyp, wp, *, name, tm=512):
    T = xn.shape[0]
    tm = min(tm, T)
    nt = T // tm

    def body(dq_ref, dk_ref, dv_ref, dl_ref, dfg_ref, x_ref, dyp_ref, w_ref, dx_ref, dw_hbm, dw_sc, sem):
        i = pl.program_id(0)

        @pl.when(i == 0)
        def _():
            dw_sc[...] = jnp.zeros_like(dw_sc)

        low = _low_lanes((tm, LANES))

        def packed(ref):
            pairs = [jnp.where(low, ref[:, (2 * j) * LANES:(2 * j + 1) * LANES],
                               _swap_lane_halves(ref[:, (2 * j + 1) * LANES:(2 * j + 2) * LANES]))
                     for j in range(HEADS // 2)]
            return jnp.concatenate(pairs, axis=1).astype(MXU_DTYPE)

        dz = jnp.concatenate(
            [packed(dq_ref), packed(dk_ref), packed(dv_ref),
             dl_ref[...].astype(MXU_DTYPE), dfg_ref[...].astype(MXU_DTYPE)], axis=1)
        dx_ref[...] = DN_ALPHA * dyp_ref[...] + _dot_nt(dz, w_ref[...])
        dw_sc[...] += _dot_tn(x_ref[...], dz)

        @pl.when(i == nt - 1)
        def _():
            dw_sc[:, :FOX_W] = dw_sc[:, :FOX_W] * (1.0 / math.sqrt(HEAD_DIM))
            cp = pltpu.make_async_copy(dw_sc, dw_hbm, sem)
            cp.start()
            cp.wait()

    row = lambda i: (i, 0)
    const = lambda i: (0, 0)
    return _pcall(
        body, name=name, grid=(nt,),
        in_specs=[pl.BlockSpec((tm, HEADS * LANES), row), pl.BlockSpec((tm, HEADS * LANES), row),
                  pl.BlockSpec((tm, HEADS * LANES), row),
                  pl.BlockSpec((tm, 2 * LRU_W), row), pl.BlockSpec((tm, LANES), row),
                  pl.BlockSpec((tm, D_MODEL), row), pl.BlockSpec((tm, D_MODEL), row),
                  pl.BlockSpec((D_MODEL, Z_PAD), const)],
        out_specs=[pl.BlockSpec((tm, D_MODEL), row), pl.BlockSpec(memory_space=pl.ANY)],
        out_shape=[jax.ShapeDtypeStruct((T, D_MODEL), f32), jax.ShapeDtypeStruct((D_MODEL, Z_PAD), f32)],
        scratch_shapes=[pltpu.VMEM((D_MODEL, Z_PAD), f32), pltpu.SemaphoreType.DMA],
        compiler_params=_params(1),
    )(dqa, dka, dva, dlxg, dfg, xn, dyp, wp)


def _split3(x):
    hi = x.astype(jnp.bfloat16)
    r1 = x - hi.astype(f32)
    mid = r1.astype(jnp.bfloat16)
    lo = (r1 - mid.astype(f32)).astype(jnp.bfloat16)
    return hi, mid, lo


def _tri_dot(tri, x):
    hi, mid, lo = _split3(x)
    return _dot(tri, hi) + _dot(tri, mid) + _dot(tri, lo)


FOX_PAD = HEADS * LANES
AUX = HEAD_DIM


def _low_lanes(shape):
    return lax.broadcasted_iota(jnp.int32, shape, 1) < HEAD_DIM


def _swap_lane_halves(x):
    return pltpu.roll(x, HEAD_DIM, 1)


def _fox_prep(qkv, fgb, *, name, tm=512):
    T = fgb.shape[0]
    tm = min(tm, T)
    nt = T // tm

    def body(qkv_ref, fg_ref, qa_ref, ka_ref, va_ref, carry):
        i = pl.program_id(0)

        @pl.when(i == 0)
        def _():
            carry[...] = jnp.zeros_like(carry)

        x = fg_ref[...]
        ls = jnp.minimum(x, 0.0) - jnp.log(1.0 + jnp.exp(-jnp.abs(x)))
        r = lax.broadcasted_iota(jnp.int32, (tm, tm), 0)
        c = lax.broadcasted_iota(jnp.int32, (tm, tm), 1)
        tri = jnp.where(r >= c, 1.0, 0.0).astype(jnp.bfloat16)
        cum = _tri_dot(tri, ls) + carry[0:1, :]
        carry[...] = jnp.broadcast_to(cum[tm - 1:tm, :], carry.shape)

        lane = lax.broadcasted_iota(jnp.int32, (tm, LANES), 1)
        low = lane < HEAD_DIM
        ones_q = jnp.where(jnp.logical_and(lane >= AUX + 3, lane < AUX + 6), 1.0, 0.0)
        ones_k = jnp.where(jnp.logical_and(lane >= AUX, lane < AUX + 3), 1.0, 0.0)
        for j in range(HEADS // 2):
            pair = [qkv_ref[:, t * FOX_W + j * LANES:t * FOX_W + (j + 1) * LANES].astype(f32) for t in range(3)]
            for odd in range(2):
                h = 2 * j + odd
                q, k, v = [_swap_lane_halves(a) if odd else a for a in pair]
                hi, mid, lo = [a.astype(f32) for a in _split3(jnp.broadcast_to(cum[:, h:h + 1], (tm, LANES)))]
                aux_q = jnp.where(lane == AUX, hi, jnp.where(lane == AUX + 1, mid, jnp.where(lane == AUX + 2, lo, ones_q)))
                aux_k = jnp.where(lane == AUX + 3, -hi,
                                  jnp.where(lane == AUX + 4, -mid, jnp.where(lane == AUX + 5, -lo, ones_k)))
                blk = slice(h * LANES, (h + 1) * LANES)
                qa_ref[:, blk] = jnp.where(low, q, aux_q).astype(qa_ref.dtype)
                ka_ref[:, blk] = jnp.where(low, k, aux_k).astype(ka_ref.dtype)
                va_ref[:, blk] = jnp.where(low, v, 1.0).astype(va_ref.dtype)

    row = lambda i: (i, 0)
    return _pcall(
        body, name=name, grid=(nt,),
        in_specs=[pl.BlockSpec((tm, QKV_W), row), pl.BlockSpec((tm, LANES), row)],
        out_specs=[pl.BlockSpec((tm, FOX_PAD), row)] * 3,
        out_shape=[jax.ShapeDtypeStruct((T, FOX_PAD), MXU_DTYPE)] * 3,
        scratch_shapes=[pltpu.VMEM((8, LANES), f32)],
        compiler_params=_params(1),
    )(qkv, fgb)


def _future_keys(tq, tk):
    r = lax.broadcasted_iota(jnp.int32, (tq, tk), 0)
    c = lax.broadcasted_iota(jnp.int32, (tq, tk), 1)
    return c > r


def _causal_steps(nq, key_major):
    if key_major:
        pairs = [(qi, ki) for ki in range(nq) for qi in range(ki, nq)]
    else:
        pairs = [(qi, ki) for qi in range(nq) for ki in range(qi + 1)]
    return (jnp.asarray([p[0] for p in pairs], jnp.int32), jnp.asarray([p[1] for p in pairs], jnp.int32))


def _fox_fwd(qa, ka, va, *, name, tq=512, hps=4):
    T = qa.shape[0]
    tq = min(tq, T)
    tk = tq
    nq = T // tq
    rep = tk // LANES
    qi_tab, ki_tab = _causal_steps(nq, key_major=False)

    def body(qi_ref, ki_ref, qa_ref, ka_ref, va_ref, o_ref, lse_ref, m_sc, acc_sc):
        t = pl.program_id(1)
        qi = qi_ref[t]
        ki = ki_ref[t]

        @pl.when(ki == 0)
        def _():
            m_sc[...] = jnp.full_like(m_sc, NEG_BIG)
            acc_sc[...] = jnp.zeros_like(acc_sc)

        def tile(diagonal):
            for h in range(hps):
                blk = slice(h * LANES, (h + 1) * LANES)
                s = _dot_nt(qa_ref[:, blk], ka_ref[:, blk])
                if diagonal:
                    s = jnp.where(_future_keys(tq, tk), NEG_BIG, s)
                m_prev = m_sc[h]
                m_new = jnp.maximum(m_prev, jnp.max(s, axis=1, keepdims=True))
                p = jnp.exp(s - jnp.tile(m_new, (1, rep)))
                acc_sc[h] = jnp.exp(m_prev - m_new) * acc_sc[h] + _dot(p.astype(MXU_DTYPE), va_ref[:, blk])
                m_sc[h] = m_new

        @pl.when(ki < qi)
        def _():
            tile(False)

        @pl.when(ki == qi)
        def _():
            tile(True)
            low = _low_lanes((tq, LANES))
            outs = []
            for h in range(hps):
                acc = acc_sc[h]
                den = _swap_lane_halves(acc)
                outs.append(acc / den)
                lse_ref[h] = m_sc[h] + jnp.log(jnp.where(low, den, acc))
            for p in range(hps // 2):
                o_ref[:, p * LANES:(p + 1) * LANES] = jnp.where(low, outs[2 * p], _swap_lane_halves(outs[2 * p + 1]))

    pair = hps * LANES
    return _pcall(
        body, name=name,
        grid_spec=pltpu.PrefetchScalarGridSpec(
            num_scalar_prefetch=2, grid=(HEADS // hps, qi_tab.shape[0]),
            in_specs=[
                pl.BlockSpec((tq, pair), lambda j, t, qi_ref, ki_ref: (qi_ref[t], j)),
                pl.BlockSpec((tk, pair), lambda j, t, qi_ref, ki_ref: (ki_ref[t], j)),
                pl.BlockSpec((tk, pair), lambda j, t, qi_ref, ki_ref: (ki_ref[t], j)),
            ],
            out_specs=[pl.BlockSpec((tq, pair // 2), lambda j, t, qi_ref, ki_ref: (qi_ref[t], j)),
                       pl.BlockSpec((hps, tq, LANES), lambda j, t, qi_ref, ki_ref: (j, qi_ref[t], 0))],
            scratch_shapes=[pltpu.VMEM((hps, tq, LANES), f32)] * 2),
        out_shape=[jax.ShapeDtypeStruct((T, FOX_W), f32), jax.ShapeDtypeStruct((HEADS, T, LANES), f32)],
        compiler_params=_params(2),
    )(qi_tab, ki_tab, qa, ka, va)


def _fox_bwd_prep(do, o, *, name, tm=512):
    T = o.shape[0]
    tm = min(tm, T)
    nt = T // tm

    def body(do_ref, o_ref, d_ref, doa_ref):
        low = _low_lanes((tm, LANES))
        for j in range(HEADS // 2):
            do2 = do_ref[:, j * LANES:(j + 1) * LANES].astype(f32)
            prod = do2 * o_ref[:, j * LANES:(j + 1) * LANES]
            for odd in range(2):
                h = 2 * j + odd
                mine = jnp.where(low, _swap_lane_halves(prod) if odd else prod, 0.0)
                d_ref[h] = jnp.broadcast_to(jnp.sum(mine, axis=1, keepdims=True), (tm, LANES))
                doh = jnp.where(low, _swap_lane_halves(do2) if odd else do2, 0.0)
                doa_ref[:, h * LANES:(h + 1) * LANES] = doh.astype(doa_ref.dtype)

    return _pcall(
        body, name=name, grid=(nt,),
        in_specs=[pl.BlockSpec((tm, FOX_W), lambda i: (i, 0)), pl.BlockSpec((tm, FOX_W), lambda i: (i, 0))],
        out_specs=[pl.BlockSpec((HEADS, tm, LANES), lambda i: (0, i, 0)), pl.BlockSpec((tm, FOX_PAD), lambda i: (i, 0))],
        out_shape=[jax.ShapeDtypeStruct((HEADS, T, LANES), f32), jax.ShapeDtypeStruct((T, FOX_PAD), MXU_DTYPE)],
        compiler_params=_params(1),
    )(do, o)


def _fox_bwd(qa, ka, va, doa, lse, drep, *, name, tq=512, hps=4):
    T = qa.shape[0]
    tq = min(tq, T)
    tk = tq
    nq = T // tq
    rep = tk // LANES
    qi_tab, ki_tab = _causal_steps(nq, key_major=True)

    def body(qi_ref, ki_ref, qa_ref, ka_ref, va_ref, doa_ref, lse_ref, d_ref, dqa_ref, dka_ref, dva_ref, dk_sc, dv_sc):
        t = pl.program_id(1)
        qi = qi_ref[t]
        ki = ki_ref[t]
        rows = pl.ds(pl.multiple_of(qi * tq, tq), tq)

        @pl.when(t == 0)
        def _():
            dqa_ref[...] = jnp.zeros_like(dqa_ref)

        @pl.when(qi == ki)
        def _():
            dk_sc[...] = jnp.zeros_like(dk_sc)
            dv_sc[...] = jnp.zeros_like(dv_sc)

        def tile(diagonal):
            for h in range(hps):
                blk = slice(h * LANES, (h + 1) * LANES)
                qh, kh, doh = qa_ref[:, blk], ka_ref[:, blk], doa_ref[:, blk]
                p = jnp.exp(_dot_nt(qh, kh) - jnp.tile(lse_ref[h], (1, rep)))
                if diagonal:
                    p = jnp.where(_future_keys(tq, tk), 0.0, p)
                dp = _dot_nt(doh, va_ref[:, blk])
                ds = (p * (dp - jnp.tile(d_ref[h], (1, rep)))).astype(MXU_DTYPE)
                dv_sc[h] += _dot_tn(p.astype(MXU_DTYPE), doh)
                dk_sc[h] += _dot_tn(ds, qh)
                dqa_ref[rows, blk] += _dot(ds, kh)

        @pl.when(qi > ki)
        def _():
            tile(False)

        @pl.when(qi == ki)
        def _():
            tile(True)

        @pl.when(qi == nq - 1)
        def _():
            for h in range(hps):
                blk = slice(h * LANES, (h + 1) * LANES)
                dka_ref[:, blk] = dk_sc[h]
                dva_ref[:, blk] = dv_sc[h]

    pair = hps * LANES
    q_blk = lambda j, t, qi_ref, ki_ref: (qi_ref[t], j)
    k_blk = lambda j, t, qi_ref, ki_ref: (ki_ref[t], j)
    stat = pl.BlockSpec((hps, tq, LANES), lambda j, t, qi_ref, ki_ref: (j, qi_ref[t], 0))
    return _pcall(
        body, name=name,
        grid_spec=pltpu.PrefetchScalarGridSpec(
            num_scalar_prefetch=2, grid=(HEADS // hps, qi_tab.shape[0]),
            in_specs=[pl.BlockSpec((tq, pair), q_blk), pl.BlockSpec((tk, pair), k_blk), pl.BlockSpec((tk, pair), k_blk),
                      pl.BlockSpec((tq, pair), q_blk), stat, stat],
            out_specs=[pl.BlockSpec((T, pair), lambda j, t, qi_ref, ki_ref: (0, j)),
                       pl.BlockSpec((tk, pair), k_blk), pl.BlockSpec((tk, pair), k_blk)],
            scratch_shapes=[pltpu.VMEM((hps, tk, LANES), f32)] * 2),
        out_shape=[jax.ShapeDtypeStruct((T, FOX_PAD), f32)] * 3,
        compiler_params=_params(2),
    )(qi_tab, ki_tab, qa, ka, va, doa, lse, drep)


def _fox_bwd_post(dqa, dka, fgb, *, name, tm=512):
    T = fgb.shape[0]
    tm = min(tm, T)
    nt = T // tm

    def body(dqa_ref, dka_ref, fg_ref, dfg_ref, dbf_ref, carry):
        i = pl.program_id(0)

        @pl.when(i == 0)
        def _():
            carry[...] = jnp.zeros_like(carry)
            dbf_ref[...] = jnp.zeros_like(dbf_ref)

        lane = lax.broadcasted_iota(jnp.int32, (tm, LANES), 1)
        dc = jnp.zeros((tm, LANES), f32)
        for h in range(HEADS):
            row_sum = dqa_ref[:, h * LANES + AUX:h * LANES + AUX + 1]
            col_sum = dka_ref[:, h * LANES + AUX + 3:h * LANES + AUX + 4]
            dc = jnp.where(lane == h, jnp.broadcast_to(row_sum - col_sum, (tm, LANES)), dc)
        r = lax.broadcasted_iota(jnp.int32, (tm, tm), 0)
        c = lax.broadcasted_iota(jnp.int32, (tm, tm), 1)
        tri = jnp.where(c >= r, 1.0, 0.0).astype(jnp.bfloat16)
        dls = _tri_dot(tri, dc) + carry[0:1, :]
        carry[...] = jnp.broadcast_to(dls[0:1, :], carry.shape)
        dfg = dls * _sigmoid(-fg_ref[...])
        dfg_ref[...] = dfg
        dbf_ref[...] += jnp.sum(dfg, axis=0, keepdims=True)

    rev = lambda i: (nt - 1 - i, 0)
    return _pcall(
        body, name=name, grid=(nt,),
        in_specs=[pl.BlockSpec((tm, FOX_PAD), rev), pl.BlockSpec((tm, FOX_PAD), rev), pl.BlockSpec((tm, LANES), rev)],
        out_specs=[pl.BlockSpec((tm, LANES), rev), pl.BlockSpec((1, LANES), lambda i: (0, 0))],
        out_shape=[jax.ShapeDtypeStruct((T, LANES), f32), jax.ShapeDtypeStruct((1, LANES), f32)],
        scratch_shapes=[pltpu.VMEM((8, LANES), f32)],
        compiler_params=_params(1),
    )(dqa, dka, fgb)


GELU_C = math.sqrt(2.0 / math.pi)
GELU_A = 0.044715


def _gelu(x):
    t = jnp.tanh(GELU_C * (x + GELU_A * x * x * x))
    return 0.5 * x * (1.0 + t), t


def _gelu_grad(x, t):
    return 0.5 * (1.0 + t) + 0.5 * x * (1.0 - t * t) * GELU_C * (1.0 + 3.0 * GELU_A * x * x)


def _expm1(x):
    e = jnp.exp(x)
    safe = jnp.where(e == 1.0, x, (e - 1.0) * x / jnp.log(jnp.where(e == 1.0, 0.5, e)))
    return jnp.where(x < -0.5, e - 1.0, safe)


def _lru_gates(u, wab_ref, bab_ref, lam_ref):
    pre = _dot(u.astype(MXU_DTYPE), wab_ref[...]) + bab_ref[...]
    r = _sigmoid(pre[:, :LRU_W])
    gi = _sigmoid(pre[:, LRU_W:])
    lam = lam_ref[...]
    sp = jnp.maximum(-lam, 0.0) + jnp.log(1.0 + jnp.exp(-jnp.abs(lam)))
    log_a = -LRU_C * r * sp
    a = jnp.exp(log_a)
    s = jnp.sqrt(-_expm1(2.0 * log_a))
    return r, gi, sp, a, s


def _lru_fwd(lxg, conv_w, conv_b, wab, bab, lam, *, name, tc=512):
    T = lxg.shape[0]
    tc = min(tc, T)
    nc = T // tc

    def body(lx_ref, lg_ref, cw_ref, cb_ref, wab_ref, bab_ref, lam_ref,
             out_ref, u_ref, hs_ref, ext, a_sc, b_sc, h_sc):
        i = pl.program_id(0)

        @pl.when(i == 0)
        def _():
            ext[0:8, :] = jnp.zeros((8, LRU_W), f32)
            h_sc[...] = jnp.zeros_like(h_sc)

        ext[8:, :] = lx_ref[...]
        u = cb_ref[...] + cw_ref[0:1, :] * ext[pl.ds(5, tc), :]
        for k in range(1, CONV_K):
            u = u + cw_ref[k:k + 1, :] * ext[pl.ds(5 + k, tc), :]
        ext[0:8, :] = ext[tc:tc + 8, :]
        u_ref[...] = u
        r, gi, sp, a, s = _lru_gates(u, wab_ref, bab_ref, lam_ref)
        a_sc[...] = a
        b_sc[...] = s * (gi * u)

        def step(t, h):
            h = a_sc[pl.ds(t, 1), :] * h + b_sc[pl.ds(t, 1), :]
            hs_ref[pl.ds(t, 1), :] = h
            return h

        h = lax.fori_loop(0, tc, step, h_sc[0:1, :], unroll=8)
        h_sc[...] = jnp.broadcast_to(h, h_sc.shape)
        gel, _ = _gelu(lg_ref[...])
        out_ref[...] = gel * hs_ref[...]

    row = lambda i: (i, 0)
    const = lambda i: (0, 0)
    return _pcall(
        body, name=name, grid=(nc,),
        in_specs=[pl.BlockSpec((tc, LRU_W), row), pl.BlockSpec((tc, LRU_W), lambda i: (i, 1)),
                  pl.BlockSpec((CONV_K, LRU_W), const), pl.BlockSpec((1, LRU_W), const),
                  pl.BlockSpec((LRU_W, 2 * LRU_W), const), pl.BlockSpec((1, 2 * LRU_W), const),
                  pl.BlockSpec((1, LRU_W), const)],
        out_specs=[pl.BlockSpec((tc, LRU_W), row)] * 3,
        out_shape=[jax.ShapeDtypeStruct((T, LRU_W), f32)] * 3,
        scratch_shapes=[pltpu.VMEM((tc + 8, LRU_W), f32), pltpu.VMEM((tc, LRU_W), f32),
                        pltpu.VMEM((tc, LRU_W), f32), pltpu.VMEM((8, LRU_W), f32)],
        compiler_params=_params(1),
    )(lxg, lxg, conv_w, conv_b, wab, bab, lam)


def _lru_bwd(dlru, lxg, u, hs, conv_w, wab, bab, lam, *, name, tc=512):
    T = lxg.shape[0]
    tc = min(tc, T)
    nc = T // tc
    bp = tc // 8

    def body(dl_ref, lx_ref, lxp_ref, lg_ref, u_ref, hs_ref, hsp_ref, cw_ref, wab_ref, bab_ref, lam_ref,
             dlxg_ref, dwab_ref, dbab_ref, dcw_ref, dcb_ref, dlam_ref,
             dh_sc, a_sc, ext, du_ext, carry):
        i = pl.program_id(0)
        first_chunk = i == nc - 1

        @pl.when(i == 0)
        def _():
            dwab_ref[...] = jnp.zeros_like(dwab_ref)
            dbab_ref[...] = jnp.zeros_like(dbab_ref)
            dcw_ref[...] = jnp.zeros_like(dcw_ref)
            dcb_ref[...] = jnp.zeros_like(dcb_ref)
            dlam_ref[...] = jnp.zeros_like(dlam_ref)
            carry[...] = jnp.zeros_like(carry)
            du_ext[tc:tc + 8, :] = jnp.zeros((8, LRU_W), f32)

        lg = lg_ref[...]
        gel, th = _gelu(lg)
        dl = dl_ref[...]
        hs = hs_ref[...]
        dlg = dl * hs * _gelu_grad(lg, th)
        u = u_ref[...]
        r, gi, sp, a, s = _lru_gates(u, wab_ref, bab_ref, lam_ref)
        a_sc[...] = a
        dh_sc[...] = dl * gel

        def step(k, c):
            t = tc - 1 - k
            dh = dh_sc[pl.ds(t, 1), :] + c
            dh_sc[pl.ds(t, 1), :] = dh
            return a_sc[pl.ds(t, 1), :] * dh

        c = lax.fori_loop(0, tc, step, carry[0:1, :], unroll=8)
        carry[...] = jnp.broadcast_to(c, carry.shape)

        ext[0:8, :] = jnp.where(first_chunk, 0.0, hsp_ref[...])
        ext[8:, :] = hs
        hprev = ext[pl.ds(7, tc), :]
        dh = dh_sc[...]
        da = dh * hprev
        giu = gi * u
        dla = da * a - (dh * giu) * (a * a / s)
        dgi = dh * s * u
        du = dh * s * gi
        dr = dla * (-LRU_C * sp)
        dlam_ref[...] += jnp.sum(dla * (-LRU_C * r), axis=0, keepdims=True) * (-_sigmoid(-lam_ref[...]))
        dpre = jnp.concatenate([dr * r * (1.0 - r), dgi * gi * (1.0 - gi)], axis=1)
        dpre_b = dpre.astype(MXU_DTYPE)
        du = du + _dot_nt(dpre_b, wab_ref[...])
        dwab_ref[...] += _dot_tn(u.astype(MXU_DTYPE), dpre_b)
        dbab_ref[...] += jnp.sum(dpre, axis=0, keepdims=True)
        dcb_ref[...] += jnp.sum(du, axis=0, keepdims=True)

        du_ext[0:tc, :] = du
        dlx = cw_ref[0:1, :] * du_ext[pl.ds(3, tc), :]
        for k in range(1, CONV_K):
            dlx = dlx + cw_ref[k:k + 1, :] * du_ext[pl.ds(3 - k, tc), :]
        du_ext[tc:tc + 8, :] = du_ext[0:8, :]
        ext[0:8, :] = jnp.where(first_chunk, 0.0, lxp_ref[...])
        ext[8:, :] = lx_ref[...]
        for k in range(CONV_K):
            dcw_ref[k:k + 1, :] += jnp.sum(du * ext[pl.ds(5 + k, tc), :], axis=0, keepdims=True)
        dlxg_ref[:, :LRU_W] = dlx.astype(dlxg_ref.dtype)
        dlxg_ref[:, LRU_W:] = dlg.astype(dlxg_ref.dtype)

    rev = lambda i: (nc - 1 - i, 0)
    prev8 = lambda i: (jnp.maximum((nc - 1 - i) * bp - 1, 0), 0)
    const = lambda i: (0, 0)
    return _pcall(
        body, name=name, grid=(nc,),
        in_specs=[
            pl.BlockSpec((tc, LRU_W), rev),
            pl.BlockSpec((tc, LRU_W), rev),
            pl.BlockSpec((8, LRU_W), prev8),
            pl.BlockSpec((tc, LRU_W), lambda i: (nc - 1 - i, 1)),
            pl.BlockSpec((tc, LRU_W), rev),
            pl.BlockSpec((tc, LRU_W), rev),
            pl.BlockSpec((8, LRU_W), prev8),
            pl.BlockSpec((CONV_K, LRU_W), const),
            pl.BlockSpec((LRU_W, 2 * LRU_W), const),
            pl.BlockSpec((1, 2 * LRU_W), const),
            pl.BlockSpec((1, LRU_W), const),
        ],
        out_specs=[
            pl.BlockSpec((tc, 2 * LRU_W), rev),
            pl.BlockSpec((LRU_W, 2 * LRU_W), const),
            pl.BlockSpec((1, 2 * LRU_W), const),
            pl.BlockSpec((8, LRU_W), const),
            pl.BlockSpec((1, LRU_W), const),
            pl.BlockSpec((1, LRU_W), const),
        ],
        out_shape=[
            jax.ShapeDtypeStruct((T, 2 * LRU_W), MXU_DTYPE),
            jax.ShapeDtypeStruct((LRU_W, 2 * LRU_W), f32),
            jax.ShapeDtypeStruct((1, 2 * LRU_W), f32),
            jax.ShapeDtypeStruct((8, LRU_W), f32),
            jax.ShapeDtypeStruct((1, LRU_W), f32),
            jax.ShapeDtypeStruct((1, LRU_W), f32),
        ],
        scratch_shapes=[pltpu.VMEM((tc, LRU_W), f32), pltpu.VMEM((tc, LRU_W), f32),
                        pltpu.VMEM((tc + 8, LRU_W), f32), pltpu.VMEM((tc + 8, LRU_W), f32),
                        pltpu.VMEM((8, LRU_W), f32)],
        compiler_params=_params(1),
    )(dlru, lxg, lxg, lxg, u, hs, hs, conv_w, wab, bab, lam)


def _mix_out(fox, lru, wo, xhat1, g1, b1, g2, b2, *, name, tm=512):
    T = fox.shape[0]
    tm = min(tm, T)
    nt = T // tm

    def body(fox_ref, lru_ref, wo_ref, xh_ref, g1_ref, b1_ref, g2_ref, b2_ref, xhat_ref, xn_ref, rstd_ref):
        mix = _dot(fox_ref[...].astype(MXU_DTYPE), wo_ref[:FOX_W, :])
        mix = mix + _dot(lru_ref[...].astype(MXU_DTYPE), wo_ref[FOX_W:, :])
        x1 = xh_ref[...] * g1_ref[...] + b1_ref[...]
        xhat, rstd = _layer_norm_stats(DN_ALPHA * x1 + mix)
        xhat_ref[...] = xhat
        xn_ref[...] = xhat * g2_ref[...] + b2_ref[...]
        rstd_ref[...] = jnp.broadcast_to(rstd, rstd_ref.shape)

    row = lambda i: (i, 0)
    const = lambda i: (0, 0)
    vec = pl.BlockSpec((1, D_MODEL), const)
    return _pcall(
        body, name=name, grid=(nt,),
        in_specs=[pl.BlockSpec((tm, FOX_W), row), pl.BlockSpec((tm, LRU_W), row),
                  pl.BlockSpec((D_MODEL, D_MODEL), const), pl.BlockSpec((tm, D_MODEL), row), vec, vec, vec, vec],
        out_specs=[pl.BlockSpec((tm, D_MODEL), row), pl.BlockSpec((tm, D_MODEL), row),
                   pl.BlockSpec((tm, LANES), row)],
        out_shape=[jax.ShapeDtypeStruct((T, D_MODEL), f32), jax.ShapeDtypeStruct((T, D_MODEL), f32),
                   jax.ShapeDtypeStruct((T, LANES), f32)],
        compiler_params=_params(1),
    )(fox, lru, wo, xhat1, g1, b1, g2, b2)


def _mix_out_bwd(dyp, fox, lru, wo, *, name, tm=512):
    T = fox.shape[0]
    tm = min(tm, T)
    nt = T // tm

    def body(dyp_ref, fox_ref, lru_ref, wo_ref, dfox_ref, dlru_ref, dwo_ref):
        i = pl.program_id(0)

        @pl.when(i == 0)
        def _():
            dwo_ref[...] = jnp.zeros_like(dwo_ref)

        dmix = dyp_ref[...].astype(MXU_DTYPE)
        dcat = _dot_nt(dmix, wo_ref[...])
        dfox_ref[...] = dcat[:, :FOX_W].astype(dfox_ref.dtype)
        dlru_ref[...] = dcat[:, FOX_W:]
        dwo_ref[:FOX_W, :] += _dot_tn(fox_ref[...].astype(MXU_DTYPE), dmix)
        dwo_ref[FOX_W:, :] += _dot_tn(lru_ref[...].astype(MXU_DTYPE), dmix)

    row = lambda i: (i, 0)
    const = lambda i: (0, 0)
    return _pcall(
        body, name=name, grid=(nt,),
        in_specs=[pl.BlockSpec((tm, D_MODEL), row), pl.BlockSpec((tm, FOX_W), row), pl.BlockSpec((tm, LRU_W), row),
                  pl.BlockSpec((D_MODEL, D_MODEL), const)],
        out_specs=[pl.BlockSpec((tm, FOX_W), row), pl.BlockSpec((tm, LRU_W), row),
                   pl.BlockSpec((D_MODEL, D_MODEL), const)],
        out_shape=[jax.ShapeDtypeStruct((T, FOX_W), MXU_DTYPE), jax.ShapeDtypeStruct((T, LRU_W), f32),
                   jax.ShapeDtypeStruct((D_MODEL, D_MODEL), f32)],
        compiler_params=_params(1),
    )(dyp, fox, lru, wo)


def make_wp(w_in):
    scale = jnp.concatenate([jnp.full((FOX_W,), 1.0 / math.sqrt(HEAD_DIM), w_in.dtype),
                             jnp.ones((IN_COLS - FOX_W,), w_in.dtype)])
    return jnp.pad(w_in * scale[None, :], ((0, 0), (0, Z_PAD - IN_COLS)))


def _block_diag(w):
    eye = jnp.eye(HEADS, dtype=w.dtype)
    return jnp.einsum("hij,hg->higj", w, eye).reshape(LRU_W, LRU_W)


def _block_diag_extract(m):
    m4 = m.reshape(HEADS, HEAD_DIM, HEADS, HEAD_DIM)
    return jnp.stack([m4[h, :, h, :] for h in range(HEADS)])


class _NoOverlap:
    def start_token(self):
        return None

    def after_attention(self, after):
        return None

    def ffn2_weights(self, w, after):
        return w["f2g"], w["f2u"], w["f2d"]

    def ffn2_grads(self, grads):
        return None

    def mixer_grads(self, dwp, dwo, small):
        return None

    def before_ffn1_bwd(self, after):
        return None


def _tied(a, token):
    return a if token is None else a + token[0, 0]


def _local_step(x, target, w, hooks=None):
    hooks = hooks or _NoOverlap()
    wp = w["wp"]
    bfp = w["bfp"]
    wab = jnp.concatenate([_block_diag(w["rg_wa"]), _block_diag(w["rg_wx"])], axis=1).astype(MXU_DTYPE)
    bab = jnp.concatenate([w["rg_ba"].reshape(1, LRU_W), w["rg_bx"].reshape(1, LRU_W)], axis=1)

    xb0, g1a, u1a, xhat1, xn1, rstd1 = _ffn_fwd(x, w["f1g"], w["f1u"], w["f1d"], w["ln1_g"],
                                                _tied(w["ln1_b"], hooks.start_token()), name="ffn1_fwd")
    qkv, lxg, fgb = _proj_in(xn1, wp, bfp, name="proj_in")
    qa, ka, va = _fox_prep(qkv, fgb, name="fox_prep")
    fox, lse = _fox_fwd(qa, ka, va, name="fox_fwd")
    token = hooks.after_attention([lse])
    lru, uconv, hs = _lru_fwd(lxg, w["conv_w"], _tied(w["conv_b"], token), wab, bab, w["lam"], name="lru_fwd")
    xhat2, x2, rstd2 = _mix_out(fox, lru, w["wo"], xhat1, w["ln1_g"], w["ln1_b"], w["ln2_g"], w["ln2_b"], name="mix_out")
    f2g, f2u, f2d = hooks.ffn2_weights(w, [rstd2])
    xb2, g2a, u2a, xhat3, _, rstd3 = _ffn_fwd(x2, f2g, f2u, f2d, w["ln3_g"], w["ln3_b"], name="ffn2_fwd")

    dy3p, dln3g, dln3b, loss = _loss_ln_bwd(xhat3, rstd3, w["ln3_g"], w["ln3_b"], target, name="loss_ln3_bwd")
    dx2, df2g, df2u, df2d = _ffn_bwd(dy3p, xb2, g2a, u2a, f2g, f2u, f2d, name="ffn2_bwd")
    token = hooks.ffn2_grads([df2g, df2u, df2d])
    dy2p, dln2g, dln2b = _ln_bwd(dx2, xhat2, rstd2, _tied(w["ln2_g"], token), name="ln2_bwd")
    dfox, dlru, dwo = _mix_out_bwd(dy2p, fox, lru, w["wo"], name="mix_out_bwd")
    dlxg, dwab, dbab, dcw, dcb, dlam = _lru_bwd(dlru, lxg, uconv, hs, w["conv_w"], wab, bab, w["lam"], name="lru_bwd")
    drep, doa = _fox_bwd_prep(dfox, fox, name="fox_bwd_prep")
    dqa, dka, dva = _fox_bwd(qa, ka, va, doa, lse, drep, name="fox_bwd")
    dfg, dbf = _fox_bwd_post(dqa, dka, fgb, name="fox_bwd_post")
    dx1, dwp = _proj_in_bwd(dqa, dka, dva, dlxg, dfg, xn1, dy2p, wp, name="proj_in_bwd")
    dy1p, dln1g, dln1b = _ln_bwd(dx1, xhat1, rstd1, w["ln1_g"], name="ln1_bwd")
    small = dict(
        ln1_g=dln1g, ln1_b=dln1b, ln2_g=dln2g, ln2_b=dln2b, ln3_g=dln3g, ln3_b=dln3b,
        b_forget=dbf[:, :HEADS], conv_w=dcw[:CONV_K], conv_b=dcb,
        rg_wa=_block_diag_extract(dwab[:, :LRU_W]), rg_wx=_block_diag_extract(dwab[:, LRU_W:]),
        rg_ba=dbab[:, :LRU_W].reshape(HEADS, HEAD_DIM), rg_bx=dbab[:, LRU_W:].reshape(HEADS, HEAD_DIM),
        lru_lambda=dlam,
    )
    hooks.before_ffn1_bwd([dln1b])
    token = hooks.mixer_grads(dwp, dwo, small)
    dx, df1g, df1u, df1d = _ffn_bwd(dy1p, xb0, g1a, u1a, w["f1g"], w["f1u"], w["f1d"], token, name="ffn1_bwd")

    grads = dict(f1g=df1g, f1u=df1u, f1d=df1d, f2g=df2g, f2u=df2u, f2d=df2d, wp=dwp, wo=dwo, **small)
    return loss, dx, grads


MESH = pl.DeviceIdType.MESH
HBM_SPEC = pl.BlockSpec(memory_space=pl.ANY)
VMEM_SPEC = pl.BlockSpec(memory_space=pltpu.VMEM)


def _position():
    return lax.axis_index("x"), lax.axis_index("y"), lax.axis_index("c")


def _other_chips(x, y):
    return [(1 - x, y), (x, 1 - y), (1 - x, 1 - y)]


def _all_gather_bf16(shards, *, name):
    n = len(shards)

    def body(*refs):
        ins, outs, stages = refs[:n], refs[n:2 * n], refs[2 * n:3 * n]
        send_sems, recv_sems, local_sems = refs[3 * n:]
        x, y, c = _position()
        me, sibling = (x, y, c), (x, y, 1 - c)
        chips = _other_chips(x, y)

        def rows(k, px, py, pc):
            r = shards[k].shape[0]
            m = r // 2
            return outs[k].at[pl.ds(pl.multiple_of((2 * px + py) * r + pc * m, 16), m), :]

        def copy(k, idx, block, to, src=None):
            return pltpu.make_async_remote_copy(
                src_ref=rows(k, *block) if src is None else src, dst_ref=rows(k, *block),
                send_sem=send_sems.at[7 * k + idx], recv_sem=recv_sems.at[7 * k + idx],
                device_id=to, device_id_type=MESH)

        started = []
        mine = []
        for k in range(n):
            m = shards[k].shape[0] // 2
            stages[k][...] = ins[k][pl.ds(pl.multiple_of(c * m, 16), m), :].astype(stages[k].dtype)
            cp = pltpu.make_async_copy(stages[k], rows(k, *me), local_sems.at[k])
            cp.start()
            mine.append(cp)
            first = [copy(k, 0, me, sibling, src=stages[k])]
            first += [copy(k, 1 + j, me, (*chip, c), src=stages[k]) for j, chip in enumerate(chips)]
            for cp in first:
                cp.start()
            started += first
        for k in range(n):
            for j, chip in enumerate(chips):
                copy(k, 1 + j, (*chip, c), me).wait_recv()
                fwd = copy(k, 4 + j, (*chip, c), sibling)
                fwd.start()
                started.append(fwd)
        for k in range(n):
            copy(k, 0, sibling, me).wait_recv()
            for j, chip in enumerate(chips):
                copy(k, 4 + j, (*chip, 1 - c), me).wait_recv()
        for cp in started:
            cp.wait_send()
        for cp in mine:
            cp.wait()

    return _pcall(
        body, name=name,
        in_specs=[VMEM_SPEC] * n, out_specs=[HBM_SPEC] * n,
        out_shape=[jax.ShapeDtypeStruct((N_SHARD * s.shape[0], s.shape[1]), MXU_DTYPE) for s in shards],
        scratch_shapes=[pltpu.VMEM((s.shape[0] // 2, s.shape[1]), MXU_DTYPE) for s in shards]
        + [pltpu.SemaphoreType.DMA((7 * n,)), pltpu.SemaphoreType.DMA((7 * n,)), pltpu.SemaphoreType.DMA((n,))],
        compiler_params=pltpu.CompilerParams(vmem_limit_bytes=VMEM_LIMIT),
    )(*shards)


def _swap_halves(gs, *, name):
    n = len(gs)

    def body(*refs):
        ins, outs = refs[:n], refs[n:2 * n]
        send_sems, recv_sems = refs[2 * n:]
        x, y, c = _position()
        cps = []
        for k in range(n):
            m = gs[k].shape[1] // 2
            src = ins[k].at[:, pl.ds(pl.multiple_of((1 - c) * m, 16), m), :]
            cp = pltpu.make_async_remote_copy(src_ref=src, dst_ref=outs[k], send_sem=send_sems.at[k],
                                              recv_sem=recv_sems.at[k], device_id=(x, y, 1 - c), device_id_type=MESH)
            cp.start()
            cps.append(cp)
        for cp in cps:
            cp.wait()

    return _pcall(
        body, name=name, in_specs=[HBM_SPEC] * n, out_specs=[HBM_SPEC] * n,
        out_shape=[jax.ShapeDtypeStruct((g.shape[0], g.shape[1] // 2, g.shape[2]), g.dtype) for g in gs],
        scratch_shapes=[pltpu.SemaphoreType.DMA((n,)), pltpu.SemaphoreType.DMA((n,))],
    )(*gs)


def _add_halves(gs, recvs, *, name, tm=256):
    n = len(gs)
    _, r, cdim = gs[0].shape
    m = r // 2
    tm = min(tm, m)
    nb = m // tm
    c_idx = lax.axis_index("c").astype(jnp.int32).reshape(1)

    def body(c_ref, *refs):
        for k in range(n):
            refs[2 * n + k][...] = (refs[k][...].astype(f32) + refs[n + k][...].astype(f32)).astype(refs[2 * n + k].dtype)

    mine = pl.BlockSpec((None, tm, cdim), lambda j, i, c_ref: (j, c_ref[0] * nb + i, 0))
    half = pl.BlockSpec((None, tm, cdim), lambda j, i, c_ref: (j, i, 0))
    return _pcall(
        body, name=name,
        grid_spec=pltpu.PrefetchScalarGridSpec(
            num_scalar_prefetch=1, grid=(N_SHARD, nb),
            in_specs=[mine] * n + [half] * n, out_specs=[half] * n),
        out_shape=[jax.ShapeDtypeStruct((N_SHARD, m, cdim), g.dtype) for g in gs],
        compiler_params=_params(2),
    )(c_idx, *gs, *recvs)


def _scatter_partials(ps, *, name):
    n = len(ps)

    def body(*refs):
        ins, outs = refs[:n], refs[n:2 * n]
        send_sems, recv_sems = refs[2 * n:]
        x, y, c = _position()
        me_chip = 2 * x + y
        cps = []
        for k in range(n):
            for j, (px, py) in enumerate(_other_chips(x, y)):
                cp = pltpu.make_async_remote_copy(
                    src_ref=ins[k].at[2 * px + py], dst_ref=outs[k].at[me_chip],
                    send_sem=send_sems.at[3 * k + j], recv_sem=recv_sems.at[3 * k + j],
                    device_id=(px, py, c), device_id_type=MESH)
                cp.start()
                cps.append(cp)
        for cp in cps:
            cp.wait()

    return _pcall(
        body, name=name, in_specs=[HBM_SPEC] * n, out_specs=[HBM_SPEC] * n,
        out_shape=[jax.ShapeDtypeStruct(p.shape, p.dtype) for p in ps],
        scratch_shapes=[pltpu.SemaphoreType.DMA((3 * n,)), pltpu.SemaphoreType.DMA((3 * n,))],
    )(*ps)


def _sum_slabs(ps, qs, *, name, tm=128):
    n = len(qs)
    _, m, cdim = qs[0].shape
    tm = min(tm, m)
    nb = m // tm
    assert m % tm == 0, (m, tm)
    where = jnp.stack([2 * lax.axis_index("x") + lax.axis_index("y"), lax.axis_index("c")]).astype(jnp.int32)

    def body(w_ref, *refs):
        for k in range(n):
            own, q1, q2, q3 = (refs[4 * k + t][...].astype(f32) for t in range(4))
            refs[4 * n + k][...] = ((own + q1) + q2) + q3

    def slab(flip):
        return pl.BlockSpec((None, tm, cdim), lambda i, w_ref: (jnp.bitwise_xor(w_ref[0], flip), i, 0))

    operands = []
    for p, q in zip(ps, qs):
        operands += [p, q, q, q]
    return _pcall(
        body, name=name,
        grid_spec=pltpu.PrefetchScalarGridSpec(
            num_scalar_prefetch=1, grid=(nb,),
            in_specs=[slab(0), slab(2), slab(1), slab(3)] * n,
            out_specs=[pl.BlockSpec((tm, cdim), lambda i, w_ref: (w_ref[1] * nb + i, 0))] * n),
        out_shape=[jax.ShapeDtypeStruct((2 * m, cdim), f32) for _ in qs],
        compiler_params=_params(1),
    )(where, *operands)


def _join_halves(fs, *, name):
    n = len(fs)

    def body(*refs):
        outs = refs[n:2 * n]
        send_sems, recv_sems = refs[2 * n:]
        x, y, c = _position()
        cps = []
        for k in range(n):
            m = fs[k].shape[0] // 2
            half = outs[k].at[pl.ds(pl.multiple_of(c * m, 8), m), :]
            cp = pltpu.make_async_remote_copy(src_ref=half, dst_ref=half, send_sem=send_sems.at[k],
                                              recv_sem=recv_sems.at[k], device_id=(x, y, 1 - c), device_id_type=MESH)
            cp.start()
            cps.append(cp)
        for cp in cps:
            cp.wait()

    return _pcall(
        body, name=name, in_specs=[HBM_SPEC] * n, out_specs=[HBM_SPEC] * n,
        out_shape=[jax.ShapeDtypeStruct(f.shape, f.dtype) for f in fs],
        input_output_aliases={k: k for k in range(n)},
        scratch_shapes=[pltpu.SemaphoreType.DMA((n,)), pltpu.SemaphoreType.DMA((n,))],
    )(*fs)


def _all_reduce_small(v, after=None, *, name):
    r = v.shape[0]
    extra = [] if after is None else [after]

    def body(v_ref, *refs):
        out_ref, buf, send_sems, recv_sems, local_sem = refs[len(extra):]
        x, y, c = _position()
        me, sibling = (x, y, c), (x, y, 1 - c)
        chips = _other_chips(x, y)

        def rows(px, py, pc):
            return buf.at[pl.ds(pl.multiple_of((4 * px + 2 * py + pc) * r, 8), r), :]

        def copy(k, block, to, src=None):
            return pltpu.make_async_remote_copy(
                src_ref=rows(*block) if src is None else src, dst_ref=rows(*block),
                send_sem=send_sems.at[k], recv_sem=recv_sems.at[k], device_id=to, device_id_type=MESH)

        mine = pltpu.make_async_copy(v_ref, rows(*me), local_sem)
        mine.start()
        first = [copy(0, me, sibling, src=v_ref)]
        first += [copy(1 + j, me, (*chip, c), src=v_ref) for j, chip in enumerate(chips)]
        for cp in first:
            cp.start()
        passed = [copy(4 + j, (*chip, c), sibling) for j, chip in enumerate(chips)]
        for j, chip in enumerate(chips):
            copy(1 + j, (*chip, c), me).wait_recv()
            passed[j].start()
        copy(0, sibling, me).wait_recv()
        for j, chip in enumerate(chips):
            copy(4 + j, (*chip, 1 - c), me).wait_recv()
        for cp in first + passed:
            cp.wait_send()
        mine.wait()
        acc = buf[0:r, :]
        for d in range(1, N_DEV):
            acc = acc + buf[d * r:(d + 1) * r, :]
        out_ref[...] = acc

    return _pcall(
        body, name=name, in_specs=[VMEM_SPEC] + [HBM_SPEC] * len(extra), out_specs=VMEM_SPEC,
        out_shape=jax.ShapeDtypeStruct((r, LANES), f32),
        scratch_shapes=[pltpu.VMEM((N_DEV * r, LANES), f32), pltpu.SemaphoreType.DMA((7,)),
                        pltpu.SemaphoreType.DMA((7,)), pltpu.SemaphoreType.DMA],
    )(v, *extra)


SEM_SPEC = pl.BlockSpec(memory_space=pltpu.SEMAPHORE)
HBM_ONLY = pl.BlockSpec(memory_space=pltpu.HBM)
EFFECT = pltpu.SideEffectType.DATAFLOW_SIDE_EFFECTING


def _split_start(bufs, copies_fn, n_sems, *, name):
    n = len(bufs)

    def body(*refs):
        send_sems, recv_sems = refs[n], refs[n + 1]
        thru = refs[n + 2:2 * n + 2]
        token = refs[2 * n + 2]
        for cp in copies_fn(thru, send_sems, recv_sems):
            cp.start()
        token[...] = jnp.zeros_like(token)

    outs = _pcall(
        body, name=name,
        out_shape=(pltpu.SemaphoreType.DMA((n_sems,)), pltpu.SemaphoreType.DMA((n_sems,)),
                   *[pltpu.HBM(b.shape, b.dtype) for b in bufs], jax.ShapeDtypeStruct((8, LANES), f32)),
        in_specs=[HBM_ONLY] * n,
        out_specs=(SEM_SPEC, SEM_SPEC, *[HBM_ONLY] * n, VMEM_SPEC),
        input_output_aliases={k: 2 + k for k in range(n)},
        compiler_params=pltpu.CompilerParams(has_side_effects=EFFECT),
    )(*[pltpu.with_memory_space_constraint(b, pltpu.HBM) for b in bufs])
    return outs[0], outs[1], list(outs[2:2 + n]), outs[2 + n]


def _split_wait(thru, send_sems, recv_sems, after, copies_fn, *, name):
    n = len(thru)

    def body(*refs):
        for cp in copies_fn(refs[:n], refs[n], refs[n + 1]):
            cp.wait_send()
            cp.wait_recv()

    return list(_pcall(
        body, name=name,
        out_shape=tuple(pltpu.HBM(b.shape, b.dtype) for b in thru),
        in_specs=[HBM_ONLY] * n + [SEM_SPEC, SEM_SPEC] + [HBM_SPEC] * len(after),
        out_specs=tuple([HBM_ONLY] * n),
        input_output_aliases={k: k for k in range(n)},
        compiler_params=pltpu.CompilerParams(has_side_effects=EFFECT),
    )(*thru, send_sems, recv_sems, *after))


def _scatter_copies(n):
    def copies(bufs, send_sems, recv_sems):
        x, y, c = _position()
        me_chip = 2 * x + y
        cps = []
        for k in range(n):
            for j, (px, py) in enumerate(_other_chips(x, y)):
                cps.append(pltpu.make_async_remote_copy(
                    src_ref=bufs[k].at[2 * px + py], dst_ref=bufs[n + k].at[me_chip],
                    send_sem=send_sems.at[3 * k + j], recv_sem=recv_sems.at[3 * k + j],
                    device_id=(px, py, c), device_id_type=MESH))
        return cps
    return copies


def _block_rows(buf, px, py, pc):
    m = buf.shape[0] // N_DEV
    return buf.at[pl.ds(pl.multiple_of((4 * px + 2 * py + pc) * m, 16), m), :]


def _gather_ici_copies(n):
    def copies(bufs, send_sems, recv_sems):
        x, y, c = _position()
        cps = []
        for k in range(n):
            rows = _block_rows(bufs[k], x, y, c)
            targets = [(x, y, 1 - c)] + [(px, py, c) for px, py in _other_chips(x, y)]
            for j, to in enumerate(targets):
                cps.append(pltpu.make_async_remote_copy(
                    src_ref=rows, dst_ref=rows, send_sem=send_sems.at[4 * k + j], recv_sem=recv_sems.at[4 * k + j],
                    device_id=to, device_id_type=MESH))
        return cps
    return copies


def _gather_d2d_copies(n):
    def copies(bufs, send_sems, recv_sems):
        x, y, c = _position()
        cps = []
        for k in range(n):
            for j, (px, py) in enumerate(_other_chips(x, y)):
                rows = _block_rows(bufs[k], px, py, c)
                cps.append(pltpu.make_async_remote_copy(
                    src_ref=rows, dst_ref=rows, send_sem=send_sems.at[3 * k + j], recv_sem=recv_sems.at[3 * k + j],
                    device_id=(x, y, 1 - c), device_id_type=MESH))
        return cps
    return copies


def _cast_halves(shards, after, *, name, tm=256):
    n = len(shards)
    r, cdim = shards[0].shape
    m = r // 2
    tm = min(tm, m)
    nb = m // tm
    assert m % tm == 0, (m, tm)
    where = jnp.stack([2 * lax.axis_index("x") + lax.axis_index("y"), lax.axis_index("c")]).astype(jnp.int32)

    def body(w_ref, *refs):
        for k in range(n):
            refs[n + 1 + k][...] = refs[k][...].astype(refs[n + 1 + k].dtype)

    return _pcall(
        body, name=name,
        grid_spec=pltpu.PrefetchScalarGridSpec(
            num_scalar_prefetch=1, grid=(nb,),
            in_specs=[pl.BlockSpec((tm, cdim), lambda i, w_ref: (w_ref[1] * nb + i, 0))] * n + [HBM_SPEC],
            out_specs=[pl.BlockSpec((tm, cdim), lambda i, w_ref: ((2 * w_ref[0] + w_ref[1]) * nb + i, 0))] * n),
        out_shape=[jax.ShapeDtypeStruct((N_SHARD * r, cdim), MXU_DTYPE) for _ in shards],
        compiler_params=_params(1),
    )(where, *shards, after)


class _Overlap(_NoOverlap):
    def __init__(self, ffn2_shards, after):
        halves = _cast_halves(ffn2_shards, after, name="ag2_cast")
        self.n = len(halves)
        self.ici = _split_start(halves, _gather_ici_copies(self.n), 4 * self.n, name="ag2_ici_start")
        self.reduced = None

    def start_token(self):
        return self.ici[3]

    def after_attention(self, after):
        send_sems, recv_sems, thru, _ = self.ici
        landed = _split_wait(thru, send_sems, recv_sems, after, _gather_ici_copies(self.n), name="ag2_ici_wait")
        self.d2d = _split_start(landed, _gather_d2d_copies(self.n), 3 * self.n, name="ag2_d2d_start")
        return self.d2d[3]

    def ffn2_weights(self, w, after):
        send_sems, recv_sems, thru, _ = self.d2d
        full = _split_wait(thru, send_sems, recv_sems, after, _gather_d2d_copies(self.n), name="ag2_d2d_wait")
        fs = D_FF // N_SHARD
        return (full[0].reshape(N_SHARD, D_MODEL, fs), full[1].reshape(N_SHARD, D_MODEL, fs),
                full[2].reshape(N_SHARD, fs, D_MODEL))

    def ffn2_grads(self, grads):
        recvs = _swap_halves(grads, name="rs_swap_ffn2")
        ps = _add_halves(grads, recvs, name="rs_add_ffn2")
        lands = [lax.empty(p.shape, p.dtype) for p in ps]
        self.scatter = _split_start(list(ps) + lands, _scatter_copies(len(ps)), 3 * len(ps), name="rs_scatter_ffn2_start")
        return self.scatter[3]

    def mixer_grads(self, dwp, dwo, small):
        self.small_sum = _all_reduce_small(_pack_small(small), name="ar_small")
        gwin = dwp[:, :IN_COLS].reshape(D_MODEL, N_SHARD, IN_SHARD).transpose(1, 0, 2).astype(GRAD_DTYPE)
        gwo = dwo.reshape(N_SHARD, D_MODEL // N_SHARD, D_MODEL).astype(GRAD_DTYPE)
        recvs = _swap_halves([gwin, gwo], name="rs_swap_mix")
        ps = [_add_halves([g], [r], name=f"rs_add_{tag}")[0] for g, r, tag in zip([gwin, gwo], recvs, ["w_in", "w_out"])]
        lands = [lax.empty(p.shape, p.dtype) for p in ps]
        self.scatter_mix = _split_start(ps + lands, _scatter_copies(2), 6, name="rs_scatter_mix_start")
        return self.scatter_mix[3]

    def mixer_reduced(self, after):
        send_sems, recv_sems, thru, _ = self.scatter_mix
        done = _split_wait(thru, send_sems, recv_sems, after, _scatter_copies(2), name="rs_scatter_mix_wait")
        return [_sum_slabs([done[k]], [done[2 + k]], name=f"rs_sum_{tag}")[0] for k, tag in enumerate(["w_in", "w_out"])]

    def before_ffn1_bwd(self, after):
        send_sems, recv_sems, thru, _ = self.scatter
        n = len(thru) // 2
        done = _split_wait(thru, send_sems, recv_sems, after, _scatter_copies(n), name="rs_scatter_ffn2_wait")
        self.reduced = list(_sum_slabs(done[:n], done[n:], name="rs_sum_ffn2"))


def _adamw(gs, ws, ms, vs, *, name, tm=256):
    n = len(gs)
    r, cdim = gs[0].shape
    tm = r if tm is None else min(tm, r)
    assert r % tm == 0, (r, tm)
    c1 = 1.0 / (1.0 - ADAM_B1 ** ADAM_STEP)
    c2 = 1.0 / (1.0 - ADAM_B2 ** ADAM_STEP)

    def body(*refs):
        for k in range(n):
            g = refs[k][...]
            w = refs[n + k][...]
            m = ADAM_B1 * refs[2 * n + k][...] + (1.0 - ADAM_B1) * g
            v = ADAM_B2 * refs[3 * n + k][...] + (1.0 - ADAM_B2) * (g * g)
            refs[4 * n + k][...] = -ADAM_LR * ((m * c1) / (jnp.sqrt(v * c2) + ADAM_EPS) + ADAM_WD * w)
            refs[5 * n + k][...] = m
            refs[6 * n + k][...] = v

    spec = pl.BlockSpec((tm, cdim), lambda i: (i, 0))
    outs = _pcall(
        body, name=name, grid=(r // tm,), in_specs=[spec] * (4 * n), out_specs=[spec] * (3 * n),
        out_shape=[jax.ShapeDtypeStruct((r, cdim), f32)] * (3 * n),
        compiler_params=_params(1),
    )(*gs, *ws, *ms, *vs)
    return outs[:n], outs[n:2 * n], outs[2 * n:]


BIG = ["ffn1_w_gate", "ffn1_w_up", "ffn1_w_down", "ffn2_w_gate", "ffn2_w_up", "ffn2_w_down"]
SMALL = ["ln1_g", "ln1_b", "b_forget", "conv_w", "conv_b", "rg_wa", "rg_ba", "rg_wx", "rg_bx", "lru_lambda",
         "ln2_g", "ln2_b", "ln3_g", "ln3_b"]
WEIGHTS = ["ffn1_w_gate", "ffn1_w_up", "ffn1_w_down", "ln1_g", "ln1_b", "w_in", "b_forget", "conv_w", "conv_b",
           "rg_wa", "rg_ba", "rg_wx", "rg_bx", "lru_lambda", "w_out", "ln2_g", "ln2_b",
           "ffn2_w_gate", "ffn2_w_up", "ffn2_w_down", "ln3_g", "ln3_b"]


def _pack_small(parts):
    rows = []
    for n in SMALL:
        flat = parts[n].reshape(-1)
        pad = (-flat.shape[0]) % LANES
        rows.append(jnp.pad(flat, (0, pad)).reshape(-1, LANES))
    packed = jnp.concatenate(rows, axis=0)
    return jnp.pad(packed, ((0, (-packed.shape[0]) % 8), (0, 0)))


def _unpack_small(packed, shapes):
    out, r0 = {}, 0
    for n in SMALL:
        size = math.prod(shapes[n])
        nr = -(-size // LANES)
        out[n] = packed[r0:r0 + nr].reshape(-1)[:size].reshape(shapes[n])
        r0 += nr
    return out


def kernel(x, ffn1_w_gate, ffn1_w_up, ffn1_w_down, ln1_g, ln1_b, w_in, b_forget, conv_w, conv_b, rg_wa, rg_ba, rg_wx, rg_bx, lru_lambda, w_out, ln2_g, ln2_b, ffn2_w_gate, ffn2_w_up, ffn2_w_down, ln3_g, ln3_b, loss_target, m_ffn1_w_gate, m_ffn1_w_up, m_ffn1_w_down, m_ln1_g, m_ln1_b, m_w_in, m_b_forget, m_conv_w, m_conv_b, m_rg_wa, m_rg_ba, m_rg_wx, m_rg_bx, m_lru_lambda, m_w_out, m_ln2_g, m_ln2_b, m_ffn2_w_gate, m_ffn2_w_up, m_ffn2_w_down, m_ln3_g, m_ln3_b, v_ffn1_w_gate, v_ffn1_w_up, v_ffn1_w_down, v_ln1_g, v_ln1_b, v_w_in, v_b_forget, v_conv_w, v_conv_b, v_rg_wa, v_rg_ba, v_rg_wx, v_rg_bx, v_lru_lambda, v_w_out, v_ln2_g, v_ln2_b, v_ffn2_w_gate, v_ffn2_w_up, v_ffn2_w_down, v_ln3_g, v_ln3_b):
    args = dict(locals())
    w = {n: args[n] for n in WEIGHTS}
    mom = {n: args["m_" + n] for n in WEIGHTS}
    var = {n: args["v_" + n] for n in WEIGHTS}
    chip = 2 * lax.axis_index("x") + lax.axis_index("y")

    g1 = _all_gather_bf16([w[n][0] for n in BIG[:3]] + [w["w_in"][0], w["w_out"][0]], name="ag_first")
    fs = D_FF // N_SHARD
    w_in_full = g1[3].reshape(N_SHARD, D_MODEL, IN_SHARD).transpose(1, 0, 2).reshape(D_MODEL, IN_COLS)
    full = dict(
        f1g=g1[0].reshape(N_SHARD, D_MODEL, fs), f1u=g1[1].reshape(N_SHARD, D_MODEL, fs),
        f1d=g1[2].reshape(N_SHARD, fs, D_MODEL),
        wp=make_wp(w_in_full), bfp=jnp.pad(b_forget, ((0, 0), (0, LANES - HEADS))), wo=g1[4],
        ln1_g=ln1_g, ln1_b=ln1_b, ln2_g=ln2_g, ln2_b=ln2_b, ln3_g=ln3_g, ln3_b=ln3_b,
        conv_b=conv_b, rg_wa=rg_wa[0], rg_wx=rg_wx[0], rg_ba=rg_ba[0], rg_bx=rg_bx[0], lam=lru_lambda,
    )
    cw_place = lax.dynamic_update_slice(jnp.zeros((8, LRU_W), f32), conv_w[0] * 0.5, (0, chip * (LRU_W // N_SHARD)))
    cw_full = _all_reduce_small(cw_place.reshape(-1, LANES), g1[0], name="ag_conv_w")
    full["conv_w"] = cw_full.reshape(8, LRU_W)[:CONV_K]

    hooks = _Overlap([w[n][0] for n in BIG[3:]], cw_full)
    loss_rep, dx, g = _local_step(x[0], loss_target[0], full, hooks)
    loss = lax.psum(loss_rep[0, 0], ("x", "y", "c"))

    gs1 = [g["f1g"], g["f1u"], g["f1d"]]
    ps1 = _add_halves(gs1, _swap_halves(gs1, name="rs_swap_ffn1"), name="rs_add_ffn1")
    lands = [lax.empty(p.shape, p.dtype) for p in ps1]
    send1, recv1, thru1, token1 = _split_start(list(ps1) + lands, _scatter_copies(3), 9, name="rs_scatter_ffn1_start")
    red = _join_halves(hooks.reduced + hooks.mixer_reduced([token1]), name="rs_join_rest")
    grads = dict(zip(BIG[3:] + ["w_in", "w_out"], red))

    small_shapes = {n: w[n].shape for n in SMALL}
    small_shapes["conv_w"] = (1, CONV_K, LRU_W)
    gs_red = _unpack_small(hooks.small_sum, small_shapes)
    gs_red["conv_w"] = lax.dynamic_slice(gs_red["conv_w"], (0, 0, chip * (LRU_W // N_SHARD)),
                                         (1, CONV_K, LRU_W // N_SHARD))
    grads.update(gs_red)

    delta, new_m, new_v = {}, {}, {}

    def adamw(names, name, **kw):
        d, nm, nv = _adamw([grads[n] for n in names], [w[n][0] for n in names], [mom[n][0] for n in names],
                           [var[n][0] for n in names], name=name, **kw)
        for i, n in enumerate(names):
            delta[n], new_m[n], new_v[n] = d[i], nm[i], nv[i]

    adamw(BIG[3:], "adamw_ffn2", tm=128)
    adamw(["w_in"], "adamw_w_in")
    adamw(["w_out"], "adamw_w_out")
    shard_shapes = {n: w[n].shape for n in SMALL}
    d, nm, nv = _adamw([_pack_small({n: grads[n] for n in SMALL})], [_pack_small({n: w[n] for n in SMALL})],
                       [_pack_small({n: mom[n] for n in SMALL})], [_pack_small({n: var[n] for n in SMALL})],
                       name="adamw_small", tm=None)
    for dst, packed in ((delta, d[0]), (new_m, nm[0]), (new_v, nv[0])):
        dst.update(_unpack_small(packed, shard_shapes))

    worked = [new_v["ffn2_w_down"], new_v["w_in"], new_v["w_out"], nv[0]]
    done1 = _split_wait(thru1, send1, recv1, worked, _scatter_copies(3), name="rs_scatter_ffn1_wait")
    red1 = _join_halves(list(_sum_slabs(done1[:3], done1[3:], name="rs_sum_ffn1")), name="rs_join_ffn1")
    grads.update(zip(BIG[:3], red1))
    adamw(BIG[:3], "adamw_ffn1", tm=128)

    def shaped(tree, n):
        return tree[n].reshape(w[n].shape)

    return (loss, dx[None], *[shaped(grads, n) for n in WEIGHTS], *[shaped(delta, n) for n in WEIGHTS],
            *[shaped(new_m, n) for n in WEIGHTS], *[shaped(new_v, n) for n in WEIGHTS])
```

```python
import functools
import math

import jax
import jax.numpy as jnp
from jax import lax
from jax.experimental import pallas as pl
from jax.experimental.pallas import tpu as pltpu

f32 = jnp.float32
MXU_DTYPE = jnp.bfloat16
GRAD_DTYPE = jnp.bfloat16

D_MODEL = 1024
D_FF = 4096
N_SHARD = 4
N_DEV = 8
FOX_W = 512
LRU_W = 512
HEADS = 8
HEAD_DIM = 64
CONV_K = 4
IN_COLS = 2568
IN_SHARD = IN_COLS // N_SHARD
QKV_W = 3 * FOX_W
Z_PAD = 2688
LANES = 128
LN_EPS = 1e-5
DN_ALPHA = 2.0 ** 0.25
LRU_C = 8.0
NEG_BIG = -1e30
VMEM_LIMIT = 56 * 1024 * 1024

ADAM_LR = 0.001
ADAM_B1 = 0.9
ADAM_B2 = 0.999
ADAM_EPS = 1e-08
ADAM_WD = 0.01
ADAM_STEP = 10


def _pcall(body, **kw):
    return pl.pallas_call(body, **kw)


def _params(n_grid, vmem=VMEM_LIMIT):
    return pltpu.CompilerParams(dimension_semantics=("arbitrary",) * n_grid, vmem_limit_bytes=vmem)


def _dot(a, b):
    return jnp.dot(a, b, preferred_element_type=f32)


def _dot_nt(a, b):
    return lax.dot_general(a, b, (((1,), (1,)), ((), ())), preferred_element_type=f32)


def _dot_tn(a, b):
    return lax.dot_general(a, b, (((0,), (0,)), ((), ())), preferred_element_type=f32)


def _sigmoid(x):
    return 1.0 / (1.0 + jnp.exp(-x))


def _layer_norm_stats(y):
    mu = jnp.mean(y, axis=-1, keepdims=True)
    yc = y - mu
    var = jnp.mean(yc * yc, axis=-1, keepdims=True)
    rstd = lax.rsqrt(var + LN_EPS)
    return yc * rstd, rstd


def _ln_backward(dy, xhat, rstd, gamma):
    dxhat = dy * gamma
    m1 = jnp.mean(dxhat, axis=-1, keepdims=True)
    m2 = jnp.mean(dxhat * xhat, axis=-1, keepdims=True)
    dyp = rstd * (dxhat - m1 - xhat * m2)
    return dyp, jnp.sum(dy * xhat, axis=0, keepdims=True), jnp.sum(dy, axis=0, keepdims=True)


def _ffn_fwd(x, wg, wu, wd, ln_g, ln_b, *, name, tm=1024, tf=512):
    T = x.shape[0]
    tm = min(tm, T)
    fs = D_FF // N_SHARD
    cpf = fs // tf
    nf = D_FF // tf
    nt = T // tm

    def body(x_ref, wg_ref, wu_ref, wd_ref, g_ref, b_ref,
             xb_ref, gact_ref, uact_ref, xhat_ref, xn_ref, rstd_ref, acc_ref):
        f = pl.program_id(1)

        @pl.when(f == 0)
        def _():
            xb_ref[...] = x_ref[...].astype(MXU_DTYPE)
            acc_ref[...] = jnp.zeros_like(acc_ref)

        xb = xb_ref[...]
        g = _dot(xb, wg_ref[...])
        u = _dot(xb, wu_ref[...])
        h = (g * _sigmoid(g)) * u
        gact_ref[...] = g.astype(gact_ref.dtype)
        uact_ref[...] = u.astype(uact_ref.dtype)
        acc_ref[...] += _dot(h.astype(MXU_DTYPE), wd_ref[...])

        @pl.when(f == nf - 1)
        def _():
            y = DN_ALPHA * x_ref[...] + 0.5 * acc_ref[...]
            xhat, rstd = _layer_norm_stats(y)
            xhat_ref[...] = xhat
            xn_ref[...] = (xhat * g_ref[...] + b_ref[...]).astype(xn_ref.dtype)
            rstd_ref[...] = jnp.broadcast_to(rstd, rstd_ref.shape)

    row = lambda i, f: (i, 0)
    return _pcall(
        body, name=name, grid=(nt, nf),
        in_specs=[
            pl.BlockSpec((tm, D_MODEL), row),
            pl.BlockSpec((None, D_MODEL, tf), lambda i, f: (f // cpf, 0, f % cpf)),
            pl.BlockSpec((None, D_MODEL, tf), lambda i, f: (f // cpf, 0, f % cpf)),
            pl.BlockSpec((None, tf, D_MODEL), lambda i, f: (f // cpf, f % cpf, 0)),
            pl.BlockSpec((1, D_MODEL), lambda i, f: (0, 0)),
            pl.BlockSpec((1, D_MODEL), lambda i, f: (0, 0)),
        ],
        out_specs=[
            pl.BlockSpec((tm, D_MODEL), row),
            pl.BlockSpec((tm, tf), lambda i, f: (i, f)),
            pl.BlockSpec((tm, tf), lambda i, f: (i, f)),
            pl.BlockSpec((tm, D_MODEL), row),
            pl.BlockSpec((tm, D_MODEL), row),
            pl.BlockSpec((tm, LANES), row),
        ],
        out_shape=[
            jax.ShapeDtypeStruct((T, D_MODEL), MXU_DTYPE),
            jax.ShapeDtypeStruct((T, D_FF), MXU_DTYPE),
            jax.ShapeDtypeStruct((T, D_FF), MXU_DTYPE),
            jax.ShapeDtypeStruct((T, D_MODEL), f32),
            jax.ShapeDtypeStruct((T, D_MODEL), MXU_DTYPE),
            jax.ShapeDtypeStruct((T, LANES), f32),
        ],
        scratch_shapes=[pltpu.VMEM((tm, D_MODEL), f32)],
        compiler_params=_params(2),
    )(x, wg, wu, wd, ln_g, ln_b)


def _ffn_up(x, wg, wu, after=None, *, name, tm=1024, tf=512):
    T = x.shape[0]
    tm = min(tm, T)
    cpf = (D_FF // N_SHARD) // tf
    nf = D_FF // tf
    extra = [] if after is None else [after]

    def body(x_ref, wg_ref, wu_ref, *refs):
        xb_ref, gact_ref, uact_ref, hact_ref = refs[len(extra):]

        @pl.when(pl.program_id(1) == 0)
        def _():
            xb_ref[...] = x_ref[...].astype(MXU_DTYPE)

        xb = xb_ref[...]
        g = _dot(xb, wg_ref[...])
        u = _dot(xb, wu_ref[...])
        gact_ref[...] = g.astype(gact_ref.dtype)
        uact_ref[...] = u.astype(uact_ref.dtype)
        hact_ref[...] = ((g * _sigmoid(g)) * u).astype(hact_ref.dtype)

    row = lambda i, f: (i, 0)
    tile = pl.BlockSpec((tm, tf), lambda i, f: (i, f))
    cols = pl.BlockSpec((None, D_MODEL, tf), lambda i, f: (f // cpf, 0, f % cpf))
    return _pcall(
        body, name=name, grid=(T // tm, nf),
        in_specs=[pl.BlockSpec((tm, D_MODEL), row), cols, cols] + [pl.BlockSpec(memory_space=pl.ANY)] * len(extra),
        out_specs=[pl.BlockSpec((tm, D_MODEL), row), tile, tile, tile],
        out_shape=[jax.ShapeDtypeStruct((T, D_MODEL), MXU_DTYPE)] + [jax.ShapeDtypeStruct((T, D_FF), MXU_DTYPE)] * 3,
        compiler_params=_params(2),
    )(x, wg, wu, *extra)


def _ffn_down_ln(x, hact, wd, ln_g, ln_b, *, name, tm=1024):
    T = x.shape[0]
    tm = min(tm, T)
    fs = D_FF // N_SHARD

    def body(x_ref, h_ref, wd_ref, g_ref, b_ref, xhat_ref, xn_ref, rstd_ref, acc_ref):
        k = pl.program_id(1)

        @pl.when(k == 0)
        def _():
            acc_ref[...] = jnp.zeros_like(acc_ref)

        acc_ref[...] += _dot(h_ref[...], wd_ref[...])

        @pl.when(k == N_SHARD - 1)
        def _():
            xhat, rstd = _layer_norm_stats(DN_ALPHA * x_ref[...] + 0.5 * acc_ref[...])
            xhat_ref[...] = xhat
            xn_ref[...] = (xhat * g_ref[...] + b_ref[...]).astype(xn_ref.dtype)
            rstd_ref[...] = jnp.broadcast_to(rstd, rstd_ref.shape)

    row = lambda i, k: (i, 0)
    vec = pl.BlockSpec((1, D_MODEL), lambda i, k: (0, 0))
    return _pcall(
        body, name=name, grid=(T // tm, N_SHARD),
        in_specs=[pl.BlockSpec((tm, D_MODEL), row), pl.BlockSpec((tm, fs), lambda i, k: (i, k)),
                  pl.BlockSpec((None, fs, D_MODEL), lambda i, k: (k, 0, 0)), vec, vec],
        out_specs=[pl.BlockSpec((tm, D_MODEL), row), pl.BlockSpec((tm, D_MODEL), row), pl.BlockSpec((tm, LANES), row)],
        out_shape=[jax.ShapeDtypeStruct((T, D_MODEL), f32), jax.ShapeDtypeStruct((T, D_MODEL), MXU_DTYPE),
                   jax.ShapeDtypeStruct((T, LANES), f32)],
        scratch_shapes=[pltpu.VMEM((tm, D_MODEL), f32)],
        compiler_params=_params(2),
    )(x, hact, wd, ln_g, ln_b)


def _ffn_bwd(dyp, xb, gact, uact, wg, wu, wd, after=None, *, name, tm=512, tf=512):
    T = dyp.shape[0]
    tm = min(tm, T)
    fs = D_FF // N_SHARD
    cpf = fs // tf
    nf = D_FF // tf
    nt = T // tm
    extra = [] if after is None else [after]

    def body(dyp_ref, xb_ref, g_ref, u_ref, wg_ref, wu_ref, wd_ref, *refs):
        dx_hbm, dwg_ref, dwu_ref, dwd_ref, dx_sc, dwg_sc, dwu_sc, dwd_sc, sem = refs[len(extra):]
        f = pl.program_id(0)
        i = pl.program_id(1)
        rows = pl.ds(pl.multiple_of(i * tm, tm), tm)
        dyp_t = dyp_ref[...]
        dy = (0.5 * dyp_t).astype(MXU_DTYPE)

        @pl.when(i == 0)
        def _():
            dwg_sc[...] = jnp.zeros_like(dwg_sc)
            dwu_sc[...] = jnp.zeros_like(dwu_sc)
            dwd_sc[...] = jnp.zeros_like(dwd_sc)

        @pl.when(f == 0)
        def _():
            dx_sc[rows, :] = DN_ALPHA * dyp_t

        g = g_ref[...].astype(f32)
        u = u_ref[...].astype(f32)
        sig = _sigmoid(g)
        silu = g * sig
        dh = _dot_nt(dy, wd_ref[...])
        dg = (dh * u * (sig * (1.0 + g * (1.0 - sig)))).astype(MXU_DTYPE)
        du = (dh * silu).astype(MXU_DTYPE)
        hb = (silu * u).astype(MXU_DTYPE)
        dx_sc[rows, :] += _dot_nt(dg, wg_ref[...]) + _dot_nt(du, wu_ref[...])
        xb_t = xb_ref[...]
        dwg_sc[...] += _dot_tn(xb_t, dg)
        dwu_sc[...] += _dot_tn(xb_t, du)
        dwd_sc[...] += _dot_tn(hb, dy)

        @pl.when(i == nt - 1)
        def _():
            dwg_ref[...] = dwg_sc[...].astype(dwg_ref.dtype)
            dwu_ref[...] = dwu_sc[...].astype(dwu_ref.dtype)
            dwd_ref[...] = dwd_sc[...].astype(dwd_ref.dtype)

        @pl.when(jnp.logical_and(f == nf - 1, i == nt - 1))
        def _():
            cp = pltpu.make_async_copy(dx_sc, dx_hbm, sem)
            cp.start()
            cp.wait()

    row = lambda f, i: (i, 0)
    return _pcall(
        body, name=name, grid=(nf, nt),
        in_specs=[
            pl.BlockSpec((tm, D_MODEL), row),
            pl.BlockSpec((tm, D_MODEL), row),
            pl.BlockSpec((tm, tf), lambda f, i: (i, f)),
            pl.BlockSpec((tm, tf), lambda f, i: (i, f)),
            pl.BlockSpec((None, D_MODEL, tf), lambda f, i: (f // cpf, 0, f % cpf)),
            pl.BlockSpec((None, D_MODEL, tf), lambda f, i: (f // cpf, 0, f % cpf)),
            pl.BlockSpec((None, tf, D_MODEL), lambda f, i: (f // cpf, f % cpf, 0)),
        ] + [pl.BlockSpec(memory_space=pl.ANY)] * len(extra),
        out_specs=[
            pl.BlockSpec(memory_space=pl.ANY),
            pl.BlockSpec((None, D_MODEL, tf), lambda f, i: (f // cpf, 0, f % cpf)),
            pl.BlockSpec((None, D_MODEL, tf), lambda f, i: (f // cpf, 0, f % cpf)),
            pl.BlockSpec((None, tf, D_MODEL), lambda f, i: (f // cpf, f % cpf, 0)),
        ],
        out_shape=[
            jax.ShapeDtypeStruct((T, D_MODEL), f32),
            jax.ShapeDtypeStruct((N_SHARD, D_MODEL, fs), GRAD_DTYPE),
            jax.ShapeDtypeStruct((N_SHARD, D_MODEL, fs), GRAD_DTYPE),
            jax.ShapeDtypeStruct((N_SHARD, fs, D_MODEL), GRAD_DTYPE),
        ],
        scratch_shapes=[pltpu.VMEM((T, D_MODEL), f32), pltpu.VMEM((D_MODEL, tf), f32),
                        pltpu.VMEM((D_MODEL, tf), f32), pltpu.VMEM((tf, D_MODEL), f32),
                        pltpu.SemaphoreType.DMA],
        compiler_params=_params(2),
    )(dyp, xb, gact, uact, wg, wu, wd, *extra)


def _loss_ln_bwd(xhat, rstd, ln_g, ln_b, target, *, name, tm=512):
    T = xhat.shape[0]
    tm = min(tm, T)
    nt = T // tm

    def body(xhat_ref, rstd_ref, g_ref, b_ref, t_ref, dyp_ref, dg_ref, db_ref, loss_ref):
        i = pl.program_id(0)

        @pl.when(i == 0)
        def _():
            dg_ref[...] = jnp.zeros_like(dg_ref)
            db_ref[...] = jnp.zeros_like(db_ref)
            loss_ref[...] = jnp.zeros_like(loss_ref)

        xhat_t = xhat_ref[...]
        gamma = g_ref[...]
        err = xhat_t * gamma + b_ref[...] - t_ref[...]
        sq = jnp.sum(jnp.sum(err * err, axis=0, keepdims=True), axis=1, keepdims=True)
        loss_ref[...] += jnp.broadcast_to(sq * (0.5 / D_MODEL), loss_ref.shape)
        dy = err * (1.0 / D_MODEL)
        dyp, dgam, dbeta = _ln_backward(dy, xhat_t, rstd_ref[:, 0:1], gamma)
        dyp_ref[...] = dyp
        dg_ref[...] += dgam
        db_ref[...] += dbeta

    row = lambda i: (i, 0)
    const = lambda i: (0, 0)
    return _pcall(
        body, name=name, grid=(nt,),
        in_specs=[pl.BlockSpec((tm, D_MODEL), row), pl.BlockSpec((tm, LANES), row),
                  pl.BlockSpec((1, D_MODEL), const), pl.BlockSpec((1, D_MODEL), const),
                  pl.BlockSpec((tm, D_MODEL), row)],
        out_specs=[pl.BlockSpec((tm, D_MODEL), row), pl.BlockSpec((1, D_MODEL), const),
                   pl.BlockSpec((1, D_MODEL), const), pl.BlockSpec((1, LANES), const)],
        out_shape=[jax.ShapeDtypeStruct((T, D_MODEL), f32), jax.ShapeDtypeStruct((1, D_MODEL), f32),
                   jax.ShapeDtypeStruct((1, D_MODEL), f32), jax.ShapeDtypeStruct((1, LANES), f32)],
        compiler_params=_params(1),
    )(xhat, rstd, ln_g, ln_b, target)


def _ln_bwd(dy, xhat, rstd, ln_g, *, name, tm=512):
    T = xhat.shape[0]
    tm = min(tm, T)
    nt = T // tm

    def body(dy_ref, xhat_ref, rstd_ref, g_ref, dyp_ref, dg_ref, db_ref):
        i = pl.program_id(0)

        @pl.when(i == 0)
        def _():
            dg_ref[...] = jnp.zeros_like(dg_ref)
            db_ref[...] = jnp.zeros_like(db_ref)

        dyp, dgam, dbeta = _ln_backward(dy_ref[...], xhat_ref[...], rstd_ref[:, 0:1], g_ref[...])
        dyp_ref[...] = dyp
        dg_ref[...] += dgam
        db_ref[...] += dbeta

    row = lambda i: (i, 0)
    const = lambda i: (0, 0)
    return _pcall(
        body, name=name, grid=(nt,),
        in_specs=[pl.BlockSpec((tm, D_MODEL), row), pl.BlockSpec((tm, D_MODEL), row),
                  pl.BlockSpec((tm, LANES), row), pl.BlockSpec((1, D_MODEL), const)],
        out_specs=[pl.BlockSpec((tm, D_MODEL), row), pl.BlockSpec((1, D_MODEL), const),
                   pl.BlockSpec((1, D_MODEL), const)],
        out_shape=[jax.ShapeDtypeStruct((T, D_MODEL), f32), jax.ShapeDtypeStruct((1, D_MODEL), f32),
                   jax.ShapeDtypeStruct((1, D_MODEL), f32)],
        compiler_params=_params(1),
    )(dy, xhat, rstd, ln_g)


def _proj_in(xn, wp, bfp, *, name, tm=512):
    T = xn.shape[0]
    tm = min(tm, T)
    nt = T // tm

    def body(x_ref, w_ref, b_ref, qkv_ref, lxg_ref, fg_ref):
        z = _dot(x_ref[...], w_ref[...])
        qkv_ref[...] = z[:, :QKV_W].astype(qkv_ref.dtype)
        lxg_ref[...] = z[:, QKV_W:QKV_W + 2 * LRU_W]
        fg_ref[...] = z[:, QKV_W + 2 * LRU_W:] + b_ref[...]

    row = lambda i: (i, 0)
    const = lambda i: (0, 0)
    return _pcall(
        body, name=name, grid=(nt,),
        in_specs=[pl.BlockSpec((tm, D_MODEL), row), pl.BlockSpec((D_MODEL, Z_PAD), const),
                  pl.BlockSpec((1, LANES), const)],
        out_specs=[pl.BlockSpec((tm, QKV_W), row), pl.BlockSpec((tm, 2 * LRU_W), row),
                   pl.BlockSpec((tm, LANES), row)],
        out_shape=[jax.ShapeDtypeStruct((T, QKV_W), MXU_DTYPE), jax.ShapeDtypeStruct((T, 2 * LRU_W), f32),
                   jax.ShapeDtypeStruct((T, LANES), f32)],
        compiler_params=_params(1),
    )(xn, wp, bfp)


def _proj_in_bwd(dqa, dka, dva, dlxg, dfg, xn, dyp, wp, *, name, tm=512):
    T = xn.shape[0]
    tm = min(tm, T)
    nt = T // tm

    def body(dq_ref, dk_ref, dv_ref, dl_ref, dfg_ref, x_ref, dyp_ref, w_ref, dx_ref, dw_hbm, dw_sc, sem):
        i = pl.program_id(0)

        @pl.when(i == 0)
        def _():
            dw_sc[...] = jnp.zeros_like(dw_sc)

        low = _low_lanes((tm, LANES))

        def packed(ref):
            pairs = [jnp.where(low, ref[:, (2 * j) * LANES:(2 * j + 1) * LANES],
                               _swap_lane_halves(ref[:, (2 * j + 1) * LANES:(2 * j + 2) * LANES]))
                     for j in range(HEADS // 2)]
            return jnp.concatenate(pairs, axis=1).astype(MXU_DTYPE)

        dz = jnp.concatenate(
            [packed(dq_ref), packed(dk_ref), packed(dv_ref),
             dl_ref[...].astype(MXU_DTYPE), dfg_ref[...].astype(MXU_DTYPE)], axis=1)
        dx_ref[...] = DN_ALPHA * dyp_ref[...] + _dot_nt(dz, w_ref[...])
        dw_sc[...] += _dot_tn(x_ref[...], dz)

        @pl.when(i == nt - 1)
        def _():
            dw_sc[:, :FOX_W] = dw_sc[:, :FOX_W] * (1.0 / math.sqrt(HEAD_DIM))
            cp = pltpu.make_async_copy(dw_sc, dw_hbm, sem)
            cp.start()
            cp.wait()

    row = lambda i: (i, 0)
    const = lambda i: (0, 0)
    return _pcall(
        body, name=name, grid=(nt,),
        in_specs=[pl.BlockSpec((tm, HEADS * LANES), row), pl.BlockSpec((tm, HEADS * LANES), row),
                  pl.BlockSpec((tm, HEADS * LANES), row),
                  pl.BlockSpec((tm, 2 * LRU_W), row), pl.BlockSpec((tm, LANES), row),
                  pl.BlockSpec((tm, D_MODEL), row), pl.BlockSpec((tm, D_MODEL), row),
                  pl.BlockSpec((D_MODEL, Z_PAD), const)],
        out_specs=[pl.BlockSpec((tm, D_MODEL), row), pl.BlockSpec(memory_space=pl.ANY)],
        out_shape=[jax.ShapeDtypeStruct((T, D_MODEL), f32), jax.ShapeDtypeStruct((D_MODEL, Z_PAD), f32)],
        scratch_shapes=[pltpu.VMEM((D_MODEL, Z_PAD), f32), pltpu.SemaphoreType.DMA],
        compiler_params=_params(1),
    )(dqa, dka, dva, dlxg, dfg, xn, dyp, wp)


def _split3(x):
    hi = x.astype(jnp.bfloat16)
    r1 = x - hi.astype(f32)
    mid = r1.astype(jnp.bfloat16)
    lo = (r1 - mid.astype(f32)).astype(jnp.bfloat16)
    return hi, mid, lo


def _tri_dot(tri, x):
    hi, mid, lo = _split3(x)
    return _dot(tri, hi) + _dot(tri, mid) + _dot(tri, lo)


FOX_PAD = HEADS * LANES
AUX = HEAD_DIM


def _low_lanes(shape):
    return lax.broadcasted_iota(jnp.int32, shape, 1) < HEAD_DIM


def _swap_lane_halves(x):
    return pltpu.roll(x, HEAD_DIM, 1)


def _fox_prep(qkv, fgb, *, name, tm=512):
    T = fgb.shape[0]
    tm = min(tm, T)
    nt = T // tm

    def body(qkv_ref, fg_ref, qa_ref, ka_ref, va_ref, carry):
        i = pl.program_id(0)

        @pl.when(i == 0)
        def _():
            carry[...] = jnp.zeros_like(carry)

        x = fg_ref[...]
        ls = jnp.minimum(x, 0.0) - jnp.log(1.0 + jnp.exp(-jnp.abs(x)))
        r = lax.broadcasted_iota(jnp.int32, (tm, tm), 0)
        c = lax.broadcasted_iota(jnp.int32, (tm, tm), 1)
        tri = jnp.where(r >= c, 1.0, 0.0).astype(jnp.bfloat16)
        cum = _tri_dot(tri, ls) + carry[0:1, :]
        carry[...] = jnp.broadcast_to(cum[tm - 1:tm, :], carry.shape)

        lane = lax.broadcasted_iota(jnp.int32, (tm, LANES), 1)
        low = lane < HEAD_DIM
        ones_q = jnp.where(jnp.logical_and(lane >= AUX + 3, lane < AUX + 6), 1.0, 0.0)
        ones_k = jnp.where(jnp.logical_and(lane >= AUX, lane < AUX + 3), 1.0, 0.0)
        for j in range(HEADS // 2):
            pair = [qkv_ref[:, t * FOX_W + j * LANES:t * FOX_W + (j + 1) * LANES].astype(f32) for t in range(3)]
            for odd in range(2):
                h = 2 * j + odd
                q, k, v = [_swap_lane_halves(a) if odd else a for a in pair]
                hi, mid, lo = [a.astype(f32) for a in _split3(jnp.broadcast_to(cum[:, h:h + 1], (tm, LANES)))]
                aux_q = jnp.where(lane == AUX, hi, jnp.where(lane == AUX + 1, mid, jnp.where(lane == AUX + 2, lo, ones_q)))
                aux_k = jnp.where(lane == AUX + 3, -hi,
                                  jnp.where(lane == AUX + 4, -mid, jnp.where(lane == AUX + 5, -lo, ones_k)))
                blk = slice(h * LANES, (h + 1) * LANES)
                qa_ref[:, blk] = jnp.where(low, q, aux_q).astype(qa_ref.dtype)
                ka_ref[:, blk] = jnp.where(low, k, aux_k).astype(ka_ref.dtype)
                va_ref[:, blk] = jnp.where(low, v, 1.0).astype(va_ref.dtype)

    row = lambda i: (i, 0)
    return _pcall(
        body, name=name, grid=(nt,),
        in_specs=[pl.BlockSpec((tm, QKV_W), row), pl.BlockSpec((tm, LANES), row)],
        out_specs=[pl.BlockSpec((tm, FOX_PAD), row)] * 3,
        out_shape=[jax.ShapeDtypeStruct((T, FOX_PAD), MXU_DTYPE)] * 3,
        scratch_shapes=[pltpu.VMEM((8, LANES), f32)],
        compiler_params=_params(1),
    )(qkv, fgb)


def _future_keys(tq, tk):
    r = lax.broadcasted_iota(jnp.int32, (tq, tk), 0)
    c = lax.broadcasted_iota(jnp.int32, (tq, tk), 1)
    return c > r


def _causal_steps(nq, key_major):
    if key_major:
        pairs = [(qi, ki) for ki in range(nq) for qi in range(ki, nq)]
    else:
        pairs = [(qi, ki) for qi in range(nq) for ki in range(qi + 1)]
    return (jnp.asarray([p[0] for p in pairs], jnp.int32), jnp.asarray([p[1] for p in pairs], jnp.int32))


def _fox_fwd(qa, ka, va, *, name, tq=512, hps=4):
    T = qa.shape[0]
    tq = min(tq, T)
    tk = tq
    nq = T // tq
    rep = tk // LANES
    qi_tab, ki_tab = _causal_steps(nq, key_major=False)

    def body(qi_ref, ki_ref, qa_ref, ka_ref, va_ref, o_ref, lse_ref, m_sc, acc_sc):
        t = pl.program_id(1)
        qi = qi_ref[t]
        ki = ki_ref[t]

        @pl.when(ki == 0)
        def _():
            m_sc[...] = jnp.full_like(m_sc, NEG_BIG)
            acc_sc[...] = jnp.zeros_like(acc_sc)

        def tile(diagonal):
            for h in range(hps):
                blk = slice(h * LANES, (h + 1) * LANES)
                s = _dot_nt(qa_ref[:, blk], ka_ref[:, blk])
                if diagonal:
                    s = jnp.where(_future_keys(tq, tk), NEG_BIG, s)
                m_prev = m_sc[h]
                m_new = jnp.maximum(m_prev, jnp.max(s, axis=1, keepdims=True))
                p = jnp.exp(s - jnp.tile(m_new, (1, rep)))
                acc_sc[h] = jnp.exp(m_prev - m_new) * acc_sc[h] + _dot(p.astype(MXU_DTYPE), va_ref[:, blk])
                m_sc[h] = m_new

        @pl.when(ki < qi)
        def _():
            tile(False)

        @pl.when(ki == qi)
        def _():
            tile(True)
            low = _low_lanes((tq, LANES))
            outs = []
            for h in range(hps):
                acc = acc_sc[h]
                den = _swap_lane_halves(acc)
                outs.append(acc / den)
                lse_ref[h] = m_sc[h] + jnp.log(jnp.where(low, den, acc))
            for p in range(hps // 2):
                o_ref[:, p * LANES:(p + 1) * LANES] = jnp.where(low, outs[2 * p], _swap_lane_halves(outs[2 * p + 1]))

    pair = hps * LANES
    return _pcall(
        body, name=name,
        grid_spec=pltpu.PrefetchScalarGridSpec(
            num_scalar_prefetch=2, grid=(HEADS // hps, qi_tab.shape[0]),
            in_specs=[
                pl.BlockSpec((tq, pair), lambda j, t, qi_ref, ki_ref: (qi_ref[t], j)),
                pl.BlockSpec((tk, pair), lambda j, t, qi_ref, ki_ref: (ki_ref[t], j)),
                pl.BlockSpec((tk, pair), lambda j, t, qi_ref, ki_ref: (ki_ref[t], j)),
            ],
            out_specs=[pl.BlockSpec((tq, pair // 2), lambda j, t, qi_ref, ki_ref: (qi_ref[t], j)),
                       pl.BlockSpec((hps, tq, LANES), lambda j, t, qi_ref, ki_ref: (j, qi_ref[t], 0))],
            scratch_shapes=[pltpu.VMEM((hps, tq, LANES), f32)] * 2),
        out_shape=[jax.ShapeDtypeStruct((T, FOX_W), f32), jax.ShapeDtypeStruct((HEADS, T, LANES), f32)],
        compiler_params=_params(2),
    )(qi_tab, ki_tab, qa, ka, va)


def _fox_bwd_prep(do, o, *, name, tm=512):
    T = o.shape[0]
    tm = min(tm, T)
    nt = T // tm

    def body(do_ref, o_ref, d_ref, doa_ref):
        low = _low_lanes((tm, LANES))
        for j in range(HEADS // 2):
            do2 = do_ref[:, j * LANES:(j + 1) * LANES].astype(f32)
            prod = do2 * o_ref[:, j * LANES:(j + 1) * LANES]
            for odd in range(2):
                h = 2 * j + odd
                mine = jnp.where(low, _swap_lane_halves(prod) if odd else prod, 0.0)
                d_ref[h] = jnp.broadcast_to(jnp.sum(mine, axis=1, keepdims=True), (tm, LANES))
                doh = jnp.where(low, _swap_lane_halves(do2) if odd else do2, 0.0)
                doa_ref[:, h * LANES:(h + 1) * LANES] = doh.astype(doa_ref.dtype)

    return _pcall(
        body, name=name, grid=(nt,),
        in_specs=[pl.BlockSpec((tm, FOX_W), lambda i: (i, 0)), pl.BlockSpec((tm, FOX_W), lambda i: (i, 0))],
        out_specs=[pl.BlockSpec((HEADS, tm, LANES), lambda i: (0, i, 0)), pl.BlockSpec((tm, FOX_PAD), lambda i: (i, 0))],
        out_shape=[jax.ShapeDtypeStruct((HEADS, T, LANES), f32), jax.ShapeDtypeStruct((T, FOX_PAD), MXU_DTYPE)],
        compiler_params=_params(1),
    )(do, o)


def _fox_bwd(qa, ka, va, doa, lse, drep, *, name, tq=512, hps=4):
    T = qa.shape[0]
    tq = min(tq, T)
    tk = tq
    nq = T // tq
    rep = tk // LANES
    qi_tab, ki_tab = _causal_steps(nq, key_major=True)

    def body(qi_ref, ki_ref, qa_ref, ka_ref, va_ref, doa_ref, lse_ref, d_ref, dqa_ref, dka_ref, dva_ref, dk_sc, dv_sc):
        t = pl.program_id(1)
        qi = qi_ref[t]
        ki = ki_ref[t]
        rows = pl.ds(pl.multiple_of(qi * tq, tq), tq)

        @pl.when(t == 0)
        def _():
            dqa_ref[...] = jnp.zeros_like(dqa_ref)

        @pl.when(qi == ki)
        def _():
            dk_sc[...] = jnp.zeros_like(dk_sc)
            dv_sc[...] = jnp.zeros_like(dv_sc)

        def tile(diagonal):
            for h in range(hps):
                blk = slice(h * LANES, (h + 1) * LANES)
                qh, kh, doh = qa_ref[:, blk], ka_ref[:, blk], doa_ref[:, blk]
                p = jnp.exp(_dot_nt(qh, kh) - jnp.tile(lse_ref[h], (1, rep)))
                if diagonal:
                    p = jnp.where(_future_keys(tq, tk), 0.0, p)
                dp = _dot_nt(doh, va_ref[:, blk])
                ds = (p * (dp - jnp.tile(d_ref[h], (1, rep)))).astype(MXU_DTYPE)
                dv_sc[h] += _dot_tn(p.astype(MXU_DTYPE), doh)
                dk_sc[h] += _dot_tn(ds, qh)
                dqa_ref[rows, blk] += _dot(ds, kh)

        @pl.when(qi > ki)
        def _():
            tile(False)

        @pl.when(qi == ki)
        def _():
            tile(True)

        @pl.when(qi == nq - 1)
        def _():
            for h in range(hps):
                blk = slice(h * LANES, (h + 1) * LANES)
                dka_ref[:, blk] = dk_sc[h]
                dva_ref[:, blk] = dv_sc[h]

    pair = hps * LANES
    q_blk = lambda j, t, qi_ref, ki_ref: (qi_ref[t], j)
    k_blk = lambda j, t, qi_ref, ki_ref: (ki_ref[t], j)
    stat = pl.BlockSpec((hps, tq, LANES), lambda j, t, qi_ref, ki_ref: (j, qi_ref[t], 0))
    return _pcall(
        body, name=name,
        grid_spec=pltpu.PrefetchScalarGridSpec(
            num_scalar_prefetch=2, grid=(HEADS // hps, qi_tab.shape[0]),
            in_specs=[pl.BlockSpec((tq, pair), q_blk), pl.BlockSpec((tk, pair), k_blk), pl.BlockSpec((tk, pair), k_blk),
                      pl.BlockSpec((tq, pair), q_blk), stat, stat],
            out_specs=[pl.BlockSpec((T, pair), lambda j, t, qi_ref, ki_ref: (0, j)),
                       pl.BlockSpec((tk, pair), k_blk), pl.BlockSpec((tk, pair), k_blk)],
            scratch_shapes=[pltpu.VMEM((hps, tk, LANES), f32)] * 2),
        out_shape=[jax.ShapeDtypeStruct((T, FOX_PAD), f32)] * 3,
        compiler_params=_params(2),
    )(qi_tab, ki_tab, qa, ka, va, doa, lse, drep)


def _fox_bwd_post(dqa, dka, fgb, *, name, tm=512):
    T = fgb.shape[0]
    tm = min(tm, T)
    nt = T // tm

    def body(dqa_ref, dka_ref, fg_ref, dfg_ref, dbf_ref, carry):
        i = pl.program_id(0)

        @pl.when(i == 0)
        def _():
            carry[...] = jnp.zeros_like(carry)
            dbf_ref[...] = jnp.zeros_like(dbf_ref)

        lane = lax.broadcasted_iota(jnp.int32, (tm, LANES), 1)
        dc = jnp.zeros((tm, LANES), f32)
        for h in range(HEADS):
            row_sum = dqa_ref[:, h * LANES + AUX:h * LANES + AUX + 1]
            col_sum = dka_ref[:, h * LANES + AUX + 3:h * LANES + AUX + 4]
            dc = jnp.where(lane == h, jnp.broadcast_to(row_sum - col_sum, (tm, LANES)), dc)
        r = lax.broadcasted_iota(jnp.int32, (tm, tm), 0)
        c = lax.broadcasted_iota(jnp.int32, (tm, tm), 1)
        tri = jnp.where(c >= r, 1.0, 0.0).astype(jnp.bfloat16)
        dls = _tri_dot(tri, dc) + carry[0:1, :]
        carry[...] = jnp.broadcast_to(dls[0:1, :], carry.shape)
        dfg = dls * _sigmoid(-fg_ref[...])
        dfg_ref[...] = dfg
        dbf_ref[...] += jnp.sum(dfg, axis=0, keepdims=True)

    rev = lambda i: (nt - 1 - i, 0)
    return _pcall(
        body, name=name, grid=(nt,),
        in_specs=[pl.BlockSpec((tm, FOX_PAD), rev), pl.BlockSpec((tm, FOX_PAD), rev), pl.BlockSpec((tm, LANES), rev)],
        out_specs=[pl.BlockSpec((tm, LANES), rev), pl.BlockSpec((1, LANES), lambda i: (0, 0))],
        out_shape=[jax.ShapeDtypeStruct((T, LANES), f32), jax.ShapeDtypeStruct((1, LANES), f32)],
        scratch_shapes=[pltpu.VMEM((8, LANES), f32)],
        compiler_params=_params(1),
    )(dqa, dka, fgb)


GELU_C = math.sqrt(2.0 / math.pi)
GELU_A = 0.044715


def _gelu(x):
    t = jnp.tanh(GELU_C * (x + GELU_A * x * x * x))
    return 0.5 * x * (1.0 + t), t


def _gelu_grad(x, t):
    return 0.5 * (1.0 + t) + 0.5 * x * (1.0 - t * t) * GELU_C * (1.0 + 3.0 * GELU_A * x * x)


def _expm1(x):
    e = jnp.exp(x)
    safe = jnp.where(e == 1.0, x, (e - 1.0) * x / jnp.log(jnp.where(e == 1.0, 0.5, e)))
    return jnp.where(x < -0.5, e - 1.0, safe)


def _lru_gates(u, wab_ref, bab_ref, lam_ref):
    pre = _dot(u.astype(MXU_DTYPE), wab_ref[...]) + bab_ref[...]
    r = _sigmoid(pre[:, :LRU_W])
    gi = _sigmoid(pre[:, LRU_W:])
    lam = lam_ref[...]
    sp = jnp.maximum(-lam, 0.0) + jnp.log(1.0 + jnp.exp(-jnp.abs(lam)))
    log_a = -LRU_C * r * sp
    a = jnp.exp(log_a)
    s = jnp.sqrt(-_expm1(2.0 * log_a))
    return r, gi, sp, a, s


def _lru_fwd(lxg, conv_w, conv_b, wab, bab, lam, *, name, tc=512):
    T = lxg.shape[0]
    tc = min(tc, T)
    nc = T // tc

    def body(lx_ref, lg_ref, cw_ref, cb_ref, wab_ref, bab_ref, lam_ref,
             out_ref, u_ref, hs_ref, ext, a_sc, b_sc, h_sc):
        i = pl.program_id(0)

        @pl.when(i == 0)
        def _():
            ext[0:8, :] = jnp.zeros((8, LRU_W), f32)
            h_sc[...] = jnp.zeros_like(h_sc)

        ext[8:, :] = lx_ref[...]
        u = cb_ref[...] + cw_ref[0:1, :] * ext[pl.ds(5, tc), :]
        for k in range(1, CONV_K):
            u = u + cw_ref[k:k + 1, :] * ext[pl.ds(5 + k, tc), :]
        ext[0:8, :] = ext[tc:tc + 8, :]
        u_ref[...] = u
        r, gi, sp, a, s = _lru_gates(u, wab_ref, bab_ref, lam_ref)
        a_sc[...] = a
        b_sc[...] = s * (gi * u)

        def step(t, h):
            h = a_sc[pl.ds(t, 1), :] * h + b_sc[pl.ds(t, 1), :]
            hs_ref[pl.ds(t, 1), :] = h
            return h

        h = lax.fori_loop(0, tc, step, h_sc[0:1, :], unroll=8)
        h_sc[...] = jnp.broadcast_to(h, h_sc.shape)
        gel, _ = _gelu(lg_ref[...])
        out_ref[...] = gel * hs_ref[...]

    row = lambda i: (i, 0)
    const = lambda i: (0, 0)
    return _pcall(
        body, name=name, grid=(nc,),
        in_specs=[pl.BlockSpec((tc, LRU_W), row), pl.BlockSpec((tc, LRU_W), lambda i: (i, 1)),
                  pl.BlockSpec((CONV_K, LRU_W), const), pl.BlockSpec((1, LRU_W), const),
                  pl.BlockSpec((LRU_W, 2 * LRU_W), const), pl.BlockSpec((1, 2 * LRU_W), const),
                  pl.BlockSpec((1, LRU_W), const)],
        out_specs=[pl.BlockSpec((tc, LRU_W), row)] * 3,
        out_shape=[jax.ShapeDtypeStruct((T, LRU_W), f32)] * 3,
        scratch_shapes=[pltpu.VMEM((tc + 8, LRU_W), f32), pltpu.VMEM((tc, LRU_W), f32),
                        pltpu.VMEM((tc, LRU_W), f32), pltpu.VMEM((8, LRU_W), f32)],
        compiler_params=_params(1),
    )(lxg, lxg, conv_w, conv_b, wab, bab, lam)


def _lru_bwd(dlru, lxg, u, hs, conv_w, wab, bab, lam, *, name, tc=512):
    T = lxg.shape[0]
    tc = min(tc, T)
    nc = T // tc
    bp = tc // 8

    def body(dl_ref, lx_ref, lxp_ref, lg_ref, u_ref, hs_ref, hsp_ref, cw_ref, wab_ref, bab_ref, lam_ref,
             dlxg_ref, dwab_ref, dbab_ref, dcw_ref, dcb_ref, dlam_ref,
             dh_sc, a_sc, ext, du_ext, carry):
        i = pl.program_id(0)
        first_chunk = i == nc - 1

        @pl.when(i == 0)
        def _():
            dwab_ref[...] = jnp.zeros_like(dwab_ref)
            dbab_ref[...] = jnp.zeros_like(dbab_ref)
            dcw_ref[...] = jnp.zeros_like(dcw_ref)
            dcb_ref[...] = jnp.zeros_like(dcb_ref)
            dlam_ref[...] = jnp.zeros_like(dlam_ref)
            carry[...] = jnp.zeros_like(carry)
            du_ext[tc:tc + 8, :] = jnp.zeros((8, LRU_W), f32)

        lg = lg_ref[...]
        gel, th = _gelu(lg)
        dl = dl_ref[...]
        hs = hs_ref[...]
        dlg = dl * hs * _gelu_grad(lg, th)
        u = u_ref[...]
        r, gi, sp, a, s = _lru_gates(u, wab_ref, bab_ref, lam_ref)
        a_sc[...] = a
        dh_sc[...] = dl * gel

        def step(k, c):
            t = tc - 1 - k
            dh = dh_sc[pl.ds(t, 1), :] + c
            dh_sc[pl.ds(t, 1), :] = dh
            return a_sc[pl.ds(t, 1), :] * dh

        c = lax.fori_loop(0, tc, step, carry[0:1, :], unroll=8)
        carry[...] = jnp.broadcast_to(c, carry.shape)

        ext[0:8, :] = jnp.where(first_chunk, 0.0, hsp_ref[...])
        ext[8:, :] = hs
        hprev = ext[pl.ds(7, tc), :]
        dh = dh_sc[...]
        da = dh * hprev
        giu = gi * u
        dla = da * a - (dh * giu) * (a * a / s)
        dgi = dh * s * u
        du = dh * s * gi
        dr = dla * (-LRU_C * sp)
        dlam_ref[...] += jnp.sum(dla * (-LRU_C * r), axis=0, keepdims=True) * (-_sigmoid(-lam_ref[...]))
        dpre = jnp.concatenate([dr * r * (1.0 - r), dgi * gi * (1.0 - gi)], axis=1)
        dpre_b = dpre.astype(MXU_DTYPE)
        du = du + _dot_nt(dpre_b, wab_ref[...])
        dwab_ref[...] += _dot_tn(u.astype(MXU_DTYPE), dpre_b)
        dbab_ref[...] += jnp.sum(dpre, axis=0, keepdims=True)
        dcb_ref[...] += jnp.sum(du, axis=0, keepdims=True)

        du_ext[0:tc, :] = du
        dlx = cw_ref[0:1, :] * du_ext[pl.ds(3, tc), :]
        for k in range(1, CONV_K):
            dlx = dlx + cw_ref[k:k + 1, :] * du_ext[pl.ds(3 - k, tc), :]
        du_ext[tc:tc + 8, :] = du_ext[0:8, :]
        ext[0:8, :] = jnp.where(first_chunk, 0.0, lxp_ref[...])
        ext[8:, :] = lx_ref[...]
        for k in range(CONV_K):
            dcw_ref[k:k + 1, :] += jnp.sum(du * ext[pl.ds(5 + k, tc), :], axis=0, keepdims=True)
        dlxg_ref[:, :LRU_W] = dlx.astype(dlxg_ref.dtype)
        dlxg_ref[:, LRU_W:] = dlg.astype(dlxg_ref.dtype)

    rev = lambda i: (nc - 1 - i, 0)
    prev8 = lambda i: (jnp.maximum((nc - 1 - i) * bp - 1, 0), 0)
    const = lambda i: (0, 0)
    return _pcall(
        body, name=name, grid=(nc,),
        in_specs=[
            pl.BlockSpec((tc, LRU_W), rev),
            pl.BlockSpec((tc, LRU_W), rev),
            pl.BlockSpec((8, LRU_W), prev8),
            pl.BlockSpec((tc, LRU_W), lambda i: (nc - 1 - i, 1)),
            pl.BlockSpec((tc, LRU_W), rev),
            pl.BlockSpec((tc, LRU_W), rev),
            pl.BlockSpec((8, LRU_W), prev8),
            pl.BlockSpec((CONV_K, LRU_W), const),
            pl.BlockSpec((LRU_W, 2 * LRU_W), const),
            pl.BlockSpec((1, 2 * LRU_W), const),
            pl.BlockSpec((1, LRU_W), const),
        ],
        out_specs=[
            pl.BlockSpec((tc, 2 * LRU_W), rev),
            pl.BlockSpec((LRU_W, 2 * LRU_W), const),
            pl.BlockSpec((1, 2 * LRU_W), const),
            pl.BlockSpec((8, LRU_W), const),
            pl.BlockSpec((1, LRU_W), const),
            pl.BlockSpec((1, LRU_W), const),
        ],
        out_shape=[
            jax.ShapeDtypeStruct((T, 2 * LRU_W), MXU_DTYPE),
            jax.ShapeDtypeStruct((LRU_W, 2 * LRU_W), f32),
            jax.ShapeDtypeStruct((1, 2 * LRU_W), f32),
            jax.ShapeDtypeStruct((8, LRU_W), f32),
            jax.ShapeDtypeStruct((1, LRU_W), f32),
            jax.ShapeDtypeStruct((1, LRU_W), f32),
        ],
        scratch_shapes=[pltpu.VMEM((tc, LRU_W), f32), pltpu.VMEM((tc, LRU_W), f32),
                        pltpu.VMEM((tc + 8, LRU_W), f32), pltpu.VMEM((tc + 8, LRU_W), f32),
                        pltpu.VMEM((8, LRU_W), f32)],
        compiler_params=_params(1),
    )(dlru, lxg, lxg, lxg, u, hs, hs, conv_w, wab, bab, lam)


def _mix_out(fox, lru, wo, xhat1, g1, b1, g2, b2, *, name, tm=512):
    T = fox.shape[0]
    tm = min(tm, T)
    nt = T // tm

    def body(fox_ref, lru_ref, wo_ref, xh_ref, g1_ref, b1_ref, g2_ref, b2_ref, xhat_ref, xn_ref, rstd_ref):
        mix = _dot(fox_ref[...].astype(MXU_DTYPE), wo_ref[:FOX_W, :])
        mix = mix + _dot(lru_ref[...].astype(MXU_DTYPE), wo_ref[FOX_W:, :])
        x1 = xh_ref[...] * g1_ref[...] + b1_ref[...]
        xhat, rstd = _layer_norm_stats(DN_ALPHA * x1 + mix)
        xhat_ref[...] = xhat
        xn_ref[...] = xhat * g2_ref[...] + b2_ref[...]
        rstd_ref[...] = jnp.broadcast_to(rstd, rstd_ref.shape)

    row = lambda i: (i, 0)
    const = lambda i: (0, 0)
    vec = pl.BlockSpec((1, D_MODEL), const)
    return _pcall(
        body, name=name, grid=(nt,),
        in_specs=[pl.BlockSpec((tm, FOX_W), row), pl.BlockSpec((tm, LRU_W), row),
                  pl.BlockSpec((D_MODEL, D_MODEL), const), pl.BlockSpec((tm, D_MODEL), row), vec, vec, vec, vec],
        out_specs=[pl.BlockSpec((tm, D_MODEL), row), pl.BlockSpec((tm, D_MODEL), row),
                   pl.BlockSpec((tm, LANES), row)],
        out_shape=[jax.ShapeDtypeStruct((T, D_MODEL), f32), jax.ShapeDtypeStruct((T, D_MODEL), f32),
                   jax.ShapeDtypeStruct((T, LANES), f32)],
        compiler_params=_params(1),
    )(fox, lru, wo, xhat1, g1, b1, g2, b2)


def _mix_out_bwd(dyp, fox, lru, wo, *, name, tm=512):
    T = fox.shape[0]
    tm = min(tm, T)
    nt = T // tm

    def body(dyp_ref, fox_ref, lru_ref, wo_ref, dfox_ref, dlru_ref, dwo_ref):
        i = pl.program_id(0)

        @pl.when(i == 0)
        def _():
            dwo_ref[...] = jnp.zeros_like(dwo_ref)

        dmix = dyp_ref[...].astype(MXU_DTYPE)
        dcat = _dot_nt(dmix, wo_ref[...])
        dfox_ref[...] = dcat[:, :FOX_W].astype(dfox_ref.dtype)
        dlru_ref[...] = dcat[:, FOX_W:]
        dwo_ref[:FOX_W, :] += _dot_tn(fox_ref[...].astype(MXU_DTYPE), dmix)
        dwo_ref[FOX_W:, :] += _dot_tn(lru_ref[...].astype(MXU_DTYPE), dmix)

    row = lambda i: (i, 0)
    const = lambda i: (0, 0)
    return _pcall(
        body, name=name, grid=(nt,),
        in_specs=[pl.BlockSpec((tm, D_MODEL), row), pl.BlockSpec((tm, FOX_W), row), pl.BlockSpec((tm, LRU_W), row),
                  pl.BlockSpec((D_MODEL, D_MODEL), const)],
        out_specs=[pl.BlockSpec((tm, FOX_W), row), pl.BlockSpec((tm, LRU_W), row),
                   pl.BlockSpec((D_MODEL, D_MODEL), const)],
        out_shape=[jax.ShapeDtypeStruct((T, FOX_W), MXU_DTYPE), jax.ShapeDtypeStruct((T, LRU_W), f32),
                   jax.ShapeDtypeStruct((D_MODEL, D_MODEL), f32)],
        compiler_params=_params(1),
    )(dyp, fox, lru, wo)


def make_wp(w_in):
    scale = jnp.concatenate([jnp.full((FOX_W,), 1.0 / math.sqrt(HEAD_DIM), w_in.dtype),
                             jnp.ones((IN_COLS - FOX_W,), w_in.dtype)])
    return jnp.pad(w_in * scale[None, :], ((0, 0), (0, Z_PAD - IN_COLS)))


def _block_diag(w):
    eye = jnp.eye(HEADS, dtype=w.dtype)
    return jnp.einsum("hij,hg->higj", w, eye).reshape(LRU_W, LRU_W)


def _block_diag_extract(m):
    m4 = m.reshape(HEADS, HEAD_DIM, HEADS, HEAD_DIM)
    return jnp.stack([m4[h, :, h, :] for h in range(HEADS)])


class _NoOverlap:
    def start_token(self):
        return None

    def late_weights(self, w, after):
        return dict(f1d=w["f1d"], wp=w["wp"], wo=w["wo"])

    def after_attention(self, after):
        return None

    def ffn2_weights(self, w, after):
        return w["f2g"], w["f2u"], w["f2d"]

    def ffn2_grads(self, grads):
        return None

    def mixer_grads(self, dwp, dwo, small):
        return None

    def before_ffn1_bwd(self, after):
        return None


def _tied(a, token):
    return a if token is None else a + token[0, 0]


def _local_step(x, target, w, hooks=None):
    hooks = hooks or _NoOverlap()
    bfp = w["bfp"]
    wab = jnp.concatenate([_block_diag(w["rg_wa"]), _block_diag(w["rg_wx"])], axis=1).astype(MXU_DTYPE)
    bab = jnp.concatenate([w["rg_ba"].reshape(1, LRU_W), w["rg_bx"].reshape(1, LRU_W)], axis=1)

    xb0, g1a, u1a, h1a = _ffn_up(x, w["f1g"], w["f1u"], hooks.start_token(), name="ffn1_up")
    late = hooks.late_weights(w, [h1a])
    f1d, wp, wo = late["f1d"], late["wp"], late["wo"]
    xhat1, xn1, rstd1 = _ffn_down_ln(x, h1a, f1d, w["ln1_g"], w["ln1_b"], name="ffn1_down")
    qkv, lxg, fgb = _proj_in(xn1, wp, bfp, name="proj_in")
    qa, ka, va = _fox_prep(qkv, fgb, name="fox_prep")
    fox, lse = _fox_fwd(qa, ka, va, name="fox_fwd")
    token = hooks.after_attention([lse])
    lru, uconv, hs = _lru_fwd(lxg, w["conv_w"], _tied(w["conv_b"], token), wab, bab, w["lam"], name="lru_fwd")
    xhat2, x2, rstd2 = _mix_out(fox, lru, wo, xhat1, w["ln1_g"], w["ln1_b"], w["ln2_g"], w["ln2_b"], name="mix_out")
    f2g, f2u, f2d = hooks.ffn2_weights(w, [rstd2])
    xb2, g2a, u2a, xhat3, _, rstd3 = _ffn_fwd(x2, f2g, f2u, f2d, w["ln3_g"], w["ln3_b"], name="ffn2_fwd")

    dy3p, dln3g, dln3b, loss = _loss_ln_bwd(xhat3, rstd3, w["ln3_g"], w["ln3_b"], target, name="loss_ln3_bwd")
    dx2, df2g, df2u, df2d = _ffn_bwd(dy3p, xb2, g2a, u2a, f2g, f2u, f2d, name="ffn2_bwd")
    token = hooks.ffn2_grads([df2g, df2u, df2d])
    dy2p, dln2g, dln2b = _ln_bwd(dx2, xhat2, rstd2, _tied(w["ln2_g"], token), name="ln2_bwd")
    dfox, dlru, dwo = _mix_out_bwd(dy2p, fox, lru, wo, name="mix_out_bwd")
    dlxg, dwab, dbab, dcw, dcb, dlam = _lru_bwd(dlru, lxg, uconv, hs, w["conv_w"], wab, bab, w["lam"], name="lru_bwd")
    drep, doa = _fox_bwd_prep(dfox, fox, name="fox_bwd_prep")
    dqa, dka, dva = _fox_bwd(qa, ka, va, doa, lse, drep, name="fox_bwd")
    dfg, dbf = _fox_bwd_post(dqa, dka, fgb, name="fox_bwd_post")
    dx1, dwp = _proj_in_bwd(dqa, dka, dva, dlxg, dfg, xn1, dy2p, wp, name="proj_in_bwd")
    dy1p, dln1g, dln1b = _ln_bwd(dx1, xhat1, rstd1, w["ln1_g"], name="ln1_bwd")
    small = dict(
        ln1_g=dln1g, ln1_b=dln1b, ln2_g=dln2g, ln2_b=dln2b, ln3_g=dln3g, ln3_b=dln3b,
        b_forget=dbf[:, :HEADS], conv_w=dcw[:CONV_K], conv_b=dcb,
        rg_wa=_block_diag_extract(dwab[:, :LRU_W]), rg_wx=_block_diag_extract(dwab[:, LRU_W:]),
        rg_ba=dbab[:, :LRU_W].reshape(HEADS, HEAD_DIM), rg_bx=dbab[:, LRU_W:].reshape(HEADS, HEAD_DIM),
        lru_lambda=dlam,
    )
    hooks.before_ffn1_bwd([dln1b])
    token = hooks.mixer_grads(dwp, dwo, small)
    dx, df1g, df1u, df1d = _ffn_bwd(dy1p, xb0, g1a, u1a, w["f1g"], w["f1u"], f1d, token, name="ffn1_bwd")

    grads = dict(f1g=df1g, f1u=df1u, f1d=df1d, f2g=df2g, f2u=df2u, f2d=df2d, wp=dwp, wo=dwo, **small)
    return loss, dx, grads


MESH = pl.DeviceIdType.MESH
HBM_SPEC = pl.BlockSpec(memory_space=pl.ANY)
VMEM_SPEC = pl.BlockSpec(memory_space=pltpu.VMEM)


def _position():
    return lax.axis_index("x"), lax.axis_index("y"), lax.axis_index("c")


def _other_chips(x, y):
    return [(1 - x, y), (x, 1 - y), (1 - x, 1 - y)]


def _all_gather_bf16(shards, *, name):
    n = len(shards)

    def body(*refs):
        ins, outs, stages = refs[:n], refs[n:2 * n], refs[2 * n:3 * n]
        send_sems, recv_sems, local_sems = refs[3 * n:]
        x, y, c = _position()
        me, sibling = (x, y, c), (x, y, 1 - c)
        chips = _other_chips(x, y)

        def rows(k, px, py, pc):
            r = shards[k].shape[0]
            m = r // 2
            return outs[k].at[pl.ds(pl.multiple_of((2 * px + py) * r + pc * m, 16), m), :]

        def copy(k, idx, block, to, src=None):
            return pltpu.make_async_remote_copy(
                src_ref=rows(k, *block) if src is None else src, dst_ref=rows(k, *block),
                send_sem=send_sems.at[7 * k + idx], recv_sem=recv_sems.at[7 * k + idx],
                device_id=to, device_id_type=MESH)

        started = []
        mine = []
        for k in range(n):
            m = shards[k].shape[0] // 2
            stages[k][...] = ins[k][pl.ds(pl.multiple_of(c * m, 16), m), :].astype(stages[k].dtype)
            cp = pltpu.make_async_copy(stages[k], rows(k, *me), local_sems.at[k])
            cp.start()
            mine.append(cp)
            first = [copy(k, 0, me, sibling, src=stages[k])]
            first += [copy(k, 1 + j, me, (*chip, c), src=stages[k]) for j, chip in enumerate(chips)]
            for cp in first:
                cp.start()
            started += first
        for k in range(n):
            for j, chip in enumerate(chips):
                copy(k, 1 + j, (*chip, c), me).wait_recv()
                fwd = copy(k, 4 + j, (*chip, c), sibling)
                fwd.start()
                started.append(fwd)
        for k in range(n):
            copy(k, 0, sibling, me).wait_recv()
            for j, chip in enumerate(chips):
                copy(k, 4 + j, (*chip, 1 - c), me).wait_recv()
        for cp in started:
            cp.wait_send()
        for cp in mine:
            cp.wait()

    return _pcall(
        body, name=name,
        in_specs=[VMEM_SPEC] * n, out_specs=[HBM_SPEC] * n,
        out_shape=[jax.ShapeDtypeStruct((N_SHARD * s.shape[0], s.shape[1]), MXU_DTYPE) for s in shards],
        scratch_shapes=[pltpu.VMEM((s.shape[0] // 2, s.shape[1]), MXU_DTYPE) for s in shards]
        + [pltpu.SemaphoreType.DMA((7 * n,)), pltpu.SemaphoreType.DMA((7 * n,)), pltpu.SemaphoreType.DMA((n,))],
        compiler_params=pltpu.CompilerParams(vmem_limit_bytes=VMEM_LIMIT),
    )(*shards)


def _swap_halves(gs, *, name):
    n = len(gs)

    def body(*refs):
        ins, outs = refs[:n], refs[n:2 * n]
        send_sems, recv_sems = refs[2 * n:]
        x, y, c = _position()
        cps = []
        for k in range(n):
            m = gs[k].shape[1] // 2
            src = ins[k].at[:, pl.ds(pl.multiple_of((1 - c) * m, 16), m), :]
            cp = pltpu.make_async_remote_copy(src_ref=src, dst_ref=outs[k], send_sem=send_sems.at[k],
                                              recv_sem=recv_sems.at[k], device_id=(x, y, 1 - c), device_id_type=MESH)
            cp.start()
            cps.append(cp)
        for cp in cps:
            cp.wait()

    return _pcall(
        body, name=name, in_specs=[HBM_SPEC] * n, out_specs=[HBM_SPEC] * n,
        out_shape=[jax.ShapeDtypeStruct((g.shape[0], g.shape[1] // 2, g.shape[2]), g.dtype) for g in gs],
        scratch_shapes=[pltpu.SemaphoreType.DMA((n,)), pltpu.SemaphoreType.DMA((n,))],
    )(*gs)


def _add_halves(gs, recvs, *, name, tm=256):
    n = len(gs)
    _, r, cdim = gs[0].shape
    m = r // 2
    tm = min(tm, m)
    nb = m // tm
    c_idx = lax.axis_index("c").astype(jnp.int32).reshape(1)

    def body(c_ref, *refs):
        for k in range(n):
            refs[2 * n + k][...] = (refs[k][...].astype(f32) + refs[n + k][...].astype(f32)).astype(refs[2 * n + k].dtype)

    mine = pl.BlockSpec((None, tm, cdim), lambda j, i, c_ref: (j, c_ref[0] * nb + i, 0))
    half = pl.BlockSpec((None, tm, cdim), lambda j, i, c_ref: (j, i, 0))
    return _pcall(
        body, name=name,
        grid_spec=pltpu.PrefetchScalarGridSpec(
            num_scalar_prefetch=1, grid=(N_SHARD, nb),
            in_specs=[mine] * n + [half] * n, out_specs=[half] * n),
        out_shape=[jax.ShapeDtypeStruct((N_SHARD, m, cdim), g.dtype) for g in gs],
        compiler_params=_params(2),
    )(c_idx, *gs, *recvs)


def _scatter_partials(ps, *, name):
    n = len(ps)

    def body(*refs):
        ins, outs = refs[:n], refs[n:2 * n]
        send_sems, recv_sems = refs[2 * n:]
        x, y, c = _position()
        me_chip = 2 * x + y
        cps = []
        for k in range(n):
            for j, (px, py) in enumerate(_other_chips(x, y)):
                cp = pltpu.make_async_remote_copy(
                    src_ref=ins[k].at[2 * px + py], dst_ref=outs[k].at[me_chip],
                    send_sem=send_sems.at[3 * k + j], recv_sem=recv_sems.at[3 * k + j],
                    device_id=(px, py, c), device_id_type=MESH)
                cp.start()
                cps.append(cp)
        for cp in cps:
            cp.wait()

    return _pcall(
        body, name=name, in_specs=[HBM_SPEC] * n, out_specs=[HBM_SPEC] * n,
        out_shape=[jax.ShapeDtypeStruct(p.shape, p.dtype) for p in ps],
        scratch_shapes=[pltpu.SemaphoreType.DMA((3 * n,)), pltpu.SemaphoreType.DMA((3 * n,))],
    )(*ps)


def _sum_slabs(ps, qs, *, name, tm=128):
    n = len(qs)
    _, m, cdim = qs[0].shape
    tm = min(tm, m)
    nb = m // tm
    assert m % tm == 0, (m, tm)
    where = jnp.stack([2 * lax.axis_index("x") + lax.axis_index("y"), lax.axis_index("c")]).astype(jnp.int32)

    def body(w_ref, *refs):
        for k in range(n):
            own, q1, q2, q3 = (refs[4 * k + t][...].astype(f32) for t in range(4))
            refs[4 * n + k][...] = ((own + q1) + q2) + q3

    def slab(flip):
        return pl.BlockSpec((None, tm, cdim), lambda i, w_ref: (jnp.bitwise_xor(w_ref[0], flip), i, 0))

    operands = []
    for p, q in zip(ps, qs):
        operands += [p, q, q, q]
    return _pcall(
        body, name=name,
        grid_spec=pltpu.PrefetchScalarGridSpec(
            num_scalar_prefetch=1, grid=(nb,),
            in_specs=[slab(0), slab(2), slab(1), slab(3)] * n,
            out_specs=[pl.BlockSpec((tm, cdim), lambda i, w_ref: (w_ref[1] * nb + i, 0))] * n),
        out_shape=[jax.ShapeDtypeStruct((2 * m, cdim), f32) for _ in qs],
        compiler_params=_params(1),
    )(where, *operands)


def _join_halves(fs, *, name):
    n = len(fs)

    def body(*refs):
        outs = refs[n:2 * n]
        send_sems, recv_sems = refs[2 * n:]
        x, y, c = _position()
        cps = []
        for k in range(n):
            m = fs[k].shape[0] // 2
            half = outs[k].at[pl.ds(pl.multiple_of(c * m, 8), m), :]
            cp = pltpu.make_async_remote_copy(src_ref=half, dst_ref=half, send_sem=send_sems.at[k],
                                              recv_sem=recv_sems.at[k], device_id=(x, y, 1 - c), device_id_type=MESH)
            cp.start()
            cps.append(cp)
        for cp in cps:
            cp.wait()

    return _pcall(
        body, name=name, in_specs=[HBM_SPEC] * n, out_specs=[HBM_SPEC] * n,
        out_shape=[jax.ShapeDtypeStruct(f.shape, f.dtype) for f in fs],
        input_output_aliases={k: k for k in range(n)},
        scratch_shapes=[pltpu.SemaphoreType.DMA((n,)), pltpu.SemaphoreType.DMA((n,))],
    )(*fs)


def _all_reduce_small(v, after=None, *, name):
    r = v.shape[0]
    extra = [] if after is None else [after]

    def body(v_ref, *refs):
        out_ref, buf, send_sems, recv_sems, local_sem = refs[len(extra):]
        x, y, c = _position()
        me, sibling = (x, y, c), (x, y, 1 - c)
        chips = _other_chips(x, y)

        def rows(px, py, pc):
            return buf.at[pl.ds(pl.multiple_of((4 * px + 2 * py + pc) * r, 8), r), :]

        def copy(k, block, to, src=None):
            return pltpu.make_async_remote_copy(
                src_ref=rows(*block) if src is None else src, dst_ref=rows(*block),
                send_sem=send_sems.at[k], recv_sem=recv_sems.at[k], device_id=to, device_id_type=MESH)

        mine = pltpu.make_async_copy(v_ref, rows(*me), local_sem)
        mine.start()
        first = [copy(0, me, sibling, src=v_ref)]
        first += [copy(1 + j, me, (*chip, c), src=v_ref) for j, chip in enumerate(chips)]
        for cp in first:
            cp.start()
        passed = [copy(4 + j, (*chip, c), sibling) for j, chip in enumerate(chips)]
        for j, chip in enumerate(chips):
            copy(1 + j, (*chip, c), me).wait_recv()
            passed[j].start()
        copy(0, sibling, me).wait_recv()
        for j, chip in enumerate(chips):
            copy(4 + j, (*chip, 1 - c), me).wait_recv()
        for cp in first + passed:
            cp.wait_send()
        mine.wait()
        acc = buf[0:r, :]
        for d in range(1, N_DEV):
            acc = acc + buf[d * r:(d + 1) * r, :]
        out_ref[...] = acc

    return _pcall(
        body, name=name, in_specs=[VMEM_SPEC] + [HBM_SPEC] * len(extra), out_specs=VMEM_SPEC,
        out_shape=jax.ShapeDtypeStruct((r, LANES), f32),
        scratch_shapes=[pltpu.VMEM((N_DEV * r, LANES), f32), pltpu.SemaphoreType.DMA((7,)),
                        pltpu.SemaphoreType.DMA((7,)), pltpu.SemaphoreType.DMA],
    )(v, *extra)


SEM_SPEC = pl.BlockSpec(memory_space=pltpu.SEMAPHORE)
HBM_ONLY = pl.BlockSpec(memory_space=pltpu.HBM)
EFFECT = pltpu.SideEffectType.DATAFLOW_SIDE_EFFECTING


def _split_start(bufs, copies_fn, n_sems, *, name):
    n = len(bufs)

    def body(*refs):
        send_sems, recv_sems = refs[n], refs[n + 1]
        thru = refs[n + 2:2 * n + 2]
        token = refs[2 * n + 2]
        for cp in copies_fn(thru, send_sems, recv_sems):
            cp.start()
        token[...] = jnp.zeros_like(token)

    outs = _pcall(
        body, name=name,
        out_shape=(pltpu.SemaphoreType.DMA((n_sems,)), pltpu.SemaphoreType.DMA((n_sems,)),
                   *[pltpu.HBM(b.shape, b.dtype) for b in bufs], jax.ShapeDtypeStruct((8, LANES), f32)),
        in_specs=[HBM_ONLY] * n,
        out_specs=(SEM_SPEC, SEM_SPEC, *[HBM_ONLY] * n, VMEM_SPEC),
        input_output_aliases={k: 2 + k for k in range(n)},
        compiler_params=pltpu.CompilerParams(has_side_effects=EFFECT),
    )(*[pltpu.with_memory_space_constraint(b, pltpu.HBM) for b in bufs])
    return outs[0], outs[1], list(outs[2:2 + n]), outs[2 + n]


def _split_wait(thru, send_sems, recv_sems, after, copies_fn, *, name):
    n = len(thru)

    def body(*refs):
        for cp in copies_fn(refs[:n], refs[n], refs[n + 1]):
            cp.wait_send()
            cp.wait_recv()

    return list(_pcall(
        body, name=name,
        out_shape=tuple(pltpu.HBM(b.shape, b.dtype) for b in thru),
        in_specs=[HBM_ONLY] * n + [SEM_SPEC, SEM_SPEC] + [HBM_SPEC] * len(after),
        out_specs=tuple([HBM_ONLY] * n),
        input_output_aliases={k: k for k in range(n)},
        compiler_params=pltpu.CompilerParams(has_side_effects=EFFECT),
    )(*thru, send_sems, recv_sems, *after))


def _scatter_copies(n):
    def copies(bufs, send_sems, recv_sems):
        x, y, c = _position()
        me_chip = 2 * x + y
        cps = []
        for k in range(n):
            for j, (px, py) in enumerate(_other_chips(x, y)):
                cps.append(pltpu.make_async_remote_copy(
                    src_ref=bufs[k].at[2 * px + py], dst_ref=bufs[n + k].at[me_chip],
                    send_sem=send_sems.at[3 * k + j], recv_sem=recv_sems.at[3 * k + j],
                    device_id=(px, py, c), device_id_type=MESH))
        return cps
    return copies


def _block_rows(buf, px, py, pc):
    m = buf.shape[0] // N_DEV
    return buf.at[pl.ds(pl.multiple_of((4 * px + 2 * py + pc) * m, 16), m), :]


def _gather_ici_copies(n):
    def copies(bufs, send_sems, recv_sems):
        x, y, c = _position()
        cps = []
        for k in range(n):
            rows = _block_rows(bufs[k], x, y, c)
            targets = [(x, y, 1 - c)] + [(px, py, c) for px, py in _other_chips(x, y)]
            for j, to in enumerate(targets):
                cps.append(pltpu.make_async_remote_copy(
                    src_ref=rows, dst_ref=rows, send_sem=send_sems.at[4 * k + j], recv_sem=recv_sems.at[4 * k + j],
                    device_id=to, device_id_type=MESH))
        return cps
    return copies


def _gather_d2d_copies(n):
    def copies(bufs, send_sems, recv_sems):
        x, y, c = _position()
        cps = []
        for k in range(n):
            for j, (px, py) in enumerate(_other_chips(x, y)):
                rows = _block_rows(bufs[k], px, py, c)
                cps.append(pltpu.make_async_remote_copy(
                    src_ref=rows, dst_ref=rows, send_sem=send_sems.at[3 * k + j], recv_sem=recv_sems.at[3 * k + j],
                    device_id=(x, y, 1 - c), device_id_type=MESH))
        return cps
    return copies


def _cast_halves(shards, after, *, name):
    n = len(shards)
    where = jnp.stack([2 * lax.axis_index("x") + lax.axis_index("y"), lax.axis_index("c")]).astype(jnp.int32)

    def body(w_ref, *refs):
        for k in range(n):
            refs[n + 1 + k][...] = refs[k][...].astype(refs[n + 1 + k].dtype)

    def half(s):
        return (s.shape[0] // 2, s.shape[1])

    return _pcall(
        body, name=name,
        grid_spec=pltpu.PrefetchScalarGridSpec(
            num_scalar_prefetch=1, grid=(1,),
            in_specs=[pl.BlockSpec(half(s), lambda i, w_ref: (w_ref[1], 0)) for s in shards] + [HBM_SPEC],
            out_specs=[pl.BlockSpec(half(s), lambda i, w_ref: (2 * w_ref[0] + w_ref[1], 0)) for s in shards]),
        out_shape=[jax.ShapeDtypeStruct((N_SHARD * s.shape[0], s.shape[1]), MXU_DTYPE) for s in shards],
        compiler_params=_params(1),
    )(where, *shards, after)


class _SplitGather:
    def __init__(self, shards, after, tag):
        self.tag = tag
        self.n = len(shards)
        halves = _cast_halves(shards, after, name=f"{tag}_cast")
        self.ici = _split_start(halves, _gather_ici_copies(self.n), 4 * self.n, name=f"{tag}_ici_start")
        self.token = self.ici[3]

    def forward(self, after):
        send_sems, recv_sems, thru, _ = self.ici
        landed = _split_wait(thru, send_sems, recv_sems, after, _gather_ici_copies(self.n), name=f"{self.tag}_ici_wait")
        self.d2d = _split_start(landed, _gather_d2d_copies(self.n), 3 * self.n, name=f"{self.tag}_d2d_start")
        return self.d2d[3]

    def finish(self, after):
        send_sems, recv_sems, thru, _ = self.d2d
        return _split_wait(thru, send_sems, recv_sems, after, _gather_d2d_copies(self.n), name=f"{self.tag}_d2d_wait")


class _Overlap(_NoOverlap):
    def __init__(self, late_shards, ffn2_shards, after):
        self.late = _SplitGather(late_shards, after, "ag1")
        self.ffn2 = _SplitGather(ffn2_shards, self.late.token, "ag2")
        self.reduced = None

    def start_token(self):
        return self.ffn2.token

    def late_weights(self, w, after):
        token = self.late.forward(after)
        f1d, w_in, wo = self.late.finish([token])
        w_in = w_in.reshape(N_SHARD, D_MODEL, IN_SHARD).transpose(1, 0, 2).reshape(D_MODEL, IN_COLS)
        return dict(f1d=f1d.reshape(N_SHARD, D_FF // N_SHARD, D_MODEL), wp=make_wp(w_in), wo=wo)

    def after_attention(self, after):
        return self.ffn2.forward(after)

    def ffn2_weights(self, w, after):
        full = self.ffn2.finish(after)
        fs = D_FF // N_SHARD
        return (full[0].reshape(N_SHARD, D_MODEL, fs), full[1].reshape(N_SHARD, D_MODEL, fs),
                full[2].reshape(N_SHARD, fs, D_MODEL))

    def ffn2_grads(self, grads):
        recvs = _swap_halves(grads, name="rs_swap_ffn2")
        ps = _add_halves(grads, recvs, name="rs_add_ffn2")
        lands = [lax.empty(p.shape, p.dtype) for p in ps]
        self.scatter = _split_start(list(ps) + lands, _scatter_copies(len(ps)), 3 * len(ps), name="rs_scatter_ffn2_start")
        return self.scatter[3]

    def mixer_grads(self, dwp, dwo, small):
        self.small_sum = _all_reduce_small(_pack_small(small), name="ar_small")
        gwin = dwp[:, :IN_COLS].reshape(D_MODEL, N_SHARD, IN_SHARD).transpose(1, 0, 2).astype(GRAD_DTYPE)
        gwo = dwo.reshape(N_SHARD, D_MODEL // N_SHARD, D_MODEL).astype(GRAD_DTYPE)
        recvs = _swap_halves([gwin, gwo], name="rs_swap_mix")
        ps = [_add_halves([g], [r], name=f"rs_add_{tag}")[0] for g, r, tag in zip([gwin, gwo], recvs, ["w_in", "w_out"])]
        lands = [lax.empty(p.shape, p.dtype) for p in ps]
        self.scatter_mix = _split_start(ps + lands, _scatter_copies(2), 6, name="rs_scatter_mix_start")
        return self.scatter_mix[3]

    def mixer_reduced(self, after):
        send_sems, recv_sems, thru, _ = self.scatter_mix
        done = _split_wait(thru, send_sems, recv_sems, after, _scatter_copies(2), name="rs_scatter_mix_wait")
        return [_sum_slabs([done[k]], [done[2 + k]], name=f"rs_sum_{tag}")[0] for k, tag in enumerate(["w_in", "w_out"])]

    def before_ffn1_bwd(self, after):
        send_sems, recv_sems, thru, _ = self.scatter
        n = len(thru) // 2
        done = _split_wait(thru, send_sems, recv_sems, after, _scatter_copies(n), name="rs_scatter_ffn2_wait")
        self.reduced = list(_sum_slabs(done[:n], done[n:], name="rs_sum_ffn2"))


def _adamw(gs, ws, ms, vs, *, name, tm=256):
    n = len(gs)
    r, cdim = gs[0].shape
    tm = r if tm is None else min(tm, r)
    assert r % tm == 0, (r, tm)
    c1 = 1.0 / (1.0 - ADAM_B1 ** ADAM_STEP)
    c2 = 1.0 / (1.0 - ADAM_B2 ** ADAM_STEP)

    def body(*refs):
        for k in range(n):
            g = refs[k][...]
            w = refs[n + k][...]
            m = ADAM_B1 * refs[2 * n + k][...] + (1.0 - ADAM_B1) * g
            v = ADAM_B2 * refs[3 * n + k][...] + (1.0 - ADAM_B2) * (g * g)
            refs[4 * n + k][...] = -ADAM_LR * ((m * c1) / (jnp.sqrt(v * c2) + ADAM_EPS) + ADAM_WD * w)
            refs[5 * n + k][...] = m
            refs[6 * n + k][...] = v

    spec = pl.BlockSpec((tm, cdim), lambda i: (i, 0))
    outs = _pcall(
        body, name=name, grid=(r // tm,), in_specs=[spec] * (4 * n), out_specs=[spec] * (3 * n),
        out_shape=[jax.ShapeDtypeStruct((r, cdim), f32)] * (3 * n),
        compiler_params=_params(1),
    )(*gs, *ws, *ms, *vs)
    return outs[:n], outs[n:2 * n], outs[2 * n:]


BIG = ["ffn1_w_gate", "ffn1_w_up", "ffn1_w_down", "ffn2_w_gate", "ffn2_w_up", "ffn2_w_down"]
SMALL = ["ln1_g", "ln1_b", "b_forget", "conv_w", "conv_b", "rg_wa", "rg_ba", "rg_wx", "rg_bx", "lru_lambda",
         "ln2_g", "ln2_b", "ln3_g", "ln3_b"]
WEIGHTS = ["ffn1_w_gate", "ffn1_w_up", "ffn1_w_down", "ln1_g", "ln1_b", "w_in", "b_forget", "conv_w", "conv_b",
           "rg_wa", "rg_ba", "rg_wx", "rg_bx", "lru_lambda", "w_out", "ln2_g", "ln2_b",
           "ffn2_w_gate", "ffn2_w_up", "ffn2_w_down", "ln3_g", "ln3_b"]


def _pack_small(parts):
    rows = []
    for n in SMALL:
        flat = parts[n].reshape(-1)
        pad = (-flat.shape[0]) % LANES
        rows.append(jnp.pad(flat, (0, pad)).reshape(-1, LANES))
    packed = jnp.concatenate(rows, axis=0)
    return jnp.pad(packed, ((0, (-packed.shape[0]) % 8), (0, 0)))


def _unpack_small(packed, shapes):
    out, r0 = {}, 0
    for n in SMALL:
        size = math.prod(shapes[n])
        nr = -(-size // LANES)
        out[n] = packed[r0:r0 + nr].reshape(-1)[:size].reshape(shapes[n])
        r0 += nr
    return out


def kernel(x, ffn1_w_gate, ffn1_w_up, ffn1_w_down, ln1_g, ln1_b, w_in, b_forget, conv_w, conv_b, rg_wa, rg_ba, rg_wx, rg_bx, lru_lambda, w_out, ln2_g, ln2_b, ffn2_w_gate, ffn2_w_up, ffn2_w_down, ln3_g, ln3_b, loss_target, m_ffn1_w_gate, m_ffn1_w_up, m_ffn1_w_down, m_ln1_g, m_ln1_b, m_w_in, m_b_forget, m_conv_w, m_conv_b, m_rg_wa, m_rg_ba, m_rg_wx, m_rg_bx, m_lru_lambda, m_w_out, m_ln2_g, m_ln2_b, m_ffn2_w_gate, m_ffn2_w_up, m_ffn2_w_down, m_ln3_g, m_ln3_b, v_ffn1_w_gate, v_ffn1_w_up, v_ffn1_w_down, v_ln1_g, v_ln1_b, v_w_in, v_b_forget, v_conv_w, v_conv_b, v_rg_wa, v_rg_ba, v_rg_wx, v_rg_bx, v_lru_lambda, v_w_out, v_ln2_g, v_ln2_b, v_ffn2_w_gate, v_ffn2_w_up, v_ffn2_w_down, v_ln3_g, v_ln3_b):
    args = dict(locals())
    w = {n: args[n] for n in WEIGHTS}
    mom = {n: args["m_" + n] for n in WEIGHTS}
    var = {n: args["v_" + n] for n in WEIGHTS}
    chip = 2 * lax.axis_index("x") + lax.axis_index("y")

    g1 = _all_gather_bf16([w[n][0] for n in BIG[:2]], name="ag_ffn1_up")
    fs = D_FF // N_SHARD
    full = dict(
        f1g=g1[0].reshape(N_SHARD, D_MODEL, fs), f1u=g1[1].reshape(N_SHARD, D_MODEL, fs),
        bfp=jnp.pad(b_forget, ((0, 0), (0, LANES - HEADS))),
        ln1_g=ln1_g, ln1_b=ln1_b, ln2_g=ln2_g, ln2_b=ln2_b, ln3_g=ln3_g, ln3_b=ln3_b,
        conv_b=conv_b, rg_wa=rg_wa[0], rg_wx=rg_wx[0], rg_ba=rg_ba[0], rg_bx=rg_bx[0], lam=lru_lambda,
    )
    cw_place = lax.dynamic_update_slice(jnp.zeros((8, LRU_W), f32), conv_w[0] * 0.5, (0, chip * (LRU_W // N_SHARD)))
    cw_full = _all_reduce_small(cw_place.reshape(-1, LANES), g1[0], name="ag_conv_w")
    full["conv_w"] = cw_full.reshape(8, LRU_W)[:CONV_K]

    hooks = _Overlap([w["ffn1_w_down"][0], w["w_in"][0], w["w_out"][0]], [w[n][0] for n in BIG[3:]], cw_full)
    loss_rep, dx, g = _local_step(x[0], loss_target[0], full, hooks)
    loss = lax.psum(loss_rep[0, 0], ("x", "y", "c"))

    gs1 = [g["f1g"], g["f1u"], g["f1d"]]
    ps1 = _add_halves(gs1, _swap_halves(gs1, name="rs_swap_ffn1"), name="rs_add_ffn1")
    lands = [lax.empty(p.shape, p.dtype) for p in ps1]
    send1, recv1, thru1, token1 = _split_start(list(ps1) + lands, _scatter_copies(3), 9, name="rs_scatter_ffn1_start")
    red = _join_halves(hooks.reduced + hooks.mixer_reduced([token1]), name="rs_join_rest")
    grads = dict(zip(BIG[3:] + ["w_in", "w_out"], red))

    small_shapes = {n: w[n].shape for n in SMALL}
    small_shapes["conv_w"] = (1, CONV_K, LRU_W)
    gs_red = _unpack_small(hooks.small_sum, small_shapes)
    gs_red["conv_w"] = lax.dynamic_slice(gs_red["conv_w"], (0, 0, chip * (LRU_W // N_SHARD)),
                                         (1, CONV_K, LRU_W // N_SHARD))
    grads.update(gs_red)

    delta, new_m, new_v = {}, {}, {}

    def adamw(names, name, **kw):
        d, nm, nv = _adamw([grads[n] for n in names], [w[n][0] for n in names], [mom[n][0] for n in names],
                           [var[n][0] for n in names], name=name, **kw)
        for i, n in enumerate(names):
            delta[n], new_m[n], new_v[n] = d[i], nm[i], nv[i]

    adamw(BIG[3:], "adamw_ffn2", tm=128)
    adamw(["w_in"], "adamw_w_in")
    adamw(["w_out"], "adamw_w_out")
    shard_shapes = {n: w[n].shape for n in SMALL}
    d, nm, nv = _adamw([_pack_small({n: grads[n] for n in SMALL})], [_pack_small({n: w[n] for n in SMALL})],
                       [_pack_small({n: mom[n] for n in SMALL})], [_pack_small({n: var[n] for n in SMALL})],
                       name="adamw_small", tm=None)
    for dst, packed in ((delta, d[0]), (new_m, nm[0]), (new_v, nv[0])):
        dst.update(_unpack_small(packed, shard_shapes))

    worked = [new_v["ffn2_w_down"], new_v["w_in"], new_v["w_out"], nv[0]]
    done1 = _split_wait(thru1, send1, recv1, worked, _scatter_copies(3), name="rs_scatter_ffn1_wait")
    red1 = _join_halves(list(_sum_slabs(done1[:3], done1[3:], name="rs_sum_ffn1")), name="rs_join_ffn1")
    grads.update(zip(BIG[:3], red1))
    adamw(BIG[:3], "adamw_ffn1", tm=128)

    def shaped(tree, n):
        return tree[n].reshape(w[n].shape)

    return (loss, dx[None], *[shaped(grads, n) for n in WEIGHTS], *[shaped(delta, n) for n in WEIGHTS],
            *[shaped(new_m, n) for n in WEIGHTS], *[shaped(new_v, n) for n in WEIGHTS])
```

```python
import functools
import math

import jax
import jax.numpy as jnp
from jax import lax
from jax.experimental import pallas as pl
from jax.experimental.pallas import tpu as pltpu

f32 = jnp.float32
MXU_DTYPE = jnp.bfloat16
GRAD_DTYPE = jnp.bfloat16

D_MODEL = 1024
D_FF = 4096
N_SHARD = 4
N_DEV = 8
FOX_W = 512
LRU_W = 512
HEADS = 8
HEAD_DIM = 64
CONV_K = 4
IN_COLS = 2568
IN_SHARD = IN_COLS // N_SHARD
QKV_W = 3 * FOX_W
Z_PAD = 2688
LANES = 128
LN_EPS = 1e-5
DN_ALPHA = 2.0 ** 0.25
LRU_C = 8.0
NEG_BIG = -1e30
VMEM_LIMIT = 56 * 1024 * 1024

ADAM_LR = 0.001
ADAM_B1 = 0.9
ADAM_B2 = 0.999
ADAM_EPS = 1e-08
ADAM_WD = 0.01
ADAM_STEP = 10


def _pcall(body, **kw):
    return pl.pallas_call(body, **kw)


def _params(n_grid, vmem=VMEM_LIMIT):
    return pltpu.CompilerParams(dimension_semantics=("arbitrary",) * n_grid, vmem_limit_bytes=vmem)


def _dot(a, b):
    return jnp.dot(a, b, preferred_element_type=f32)


def _dot_nt(a, b):
    return lax.dot_general(a, b, (((1,), (1,)), ((), ())), preferred_element_type=f32)


def _dot_tn(a, b):
    return lax.dot_general(a, b, (((0,), (0,)), ((), ())), preferred_element_type=f32)


def _sigmoid(x):
    return 1.0 / (1.0 + jnp.exp(-x))


def _layer_norm_stats(y):
    mu = jnp.mean(y, axis=-1, keepdims=True)
    yc = y - mu
    var = jnp.mean(yc * yc, axis=-1, keepdims=True)
    rstd = lax.rsqrt(var + LN_EPS)
    return yc * rstd, rstd


def _ln_backward(dy, xhat, rstd, gamma):
    dxhat = dy * gamma
    m1 = jnp.mean(dxhat, axis=-1, keepdims=True)
    m2 = jnp.mean(dxhat * xhat, axis=-1, keepdims=True)
    dyp = rstd * (dxhat - m1 - xhat * m2)
    return dyp, jnp.sum(dy * xhat, axis=0, keepdims=True), jnp.sum(dy, axis=0, keepdims=True)


def _ffn_fwd(x, wg, wu, wd, ln_g, ln_b, *, name, tm=1024, tf=512):
    T = x.shape[0]
    tm = min(tm, T)
    fs = D_FF // N_SHARD
    cpf = fs // tf
    nf = D_FF // tf
    nt = T // tm

    def body(x_ref, wg_ref, wu_ref, wd_ref, g_ref, b_ref,
             xb_ref, gact_ref, uact_ref, xhat_ref, xn_ref, rstd_ref, acc_ref):
        f = pl.program_id(1)

        @pl.when(f == 0)
        def _():
            xb_ref[...] = x_ref[...].astype(MXU_DTYPE)
            acc_ref[...] = jnp.zeros_like(acc_ref)

        xb = xb_ref[...]
        g = _dot(xb, wg_ref[...])
        u = _dot(xb, wu_ref[...])
        h = (g * _sigmoid(g)) * u
        gact_ref[...] = g.astype(gact_ref.dtype)
        uact_ref[...] = u.astype(uact_ref.dtype)
        acc_ref[...] += _dot(h.astype(MXU_DTYPE), wd_ref[...])

        @pl.when(f == nf - 1)
        def _():
            y = DN_ALPHA * x_ref[...] + 0.5 * acc_ref[...]
            xhat, rstd = _layer_norm_stats(y)
            xhat_ref[...] = xhat
            xn_ref[...] = (xhat * g_ref[...] + b_ref[...]).astype(xn_ref.dtype)
            rstd_ref[...] = jnp.broadcast_to(rstd, rstd_ref.shape)

    row = lambda i, f: (i, 0)
    return _pcall(
        body, name=name, grid=(nt, nf),
        in_specs=[
            pl.BlockSpec((tm, D_MODEL), row),
            pl.BlockSpec((None, D_MODEL, tf), lambda i, f: (f // cpf, 0, f % cpf)),
            pl.BlockSpec((None, D_MODEL, tf), lambda i, f: (f // cpf, 0, f % cpf)),
            pl.BlockSpec((None, tf, D_MODEL), lambda i, f: (f // cpf, f % cpf, 0)),
            pl.BlockSpec((1, D_MODEL), lambda i, f: (0, 0)),
            pl.BlockSpec((1, D_MODEL), lambda i, f: (0, 0)),
        ],
        out_specs=[
            pl.BlockSpec((tm, D_MODEL), row),
            pl.BlockSpec((tm, tf), lambda i, f: (i, f)),
            pl.BlockSpec((tm, tf), lambda i, f: (i, f)),
            pl.BlockSpec((tm, D_MODEL), row),
            pl.BlockSpec((tm, D_MODEL), row),
            pl.BlockSpec((tm, LANES), row),
        ],
        out_shape=[
            jax.ShapeDtypeStruct((T, D_MODEL), MXU_DTYPE),
            jax.ShapeDtypeStruct((T, D_FF), MXU_DTYPE),
            jax.ShapeDtypeStruct((T, D_FF), MXU_DTYPE),
            jax.ShapeDtypeStruct((T, D_MODEL), f32),
            jax.ShapeDtypeStruct((T, D_MODEL), MXU_DTYPE),
            jax.ShapeDtypeStruct((T, LANES), f32),
        ],
        scratch_shapes=[pltpu.VMEM((tm, D_MODEL), f32)],
        compiler_params=_params(2),
    )(x, wg, wu, wd, ln_g, ln_b)


def _ffn_up(x, wg, wu, after=None, *, name, tm=1024, tf=512):
    T = x.shape[0]
    tm = min(tm, T)
    cpf = (D_FF // N_SHARD) // tf
    nf = D_FF // tf
    extra = [] if after is None else [after]

    def body(x_ref, wg_ref, wu_ref, *refs):
        xb_ref, gact_ref, uact_ref, hact_ref = refs[len(extra):]

        @pl.when(pl.program_id(1) == 0)
        def _():
            xb_ref[...] = x_ref[...].astype(MXU_DTYPE)

        xb = xb_ref[...]
        g = _dot(xb, wg_ref[...])
        u = _dot(xb, wu_ref[...])
        gact_ref[...] = g.astype(gact_ref.dtype)
        uact_ref[...] = u.astype(uact_ref.dtype)
        hact_ref[...] = ((g * _sigmoid(g)) * u).astype(hact_ref.dtype)

    row = lambda i, f: (i, 0)
    tile = pl.BlockSpec((tm, tf), lambda i, f: (i, f))
    cols = pl.BlockSpec((None, D_MODEL, tf), lambda i, f: (f // cpf, 0, f % cpf))
    return _pcall(
        body, name=name, grid=(T // tm, nf),
        in_specs=[pl.BlockSpec((tm, D_MODEL), row), cols, cols] + [pl.BlockSpec(memory_space=pl.ANY)] * len(extra),
        out_specs=[pl.BlockSpec((tm, D_MODEL), row), tile, tile, tile],
        out_shape=[jax.ShapeDtypeStruct((T, D_MODEL), MXU_DTYPE)] + [jax.ShapeDtypeStruct((T, D_FF), MXU_DTYPE)] * 3,
        compiler_params=_params(2),
    )(x, wg, wu, *extra)


def _ffn_down_ln(x, hact, wd, ln_g, ln_b, *, name, tm=1024):
    T = x.shape[0]
    tm = min(tm, T)
    fs = D_FF // N_SHARD

    def body(x_ref, h_ref, wd_ref, g_ref, b_ref, xhat_ref, xn_ref, rstd_ref, acc_ref):
        k = pl.program_id(1)

        @pl.when(k == 0)
        def _():
            acc_ref[...] = jnp.zeros_like(acc_ref)

        acc_ref[...] += _dot(h_ref[...], wd_ref[...])

        @pl.when(k == N_SHARD - 1)
        def _():
            xhat, rstd = _layer_norm_stats(DN_ALPHA * x_ref[...] + 0.5 * acc_ref[...])
            xhat_ref[...] = xhat
            xn_ref[...] = (xhat * g_ref[...] + b_ref[...]).astype(xn_ref.dtype)
            rstd_ref[...] = jnp.broadcast_to(rstd, rstd_ref.shape)

    row = lambda i, k: (i, 0)
    vec = pl.BlockSpec((1, D_MODEL), lambda i, k: (0, 0))
    return _pcall(
        body, name=name, grid=(T // tm, N_SHARD),
        in_specs=[pl.BlockSpec((tm, D_MODEL), row), pl.BlockSpec((tm, fs), lambda i, k: (i, k)),
                  pl.BlockSpec((None, fs, D_MODEL), lambda i, k: (k, 0, 0)), vec, vec],
        out_specs=[pl.BlockSpec((tm, D_MODEL), row), pl.BlockSpec((tm, D_MODEL), row), pl.BlockSpec((tm, LANES), row)],
        out_shape=[jax.ShapeDtypeStruct((T, D_MODEL), f32), jax.ShapeDtypeStruct((T, D_MODEL), MXU_DTYPE),
                   jax.ShapeDtypeStruct((T, LANES), f32)],
        scratch_shapes=[pltpu.VMEM((tm, D_MODEL), f32)],
        compiler_params=_params(2),
    )(x, hact, wd, ln_g, ln_b)


def _ffn_bwd(dyp, xb, gact, uact, wg, wu, wd, after=None, *, name, tm=512, tf=512, part=None, dx_init=None):
    T = dyp.shape[0]
    tm = min(tm, T)
    fs = D_FF // N_SHARD
    cpf = fs // tf
    nt = T // tm
    nf = D_FF // tf if part is None else N_SHARD
    wf = fs if part is None else tf
    slab = (lambda f: f // cpf) if part is None else (lambda f: f)
    chunk = (lambda f: f % cpf) if part is None else (lambda f: part)
    extra = ([] if dx_init is None else [dx_init]) + ([] if after is None else [after])

    def body(dyp_ref, xb_ref, g_ref, u_ref, wg_ref, wu_ref, wd_ref, *refs):
        dx_hbm, dwg_ref, dwu_ref, dwd_ref, dx_sc, dwg_sc, dwu_sc, dwd_sc, sem = refs[len(extra):]
        f = pl.program_id(0)
        i = pl.program_id(1)
        rows = pl.ds(pl.multiple_of(i * tm, tm), tm)
        dyp_t = dyp_ref[...]
        dy = (0.5 * dyp_t).astype(MXU_DTYPE)

        @pl.when(i == 0)
        def _():
            dwg_sc[...] = jnp.zeros_like(dwg_sc)
            dwu_sc[...] = jnp.zeros_like(dwu_sc)
            dwd_sc[...] = jnp.zeros_like(dwd_sc)

        @pl.when(f == 0)
        def _():
            dx_sc[rows, :] = DN_ALPHA * dyp_t if dx_init is None else refs[0][...]

        g = g_ref[...].astype(f32)
        u = u_ref[...].astype(f32)
        sig = _sigmoid(g)
        silu = g * sig
        dh = _dot_nt(dy, wd_ref[...])
        dg = (dh * u * (sig * (1.0 + g * (1.0 - sig)))).astype(MXU_DTYPE)
        du = (dh * silu).astype(MXU_DTYPE)
        hb = (silu * u).astype(MXU_DTYPE)
        dx_sc[rows, :] += _dot_nt(dg, wg_ref[...]) + _dot_nt(du, wu_ref[...])
        xb_t = xb_ref[...]
        dwg_sc[...] += _dot_tn(xb_t, dg)
        dwu_sc[...] += _dot_tn(xb_t, du)
        dwd_sc[...] += _dot_tn(hb, dy)

        @pl.when(i == nt - 1)
        def _():
            dwg_ref[...] = dwg_sc[...].astype(dwg_ref.dtype)
            dwu_ref[...] = dwu_sc[...].astype(dwu_ref.dtype)
            dwd_ref[...] = dwd_sc[...].astype(dwd_ref.dtype)

        @pl.when(jnp.logical_and(f == nf - 1, i == nt - 1))
        def _():
            cp = pltpu.make_async_copy(dx_sc, dx_hbm, sem)
            cp.start()
            cp.wait()

    row = lambda f, i: (i, 0)
    return _pcall(
        body, name=name, grid=(nf, nt),
        in_specs=[
            pl.BlockSpec((tm, D_MODEL), row),
            pl.BlockSpec((tm, D_MODEL), row),
            pl.BlockSpec((tm, tf), lambda f, i: (i, slab(f) * cpf + chunk(f))),
            pl.BlockSpec((tm, tf), lambda f, i: (i, slab(f) * cpf + chunk(f))),
            pl.BlockSpec((None, D_MODEL, tf), lambda f, i: (slab(f), 0, chunk(f))),
            pl.BlockSpec((None, D_MODEL, tf), lambda f, i: (slab(f), 0, chunk(f))),
            pl.BlockSpec((None, tf, D_MODEL), lambda f, i: (slab(f), chunk(f), 0)),
        ] + ([] if dx_init is None else [pl.BlockSpec((tm, D_MODEL), row)])
        + ([] if after is None else [pl.BlockSpec(memory_space=pl.ANY)]),
        out_specs=[
            pl.BlockSpec(memory_space=pl.ANY),
            pl.BlockSpec((None, D_MODEL, tf), lambda f, i: (slab(f), 0, chunk(f) if part is None else 0)),
            pl.BlockSpec((None, D_MODEL, tf), lambda f, i: (slab(f), 0, chunk(f) if part is None else 0)),
            pl.BlockSpec((None, tf, D_MODEL), lambda f, i: (slab(f), chunk(f) if part is None else 0, 0)),
        ],
        out_shape=[
            jax.ShapeDtypeStruct((T, D_MODEL), f32),
            jax.ShapeDtypeStruct((N_SHARD, D_MODEL, wf), GRAD_DTYPE),
            jax.ShapeDtypeStruct((N_SHARD, D_MODEL, wf), GRAD_DTYPE),
            jax.ShapeDtypeStruct((N_SHARD, wf, D_MODEL), GRAD_DTYPE),
        ],
        scratch_shapes=[pltpu.VMEM((T, D_MODEL), f32), pltpu.VMEM((D_MODEL, tf), f32),
                        pltpu.VMEM((D_MODEL, tf), f32), pltpu.VMEM((tf, D_MODEL), f32),
                        pltpu.SemaphoreType.DMA],
        compiler_params=_params(2),
    )(dyp, xb, gact, uact, wg, wu, wd, *extra)


def _loss_ln_bwd(xhat, rstd, ln_g, ln_b, target, *, name, tm=512):
    T = xhat.shape[0]
    tm = min(tm, T)
    nt = T // tm

    def body(xhat_ref, rstd_ref, g_ref, b_ref, t_ref, dyp_ref, dg_ref, db_ref, loss_ref):
        i = pl.program_id(0)

        @pl.when(i == 0)
        def _():
            dg_ref[...] = jnp.zeros_like(dg_ref)
            db_ref[...] = jnp.zeros_like(db_ref)
            loss_ref[...] = jnp.zeros_like(loss_ref)

        xhat_t = xhat_ref[...]
        gamma = g_ref[...]
        err = xhat_t * gamma + b_ref[...] - t_ref[...]
        sq = jnp.sum(jnp.sum(err * err, axis=0, keepdims=True), axis=1, keepdims=True)
        loss_ref[...] += jnp.broadcast_to(sq * (0.5 / D_MODEL), loss_ref.shape)
        dy = err * (1.0 / D_MODEL)
        dyp, dgam, dbeta = _ln_backward(dy, xhat_t, rstd_ref[:, 0:1], gamma)
        dyp_ref[...] = dyp
        dg_ref[...] += dgam
        db_ref[...] += dbeta

    row = lambda i: (i, 0)
    const = lambda i: (0, 0)
    return _pcall(
        body, name=name, grid=(nt,),
        in_specs=[pl.BlockSpec((tm, D_MODEL), row), pl.BlockSpec((tm, LANES), row),
                  pl.BlockSpec((1, D_MODEL), const), pl.BlockSpec((1, D_MODEL), const),
                  pl.BlockSpec((tm, D_MODEL), row)],
        out_specs=[pl.BlockSpec((tm, D_MODEL), row), pl.BlockSpec((1, D_MODEL), const),
                   pl.BlockSpec((1, D_MODEL), const), pl.BlockSpec((1, LANES), const)],
        out_shape=[jax.ShapeDtypeStruct((T, D_MODEL), f32), jax.ShapeDtypeStruct((1, D_MODEL), f32),
                   jax.ShapeDtypeStruct((1, D_MODEL), f32), jax.ShapeDtypeStruct((1, LANES), f32)],
        compiler_params=_params(1),
    )(xhat, rstd, ln_g, ln_b, target)


def _ln_bwd(dy, xhat, rstd, ln_g, *, name, tm=512):
    T = xhat.shape[0]
    tm = min(tm, T)
    nt = T // tm

    def body(dy_ref, xhat_ref, rstd_ref, g_ref, dyp_ref, dg_ref, db_ref):
        i = pl.program_id(0)

        @pl.when(i == 0)
        def _():
            dg_ref[...] = jnp.zeros_like(dg_ref)
            db_ref[...] = jnp.zeros_like(db_ref)

        dyp, dgam, dbeta = _ln_backward(dy_ref[...], xhat_ref[...], rstd_ref[:, 0:1], g_ref[...])
        dyp_ref[...] = dyp
        dg_ref[...] += dgam
        db_ref[...] += dbeta

    row = lambda i: (i, 0)
    const = lambda i: (0, 0)
    return _pcall(
        body, name=name, grid=(nt,),
        in_specs=[pl.BlockSpec((tm, D_MODEL), row), pl.BlockSpec((tm, D_MODEL), row),
                  pl.BlockSpec((tm, LANES), row), pl.BlockSpec((1, D_MODEL), const)],
        out_specs=[pl.BlockSpec((tm, D_MODEL), row), pl.BlockSpec((1, D_MODEL), const),
                   pl.BlockSpec((1, D_MODEL), const)],
        out_shape=[jax.ShapeDtypeStruct((T, D_MODEL), f32), jax.ShapeDtypeStruct((1, D_MODEL), f32),
                   jax.ShapeDtypeStruct((1, D_MODEL), f32)],
        compiler_params=_params(1),
    )(dy, xhat, rstd, ln_g)


def _proj_in(xn, wp, bfp, *, name, tm=512):
    T = xn.shape[0]
    tm = min(tm, T)
    nt = T // tm

    def body(x_ref, w_ref, b_ref, qkv_ref, lxg_ref, fg_ref):
        z = _dot(x_ref[...], w_ref[...])
        qkv_ref[...] = z[:, :QKV_W].astype(qkv_ref.dtype)
        lxg_ref[...] = z[:, QKV_W:QKV_W + 2 * LRU_W]
        fg_ref[...] = z[:, QKV_W + 2 * LRU_W:] + b_ref[...]

    row = lambda i: (i, 0)
    const = lambda i: (0, 0)
    return _pcall(
        body, name=name, grid=(nt,),
        in_specs=[pl.BlockSpec((tm, D_MODEL), row), pl.BlockSpec((D_MODEL, Z_PAD), const),
                  pl.BlockSpec((1, LANES), const)],
        out_specs=[pl.BlockSpec((tm, QKV_W), row), pl.BlockSpec((tm, 2 * LRU_W), row),
                   pl.BlockSpec((tm, LANES), row)],
        out_shape=[jax.ShapeDtypeStruct((T, QKV_W), MXU_DTYPE), jax.ShapeDtypeStruct((T, 2 * LRU_W), f32),
                   jax.ShapeDtypeStruct((T, LANES), f32)],
        compiler_params=_params(1),
    )(xn, wp, bfp)


def _proj_in_bwd(dqa, dka, dva, dlxg, dfg, xn, dyp, wp, *, name, tm=512):
    T = xn.shape[0]
    tm = min(tm, T)
    nt = T // tm

    def body(dq_ref, dk_ref, dv_ref, dl_ref, dfg_ref, x_ref, dyp_ref, w_ref, dx_ref, dw_hbm, dw_sc, sem):
        i = pl.program_id(0)

        @pl.when(i == 0)
        def _():
            dw_sc[...] = jnp.zeros_like(dw_sc)

        low = _low_lanes((tm, LANES))

        def packed(ref):
            pairs = [jnp.where(low, ref[:, (2 * j) * LANES:(2 * j + 1) * LANES],
                               _swap_lane_halves(ref[:, (2 * j + 1) * LANES:(2 * j + 2) * LANES]))
                     for j in range(HEADS // 2)]
            return jnp.concatenate(pairs, axis=1).astype(MXU_DTYPE)

        dz = jnp.concatenate(
            [packed(dq_ref), packed(dk_ref), packed(dv_ref),
             dl_ref[...].astype(MXU_DTYPE), dfg_ref[...].astype(MXU_DTYPE)], axis=1)
        dx_ref[...] = DN_ALPHA * dyp_ref[...] + _dot_nt(dz, w_ref[...])
        dw_sc[...] += _dot_tn(x_ref[...], dz)

        @pl.when(i == nt - 1)
        def _():
            dw_sc[:, :FOX_W] = dw_sc[:, :FOX_W] * (1.0 / math.sqrt(HEAD_DIM))
            cp = pltpu.make_async_copy(dw_sc, dw_hbm, sem)
            cp.start()
            cp.wait()

    row = lambda i: (i, 0)
    const = lambda i: (0, 0)
    return _pcall(
        body, name=name, grid=(nt,),
        in_specs=[pl.BlockSpec((tm, HEADS * LANES), row), pl.BlockSpec((tm, HEADS * LANES), row),
                  pl.BlockSpec((tm, HEADS * LANES), row),
                  pl.BlockSpec((tm, 2 * LRU_W), row), pl.BlockSpec((tm, LANES), row),
                  pl.BlockSpec((tm, D_MODEL), row), pl.BlockSpec((tm, D_MODEL), row),
                  pl.BlockSpec((D_MODEL, Z_PAD), const)],
        out_specs=[pl.BlockSpec((tm, D_MODEL), row), pl.BlockSpec(memory_space=pl.ANY)],
        out_shape=[jax.ShapeDtypeStruct((T, D_MODEL), f32), jax.ShapeDtypeStruct((D_MODEL, Z_PAD), f32)],
        scratch_shapes=[pltpu.VMEM((D_MODEL, Z_PAD), f32), pltpu.SemaphoreType.DMA],
        compiler_params=_params(1),
    )(dqa, dka, dva, dlxg, dfg, xn, dyp, wp)


def _split3(x):
    hi = x.astype(jnp.bfloat16)
    r1 = x - hi.astype(f32)
    mid = r1.astype(jnp.bfloat16)
    lo = (r1 - mid.astype(f32)).astype(jnp.bfloat16)
    return hi, mid, lo


def _tri_dot(tri, x):
    hi, mid, lo = _split3(x)
    return _dot(tri, hi) + _dot(tri, mid) + _dot(tri, lo)


FOX_PAD = HEADS * LANES
AUX = HEAD_DIM


def _low_lanes(shape):
    return lax.broadcasted_iota(jnp.int32, shape, 1) < HEAD_DIM


def _swap_lane_halves(x):
    return pltpu.roll(x, HEAD_DIM, 1)


def _fox_prep(qkv, fgb, *, name, tm=512):
    T = fgb.shape[0]
    tm = min(tm, T)
    nt = T // tm

    def body(qkv_ref, fg_ref, qa_ref, ka_ref, va_ref, carry):
        i = pl.program_id(0)

        @pl.when(i == 0)
        def _():
            carry[...] = jnp.zeros_like(carry)

        x = fg_ref[...]
        ls = jnp.minimum(x, 0.0) - jnp.log(1.0 + jnp.exp(-jnp.abs(x)))
        r = lax.broadcasted_iota(jnp.int32, (tm, tm), 0)
        c = lax.broadcasted_iota(jnp.int32, (tm, tm), 1)
        tri = jnp.where(r >= c, 1.0, 0.0).astype(jnp.bfloat16)
        cum = _tri_dot(tri, ls) + carry[0:1, :]
        carry[...] = jnp.broadcast_to(cum[tm - 1:tm, :], carry.shape)

        lane = lax.broadcasted_iota(jnp.int32, (tm, LANES), 1)
        low = lane < HEAD_DIM
        ones_q = jnp.where(jnp.logical_and(lane >= AUX + 3, lane < AUX + 6), 1.0, 0.0)
        ones_k = jnp.where(jnp.logical_and(lane >= AUX, lane < AUX + 3), 1.0, 0.0)
        for j in range(HEADS // 2):
            pair = [qkv_ref[:, t * FOX_W + j * LANES:t * FOX_W + (j + 1) * LANES].astype(f32) for t in range(3)]
            for odd in range(2):
                h = 2 * j + odd
                q, k, v = [_swap_lane_halves(a) if odd else a for a in pair]
                hi, mid, lo = [a.astype(f32) for a in _split3(jnp.broadcast_to(cum[:, h:h + 1], (tm, LANES)))]
                aux_q = jnp.where(lane == AUX, hi, jnp.where(lane == AUX + 1, mid, jnp.where(lane == AUX + 2, lo, ones_q)))
                aux_k = jnp.where(lane == AUX + 3, -hi,
                                  jnp.where(lane == AUX + 4, -mid, jnp.where(lane == AUX + 5, -lo, ones_k)))
                blk = slice(h * LANES, (h + 1) * LANES)
                qa_ref[:, blk] = jnp.where(low, q, aux_q).astype(qa_ref.dtype)
                ka_ref[:, blk] = jnp.where(low, k, aux_k).astype(ka_ref.dtype)
                va_ref[:, blk] = jnp.where(low, v, 1.0).astype(va_ref.dtype)

    row = lambda i: (i, 0)
    return _pcall(
        body, name=name, grid=(nt,),
        in_specs=[pl.BlockSpec((tm, QKV_W), row), pl.BlockSpec((tm, LANES), row)],
        out_specs=[pl.BlockSpec((tm, FOX_PAD), row)] * 3,
        out_shape=[jax.ShapeDtypeStruct((T, FOX_PAD), MXU_DTYPE)] * 3,
        scratch_shapes=[pltpu.VMEM((8, LANES), f32)],
        compiler_params=_params(1),
    )(qkv, fgb)


def _future_keys(tq, tk):
    r = lax.broadcasted_iota(jnp.int32, (tq, tk), 0)
    c = lax.broadcasted_iota(jnp.int32, (tq, tk), 1)
    return c > r


def _causal_steps(nq, key_major):
    if key_major:
        pairs = [(qi, ki) for ki in range(nq) for qi in range(ki, nq)]
    else:
        pairs = [(qi, ki) for qi in range(nq) for ki in range(qi + 1)]
    return (jnp.asarray([p[0] for p in pairs], jnp.int32), jnp.asarray([p[1] for p in pairs], jnp.int32))


def _fox_fwd(qa, ka, va, *, name, tq=512, hps=4):
    T = qa.shape[0]
    tq = min(tq, T)
    tk = tq
    nq = T // tq
    rep = tk // LANES
    qi_tab, ki_tab = _causal_steps(nq, key_major=False)

    def body(qi_ref, ki_ref, qa_ref, ka_ref, va_ref, o_ref, lse_ref, m_sc, acc_sc):
        t = pl.program_id(1)
        qi = qi_ref[t]
        ki = ki_ref[t]

        @pl.when(ki == 0)
        def _():
            m_sc[...] = jnp.full_like(m_sc, NEG_BIG)
            acc_sc[...] = jnp.zeros_like(acc_sc)

        def tile(diagonal):
            for h in range(hps):
                blk = slice(h * LANES, (h + 1) * LANES)
                s = _dot_nt(qa_ref[:, blk], ka_ref[:, blk])
                if diagonal:
                    s = jnp.where(_future_keys(tq, tk), NEG_BIG, s)
                m_prev = m_sc[h]
                m_new = jnp.maximum(m_prev, jnp.max(s, axis=1, keepdims=True))
                p = jnp.exp(s - jnp.tile(m_new, (1, rep)))
                acc_sc[h] = jnp.exp(m_prev - m_new) * acc_sc[h] + _dot(p.astype(MXU_DTYPE), va_ref[:, blk])
                m_sc[h] = m_new

        @pl.when(ki < qi)
        def _():
            tile(False)

        @pl.when(ki == qi)
        def _():
            tile(True)
            low = _low_lanes((tq, LANES))
            outs = []
            for h in range(hps):
                acc = acc_sc[h]
                den = _swap_lane_halves(acc)
                outs.append(acc / den)
                lse_ref[h] = m_sc[h] + jnp.log(jnp.where(low, den, acc))
            for p in range(hps // 2):
                o_ref[:, p * LANES:(p + 1) * LANES] = jnp.where(low, outs[2 * p], _swap_lane_halves(outs[2 * p + 1]))

    pair = hps * LANES
    return _pcall(
        body, name=name,
        grid_spec=pltpu.PrefetchScalarGridSpec(
            num_scalar_prefetch=2, grid=(HEADS // hps, qi_tab.shape[0]),
            in_specs=[
                pl.BlockSpec((tq, pair), lambda j, t, qi_ref, ki_ref: (qi_ref[t], j)),
                pl.BlockSpec((tk, pair), lambda j, t, qi_ref, ki_ref: (ki_ref[t], j)),
                pl.BlockSpec((tk, pair), lambda j, t, qi_ref, ki_ref: (ki_ref[t], j)),
            ],
            out_specs=[pl.BlockSpec((tq, pair // 2), lambda j, t, qi_ref, ki_ref: (qi_ref[t], j)),
                       pl.BlockSpec((hps, tq, LANES), lambda j, t, qi_ref, ki_ref: (j, qi_ref[t], 0))],
            scratch_shapes=[pltpu.VMEM((hps, tq, LANES), f32)] * 2),
        out_shape=[jax.ShapeDtypeStruct((T, FOX_W), f32), jax.ShapeDtypeStruct((HEADS, T, LANES), f32)],
        compiler_params=_params(2),
    )(qi_tab, ki_tab, qa, ka, va)


def _fox_bwd_prep(do, o, *, name, tm=512):
    T = o.shape[0]
    tm = min(tm, T)
    nt = T // tm

    def body(do_ref, o_ref, d_ref, doa_ref):
        low = _low_lanes((tm, LANES))
        for j in range(HEADS // 2):
            do2 = do_ref[:, j * LANES:(j + 1) * LANES].astype(f32)
            prod = do2 * o_ref[:, j * LANES:(j + 1) * LANES]
            for odd in range(2):
                h = 2 * j + odd
                mine = jnp.where(low, _swap_lane_halves(prod) if odd else prod, 0.0)
                d_ref[h] = jnp.broadcast_to(jnp.sum(mine, axis=1, keepdims=True), (tm, LANES))
                doh = jnp.where(low, _swap_lane_halves(do2) if odd else do2, 0.0)
                doa_ref[:, h * LANES:(h + 1) * LANES] = doh.astype(doa_ref.dtype)

    return _pcall(
        body, name=name, grid=(nt,),
        in_specs=[pl.BlockSpec((tm, FOX_W), lambda i: (i, 0)), pl.BlockSpec((tm, FOX_W), lambda i: (i, 0))],
        out_specs=[pl.BlockSpec((HEADS, tm, LANES), lambda i: (0, i, 0)), pl.BlockSpec((tm, FOX_PAD), lambda i: (i, 0))],
        out_shape=[jax.ShapeDtypeStruct((HEADS, T, LANES), f32), jax.ShapeDtypeStruct((T, FOX_PAD), MXU_DTYPE)],
        compiler_params=_params(1),
    )(do, o)


def _fox_bwd(qa, ka, va, doa, lse, drep, *, name, tq=512, hps=4):
    T = qa.shape[0]
    tq = min(tq, T)
    tk = tq
    nq = T // tq
    rep = tk // LANES
    qi_tab, ki_tab = _causal_steps(nq, key_major=True)

    def body(qi_ref, ki_ref, qa_ref, ka_ref, va_ref, doa_ref, lse_ref, d_ref, dqa_ref, dka_ref, dva_ref, dk_sc, dv_sc):
        t = pl.program_id(1)
        qi = qi_ref[t]
        ki = ki_ref[t]
        rows = pl.ds(pl.multiple_of(qi * tq, tq), tq)

        @pl.when(t == 0)
        def _():
            dqa_ref[...] = jnp.zeros_like(dqa_ref)

        @pl.when(qi == ki)
        def _():
            dk_sc[...] = jnp.zeros_like(dk_sc)
            dv_sc[...] = jnp.zeros_like(dv_sc)

        def tile(diagonal):
            for h in range(hps):
                blk = slice(h * LANES, (h + 1) * LANES)
                qh, kh, doh = qa_ref[:, blk], ka_ref[:, blk], doa_ref[:, blk]
                p = jnp.exp(_dot_nt(qh, kh) - jnp.tile(lse_ref[h], (1, rep)))
                if diagonal:
                    p = jnp.where(_future_keys(tq, tk), 0.0, p)
                dp = _dot_nt(doh, va_ref[:, blk])
                ds = (p * (dp - jnp.tile(d_ref[h], (1, rep)))).astype(MXU_DTYPE)
                dv_sc[h] += _dot_tn(p.astype(MXU_DTYPE), doh)
                dk_sc[h] += _dot_tn(ds, qh)
                dqa_ref[rows, blk] += _dot(ds, kh)

        @pl.when(qi > ki)
        def _():
            tile(False)

        @pl.when(qi == ki)
        def _():
            tile(True)

        @pl.when(qi == nq - 1)
        def _():
            for h in range(hps):
                blk = slice(h * LANES, (h + 1) * LANES)
                dka_ref[:, blk] = dk_sc[h]
                dva_ref[:, blk] = dv_sc[h]

    pair = hps * LANES
    q_blk = lambda j, t, qi_ref, ki_ref: (qi_ref[t], j)
    k_blk = lambda j, t, qi_ref, ki_ref: (ki_ref[t], j)
    stat = pl.BlockSpec((hps, tq, LANES), lambda j, t, qi_ref, ki_ref: (j, qi_ref[t], 0))
    return _pcall(
        body, name=name,
        grid_spec=pltpu.PrefetchScalarGridSpec(
            num_scalar_prefetch=2, grid=(HEADS // hps, qi_tab.shape[0]),
            in_specs=[pl.BlockSpec((tq, pair), q_blk), pl.BlockSpec((tk, pair), k_blk), pl.BlockSpec((tk, pair), k_blk),
                      pl.BlockSpec((tq, pair), q_blk), stat, stat],
            out_specs=[pl.BlockSpec((T, pair), lambda j, t, qi_ref, ki_ref: (0, j)),
                       pl.BlockSpec((tk, pair), k_blk), pl.BlockSpec((tk, pair), k_blk)],
            scratch_shapes=[pltpu.VMEM((hps, tk, LANES), f32)] * 2),
        out_shape=[jax.ShapeDtypeStruct((T, FOX_PAD), f32)] * 3,
        compiler_params=_params(2),
    )(qi_tab, ki_tab, qa, ka, va, doa, lse, drep)


def _fox_bwd_post(dqa, dka, fgb, *, name, tm=512):
    T = fgb.shape[0]
    tm = min(tm, T)
    nt = T // tm

    def body(dqa_ref, dka_ref, fg_ref, dfg_ref, dbf_ref, carry):
        i = pl.program_id(0)

        @pl.when(i == 0)
        def _():
            carry[...] = jnp.zeros_like(carry)
            dbf_ref[...] = jnp.zeros_like(dbf_ref)

        lane = lax.broadcasted_iota(jnp.int32, (tm, LANES), 1)
        dc = jnp.zeros((tm, LANES), f32)
        for h in range(HEADS):
            row_sum = dqa_ref[:, h * LANES + AUX:h * LANES + AUX + 1]
            col_sum = dka_ref[:, h * LANES + AUX + 3:h * LANES + AUX + 4]
            dc = jnp.where(lane == h, jnp.broadcast_to(row_sum - col_sum, (tm, LANES)), dc)
        r = lax.broadcasted_iota(jnp.int32, (tm, tm), 0)
        c = lax.broadcasted_iota(jnp.int32, (tm, tm), 1)
        tri = jnp.where(c >= r, 1.0, 0.0).astype(jnp.bfloat16)
        dls = _tri_dot(tri, dc) + carry[0:1, :]
        carry[...] = jnp.broadcast_to(dls[0:1, :], carry.shape)
        dfg = dls * _sigmoid(-fg_ref[...])
        dfg_ref[...] = dfg
        dbf_ref[...] += jnp.sum(dfg, axis=0, keepdims=True)

    rev = lambda i: (nt - 1 - i, 0)
    return _pcall(
        body, name=name, grid=(nt,),
        in_specs=[pl.BlockSpec((tm, FOX_PAD), rev), pl.BlockSpec((tm, FOX_PAD), rev), pl.BlockSpec((tm, LANES), rev)],
        out_specs=[pl.BlockSpec((tm, LANES), rev), pl.BlockSpec((1, LANES), lambda i: (0, 0))],
        out_shape=[jax.ShapeDtypeStruct((T, LANES), f32), jax.ShapeDtypeStruct((1, LANES), f32)],
        scratch_shapes=[pltpu.VMEM((8, LANES), f32)],
        compiler_params=_params(1),
    )(dqa, dka, fgb)


GELU_C = math.sqrt(2.0 / math.pi)
GELU_A = 0.044715


def _gelu(x):
    t = jnp.tanh(GELU_C * (x + GELU_A * x * x * x))
    return 0.5 * x * (1.0 + t), t


def _gelu_grad(x, t):
    return 0.5 * (1.0 + t) + 0.5 * x * (1.0 - t * t) * GELU_C * (1.0 + 3.0 * GELU_A * x * x)


def _expm1(x):
    e = jnp.exp(x)
    safe = jnp.where(e == 1.0, x, (e - 1.0) * x / jnp.log(jnp.where(e == 1.0, 0.5, e)))
    return jnp.where(x < -0.5, e - 1.0, safe)


def _lru_gates(u, wab_ref, bab_ref, lam_ref):
    pre = _dot(u.astype(MXU_DTYPE), wab_ref[...]) + bab_ref[...]
    r = _sigmoid(pre[:, :LRU_W])
    gi = _sigmoid(pre[:, LRU_W:])
    lam = lam_ref[...]
    sp = jnp.maximum(-lam, 0.0) + jnp.log(1.0 + jnp.exp(-jnp.abs(lam)))
    log_a = -LRU_C * r * sp
    a = jnp.exp(log_a)
    s = jnp.sqrt(-_expm1(2.0 * log_a))
    return r, gi, sp, a, s


def _lru_fwd(lxg, conv_w, conv_b, wab, bab, lam, *, name, tc=512):
    T = lxg.shape[0]
    tc = min(tc, T)
    nc = T // tc

    def body(lx_ref, lg_ref, cw_ref, cb_ref, wab_ref, bab_ref, lam_ref,
             out_ref, u_ref, hs_ref, ext, a_sc, b_sc, h_sc):
        i = pl.program_id(0)

        @pl.when(i == 0)
        def _():
            ext[0:8, :] = jnp.zeros((8, LRU_W), f32)
            h_sc[...] = jnp.zeros_like(h_sc)

        ext[8:, :] = lx_ref[...]
        u = cb_ref[...] + cw_ref[0:1, :] * ext[pl.ds(5, tc), :]
        for k in range(1, CONV_K):
            u = u + cw_ref[k:k + 1, :] * ext[pl.ds(5 + k, tc), :]
        ext[0:8, :] = ext[tc:tc + 8, :]
        u_ref[...] = u
        r, gi, sp, a, s = _lru_gates(u, wab_ref, bab_ref, lam_ref)
        a_sc[...] = a
        b_sc[...] = s * (gi * u)

        def step(t, h):
            h = a_sc[pl.ds(t, 1), :] * h + b_sc[pl.ds(t, 1), :]
            hs_ref[pl.ds(t, 1), :] = h
            return h

        h = lax.fori_loop(0, tc, step, h_sc[0:1, :], unroll=8)
        h_sc[...] = jnp.broadcast_to(h, h_sc.shape)
        gel, _ = _gelu(lg_ref[...])
        out_ref[...] = gel * hs_ref[...]

    row = lambda i: (i, 0)
    const = lambda i: (0, 0)
    return _pcall(
        body, name=name, grid=(nc,),
        in_specs=[pl.BlockSpec((tc, LRU_W), row), pl.BlockSpec((tc, LRU_W), lambda i: (i, 1)),
                  pl.BlockSpec((CONV_K, LRU_W), const), pl.BlockSpec((1, LRU_W), const),
                  pl.BlockSpec((LRU_W, 2 * LRU_W), const), pl.BlockSpec((1, 2 * LRU_W), const),
                  pl.BlockSpec((1, LRU_W), const)],
        out_specs=[pl.BlockSpec((tc, LRU_W), row)] * 3,
        out_shape=[jax.ShapeDtypeStruct((T, LRU_W), f32)] * 3,
        scratch_shapes=[pltpu.VMEM((tc + 8, LRU_W), f32), pltpu.VMEM((tc, LRU_W), f32),
                        pltpu.VMEM((tc, LRU_W), f32), pltpu.VMEM((8, LRU_W), f32)],
        compiler_params=_params(1),
    )(lxg, lxg, conv_w, conv_b, wab, bab, lam)


def _lru_bwd(dlru, lxg, u, hs, conv_w, wab, bab, lam, *, name, tc=512):
    T = lxg.shape[0]
    tc = min(tc, T)
    nc = T // tc
    bp = tc // 8

    def body(dl_ref, lx_ref, lxp_ref, lg_ref, u_ref, hs_ref, hsp_ref, cw_ref, wab_ref, bab_ref, lam_ref,
             dlxg_ref, dwab_ref, dbab_ref, dcw_ref, dcb_ref, dlam_ref,
             dh_sc, a_sc, ext, du_ext, carry):
        i = pl.program_id(0)
        first_chunk = i == nc - 1

        @pl.when(i == 0)
        def _():
            dwab_ref[...] = jnp.zeros_like(dwab_ref)
            dbab_ref[...] = jnp.zeros_like(dbab_ref)
            dcw_ref[...] = jnp.zeros_like(dcw_ref)
            dcb_ref[...] = jnp.zeros_like(dcb_ref)
            dlam_ref[...] = jnp.zeros_like(dlam_ref)
            carry[...] = jnp.zeros_like(carry)
            du_ext[tc:tc + 8, :] = jnp.zeros((8, LRU_W), f32)

        lg = lg_ref[...]
        gel, th = _gelu(lg)
        dl = dl_ref[...]
        hs = hs_ref[...]
        dlg = dl * hs * _gelu_grad(lg, th)
        u = u_ref[...]
        r, gi, sp, a, s = _lru_gates(u, wab_ref, bab_ref, lam_ref)
        a_sc[...] = a
        dh_sc[...] = dl * gel

        def step(k, c):
            t = tc - 1 - k
            dh = dh_sc[pl.ds(t, 1), :] + c
            dh_sc[pl.ds(t, 1), :] = dh
            return a_sc[pl.ds(t, 1), :] * dh

        c = lax.fori_loop(0, tc, step, carry[0:1, :], unroll=8)
        carry[...] = jnp.broadcast_to(c, carry.shape)

        ext[0:8, :] = jnp.where(first_chunk, 0.0, hsp_ref[...])
        ext[8:, :] = hs
        hprev = ext[pl.ds(7, tc), :]
        dh = dh_sc[...]
        da = dh * hprev
        giu = gi * u
        dla = da * a - (dh * giu) * (a * a / s)
        dgi = dh * s * u
        du = dh * s * gi
        dr = dla * (-LRU_C * sp)
        dlam_ref[...] += jnp.sum(dla * (-LRU_C * r), axis=0, keepdims=True) * (-_sigmoid(-lam_ref[...]))
        dpre = jnp.concatenate([dr * r * (1.0 - r), dgi * gi * (1.0 - gi)], axis=1)
        dpre_b = dpre.astype(MXU_DTYPE)
        du = du + _dot_nt(dpre_b, wab_ref[...])
        dwab_ref[...] += _dot_tn(u.astype(MXU_DTYPE), dpre_b)
        dbab_ref[...] += jnp.sum(dpre, axis=0, keepdims=True)
        dcb_ref[...] += jnp.sum(du, axis=0, keepdims=True)

        du_ext[0:tc, :] = du
        dlx = cw_ref[0:1, :] * du_ext[pl.ds(3, tc), :]
        for k in range(1, CONV_K):
            dlx = dlx + cw_ref[k:k + 1, :] * du_ext[pl.ds(3 - k, tc), :]
        du_ext[tc:tc + 8, :] = du_ext[0:8, :]
        ext[0:8, :] = jnp.where(first_chunk, 0.0, lxp_ref[...])
        ext[8:, :] = lx_ref[...]
        for k in range(CONV_K):
            dcw_ref[k:k + 1, :] += jnp.sum(du * ext[pl.ds(5 + k, tc), :], axis=0, keepdims=True)
        dlxg_ref[:, :LRU_W] = dlx.astype(dlxg_ref.dtype)
        dlxg_ref[:, LRU_W:] = dlg.astype(dlxg_ref.dtype)

    rev = lambda i: (nc - 1 - i, 0)
    prev8 = lambda i: (jnp.maximum((nc - 1 - i) * bp - 1, 0), 0)
    const = lambda i: (0, 0)
    return _pcall(
        body, name=name, grid=(nc,),
        in_specs=[
            pl.BlockSpec((tc, LRU_W), rev),
            pl.BlockSpec((tc, LRU_W), rev),
            pl.BlockSpec((8, LRU_W), prev8),
            pl.BlockSpec((tc, LRU_W), lambda i: (nc - 1 - i, 1)),
            pl.BlockSpec((tc, LRU_W), rev),
            pl.BlockSpec((tc, LRU_W), rev),
            pl.BlockSpec((8, LRU_W), prev8),
            pl.BlockSpec((CONV_K, LRU_W), const),
            pl.BlockSpec((LRU_W, 2 * LRU_W), const),
            pl.BlockSpec((1, 2 * LRU_W), const),
            pl.BlockSpec((1, LRU_W), const),
        ],
        out_specs=[
            pl.BlockSpec((tc, 2 * LRU_W), rev),
            pl.BlockSpec((LRU_W, 2 * LRU_W), const),
            pl.BlockSpec((1, 2 * LRU_W), const),
            pl.BlockSpec((8, LRU_W), const),
            pl.BlockSpec((1, LRU_W), const),
            pl.BlockSpec((1, LRU_W), const),
        ],
        out_shape=[
            jax.ShapeDtypeStruct((T, 2 * LRU_W), MXU_DTYPE),
            jax.ShapeDtypeStruct((LRU_W, 2 * LRU_W), f32),
            jax.ShapeDtypeStruct((1, 2 * LRU_W), f32),
            jax.ShapeDtypeStruct((8, LRU_W), f32),
            jax.ShapeDtypeStruct((1, LRU_W), f32),
            jax.ShapeDtypeStruct((1, LRU_W), f32),
        ],
        scratch_shapes=[pltpu.VMEM((tc, LRU_W), f32), pltpu.VMEM((tc, LRU_W), f32),
                        pltpu.VMEM((tc + 8, LRU_W), f32), pltpu.VMEM((tc + 8, LRU_W), f32),
                        pltpu.VMEM((8, LRU_W), f32)],
        compiler_params=_params(1),
    )(dlru, lxg, lxg, lxg, u, hs, hs, conv_w, wab, bab, lam)


def _mix_out(fox, lru, wo, xhat1, g1, b1, g2, b2, *, name, tm=512):
    T = fox.shape[0]
    tm = min(tm, T)
    nt = T // tm

    def body(fox_ref, lru_ref, wo_ref, xh_ref, g1_ref, b1_ref, g2_ref, b2_ref, xhat_ref, xn_ref, rstd_ref):
        mix = _dot(fox_ref[...].astype(MXU_DTYPE), wo_ref[:FOX_W, :])
        mix = mix + _dot(lru_ref[...].astype(MXU_DTYPE), wo_ref[FOX_W:, :])
        x1 = xh_ref[...] * g1_ref[...] + b1_ref[...]
        xhat, rstd = _layer_norm_stats(DN_ALPHA * x1 + mix)
        xhat_ref[...] = xhat
        xn_ref[...] = xhat * g2_ref[...] + b2_ref[...]
        rstd_ref[...] = jnp.broadcast_to(rstd, rstd_ref.shape)

    row = lambda i: (i, 0)
    const = lambda i: (0, 0)
    vec = pl.BlockSpec((1, D_MODEL), const)
    return _pcall(
        body, name=name, grid=(nt,),
        in_specs=[pl.BlockSpec((tm, FOX_W), row), pl.BlockSpec((tm, LRU_W), row),
                  pl.BlockSpec((D_MODEL, D_MODEL), const), pl.BlockSpec((tm, D_MODEL), row), vec, vec, vec, vec],
        out_specs=[pl.BlockSpec((tm, D_MODEL), row), pl.BlockSpec((tm, D_MODEL), row),
                   pl.BlockSpec((tm, LANES), row)],
        out_shape=[jax.ShapeDtypeStruct((T, D_MODEL), f32), jax.ShapeDtypeStruct((T, D_MODEL), f32),
                   jax.ShapeDtypeStruct((T, LANES), f32)],
        compiler_params=_params(1),
    )(fox, lru, wo, xhat1, g1, b1, g2, b2)


def _mix_out_bwd(dyp, fox, lru, wo, *, name, tm=512):
    T = fox.shape[0]
    tm = min(tm, T)
    nt = T // tm

    def body(dyp_ref, fox_ref, lru_ref, wo_ref, dfox_ref, dlru_ref, dwo_ref):
        i = pl.program_id(0)

        @pl.when(i == 0)
        def _():
            dwo_ref[...] = jnp.zeros_like(dwo_ref)

        dmix = dyp_ref[...].astype(MXU_DTYPE)
        dcat = _dot_nt(dmix, wo_ref[...])
        dfox_ref[...] = dcat[:, :FOX_W].astype(dfox_ref.dtype)
        dlru_ref[...] = dcat[:, FOX_W:]
        dwo_ref[:FOX_W, :] += _dot_tn(fox_ref[...].astype(MXU_DTYPE), dmix)
        dwo_ref[FOX_W:, :] += _dot_tn(lru_ref[...].astype(MXU_DTYPE), dmix)

    row = lambda i: (i, 0)
    const = lambda i: (0, 0)
    return _pcall(
        body, name=name, grid=(nt,),
        in_specs=[pl.BlockSpec((tm, D_MODEL), row), pl.BlockSpec((tm, FOX_W), row), pl.BlockSpec((tm, LRU_W), row),
                  pl.BlockSpec((D_MODEL, D_MODEL), const)],
        out_specs=[pl.BlockSpec((tm, FOX_W), row), pl.BlockSpec((tm, LRU_W), row),
                   pl.BlockSpec((D_MODEL, D_MODEL), const)],
        out_shape=[jax.ShapeDtypeStruct((T, FOX_W), MXU_DTYPE), jax.ShapeDtypeStruct((T, LRU_W), f32),
                   jax.ShapeDtypeStruct((D_MODEL, D_MODEL), f32)],
        compiler_params=_params(1),
    )(dyp, fox, lru, wo)


def make_wp(w_in):
    scale = jnp.concatenate([jnp.full((FOX_W,), 1.0 / math.sqrt(HEAD_DIM), w_in.dtype),
                             jnp.ones((IN_COLS - FOX_W,), w_in.dtype)])
    return jnp.pad(w_in * scale[None, :], ((0, 0), (0, Z_PAD - IN_COLS)))


def _block_diag(w):
    eye = jnp.eye(HEADS, dtype=w.dtype)
    return jnp.einsum("hij,hg->higj", w, eye).reshape(LRU_W, LRU_W)


def _block_diag_extract(m):
    m4 = m.reshape(HEADS, HEAD_DIM, HEADS, HEAD_DIM)
    return jnp.stack([m4[h, :, h, :] for h in range(HEADS)])


class _NoOverlap:
    def start_token(self):
        return None

    def late_weights(self, w, after):
        return dict(f1d=w["f1d"], wp=w["wp"], wo=w["wo"])

    def after_attention(self, after):
        return None

    def ffn2_weights(self, w, after):
        return w["f2g"], w["f2u"], w["f2d"]

    def ffn2_grads(self, grads):
        return None

    def ffn1_grads(self, grads):
        return None

    def mixer_grads(self, dwp, dwo, small):
        return None

    def before_ffn1_bwd(self, after):
        return None


def _tied(a, token):
    return a if token is None else a + token[0, 0]


def _local_step(x, target, w, hooks=None):
    hooks = hooks or _NoOverlap()
    bfp = w["bfp"]
    wab = jnp.concatenate([_block_diag(w["rg_wa"]), _block_diag(w["rg_wx"])], axis=1).astype(MXU_DTYPE)
    bab = jnp.concatenate([w["rg_ba"].reshape(1, LRU_W), w["rg_bx"].reshape(1, LRU_W)], axis=1)

    xb0, g1a, u1a, h1a = _ffn_up(x, w["f1g"], w["f1u"], hooks.start_token(), name="ffn1_up")
    late = hooks.late_weights(w, [h1a])
    f1d, wp, wo = late["f1d"], late["wp"], late["wo"]
    xhat1, xn1, rstd1 = _ffn_down_ln(x, h1a, f1d, w["ln1_g"], w["ln1_b"], name="ffn1_down")
    qkv, lxg, fgb = _proj_in(xn1, wp, bfp, name="proj_in")
    qa, ka, va = _fox_prep(qkv, fgb, name="fox_prep")
    fox, lse = _fox_fwd(qa, ka, va, name="fox_fwd")
    token = hooks.after_attention([lse])
    lru, uconv, hs = _lru_fwd(lxg, w["conv_w"], _tied(w["conv_b"], token), wab, bab, w["lam"], name="lru_fwd")
    xhat2, x2, rstd2 = _mix_out(fox, lru, wo, xhat1, w["ln1_g"], w["ln1_b"], w["ln2_g"], w["ln2_b"], name="mix_out")
    f2g, f2u, f2d = hooks.ffn2_weights(w, [rstd2])
    xb2, g2a, u2a, xhat3, _, rstd3 = _ffn_fwd(x2, f2g, f2u, f2d, w["ln3_g"], w["ln3_b"], name="ffn2_fwd")

    dy3p, dln3g, dln3b, loss = _loss_ln_bwd(xhat3, rstd3, w["ln3_g"], w["ln3_b"], target, name="loss_ln3_bwd")
    dx2, df2g, df2u, df2d = _ffn_bwd(dy3p, xb2, g2a, u2a, f2g, f2u, f2d, name="ffn2_bwd")
    token = hooks.ffn2_grads([df2g, df2u, df2d])
    dy2p, dln2g, dln2b = _ln_bwd(dx2, xhat2, rstd2, _tied(w["ln2_g"], token), name="ln2_bwd")
    dfox, dlru, dwo = _mix_out_bwd(dy2p, fox, lru, wo, name="mix_out_bwd")
    dlxg, dwab, dbab, dcw, dcb, dlam = _lru_bwd(dlru, lxg, uconv, hs, w["conv_w"], wab, bab, w["lam"], name="lru_bwd")
    drep, doa = _fox_bwd_prep(dfox, fox, name="fox_bwd_prep")
    dqa, dka, dva = _fox_bwd(qa, ka, va, doa, lse, drep, name="fox_bwd")
    dfg, dbf = _fox_bwd_post(dqa, dka, fgb, name="fox_bwd_post")
    dx1, dwp = _proj_in_bwd(dqa, dka, dva, dlxg, dfg, xn1, dy2p, wp, name="proj_in_bwd")
    dy1p, dln1g, dln1b = _ln_bwd(dx1, xhat1, rstd1, w["ln1_g"], name="ln1_bwd")
    small = dict(
        ln1_g=dln1g, ln1_b=dln1b, ln2_g=dln2g, ln2_b=dln2b, ln3_g=dln3g, ln3_b=dln3b,
        b_forget=dbf[:, :HEADS], conv_w=dcw[:CONV_K], conv_b=dcb,
        rg_wa=_block_diag_extract(dwab[:, :LRU_W]), rg_wx=_block_diag_extract(dwab[:, LRU_W:]),
        rg_ba=dbab[:, :LRU_W].reshape(HEADS, HEAD_DIM), rg_bx=dbab[:, LRU_W:].reshape(HEADS, HEAD_DIM),
        lru_lambda=dlam,
    )
    hooks.before_ffn1_bwd([dln1b])
    token = hooks.mixer_grads(dwp, dwo, small)
    dx_a, *grads_a = _ffn_bwd(dy1p, xb0, g1a, u1a, w["f1g"], w["f1u"], f1d, token, name="ffn1_bwd_a", part=0)
    token = hooks.ffn1_grads(grads_a)
    dx, *grads_b = _ffn_bwd(dy1p, xb0, g1a, u1a, w["f1g"], w["f1u"], f1d, token, name="ffn1_bwd_b", part=1,
                            dx_init=dx_a)

    grads = dict(f1=(grads_a, grads_b), f2g=df2g, f2u=df2u, f2d=df2d, wp=dwp, wo=dwo, **small)
    return loss, dx, grads


MESH = pl.DeviceIdType.MESH
HBM_SPEC = pl.BlockSpec(memory_space=pl.ANY)
VMEM_SPEC = pl.BlockSpec(memory_space=pltpu.VMEM)


def _position():
    return lax.axis_index("x"), lax.axis_index("y"), lax.axis_index("c")


def _other_chips(x, y):
    return [(1 - x, y), (x, 1 - y), (1 - x, 1 - y)]


def _all_gather_bf16(shards, *, name):
    n = len(shards)

    def body(*refs):
        ins, outs, stages = refs[:n], refs[n:2 * n], refs[2 * n:3 * n]
        send_sems, recv_sems, local_sems = refs[3 * n:]
        x, y, c = _position()
        me, sibling = (x, y, c), (x, y, 1 - c)
        chips = _other_chips(x, y)

        def rows(k, px, py, pc):
            r = shards[k].shape[0]
            m = r // 2
            return outs[k].at[pl.ds(pl.multiple_of((2 * px + py) * r + pc * m, 16), m), :]

        def copy(k, idx, block, to, src=None):
            return pltpu.make_async_remote_copy(
                src_ref=rows(k, *block) if src is None else src, dst_ref=rows(k, *block),
                send_sem=send_sems.at[7 * k + idx], recv_sem=recv_sems.at[7 * k + idx],
                device_id=to, device_id_type=MESH)

        started = []
        mine = []
        for k in range(n):
            m = shards[k].shape[0] // 2
            stages[k][...] = ins[k][pl.ds(pl.multiple_of(c * m, 16), m), :].astype(stages[k].dtype)
            cp = pltpu.make_async_copy(stages[k], rows(k, *me), local_sems.at[k])
            cp.start()
            mine.append(cp)
            first = [copy(k, 0, me, sibling, src=stages[k])]
            first += [copy(k, 1 + j, me, (*chip, c), src=stages[k]) for j, chip in enumerate(chips)]
            for cp in first:
                cp.start()
            started += first
        for k in range(n):
            for j, chip in enumerate(chips):
                copy(k, 1 + j, (*chip, c), me).wait_recv()
                fwd = copy(k, 4 + j, (*chip, c), sibling)
                fwd.start()
                started.append(fwd)
        for k in range(n):
            copy(k, 0, sibling, me).wait_recv()
            for j, chip in enumerate(chips):
                copy(k, 4 + j, (*chip, 1 - c), me).wait_recv()
        for cp in started:
            cp.wait_send()
        for cp in mine:
            cp.wait()

    return _pcall(
        body, name=name,
        in_specs=[VMEM_SPEC] * n, out_specs=[HBM_SPEC] * n,
        out_shape=[jax.ShapeDtypeStruct((N_SHARD * s.shape[0], s.shape[1]), MXU_DTYPE) for s in shards],
        scratch_shapes=[pltpu.VMEM((s.shape[0] // 2, s.shape[1]), MXU_DTYPE) for s in shards]
        + [pltpu.SemaphoreType.DMA((7 * n,)), pltpu.SemaphoreType.DMA((7 * n,)), pltpu.SemaphoreType.DMA((n,))],
        compiler_params=pltpu.CompilerParams(vmem_limit_bytes=VMEM_LIMIT),
    )(*shards)


def _swap_halves(gs, *, name):
    n = len(gs)

    def body(*refs):
        ins, outs = refs[:n], refs[n:2 * n]
        send_sems, recv_sems = refs[2 * n:]
        x, y, c = _position()
        cps = []
        for k in range(n):
            m = gs[k].shape[1] // 2
            src = ins[k].at[:, pl.ds(pl.multiple_of((1 - c) * m, 16), m), :]
            cp = pltpu.make_async_remote_copy(src_ref=src, dst_ref=outs[k], send_sem=send_sems.at[k],
                                              recv_sem=recv_sems.at[k], device_id=(x, y, 1 - c), device_id_type=MESH)
            cp.start()
            cps.append(cp)
        for cp in cps:
            cp.wait()

    return _pcall(
        body, name=name, in_specs=[HBM_SPEC] * n, out_specs=[HBM_SPEC] * n,
        out_shape=[jax.ShapeDtypeStruct((g.shape[0], g.shape[1] // 2, g.shape[2]), g.dtype) for g in gs],
        scratch_shapes=[pltpu.SemaphoreType.DMA((n,)), pltpu.SemaphoreType.DMA((n,))],
    )(*gs)


def _add_halves(gs, recvs, *, name, tm=256):
    n = len(gs)
    _, r, cdim = gs[0].shape
    m = r // 2
    tm = min(tm, m)
    nb = m // tm
    c_idx = lax.axis_index("c").astype(jnp.int32).reshape(1)

    def body(c_ref, *refs):
        for k in range(n):
            refs[2 * n + k][...] = (refs[k][...].astype(f32) + refs[n + k][...].astype(f32)).astype(refs[2 * n + k].dtype)

    mine = pl.BlockSpec((None, tm, cdim), lambda j, i, c_ref: (j, c_ref[0] * nb + i, 0))
    half = pl.BlockSpec((None, tm, cdim), lambda j, i, c_ref: (j, i, 0))
    return _pcall(
        body, name=name,
        grid_spec=pltpu.PrefetchScalarGridSpec(
            num_scalar_prefetch=1, grid=(N_SHARD, nb),
            in_specs=[mine] * n + [half] * n, out_specs=[half] * n),
        out_shape=[jax.ShapeDtypeStruct((N_SHARD, m, cdim), g.dtype) for g in gs],
        compiler_params=_params(2),
    )(c_idx, *gs, *recvs)


def _scatter_partials(ps, *, name):
    n = len(ps)

    def body(*refs):
        ins, outs = refs[:n], refs[n:2 * n]
        send_sems, recv_sems = refs[2 * n:]
        x, y, c = _position()
        me_chip = 2 * x + y
        cps = []
        for k in range(n):
            for j, (px, py) in enumerate(_other_chips(x, y)):
                cp = pltpu.make_async_remote_copy(
                    src_ref=ins[k].at[2 * px + py], dst_ref=outs[k].at[me_chip],
                    send_sem=send_sems.at[3 * k + j], recv_sem=recv_sems.at[3 * k + j],
                    device_id=(px, py, c), device_id_type=MESH)
                cp.start()
                cps.append(cp)
        for cp in cps:
            cp.wait()

    return _pcall(
        body, name=name, in_specs=[HBM_SPEC] * n, out_specs=[HBM_SPEC] * n,
        out_shape=[jax.ShapeDtypeStruct(p.shape, p.dtype) for p in ps],
        scratch_shapes=[pltpu.SemaphoreType.DMA((3 * n,)), pltpu.SemaphoreType.DMA((3 * n,))],
    )(*ps)


def _sum_slabs(ps, qs, *, name, tm=128):
    n = len(qs)
    _, m, cdim = qs[0].shape
    tm = min(tm, m)
    nb = m // tm
    assert m % tm == 0, (m, tm)
    where = jnp.stack([2 * lax.axis_index("x") + lax.axis_index("y"), lax.axis_index("c")]).astype(jnp.int32)

    def body(w_ref, *refs):
        for k in range(n):
            own, q1, q2, q3 = (refs[4 * k + t][...].astype(f32) for t in range(4))
            refs[4 * n + k][...] = ((own + q1) + q2) + q3

    def slab(flip):
        return pl.BlockSpec((None, tm, cdim), lambda i, w_ref: (jnp.bitwise_xor(w_ref[0], flip), i, 0))

    operands = []
    for p, q in zip(ps, qs):
        operands += [p, q, q, q]
    return _pcall(
        body, name=name,
        grid_spec=pltpu.PrefetchScalarGridSpec(
            num_scalar_prefetch=1, grid=(nb,),
            in_specs=[slab(0), slab(2), slab(1), slab(3)] * n,
            out_specs=[pl.BlockSpec((tm, cdim), lambda i, w_ref: (w_ref[1] * nb + i, 0))] * n),
        out_shape=[jax.ShapeDtypeStruct((2 * m, cdim), f32) for _ in qs],
        compiler_params=_params(1),
    )(where, *operands)


def _join_halves(fs, *, name):
    n = len(fs)

    def body(*refs):
        outs = refs[n:2 * n]
        send_sems, recv_sems = refs[2 * n:]
        x, y, c = _position()
        cps = []
        for k in range(n):
            m = fs[k].shape[0] // 2
            half = outs[k].at[pl.ds(pl.multiple_of(c * m, 8), m), :]
            cp = pltpu.make_async_remote_copy(src_ref=half, dst_ref=half, send_sem=send_sems.at[k],
                                              recv_sem=recv_sems.at[k], device_id=(x, y, 1 - c), device_id_type=MESH)
            cp.start()
            cps.append(cp)
        for cp in cps:
            cp.wait()

    return _pcall(
        body, name=name, in_specs=[HBM_SPEC] * n, out_specs=[HBM_SPEC] * n,
        out_shape=[jax.ShapeDtypeStruct(f.shape, f.dtype) for f in fs],
        input_output_aliases={k: k for k in range(n)},
        scratch_shapes=[pltpu.SemaphoreType.DMA((n,)), pltpu.SemaphoreType.DMA((n,))],
    )(*fs)


def _all_reduce_small(v, after=None, *, name):
    r = v.shape[0]
    extra = [] if after is None else [after]

    def body(v_ref, *refs):
        out_ref, buf, send_sems, recv_sems, local_sem = refs[len(extra):]
        x, y, c = _position()
        me, sibling = (x, y, c), (x, y, 1 - c)
        chips = _other_chips(x, y)

        def rows(px, py, pc):
            return buf.at[pl.ds(pl.multiple_of((4 * px + 2 * py + pc) * r, 8), r), :]

        def copy(k, block, to, src=None):
            return pltpu.make_async_remote_copy(
                src_ref=rows(*block) if src is None else src, dst_ref=rows(*block),
                send_sem=send_sems.at[k], recv_sem=recv_sems.at[k], device_id=to, device_id_type=MESH)

        mine = pltpu.make_async_copy(v_ref, rows(*me), local_sem)
        mine.start()
        first = [copy(0, me, sibling, src=v_ref)]
        first += [copy(1 + j, me, (*chip, c), src=v_ref) for j, chip in enumerate(chips)]
        for cp in first:
            cp.start()
        passed = [copy(4 + j, (*chip, c), sibling) for j, chip in enumerate(chips)]
        for j, chip in enumerate(chips):
            copy(1 + j, (*chip, c), me).wait_recv()
            passed[j].start()
        copy(0, sibling, me).wait_recv()
        for j, chip in enumerate(chips):
            copy(4 + j, (*chip, 1 - c), me).wait_recv()
        for cp in first + passed:
            cp.wait_send()
        mine.wait()
        acc = buf[0:r, :]
        for d in range(1, N_DEV):
            acc = acc + buf[d * r:(d + 1) * r, :]
        out_ref[...] = acc

    return _pcall(
        body, name=name, in_specs=[VMEM_SPEC] + [HBM_SPEC] * len(extra), out_specs=VMEM_SPEC,
        out_shape=jax.ShapeDtypeStruct((r, LANES), f32),
        scratch_shapes=[pltpu.VMEM((N_DEV * r, LANES), f32), pltpu.SemaphoreType.DMA((7,)),
                        pltpu.SemaphoreType.DMA((7,)), pltpu.SemaphoreType.DMA],
    )(v, *extra)


SEM_SPEC = pl.BlockSpec(memory_space=pltpu.SEMAPHORE)
HBM_ONLY = pl.BlockSpec(memory_space=pltpu.HBM)
EFFECT = pltpu.SideEffectType.DATAFLOW_SIDE_EFFECTING


def _split_start(bufs, copies_fn, n_sems, *, name):
    n = len(bufs)

    def body(*refs):
        send_sems, recv_sems = refs[n], refs[n + 1]
        thru = refs[n + 2:2 * n + 2]
        token = refs[2 * n + 2]
        for cp in copies_fn(thru, send_sems, recv_sems):
            cp.start()
        token[...] = jnp.zeros_like(token)

    outs = _pcall(
        body, name=name,
        out_shape=(pltpu.SemaphoreType.DMA((n_sems,)), pltpu.SemaphoreType.DMA((n_sems,)),
                   *[pltpu.HBM(b.shape, b.dtype) for b in bufs], jax.ShapeDtypeStruct((8, LANES), f32)),
        in_specs=[HBM_ONLY] * n,
        out_specs=(SEM_SPEC, SEM_SPEC, *[HBM_ONLY] * n, VMEM_SPEC),
        input_output_aliases={k: 2 + k for k in range(n)},
        compiler_params=pltpu.CompilerParams(has_side_effects=EFFECT),
    )(*[pltpu.with_memory_space_constraint(b, pltpu.HBM) for b in bufs])
    return outs[0], outs[1], list(outs[2:2 + n]), outs[2 + n]


def _split_wait(thru, send_sems, recv_sems, after, copies_fn, *, name):
    n = len(thru)

    def body(*refs):
        for cp in copies_fn(refs[:n], refs[n], refs[n + 1]):
            cp.wait_send()
            cp.wait_recv()

    return list(_pcall(
        body, name=name,
        out_shape=tuple(pltpu.HBM(b.shape, b.dtype) for b in thru),
        in_specs=[HBM_ONLY] * n + [SEM_SPEC, SEM_SPEC] + [HBM_SPEC] * len(after),
        out_specs=tuple([HBM_ONLY] * n),
        input_output_aliases={k: k for k in range(n)},
        compiler_params=pltpu.CompilerParams(has_side_effects=EFFECT),
    )(*thru, send_sems, recv_sems, *after))


def _scatter_copies(n):
    def copies(bufs, send_sems, recv_sems):
        x, y, c = _position()
        me_chip = 2 * x + y
        cps = []
        for k in range(n):
            for j, (px, py) in enumerate(_other_chips(x, y)):
                cps.append(pltpu.make_async_remote_copy(
                    src_ref=bufs[k].at[2 * px + py], dst_ref=bufs[n + k].at[me_chip],
                    send_sem=send_sems.at[3 * k + j], recv_sem=recv_sems.at[3 * k + j],
                    device_id=(px, py, c), device_id_type=MESH))
        return cps
    return copies


def _block_rows(buf, px, py, pc):
    m = buf.shape[0] // N_DEV
    return buf.at[pl.ds(pl.multiple_of((4 * px + 2 * py + pc) * m, 16), m), :]


def _gather_ici_copies(n):
    def copies(bufs, send_sems, recv_sems):
        x, y, c = _position()
        cps = []
        for k in range(n):
            rows = _block_rows(bufs[k], x, y, c)
            targets = [(x, y, 1 - c)] + [(px, py, c) for px, py in _other_chips(x, y)]
            for j, to in enumerate(targets):
                cps.append(pltpu.make_async_remote_copy(
                    src_ref=rows, dst_ref=rows, send_sem=send_sems.at[4 * k + j], recv_sem=recv_sems.at[4 * k + j],
                    device_id=to, device_id_type=MESH))
        return cps
    return copies


def _gather_d2d_copies(n):
    def copies(bufs, send_sems, recv_sems):
        x, y, c = _position()
        cps = []
        for k in range(n):
            for j, (px, py) in enumerate(_other_chips(x, y)):
                rows = _block_rows(bufs[k], px, py, c)
                cps.append(pltpu.make_async_remote_copy(
                    src_ref=rows, dst_ref=rows, send_sem=send_sems.at[3 * k + j], recv_sem=recv_sems.at[3 * k + j],
                    device_id=(x, y, 1 - c), device_id_type=MESH))
        return cps
    return copies


def _cast_halves(shards, after, *, name):
    n = len(shards)
    where = jnp.stack([2 * lax.axis_index("x") + lax.axis_index("y"), lax.axis_index("c")]).astype(jnp.int32)

    def body(w_ref, *refs):
        for k in range(n):
            refs[n + 1 + k][...] = refs[k][...].astype(refs[n + 1 + k].dtype)

    def half(s):
        return (s.shape[0] // 2, s.shape[1])

    return _pcall(
        body, name=name,
        grid_spec=pltpu.PrefetchScalarGridSpec(
            num_scalar_prefetch=1, grid=(1,),
            in_specs=[pl.BlockSpec(half(s), lambda i, w_ref: (w_ref[1], 0)) for s in shards] + [HBM_SPEC],
            out_specs=[pl.BlockSpec(half(s), lambda i, w_ref: (2 * w_ref[0] + w_ref[1], 0)) for s in shards]),
        out_shape=[jax.ShapeDtypeStruct((N_SHARD * s.shape[0], s.shape[1]), MXU_DTYPE) for s in shards],
        compiler_params=_params(1),
    )(where, *shards, after)


class _SplitGather:
    def __init__(self, shards, after, tag):
        self.tag = tag
        self.n = len(shards)
        halves = _cast_halves(shards, after, name=f"{tag}_cast")
        self.ici = _split_start(halves, _gather_ici_copies(self.n), 4 * self.n, name=f"{tag}_ici_start")
        self.token = self.ici[3]

    def forward(self, after):
        send_sems, recv_sems, thru, _ = self.ici
        landed = _split_wait(thru, send_sems, recv_sems, after, _gather_ici_copies(self.n), name=f"{self.tag}_ici_wait")
        self.d2d = _split_start(landed, _gather_d2d_copies(self.n), 3 * self.n, name=f"{self.tag}_d2d_start")
        return self.d2d[3]

    def finish(self, after):
        send_sems, recv_sems, thru, _ = self.d2d
        return _split_wait(thru, send_sems, recv_sems, after, _gather_d2d_copies(self.n), name=f"{self.tag}_d2d_wait")


class _Overlap(_NoOverlap):
    def __init__(self, late_shards, ffn2_shards, after):
        self.late = _SplitGather(late_shards, after, "ag1")
        self.ffn2 = _SplitGather(ffn2_shards, self.late.token, "ag2")
        self.reduced = None
        self.ffn1_parts = []

    def start_token(self):
        return self.ffn2.token

    def late_weights(self, w, after):
        token = self.late.forward(after)
        f1d, w_in, wo = self.late.finish([token])
        w_in = w_in.reshape(N_SHARD, D_MODEL, IN_SHARD).transpose(1, 0, 2).reshape(D_MODEL, IN_COLS)
        return dict(f1d=f1d.reshape(N_SHARD, D_FF // N_SHARD, D_MODEL), wp=make_wp(w_in), wo=wo)

    def after_attention(self, after):
        return self.ffn2.forward(after)

    def ffn2_weights(self, w, after):
        full = self.ffn2.finish(after)
        fs = D_FF // N_SHARD
        return (full[0].reshape(N_SHARD, D_MODEL, fs), full[1].reshape(N_SHARD, D_MODEL, fs),
                full[2].reshape(N_SHARD, fs, D_MODEL))

    def ffn2_grads(self, grads):
        recvs = _swap_halves(grads, name="rs_swap_ffn2")
        ps = _add_halves(grads, recvs, name="rs_add_ffn2")
        lands = [lax.empty(p.shape, p.dtype) for p in ps]
        self.scatter = _split_start(list(ps) + lands, _scatter_copies(len(ps)), 3 * len(ps), name="rs_scatter_ffn2_start")
        return self.scatter[3]

    def ffn1_grads(self, grads):
        tag = "ffn1" + "ab"[len(self.ffn1_parts)]
        recvs = _swap_halves(grads, name=f"rs_swap_{tag}")
        ps = list(_add_halves(grads[:2], recvs[:2], name=f"rs_add_{tag}_gu"))
        ps += list(_add_halves(grads[2:], recvs[2:], name=f"rs_add_{tag}_d"))
        lands = [lax.empty(p.shape, p.dtype) for p in ps]
        started = _split_start(ps + lands, _scatter_copies(3), 9, name=f"rs_scatter_{tag}_start")
        self.ffn1_parts.append((tag, started))
        return started[3]

    def ffn1_reduced(self, after):
        sums = []
        for tag, (send_sems, recv_sems, thru, _) in self.ffn1_parts:
            done = _split_wait(thru, send_sems, recv_sems, after, _scatter_copies(3), name=f"rs_scatter_{tag}_wait")
            sums += list(_sum_slabs(done[:2], done[3:5], name=f"rs_sum_{tag}_gu"))
            sums += list(_sum_slabs(done[2:3], done[5:], name=f"rs_sum_{tag}_d"))
        return sums

    def mixer_grads(self, dwp, dwo, small):
        self.small_sum = _all_reduce_small(_pack_small(small), name="ar_small")
        gwin = dwp[:, :IN_COLS].reshape(D_MODEL, N_SHARD, IN_SHARD).transpose(1, 0, 2).astype(GRAD_DTYPE)
        gwo = dwo.reshape(N_SHARD, D_MODEL // N_SHARD, D_MODEL).astype(GRAD_DTYPE)
        recvs = _swap_halves([gwin, gwo], name="rs_swap_mix")
        ps = [_add_halves([g], [r], name=f"rs_add_{tag}")[0] for g, r, tag in zip([gwin, gwo], recvs, ["w_in", "w_out"])]
        lands = [lax.empty(p.shape, p.dtype) for p in ps]
        self.scatter_mix = _split_start(ps + lands, _scatter_copies(2), 6, name="rs_scatter_mix_start")
        return self.scatter_mix[3]

    def mixer_reduced(self, after):
        send_sems, recv_sems, thru, _ = self.scatter_mix
        done = _split_wait(thru, send_sems, recv_sems, after, _scatter_copies(2), name="rs_scatter_mix_wait")
        return [_sum_slabs([done[k]], [done[2 + k]], name=f"rs_sum_{tag}")[0] for k, tag in enumerate(["w_in", "w_out"])]

    def before_ffn1_bwd(self, after):
        send_sems, recv_sems, thru, _ = self.scatter
        n = len(thru) // 2
        done = _split_wait(thru, send_sems, recv_sems, after, _scatter_copies(n), name="rs_scatter_ffn2_wait")
        self.reduced = list(_sum_slabs(done[:n], done[n:], name="rs_sum_ffn2"))


def _adamw(gs, ws, ms, vs, *, name, tm=256):
    n = len(gs)
    r, cdim = gs[0].shape
    tm = r if tm is None else min(tm, r)
    assert r % tm == 0, (r, tm)
    c1 = 1.0 / (1.0 - ADAM_B1 ** ADAM_STEP)
    c2 = 1.0 / (1.0 - ADAM_B2 ** ADAM_STEP)

    def body(*refs):
        for k in range(n):
            g = refs[k][...]
            w = refs[n + k][...]
            m = ADAM_B1 * refs[2 * n + k][...] + (1.0 - ADAM_B1) * g
            v = ADAM_B2 * refs[3 * n + k][...] + (1.0 - ADAM_B2) * (g * g)
            refs[4 * n + k][...] = -ADAM_LR * ((m * c1) / (jnp.sqrt(v * c2) + ADAM_EPS) + ADAM_WD * w)
            refs[5 * n + k][...] = m
            refs[6 * n + k][...] = v

    spec = pl.BlockSpec((tm, cdim), lambda i: (i, 0))
    outs = _pcall(
        body, name=name, grid=(r // tm,), in_specs=[spec] * (4 * n), out_specs=[spec] * (3 * n),
        out_shape=[jax.ShapeDtypeStruct((r, cdim), f32)] * (3 * n),
        compiler_params=_params(1),
    )(*gs, *ws, *ms, *vs)
    return outs[:n], outs[n:2 * n], outs[2 * n:]


BIG = ["ffn1_w_gate", "ffn1_w_up", "ffn1_w_down", "ffn2_w_gate", "ffn2_w_up", "ffn2_w_down"]
SMALL = ["ln1_g", "ln1_b", "b_forget", "conv_w", "conv_b", "rg_wa", "rg_ba", "rg_wx", "rg_bx", "lru_lambda",
         "ln2_g", "ln2_b", "ln3_g", "ln3_b"]
WEIGHTS = ["ffn1_w_gate", "ffn1_w_up", "ffn1_w_down", "ln1_g", "ln1_b", "w_in", "b_forget", "conv_w", "conv_b",
           "rg_wa", "rg_ba", "rg_wx", "rg_bx", "lru_lambda", "w_out", "ln2_g", "ln2_b",
           "ffn2_w_gate", "ffn2_w_up", "ffn2_w_down", "ln3_g", "ln3_b"]


def _pack_small(parts):
    rows = []
    for n in SMALL:
        flat = parts[n].reshape(-1)
        pad = (-flat.shape[0]) % LANES
        rows.append(jnp.pad(flat, (0, pad)).reshape(-1, LANES))
    packed = jnp.concatenate(rows, axis=0)
    return jnp.pad(packed, ((0, (-packed.shape[0]) % 8), (0, 0)))


def _unpack_small(packed, shapes):
    out, r0 = {}, 0
    for n in SMALL:
        size = math.prod(shapes[n])
        nr = -(-size // LANES)
        out[n] = packed[r0:r0 + nr].reshape(-1)[:size].reshape(shapes[n])
        r0 += nr
    return out


def kernel(x, ffn1_w_gate, ffn1_w_up, ffn1_w_down, ln1_g, ln1_b, w_in, b_forget, conv_w, conv_b, rg_wa, rg_ba, rg_wx, rg_bx, lru_lambda, w_out, ln2_g, ln2_b, ffn2_w_gate, ffn2_w_up, ffn2_w_down, ln3_g, ln3_b, loss_target, m_ffn1_w_gate, m_ffn1_w_up, m_ffn1_w_down, m_ln1_g, m_ln1_b, m_w_in, m_b_forget, m_conv_w, m_conv_b, m_rg_wa, m_rg_ba, m_rg_wx, m_rg_bx, m_lru_lambda, m_w_out, m_ln2_g, m_ln2_b, m_ffn2_w_gate, m_ffn2_w_up, m_ffn2_w_down, m_ln3_g, m_ln3_b, v_ffn1_w_gate, v_ffn1_w_up, v_ffn1_w_down, v_ln1_g, v_ln1_b, v_w_in, v_b_forget, v_conv_w, v_conv_b, v_rg_wa, v_rg_ba, v_rg_wx, v_rg_bx, v_lru_lambda, v_w_out, v_ln2_g, v_ln2_b, v_ffn2_w_gate, v_ffn2_w_up, v_ffn2_w_down, v_ln3_g, v_ln3_b):
    args = dict(locals())
    w = {n: args[n] for n in WEIGHTS}
    mom = {n: args["m_" + n] for n in WEIGHTS}
    var = {n: args["v_" + n] for n in WEIGHTS}
    chip = 2 * lax.axis_index("x") + lax.axis_index("y")

    g1 = _all_gather_bf16([w[n][0] for n in BIG[:2]], name="ag_ffn1_up")
    fs = D_FF // N_SHARD
    full = dict(
        f1g=g1[0].reshape(N_SHARD, D_MODEL, fs), f1u=g1[1].reshape(N_SHARD, D_MODEL, fs),
        bfp=jnp.pad(b_forget, ((0, 0), (0, LANES - HEADS))),
        ln1_g=ln1_g, ln1_b=ln1_b, ln2_g=ln2_g, ln2_b=ln2_b, ln3_g=ln3_g, ln3_b=ln3_b,
        conv_b=conv_b, rg_wa=rg_wa[0], rg_wx=rg_wx[0], rg_ba=rg_ba[0], rg_bx=rg_bx[0], lam=lru_lambda,
    )
    cw_place = lax.dynamic_update_slice(jnp.zeros((8, LRU_W), f32), conv_w[0] * 0.5, (0, chip * (LRU_W // N_SHARD)))
    cw_full = _all_reduce_small(cw_place.reshape(-1, LANES), g1[0], name="ag_conv_w")
    full["conv_w"] = cw_full.reshape(8, LRU_W)[:CONV_K]

    hooks = _Overlap([w["ffn1_w_down"][0], w["w_in"][0], w["w_out"][0]], [w[n][0] for n in BIG[3:]], cw_full)
    loss_rep, dx, g = _local_step(x[0], loss_target[0], full, hooks)
    loss = lax.psum(loss_rep[0, 0], ("x", "y", "c"))

    token1 = hooks.ffn1_grads(g["f1"][1])
    red = _join_halves(hooks.reduced + hooks.mixer_reduced([token1]), name="rs_join_rest")
    grads = dict(zip(BIG[3:] + ["w_in", "w_out"], red))

    small_shapes = {n: w[n].shape for n in SMALL}
    small_shapes["conv_w"] = (1, CONV_K, LRU_W)
    gs_red = _unpack_small(hooks.small_sum, small_shapes)
    gs_red["conv_w"] = lax.dynamic_slice(gs_red["conv_w"], (0, 0, chip * (LRU_W // N_SHARD)),
                                         (1, CONV_K, LRU_W // N_SHARD))
    grads.update(gs_red)

    delta, new_m, new_v = {}, {}, {}

    def adamw(names, name, **kw):
        d, nm, nv = _adamw([grads[n] for n in names], [w[n][0] for n in names], [mom[n][0] for n in names],
                           [var[n][0] for n in names], name=name, **kw)
        for i, n in enumerate(names):
            delta[n], new_m[n], new_v[n] = d[i], nm[i], nv[i]

    adamw(BIG[3:], "adamw_ffn2", tm=128)
    adamw(["w_in"], "adamw_w_in")
    adamw(["w_out"], "adamw_w_out")
    shard_shapes = {n: w[n].shape for n in SMALL}
    d, nm, nv = _adamw([_pack_small({n: grads[n] for n in SMALL})], [_pack_small({n: w[n] for n in SMALL})],
                       [_pack_small({n: mom[n] for n in SMALL})], [_pack_small({n: var[n] for n in SMALL})],
                       name="adamw_small", tm=None)
    for dst, packed in ((delta, d[0]), (new_m, nm[0]), (new_v, nv[0])):
        dst.update(_unpack_small(packed, shard_shapes))

    worked = [new_v["ffn2_w_down"], new_v["w_in"], new_v["w_out"], nv[0]]
    ga, ua, da, gb, ub, db = _join_halves(hooks.ffn1_reduced(worked), name="rs_join_ffn1")
    grads["ffn1_w_gate"] = jnp.concatenate([ga, gb], axis=1)
    grads["ffn1_w_up"] = jnp.concatenate([ua, ub], axis=1)
    grads["ffn1_w_down"] = jnp.concatenate([da, db], axis=0)
    adamw(BIG[:3], "adamw_ffn1", tm=128)

    def shaped(tree, n):
        return tree[n].reshape(w[n].shape)

    return (loss, dx[None], *[shaped(grads, n) for n in WEIGHTS], *[shaped(delta, n) for n in WEIGHTS],
            *[shaped(new_m, n) for n in WEIGHTS], *[shaped(new_v, n) for n in WEIGHTS])
```

```python
import functools
import math

import jax
import jax.numpy as jnp
from jax import lax
from jax.experimental import pallas as pl
from jax.experimental.pallas import tpu as pltpu

f32 = jnp.float32
MXU_DTYPE = jnp.bfloat16
GRAD_DTYPE = jnp.bfloat16

D_MODEL = 1024
D_FF = 4096
N_SHARD = 4
N_DEV = 8
FOX_W = 512
LRU_W = 512
HEADS = 8
HEAD_DIM = 64
CONV_K = 4
IN_COLS = 2568
IN_SHARD = IN_COLS // N_SHARD
QKV_W = 3 * FOX_W
Z_PAD = 2688
LANES = 128
LN_EPS = 1e-5
DN_ALPHA = 2.0 ** 0.25
LRU_C = 8.0
NEG_BIG = -1e30
VMEM_LIMIT = 56 * 1024 * 1024

ADAM_LR = 0.001
ADAM_B1 = 0.9
ADAM_B2 = 0.999
ADAM_EPS = 1e-08
ADAM_WD = 0.01
ADAM_STEP = 10


def _pcall(body, **kw):
    return pl.pallas_call(body, **kw)


def _params(n_grid, vmem=VMEM_LIMIT):
    return pltpu.CompilerParams(dimension_semantics=("arbitrary",) * n_grid, vmem_limit_bytes=vmem)


def _dot(a, b):
    return jnp.dot(a, b, preferred_element_type=f32)


def _dot_nt(a, b):
    return lax.dot_general(a, b, (((1,), (1,)), ((), ())), preferred_element_type=f32)


def _dot_tn(a, b):
    return lax.dot_general(a, b, (((0,), (0,)), ((), ())), preferred_element_type=f32)


def _sigmoid(x):
    return 1.0 / (1.0 + jnp.exp(-x))


def _layer_norm_stats(y):
    mu = jnp.mean(y, axis=-1, keepdims=True)
    yc = y - mu
    var = jnp.mean(yc * yc, axis=-1, keepdims=True)
    rstd = lax.rsqrt(var + LN_EPS)
    return yc * rstd, rstd


def _ln_backward(dy, xhat, rstd, gamma):
    dxhat = dy * gamma
    m1 = jnp.mean(dxhat, axis=-1, keepdims=True)
    m2 = jnp.mean(dxhat * xhat, axis=-1, keepdims=True)
    dyp = rstd * (dxhat - m1 - xhat * m2)
    return dyp, jnp.sum(dy * xhat, axis=0, keepdims=True), jnp.sum(dy, axis=0, keepdims=True)


def _ffn_fwd(x, wg, wu, wd, ln_g, ln_b, *, name, tm=1024, tf=512):
    T = x.shape[0]
    tm = min(tm, T)
    fs = D_FF // N_SHARD
    cpf = fs // tf
    nf = D_FF // tf
    nt = T // tm

    def body(x_ref, wg_ref, wu_ref, wd_ref, g_ref, b_ref,
             xb_ref, gact_ref, uact_ref, xhat_ref, xn_ref, rstd_ref, acc_ref):
        f = pl.program_id(1)

        @pl.when(f == 0)
        def _():
            xb_ref[...] = x_ref[...].astype(MXU_DTYPE)
            acc_ref[...] = jnp.zeros_like(acc_ref)

        xb = xb_ref[...]
        g = _dot(xb, wg_ref[...])
        u = _dot(xb, wu_ref[...])
        h = (g * _sigmoid(g)) * u
        gact_ref[...] = g.astype(gact_ref.dtype)
        uact_ref[...] = u.astype(uact_ref.dtype)
        acc_ref[...] += _dot(h.astype(MXU_DTYPE), wd_ref[...])

        @pl.when(f == nf - 1)
        def _():
            y = DN_ALPHA * x_ref[...] + 0.5 * acc_ref[...]
            xhat, rstd = _layer_norm_stats(y)
            xhat_ref[...] = xhat
            xn_ref[...] = (xhat * g_ref[...] + b_ref[...]).astype(xn_ref.dtype)
            rstd_ref[...] = jnp.broadcast_to(rstd, rstd_ref.shape)

    row = lambda i, f: (i, 0)
    return _pcall(
        body, name=name, grid=(nt, nf),
        in_specs=[
            pl.BlockSpec((tm, D_MODEL), row),
            pl.BlockSpec((None, D_MODEL, tf), lambda i, f: (f // cpf, 0, f % cpf)),
            pl.BlockSpec((None, D_MODEL, tf), lambda i, f: (f // cpf, 0, f % cpf)),
            pl.BlockSpec((None, tf, D_MODEL), lambda i, f: (f // cpf, f % cpf, 0)),
            pl.BlockSpec((1, D_MODEL), lambda i, f: (0, 0)),
            pl.BlockSpec((1, D_MODEL), lambda i, f: (0, 0)),
        ],
        out_specs=[
            pl.BlockSpec((tm, D_MODEL), row),
            pl.BlockSpec((tm, tf), lambda i, f: (i, f)),
            pl.BlockSpec((tm, tf), lambda i, f: (i, f)),
            pl.BlockSpec((tm, D_MODEL), row),
            pl.BlockSpec((tm, D_MODEL), row),
            pl.BlockSpec((tm, LANES), row),
        ],
        out_shape=[
            jax.ShapeDtypeStruct((T, D_MODEL), MXU_DTYPE),
            jax.ShapeDtypeStruct((T, D_FF), MXU_DTYPE),
            jax.ShapeDtypeStruct((T, D_FF), MXU_DTYPE),
            jax.ShapeDtypeStruct((T, D_MODEL), f32),
            jax.ShapeDtypeStruct((T, D_MODEL), MXU_DTYPE),
            jax.ShapeDtypeStruct((T, LANES), f32),
        ],
        scratch_shapes=[pltpu.VMEM((tm, D_MODEL), f32)],
        compiler_params=_params(2),
    )(x, wg, wu, wd, ln_g, ln_b)


def _ffn_up(x, wg, wu, after=None, *, name, tm=1024, tf=512):
    T = x.shape[0]
    tm = min(tm, T)
    cpf = (D_FF // N_SHARD) // tf
    nf = D_FF // tf
    extra = [] if after is None else [after]

    def body(x_ref, wg_ref, wu_ref, *refs):
        xb_ref, gact_ref, uact_ref, hact_ref = refs[len(extra):]

        @pl.when(pl.program_id(1) == 0)
        def _():
            xb_ref[...] = x_ref[...].astype(MXU_DTYPE)

        xb = xb_ref[...]
        g = _dot(xb, wg_ref[...])
        u = _dot(xb, wu_ref[...])
        gact_ref[...] = g.astype(gact_ref.dtype)
        uact_ref[...] = u.astype(uact_ref.dtype)
        hact_ref[...] = ((g * _sigmoid(g)) * u).astype(hact_ref.dtype)

    row = lambda i, f: (i, 0)
    tile = pl.BlockSpec((tm, tf), lambda i, f: (i, f))
    cols = pl.BlockSpec((None, D_MODEL, tf), lambda i, f: (f // cpf, 0, f % cpf))
    return _pcall(
        body, name=name, grid=(T // tm, nf),
        in_specs=[pl.BlockSpec((tm, D_MODEL), row), cols, cols] + [pl.BlockSpec(memory_space=pl.ANY)] * len(extra),
        out_specs=[pl.BlockSpec((tm, D_MODEL), row), tile, tile, tile],
        out_shape=[jax.ShapeDtypeStruct((T, D_MODEL), MXU_DTYPE)] + [jax.ShapeDtypeStruct((T, D_FF), MXU_DTYPE)] * 3,
        compiler_params=_params(2),
    )(x, wg, wu, *extra)


def _ffn_down_ln(x, hact, wd, ln_g, ln_b, *, name, tm=1024):
    T = x.shape[0]
    tm = min(tm, T)
    fs = D_FF // N_SHARD

    def body(x_ref, h_ref, wd_ref, g_ref, b_ref, xhat_ref, xn_ref, rstd_ref, acc_ref):
        k = pl.program_id(1)

        @pl.when(k == 0)
        def _():
            acc_ref[...] = jnp.zeros_like(acc_ref)

        acc_ref[...] += _dot(h_ref[...], wd_ref[...])

        @pl.when(k == N_SHARD - 1)
        def _():
            xhat, rstd = _layer_norm_stats(DN_ALPHA * x_ref[...] + 0.5 * acc_ref[...])
            xhat_ref[...] = xhat
            xn_ref[...] = (xhat * g_ref[...] + b_ref[...]).astype(xn_ref.dtype)
            rstd_ref[...] = jnp.broadcast_to(rstd, rstd_ref.shape)

    row = lambda i, k: (i, 0)
    vec = pl.BlockSpec((1, D_MODEL), lambda i, k: (0, 0))
    return _pcall(
        body, name=name, grid=(T // tm, N_SHARD),
        in_specs=[pl.BlockSpec((tm, D_MODEL), row), pl.BlockSpec((tm, fs), lambda i, k: (i, k)),
                  pl.BlockSpec((None, fs, D_MODEL), lambda i, k: (k, 0, 0)), vec, vec],
        out_specs=[pl.BlockSpec((tm, D_MODEL), row), pl.BlockSpec((tm, D_MODEL), row), pl.BlockSpec((tm, LANES), row)],
        out_shape=[jax.ShapeDtypeStruct((T, D_MODEL), f32), jax.ShapeDtypeStruct((T, D_MODEL), MXU_DTYPE),
                   jax.ShapeDtypeStruct((T, LANES), f32)],
        scratch_shapes=[pltpu.VMEM((tm, D_MODEL), f32)],
        compiler_params=_params(2),
    )(x, hact, wd, ln_g, ln_b)


def _ffn_bwd(dyp, xb, gact, uact, wg, wu, wd, after=None, *, name, tm=512, tf=512, part=None, dx_init=None):
    T = dyp.shape[0]
    tm = min(tm, T)
    fs = D_FF // N_SHARD
    cpf = fs // tf
    nt = T // tm
    nf = D_FF // tf if part is None else N_SHARD
    wf = fs if part is None else tf
    slab = (lambda f: f // cpf) if part is None else (lambda f: f)
    chunk = (lambda f: f % cpf) if part is None else (lambda f: part)
    extra = ([] if dx_init is None else [dx_init]) + ([] if after is None else [after])

    def body(dyp_ref, xb_ref, g_ref, u_ref, wg_ref, wu_ref, wd_ref, *refs):
        dx_hbm, dwg_ref, dwu_ref, dwd_ref, dx_sc, dwg_sc, dwu_sc, dwd_sc, sem = refs[len(extra):]
        f = pl.program_id(0)
        i = pl.program_id(1)
        rows = pl.ds(pl.multiple_of(i * tm, tm), tm)
        dyp_t = dyp_ref[...]
        dy = (0.5 * dyp_t).astype(MXU_DTYPE)

        @pl.when(i == 0)
        def _():
            dwg_sc[...] = jnp.zeros_like(dwg_sc)
            dwu_sc[...] = jnp.zeros_like(dwu_sc)
            dwd_sc[...] = jnp.zeros_like(dwd_sc)

        @pl.when(f == 0)
        def _():
            dx_sc[rows, :] = DN_ALPHA * dyp_t if dx_init is None else refs[0][...]

        g = g_ref[...].astype(f32)
        u = u_ref[...].astype(f32)
        sig = _sigmoid(g)
        silu = g * sig
        dh = _dot_nt(dy, wd_ref[...])
        dg = (dh * u * (sig * (1.0 + g * (1.0 - sig)))).astype(MXU_DTYPE)
        du = (dh * silu).astype(MXU_DTYPE)
        hb = (silu * u).astype(MXU_DTYPE)
        dx_sc[rows, :] += _dot_nt(dg, wg_ref[...]) + _dot_nt(du, wu_ref[...])
        xb_t = xb_ref[...]
        dwg_sc[...] += _dot_tn(xb_t, dg)
        dwu_sc[...] += _dot_tn(xb_t, du)
        dwd_sc[...] += _dot_tn(hb, dy)

        @pl.when(i == nt - 1)
        def _():
            dwg_ref[...] = dwg_sc[...].astype(dwg_ref.dtype)
            dwu_ref[...] = dwu_sc[...].astype(dwu_ref.dtype)
            dwd_ref[...] = dwd_sc[...].astype(dwd_ref.dtype)

        @pl.when(jnp.logical_and(f == nf - 1, i == nt - 1))
        def _():
            cp = pltpu.make_async_copy(dx_sc, dx_hbm, sem)
            cp.start()
            cp.wait()

    row = lambda f, i: (i, 0)
    return _pcall(
        body, name=name, grid=(nf, nt),
        in_specs=[
            pl.BlockSpec((tm, D_MODEL), row),
            pl.BlockSpec((tm, D_MODEL), row),
            pl.BlockSpec((tm, tf), lambda f, i: (i, slab(f) * cpf + chunk(f))),
            pl.BlockSpec((tm, tf), lambda f, i: (i, slab(f) * cpf + chunk(f))),
            pl.BlockSpec((None, D_MODEL, tf), lambda f, i: (slab(f), 0, chunk(f))),
            pl.BlockSpec((None, D_MODEL, tf), lambda f, i: (slab(f), 0, chunk(f))),
            pl.BlockSpec((None, tf, D_MODEL), lambda f, i: (slab(f), chunk(f), 0)),
        ] + ([] if dx_init is None else [pl.BlockSpec((tm, D_MODEL), row)])
        + ([] if after is None else [pl.BlockSpec(memory_space=pl.ANY)]),
        out_specs=[
            pl.BlockSpec(memory_space=pl.ANY),
            pl.BlockSpec((None, D_MODEL, tf), lambda f, i: (slab(f), 0, chunk(f) if part is None else 0)),
            pl.BlockSpec((None, D_MODEL, tf), lambda f, i: (slab(f), 0, chunk(f) if part is None else 0)),
            pl.BlockSpec((None, tf, D_MODEL), lambda f, i: (slab(f), chunk(f) if part is None else 0, 0)),
        ],
        out_shape=[
            jax.ShapeDtypeStruct((T, D_MODEL), f32),
            jax.ShapeDtypeStruct((N_SHARD, D_MODEL, wf), GRAD_DTYPE),
            jax.ShapeDtypeStruct((N_SHARD, D_MODEL, wf), GRAD_DTYPE),
            jax.ShapeDtypeStruct((N_SHARD, wf, D_MODEL), GRAD_DTYPE),
        ],
        scratch_shapes=[pltpu.VMEM((T, D_MODEL), f32), pltpu.VMEM((D_MODEL, tf), f32),
                        pltpu.VMEM((D_MODEL, tf), f32), pltpu.VMEM((tf, D_MODEL), f32),
                        pltpu.SemaphoreType.DMA],
        compiler_params=_params(2),
    )(dyp, xb, gact, uact, wg, wu, wd, *extra)


def _loss_ln_bwd(xhat, rstd, ln_g, ln_b, target, *, name, tm=512):
    T = xhat.shape[0]
    tm = min(tm, T)
    nt = T // tm

    def body(xhat_ref, rstd_ref, g_ref, b_ref, t_ref, dyp_ref, dg_ref, db_ref, loss_ref):
        i = pl.program_id(0)

        @pl.when(i == 0)
        def _():
            dg_ref[...] = jnp.zeros_like(dg_ref)
            db_ref[...] = jnp.zeros_like(db_ref)
            loss_ref[...] = jnp.zeros_like(loss_ref)

        xhat_t = xhat_ref[...]
        gamma = g_ref[...]
        err = xhat_t * gamma + b_ref[...] - t_ref[...]
        sq = jnp.sum(jnp.sum(err * err, axis=0, keepdims=True), axis=1, keepdims=True)
        loss_ref[...] += jnp.broadcast_to(sq * (0.5 / D_MODEL), loss_ref.shape)
        dy = err * (1.0 / D_MODEL)
        dyp, dgam, dbeta = _ln_backward(dy, xhat_t, rstd_ref[:, 0:1], gamma)
        dyp_ref[...] = dyp
        dg_ref[...] += dgam
        db_ref[...] += dbeta

    row = lambda i: (i, 0)
    const = lambda i: (0, 0)
    return _pcall(
        body, name=name, grid=(nt,),
        in_specs=[pl.BlockSpec((tm, D_MODEL), row), pl.BlockSpec((tm, LANES), row),
                  pl.BlockSpec((1, D_MODEL), const), pl.BlockSpec((1, D_MODEL), const),
                  pl.BlockSpec((tm, D_MODEL), row)],
        out_specs=[pl.BlockSpec((tm, D_MODEL), row), pl.BlockSpec((1, D_MODEL), const),
                   pl.BlockSpec((1, D_MODEL), const), pl.BlockSpec((1, LANES), const)],
        out_shape=[jax.ShapeDtypeStruct((T, D_MODEL), f32), jax.ShapeDtypeStruct((1, D_MODEL), f32),
                   jax.ShapeDtypeStruct((1, D_MODEL), f32), jax.ShapeDtypeStruct((1, LANES), f32)],
        compiler_params=_params(1),
    )(xhat, rstd, ln_g, ln_b, target)


def _ln_bwd(dy, xhat, rstd, ln_g, *, name, tm=512):
    T = xhat.shape[0]
    tm = min(tm, T)
    nt = T // tm

    def body(dy_ref, xhat_ref, rstd_ref, g_ref, dyp_ref, dg_ref, db_ref):
        i = pl.program_id(0)

        @pl.when(i == 0)
        def _():
            dg_ref[...] = jnp.zeros_like(dg_ref)
            db_ref[...] = jnp.zeros_like(db_ref)

        dyp, dgam, dbeta = _ln_backward(dy_ref[...], xhat_ref[...], rstd_ref[:, 0:1], g_ref[...])
        dyp_ref[...] = dyp
        dg_ref[...] += dgam
        db_ref[...] += dbeta

    row = lambda i: (i, 0)
    const = lambda i: (0, 0)
    return _pcall(
        body, name=name, grid=(nt,),
        in_specs=[pl.BlockSpec((tm, D_MODEL), row), pl.BlockSpec((tm, D_MODEL), row),
                  pl.BlockSpec((tm, LANES), row), pl.BlockSpec((1, D_MODEL), const)],
        out_specs=[pl.BlockSpec((tm, D_MODEL), row), pl.BlockSpec((1, D_MODEL), const),
                   pl.BlockSpec((1, D_MODEL), const)],
        out_shape=[jax.ShapeDtypeStruct((T, D_MODEL), f32), jax.ShapeDtypeStruct((1, D_MODEL), f32),
                   jax.ShapeDtypeStruct((1, D_MODEL), f32)],
        compiler_params=_params(1),
    )(dy, xhat, rstd, ln_g)


def _proj_in(xn, wp, bfp, *, name, tm=512):
    T = xn.shape[0]
    tm = min(tm, T)
    nt = T // tm

    def body(x_ref, w_ref, b_ref, qkv_ref, lxg_ref, fg_ref):
        z = _dot(x_ref[...], w_ref[...])
        qkv_ref[...] = z[:, :QKV_W].astype(qkv_ref.dtype)
        lxg_ref[...] = z[:, QKV_W:QKV_W + 2 * LRU_W]
        fg_ref[...] = z[:, QKV_W + 2 * LRU_W:] + b_ref[...]

    row = lambda i: (i, 0)
    const = lambda i: (0, 0)
    return _pcall(
        body, name=name, grid=(nt,),
        in_specs=[pl.BlockSpec((tm, D_MODEL), row), pl.BlockSpec((D_MODEL, Z_PAD), const),
                  pl.BlockSpec((1, LANES), const)],
        out_specs=[pl.BlockSpec((tm, QKV_W), row), pl.BlockSpec((tm, 2 * LRU_W), row),
                   pl.BlockSpec((tm, LANES), row)],
        out_shape=[jax.ShapeDtypeStruct((T, QKV_W), MXU_DTYPE), jax.ShapeDtypeStruct((T, 2 * LRU_W), f32),
                   jax.ShapeDtypeStruct((T, LANES), f32)],
        compiler_params=_params(1),
    )(xn, wp, bfp)


def _proj_in_bwd(dqa, dka, dva, dlxg, dfg, xn, dyp, wp, *, name, tm=512):
    T = xn.shape[0]
    tm = min(tm, T)
    nt = T // tm

    def body(dq_ref, dk_ref, dv_ref, dl_ref, dfg_ref, x_ref, dyp_ref, w_ref, dx_ref, dw_hbm, dw_sc, sem):
        i = pl.program_id(0)

        @pl.when(i == 0)
        def _():
            dw_sc[...] = jnp.zeros_like(dw_sc)

        low = _low_lanes((tm, LANES))

        def packed(ref):
            pairs = [jnp.where(low, ref[:, (2 * j) * LANES:(2 * j + 1) * LANES],
                               _swap_lane_halves(ref[:, (2 * j + 1) * LANES:(2 * j + 2) * LANES]))
                     for j in range(HEADS // 2)]
            return jnp.concatenate(pairs, axis=1).astype(MXU_DTYPE)

        dz = jnp.concatenate(
            [packed(dq_ref), packed(dk_ref), packed(dv_ref),
             dl_ref[...].astype(MXU_DTYPE), dfg_ref[...].astype(MXU_DTYPE)], axis=1)
        dx_ref[...] = DN_ALPHA * dyp_ref[...] + _dot_nt(dz, w_ref[...])
        dw_sc[...] += _dot_tn(x_ref[...], dz)

        @pl.when(i == nt - 1)
        def _():
            dw_sc[:, :FOX_W] = dw_sc[:, :FOX_W] * (1.0 / math.sqrt(HEAD_DIM))
            cp = pltpu.make_async_copy(dw_sc, dw_hbm, sem)
            cp.start()
            cp.wait()

    row = lambda i: (i, 0)
    const = lambda i: (0, 0)
    return _pcall(
        body, name=name, grid=(nt,),
        in_specs=[pl.BlockSpec((tm, HEADS * LANES), row), pl.BlockSpec((tm, HEADS * LANES), row),
                  pl.BlockSpec((tm, HEADS * LANES), row),
                  pl.BlockSpec((tm, 2 * LRU_W), row), pl.BlockSpec((tm, LANES), row),
                  pl.BlockSpec((tm, D_MODEL), row), pl.BlockSpec((tm, D_MODEL), row),
                  pl.BlockSpec((D_MODEL, Z_PAD), const)],
        out_specs=[pl.BlockSpec((tm, D_MODEL), row), pl.BlockSpec(memory_space=pl.ANY)],
        out_shape=[jax.ShapeDtypeStruct((T, D_MODEL), f32), jax.ShapeDtypeStruct((D_MODEL, Z_PAD), f32)],
        scratch_shapes=[pltpu.VMEM((D_MODEL, Z_PAD), f32), pltpu.SemaphoreType.DMA],
        compiler_params=_params(1),
    )(dqa, dka, dva, dlxg, dfg, xn, dyp, wp)


def _split3(x):
    hi = x.astype(jnp.bfloat16)
    r1 = x - hi.astype(f32)
    mid = r1.astype(jnp.bfloat16)
    lo = (r1 - mid.astype(f32)).astype(jnp.bfloat16)
    return hi, mid, lo


def _tri_dot(tri, x):
    hi, mid, lo = _split3(x)
    return _dot(tri, hi) + _dot(tri, mid) + _dot(tri, lo)


FOX_PAD = HEADS * LANES
AUX = HEAD_DIM


def _low_lanes(shape):
    return lax.broadcasted_iota(jnp.int32, shape, 1) < HEAD_DIM


def _swap_lane_halves(x):
    return pltpu.roll(x, HEAD_DIM, 1)


def _fox_prep(qkv, fgb, *, name, tm=512):
    T = fgb.shape[0]
    tm = min(tm, T)
    nt = T // tm

    def body(qkv_ref, fg_ref, qa_ref, ka_ref, va_ref, carry):
        i = pl.program_id(0)

        @pl.when(i == 0)
        def _():
            carry[...] = jnp.zeros_like(carry)

        x = fg_ref[...]
        ls = jnp.minimum(x, 0.0) - jnp.log(1.0 + jnp.exp(-jnp.abs(x)))
        r = lax.broadcasted_iota(jnp.int32, (tm, tm), 0)
        c = lax.broadcasted_iota(jnp.int32, (tm, tm), 1)
        tri = jnp.where(r >= c, 1.0, 0.0).astype(jnp.bfloat16)
        cum = _tri_dot(tri, ls) + carry[0:1, :]
        carry[...] = jnp.broadcast_to(cum[tm - 1:tm, :], carry.shape)

        lane = lax.broadcasted_iota(jnp.int32, (tm, LANES), 1)
        low = lane < HEAD_DIM
        ones_q = jnp.where(jnp.logical_and(lane >= AUX + 3, lane < AUX + 6), 1.0, 0.0)
        ones_k = jnp.where(jnp.logical_and(lane >= AUX, lane < AUX + 3), 1.0, 0.0)
        for j in range(HEADS // 2):
            pair = [qkv_ref[:, t * FOX_W + j * LANES:t * FOX_W + (j + 1) * LANES].astype(f32) for t in range(3)]
            for odd in range(2):
                h = 2 * j + odd
                q, k, v = [_swap_lane_halves(a) if odd else a for a in pair]
                hi, mid, lo = [a.astype(f32) for a in _split3(jnp.broadcast_to(cum[:, h:h + 1], (tm, LANES)))]
                aux_q = jnp.where(lane == AUX, hi, jnp.where(lane == AUX + 1, mid, jnp.where(lane == AUX + 2, lo, ones_q)))
                aux_k = jnp.where(lane == AUX + 3, -hi,
                                  jnp.where(lane == AUX + 4, -mid, jnp.where(lane == AUX + 5, -lo, ones_k)))
                blk = slice(h * LANES, (h + 1) * LANES)
                qa_ref[:, blk] = jnp.where(low, q, aux_q).astype(qa_ref.dtype)
                ka_ref[:, blk] = jnp.where(low, k, aux_k).astype(ka_ref.dtype)
                va_ref[:, blk] = jnp.where(low, v, 1.0).astype(va_ref.dtype)

    row = lambda i: (i, 0)
    return _pcall(
        body, name=name, grid=(nt,),
        in_specs=[pl.BlockSpec((tm, QKV_W), row), pl.BlockSpec((tm, LANES), row)],
        out_specs=[pl.BlockSpec((tm, FOX_PAD), row)] * 3,
        out_shape=[jax.ShapeDtypeStruct((T, FOX_PAD), MXU_DTYPE)] * 3,
        scratch_shapes=[pltpu.VMEM((8, LANES), f32)],
        compiler_params=_params(1),
    )(qkv, fgb)


def _future_keys(tq, tk):
    r = lax.broadcasted_iota(jnp.int32, (tq, tk), 0)
    c = lax.broadcasted_iota(jnp.int32, (tq, tk), 1)
    return c > r


def _causal_steps(nq, key_major):
    if key_major:
        pairs = [(qi, ki) for ki in range(nq) for qi in range(ki, nq)]
    else:
        pairs = [(qi, ki) for qi in range(nq) for ki in range(qi + 1)]
    return (jnp.asarray([p[0] for p in pairs], jnp.int32), jnp.asarray([p[1] for p in pairs], jnp.int32))


def _fox_fwd(qa, ka, va, *, name, tq=512, hps=4):
    T = qa.shape[0]
    tq = min(tq, T)
    tk = tq
    nq = T // tq
    rep = tk // LANES
    qi_tab, ki_tab = _causal_steps(nq, key_major=False)

    def body(qi_ref, ki_ref, qa_ref, ka_ref, va_ref, o_ref, lse_ref, m_sc, acc_sc):
        t = pl.program_id(1)
        qi = qi_ref[t]
        ki = ki_ref[t]

        @pl.when(ki == 0)
        def _():
            m_sc[...] = jnp.full_like(m_sc, NEG_BIG)
            acc_sc[...] = jnp.zeros_like(acc_sc)

        def tile(diagonal):
            for h in range(hps):
                blk = slice(h * LANES, (h + 1) * LANES)
                s = _dot_nt(qa_ref[:, blk], ka_ref[:, blk])
                if diagonal:
                    s = jnp.where(_future_keys(tq, tk), NEG_BIG, s)
                m_prev = m_sc[h]
                m_new = jnp.maximum(m_prev, jnp.max(s, axis=1, keepdims=True))
                p = jnp.exp(s - jnp.tile(m_new, (1, rep)))
                acc_sc[h] = jnp.exp(m_prev - m_new) * acc_sc[h] + _dot(p.astype(MXU_DTYPE), va_ref[:, blk])
                m_sc[h] = m_new

        @pl.when(ki < qi)
        def _():
            tile(False)

        @pl.when(ki == qi)
        def _():
            tile(True)
            low = _low_lanes((tq, LANES))
            outs = []
            for h in range(hps):
                acc = acc_sc[h]
                den = _swap_lane_halves(acc)
                outs.append(acc / den)
                lse_ref[h] = m_sc[h] + jnp.log(jnp.where(low, den, acc))
            for p in range(hps // 2):
                o_ref[:, p * LANES:(p + 1) * LANES] = jnp.where(low, outs[2 * p], _swap_lane_halves(outs[2 * p + 1]))

    pair = hps * LANES
    return _pcall(
        body, name=name,
        grid_spec=pltpu.PrefetchScalarGridSpec(
            num_scalar_prefetch=2, grid=(HEADS // hps, qi_tab.shape[0]),
            in_specs=[
                pl.BlockSpec((tq, pair), lambda j, t, qi_ref, ki_ref: (qi_ref[t], j)),
                pl.BlockSpec((tk, pair), lambda j, t, qi_ref, ki_ref: (ki_ref[t], j)),
                pl.BlockSpec((tk, pair), lambda j, t, qi_ref, ki_ref: (ki_ref[t], j)),
            ],
            out_specs=[pl.BlockSpec((tq, pair // 2), lambda j, t, qi_ref, ki_ref: (qi_ref[t], j)),
                       pl.BlockSpec((hps, tq, LANES), lambda j, t, qi_ref, ki_ref: (j, qi_ref[t], 0))],
            scratch_shapes=[pltpu.VMEM((hps, tq, LANES), f32)] * 2),
        out_shape=[jax.ShapeDtypeStruct((T, FOX_W), f32), jax.ShapeDtypeStruct((HEADS, T, LANES), f32)],
        compiler_params=_params(2),
    )(qi_tab, ki_tab, qa, ka, va)


def _fox_bwd_prep(do, o, *, name, tm=512):
    T = o.shape[0]
    tm = min(tm, T)
    nt = T // tm

    def body(do_ref, o_ref, d_ref, doa_ref):
        low = _low_lanes((tm, LANES))
        for j in range(HEADS // 2):
            do2 = do_ref[:, j * LANES:(j + 1) * LANES].astype(f32)
            prod = do2 * o_ref[:, j * LANES:(j + 1) * LANES]
            for odd in range(2):
                h = 2 * j + odd
                mine = jnp.where(low, _swap_lane_halves(prod) if odd else prod, 0.0)
                d_ref[h] = jnp.broadcast_to(jnp.sum(mine, axis=1, keepdims=True), (tm, LANES))
                doh = jnp.where(low, _swap_lane_halves(do2) if odd else do2, 0.0)
                doa_ref[:, h * LANES:(h + 1) * LANES] = doh.astype(doa_ref.dtype)

    return _pcall(
        body, name=name, grid=(nt,),
        in_specs=[pl.BlockSpec((tm, FOX_W), lambda i: (i, 0)), pl.BlockSpec((tm, FOX_W), lambda i: (i, 0))],
        out_specs=[pl.BlockSpec((HEADS, tm, LANES), lambda i: (0, i, 0)), pl.BlockSpec((tm, FOX_PAD), lambda i: (i, 0))],
        out_shape=[jax.ShapeDtypeStruct((HEADS, T, LANES), f32), jax.ShapeDtypeStruct((T, FOX_PAD), MXU_DTYPE)],
        compiler_params=_params(1),
    )(do, o)


def _fox_bwd(qa, ka, va, doa, lse, drep, *, name, tq=512, hps=4):
    T = qa.shape[0]
    tq = min(tq, T)
    tk = tq
    nq = T // tq
    rep = tk // LANES
    qi_tab, ki_tab = _causal_steps(nq, key_major=True)

    def body(qi_ref, ki_ref, qa_ref, ka_ref, va_ref, doa_ref, lse_ref, d_ref, dqa_ref, dka_ref, dva_ref, dk_sc, dv_sc):
        t = pl.program_id(1)
        qi = qi_ref[t]
        ki = ki_ref[t]
        rows = pl.ds(pl.multiple_of(qi * tq, tq), tq)

        @pl.when(t == 0)
        def _():
            dqa_ref[...] = jnp.zeros_like(dqa_ref)

        @pl.when(qi == ki)
        def _():
            dk_sc[...] = jnp.zeros_like(dk_sc)
            dv_sc[...] = jnp.zeros_like(dv_sc)

        def tile(diagonal):
            for h in range(hps):
                blk = slice(h * LANES, (h + 1) * LANES)
                qh, kh, doh = qa_ref[:, blk], ka_ref[:, blk], doa_ref[:, blk]
                p = jnp.exp(_dot_nt(qh, kh) - jnp.tile(lse_ref[h], (1, rep)))
                if diagonal:
                    p = jnp.where(_future_keys(tq, tk), 0.0, p)
                dp = _dot_nt(doh, va_ref[:, blk])
                ds = (p * (dp - jnp.tile(d_ref[h], (1, rep)))).astype(MXU_DTYPE)
                dv_sc[h] += _dot_tn(p.astype(MXU_DTYPE), doh)
                dk_sc[h] += _dot_tn(ds, qh)
                dqa_ref[rows, blk] += _dot(ds, kh)

        @pl.when(qi > ki)
        def _():
            tile(False)

        @pl.when(qi == ki)
        def _():
            tile(True)

        @pl.when(qi == nq - 1)
        def _():
            for h in range(hps):
                blk = slice(h * LANES, (h + 1) * LANES)
                dka_ref[:, blk] = dk_sc[h]
                dva_ref[:, blk] = dv_sc[h]

    pair = hps * LANES
    q_blk = lambda j, t, qi_ref, ki_ref: (qi_ref[t], j)
    k_blk = lambda j, t, qi_ref, ki_ref: (ki_ref[t], j)
    stat = pl.BlockSpec((hps, tq, LANES), lambda j, t, qi_ref, ki_ref: (j, qi_ref[t], 0))
    return _pcall(
        body, name=name,
        grid_spec=pltpu.PrefetchScalarGridSpec(
            num_scalar_prefetch=2, grid=(HEADS // hps, qi_tab.shape[0]),
            in_specs=[pl.BlockSpec((tq, pair), q_blk), pl.BlockSpec((tk, pair), k_blk), pl.BlockSpec((tk, pair), k_blk),
                      pl.BlockSpec((tq, pair), q_blk), stat, stat],
            out_specs=[pl.BlockSpec((T, pair), lambda j, t, qi_ref, ki_ref: (0, j)),
                       pl.BlockSpec((tk, pair), k_blk), pl.BlockSpec((tk, pair), k_blk)],
            scratch_shapes=[pltpu.VMEM((hps, tk, LANES), f32)] * 2),
        out_shape=[jax.ShapeDtypeStruct((T, FOX_PAD), f32)] * 3,
        compiler_params=_params(2),
    )(qi_tab, ki_tab, qa, ka, va, doa, lse, drep)


def _fox_bwd_post(dqa, dka, fgb, *, name, tm=512):
    T = fgb.shape[0]
    tm = min(tm, T)
    nt = T // tm

    def body(dqa_ref, dka_ref, fg_ref, dfg_ref, dbf_ref, carry):
        i = pl.program_id(0)

        @pl.when(i == 0)
        def _():
            carry[...] = jnp.zeros_like(carry)
            dbf_ref[...] = jnp.zeros_like(dbf_ref)

        lane = lax.broadcasted_iota(jnp.int32, (tm, LANES), 1)
        dc = jnp.zeros((tm, LANES), f32)
        for h in range(HEADS):
            row_sum = dqa_ref[:, h * LANES + AUX:h * LANES + AUX + 1]
            col_sum = dka_ref[:, h * LANES + AUX + 3:h * LANES + AUX + 4]
            dc = jnp.where(lane == h, jnp.broadcast_to(row_sum - col_sum, (tm, LANES)), dc)
        r = lax.broadcasted_iota(jnp.int32, (tm, tm), 0)
        c = lax.broadcasted_iota(jnp.int32, (tm, tm), 1)
        tri = jnp.where(c >= r, 1.0, 0.0).astype(jnp.bfloat16)
        dls = _tri_dot(tri, dc) + carry[0:1, :]
        carry[...] = jnp.broadcast_to(dls[0:1, :], carry.shape)
        dfg = dls * _sigmoid(-fg_ref[...])
        dfg_ref[...] = dfg
        dbf_ref[...] += jnp.sum(dfg, axis=0, keepdims=True)

    rev = lambda i: (nt - 1 - i, 0)
    return _pcall(
        body, name=name, grid=(nt,),
        in_specs=[pl.BlockSpec((tm, FOX_PAD), rev), pl.BlockSpec((tm, FOX_PAD), rev), pl.BlockSpec((tm, LANES), rev)],
        out_specs=[pl.BlockSpec((tm, LANES), rev), pl.BlockSpec((1, LANES), lambda i: (0, 0))],
        out_shape=[jax.ShapeDtypeStruct((T, LANES), f32), jax.ShapeDtypeStruct((1, LANES), f32)],
        scratch_shapes=[pltpu.VMEM((8, LANES), f32)],
        compiler_params=_params(1),
    )(dqa, dka, fgb)


GELU_C = math.sqrt(2.0 / math.pi)
GELU_A = 0.044715


def _gelu(x):
    t = jnp.tanh(GELU_C * (x + GELU_A * x * x * x))
    return 0.5 * x * (1.0 + t), t


def _gelu_grad(x, t):
    return 0.5 * (1.0 + t) + 0.5 * x * (1.0 - t * t) * GELU_C * (1.0 + 3.0 * GELU_A * x * x)


def _expm1(x):
    e = jnp.exp(x)
    safe = jnp.where(e == 1.0, x, (e - 1.0) * x / jnp.log(jnp.where(e == 1.0, 0.5, e)))
    return jnp.where(x < -0.5, e - 1.0, safe)


def _lru_gates(u, wab_ref, bab_ref, lam_ref):
    pre = _dot(u.astype(MXU_DTYPE), wab_ref[...]) + bab_ref[...]
    r = _sigmoid(pre[:, :LRU_W])
    gi = _sigmoid(pre[:, LRU_W:])
    lam = lam_ref[...]
    sp = jnp.maximum(-lam, 0.0) + jnp.log(1.0 + jnp.exp(-jnp.abs(lam)))
    log_a = -LRU_C * r * sp
    a = jnp.exp(log_a)
    s = jnp.sqrt(-_expm1(2.0 * log_a))
    return r, gi, sp, a, s


def _lru_fwd(lxg, conv_w, conv_b, wab, bab, lam, *, name, tc=512):
    T = lxg.shape[0]
    tc = min(tc, T)
    nc = T // tc

    def body(lx_ref, lg_ref, cw_ref, cb_ref, wab_ref, bab_ref, lam_ref,
             out_ref, u_ref, hs_ref, ext, a_sc, b_sc, h_sc):
        i = pl.program_id(0)

        @pl.when(i == 0)
        def _():
            ext[0:8, :] = jnp.zeros((8, LRU_W), f32)
            h_sc[...] = jnp.zeros_like(h_sc)

        ext[8:, :] = lx_ref[...]
        u = cb_ref[...] + cw_ref[0:1, :] * ext[pl.ds(5, tc), :]
        for k in range(1, CONV_K):
            u = u + cw_ref[k:k + 1, :] * ext[pl.ds(5 + k, tc), :]
        ext[0:8, :] = ext[tc:tc + 8, :]
        u_ref[...] = u
        r, gi, sp, a, s = _lru_gates(u, wab_ref, bab_ref, lam_ref)
        a_sc[...] = a
        b_sc[...] = s * (gi * u)

        def step(t, h):
            h = a_sc[pl.ds(t, 1), :] * h + b_sc[pl.ds(t, 1), :]
            hs_ref[pl.ds(t, 1), :] = h
            return h

        h = lax.fori_loop(0, tc, step, h_sc[0:1, :], unroll=8)
        h_sc[...] = jnp.broadcast_to(h, h_sc.shape)
        gel, _ = _gelu(lg_ref[...])
        out_ref[...] = gel * hs_ref[...]

    row = lambda i: (i, 0)
    const = lambda i: (0, 0)
    return _pcall(
        body, name=name, grid=(nc,),
        in_specs=[pl.BlockSpec((tc, LRU_W), row), pl.BlockSpec((tc, LRU_W), lambda i: (i, 1)),
                  pl.BlockSpec((CONV_K, LRU_W), const), pl.BlockSpec((1, LRU_W), const),
                  pl.BlockSpec((LRU_W, 2 * LRU_W), const), pl.BlockSpec((1, 2 * LRU_W), const),
                  pl.BlockSpec((1, LRU_W), const)],
        out_specs=[pl.BlockSpec((tc, LRU_W), row)] * 3,
        out_shape=[jax.ShapeDtypeStruct((T, LRU_W), f32)] * 3,
        scratch_shapes=[pltpu.VMEM((tc + 8, LRU_W), f32), pltpu.VMEM((tc, LRU_W), f32),
                        pltpu.VMEM((tc, LRU_W), f32), pltpu.VMEM((8, LRU_W), f32)],
        compiler_params=_params(1),
    )(lxg, lxg, conv_w, conv_b, wab, bab, lam)


def _lru_bwd(dlru, lxg, u, hs, conv_w, wab, bab, lam, *, name, tc=512):
    T = lxg.shape[0]
    tc = min(tc, T)
    nc = T // tc
    bp = tc // 8

    def body(dl_ref, lx_ref, lxp_ref, lg_ref, u_ref, hs_ref, hsp_ref, cw_ref, wab_ref, bab_ref, lam_ref,
             dlxg_ref, dwab_ref, dbab_ref, dcw_ref, dcb_ref, dlam_ref,
             dh_sc, a_sc, ext, du_ext, carry):
        i = pl.program_id(0)
        first_chunk = i == nc - 1

        @pl.when(i == 0)
        def _():
            dwab_ref[...] = jnp.zeros_like(dwab_ref)
            dbab_ref[...] = jnp.zeros_like(dbab_ref)
            dcw_ref[...] = jnp.zeros_like(dcw_ref)
            dcb_ref[...] = jnp.zeros_like(dcb_ref)
            dlam_ref[...] = jnp.zeros_like(dlam_ref)
            carry[...] = jnp.zeros_like(carry)
            du_ext[tc:tc + 8, :] = jnp.zeros((8, LRU_W), f32)

        lg = lg_ref[...]
        gel, th = _gelu(lg)
        dl = dl_ref[...]
        hs = hs_ref[...]
        dlg = dl * hs * _gelu_grad(lg, th)
        u = u_ref[...]
        r, gi, sp, a, s = _lru_gates(u, wab_ref, bab_ref, lam_ref)
        a_sc[...] = a
        dh_sc[...] = dl * gel

        def step(k, c):
            t = tc - 1 - k
            dh = dh_sc[pl.ds(t, 1), :] + c
            dh_sc[pl.ds(t, 1), :] = dh
            return a_sc[pl.ds(t, 1), :] * dh

        c = lax.fori_loop(0, tc, step, carry[0:1, :], unroll=8)
        carry[...] = jnp.broadcast_to(c, carry.shape)

        ext[0:8, :] = jnp.where(first_chunk, 0.0, hsp_ref[...])
        ext[8:, :] = hs
        hprev = ext[pl.ds(7, tc), :]
        dh = dh_sc[...]
        da = dh * hprev
        giu = gi * u
        dla = da * a - (dh * giu) * (a * a / s)
        dgi = dh * s * u
        du = dh * s * gi
        dr = dla * (-LRU_C * sp)
        dlam_ref[...] += jnp.sum(dla * (-LRU_C * r), axis=0, keepdims=True) * (-_sigmoid(-lam_ref[...]))
        dpre = jnp.concatenate([dr * r * (1.0 - r), dgi * gi * (1.0 - gi)], axis=1)
        dpre_b = dpre.astype(MXU_DTYPE)
        du = du + _dot_nt(dpre_b, wab_ref[...])
        dwab_ref[...] += _dot_tn(u.astype(MXU_DTYPE), dpre_b)
        dbab_ref[...] += jnp.sum(dpre, axis=0, keepdims=True)
        dcb_ref[...] += jnp.sum(du, axis=0, keepdims=True)

        du_ext[0:tc, :] = du
        dlx = cw_ref[0:1, :] * du_ext[pl.ds(3, tc), :]
        for k in range(1, CONV_K):
            dlx = dlx + cw_ref[k:k + 1, :] * du_ext[pl.ds(3 - k, tc), :]
        du_ext[tc:tc + 8, :] = du_ext[0:8, :]
        ext[0:8, :] = jnp.where(first_chunk, 0.0, lxp_ref[...])
        ext[8:, :] = lx_ref[...]
        for k in range(CONV_K):
            dcw_ref[k:k + 1, :] += jnp.sum(du * ext[pl.ds(5 + k, tc), :], axis=0, keepdims=True)
        dlxg_ref[:, :LRU_W] = dlx.astype(dlxg_ref.dtype)
        dlxg_ref[:, LRU_W:] = dlg.astype(dlxg_ref.dtype)

    rev = lambda i: (nc - 1 - i, 0)
    prev8 = lambda i: (jnp.maximum((nc - 1 - i) * bp - 1, 0), 0)
    const = lambda i: (0, 0)
    return _pcall(
        body, name=name, grid=(nc,),
        in_specs=[
            pl.BlockSpec((tc, LRU_W), rev),
            pl.BlockSpec((tc, LRU_W), rev),
            pl.BlockSpec((8, LRU_W), prev8),
            pl.BlockSpec((tc, LRU_W), lambda i: (nc - 1 - i, 1)),
            pl.BlockSpec((tc, LRU_W), rev),
            pl.BlockSpec((tc, LRU_W), rev),
            pl.BlockSpec((8, LRU_W), prev8),
            pl.BlockSpec((CONV_K, LRU_W), const),
            pl.BlockSpec((LRU_W, 2 * LRU_W), const),
            pl.BlockSpec((1, 2 * LRU_W), const),
            pl.BlockSpec((1, LRU_W), const),
        ],
        out_specs=[
            pl.BlockSpec((tc, 2 * LRU_W), rev),
            pl.BlockSpec((LRU_W, 2 * LRU_W), const),
            pl.BlockSpec((1, 2 * LRU_W), const),
            pl.BlockSpec((8, LRU_W), const),
            pl.BlockSpec((1, LRU_W), const),
            pl.BlockSpec((1, LRU_W), const),
        ],
        out_shape=[
            jax.ShapeDtypeStruct((T, 2 * LRU_W), MXU_DTYPE),
            jax.ShapeDtypeStruct((LRU_W, 2 * LRU_W), f32),
            jax.ShapeDtypeStruct((1, 2 * LRU_W), f32),
            jax.ShapeDtypeStruct((8, LRU_W), f32),
            jax.ShapeDtypeStruct((1, LRU_W), f32),
            jax.ShapeDtypeStruct((1, LRU_W), f32),
        ],
        scratch_shapes=[pltpu.VMEM((tc, LRU_W), f32), pltpu.VMEM((tc, LRU_W), f32),
                        pltpu.VMEM((tc + 8, LRU_W), f32), pltpu.VMEM((tc + 8, LRU_W), f32),
                        pltpu.VMEM((8, LRU_W), f32)],
        compiler_params=_params(1),
    )(dlru, lxg, lxg, lxg, u, hs, hs, conv_w, wab, bab, lam)


def _mix_out(fox, lru, wo, xhat1, g1, b1, g2, b2, *, name, tm=512):
    T = fox.shape[0]
    tm = min(tm, T)
    nt = T // tm

    def body(fox_ref, lru_ref, wo_ref, xh_ref, g1_ref, b1_ref, g2_ref, b2_ref, xhat_ref, xn_ref, rstd_ref):
        mix = _dot(fox_ref[...].astype(MXU_DTYPE), wo_ref[:FOX_W, :])
        mix = mix + _dot(lru_ref[...].astype(MXU_DTYPE), wo_ref[FOX_W:, :])
        x1 = xh_ref[...] * g1_ref[...] + b1_ref[...]
        xhat, rstd = _layer_norm_stats(DN_ALPHA * x1 + mix)
        xhat_ref[...] = xhat
        xn_ref[...] = xhat * g2_ref[...] + b2_ref[...]
        rstd_ref[...] = jnp.broadcast_to(rstd, rstd_ref.shape)

    row = lambda i: (i, 0)
    const = lambda i: (0, 0)
    vec = pl.BlockSpec((1, D_MODEL), const)
    return _pcall(
        body, name=name, grid=(nt,),
        in_specs=[pl.BlockSpec((tm, FOX_W), row), pl.BlockSpec((tm, LRU_W), row),
                  pl.BlockSpec((D_MODEL, D_MODEL), const), pl.BlockSpec((tm, D_MODEL), row), vec, vec, vec, vec],
        out_specs=[pl.BlockSpec((tm, D_MODEL), row), pl.BlockSpec((tm, D_MODEL), row),
                   pl.BlockSpec((tm, LANES), row)],
        out_shape=[jax.ShapeDtypeStruct((T, D_MODEL), f32), jax.ShapeDtypeStruct((T, D_MODEL), f32),
                   jax.ShapeDtypeStruct((T, LANES), f32)],
        compiler_params=_params(1),
    )(fox, lru, wo, xhat1, g1, b1, g2, b2)


def _mix_out_bwd(dyp, fox, lru, wo, *, name, tm=512):
    T = fox.shape[0]
    tm = min(tm, T)
    nt = T // tm

    def body(dyp_ref, fox_ref, lru_ref, wo_ref, dfox_ref, dlru_ref, dwo_ref):
        i = pl.program_id(0)

        @pl.when(i == 0)
        def _():
            dwo_ref[...] = jnp.zeros_like(dwo_ref)

        dmix = dyp_ref[...].astype(MXU_DTYPE)
        dcat = _dot_nt(dmix, wo_ref[...])
        dfox_ref[...] = dcat[:, :FOX_W].astype(dfox_ref.dtype)
        dlru_ref[...] = dcat[:, FOX_W:]
        dwo_ref[:FOX_W, :] += _dot_tn(fox_ref[...].astype(MXU_DTYPE), dmix)
        dwo_ref[FOX_W:, :] += _dot_tn(lru_ref[...].astype(MXU_DTYPE), dmix)

    row = lambda i: (i, 0)
    const = lambda i: (0, 0)
    return _pcall(
        body, name=name, grid=(nt,),
        in_specs=[pl.BlockSpec((tm, D_MODEL), row), pl.BlockSpec((tm, FOX_W), row), pl.BlockSpec((tm, LRU_W), row),
                  pl.BlockSpec((D_MODEL, D_MODEL), const)],
        out_specs=[pl.BlockSpec((tm, FOX_W), row), pl.BlockSpec((tm, LRU_W), row),
                   pl.BlockSpec((D_MODEL, D_MODEL), const)],
        out_shape=[jax.ShapeDtypeStruct((T, FOX_W), MXU_DTYPE), jax.ShapeDtypeStruct((T, LRU_W), f32),
                   jax.ShapeDtypeStruct((D_MODEL, D_MODEL), f32)],
        compiler_params=_params(1),
    )(dyp, fox, lru, wo)


def make_wp(w_in):
    scale = jnp.concatenate([jnp.full((FOX_W,), 1.0 / math.sqrt(HEAD_DIM), w_in.dtype),
                             jnp.ones((IN_COLS - FOX_W,), w_in.dtype)])
    return jnp.pad(w_in * scale[None, :], ((0, 0), (0, Z_PAD - IN_COLS)))


def _block_diag(w):
    eye = jnp.eye(HEADS, dtype=w.dtype)
    return jnp.einsum("hij,hg->higj", w, eye).reshape(LRU_W, LRU_W)


def _block_diag_extract(m):
    m4 = m.reshape(HEADS, HEAD_DIM, HEADS, HEAD_DIM)
    return jnp.stack([m4[h, :, h, :] for h in range(HEADS)])


class _NoOverlap:
    def start_token(self):
        return None

    def late_weights(self, w, after):
        return dict(f1d=w["f1d"], wp=w["wp"], wo=w["wo"])

    def after_attention(self, after):
        return None

    def ffn2_weights(self, w, after):
        return w["f2g"], w["f2u"], w["f2d"]

    def ffn2_grads(self, grads):
        return None

    def ffn1_grads(self, grads):
        return None

    def mixer_grads(self, dwp, dwo, small, loss):
        return None

    def before_ffn1_bwd(self, after):
        return None


def _tied(a, token):
    return a if token is None else a + token[0, 0]


def _local_step(x, target, w, hooks=None):
    hooks = hooks or _NoOverlap()
    bfp = w["bfp"]
    wab = jnp.concatenate([_block_diag(w["rg_wa"]), _block_diag(w["rg_wx"])], axis=1).astype(MXU_DTYPE)
    bab = jnp.concatenate([w["rg_ba"].reshape(1, LRU_W), w["rg_bx"].reshape(1, LRU_W)], axis=1)

    xb0, g1a, u1a, h1a = _ffn_up(x, w["f1g"], w["f1u"], hooks.start_token(), name="ffn1_up")
    late = hooks.late_weights(w, [h1a])
    f1d, wp, wo = late["f1d"], late["wp"], late["wo"]
    xhat1, xn1, rstd1 = _ffn_down_ln(x, h1a, f1d, w["ln1_g"], w["ln1_b"], name="ffn1_down")
    qkv, lxg, fgb = _proj_in(xn1, wp, bfp, name="proj_in")
    qa, ka, va = _fox_prep(qkv, fgb, name="fox_prep")
    fox, lse = _fox_fwd(qa, ka, va, name="fox_fwd")
    token = hooks.after_attention([lse])
    lru, uconv, hs = _lru_fwd(lxg, w["conv_w"], _tied(w["conv_b"], token), wab, bab, w["lam"], name="lru_fwd")
    xhat2, x2, rstd2 = _mix_out(fox, lru, wo, xhat1, w["ln1_g"], w["ln1_b"], w["ln2_g"], w["ln2_b"], name="mix_out")
    f2g, f2u, f2d = hooks.ffn2_weights(w, [rstd2])
    xb2, g2a, u2a, xhat3, _, rstd3 = _ffn_fwd(x2, f2g, f2u, f2d, w["ln3_g"], w["ln3_b"], name="ffn2_fwd")

    dy3p, dln3g, dln3b, loss = _loss_ln_bwd(xhat3, rstd3, w["ln3_g"], w["ln3_b"], target, name="loss_ln3_bwd")
    dx2, df2g, df2u, df2d = _ffn_bwd(dy3p, xb2, g2a, u2a, f2g, f2u, f2d, name="ffn2_bwd")
    token = hooks.ffn2_grads([df2g, df2u, df2d])
    dy2p, dln2g, dln2b = _ln_bwd(dx2, xhat2, rstd2, _tied(w["ln2_g"], token), name="ln2_bwd")
    dfox, dlru, dwo = _mix_out_bwd(dy2p, fox, lru, wo, name="mix_out_bwd")
    dlxg, dwab, dbab, dcw, dcb, dlam = _lru_bwd(dlru, lxg, uconv, hs, w["conv_w"], wab, bab, w["lam"], name="lru_bwd")
    drep, doa = _fox_bwd_prep(dfox, fox, name="fox_bwd_prep")
    dqa, dka, dva = _fox_bwd(qa, ka, va, doa, lse, drep, name="fox_bwd")
    dfg, dbf = _fox_bwd_post(dqa, dka, fgb, name="fox_bwd_post")
    dx1, dwp = _proj_in_bwd(dqa, dka, dva, dlxg, dfg, xn1, dy2p, wp, name="proj_in_bwd")
    dy1p, dln1g, dln1b = _ln_bwd(dx1, xhat1, rstd1, w["ln1_g"], name="ln1_bwd")
    small = dict(
        ln1_g=dln1g, ln1_b=dln1b, ln2_g=dln2g, ln2_b=dln2b, ln3_g=dln3g, ln3_b=dln3b,
        b_forget=dbf[:, :HEADS], conv_w=dcw[:CONV_K], conv_b=dcb,
        rg_wa=_block_diag_extract(dwab[:, :LRU_W]), rg_wx=_block_diag_extract(dwab[:, LRU_W:]),
        rg_ba=dbab[:, :LRU_W].reshape(HEADS, HEAD_DIM), rg_bx=dbab[:, LRU_W:].reshape(HEADS, HEAD_DIM),
        lru_lambda=dlam,
    )
    hooks.before_ffn1_bwd([dln1b])
    token = hooks.mixer_grads(dwp, dwo, small, loss)
    dx_a, *grads_a = _ffn_bwd(dy1p, xb0, g1a, u1a, w["f1g"], w["f1u"], f1d, token, name="ffn1_bwd_a", part=0)
    token = hooks.ffn1_grads(grads_a)
    dx, *grads_b = _ffn_bwd(dy1p, xb0, g1a, u1a, w["f1g"], w["f1u"], f1d, token, name="ffn1_bwd_b", part=1,
                            dx_init=dx_a)

    grads = dict(f1=(grads_a, grads_b), f2g=df2g, f2u=df2u, f2d=df2d, wp=dwp, wo=dwo, **small)
    return loss, dx, grads


MESH = pl.DeviceIdType.MESH
HBM_SPEC = pl.BlockSpec(memory_space=pl.ANY)
VMEM_SPEC = pl.BlockSpec(memory_space=pltpu.VMEM)


def _position():
    return lax.axis_index("x"), lax.axis_index("y"), lax.axis_index("c")


def _other_chips(x, y):
    return [(1 - x, y), (x, 1 - y), (1 - x, 1 - y)]


def _all_gather_bf16(shards, *, name):
    n = len(shards)

    def body(*refs):
        ins, outs, stages = refs[:n], refs[n:2 * n], refs[2 * n:3 * n]
        send_sems, recv_sems, local_sems = refs[3 * n:]
        x, y, c = _position()
        me, sibling = (x, y, c), (x, y, 1 - c)
        chips = _other_chips(x, y)

        def rows(k, px, py, pc):
            r = shards[k].shape[0]
            m = r // 2
            return outs[k].at[pl.ds(pl.multiple_of((2 * px + py) * r + pc * m, 16), m), :]

        def copy(k, idx, block, to, src=None):
            return pltpu.make_async_remote_copy(
                src_ref=rows(k, *block) if src is None else src, dst_ref=rows(k, *block),
                send_sem=send_sems.at[7 * k + idx], recv_sem=recv_sems.at[7 * k + idx],
                device_id=to, device_id_type=MESH)

        started = []
        mine = []
        for k in range(n):
            m = shards[k].shape[0] // 2
            stages[k][...] = ins[k][pl.ds(pl.multiple_of(c * m, 16), m), :].astype(stages[k].dtype)
            cp = pltpu.make_async_copy(stages[k], rows(k, *me), local_sems.at[k])
            cp.start()
            mine.append(cp)
            first = [copy(k, 0, me, sibling, src=stages[k])]
            first += [copy(k, 1 + j, me, (*chip, c), src=stages[k]) for j, chip in enumerate(chips)]
            for cp in first:
                cp.start()
            started += first
        for k in range(n):
            for j, chip in enumerate(chips):
                copy(k, 1 + j, (*chip, c), me).wait_recv()
                fwd = copy(k, 4 + j, (*chip, c), sibling)
                fwd.start()
                started.append(fwd)
        for k in range(n):
            copy(k, 0, sibling, me).wait_recv()
            for j, chip in enumerate(chips):
                copy(k, 4 + j, (*chip, 1 - c), me).wait_recv()
        for cp in started:
            cp.wait_send()
        for cp in mine:
            cp.wait()

    return _pcall(
        body, name=name,
        in_specs=[VMEM_SPEC] * n, out_specs=[HBM_SPEC] * n,
        out_shape=[jax.ShapeDtypeStruct((N_SHARD * s.shape[0], s.shape[1]), MXU_DTYPE) for s in shards],
        scratch_shapes=[pltpu.VMEM((s.shape[0] // 2, s.shape[1]), MXU_DTYPE) for s in shards]
        + [pltpu.SemaphoreType.DMA((7 * n,)), pltpu.SemaphoreType.DMA((7 * n,)), pltpu.SemaphoreType.DMA((n,))],
        compiler_params=pltpu.CompilerParams(vmem_limit_bytes=VMEM_LIMIT),
    )(*shards)


def _swap_halves(gs, *, name):
    n = len(gs)

    def body(*refs):
        ins, outs = refs[:n], refs[n:2 * n]
        send_sems, recv_sems = refs[2 * n:]
        x, y, c = _position()
        cps = []
        for k in range(n):
            m = gs[k].shape[1] // 2
            src = ins[k].at[:, pl.ds(pl.multiple_of((1 - c) * m, 16), m), :]
            cp = pltpu.make_async_remote_copy(src_ref=src, dst_ref=outs[k], send_sem=send_sems.at[k],
                                              recv_sem=recv_sems.at[k], device_id=(x, y, 1 - c), device_id_type=MESH)
            cp.start()
            cps.append(cp)
        for cp in cps:
            cp.wait()

    return _pcall(
        body, name=name, in_specs=[HBM_SPEC] * n, out_specs=[HBM_SPEC] * n,
        out_shape=[jax.ShapeDtypeStruct((g.shape[0], g.shape[1] // 2, g.shape[2]), g.dtype) for g in gs],
        scratch_shapes=[pltpu.SemaphoreType.DMA((n,)), pltpu.SemaphoreType.DMA((n,))],
    )(*gs)


def _add_halves(gs, recvs, *, name, tm=256):
    n = len(gs)
    _, r, cdim = gs[0].shape
    m = r // 2
    tm = min(tm, m)
    nb = m // tm
    c_idx = lax.axis_index("c").astype(jnp.int32).reshape(1)

    def body(c_ref, *refs):
        for k in range(n):
            refs[2 * n + k][...] = (refs[k][...].astype(f32) + refs[n + k][...].astype(f32)).astype(refs[2 * n + k].dtype)

    mine = pl.BlockSpec((None, tm, cdim), lambda j, i, c_ref: (j, c_ref[0] * nb + i, 0))
    half = pl.BlockSpec((None, tm, cdim), lambda j, i, c_ref: (j, i, 0))
    return _pcall(
        body, name=name,
        grid_spec=pltpu.PrefetchScalarGridSpec(
            num_scalar_prefetch=1, grid=(N_SHARD, nb),
            in_specs=[mine] * n + [half] * n, out_specs=[half] * n),
        out_shape=[jax.ShapeDtypeStruct((N_SHARD, m, cdim), g.dtype) for g in gs],
        compiler_params=_params(2),
    )(c_idx, *gs, *recvs)


def _scatter_partials(ps, *, name):
    n = len(ps)

    def body(*refs):
        ins, outs = refs[:n], refs[n:2 * n]
        send_sems, recv_sems = refs[2 * n:]
        x, y, c = _position()
        me_chip = 2 * x + y
        cps = []
        for k in range(n):
            for j, (px, py) in enumerate(_other_chips(x, y)):
                cp = pltpu.make_async_remote_copy(
                    src_ref=ins[k].at[2 * px + py], dst_ref=outs[k].at[me_chip],
                    send_sem=send_sems.at[3 * k + j], recv_sem=recv_sems.at[3 * k + j],
                    device_id=(px, py, c), device_id_type=MESH)
                cp.start()
                cps.append(cp)
        for cp in cps:
            cp.wait()

    return _pcall(
        body, name=name, in_specs=[HBM_SPEC] * n, out_specs=[HBM_SPEC] * n,
        out_shape=[jax.ShapeDtypeStruct(p.shape, p.dtype) for p in ps],
        scratch_shapes=[pltpu.SemaphoreType.DMA((3 * n,)), pltpu.SemaphoreType.DMA((3 * n,))],
    )(*ps)


def _sum_slabs(ps, qs, *, name, tm=128):
    n = len(qs)
    _, m, cdim = qs[0].shape
    tm = min(tm, m)
    nb = m // tm
    assert m % tm == 0, (m, tm)
    where = jnp.stack([2 * lax.axis_index("x") + lax.axis_index("y"), lax.axis_index("c")]).astype(jnp.int32)

    def body(w_ref, *refs):
        for k in range(n):
            own, q1, q2, q3 = (refs[4 * k + t][...].astype(f32) for t in range(4))
            refs[4 * n + k][...] = ((own + q1) + q2) + q3

    def slab(flip):
        return pl.BlockSpec((None, tm, cdim), lambda i, w_ref: (jnp.bitwise_xor(w_ref[0], flip), i, 0))

    operands = []
    for p, q in zip(ps, qs):
        operands += [p, q, q, q]
    return _pcall(
        body, name=name,
        grid_spec=pltpu.PrefetchScalarGridSpec(
            num_scalar_prefetch=1, grid=(nb,),
            in_specs=[slab(0), slab(2), slab(1), slab(3)] * n,
            out_specs=[pl.BlockSpec((tm, cdim), lambda i, w_ref: (w_ref[1] * nb + i, 0))] * n),
        out_shape=[jax.ShapeDtypeStruct((2 * m, cdim), f32) for _ in qs],
        compiler_params=_params(1),
    )(where, *operands)


def _join_halves(fs, *, name):
    n = len(fs)

    def body(*refs):
        outs = refs[n:2 * n]
        send_sems, recv_sems = refs[2 * n:]
        x, y, c = _position()
        cps = []
        for k in range(n):
            m = fs[k].shape[0] // 2
            half = outs[k].at[pl.ds(pl.multiple_of(c * m, 8), m), :]
            cp = pltpu.make_async_remote_copy(src_ref=half, dst_ref=half, send_sem=send_sems.at[k],
                                              recv_sem=recv_sems.at[k], device_id=(x, y, 1 - c), device_id_type=MESH)
            cp.start()
            cps.append(cp)
        for cp in cps:
            cp.wait()

    return _pcall(
        body, name=name, in_specs=[HBM_SPEC] * n, out_specs=[HBM_SPEC] * n,
        out_shape=[jax.ShapeDtypeStruct(f.shape, f.dtype) for f in fs],
        input_output_aliases={k: k for k in range(n)},
        scratch_shapes=[pltpu.SemaphoreType.DMA((n,)), pltpu.SemaphoreType.DMA((n,))],
    )(*fs)


def _all_reduce_small(v, after=None, *, name):
    r = v.shape[0]
    extra = [] if after is None else [after]

    def body(v_ref, *refs):
        out_ref, buf, send_sems, recv_sems, local_sem = refs[len(extra):]
        x, y, c = _position()
        me, sibling = (x, y, c), (x, y, 1 - c)
        chips = _other_chips(x, y)

        def rows(px, py, pc):
            return buf.at[pl.ds(pl.multiple_of((4 * px + 2 * py + pc) * r, 8), r), :]

        def copy(k, block, to, src=None):
            return pltpu.make_async_remote_copy(
                src_ref=rows(*block) if src is None else src, dst_ref=rows(*block),
                send_sem=send_sems.at[k], recv_sem=recv_sems.at[k], device_id=to, device_id_type=MESH)

        mine = pltpu.make_async_copy(v_ref, rows(*me), local_sem)
        mine.start()
        first = [copy(0, me, sibling, src=v_ref)]
        first += [copy(1 + j, me, (*chip, c), src=v_ref) for j, chip in enumerate(chips)]
        for cp in first:
            cp.start()
        passed = [copy(4 + j, (*chip, c), sibling) for j, chip in enumerate(chips)]
        for j, chip in enumerate(chips):
            copy(1 + j, (*chip, c), me).wait_recv()
            passed[j].start()
        copy(0, sibling, me).wait_recv()
        for j, chip in enumerate(chips):
            copy(4 + j, (*chip, 1 - c), me).wait_recv()
        for cp in first + passed:
            cp.wait_send()
        mine.wait()
        acc = buf[0:r, :]
        for d in range(1, N_DEV):
            acc = acc + buf[d * r:(d + 1) * r, :]
        out_ref[...] = acc

    return _pcall(
        body, name=name, in_specs=[VMEM_SPEC] + [HBM_SPEC] * len(extra), out_specs=VMEM_SPEC,
        out_shape=jax.ShapeDtypeStruct((r, LANES), f32),
        scratch_shapes=[pltpu.VMEM((N_DEV * r, LANES), f32), pltpu.SemaphoreType.DMA((7,)),
                        pltpu.SemaphoreType.DMA((7,)), pltpu.SemaphoreType.DMA],
    )(v, *extra)


SEM_SPEC = pl.BlockSpec(memory_space=pltpu.SEMAPHORE)
HBM_ONLY = pl.BlockSpec(memory_space=pltpu.HBM)
EFFECT = pltpu.SideEffectType.DATAFLOW_SIDE_EFFECTING


def _split_start(bufs, copies_fn, n_sems, *, name):
    n = len(bufs)

    def body(*refs):
        send_sems, recv_sems = refs[n], refs[n + 1]
        thru = refs[n + 2:2 * n + 2]
        token = refs[2 * n + 2]
        for cp in copies_fn(thru, send_sems, recv_sems):
            cp.start()
        token[...] = jnp.zeros_like(token)

    outs = _pcall(
        body, name=name,
        out_shape=(pltpu.SemaphoreType.DMA((n_sems,)), pltpu.SemaphoreType.DMA((n_sems,)),
                   *[pltpu.HBM(b.shape, b.dtype) for b in bufs], jax.ShapeDtypeStruct((8, LANES), f32)),
        in_specs=[HBM_ONLY] * n,
        out_specs=(SEM_SPEC, SEM_SPEC, *[HBM_ONLY] * n, VMEM_SPEC),
        input_output_aliases={k: 2 + k for k in range(n)},
        compiler_params=pltpu.CompilerParams(has_side_effects=EFFECT),
    )(*[pltpu.with_memory_space_constraint(b, pltpu.HBM) for b in bufs])
    return outs[0], outs[1], list(outs[2:2 + n]), outs[2 + n]


def _split_wait(thru, send_sems, recv_sems, after, copies_fn, *, name):
    n = len(thru)

    def body(*refs):
        for cp in copies_fn(refs[:n], refs[n], refs[n + 1]):
            cp.wait_send()
            cp.wait_recv()

    return list(_pcall(
        body, name=name,
        out_shape=tuple(pltpu.HBM(b.shape, b.dtype) for b in thru),
        in_specs=[HBM_ONLY] * n + [SEM_SPEC, SEM_SPEC] + [HBM_SPEC] * len(after),
        out_specs=tuple([HBM_ONLY] * n),
        input_output_aliases={k: k for k in range(n)},
        compiler_params=pltpu.CompilerParams(has_side_effects=EFFECT),
    )(*thru, send_sems, recv_sems, *after))


def _scatter_copies(n):
    def copies(bufs, send_sems, recv_sems):
        x, y, c = _position()
        me_chip = 2 * x + y
        cps = []
        for k in range(n):
            for j, (px, py) in enumerate(_other_chips(x, y)):
                cps.append(pltpu.make_async_remote_copy(
                    src_ref=bufs[k].at[2 * px + py], dst_ref=bufs[n + k].at[me_chip],
                    send_sem=send_sems.at[3 * k + j], recv_sem=recv_sems.at[3 * k + j],
                    device_id=(px, py, c), device_id_type=MESH))
        return cps
    return copies


def _block_rows(buf, px, py, pc):
    m = buf.shape[0] // N_DEV
    return buf.at[pl.ds(pl.multiple_of((4 * px + 2 * py + pc) * m, 16), m), :]


def _gather_ici_copies(n):
    def copies(bufs, send_sems, recv_sems):
        x, y, c = _position()
        cps = []
        for k in range(n):
            rows = _block_rows(bufs[k], x, y, c)
            targets = [(x, y, 1 - c)] + [(px, py, c) for px, py in _other_chips(x, y)]
            for j, to in enumerate(targets):
                cps.append(pltpu.make_async_remote_copy(
                    src_ref=rows, dst_ref=rows, send_sem=send_sems.at[4 * k + j], recv_sem=recv_sems.at[4 * k + j],
                    device_id=to, device_id_type=MESH))
        return cps
    return copies


def _gather_d2d_copies(n):
    def copies(bufs, send_sems, recv_sems):
        x, y, c = _position()
        cps = []
        for k in range(n):
            for j, (px, py) in enumerate(_other_chips(x, y)):
                rows = _block_rows(bufs[k], px, py, c)
                cps.append(pltpu.make_async_remote_copy(
                    src_ref=rows, dst_ref=rows, send_sem=send_sems.at[3 * k + j], recv_sem=recv_sems.at[3 * k + j],
                    device_id=(x, y, 1 - c), device_id_type=MESH))
        return cps
    return copies


def _cast_halves(shards, after, *, name):
    n = len(shards)
    where = jnp.stack([2 * lax.axis_index("x") + lax.axis_index("y"), lax.axis_index("c")]).astype(jnp.int32)

    def body(w_ref, *refs):
        for k in range(n):
            refs[n + 1 + k][...] = refs[k][...].astype(refs[n + 1 + k].dtype)

    def half(s):
        return (s.shape[0] // 2, s.shape[1])

    return _pcall(
        body, name=name,
        grid_spec=pltpu.PrefetchScalarGridSpec(
            num_scalar_prefetch=1, grid=(1,),
            in_specs=[pl.BlockSpec(half(s), lambda i, w_ref: (w_ref[1], 0)) for s in shards] + [HBM_SPEC],
            out_specs=[pl.BlockSpec(half(s), lambda i, w_ref: (2 * w_ref[0] + w_ref[1], 0)) for s in shards]),
        out_shape=[jax.ShapeDtypeStruct((N_SHARD * s.shape[0], s.shape[1]), MXU_DTYPE) for s in shards],
        compiler_params=_params(1),
    )(where, *shards, after)


class _SplitGather:
    def __init__(self, shards, after, tag):
        self.tag = tag
        self.n = len(shards)
        halves = _cast_halves(shards, after, name=f"{tag}_cast")
        self.ici = _split_start(halves, _gather_ici_copies(self.n), 4 * self.n, name=f"{tag}_ici_start")
        self.token = self.ici[3]

    def forward(self, after):
        send_sems, recv_sems, thru, _ = self.ici
        landed = _split_wait(thru, send_sems, recv_sems, after, _gather_ici_copies(self.n), name=f"{self.tag}_ici_wait")
        self.d2d = _split_start(landed, _gather_d2d_copies(self.n), 3 * self.n, name=f"{self.tag}_d2d_start")
        return self.d2d[3]

    def finish(self, after):
        send_sems, recv_sems, thru, _ = self.d2d
        return _split_wait(thru, send_sems, recv_sems, after, _gather_d2d_copies(self.n), name=f"{self.tag}_d2d_wait")


class _Overlap(_NoOverlap):
    def __init__(self, late_shards, ffn2_shards, after):
        self.late = _SplitGather(late_shards, after, "ag1")
        self.ffn2 = _SplitGather(ffn2_shards, self.late.token, "ag2")
        self.reduced = None
        self.ffn1_parts = []

    def start_token(self):
        return self.ffn2.token

    def late_weights(self, w, after):
        token = self.late.forward(after)
        f1d, w_in, wo = self.late.finish([token])
        w_in = w_in.reshape(N_SHARD, D_MODEL, IN_SHARD).transpose(1, 0, 2).reshape(D_MODEL, IN_COLS)
        return dict(f1d=f1d.reshape(N_SHARD, D_FF // N_SHARD, D_MODEL), wp=make_wp(w_in), wo=wo)

    def after_attention(self, after):
        return self.ffn2.forward(after)

    def ffn2_weights(self, w, after):
        full = self.ffn2.finish(after)
        fs = D_FF // N_SHARD
        return (full[0].reshape(N_SHARD, D_MODEL, fs), full[1].reshape(N_SHARD, D_MODEL, fs),
                full[2].reshape(N_SHARD, fs, D_MODEL))

    def ffn2_grads(self, grads):
        recvs = _swap_halves(grads, name="rs_swap_ffn2")
        ps = _add_halves(grads, recvs, name="rs_add_ffn2")
        lands = [lax.empty(p.shape, p.dtype) for p in ps]
        self.scatter = _split_start(list(ps) + lands, _scatter_copies(len(ps)), 3 * len(ps), name="rs_scatter_ffn2_start")
        return self.scatter[3]

    def ffn1_grads(self, grads):
        tag = "ffn1" + "ab"[len(self.ffn1_parts)]
        recvs = _swap_halves(grads, name=f"rs_swap_{tag}")
        ps = list(_add_halves(grads[:2], recvs[:2], name=f"rs_add_{tag}_gu"))
        ps += list(_add_halves(grads[2:], recvs[2:], name=f"rs_add_{tag}_d"))
        lands = [lax.empty(p.shape, p.dtype) for p in ps]
        started = _split_start(ps + lands, _scatter_copies(3), 9, name=f"rs_scatter_{tag}_start")
        self.ffn1_parts.append((tag, started))
        return started[3]

    def ffn1_reduced(self, after):
        sums = []
        for tag, (send_sems, recv_sems, thru, _) in self.ffn1_parts:
            done = _split_wait(thru, send_sems, recv_sems, after, _scatter_copies(3), name=f"rs_scatter_{tag}_wait")
            sums += list(_sum_slabs(done[:2], done[3:5], name=f"rs_sum_{tag}_gu"))
            sums += list(_sum_slabs(done[2:3], done[5:], name=f"rs_sum_{tag}_d"))
        return sums

    def mixer_grads(self, dwp, dwo, small, loss):
        packed = jnp.concatenate([_pack_small(small), jnp.broadcast_to(loss, (8, LANES))], axis=0)
        summed = _all_reduce_small(packed, name="ar_small")
        self.small_sum, self.loss_sum = summed[:-8], summed[-8, 0]
        gwin = dwp[:, :IN_COLS].reshape(D_MODEL, N_SHARD, IN_SHARD).transpose(1, 0, 2).astype(GRAD_DTYPE)
        gwo = dwo.reshape(N_SHARD, D_MODEL // N_SHARD, D_MODEL).astype(GRAD_DTYPE)
        recvs = _swap_halves([gwin, gwo], name="rs_swap_mix")
        ps = [_add_halves([g], [r], name=f"rs_add_{tag}")[0] for g, r, tag in zip([gwin, gwo], recvs, ["w_in", "w_out"])]
        lands = [lax.empty(p.shape, p.dtype) for p in ps]
        self.scatter_mix = _split_start(ps + lands, _scatter_copies(2), 6, name="rs_scatter_mix_start")
        return self.scatter_mix[3]

    def mixer_reduced(self, after):
        send_sems, recv_sems, thru, _ = self.scatter_mix
        done = _split_wait(thru, send_sems, recv_sems, after, _scatter_copies(2), name="rs_scatter_mix_wait")
        return [_sum_slabs([done[k]], [done[2 + k]], name=f"rs_sum_{tag}")[0] for k, tag in enumerate(["w_in", "w_out"])]

    def before_ffn1_bwd(self, after):
        send_sems, recv_sems, thru, _ = self.scatter
        n = len(thru) // 2
        done = _split_wait(thru, send_sems, recv_sems, after, _scatter_copies(n), name="rs_scatter_ffn2_wait")
        self.reduced = list(_sum_slabs(done[:n], done[n:], name="rs_sum_ffn2"))


def _adamw(gs, ws, ms, vs, *, name, tm=256):
    n = len(gs)
    r, cdim = gs[0].shape
    tm = r if tm is None else min(tm, r)
    assert r % tm == 0, (r, tm)
    c1 = 1.0 / (1.0 - ADAM_B1 ** ADAM_STEP)
    c2 = 1.0 / (1.0 - ADAM_B2 ** ADAM_STEP)

    def body(*refs):
        for k in range(n):
            g = refs[k][...]
            w = refs[n + k][...]
            m = ADAM_B1 * refs[2 * n + k][...] + (1.0 - ADAM_B1) * g
            v = ADAM_B2 * refs[3 * n + k][...] + (1.0 - ADAM_B2) * (g * g)
            refs[4 * n + k][...] = g
            refs[5 * n + k][...] = -ADAM_LR * ((m * c1) / (jnp.sqrt(v * c2) + ADAM_EPS) + ADAM_WD * w)
            refs[6 * n + k][...] = m
            refs[7 * n + k][...] = v

    flat = pl.BlockSpec((tm, cdim), lambda i: (i, 0))
    like_w = flat if ws[0].ndim == 2 else pl.BlockSpec((None, tm, cdim), lambda i: (0, i, 0))
    outs = _pcall(
        body, name=name, grid=(r // tm,), in_specs=[flat] * n + [like_w] * (3 * n), out_specs=[like_w] * (4 * n),
        out_shape=[jax.ShapeDtypeStruct(ws[0].shape, f32)] * (4 * n),
        compiler_params=_params(1),
    )(*gs, *ws, *ms, *vs)
    return outs[:n], outs[n:2 * n], outs[2 * n:3 * n], outs[3 * n:]


BIG = ["ffn1_w_gate", "ffn1_w_up", "ffn1_w_down", "ffn2_w_gate", "ffn2_w_up", "ffn2_w_down"]
SMALL = ["ln1_g", "ln1_b", "b_forget", "conv_w", "conv_b", "rg_wa", "rg_ba", "rg_wx", "rg_bx", "lru_lambda",
         "ln2_g", "ln2_b", "ln3_g", "ln3_b"]
WEIGHTS = ["ffn1_w_gate", "ffn1_w_up", "ffn1_w_down", "ln1_g", "ln1_b", "w_in", "b_forget", "conv_w", "conv_b",
           "rg_wa", "rg_ba", "rg_wx", "rg_bx", "lru_lambda", "w_out", "ln2_g", "ln2_b",
           "ffn2_w_gate", "ffn2_w_up", "ffn2_w_down", "ln3_g", "ln3_b"]


def _pack_small(parts):
    rows = []
    for n in SMALL:
        flat = parts[n].reshape(-1)
        pad = (-flat.shape[0]) % LANES
        rows.append(jnp.pad(flat, (0, pad)).reshape(-1, LANES))
    packed = jnp.concatenate(rows, axis=0)
    return jnp.pad(packed, ((0, (-packed.shape[0]) % 8), (0, 0)))


def _unpack_small(packed, shapes):
    out, r0 = {}, 0
    for n in SMALL:
        size = math.prod(shapes[n])
        nr = -(-size // LANES)
        out[n] = packed[r0:r0 + nr].reshape(-1)[:size].reshape(shapes[n])
        r0 += nr
    return out


def kernel(x, ffn1_w_gate, ffn1_w_up, ffn1_w_down, ln1_g, ln1_b, w_in, b_forget, conv_w, conv_b, rg_wa, rg_ba, rg_wx, rg_bx, lru_lambda, w_out, ln2_g, ln2_b, ffn2_w_gate, ffn2_w_up, ffn2_w_down, ln3_g, ln3_b, loss_target, m_ffn1_w_gate, m_ffn1_w_up, m_ffn1_w_down, m_ln1_g, m_ln1_b, m_w_in, m_b_forget, m_conv_w, m_conv_b, m_rg_wa, m_rg_ba, m_rg_wx, m_rg_bx, m_lru_lambda, m_w_out, m_ln2_g, m_ln2_b, m_ffn2_w_gate, m_ffn2_w_up, m_ffn2_w_down, m_ln3_g, m_ln3_b, v_ffn1_w_gate, v_ffn1_w_up, v_ffn1_w_down, v_ln1_g, v_ln1_b, v_w_in, v_b_forget, v_conv_w, v_conv_b, v_rg_wa, v_rg_ba, v_rg_wx, v_rg_bx, v_lru_lambda, v_w_out, v_ln2_g, v_ln2_b, v_ffn2_w_gate, v_ffn2_w_up, v_ffn2_w_down, v_ln3_g, v_ln3_b):
    args = dict(locals())
    w = {n: args[n] for n in WEIGHTS}
    mom = {n: args["m_" + n] for n in WEIGHTS}
    var = {n: args["v_" + n] for n in WEIGHTS}
    chip = 2 * lax.axis_index("x") + lax.axis_index("y")

    g1 = _all_gather_bf16([w[n][0] for n in BIG[:2]], name="ag_ffn1_up")
    fs = D_FF // N_SHARD
    full = dict(
        f1g=g1[0].reshape(N_SHARD, D_MODEL, fs), f1u=g1[1].reshape(N_SHARD, D_MODEL, fs),
        bfp=jnp.pad(b_forget, ((0, 0), (0, LANES - HEADS))),
        ln1_g=ln1_g, ln1_b=ln1_b, ln2_g=ln2_g, ln2_b=ln2_b, ln3_g=ln3_g, ln3_b=ln3_b,
        conv_b=conv_b, rg_wa=rg_wa[0], rg_wx=rg_wx[0], rg_ba=rg_ba[0], rg_bx=rg_bx[0], lam=lru_lambda,
    )
    cw_place = lax.dynamic_update_slice(jnp.zeros((8, LRU_W), f32), conv_w[0] * 0.5, (0, chip * (LRU_W // N_SHARD)))
    cw_full = _all_reduce_small(cw_place.reshape(-1, LANES), g1[0], name="ag_conv_w")
    full["conv_w"] = cw_full.reshape(8, LRU_W)[:CONV_K]

    hooks = _Overlap([w["ffn1_w_down"][0], w["w_in"][0], w["w_out"][0]], [w[n][0] for n in BIG[3:]], cw_full)
    loss_rep, dx, g = _local_step(x[0], loss_target[0], full, hooks)
    loss = hooks.loss_sum

    token1 = hooks.ffn1_grads(g["f1"][1])
    red = _join_halves(hooks.reduced + hooks.mixer_reduced([token1]), name="rs_join_rest")
    grads = dict(zip(BIG[3:] + ["w_in", "w_out"], red))

    small_shapes = {n: w[n].shape for n in SMALL}
    small_shapes["conv_w"] = (1, CONV_K, LRU_W)
    gs_red = _unpack_small(hooks.small_sum, small_shapes)
    gs_red["conv_w"] = lax.dynamic_slice(gs_red["conv_w"], (0, 0, chip * (LRU_W // N_SHARD)),
                                         (1, CONV_K, LRU_W // N_SHARD))
    grads.update(gs_red)

    delta, new_m, new_v = {}, {}, {}

    def adamw(names, name, **kw):
        g3, d, nm, nv = _adamw([grads[n] for n in names], [w[n] for n in names], [mom[n] for n in names],
                               [var[n] for n in names], name=name, **kw)
        for i, n in enumerate(names):
            grads[n], delta[n], new_m[n], new_v[n] = g3[i], d[i], nm[i], nv[i]

    adamw(BIG[3:], "adamw_ffn2", tm=128)
    adamw(["w_in"], "adamw_w_in")
    adamw(["w_out"], "adamw_w_out")
    shard_shapes = {n: w[n].shape for n in SMALL}
    _, d, nm, nv = _adamw([_pack_small({n: grads[n] for n in SMALL})], [_pack_small({n: w[n] for n in SMALL})],
                          [_pack_small({n: mom[n] for n in SMALL})], [_pack_small({n: var[n] for n in SMALL})],
                          name="adamw_small", tm=None)
    for dst, packed in ((delta, d[0]), (new_m, nm[0]), (new_v, nv[0])):
        dst.update(_unpack_small(packed, shard_shapes))

    worked = [new_v["ffn2_w_down"], new_v["w_in"], new_v["w_out"], nv[0]]
    ga, ua, da, gb, ub, db = _join_halves(hooks.ffn1_reduced(worked), name="rs_join_ffn1")
    grads["ffn1_w_gate"] = jnp.concatenate([ga, gb], axis=1)
    grads["ffn1_w_up"] = jnp.concatenate([ua, ub], axis=1)
    grads["ffn1_w_down"] = jnp.concatenate([da, db], axis=0)
    adamw(BIG[:3], "adamw_ffn1", tm=128)

    def shaped(tree, n):
        return tree[n].reshape(w[n].shape)

    return (loss, dx[None], *[shaped(grads, n) for n in WEIGHTS], *[shaped(delta, n) for n in WEIGHTS],
            *[shaped(new_m, n) for n in WEIGHTS], *[shaped(new_v, n) for n in WEIGHTS])
```

```python
import functools
import math

import jax
import jax.numpy as jnp
from jax import lax
from jax.experimental import pallas as pl
from jax.experimental.pallas import tpu as pltpu

f32 = jnp.float32
MXU_DTYPE = jnp.bfloat16
GRAD_DTYPE = jnp.bfloat16

D_MODEL = 1024
D_FF = 4096
N_SHARD = 4
N_DEV = 8
FOX_W = 512
LRU_W = 512
HEADS = 8
HEAD_DIM = 64
CONV_K = 4
IN_COLS = 2568
IN_SHARD = IN_COLS // N_SHARD
QKV_W = 3 * FOX_W
Z_PAD = 2688
LANES = 128
LN_EPS = 1e-5
DN_ALPHA = 2.0 ** 0.25
LRU_C = 8.0
NEG_BIG = -1e30
VMEM_LIMIT = 56 * 1024 * 1024

ADAM_LR = 0.001
ADAM_B1 = 0.9
ADAM_B2 = 0.999
ADAM_EPS = 1e-08
ADAM_WD = 0.01
ADAM_STEP = 10


def _pcall(body, **kw):
    return pl.pallas_call(body, **kw)


def _params(n_grid, vmem=VMEM_LIMIT):
    return pltpu.CompilerParams(dimension_semantics=("arbitrary",) * n_grid, vmem_limit_bytes=vmem)


def _dot(a, b):
    return jnp.dot(a, b, preferred_element_type=f32)


def _dot_nt(a, b):
    return lax.dot_general(a, b, (((1,), (1,)), ((), ())), preferred_element_type=f32)


def _dot_tn(a, b):
    return lax.dot_general(a, b, (((0,), (0,)), ((), ())), preferred_element_type=f32)


def _sigmoid(x):
    return 1.0 / (1.0 + jnp.exp(-x))


def _layer_norm_stats(y):
    mu = jnp.mean(y, axis=-1, keepdims=True)
    yc = y - mu
    var = jnp.mean(yc * yc, axis=-1, keepdims=True)
    rstd = lax.rsqrt(var + LN_EPS)
    return yc * rstd, rstd


def _ln_backward(dy, xhat, rstd, gamma):
    dxhat = dy * gamma
    m1 = jnp.mean(dxhat, axis=-1, keepdims=True)
    m2 = jnp.mean(dxhat * xhat, axis=-1, keepdims=True)
    dyp = rstd * (dxhat - m1 - xhat * m2)
    return dyp, jnp.sum(dy * xhat, axis=0, keepdims=True), jnp.sum(dy, axis=0, keepdims=True)


def _ffn_fwd(x, wg, wu, wd, ln_g, ln_b, *, name, tm=1024, tf=512):
    T = x.shape[0]
    tm = min(tm, T)
    fs = D_FF // N_SHARD
    cpf = fs // tf
    nf = D_FF // tf
    nt = T // tm

    def body(x_ref, wg_ref, wu_ref, wd_ref, g_ref, b_ref,
             xb_ref, gact_ref, uact_ref, xhat_ref, xn_ref, rstd_ref, acc_ref):
        f = pl.program_id(1)

        @pl.when(f == 0)
        def _():
            xb_ref[...] = x_ref[...].astype(MXU_DTYPE)
            acc_ref[...] = jnp.zeros_like(acc_ref)

        xb = xb_ref[...]
        g = _dot(xb, wg_ref[...])
        u = _dot(xb, wu_ref[...])
        h = (g * _sigmoid(g)) * u
        gact_ref[...] = g.astype(gact_ref.dtype)
        uact_ref[...] = u.astype(uact_ref.dtype)
        acc_ref[...] += _dot(h.astype(MXU_DTYPE), wd_ref[...])

        @pl.when(f == nf - 1)
        def _():
            y = DN_ALPHA * x_ref[...] + 0.5 * acc_ref[...]
            xhat, rstd = _layer_norm_stats(y)
            xhat_ref[...] = xhat
            xn_ref[...] = (xhat * g_ref[...] + b_ref[...]).astype(xn_ref.dtype)
            rstd_ref[...] = jnp.broadcast_to(rstd, rstd_ref.shape)

    row = lambda i, f: (i, 0)
    return _pcall(
        body, name=name, grid=(nt, nf),
        in_specs=[
            pl.BlockSpec((tm, D_MODEL), row),
            pl.BlockSpec((None, D_MODEL, tf), lambda i, f: (f // cpf, 0, f % cpf)),
            pl.BlockSpec((None, D_MODEL, tf), lambda i, f: (f // cpf, 0, f % cpf)),
            pl.BlockSpec((None, tf, D_MODEL), lambda i, f: (f // cpf, f % cpf, 0)),
            pl.BlockSpec((1, D_MODEL), lambda i, f: (0, 0)),
            pl.BlockSpec((1, D_MODEL), lambda i, f: (0, 0)),
        ],
        out_specs=[
            pl.BlockSpec((tm, D_MODEL), row),
            pl.BlockSpec((tm, tf), lambda i, f: (i, f)),
            pl.BlockSpec((tm, tf), lambda i, f: (i, f)),
            pl.BlockSpec((tm, D_MODEL), row),
            pl.BlockSpec((tm, D_MODEL), row),
            pl.BlockSpec((tm, LANES), row),
        ],
        out_shape=[
            jax.ShapeDtypeStruct((T, D_MODEL), MXU_DTYPE),
            jax.ShapeDtypeStruct((T, D_FF), MXU_DTYPE),
            jax.ShapeDtypeStruct((T, D_FF), MXU_DTYPE),
            jax.ShapeDtypeStruct((T, D_MODEL), f32),
            jax.ShapeDtypeStruct((T, D_MODEL), MXU_DTYPE),
            jax.ShapeDtypeStruct((T, LANES), f32),
        ],
        scratch_shapes=[pltpu.VMEM((tm, D_MODEL), f32)],
        compiler_params=_params(2),
    )(x, wg, wu, wd, ln_g, ln_b)


def _ffn_gate(x, wg, after=None, *, name, tm=1024, tf=512):
    T = x.shape[0]
    tm = min(tm, T)
    cpf = (D_FF // N_SHARD) // tf
    extra = [] if after is None else [after]

    def body(x_ref, wg_ref, *refs):
        xb_ref, gact_ref = refs[len(extra):]

        @pl.when(pl.program_id(1) == 0)
        def _():
            xb_ref[...] = x_ref[...].astype(MXU_DTYPE)

        gact_ref[...] = _dot(xb_ref[...], wg_ref[...]).astype(gact_ref.dtype)

    row = lambda i, f: (i, 0)
    return _pcall(
        body, name=name, grid=(T // tm, D_FF // tf),
        in_specs=[pl.BlockSpec((tm, D_MODEL), row),
                  pl.BlockSpec((None, D_MODEL, tf), lambda i, f: (f // cpf, 0, f % cpf))]
        + [pl.BlockSpec(memory_space=pl.ANY)] * len(extra),
        out_specs=[pl.BlockSpec((tm, D_MODEL), row), pl.BlockSpec((tm, tf), lambda i, f: (i, f))],
        out_shape=[jax.ShapeDtypeStruct((T, D_MODEL), MXU_DTYPE), jax.ShapeDtypeStruct((T, D_FF), MXU_DTYPE)],
        compiler_params=_params(2),
    )(x, wg, *extra)


def _ffn_up(xb, gact, wu, *, name, tm=1024, tf=512):
    T = xb.shape[0]
    tm = min(tm, T)
    cpf = (D_FF // N_SHARD) // tf

    def body(xb_ref, g_ref, wu_ref, uact_ref, hact_ref):
        g = g_ref[...].astype(f32)
        u = _dot(xb_ref[...], wu_ref[...])
        uact_ref[...] = u.astype(uact_ref.dtype)
        hact_ref[...] = ((g * _sigmoid(g)) * u).astype(hact_ref.dtype)

    tile = pl.BlockSpec((tm, tf), lambda i, f: (i, f))
    return _pcall(
        body, name=name, grid=(T // tm, D_FF // tf),
        in_specs=[pl.BlockSpec((tm, D_MODEL), lambda i, f: (i, 0)), tile,
                  pl.BlockSpec((None, D_MODEL, tf), lambda i, f: (f // cpf, 0, f % cpf))],
        out_specs=[tile, tile],
        out_shape=[jax.ShapeDtypeStruct((T, D_FF), MXU_DTYPE)] * 2,
        compiler_params=_params(2),
    )(xb, gact, wu)


def _ffn_down_ln(x, hact, wd, ln_g, ln_b, *, name, tm=1024):
    T = x.shape[0]
    tm = min(tm, T)
    fs = D_FF // N_SHARD

    def body(x_ref, h_ref, wd_ref, g_ref, b_ref, xhat_ref, xn_ref, rstd_ref, acc_ref):
        k = pl.program_id(1)

        @pl.when(k == 0)
        def _():
            acc_ref[...] = jnp.zeros_like(acc_ref)

        acc_ref[...] += _dot(h_ref[...], wd_ref[...])

        @pl.when(k == N_SHARD - 1)
        def _():
            xhat, rstd = _layer_norm_stats(DN_ALPHA * x_ref[...] + 0.5 * acc_ref[...])
            xhat_ref[...] = xhat
            xn_ref[...] = (xhat * g_ref[...] + b_ref[...]).astype(xn_ref.dtype)
            rstd_ref[...] = jnp.broadcast_to(rstd, rstd_ref.shape)

    row = lambda i, k: (i, 0)
    vec = pl.BlockSpec((1, D_MODEL), lambda i, k: (0, 0))
    return _pcall(
        body, name=name, grid=(T // tm, N_SHARD),
        in_specs=[pl.BlockSpec((tm, D_MODEL), row), pl.BlockSpec((tm, fs), lambda i, k: (i, k)),
                  pl.BlockSpec((None, fs, D_MODEL), lambda i, k: (k, 0, 0)), vec, vec],
        out_specs=[pl.BlockSpec((tm, D_MODEL), row), pl.BlockSpec((tm, D_MODEL), row), pl.BlockSpec((tm, LANES), row)],
        out_shape=[jax.ShapeDtypeStruct((T, D_MODEL), f32), jax.ShapeDtypeStruct((T, D_MODEL), MXU_DTYPE),
                   jax.ShapeDtypeStruct((T, LANES), f32)],
        scratch_shapes=[pltpu.VMEM((tm, D_MODEL), f32)],
        compiler_params=_params(2),
    )(x, hact, wd, ln_g, ln_b)


def _ffn_bwd(dyp, xb, gact, uact, wg, wu, wd, after=None, *, name, tm=512, tf=512, part=None, dx_init=None):
    T = dyp.shape[0]
    tm = min(tm, T)
    fs = D_FF // N_SHARD
    cpf = fs // tf
    nt = T // tm
    nf = D_FF // tf if part is None else N_SHARD
    wf = fs if part is None else tf
    slab = (lambda f: f // cpf) if part is None else (lambda f: f)
    chunk = (lambda f: f % cpf) if part is None else (lambda f: part)
    extra = ([] if dx_init is None else [dx_init]) + ([] if after is None else [after])

    def body(dyp_ref, xb_ref, g_ref, u_ref, wg_ref, wu_ref, wd_ref, *refs):
        dx_hbm, dwg_ref, dwu_ref, dwd_ref, dx_sc, dwg_sc, dwu_sc, dwd_sc, sem = refs[len(extra):]
        f = pl.program_id(0)
        i = pl.program_id(1)
        rows = pl.ds(pl.multiple_of(i * tm, tm), tm)
        dyp_t = dyp_ref[...]
        dy = (0.5 * dyp_t).astype(MXU_DTYPE)

        @pl.when(i == 0)
        def _():
            dwg_sc[...] = jnp.zeros_like(dwg_sc)
            dwu_sc[...] = jnp.zeros_like(dwu_sc)
            dwd_sc[...] = jnp.zeros_like(dwd_sc)

        @pl.when(f == 0)
        def _():
            dx_sc[rows, :] = DN_ALPHA * dyp_t if dx_init is None else refs[0][...]

        g = g_ref[...].astype(f32)
        u = u_ref[...].astype(f32)
        sig = _sigmoid(g)
        silu = g * sig
        dh = _dot_nt(dy, wd_ref[...])
        dg = (dh * u * (sig * (1.0 + g * (1.0 - sig)))).astype(MXU_DTYPE)
        du = (dh * silu).astype(MXU_DTYPE)
        hb = (silu * u).astype(MXU_DTYPE)
        dx_sc[rows, :] += _dot_nt(dg, wg_ref[...]) + _dot_nt(du, wu_ref[...])
        xb_t = xb_ref[...]
        dwg_sc[...] += _dot_tn(xb_t, dg)
        dwu_sc[...] += _dot_tn(xb_t, du)
        dwd_sc[...] += _dot_tn(hb, dy)

        @pl.when(i == nt - 1)
        def _():
            dwg_ref[...] = dwg_sc[...].astype(dwg_ref.dtype)
            dwu_ref[...] = dwu_sc[...].astype(dwu_ref.dtype)
            dwd_ref[...] = dwd_sc[...].astype(dwd_ref.dtype)

        @pl.when(jnp.logical_and(f == nf - 1, i == nt - 1))
        def _():
            cp = pltpu.make_async_copy(dx_sc, dx_hbm, sem)
            cp.start()
            cp.wait()

    row = lambda f, i: (i, 0)
    return _pcall(
        body, name=name, grid=(nf, nt),
        in_specs=[
            pl.BlockSpec((tm, D_MODEL), row),
            pl.BlockSpec((tm, D_MODEL), row),
            pl.BlockSpec((tm, tf), lambda f, i: (i, slab(f) * cpf + chunk(f))),
            pl.BlockSpec((tm, tf), lambda f, i: (i, slab(f) * cpf + chunk(f))),
            pl.BlockSpec((None, D_MODEL, tf), lambda f, i: (slab(f), 0, chunk(f))),
            pl.BlockSpec((None, D_MODEL, tf), lambda f, i: (slab(f), 0, chunk(f))),
            pl.BlockSpec((None, tf, D_MODEL), lambda f, i: (slab(f), chunk(f), 0)),
        ] + ([] if dx_init is None else [pl.BlockSpec((tm, D_MODEL), row)])
        + ([] if after is None else [pl.BlockSpec(memory_space=pl.ANY)]),
        out_specs=[
            pl.BlockSpec(memory_space=pl.ANY),
            pl.BlockSpec((None, D_MODEL, tf), lambda f, i: (slab(f), 0, chunk(f) if part is None else 0)),
            pl.BlockSpec((None, D_MODEL, tf), lambda f, i: (slab(f), 0, chunk(f) if part is None else 0)),
            pl.BlockSpec((None, tf, D_MODEL), lambda f, i: (slab(f), chunk(f) if part is None else 0, 0)),
        ],
        out_shape=[
            jax.ShapeDtypeStruct((T, D_MODEL), f32),
            jax.ShapeDtypeStruct((N_SHARD, D_MODEL, wf), GRAD_DTYPE),
            jax.ShapeDtypeStruct((N_SHARD, D_MODEL, wf), GRAD_DTYPE),
            jax.ShapeDtypeStruct((N_SHARD, wf, D_MODEL), GRAD_DTYPE),
        ],
        scratch_shapes=[pltpu.VMEM((T, D_MODEL), f32), pltpu.VMEM((D_MODEL, tf), f32),
                        pltpu.VMEM((D_MODEL, tf), f32), pltpu.VMEM((tf, D_MODEL), f32),
                        pltpu.SemaphoreType.DMA],
        compiler_params=_params(2),
    )(dyp, xb, gact, uact, wg, wu, wd, *extra)


def _loss_ln_bwd(xhat, rstd, ln_g, ln_b, target, *, name, tm=512):
    T = xhat.shape[0]
    tm = min(tm, T)
    nt = T // tm

    def body(xhat_ref, rstd_ref, g_ref, b_ref, t_ref, dyp_ref, dg_ref, db_ref, loss_ref):
        i = pl.program_id(0)

        @pl.when(i == 0)
        def _():
            dg_ref[...] = jnp.zeros_like(dg_ref)
            db_ref[...] = jnp.zeros_like(db_ref)
            loss_ref[...] = jnp.zeros_like(loss_ref)

        xhat_t = xhat_ref[...]
        gamma = g_ref[...]
        err = xhat_t * gamma + b_ref[...] - t_ref[...]
        sq = jnp.sum(jnp.sum(err * err, axis=0, keepdims=True), axis=1, keepdims=True)
        loss_ref[...] += jnp.broadcast_to(sq * (0.5 / D_MODEL), loss_ref.shape)
        dy = err * (1.0 / D_MODEL)
        dyp, dgam, dbeta = _ln_backward(dy, xhat_t, rstd_ref[:, 0:1], gamma)
        dyp_ref[...] = dyp
        dg_ref[...] += dgam
        db_ref[...] += dbeta

    row = lambda i: (i, 0)
    const = lambda i: (0, 0)
    return _pcall(
        body, name=name, grid=(nt,),
        in_specs=[pl.BlockSpec((tm, D_MODEL), row), pl.BlockSpec((tm, LANES), row),
                  pl.BlockSpec((1, D_MODEL), const), pl.BlockSpec((1, D_MODEL), const),
                  pl.BlockSpec((tm, D_MODEL), row)],
        out_specs=[pl.BlockSpec((tm, D_MODEL), row), pl.BlockSpec((1, D_MODEL), const),
                   pl.BlockSpec((1, D_MODEL), const), pl.BlockSpec((1, LANES), const)],
        out_shape=[jax.ShapeDtypeStruct((T, D_MODEL), f32), jax.ShapeDtypeStruct((1, D_MODEL), f32),
                   jax.ShapeDtypeStruct((1, D_MODEL), f32), jax.ShapeDtypeStruct((1, LANES), f32)],
        compiler_params=_params(1),
    )(xhat, rstd, ln_g, ln_b, target)


def _ln_bwd(dy, xhat, rstd, ln_g, *, name, tm=512):
    T = xhat.shape[0]
    tm = min(tm, T)
    nt = T // tm

    def body(dy_ref, xhat_ref, rstd_ref, g_ref, dyp_ref, dg_ref, db_ref):
        i = pl.program_id(0)

        @pl.when(i == 0)
        def _():
            dg_ref[...] = jnp.zeros_like(dg_ref)
            db_ref[...] = jnp.zeros_like(db_ref)

        dyp, dgam, dbeta = _ln_backward(dy_ref[...], xhat_ref[...], rstd_ref[:, 0:1], g_ref[...])
        dyp_ref[...] = dyp
        dg_ref[...] += dgam
        db_ref[...] += dbeta

    row = lambda i: (i, 0)
    const = lambda i: (0, 0)
    return _pcall(
        body, name=name, grid=(nt,),
        in_specs=[pl.BlockSpec((tm, D_MODEL), row), pl.BlockSpec((tm, D_MODEL), row),
                  pl.BlockSpec((tm, LANES), row), pl.BlockSpec((1, D_MODEL), const)],
        out_specs=[pl.BlockSpec((tm, D_MODEL), row), pl.BlockSpec((1, D_MODEL), const),
                   pl.BlockSpec((1, D_MODEL), const)],
        out_shape=[jax.ShapeDtypeStruct((T, D_MODEL), f32), jax.ShapeDtypeStruct((1, D_MODEL), f32),
                   jax.ShapeDtypeStruct((1, D_MODEL), f32)],
        compiler_params=_params(1),
    )(dy, xhat, rstd, ln_g)


def _proj_in(xn, wp, bfp, *, name, tm=512):
    T = xn.shape[0]
    tm = min(tm, T)
    nt = T // tm

    def body(x_ref, w_ref, b_ref, qkv_ref, lxg_ref, fg_ref):
        z = _dot(x_ref[...], w_ref[...])
        qkv_ref[...] = z[:, :QKV_W].astype(qkv_ref.dtype)
        lxg_ref[...] = z[:, QKV_W:QKV_W + 2 * LRU_W]
        fg_ref[...] = z[:, QKV_W + 2 * LRU_W:] + b_ref[...]

    row = lambda i: (i, 0)
    const = lambda i: (0, 0)
    return _pcall(
        body, name=name, grid=(nt,),
        in_specs=[pl.BlockSpec((tm, D_MODEL), row), pl.BlockSpec((D_MODEL, Z_PAD), const),
                  pl.BlockSpec((1, LANES), const)],
        out_specs=[pl.BlockSpec((tm, QKV_W), row), pl.BlockSpec((tm, 2 * LRU_W), row),
                   pl.BlockSpec((tm, LANES), row)],
        out_shape=[jax.ShapeDtypeStruct((T, QKV_W), MXU_DTYPE), jax.ShapeDtypeStruct((T, 2 * LRU_W), f32),
                   jax.ShapeDtypeStruct((T, LANES), f32)],
        compiler_params=_params(1),
    )(xn, wp, bfp)


def _proj_in_bwd(dqa, dka, dva, dlxg, dfg, xn, dyp, wp, *, name, tm=512):
    T = xn.shape[0]
    tm = min(tm, T)
    nt = T // tm

    def body(dq_ref, dk_ref, dv_ref, dl_ref, dfg_ref, x_ref, dyp_ref, w_ref, dx_ref, dw_hbm, dw_sc, sem):
        i = pl.program_id(0)

        @pl.when(i == 0)
        def _():
            dw_sc[...] = jnp.zeros_like(dw_sc)

        low = _low_lanes((tm, LANES))

        def packed(ref):
            pairs = [jnp.where(low, ref[:, (2 * j) * LANES:(2 * j + 1) * LANES],
                               _swap_lane_halves(ref[:, (2 * j + 1) * LANES:(2 * j + 2) * LANES]))
                     for j in range(HEADS // 2)]
            return jnp.concatenate(pairs, axis=1).astype(MXU_DTYPE)

        dz = jnp.concatenate(
            [packed(dq_ref), packed(dk_ref), packed(dv_ref),
             dl_ref[...].astype(MXU_DTYPE), dfg_ref[...].astype(MXU_DTYPE)], axis=1)
        dx_ref[...] = DN_ALPHA * dyp_ref[...] + _dot_nt(dz, w_ref[...])
        dw_sc[...] += _dot_tn(x_ref[...], dz)

        @pl.when(i == nt - 1)
        def _():
            dw_sc[:, :FOX_W] = dw_sc[:, :FOX_W] * (1.0 / math.sqrt(HEAD_DIM))
            cp = pltpu.make_async_copy(dw_sc, dw_hbm, sem)
            cp.start()
            cp.wait()

    row = lambda i: (i, 0)
    const = lambda i: (0, 0)
    return _pcall(
        body, name=name, grid=(nt,),
        in_specs=[pl.BlockSpec((tm, HEADS * LANES), row), pl.BlockSpec((tm, HEADS * LANES), row),
                  pl.BlockSpec((tm, HEADS * LANES), row),
                  pl.BlockSpec((tm, 2 * LRU_W), row), pl.BlockSpec((tm, LANES), row),
                  pl.BlockSpec((tm, D_MODEL), row), pl.BlockSpec((tm, D_MODEL), row),
                  pl.BlockSpec((D_MODEL, Z_PAD), const)],
        out_specs=[pl.BlockSpec((tm, D_MODEL), row), pl.BlockSpec(memory_space=pl.ANY)],
        out_shape=[jax.ShapeDtypeStruct((T, D_MODEL), f32), jax.ShapeDtypeStruct((D_MODEL, Z_PAD), f32)],
        scratch_shapes=[pltpu.VMEM((D_MODEL, Z_PAD), f32), pltpu.SemaphoreType.DMA],
        compiler_params=_params(1),
    )(dqa, dka, dva, dlxg, dfg, xn, dyp, wp)


def _split3(x):
    hi = x.astype(jnp.bfloat16)
    r1 = x - hi.astype(f32)
    mid = r1.astype(jnp.bfloat16)
    lo = (r1 - mid.astype(f32)).astype(jnp.bfloat16)
    return hi, mid, lo


def _tri_dot(tri, x):
    hi, mid, lo = _split3(x)
    return _dot(tri, hi) + _dot(tri, mid) + _dot(tri, lo)


FOX_PAD = HEADS * LANES
AUX = HEAD_DIM


def _low_lanes(shape):
    return lax.broadcasted_iota(jnp.int32, shape, 1) < HEAD_DIM


def _swap_lane_halves(x):
    return pltpu.roll(x, HEAD_DIM, 1)


def _fox_prep(qkv, fgb, *, name, tm=512):
    T = fgb.shape[0]
    tm = min(tm, T)
    nt = T // tm

    def body(qkv_ref, fg_ref, qa_ref, ka_ref, va_ref, carry):
        i = pl.program_id(0)

        @pl.when(i == 0)
        def _():
            carry[...] = jnp.zeros_like(carry)

        x = fg_ref[...]
        ls = jnp.minimum(x, 0.0) - jnp.log(1.0 + jnp.exp(-jnp.abs(x)))
        r = lax.broadcasted_iota(jnp.int32, (tm, tm), 0)
        c = lax.broadcasted_iota(jnp.int32, (tm, tm), 1)
        tri = jnp.where(r >= c, 1.0, 0.0).astype(jnp.bfloat16)
        cum = _tri_dot(tri, ls) + carry[0:1, :]
        carry[...] = jnp.broadcast_to(cum[tm - 1:tm, :], carry.shape)

        lane = lax.broadcasted_iota(jnp.int32, (tm, LANES), 1)
        low = lane < HEAD_DIM
        ones_q = jnp.where(jnp.logical_and(lane >= AUX + 3, lane < AUX + 6), 1.0, 0.0)
        ones_k = jnp.where(jnp.logical_and(lane >= AUX, lane < AUX + 3), 1.0, 0.0)
        for j in range(HEADS // 2):
            pair = [qkv_ref[:, t * FOX_W + j * LANES:t * FOX_W + (j + 1) * LANES].astype(f32) for t in range(3)]
            for odd in range(2):
                h = 2 * j + odd
                q, k, v = [_swap_lane_halves(a) if odd else a for a in pair]
                hi, mid, lo = [a.astype(f32) for a in _split3(jnp.broadcast_to(cum[:, h:h + 1], (tm, LANES)))]
                aux_q = jnp.where(lane == AUX, hi, jnp.where(lane == AUX + 1, mid, jnp.where(lane == AUX + 2, lo, ones_q)))
                aux_k = jnp.where(lane == AUX + 3, -hi,
                                  jnp.where(lane == AUX + 4, -mid, jnp.where(lane == AUX + 5, -lo, ones_k)))
                blk = slice(h * LANES, (h + 1) * LANES)
                qa_ref[:, blk] = jnp.where(low, q, aux_q).astype(qa_ref.dtype)
                ka_ref[:, blk] = jnp.where(low, k, aux_k).astype(ka_ref.dtype)
                va_ref[:, blk] = jnp.where(low, v, 1.0).astype(va_ref.dtype)

    row = lambda i: (i, 0)
    return _pcall(
        body, name=name, grid=(nt,),
        in_specs=[pl.BlockSpec((tm, QKV_W), row), pl.BlockSpec((tm, LANES), row)],
        out_specs=[pl.BlockSpec((tm, FOX_PAD), row)] * 3,
        out_shape=[jax.ShapeDtypeStruct((T, FOX_PAD), MXU_DTYPE)] * 3,
        scratch_shapes=[pltpu.VMEM((8, LANES), f32)],
        compiler_params=_params(1),
    )(qkv, fgb)


def _future_keys(tq, tk):
    r = lax.broadcasted_iota(jnp.int32, (tq, tk), 0)
    c = lax.broadcasted_iota(jnp.int32, (tq, tk), 1)
    return c > r


def _causal_steps(nq, key_major):
    if key_major:
        pairs = [(qi, ki) for ki in range(nq) for qi in range(ki, nq)]
    else:
        pairs = [(qi, ki) for qi in range(nq) for ki in range(qi + 1)]
    return (jnp.asarray([p[0] for p in pairs], jnp.int32), jnp.asarray([p[1] for p in pairs], jnp.int32))


def _fox_fwd(qa, ka, va, *, name, tq=512, hps=8):
    T = qa.shape[0]
    tq = min(tq, T)
    tk = tq
    nq = T // tq
    rep = tk // LANES
    qi_tab, ki_tab = _causal_steps(nq, key_major=False)

    def body(qi_ref, ki_ref, qa_ref, ka_ref, va_ref, o_ref, lse_ref, m_sc, acc_sc):
        t = pl.program_id(1)
        qi = qi_ref[t]
        ki = ki_ref[t]

        @pl.when(ki == 0)
        def _():
            m_sc[...] = jnp.full_like(m_sc, NEG_BIG)
            acc_sc[...] = jnp.zeros_like(acc_sc)

        def tile(diagonal):
            for h in range(hps):
                blk = slice(h * LANES, (h + 1) * LANES)
                s = _dot_nt(qa_ref[:, blk], ka_ref[:, blk])
                if diagonal:
                    s = jnp.where(_future_keys(tq, tk), NEG_BIG, s)
                m_prev = m_sc[h]
                m_new = jnp.maximum(m_prev, jnp.max(s, axis=1, keepdims=True))
                p = jnp.exp(s - jnp.tile(m_new, (1, rep)))
                acc_sc[h] = jnp.exp(m_prev - m_new) * acc_sc[h] + _dot(p.astype(MXU_DTYPE), va_ref[:, blk])
                m_sc[h] = m_new

        @pl.when(ki < qi)
        def _():
            tile(False)

        @pl.when(ki == qi)
        def _():
            tile(True)
            low = _low_lanes((tq, LANES))
            outs = []
            for h in range(hps):
                acc = acc_sc[h]
                den = _swap_lane_halves(acc)
                outs.append(acc / den)
                lse_ref[h] = m_sc[h] + jnp.log(jnp.where(low, den, acc))
            for p in range(hps // 2):
                o_ref[:, p * LANES:(p + 1) * LANES] = jnp.where(low, outs[2 * p], _swap_lane_halves(outs[2 * p + 1]))

    pair = hps * LANES
    return _pcall(
        body, name=name,
        grid_spec=pltpu.PrefetchScalarGridSpec(
            num_scalar_prefetch=2, grid=(HEADS // hps, qi_tab.shape[0]),
            in_specs=[
                pl.BlockSpec((tq, pair), lambda j, t, qi_ref, ki_ref: (qi_ref[t], j)),
                pl.BlockSpec((tk, pair), lambda j, t, qi_ref, ki_ref: (ki_ref[t], j)),
                pl.BlockSpec((tk, pair), lambda j, t, qi_ref, ki_ref: (ki_ref[t], j)),
            ],
            out_specs=[pl.BlockSpec((tq, pair // 2), lambda j, t, qi_ref, ki_ref: (qi_ref[t], j)),
                       pl.BlockSpec((hps, tq, LANES), lambda j, t, qi_ref, ki_ref: (j, qi_ref[t], 0))],
            scratch_shapes=[pltpu.VMEM((hps, tq, LANES), f32)] * 2),
        out_shape=[jax.ShapeDtypeStruct((T, FOX_W), f32), jax.ShapeDtypeStruct((HEADS, T, LANES), f32)],
        compiler_params=_params(2),
    )(qi_tab, ki_tab, qa, ka, va)


def _fox_bwd_prep(do, o, *, name, tm=512):
    T = o.shape[0]
    tm = min(tm, T)
    nt = T // tm

    def body(do_ref, o_ref, d_ref, doa_ref):
        low = _low_lanes((tm, LANES))
        for j in range(HEADS // 2):
            do2 = do_ref[:, j * LANES:(j + 1) * LANES].astype(f32)
            prod = do2 * o_ref[:, j * LANES:(j + 1) * LANES]
            for odd in range(2):
                h = 2 * j + odd
                mine = jnp.where(low, _swap_lane_halves(prod) if odd else prod, 0.0)
                d_ref[h] = jnp.broadcast_to(jnp.sum(mine, axis=1, keepdims=True), (tm, LANES))
                doh = jnp.where(low, _swap_lane_halves(do2) if odd else do2, 0.0)
                doa_ref[:, h * LANES:(h + 1) * LANES] = doh.astype(doa_ref.dtype)

    return _pcall(
        body, name=name, grid=(nt,),
        in_specs=[pl.BlockSpec((tm, FOX_W), lambda i: (i, 0)), pl.BlockSpec((tm, FOX_W), lambda i: (i, 0))],
        out_specs=[pl.BlockSpec((HEADS, tm, LANES), lambda i: (0, i, 0)), pl.BlockSpec((tm, FOX_PAD), lambda i: (i, 0))],
        out_shape=[jax.ShapeDtypeStruct((HEADS, T, LANES), f32), jax.ShapeDtypeStruct((T, FOX_PAD), MXU_DTYPE)],
        compiler_params=_params(1),
    )(do, o)


def _fox_bwd(qa, ka, va, doa, lse, drep, *, name, tq=512, hps=4):
    T = qa.shape[0]
    tq = min(tq, T)
    tk = tq
    nq = T // tq
    rep = tk // LANES
    qi_tab, ki_tab = _causal_steps(nq, key_major=True)

    def body(qi_ref, ki_ref, qa_ref, ka_ref, va_ref, doa_ref, lse_ref, d_ref, dqa_ref, dka_ref, dva_ref, dk_sc, dv_sc):
        t = pl.program_id(1)
        qi = qi_ref[t]
        ki = ki_ref[t]
        rows = pl.ds(pl.multiple_of(qi * tq, tq), tq)

        @pl.when(t == 0)
        def _():
            dqa_ref[...] = jnp.zeros_like(dqa_ref)

        @pl.when(qi == ki)
        def _():
            dk_sc[...] = jnp.zeros_like(dk_sc)
            dv_sc[...] = jnp.zeros_like(dv_sc)

        def tile(diagonal):
            for h in range(hps):
                blk = slice(h * LANES, (h + 1) * LANES)
                qh, kh, doh = qa_ref[:, blk], ka_ref[:, blk], doa_ref[:, blk]
                p = jnp.exp(_dot_nt(qh, kh) - jnp.tile(lse_ref[h], (1, rep)))
                if diagonal:
                    p = jnp.where(_future_keys(tq, tk), 0.0, p)
                dp = _dot_nt(doh, va_ref[:, blk])
                ds = (p * (dp - jnp.tile(d_ref[h], (1, rep)))).astype(MXU_DTYPE)
                dv_sc[h] += _dot_tn(p.astype(MXU_DTYPE), doh)
                dk_sc[h] += _dot_tn(ds, qh)
                dqa_ref[rows, blk] += _dot(ds, kh)

        @pl.when(qi > ki)
        def _():
            tile(False)

        @pl.when(qi == ki)
        def _():
            tile(True)

        @pl.when(qi == nq - 1)
        def _():
            for h in range(hps):
                blk = slice(h * LANES, (h + 1) * LANES)
                dka_ref[:, blk] = dk_sc[h]
                dva_ref[:, blk] = dv_sc[h]

    pair = hps * LANES
    q_blk = lambda j, t, qi_ref, ki_ref: (qi_ref[t], j)
    k_blk = lambda j, t, qi_ref, ki_ref: (ki_ref[t], j)
    stat = pl.BlockSpec((hps, tq, LANES), lambda j, t, qi_ref, ki_ref: (j, qi_ref[t], 0))
    return _pcall(
        body, name=name,
        grid_spec=pltpu.PrefetchScalarGridSpec(
            num_scalar_prefetch=2, grid=(HEADS // hps, qi_tab.shape[0]),
            in_specs=[pl.BlockSpec((tq, pair), q_blk), pl.BlockSpec((tk, pair), k_blk), pl.BlockSpec((tk, pair), k_blk),
                      pl.BlockSpec((tq, pair), q_blk), stat, stat],
            out_specs=[pl.BlockSpec((T, pair), lambda j, t, qi_ref, ki_ref: (0, j)),
                       pl.BlockSpec((tk, pair), k_blk), pl.BlockSpec((tk, pair), k_blk)],
            scratch_shapes=[pltpu.VMEM((hps, tk, LANES), f32)] * 2),
        out_shape=[jax.ShapeDtypeStruct((T, FOX_PAD), f32)] * 3,
        compiler_params=_params(2),
    )(qi_tab, ki_tab, qa, ka, va, doa, lse, drep)


def _fox_bwd_post(dqa, dka, fgb, *, name, tm=512):
    T = fgb.shape[0]
    tm = min(tm, T)
    nt = T // tm

    def body(dqa_ref, dka_ref, fg_ref, dfg_ref, dbf_ref, carry):
        i = pl.program_id(0)

        @pl.when(i == 0)
        def _():
            carry[...] = jnp.zeros_like(carry)
            dbf_ref[...] = jnp.zeros_like(dbf_ref)

        lane = lax.broadcasted_iota(jnp.int32, (tm, LANES), 1)
        dc = jnp.zeros((tm, LANES), f32)
        for h in range(HEADS):
            row_sum = dqa_ref[:, h * LANES + AUX:h * LANES + AUX + 1]
            col_sum = dka_ref[:, h * LANES + AUX + 3:h * LANES + AUX + 4]
            dc = jnp.where(lane == h, jnp.broadcast_to(row_sum - col_sum, (tm, LANES)), dc)
        r = lax.broadcasted_iota(jnp.int32, (tm, tm), 0)
        c = lax.broadcasted_iota(jnp.int32, (tm, tm), 1)
        tri = jnp.where(c >= r, 1.0, 0.0).astype(jnp.bfloat16)
        dls = _tri_dot(tri, dc) + carry[0:1, :]
        carry[...] = jnp.broadcast_to(dls[0:1, :], carry.shape)
        dfg = dls * _sigmoid(-fg_ref[...])
        dfg_ref[...] = dfg
        dbf_ref[...] += jnp.sum(dfg, axis=0, keepdims=True)

    rev = lambda i: (nt - 1 - i, 0)
    return _pcall(
        body, name=name, grid=(nt,),
        in_specs=[pl.BlockSpec((tm, FOX_PAD), rev), pl.BlockSpec((tm, FOX_PAD), rev), pl.BlockSpec((tm, LANES), rev)],
        out_specs=[pl.BlockSpec((tm, LANES), rev), pl.BlockSpec((1, LANES), lambda i: (0, 0))],
        out_shape=[jax.ShapeDtypeStruct((T, LANES), f32), jax.ShapeDtypeStruct((1, LANES), f32)],
        scratch_shapes=[pltpu.VMEM((8, LANES), f32)],
        compiler_params=_params(1),
    )(dqa, dka, fgb)


GELU_C = math.sqrt(2.0 / math.pi)
GELU_A = 0.044715


def _gelu(x):
    t = jnp.tanh(GELU_C * (x + GELU_A * x * x * x))
    return 0.5 * x * (1.0 + t), t


def _gelu_grad(x, t):
    return 0.5 * (1.0 + t) + 0.5 * x * (1.0 - t * t) * GELU_C * (1.0 + 3.0 * GELU_A * x * x)


def _expm1(x):
    e = jnp.exp(x)
    safe = jnp.where(e == 1.0, x, (e - 1.0) * x / jnp.log(jnp.where(e == 1.0, 0.5, e)))
    return jnp.where(x < -0.5, e - 1.0, safe)


def _lru_gates(u, wab_ref, bab_ref, lam_ref):
    pre = _dot(u.astype(MXU_DTYPE), wab_ref[...]) + bab_ref[...]
    r = _sigmoid(pre[:, :LRU_W])
    gi = _sigmoid(pre[:, LRU_W:])
    lam = lam_ref[...]
    sp = jnp.maximum(-lam, 0.0) + jnp.log(1.0 + jnp.exp(-jnp.abs(lam)))
    log_a = -LRU_C * r * sp
    a = jnp.exp(log_a)
    s = jnp.sqrt(-_expm1(2.0 * log_a))
    return r, gi, sp, a, s


def _lru_fwd(lxg, conv_w, conv_b, wab, bab, lam, *, name, tc=512):
    T = lxg.shape[0]
    tc = min(tc, T)
    nc = T // tc

    def body(lx_ref, lg_ref, cw_ref, cb_ref, wab_ref, bab_ref, lam_ref,
             out_ref, u_ref, hs_ref, ext, a_sc, b_sc, h_sc):
        i = pl.program_id(0)

        @pl.when(i == 0)
        def _():
            ext[0:8, :] = jnp.zeros((8, LRU_W), f32)
            h_sc[...] = jnp.zeros_like(h_sc)

        ext[8:, :] = lx_ref[...]
        u = cb_ref[...] + cw_ref[0:1, :] * ext[pl.ds(5, tc), :]
        for k in range(1, CONV_K):
            u = u + cw_ref[k:k + 1, :] * ext[pl.ds(5 + k, tc), :]
        ext[0:8, :] = ext[tc:tc + 8, :]
        u_ref[...] = u
        r, gi, sp, a, s = _lru_gates(u, wab_ref, bab_ref, lam_ref)
        a_sc[...] = a
        b_sc[...] = s * (gi * u)

        def step(t, h):
            h = a_sc[pl.ds(t, 1), :] * h + b_sc[pl.ds(t, 1), :]
            hs_ref[pl.ds(t, 1), :] = h
            return h

        h = lax.fori_loop(0, tc, step, h_sc[0:1, :], unroll=8)
        h_sc[...] = jnp.broadcast_to(h, h_sc.shape)
        gel, _ = _gelu(lg_ref[...])
        out_ref[...] = gel * hs_ref[...]

    row = lambda i: (i, 0)
    const = lambda i: (0, 0)
    return _pcall(
        body, name=name, grid=(nc,),
        in_specs=[pl.BlockSpec((tc, LRU_W), row), pl.BlockSpec((tc, LRU_W), lambda i: (i, 1)),
                  pl.BlockSpec((CONV_K, LRU_W), const), pl.BlockSpec((1, LRU_W), const),
                  pl.BlockSpec((LRU_W, 2 * LRU_W), const), pl.BlockSpec((1, 2 * LRU_W), const),
                  pl.BlockSpec((1, LRU_W), const)],
        out_specs=[pl.BlockSpec((tc, LRU_W), row)] * 3,
        out_shape=[jax.ShapeDtypeStruct((T, LRU_W), f32)] * 3,
        scratch_shapes=[pltpu.VMEM((tc + 8, LRU_W), f32), pltpu.VMEM((tc, LRU_W), f32),
                        pltpu.VMEM((tc, LRU_W), f32), pltpu.VMEM((8, LRU_W), f32)],
        compiler_params=_params(1),
    )(lxg, lxg, conv_w, conv_b, wab, bab, lam)


def _lru_bwd(dlru, lxg, u, hs, conv_w, wab, bab, lam, *, name, tc=512):
    T = lxg.shape[0]
    tc = min(tc, T)
    nc = T // tc
    bp = tc // 8

    def body(dl_ref, lx_ref, lxp_ref, lg_ref, u_ref, hs_ref, hsp_ref, cw_ref, wab_ref, bab_ref, lam_ref,
             dlxg_ref, dwab_ref, dbab_ref, dcw_ref, dcb_ref, dlam_ref,
             dh_sc, a_sc, ext, du_ext, carry):
        i = pl.program_id(0)
        first_chunk = i == nc - 1

        @pl.when(i == 0)
        def _():
            dwab_ref[...] = jnp.zeros_like(dwab_ref)
            dbab_ref[...] = jnp.zeros_like(dbab_ref)
            dcw_ref[...] = jnp.zeros_like(dcw_ref)
            dcb_ref[...] = jnp.zeros_like(dcb_ref)
            dlam_ref[...] = jnp.zeros_like(dlam_ref)
            carry[...] = jnp.zeros_like(carry)
            du_ext[tc:tc + 8, :] = jnp.zeros((8, LRU_W), f32)

        lg = lg_ref[...]
        gel, th = _gelu(lg)
        dl = dl_ref[...]
        hs = hs_ref[...]
        dlg = dl * hs * _gelu_grad(lg, th)
        u = u_ref[...]
        r, gi, sp, a, s = _lru_gates(u, wab_ref, bab_ref, lam_ref)
        a_sc[...] = a
        dh_sc[...] = dl * gel

        def step(k, c):
            t = tc - 1 - k
            dh = dh_sc[pl.ds(t, 1), :] + c
            dh_sc[pl.ds(t, 1), :] = dh
            return a_sc[pl.ds(t, 1), :] * dh

        c = lax.fori_loop(0, tc, step, carry[0:1, :], unroll=8)
        carry[...] = jnp.broadcast_to(c, carry.shape)

        ext[0:8, :] = jnp.where(first_chunk, 0.0, hsp_ref[...])
        ext[8:, :] = hs
        hprev = ext[pl.ds(7, tc), :]
        dh = dh_sc[...]
        da = dh * hprev
        giu = gi * u
        dla = da * a - (dh * giu) * (a * a / s)
        dgi = dh * s * u
        du = dh * s * gi
        dr = dla * (-LRU_C * sp)
        dlam_ref[...] += jnp.sum(dla * (-LRU_C * r), axis=0, keepdims=True) * (-_sigmoid(-lam_ref[...]))
        dpre = jnp.concatenate([dr * r * (1.0 - r), dgi * gi * (1.0 - gi)], axis=1)
        dpre_b = dpre.astype(MXU_DTYPE)
        du = du + _dot_nt(dpre_b, wab_ref[...])
        dwab_ref[...] += _dot_tn(u.astype(MXU_DTYPE), dpre_b)
        dbab_ref[...] += jnp.sum(dpre, axis=0, keepdims=True)
        dcb_ref[...] += jnp.sum(du, axis=0, keepdims=True)

        du_ext[0:tc, :] = du
        dlx = cw_ref[0:1, :] * du_ext[pl.ds(3, tc), :]
        for k in range(1, CONV_K):
            dlx = dlx + cw_ref[k:k + 1, :] * du_ext[pl.ds(3 - k, tc), :]
        du_ext[tc:tc + 8, :] = du_ext[0:8, :]
        ext[0:8, :] = jnp.where(first_chunk, 0.0, lxp_ref[...])
        ext[8:, :] = lx_ref[...]
        for k in range(CONV_K):
            dcw_ref[k:k + 1, :] += jnp.sum(du * ext[pl.ds(5 + k, tc), :], axis=0, keepdims=True)
        dlxg_ref[:, :LRU_W] = dlx.astype(dlxg_ref.dtype)
        dlxg_ref[:, LRU_W:] = dlg.astype(dlxg_ref.dtype)

    rev = lambda i: (nc - 1 - i, 0)
    prev8 = lambda i: (jnp.maximum((nc - 1 - i) * bp - 1, 0), 0)
    const = lambda i: (0, 0)
    return _pcall(
        body, name=name, grid=(nc,),
        in_specs=[
            pl.BlockSpec((tc, LRU_W), rev),
            pl.BlockSpec((tc, LRU_W), rev),
            pl.BlockSpec((8, LRU_W), prev8),
            pl.BlockSpec((tc, LRU_W), lambda i: (nc - 1 - i, 1)),
            pl.BlockSpec((tc, LRU_W), rev),
            pl.BlockSpec((tc, LRU_W), rev),
            pl.BlockSpec((8, LRU_W), prev8),
            pl.BlockSpec((CONV_K, LRU_W), const),
            pl.BlockSpec((LRU_W, 2 * LRU_W), const),
            pl.BlockSpec((1, 2 * LRU_W), const),
            pl.BlockSpec((1, LRU_W), const),
        ],
        out_specs=[
            pl.BlockSpec((tc, 2 * LRU_W), rev),
            pl.BlockSpec((LRU_W, 2 * LRU_W), const),
            pl.BlockSpec((1, 2 * LRU_W), const),
            pl.BlockSpec((8, LRU_W), const),
            pl.BlockSpec((1, LRU_W), const),
            pl.BlockSpec((1, LRU_W), const),
        ],
        out_shape=[
            jax.ShapeDtypeStruct((T, 2 * LRU_W), MXU_DTYPE),
            jax.ShapeDtypeStruct((LRU_W, 2 * LRU_W), f32),
            jax.ShapeDtypeStruct((1, 2 * LRU_W), f32),
            jax.ShapeDtypeStruct((8, LRU_W), f32),
            jax.ShapeDtypeStruct((1, LRU_W), f32),
            jax.ShapeDtypeStruct((1, LRU_W), f32),
        ],
        scratch_shapes=[pltpu.VMEM((tc, LRU_W), f32), pltpu.VMEM((tc, LRU_W), f32),
                        pltpu.VMEM((tc + 8, LRU_W), f32), pltpu.VMEM((tc + 8, LRU_W), f32),
                        pltpu.VMEM((8, LRU_W), f32)],
        compiler_params=_params(1),
    )(dlru, lxg, lxg, lxg, u, hs, hs, conv_w, wab, bab, lam)


def _mix_out(fox, lru, wo, xhat1, g1, b1, g2, b2, *, name, tm=512):
    T = fox.shape[0]
    tm = min(tm, T)
    nt = T // tm

    def body(fox_ref, lru_ref, wo_ref, xh_ref, g1_ref, b1_ref, g2_ref, b2_ref, xhat_ref, xn_ref, rstd_ref):
        mix = _dot(fox_ref[...].astype(MXU_DTYPE), wo_ref[:FOX_W, :])
        mix = mix + _dot(lru_ref[...].astype(MXU_DTYPE), wo_ref[FOX_W:, :])
        x1 = xh_ref[...] * g1_ref[...] + b1_ref[...]
        xhat, rstd = _layer_norm_stats(DN_ALPHA * x1 + mix)
        xhat_ref[...] = xhat
        xn_ref[...] = xhat * g2_ref[...] + b2_ref[...]
        rstd_ref[...] = jnp.broadcast_to(rstd, rstd_ref.shape)

    row = lambda i: (i, 0)
    const = lambda i: (0, 0)
    vec = pl.BlockSpec((1, D_MODEL), const)
    return _pcall(
        body, name=name, grid=(nt,),
        in_specs=[pl.BlockSpec((tm, FOX_W), row), pl.BlockSpec((tm, LRU_W), row),
                  pl.BlockSpec((D_MODEL, D_MODEL), const), pl.BlockSpec((tm, D_MODEL), row), vec, vec, vec, vec],
        out_specs=[pl.BlockSpec((tm, D_MODEL), row), pl.BlockSpec((tm, D_MODEL), row),
                   pl.BlockSpec((tm, LANES), row)],
        out_shape=[jax.ShapeDtypeStruct((T, D_MODEL), f32), jax.ShapeDtypeStruct((T, D_MODEL), f32),
                   jax.ShapeDtypeStruct((T, LANES), f32)],
        compiler_params=_params(1),
    )(fox, lru, wo, xhat1, g1, b1, g2, b2)


def _mix_out_bwd(dyp, fox, lru, wo, *, name, tm=512):
    T = fox.shape[0]
    tm = min(tm, T)
    nt = T // tm

    def body(dyp_ref, fox_ref, lru_ref, wo_ref, dfox_ref, dlru_ref, dwo_ref):
        i = pl.program_id(0)

        @pl.when(i == 0)
        def _():
            dwo_ref[...] = jnp.zeros_like(dwo_ref)

        dmix = dyp_ref[...].astype(MXU_DTYPE)
        dcat = _dot_nt(dmix, wo_ref[...])
        dfox_ref[...] = dcat[:, :FOX_W].astype(dfox_ref.dtype)
        dlru_ref[...] = dcat[:, FOX_W:]
        dwo_ref[:FOX_W, :] += _dot_tn(fox_ref[...].astype(MXU_DTYPE), dmix)
        dwo_ref[FOX_W:, :] += _dot_tn(lru_ref[...].astype(MXU_DTYPE), dmix)

    row = lambda i: (i, 0)
    const = lambda i: (0, 0)
    return _pcall(
        body, name=name, grid=(nt,),
        in_specs=[pl.BlockSpec((tm, D_MODEL), row), pl.BlockSpec((tm, FOX_W), row), pl.BlockSpec((tm, LRU_W), row),
                  pl.BlockSpec((D_MODEL, D_MODEL), const)],
        out_specs=[pl.BlockSpec((tm, FOX_W), row), pl.BlockSpec((tm, LRU_W), row),
                   pl.BlockSpec((D_MODEL, D_MODEL), const)],
        out_shape=[jax.ShapeDtypeStruct((T, FOX_W), MXU_DTYPE), jax.ShapeDtypeStruct((T, LRU_W), f32),
                   jax.ShapeDtypeStruct((D_MODEL, D_MODEL), f32)],
        compiler_params=_params(1),
    )(dyp, fox, lru, wo)


def make_wp(w_in):
    scale = jnp.concatenate([jnp.full((FOX_W,), 1.0 / math.sqrt(HEAD_DIM), w_in.dtype),
                             jnp.ones((IN_COLS - FOX_W,), w_in.dtype)])
    return jnp.pad(w_in * scale[None, :], ((0, 0), (0, Z_PAD - IN_COLS)))


def _block_diag(w):
    eye = jnp.eye(HEADS, dtype=w.dtype)
    return jnp.einsum("hij,hg->higj", w, eye).reshape(LRU_W, LRU_W)


def _block_diag_extract(m):
    m4 = m.reshape(HEADS, HEAD_DIM, HEADS, HEAD_DIM)
    return jnp.stack([m4[h, :, h, :] for h in range(HEADS)])


class _NoOverlap:
    def start_token(self):
        return None

    def up_weight(self, w, after):
        return w["f1u"]

    def late_weights(self, w, after):
        return dict(f1d=w["f1d"], wp=w["wp"], wo=w["wo"])

    def after_attention(self, after):
        return None

    def ffn2_weights(self, w, after):
        return w["f2g"], w["f2u"], w["f2d"]

    def ffn2_grads(self, grads):
        return None

    def ffn1_grads(self, grads):
        return None

    def mixer_grads(self, dwp, dwo, small, loss):
        return None

    def before_ffn1_bwd(self, after):
        return None


def _tied(a, token):
    return a if token is None else a + token[0, 0]


def _local_step(x, target, w, hooks=None):
    hooks = hooks or _NoOverlap()
    bfp = w["bfp"]
    wab = jnp.concatenate([_block_diag(w["rg_wa"]), _block_diag(w["rg_wx"])], axis=1).astype(MXU_DTYPE)
    bab = jnp.concatenate([w["rg_ba"].reshape(1, LRU_W), w["rg_bx"].reshape(1, LRU_W)], axis=1)

    xb0, g1a = _ffn_gate(x, w["f1g"], hooks.start_token(), name="ffn1_gate")
    f1u = hooks.up_weight(w, [g1a])
    u1a, h1a = _ffn_up(xb0, g1a, f1u, name="ffn1_up")
    late = hooks.late_weights(w, [h1a])
    f1d, wp, wo = late["f1d"], late["wp"], late["wo"]
    xhat1, xn1, rstd1 = _ffn_down_ln(x, h1a, f1d, w["ln1_g"], w["ln1_b"], name="ffn1_down")
    qkv, lxg, fgb = _proj_in(xn1, wp, bfp, name="proj_in")
    qa, ka, va = _fox_prep(qkv, fgb, name="fox_prep")
    fox, lse = _fox_fwd(qa, ka, va, name="fox_fwd")
    token = hooks.after_attention([lse])
    lru, uconv, hs = _lru_fwd(lxg, w["conv_w"], _tied(w["conv_b"], token), wab, bab, w["lam"], name="lru_fwd")
    xhat2, x2, rstd2 = _mix_out(fox, lru, wo, xhat1, w["ln1_g"], w["ln1_b"], w["ln2_g"], w["ln2_b"], name="mix_out")
    f2g, f2u, f2d = hooks.ffn2_weights(w, [rstd2])
    xb2, g2a, u2a, xhat3, _, rstd3 = _ffn_fwd(x2, f2g, f2u, f2d, w["ln3_g"], w["ln3_b"], name="ffn2_fwd")

    dy3p, dln3g, dln3b, loss = _loss_ln_bwd(xhat3, rstd3, w["ln3_g"], w["ln3_b"], target, name="loss_ln3_bwd")
    dx2, df2g, df2u, df2d = _ffn_bwd(dy3p, xb2, g2a, u2a, f2g, f2u, f2d, name="ffn2_bwd")
    token = hooks.ffn2_grads([df2g, df2u, df2d])
    dy2p, dln2g, dln2b = _ln_bwd(dx2, xhat2, rstd2, _tied(w["ln2_g"], token), name="ln2_bwd")
    dfox, dlru, dwo = _mix_out_bwd(dy2p, fox, lru, wo, name="mix_out_bwd")
    dlxg, dwab, dbab, dcw, dcb, dlam = _lru_bwd(dlru, lxg, uconv, hs, w["conv_w"], wab, bab, w["lam"], name="lru_bwd")
    drep, doa = _fox_bwd_prep(dfox, fox, name="fox_bwd_prep")
    dqa, dka, dva = _fox_bwd(qa, ka, va, doa, lse, drep, name="fox_bwd")
    dfg, dbf = _fox_bwd_post(dqa, dka, fgb, name="fox_bwd_post")
    dx1, dwp = _proj_in_bwd(dqa, dka, dva, dlxg, dfg, xn1, dy2p, wp, name="proj_in_bwd")
    dy1p, dln1g, dln1b = _ln_bwd(dx1, xhat1, rstd1, w["ln1_g"], name="ln1_bwd")
    small = dict(
        ln1_g=dln1g, ln1_b=dln1b, ln2_g=dln2g, ln2_b=dln2b, ln3_g=dln3g, ln3_b=dln3b,
        b_forget=dbf[:, :HEADS], conv_w=dcw[:CONV_K], conv_b=dcb,
        rg_wa=_block_diag_extract(dwab[:, :LRU_W]), rg_wx=_block_diag_extract(dwab[:, LRU_W:]),
        rg_ba=dbab[:, :LRU_W].reshape(HEADS, HEAD_DIM), rg_bx=dbab[:, LRU_W:].reshape(HEADS, HEAD_DIM),
        lru_lambda=dlam,
    )
    hooks.before_ffn1_bwd([dln1b])
    token = hooks.mixer_grads(dwp, dwo, small, loss)
    dx_a, *grads_a = _ffn_bwd(dy1p, xb0, g1a, u1a, w["f1g"], f1u, f1d, token, name="ffn1_bwd_a", part=0)
    token = hooks.ffn1_grads(grads_a)
    dx, *grads_b = _ffn_bwd(dy1p, xb0, g1a, u1a, w["f1g"], f1u, f1d, token, name="ffn1_bwd_b", part=1,
                            dx_init=dx_a)

    grads = dict(f1=(grads_a, grads_b), f2g=df2g, f2u=df2u, f2d=df2d, wp=dwp, wo=dwo, **small)
    return loss, dx, grads


MESH = pl.DeviceIdType.MESH
HBM_SPEC = pl.BlockSpec(memory_space=pl.ANY)
VMEM_SPEC = pl.BlockSpec(memory_space=pltpu.VMEM)


def _position():
    return lax.axis_index("x"), lax.axis_index("y"), lax.axis_index("c")


def _other_chips(x, y):
    return [(1 - x, y), (x, 1 - y), (1 - x, 1 - y)]


def _all_gather_bf16(shards, *, name):
    n = len(shards)

    def body(*refs):
        ins, outs, stages = refs[:n], refs[n:2 * n], refs[2 * n:3 * n]
        send_sems, recv_sems, local_sems = refs[3 * n:]
        x, y, c = _position()
        me, sibling = (x, y, c), (x, y, 1 - c)
        chips = _other_chips(x, y)

        def rows(k, px, py, pc):
            r = shards[k].shape[0]
            m = r // 2
            return outs[k].at[pl.ds(pl.multiple_of((2 * px + py) * r + pc * m, 16), m), :]

        def copy(k, idx, block, to, src=None):
            return pltpu.make_async_remote_copy(
                src_ref=rows(k, *block) if src is None else src, dst_ref=rows(k, *block),
                send_sem=send_sems.at[7 * k + idx], recv_sem=recv_sems.at[7 * k + idx],
                device_id=to, device_id_type=MESH)

        started = []
        mine = []
        for k in range(n):
            m = shards[k].shape[0] // 2
            stages[k][...] = ins[k][pl.ds(pl.multiple_of(c * m, 16), m), :].astype(stages[k].dtype)
            cp = pltpu.make_async_copy(stages[k], rows(k, *me), local_sems.at[k])
            cp.start()
            mine.append(cp)
            first = [copy(k, 0, me, sibling, src=stages[k])]
            first += [copy(k, 1 + j, me, (*chip, c), src=stages[k]) for j, chip in enumerate(chips)]
            for cp in first:
                cp.start()
            started += first
        for k in range(n):
            for j, chip in enumerate(chips):
                copy(k, 1 + j, (*chip, c), me).wait_recv()
                fwd = copy(k, 4 + j, (*chip, c), sibling)
                fwd.start()
                started.append(fwd)
        for k in range(n):
            copy(k, 0, sibling, me).wait_recv()
            for j, chip in enumerate(chips):
                copy(k, 4 + j, (*chip, 1 - c), me).wait_recv()
        for cp in started:
            cp.wait_send()
        for cp in mine:
            cp.wait()

    return _pcall(
        body, name=name,
        in_specs=[VMEM_SPEC] * n, out_specs=[HBM_SPEC] * n,
        out_shape=[jax.ShapeDtypeStruct((N_SHARD * s.shape[0], s.shape[1]), MXU_DTYPE) for s in shards],
        scratch_shapes=[pltpu.VMEM((s.shape[0] // 2, s.shape[1]), MXU_DTYPE) for s in shards]
        + [pltpu.SemaphoreType.DMA((7 * n,)), pltpu.SemaphoreType.DMA((7 * n,)), pltpu.SemaphoreType.DMA((n,))],
        compiler_params=pltpu.CompilerParams(vmem_limit_bytes=VMEM_LIMIT),
    )(*shards)


def _swap_halves(gs, *, name):
    n = len(gs)

    def body(*refs):
        ins, outs = refs[:n], refs[n:2 * n]
        send_sems, recv_sems = refs[2 * n:]
        x, y, c = _position()
        cps = []
        for k in range(n):
            m = gs[k].shape[1] // 2
            src = ins[k].at[:, pl.ds(pl.multiple_of((1 - c) * m, 16), m), :]
            cp = pltpu.make_async_remote_copy(src_ref=src, dst_ref=outs[k], send_sem=send_sems.at[k],
                                              recv_sem=recv_sems.at[k], device_id=(x, y, 1 - c), device_id_type=MESH)
            cp.start()
            cps.append(cp)
        for cp in cps:
            cp.wait()

    return _pcall(
        body, name=name, in_specs=[HBM_SPEC] * n, out_specs=[HBM_SPEC] * n,
        out_shape=[jax.ShapeDtypeStruct((g.shape[0], g.shape[1] // 2, g.shape[2]), g.dtype) for g in gs],
        scratch_shapes=[pltpu.SemaphoreType.DMA((n,)), pltpu.SemaphoreType.DMA((n,))],
    )(*gs)


def _add_halves(gs, recvs, *, name, tm=256):
    n = len(gs)
    _, r, cdim = gs[0].shape
    m = r // 2
    tm = min(tm, m)
    nb = m // tm
    c_idx = lax.axis_index("c").astype(jnp.int32).reshape(1)

    def body(c_ref, *refs):
        for k in range(n):
            refs[2 * n + k][...] = (refs[k][...].astype(f32) + refs[n + k][...].astype(f32)).astype(refs[2 * n + k].dtype)

    mine = pl.BlockSpec((None, tm, cdim), lambda j, i, c_ref: (j, c_ref[0] * nb + i, 0))
    half = pl.BlockSpec((None, tm, cdim), lambda j, i, c_ref: (j, i, 0))
    return _pcall(
        body, name=name,
        grid_spec=pltpu.PrefetchScalarGridSpec(
            num_scalar_prefetch=1, grid=(N_SHARD, nb),
            in_specs=[mine] * n + [half] * n, out_specs=[half] * n),
        out_shape=[jax.ShapeDtypeStruct((N_SHARD, m, cdim), g.dtype) for g in gs],
        compiler_params=_params(2),
    )(c_idx, *gs, *recvs)


def _scatter_partials(ps, *, name):
    n = len(ps)

    def body(*refs):
        ins, outs = refs[:n], refs[n:2 * n]
        send_sems, recv_sems = refs[2 * n:]
        x, y, c = _position()
        me_chip = 2 * x + y
        cps = []
        for k in range(n):
            for j, (px, py) in enumerate(_other_chips(x, y)):
                cp = pltpu.make_async_remote_copy(
                    src_ref=ins[k].at[2 * px + py], dst_ref=outs[k].at[me_chip],
                    send_sem=send_sems.at[3 * k + j], recv_sem=recv_sems.at[3 * k + j],
                    device_id=(px, py, c), device_id_type=MESH)
                cp.start()
                cps.append(cp)
        for cp in cps:
            cp.wait()

    return _pcall(
        body, name=name, in_specs=[HBM_SPEC] * n, out_specs=[HBM_SPEC] * n,
        out_shape=[jax.ShapeDtypeStruct(p.shape, p.dtype) for p in ps],
        scratch_shapes=[pltpu.SemaphoreType.DMA((3 * n,)), pltpu.SemaphoreType.DMA((3 * n,))],
    )(*ps)


def _sum_slabs(ps, qs, *, name, tm=128):
    n = len(qs)
    _, m, cdim = qs[0].shape
    tm = min(tm, m)
    nb = m // tm
    assert m % tm == 0, (m, tm)
    where = jnp.stack([2 * lax.axis_index("x") + lax.axis_index("y"), lax.axis_index("c")]).astype(jnp.int32)

    def body(w_ref, *refs):
        for k in range(n):
            own, q1, q2, q3 = (refs[4 * k + t][...].astype(f32) for t in range(4))
            refs[4 * n + k][...] = ((own + q1) + q2) + q3

    def slab(flip):
        return pl.BlockSpec((None, tm, cdim), lambda i, w_ref: (jnp.bitwise_xor(w_ref[0], flip), i, 0))

    operands = []
    for p, q in zip(ps, qs):
        operands += [p, q, q, q]
    return _pcall(
        body, name=name,
        grid_spec=pltpu.PrefetchScalarGridSpec(
            num_scalar_prefetch=1, grid=(nb,),
            in_specs=[slab(0), slab(2), slab(1), slab(3)] * n,
            out_specs=[pl.BlockSpec((tm, cdim), lambda i, w_ref: (w_ref[1] * nb + i, 0))] * n),
        out_shape=[jax.ShapeDtypeStruct((2 * m, cdim), f32) for _ in qs],
        compiler_params=_params(1),
    )(where, *operands)


def _join_halves(fs, *, name):
    n = len(fs)

    def body(*refs):
        outs = refs[n:2 * n]
        send_sems, recv_sems = refs[2 * n:]
        x, y, c = _position()
        cps = []
        for k in range(n):
            m = fs[k].shape[0] // 2
            half = outs[k].at[pl.ds(pl.multiple_of(c * m, 8), m), :]
            cp = pltpu.make_async_remote_copy(src_ref=half, dst_ref=half, send_sem=send_sems.at[k],
                                              recv_sem=recv_sems.at[k], device_id=(x, y, 1 - c), device_id_type=MESH)
            cp.start()
            cps.append(cp)
        for cp in cps:
            cp.wait()

    return _pcall(
        body, name=name, in_specs=[HBM_SPEC] * n, out_specs=[HBM_SPEC] * n,
        out_shape=[jax.ShapeDtypeStruct(f.shape, f.dtype) for f in fs],
        input_output_aliases={k: k for k in range(n)},
        scratch_shapes=[pltpu.SemaphoreType.DMA((n,)), pltpu.SemaphoreType.DMA((n,))],
    )(*fs)


def _all_reduce_small(v, after=None, *, name):
    r = v.shape[0]
    extra = [] if after is None else [after]

    def body(v_ref, *refs):
        out_ref, buf, send_sems, recv_sems, local_sem = refs[len(extra):]
        x, y, c = _position()
        me, sibling = (x, y, c), (x, y, 1 - c)
        chips = _other_chips(x, y)

        def rows(px, py, pc):
            return buf.at[pl.ds(pl.multiple_of((4 * px + 2 * py + pc) * r, 8), r), :]

        def copy(k, block, to, src=None):
            return pltpu.make_async_remote_copy(
                src_ref=rows(*block) if src is None else src, dst_ref=rows(*block),
                send_sem=send_sems.at[k], recv_sem=recv_sems.at[k], device_id=to, device_id_type=MESH)

        mine = pltpu.make_async_copy(v_ref, rows(*me), local_sem)
        mine.start()
        first = [copy(0, me, sibling, src=v_ref)]
        first += [copy(1 + j, me, (*chip, c), src=v_ref) for j, chip in enumerate(chips)]
        for cp in first:
            cp.start()
        passed = [copy(4 + j, (*chip, c), sibling) for j, chip in enumerate(chips)]
        for j, chip in enumerate(chips):
            copy(1 + j, (*chip, c), me).wait_recv()
            passed[j].start()
        copy(0, sibling, me).wait_recv()
        for j, chip in enumerate(chips):
            copy(4 + j, (*chip, 1 - c), me).wait_recv()
        for cp in first + passed:
            cp.wait_send()
        mine.wait()
        acc = buf[0:r, :]
        for d in range(1, N_DEV):
            acc = acc + buf[d * r:(d + 1) * r, :]
        out_ref[...] = acc

    return _pcall(
        body, name=name, in_specs=[VMEM_SPEC] + [HBM_SPEC] * len(extra), out_specs=VMEM_SPEC,
        out_shape=jax.ShapeDtypeStruct((r, LANES), f32),
        scratch_shapes=[pltpu.VMEM((N_DEV * r, LANES), f32), pltpu.SemaphoreType.DMA((7,)),
                        pltpu.SemaphoreType.DMA((7,)), pltpu.SemaphoreType.DMA],
    )(v, *extra)


SEM_SPEC = pl.BlockSpec(memory_space=pltpu.SEMAPHORE)
HBM_ONLY = pl.BlockSpec(memory_space=pltpu.HBM)
EFFECT = pltpu.SideEffectType.DATAFLOW_SIDE_EFFECTING


def _split_start(bufs, copies_fn, n_sems, *, name):
    n = len(bufs)

    def body(*refs):
        send_sems, recv_sems = refs[n], refs[n + 1]
        thru = refs[n + 2:2 * n + 2]
        token = refs[2 * n + 2]
        for cp in copies_fn(thru, send_sems, recv_sems):
            cp.start()
        token[...] = jnp.zeros_like(token)

    outs = _pcall(
        body, name=name,
        out_shape=(pltpu.SemaphoreType.DMA((n_sems,)), pltpu.SemaphoreType.DMA((n_sems,)),
                   *[pltpu.HBM(b.shape, b.dtype) for b in bufs], jax.ShapeDtypeStruct((8, LANES), f32)),
        in_specs=[HBM_ONLY] * n,
        out_specs=(SEM_SPEC, SEM_SPEC, *[HBM_ONLY] * n, VMEM_SPEC),
        input_output_aliases={k: 2 + k for k in range(n)},
        compiler_params=pltpu.CompilerParams(has_side_effects=EFFECT),
    )(*[pltpu.with_memory_space_constraint(b, pltpu.HBM) for b in bufs])
    return outs[0], outs[1], list(outs[2:2 + n]), outs[2 + n]


def _split_wait(thru, send_sems, recv_sems, after, copies_fn, *, name):
    n = len(thru)

    def body(*refs):
        for cp in copies_fn(refs[:n], refs[n], refs[n + 1]):
            cp.wait_send()
            cp.wait_recv()

    return list(_pcall(
        body, name=name,
        out_shape=tuple(pltpu.HBM(b.shape, b.dtype) for b in thru),
        in_specs=[HBM_ONLY] * n + [SEM_SPEC, SEM_SPEC] + [HBM_SPEC] * len(after),
        out_specs=tuple([HBM_ONLY] * n),
        input_output_aliases={k: k for k in range(n)},
        compiler_params=pltpu.CompilerParams(has_side_effects=EFFECT),
    )(*thru, send_sems, recv_sems, *after))


def _scatter_copies(n):
    def copies(bufs, send_sems, recv_sems):
        x, y, c = _position()
        me_chip = 2 * x + y
        cps = []
        for k in range(n):
            for j, (px, py) in enumerate(_other_chips(x, y)):
                cps.append(pltpu.make_async_remote_copy(
                    src_ref=bufs[k].at[2 * px + py], dst_ref=bufs[n + k].at[me_chip],
                    send_sem=send_sems.at[3 * k + j], recv_sem=recv_sems.at[3 * k + j],
                    device_id=(px, py, c), device_id_type=MESH))
        return cps
    return copies


def _block_rows(buf, px, py, pc):
    m = buf.shape[0] // N_DEV
    return buf.at[pl.ds(pl.multiple_of((4 * px + 2 * py + pc) * m, 16), m), :]


def _gather_ici_copies(n):
    def copies(bufs, send_sems, recv_sems):
        x, y, c = _position()
        cps = []
        for k in range(n):
            rows = _block_rows(bufs[k], x, y, c)
            targets = [(x, y, 1 - c)] + [(px, py, c) for px, py in _other_chips(x, y)]
            for j, to in enumerate(targets):
                cps.append(pltpu.make_async_remote_copy(
                    src_ref=rows, dst_ref=rows, send_sem=send_sems.at[4 * k + j], recv_sem=recv_sems.at[4 * k + j],
                    device_id=to, device_id_type=MESH))
        return cps
    return copies


def _gather_d2d_copies(n):
    def copies(bufs, send_sems, recv_sems):
        x, y, c = _position()
        cps = []
        for k in range(n):
            for j, (px, py) in enumerate(_other_chips(x, y)):
                rows = _block_rows(bufs[k], px, py, c)
                cps.append(pltpu.make_async_remote_copy(
                    src_ref=rows, dst_ref=rows, send_sem=send_sems.at[3 * k + j], recv_sem=recv_sems.at[3 * k + j],
                    device_id=(x, y, 1 - c), device_id_type=MESH))
        return cps
    return copies


def _cast_halves(shards, after, *, name):
    n = len(shards)
    where = jnp.stack([2 * lax.axis_index("x") + lax.axis_index("y"), lax.axis_index("c")]).astype(jnp.int32)

    def body(w_ref, *refs):
        for k in range(n):
            refs[n + 1 + k][...] = refs[k][...].astype(refs[n + 1 + k].dtype)

    def half(s):
        return (s.shape[0] // 2, s.shape[1])

    return _pcall(
        body, name=name,
        grid_spec=pltpu.PrefetchScalarGridSpec(
            num_scalar_prefetch=1, grid=(1,),
            in_specs=[pl.BlockSpec(half(s), lambda i, w_ref: (w_ref[1], 0)) for s in shards] + [HBM_SPEC],
            out_specs=[pl.BlockSpec(half(s), lambda i, w_ref: (2 * w_ref[0] + w_ref[1], 0)) for s in shards]),
        out_shape=[jax.ShapeDtypeStruct((N_SHARD * s.shape[0], s.shape[1]), MXU_DTYPE) for s in shards],
        compiler_params=_params(1),
    )(where, *shards, after)


class _SplitGather:
    def __init__(self, shards, after, tag):
        self.tag = tag
        self.n = len(shards)
        halves = _cast_halves(shards, after, name=f"{tag}_cast")
        self.ici = _split_start(halves, _gather_ici_copies(self.n), 4 * self.n, name=f"{tag}_ici_start")
        self.token = self.ici[3]

    def forward(self, after):
        send_sems, recv_sems, thru, _ = self.ici
        landed = _split_wait(thru, send_sems, recv_sems, after, _gather_ici_copies(self.n), name=f"{self.tag}_ici_wait")
        self.d2d = _split_start(landed, _gather_d2d_copies(self.n), 3 * self.n, name=f"{self.tag}_d2d_start")
        return self.d2d[3]

    def finish(self, after):
        send_sems, recv_sems, thru, _ = self.d2d
        return _split_wait(thru, send_sems, recv_sems, after, _gather_d2d_copies(self.n), name=f"{self.tag}_d2d_wait")


class _Overlap(_NoOverlap):
    def __init__(self, up_shard, late_shards, ffn2_shards, after):
        self.up = _SplitGather([up_shard], after, "ag0")
        self.late = _SplitGather(late_shards, self.up.token, "ag1")
        self.ffn2 = _SplitGather(ffn2_shards, self.late.token, "ag2")
        self.reduced = None
        self.ffn1_parts = []

    def start_token(self):
        return self.ffn2.token

    def up_weight(self, w, after):
        token = self.up.forward(after)
        return self.up.finish([token])[0].reshape(N_SHARD, D_MODEL, D_FF // N_SHARD)

    def late_weights(self, w, after):
        token = self.late.forward(after)
        f1d, w_in, wo = self.late.finish([token])
        w_in = w_in.reshape(N_SHARD, D_MODEL, IN_SHARD).transpose(1, 0, 2).reshape(D_MODEL, IN_COLS)
        return dict(f1d=f1d.reshape(N_SHARD, D_FF // N_SHARD, D_MODEL), wp=make_wp(w_in), wo=wo)

    def after_attention(self, after):
        return self.ffn2.forward(after)

    def ffn2_weights(self, w, after):
        full = self.ffn2.finish(after)
        fs = D_FF // N_SHARD
        return (full[0].reshape(N_SHARD, D_MODEL, fs), full[1].reshape(N_SHARD, D_MODEL, fs),
                full[2].reshape(N_SHARD, fs, D_MODEL))

    def ffn2_grads(self, grads):
        recvs = _swap_halves(grads, name="rs_swap_ffn2")
        ps = _add_halves(grads, recvs, name="rs_add_ffn2")
        lands = [lax.empty(p.shape, p.dtype) for p in ps]
        self.scatter = _split_start(list(ps) + lands, _scatter_copies(len(ps)), 3 * len(ps), name="rs_scatter_ffn2_start")
        return self.scatter[3]

    def ffn1_grads(self, grads):
        tag = "ffn1" + "ab"[len(self.ffn1_parts)]
        recvs = _swap_halves(grads, name=f"rs_swap_{tag}")
        ps = list(_add_halves(grads[:2], recvs[:2], name=f"rs_add_{tag}_gu"))
        ps += list(_add_halves(grads[2:], recvs[2:], name=f"rs_add_{tag}_d"))
        lands = [lax.empty(p.shape, p.dtype) for p in ps]
        started = _split_start(ps + lands, _scatter_copies(3), 9, name=f"rs_scatter_{tag}_start")
        self.ffn1_parts.append((tag, started))
        return started[3]

    def ffn1_reduced(self, after):
        sums = []
        for tag, (send_sems, recv_sems, thru, _) in self.ffn1_parts:
            done = _split_wait(thru, send_sems, recv_sems, after, _scatter_copies(3), name=f"rs_scatter_{tag}_wait")
            sums += list(_sum_slabs(done[:2], done[3:5], name=f"rs_sum_{tag}_gu"))
            sums += list(_sum_slabs(done[2:3], done[5:], name=f"rs_sum_{tag}_d"))
        return sums

    def mixer_grads(self, dwp, dwo, small, loss):
        packed = jnp.concatenate([_pack_small(small), jnp.broadcast_to(loss, (8, LANES))], axis=0)
        summed = _all_reduce_small(packed, name="ar_small")
        self.small_sum, self.loss_sum = summed[:-8], summed[-8, 0]
        gwin = dwp[:, :IN_COLS].reshape(D_MODEL, N_SHARD, IN_SHARD).transpose(1, 0, 2).astype(GRAD_DTYPE)
        gwo = dwo.reshape(N_SHARD, D_MODEL // N_SHARD, D_MODEL).astype(GRAD_DTYPE)
        recvs = _swap_halves([gwin, gwo], name="rs_swap_mix")
        ps = [_add_halves([g], [r], name=f"rs_add_{tag}")[0] for g, r, tag in zip([gwin, gwo], recvs, ["w_in", "w_out"])]
        lands = [lax.empty(p.shape, p.dtype) for p in ps]
        self.scatter_mix = _split_start(ps + lands, _scatter_copies(2), 6, name="rs_scatter_mix_start")
        return self.scatter_mix[3]

    def mixer_reduced(self, after):
        send_sems, recv_sems, thru, _ = self.scatter_mix
        done = _split_wait(thru, send_sems, recv_sems, after, _scatter_copies(2), name="rs_scatter_mix_wait")
        return [_sum_slabs([done[k]], [done[2 + k]], name=f"rs_sum_{tag}")[0] for k, tag in enumerate(["w_in", "w_out"])]

    def before_ffn1_bwd(self, after):
        send_sems, recv_sems, thru, _ = self.scatter
        n = len(thru) // 2
        done = _split_wait(thru, send_sems, recv_sems, after, _scatter_copies(n), name="rs_scatter_ffn2_wait")
        self.reduced = list(_sum_slabs(done[:n], done[n:], name="rs_sum_ffn2"))


def _adamw(gs, ws, ms, vs, *, name, tm=256):
    n = len(gs)
    r, cdim = gs[0].shape
    tm = r if tm is None else min(tm, r)
    assert r % tm == 0, (r, tm)
    c1 = 1.0 / (1.0 - ADAM_B1 ** ADAM_STEP)
    c2 = 1.0 / (1.0 - ADAM_B2 ** ADAM_STEP)

    def body(*refs):
        for k in range(n):
            g = refs[k][...]
            w = refs[n + k][...]
            m = ADAM_B1 * refs[2 * n + k][...] + (1.0 - ADAM_B1) * g
            v = ADAM_B2 * refs[3 * n + k][...] + (1.0 - ADAM_B2) * (g * g)
            refs[4 * n + k][...] = g
            refs[5 * n + k][...] = -ADAM_LR * ((m * c1) / (jnp.sqrt(v * c2) + ADAM_EPS) + ADAM_WD * w)
            refs[6 * n + k][...] = m
            refs[7 * n + k][...] = v

    flat = pl.BlockSpec((tm, cdim), lambda i: (i, 0))
    like_w = flat if ws[0].ndim == 2 else pl.BlockSpec((None, tm, cdim), lambda i: (0, i, 0))
    outs = _pcall(
        body, name=name, grid=(r // tm,), in_specs=[flat] * n + [like_w] * (3 * n), out_specs=[like_w] * (4 * n),
        out_shape=[jax.ShapeDtypeStruct(ws[0].shape, f32)] * (4 * n),
        compiler_params=_params(1),
    )(*gs, *ws, *ms, *vs)
    return outs[:n], outs[n:2 * n], outs[2 * n:3 * n], outs[3 * n:]


BIG = ["ffn1_w_gate", "ffn1_w_up", "ffn1_w_down", "ffn2_w_gate", "ffn2_w_up", "ffn2_w_down"]
SMALL = ["ln1_g", "ln1_b", "b_forget", "conv_w", "conv_b", "rg_wa", "rg_ba", "rg_wx", "rg_bx", "lru_lambda",
         "ln2_g", "ln2_b", "ln3_g", "ln3_b"]
WEIGHTS = ["ffn1_w_gate", "ffn1_w_up", "ffn1_w_down", "ln1_g", "ln1_b", "w_in", "b_forget", "conv_w", "conv_b",
           "rg_wa", "rg_ba", "rg_wx", "rg_bx", "lru_lambda", "w_out", "ln2_g", "ln2_b",
           "ffn2_w_gate", "ffn2_w_up", "ffn2_w_down", "ln3_g", "ln3_b"]


def _pack_small(parts):
    rows = []
    for n in SMALL:
        flat = parts[n].reshape(-1)
        pad = (-flat.shape[0]) % LANES
        rows.append(jnp.pad(flat, (0, pad)).reshape(-1, LANES))
    packed = jnp.concatenate(rows, axis=0)
    return jnp.pad(packed, ((0, (-packed.shape[0]) % 8), (0, 0)))


def _unpack_small(packed, shapes):
    out, r0 = {}, 0
    for n in SMALL:
        size = math.prod(shapes[n])
        nr = -(-size // LANES)
        out[n] = packed[r0:r0 + nr].reshape(-1)[:size].reshape(shapes[n])
        r0 += nr
    return out


def kernel(x, ffn1_w_gate, ffn1_w_up, ffn1_w_down, ln1_g, ln1_b, w_in, b_forget, conv_w, conv_b, rg_wa, rg_ba, rg_wx, rg_bx, lru_lambda, w_out, ln2_g, ln2_b, ffn2_w_gate, ffn2_w_up, ffn2_w_down, ln3_g, ln3_b, loss_target, m_ffn1_w_gate, m_ffn1_w_up, m_ffn1_w_down, m_ln1_g, m_ln1_b, m_w_in, m_b_forget, m_conv_w, m_conv_b, m_rg_wa, m_rg_ba, m_rg_wx, m_rg_bx, m_lru_lambda, m_w_out, m_ln2_g, m_ln2_b, m_ffn2_w_gate, m_ffn2_w_up, m_ffn2_w_down, m_ln3_g, m_ln3_b, v_ffn1_w_gate, v_ffn1_w_up, v_ffn1_w_down, v_ln1_g, v_ln1_b, v_w_in, v_b_forget, v_conv_w, v_conv_b, v_rg_wa, v_rg_ba, v_rg_wx, v_rg_bx, v_lru_lambda, v_w_out, v_ln2_g, v_ln2_b, v_ffn2_w_gate, v_ffn2_w_up, v_ffn2_w_down, v_ln3_g, v_ln3_b):
    args = dict(locals())
    w = {n: args[n] for n in WEIGHTS}
    mom = {n: args["m_" + n] for n in WEIGHTS}
    var = {n: args["v_" + n] for n in WEIGHTS}
    chip = 2 * lax.axis_index("x") + lax.axis_index("y")

    g1 = _all_gather_bf16([w["ffn1_w_gate"][0]], name="ag_ffn1_gate")
    full = dict(
        f1g=g1[0].reshape(N_SHARD, D_MODEL, D_FF // N_SHARD),
        bfp=jnp.pad(b_forget, ((0, 0), (0, LANES - HEADS))),
        ln1_g=ln1_g, ln1_b=ln1_b, ln2_g=ln2_g, ln2_b=ln2_b, ln3_g=ln3_g, ln3_b=ln3_b,
        conv_b=conv_b, rg_wa=rg_wa[0], rg_wx=rg_wx[0], rg_ba=rg_ba[0], rg_bx=rg_bx[0], lam=lru_lambda,
    )
    cw_place = lax.dynamic_update_slice(jnp.zeros((8, LRU_W), f32), conv_w[0] * 0.5, (0, chip * (LRU_W // N_SHARD)))
    cw_full = _all_reduce_small(cw_place.reshape(-1, LANES), g1[0], name="ag_conv_w")
    full["conv_w"] = cw_full.reshape(8, LRU_W)[:CONV_K]

    hooks = _Overlap(w["ffn1_w_up"][0], [w["ffn1_w_down"][0], w["w_in"][0], w["w_out"][0]],
                     [w[n][0] for n in BIG[3:]], cw_full)
    loss_rep, dx, g = _local_step(x[0], loss_target[0], full, hooks)
    loss = hooks.loss_sum

    token1 = hooks.ffn1_grads(g["f1"][1])
    red = _join_halves(hooks.reduced + hooks.mixer_reduced([token1]), name="rs_join_rest")
    grads = dict(zip(BIG[3:] + ["w_in", "w_out"], red))

    small_shapes = {n: w[n].shape for n in SMALL}
    small_shapes["conv_w"] = (1, CONV_K, LRU_W)
    gs_red = _unpack_small(hooks.small_sum, small_shapes)
    gs_red["conv_w"] = lax.dynamic_slice(gs_red["conv_w"], (0, 0, chip * (LRU_W // N_SHARD)),
                                         (1, CONV_K, LRU_W // N_SHARD))
    grads.update(gs_red)

    delta, new_m, new_v = {}, {}, {}

    def adamw(names, name, **kw):
        g3, d, nm, nv = _adamw([grads[n] for n in names], [w[n] for n in names], [mom[n] for n in names],
                               [var[n] for n in names], name=name, **kw)
        for i, n in enumerate(names):
            grads[n], delta[n], new_m[n], new_v[n] = g3[i], d[i], nm[i], nv[i]

    adamw(BIG[3:], "adamw_ffn2", tm=128)
    adamw(["w_in"], "adamw_w_in")
    adamw(["w_out"], "adamw_w_out")
    shard_shapes = {n: w[n].shape for n in SMALL}
    _, d, nm, nv = _adamw([_pack_small({n: grads[n] for n in SMALL})], [_pack_small({n: w[n] for n in SMALL})],
                          [_pack_small({n: mom[n] for n in SMALL})], [_pack_small({n: var[n] for n in SMALL})],
                          name="adamw_small", tm=None)
    for dst, packed in ((delta, d[0]), (new_m, nm[0]), (new_v, nv[0])):
        dst.update(_unpack_small(packed, shard_shapes))

    worked = [new_v["ffn2_w_down"], new_v["w_in"], new_v["w_out"], nv[0]]
    ga, ua, da, gb, ub, db = _join_halves(hooks.ffn1_reduced(worked), name="rs_join_ffn1")
    grads["ffn1_w_gate"] = jnp.concatenate([ga, gb], axis=1)
    grads["ffn1_w_up"] = jnp.concatenate([ua, ub], axis=1)
    grads["ffn1_w_down"] = jnp.concatenate([da, db], axis=0)
    adamw(BIG[:3], "adamw_ffn1", tm=128)

    def shaped(tree, n):
        return tree[n].reshape(w[n].shape)

    return (loss, dx[None], *[shaped(grads, n) for n in WEIGHTS], *[shaped(delta, n) for n in WEIGHTS],
            *[shaped(new_m, n) for n in WEIGHTS], *[shaped(new_v, n) for n in WEIGHTS])
```

```python
import functools
import math

import jax
import jax.numpy as jnp
from jax import lax
from jax.experimental import pallas as pl
from jax.experimental.pallas import tpu as pltpu

f32 = jnp.float32
MXU_DTYPE = jnp.bfloat16
GRAD_DTYPE = jnp.bfloat16

D_MODEL = 1024
D_FF = 4096
N_SHARD = 4
N_DEV = 8
FOX_W = 512
LRU_W = 512
HEADS = 8
HEAD_DIM = 64
CONV_K = 4
IN_COLS = 2568
IN_SHARD = IN_COLS // N_SHARD
QKV_W = 3 * FOX_W
Z_PAD = 2688
LANES = 128
LN_EPS = 1e-5
DN_ALPHA = 2.0 ** 0.25
LRU_C = 8.0
NEG_BIG = -1e30
VMEM_LIMIT = 56 * 1024 * 1024

ADAM_LR = 0.001
ADAM_B1 = 0.9
ADAM_B2 = 0.999
ADAM_EPS = 1e-08
ADAM_WD = 0.01
ADAM_STEP = 10


def _pcall(body, **kw):
    return pl.pallas_call(body, **kw)


def _params(n_grid, vmem=VMEM_LIMIT):
    return pltpu.CompilerParams(dimension_semantics=("arbitrary",) * n_grid, vmem_limit_bytes=vmem)


def _dot(a, b):
    return jnp.dot(a, b, preferred_element_type=f32)


def _dot_nt(a, b):
    return lax.dot_general(a, b, (((1,), (1,)), ((), ())), preferred_element_type=f32)


def _dot_tn(a, b):
    return lax.dot_general(a, b, (((0,), (0,)), ((), ())), preferred_element_type=f32)


def _sigmoid(x):
    return 1.0 / (1.0 + jnp.exp(-x))


def _layer_norm_stats(y):
    mu = jnp.mean(y, axis=-1, keepdims=True)
    yc = y - mu
    var = jnp.mean(yc * yc, axis=-1, keepdims=True)
    rstd = lax.rsqrt(var + LN_EPS)
    return yc * rstd, rstd


def _ln_backward(dy, xhat, rstd, gamma):
    dxhat = dy * gamma
    m1 = jnp.mean(dxhat, axis=-1, keepdims=True)
    m2 = jnp.mean(dxhat * xhat, axis=-1, keepdims=True)
    dyp = rstd * (dxhat - m1 - xhat * m2)
    return dyp, jnp.sum(dy * xhat, axis=0, keepdims=True), jnp.sum(dy, axis=0, keepdims=True)


def _ffn_fwd(x, wg, wu, wd, ln_g, ln_b, *, name, tm=1024, tf=512):
    T = x.shape[0]
    tm = min(tm, T)
    fs = D_FF // N_SHARD
    cpf = fs // tf
    nf = D_FF // tf
    nt = T // tm

    def body(x_ref, wg_ref, wu_ref, wd_ref, g_ref, b_ref,
             xb_ref, gact_ref, uact_ref, xhat_ref, xn_ref, rstd_ref, acc_ref):
        f = pl.program_id(1)

        @pl.when(f == 0)
        def _():
            xb_ref[...] = x_ref[...].astype(MXU_DTYPE)
            acc_ref[...] = jnp.zeros_like(acc_ref)

        xb = xb_ref[...]
        g = _dot(xb, wg_ref[...])
        u = _dot(xb, wu_ref[...])
        h = (g * _sigmoid(g)) * u
        gact_ref[...] = g.astype(gact_ref.dtype)
        uact_ref[...] = u.astype(uact_ref.dtype)
        acc_ref[...] += _dot(h.astype(MXU_DTYPE), wd_ref[...])

        @pl.when(f == nf - 1)
        def _():
            y = DN_ALPHA * x_ref[...] + 0.5 * acc_ref[...]
            xhat, rstd = _layer_norm_stats(y)
            xhat_ref[...] = xhat
            xn_ref[...] = (xhat * g_ref[...] + b_ref[...]).astype(xn_ref.dtype)
            rstd_ref[...] = jnp.broadcast_to(rstd, rstd_ref.shape)

    row = lambda i, f: (i, 0)
    return _pcall(
        body, name=name, grid=(nt, nf),
        in_specs=[
            pl.BlockSpec((tm, D_MODEL), row),
            pl.BlockSpec((None, D_MODEL, tf), lambda i, f: (f // cpf, 0, f % cpf)),
            pl.BlockSpec((None, D_MODEL, tf), lambda i, f: (f // cpf, 0, f % cpf)),
            pl.BlockSpec((None, tf, D_MODEL), lambda i, f: (f // cpf, f % cpf, 0)),
            pl.BlockSpec((1, D_MODEL), lambda i, f: (0, 0)),
            pl.BlockSpec((1, D_MODEL), lambda i, f: (0, 0)),
        ],
        out_specs=[
            pl.BlockSpec((tm, D_MODEL), row),
            pl.BlockSpec((tm, tf), lambda i, f: (i, f)),
            pl.BlockSpec((tm, tf), lambda i, f: (i, f)),
            pl.BlockSpec((tm, D_MODEL), row),
            pl.BlockSpec((tm, D_MODEL), row),
            pl.BlockSpec((tm, LANES), row),
        ],
        out_shape=[
            jax.ShapeDtypeStruct((T, D_MODEL), MXU_DTYPE),
            jax.ShapeDtypeStruct((T, D_FF), MXU_DTYPE),
            jax.ShapeDtypeStruct((T, D_FF), MXU_DTYPE),
            jax.ShapeDtypeStruct((T, D_MODEL), f32),
            jax.ShapeDtypeStruct((T, D_MODEL), MXU_DTYPE),
            jax.ShapeDtypeStruct((T, LANES), f32),
        ],
        scratch_shapes=[pltpu.VMEM((tm, D_MODEL), f32)],
        compiler_params=_params(2),
    )(x, wg, wu, wd, ln_g, ln_b)


def _ffn_up(x, wg, wu, after=None, *, name, tm=1024, tf=512):
    T = x.shape[0]
    tm = min(tm, T)
    cpf = (D_FF // N_SHARD) // tf
    nf = D_FF // tf
    extra = [] if after is None else [after]

    def body(x_ref, wg_ref, wu_ref, *refs):
        xb_ref, gact_ref, uact_ref, hact_ref = refs[len(extra):]

        @pl.when(pl.program_id(1) == 0)
        def _():
            xb_ref[...] = x_ref[...].astype(MXU_DTYPE)

        xb = xb_ref[...]
        g = _dot(xb, wg_ref[...])
        u = _dot(xb, wu_ref[...])
        gact_ref[...] = g.astype(gact_ref.dtype)
        uact_ref[...] = u.astype(uact_ref.dtype)
        hact_ref[...] = ((g * _sigmoid(g)) * u).astype(hact_ref.dtype)

    row = lambda i, f: (i, 0)
    tile = pl.BlockSpec((tm, tf), lambda i, f: (i, f))
    cols = pl.BlockSpec((None, D_MODEL, tf), lambda i, f: (f // cpf, 0, f % cpf))
    return _pcall(
        body, name=name, grid=(T // tm, nf),
        in_specs=[pl.BlockSpec((tm, D_MODEL), row), cols, cols] + [pl.BlockSpec(memory_space=pl.ANY)] * len(extra),
        out_specs=[pl.BlockSpec((tm, D_MODEL), row), tile, tile, tile],
        out_shape=[jax.ShapeDtypeStruct((T, D_MODEL), MXU_DTYPE)] + [jax.ShapeDtypeStruct((T, D_FF), MXU_DTYPE)] * 3,
        compiler_params=_params(2),
    )(x, wg, wu, *extra)


def _ffn_down_ln(x, hact, wd, ln_g, ln_b, *, name, tm=1024):
    T = x.shape[0]
    tm = min(tm, T)
    fs = D_FF // N_SHARD

    def body(x_ref, h_ref, wd_ref, g_ref, b_ref, xhat_ref, xn_ref, rstd_ref, acc_ref):
        k = pl.program_id(1)

        @pl.when(k == 0)
        def _():
            acc_ref[...] = jnp.zeros_like(acc_ref)

        acc_ref[...] += _dot(h_ref[...], wd_ref[...])

        @pl.when(k == N_SHARD - 1)
        def _():
            xhat, rstd = _layer_norm_stats(DN_ALPHA * x_ref[...] + 0.5 * acc_ref[...])
            xhat_ref[...] = xhat
            xn_ref[...] = (xhat * g_ref[...] + b_ref[...]).astype(xn_ref.dtype)
            rstd_ref[...] = jnp.broadcast_to(rstd, rstd_ref.shape)

    row = lambda i, k: (i, 0)
    vec = pl.BlockSpec((1, D_MODEL), lambda i, k: (0, 0))
    return _pcall(
        body, name=name, grid=(T // tm, N_SHARD),
        in_specs=[pl.BlockSpec((tm, D_MODEL), row), pl.BlockSpec((tm, fs), lambda i, k: (i, k)),
                  pl.BlockSpec((None, fs, D_MODEL), lambda i, k: (k, 0, 0)), vec, vec],
        out_specs=[pl.BlockSpec((tm, D_MODEL), row), pl.BlockSpec((tm, D_MODEL), row), pl.BlockSpec((tm, LANES), row)],
        out_shape=[jax.ShapeDtypeStruct((T, D_MODEL), f32), jax.ShapeDtypeStruct((T, D_MODEL), MXU_DTYPE),
                   jax.ShapeDtypeStruct((T, LANES), f32)],
        scratch_shapes=[pltpu.VMEM((tm, D_MODEL), f32)],
        compiler_params=_params(2),
    )(x, hact, wd, ln_g, ln_b)


def _ffn_bwd(dyp, xb, gact, uact, wg, wu, wd, after=None, *, name, tm=512, tf=512, part=None, dx_init=None):
    T = dyp.shape[0]
    tm = min(tm, T)
    fs = D_FF // N_SHARD
    cpf = fs // tf
    nt = T // tm
    nf = D_FF // tf if part is None else N_SHARD
    wf = fs if part is None else tf
    slab = (lambda f: f // cpf) if part is None else (lambda f: f)
    chunk = (lambda f: f % cpf) if part is None else (lambda f: part)
    extra = ([] if dx_init is None else [dx_init]) + ([] if after is None else [after])

    def body(dyp_ref, xb_ref, g_ref, u_ref, wg_ref, wu_ref, wd_ref, *refs):
        dx_hbm, dwg_ref, dwu_ref, dwd_ref, dx_sc, dwg_sc, dwu_sc, dwd_sc, sem = refs[len(extra):]
        f = pl.program_id(0)
        i = pl.program_id(1)
        rows = pl.ds(pl.multiple_of(i * tm, tm), tm)
        dyp_t = dyp_ref[...]
        dy = (0.5 * dyp_t).astype(MXU_DTYPE)

        @pl.when(i == 0)
        def _():
            dwg_sc[...] = jnp.zeros_like(dwg_sc)
            dwu_sc[...] = jnp.zeros_like(dwu_sc)
            dwd_sc[...] = jnp.zeros_like(dwd_sc)

        @pl.when(f == 0)
        def _():
            dx_sc[rows, :] = DN_ALPHA * dyp_t if dx_init is None else refs[0][...]

        g = g_ref[...].astype(f32)
        u = u_ref[...].astype(f32)
        sig = _sigmoid(g)
        silu = g * sig
        dh = _dot_nt(dy, wd_ref[...])
        dg = (dh * u * (sig * (1.0 + g * (1.0 - sig)))).astype(MXU_DTYPE)
        du = (dh * silu).astype(MXU_DTYPE)
        hb = (silu * u).astype(MXU_DTYPE)
        dx_sc[rows, :] += _dot_nt(dg, wg_ref[...]) + _dot_nt(du, wu_ref[...])
        xb_t = xb_ref[...]
        dwg_sc[...] += _dot_tn(xb_t, dg)
        dwu_sc[...] += _dot_tn(xb_t, du)
        dwd_sc[...] += _dot_tn(hb, dy)

        @pl.when(i == nt - 1)
        def _():
            dwg_ref[...] = dwg_sc[...].astype(dwg_ref.dtype)
            dwu_ref[...] = dwu_sc[...].astype(dwu_ref.dtype)
            dwd_ref[...] = dwd_sc[...].astype(dwd_ref.dtype)

        @pl.when(jnp.logical_and(f == nf - 1, i == nt - 1))
        def _():
            cp = pltpu.make_async_copy(dx_sc, dx_hbm, sem)
            cp.start()
            cp.wait()

    row = lambda f, i: (i, 0)
    return _pcall(
        body, name=name, grid=(nf, nt),
        in_specs=[
            pl.BlockSpec((tm, D_MODEL), row),
            pl.BlockSpec((tm, D_MODEL), row),
            pl.BlockSpec((tm, tf), lambda f, i: (i, slab(f) * cpf + chunk(f))),
            pl.BlockSpec((tm, tf), lambda f, i: (i, slab(f) * cpf + chunk(f))),
            pl.BlockSpec((None, D_MODEL, tf), lambda f, i: (slab(f), 0, chunk(f))),
            pl.BlockSpec((None, D_MODEL, tf), lambda f, i: (slab(f), 0, chunk(f))),
            pl.BlockSpec((None, tf, D_MODEL), lambda f, i: (slab(f), chunk(f), 0)),
        ] + ([] if dx_init is None else [pl.BlockSpec((tm, D_MODEL), row)])
        + ([] if after is None else [pl.BlockSpec(memory_space=pl.ANY)]),
        out_specs=[
            pl.BlockSpec(memory_space=pl.ANY),
            pl.BlockSpec((None, D_MODEL, tf), lambda f, i: (slab(f), 0, chunk(f) if part is None else 0)),
            pl.BlockSpec((None, D_MODEL, tf), lambda f, i: (slab(f), 0, chunk(f) if part is None else 0)),
            pl.BlockSpec((None, tf, D_MODEL), lambda f, i: (slab(f), chunk(f) if part is None else 0, 0)),
        ],
        out_shape=[
            jax.ShapeDtypeStruct((T, D_MODEL), f32),
            jax.ShapeDtypeStruct((N_SHARD, D_MODEL, wf), GRAD_DTYPE),
            jax.ShapeDtypeStruct((N_SHARD, D_MODEL, wf), GRAD_DTYPE),
            jax.ShapeDtypeStruct((N_SHARD, wf, D_MODEL), GRAD_DTYPE),
        ],
        scratch_shapes=[pltpu.VMEM((T, D_MODEL), f32), pltpu.VMEM((D_MODEL, tf), f32),
                        pltpu.VMEM((D_MODEL, tf), f32), pltpu.VMEM((tf, D_MODEL), f32),
                        pltpu.SemaphoreType.DMA],
        compiler_params=_params(2),
    )(dyp, xb, gact, uact, wg, wu, wd, *extra)


def _loss_ln_bwd(xhat, rstd, ln_g, ln_b, target, *, name, tm=512):
    T = xhat.shape[0]
    tm = min(tm, T)
    nt = T // tm

    def body(xhat_ref, rstd_ref, g_ref, b_ref, t_ref, dyp_ref, dg_ref, db_ref, loss_ref):
        i = pl.program_id(0)

        @pl.when(i == 0)
        def _():
            dg_ref[...] = jnp.zeros_like(dg_ref)
            db_ref[...] = jnp.zeros_like(db_ref)
            loss_ref[...] = jnp.zeros_like(loss_ref)

        xhat_t = xhat_ref[...]
        gamma = g_ref[...]
        err = xhat_t * gamma + b_ref[...] - t_ref[...]
        sq = jnp.sum(jnp.sum(err * err, axis=0, keepdims=True), axis=1, keepdims=True)
        loss_ref[...] += jnp.broadcast_to(sq * (0.5 / D_MODEL), loss_ref.shape)
        dy = err * (1.0 / D_MODEL)
        dyp, dgam, dbeta = _ln_backward(dy, xhat_t, rstd_ref[:, 0:1], gamma)
        dyp_ref[...] = dyp
        dg_ref[...] += dgam
        db_ref[...] += dbeta

    row = lambda i: (i, 0)
    const = lambda i: (0, 0)
    return _pcall(
        body, name=name, grid=(nt,),
        in_specs=[pl.BlockSpec((tm, D_MODEL), row), pl.BlockSpec((tm, LANES), row),
                  pl.BlockSpec((1, D_MODEL), const), pl.BlockSpec((1, D_MODEL), const),
                  pl.BlockSpec((tm, D_MODEL), row)],
        out_specs=[pl.BlockSpec((tm, D_MODEL), row), pl.BlockSpec((1, D_MODEL), const),
                   pl.BlockSpec((1, D_MODEL), const), pl.BlockSpec((1, LANES), const)],
        out_shape=[jax.ShapeDtypeStruct((T, D_MODEL), f32), jax.ShapeDtypeStruct((1, D_MODEL), f32),
                   jax.ShapeDtypeStruct((1, D_MODEL), f32), jax.ShapeDtypeStruct((1, LANES), f32)],
        compiler_params=_params(1),
    )(xhat, rstd, ln_g, ln_b, target)


def _ln_bwd(dy, xhat, rstd, ln_g, *, name, tm=512):
    T = xhat.shape[0]
    tm = min(tm, T)
    nt = T // tm

    def body(dy_ref, xhat_ref, rstd_ref, g_ref, dyp_ref, dg_ref, db_ref):
        i = pl.program_id(0)

        @pl.when(i == 0)
        def _():
            dg_ref[...] = jnp.zeros_like(dg_ref)
            db_ref[...] = jnp.zeros_like(db_ref)

        dyp, dgam, dbeta = _ln_backward(dy_ref[...], xhat_ref[...], rstd_ref[:, 0:1], g_ref[...])
        dyp_ref[...] = dyp
        dg_ref[...] += dgam
        db_ref[...] += dbeta

    row = lambda i: (i, 0)
    const = lambda i: (0, 0)
    return _pcall(
        body, name=name, grid=(nt,),
        in_specs=[pl.BlockSpec((tm, D_MODEL), row), pl.BlockSpec((tm, D_MODEL), row),
                  pl.BlockSpec((tm, LANES), row), pl.BlockSpec((1, D_MODEL), const)],
        out_specs=[pl.BlockSpec((tm, D_MODEL), row), pl.BlockSpec((1, D_MODEL), const),
                   pl.BlockSpec((1, D_MODEL), const)],
        out_shape=[jax.ShapeDtypeStruct((T, D_MODEL), f32), jax.ShapeDtypeStruct((1, D_MODEL), f32),
                   jax.ShapeDtypeStruct((1, D_MODEL), f32)],
        compiler_params=_params(1),
    )(dy, xhat, rstd, ln_g)


def _proj_in(xn, wp, bfp, *, name, tm=512):
    T = xn.shape[0]
    tm = min(tm, T)
    nt = T // tm

    def body(x_ref, w_ref, b_ref, qkv_ref, lxg_ref, fg_ref):
        z = _dot(x_ref[...], w_ref[...])
        qkv_ref[...] = z[:, :QKV_W].astype(qkv_ref.dtype)
        lxg_ref[...] = z[:, QKV_W:QKV_W + 2 * LRU_W]
        fg_ref[...] = z[:, QKV_W + 2 * LRU_W:] + b_ref[...]

    row = lambda i: (i, 0)
    const = lambda i: (0, 0)
    return _pcall(
        body, name=name, grid=(nt,),
        in_specs=[pl.BlockSpec((tm, D_MODEL), row), pl.BlockSpec((D_MODEL, Z_PAD), const),
                  pl.BlockSpec((1, LANES), const)],
        out_specs=[pl.BlockSpec((tm, QKV_W), row), pl.BlockSpec((tm, 2 * LRU_W), row),
                   pl.BlockSpec((tm, LANES), row)],
        out_shape=[jax.ShapeDtypeStruct((T, QKV_W), MXU_DTYPE), jax.ShapeDtypeStruct((T, 2 * LRU_W), f32),
                   jax.ShapeDtypeStruct((T, LANES), f32)],
        compiler_params=_params(1),
    )(xn, wp, bfp)


def _proj_in_bwd(dqa, dka, dva, dlxg, dfg, xn, dyp, wp, *, name, tm=512):
    T = xn.shape[0]
    tm = min(tm, T)
    nt = T // tm

    def body(dq_ref, dk_ref, dv_ref, dl_ref, dfg_ref, x_ref, dyp_ref, w_ref, dx_ref, dw_hbm, dw_sc, sem):
        i = pl.program_id(0)

        @pl.when(i == 0)
        def _():
            dw_sc[...] = jnp.zeros_like(dw_sc)

        low = _low_lanes((tm, LANES))

        def packed(ref):
            pairs = [jnp.where(low, ref[:, (2 * j) * LANES:(2 * j + 1) * LANES],
                               _swap_lane_halves(ref[:, (2 * j + 1) * LANES:(2 * j + 2) * LANES]))
                     for j in range(HEADS // 2)]
            return jnp.concatenate(pairs, axis=1).astype(MXU_DTYPE)

        dz = jnp.concatenate(
            [packed(dq_ref), packed(dk_ref), packed(dv_ref),
             dl_ref[...].astype(MXU_DTYPE), dfg_ref[...].astype(MXU_DTYPE)], axis=1)
        dx_ref[...] = DN_ALPHA * dyp_ref[...] + _dot_nt(dz, w_ref[...])
        dw_sc[...] += _dot_tn(x_ref[...], dz)

        @pl.when(i == nt - 1)
        def _():
            dw_sc[:, :FOX_W] = dw_sc[:, :FOX_W] * (1.0 / math.sqrt(HEAD_DIM))
            cp = pltpu.make_async_copy(dw_sc, dw_hbm, sem)
            cp.start()
            cp.wait()

    row = lambda i: (i, 0)
    const = lambda i: (0, 0)
    return _pcall(
        body, name=name, grid=(nt,),
        in_specs=[pl.BlockSpec((tm, HEADS * LANES), row), pl.BlockSpec((tm, HEADS * LANES), row),
                  pl.BlockSpec((tm, HEADS * LANES), row),
                  pl.BlockSpec((tm, 2 * LRU_W), row), pl.BlockSpec((tm, LANES), row),
                  pl.BlockSpec((tm, D_MODEL), row), pl.BlockSpec((tm, D_MODEL), row),
                  pl.BlockSpec((D_MODEL, Z_PAD), const)],
        out_specs=[pl.BlockSpec((tm, D_MODEL), row), pl.BlockSpec(memory_space=pl.ANY)],
        out_shape=[jax.ShapeDtypeStruct((T, D_MODEL), f32), jax.ShapeDtypeStruct((D_MODEL, Z_PAD), f32)],
        scratch_shapes=[pltpu.VMEM((D_MODEL, Z_PAD), f32), pltpu.SemaphoreType.DMA],
        compiler_params=_params(1),
    )(dqa, dka, dva, dlxg, dfg, xn, dyp, wp)


def _split3(x):
    hi = x.astype(jnp.bfloat16)
    r1 = x - hi.astype(f32)
    mid = r1.astype(jnp.bfloat16)
    lo = (r1 - mid.astype(f32)).astype(jnp.bfloat16)
    return hi, mid, lo


def _tri_dot(tri, x):
    hi, mid, lo = _split3(x)
    return _dot(tri, hi) + _dot(tri, mid) + _dot(tri, lo)


FOX_PAD = HEADS * LANES
AUX = HEAD_DIM


def _low_lanes(shape):
    return lax.broadcasted_iota(jnp.int32, shape, 1) < HEAD_DIM


def _swap_lane_halves(x):
    return pltpu.roll(x, HEAD_DIM, 1)


def _fox_prep(qkv, fgb, *, name, tm=512):
    T = fgb.shape[0]
    tm = min(tm, T)
    nt = T // tm

    def body(qkv_ref, fg_ref, qa_ref, ka_ref, va_ref, carry):
        i = pl.program_id(0)

        @pl.when(i == 0)
        def _():
            carry[...] = jnp.zeros_like(carry)

        x = fg_ref[...]
        ls = jnp.minimum(x, 0.0) - jnp.log(1.0 + jnp.exp(-jnp.abs(x)))
        r = lax.broadcasted_iota(jnp.int32, (tm, tm), 0)
        c = lax.broadcasted_iota(jnp.int32, (tm, tm), 1)
        tri = jnp.where(r >= c, 1.0, 0.0).astype(jnp.bfloat16)
        cum = _tri_dot(tri, ls) + carry[0:1, :]
        carry[...] = jnp.broadcast_to(cum[tm - 1:tm, :], carry.shape)

        lane = lax.broadcasted_iota(jnp.int32, (tm, LANES), 1)
        low = lane < HEAD_DIM
        ones_q = jnp.where(jnp.logical_and(lane >= AUX + 3, lane < AUX + 6), 1.0, 0.0)
        ones_k = jnp.where(jnp.logical_and(lane >= AUX, lane < AUX + 3), 1.0, 0.0)
        for j in range(HEADS // 2):
            pair = [qkv_ref[:, t * FOX_W + j * LANES:t * FOX_W + (j + 1) * LANES].astype(f32) for t in range(3)]
            for odd in range(2):
                h = 2 * j + odd
                q, k, v = [_swap_lane_halves(a) if odd else a for a in pair]
                hi, mid, lo = [a.astype(f32) for a in _split3(jnp.broadcast_to(cum[:, h:h + 1], (tm, LANES)))]
                aux_q = jnp.where(lane == AUX, hi, jnp.where(lane == AUX + 1, mid, jnp.where(lane == AUX + 2, lo, ones_q)))
                aux_k = jnp.where(lane == AUX + 3, -hi,
                                  jnp.where(lane == AUX + 4, -mid, jnp.where(lane == AUX + 5, -lo, ones_k)))
                blk = slice(h * LANES, (h + 1) * LANES)
                qa_ref[:, blk] = jnp.where(low, q, aux_q).astype(qa_ref.dtype)
                ka_ref[:, blk] = jnp.where(low, k, aux_k).astype(ka_ref.dtype)
                va_ref[:, blk] = jnp.where(low, v, 1.0).astype(va_ref.dtype)

    row = lambda i: (i, 0)
    return _pcall(
        body, name=name, grid=(nt,),
        in_specs=[pl.BlockSpec((tm, QKV_W), row), pl.BlockSpec((tm, LANES), row)],
        out_specs=[pl.BlockSpec((tm, FOX_PAD), row)] * 3,
        out_shape=[jax.ShapeDtypeStruct((T, FOX_PAD), MXU_DTYPE)] * 3,
        scratch_shapes=[pltpu.VMEM((8, LANES), f32)],
        compiler_params=_params(1),
    )(qkv, fgb)


def _future_keys(tq, tk):
    r = lax.broadcasted_iota(jnp.int32, (tq, tk), 0)
    c = lax.broadcasted_iota(jnp.int32, (tq, tk), 1)
    return c > r


def _causal_steps(nq, key_major):
    if key_major:
        pairs = [(qi, ki) for ki in range(nq) for qi in range(ki, nq)]
    else:
        pairs = [(qi, ki) for qi in range(nq) for ki in range(qi + 1)]
    return (jnp.asarray([p[0] for p in pairs], jnp.int32), jnp.asarray([p[1] for p in pairs], jnp.int32))


def _fox_fwd(qa, ka, va, *, name, tq=512, hps=8):
    T = qa.shape[0]
    tq = min(tq, T)
    tk = tq
    nq = T // tq
    rep = tk // LANES
    qi_tab, ki_tab = _causal_steps(nq, key_major=False)

    def body(qi_ref, ki_ref, qa_ref, ka_ref, va_ref, o_ref, lse_ref, m_sc, acc_sc):
        t = pl.program_id(1)
        qi = qi_ref[t]
        ki = ki_ref[t]

        @pl.when(ki == 0)
        def _():
            m_sc[...] = jnp.full_like(m_sc, NEG_BIG)
            acc_sc[...] = jnp.zeros_like(acc_sc)

        def tile(diagonal):
            for h in range(hps):
                blk = slice(h * LANES, (h + 1) * LANES)
                s = _dot_nt(qa_ref[:, blk], ka_ref[:, blk])
                if diagonal:
                    s = jnp.where(_future_keys(tq, tk), NEG_BIG, s)
                m_prev = m_sc[h]
                m_new = jnp.maximum(m_prev, jnp.max(s, axis=1, keepdims=True))
                p = jnp.exp(s - jnp.tile(m_new, (1, rep)))
                acc_sc[h] = jnp.exp(m_prev - m_new) * acc_sc[h] + _dot(p.astype(MXU_DTYPE), va_ref[:, blk])
                m_sc[h] = m_new

        @pl.when(ki < qi)
        def _():
            tile(False)

        @pl.when(ki == qi)
        def _():
            tile(True)
            low = _low_lanes((tq, LANES))
            outs = []
            for h in range(hps):
                acc = acc_sc[h]
                den = _swap_lane_halves(acc)
                outs.append(acc / den)
                lse_ref[h] = m_sc[h] + jnp.log(jnp.where(low, den, acc))
            for p in range(hps // 2):
                o_ref[:, p * LANES:(p + 1) * LANES] = jnp.where(low, outs[2 * p], _swap_lane_halves(outs[2 * p + 1]))

    pair = hps * LANES
    return _pcall(
        body, name=name,
        grid_spec=pltpu.PrefetchScalarGridSpec(
            num_scalar_prefetch=2, grid=(HEADS // hps, qi_tab.shape[0]),
            in_specs=[
                pl.BlockSpec((tq, pair), lambda j, t, qi_ref, ki_ref: (qi_ref[t], j)),
                pl.BlockSpec((tk, pair), lambda j, t, qi_ref, ki_ref: (ki_ref[t], j)),
                pl.BlockSpec((tk, pair), lambda j, t, qi_ref, ki_ref: (ki_ref[t], j)),
            ],
            out_specs=[pl.BlockSpec((tq, pair // 2), lambda j, t, qi_ref, ki_ref: (qi_ref[t], j)),
                       pl.BlockSpec((hps, tq, LANES), lambda j, t, qi_ref, ki_ref: (j, qi_ref[t], 0))],
            scratch_shapes=[pltpu.VMEM((hps, tq, LANES), f32)] * 2),
        out_shape=[jax.ShapeDtypeStruct((T, FOX_W), f32), jax.ShapeDtypeStruct((HEADS, T, LANES), f32)],
        compiler_params=_params(2),
    )(qi_tab, ki_tab, qa, ka, va)


def _fox_bwd_prep(do, o, *, name, tm=512):
    T = o.shape[0]
    tm = min(tm, T)
    nt = T // tm

    def body(do_ref, o_ref, d_ref, doa_ref):
        low = _low_lanes((tm, LANES))
        for j in range(HEADS // 2):
            do2 = do_ref[:, j * LANES:(j + 1) * LANES].astype(f32)
            prod = do2 * o_ref[:, j * LANES:(j + 1) * LANES]
            for odd in range(2):
                h = 2 * j + odd
                mine = jnp.where(low, _swap_lane_halves(prod) if odd else prod, 0.0)
                d_ref[h] = jnp.broadcast_to(jnp.sum(mine, axis=1, keepdims=True), (tm, LANES))
                doh = jnp.where(low, _swap_lane_halves(do2) if odd else do2, 0.0)
                doa_ref[:, h * LANES:(h + 1) * LANES] = doh.astype(doa_ref.dtype)

    return _pcall(
        body, name=name, grid=(nt,),
        in_specs=[pl.BlockSpec((tm, FOX_W), lambda i: (i, 0)), pl.BlockSpec((tm, FOX_W), lambda i: (i, 0))],
        out_specs=[pl.BlockSpec((HEADS, tm, LANES), lambda i: (0, i, 0)), pl.BlockSpec((tm, FOX_PAD), lambda i: (i, 0))],
        out_shape=[jax.ShapeDtypeStruct((HEADS, T, LANES), f32), jax.ShapeDtypeStruct((T, FOX_PAD), MXU_DTYPE)],
        compiler_params=_params(1),
    )(do, o)


def _fox_bwd(qa, ka, va, doa, lse, drep, *, name, tq=512, hps=4):
    T = qa.shape[0]
    tq = min(tq, T)
    tk = tq
    nq = T // tq
    rep = tk // LANES
    qi_tab, ki_tab = _causal_steps(nq, key_major=True)

    def body(qi_ref, ki_ref, qa_ref, ka_ref, va_ref, doa_ref, lse_ref, d_ref, dqa_ref, dka_ref, dva_ref, dk_sc, dv_sc):
        t = pl.program_id(1)
        qi = qi_ref[t]
        ki = ki_ref[t]
        rows = pl.ds(pl.multiple_of(qi * tq, tq), tq)

        @pl.when(t == 0)
        def _():
            dqa_ref[...] = jnp.zeros_like(dqa_ref)

        @pl.when(qi == ki)
        def _():
            dk_sc[...] = jnp.zeros_like(dk_sc)
            dv_sc[...] = jnp.zeros_like(dv_sc)

        def tile(diagonal):
            for h in range(hps):
                blk = slice(h * LANES, (h + 1) * LANES)
                qh, kh, doh = qa_ref[:, blk], ka_ref[:, blk], doa_ref[:, blk]
                p = jnp.exp(_dot_nt(qh, kh) - jnp.tile(lse_ref[h], (1, rep)))
                if diagonal:
                    p = jnp.where(_future_keys(tq, tk), 0.0, p)
                dp = _dot_nt(doh, va_ref[:, blk])
                ds = (p * (dp - jnp.tile(d_ref[h], (1, rep)))).astype(MXU_DTYPE)
                dv_sc[h] += _dot_tn(p.astype(MXU_DTYPE), doh)
                dk_sc[h] += _dot_tn(ds, qh)
                dqa_ref[rows, blk] += _dot(ds, kh)

        @pl.when(qi > ki)
        def _():
            tile(False)

        @pl.when(qi == ki)
        def _():
            tile(True)

        @pl.when(qi == nq - 1)
        def _():
            for h in range(hps):
                blk = slice(h * LANES, (h + 1) * LANES)
                dka_ref[:, blk] = dk_sc[h]
                dva_ref[:, blk] = dv_sc[h]

    pair = hps * LANES
    q_blk = lambda j, t, qi_ref, ki_ref: (qi_ref[t], j)
    k_blk = lambda j, t, qi_ref, ki_ref: (ki_ref[t], j)
    stat = pl.BlockSpec((hps, tq, LANES), lambda j, t, qi_ref, ki_ref: (j, qi_ref[t], 0))
    return _pcall(
        body, name=name,
        grid_spec=pltpu.PrefetchScalarGridSpec(
            num_scalar_prefetch=2, grid=(HEADS // hps, qi_tab.shape[0]),
            in_specs=[pl.BlockSpec((tq, pair), q_blk), pl.BlockSpec((tk, pair), k_blk), pl.BlockSpec((tk, pair), k_blk),
                      pl.BlockSpec((tq, pair), q_blk), stat, stat],
            out_specs=[pl.BlockSpec((T, pair), lambda j, t, qi_ref, ki_ref: (0, j)),
                       pl.BlockSpec((tk, pair), k_blk), pl.BlockSpec((tk, pair), k_blk)],
            scratch_shapes=[pltpu.VMEM((hps, tk, LANES), f32)] * 2),
        out_shape=[jax.ShapeDtypeStruct((T, FOX_PAD), f32)] * 3,
        compiler_params=_params(2),
    )(qi_tab, ki_tab, qa, ka, va, doa, lse, drep)


def _fox_bwd_post(dqa, dka, fgb, *, name, tm=512):
    T = fgb.shape[0]
    tm = min(tm, T)
    nt = T // tm

    def body(dqa_ref, dka_ref, fg_ref, dfg_ref, dbf_ref, carry):
        i = pl.program_id(0)

        @pl.when(i == 0)
        def _():
            carry[...] = jnp.zeros_like(carry)
            dbf_ref[...] = jnp.zeros_like(dbf_ref)

        lane = lax.broadcasted_iota(jnp.int32, (tm, LANES), 1)
        dc = jnp.zeros((tm, LANES), f32)
        for h in range(HEADS):
            row_sum = dqa_ref[:, h * LANES + AUX:h * LANES + AUX + 1]
            col_sum = dka_ref[:, h * LANES + AUX + 3:h * LANES + AUX + 4]
            dc = jnp.where(lane == h, jnp.broadcast_to(row_sum - col_sum, (tm, LANES)), dc)
        r = lax.broadcasted_iota(jnp.int32, (tm, tm), 0)
        c = lax.broadcasted_iota(jnp.int32, (tm, tm), 1)
        tri = jnp.where(c >= r, 1.0, 0.0).astype(jnp.bfloat16)
        dls = _tri_dot(tri, dc) + carry[0:1, :]
        carry[...] = jnp.broadcast_to(dls[0:1, :], carry.shape)
        dfg = dls * _sigmoid(-fg_ref[...])
        dfg_ref[...] = dfg
        dbf_ref[...] += jnp.sum(dfg, axis=0, keepdims=True)

    rev = lambda i: (nt - 1 - i, 0)
    return _pcall(
        body, name=name, grid=(nt,),
        in_specs=[pl.BlockSpec((tm, FOX_PAD), rev), pl.BlockSpec((tm, FOX_PAD), rev), pl.BlockSpec((tm, LANES), rev)],
        out_specs=[pl.BlockSpec((tm, LANES), rev), pl.BlockSpec((1, LANES), lambda i: (0, 0))],
        out_shape=[jax.ShapeDtypeStruct((T, LANES), f32), jax.ShapeDtypeStruct((1, LANES), f32)],
        scratch_shapes=[pltpu.VMEM((8, LANES), f32)],
        compiler_params=_params(1),
    )(dqa, dka, fgb)


GELU_C = math.sqrt(2.0 / math.pi)
GELU_A = 0.044715


def _gelu(x):
    t = jnp.tanh(GELU_C * (x + GELU_A * x * x * x))
    return 0.5 * x * (1.0 + t), t


def _gelu_grad(x, t):
    return 0.5 * (1.0 + t) + 0.5 * x * (1.0 - t * t) * GELU_C * (1.0 + 3.0 * GELU_A * x * x)


def _expm1(x):
    e = jnp.exp(x)
    safe = jnp.where(e == 1.0, x, (e - 1.0) * x / jnp.log(jnp.where(e == 1.0, 0.5, e)))
    return jnp.where(x < -0.5, e - 1.0, safe)


def _lru_gates(u, wab_ref, bab_ref, lam_ref):
    pre = _dot(u.astype(MXU_DTYPE), wab_ref[...]) + bab_ref[...]
    r = _sigmoid(pre[:, :LRU_W])
    gi = _sigmoid(pre[:, LRU_W:])
    lam = lam_ref[...]
    sp = jnp.maximum(-lam, 0.0) + jnp.log(1.0 + jnp.exp(-jnp.abs(lam)))
    log_a = -LRU_C * r * sp
    a = jnp.exp(log_a)
    s = jnp.sqrt(-_expm1(2.0 * log_a))
    return r, gi, sp, a, s


def _lru_fwd(lxg, conv_w, conv_b, wab, bab, lam, *, name, tc=512):
    T = lxg.shape[0]
    tc = min(tc, T)
    nc = T // tc

    def body(lx_ref, lg_ref, cw_ref, cb_ref, wab_ref, bab_ref, lam_ref,
             out_ref, u_ref, hs_ref, ext, a_sc, b_sc, h_sc):
        i = pl.program_id(0)

        @pl.when(i == 0)
        def _():
            ext[0:8, :] = jnp.zeros((8, LRU_W), f32)
            h_sc[...] = jnp.zeros_like(h_sc)

        ext[8:, :] = lx_ref[...]
        u = cb_ref[...] + cw_ref[0:1, :] * ext[pl.ds(5, tc), :]
        for k in range(1, CONV_K):
            u = u + cw_ref[k:k + 1, :] * ext[pl.ds(5 + k, tc), :]
        ext[0:8, :] = ext[tc:tc + 8, :]
        u_ref[...] = u
        r, gi, sp, a, s = _lru_gates(u, wab_ref, bab_ref, lam_ref)
        a_sc[...] = a
        b_sc[...] = s * (gi * u)

        def step(t, h):
            h = a_sc[pl.ds(t, 1), :] * h + b_sc[pl.ds(t, 1), :]
            hs_ref[pl.ds(t, 1), :] = h
            return h

        h = lax.fori_loop(0, tc, step, h_sc[0:1, :], unroll=8)
        h_sc[...] = jnp.broadcast_to(h, h_sc.shape)
        gel, _ = _gelu(lg_ref[...])
        out_ref[...] = gel * hs_ref[...]

    row = lambda i: (i, 0)
    const = lambda i: (0, 0)
    return _pcall(
        body, name=name, grid=(nc,),
        in_specs=[pl.BlockSpec((tc, LRU_W), row), pl.BlockSpec((tc, LRU_W), lambda i: (i, 1)),
                  pl.BlockSpec((CONV_K, LRU_W), const), pl.BlockSpec((1, LRU_W), const),
                  pl.BlockSpec((LRU_W, 2 * LRU_W), const), pl.BlockSpec((1, 2 * LRU_W), const),
                  pl.BlockSpec((1, LRU_W), const)],
        out_specs=[pl.BlockSpec((tc, LRU_W), row)] * 3,
        out_shape=[jax.ShapeDtypeStruct((T, LRU_W), f32)] * 3,
        scratch_shapes=[pltpu.VMEM((tc + 8, LRU_W), f32), pltpu.VMEM((tc, LRU_W), f32),
                        pltpu.VMEM((tc, LRU_W), f32), pltpu.VMEM((8, LRU_W), f32)],
        compiler_params=_params(1),
    )(lxg, lxg, conv_w, conv_b, wab, bab, lam)


def _lru_bwd(dlru, lxg, u, hs, conv_w, wab, bab, lam, *, name, tc=512):
    T = lxg.shape[0]
    tc = min(tc, T)
    nc = T // tc
    bp = tc // 8

    def body(dl_ref, lx_ref, lxp_ref, lg_ref, u_ref, hs_ref, hsp_ref, cw_ref, wab_ref, bab_ref, lam_ref,
             dlxg_ref, dwab_ref, dbab_ref, dcw_ref, dcb_ref, dlam_ref,
             dh_sc, a_sc, ext, du_ext, carry):
        i = pl.program_id(0)
        first_chunk = i == nc - 1

        @pl.when(i == 0)
        def _():
            dwab_ref[...] = jnp.zeros_like(dwab_ref)
            dbab_ref[...] = jnp.zeros_like(dbab_ref)
            dcw_ref[...] = jnp.zeros_like(dcw_ref)
            dcb_ref[...] = jnp.zeros_like(dcb_ref)
            dlam_ref[...] = jnp.zeros_like(dlam_ref)
            carry[...] = jnp.zeros_like(carry)
            du_ext[tc:tc + 8, :] = jnp.zeros((8, LRU_W), f32)

        lg = lg_ref[...]
        gel, th = _gelu(lg)
        dl = dl_ref[...]
        hs = hs_ref[...]
        dlg = dl * hs * _gelu_grad(lg, th)
        u = u_ref[...]
        r, gi, sp, a, s = _lru_gates(u, wab_ref, bab_ref, lam_ref)
        a_sc[...] = a
        dh_sc[...] = dl * gel

        def step(k, c):
            t = tc - 1 - k
            dh = dh_sc[pl.ds(t, 1), :] + c
            dh_sc[pl.ds(t, 1), :] = dh
            return a_sc[pl.ds(t, 1), :] * dh

        c = lax.fori_loop(0, tc, step, carry[0:1, :], unroll=8)
        carry[...] = jnp.broadcast_to(c, carry.shape)

        ext[0:8, :] = jnp.where(first_chunk, 0.0, hsp_ref[...])
        ext[8:, :] = hs
        hprev = ext[pl.ds(7, tc), :]
        dh = dh_sc[...]
        da = dh * hprev
        giu = gi * u
        dla = da * a - (dh * giu) * (a * a / s)
        dgi = dh * s * u
        du = dh * s * gi
        dr = dla * (-LRU_C * sp)
        dlam_ref[...] += jnp.sum(dla * (-LRU_C * r), axis=0, keepdims=True) * (-_sigmoid(-lam_ref[...]))
        dpre = jnp.concatenate([dr * r * (1.0 - r), dgi * gi * (1.0 - gi)], axis=1)
        dpre_b = dpre.astype(MXU_DTYPE)
        du = du + _dot_nt(dpre_b, wab_ref[...])
        dwab_ref[...] += _dot_tn(u.astype(MXU_DTYPE), dpre_b)
        dbab_ref[...] += jnp.sum(dpre, axis=0, keepdims=True)
        dcb_ref[...] += jnp.sum(du, axis=0, keepdims=True)

        du_ext[0:tc, :] = du
        dlx = cw_ref[0:1, :] * du_ext[pl.ds(3, tc), :]
        for k in range(1, CONV_K):
            dlx = dlx + cw_ref[k:k + 1, :] * du_ext[pl.ds(3 - k, tc), :]
        du_ext[tc:tc + 8, :] = du_ext[0:8, :]
        ext[0:8, :] = jnp.where(first_chunk, 0.0, lxp_ref[...])
        ext[8:, :] = lx_ref[...]
        for k in range(CONV_K):
            dcw_ref[k:k + 1, :] += jnp.sum(du * ext[pl.ds(5 + k, tc), :], axis=0, keepdims=True)
        dlxg_ref[:, :LRU_W] = dlx.astype(dlxg_ref.dtype)
        dlxg_ref[:, LRU_W:] = dlg.astype(dlxg_ref.dtype)

    rev = lambda i: (nc - 1 - i, 0)
    prev8 = lambda i: (jnp.maximum((nc - 1 - i) * bp - 1, 0), 0)
    const = lambda i: (0, 0)
    return _pcall(
        body, name=name, grid=(nc,),
        in_specs=[
            pl.BlockSpec((tc, LRU_W), rev),
            pl.BlockSpec((tc, LRU_W), rev),
            pl.BlockSpec((8, LRU_W), prev8),
            pl.BlockSpec((tc, LRU_W), lambda i: (nc - 1 - i, 1)),
            pl.BlockSpec((tc, LRU_W), rev),
            pl.BlockSpec((tc, LRU_W), rev),
            pl.BlockSpec((8, LRU_W), prev8),
            pl.BlockSpec((CONV_K, LRU_W), const),
            pl.BlockSpec((LRU_W, 2 * LRU_W), const),
            pl.BlockSpec((1, 2 * LRU_W), const),
            pl.BlockSpec((1, LRU_W), const),
        ],
        out_specs=[
            pl.BlockSpec((tc, 2 * LRU_W), rev),
            pl.BlockSpec((LRU_W, 2 * LRU_W), const),
            pl.BlockSpec((1, 2 * LRU_W), const),
            pl.BlockSpec((8, LRU_W), const),
            pl.BlockSpec((1, LRU_W), const),
            pl.BlockSpec((1, LRU_W), const),
        ],
        out_shape=[
            jax.ShapeDtypeStruct((T, 2 * LRU_W), MXU_DTYPE),
            jax.ShapeDtypeStruct((LRU_W, 2 * LRU_W), f32),
            jax.ShapeDtypeStruct((1, 2 * LRU_W), f32),
            jax.ShapeDtypeStruct((8, LRU_W), f32),
            jax.ShapeDtypeStruct((1, LRU_W), f32),
            jax.ShapeDtypeStruct((1, LRU_W), f32),
        ],
        scratch_shapes=[pltpu.VMEM((tc, LRU_W), f32), pltpu.VMEM((tc, LRU_W), f32),
                        pltpu.VMEM((tc + 8, LRU_W), f32), pltpu.VMEM((tc + 8, LRU_W), f32),
                        pltpu.VMEM((8, LRU_W), f32)],
        compiler_params=_params(1),
    )(dlru, lxg, lxg, lxg, u, hs, hs, conv_w, wab, bab, lam)


def _mix_out(fox, lru, wo, xhat1, g1, b1, g2, b2, *, name, tm=512):
    T = fox.shape[0]
    tm = min(tm, T)
    nt = T // tm

    def body(fox_ref, lru_ref, wo_ref, xh_ref, g1_ref, b1_ref, g2_ref, b2_ref, xhat_ref, xn_ref, rstd_ref):
        mix = _dot(fox_ref[...].astype(MXU_DTYPE), wo_ref[:FOX_W, :])
        mix = mix + _dot(lru_ref[...].astype(MXU_DTYPE), wo_ref[FOX_W:, :])
        x1 = xh_ref[...] * g1_ref[...] + b1_ref[...]
        xhat, rstd = _layer_norm_stats(DN_ALPHA * x1 + mix)
        xhat_ref[...] = xhat
        xn_ref[...] = xhat * g2_ref[...] + b2_ref[...]
        rstd_ref[...] = jnp.broadcast_to(rstd, rstd_ref.shape)

    row = lambda i: (i, 0)
    const = lambda i: (0, 0)
    vec = pl.BlockSpec((1, D_MODEL), const)
    return _pcall(
        body, name=name, grid=(nt,),
        in_specs=[pl.BlockSpec((tm, FOX_W), row), pl.BlockSpec((tm, LRU_W), row),
                  pl.BlockSpec((D_MODEL, D_MODEL), const), pl.BlockSpec((tm, D_MODEL), row), vec, vec, vec, vec],
        out_specs=[pl.BlockSpec((tm, D_MODEL), row), pl.BlockSpec((tm, D_MODEL), row),
                   pl.BlockSpec((tm, LANES), row)],
        out_shape=[jax.ShapeDtypeStruct((T, D_MODEL), f32), jax.ShapeDtypeStruct((T, D_MODEL), f32),
                   jax.ShapeDtypeStruct((T, LANES), f32)],
        compiler_params=_params(1),
    )(fox, lru, wo, xhat1, g1, b1, g2, b2)


def _mix_out_bwd(dyp, fox, lru, wo, *, name, tm=512):
    T = fox.shape[0]
    tm = min(tm, T)
    nt = T // tm

    def body(dyp_ref, fox_ref, lru_ref, wo_ref, dfox_ref, dlru_ref, dwo_ref):
        i = pl.program_id(0)

        @pl.when(i == 0)
        def _():
            dwo_ref[...] = jnp.zeros_like(dwo_ref)

        dmix = dyp_ref[...].astype(MXU_DTYPE)
        dcat = _dot_nt(dmix, wo_ref[...])
        dfox_ref[...] = dcat[:, :FOX_W].astype(dfox_ref.dtype)
        dlru_ref[...] = dcat[:, FOX_W:]
        dwo_ref[:FOX_W, :] += _dot_tn(fox_ref[...].astype(MXU_DTYPE), dmix)
        dwo_ref[FOX_W:, :] += _dot_tn(lru_ref[...].astype(MXU_DTYPE), dmix)

    row = lambda i: (i, 0)
    const = lambda i: (0, 0)
    return _pcall(
        body, name=name, grid=(nt,),
        in_specs=[pl.BlockSpec((tm, D_MODEL), row), pl.BlockSpec((tm, FOX_W), row), pl.BlockSpec((tm, LRU_W), row),
                  pl.BlockSpec((D_MODEL, D_MODEL), const)],
        out_specs=[pl.BlockSpec((tm, FOX_W), row), pl.BlockSpec((tm, LRU_W), row),
                   pl.BlockSpec((D_MODEL, D_MODEL), const)],
        out_shape=[jax.ShapeDtypeStruct((T, FOX_W), MXU_DTYPE), jax.ShapeDtypeStruct((T, LRU_W), f32),
                   jax.ShapeDtypeStruct((D_MODEL, D_MODEL), f32)],
        compiler_params=_params(1),
    )(dyp, fox, lru, wo)


def make_wp(w_in):
    scale = jnp.concatenate([jnp.full((FOX_W,), 1.0 / math.sqrt(HEAD_DIM), w_in.dtype),
                             jnp.ones((IN_COLS - FOX_W,), w_in.dtype)])
    return jnp.pad(w_in * scale[None, :], ((0, 0), (0, Z_PAD - IN_COLS)))


def _block_diag(w):
    eye = jnp.eye(HEADS, dtype=w.dtype)
    return jnp.einsum("hij,hg->higj", w, eye).reshape(LRU_W, LRU_W)


def _block_diag_extract(m):
    m4 = m.reshape(HEADS, HEAD_DIM, HEADS, HEAD_DIM)
    return jnp.stack([m4[h, :, h, :] for h in range(HEADS)])


class _NoOverlap:
    def start_token(self):
        return None

    def late_weights(self, w, after):
        return dict(f1d=w["f1d"], wp=w["wp"], wo=w["wo"])

    def after_attention(self, after):
        return None

    def ffn2_weights(self, w, after):
        return w["f2g"], w["f2u"], w["f2d"]

    def ffn2_grads(self, grads):
        return None

    def ffn1_grads(self, grads):
        return None

    def mixer_grads(self, dwp, dwo, small, loss):
        return None

    def before_ffn1_bwd(self, after):
        return None


def _tied(a, token):
    return a if token is None else a + token[0, 0]


def _local_step(x, target, w, hooks=None):
    hooks = hooks or _NoOverlap()
    bfp = w["bfp"]
    wab = jnp.concatenate([_block_diag(w["rg_wa"]), _block_diag(w["rg_wx"])], axis=1).astype(MXU_DTYPE)
    bab = jnp.concatenate([w["rg_ba"].reshape(1, LRU_W), w["rg_bx"].reshape(1, LRU_W)], axis=1)

    xb0, g1a, u1a, h1a = _ffn_up(x, w["f1g"], w["f1u"], hooks.start_token(), name="ffn1_up")
    late = hooks.late_weights(w, [h1a])
    f1d, wp, wo = late["f1d"], late["wp"], late["wo"]
    xhat1, xn1, rstd1 = _ffn_down_ln(x, h1a, f1d, w["ln1_g"], w["ln1_b"], name="ffn1_down")
    qkv, lxg, fgb = _proj_in(xn1, wp, bfp, name="proj_in")
    qa, ka, va = _fox_prep(qkv, fgb, name="fox_prep")
    fox, lse = _fox_fwd(qa, ka, va, name="fox_fwd")
    token = hooks.after_attention([lse])
    lru, uconv, hs = _lru_fwd(lxg, w["conv_w"], _tied(w["conv_b"], token), wab, bab, w["lam"], name="lru_fwd")
    xhat2, x2, rstd2 = _mix_out(fox, lru, wo, xhat1, w["ln1_g"], w["ln1_b"], w["ln2_g"], w["ln2_b"], name="mix_out")
    f2g, f2u, f2d = hooks.ffn2_weights(w, [rstd2])
    xb2, g2a, u2a, xhat3, _, rstd3 = _ffn_fwd(x2, f2g, f2u, f2d, w["ln3_g"], w["ln3_b"], name="ffn2_fwd")

    dy3p, dln3g, dln3b, loss = _loss_ln_bwd(xhat3, rstd3, w["ln3_g"], w["ln3_b"], target, name="loss_ln3_bwd")
    dx2, df2g, df2u, df2d = _ffn_bwd(dy3p, xb2, g2a, u2a, f2g, f2u, f2d, name="ffn2_bwd")
    token = hooks.ffn2_grads([df2g, df2u, df2d])
    dy2p, dln2g, dln2b = _ln_bwd(dx2, xhat2, rstd2, _tied(w["ln2_g"], token), name="ln2_bwd")
    dfox, dlru, dwo = _mix_out_bwd(dy2p, fox, lru, wo, name="mix_out_bwd")
    dlxg, dwab, dbab, dcw, dcb, dlam = _lru_bwd(dlru, lxg, uconv, hs, w["conv_w"], wab, bab, w["lam"], name="lru_bwd")
    drep, doa = _fox_bwd_prep(dfox, fox, name="fox_bwd_prep")
    dqa, dka, dva = _fox_bwd(qa, ka, va, doa, lse, drep, name="fox_bwd")
    dfg, dbf = _fox_bwd_post(dqa, dka, fgb, name="fox_bwd_post")
    dx1, dwp = _proj_in_bwd(dqa, dka, dva, dlxg, dfg, xn1, dy2p, wp, name="proj_in_bwd")
    dy1p, dln1g, dln1b = _ln_bwd(dx1, xhat1, rstd1, w["ln1_g"], name="ln1_bwd")
    small = dict(
        ln1_g=dln1g, ln1_b=dln1b, ln2_g=dln2g, ln2_b=dln2b, ln3_g=dln3g, ln3_b=dln3b,
        b_forget=dbf[:, :HEADS], conv_w=dcw[:CONV_K], conv_b=dcb,
        rg_wa=_block_diag_extract(dwab[:, :LRU_W]), rg_wx=_block_diag_extract(dwab[:, LRU_W:]),
        rg_ba=dbab[:, :LRU_W].reshape(HEADS, HEAD_DIM), rg_bx=dbab[:, LRU_W:].reshape(HEADS, HEAD_DIM),
        lru_lambda=dlam,
    )
    hooks.before_ffn1_bwd([dln1b])
    token = hooks.mixer_grads(dwp, dwo, small, loss)
    dx_a, *grads_a = _ffn_bwd(dy1p, xb0, g1a, u1a, w["f1g"], w["f1u"], f1d, token, name="ffn1_bwd_a", part=0)
    token = hooks.ffn1_grads(grads_a)
    dx, *grads_b = _ffn_bwd(dy1p, xb0, g1a, u1a, w["f1g"], w["f1u"], f1d, token, name="ffn1_bwd_b", part=1,
                            dx_init=dx_a)

    grads = dict(f1=(grads_a, grads_b), f2g=df2g, f2u=df2u, f2d=df2d, wp=dwp, wo=dwo, **small)
    return loss, dx, grads


MESH = pl.DeviceIdType.MESH
HBM_SPEC = pl.BlockSpec(memory_space=pl.ANY)
VMEM_SPEC = pl.BlockSpec(memory_space=pltpu.VMEM)


def _position():
    return lax.axis_index("x"), lax.axis_index("y"), lax.axis_index("c")


def _other_chips(x, y):
    return [(1 - x, y), (x, 1 - y), (1 - x, 1 - y)]


def _all_gather_bf16(shards, *, name):
    n = len(shards)

    def body(*refs):
        ins, outs, stages = refs[:n], refs[n:2 * n], refs[2 * n:3 * n]
        send_sems, recv_sems, local_sems = refs[3 * n:]
        x, y, c = _position()
        me, sibling = (x, y, c), (x, y, 1 - c)
        chips = _other_chips(x, y)

        def rows(k, px, py, pc):
            r = shards[k].shape[0]
            m = r // 2
            return outs[k].at[pl.ds(pl.multiple_of((2 * px + py) * r + pc * m, 16), m), :]

        def copy(k, idx, block, to, src=None):
            return pltpu.make_async_remote_copy(
                src_ref=rows(k, *block) if src is None else src, dst_ref=rows(k, *block),
                send_sem=send_sems.at[7 * k + idx], recv_sem=recv_sems.at[7 * k + idx],
                device_id=to, device_id_type=MESH)

        started = []
        mine = []
        for k in range(n):
            m = shards[k].shape[0] // 2
            stages[k][...] = ins[k][pl.ds(pl.multiple_of(c * m, 16), m), :].astype(stages[k].dtype)
            cp = pltpu.make_async_copy(stages[k], rows(k, *me), local_sems.at[k])
            cp.start()
            mine.append(cp)
            first = [copy(k, 0, me, sibling, src=stages[k])]
            first += [copy(k, 1 + j, me, (*chip, c), src=stages[k]) for j, chip in enumerate(chips)]
            for cp in first:
                cp.start()
            started += first
        for k in range(n):
            for j, chip in enumerate(chips):
                copy(k, 1 + j, (*chip, c), me).wait_recv()
                fwd = copy(k, 4 + j, (*chip, c), sibling)
                fwd.start()
                started.append(fwd)
        for k in range(n):
            copy(k, 0, sibling, me).wait_recv()
            for j, chip in enumerate(chips):
                copy(k, 4 + j, (*chip, 1 - c), me).wait_recv()
        for cp in started:
            cp.wait_send()
        for cp in mine:
            cp.wait()

    return _pcall(
        body, name=name,
        in_specs=[VMEM_SPEC] * n, out_specs=[HBM_SPEC] * n,
        out_shape=[jax.ShapeDtypeStruct((N_SHARD * s.shape[0], s.shape[1]), MXU_DTYPE) for s in shards],
        scratch_shapes=[pltpu.VMEM((s.shape[0] // 2, s.shape[1]), MXU_DTYPE) for s in shards]
        + [pltpu.SemaphoreType.DMA((7 * n,)), pltpu.SemaphoreType.DMA((7 * n,)), pltpu.SemaphoreType.DMA((n,))],
        compiler_params=pltpu.CompilerParams(vmem_limit_bytes=VMEM_LIMIT),
    )(*shards)


def _swap_halves(gs, *, name):
    n = len(gs)

    def body(*refs):
        ins, outs = refs[:n], refs[n:2 * n]
        send_sems, recv_sems = refs[2 * n:]
        x, y, c = _position()
        cps = []
        for k in range(n):
            m = gs[k].shape[1] // 2
            src = ins[k].at[:, pl.ds(pl.multiple_of((1 - c) * m, 16), m), :]
            cp = pltpu.make_async_remote_copy(src_ref=src, dst_ref=outs[k], send_sem=send_sems.at[k],
                                              recv_sem=recv_sems.at[k], device_id=(x, y, 1 - c), device_id_type=MESH)
            cp.start()
            cps.append(cp)
        for cp in cps:
            cp.wait()

    return _pcall(
        body, name=name, in_specs=[HBM_SPEC] * n, out_specs=[HBM_SPEC] * n,
        out_shape=[jax.ShapeDtypeStruct((g.shape[0], g.shape[1] // 2, g.shape[2]), g.dtype) for g in gs],
        scratch_shapes=[pltpu.SemaphoreType.DMA((n,)), pltpu.SemaphoreType.DMA((n,))],
    )(*gs)


def _add_halves(gs, recvs, *, name, tm=256):
    n = len(gs)
    _, r, cdim = gs[0].shape
    m = r // 2
    tm = min(tm, m)
    nb = m // tm
    c_idx = lax.axis_index("c").astype(jnp.int32).reshape(1)

    def body(c_ref, *refs):
        for k in range(n):
            refs[2 * n + k][...] = (refs[k][...].astype(f32) + refs[n + k][...].astype(f32)).astype(refs[2 * n + k].dtype)

    mine = pl.BlockSpec((None, tm, cdim), lambda j, i, c_ref: (j, c_ref[0] * nb + i, 0))
    half = pl.BlockSpec((None, tm, cdim), lambda j, i, c_ref: (j, i, 0))
    return _pcall(
        body, name=name,
        grid_spec=pltpu.PrefetchScalarGridSpec(
            num_scalar_prefetch=1, grid=(N_SHARD, nb),
            in_specs=[mine] * n + [half] * n, out_specs=[half] * n),
        out_shape=[jax.ShapeDtypeStruct((N_SHARD, m, cdim), g.dtype) for g in gs],
        compiler_params=_params(2),
    )(c_idx, *gs, *recvs)


def _scatter_partials(ps, *, name):
    n = len(ps)

    def body(*refs):
        ins, outs = refs[:n], refs[n:2 * n]
        send_sems, recv_sems = refs[2 * n:]
        x, y, c = _position()
        me_chip = 2 * x + y
        cps = []
        for k in range(n):
            for j, (px, py) in enumerate(_other_chips(x, y)):
                cp = pltpu.make_async_remote_copy(
                    src_ref=ins[k].at[2 * px + py], dst_ref=outs[k].at[me_chip],
                    send_sem=send_sems.at[3 * k + j], recv_sem=recv_sems.at[3 * k + j],
                    device_id=(px, py, c), device_id_type=MESH)
                cp.start()
                cps.append(cp)
        for cp in cps:
            cp.wait()

    return _pcall(
        body, name=name, in_specs=[HBM_SPEC] * n, out_specs=[HBM_SPEC] * n,
        out_shape=[jax.ShapeDtypeStruct(p.shape, p.dtype) for p in ps],
        scratch_shapes=[pltpu.SemaphoreType.DMA((3 * n,)), pltpu.SemaphoreType.DMA((3 * n,))],
    )(*ps)


def _sum_slabs(ps, qs, *, name, tm=128):
    n = len(qs)
    _, m, cdim = qs[0].shape
    tm = min(tm, m)
    nb = m // tm
    assert m % tm == 0, (m, tm)
    where = jnp.stack([2 * lax.axis_index("x") + lax.axis_index("y"), lax.axis_index("c")]).astype(jnp.int32)

    def body(w_ref, *refs):
        for k in range(n):
            own, q1, q2, q3 = (refs[4 * k + t][...].astype(f32) for t in range(4))
            refs[4 * n + k][...] = ((own + q1) + q2) + q3

    def slab(flip):
        return pl.BlockSpec((None, tm, cdim), lambda i, w_ref: (jnp.bitwise_xor(w_ref[0], flip), i, 0))

    operands = []
    for p, q in zip(ps, qs):
        operands += [p, q, q, q]
    return _pcall(
        body, name=name,
        grid_spec=pltpu.PrefetchScalarGridSpec(
            num_scalar_prefetch=1, grid=(nb,),
            in_specs=[slab(0), slab(2), slab(1), slab(3)] * n,
            out_specs=[pl.BlockSpec((tm, cdim), lambda i, w_ref: (w_ref[1] * nb + i, 0))] * n),
        out_shape=[jax.ShapeDtypeStruct((2 * m, cdim), f32) for _ in qs],
        compiler_params=_params(1),
    )(where, *operands)


def _join_halves(fs, *, name):
    n = len(fs)

    def body(*refs):
        outs = refs[n:2 * n]
        send_sems, recv_sems = refs[2 * n:]
        x, y, c = _position()
        cps = []
        for k in range(n):
            m = fs[k].shape[0] // 2
            half = outs[k].at[pl.ds(pl.multiple_of(c * m, 8), m), :]
            cp = pltpu.make_async_remote_copy(src_ref=half, dst_ref=half, send_sem=send_sems.at[k],
                                              recv_sem=recv_sems.at[k], device_id=(x, y, 1 - c), device_id_type=MESH)
            cp.start()
            cps.append(cp)
        for cp in cps:
            cp.wait()

    return _pcall(
        body, name=name, in_specs=[HBM_SPEC] * n, out_specs=[HBM_SPEC] * n,
        out_shape=[jax.ShapeDtypeStruct(f.shape, f.dtype) for f in fs],
        input_output_aliases={k: k for k in range(n)},
        scratch_shapes=[pltpu.SemaphoreType.DMA((n,)), pltpu.SemaphoreType.DMA((n,))],
    )(*fs)


def _all_reduce_small(v, after=None, *, name):
    r = v.shape[0]
    extra = [] if after is None else [after]

    def body(v_ref, *refs):
        out_ref, buf, send_sems, recv_sems, local_sem = refs[len(extra):]
        x, y, c = _position()
        me, sibling = (x, y, c), (x, y, 1 - c)
        chips = _other_chips(x, y)

        def rows(px, py, pc):
            return buf.at[pl.ds(pl.multiple_of((4 * px + 2 * py + pc) * r, 8), r), :]

        def copy(k, block, to, src=None):
            return pltpu.make_async_remote_copy(
                src_ref=rows(*block) if src is None else src, dst_ref=rows(*block),
                send_sem=send_sems.at[k], recv_sem=recv_sems.at[k], device_id=to, device_id_type=MESH)

        mine = pltpu.make_async_copy(v_ref, rows(*me), local_sem)
        mine.start()
        first = [copy(0, me, sibling, src=v_ref)]
        first += [copy(1 + j, me, (*chip, c), src=v_ref) for j, chip in enumerate(chips)]
        for cp in first:
            cp.start()
        passed = [copy(4 + j, (*chip, c), sibling) for j, chip in enumerate(chips)]
        for j, chip in enumerate(chips):
            copy(1 + j, (*chip, c), me).wait_recv()
            passed[j].start()
        copy(0, sibling, me).wait_recv()
        for j, chip in enumerate(chips):
            copy(4 + j, (*chip, 1 - c), me).wait_recv()
        for cp in first + passed:
            cp.wait_send()
        mine.wait()
        acc = buf[0:r, :]
        for d in range(1, N_DEV):
            acc = acc + buf[d * r:(d + 1) * r, :]
        out_ref[...] = acc

    return _pcall(
        body, name=name, in_specs=[VMEM_SPEC] + [HBM_SPEC] * len(extra), out_specs=VMEM_SPEC,
        out_shape=jax.ShapeDtypeStruct((r, LANES), f32),
        scratch_shapes=[pltpu.VMEM((N_DEV * r, LANES), f32), pltpu.SemaphoreType.DMA((7,)),
                        pltpu.SemaphoreType.DMA((7,)), pltpu.SemaphoreType.DMA],
    )(v, *extra)


SEM_SPEC = pl.BlockSpec(memory_space=pltpu.SEMAPHORE)
HBM_ONLY = pl.BlockSpec(memory_space=pltpu.HBM)
EFFECT = pltpu.SideEffectType.DATAFLOW_SIDE_EFFECTING


def _sends(copies):
    return copies[0] if isinstance(copies, tuple) else copies


def _arrivals(copies):
    return copies[1] if isinstance(copies, tuple) else copies


def _split_start(bufs, copies_fn, n_sems, *, name):
    n = len(bufs)

    def body(*refs):
        send_sems, recv_sems = refs[n], refs[n + 1]
        thru = refs[n + 2:2 * n + 2]
        token = refs[2 * n + 2]
        for cp in _sends(copies_fn(thru, send_sems, recv_sems)):
            cp.start()
        token[...] = jnp.zeros_like(token)

    outs = _pcall(
        body, name=name,
        out_shape=(pltpu.SemaphoreType.DMA((n_sems,)), pltpu.SemaphoreType.DMA((n_sems,)),
                   *[pltpu.HBM(b.shape, b.dtype) for b in bufs], jax.ShapeDtypeStruct((8, LANES), f32)),
        in_specs=[HBM_ONLY] * n,
        out_specs=(SEM_SPEC, SEM_SPEC, *[HBM_ONLY] * n, VMEM_SPEC),
        input_output_aliases={k: 2 + k for k in range(n)},
        compiler_params=pltpu.CompilerParams(has_side_effects=EFFECT),
    )(*[pltpu.with_memory_space_constraint(b, pltpu.HBM) for b in bufs])
    return outs[0], outs[1], list(outs[2:2 + n]), outs[2 + n]


def _split_wait(thru, send_sems, recv_sems, after, copies_fn, *, name):
    n = len(thru)

    def body(*refs):
        copies = copies_fn(refs[:n], refs[n], refs[n + 1])
        for cp in _sends(copies):
            cp.wait_send()
        for cp in _arrivals(copies):
            cp.wait_recv()

    return list(_pcall(
        body, name=name,
        out_shape=tuple(pltpu.HBM(b.shape, b.dtype) for b in thru),
        in_specs=[HBM_ONLY] * n + [SEM_SPEC, SEM_SPEC] + [HBM_SPEC] * len(after),
        out_specs=tuple([HBM_ONLY] * n),
        input_output_aliases={k: k for k in range(n)},
        compiler_params=pltpu.CompilerParams(has_side_effects=EFFECT),
    )(*thru, send_sems, recv_sems, *after))


def _scatter_copies(n):
    def copies(bufs, send_sems, recv_sems):
        x, y, c = _position()
        me_chip = 2 * x + y
        cps = []
        for k in range(n):
            for j, (px, py) in enumerate(_other_chips(x, y)):
                cps.append(pltpu.make_async_remote_copy(
                    src_ref=bufs[k].at[2 * px + py], dst_ref=bufs[n + k].at[me_chip],
                    send_sem=send_sems.at[3 * k + j], recv_sem=recv_sems.at[3 * k + j],
                    device_id=(px, py, c), device_id_type=MESH))
        return cps
    return copies


N_PEERS = N_DEV - 1


def _direct_copies(n):
    def copies(bufs, send_sems, recv_sems):
        x, y, c = _position()
        me_chip = 2 * x + y
        sends, arrivals = [], []
        for k in range(n):
            m = bufs[k].shape[1] // 2
            land = bufs[n + k]

            def rows(slab, half, k=k, m=m):
                start = half * m if isinstance(half, int) else pl.multiple_of(half * m, 16)
                return bufs[k].at[slab, pl.ds(start, m), :]

            def copy(src, slot, send_idx, recv_idx, to, k=k, land=land):
                return pltpu.make_async_remote_copy(
                    src_ref=src, dst_ref=land.at[slot], send_sem=send_sems.at[N_PEERS * k + send_idx],
                    recv_sem=recv_sems.at[N_PEERS * k + recv_idx], device_id=to, device_id_type=MESH)

            sends.append(copy(rows(me_chip, 1 - c), 0, 0, 0, (x, y, 1 - c)))
            arrivals.append(copy(rows(me_chip, c), 0, 0, 0, (x, y, 1 - c)))
            for t, (px, py) in enumerate(_other_chips(x, y)):
                for core in range(2):
                    sends.append(copy(rows(2 * px + py, core), 1 + 2 * t + c, 1 + 2 * t + core, 1 + 2 * t + c,
                                      (px, py, core)))
                    arrivals.append(copy(rows(me_chip, c), 1 + 2 * t + core, 1 + 2 * t + core, 1 + 2 * t + core,
                                         (px, py, core)))
        return sends, arrivals
    return copies


def _sum_direct(gs, lands, *, name, tm=128):
    n = len(gs)
    _, m, cdim = lands[0].shape
    tm = min(tm, m)
    nb = m // tm
    assert m % tm == 0, (m, tm)
    where = jnp.stack([2 * lax.axis_index("x") + lax.axis_index("y"), lax.axis_index("c")]).astype(jnp.int32)

    def body(w_ref, *refs):
        for k in range(n):
            acc = refs[2 * k][...].astype(f32)
            for slot in range(N_PEERS):
                acc = acc + refs[2 * k + 1][slot].astype(f32)
            refs[2 * n + k][...] = acc

    own = pl.BlockSpec((None, tm, cdim), lambda i, w_ref: (w_ref[0], w_ref[1] * nb + i, 0))
    landed = pl.BlockSpec((N_PEERS, tm, cdim), lambda i, w_ref: (0, i, 0))
    operands = []
    for g, land in zip(gs, lands):
        operands += [g, land]
    return _pcall(
        body, name=name,
        grid_spec=pltpu.PrefetchScalarGridSpec(
            num_scalar_prefetch=1, grid=(nb,), in_specs=[own, landed] * n,
            out_specs=[pl.BlockSpec((tm, cdim), lambda i, w_ref: (w_ref[1] * nb + i, 0))] * n),
        out_shape=[jax.ShapeDtypeStruct((2 * m, cdim), f32) for _ in gs],
        compiler_params=_params(1),
    )(where, *operands)


def _block_rows(buf, px, py, pc):
    m = buf.shape[0] // N_DEV
    return buf.at[pl.ds(pl.multiple_of((4 * px + 2 * py + pc) * m, 16), m), :]


def _gather_ici_copies(n):
    def copies(bufs, send_sems, recv_sems):
        x, y, c = _position()
        cps = []
        for k in range(n):
            rows = _block_rows(bufs[k], x, y, c)
            targets = [(x, y, 1 - c)] + [(px, py, c) for px, py in _other_chips(x, y)]
            for j, to in enumerate(targets):
                cps.append(pltpu.make_async_remote_copy(
                    src_ref=rows, dst_ref=rows, send_sem=send_sems.at[4 * k + j], recv_sem=recv_sems.at[4 * k + j],
                    device_id=to, device_id_type=MESH))
        return cps
    return copies


def _gather_d2d_copies(n):
    def copies(bufs, send_sems, recv_sems):
        x, y, c = _position()
        cps = []
        for k in range(n):
            for j, (px, py) in enumerate(_other_chips(x, y)):
                rows = _block_rows(bufs[k], px, py, c)
                cps.append(pltpu.make_async_remote_copy(
                    src_ref=rows, dst_ref=rows, send_sem=send_sems.at[3 * k + j], recv_sem=recv_sems.at[3 * k + j],
                    device_id=(x, y, 1 - c), device_id_type=MESH))
        return cps
    return copies


def _cast_halves(shards, after, *, name):
    n = len(shards)
    where = jnp.stack([2 * lax.axis_index("x") + lax.axis_index("y"), lax.axis_index("c")]).astype(jnp.int32)

    def body(w_ref, *refs):
        for k in range(n):
            refs[n + 1 + k][...] = refs[k][...].astype(refs[n + 1 + k].dtype)

    def half(s):
        return (s.shape[0] // 2, s.shape[1])

    return _pcall(
        body, name=name,
        grid_spec=pltpu.PrefetchScalarGridSpec(
            num_scalar_prefetch=1, grid=(1,),
            in_specs=[pl.BlockSpec(half(s), lambda i, w_ref: (w_ref[1], 0)) for s in shards] + [HBM_SPEC],
            out_specs=[pl.BlockSpec(half(s), lambda i, w_ref: (2 * w_ref[0] + w_ref[1], 0)) for s in shards]),
        out_shape=[jax.ShapeDtypeStruct((N_SHARD * s.shape[0], s.shape[1]), MXU_DTYPE) for s in shards],
        compiler_params=_params(1),
    )(where, *shards, after)


class _SplitGather:
    def __init__(self, shards, after, tag):
        self.tag = tag
        self.n = len(shards)
        halves = _cast_halves(shards, after, name=f"{tag}_cast")
        self.ici = _split_start(halves, _gather_ici_copies(self.n), 4 * self.n, name=f"{tag}_ici_start")
        self.token = self.ici[3]

    def forward(self, after):
        send_sems, recv_sems, thru, _ = self.ici
        landed = _split_wait(thru, send_sems, recv_sems, after, _gather_ici_copies(self.n), name=f"{self.tag}_ici_wait")
        self.d2d = _split_start(landed, _gather_d2d_copies(self.n), 3 * self.n, name=f"{self.tag}_d2d_start")
        return self.d2d[3]

    def finish(self, after):
        send_sems, recv_sems, thru, _ = self.d2d
        return _split_wait(thru, send_sems, recv_sems, after, _gather_d2d_copies(self.n), name=f"{self.tag}_d2d_wait")


class _Overlap(_NoOverlap):
    def __init__(self, late_shards, ffn2_shards, after):
        self.late = _SplitGather(late_shards, after, "ag1")
        self.ffn2 = _SplitGather(ffn2_shards, self.late.token, "ag2")
        self.reduced = None
        self.ffn1_parts = []

    def start_token(self):
        return self.ffn2.token

    def late_weights(self, w, after):
        token = self.late.forward(after)
        f1d, w_in, wo = self.late.finish([token])
        w_in = w_in.reshape(N_SHARD, D_MODEL, IN_SHARD).transpose(1, 0, 2).reshape(D_MODEL, IN_COLS)
        return dict(f1d=f1d.reshape(N_SHARD, D_FF // N_SHARD, D_MODEL), wp=make_wp(w_in), wo=wo)

    def after_attention(self, after):
        return self.ffn2.forward(after)

    def ffn2_weights(self, w, after):
        full = self.ffn2.finish(after)
        fs = D_FF // N_SHARD
        return (full[0].reshape(N_SHARD, D_MODEL, fs), full[1].reshape(N_SHARD, D_MODEL, fs),
                full[2].reshape(N_SHARD, fs, D_MODEL))

    @staticmethod
    def _send_direct(grads, tag):
        lands = [lax.empty((N_PEERS, g.shape[1] // 2, g.shape[2]), g.dtype) for g in grads]
        return _split_start(list(grads) + lands, _direct_copies(len(grads)), N_PEERS * len(grads),
                            name=f"rs_direct_{tag}_start")

    def ffn2_grads(self, grads):
        self.scatter = self._send_direct(grads, "ffn2")
        return self.scatter[3]

    def ffn1_grads(self, grads):
        tag = "ffn1" + "ab"[len(self.ffn1_parts)]
        recvs = _swap_halves(grads, name=f"rs_swap_{tag}")
        ps = list(_add_halves(grads[:2], recvs[:2], name=f"rs_add_{tag}_gu"))
        ps += list(_add_halves(grads[2:], recvs[2:], name=f"rs_add_{tag}_d"))
        lands = [lax.empty(p.shape, p.dtype) for p in ps]
        started = _split_start(ps + lands, _scatter_copies(3), 9, name=f"rs_scatter_{tag}_start")
        self.ffn1_parts.append((tag, started))
        return started[3]

    def ffn1_reduced(self, after):
        sums = []
        for tag, (send_sems, recv_sems, thru, _) in self.ffn1_parts:
            done = _split_wait(thru, send_sems, recv_sems, after, _scatter_copies(3), name=f"rs_scatter_{tag}_wait")
            sums += list(_sum_slabs(done[:2], done[3:5], name=f"rs_sum_{tag}_gu"))
            sums += list(_sum_slabs(done[2:3], done[5:], name=f"rs_sum_{tag}_d"))
        return sums

    def mixer_grads(self, dwp, dwo, small, loss):
        packed = jnp.concatenate([_pack_small(small), jnp.broadcast_to(loss, (8, LANES))], axis=0)
        summed = _all_reduce_small(packed, name="ar_small")
        self.small_sum, self.loss_sum = summed[:-8], summed[-8, 0]
        gwin = dwp[:, :IN_COLS].reshape(D_MODEL, N_SHARD, IN_SHARD).transpose(1, 0, 2).astype(GRAD_DTYPE)
        gwo = dwo.reshape(N_SHARD, D_MODEL // N_SHARD, D_MODEL).astype(GRAD_DTYPE)
        self.scatter_mix = self._send_direct([gwin, gwo], "mix")
        return self.scatter_mix[3]

    def mixer_reduced(self, after):
        send_sems, recv_sems, thru, _ = self.scatter_mix
        done = _split_wait(thru, send_sems, recv_sems, after, _direct_copies(2), name="rs_direct_mix_wait")
        return [_sum_direct([done[k]], [done[2 + k]], name=f"rs_sum_{tag}")[0] for k, tag in enumerate(["w_in", "w_out"])]

    def before_ffn1_bwd(self, after):
        send_sems, recv_sems, thru, _ = self.scatter
        n = len(thru) // 2
        done = _split_wait(thru, send_sems, recv_sems, after, _direct_copies(n), name="rs_direct_ffn2_wait")
        self.reduced = list(_sum_direct(done[:n], done[n:], name="rs_sum_ffn2"))


def _adamw(gs, ws, ms, vs, *, name, tm=256):
    n = len(gs)
    r, cdim = gs[0].shape
    tm = r if tm is None else min(tm, r)
    assert r % tm == 0, (r, tm)
    c1 = 1.0 / (1.0 - ADAM_B1 ** ADAM_STEP)
    c2 = 1.0 / (1.0 - ADAM_B2 ** ADAM_STEP)

    def body(*refs):
        for k in range(n):
            g = refs[k][...]
            w = refs[n + k][...]
            m = ADAM_B1 * refs[2 * n + k][...] + (1.0 - ADAM_B1) * g
            v = ADAM_B2 * refs[3 * n + k][...] + (1.0 - ADAM_B2) * (g * g)
            refs[4 * n + k][...] = g
            refs[5 * n + k][...] = -ADAM_LR * ((m * c1) / (jnp.sqrt(v * c2) + ADAM_EPS) + ADAM_WD * w)
            refs[6 * n + k][...] = m
            refs[7 * n + k][...] = v

    flat = pl.BlockSpec((tm, cdim), lambda i: (i, 0))
    like_w = flat if ws[0].ndim == 2 else pl.BlockSpec((None, tm, cdim), lambda i: (0, i, 0))
    outs = _pcall(
        body, name=name, grid=(r // tm,), in_specs=[flat] * n + [like_w] * (3 * n), out_specs=[like_w] * (4 * n),
        out_shape=[jax.ShapeDtypeStruct(ws[0].shape, f32)] * (4 * n),
        compiler_params=_params(1),
    )(*gs, *ws, *ms, *vs)
    return outs[:n], outs[n:2 * n], outs[2 * n:3 * n], outs[3 * n:]


BIG = ["ffn1_w_gate", "ffn1_w_up", "ffn1_w_down", "ffn2_w_gate", "ffn2_w_up", "ffn2_w_down"]
SMALL = ["ln1_g", "ln1_b", "b_forget", "conv_w", "conv_b", "rg_wa", "rg_ba", "rg_wx", "rg_bx", "lru_lambda",
         "ln2_g", "ln2_b", "ln3_g", "ln3_b"]
WEIGHTS = ["ffn1_w_gate", "ffn1_w_up", "ffn1_w_down", "ln1_g", "ln1_b", "w_in", "b_forget", "conv_w", "conv_b",
           "rg_wa", "rg_ba", "rg_wx", "rg_bx", "lru_lambda", "w_out", "ln2_g", "ln2_b",
           "ffn2_w_gate", "ffn2_w_up", "ffn2_w_down", "ln3_g", "ln3_b"]


def _pack_small(parts):
    rows = []
    for n in SMALL:
        flat = parts[n].reshape(-1)
        pad = (-flat.shape[0]) % LANES
        rows.append(jnp.pad(flat, (0, pad)).reshape(-1, LANES))
    packed = jnp.concatenate(rows, axis=0)
    return jnp.pad(packed, ((0, (-packed.shape[0]) % 8), (0, 0)))


def _unpack_small(packed, shapes):
    out, r0 = {}, 0
    for n in SMALL:
        size = math.prod(shapes[n])
        nr = -(-size // LANES)
        out[n] = packed[r0:r0 + nr].reshape(-1)[:size].reshape(shapes[n])
        r0 += nr
    return out


def kernel(x, ffn1_w_gate, ffn1_w_up, ffn1_w_down, ln1_g, ln1_b, w_in, b_forget, conv_w, conv_b, rg_wa, rg_ba, rg_wx, rg_bx, lru_lambda, w_out, ln2_g, ln2_b, ffn2_w_gate, ffn2_w_up, ffn2_w_down, ln3_g, ln3_b, loss_target, m_ffn1_w_gate, m_ffn1_w_up, m_ffn1_w_down, m_ln1_g, m_ln1_b, m_w_in, m_b_forget, m_conv_w, m_conv_b, m_rg_wa, m_rg_ba, m_rg_wx, m_rg_bx, m_lru_lambda, m_w_out, m_ln2_g, m_ln2_b, m_ffn2_w_gate, m_ffn2_w_up, m_ffn2_w_down, m_ln3_g, m_ln3_b, v_ffn1_w_gate, v_ffn1_w_up, v_ffn1_w_down, v_ln1_g, v_ln1_b, v_w_in, v_b_forget, v_conv_w, v_conv_b, v_rg_wa, v_rg_ba, v_rg_wx, v_rg_bx, v_lru_lambda, v_w_out, v_ln2_g, v_ln2_b, v_ffn2_w_gate, v_ffn2_w_up, v_ffn2_w_down, v_ln3_g, v_ln3_b):
    args = dict(locals())
    w = {n: args[n] for n in WEIGHTS}
    mom = {n: args["m_" + n] for n in WEIGHTS}
    var = {n: args["v_" + n] for n in WEIGHTS}
    chip = 2 * lax.axis_index("x") + lax.axis_index("y")

    g1 = _all_gather_bf16([w[n][0] for n in BIG[:2]], name="ag_ffn1_up")
    fs = D_FF // N_SHARD
    full = dict(
        f1g=g1[0].reshape(N_SHARD, D_MODEL, fs), f1u=g1[1].reshape(N_SHARD, D_MODEL, fs),
        bfp=jnp.pad(b_forget, ((0, 0), (0, LANES - HEADS))),
        ln1_g=ln1_g, ln1_b=ln1_b, ln2_g=ln2_g, ln2_b=ln2_b, ln3_g=ln3_g, ln3_b=ln3_b,
        conv_b=conv_b, rg_wa=rg_wa[0], rg_wx=rg_wx[0], rg_ba=rg_ba[0], rg_bx=rg_bx[0], lam=lru_lambda,
    )
    cw_place = lax.dynamic_update_slice(jnp.zeros((8, LRU_W), f32), conv_w[0] * 0.5, (0, chip * (LRU_W // N_SHARD)))
    cw_full = _all_reduce_small(cw_place.reshape(-1, LANES), g1[0], name="ag_conv_w")
    full["conv_w"] = cw_full.reshape(8, LRU_W)[:CONV_K]

    hooks = _Overlap([w["ffn1_w_down"][0], w["w_in"][0], w["w_out"][0]], [w[n][0] for n in BIG[3:]], cw_full)
    loss_rep, dx, g = _local_step(x[0], loss_target[0], full, hooks)
    loss = hooks.loss_sum

    token1 = hooks.ffn1_grads(g["f1"][1])
    red = _join_halves(hooks.reduced + hooks.mixer_reduced([token1]), name="rs_join_rest")
    grads = dict(zip(BIG[3:] + ["w_in", "w_out"], red))

    small_shapes = {n: w[n].shape for n in SMALL}
    small_shapes["conv_w"] = (1, CONV_K, LRU_W)
    gs_red = _unpack_small(hooks.small_sum, small_shapes)
    gs_red["conv_w"] = lax.dynamic_slice(gs_red["conv_w"], (0, 0, chip * (LRU_W // N_SHARD)),
                                         (1, CONV_K, LRU_W // N_SHARD))
    grads.update(gs_red)

    delta, new_m, new_v = {}, {}, {}

    def adamw(names, name, **kw):
        g3, d, nm, nv = _adamw([grads[n] for n in names], [w[n] for n in names], [mom[n] for n in names],
                               [var[n] for n in names], name=name, **kw)
        for i, n in enumerate(names):
            grads[n], delta[n], new_m[n], new_v[n] = g3[i], d[i], nm[i], nv[i]

    adamw(BIG[3:], "adamw_ffn2", tm=128)
    adamw(["w_in"], "adamw_w_in")
    adamw(["w_out"], "adamw_w_out")
    shard_shapes = {n: w[n].shape for n in SMALL}
    _, d, nm, nv = _adamw([_pack_small({n: grads[n] for n in SMALL})], [_pack_small({n: w[n] for n in SMALL})],
                          [_pack_small({n: mom[n] for n in SMALL})], [_pack_small({n: var[n] for n in SMALL})],
                          name="adamw_small", tm=None)
    for dst, packed in ((delta, d[0]), (new_m, nm[0]), (new_v, nv[0])):
        dst.update(_unpack_small(packed, shard_shapes))

    worked = [new_v["ffn2_w_down"], new_v["w_in"], new_v["w_out"], nv[0]]
    ga, ua, da, gb, ub, db = _join_halves(hooks.ffn1_reduced(worked), name="rs_join_ffn1")
    grads["ffn1_w_gate"] = jnp.concatenate([ga, gb], axis=1)
    grads["ffn1_w_up"] = jnp.concatenate([ua, ub], axis=1)
    grads["ffn1_w_down"] = jnp.concatenate([da, db], axis=0)
    adamw(BIG[:3], "adamw_ffn1", tm=128)

    def shaped(tree, n):
        return tree[n].reshape(w[n].shape)

    return (loss, dx[None], *[shaped(grads, n) for n in WEIGHTS], *[shaped(delta, n) for n in WEIGHTS],
            *[shaped(new_m, n) for n in WEIGHTS], *[shaped(new_v, n) for n in WEIGHTS])
```

```python
import functools
import math

import jax
import jax.numpy as jnp
from jax import lax
from jax.experimental import pallas as pl
from jax.experimental.pallas import tpu as pltpu

f32 = jnp.float32
MXU_DTYPE = jnp.bfloat16
GRAD_DTYPE = jnp.bfloat16

D_MODEL = 1024
D_FF = 4096
N_SHARD = 4
N_DEV = 8
FOX_W = 512
LRU_W = 512
HEADS = 8
HEAD_DIM = 64
CONV_K = 4
IN_COLS = 2568
IN_SHARD = IN_COLS // N_SHARD
QKV_W = 3 * FOX_W
Z_PAD = 2688
LANES = 128
LN_EPS = 1e-5
DN_ALPHA = 2.0 ** 0.25
LRU_C = 8.0
NEG_BIG = -1e30
VMEM_LIMIT = 56 * 1024 * 1024

ADAM_LR = 0.001
ADAM_B1 = 0.9
ADAM_B2 = 0.999
ADAM_EPS = 1e-08
ADAM_WD = 0.01
ADAM_STEP = 10


def _pcall(body, **kw):
    return pl.pallas_call(body, **kw)


def _params(n_grid, vmem=VMEM_LIMIT):
    return pltpu.CompilerParams(dimension_semantics=("arbitrary",) * n_grid, vmem_limit_bytes=vmem)


def _dot(a, b):
    return jnp.dot(a, b, preferred_element_type=f32)


def _dot_nt(a, b):
    return lax.dot_general(a, b, (((1,), (1,)), ((), ())), preferred_element_type=f32)


def _dot_tn(a, b):
    return lax.dot_general(a, b, (((0,), (0,)), ((), ())), preferred_element_type=f32)


def _sigmoid(x):
    return 1.0 / (1.0 + jnp.exp(-x))


def _layer_norm_stats(y):
    mu = jnp.mean(y, axis=-1, keepdims=True)
    yc = y - mu
    var = jnp.mean(yc * yc, axis=-1, keepdims=True)
    rstd = lax.rsqrt(var + LN_EPS)
    return yc * rstd, rstd


def _ln_backward(dy, xhat, rstd, gamma):
    dxhat = dy * gamma
    m1 = jnp.mean(dxhat, axis=-1, keepdims=True)
    m2 = jnp.mean(dxhat * xhat, axis=-1, keepdims=True)
    dyp = rstd * (dxhat - m1 - xhat * m2)
    return dyp, jnp.sum(dy * xhat, axis=0, keepdims=True), jnp.sum(dy, axis=0, keepdims=True)


def _ffn_fwd(x, wg, wu, wd, ln_g, ln_b, *, name, tm=1024, tf=512):
    T = x.shape[0]
    tm = min(tm, T)
    fs = D_FF // N_SHARD
    cpf = fs // tf
    nf = D_FF // tf
    nt = T // tm

    def body(x_ref, wg_ref, wu_ref, wd_ref, g_ref, b_ref,
             xb_ref, gact_ref, uact_ref, xhat_ref, xn_ref, rstd_ref, acc_ref):
        f = pl.program_id(1)

        @pl.when(f == 0)
        def _():
            xb_ref[...] = x_ref[...].astype(MXU_DTYPE)
            acc_ref[...] = jnp.zeros_like(acc_ref)

        xb = xb_ref[...]
        g = _dot(xb, wg_ref[...])
        u = _dot(xb, wu_ref[...])
        h = (g * _sigmoid(g)) * u
        gact_ref[...] = g.astype(gact_ref.dtype)
        uact_ref[...] = u.astype(uact_ref.dtype)
        acc_ref[...] += _dot(h.astype(MXU_DTYPE), wd_ref[...])

        @pl.when(f == nf - 1)
        def _():
            y = DN_ALPHA * x_ref[...] + 0.5 * acc_ref[...]
            xhat, rstd = _layer_norm_stats(y)
            xhat_ref[...] = xhat
            xn_ref[...] = (xhat * g_ref[...] + b_ref[...]).astype(xn_ref.dtype)
            rstd_ref[...] = jnp.broadcast_to(rstd, rstd_ref.shape)

    row = lambda i, f: (i, 0)
    return _pcall(
        body, name=name, grid=(nt, nf),
        in_specs=[
            pl.BlockSpec((tm, D_MODEL), row),
            pl.BlockSpec((None, D_MODEL, tf), lambda i, f: (f // cpf, 0, f % cpf)),
            pl.BlockSpec((None, D_MODEL, tf), lambda i, f: (f // cpf, 0, f % cpf)),
            pl.BlockSpec((None, tf, D_MODEL), lambda i, f: (f // cpf, f % cpf, 0)),
            pl.BlockSpec((1, D_MODEL), lambda i, f: (0, 0)),
            pl.BlockSpec((1, D_MODEL), lambda i, f: (0, 0)),
        ],
        out_specs=[
            pl.BlockSpec((tm, D_MODEL), row),
            pl.BlockSpec((tm, tf), lambda i, f: (i, f)),
            pl.BlockSpec((tm, tf), lambda i, f: (i, f)),
            pl.BlockSpec((tm, D_MODEL), row),
            pl.BlockSpec((tm, D_MODEL), row),
            pl.BlockSpec((tm, LANES), row),
        ],
        out_shape=[
            jax.ShapeDtypeStruct((T, D_MODEL), MXU_DTYPE),
            jax.ShapeDtypeStruct((T, D_FF), MXU_DTYPE),
            jax.ShapeDtypeStruct((T, D_FF), MXU_DTYPE),
            jax.ShapeDtypeStruct((T, D_MODEL), f32),
            jax.ShapeDtypeStruct((T, D_MODEL), MXU_DTYPE),
            jax.ShapeDtypeStruct((T, LANES), f32),
        ],
        scratch_shapes=[pltpu.VMEM((tm, D_MODEL), f32)],
        compiler_params=_params(2),
    )(x, wg, wu, wd, ln_g, ln_b)


def _ffn_up(x, wg, wu, after=None, *, name, tm=1024, tf=512):
    T = x.shape[0]
    tm = min(tm, T)
    cpf = (D_FF // N_SHARD) // tf
    nf = D_FF // tf
    extra = [] if after is None else [after]

    def body(x_ref, wg_ref, wu_ref, *refs):
        xb_ref, gact_ref, uact_ref, hact_ref = refs[len(extra):]

        @pl.when(pl.program_id(1) == 0)
        def _():
            xb_ref[...] = x_ref[...].astype(MXU_DTYPE)

        xb = xb_ref[...]
        g = _dot(xb, wg_ref[...])
        u = _dot(xb, wu_ref[...])
        gact_ref[...] = g.astype(gact_ref.dtype)
        uact_ref[...] = u.astype(uact_ref.dtype)
        hact_ref[...] = ((g * _sigmoid(g)) * u).astype(hact_ref.dtype)

    row = lambda i, f: (i, 0)
    tile = pl.BlockSpec((tm, tf), lambda i, f: (i, f))
    cols = pl.BlockSpec((None, D_MODEL, tf), lambda i, f: (f // cpf, 0, f % cpf))
    return _pcall(
        body, name=name, grid=(T // tm, nf),
        in_specs=[pl.BlockSpec((tm, D_MODEL), row), cols, cols] + [pl.BlockSpec(memory_space=pl.ANY)] * len(extra),
        out_specs=[pl.BlockSpec((tm, D_MODEL), row), tile, tile, tile],
        out_shape=[jax.ShapeDtypeStruct((T, D_MODEL), MXU_DTYPE)] + [jax.ShapeDtypeStruct((T, D_FF), MXU_DTYPE)] * 3,
        compiler_params=_params(2),
    )(x, wg, wu, *extra)


def _ffn_down_ln(x, hact, wd, ln_g, ln_b, *, name, tm=1024):
    T = x.shape[0]
    tm = min(tm, T)
    fs = D_FF // N_SHARD

    def body(x_ref, h_ref, wd_ref, g_ref, b_ref, xhat_ref, xn_ref, rstd_ref, acc_ref):
        k = pl.program_id(1)

        @pl.when(k == 0)
        def _():
            acc_ref[...] = jnp.zeros_like(acc_ref)

        acc_ref[...] += _dot(h_ref[...], wd_ref[...])

        @pl.when(k == N_SHARD - 1)
        def _():
            xhat, rstd = _layer_norm_stats(DN_ALPHA * x_ref[...] + 0.5 * acc_ref[...])
            xhat_ref[...] = xhat
            xn_ref[...] = (xhat * g_ref[...] + b_ref[...]).astype(xn_ref.dtype)
            rstd_ref[...] = jnp.broadcast_to(rstd, rstd_ref.shape)

    row = lambda i, k: (i, 0)
    vec = pl.BlockSpec((1, D_MODEL), lambda i, k: (0, 0))
    return _pcall(
        body, name=name, grid=(T // tm, N_SHARD),
        in_specs=[pl.BlockSpec((tm, D_MODEL), row), pl.BlockSpec((tm, fs), lambda i, k: (i, k)),
                  pl.BlockSpec((None, fs, D_MODEL), lambda i, k: (k, 0, 0)), vec, vec],
        out_specs=[pl.BlockSpec((tm, D_MODEL), row), pl.BlockSpec((tm, D_MODEL), row), pl.BlockSpec((tm, LANES), row)],
        out_shape=[jax.ShapeDtypeStruct((T, D_MODEL), f32), jax.ShapeDtypeStruct((T, D_MODEL), MXU_DTYPE),
                   jax.ShapeDtypeStruct((T, LANES), f32)],
        scratch_shapes=[pltpu.VMEM((tm, D_MODEL), f32)],
        compiler_params=_params(2),
    )(x, hact, wd, ln_g, ln_b)


def _ffn_bwd(dyp, xb, gact, uact, wg, wu, wd, after=None, *, name, tm=512, tf=512, part=None, dx_init=None):
    T = dyp.shape[0]
    tm = min(tm, T)
    fs = D_FF // N_SHARD
    cpf = fs // tf
    nt = T // tm
    nf = D_FF // tf if part is None else N_SHARD
    wf = fs if part is None else tf
    slab = (lambda f: f // cpf) if part is None else (lambda f: f)
    chunk = (lambda f: f % cpf) if part is None else (lambda f: part)
    extra = ([] if dx_init is None else [dx_init]) + ([] if after is None else [after])

    def body(dyp_ref, xb_ref, g_ref, u_ref, wg_ref, wu_ref, wd_ref, *refs):
        dx_hbm, dwg_ref, dwu_ref, dwd_ref, dx_sc, dwg_sc, dwu_sc, dwd_sc, sem = refs[len(extra):]
        f = pl.program_id(0)
        i = pl.program_id(1)
        rows = pl.ds(pl.multiple_of(i * tm, tm), tm)
        dyp_t = dyp_ref[...]
        dy = (0.5 * dyp_t).astype(MXU_DTYPE)

        @pl.when(i == 0)
        def _():
            dwg_sc[...] = jnp.zeros_like(dwg_sc)
            dwu_sc[...] = jnp.zeros_like(dwu_sc)
            dwd_sc[...] = jnp.zeros_like(dwd_sc)

        @pl.when(f == 0)
        def _():
            dx_sc[rows, :] = DN_ALPHA * dyp_t if dx_init is None else refs[0][...]

        g = g_ref[...].astype(f32)
        u = u_ref[...].astype(f32)
        sig = _sigmoid(g)
        silu = g * sig
        dh = _dot_nt(dy, wd_ref[...])
        dg = (dh * u * (sig * (1.0 + g * (1.0 - sig)))).astype(MXU_DTYPE)
        du = (dh * silu).astype(MXU_DTYPE)
        hb = (silu * u).astype(MXU_DTYPE)
        dx_sc[rows, :] += _dot_nt(dg, wg_ref[...]) + _dot_nt(du, wu_ref[...])
        xb_t = xb_ref[...]
        dwg_sc[...] += _dot_tn(xb_t, dg)
        dwu_sc[...] += _dot_tn(xb_t, du)
        dwd_sc[...] += _dot_tn(hb, dy)

        @pl.when(i == nt - 1)
        def _():
            dwg_ref[...] = dwg_sc[...].astype(dwg_ref.dtype)
            dwu_ref[...] = dwu_sc[...].astype(dwu_ref.dtype)
            dwd_ref[...] = dwd_sc[...].astype(dwd_ref.dtype)

        @pl.when(jnp.logical_and(f == nf - 1, i == nt - 1))
        def _():
            cp = pltpu.make_async_copy(dx_sc, dx_hbm, sem)
            cp.start()
            cp.wait()

    row = lambda f, i: (i, 0)
    return _pcall(
        body, name=name, grid=(nf, nt),
        in_specs=[
            pl.BlockSpec((tm, D_MODEL), row),
            pl.BlockSpec((tm, D_MODEL), row),
            pl.BlockSpec((tm, tf), lambda f, i: (i, slab(f) * cpf + chunk(f))),
            pl.BlockSpec((tm, tf), lambda f, i: (i, slab(f) * cpf + chunk(f))),
            pl.BlockSpec((None, D_MODEL, tf), lambda f, i: (slab(f), 0, chunk(f))),
            pl.BlockSpec((None, D_MODEL, tf), lambda f, i: (slab(f), 0, chunk(f))),
            pl.BlockSpec((None, tf, D_MODEL), lambda f, i: (slab(f), chunk(f), 0)),
        ] + ([] if dx_init is None else [pl.BlockSpec((tm, D_MODEL), row)])
        + ([] if after is None else [pl.BlockSpec(memory_space=pl.ANY)]),
        out_specs=[
            pl.BlockSpec(memory_space=pl.ANY),
            pl.BlockSpec((None, D_MODEL, tf), lambda f, i: (slab(f), 0, chunk(f) if part is None else 0)),
            pl.BlockSpec((None, D_MODEL, tf), lambda f, i: (slab(f), 0, chunk(f) if part is None else 0)),
            pl.BlockSpec((None, tf, D_MODEL), lambda f, i: (slab(f), chunk(f) if part is None else 0, 0)),
        ],
        out_shape=[
            jax.ShapeDtypeStruct((T, D_MODEL), f32),
            jax.ShapeDtypeStruct((N_SHARD, D_MODEL, wf), GRAD_DTYPE),
            jax.ShapeDtypeStruct((N_SHARD, D_MODEL, wf), GRAD_DTYPE),
            jax.ShapeDtypeStruct((N_SHARD, wf, D_MODEL), GRAD_DTYPE),
        ],
        scratch_shapes=[pltpu.VMEM((T, D_MODEL), f32), pltpu.VMEM((D_MODEL, tf), f32),
                        pltpu.VMEM((D_MODEL, tf), f32), pltpu.VMEM((tf, D_MODEL), f32),
                        pltpu.SemaphoreType.DMA],
        compiler_params=_params(2),
    )(dyp, xb, gact, uact, wg, wu, wd, *extra)


def _loss_ln_bwd(xhat, rstd, ln_g, ln_b, target, *, name, tm=512):
    T = xhat.shape[0]
    tm = min(tm, T)
    nt = T // tm

    def body(xhat_ref, rstd_ref, g_ref, b_ref, t_ref, dyp_ref, dg_ref, db_ref, loss_ref):
        i = pl.program_id(0)

        @pl.when(i == 0)
        def _():
            dg_ref[...] = jnp.zeros_like(dg_ref)
            db_ref[...] = jnp.zeros_like(db_ref)
            loss_ref[...] = jnp.zeros_like(loss_ref)

        xhat_t = xhat_ref[...]
        gamma = g_ref[...]
        err = xhat_t * gamma + b_ref[...] - t_ref[...]
        sq = jnp.sum(jnp.sum(err * err, axis=0, keepdims=True), axis=1, keepdims=True)
        loss_ref[...] += jnp.broadcast_to(sq * (0.5 / D_MODEL), loss_ref.shape)
        dy = err * (1.0 / D_MODEL)
        dyp, dgam, dbeta = _ln_backward(dy, xhat_t, rstd_ref[:, 0:1], gamma)
        dyp_ref[...] = dyp
        dg_ref[...] += dgam
        db_ref[...] += dbeta

    row = lambda i: (i, 0)
    const = lambda i: (0, 0)
    return _pcall(
        body, name=name, grid=(nt,),
        in_specs=[pl.BlockSpec((tm, D_MODEL), row), pl.BlockSpec((tm, LANES), row),
                  pl.BlockSpec((1, D_MODEL), const), pl.BlockSpec((1, D_MODEL), const),
                  pl.BlockSpec((tm, D_MODEL), row)],
        out_specs=[pl.BlockSpec((tm, D_MODEL), row), pl.BlockSpec((1, D_MODEL), const),
                   pl.BlockSpec((1, D_MODEL), const), pl.BlockSpec((1, LANES), const)],
        out_shape=[jax.ShapeDtypeStruct((T, D_MODEL), f32), jax.ShapeDtypeStruct((1, D_MODEL), f32),
                   jax.ShapeDtypeStruct((1, D_MODEL), f32), jax.ShapeDtypeStruct((1, LANES), f32)],
        compiler_params=_params(1),
    )(xhat, rstd, ln_g, ln_b, target)


def _ln_bwd(dy, xhat, rstd, ln_g, *, name, tm=512):
    T = xhat.shape[0]
    tm = min(tm, T)
    nt = T // tm

    def body(dy_ref, xhat_ref, rstd_ref, g_ref, dyp_ref, dg_ref, db_ref):
        i = pl.program_id(0)

        @pl.when(i == 0)
        def _():
            dg_ref[...] = jnp.zeros_like(dg_ref)
            db_ref[...] = jnp.zeros_like(db_ref)

        dyp, dgam, dbeta = _ln_backward(dy_ref[...], xhat_ref[...], rstd_ref[:, 0:1], g_ref[...])
        dyp_ref[...] = dyp
        dg_ref[...] += dgam
        db_ref[...] += dbeta

    row = lambda i: (i, 0)
    const = lambda i: (0, 0)
    return _pcall(
        body, name=name, grid=(nt,),
        in_specs=[pl.BlockSpec((tm, D_MODEL), row), pl.BlockSpec((tm, D_MODEL), row),
                  pl.BlockSpec((tm, LANES), row), pl.BlockSpec((1, D_MODEL), const)],
        out_specs=[pl.BlockSpec((tm, D_MODEL), row), pl.BlockSpec((1, D_MODEL), const),
                   pl.BlockSpec((1, D_MODEL), const)],
        out_shape=[jax.ShapeDtypeStruct((T, D_MODEL), f32), jax.ShapeDtypeStruct((1, D_MODEL), f32),
                   jax.ShapeDtypeStruct((1, D_MODEL), f32)],
        compiler_params=_params(1),
    )(dy, xhat, rstd, ln_g)


def _proj_in(xn, wp, bfp, *, name, tm=512):
    T = xn.shape[0]
    tm = min(tm, T)
    nt = T // tm

    def body(x_ref, w_ref, b_ref, qkv_ref, lxg_ref, fg_ref):
        z = _dot(x_ref[...], w_ref[...])
        qkv_ref[...] = z[:, :QKV_W].astype(qkv_ref.dtype)
        lxg_ref[...] = z[:, QKV_W:QKV_W + 2 * LRU_W]
        fg_ref[...] = z[:, QKV_W + 2 * LRU_W:] + b_ref[...]

    row = lambda i: (i, 0)
    const = lambda i: (0, 0)
    return _pcall(
        body, name=name, grid=(nt,),
        in_specs=[pl.BlockSpec((tm, D_MODEL), row), pl.BlockSpec((D_MODEL, Z_PAD), const),
                  pl.BlockSpec((1, LANES), const)],
        out_specs=[pl.BlockSpec((tm, QKV_W), row), pl.BlockSpec((tm, 2 * LRU_W), row),
                   pl.BlockSpec((tm, LANES), row)],
        out_shape=[jax.ShapeDtypeStruct((T, QKV_W), MXU_DTYPE), jax.ShapeDtypeStruct((T, 2 * LRU_W), f32),
                   jax.ShapeDtypeStruct((T, LANES), f32)],
        compiler_params=_params(1),
    )(xn, wp, bfp)


def _proj_in_bwd(dqa, dka, dva, dlxg, dfg, xn, dyp, wp, *, name, tm=512):
    T = xn.shape[0]
    tm = min(tm, T)
    nt = T // tm

    def body(dq_ref, dk_ref, dv_ref, dl_ref, dfg_ref, x_ref, dyp_ref, w_ref, dx_ref, dw_hbm, dw_sc, sem):
        i = pl.program_id(0)

        @pl.when(i == 0)
        def _():
            dw_sc[...] = jnp.zeros_like(dw_sc)

        low = _low_lanes((tm, LANES))

        def packed(ref):
            pairs = [jnp.where(low, ref[:, (2 * j) * LANES:(2 * j + 1) * LANES],
                               _swap_lane_halves(ref[:, (2 * j + 1) * LANES:(2 * j + 2) * LANES]))
                     for j in range(HEADS // 2)]
            return jnp.concatenate(pairs, axis=1).astype(MXU_DTYPE)

        dz = jnp.concatenate(
            [packed(dq_ref), packed(dk_ref), packed(dv_ref),
             dl_ref[...].astype(MXU_DTYPE), dfg_ref[...].astype(MXU_DTYPE)], axis=1)
        dx_ref[...] = DN_ALPHA * dyp_ref[...] + _dot_nt(dz, w_ref[...])
        dw_sc[...] += _dot_tn(x_ref[...], dz)

        @pl.when(i == nt - 1)
        def _():
            dw_sc[:, :FOX_W] = dw_sc[:, :FOX_W] * (1.0 / math.sqrt(HEAD_DIM))
            cp = pltpu.make_async_copy(dw_sc, dw_hbm, sem)
            cp.start()
            cp.wait()

    row = lambda i: (i, 0)
    const = lambda i: (0, 0)
    return _pcall(
        body, name=name, grid=(nt,),
        in_specs=[pl.BlockSpec((tm, HEADS * LANES), row), pl.BlockSpec((tm, HEADS * LANES), row),
                  pl.BlockSpec((tm, HEADS * LANES), row),
                  pl.BlockSpec((tm, 2 * LRU_W), row), pl.BlockSpec((tm, LANES), row),
                  pl.BlockSpec((tm, D_MODEL), row), pl.BlockSpec((tm, D_MODEL), row),
                  pl.BlockSpec((D_MODEL, Z_PAD), const)],
        out_specs=[pl.BlockSpec((tm, D_MODEL), row), pl.BlockSpec(memory_space=pl.ANY)],
        out_shape=[jax.ShapeDtypeStruct((T, D_MODEL), f32), jax.ShapeDtypeStruct((D_MODEL, Z_PAD), f32)],
        scratch_shapes=[pltpu.VMEM((D_MODEL, Z_PAD), f32), pltpu.SemaphoreType.DMA],
        compiler_params=_params(1),
    )(dqa, dka, dva, dlxg, dfg, xn, dyp, wp)


def _split3(x):
    hi = x.astype(jnp.bfloat16)
    r1 = x - hi.astype(f32)
    mid = r1.astype(jnp.bfloat16)
    lo = (r1 - mid.astype(f32)).astype(jnp.bfloat16)
    return hi, mid, lo


def _tri_dot(tri, x):
    hi, mid, lo = _split3(x)
    return _dot(tri, hi) + _dot(tri, mid) + _dot(tri, lo)


FOX_PAD = HEADS * LANES
AUX = HEAD_DIM


def _low_lanes(shape):
    return lax.broadcasted_iota(jnp.int32, shape, 1) < HEAD_DIM


def _swap_lane_halves(x):
    return pltpu.roll(x, HEAD_DIM, 1)


def _fox_prep(qkv, fgb, *, name, tm=512):
    T = fgb.shape[0]
    tm = min(tm, T)
    nt = T // tm

    def body(qkv_ref, fg_ref, qa_ref, ka_ref, va_ref, carry):
        i = pl.program_id(0)

        @pl.when(i == 0)
        def _():
            carry[...] = jnp.zeros_like(carry)

        x = fg_ref[...]
        ls = jnp.minimum(x, 0.0) - jnp.log(1.0 + jnp.exp(-jnp.abs(x)))
        r = lax.broadcasted_iota(jnp.int32, (tm, tm), 0)
        c = lax.broadcasted_iota(jnp.int32, (tm, tm), 1)
        tri = jnp.where(r >= c, 1.0, 0.0).astype(jnp.bfloat16)
        cum = _tri_dot(tri, ls) + carry[0:1, :]
        carry[...] = jnp.broadcast_to(cum[tm - 1:tm, :], carry.shape)

        lane = lax.broadcasted_iota(jnp.int32, (tm, LANES), 1)
        low = lane < HEAD_DIM
        ones_q = jnp.where(jnp.logical_and(lane >= AUX + 3, lane < AUX + 6), 1.0, 0.0)
        ones_k = jnp.where(jnp.logical_and(lane >= AUX, lane < AUX + 3), 1.0, 0.0)
        for j in range(HEADS // 2):
            pair = [qkv_ref[:, t * FOX_W + j * LANES:t * FOX_W + (j + 1) * LANES].astype(f32) for t in range(3)]
            for odd in range(2):
                h = 2 * j + odd
                q, k, v = [_swap_lane_halves(a) if odd else a for a in pair]
                hi, mid, lo = [a.astype(f32) for a in _split3(jnp.broadcast_to(cum[:, h:h + 1], (tm, LANES)))]
                aux_q = jnp.where(lane == AUX, hi, jnp.where(lane == AUX + 1, mid, jnp.where(lane == AUX + 2, lo, ones_q)))
                aux_k = jnp.where(lane == AUX + 3, -hi,
                                  jnp.where(lane == AUX + 4, -mid, jnp.where(lane == AUX + 5, -lo, ones_k)))
                blk = slice(h * LANES, (h + 1) * LANES)
                qa_ref[:, blk] = jnp.where(low, q, aux_q).astype(qa_ref.dtype)
                ka_ref[:, blk] = jnp.where(low, k, aux_k).astype(ka_ref.dtype)
                va_ref[:, blk] = jnp.where(low, v, 1.0).astype(va_ref.dtype)

    row = lambda i: (i, 0)
    return _pcall(
        body, name=name, grid=(nt,),
        in_specs=[pl.BlockSpec((tm, QKV_W), row), pl.BlockSpec((tm, LANES), row)],
        out_specs=[pl.BlockSpec((tm, FOX_PAD), row)] * 3,
        out_shape=[jax.ShapeDtypeStruct((T, FOX_PAD), MXU_DTYPE)] * 3,
        scratch_shapes=[pltpu.VMEM((8, LANES), f32)],
        compiler_params=_params(1),
    )(qkv, fgb)


def _future_keys(tq, tk):
    r = lax.broadcasted_iota(jnp.int32, (tq, tk), 0)
    c = lax.broadcasted_iota(jnp.int32, (tq, tk), 1)
    return c > r


def _causal_steps(nq, key_major):
    if key_major:
        pairs = [(qi, ki) for ki in range(nq) for qi in range(ki, nq)]
    else:
        pairs = [(qi, ki) for qi in range(nq) for ki in range(qi + 1)]
    return (jnp.asarray([p[0] for p in pairs], jnp.int32), jnp.asarray([p[1] for p in pairs], jnp.int32))


def _fox_fwd(qa, ka, va, *, name, tq=512, hps=8):
    T = qa.shape[0]
    tq = min(tq, T)
    tk = tq
    nq = T // tq
    rep = tk // LANES
    qi_tab, ki_tab = _causal_steps(nq, key_major=False)

    def body(qi_ref, ki_ref, qa_ref, ka_ref, va_ref, o_ref, lse_ref, m_sc, acc_sc):
        t = pl.program_id(1)
        qi = qi_ref[t]
        ki = ki_ref[t]

        @pl.when(ki == 0)
        def _():
            m_sc[...] = jnp.full_like(m_sc, NEG_BIG)
            acc_sc[...] = jnp.zeros_like(acc_sc)

        def tile(diagonal):
            for h in range(hps):
                blk = slice(h * LANES, (h + 1) * LANES)
                s = _dot_nt(qa_ref[:, blk], ka_ref[:, blk])
                if diagonal:
                    s = jnp.where(_future_keys(tq, tk), NEG_BIG, s)
                m_prev = m_sc[h]
                m_new = jnp.maximum(m_prev, jnp.max(s, axis=1, keepdims=True))
                p = jnp.exp(s - jnp.tile(m_new, (1, rep)))
                acc_sc[h] = jnp.exp(m_prev - m_new) * acc_sc[h] + _dot(p.astype(MXU_DTYPE), va_ref[:, blk])
                m_sc[h] = m_new

        @pl.when(ki < qi)
        def _():
            tile(False)

        @pl.when(ki == qi)
        def _():
            tile(True)
            low = _low_lanes((tq, LANES))
            outs = []
            for h in range(hps):
                acc = acc_sc[h]
                den = _swap_lane_halves(acc)
                outs.append(acc / den)
                lse_ref[h] = m_sc[h] + jnp.log(jnp.where(low, den, acc))
            for p in range(hps // 2):
                o_ref[:, p * LANES:(p + 1) * LANES] = jnp.where(low, outs[2 * p], _swap_lane_halves(outs[2 * p + 1]))

    pair = hps * LANES
    return _pcall(
        body, name=name,
        grid_spec=pltpu.PrefetchScalarGridSpec(
            num_scalar_prefetch=2, grid=(HEADS // hps, qi_tab.shape[0]),
            in_specs=[
                pl.BlockSpec((tq, pair), lambda j, t, qi_ref, ki_ref: (qi_ref[t], j)),
                pl.BlockSpec((tk, pair), lambda j, t, qi_ref, ki_ref: (ki_ref[t], j)),
                pl.BlockSpec((tk, pair), lambda j, t, qi_ref, ki_ref: (ki_ref[t], j)),
            ],
            out_specs=[pl.BlockSpec((tq, pair // 2), lambda j, t, qi_ref, ki_ref: (qi_ref[t], j)),
                       pl.BlockSpec((hps, tq, LANES), lambda j, t, qi_ref, ki_ref: (j, qi_ref[t], 0))],
            scratch_shapes=[pltpu.VMEM((hps, tq, LANES), f32)] * 2),
        out_shape=[jax.ShapeDtypeStruct((T, FOX_W), f32), jax.ShapeDtypeStruct((HEADS, T, LANES), f32)],
        compiler_params=_params(2),
    )(qi_tab, ki_tab, qa, ka, va)


def _fox_bwd_prep(do, o, *, name, tm=512):
    T = o.shape[0]
    tm = min(tm, T)
    nt = T // tm

    def body(do_ref, o_ref, d_ref, doa_ref):
        low = _low_lanes((tm, LANES))
        for j in range(HEADS // 2):
            do2 = do_ref[:, j * LANES:(j + 1) * LANES].astype(f32)
            prod = do2 * o_ref[:, j * LANES:(j + 1) * LANES]
            for odd in range(2):
                h = 2 * j + odd
                mine = jnp.where(low, _swap_lane_halves(prod) if odd else prod, 0.0)
                d_ref[h] = jnp.broadcast_to(jnp.sum(mine, axis=1, keepdims=True), (tm, LANES))
                doh = jnp.where(low, _swap_lane_halves(do2) if odd else do2, 0.0)
                doa_ref[:, h * LANES:(h + 1) * LANES] = doh.astype(doa_ref.dtype)

    return _pcall(
        body, name=name, grid=(nt,),
        in_specs=[pl.BlockSpec((tm, FOX_W), lambda i: (i, 0)), pl.BlockSpec((tm, FOX_W), lambda i: (i, 0))],
        out_specs=[pl.BlockSpec((HEADS, tm, LANES), lambda i: (0, i, 0)), pl.BlockSpec((tm, FOX_PAD), lambda i: (i, 0))],
        out_shape=[jax.ShapeDtypeStruct((HEADS, T, LANES), f32), jax.ShapeDtypeStruct((T, FOX_PAD), MXU_DTYPE)],
        compiler_params=_params(1),
    )(do, o)


def _fox_bwd(qa, ka, va, doa, lse, drep, *, name, tq=512, hps=4):
    T = qa.shape[0]
    tq = min(tq, T)
    tk = tq
    nq = T // tq
    rep = tk // LANES
    qi_tab, ki_tab = _causal_steps(nq, key_major=True)

    def body(qi_ref, ki_ref, qa_ref, ka_ref, va_ref, doa_ref, lse_ref, d_ref, dqa_ref, dka_ref, dva_ref, dk_sc, dv_sc):
        t = pl.program_id(1)
        qi = qi_ref[t]
        ki = ki_ref[t]
        rows = pl.ds(pl.multiple_of(qi * tq, tq), tq)

        @pl.when(t == 0)
        def _():
            dqa_ref[...] = jnp.zeros_like(dqa_ref)

        @pl.when(qi == ki)
        def _():
            dk_sc[...] = jnp.zeros_like(dk_sc)
            dv_sc[...] = jnp.zeros_like(dv_sc)

        def tile(diagonal):
            for h in range(hps):
                blk = slice(h * LANES, (h + 1) * LANES)
                qh, kh, doh = qa_ref[:, blk], ka_ref[:, blk], doa_ref[:, blk]
                p = jnp.exp(_dot_nt(qh, kh) - jnp.tile(lse_ref[h], (1, rep)))
                if diagonal:
                    p = jnp.where(_future_keys(tq, tk), 0.0, p)
                dp = _dot_nt(doh, va_ref[:, blk])
                ds = (p * (dp - jnp.tile(d_ref[h], (1, rep)))).astype(MXU_DTYPE)
                dv_sc[h] += _dot_tn(p.astype(MXU_DTYPE), doh)
                dk_sc[h] += _dot_tn(ds, qh)
                dqa_ref[rows, blk] += _dot(ds, kh)

        @pl.when(qi > ki)
        def _():
            tile(False)

        @pl.when(qi == ki)
        def _():
            tile(True)

        @pl.when(qi == nq - 1)
        def _():
            for h in range(hps):
                blk = slice(h * LANES, (h + 1) * LANES)
                dka_ref[:, blk] = dk_sc[h]
                dva_ref[:, blk] = dv_sc[h]

    pair = hps * LANES
    q_blk = lambda j, t, qi_ref, ki_ref: (qi_ref[t], j)
    k_blk = lambda j, t, qi_ref, ki_ref: (ki_ref[t], j)
    stat = pl.BlockSpec((hps, tq, LANES), lambda j, t, qi_ref, ki_ref: (j, qi_ref[t], 0))
    return _pcall(
        body, name=name,
        grid_spec=pltpu.PrefetchScalarGridSpec(
            num_scalar_prefetch=2, grid=(HEADS // hps, qi_tab.shape[0]),
            in_specs=[pl.BlockSpec((tq, pair), q_blk), pl.BlockSpec((tk, pair), k_blk), pl.BlockSpec((tk, pair), k_blk),
                      pl.BlockSpec((tq, pair), q_blk), stat, stat],
            out_specs=[pl.BlockSpec((T, pair), lambda j, t, qi_ref, ki_ref: (0, j)),
                       pl.BlockSpec((tk, pair), k_blk), pl.BlockSpec((tk, pair), k_blk)],
            scratch_shapes=[pltpu.VMEM((hps, tk, LANES), f32)] * 2),
        out_shape=[jax.ShapeDtypeStruct((T, FOX_PAD), f32)] * 3,
        compiler_params=_params(2),
    )(qi_tab, ki_tab, qa, ka, va, doa, lse, drep)


def _fox_bwd_post(dqa, dka, fgb, *, name, tm=512):
    T = fgb.shape[0]
    tm = min(tm, T)
    nt = T // tm

    def body(dqa_ref, dka_ref, fg_ref, dfg_ref, dbf_ref, carry):
        i = pl.program_id(0)

        @pl.when(i == 0)
        def _():
            carry[...] = jnp.zeros_like(carry)
            dbf_ref[...] = jnp.zeros_like(dbf_ref)

        lane = lax.broadcasted_iota(jnp.int32, (tm, LANES), 1)
        dc = jnp.zeros((tm, LANES), f32)
        for h in range(HEADS):
            row_sum = dqa_ref[:, h * LANES + AUX:h * LANES + AUX + 1]
            col_sum = dka_ref[:, h * LANES + AUX + 3:h * LANES + AUX + 4]
            dc = jnp.where(lane == h, jnp.broadcast_to(row_sum - col_sum, (tm, LANES)), dc)
        r = lax.broadcasted_iota(jnp.int32, (tm, tm), 0)
        c = lax.broadcasted_iota(jnp.int32, (tm, tm), 1)
        tri = jnp.where(c >= r, 1.0, 0.0).astype(jnp.bfloat16)
        dls = _tri_dot(tri, dc) + carry[0:1, :]
        carry[...] = jnp.broadcast_to(dls[0:1, :], carry.shape)
        dfg = dls * _sigmoid(-fg_ref[...])
        dfg_ref[...] = dfg
        dbf_ref[...] += jnp.sum(dfg, axis=0, keepdims=True)

    rev = lambda i: (nt - 1 - i, 0)
    return _pcall(
        body, name=name, grid=(nt,),
        in_specs=[pl.BlockSpec((tm, FOX_PAD), rev), pl.BlockSpec((tm, FOX_PAD), rev), pl.BlockSpec((tm, LANES), rev)],
        out_specs=[pl.BlockSpec((tm, LANES), rev), pl.BlockSpec((1, LANES), lambda i: (0, 0))],
        out_shape=[jax.ShapeDtypeStruct((T, LANES), f32), jax.ShapeDtypeStruct((1, LANES), f32)],
        scratch_shapes=[pltpu.VMEM((8, LANES), f32)],
        compiler_params=_params(1),
    )(dqa, dka, fgb)


GELU_C = math.sqrt(2.0 / math.pi)
GELU_A = 0.044715


def _gelu(x):
    t = jnp.tanh(GELU_C * (x + GELU_A * x * x * x))
    return 0.5 * x * (1.0 + t), t


def _gelu_grad(x, t):
    return 0.5 * (1.0 + t) + 0.5 * x * (1.0 - t * t) * GELU_C * (1.0 + 3.0 * GELU_A * x * x)


EXPM1_SERIES_BELOW = 0.25


def _expm1(x, e):
    series = x * (1.0 + x * (1 / 2 + x * (1 / 6 + x * (1 / 24 + x * (1 / 120 + x * (1 / 720))))))
    return jnp.where(x > -EXPM1_SERIES_BELOW, series, e - 1.0)


def _lru_gates(u, wab_ref, bab_ref, lam_ref):
    pre = _dot(u.astype(MXU_DTYPE), wab_ref[...]) + bab_ref[...]
    r = _sigmoid(pre[:, :LRU_W])
    gi = _sigmoid(pre[:, LRU_W:])
    lam = lam_ref[...]
    sp = jnp.maximum(-lam, 0.0) + jnp.log(1.0 + jnp.exp(-jnp.abs(lam)))
    log_a = -LRU_C * r * sp
    a = jnp.exp(log_a)
    s = jnp.sqrt(-_expm1(2.0 * log_a, a * a))
    return r, gi, sp, a, s


def _lru_fwd(lxg, conv_w, conv_b, wab, bab, lam, *, name, tc=512):
    T = lxg.shape[0]
    tc = min(tc, T)
    nc = T // tc

    def body(lx_ref, lg_ref, cw_ref, cb_ref, wab_ref, bab_ref, lam_ref,
             out_ref, u_ref, hs_ref, ext, a_sc, b_sc, h_sc):
        i = pl.program_id(0)

        @pl.when(i == 0)
        def _():
            ext[0:8, :] = jnp.zeros((8, LRU_W), f32)
            h_sc[...] = jnp.zeros_like(h_sc)

        ext[8:, :] = lx_ref[...]
        u = cb_ref[...] + cw_ref[0:1, :] * ext[pl.ds(5, tc), :]
        for k in range(1, CONV_K):
            u = u + cw_ref[k:k + 1, :] * ext[pl.ds(5 + k, tc), :]
        ext[0:8, :] = ext[tc:tc + 8, :]
        u_ref[...] = u
        r, gi, sp, a, s = _lru_gates(u, wab_ref, bab_ref, lam_ref)
        a_sc[...] = a
        b_sc[...] = s * (gi * u)

        def step(t, h):
            h = a_sc[pl.ds(t, 1), :] * h + b_sc[pl.ds(t, 1), :]
            hs_ref[pl.ds(t, 1), :] = h
            return h

        h = lax.fori_loop(0, tc, step, h_sc[0:1, :], unroll=8)
        h_sc[...] = jnp.broadcast_to(h, h_sc.shape)
        gel, _ = _gelu(lg_ref[...])
        out_ref[...] = gel * hs_ref[...]

    row = lambda i: (i, 0)
    const = lambda i: (0, 0)
    return _pcall(
        body, name=name, grid=(nc,),
        in_specs=[pl.BlockSpec((tc, LRU_W), row), pl.BlockSpec((tc, LRU_W), lambda i: (i, 1)),
                  pl.BlockSpec((CONV_K, LRU_W), const), pl.BlockSpec((1, LRU_W), const),
                  pl.BlockSpec((LRU_W, 2 * LRU_W), const), pl.BlockSpec((1, 2 * LRU_W), const),
                  pl.BlockSpec((1, LRU_W), const)],
        out_specs=[pl.BlockSpec((tc, LRU_W), row)] * 3,
        out_shape=[jax.ShapeDtypeStruct((T, LRU_W), f32)] * 3,
        scratch_shapes=[pltpu.VMEM((tc + 8, LRU_W), f32), pltpu.VMEM((tc, LRU_W), f32),
                        pltpu.VMEM((tc, LRU_W), f32), pltpu.VMEM((8, LRU_W), f32)],
        compiler_params=_params(1),
    )(lxg, lxg, conv_w, conv_b, wab, bab, lam)


def _lru_bwd(dlru, lxg, u, hs, conv_w, wab, bab, lam, *, name, tc=512):
    T = lxg.shape[0]
    tc = min(tc, T)
    nc = T // tc
    bp = tc // 8

    def body(dl_ref, lx_ref, lxp_ref, lg_ref, u_ref, hs_ref, hsp_ref, cw_ref, wab_ref, bab_ref, lam_ref,
             dlxg_ref, dwab_ref, dbab_ref, dcw_ref, dcb_ref, dlam_ref,
             dh_sc, a_sc, ext, du_ext, carry):
        i = pl.program_id(0)
        first_chunk = i == nc - 1

        @pl.when(i == 0)
        def _():
            dwab_ref[...] = jnp.zeros_like(dwab_ref)
            dbab_ref[...] = jnp.zeros_like(dbab_ref)
            dcw_ref[...] = jnp.zeros_like(dcw_ref)
            dcb_ref[...] = jnp.zeros_like(dcb_ref)
            dlam_ref[...] = jnp.zeros_like(dlam_ref)
            carry[...] = jnp.zeros_like(carry)
            du_ext[tc:tc + 8, :] = jnp.zeros((8, LRU_W), f32)

        lg = lg_ref[...]
        gel, th = _gelu(lg)
        dl = dl_ref[...]
        hs = hs_ref[...]
        dlg = dl * hs * _gelu_grad(lg, th)
        u = u_ref[...]
        r, gi, sp, a, s = _lru_gates(u, wab_ref, bab_ref, lam_ref)
        a_sc[...] = a
        dh_sc[...] = dl * gel

        def step(k, c):
            t = tc - 1 - k
            dh = dh_sc[pl.ds(t, 1), :] + c
            dh_sc[pl.ds(t, 1), :] = dh
            return a_sc[pl.ds(t, 1), :] * dh

        c = lax.fori_loop(0, tc, step, carry[0:1, :], unroll=8)
        carry[...] = jnp.broadcast_to(c, carry.shape)

        ext[0:8, :] = jnp.where(first_chunk, 0.0, hsp_ref[...])
        ext[8:, :] = hs
        hprev = ext[pl.ds(7, tc), :]
        dh = dh_sc[...]
        da = dh * hprev
        giu = gi * u
        dla = da * a - (dh * giu) * (a * a / s)
        dgi = dh * s * u
        du = dh * s * gi
        dr = dla * (-LRU_C * sp)
        dlam_ref[...] += jnp.sum(dla * (-LRU_C * r), axis=0, keepdims=True) * (-_sigmoid(-lam_ref[...]))
        dpre = jnp.concatenate([dr * r * (1.0 - r), dgi * gi * (1.0 - gi)], axis=1)
        dpre_b = dpre.astype(MXU_DTYPE)
        du = du + _dot_nt(dpre_b, wab_ref[...])
        dwab_ref[...] += _dot_tn(u.astype(MXU_DTYPE), dpre_b)
        dbab_ref[...] += jnp.sum(dpre, axis=0, keepdims=True)
        dcb_ref[...] += jnp.sum(du, axis=0, keepdims=True)

        du_ext[0:tc, :] = du
        dlx = cw_ref[0:1, :] * du_ext[pl.ds(3, tc), :]
        for k in range(1, CONV_K):
            dlx = dlx + cw_ref[k:k + 1, :] * du_ext[pl.ds(3 - k, tc), :]
        du_ext[tc:tc + 8, :] = du_ext[0:8, :]
        ext[0:8, :] = jnp.where(first_chunk, 0.0, lxp_ref[...])
        ext[8:, :] = lx_ref[...]
        for k in range(CONV_K):
            dcw_ref[k:k + 1, :] += jnp.sum(du * ext[pl.ds(5 + k, tc), :], axis=0, keepdims=True)
        dlxg_ref[:, :LRU_W] = dlx.astype(dlxg_ref.dtype)
        dlxg_ref[:, LRU_W:] = dlg.astype(dlxg_ref.dtype)

    rev = lambda i: (nc - 1 - i, 0)
    prev8 = lambda i: (jnp.maximum((nc - 1 - i) * bp - 1, 0), 0)
    const = lambda i: (0, 0)
    return _pcall(
        body, name=name, grid=(nc,),
        in_specs=[
            pl.BlockSpec((tc, LRU_W), rev),
            pl.BlockSpec((tc, LRU_W), rev),
            pl.BlockSpec((8, LRU_W), prev8),
            pl.BlockSpec((tc, LRU_W), lambda i: (nc - 1 - i, 1)),
            pl.BlockSpec((tc, LRU_W), rev),
            pl.BlockSpec((tc, LRU_W), rev),
            pl.BlockSpec((8, LRU_W), prev8),
            pl.BlockSpec((CONV_K, LRU_W), const),
            pl.BlockSpec((LRU_W, 2 * LRU_W), const),
            pl.BlockSpec((1, 2 * LRU_W), const),
            pl.BlockSpec((1, LRU_W), const),
        ],
        out_specs=[
            pl.BlockSpec((tc, 2 * LRU_W), rev),
            pl.BlockSpec((LRU_W, 2 * LRU_W), const),
            pl.BlockSpec((1, 2 * LRU_W), const),
            pl.BlockSpec((8, LRU_W), const),
            pl.BlockSpec((1, LRU_W), const),
            pl.BlockSpec((1, LRU_W), const),
        ],
        out_shape=[
            jax.ShapeDtypeStruct((T, 2 * LRU_W), MXU_DTYPE),
            jax.ShapeDtypeStruct((LRU_W, 2 * LRU_W), f32),
            jax.ShapeDtypeStruct((1, 2 * LRU_W), f32),
            jax.ShapeDtypeStruct((8, LRU_W), f32),
            jax.ShapeDtypeStruct((1, LRU_W), f32),
            jax.ShapeDtypeStruct((1, LRU_W), f32),
        ],
        scratch_shapes=[pltpu.VMEM((tc, LRU_W), f32), pltpu.VMEM((tc, LRU_W), f32),
                        pltpu.VMEM((tc + 8, LRU_W), f32), pltpu.VMEM((tc + 8, LRU_W), f32),
                        pltpu.VMEM((8, LRU_W), f32)],
        compiler_params=_params(1),
    )(dlru, lxg, lxg, lxg, u, hs, hs, conv_w, wab, bab, lam)


def _mix_out(fox, lru, wo, xhat1, g1, b1, g2, b2, *, name, tm=512):
    T = fox.shape[0]
    tm = min(tm, T)
    nt = T // tm

    def body(fox_ref, lru_ref, wo_ref, xh_ref, g1_ref, b1_ref, g2_ref, b2_ref, xhat_ref, xn_ref, rstd_ref):
        mix = _dot(fox_ref[...].astype(MXU_DTYPE), wo_ref[:FOX_W, :])
        mix = mix + _dot(lru_ref[...].astype(MXU_DTYPE), wo_ref[FOX_W:, :])
        x1 = xh_ref[...] * g1_ref[...] + b1_ref[...]
        xhat, rstd = _layer_norm_stats(DN_ALPHA * x1 + mix)
        xhat_ref[...] = xhat
        xn_ref[...] = xhat * g2_ref[...] + b2_ref[...]
        rstd_ref[...] = jnp.broadcast_to(rstd, rstd_ref.shape)

    row = lambda i: (i, 0)
    const = lambda i: (0, 0)
    vec = pl.BlockSpec((1, D_MODEL), const)
    return _pcall(
        body, name=name, grid=(nt,),
        in_specs=[pl.BlockSpec((tm, FOX_W), row), pl.BlockSpec((tm, LRU_W), row),
                  pl.BlockSpec((D_MODEL, D_MODEL), const), pl.BlockSpec((tm, D_MODEL), row), vec, vec, vec, vec],
        out_specs=[pl.BlockSpec((tm, D_MODEL), row), pl.BlockSpec((tm, D_MODEL), row),
                   pl.BlockSpec((tm, LANES), row)],
        out_shape=[jax.ShapeDtypeStruct((T, D_MODEL), f32), jax.ShapeDtypeStruct((T, D_MODEL), f32),
                   jax.ShapeDtypeStruct((T, LANES), f32)],
        compiler_params=_params(1),
    )(fox, lru, wo, xhat1, g1, b1, g2, b2)


def _mix_out_bwd(dyp, fox, lru, wo, *, name, tm=512):
    T = fox.shape[0]
    tm = min(tm, T)
    nt = T // tm

    def body(dyp_ref, fox_ref, lru_ref, wo_ref, dfox_ref, dlru_ref, dwo_ref):
        i = pl.program_id(0)

        @pl.when(i == 0)
        def _():
            dwo_ref[...] = jnp.zeros_like(dwo_ref)

        dmix = dyp_ref[...].astype(MXU_DTYPE)
        dcat = _dot_nt(dmix, wo_ref[...])
        dfox_ref[...] = dcat[:, :FOX_W].astype(dfox_ref.dtype)
        dlru_ref[...] = dcat[:, FOX_W:]
        dwo_ref[:FOX_W, :] += _dot_tn(fox_ref[...].astype(MXU_DTYPE), dmix)
        dwo_ref[FOX_W:, :] += _dot_tn(lru_ref[...].astype(MXU_DTYPE), dmix)

    row = lambda i: (i, 0)
    const = lambda i: (0, 0)
    return _pcall(
        body, name=name, grid=(nt,),
        in_specs=[pl.BlockSpec((tm, D_MODEL), row), pl.BlockSpec((tm, FOX_W), row), pl.BlockSpec((tm, LRU_W), row),
                  pl.BlockSpec((D_MODEL, D_MODEL), const)],
        out_specs=[pl.BlockSpec((tm, FOX_W), row), pl.BlockSpec((tm, LRU_W), row),
                   pl.BlockSpec((D_MODEL, D_MODEL), const)],
        out_shape=[jax.ShapeDtypeStruct((T, FOX_W), MXU_DTYPE), jax.ShapeDtypeStruct((T, LRU_W), f32),
                   jax.ShapeDtypeStruct((D_MODEL, D_MODEL), f32)],
        compiler_params=_params(1),
    )(dyp, fox, lru, wo)


def make_wp(w_in):
    scale = jnp.concatenate([jnp.full((FOX_W,), 1.0 / math.sqrt(HEAD_DIM), w_in.dtype),
                             jnp.ones((IN_COLS - FOX_W,), w_in.dtype)])
    return jnp.pad(w_in * scale[None, :], ((0, 0), (0, Z_PAD - IN_COLS)))


def _block_diag(w):
    eye = jnp.eye(HEADS, dtype=w.dtype)
    return jnp.einsum("hij,hg->higj", w, eye).reshape(LRU_W, LRU_W)


def _block_diag_extract(m):
    m4 = m.reshape(HEADS, HEAD_DIM, HEADS, HEAD_DIM)
    return jnp.stack([m4[h, :, h, :] for h in range(HEADS)])


class _NoOverlap:
    def start_token(self):
        return None

    def late_weights(self, w, after):
        return dict(f1d=w["f1d"], wp=w["wp"], wo=w["wo"])

    def after_attention(self, after):
        return None

    def ffn2_weights(self, w, after):
        return w["f2g"], w["f2u"], w["f2d"]

    def ffn2_grads(self, grads):
        return None

    def ffn1_grads(self, grads):
        return None

    def mixer_grads(self, dwp, dwo, small, loss):
        return None

    def before_ffn1_bwd(self, after):
        return None


def _tied(a, token):
    return a if token is None else a + token[0, 0]


def _local_step(x, target, w, hooks=None):
    hooks = hooks or _NoOverlap()
    bfp = w["bfp"]
    wab = jnp.concatenate([_block_diag(w["rg_wa"]), _block_diag(w["rg_wx"])], axis=1).astype(MXU_DTYPE)
    bab = jnp.concatenate([w["rg_ba"].reshape(1, LRU_W), w["rg_bx"].reshape(1, LRU_W)], axis=1)

    xb0, g1a, u1a, h1a = _ffn_up(x, w["f1g"], w["f1u"], hooks.start_token(), name="ffn1_up")
    late = hooks.late_weights(w, [h1a])
    f1d, wp, wo = late["f1d"], late["wp"], late["wo"]
    xhat1, xn1, rstd1 = _ffn_down_ln(x, h1a, f1d, w["ln1_g"], w["ln1_b"], name="ffn1_down")
    qkv, lxg, fgb = _proj_in(xn1, wp, bfp, name="proj_in")
    qa, ka, va = _fox_prep(qkv, fgb, name="fox_prep")
    fox, lse = _fox_fwd(qa, ka, va, name="fox_fwd")
    token = hooks.after_attention([lse])
    lru, uconv, hs = _lru_fwd(lxg, w["conv_w"], _tied(w["conv_b"], token), wab, bab, w["lam"], name="lru_fwd")
    xhat2, x2, rstd2 = _mix_out(fox, lru, wo, xhat1, w["ln1_g"], w["ln1_b"], w["ln2_g"], w["ln2_b"], name="mix_out")
    f2g, f2u, f2d = hooks.ffn2_weights(w, [rstd2])
    xb2, g2a, u2a, xhat3, _, rstd3 = _ffn_fwd(x2, f2g, f2u, f2d, w["ln3_g"], w["ln3_b"], name="ffn2_fwd")

    dy3p, dln3g, dln3b, loss = _loss_ln_bwd(xhat3, rstd3, w["ln3_g"], w["ln3_b"], target, name="loss_ln3_bwd")
    dx2, df2g, df2u, df2d = _ffn_bwd(dy3p, xb2, g2a, u2a, f2g, f2u, f2d, name="ffn2_bwd")
    token = hooks.ffn2_grads([df2g, df2u, df2d])
    dy2p, dln2g, dln2b = _ln_bwd(dx2, xhat2, rstd2, _tied(w["ln2_g"], token), name="ln2_bwd")
    dfox, dlru, dwo = _mix_out_bwd(dy2p, fox, lru, wo, name="mix_out_bwd")
    dlxg, dwab, dbab, dcw, dcb, dlam = _lru_bwd(dlru, lxg, uconv, hs, w["conv_w"], wab, bab, w["lam"], name="lru_bwd")
    drep, doa = _fox_bwd_prep(dfox, fox, name="fox_bwd_prep")
    dqa, dka, dva = _fox_bwd(qa, ka, va, doa, lse, drep, name="fox_bwd")
    dfg, dbf = _fox_bwd_post(dqa, dka, fgb, name="fox_bwd_post")
    dx1, dwp = _proj_in_bwd(dqa, dka, dva, dlxg, dfg, xn1, dy2p, wp, name="proj_in_bwd")
    dy1p, dln1g, dln1b = _ln_bwd(dx1, xhat1, rstd1, w["ln1_g"], name="ln1_bwd")
    small = dict(
        ln1_g=dln1g, ln1_b=dln1b, ln2_g=dln2g, ln2_b=dln2b, ln3_g=dln3g, ln3_b=dln3b,
        b_forget=dbf[:, :HEADS], conv_w=dcw[:CONV_K], conv_b=dcb,
        rg_wa=_block_diag_extract(dwab[:, :LRU_W]), rg_wx=_block_diag_extract(dwab[:, LRU_W:]),
        rg_ba=dbab[:, :LRU_W].reshape(HEADS, HEAD_DIM), rg_bx=dbab[:, LRU_W:].reshape(HEADS, HEAD_DIM),
        lru_lambda=dlam,
    )
    hooks.before_ffn1_bwd([dln1b])
    token = hooks.mixer_grads(dwp, dwo, small, loss)
    dx_a, *grads_a = _ffn_bwd(dy1p, xb0, g1a, u1a, w["f1g"], w["f1u"], f1d, token, name="ffn1_bwd_a", part=0)
    token = hooks.ffn1_grads(grads_a)
    dx, *grads_b = _ffn_bwd(dy1p, xb0, g1a, u1a, w["f1g"], w["f1u"], f1d, token, name="ffn1_bwd_b", part=1,
                            dx_init=dx_a)

    grads = dict(f1=(grads_a, grads_b), f2g=df2g, f2u=df2u, f2d=df2d, wp=dwp, wo=dwo, **small)
    return loss, dx, grads


MESH = pl.DeviceIdType.MESH
HBM_SPEC = pl.BlockSpec(memory_space=pl.ANY)
VMEM_SPEC = pl.BlockSpec(memory_space=pltpu.VMEM)


def _position():
    return lax.axis_index("x"), lax.axis_index("y"), lax.axis_index("c")


def _other_chips(x, y):
    return [(1 - x, y), (x, 1 - y), (1 - x, 1 - y)]


def _all_gather_bf16(shards, *, name):
    n = len(shards)

    def body(*refs):
        ins, outs, stages = refs[:n], refs[n:2 * n], refs[2 * n:3 * n]
        send_sems, recv_sems, local_sems = refs[3 * n:]
        x, y, c = _position()
        me, sibling = (x, y, c), (x, y, 1 - c)
        chips = _other_chips(x, y)

        def rows(k, px, py, pc):
            r = shards[k].shape[0]
            m = r // 2
            return outs[k].at[pl.ds(pl.multiple_of((2 * px + py) * r + pc * m, 16), m), :]

        def copy(k, idx, block, to, src=None):
            return pltpu.make_async_remote_copy(
                src_ref=rows(k, *block) if src is None else src, dst_ref=rows(k, *block),
                send_sem=send_sems.at[7 * k + idx], recv_sem=recv_sems.at[7 * k + idx],
                device_id=to, device_id_type=MESH)

        started = []
        mine = []
        for k in range(n):
            m = shards[k].shape[0] // 2
            stages[k][...] = ins[k][pl.ds(pl.multiple_of(c * m, 16), m), :].astype(stages[k].dtype)
            cp = pltpu.make_async_copy(stages[k], rows(k, *me), local_sems.at[k])
            cp.start()
            mine.append(cp)
            first = [copy(k, 0, me, sibling, src=stages[k])]
            first += [copy(k, 1 + j, me, (*chip, c), src=stages[k]) for j, chip in enumerate(chips)]
            for cp in first:
                cp.start()
            started += first
        for k in range(n):
            for j, chip in enumerate(chips):
                copy(k, 1 + j, (*chip, c), me).wait_recv()
                fwd = copy(k, 4 + j, (*chip, c), sibling)
                fwd.start()
                started.append(fwd)
        for k in range(n):
            copy(k, 0, sibling, me).wait_recv()
            for j, chip in enumerate(chips):
                copy(k, 4 + j, (*chip, 1 - c), me).wait_recv()
        for cp in started:
            cp.wait_send()
        for cp in mine:
            cp.wait()

    return _pcall(
        body, name=name,
        in_specs=[VMEM_SPEC] * n, out_specs=[HBM_SPEC] * n,
        out_shape=[jax.ShapeDtypeStruct((N_SHARD * s.shape[0], s.shape[1]), MXU_DTYPE) for s in shards],
        scratch_shapes=[pltpu.VMEM((s.shape[0] // 2, s.shape[1]), MXU_DTYPE) for s in shards]
        + [pltpu.SemaphoreType.DMA((7 * n,)), pltpu.SemaphoreType.DMA((7 * n,)), pltpu.SemaphoreType.DMA((n,))],
        compiler_params=pltpu.CompilerParams(vmem_limit_bytes=VMEM_LIMIT),
    )(*shards)


def _swap_halves(gs, *, name):
    n = len(gs)

    def body(*refs):
        ins, outs = refs[:n], refs[n:2 * n]
        send_sems, recv_sems = refs[2 * n:]
        x, y, c = _position()
        cps = []
        for k in range(n):
            m = gs[k].shape[1] // 2
            src = ins[k].at[:, pl.ds(pl.multiple_of((1 - c) * m, 16), m), :]
            cp = pltpu.make_async_remote_copy(src_ref=src, dst_ref=outs[k], send_sem=send_sems.at[k],
                                              recv_sem=recv_sems.at[k], device_id=(x, y, 1 - c), device_id_type=MESH)
            cp.start()
            cps.append(cp)
        for cp in cps:
            cp.wait()

    return _pcall(
        body, name=name, in_specs=[HBM_SPEC] * n, out_specs=[HBM_SPEC] * n,
        out_shape=[jax.ShapeDtypeStruct((g.shape[0], g.shape[1] // 2, g.shape[2]), g.dtype) for g in gs],
        scratch_shapes=[pltpu.SemaphoreType.DMA((n,)), pltpu.SemaphoreType.DMA((n,))],
    )(*gs)


def _add_halves(gs, recvs, *, name, tm=256):
    n = len(gs)
    _, r, cdim = gs[0].shape
    m = r // 2
    tm = min(tm, m)
    nb = m // tm
    c_idx = lax.axis_index("c").astype(jnp.int32).reshape(1)

    def body(c_ref, *refs):
        for k in range(n):
            refs[2 * n + k][...] = (refs[k][...].astype(f32) + refs[n + k][...].astype(f32)).astype(refs[2 * n + k].dtype)

    mine = pl.BlockSpec((None, tm, cdim), lambda j, i, c_ref: (j, c_ref[0] * nb + i, 0))
    half = pl.BlockSpec((None, tm, cdim), lambda j, i, c_ref: (j, i, 0))
    return _pcall(
        body, name=name,
        grid_spec=pltpu.PrefetchScalarGridSpec(
            num_scalar_prefetch=1, grid=(N_SHARD, nb),
            in_specs=[mine] * n + [half] * n, out_specs=[half] * n),
        out_shape=[jax.ShapeDtypeStruct((N_SHARD, m, cdim), g.dtype) for g in gs],
        compiler_params=_params(2),
    )(c_idx, *gs, *recvs)


def _scatter_partials(ps, *, name):
    n = len(ps)

    def body(*refs):
        ins, outs = refs[:n], refs[n:2 * n]
        send_sems, recv_sems = refs[2 * n:]
        x, y, c = _position()
        me_chip = 2 * x + y
        cps = []
        for k in range(n):
            for j, (px, py) in enumerate(_other_chips(x, y)):
                cp = pltpu.make_async_remote_copy(
                    src_ref=ins[k].at[2 * px + py], dst_ref=outs[k].at[me_chip],
                    send_sem=send_sems.at[3 * k + j], recv_sem=recv_sems.at[3 * k + j],
                    device_id=(px, py, c), device_id_type=MESH)
                cp.start()
                cps.append(cp)
        for cp in cps:
            cp.wait()

    return _pcall(
        body, name=name, in_specs=[HBM_SPEC] * n, out_specs=[HBM_SPEC] * n,
        out_shape=[jax.ShapeDtypeStruct(p.shape, p.dtype) for p in ps],
        scratch_shapes=[pltpu.SemaphoreType.DMA((3 * n,)), pltpu.SemaphoreType.DMA((3 * n,))],
    )(*ps)


def _sum_slabs(ps, qs, *, name, tm=128):
    n = len(qs)
    _, m, cdim = qs[0].shape
    tm = min(tm, m)
    nb = m // tm
    assert m % tm == 0, (m, tm)
    where = jnp.stack([2 * lax.axis_index("x") + lax.axis_index("y"), lax.axis_index("c")]).astype(jnp.int32)

    def body(w_ref, *refs):
        for k in range(n):
            own, q1, q2, q3 = (refs[4 * k + t][...].astype(f32) for t in range(4))
            refs[4 * n + k][...] = ((own + q1) + q2) + q3

    def slab(flip):
        return pl.BlockSpec((None, tm, cdim), lambda i, w_ref: (jnp.bitwise_xor(w_ref[0], flip), i, 0))

    operands = []
    for p, q in zip(ps, qs):
        operands += [p, q, q, q]
    return _pcall(
        body, name=name,
        grid_spec=pltpu.PrefetchScalarGridSpec(
            num_scalar_prefetch=1, grid=(nb,),
            in_specs=[slab(0), slab(2), slab(1), slab(3)] * n,
            out_specs=[pl.BlockSpec((tm, cdim), lambda i, w_ref: (w_ref[1] * nb + i, 0))] * n),
        out_shape=[jax.ShapeDtypeStruct((2 * m, cdim), f32) for _ in qs],
        compiler_params=_params(1),
    )(where, *operands)


def _join_halves(fs, *, name):
    n = len(fs)

    def body(*refs):
        outs = refs[n:2 * n]
        send_sems, recv_sems = refs[2 * n:]
        x, y, c = _position()
        cps = []
        for k in range(n):
            m = fs[k].shape[0] // 2
            half = outs[k].at[pl.ds(pl.multiple_of(c * m, 8), m), :]
            cp = pltpu.make_async_remote_copy(src_ref=half, dst_ref=half, send_sem=send_sems.at[k],
                                              recv_sem=recv_sems.at[k], device_id=(x, y, 1 - c), device_id_type=MESH)
            cp.start()
            cps.append(cp)
        for cp in cps:
            cp.wait()

    return _pcall(
        body, name=name, in_specs=[HBM_SPEC] * n, out_specs=[HBM_SPEC] * n,
        out_shape=[jax.ShapeDtypeStruct(f.shape, f.dtype) for f in fs],
        input_output_aliases={k: k for k in range(n)},
        scratch_shapes=[pltpu.SemaphoreType.DMA((n,)), pltpu.SemaphoreType.DMA((n,))],
    )(*fs)


def _all_reduce_small(v, after=None, *, name):
    r = v.shape[0]
    extra = [] if after is None else [after]

    def body(v_ref, *refs):
        out_ref, buf, send_sems, recv_sems, local_sem = refs[len(extra):]
        x, y, c = _position()
        me, sibling = (x, y, c), (x, y, 1 - c)
        chips = _other_chips(x, y)

        def rows(px, py, pc):
            return buf.at[pl.ds(pl.multiple_of((4 * px + 2 * py + pc) * r, 8), r), :]

        def copy(k, block, to, src=None):
            return pltpu.make_async_remote_copy(
                src_ref=rows(*block) if src is None else src, dst_ref=rows(*block),
                send_sem=send_sems.at[k], recv_sem=recv_sems.at[k], device_id=to, device_id_type=MESH)

        mine = pltpu.make_async_copy(v_ref, rows(*me), local_sem)
        mine.start()
        first = [copy(0, me, sibling, src=v_ref)]
        first += [copy(1 + j, me, (*chip, c), src=v_ref) for j, chip in enumerate(chips)]
        for cp in first:
            cp.start()
        passed = [copy(4 + j, (*chip, c), sibling) for j, chip in enumerate(chips)]
        for j, chip in enumerate(chips):
            copy(1 + j, (*chip, c), me).wait_recv()
            passed[j].start()
        copy(0, sibling, me).wait_recv()
        for j, chip in enumerate(chips):
            copy(4 + j, (*chip, 1 - c), me).wait_recv()
        for cp in first + passed:
            cp.wait_send()
        mine.wait()
        acc = buf[0:r, :]
        for d in range(1, N_DEV):
            acc = acc + buf[d * r:(d + 1) * r, :]
        out_ref[...] = acc

    return _pcall(
        body, name=name, in_specs=[VMEM_SPEC] + [HBM_SPEC] * len(extra), out_specs=VMEM_SPEC,
        out_shape=jax.ShapeDtypeStruct((r, LANES), f32),
        scratch_shapes=[pltpu.VMEM((N_DEV * r, LANES), f32), pltpu.SemaphoreType.DMA((7,)),
                        pltpu.SemaphoreType.DMA((7,)), pltpu.SemaphoreType.DMA],
    )(v, *extra)


SEM_SPEC = pl.BlockSpec(memory_space=pltpu.SEMAPHORE)
HBM_ONLY = pl.BlockSpec(memory_space=pltpu.HBM)
EFFECT = pltpu.SideEffectType.DATAFLOW_SIDE_EFFECTING


def _sends(copies):
    return copies[0] if isinstance(copies, tuple) else copies


def _arrivals(copies):
    return copies[1] if isinstance(copies, tuple) else copies


def _split_start(bufs, copies_fn, n_sems, *, name):
    n = len(bufs)

    def body(*refs):
        send_sems, recv_sems = refs[n], refs[n + 1]
        thru = refs[n + 2:2 * n + 2]
        token = refs[2 * n + 2]
        for cp in _sends(copies_fn(thru, send_sems, recv_sems)):
            cp.start()
        token[...] = jnp.zeros_like(token)

    outs = _pcall(
        body, name=name,
        out_shape=(pltpu.SemaphoreType.DMA((n_sems,)), pltpu.SemaphoreType.DMA((n_sems,)),
                   *[pltpu.HBM(b.shape, b.dtype) for b in bufs], jax.ShapeDtypeStruct((8, LANES), f32)),
        in_specs=[HBM_ONLY] * n,
        out_specs=(SEM_SPEC, SEM_SPEC, *[HBM_ONLY] * n, VMEM_SPEC),
        input_output_aliases={k: 2 + k for k in range(n)},
        compiler_params=pltpu.CompilerParams(has_side_effects=EFFECT),
    )(*[pltpu.with_memory_space_constraint(b, pltpu.HBM) for b in bufs])
    return outs[0], outs[1], list(outs[2:2 + n]), outs[2 + n]


def _split_wait(thru, send_sems, recv_sems, after, copies_fn, *, name):
    n = len(thru)

    def body(*refs):
        copies = copies_fn(refs[:n], refs[n], refs[n + 1])
        for cp in _sends(copies):
            cp.wait_send()
        for cp in _arrivals(copies):
            cp.wait_recv()

    return list(_pcall(
        body, name=name,
        out_shape=tuple(pltpu.HBM(b.shape, b.dtype) for b in thru),
        in_specs=[HBM_ONLY] * n + [SEM_SPEC, SEM_SPEC] + [HBM_SPEC] * len(after),
        out_specs=tuple([HBM_ONLY] * n),
        input_output_aliases={k: k for k in range(n)},
        compiler_params=pltpu.CompilerParams(has_side_effects=EFFECT),
    )(*thru, send_sems, recv_sems, *after))


def _scatter_copies(n):
    def copies(bufs, send_sems, recv_sems):
        x, y, c = _position()
        me_chip = 2 * x + y
        cps = []
        for k in range(n):
            for j, (px, py) in enumerate(_other_chips(x, y)):
                cps.append(pltpu.make_async_remote_copy(
                    src_ref=bufs[k].at[2 * px + py], dst_ref=bufs[n + k].at[me_chip],
                    send_sem=send_sems.at[3 * k + j], recv_sem=recv_sems.at[3 * k + j],
                    device_id=(px, py, c), device_id_type=MESH))
        return cps
    return copies


N_PEERS = N_DEV - 1


def _direct_copies(n):
    def copies(bufs, send_sems, recv_sems):
        x, y, c = _position()
        me_chip = 2 * x + y
        sends, arrivals = [], []
        for k in range(n):
            m = bufs[k].shape[1] // 2
            land = bufs[n + k]

            def rows(slab, half, k=k, m=m):
                start = half * m if isinstance(half, int) else pl.multiple_of(half * m, 16)
                return bufs[k].at[slab, pl.ds(start, m), :]

            def copy(src, slot, send_idx, recv_idx, to, k=k, land=land):
                return pltpu.make_async_remote_copy(
                    src_ref=src, dst_ref=land.at[slot], send_sem=send_sems.at[N_PEERS * k + send_idx],
                    recv_sem=recv_sems.at[N_PEERS * k + recv_idx], device_id=to, device_id_type=MESH)

            sends.append(copy(rows(me_chip, 1 - c), 0, 0, 0, (x, y, 1 - c)))
            arrivals.append(copy(rows(me_chip, c), 0, 0, 0, (x, y, 1 - c)))
            for t, (px, py) in enumerate(_other_chips(x, y)):
                for core in range(2):
                    sends.append(copy(rows(2 * px + py, core), 1 + 2 * t + c, 1 + 2 * t + core, 1 + 2 * t + c,
                                      (px, py, core)))
                    arrivals.append(copy(rows(me_chip, c), 1 + 2 * t + core, 1 + 2 * t + core, 1 + 2 * t + core,
                                         (px, py, core)))
        return sends, arrivals
    return copies


def _sum_direct(gs, lands, *, name, tm=128):
    n = len(gs)
    _, m, cdim = lands[0].shape
    tm = min(tm, m)
    nb = m // tm
    assert m % tm == 0, (m, tm)
    where = jnp.stack([2 * lax.axis_index("x") + lax.axis_index("y"), lax.axis_index("c")]).astype(jnp.int32)

    def body(w_ref, *refs):
        for k in range(n):
            acc = refs[2 * k][...].astype(f32)
            for slot in range(N_PEERS):
                acc = acc + refs[2 * k + 1][slot].astype(f32)
            refs[2 * n + k][...] = acc

    own = pl.BlockSpec((None, tm, cdim), lambda i, w_ref: (w_ref[0], w_ref[1] * nb + i, 0))
    landed = pl.BlockSpec((N_PEERS, tm, cdim), lambda i, w_ref: (0, i, 0))
    operands = []
    for g, land in zip(gs, lands):
        operands += [g, land]
    return _pcall(
        body, name=name,
        grid_spec=pltpu.PrefetchScalarGridSpec(
            num_scalar_prefetch=1, grid=(nb,), in_specs=[own, landed] * n,
            out_specs=[pl.BlockSpec((tm, cdim), lambda i, w_ref: (w_ref[1] * nb + i, 0))] * n),
        out_shape=[jax.ShapeDtypeStruct((2 * m, cdim), f32) for _ in gs],
        compiler_params=_params(1),
    )(where, *operands)


def _block_rows(buf, px, py, pc):
    m = buf.shape[0] // N_DEV
    return buf.at[pl.ds(pl.multiple_of((4 * px + 2 * py + pc) * m, 16), m), :]


def _gather_ici_copies(n):
    def copies(bufs, send_sems, recv_sems):
        x, y, c = _position()
        cps = []
        for k in range(n):
            rows = _block_rows(bufs[k], x, y, c)
            targets = [(x, y, 1 - c)] + [(px, py, c) for px, py in _other_chips(x, y)]
            for j, to in enumerate(targets):
                cps.append(pltpu.make_async_remote_copy(
                    src_ref=rows, dst_ref=rows, send_sem=send_sems.at[4 * k + j], recv_sem=recv_sems.at[4 * k + j],
                    device_id=to, device_id_type=MESH))
        return cps
    return copies


def _gather_d2d_copies(n):
    def copies(bufs, send_sems, recv_sems):
        x, y, c = _position()
        cps = []
        for k in range(n):
            for j, (px, py) in enumerate(_other_chips(x, y)):
                rows = _block_rows(bufs[k], px, py, c)
                cps.append(pltpu.make_async_remote_copy(
                    src_ref=rows, dst_ref=rows, send_sem=send_sems.at[3 * k + j], recv_sem=recv_sems.at[3 * k + j],
                    device_id=(x, y, 1 - c), device_id_type=MESH))
        return cps
    return copies


def _cast_halves(shards, after, *, name):
    n = len(shards)
    where = jnp.stack([2 * lax.axis_index("x") + lax.axis_index("y"), lax.axis_index("c")]).astype(jnp.int32)

    def body(w_ref, *refs):
        for k in range(n):
            refs[n + 1 + k][...] = refs[k][...].astype(refs[n + 1 + k].dtype)

    def half(s):
        return (s.shape[0] // 2, s.shape[1])

    return _pcall(
        body, name=name,
        grid_spec=pltpu.PrefetchScalarGridSpec(
            num_scalar_prefetch=1, grid=(1,),
            in_specs=[pl.BlockSpec(half(s), lambda i, w_ref: (w_ref[1], 0)) for s in shards] + [HBM_SPEC],
            out_specs=[pl.BlockSpec(half(s), lambda i, w_ref: (2 * w_ref[0] + w_ref[1], 0)) for s in shards]),
        out_shape=[jax.ShapeDtypeStruct((N_SHARD * s.shape[0], s.shape[1]), MXU_DTYPE) for s in shards],
        compiler_params=_params(1),
    )(where, *shards, after)


class _SplitGather:
    def __init__(self, shards, after, tag):
        self.tag = tag
        self.n = len(shards)
        halves = _cast_halves(shards, after, name=f"{tag}_cast")
        self.ici = _split_start(halves, _gather_ici_copies(self.n), 4 * self.n, name=f"{tag}_ici_start")
        self.token = self.ici[3]

    def forward(self, after):
        send_sems, recv_sems, thru, _ = self.ici
        landed = _split_wait(thru, send_sems, recv_sems, after, _gather_ici_copies(self.n), name=f"{self.tag}_ici_wait")
        self.d2d = _split_start(landed, _gather_d2d_copies(self.n), 3 * self.n, name=f"{self.tag}_d2d_start")
        return self.d2d[3]

    def finish(self, after):
        send_sems, recv_sems, thru, _ = self.d2d
        return _split_wait(thru, send_sems, recv_sems, after, _gather_d2d_copies(self.n), name=f"{self.tag}_d2d_wait")


class _Overlap(_NoOverlap):
    def __init__(self, late_shards, ffn2_shards, after):
        self.late = _SplitGather(late_shards, after, "ag1")
        self.ffn2 = _SplitGather(ffn2_shards, self.late.token, "ag2")
        self.reduced = None
        self.ffn1_parts = []

    def start_token(self):
        return self.ffn2.token

    def late_weights(self, w, after):
        token = self.late.forward(after)
        f1d, w_in, wo = self.late.finish([token])
        w_in = w_in.reshape(N_SHARD, D_MODEL, IN_SHARD).transpose(1, 0, 2).reshape(D_MODEL, IN_COLS)
        return dict(f1d=f1d.reshape(N_SHARD, D_FF // N_SHARD, D_MODEL), wp=make_wp(w_in), wo=wo)

    def after_attention(self, after):
        return self.ffn2.forward(after)

    def ffn2_weights(self, w, after):
        full = self.ffn2.finish(after)
        fs = D_FF // N_SHARD
        return (full[0].reshape(N_SHARD, D_MODEL, fs), full[1].reshape(N_SHARD, D_MODEL, fs),
                full[2].reshape(N_SHARD, fs, D_MODEL))

    @staticmethod
    def _send_direct(grads, tag):
        lands = [lax.empty((N_PEERS, g.shape[1] // 2, g.shape[2]), g.dtype) for g in grads]
        return _split_start(list(grads) + lands, _direct_copies(len(grads)), N_PEERS * len(grads),
                            name=f"rs_direct_{tag}_start")

    def ffn2_grads(self, grads):
        self.scatter = self._send_direct(grads, "ffn2")
        return self.scatter[3]

    def ffn1_grads(self, grads):
        tag = "ffn1" + "ab"[len(self.ffn1_parts)]
        if not self.ffn1_parts:
            started = self._send_direct(grads, tag)
        else:
            recvs = _swap_halves(grads, name=f"rs_swap_{tag}")
            ps = list(_add_halves(grads[:2], recvs[:2], name=f"rs_add_{tag}_gu"))
            ps += list(_add_halves(grads[2:], recvs[2:], name=f"rs_add_{tag}_d"))
            lands = [lax.empty(p.shape, p.dtype) for p in ps]
            started = _split_start(ps + lands, _scatter_copies(3), 9, name=f"rs_scatter_{tag}_start")
        self.ffn1_parts.append((tag, started))
        return started[3]

    def ffn1_reduced(self, after):
        sums = []
        for direct, (tag, (send_sems, recv_sems, thru, _)) in zip((True, False), self.ffn1_parts):
            plan, add = (_direct_copies, _sum_direct) if direct else (_scatter_copies, _sum_slabs)
            done = _split_wait(thru, send_sems, recv_sems, after, plan(3), name=f"rs_{tag}_wait")
            sums += list(add(done[:2], done[3:5], name=f"rs_sum_{tag}_gu"))
            sums += list(add(done[2:3], done[5:], name=f"rs_sum_{tag}_d"))
        return sums

    def mixer_grads(self, dwp, dwo, small, loss):
        packed = jnp.concatenate([_pack_small(small), jnp.broadcast_to(loss, (8, LANES))], axis=0)
        summed = _all_reduce_small(packed, name="ar_small")
        self.small_sum, self.loss_sum = summed[:-8], summed[-8, 0]
        gwin = dwp[:, :IN_COLS].reshape(D_MODEL, N_SHARD, IN_SHARD).transpose(1, 0, 2).astype(GRAD_DTYPE)
        gwo = dwo.reshape(N_SHARD, D_MODEL // N_SHARD, D_MODEL).astype(GRAD_DTYPE)
        self.scatter_mix = self._send_direct([gwin, gwo], "mix")
        return self.scatter_mix[3]

    def mixer_reduced(self, after):
        send_sems, recv_sems, thru, _ = self.scatter_mix
        done = _split_wait(thru, send_sems, recv_sems, after, _direct_copies(2), name="rs_direct_mix_wait")
        return [_sum_direct([done[k]], [done[2 + k]], name=f"rs_sum_{tag}")[0] for k, tag in enumerate(["w_in", "w_out"])]

    def before_ffn1_bwd(self, after):
        send_sems, recv_sems, thru, _ = self.scatter
        n = len(thru) // 2
        done = _split_wait(thru, send_sems, recv_sems, after, _direct_copies(n), name="rs_direct_ffn2_wait")
        self.reduced = list(_sum_direct(done[:n], done[n:], name="rs_sum_ffn2"))


def _adamw(gs, ws, ms, vs, *, name, tm=256):
    n = len(gs)
    r, cdim = gs[0].shape
    tm = r if tm is None else min(tm, r)
    assert r % tm == 0, (r, tm)
    c1 = 1.0 / (1.0 - ADAM_B1 ** ADAM_STEP)
    c2 = 1.0 / (1.0 - ADAM_B2 ** ADAM_STEP)

    def body(*refs):
        for k in range(n):
            g = refs[k][...]
            w = refs[n + k][...]
            m = ADAM_B1 * refs[2 * n + k][...] + (1.0 - ADAM_B1) * g
            v = ADAM_B2 * refs[3 * n + k][...] + (1.0 - ADAM_B2) * (g * g)
            refs[4 * n + k][...] = g
            refs[5 * n + k][...] = -ADAM_LR * ((m * c1) / (jnp.sqrt(v * c2) + ADAM_EPS) + ADAM_WD * w)
            refs[6 * n + k][...] = m
            refs[7 * n + k][...] = v

    flat = pl.BlockSpec((tm, cdim), lambda i: (i, 0))
    like_w = flat if ws[0].ndim == 2 else pl.BlockSpec((None, tm, cdim), lambda i: (0, i, 0))
    outs = _pcall(
        body, name=name, grid=(r // tm,), in_specs=[flat] * n + [like_w] * (3 * n), out_specs=[like_w] * (4 * n),
        out_shape=[jax.ShapeDtypeStruct(ws[0].shape, f32)] * (4 * n),
        compiler_params=_params(1),
    )(*gs, *ws, *ms, *vs)
    return outs[:n], outs[n:2 * n], outs[2 * n:3 * n], outs[3 * n:]


BIG = ["ffn1_w_gate", "ffn1_w_up", "ffn1_w_down", "ffn2_w_gate", "ffn2_w_up", "ffn2_w_down"]
SMALL = ["ln1_g", "ln1_b", "b_forget", "conv_w", "conv_b", "rg_wa", "rg_ba", "rg_wx", "rg_bx", "lru_lambda",
         "ln2_g", "ln2_b", "ln3_g", "ln3_b"]
WEIGHTS = ["ffn1_w_gate", "ffn1_w_up", "ffn1_w_down", "ln1_g", "ln1_b", "w_in", "b_forget", "conv_w", "conv_b",
           "rg_wa", "rg_ba", "rg_wx", "rg_bx", "lru_lambda", "w_out", "ln2_g", "ln2_b",
           "ffn2_w_gate", "ffn2_w_up", "ffn2_w_down", "ln3_g", "ln3_b"]


def _pack_small(parts):
    rows = []
    for n in SMALL:
        flat = parts[n].reshape(-1)
        pad = (-flat.shape[0]) % LANES
        rows.append(jnp.pad(flat, (0, pad)).reshape(-1, LANES))
    packed = jnp.concatenate(rows, axis=0)
    return jnp.pad(packed, ((0, (-packed.shape[0]) % 8), (0, 0)))


def _unpack_small(packed, shapes):
    out, r0 = {}, 0
    for n in SMALL:
        size = math.prod(shapes[n])
        nr = -(-size // LANES)
        out[n] = packed[r0:r0 + nr].reshape(-1)[:size].reshape(shapes[n])
        r0 += nr
    return out


def kernel(x, ffn1_w_gate, ffn1_w_up, ffn1_w_down, ln1_g, ln1_b, w_in, b_forget, conv_w, conv_b, rg_wa, rg_ba, rg_wx, rg_bx, lru_lambda, w_out, ln2_g, ln2_b, ffn2_w_gate, ffn2_w_up, ffn2_w_down, ln3_g, ln3_b, loss_target, m_ffn1_w_gate, m_ffn1_w_up, m_ffn1_w_down, m_ln1_g, m_ln1_b, m_w_in, m_b_forget, m_conv_w, m_conv_b, m_rg_wa, m_rg_ba, m_rg_wx, m_rg_bx, m_lru_lambda, m_w_out, m_ln2_g, m_ln2_b, m_ffn2_w_gate, m_ffn2_w_up, m_ffn2_w_down, m_ln3_g, m_ln3_b, v_ffn1_w_gate, v_ffn1_w_up, v_ffn1_w_down, v_ln1_g, v_ln1_b, v_w_in, v_b_forget, v_conv_w, v_conv_b, v_rg_wa, v_rg_ba, v_rg_wx, v_rg_bx, v_lru_lambda, v_w_out, v_ln2_g, v_ln2_b, v_ffn2_w_gate, v_ffn2_w_up, v_ffn2_w_down, v_ln3_g, v_ln3_b):
    args = dict(locals())
    w = {n: args[n] for n in WEIGHTS}
    mom = {n: args["m_" + n] for n in WEIGHTS}
    var = {n: args["v_" + n] for n in WEIGHTS}
    chip = 2 * lax.axis_index("x") + lax.axis_index("y")

    g1 = _all_gather_bf16([w[n][0] for n in BIG[:2]], name="ag_ffn1_up")
    fs = D_FF // N_SHARD
    full = dict(
        f1g=g1[0].reshape(N_SHARD, D_MODEL, fs), f1u=g1[1].reshape(N_SHARD, D_MODEL, fs),
        bfp=jnp.pad(b_forget, ((0, 0), (0, LANES - HEADS))),
        ln1_g=ln1_g, ln1_b=ln1_b, ln2_g=ln2_g, ln2_b=ln2_b, ln3_g=ln3_g, ln3_b=ln3_b,
        conv_b=conv_b, rg_wa=rg_wa[0], rg_wx=rg_wx[0], rg_ba=rg_ba[0], rg_bx=rg_bx[0], lam=lru_lambda,
    )
    cw_place = lax.dynamic_update_slice(jnp.zeros((8, LRU_W), f32), conv_w[0] * 0.5, (0, chip * (LRU_W // N_SHARD)))
    cw_full = _all_reduce_small(cw_place.reshape(-1, LANES), g1[0], name="ag_conv_w")
    full["conv_w"] = cw_full.reshape(8, LRU_W)[:CONV_K]

    hooks = _Overlap([w["ffn1_w_down"][0], w["w_in"][0], w["w_out"][0]], [w[n][0] for n in BIG[3:]], cw_full)
    loss_rep, dx, g = _local_step(x[0], loss_target[0], full, hooks)
    loss = hooks.loss_sum

    token1 = hooks.ffn1_grads(g["f1"][1])
    red = _join_halves(hooks.reduced + hooks.mixer_reduced([token1]), name="rs_join_rest")
    grads = dict(zip(BIG[3:] + ["w_in", "w_out"], red))

    small_shapes = {n: w[n].shape for n in SMALL}
    small_shapes["conv_w"] = (1, CONV_K, LRU_W)
    gs_red = _unpack_small(hooks.small_sum, small_shapes)
    gs_red["conv_w"] = lax.dynamic_slice(gs_red["conv_w"], (0, 0, chip * (LRU_W // N_SHARD)),
                                         (1, CONV_K, LRU_W // N_SHARD))
    grads.update(gs_red)

    delta, new_m, new_v = {}, {}, {}

    def adamw(names, name, **kw):
        g3, d, nm, nv = _adamw([grads[n] for n in names], [w[n] for n in names], [mom[n] for n in names],
                               [var[n] for n in names], name=name, **kw)
        for i, n in enumerate(names):
            grads[n], delta[n], new_m[n], new_v[n] = g3[i], d[i], nm[i], nv[i]

    adamw(BIG[3:], "adamw_ffn2", tm=128)
    adamw(["w_in"], "adamw_w_in")
    adamw(["w_out"], "adamw_w_out")
    shard_shapes = {n: w[n].shape for n in SMALL}
    _, d, nm, nv = _adamw([_pack_small({n: grads[n] for n in SMALL})], [_pack_small({n: w[n] for n in SMALL})],
                          [_pack_small({n: mom[n] for n in SMALL})], [_pack_small({n: var[n] for n in SMALL})],
                          name="adamw_small", tm=None)
    for dst, packed in ((delta, d[0]), (new_m, nm[0]), (new_v, nv[0])):
        dst.update(_unpack_small(packed, shard_shapes))

    worked = [new_v["ffn2_w_down"], new_v["w_in"], new_v["w_out"], nv[0]]
    ga, ua, da, gb, ub, db = _join_halves(hooks.ffn1_reduced(worked), name="rs_join_ffn1")
    grads["ffn1_w_gate"] = jnp.concatenate([ga, gb], axis=1)
    grads["ffn1_w_up"] = jnp.concatenate([ua, ub], axis=1)
    grads["ffn1_w_down"] = jnp.concatenate([da, db], axis=0)
    adamw(BIG[:3], "adamw_ffn1", tm=128)

    def shaped(tree, n):
        return tree[n].reshape(w[n].shape)

    return (loss, dx[None], *[shaped(grads, n) for n in WEIGHTS], *[shaped(delta, n) for n in WEIGHTS],
            *[shaped(new_m, n) for n in WEIGHTS], *[shaped(new_v, n) for n in WEIGHTS])
```

```python
import functools
import math

import jax
import jax.numpy as jnp
from jax import lax
from jax.experimental import pallas as pl
from jax.experimental.pallas import tpu as pltpu

f32 = jnp.float32
MXU_DTYPE = jnp.bfloat16
GRAD_DTYPE = jnp.bfloat16

D_MODEL = 1024
D_FF = 4096
N_SHARD = 4
N_DEV = 8
FOX_W = 512
LRU_W = 512
HEADS = 8
HEAD_DIM = 64
CONV_K = 4
IN_COLS = 2568
IN_SHARD = IN_COLS // N_SHARD
QKV_W = 3 * FOX_W
Z_PAD = 2688
LANES = 128
LN_EPS = 1e-5
DN_ALPHA = 2.0 ** 0.25
LRU_C = 8.0
NEG_BIG = -1e30
VMEM_LIMIT = 56 * 1024 * 1024

ADAM_LR = 0.001
ADAM_B1 = 0.9
ADAM_B2 = 0.999
ADAM_EPS = 1e-08
ADAM_WD = 0.01
ADAM_STEP = 10


def _pcall(body, **kw):
    return pl.pallas_call(body, **kw)


def _params(n_grid, vmem=VMEM_LIMIT):
    return pltpu.CompilerParams(dimension_semantics=("arbitrary",) * n_grid, vmem_limit_bytes=vmem)


def _dot(a, b):
    return jnp.dot(a, b, preferred_element_type=f32)


def _dot_nt(a, b):
    return lax.dot_general(a, b, (((1,), (1,)), ((), ())), preferred_element_type=f32)


def _dot_tn(a, b):
    return lax.dot_general(a, b, (((0,), (0,)), ((), ())), preferred_element_type=f32)


def _sigmoid(x):
    return 1.0 / (1.0 + jnp.exp(-x))


def _layer_norm_stats(y):
    mu = jnp.mean(y, axis=-1, keepdims=True)
    yc = y - mu
    var = jnp.mean(yc * yc, axis=-1, keepdims=True)
    rstd = lax.rsqrt(var + LN_EPS)
    return yc * rstd, rstd


def _ln_backward(dy, xhat, rstd, gamma):
    dxhat = dy * gamma
    m1 = jnp.mean(dxhat, axis=-1, keepdims=True)
    m2 = jnp.mean(dxhat * xhat, axis=-1, keepdims=True)
    dyp = rstd * (dxhat - m1 - xhat * m2)
    return dyp, jnp.sum(dy * xhat, axis=0, keepdims=True), jnp.sum(dy, axis=0, keepdims=True)


def _ffn_fwd(x, wg, wu, wd, ln_g, ln_b, *, name, tm=1024, tf=512):
    T = x.shape[0]
    tm = min(tm, T)
    fs = D_FF // N_SHARD
    cpf = fs // tf
    nf = D_FF // tf
    nt = T // tm

    def body(x_ref, wg_ref, wu_ref, wd_ref, g_ref, b_ref,
             xb_ref, gact_ref, uact_ref, xhat_ref, xn_ref, rstd_ref, acc_ref):
        f = pl.program_id(1)

        @pl.when(f == 0)
        def _():
            xb_ref[...] = x_ref[...].astype(MXU_DTYPE)
            acc_ref[...] = jnp.zeros_like(acc_ref)

        xb = xb_ref[...]
        g = _dot(xb, wg_ref[...])
        u = _dot(xb, wu_ref[...])
        h = (g * _sigmoid(g)) * u
        gact_ref[...] = g.astype(gact_ref.dtype)
        uact_ref[...] = u.astype(uact_ref.dtype)
        acc_ref[...] += _dot(h.astype(MXU_DTYPE), wd_ref[...])

        @pl.when(f == nf - 1)
        def _():
            y = DN_ALPHA * x_ref[...] + 0.5 * acc_ref[...]
            xhat, rstd = _layer_norm_stats(y)
            xhat_ref[...] = xhat
            xn_ref[...] = (xhat * g_ref[...] + b_ref[...]).astype(xn_ref.dtype)
            rstd_ref[...] = jnp.broadcast_to(rstd, rstd_ref.shape)

    row = lambda i, f: (i, 0)
    return _pcall(
        body, name=name, grid=(nt, nf),
        in_specs=[
            pl.BlockSpec((tm, D_MODEL), row),
            pl.BlockSpec((None, D_MODEL, tf), lambda i, f: (f // cpf, 0, f % cpf)),
            pl.BlockSpec((None, D_MODEL, tf), lambda i, f: (f // cpf, 0, f % cpf)),
            pl.BlockSpec((None, tf, D_MODEL), lambda i, f: (f // cpf, f % cpf, 0)),
            pl.BlockSpec((1, D_MODEL), lambda i, f: (0, 0)),
            pl.BlockSpec((1, D_MODEL), lambda i, f: (0, 0)),
        ],
        out_specs=[
            pl.BlockSpec((tm, D_MODEL), row),
            pl.BlockSpec((tm, tf), lambda i, f: (i, f)),
            pl.BlockSpec((tm, tf), lambda i, f: (i, f)),
            pl.BlockSpec((tm, D_MODEL), row),
            pl.BlockSpec((tm, D_MODEL), row),
            pl.BlockSpec((tm, LANES), row),
        ],
        out_shape=[
            jax.ShapeDtypeStruct((T, D_MODEL), MXU_DTYPE),
            jax.ShapeDtypeStruct((T, D_FF), MXU_DTYPE),
            jax.ShapeDtypeStruct((T, D_FF), MXU_DTYPE),
            jax.ShapeDtypeStruct((T, D_MODEL), f32),
            jax.ShapeDtypeStruct((T, D_MODEL), MXU_DTYPE),
            jax.ShapeDtypeStruct((T, LANES), f32),
        ],
        scratch_shapes=[pltpu.VMEM((tm, D_MODEL), f32)],
        compiler_params=_params(2),
    )(x, wg, wu, wd, ln_g, ln_b)


def _ffn_up(x, wg, wu, after=None, *, name, tm=1024, tf=512):
    T = x.shape[0]
    tm = min(tm, T)
    cpf = (D_FF // N_SHARD) // tf
    nf = D_FF // tf
    extra = [] if after is None else [after]

    def body(x_ref, wg_ref, wu_ref, *refs):
        xb_ref, gact_ref, uact_ref, hact_ref = refs[len(extra):]

        @pl.when(pl.program_id(1) == 0)
        def _():
            xb_ref[...] = x_ref[...].astype(MXU_DTYPE)

        xb = xb_ref[...]
        g = _dot(xb, wg_ref[...])
        u = _dot(xb, wu_ref[...])
        gact_ref[...] = g.astype(gact_ref.dtype)
        uact_ref[...] = u.astype(uact_ref.dtype)
        hact_ref[...] = ((g * _sigmoid(g)) * u).astype(hact_ref.dtype)

    row = lambda i, f: (i, 0)
    tile = pl.BlockSpec((tm, tf), lambda i, f: (i, f))
    cols = pl.BlockSpec((None, D_MODEL, tf), lambda i, f: (f // cpf, 0, f % cpf))
    return _pcall(
        body, name=name, grid=(T // tm, nf),
        in_specs=[pl.BlockSpec((tm, D_MODEL), row), cols, cols] + [pl.BlockSpec(memory_space=pl.ANY)] * len(extra),
        out_specs=[pl.BlockSpec((tm, D_MODEL), row), tile, tile, tile],
        out_shape=[jax.ShapeDtypeStruct((T, D_MODEL), MXU_DTYPE)] + [jax.ShapeDtypeStruct((T, D_FF), MXU_DTYPE)] * 3,
        compiler_params=_params(2),
    )(x, wg, wu, *extra)


def _ffn_down_ln(x, hact, wd, ln_g, ln_b, *, name, tm=1024):
    T = x.shape[0]
    tm = min(tm, T)
    fs = D_FF // N_SHARD

    def body(x_ref, h_ref, wd_ref, g_ref, b_ref, xhat_ref, xn_ref, rstd_ref, acc_ref):
        k = pl.program_id(1)

        @pl.when(k == 0)
        def _():
            acc_ref[...] = jnp.zeros_like(acc_ref)

        acc_ref[...] += _dot(h_ref[...], wd_ref[...])

        @pl.when(k == N_SHARD - 1)
        def _():
            xhat, rstd = _layer_norm_stats(DN_ALPHA * x_ref[...] + 0.5 * acc_ref[...])
            xhat_ref[...] = xhat
            xn_ref[...] = (xhat * g_ref[...] + b_ref[...]).astype(xn_ref.dtype)
            rstd_ref[...] = jnp.broadcast_to(rstd, rstd_ref.shape)

    row = lambda i, k: (i, 0)
    vec = pl.BlockSpec((1, D_MODEL), lambda i, k: (0, 0))
    return _pcall(
        body, name=name, grid=(T // tm, N_SHARD),
        in_specs=[pl.BlockSpec((tm, D_MODEL), row), pl.BlockSpec((tm, fs), lambda i, k: (i, k)),
                  pl.BlockSpec((None, fs, D_MODEL), lambda i, k: (k, 0, 0)), vec, vec],
        out_specs=[pl.BlockSpec((tm, D_MODEL), row), pl.BlockSpec((tm, D_MODEL), row), pl.BlockSpec((tm, LANES), row)],
        out_shape=[jax.ShapeDtypeStruct((T, D_MODEL), f32), jax.ShapeDtypeStruct((T, D_MODEL), MXU_DTYPE),
                   jax.ShapeDtypeStruct((T, LANES), f32)],
        scratch_shapes=[pltpu.VMEM((tm, D_MODEL), f32)],
        compiler_params=_params(2),
    )(x, hact, wd, ln_g, ln_b)


def _ffn_bwd(dyp, xb, gact, uact, wg, wu, wd, after=None, *, name, tm=512, tf=512, part=None, dx_init=None):
    T = dyp.shape[0]
    tm = min(tm, T)
    fs = D_FF // N_SHARD
    cpf = fs // tf
    nt = T // tm
    nf = D_FF // tf if part is None else N_SHARD
    wf = fs if part is None else tf
    slab = (lambda f: f // cpf) if part is None else (lambda f: f)
    chunk = (lambda f: f % cpf) if part is None else (lambda f: part)
    extra = ([] if dx_init is None else [dx_init]) + ([] if after is None else [after])

    def body(dyp_ref, xb_ref, g_ref, u_ref, wg_ref, wu_ref, wd_ref, *refs):
        dx_hbm, dwg_ref, dwu_ref, dwd_ref, dx_sc, dwg_sc, dwu_sc, dwd_sc, sem = refs[len(extra):]
        f = pl.program_id(0)
        i = pl.program_id(1)
        rows = pl.ds(pl.multiple_of(i * tm, tm), tm)
        dyp_t = dyp_ref[...]
        dy = (0.5 * dyp_t).astype(MXU_DTYPE)

        @pl.when(i == 0)
        def _():
            dwg_sc[...] = jnp.zeros_like(dwg_sc)
            dwu_sc[...] = jnp.zeros_like(dwu_sc)
            dwd_sc[...] = jnp.zeros_like(dwd_sc)

        @pl.when(f == 0)
        def _():
            dx_sc[rows, :] = DN_ALPHA * dyp_t if dx_init is None else refs[0][...]

        g = g_ref[...].astype(f32)
        u = u_ref[...].astype(f32)
        sig = _sigmoid(g)
        silu = g * sig
        dh = _dot_nt(dy, wd_ref[...])
        dg = (dh * u * (sig * (1.0 + g * (1.0 - sig)))).astype(MXU_DTYPE)
        du = (dh * silu).astype(MXU_DTYPE)
        hb = (silu * u).astype(MXU_DTYPE)
        dx_sc[rows, :] += _dot_nt(dg, wg_ref[...]) + _dot_nt(du, wu_ref[...])
        xb_t = xb_ref[...]
        dwg_sc[...] += _dot_tn(xb_t, dg)
        dwu_sc[...] += _dot_tn(xb_t, du)
        dwd_sc[...] += _dot_tn(hb, dy)

        @pl.when(i == nt - 1)
        def _():
            dwg_ref[...] = dwg_sc[...].astype(dwg_ref.dtype)
            dwu_ref[...] = dwu_sc[...].astype(dwu_ref.dtype)
            dwd_ref[...] = dwd_sc[...].astype(dwd_ref.dtype)

        @pl.when(jnp.logical_and(f == nf - 1, i == nt - 1))
        def _():
            cp = pltpu.make_async_copy(dx_sc, dx_hbm, sem)
            cp.start()
            cp.wait()

    row = lambda f, i: (i, 0)
    return _pcall(
        body, name=name, grid=(nf, nt),
        in_specs=[
            pl.BlockSpec((tm, D_MODEL), row),
            pl.BlockSpec((tm, D_MODEL), row),
            pl.BlockSpec((tm, tf), lambda f, i: (i, slab(f) * cpf + chunk(f))),
            pl.BlockSpec((tm, tf), lambda f, i: (i, slab(f) * cpf + chunk(f))),
            pl.BlockSpec((None, D_MODEL, tf), lambda f, i: (slab(f), 0, chunk(f))),
            pl.BlockSpec((None, D_MODEL, tf), lambda f, i: (slab(f), 0, chunk(f))),
            pl.BlockSpec((None, tf, D_MODEL), lambda f, i: (slab(f), chunk(f), 0)),
        ] + ([] if dx_init is None else [pl.BlockSpec((tm, D_MODEL), row)])
        + ([] if after is None else [pl.BlockSpec(memory_space=pl.ANY)]),
        out_specs=[
            pl.BlockSpec(memory_space=pl.ANY),
            pl.BlockSpec((None, D_MODEL, tf), lambda f, i: (slab(f), 0, chunk(f) if part is None else 0)),
            pl.BlockSpec((None, D_MODEL, tf), lambda f, i: (slab(f), 0, chunk(f) if part is None else 0)),
            pl.BlockSpec((None, tf, D_MODEL), lambda f, i: (slab(f), chunk(f) if part is None else 0, 0)),
        ],
        out_shape=[
            jax.ShapeDtypeStruct((T, D_MODEL), f32),
            jax.ShapeDtypeStruct((N_SHARD, D_MODEL, wf), GRAD_DTYPE),
            jax.ShapeDtypeStruct((N_SHARD, D_MODEL, wf), GRAD_DTYPE),
            jax.ShapeDtypeStruct((N_SHARD, wf, D_MODEL), GRAD_DTYPE),
        ],
        scratch_shapes=[pltpu.VMEM((T, D_MODEL), f32), pltpu.VMEM((D_MODEL, tf), f32),
                        pltpu.VMEM((D_MODEL, tf), f32), pltpu.VMEM((tf, D_MODEL), f32),
                        pltpu.SemaphoreType.DMA],
        compiler_params=_params(2),
    )(dyp, xb, gact, uact, wg, wu, wd, *extra)


def _loss_ln_bwd(xhat, rstd, ln_g, ln_b, target, *, name, tm=512):
    T = xhat.shape[0]
    tm = min(tm, T)
    nt = T // tm

    def body(xhat_ref, rstd_ref, g_ref, b_ref, t_ref, dyp_ref, dg_ref, db_ref, loss_ref):
        i = pl.program_id(0)

        @pl.when(i == 0)
        def _():
            dg_ref[...] = jnp.zeros_like(dg_ref)
            db_ref[...] = jnp.zeros_like(db_ref)
            loss_ref[...] = jnp.zeros_like(loss_ref)

        xhat_t = xhat_ref[...]
        gamma = g_ref[...]
        err = xhat_t * gamma + b_ref[...] - t_ref[...]
        sq = jnp.sum(jnp.sum(err * err, axis=0, keepdims=True), axis=1, keepdims=True)
        loss_ref[...] += jnp.broadcast_to(sq * (0.5 / D_MODEL), loss_ref.shape)
        dy = err * (1.0 / D_MODEL)
        dyp, dgam, dbeta = _ln_backward(dy, xhat_t, rstd_ref[:, 0:1], gamma)
        dyp_ref[...] = dyp
        dg_ref[...] += dgam
        db_ref[...] += dbeta

    row = lambda i: (i, 0)
    const = lambda i: (0, 0)
    return _pcall(
        body, name=name, grid=(nt,),
        in_specs=[pl.BlockSpec((tm, D_MODEL), row), pl.BlockSpec((tm, LANES), row),
                  pl.BlockSpec((1, D_MODEL), const), pl.BlockSpec((1, D_MODEL), const),
                  pl.BlockSpec((tm, D_MODEL), row)],
        out_specs=[pl.BlockSpec((tm, D_MODEL), row), pl.BlockSpec((1, D_MODEL), const),
                   pl.BlockSpec((1, D_MODEL), const), pl.BlockSpec((1, LANES), const)],
        out_shape=[jax.ShapeDtypeStruct((T, D_MODEL), f32), jax.ShapeDtypeStruct((1, D_MODEL), f32),
                   jax.ShapeDtypeStruct((1, D_MODEL), f32), jax.ShapeDtypeStruct((1, LANES), f32)],
        compiler_params=_params(1),
    )(xhat, rstd, ln_g, ln_b, target)


def _ln_bwd(dy, xhat, rstd, ln_g, *, name, tm=512):
    T = xhat.shape[0]
    tm = min(tm, T)
    nt = T // tm

    def body(dy_ref, xhat_ref, rstd_ref, g_ref, dyp_ref, dg_ref, db_ref):
        i = pl.program_id(0)

        @pl.when(i == 0)
        def _():
            dg_ref[...] = jnp.zeros_like(dg_ref)
            db_ref[...] = jnp.zeros_like(db_ref)

        dyp, dgam, dbeta = _ln_backward(dy_ref[...], xhat_ref[...], rstd_ref[:, 0:1], g_ref[...])
        dyp_ref[...] = dyp
        dg_ref[...] += dgam
        db_ref[...] += dbeta

    row = lambda i: (i, 0)
    const = lambda i: (0, 0)
    return _pcall(
        body, name=name, grid=(nt,),
        in_specs=[pl.BlockSpec((tm, D_MODEL), row), pl.BlockSpec((tm, D_MODEL), row),
                  pl.BlockSpec((tm, LANES), row), pl.BlockSpec((1, D_MODEL), const)],
        out_specs=[pl.BlockSpec((tm, D_MODEL), row), pl.BlockSpec((1, D_MODEL), const),
                   pl.BlockSpec((1, D_MODEL), const)],
        out_shape=[jax.ShapeDtypeStruct((T, D_MODEL), f32), jax.ShapeDtypeStruct((1, D_MODEL), f32),
                   jax.ShapeDtypeStruct((1, D_MODEL), f32)],
        compiler_params=_params(1),
    )(dy, xhat, rstd, ln_g)


def _proj_in(xn, wp, bfp, *, name, tm=512):
    T = xn.shape[0]
    tm = min(tm, T)
    nt = T // tm

    def body(x_ref, w_ref, b_ref, qkv_ref, lxg_ref, fg_ref):
        z = _dot(x_ref[...], w_ref[...])
        qkv_ref[...] = z[:, :QKV_W].astype(qkv_ref.dtype)
        lxg_ref[...] = z[:, QKV_W:QKV_W + 2 * LRU_W]
        fg_ref[...] = z[:, QKV_W + 2 * LRU_W:] + b_ref[...]

    row = lambda i: (i, 0)
    const = lambda i: (0, 0)
    return _pcall(
        body, name=name, grid=(nt,),
        in_specs=[pl.BlockSpec((tm, D_MODEL), row), pl.BlockSpec((D_MODEL, Z_PAD), const),
                  pl.BlockSpec((1, LANES), const)],
        out_specs=[pl.BlockSpec((tm, QKV_W), row), pl.BlockSpec((tm, 2 * LRU_W), row),
                   pl.BlockSpec((tm, LANES), row)],
        out_shape=[jax.ShapeDtypeStruct((T, QKV_W), MXU_DTYPE), jax.ShapeDtypeStruct((T, 2 * LRU_W), f32),
                   jax.ShapeDtypeStruct((T, LANES), f32)],
        compiler_params=_params(1),
    )(xn, wp, bfp)


def _proj_in_bwd(dqa, dka, dva, dlxg, dfg, xn, dyp, wp, *, name, tm=512):
    T = xn.shape[0]
    tm = min(tm, T)
    nt = T // tm

    def body(dq_ref, dk_ref, dv_ref, dl_ref, dfg_ref, x_ref, dyp_ref, w_ref, dx_ref, dw_hbm, dw_sc, sem):
        i = pl.program_id(0)

        @pl.when(i == 0)
        def _():
            dw_sc[...] = jnp.zeros_like(dw_sc)

        low = _low_lanes((tm, LANES))

        def packed(ref):
            pairs = [jnp.where(low, ref[:, (2 * j) * LANES:(2 * j + 1) * LANES],
                               _swap_lane_halves(ref[:, (2 * j + 1) * LANES:(2 * j + 2) * LANES]))
                     for j in range(HEADS // 2)]
            return jnp.concatenate(pairs, axis=1).astype(MXU_DTYPE)

        dz = jnp.concatenate(
            [packed(dq_ref), packed(dk_ref), packed(dv_ref),
             dl_ref[...].astype(MXU_DTYPE), dfg_ref[...].astype(MXU_DTYPE)], axis=1)
        dx_ref[...] = DN_ALPHA * dyp_ref[...] + _dot_nt(dz, w_ref[...])
        dw_sc[...] += _dot_tn(x_ref[...], dz)

        @pl.when(i == nt - 1)
        def _():
            dw_sc[:, :FOX_W] = dw_sc[:, :FOX_W] * (1.0 / math.sqrt(HEAD_DIM))
            cp = pltpu.make_async_copy(dw_sc, dw_hbm, sem)
            cp.start()
            cp.wait()

    row = lambda i: (i, 0)
    const = lambda i: (0, 0)
    return _pcall(
        body, name=name, grid=(nt,),
        in_specs=[pl.BlockSpec((tm, HEADS * LANES), row), pl.BlockSpec((tm, HEADS * LANES), row),
                  pl.BlockSpec((tm, HEADS * LANES), row),
                  pl.BlockSpec((tm, 2 * LRU_W), row), pl.BlockSpec((tm, LANES), row),
                  pl.BlockSpec((tm, D_MODEL), row), pl.BlockSpec((tm, D_MODEL), row),
                  pl.BlockSpec((D_MODEL, Z_PAD), const)],
        out_specs=[pl.BlockSpec((tm, D_MODEL), row), pl.BlockSpec(memory_space=pl.ANY)],
        out_shape=[jax.ShapeDtypeStruct((T, D_MODEL), f32), jax.ShapeDtypeStruct((D_MODEL, Z_PAD), f32)],
        scratch_shapes=[pltpu.VMEM((D_MODEL, Z_PAD), f32), pltpu.SemaphoreType.DMA],
        compiler_params=_params(1),
    )(dqa, dka, dva, dlxg, dfg, xn, dyp, wp)


def _split3(x):
    hi = x.astype(jnp.bfloat16)
    r1 = x - hi.astype(f32)
    mid = r1.astype(jnp.bfloat16)
    lo = (r1 - mid.astype(f32)).astype(jnp.bfloat16)
    return hi, mid, lo


def _tri_dot(tri, x):
    hi, mid, lo = _split3(x)
    return _dot(tri, hi) + _dot(tri, mid) + _dot(tri, lo)


FOX_PAD = HEADS * LANES
AUX = HEAD_DIM


def _low_lanes(shape):
    return lax.broadcasted_iota(jnp.int32, shape, 1) < HEAD_DIM


def _swap_lane_halves(x):
    return pltpu.roll(x, HEAD_DIM, 1)


def _fox_prep(qkv, fgb, *, name, tm=512):
    T = fgb.shape[0]
    tm = min(tm, T)
    nt = T // tm

    def body(qkv_ref, fg_ref, qa_ref, ka_ref, va_ref, carry):
        i = pl.program_id(0)

        @pl.when(i == 0)
        def _():
            carry[...] = jnp.zeros_like(carry)

        x = fg_ref[...]
        ls = jnp.minimum(x, 0.0) - jnp.log(1.0 + jnp.exp(-jnp.abs(x)))
        r = lax.broadcasted_iota(jnp.int32, (tm, tm), 0)
        c = lax.broadcasted_iota(jnp.int32, (tm, tm), 1)
        tri = jnp.where(r >= c, 1.0, 0.0).astype(jnp.bfloat16)
        cum = _tri_dot(tri, ls) + carry[0:1, :]
        carry[...] = jnp.broadcast_to(cum[tm - 1:tm, :], carry.shape)

        lane = lax.broadcasted_iota(jnp.int32, (tm, LANES), 1)
        low = lane < HEAD_DIM
        ones_q = jnp.where(jnp.logical_and(lane >= AUX + 3, lane < AUX + 6), 1.0, 0.0)
        ones_k = jnp.where(jnp.logical_and(lane >= AUX, lane < AUX + 3), 1.0, 0.0)
        for j in range(HEADS // 2):
            pair = [qkv_ref[:, t * FOX_W + j * LANES:t * FOX_W + (j + 1) * LANES].astype(f32) for t in range(3)]
            for odd in range(2):
                h = 2 * j + odd
                q, k, v = [_swap_lane_halves(a) if odd else a for a in pair]
                hi, mid, lo = [a.astype(f32) for a in _split3(jnp.broadcast_to(cum[:, h:h + 1], (tm, LANES)))]
                aux_q = jnp.where(lane == AUX, hi, jnp.where(lane == AUX + 1, mid, jnp.where(lane == AUX + 2, lo, ones_q)))
                aux_k = jnp.where(lane == AUX + 3, -hi,
                                  jnp.where(lane == AUX + 4, -mid, jnp.where(lane == AUX + 5, -lo, ones_k)))
                blk = slice(h * LANES, (h + 1) * LANES)
                qa_ref[:, blk] = jnp.where(low, q, aux_q).astype(qa_ref.dtype)
                ka_ref[:, blk] = jnp.where(low, k, aux_k).astype(ka_ref.dtype)
                va_ref[:, blk] = jnp.where(low, v, 1.0).astype(va_ref.dtype)

    row = lambda i: (i, 0)
    return _pcall(
        body, name=name, grid=(nt,),
        in_specs=[pl.BlockSpec((tm, QKV_W), row), pl.BlockSpec((tm, LANES), row)],
        out_specs=[pl.BlockSpec((tm, FOX_PAD), row)] * 3,
        out_shape=[jax.ShapeDtypeStruct((T, FOX_PAD), MXU_DTYPE)] * 3,
        scratch_shapes=[pltpu.VMEM((8, LANES), f32)],
        compiler_params=_params(1),
    )(qkv, fgb)


def _future_keys(tq, tk):
    r = lax.broadcasted_iota(jnp.int32, (tq, tk), 0)
    c = lax.broadcasted_iota(jnp.int32, (tq, tk), 1)
    return c > r


def _causal_steps(nq, key_major):
    if key_major:
        pairs = [(qi, ki) for ki in range(nq) for qi in range(ki, nq)]
    else:
        pairs = [(qi, ki) for qi in range(nq) for ki in range(qi + 1)]
    return (jnp.asarray([p[0] for p in pairs], jnp.int32), jnp.asarray([p[1] for p in pairs], jnp.int32))


def _fox_fwd(qa, ka, va, *, name, tq=512, hps=8):
    T = qa.shape[0]
    tq = min(tq, T)
    tk = tq
    nq = T // tq
    rep = tk // LANES
    qi_tab, ki_tab = _causal_steps(nq, key_major=False)

    def body(qi_ref, ki_ref, qa_ref, ka_ref, va_ref, o_ref, lse_ref, m_sc, acc_sc):
        t = pl.program_id(1)
        qi = qi_ref[t]
        ki = ki_ref[t]

        @pl.when(ki == 0)
        def _():
            m_sc[...] = jnp.full_like(m_sc, NEG_BIG)
            acc_sc[...] = jnp.zeros_like(acc_sc)

        def tile(diagonal):
            for h in range(hps):
                blk = slice(h * LANES, (h + 1) * LANES)
                s = _dot_nt(qa_ref[:, blk], ka_ref[:, blk])
                if diagonal:
                    s = jnp.where(_future_keys(tq, tk), NEG_BIG, s)
                m_prev = m_sc[h]
                m_new = jnp.maximum(m_prev, jnp.max(s, axis=1, keepdims=True))
                p = jnp.exp(s - jnp.tile(m_new, (1, rep)))
                acc_sc[h] = jnp.exp(m_prev - m_new) * acc_sc[h] + _dot(p.astype(MXU_DTYPE), va_ref[:, blk])
                m_sc[h] = m_new

        @pl.when(ki < qi)
        def _():
            tile(False)

        @pl.when(ki == qi)
        def _():
            tile(True)
            low = _low_lanes((tq, LANES))
            outs = []
            for h in range(hps):
                acc = acc_sc[h]
                den = _swap_lane_halves(acc)
                outs.append(acc / den)
                lse_ref[h] = m_sc[h] + jnp.log(jnp.where(low, den, acc))
            for p in range(hps // 2):
                o_ref[:, p * LANES:(p + 1) * LANES] = jnp.where(low, outs[2 * p], _swap_lane_halves(outs[2 * p + 1]))

    pair = hps * LANES
    return _pcall(
        body, name=name,
        grid_spec=pltpu.PrefetchScalarGridSpec(
            num_scalar_prefetch=2, grid=(HEADS // hps, qi_tab.shape[0]),
            in_specs=[
                pl.BlockSpec((tq, pair), lambda j, t, qi_ref, ki_ref: (qi_ref[t], j)),
                pl.BlockSpec((tk, pair), lambda j, t, qi_ref, ki_ref: (ki_ref[t], j)),
                pl.BlockSpec((tk, pair), lambda j, t, qi_ref, ki_ref: (ki_ref[t], j)),
            ],
            out_specs=[pl.BlockSpec((tq, pair // 2), lambda j, t, qi_ref, ki_ref: (qi_ref[t], j)),
                       pl.BlockSpec((hps, tq, LANES), lambda j, t, qi_ref, ki_ref: (j, qi_ref[t], 0))],
            scratch_shapes=[pltpu.VMEM((hps, tq, LANES), f32)] * 2),
        out_shape=[jax.ShapeDtypeStruct((T, FOX_W), f32), jax.ShapeDtypeStruct((HEADS, T, LANES), f32)],
        compiler_params=_params(2),
    )(qi_tab, ki_tab, qa, ka, va)


def _fox_bwd_prep(do, o, *, name, tm=512):
    T = o.shape[0]
    tm = min(tm, T)
    nt = T // tm

    def body(do_ref, o_ref, d_ref, doa_ref):
        low = _low_lanes((tm, LANES))
        for j in range(HEADS // 2):
            do2 = do_ref[:, j * LANES:(j + 1) * LANES].astype(f32)
            prod = do2 * o_ref[:, j * LANES:(j + 1) * LANES]
            for odd in range(2):
                h = 2 * j + odd
                mine = jnp.where(low, _swap_lane_halves(prod) if odd else prod, 0.0)
                d_ref[h] = jnp.broadcast_to(jnp.sum(mine, axis=1, keepdims=True), (tm, LANES))
                doh = jnp.where(low, _swap_lane_halves(do2) if odd else do2, 0.0)
                doa_ref[:, h * LANES:(h + 1) * LANES] = doh.astype(doa_ref.dtype)

    return _pcall(
        body, name=name, grid=(nt,),
        in_specs=[pl.BlockSpec((tm, FOX_W), lambda i: (i, 0)), pl.BlockSpec((tm, FOX_W), lambda i: (i, 0))],
        out_specs=[pl.BlockSpec((HEADS, tm, LANES), lambda i: (0, i, 0)), pl.BlockSpec((tm, FOX_PAD), lambda i: (i, 0))],
        out_shape=[jax.ShapeDtypeStruct((HEADS, T, LANES), f32), jax.ShapeDtypeStruct((T, FOX_PAD), MXU_DTYPE)],
        compiler_params=_params(1),
    )(do, o)


def _fox_bwd(qa, ka, va, doa, lse, drep, *, name, tq=512, hps=4):
    T = qa.shape[0]
    tq = min(tq, T)
    tk = tq
    nq = T // tq
    rep = tk // LANES
    qi_tab, ki_tab = _causal_steps(nq, key_major=True)

    def body(qi_ref, ki_ref, qa_ref, ka_ref, va_ref, doa_ref, lse_ref, d_ref, dqa_ref, dka_ref, dva_ref, dk_sc, dv_sc):
        t = pl.program_id(1)
        qi = qi_ref[t]
        ki = ki_ref[t]
        rows = pl.ds(pl.multiple_of(qi * tq, tq), tq)

        @pl.when(t == 0)
        def _():
            dqa_ref[...] = jnp.zeros_like(dqa_ref)

        @pl.when(qi == ki)
        def _():
            dk_sc[...] = jnp.zeros_like(dk_sc)
            dv_sc[...] = jnp.zeros_like(dv_sc)

        def tile(diagonal):
            for h in range(hps):
                blk = slice(h * LANES, (h + 1) * LANES)
                qh, kh, doh = qa_ref[:, blk], ka_ref[:, blk], doa_ref[:, blk]
                p = jnp.exp(_dot_nt(qh, kh) - jnp.tile(lse_ref[h], (1, rep)))
                if diagonal:
                    p = jnp.where(_future_keys(tq, tk), 0.0, p)
                dp = _dot_nt(doh, va_ref[:, blk])
                ds = (p * (dp - jnp.tile(d_ref[h], (1, rep)))).astype(MXU_DTYPE)
                dv_sc[h] += _dot_tn(p.astype(MXU_DTYPE), doh)
                dk_sc[h] += _dot_tn(ds, qh)
                dqa_ref[rows, blk] += _dot(ds, kh)

        @pl.when(qi > ki)
        def _():
            tile(False)

        @pl.when(qi == ki)
        def _():
            tile(True)

        @pl.when(qi == nq - 1)
        def _():
            for h in range(hps):
                blk = slice(h * LANES, (h + 1) * LANES)
                dka_ref[:, blk] = dk_sc[h]
                dva_ref[:, blk] = dv_sc[h]

    pair = hps * LANES
    q_blk = lambda j, t, qi_ref, ki_ref: (qi_ref[t], j)
    k_blk = lambda j, t, qi_ref, ki_ref: (ki_ref[t], j)
    stat = pl.BlockSpec((hps, tq, LANES), lambda j, t, qi_ref, ki_ref: (j, qi_ref[t], 0))
    return _pcall(
        body, name=name,
        grid_spec=pltpu.PrefetchScalarGridSpec(
            num_scalar_prefetch=2, grid=(HEADS // hps, qi_tab.shape[0]),
            in_specs=[pl.BlockSpec((tq, pair), q_blk), pl.BlockSpec((tk, pair), k_blk), pl.BlockSpec((tk, pair), k_blk),
                      pl.BlockSpec((tq, pair), q_blk), stat, stat],
            out_specs=[pl.BlockSpec((T, pair), lambda j, t, qi_ref, ki_ref: (0, j)),
                       pl.BlockSpec((tk, pair), k_blk), pl.BlockSpec((tk, pair), k_blk)],
            scratch_shapes=[pltpu.VMEM((hps, tk, LANES), f32)] * 2),
        out_shape=[jax.ShapeDtypeStruct((T, FOX_PAD), f32)] * 3,
        compiler_params=_params(2),
    )(qi_tab, ki_tab, qa, ka, va, doa, lse, drep)


def _fox_bwd_post(dqa, dka, fgb, *, name, tm=512):
    T = fgb.shape[0]
    tm = min(tm, T)
    nt = T // tm

    def body(dqa_ref, dka_ref, fg_ref, dfg_ref, dbf_ref, carry):
        i = pl.program_id(0)

        @pl.when(i == 0)
        def _():
            carry[...] = jnp.zeros_like(carry)
            dbf_ref[...] = jnp.zeros_like(dbf_ref)

        lane = lax.broadcasted_iota(jnp.int32, (tm, LANES), 1)
        dc = jnp.zeros((tm, LANES), f32)
        for h in range(HEADS):
            row_sum = dqa_ref[:, h * LANES + AUX:h * LANES + AUX + 1]
            col_sum = dka_ref[:, h * LANES + AUX + 3:h * LANES + AUX + 4]
            dc = jnp.where(lane == h, jnp.broadcast_to(row_sum - col_sum, (tm, LANES)), dc)
        r = lax.broadcasted_iota(jnp.int32, (tm, tm), 0)
        c = lax.broadcasted_iota(jnp.int32, (tm, tm), 1)
        tri = jnp.where(c >= r, 1.0, 0.0).astype(jnp.bfloat16)
        dls = _tri_dot(tri, dc) + carry[0:1, :]
        carry[...] = jnp.broadcast_to(dls[0:1, :], carry.shape)
        dfg = dls * _sigmoid(-fg_ref[...])
        dfg_ref[...] = dfg
        dbf_ref[...] += jnp.sum(dfg, axis=0, keepdims=True)

    rev = lambda i: (nt - 1 - i, 0)
    return _pcall(
        body, name=name, grid=(nt,),
        in_specs=[pl.BlockSpec((tm, FOX_PAD), rev), pl.BlockSpec((tm, FOX_PAD), rev), pl.BlockSpec((tm, LANES), rev)],
        out_specs=[pl.BlockSpec((tm, LANES), rev), pl.BlockSpec((1, LANES), lambda i: (0, 0))],
        out_shape=[jax.ShapeDtypeStruct((T, LANES), f32), jax.ShapeDtypeStruct((1, LANES), f32)],
        scratch_shapes=[pltpu.VMEM((8, LANES), f32)],
        compiler_params=_params(1),
    )(dqa, dka, fgb)


GELU_C = math.sqrt(2.0 / math.pi)
GELU_A = 0.044715


def _gelu(x):
    t = jnp.tanh(GELU_C * (x + GELU_A * x * x * x))
    return 0.5 * x * (1.0 + t), t


def _gelu_grad(x, t):
    return 0.5 * (1.0 + t) + 0.5 * x * (1.0 - t * t) * GELU_C * (1.0 + 3.0 * GELU_A * x * x)


EXPM1_SERIES_BELOW = 0.25


def _expm1(x, e):
    series = x * (1.0 + x * (1 / 2 + x * (1 / 6 + x * (1 / 24 + x * (1 / 120 + x * (1 / 720))))))
    return jnp.where(x > -EXPM1_SERIES_BELOW, series, e - 1.0)


def _lru_gates(u, wab_ref, bab_ref, lam_ref):
    pre = _dot(u.astype(MXU_DTYPE), wab_ref[...]) + bab_ref[...]
    r = _sigmoid(pre[:, :LRU_W])
    gi = _sigmoid(pre[:, LRU_W:])
    lam = lam_ref[...]
    sp = jnp.maximum(-lam, 0.0) + jnp.log(1.0 + jnp.exp(-jnp.abs(lam)))
    log_a = -LRU_C * r * sp
    a = jnp.exp(log_a)
    s = jnp.sqrt(-_expm1(2.0 * log_a, a * a))
    return r, gi, sp, a, s


def _lru_fwd(lxg, conv_w, conv_b, wab, bab, lam, *, name, tc=512):
    T = lxg.shape[0]
    tc = min(tc, T)
    nc = T // tc

    def body(lx_ref, lg_ref, cw_ref, cb_ref, wab_ref, bab_ref, lam_ref,
             out_ref, u_ref, hs_ref, ext, a_sc, b_sc, h_sc):
        i = pl.program_id(0)

        @pl.when(i == 0)
        def _():
            ext[0:8, :] = jnp.zeros((8, LRU_W), f32)
            h_sc[...] = jnp.zeros_like(h_sc)

        ext[8:, :] = lx_ref[...]
        u = cb_ref[...] + cw_ref[0:1, :] * ext[pl.ds(5, tc), :]
        for k in range(1, CONV_K):
            u = u + cw_ref[k:k + 1, :] * ext[pl.ds(5 + k, tc), :]
        ext[0:8, :] = ext[tc:tc + 8, :]
        u_ref[...] = u
        r, gi, sp, a, s = _lru_gates(u, wab_ref, bab_ref, lam_ref)
        a_sc[...] = a
        b_sc[...] = s * (gi * u)

        def step(t, h):
            h = a_sc[pl.ds(t, 1), :] * h + b_sc[pl.ds(t, 1), :]
            hs_ref[pl.ds(t, 1), :] = h
            return h

        h = lax.fori_loop(0, tc, step, h_sc[0:1, :], unroll=8)
        h_sc[...] = jnp.broadcast_to(h, h_sc.shape)
        gel, _ = _gelu(lg_ref[...])
        out_ref[...] = gel * hs_ref[...]

    row = lambda i: (i, 0)
    const = lambda i: (0, 0)
    return _pcall(
        body, name=name, grid=(nc,),
        in_specs=[pl.BlockSpec((tc, LRU_W), row), pl.BlockSpec((tc, LRU_W), lambda i: (i, 1)),
                  pl.BlockSpec((CONV_K, LRU_W), const), pl.BlockSpec((1, LRU_W), const),
                  pl.BlockSpec((LRU_W, 2 * LRU_W), const), pl.BlockSpec((1, 2 * LRU_W), const),
                  pl.BlockSpec((1, LRU_W), const)],
        out_specs=[pl.BlockSpec((tc, LRU_W), row)] * 3,
        out_shape=[jax.ShapeDtypeStruct((T, LRU_W), f32)] * 3,
        scratch_shapes=[pltpu.VMEM((tc + 8, LRU_W), f32), pltpu.VMEM((tc, LRU_W), f32),
                        pltpu.VMEM((tc, LRU_W), f32), pltpu.VMEM((8, LRU_W), f32)],
        compiler_params=_params(1),
    )(lxg, lxg, conv_w, conv_b, wab, bab, lam)


def _lru_bwd(dlru, lxg, u, hs, conv_w, wab, bab, lam, *, name, tc=512):
    T = lxg.shape[0]
    tc = min(tc, T)
    nc = T // tc
    bp = tc // 8

    def body(dl_ref, lx_ref, lxp_ref, lg_ref, u_ref, hs_ref, hsp_ref, cw_ref, wab_ref, bab_ref, lam_ref,
             dlxg_ref, dwab_ref, dbab_ref, dcw_ref, dcb_ref, dlam_ref,
             dh_sc, a_sc, ext, du_ext, carry):
        i = pl.program_id(0)
        first_chunk = i == nc - 1

        @pl.when(i == 0)
        def _():
            dwab_ref[...] = jnp.zeros_like(dwab_ref)
            dbab_ref[...] = jnp.zeros_like(dbab_ref)
            dcw_ref[...] = jnp.zeros_like(dcw_ref)
            dcb_ref[...] = jnp.zeros_like(dcb_ref)
            dlam_ref[...] = jnp.zeros_like(dlam_ref)
            carry[...] = jnp.zeros_like(carry)
            du_ext[tc:tc + 8, :] = jnp.zeros((8, LRU_W), f32)

        lg = lg_ref[...]
        gel, th = _gelu(lg)
        dl = dl_ref[...]
        hs = hs_ref[...]
        dlg = dl * hs * _gelu_grad(lg, th)
        u = u_ref[...]
        r, gi, sp, a, s = _lru_gates(u, wab_ref, bab_ref, lam_ref)
        a_sc[...] = a
        dh_sc[...] = dl * gel

        def step(k, c):
            t = tc - 1 - k
            dh = dh_sc[pl.ds(t, 1), :] + c
            dh_sc[pl.ds(t, 1), :] = dh
            return a_sc[pl.ds(t, 1), :] * dh

        c = lax.fori_loop(0, tc, step, carry[0:1, :], unroll=8)
        carry[...] = jnp.broadcast_to(c, carry.shape)

        ext[0:8, :] = jnp.where(first_chunk, 0.0, hsp_ref[...])
        ext[8:, :] = hs
        hprev = ext[pl.ds(7, tc), :]
        dh = dh_sc[...]
        da = dh * hprev
        giu = gi * u
        dla = da * a - (dh * giu) * (a * a / s)
        dgi = dh * s * u
        du = dh * s * gi
        dr = dla * (-LRU_C * sp)
        dlam_ref[...] += jnp.sum(dla * (-LRU_C * r), axis=0, keepdims=True) * (-_sigmoid(-lam_ref[...]))
        dpre = jnp.concatenate([dr * r * (1.0 - r), dgi * gi * (1.0 - gi)], axis=1)
        dpre_b = dpre.astype(MXU_DTYPE)
        du = du + _dot_nt(dpre_b, wab_ref[...])
        dwab_ref[...] += _dot_tn(u.astype(MXU_DTYPE), dpre_b)
        dbab_ref[...] += jnp.sum(dpre, axis=0, keepdims=True)
        dcb_ref[...] += jnp.sum(du, axis=0, keepdims=True)

        du_ext[0:tc, :] = du
        dlx = cw_ref[0:1, :] * du_ext[pl.ds(3, tc), :]
        for k in range(1, CONV_K):
            dlx = dlx + cw_ref[k:k + 1, :] * du_ext[pl.ds(3 - k, tc), :]
        du_ext[tc:tc + 8, :] = du_ext[0:8, :]
        ext[0:8, :] = jnp.where(first_chunk, 0.0, lxp_ref[...])
        ext[8:, :] = lx_ref[...]
        for k in range(CONV_K):
            dcw_ref[k:k + 1, :] += jnp.sum(du * ext[pl.ds(5 + k, tc), :], axis=0, keepdims=True)
        dlxg_ref[:, :LRU_W] = dlx.astype(dlxg_ref.dtype)
        dlxg_ref[:, LRU_W:] = dlg.astype(dlxg_ref.dtype)

    rev = lambda i: (nc - 1 - i, 0)
    prev8 = lambda i: (jnp.maximum((nc - 1 - i) * bp - 1, 0), 0)
    const = lambda i: (0, 0)
    return _pcall(
        body, name=name, grid=(nc,),
        in_specs=[
            pl.BlockSpec((tc, LRU_W), rev),
            pl.BlockSpec((tc, LRU_W), rev),
            pl.BlockSpec((8, LRU_W), prev8),
            pl.BlockSpec((tc, LRU_W), lambda i: (nc - 1 - i, 1)),
            pl.BlockSpec((tc, LRU_W), rev),
            pl.BlockSpec((tc, LRU_W), rev),
            pl.BlockSpec((8, LRU_W), prev8),
            pl.BlockSpec((CONV_K, LRU_W), const),
            pl.BlockSpec((LRU_W, 2 * LRU_W), const),
            pl.BlockSpec((1, 2 * LRU_W), const),
            pl.BlockSpec((1, LRU_W), const),
        ],
        out_specs=[
            pl.BlockSpec((tc, 2 * LRU_W), rev),
            pl.BlockSpec((LRU_W, 2 * LRU_W), const),
            pl.BlockSpec((1, 2 * LRU_W), const),
            pl.BlockSpec((8, LRU_W), const),
            pl.BlockSpec((1, LRU_W), const),
            pl.BlockSpec((1, LRU_W), const),
        ],
        out_shape=[
            jax.ShapeDtypeStruct((T, 2 * LRU_W), MXU_DTYPE),
            jax.ShapeDtypeStruct((LRU_W, 2 * LRU_W), f32),
            jax.ShapeDtypeStruct((1, 2 * LRU_W), f32),
            jax.ShapeDtypeStruct((8, LRU_W), f32),
            jax.ShapeDtypeStruct((1, LRU_W), f32),
            jax.ShapeDtypeStruct((1, LRU_W), f32),
        ],
        scratch_shapes=[pltpu.VMEM((tc, LRU_W), f32), pltpu.VMEM((tc, LRU_W), f32),
                        pltpu.VMEM((tc + 8, LRU_W), f32), pltpu.VMEM((tc + 8, LRU_W), f32),
                        pltpu.VMEM((8, LRU_W), f32)],
        compiler_params=_params(1),
    )(dlru, lxg, lxg, lxg, u, hs, hs, conv_w, wab, bab, lam)


def _mix_out(fox, lru, wo, xhat1, g1, b1, g2, b2, *, name, tm=512):
    T = fox.shape[0]
    tm = min(tm, T)
    nt = T // tm

    def body(fox_ref, lru_ref, wo_ref, xh_ref, g1_ref, b1_ref, g2_ref, b2_ref, xhat_ref, xn_ref, rstd_ref):
        mix = _dot(fox_ref[...].astype(MXU_DTYPE), wo_ref[:FOX_W, :])
        mix = mix + _dot(lru_ref[...].astype(MXU_DTYPE), wo_ref[FOX_W:, :])
        x1 = xh_ref[...] * g1_ref[...] + b1_ref[...]
        xhat, rstd = _layer_norm_stats(DN_ALPHA * x1 + mix)
        xhat_ref[...] = xhat
        xn_ref[...] = xhat * g2_ref[...] + b2_ref[...]
        rstd_ref[...] = jnp.broadcast_to(rstd, rstd_ref.shape)

    row = lambda i: (i, 0)
    const = lambda i: (0, 0)
    vec = pl.BlockSpec((1, D_MODEL), const)
    return _pcall(
        body, name=name, grid=(nt,),
        in_specs=[pl.BlockSpec((tm, FOX_W), row), pl.BlockSpec((tm, LRU_W), row),
                  pl.BlockSpec((D_MODEL, D_MODEL), const), pl.BlockSpec((tm, D_MODEL), row), vec, vec, vec, vec],
        out_specs=[pl.BlockSpec((tm, D_MODEL), row), pl.BlockSpec((tm, D_MODEL), row),
                   pl.BlockSpec((tm, LANES), row)],
        out_shape=[jax.ShapeDtypeStruct((T, D_MODEL), f32), jax.ShapeDtypeStruct((T, D_MODEL), f32),
                   jax.ShapeDtypeStruct((T, LANES), f32)],
        compiler_params=_params(1),
    )(fox, lru, wo, xhat1, g1, b1, g2, b2)


def _mix_out_bwd(dyp, fox, lru, wo, *, name, tm=512):
    T = fox.shape[0]
    tm = min(tm, T)
    nt = T // tm

    def body(dyp_ref, fox_ref, lru_ref, wo_ref, dfox_ref, dlru_ref, dwo_ref):
        i = pl.program_id(0)

        @pl.when(i == 0)
        def _():
            dwo_ref[...] = jnp.zeros_like(dwo_ref)

        dmix = dyp_ref[...].astype(MXU_DTYPE)
        dcat = _dot_nt(dmix, wo_ref[...])
        dfox_ref[...] = dcat[:, :FOX_W].astype(dfox_ref.dtype)
        dlru_ref[...] = dcat[:, FOX_W:]
        dwo_ref[:FOX_W, :] += _dot_tn(fox_ref[...].astype(MXU_DTYPE), dmix)
        dwo_ref[FOX_W:, :] += _dot_tn(lru_ref[...].astype(MXU_DTYPE), dmix)

    row = lambda i: (i, 0)
    const = lambda i: (0, 0)
    return _pcall(
        body, name=name, grid=(nt,),
        in_specs=[pl.BlockSpec((tm, D_MODEL), row), pl.BlockSpec((tm, FOX_W), row), pl.BlockSpec((tm, LRU_W), row),
                  pl.BlockSpec((D_MODEL, D_MODEL), const)],
        out_specs=[pl.BlockSpec((tm, FOX_W), row), pl.BlockSpec((tm, LRU_W), row),
                   pl.BlockSpec((D_MODEL, D_MODEL), const)],
        out_shape=[jax.ShapeDtypeStruct((T, FOX_W), MXU_DTYPE), jax.ShapeDtypeStruct((T, LRU_W), f32),
                   jax.ShapeDtypeStruct((D_MODEL, D_MODEL), f32)],
        compiler_params=_params(1),
    )(dyp, fox, lru, wo)


def make_wp(w_in):
    scale = jnp.concatenate([jnp.full((FOX_W,), 1.0 / math.sqrt(HEAD_DIM), w_in.dtype),
                             jnp.ones((IN_COLS - FOX_W,), w_in.dtype)])
    return jnp.pad(w_in * scale[None, :], ((0, 0), (0, Z_PAD - IN_COLS)))


def _block_diag(w):
    eye = jnp.eye(HEADS, dtype=w.dtype)
    return jnp.einsum("hij,hg->higj", w, eye).reshape(LRU_W, LRU_W)


def _block_diag_extract(m):
    m4 = m.reshape(HEADS, HEAD_DIM, HEADS, HEAD_DIM)
    return jnp.stack([m4[h, :, h, :] for h in range(HEADS)])


class _NoOverlap:
    def start_token(self):
        return None

    def late_weights(self, w, after):
        return dict(f1d=w["f1d"], wp=w["wp"], wo=w["wo"])

    def after_attention(self, after):
        return None

    def ffn2_weights(self, w, after):
        return w["f2g"], w["f2u"], w["f2d"]

    def ffn2_grads(self, grads):
        return None

    def ffn1_grads(self, grads):
        return None

    def mixer_grads(self, dwp, dwo, small, loss):
        return None

    def before_ffn1_bwd(self, after):
        return None


def _tied(a, token):
    return a if token is None else a + token[0, 0]


def _local_step(x, target, w, hooks=None):
    hooks = hooks or _NoOverlap()
    bfp = w["bfp"]
    wab = jnp.concatenate([_block_diag(w["rg_wa"]), _block_diag(w["rg_wx"])], axis=1).astype(MXU_DTYPE)
    bab = jnp.concatenate([w["rg_ba"].reshape(1, LRU_W), w["rg_bx"].reshape(1, LRU_W)], axis=1)

    xb0, g1a, u1a, h1a = _ffn_up(x, w["f1g"], w["f1u"], hooks.start_token(), name="ffn1_up")
    late = hooks.late_weights(w, [h1a])
    f1d, wp, wo = late["f1d"], late["wp"], late["wo"]
    xhat1, xn1, rstd1 = _ffn_down_ln(x, h1a, f1d, w["ln1_g"], w["ln1_b"], name="ffn1_down")
    qkv, lxg, fgb = _proj_in(xn1, wp, bfp, name="proj_in")
    qa, ka, va = _fox_prep(qkv, fgb, name="fox_prep")
    fox, lse = _fox_fwd(qa, ka, va, name="fox_fwd")
    token = hooks.after_attention([lse])
    lru, uconv, hs = _lru_fwd(lxg, w["conv_w"], _tied(w["conv_b"], token), wab, bab, w["lam"], name="lru_fwd")
    xhat2, x2, rstd2 = _mix_out(fox, lru, wo, xhat1, w["ln1_g"], w["ln1_b"], w["ln2_g"], w["ln2_b"], name="mix_out")
    f2g, f2u, f2d = hooks.ffn2_weights(w, [rstd2])
    xb2, g2a, u2a, xhat3, _, rstd3 = _ffn_fwd(x2, f2g, f2u, f2d, w["ln3_g"], w["ln3_b"], name="ffn2_fwd")

    dy3p, dln3g, dln3b, loss = _loss_ln_bwd(xhat3, rstd3, w["ln3_g"], w["ln3_b"], target, name="loss_ln3_bwd")
    dx2, df2g, df2u, df2d = _ffn_bwd(dy3p, xb2, g2a, u2a, f2g, f2u, f2d, name="ffn2_bwd")
    token = hooks.ffn2_grads([df2g, df2u, df2d])
    dy2p, dln2g, dln2b = _ln_bwd(dx2, xhat2, rstd2, _tied(w["ln2_g"], token), name="ln2_bwd")
    dfox, dlru, dwo = _mix_out_bwd(dy2p, fox, lru, wo, name="mix_out_bwd")
    dlxg, dwab, dbab, dcw, dcb, dlam = _lru_bwd(dlru, lxg, uconv, hs, w["conv_w"], wab, bab, w["lam"], name="lru_bwd")
    drep, doa = _fox_bwd_prep(dfox, fox, name="fox_bwd_prep")
    dqa, dka, dva = _fox_bwd(qa, ka, va, doa, lse, drep, name="fox_bwd")
    dfg, dbf = _fox_bwd_post(dqa, dka, fgb, name="fox_bwd_post")
    dx1, dwp = _proj_in_bwd(dqa, dka, dva, dlxg, dfg, xn1, dy2p, wp, name="proj_in_bwd")
    dy1p, dln1g, dln1b = _ln_bwd(dx1, xhat1, rstd1, w["ln1_g"], name="ln1_bwd")
    small = dict(
        ln1_g=dln1g, ln1_b=dln1b, ln2_g=dln2g, ln2_b=dln2b, ln3_g=dln3g, ln3_b=dln3b,
        b_forget=dbf[:, :HEADS], conv_w=dcw[:CONV_K], conv_b=dcb,
        rg_wa=_block_diag_extract(dwab[:, :LRU_W]), rg_wx=_block_diag_extract(dwab[:, LRU_W:]),
        rg_ba=dbab[:, :LRU_W].reshape(HEADS, HEAD_DIM), rg_bx=dbab[:, LRU_W:].reshape(HEADS, HEAD_DIM),
        lru_lambda=dlam,
    )
    hooks.before_ffn1_bwd([dln1b])
    token = hooks.mixer_grads(dwp, dwo, small, loss)
    dx_a, *grads_a = _ffn_bwd(dy1p, xb0, g1a, u1a, w["f1g"], w["f1u"], f1d, token, name="ffn1_bwd_a", part=0)
    token = hooks.ffn1_grads(grads_a)
    dx, *grads_b = _ffn_bwd(dy1p, xb0, g1a, u1a, w["f1g"], w["f1u"], f1d, token, name="ffn1_bwd_b", part=1,
                            dx_init=dx_a)

    grads = dict(f1=(grads_a, grads_b), f2g=df2g, f2u=df2u, f2d=df2d, wp=dwp, wo=dwo, **small)
    return loss, dx, grads


MESH = pl.DeviceIdType.MESH
HBM_SPEC = pl.BlockSpec(memory_space=pl.ANY)
VMEM_SPEC = pl.BlockSpec(memory_space=pltpu.VMEM)


def _position():
    return lax.axis_index("x"), lax.axis_index("y"), lax.axis_index("c")


def _other_chips(x, y):
    return [(1 - x, y), (x, 1 - y), (1 - x, 1 - y)]


def _all_gather_bf16(shards, *, name):
    n = len(shards)

    def body(*refs):
        ins, outs, stages = refs[:n], refs[n:2 * n], refs[2 * n:3 * n]
        send_sems, recv_sems, local_sems = refs[3 * n:]
        x, y, c = _position()
        me, sibling = (x, y, c), (x, y, 1 - c)
        chips = _other_chips(x, y)

        def rows(k, px, py, pc):
            r = shards[k].shape[0]
            m = r // 2
            return outs[k].at[pl.ds(pl.multiple_of((2 * px + py) * r + pc * m, 16), m), :]

        def copy(k, idx, block, to, src=None):
            return pltpu.make_async_remote_copy(
                src_ref=rows(k, *block) if src is None else src, dst_ref=rows(k, *block),
                send_sem=send_sems.at[7 * k + idx], recv_sem=recv_sems.at[7 * k + idx],
                device_id=to, device_id_type=MESH)

        started = []
        mine = []
        for k in range(n):
            m = shards[k].shape[0] // 2
            stages[k][...] = ins[k][pl.ds(pl.multiple_of(c * m, 16), m), :].astype(stages[k].dtype)
            cp = pltpu.make_async_copy(stages[k], rows(k, *me), local_sems.at[k])
            cp.start()
            mine.append(cp)
            first = [copy(k, 0, me, sibling, src=stages[k])]
            first += [copy(k, 1 + j, me, (*chip, c), src=stages[k]) for j, chip in enumerate(chips)]
            for cp in first:
                cp.start()
            started += first
        for k in range(n):
            for j, chip in enumerate(chips):
                copy(k, 1 + j, (*chip, c), me).wait_recv()
                fwd = copy(k, 4 + j, (*chip, c), sibling)
                fwd.start()
                started.append(fwd)
        for k in range(n):
            copy(k, 0, sibling, me).wait_recv()
            for j, chip in enumerate(chips):
                copy(k, 4 + j, (*chip, 1 - c), me).wait_recv()
        for cp in started:
            cp.wait_send()
        for cp in mine:
            cp.wait()

    return _pcall(
        body, name=name,
        in_specs=[VMEM_SPEC] * n, out_specs=[HBM_SPEC] * n,
        out_shape=[jax.ShapeDtypeStruct((N_SHARD * s.shape[0], s.shape[1]), MXU_DTYPE) for s in shards],
        scratch_shapes=[pltpu.VMEM((s.shape[0] // 2, s.shape[1]), MXU_DTYPE) for s in shards]
        + [pltpu.SemaphoreType.DMA((7 * n,)), pltpu.SemaphoreType.DMA((7 * n,)), pltpu.SemaphoreType.DMA((n,))],
        compiler_params=pltpu.CompilerParams(vmem_limit_bytes=VMEM_LIMIT),
    )(*shards)


def _swap_halves(gs, *, name):
    n = len(gs)

    def body(*refs):
        ins, outs = refs[:n], refs[n:2 * n]
        send_sems, recv_sems = refs[2 * n:]
        x, y, c = _position()
        cps = []
        for k in range(n):
            m = gs[k].shape[1] // 2
            src = ins[k].at[:, pl.ds(pl.multiple_of((1 - c) * m, 16), m), :]
            cp = pltpu.make_async_remote_copy(src_ref=src, dst_ref=outs[k], send_sem=send_sems.at[k],
                                              recv_sem=recv_sems.at[k], device_id=(x, y, 1 - c), device_id_type=MESH)
            cp.start()
            cps.append(cp)
        for cp in cps:
            cp.wait()

    return _pcall(
        body, name=name, in_specs=[HBM_SPEC] * n, out_specs=[HBM_SPEC] * n,
        out_shape=[jax.ShapeDtypeStruct((g.shape[0], g.shape[1] // 2, g.shape[2]), g.dtype) for g in gs],
        scratch_shapes=[pltpu.SemaphoreType.DMA((n,)), pltpu.SemaphoreType.DMA((n,))],
    )(*gs)


def _add_halves(gs, recvs, *, name, tm=256):
    n = len(gs)
    _, r, cdim = gs[0].shape
    m = r // 2
    tm = min(tm, m)
    nb = m // tm
    c_idx = lax.axis_index("c").astype(jnp.int32).reshape(1)

    def body(c_ref, *refs):
        for k in range(n):
            refs[2 * n + k][...] = (refs[k][...].astype(f32) + refs[n + k][...].astype(f32)).astype(refs[2 * n + k].dtype)

    mine = pl.BlockSpec((None, tm, cdim), lambda j, i, c_ref: (j, c_ref[0] * nb + i, 0))
    half = pl.BlockSpec((None, tm, cdim), lambda j, i, c_ref: (j, i, 0))
    return _pcall(
        body, name=name,
        grid_spec=pltpu.PrefetchScalarGridSpec(
            num_scalar_prefetch=1, grid=(N_SHARD, nb),
            in_specs=[mine] * n + [half] * n, out_specs=[half] * n),
        out_shape=[jax.ShapeDtypeStruct((N_SHARD, m, cdim), g.dtype) for g in gs],
        compiler_params=_params(2),
    )(c_idx, *gs, *recvs)


def _scatter_partials(ps, *, name):
    n = len(ps)

    def body(*refs):
        ins, outs = refs[:n], refs[n:2 * n]
        send_sems, recv_sems = refs[2 * n:]
        x, y, c = _position()
        me_chip = 2 * x + y
        cps = []
        for k in range(n):
            for j, (px, py) in enumerate(_other_chips(x, y)):
                cp = pltpu.make_async_remote_copy(
                    src_ref=ins[k].at[2 * px + py], dst_ref=outs[k].at[me_chip],
                    send_sem=send_sems.at[3 * k + j], recv_sem=recv_sems.at[3 * k + j],
                    device_id=(px, py, c), device_id_type=MESH)
                cp.start()
                cps.append(cp)
        for cp in cps:
            cp.wait()

    return _pcall(
        body, name=name, in_specs=[HBM_SPEC] * n, out_specs=[HBM_SPEC] * n,
        out_shape=[jax.ShapeDtypeStruct(p.shape, p.dtype) for p in ps],
        scratch_shapes=[pltpu.SemaphoreType.DMA((3 * n,)), pltpu.SemaphoreType.DMA((3 * n,))],
    )(*ps)


def _sum_slabs(ps, qs, *, name, tm=128):
    n = len(qs)
    _, m, cdim = qs[0].shape
    tm = min(tm, m)
    nb = m // tm
    assert m % tm == 0, (m, tm)
    where = jnp.stack([2 * lax.axis_index("x") + lax.axis_index("y"), lax.axis_index("c")]).astype(jnp.int32)

    def body(w_ref, *refs):
        for k in range(n):
            own, q1, q2, q3 = (refs[4 * k + t][...].astype(f32) for t in range(4))
            refs[4 * n + k][...] = ((own + q1) + q2) + q3

    def slab(flip):
        return pl.BlockSpec((None, tm, cdim), lambda i, w_ref: (jnp.bitwise_xor(w_ref[0], flip), i, 0))

    operands = []
    for p, q in zip(ps, qs):
        operands += [p, q, q, q]
    return _pcall(
        body, name=name,
        grid_spec=pltpu.PrefetchScalarGridSpec(
            num_scalar_prefetch=1, grid=(nb,),
            in_specs=[slab(0), slab(2), slab(1), slab(3)] * n,
            out_specs=[pl.BlockSpec((tm, cdim), lambda i, w_ref: (w_ref[1] * nb + i, 0))] * n),
        out_shape=[jax.ShapeDtypeStruct((2 * m, cdim), f32) for _ in qs],
        compiler_params=_params(1),
    )(where, *operands)


def _join_halves(fs, *, name):
    n = len(fs)

    def body(*refs):
        outs = refs[n:2 * n]
        send_sems, recv_sems = refs[2 * n:]
        x, y, c = _position()
        cps = []
        for k in range(n):
            m = fs[k].shape[0] // 2
            half = outs[k].at[pl.ds(pl.multiple_of(c * m, 8), m), :]
            cp = pltpu.make_async_remote_copy(src_ref=half, dst_ref=half, send_sem=send_sems.at[k],
                                              recv_sem=recv_sems.at[k], device_id=(x, y, 1 - c), device_id_type=MESH)
            cp.start()
            cps.append(cp)
        for cp in cps:
            cp.wait()

    return _pcall(
        body, name=name, in_specs=[HBM_SPEC] * n, out_specs=[HBM_SPEC] * n,
        out_shape=[jax.ShapeDtypeStruct(f.shape, f.dtype) for f in fs],
        input_output_aliases={k: k for k in range(n)},
        scratch_shapes=[pltpu.SemaphoreType.DMA((n,)), pltpu.SemaphoreType.DMA((n,))],
    )(*fs)


def _all_reduce_small(v, after=None, *, name):
    r = v.shape[0]
    extra = [] if after is None else [after]

    def body(v_ref, *refs):
        out_ref, buf, send_sems, recv_sems, local_sem = refs[len(extra):]
        x, y, c = _position()
        me, sibling = (x, y, c), (x, y, 1 - c)
        chips = _other_chips(x, y)

        def rows(px, py, pc):
            return buf.at[pl.ds(pl.multiple_of((4 * px + 2 * py + pc) * r, 8), r), :]

        def copy(k, block, to, src=None):
            return pltpu.make_async_remote_copy(
                src_ref=rows(*block) if src is None else src, dst_ref=rows(*block),
                send_sem=send_sems.at[k], recv_sem=recv_sems.at[k], device_id=to, device_id_type=MESH)

        mine = pltpu.make_async_copy(v_ref, rows(*me), local_sem)
        mine.start()
        first = [copy(0, me, sibling, src=v_ref)]
        first += [copy(1 + j, me, (*chip, c), src=v_ref) for j, chip in enumerate(chips)]
        for cp in first:
            cp.start()
        passed = [copy(4 + j, (*chip, c), sibling) for j, chip in enumerate(chips)]
        for j, chip in enumerate(chips):
            copy(1 + j, (*chip, c), me).wait_recv()
            passed[j].start()
        copy(0, sibling, me).wait_recv()
        for j, chip in enumerate(chips):
            copy(4 + j, (*chip, 1 - c), me).wait_recv()
        for cp in first + passed:
            cp.wait_send()
        mine.wait()
        acc = buf[0:r, :]
        for d in range(1, N_DEV):
            acc = acc + buf[d * r:(d + 1) * r, :]
        out_ref[...] = acc

    return _pcall(
        body, name=name, in_specs=[VMEM_SPEC] + [HBM_SPEC] * len(extra), out_specs=VMEM_SPEC,
        out_shape=jax.ShapeDtypeStruct((r, LANES), f32),
        scratch_shapes=[pltpu.VMEM((N_DEV * r, LANES), f32), pltpu.SemaphoreType.DMA((7,)),
                        pltpu.SemaphoreType.DMA((7,)), pltpu.SemaphoreType.DMA],
    )(v, *extra)


SEM_SPEC = pl.BlockSpec(memory_space=pltpu.SEMAPHORE)
HBM_ONLY = pl.BlockSpec(memory_space=pltpu.HBM)
EFFECT = pltpu.SideEffectType.DATAFLOW_SIDE_EFFECTING


def _sends(copies):
    return copies[0] if isinstance(copies, tuple) else copies


def _arrivals(copies):
    return copies[1] if isinstance(copies, tuple) else copies


def _split_start(bufs, copies_fn, n_sems, *, name):
    n = len(bufs)

    def body(*refs):
        send_sems, recv_sems = refs[n], refs[n + 1]
        thru = refs[n + 2:2 * n + 2]
        token = refs[2 * n + 2]
        for cp in _sends(copies_fn(thru, send_sems, recv_sems)):
            cp.start()
        token[...] = jnp.zeros_like(token)

    outs = _pcall(
        body, name=name,
        out_shape=(pltpu.SemaphoreType.DMA((n_sems,)), pltpu.SemaphoreType.DMA((n_sems,)),
                   *[pltpu.HBM(b.shape, b.dtype) for b in bufs], jax.ShapeDtypeStruct((8, LANES), f32)),
        in_specs=[HBM_ONLY] * n,
        out_specs=(SEM_SPEC, SEM_SPEC, *[HBM_ONLY] * n, VMEM_SPEC),
        input_output_aliases={k: 2 + k for k in range(n)},
        compiler_params=pltpu.CompilerParams(has_side_effects=EFFECT),
    )(*[pltpu.with_memory_space_constraint(b, pltpu.HBM) for b in bufs])
    return outs[0], outs[1], list(outs[2:2 + n]), outs[2 + n]


def _split_wait(thru, send_sems, recv_sems, after, copies_fn, *, name):
    n = len(thru)

    def body(*refs):
        copies = copies_fn(refs[:n], refs[n], refs[n + 1])
        for cp in _sends(copies):
            cp.wait_send()
        for cp in _arrivals(copies):
            cp.wait_recv()

    return list(_pcall(
        body, name=name,
        out_shape=tuple(pltpu.HBM(b.shape, b.dtype) for b in thru),
        in_specs=[HBM_ONLY] * n + [SEM_SPEC, SEM_SPEC] + [HBM_SPEC] * len(after),
        out_specs=tuple([HBM_ONLY] * n),
        input_output_aliases={k: k for k in range(n)},
        compiler_params=pltpu.CompilerParams(has_side_effects=EFFECT),
    )(*thru, send_sems, recv_sems, *after))


def _scatter_copies(n):
    def copies(bufs, send_sems, recv_sems):
        x, y, c = _position()
        me_chip = 2 * x + y
        cps = []
        for k in range(n):
            for j, (px, py) in enumerate(_other_chips(x, y)):
                cps.append(pltpu.make_async_remote_copy(
                    src_ref=bufs[k].at[2 * px + py], dst_ref=bufs[n + k].at[me_chip],
                    send_sem=send_sems.at[3 * k + j], recv_sem=recv_sems.at[3 * k + j],
                    device_id=(px, py, c), device_id_type=MESH))
        return cps
    return copies


N_PEERS = N_DEV - 1


def _direct_copies(n):
    def copies(bufs, send_sems, recv_sems):
        x, y, c = _position()
        me_chip = 2 * x + y
        sends, arrivals = [], []
        for k in range(n):
            m = bufs[k].shape[1] // 2
            land = bufs[n + k]

            def rows(slab, half, k=k, m=m):
                start = half * m if isinstance(half, int) else pl.multiple_of(half * m, 16)
                return bufs[k].at[slab, pl.ds(start, m), :]

            def copy(src, slot, send_idx, recv_idx, to, k=k, land=land):
                return pltpu.make_async_remote_copy(
                    src_ref=src, dst_ref=land.at[slot], send_sem=send_sems.at[N_PEERS * k + send_idx],
                    recv_sem=recv_sems.at[N_PEERS * k + recv_idx], device_id=to, device_id_type=MESH)

            sends.append(copy(rows(me_chip, 1 - c), 0, 0, 0, (x, y, 1 - c)))
            arrivals.append(copy(rows(me_chip, c), 0, 0, 0, (x, y, 1 - c)))
            for t, (px, py) in enumerate(_other_chips(x, y)):
                for core in range(2):
                    sends.append(copy(rows(2 * px + py, core), 1 + 2 * t + c, 1 + 2 * t + core, 1 + 2 * t + c,
                                      (px, py, core)))
                    arrivals.append(copy(rows(me_chip, c), 1 + 2 * t + core, 1 + 2 * t + core, 1 + 2 * t + core,
                                         (px, py, core)))
        return sends, arrivals
    return copies


def _sum_direct(gs, lands, *, name, tm=128):
    n = len(gs)
    _, m, cdim = lands[0].shape
    tm = min(tm, m)
    nb = m // tm
    assert m % tm == 0, (m, tm)
    where = jnp.stack([2 * lax.axis_index("x") + lax.axis_index("y"), lax.axis_index("c")]).astype(jnp.int32)

    def body(w_ref, *refs):
        for k in range(n):
            acc = refs[2 * k][...].astype(f32)
            for slot in range(N_PEERS):
                acc = acc + refs[2 * k + 1][slot].astype(f32)
            refs[2 * n + k][...] = acc

    own = pl.BlockSpec((None, tm, cdim), lambda i, w_ref: (w_ref[0], w_ref[1] * nb + i, 0))
    landed = pl.BlockSpec((N_PEERS, tm, cdim), lambda i, w_ref: (0, i, 0))
    operands = []
    for g, land in zip(gs, lands):
        operands += [g, land]
    return _pcall(
        body, name=name,
        grid_spec=pltpu.PrefetchScalarGridSpec(
            num_scalar_prefetch=1, grid=(nb,), in_specs=[own, landed] * n,
            out_specs=[pl.BlockSpec((tm, cdim), lambda i, w_ref: (w_ref[1] * nb + i, 0))] * n),
        out_shape=[jax.ShapeDtypeStruct((2 * m, cdim), f32) for _ in gs],
        compiler_params=_params(1),
    )(where, *operands)


def _broadcast_copies(bufs, send_sems, recv_sems):
    v, land = bufs
    x, y, c = _position()

    def copy(slot, send_idx, recv_idx, to):
        return pltpu.make_async_remote_copy(src_ref=v, dst_ref=land.at[slot], send_sem=send_sems.at[send_idx],
                                            recv_sem=recv_sems.at[recv_idx], device_id=to, device_id_type=MESH)

    sends = [copy(0, 0, 0, (x, y, 1 - c))]
    arrivals = [copy(0, 0, 0, (x, y, 1 - c))]
    for t, (px, py) in enumerate(_other_chips(x, y)):
        for core in range(2):
            sends.append(copy(1 + 2 * t + c, 1 + 2 * t + core, 1 + 2 * t + c, (px, py, core)))
            arrivals.append(copy(1 + 2 * t + core, 1 + 2 * t + core, 1 + 2 * t + core, (px, py, core)))
    return sends, arrivals


def _sum_in_device_order(v, land, *, name):
    r, cdim = v.shape
    x, y, c = _position()
    slots, mine = [], []
    for d in range(N_DEV):
        dx, dy, dc = d // 4, (d // 2) % 2, d % 2
        fx, fy = jnp.bitwise_xor(dx, x), jnp.bitwise_xor(dy, y)
        t = jnp.where(fx == 1, jnp.where(fy == 1, 2, 0), 1)
        slots.append(jnp.where(jnp.logical_and(fx == 0, fy == 0), 0, 1 + 2 * t + dc))
        mine.append(jnp.logical_and(jnp.logical_and(fx == 0, fy == 0), dc == c))
    table = jnp.stack(slots + mine).astype(jnp.int32)

    def body(tab_ref, v_ref, *refs):
        out_ref = refs[N_DEV]
        acc = None
        for d in range(N_DEV):
            term = jnp.where(tab_ref[N_DEV + d] == 1, v_ref[...], refs[d][...])
            acc = term if acc is None else acc + term
        out_ref[...] = acc

    whole = pl.BlockSpec((r, cdim), lambda i, tab_ref: (0, 0))
    landed = [pl.BlockSpec((None, r, cdim), functools.partial(lambda i, tab_ref, d: (tab_ref[d], 0, 0), d=d))
              for d in range(N_DEV)]
    return _pcall(
        body, name=name,
        grid_spec=pltpu.PrefetchScalarGridSpec(num_scalar_prefetch=1, grid=(1,), in_specs=[whole] + landed,
                                               out_specs=whole),
        out_shape=jax.ShapeDtypeStruct((r, cdim), f32),
        compiler_params=_params(1),
    )(table, v, *[land] * N_DEV)


def _block_rows(buf, px, py, pc):
    m = buf.shape[0] // N_DEV
    return buf.at[pl.ds(pl.multiple_of((4 * px + 2 * py + pc) * m, 16), m), :]


def _gather_ici_copies(n):
    def copies(bufs, send_sems, recv_sems):
        x, y, c = _position()
        cps = []
        for k in range(n):
            rows = _block_rows(bufs[k], x, y, c)
            targets = [(x, y, 1 - c)] + [(px, py, c) for px, py in _other_chips(x, y)]
            for j, to in enumerate(targets):
                cps.append(pltpu.make_async_remote_copy(
                    src_ref=rows, dst_ref=rows, send_sem=send_sems.at[4 * k + j], recv_sem=recv_sems.at[4 * k + j],
                    device_id=to, device_id_type=MESH))
        return cps
    return copies


def _gather_d2d_copies(n):
    def copies(bufs, send_sems, recv_sems):
        x, y, c = _position()
        cps = []
        for k in range(n):
            for j, (px, py) in enumerate(_other_chips(x, y)):
                rows = _block_rows(bufs[k], px, py, c)
                cps.append(pltpu.make_async_remote_copy(
                    src_ref=rows, dst_ref=rows, send_sem=send_sems.at[3 * k + j], recv_sem=recv_sems.at[3 * k + j],
                    device_id=(x, y, 1 - c), device_id_type=MESH))
        return cps
    return copies


def _cast_halves(shards, after, *, name):
    n = len(shards)
    where = jnp.stack([2 * lax.axis_index("x") + lax.axis_index("y"), lax.axis_index("c")]).astype(jnp.int32)

    def body(w_ref, *refs):
        for k in range(n):
            refs[n + 1 + k][...] = refs[k][...].astype(refs[n + 1 + k].dtype)

    def half(s):
        return (s.shape[0] // 2, s.shape[1])

    return _pcall(
        body, name=name,
        grid_spec=pltpu.PrefetchScalarGridSpec(
            num_scalar_prefetch=1, grid=(1,),
            in_specs=[pl.BlockSpec(half(s), lambda i, w_ref: (w_ref[1], 0)) for s in shards] + [HBM_SPEC],
            out_specs=[pl.BlockSpec(half(s), lambda i, w_ref: (2 * w_ref[0] + w_ref[1], 0)) for s in shards]),
        out_shape=[jax.ShapeDtypeStruct((N_SHARD * s.shape[0], s.shape[1]), MXU_DTYPE) for s in shards],
        compiler_params=_params(1),
    )(where, *shards, after)


class _SplitGather:
    def __init__(self, shards, after, tag):
        self.tag = tag
        self.n = len(shards)
        halves = _cast_halves(shards, after, name=f"{tag}_cast")
        self.ici = _split_start(halves, _gather_ici_copies(self.n), 4 * self.n, name=f"{tag}_ici_start")
        self.token = self.ici[3]

    def forward(self, after):
        send_sems, recv_sems, thru, _ = self.ici
        landed = _split_wait(thru, send_sems, recv_sems, after, _gather_ici_copies(self.n), name=f"{self.tag}_ici_wait")
        self.d2d = _split_start(landed, _gather_d2d_copies(self.n), 3 * self.n, name=f"{self.tag}_d2d_start")
        return self.d2d[3]

    def finish(self, after):
        send_sems, recv_sems, thru, _ = self.d2d
        return _split_wait(thru, send_sems, recv_sems, after, _gather_d2d_copies(self.n), name=f"{self.tag}_d2d_wait")


class _Overlap(_NoOverlap):
    def __init__(self, late_shards, ffn2_shards, after):
        self.late = _SplitGather(late_shards, after, "ag1")
        self.ffn2 = _SplitGather(ffn2_shards, self.late.token, "ag2")
        self.reduced = None
        self.ffn1_parts = []

    def start_token(self):
        return self.ffn2.token

    def late_weights(self, w, after):
        token = self.late.forward(after)
        f1d, w_in, wo = self.late.finish([token])
        w_in = w_in.reshape(N_SHARD, D_MODEL, IN_SHARD).transpose(1, 0, 2).reshape(D_MODEL, IN_COLS)
        return dict(f1d=f1d.reshape(N_SHARD, D_FF // N_SHARD, D_MODEL), wp=make_wp(w_in), wo=wo)

    def after_attention(self, after):
        return self.ffn2.forward(after)

    def ffn2_weights(self, w, after):
        full = self.ffn2.finish(after)
        fs = D_FF // N_SHARD
        return (full[0].reshape(N_SHARD, D_MODEL, fs), full[1].reshape(N_SHARD, D_MODEL, fs),
                full[2].reshape(N_SHARD, fs, D_MODEL))

    @staticmethod
    def _send_direct(grads, tag):
        lands = [lax.empty((N_PEERS, g.shape[1] // 2, g.shape[2]), g.dtype) for g in grads]
        return _split_start(list(grads) + lands, _direct_copies(len(grads)), N_PEERS * len(grads),
                            name=f"rs_direct_{tag}_start")

    def ffn2_grads(self, grads):
        self.scatter = self._send_direct(grads, "ffn2")
        return self.scatter[3]

    def ffn1_grads(self, grads):
        tag = "ffn1" + "ab"[len(self.ffn1_parts)]
        if not self.ffn1_parts:
            started = self._send_direct(grads, tag)
        else:
            recvs = _swap_halves(grads, name=f"rs_swap_{tag}")
            ps = list(_add_halves(grads[:2], recvs[:2], name=f"rs_add_{tag}_gu"))
            ps += list(_add_halves(grads[2:], recvs[2:], name=f"rs_add_{tag}_d"))
            lands = [lax.empty(p.shape, p.dtype) for p in ps]
            started = _split_start(ps + lands, _scatter_copies(3), 9, name=f"rs_scatter_{tag}_start")
        self.ffn1_parts.append((tag, started))
        return started[3]

    def ffn1_reduced(self, after):
        sums = []
        for direct, (tag, (send_sems, recv_sems, thru, _)) in zip((True, False), self.ffn1_parts):
            plan, add = (_direct_copies, _sum_direct) if direct else (_scatter_copies, _sum_slabs)
            done = _split_wait(thru, send_sems, recv_sems, after, plan(3), name=f"rs_{tag}_wait")
            sums += list(add(done[:2], done[3:5], name=f"rs_sum_{tag}_gu"))
            sums += list(add(done[2:3], done[5:], name=f"rs_sum_{tag}_d"))
        return sums

    def mixer_grads(self, dwp, dwo, small, loss):
        packed = jnp.concatenate([_pack_small(small), jnp.broadcast_to(loss, (8, LANES))], axis=0)
        land = lax.empty((N_PEERS,) + packed.shape, packed.dtype)
        self.small = _split_start([packed, land], _broadcast_copies, N_PEERS, name="ar_small_start")
        gwin = dwp[:, :IN_COLS].reshape(D_MODEL, N_SHARD, IN_SHARD).transpose(1, 0, 2).astype(GRAD_DTYPE)
        gwo = dwo.reshape(N_SHARD, D_MODEL // N_SHARD, D_MODEL).astype(GRAD_DTYPE)
        self.scatter_mix = self._send_direct([gwin, gwo], "mix")
        return self.small[3] + self.scatter_mix[3]

    def small_summed(self, after):
        send_sems, recv_sems, thru, _ = self.small
        packed, land = _split_wait(thru, send_sems, recv_sems, after, _broadcast_copies, name="ar_small_wait")
        summed = _sum_in_device_order(packed, land, name="ar_small_sum")
        return summed[:-8], summed[-8, 0]

    def mixer_reduced(self, after):
        send_sems, recv_sems, thru, _ = self.scatter_mix
        done = _split_wait(thru, send_sems, recv_sems, after, _direct_copies(2), name="rs_direct_mix_wait")
        return [_sum_direct([done[k]], [done[2 + k]], name=f"rs_sum_{tag}")[0] for k, tag in enumerate(["w_in", "w_out"])]

    def before_ffn1_bwd(self, after):
        send_sems, recv_sems, thru, _ = self.scatter
        n = len(thru) // 2
        done = _split_wait(thru, send_sems, recv_sems, after, _direct_copies(n), name="rs_direct_ffn2_wait")
        self.reduced = list(_sum_direct(done[:n], done[n:], name="rs_sum_ffn2"))


def _adamw(gs, ws, ms, vs, *, name, tm=256):
    n = len(gs)
    r, cdim = gs[0].shape
    tm = r if tm is None else min(tm, r)
    assert r % tm == 0, (r, tm)
    c1 = 1.0 / (1.0 - ADAM_B1 ** ADAM_STEP)
    c2 = 1.0 / (1.0 - ADAM_B2 ** ADAM_STEP)

    def body(*refs):
        for k in range(n):
            g = refs[k][...]
            w = refs[n + k][...]
            m = ADAM_B1 * refs[2 * n + k][...] + (1.0 - ADAM_B1) * g
            v = ADAM_B2 * refs[3 * n + k][...] + (1.0 - ADAM_B2) * (g * g)
            refs[4 * n + k][...] = g
            refs[5 * n + k][...] = -ADAM_LR * ((m * c1) / (jnp.sqrt(v * c2) + ADAM_EPS) + ADAM_WD * w)
            refs[6 * n + k][...] = m
            refs[7 * n + k][...] = v

    flat = pl.BlockSpec((tm, cdim), lambda i: (i, 0))
    like_w = flat if ws[0].ndim == 2 else pl.BlockSpec((None, tm, cdim), lambda i: (0, i, 0))
    outs = _pcall(
        body, name=name, grid=(r // tm,), in_specs=[flat] * n + [like_w] * (3 * n), out_specs=[like_w] * (4 * n),
        out_shape=[jax.ShapeDtypeStruct(ws[0].shape, f32)] * (4 * n),
        compiler_params=_params(1),
    )(*gs, *ws, *ms, *vs)
    return outs[:n], outs[n:2 * n], outs[2 * n:3 * n], outs[3 * n:]


BIG = ["ffn1_w_gate", "ffn1_w_up", "ffn1_w_down", "ffn2_w_gate", "ffn2_w_up", "ffn2_w_down"]
SMALL = ["ln1_g", "ln1_b", "b_forget", "conv_w", "conv_b", "rg_wa", "rg_ba", "rg_wx", "rg_bx", "lru_lambda",
         "ln2_g", "ln2_b", "ln3_g", "ln3_b"]
WEIGHTS = ["ffn1_w_gate", "ffn1_w_up", "ffn1_w_down", "ln1_g", "ln1_b", "w_in", "b_forget", "conv_w", "conv_b",
           "rg_wa", "rg_ba", "rg_wx", "rg_bx", "lru_lambda", "w_out", "ln2_g", "ln2_b",
           "ffn2_w_gate", "ffn2_w_up", "ffn2_w_down", "ln3_g", "ln3_b"]


def _pack_small(parts):
    rows = []
    for n in SMALL:
        flat = parts[n].reshape(-1)
        pad = (-flat.shape[0]) % LANES
        rows.append(jnp.pad(flat, (0, pad)).reshape(-1, LANES))
    packed = jnp.concatenate(rows, axis=0)
    return jnp.pad(packed, ((0, (-packed.shape[0]) % 8), (0, 0)))


def _unpack_small(packed, shapes):
    out, r0 = {}, 0
    for n in SMALL:
        size = math.prod(shapes[n])
        nr = -(-size // LANES)
        out[n] = packed[r0:r0 + nr].reshape(-1)[:size].reshape(shapes[n])
        r0 += nr
    return out


def kernel(x, ffn1_w_gate, ffn1_w_up, ffn1_w_down, ln1_g, ln1_b, w_in, b_forget, conv_w, conv_b, rg_wa, rg_ba, rg_wx, rg_bx, lru_lambda, w_out, ln2_g, ln2_b, ffn2_w_gate, ffn2_w_up, ffn2_w_down, ln3_g, ln3_b, loss_target, m_ffn1_w_gate, m_ffn1_w_up, m_ffn1_w_down, m_ln1_g, m_ln1_b, m_w_in, m_b_forget, m_conv_w, m_conv_b, m_rg_wa, m_rg_ba, m_rg_wx, m_rg_bx, m_lru_lambda, m_w_out, m_ln2_g, m_ln2_b, m_ffn2_w_gate, m_ffn2_w_up, m_ffn2_w_down, m_ln3_g, m_ln3_b, v_ffn1_w_gate, v_ffn1_w_up, v_ffn1_w_down, v_ln1_g, v_ln1_b, v_w_in, v_b_forget, v_conv_w, v_conv_b, v_rg_wa, v_rg_ba, v_rg_wx, v_rg_bx, v_lru_lambda, v_w_out, v_ln2_g, v_ln2_b, v_ffn2_w_gate, v_ffn2_w_up, v_ffn2_w_down, v_ln3_g, v_ln3_b):
    args = dict(locals())
    w = {n: args[n] for n in WEIGHTS}
    mom = {n: args["m_" + n] for n in WEIGHTS}
    var = {n: args["v_" + n] for n in WEIGHTS}
    chip = 2 * lax.axis_index("x") + lax.axis_index("y")

    g1 = _all_gather_bf16([w[n][0] for n in BIG[:2]], name="ag_ffn1_up")
    fs = D_FF // N_SHARD
    full = dict(
        f1g=g1[0].reshape(N_SHARD, D_MODEL, fs), f1u=g1[1].reshape(N_SHARD, D_MODEL, fs),
        bfp=jnp.pad(b_forget, ((0, 0), (0, LANES - HEADS))),
        ln1_g=ln1_g, ln1_b=ln1_b, ln2_g=ln2_g, ln2_b=ln2_b, ln3_g=ln3_g, ln3_b=ln3_b,
        conv_b=conv_b, rg_wa=rg_wa[0], rg_wx=rg_wx[0], rg_ba=rg_ba[0], rg_bx=rg_bx[0], lam=lru_lambda,
    )
    cw_place = lax.dynamic_update_slice(jnp.zeros((8, LRU_W), f32), conv_w[0] * 0.5, (0, chip * (LRU_W // N_SHARD)))
    cw_full = _all_reduce_small(cw_place.reshape(-1, LANES), g1[0], name="ag_conv_w")
    full["conv_w"] = cw_full.reshape(8, LRU_W)[:CONV_K]

    hooks = _Overlap([w["ffn1_w_down"][0], w["w_in"][0], w["w_out"][0]], [w[n][0] for n in BIG[3:]], cw_full)
    loss_rep, dx, g = _local_step(x[0], loss_target[0], full, hooks)

    token1 = hooks.ffn1_grads(g["f1"][1])
    red = _join_halves(hooks.reduced + hooks.mixer_reduced([token1]), name="rs_join_rest")
    grads = dict(zip(BIG[3:] + ["w_in", "w_out"], red))

    small_sum, loss = hooks.small_summed(red)
    small_shapes = {n: w[n].shape for n in SMALL}
    small_shapes["conv_w"] = (1, CONV_K, LRU_W)
    gs_red = _unpack_small(small_sum, small_shapes)
    gs_red["conv_w"] = lax.dynamic_slice(gs_red["conv_w"], (0, 0, chip * (LRU_W // N_SHARD)),
                                         (1, CONV_K, LRU_W // N_SHARD))
    grads.update(gs_red)

    delta, new_m, new_v = {}, {}, {}

    def adamw(names, name, **kw):
        g3, d, nm, nv = _adamw([grads[n] for n in names], [w[n] for n in names], [mom[n] for n in names],
                               [var[n] for n in names], name=name, **kw)
        for i, n in enumerate(names):
            grads[n], delta[n], new_m[n], new_v[n] = g3[i], d[i], nm[i], nv[i]

    adamw(BIG[3:], "adamw_ffn2", tm=128)
    adamw(["w_in"], "adamw_w_in")
    adamw(["w_out"], "adamw_w_out")
    shard_shapes = {n: w[n].shape for n in SMALL}
    _, d, nm, nv = _adamw([_pack_small({n: grads[n] for n in SMALL})], [_pack_small({n: w[n] for n in SMALL})],
                          [_pack_small({n: mom[n] for n in SMALL})], [_pack_small({n: var[n] for n in SMALL})],
                          name="adamw_small", tm=None)
    for dst, packed in ((delta, d[0]), (new_m, nm[0]), (new_v, nv[0])):
        dst.update(_unpack_small(packed, shard_shapes))

    worked = [new_v["ffn2_w_down"], new_v["w_in"], new_v["w_out"], nv[0]]
    ga, ua, da, gb, ub, db = _join_halves(hooks.ffn1_reduced(worked), name="rs_join_ffn1")
    grads["ffn1_w_gate"] = jnp.concatenate([ga, gb], axis=1)
    grads["ffn1_w_up"] = jnp.concatenate([ua, ub], axis=1)
    grads["ffn1_w_down"] = jnp.concatenate([da, db], axis=0)
    adamw(BIG[:3], "adamw_ffn1", tm=128)

    def shaped(tree, n):
        return tree[n].reshape(w[n].shape)

    return (loss, dx[None], *[shaped(grads, n) for n in WEIGHTS], *[shaped(delta, n) for n in WEIGHTS],
            *[shaped(new_m, n) for n in WEIGHTS], *[shaped(new_v, n) for n in WEIGHTS])
```

```python
import functools
import math

import jax
import jax.numpy as jnp
from jax import lax
from jax.experimental import pallas as pl
from jax.experimental.pallas import tpu as pltpu

f32 = jnp.float32
MXU_DTYPE = jnp.bfloat16
GRAD_DTYPE = jnp.bfloat16

D_MODEL = 1024
D_FF = 4096
N_SHARD = 4
N_DEV = 8
FOX_W = 512
LRU_W = 512
HEADS = 8
HEAD_DIM = 64
CONV_K = 4
IN_COLS = 2568
IN_SHARD = IN_COLS // N_SHARD
QKV_W = 3 * FOX_W
Z_PAD = 2688
LANES = 128
LN_EPS = 1e-5
DN_ALPHA = 2.0 ** 0.25
LRU_C = 8.0
NEG_BIG = -1e30
VMEM_LIMIT = 56 * 1024 * 1024

ADAM_LR = 0.001
ADAM_B1 = 0.9
ADAM_B2 = 0.999
ADAM_EPS = 1e-08
ADAM_WD = 0.01
ADAM_STEP = 10


def _pcall(body, **kw):
    return pl.pallas_call(body, **kw)


def _params(n_grid, vmem=VMEM_LIMIT):
    return pltpu.CompilerParams(dimension_semantics=("arbitrary",) * n_grid, vmem_limit_bytes=vmem)


def _dot(a, b):
    return jnp.dot(a, b, preferred_element_type=f32)


def _dot_nt(a, b):
    return lax.dot_general(a, b, (((1,), (1,)), ((), ())), preferred_element_type=f32)


def _dot_tn(a, b):
    return lax.dot_general(a, b, (((0,), (0,)), ((), ())), preferred_element_type=f32)


def _sigmoid(x):
    return 1.0 / (1.0 + jnp.exp(-x))


def _layer_norm_stats(y):
    mu = jnp.mean(y, axis=-1, keepdims=True)
    yc = y - mu
    var = jnp.mean(yc * yc, axis=-1, keepdims=True)
    rstd = lax.rsqrt(var + LN_EPS)
    return yc * rstd, rstd


def _ln_backward(dy, xhat, rstd, gamma):
    dxhat = dy * gamma
    m1 = jnp.mean(dxhat, axis=-1, keepdims=True)
    m2 = jnp.mean(dxhat * xhat, axis=-1, keepdims=True)
    dyp = rstd * (dxhat - m1 - xhat * m2)
    return dyp, jnp.sum(dy * xhat, axis=0, keepdims=True), jnp.sum(dy, axis=0, keepdims=True)


def _ffn_fwd(x, wg, wu, wd, ln_g, ln_b, *, name, tm=1024, tf=512):
    T = x.shape[0]
    tm = min(tm, T)
    fs = D_FF // N_SHARD
    cpf = fs // tf
    nf = D_FF // tf
    nt = T // tm

    def body(x_ref, wg_ref, wu_ref, wd_ref, g_ref, b_ref,
             xb_ref, gact_ref, uact_ref, xhat_ref, xn_ref, rstd_ref, acc_ref):
        f = pl.program_id(1)

        @pl.when(f == 0)
        def _():
            xb_ref[...] = x_ref[...].astype(MXU_DTYPE)
            acc_ref[...] = jnp.zeros_like(acc_ref)

        xb = xb_ref[...]
        g = _dot(xb, wg_ref[...])
        u = _dot(xb, wu_ref[...])
        h = (g * _sigmoid(g)) * u
        gact_ref[...] = g.astype(gact_ref.dtype)
        uact_ref[...] = u.astype(uact_ref.dtype)
        acc_ref[...] += _dot(h.astype(MXU_DTYPE), wd_ref[...])

        @pl.when(f == nf - 1)
        def _():
            y = DN_ALPHA * x_ref[...] + 0.5 * acc_ref[...]
            xhat, rstd = _layer_norm_stats(y)
            xhat_ref[...] = xhat
            xn_ref[...] = (xhat * g_ref[...] + b_ref[...]).astype(xn_ref.dtype)
            rstd_ref[...] = jnp.broadcast_to(rstd, rstd_ref.shape)

    row = lambda i, f: (i, 0)
    return _pcall(
        body, name=name, grid=(nt, nf),
        in_specs=[
            pl.BlockSpec((tm, D_MODEL), row),
            pl.BlockSpec((None, D_MODEL, tf), lambda i, f: (f // cpf, 0, f % cpf)),
            pl.BlockSpec((None, D_MODEL, tf), lambda i, f: (f // cpf, 0, f % cpf)),
            pl.BlockSpec((None, tf, D_MODEL), lambda i, f: (f // cpf, f % cpf, 0)),
            pl.BlockSpec((1, D_MODEL), lambda i, f: (0, 0)),
            pl.BlockSpec((1, D_MODEL), lambda i, f: (0, 0)),
        ],
        out_specs=[
            pl.BlockSpec((tm, D_MODEL), row),
            pl.BlockSpec((tm, tf), lambda i, f: (i, f)),
            pl.BlockSpec((tm, tf), lambda i, f: (i, f)),
            pl.BlockSpec((tm, D_MODEL), row),
            pl.BlockSpec((tm, D_MODEL), row),
            pl.BlockSpec((tm, LANES), row),
        ],
        out_shape=[
            jax.ShapeDtypeStruct((T, D_MODEL), MXU_DTYPE),
            jax.ShapeDtypeStruct((T, D_FF), MXU_DTYPE),
            jax.ShapeDtypeStruct((T, D_FF), MXU_DTYPE),
            jax.ShapeDtypeStruct((T, D_MODEL), f32),
            jax.ShapeDtypeStruct((T, D_MODEL), MXU_DTYPE),
            jax.ShapeDtypeStruct((T, LANES), f32),
        ],
        scratch_shapes=[pltpu.VMEM((tm, D_MODEL), f32)],
        compiler_params=_params(2),
    )(x, wg, wu, wd, ln_g, ln_b)


def _ffn_up(x, wg, wu, after=None, *, name, tm=1024, tf=512):
    T = x.shape[0]
    tm = min(tm, T)
    cpf = (D_FF // N_SHARD) // tf
    nf = D_FF // tf
    extra = [] if after is None else [after]

    def body(x_ref, wg_ref, wu_ref, *refs):
        xb_ref, gact_ref, uact_ref, hact_ref = refs[len(extra):]

        @pl.when(pl.program_id(1) == 0)
        def _():
            xb_ref[...] = x_ref[...].astype(MXU_DTYPE)

        xb = xb_ref[...]
        g = _dot(xb, wg_ref[...])
        u = _dot(xb, wu_ref[...])
        gact_ref[...] = g.astype(gact_ref.dtype)
        uact_ref[...] = u.astype(uact_ref.dtype)
        hact_ref[...] = ((g * _sigmoid(g)) * u).astype(hact_ref.dtype)

    row = lambda i, f: (i, 0)
    tile = pl.BlockSpec((tm, tf), lambda i, f: (i, f))
    cols = pl.BlockSpec((None, D_MODEL, tf), lambda i, f: (f // cpf, 0, f % cpf))
    return _pcall(
        body, name=name, grid=(T // tm, nf),
        in_specs=[pl.BlockSpec((tm, D_MODEL), row), cols, cols] + [pl.BlockSpec(memory_space=pl.ANY)] * len(extra),
        out_specs=[pl.BlockSpec((tm, D_MODEL), row), tile, tile, tile],
        out_shape=[jax.ShapeDtypeStruct((T, D_MODEL), MXU_DTYPE)] + [jax.ShapeDtypeStruct((T, D_FF), MXU_DTYPE)] * 3,
        compiler_params=_params(2),
    )(x, wg, wu, *extra)


def _ffn_down_ln(x, hact, wd, ln_g, ln_b, *, name, tm=1024):
    T = x.shape[0]
    tm = min(tm, T)
    fs = D_FF // N_SHARD

    def body(x_ref, h_ref, wd_ref, g_ref, b_ref, xhat_ref, xn_ref, rstd_ref, acc_ref):
        k = pl.program_id(1)

        @pl.when(k == 0)
        def _():
            acc_ref[...] = jnp.zeros_like(acc_ref)

        acc_ref[...] += _dot(h_ref[...], wd_ref[...])

        @pl.when(k == N_SHARD - 1)
        def _():
            xhat, rstd = _layer_norm_stats(DN_ALPHA * x_ref[...] + 0.5 * acc_ref[...])
            xhat_ref[...] = xhat
            xn_ref[...] = (xhat * g_ref[...] + b_ref[...]).astype(xn_ref.dtype)
            rstd_ref[...] = jnp.broadcast_to(rstd, rstd_ref.shape)

    row = lambda i, k: (i, 0)
    vec = pl.BlockSpec((1, D_MODEL), lambda i, k: (0, 0))
    return _pcall(
        body, name=name, grid=(T // tm, N_SHARD),
        in_specs=[pl.BlockSpec((tm, D_MODEL), row), pl.BlockSpec((tm, fs), lambda i, k: (i, k)),
                  pl.BlockSpec((None, fs, D_MODEL), lambda i, k: (k, 0, 0)), vec, vec],
        out_specs=[pl.BlockSpec((tm, D_MODEL), row), pl.BlockSpec((tm, D_MODEL), row), pl.BlockSpec((tm, LANES), row)],
        out_shape=[jax.ShapeDtypeStruct((T, D_MODEL), f32), jax.ShapeDtypeStruct((T, D_MODEL), MXU_DTYPE),
                   jax.ShapeDtypeStruct((T, LANES), f32)],
        scratch_shapes=[pltpu.VMEM((tm, D_MODEL), f32)],
        compiler_params=_params(2),
    )(x, hact, wd, ln_g, ln_b)


def _ffn_bwd(dyp, xb, gact, uact, wg, wu, wd, after=None, *, name, tm=512, tf=512, part=None, dx_init=None):
    T = dyp.shape[0]
    tm = min(tm, T)
    fs = D_FF // N_SHARD
    cpf = fs // tf
    nt = T // tm
    nf = D_FF // tf if part is None else N_SHARD
    wf = fs if part is None else tf
    slab = (lambda f: f // cpf) if part is None else (lambda f: f)
    chunk = (lambda f: f % cpf) if part is None else (lambda f: part)
    extra = ([] if dx_init is None else [dx_init]) + ([] if after is None else [after])

    def body(dyp_ref, xb_ref, g_ref, u_ref, wg_ref, wu_ref, wd_ref, *refs):
        dx_hbm, dwg_ref, dwu_ref, dwd_ref, dx_sc, dwg_sc, dwu_sc, dwd_sc, sem = refs[len(extra):]
        f = pl.program_id(0)
        i = pl.program_id(1)
        rows = pl.ds(pl.multiple_of(i * tm, tm), tm)
        dyp_t = dyp_ref[...]
        dy = (0.5 * dyp_t).astype(MXU_DTYPE)

        @pl.when(i == 0)
        def _():
            dwg_sc[...] = jnp.zeros_like(dwg_sc)
            dwu_sc[...] = jnp.zeros_like(dwu_sc)
            dwd_sc[...] = jnp.zeros_like(dwd_sc)

        @pl.when(f == 0)
        def _():
            dx_sc[rows, :] = DN_ALPHA * dyp_t if dx_init is None else refs[0][...]

        g = g_ref[...].astype(f32)
        u = u_ref[...].astype(f32)
        sig = _sigmoid(g)
        silu = g * sig
        dh = _dot_nt(dy, wd_ref[...])
        dg = (dh * u * (sig * (1.0 + g * (1.0 - sig)))).astype(MXU_DTYPE)
        du = (dh * silu).astype(MXU_DTYPE)
        hb = (silu * u).astype(MXU_DTYPE)
        dx_sc[rows, :] += _dot_nt(dg, wg_ref[...]) + _dot_nt(du, wu_ref[...])
        xb_t = xb_ref[...]
        dwg_sc[...] += _dot_tn(xb_t, dg)
        dwu_sc[...] += _dot_tn(xb_t, du)
        dwd_sc[...] += _dot_tn(hb, dy)

        @pl.when(i == nt - 1)
        def _():
            dwg_ref[...] = dwg_sc[...].astype(dwg_ref.dtype)
            dwu_ref[...] = dwu_sc[...].astype(dwu_ref.dtype)
            dwd_ref[...] = dwd_sc[...].astype(dwd_ref.dtype)

        @pl.when(jnp.logical_and(f == nf - 1, i == nt - 1))
        def _():
            cp = pltpu.make_async_copy(dx_sc, dx_hbm, sem)
            cp.start()
            cp.wait()

    row = lambda f, i: (i, 0)
    return _pcall(
        body, name=name, grid=(nf, nt),
        in_specs=[
            pl.BlockSpec((tm, D_MODEL), row),
            pl.BlockSpec((tm, D_MODEL), row),
            pl.BlockSpec((tm, tf), lambda f, i: (i, slab(f) * cpf + chunk(f))),
            pl.BlockSpec((tm, tf), lambda f, i: (i, slab(f) * cpf + chunk(f))),
            pl.BlockSpec((None, D_MODEL, tf), lambda f, i: (slab(f), 0, chunk(f))),
            pl.BlockSpec((None, D_MODEL, tf), lambda f, i: (slab(f), 0, chunk(f))),
            pl.BlockSpec((None, tf, D_MODEL), lambda f, i: (slab(f), chunk(f), 0)),
        ] + ([] if dx_init is None else [pl.BlockSpec((tm, D_MODEL), row)])
        + ([] if after is None else [pl.BlockSpec(memory_space=pl.ANY)]),
        out_specs=[
            pl.BlockSpec(memory_space=pl.ANY),
            pl.BlockSpec((None, D_MODEL, tf), lambda f, i: (slab(f), 0, chunk(f) if part is None else 0)),
            pl.BlockSpec((None, D_MODEL, tf), lambda f, i: (slab(f), 0, chunk(f) if part is None else 0)),
            pl.BlockSpec((None, tf, D_MODEL), lambda f, i: (slab(f), chunk(f) if part is None else 0, 0)),
        ],
        out_shape=[
            jax.ShapeDtypeStruct((T, D_MODEL), f32),
            jax.ShapeDtypeStruct((N_SHARD, D_MODEL, wf), GRAD_DTYPE),
            jax.ShapeDtypeStruct((N_SHARD, D_MODEL, wf), GRAD_DTYPE),
            jax.ShapeDtypeStruct((N_SHARD, wf, D_MODEL), GRAD_DTYPE),
        ],
        scratch_shapes=[pltpu.VMEM((T, D_MODEL), f32), pltpu.VMEM((D_MODEL, tf), f32),
                        pltpu.VMEM((D_MODEL, tf), f32), pltpu.VMEM((tf, D_MODEL), f32),
                        pltpu.SemaphoreType.DMA],
        compiler_params=_params(2),
    )(dyp, xb, gact, uact, wg, wu, wd, *extra)


def _loss_ln_bwd(xhat, rstd, ln_g, ln_b, target, *, name, tm=512):
    T = xhat.shape[0]
    tm = min(tm, T)
    nt = T // tm

    def body(xhat_ref, rstd_ref, g_ref, b_ref, t_ref, dyp_ref, dg_ref, db_ref, loss_ref):
        i = pl.program_id(0)

        @pl.when(i == 0)
        def _():
            dg_ref[...] = jnp.zeros_like(dg_ref)
            db_ref[...] = jnp.zeros_like(db_ref)
            loss_ref[...] = jnp.zeros_like(loss_ref)

        xhat_t = xhat_ref[...]
        gamma = g_ref[...]
        err = xhat_t * gamma + b_ref[...] - t_ref[...]
        sq = jnp.sum(jnp.sum(err * err, axis=0, keepdims=True), axis=1, keepdims=True)
        loss_ref[...] += jnp.broadcast_to(sq * (0.5 / D_MODEL), loss_ref.shape)
        dy = err * (1.0 / D_MODEL)
        dyp, dgam, dbeta = _ln_backward(dy, xhat_t, rstd_ref[:, 0:1], gamma)
        dyp_ref[...] = dyp
        dg_ref[...] += dgam
        db_ref[...] += dbeta

    row = lambda i: (i, 0)
    const = lambda i: (0, 0)
    return _pcall(
        body, name=name, grid=(nt,),
        in_specs=[pl.BlockSpec((tm, D_MODEL), row), pl.BlockSpec((tm, LANES), row),
                  pl.BlockSpec((1, D_MODEL), const), pl.BlockSpec((1, D_MODEL), const),
                  pl.BlockSpec((tm, D_MODEL), row)],
        out_specs=[pl.BlockSpec((tm, D_MODEL), row), pl.BlockSpec((1, D_MODEL), const),
                   pl.BlockSpec((1, D_MODEL), const), pl.BlockSpec((1, LANES), const)],
        out_shape=[jax.ShapeDtypeStruct((T, D_MODEL), f32), jax.ShapeDtypeStruct((1, D_MODEL), f32),
                   jax.ShapeDtypeStruct((1, D_MODEL), f32), jax.ShapeDtypeStruct((1, LANES), f32)],
        compiler_params=_params(1),
    )(xhat, rstd, ln_g, ln_b, target)


def _proj_in(xn, wp, bfp, *, name, tm=512):
    T = xn.shape[0]
    tm = min(tm, T)
    nt = T // tm

    def body(x_ref, w_ref, b_ref, qkv_ref, lxg_ref, fg_ref):
        z = _dot(x_ref[...], w_ref[...])
        qkv_ref[...] = z[:, :QKV_W].astype(qkv_ref.dtype)
        lxg_ref[...] = z[:, QKV_W:QKV_W + 2 * LRU_W]
        fg_ref[...] = z[:, QKV_W + 2 * LRU_W:] + b_ref[...]

    row = lambda i: (i, 0)
    const = lambda i: (0, 0)
    return _pcall(
        body, name=name, grid=(nt,),
        in_specs=[pl.BlockSpec((tm, D_MODEL), row), pl.BlockSpec((D_MODEL, Z_PAD), const),
                  pl.BlockSpec((1, LANES), const)],
        out_specs=[pl.BlockSpec((tm, QKV_W), row), pl.BlockSpec((tm, 2 * LRU_W), row),
                   pl.BlockSpec((tm, LANES), row)],
        out_shape=[jax.ShapeDtypeStruct((T, QKV_W), MXU_DTYPE), jax.ShapeDtypeStruct((T, 2 * LRU_W), f32),
                   jax.ShapeDtypeStruct((T, LANES), f32)],
        compiler_params=_params(1),
    )(xn, wp, bfp)


def _proj_in_bwd(dqa, dka, dva, dlxg, dfg, xn, dyp, wp, xhat, rstd, ln_g, *, name, tm=512):
    T = xn.shape[0]
    tm = min(tm, T)
    nt = T // tm

    def body(dq_ref, dk_ref, dv_ref, dl_ref, dfg_ref, x_ref, dyp_ref, w_ref, xhat_ref, rstd_ref, g_ref,
             dpre_ref, dw_hbm, dgam_ref, dbeta_ref, dw_sc, sem):
        i = pl.program_id(0)

        @pl.when(i == 0)
        def _():
            dw_sc[...] = jnp.zeros_like(dw_sc)
            dgam_ref[...] = jnp.zeros_like(dgam_ref)
            dbeta_ref[...] = jnp.zeros_like(dbeta_ref)

        low = _low_lanes((tm, LANES))

        def packed(ref):
            pairs = [jnp.where(low, ref[:, (2 * j) * LANES:(2 * j + 1) * LANES],
                               _swap_lane_halves(ref[:, (2 * j + 1) * LANES:(2 * j + 2) * LANES]))
                     for j in range(HEADS // 2)]
            return jnp.concatenate(pairs, axis=1).astype(MXU_DTYPE)

        dz = jnp.concatenate(
            [packed(dq_ref), packed(dk_ref), packed(dv_ref),
             dl_ref[...].astype(MXU_DTYPE), dfg_ref[...].astype(MXU_DTYPE)], axis=1)
        dx = DN_ALPHA * dyp_ref[...] + _dot_nt(dz, w_ref[...])
        dpre, dgam, dbeta = _ln_backward(dx, xhat_ref[...], rstd_ref[:, 0:1], g_ref[...])
        dpre_ref[...] = dpre
        dgam_ref[...] += dgam
        dbeta_ref[...] += dbeta
        dw_sc[...] += _dot_tn(x_ref[...], dz)

        @pl.when(i == nt - 1)
        def _():
            dw_sc[:, :FOX_W] = dw_sc[:, :FOX_W] * (1.0 / math.sqrt(HEAD_DIM))
            cp = pltpu.make_async_copy(dw_sc, dw_hbm, sem)
            cp.start()
            cp.wait()

    row = lambda i: (i, 0)
    const = lambda i: (0, 0)
    return _pcall(
        body, name=name, grid=(nt,),
        in_specs=[pl.BlockSpec((tm, HEADS * LANES), row), pl.BlockSpec((tm, HEADS * LANES), row),
                  pl.BlockSpec((tm, HEADS * LANES), row),
                  pl.BlockSpec((tm, 2 * LRU_W), row), pl.BlockSpec((tm, LANES), row),
                  pl.BlockSpec((tm, D_MODEL), row), pl.BlockSpec((tm, D_MODEL), row),
                  pl.BlockSpec((D_MODEL, Z_PAD), const),
                  pl.BlockSpec((tm, D_MODEL), row), pl.BlockSpec((tm, LANES), row), pl.BlockSpec((1, D_MODEL), const)],
        out_specs=[pl.BlockSpec((tm, D_MODEL), row), pl.BlockSpec(memory_space=pl.ANY),
                   pl.BlockSpec((1, D_MODEL), const), pl.BlockSpec((1, D_MODEL), const)],
        out_shape=[jax.ShapeDtypeStruct((T, D_MODEL), f32), jax.ShapeDtypeStruct((D_MODEL, Z_PAD), f32),
                   jax.ShapeDtypeStruct((1, D_MODEL), f32), jax.ShapeDtypeStruct((1, D_MODEL), f32)],
        scratch_shapes=[pltpu.VMEM((D_MODEL, Z_PAD), f32), pltpu.SemaphoreType.DMA],
        compiler_params=_params(1),
    )(dqa, dka, dva, dlxg, dfg, xn, dyp, wp, xhat, rstd, ln_g)


def _split3(x):
    hi = x.astype(jnp.bfloat16)
    r1 = x - hi.astype(f32)
    mid = r1.astype(jnp.bfloat16)
    lo = (r1 - mid.astype(f32)).astype(jnp.bfloat16)
    return hi, mid, lo


def _tri_dot(tri, x):
    hi, mid, lo = _split3(x)
    return _dot(tri, hi) + _dot(tri, mid) + _dot(tri, lo)


FOX_PAD = HEADS * LANES
AUX = HEAD_DIM


def _low_lanes(shape):
    return lax.broadcasted_iota(jnp.int32, shape, 1) < HEAD_DIM


def _swap_lane_halves(x):
    return pltpu.roll(x, HEAD_DIM, 1)


def _fox_prep(qkv, fgb, *, name, tm=512):
    T = fgb.shape[0]
    tm = min(tm, T)
    nt = T // tm

    def body(qkv_ref, fg_ref, qa_ref, ka_ref, va_ref, carry):
        i = pl.program_id(0)

        @pl.when(i == 0)
        def _():
            carry[...] = jnp.zeros_like(carry)

        x = fg_ref[...]
        ls = jnp.minimum(x, 0.0) - jnp.log(1.0 + jnp.exp(-jnp.abs(x)))
        r = lax.broadcasted_iota(jnp.int32, (tm, tm), 0)
        c = lax.broadcasted_iota(jnp.int32, (tm, tm), 1)
        tri = jnp.where(r >= c, 1.0, 0.0).astype(jnp.bfloat16)
        cum = _tri_dot(tri, ls) + carry[0:1, :]
        carry[...] = jnp.broadcast_to(cum[tm - 1:tm, :], carry.shape)

        lane = lax.broadcasted_iota(jnp.int32, (tm, LANES), 1)
        low = lane < HEAD_DIM
        ones_q = jnp.where(jnp.logical_and(lane >= AUX + 3, lane < AUX + 6), 1.0, 0.0)
        ones_k = jnp.where(jnp.logical_and(lane >= AUX, lane < AUX + 3), 1.0, 0.0)
        for j in range(HEADS // 2):
            pair = [qkv_ref[:, t * FOX_W + j * LANES:t * FOX_W + (j + 1) * LANES].astype(f32) for t in range(3)]
            for odd in range(2):
                h = 2 * j + odd
                q, k, v = [_swap_lane_halves(a) if odd else a for a in pair]
                hi, mid, lo = [a.astype(f32) for a in _split3(jnp.broadcast_to(cum[:, h:h + 1], (tm, LANES)))]
                aux_q = jnp.where(lane == AUX, hi, jnp.where(lane == AUX + 1, mid, jnp.where(lane == AUX + 2, lo, ones_q)))
                aux_k = jnp.where(lane == AUX + 3, -hi,
                                  jnp.where(lane == AUX + 4, -mid, jnp.where(lane == AUX + 5, -lo, ones_k)))
                blk = slice(h * LANES, (h + 1) * LANES)
                qa_ref[:, blk] = jnp.where(low, q, aux_q).astype(qa_ref.dtype)
                ka_ref[:, blk] = jnp.where(low, k, aux_k).astype(ka_ref.dtype)
                va_ref[:, blk] = jnp.where(low, v, 1.0).astype(va_ref.dtype)

    row = lambda i: (i, 0)
    return _pcall(
        body, name=name, grid=(nt,),
        in_specs=[pl.BlockSpec((tm, QKV_W), row), pl.BlockSpec((tm, LANES), row)],
        out_specs=[pl.BlockSpec((tm, FOX_PAD), row)] * 3,
        out_shape=[jax.ShapeDtypeStruct((T, FOX_PAD), MXU_DTYPE)] * 3,
        scratch_shapes=[pltpu.VMEM((8, LANES), f32)],
        compiler_params=_params(1),
    )(qkv, fgb)


def _future_keys(tq, tk):
    r = lax.broadcasted_iota(jnp.int32, (tq, tk), 0)
    c = lax.broadcasted_iota(jnp.int32, (tq, tk), 1)
    return c > r


def _causal_steps(nq, key_major):
    if key_major:
        pairs = [(qi, ki) for ki in range(nq) for qi in range(ki, nq)]
    else:
        pairs = [(qi, ki) for qi in range(nq) for ki in range(qi + 1)]
    return (jnp.asarray([p[0] for p in pairs], jnp.int32), jnp.asarray([p[1] for p in pairs], jnp.int32))


def _fox_fwd(qa, ka, va, *, name, tq=512, hps=8):
    T = qa.shape[0]
    tq = min(tq, T)
    tk = tq
    nq = T // tq
    rep = tk // LANES
    qi_tab, ki_tab = _causal_steps(nq, key_major=False)

    def body(qi_ref, ki_ref, qa_ref, ka_ref, va_ref, o_ref, lse_ref, m_sc, acc_sc):
        t = pl.program_id(1)
        qi = qi_ref[t]
        ki = ki_ref[t]

        @pl.when(ki == 0)
        def _():
            m_sc[...] = jnp.full_like(m_sc, NEG_BIG)
            acc_sc[...] = jnp.zeros_like(acc_sc)

        def tile(diagonal):
            for h in range(hps):
                blk = slice(h * LANES, (h + 1) * LANES)
                s = _dot_nt(qa_ref[:, blk], ka_ref[:, blk])
                if diagonal:
                    s = jnp.where(_future_keys(tq, tk), NEG_BIG, s)
                m_prev = m_sc[h]
                m_new = jnp.maximum(m_prev, jnp.max(s, axis=1, keepdims=True))
                p = jnp.exp(s - jnp.tile(m_new, (1, rep)))
                acc_sc[h] = jnp.exp(m_prev - m_new) * acc_sc[h] + _dot(p.astype(MXU_DTYPE), va_ref[:, blk])
                m_sc[h] = m_new

        @pl.when(ki < qi)
        def _():
            tile(False)

        @pl.when(ki == qi)
        def _():
            tile(True)
            low = _low_lanes((tq, LANES))
            outs = []
            for h in range(hps):
                acc = acc_sc[h]
                den = _swap_lane_halves(acc)
                outs.append(acc / den)
                lse_ref[h] = m_sc[h] + jnp.log(jnp.where(low, den, acc))
            for p in range(hps // 2):
                o_ref[:, p * LANES:(p + 1) * LANES] = jnp.where(low, outs[2 * p], _swap_lane_halves(outs[2 * p + 1]))

    pair = hps * LANES
    return _pcall(
        body, name=name,
        grid_spec=pltpu.PrefetchScalarGridSpec(
            num_scalar_prefetch=2, grid=(HEADS // hps, qi_tab.shape[0]),
            in_specs=[
                pl.BlockSpec((tq, pair), lambda j, t, qi_ref, ki_ref: (qi_ref[t], j)),
                pl.BlockSpec((tk, pair), lambda j, t, qi_ref, ki_ref: (ki_ref[t], j)),
                pl.BlockSpec((tk, pair), lambda j, t, qi_ref, ki_ref: (ki_ref[t], j)),
            ],
            out_specs=[pl.BlockSpec((tq, pair // 2), lambda j, t, qi_ref, ki_ref: (qi_ref[t], j)),
                       pl.BlockSpec((hps, tq, LANES), lambda j, t, qi_ref, ki_ref: (j, qi_ref[t], 0))],
            scratch_shapes=[pltpu.VMEM((hps, tq, LANES), f32)] * 2),
        out_shape=[jax.ShapeDtypeStruct((T, FOX_W), f32), jax.ShapeDtypeStruct((HEADS, T, LANES), f32)],
        compiler_params=_params(2),
    )(qi_tab, ki_tab, qa, ka, va)


def _fox_bwd_prep(do, o, *, name, tm=512):
    T = o.shape[0]
    tm = min(tm, T)
    nt = T // tm

    def body(do_ref, o_ref, d_ref, doa_ref):
        low = _low_lanes((tm, LANES))
        for j in range(HEADS // 2):
            do2 = do_ref[:, j * LANES:(j + 1) * LANES].astype(f32)
            prod = do2 * o_ref[:, j * LANES:(j + 1) * LANES]
            for odd in range(2):
                h = 2 * j + odd
                mine = jnp.where(low, _swap_lane_halves(prod) if odd else prod, 0.0)
                d_ref[h] = jnp.broadcast_to(jnp.sum(mine, axis=1, keepdims=True), (tm, LANES))
                doh = jnp.where(low, _swap_lane_halves(do2) if odd else do2, 0.0)
                doa_ref[:, h * LANES:(h + 1) * LANES] = doh.astype(doa_ref.dtype)

    return _pcall(
        body, name=name, grid=(nt,),
        in_specs=[pl.BlockSpec((tm, FOX_W), lambda i: (i, 0)), pl.BlockSpec((tm, FOX_W), lambda i: (i, 0))],
        out_specs=[pl.BlockSpec((HEADS, tm, LANES), lambda i: (0, i, 0)), pl.BlockSpec((tm, FOX_PAD), lambda i: (i, 0))],
        out_shape=[jax.ShapeDtypeStruct((HEADS, T, LANES), f32), jax.ShapeDtypeStruct((T, FOX_PAD), MXU_DTYPE)],
        compiler_params=_params(1),
    )(do, o)


def _fox_bwd(qa, ka, va, doa, lse, drep, *, name, tq=512, hps=4):
    T = qa.shape[0]
    tq = min(tq, T)
    tk = tq
    nq = T // tq
    rep = tk // LANES
    qi_tab, ki_tab = _causal_steps(nq, key_major=True)

    def body(qi_ref, ki_ref, qa_ref, ka_ref, va_ref, doa_ref, lse_ref, d_ref, dqa_ref, dka_ref, dva_ref, dk_sc, dv_sc):
        t = pl.program_id(1)
        qi = qi_ref[t]
        ki = ki_ref[t]
        rows = pl.ds(pl.multiple_of(qi * tq, tq), tq)

        @pl.when(t == 0)
        def _():
            dqa_ref[...] = jnp.zeros_like(dqa_ref)

        @pl.when(qi == ki)
        def _():
            dk_sc[...] = jnp.zeros_like(dk_sc)
            dv_sc[...] = jnp.zeros_like(dv_sc)

        def tile(diagonal):
            for h in range(hps):
                blk = slice(h * LANES, (h + 1) * LANES)
                qh, kh, doh = qa_ref[:, blk], ka_ref[:, blk], doa_ref[:, blk]
                p = jnp.exp(_dot_nt(qh, kh) - jnp.tile(lse_ref[h], (1, rep)))
                if diagonal:
                    p = jnp.where(_future_keys(tq, tk), 0.0, p)
                dp = _dot_nt(doh, va_ref[:, blk])
                ds = (p * (dp - jnp.tile(d_ref[h], (1, rep)))).astype(MXU_DTYPE)
                dv_sc[h] += _dot_tn(p.astype(MXU_DTYPE), doh)
                dk_sc[h] += _dot_tn(ds, qh)
                dqa_ref[rows, blk] += _dot(ds, kh)

        @pl.when(qi > ki)
        def _():
            tile(False)

        @pl.when(qi == ki)
        def _():
            tile(True)

        @pl.when(qi == nq - 1)
        def _():
            for h in range(hps):
                blk = slice(h * LANES, (h + 1) * LANES)
                dka_ref[:, blk] = dk_sc[h]
                dva_ref[:, blk] = dv_sc[h]

    pair = hps * LANES
    q_blk = lambda j, t, qi_ref, ki_ref: (qi_ref[t], j)
    k_blk = lambda j, t, qi_ref, ki_ref: (ki_ref[t], j)
    stat = pl.BlockSpec((hps, tq, LANES), lambda j, t, qi_ref, ki_ref: (j, qi_ref[t], 0))
    return _pcall(
        body, name=name,
        grid_spec=pltpu.PrefetchScalarGridSpec(
            num_scalar_prefetch=2, grid=(HEADS // hps, qi_tab.shape[0]),
            in_specs=[pl.BlockSpec((tq, pair), q_blk), pl.BlockSpec((tk, pair), k_blk), pl.BlockSpec((tk, pair), k_blk),
                      pl.BlockSpec((tq, pair), q_blk), stat, stat],
            out_specs=[pl.BlockSpec((T, pair), lambda j, t, qi_ref, ki_ref: (0, j)),
                       pl.BlockSpec((tk, pair), k_blk), pl.BlockSpec((tk, pair), k_blk)],
            scratch_shapes=[pltpu.VMEM((hps, tk, LANES), f32)] * 2),
        out_shape=[jax.ShapeDtypeStruct((T, FOX_PAD), f32)] * 3,
        compiler_params=_params(2),
    )(qi_tab, ki_tab, qa, ka, va, doa, lse, drep)


def _fox_bwd_post(dqa, dka, fgb, *, name, tm=512):
    T = fgb.shape[0]
    tm = min(tm, T)
    nt = T // tm

    def body(dqa_ref, dka_ref, fg_ref, dfg_ref, dbf_ref, carry):
        i = pl.program_id(0)

        @pl.when(i == 0)
        def _():
            carry[...] = jnp.zeros_like(carry)
            dbf_ref[...] = jnp.zeros_like(dbf_ref)

        lane = lax.broadcasted_iota(jnp.int32, (tm, LANES), 1)
        dc = jnp.zeros((tm, LANES), f32)
        for h in range(HEADS):
            row_sum = dqa_ref[:, h * LANES + AUX:h * LANES + AUX + 1]
            col_sum = dka_ref[:, h * LANES + AUX + 3:h * LANES + AUX + 4]
            dc = jnp.where(lane == h, jnp.broadcast_to(row_sum - col_sum, (tm, LANES)), dc)
        r = lax.broadcasted_iota(jnp.int32, (tm, tm), 0)
        c = lax.broadcasted_iota(jnp.int32, (tm, tm), 1)
        tri = jnp.where(c >= r, 1.0, 0.0).astype(jnp.bfloat16)
        dls = _tri_dot(tri, dc) + carry[0:1, :]
        carry[...] = jnp.broadcast_to(dls[0:1, :], carry.shape)
        dfg = dls * _sigmoid(-fg_ref[...])
        dfg_ref[...] = dfg
        dbf_ref[...] += jnp.sum(dfg, axis=0, keepdims=True)

    rev = lambda i: (nt - 1 - i, 0)
    return _pcall(
        body, name=name, grid=(nt,),
        in_specs=[pl.BlockSpec((tm, FOX_PAD), rev), pl.BlockSpec((tm, FOX_PAD), rev), pl.BlockSpec((tm, LANES), rev)],
        out_specs=[pl.BlockSpec((tm, LANES), rev), pl.BlockSpec((1, LANES), lambda i: (0, 0))],
        out_shape=[jax.ShapeDtypeStruct((T, LANES), f32), jax.ShapeDtypeStruct((1, LANES), f32)],
        scratch_shapes=[pltpu.VMEM((8, LANES), f32)],
        compiler_params=_params(1),
    )(dqa, dka, fgb)


GELU_C = math.sqrt(2.0 / math.pi)
GELU_A = 0.044715


def _gelu(x):
    t = jnp.tanh(GELU_C * (x + GELU_A * x * x * x))
    return 0.5 * x * (1.0 + t), t


def _gelu_grad(x, t):
    return 0.5 * (1.0 + t) + 0.5 * x * (1.0 - t * t) * GELU_C * (1.0 + 3.0 * GELU_A * x * x)


EXPM1_SERIES_BELOW = 0.25


def _expm1(x, e):
    series = x * (1.0 + x * (1 / 2 + x * (1 / 6 + x * (1 / 24 + x * (1 / 120 + x * (1 / 720))))))
    return jnp.where(x > -EXPM1_SERIES_BELOW, series, e - 1.0)


def _lru_gates(u, wab_ref, bab_ref, lam_ref):
    pre = _dot(u.astype(MXU_DTYPE), wab_ref[...]) + bab_ref[...]
    r = _sigmoid(pre[:, :LRU_W])
    gi = _sigmoid(pre[:, LRU_W:])
    lam = lam_ref[...]
    sp = jnp.maximum(-lam, 0.0) + jnp.log(1.0 + jnp.exp(-jnp.abs(lam)))
    log_a = -LRU_C * r * sp
    a = jnp.exp(log_a)
    s = jnp.sqrt(-_expm1(2.0 * log_a, a * a))
    return r, gi, sp, a, s


def _lru_fwd(lxg, conv_w, conv_b, wab, bab, lam, *, name, tc=512):
    T = lxg.shape[0]
    tc = min(tc, T)
    nc = T // tc

    def body(lx_ref, lg_ref, cw_ref, cb_ref, wab_ref, bab_ref, lam_ref,
             out_ref, u_ref, hs_ref, ext, a_sc, b_sc, h_sc):
        i = pl.program_id(0)

        @pl.when(i == 0)
        def _():
            ext[0:8, :] = jnp.zeros((8, LRU_W), f32)
            h_sc[...] = jnp.zeros_like(h_sc)

        ext[8:, :] = lx_ref[...]
        u = cb_ref[...] + cw_ref[0:1, :] * ext[pl.ds(5, tc), :]
        for k in range(1, CONV_K):
            u = u + cw_ref[k:k + 1, :] * ext[pl.ds(5 + k, tc), :]
        ext[0:8, :] = ext[tc:tc + 8, :]
        u_ref[...] = u
        r, gi, sp, a, s = _lru_gates(u, wab_ref, bab_ref, lam_ref)
        a_sc[...] = a
        b_sc[...] = s * (gi * u)

        def step(t, h):
            h = a_sc[pl.ds(t, 1), :] * h + b_sc[pl.ds(t, 1), :]
            hs_ref[pl.ds(t, 1), :] = h
            return h

        h = lax.fori_loop(0, tc, step, h_sc[0:1, :], unroll=8)
        h_sc[...] = jnp.broadcast_to(h, h_sc.shape)
        gel, _ = _gelu(lg_ref[...])
        out_ref[...] = gel * hs_ref[...]

    row = lambda i: (i, 0)
    const = lambda i: (0, 0)
    return _pcall(
        body, name=name, grid=(nc,),
        in_specs=[pl.BlockSpec((tc, LRU_W), row), pl.BlockSpec((tc, LRU_W), lambda i: (i, 1)),
                  pl.BlockSpec((CONV_K, LRU_W), const), pl.BlockSpec((1, LRU_W), const),
                  pl.BlockSpec((LRU_W, 2 * LRU_W), const), pl.BlockSpec((1, 2 * LRU_W), const),
                  pl.BlockSpec((1, LRU_W), const)],
        out_specs=[pl.BlockSpec((tc, LRU_W), row)] * 3,
        out_shape=[jax.ShapeDtypeStruct((T, LRU_W), f32)] * 3,
        scratch_shapes=[pltpu.VMEM((tc + 8, LRU_W), f32), pltpu.VMEM((tc, LRU_W), f32),
                        pltpu.VMEM((tc, LRU_W), f32), pltpu.VMEM((8, LRU_W), f32)],
        compiler_params=_params(1),
    )(lxg, lxg, conv_w, conv_b, wab, bab, lam)


def _lru_bwd(dlru, lxg, u, hs, conv_w, wab, bab, lam, *, name, tc=512):
    T = lxg.shape[0]
    tc = min(tc, T)
    nc = T // tc
    bp = tc // 8

    def body(dl_ref, lx_ref, lxp_ref, lg_ref, u_ref, hs_ref, hsp_ref, cw_ref, wab_ref, bab_ref, lam_ref,
             dlxg_ref, dwab_ref, dbab_ref, dcw_ref, dcb_ref, dlam_ref,
             dh_sc, a_sc, ext, du_ext, carry):
        i = pl.program_id(0)
        first_chunk = i == nc - 1

        @pl.when(i == 0)
        def _():
            dwab_ref[...] = jnp.zeros_like(dwab_ref)
            dbab_ref[...] = jnp.zeros_like(dbab_ref)
            dcw_ref[...] = jnp.zeros_like(dcw_ref)
            dcb_ref[...] = jnp.zeros_like(dcb_ref)
            dlam_ref[...] = jnp.zeros_like(dlam_ref)
            carry[...] = jnp.zeros_like(carry)
            du_ext[tc:tc + 8, :] = jnp.zeros((8, LRU_W), f32)

        lg = lg_ref[...]
        gel, th = _gelu(lg)
        dl = dl_ref[...]
        hs = hs_ref[...]
        dlg = dl * hs * _gelu_grad(lg, th)
        u = u_ref[...]
        r, gi, sp, a, s = _lru_gates(u, wab_ref, bab_ref, lam_ref)
        a_sc[...] = a
        dh_sc[...] = dl * gel

        def step(k, c):
            t = tc - 1 - k
            dh = dh_sc[pl.ds(t, 1), :] + c
            dh_sc[pl.ds(t, 1), :] = dh
            return a_sc[pl.ds(t, 1), :] * dh

        c = lax.fori_loop(0, tc, step, carry[0:1, :], unroll=8)
        carry[...] = jnp.broadcast_to(c, carry.shape)

        ext[0:8, :] = jnp.where(first_chunk, 0.0, hsp_ref[...])
        ext[8:, :] = hs
        hprev = ext[pl.ds(7, tc), :]
        dh = dh_sc[...]
        da = dh * hprev
        giu = gi * u
        dla = da * a - (dh * giu) * (a * a / s)
        dgi = dh * s * u
        du = dh * s * gi
        dr = dla * (-LRU_C * sp)
        dlam_ref[...] += jnp.sum(dla * (-LRU_C * r), axis=0, keepdims=True) * (-_sigmoid(-lam_ref[...]))
        dpre = jnp.concatenate([dr * r * (1.0 - r), dgi * gi * (1.0 - gi)], axis=1)
        dpre_b = dpre.astype(MXU_DTYPE)
        du = du + _dot_nt(dpre_b, wab_ref[...])
        dwab_ref[...] += _dot_tn(u.astype(MXU_DTYPE), dpre_b)
        dbab_ref[...] += jnp.sum(dpre, axis=0, keepdims=True)
        dcb_ref[...] += jnp.sum(du, axis=0, keepdims=True)

        du_ext[0:tc, :] = du
        dlx = cw_ref[0:1, :] * du_ext[pl.ds(3, tc), :]
        for k in range(1, CONV_K):
            dlx = dlx + cw_ref[k:k + 1, :] * du_ext[pl.ds(3 - k, tc), :]
        du_ext[tc:tc + 8, :] = du_ext[0:8, :]
        ext[0:8, :] = jnp.where(first_chunk, 0.0, lxp_ref[...])
        ext[8:, :] = lx_ref[...]
        for k in range(CONV_K):
            dcw_ref[k:k + 1, :] += jnp.sum(du * ext[pl.ds(5 + k, tc), :], axis=0, keepdims=True)
        dlxg_ref[:, :LRU_W] = dlx.astype(dlxg_ref.dtype)
        dlxg_ref[:, LRU_W:] = dlg.astype(dlxg_ref.dtype)

    rev = lambda i: (nc - 1 - i, 0)
    prev8 = lambda i: (jnp.maximum((nc - 1 - i) * bp - 1, 0), 0)
    const = lambda i: (0, 0)
    return _pcall(
        body, name=name, grid=(nc,),
        in_specs=[
            pl.BlockSpec((tc, LRU_W), rev),
            pl.BlockSpec((tc, LRU_W), rev),
            pl.BlockSpec((8, LRU_W), prev8),
            pl.BlockSpec((tc, LRU_W), lambda i: (nc - 1 - i, 1)),
            pl.BlockSpec((tc, LRU_W), rev),
            pl.BlockSpec((tc, LRU_W), rev),
            pl.BlockSpec((8, LRU_W), prev8),
            pl.BlockSpec((CONV_K, LRU_W), const),
            pl.BlockSpec((LRU_W, 2 * LRU_W), const),
            pl.BlockSpec((1, 2 * LRU_W), const),
            pl.BlockSpec((1, LRU_W), const),
        ],
        out_specs=[
            pl.BlockSpec((tc, 2 * LRU_W), rev),
            pl.BlockSpec((LRU_W, 2 * LRU_W), const),
            pl.BlockSpec((1, 2 * LRU_W), const),
            pl.BlockSpec((8, LRU_W), const),
            pl.BlockSpec((1, LRU_W), const),
            pl.BlockSpec((1, LRU_W), const),
        ],
        out_shape=[
            jax.ShapeDtypeStruct((T, 2 * LRU_W), MXU_DTYPE),
            jax.ShapeDtypeStruct((LRU_W, 2 * LRU_W), f32),
            jax.ShapeDtypeStruct((1, 2 * LRU_W), f32),
            jax.ShapeDtypeStruct((8, LRU_W), f32),
            jax.ShapeDtypeStruct((1, LRU_W), f32),
            jax.ShapeDtypeStruct((1, LRU_W), f32),
        ],
        scratch_shapes=[pltpu.VMEM((tc, LRU_W), f32), pltpu.VMEM((tc, LRU_W), f32),
                        pltpu.VMEM((tc + 8, LRU_W), f32), pltpu.VMEM((tc + 8, LRU_W), f32),
                        pltpu.VMEM((8, LRU_W), f32)],
        compiler_params=_params(1),
    )(dlru, lxg, lxg, lxg, u, hs, hs, conv_w, wab, bab, lam)


def _mix_out(fox, lru, wo, xhat1, g1, b1, g2, b2, *, name, tm=512):
    T = fox.shape[0]
    tm = min(tm, T)
    nt = T // tm

    def body(fox_ref, lru_ref, wo_ref, xh_ref, g1_ref, b1_ref, g2_ref, b2_ref, xhat_ref, xn_ref, rstd_ref):
        mix = _dot(fox_ref[...].astype(MXU_DTYPE), wo_ref[:FOX_W, :])
        mix = mix + _dot(lru_ref[...].astype(MXU_DTYPE), wo_ref[FOX_W:, :])
        x1 = xh_ref[...] * g1_ref[...] + b1_ref[...]
        xhat, rstd = _layer_norm_stats(DN_ALPHA * x1 + mix)
        xhat_ref[...] = xhat
        xn_ref[...] = xhat * g2_ref[...] + b2_ref[...]
        rstd_ref[...] = jnp.broadcast_to(rstd, rstd_ref.shape)

    row = lambda i: (i, 0)
    const = lambda i: (0, 0)
    vec = pl.BlockSpec((1, D_MODEL), const)
    return _pcall(
        body, name=name, grid=(nt,),
        in_specs=[pl.BlockSpec((tm, FOX_W), row), pl.BlockSpec((tm, LRU_W), row),
                  pl.BlockSpec((D_MODEL, D_MODEL), const), pl.BlockSpec((tm, D_MODEL), row), vec, vec, vec, vec],
        out_specs=[pl.BlockSpec((tm, D_MODEL), row), pl.BlockSpec((tm, D_MODEL), row),
                   pl.BlockSpec((tm, LANES), row)],
        out_shape=[jax.ShapeDtypeStruct((T, D_MODEL), f32), jax.ShapeDtypeStruct((T, D_MODEL), f32),
                   jax.ShapeDtypeStruct((T, LANES), f32)],
        compiler_params=_params(1),
    )(fox, lru, wo, xhat1, g1, b1, g2, b2)


def _mix_out_bwd(dy, xhat, rstd, ln_g, fox, lru, wo, *, name, tm=512):
    T = fox.shape[0]
    tm = min(tm, T)
    nt = T // tm

    def body(dy_ref, xhat_ref, rstd_ref, g_ref, fox_ref, lru_ref, wo_ref,
             dyp_ref, dgam_ref, dbeta_ref, dfox_ref, dlru_ref, dwo_ref):
        i = pl.program_id(0)

        @pl.when(i == 0)
        def _():
            dwo_ref[...] = jnp.zeros_like(dwo_ref)
            dgam_ref[...] = jnp.zeros_like(dgam_ref)
            dbeta_ref[...] = jnp.zeros_like(dbeta_ref)

        dyp, dgam, dbeta = _ln_backward(dy_ref[...], xhat_ref[...], rstd_ref[:, 0:1], g_ref[...])
        dyp_ref[...] = dyp
        dgam_ref[...] += dgam
        dbeta_ref[...] += dbeta
        dmix = dyp.astype(MXU_DTYPE)
        dcat = _dot_nt(dmix, wo_ref[...])
        dfox_ref[...] = dcat[:, :FOX_W].astype(dfox_ref.dtype)
        dlru_ref[...] = dcat[:, FOX_W:]
        dwo_ref[:FOX_W, :] += _dot_tn(fox_ref[...].astype(MXU_DTYPE), dmix)
        dwo_ref[FOX_W:, :] += _dot_tn(lru_ref[...].astype(MXU_DTYPE), dmix)

    row = lambda i: (i, 0)
    const = lambda i: (0, 0)
    return _pcall(
        body, name=name, grid=(nt,),
        in_specs=[pl.BlockSpec((tm, D_MODEL), row), pl.BlockSpec((tm, D_MODEL), row), pl.BlockSpec((tm, LANES), row),
                  pl.BlockSpec((1, D_MODEL), const),
                  pl.BlockSpec((tm, FOX_W), row), pl.BlockSpec((tm, LRU_W), row),
                  pl.BlockSpec((D_MODEL, D_MODEL), const)],
        out_specs=[pl.BlockSpec((tm, D_MODEL), row), pl.BlockSpec((1, D_MODEL), const), pl.BlockSpec((1, D_MODEL), const),
                   pl.BlockSpec((tm, FOX_W), row), pl.BlockSpec((tm, LRU_W), row),
                   pl.BlockSpec((D_MODEL, D_MODEL), const)],
        out_shape=[jax.ShapeDtypeStruct((T, D_MODEL), f32), jax.ShapeDtypeStruct((1, D_MODEL), f32),
                   jax.ShapeDtypeStruct((1, D_MODEL), f32),
                   jax.ShapeDtypeStruct((T, FOX_W), MXU_DTYPE), jax.ShapeDtypeStruct((T, LRU_W), f32),
                   jax.ShapeDtypeStruct((D_MODEL, D_MODEL), f32)],
        compiler_params=_params(1),
    )(dy, xhat, rstd, ln_g, fox, lru, wo)


def make_wp(w_in):
    scale = jnp.concatenate([jnp.full((FOX_W,), 1.0 / math.sqrt(HEAD_DIM), w_in.dtype),
                             jnp.ones((IN_COLS - FOX_W,), w_in.dtype)])
    return jnp.pad(w_in * scale[None, :], ((0, 0), (0, Z_PAD - IN_COLS)))


def _block_diag(w):
    eye = jnp.eye(HEADS, dtype=w.dtype)
    return jnp.einsum("hij,hg->higj", w, eye).reshape(LRU_W, LRU_W)


def _block_diag_extract(m):
    m4 = m.reshape(HEADS, HEAD_DIM, HEADS, HEAD_DIM)
    return jnp.stack([m4[h, :, h, :] for h in range(HEADS)])


class _NoOverlap:
    def start_token(self):
        return None

    def late_weights(self, w, after):
        return dict(f1d=w["f1d"], wp=w["wp"], wo=w["wo"])

    def after_attention(self, after):
        return None

    def ffn2_weights(self, w, after):
        return w["f2g"], w["f2u"], w["f2d"]

    def ffn2_grads(self, grads):
        return None

    def ffn1_grads(self, grads):
        return None

    def mixer_grads(self, dwp, dwo, small, loss):
        return None

    def before_ffn1_bwd(self, after):
        return None


def _tied(a, token):
    return a if token is None else a + token[0, 0]


def _local_step(x, target, w, hooks=None):
    hooks = hooks or _NoOverlap()
    bfp = w["bfp"]
    wab = jnp.concatenate([_block_diag(w["rg_wa"]), _block_diag(w["rg_wx"])], axis=1).astype(MXU_DTYPE)
    bab = jnp.concatenate([w["rg_ba"].reshape(1, LRU_W), w["rg_bx"].reshape(1, LRU_W)], axis=1)

    xb0, g1a, u1a, h1a = _ffn_up(x, w["f1g"], w["f1u"], hooks.start_token(), name="ffn1_up")
    late = hooks.late_weights(w, [h1a])
    f1d, wp, wo = late["f1d"], late["wp"], late["wo"]
    xhat1, xn1, rstd1 = _ffn_down_ln(x, h1a, f1d, w["ln1_g"], w["ln1_b"], name="ffn1_down")
    qkv, lxg, fgb = _proj_in(xn1, wp, bfp, name="proj_in")
    qa, ka, va = _fox_prep(qkv, fgb, name="fox_prep")
    fox, lse = _fox_fwd(qa, ka, va, name="fox_fwd")
    token = hooks.after_attention([lse])
    lru, uconv, hs = _lru_fwd(lxg, w["conv_w"], _tied(w["conv_b"], token), wab, bab, w["lam"], name="lru_fwd")
    xhat2, x2, rstd2 = _mix_out(fox, lru, wo, xhat1, w["ln1_g"], w["ln1_b"], w["ln2_g"], w["ln2_b"], name="mix_out")
    f2g, f2u, f2d = hooks.ffn2_weights(w, [rstd2])
    xb2, g2a, u2a, xhat3, _, rstd3 = _ffn_fwd(x2, f2g, f2u, f2d, w["ln3_g"], w["ln3_b"], name="ffn2_fwd")

    dy3p, dln3g, dln3b, loss = _loss_ln_bwd(xhat3, rstd3, w["ln3_g"], w["ln3_b"], target, name="loss_ln3_bwd")
    dx2, df2g, df2u, df2d = _ffn_bwd(dy3p, xb2, g2a, u2a, f2g, f2u, f2d, name="ffn2_bwd")
    token = hooks.ffn2_grads([df2g, df2u, df2d])
    dy2p, dln2g, dln2b, dfox, dlru, dwo = _mix_out_bwd(dx2, xhat2, rstd2, _tied(w["ln2_g"], token), fox, lru, wo,
                                                       name="mix_out_bwd")
    dlxg, dwab, dbab, dcw, dcb, dlam = _lru_bwd(dlru, lxg, uconv, hs, w["conv_w"], wab, bab, w["lam"], name="lru_bwd")
    drep, doa = _fox_bwd_prep(dfox, fox, name="fox_bwd_prep")
    dqa, dka, dva = _fox_bwd(qa, ka, va, doa, lse, drep, name="fox_bwd")
    dfg, dbf = _fox_bwd_post(dqa, dka, fgb, name="fox_bwd_post")
    dy1p, dwp, dln1g, dln1b = _proj_in_bwd(dqa, dka, dva, dlxg, dfg, xn1, dy2p, wp, xhat1, rstd1, w["ln1_g"],
                                           name="proj_in_bwd")
    small = dict(
        ln1_g=dln1g, ln1_b=dln1b, ln2_g=dln2g, ln2_b=dln2b, ln3_g=dln3g, ln3_b=dln3b,
        b_forget=dbf[:, :HEADS], conv_w=dcw[:CONV_K], conv_b=dcb,
        rg_wa=_block_diag_extract(dwab[:, :LRU_W]), rg_wx=_block_diag_extract(dwab[:, LRU_W:]),
        rg_ba=dbab[:, :LRU_W].reshape(HEADS, HEAD_DIM), rg_bx=dbab[:, LRU_W:].reshape(HEADS, HEAD_DIM),
        lru_lambda=dlam,
    )
    hooks.before_ffn1_bwd([dln1b])
    token = hooks.mixer_grads(dwp, dwo, small, loss)
    dx_a, *grads_a = _ffn_bwd(dy1p, xb0, g1a, u1a, w["f1g"], w["f1u"], f1d, token, name="ffn1_bwd_a", part=0)
    token = hooks.ffn1_grads(grads_a)
    dx, *grads_b = _ffn_bwd(dy1p, xb0, g1a, u1a, w["f1g"], w["f1u"], f1d, token, name="ffn1_bwd_b", part=1,
                            dx_init=dx_a)

    grads = dict(f1=(grads_a, grads_b), f2g=df2g, f2u=df2u, f2d=df2d, wp=dwp, wo=dwo, **small)
    return loss, dx, grads


MESH = pl.DeviceIdType.MESH
HBM_SPEC = pl.BlockSpec(memory_space=pl.ANY)
VMEM_SPEC = pl.BlockSpec(memory_space=pltpu.VMEM)


def _position():
    return lax.axis_index("x"), lax.axis_index("y"), lax.axis_index("c")


def _other_chips(x, y):
    return [(1 - x, y), (x, 1 - y), (1 - x, 1 - y)]


def _all_gather_bf16(shards, *, name):
    n = len(shards)

    def body(*refs):
        ins, outs, stages = refs[:n], refs[n:2 * n], refs[2 * n:3 * n]
        send_sems, recv_sems, local_sems = refs[3 * n:]
        x, y, c = _position()
        me, sibling = (x, y, c), (x, y, 1 - c)
        chips = _other_chips(x, y)

        def rows(k, px, py, pc):
            r = shards[k].shape[0]
            m = r // 2
            return outs[k].at[pl.ds(pl.multiple_of((2 * px + py) * r + pc * m, 16), m), :]

        def copy(k, idx, block, to, src=None):
            return pltpu.make_async_remote_copy(
                src_ref=rows(k, *block) if src is None else src, dst_ref=rows(k, *block),
                send_sem=send_sems.at[7 * k + idx], recv_sem=recv_sems.at[7 * k + idx],
                device_id=to, device_id_type=MESH)

        started = []
        mine = []
        for k in range(n):
            m = shards[k].shape[0] // 2
            stages[k][...] = ins[k][pl.ds(pl.multiple_of(c * m, 16), m), :].astype(stages[k].dtype)
            cp = pltpu.make_async_copy(stages[k], rows(k, *me), local_sems.at[k])
            cp.start()
            mine.append(cp)
            first = [copy(k, 0, me, sibling, src=stages[k])]
            first += [copy(k, 1 + j, me, (*chip, c), src=stages[k]) for j, chip in enumerate(chips)]
            for cp in first:
                cp.start()
            started += first
        for k in range(n):
            for j, chip in enumerate(chips):
                copy(k, 1 + j, (*chip, c), me).wait_recv()
                fwd = copy(k, 4 + j, (*chip, c), sibling)
                fwd.start()
                started.append(fwd)
        for k in range(n):
            copy(k, 0, sibling, me).wait_recv()
            for j, chip in enumerate(chips):
                copy(k, 4 + j, (*chip, 1 - c), me).wait_recv()
        for cp in started:
            cp.wait_send()
        for cp in mine:
            cp.wait()

    return _pcall(
        body, name=name,
        in_specs=[VMEM_SPEC] * n, out_specs=[HBM_SPEC] * n,
        out_shape=[jax.ShapeDtypeStruct((N_SHARD * s.shape[0], s.shape[1]), MXU_DTYPE) for s in shards],
        scratch_shapes=[pltpu.VMEM((s.shape[0] // 2, s.shape[1]), MXU_DTYPE) for s in shards]
        + [pltpu.SemaphoreType.DMA((7 * n,)), pltpu.SemaphoreType.DMA((7 * n,)), pltpu.SemaphoreType.DMA((n,))],
        compiler_params=pltpu.CompilerParams(vmem_limit_bytes=VMEM_LIMIT),
    )(*shards)


def _swap_halves(gs, *, name):
    n = len(gs)

    def body(*refs):
        ins, outs = refs[:n], refs[n:2 * n]
        send_sems, recv_sems = refs[2 * n:]
        x, y, c = _position()
        cps = []
        for k in range(n):
            m = gs[k].shape[1] // 2
            src = ins[k].at[:, pl.ds(pl.multiple_of((1 - c) * m, 16), m), :]
            cp = pltpu.make_async_remote_copy(src_ref=src, dst_ref=outs[k], send_sem=send_sems.at[k],
                                              recv_sem=recv_sems.at[k], device_id=(x, y, 1 - c), device_id_type=MESH)
            cp.start()
            cps.append(cp)
        for cp in cps:
            cp.wait()

    return _pcall(
        body, name=name, in_specs=[HBM_SPEC] * n, out_specs=[HBM_SPEC] * n,
        out_shape=[jax.ShapeDtypeStruct((g.shape[0], g.shape[1] // 2, g.shape[2]), g.dtype) for g in gs],
        scratch_shapes=[pltpu.SemaphoreType.DMA((n,)), pltpu.SemaphoreType.DMA((n,))],
    )(*gs)


def _add_halves(gs, recvs, *, name, tm=256):
    n = len(gs)
    _, r, cdim = gs[0].shape
    m = r // 2
    tm = min(tm, m)
    nb = m // tm
    c_idx = lax.axis_index("c").astype(jnp.int32).reshape(1)

    def body(c_ref, *refs):
        for k in range(n):
            refs[2 * n + k][...] = (refs[k][...].astype(f32) + refs[n + k][...].astype(f32)).astype(refs[2 * n + k].dtype)

    mine = pl.BlockSpec((None, tm, cdim), lambda j, i, c_ref: (j, c_ref[0] * nb + i, 0))
    half = pl.BlockSpec((None, tm, cdim), lambda j, i, c_ref: (j, i, 0))
    return _pcall(
        body, name=name,
        grid_spec=pltpu.PrefetchScalarGridSpec(
            num_scalar_prefetch=1, grid=(N_SHARD, nb),
            in_specs=[mine] * n + [half] * n, out_specs=[half] * n),
        out_shape=[jax.ShapeDtypeStruct((N_SHARD, m, cdim), g.dtype) for g in gs],
        compiler_params=_params(2),
    )(c_idx, *gs, *recvs)


def _scatter_partials(ps, *, name):
    n = len(ps)

    def body(*refs):
        ins, outs = refs[:n], refs[n:2 * n]
        send_sems, recv_sems = refs[2 * n:]
        x, y, c = _position()
        me_chip = 2 * x + y
        cps = []
        for k in range(n):
            for j, (px, py) in enumerate(_other_chips(x, y)):
                cp = pltpu.make_async_remote_copy(
                    src_ref=ins[k].at[2 * px + py], dst_ref=outs[k].at[me_chip],
                    send_sem=send_sems.at[3 * k + j], recv_sem=recv_sems.at[3 * k + j],
                    device_id=(px, py, c), device_id_type=MESH)
                cp.start()
                cps.append(cp)
        for cp in cps:
            cp.wait()

    return _pcall(
        body, name=name, in_specs=[HBM_SPEC] * n, out_specs=[HBM_SPEC] * n,
        out_shape=[jax.ShapeDtypeStruct(p.shape, p.dtype) for p in ps],
        scratch_shapes=[pltpu.SemaphoreType.DMA((3 * n,)), pltpu.SemaphoreType.DMA((3 * n,))],
    )(*ps)


def _sum_slabs(ps, qs, *, name, tm=128):
    n = len(qs)
    _, m, cdim = qs[0].shape
    tm = min(tm, m)
    nb = m // tm
    assert m % tm == 0, (m, tm)
    where = jnp.stack([2 * lax.axis_index("x") + lax.axis_index("y"), lax.axis_index("c")]).astype(jnp.int32)

    def body(w_ref, *refs):
        for k in range(n):
            own, q1, q2, q3 = (refs[4 * k + t][...].astype(f32) for t in range(4))
            refs[4 * n + k][...] = ((own + q1) + q2) + q3

    def slab(flip):
        return pl.BlockSpec((None, tm, cdim), lambda i, w_ref: (jnp.bitwise_xor(w_ref[0], flip), i, 0))

    operands = []
    for p, q in zip(ps, qs):
        operands += [p, q, q, q]
    return _pcall(
        body, name=name,
        grid_spec=pltpu.PrefetchScalarGridSpec(
            num_scalar_prefetch=1, grid=(nb,),
            in_specs=[slab(0), slab(2), slab(1), slab(3)] * n,
            out_specs=[pl.BlockSpec((tm, cdim), lambda i, w_ref: (w_ref[1] * nb + i, 0))] * n),
        out_shape=[jax.ShapeDtypeStruct((2 * m, cdim), f32) for _ in qs],
        compiler_params=_params(1),
    )(where, *operands)


def _join_halves(fs, *, name):
    n = len(fs)

    def body(*refs):
        outs = refs[n:2 * n]
        send_sems, recv_sems = refs[2 * n:]
        x, y, c = _position()
        cps = []
        for k in range(n):
            m = fs[k].shape[0] // 2
            half = outs[k].at[pl.ds(pl.multiple_of(c * m, 8), m), :]
            cp = pltpu.make_async_remote_copy(src_ref=half, dst_ref=half, send_sem=send_sems.at[k],
                                              recv_sem=recv_sems.at[k], device_id=(x, y, 1 - c), device_id_type=MESH)
            cp.start()
            cps.append(cp)
        for cp in cps:
            cp.wait()

    return _pcall(
        body, name=name, in_specs=[HBM_SPEC] * n, out_specs=[HBM_SPEC] * n,
        out_shape=[jax.ShapeDtypeStruct(f.shape, f.dtype) for f in fs],
        input_output_aliases={k: k for k in range(n)},
        scratch_shapes=[pltpu.SemaphoreType.DMA((n,)), pltpu.SemaphoreType.DMA((n,))],
    )(*fs)


def _all_reduce_small(v, after=None, *, name):
    r = v.shape[0]
    extra = [] if after is None else [after]

    def body(v_ref, *refs):
        out_ref, buf, send_sems, recv_sems, local_sem = refs[len(extra):]
        x, y, c = _position()
        me, sibling = (x, y, c), (x, y, 1 - c)
        chips = _other_chips(x, y)

        def rows(px, py, pc):
            return buf.at[pl.ds(pl.multiple_of((4 * px + 2 * py + pc) * r, 8), r), :]

        def copy(k, block, to, src=None):
            return pltpu.make_async_remote_copy(
                src_ref=rows(*block) if src is None else src, dst_ref=rows(*block),
                send_sem=send_sems.at[k], recv_sem=recv_sems.at[k], device_id=to, device_id_type=MESH)

        mine = pltpu.make_async_copy(v_ref, rows(*me), local_sem)
        mine.start()
        first = [copy(0, me, sibling, src=v_ref)]
        first += [copy(1 + j, me, (*chip, c), src=v_ref) for j, chip in enumerate(chips)]
        for cp in first:
            cp.start()
        passed = [copy(4 + j, (*chip, c), sibling) for j, chip in enumerate(chips)]
        for j, chip in enumerate(chips):
            copy(1 + j, (*chip, c), me).wait_recv()
            passed[j].start()
        copy(0, sibling, me).wait_recv()
        for j, chip in enumerate(chips):
            copy(4 + j, (*chip, 1 - c), me).wait_recv()
        for cp in first + passed:
            cp.wait_send()
        mine.wait()
        acc = buf[0:r, :]
        for d in range(1, N_DEV):
            acc = acc + buf[d * r:(d + 1) * r, :]
        out_ref[...] = acc

    return _pcall(
        body, name=name, in_specs=[VMEM_SPEC] + [HBM_SPEC] * len(extra), out_specs=VMEM_SPEC,
        out_shape=jax.ShapeDtypeStruct((r, LANES), f32),
        scratch_shapes=[pltpu.VMEM((N_DEV * r, LANES), f32), pltpu.SemaphoreType.DMA((7,)),
                        pltpu.SemaphoreType.DMA((7,)), pltpu.SemaphoreType.DMA],
    )(v, *extra)


SEM_SPEC = pl.BlockSpec(memory_space=pltpu.SEMAPHORE)
HBM_ONLY = pl.BlockSpec(memory_space=pltpu.HBM)
EFFECT = pltpu.SideEffectType.DATAFLOW_SIDE_EFFECTING


def _sends(copies):
    return copies[0] if isinstance(copies, tuple) else copies


def _arrivals(copies):
    return copies[1] if isinstance(copies, tuple) else copies


def _split_start(bufs, copies_fn, n_sems, *, name):
    n = len(bufs)

    def body(*refs):
        send_sems, recv_sems = refs[n], refs[n + 1]
        thru = refs[n + 2:2 * n + 2]
        token = refs[2 * n + 2]
        for cp in _sends(copies_fn(thru, send_sems, recv_sems)):
            cp.start()
        token[...] = jnp.zeros_like(token)

    outs = _pcall(
        body, name=name,
        out_shape=(pltpu.SemaphoreType.DMA((n_sems,)), pltpu.SemaphoreType.DMA((n_sems,)),
                   *[pltpu.HBM(b.shape, b.dtype) for b in bufs], jax.ShapeDtypeStruct((8, LANES), f32)),
        in_specs=[HBM_ONLY] * n,
        out_specs=(SEM_SPEC, SEM_SPEC, *[HBM_ONLY] * n, VMEM_SPEC),
        input_output_aliases={k: 2 + k for k in range(n)},
        compiler_params=pltpu.CompilerParams(has_side_effects=EFFECT),
    )(*[pltpu.with_memory_space_constraint(b, pltpu.HBM) for b in bufs])
    return outs[0], outs[1], list(outs[2:2 + n]), outs[2 + n]


def _split_wait(thru, send_sems, recv_sems, after, copies_fn, *, name):
    n = len(thru)

    def body(*refs):
        copies = copies_fn(refs[:n], refs[n], refs[n + 1])
        for cp in _sends(copies):
            cp.wait_send()
        for cp in _arrivals(copies):
            cp.wait_recv()

    return list(_pcall(
        body, name=name,
        out_shape=tuple(pltpu.HBM(b.shape, b.dtype) for b in thru),
        in_specs=[HBM_ONLY] * n + [SEM_SPEC, SEM_SPEC] + [HBM_SPEC] * len(after),
        out_specs=tuple([HBM_ONLY] * n),
        input_output_aliases={k: k for k in range(n)},
        compiler_params=pltpu.CompilerParams(has_side_effects=EFFECT),
    )(*thru, send_sems, recv_sems, *after))


def _scatter_copies(n):
    def copies(bufs, send_sems, recv_sems):
        x, y, c = _position()
        me_chip = 2 * x + y
        cps = []
        for k in range(n):
            for j, (px, py) in enumerate(_other_chips(x, y)):
                cps.append(pltpu.make_async_remote_copy(
                    src_ref=bufs[k].at[2 * px + py], dst_ref=bufs[n + k].at[me_chip],
                    send_sem=send_sems.at[3 * k + j], recv_sem=recv_sems.at[3 * k + j],
                    device_id=(px, py, c), device_id_type=MESH))
        return cps
    return copies


N_PEERS = N_DEV - 1


def _direct_copies(n):
    def copies(bufs, send_sems, recv_sems):
        x, y, c = _position()
        me_chip = 2 * x + y
        sends, arrivals = [], []
        for k in range(n):
            m = bufs[k].shape[1] // 2
            land = bufs[n + k]

            def rows(slab, half, k=k, m=m):
                start = half * m if isinstance(half, int) else pl.multiple_of(half * m, 16)
                return bufs[k].at[slab, pl.ds(start, m), :]

            def copy(src, slot, send_idx, recv_idx, to, k=k, land=land):
                return pltpu.make_async_remote_copy(
                    src_ref=src, dst_ref=land.at[slot], send_sem=send_sems.at[N_PEERS * k + send_idx],
                    recv_sem=recv_sems.at[N_PEERS * k + recv_idx], device_id=to, device_id_type=MESH)

            sends.append(copy(rows(me_chip, 1 - c), 0, 0, 0, (x, y, 1 - c)))
            arrivals.append(copy(rows(me_chip, c), 0, 0, 0, (x, y, 1 - c)))
            for t, (px, py) in enumerate(_other_chips(x, y)):
                for core in range(2):
                    sends.append(copy(rows(2 * px + py, core), 1 + 2 * t + c, 1 + 2 * t + core, 1 + 2 * t + c,
                                      (px, py, core)))
                    arrivals.append(copy(rows(me_chip, c), 1 + 2 * t + core, 1 + 2 * t + core, 1 + 2 * t + core,
                                         (px, py, core)))
        return sends, arrivals
    return copies


def _sum_direct(gs, lands, *, name, tm=128):
    n = len(gs)
    _, m, cdim = lands[0].shape
    tm = min(tm, m)
    nb = m // tm
    assert m % tm == 0, (m, tm)
    where = jnp.stack([2 * lax.axis_index("x") + lax.axis_index("y"), lax.axis_index("c")]).astype(jnp.int32)

    def body(w_ref, *refs):
        for k in range(n):
            acc = refs[2 * k][...].astype(f32)
            for slot in range(N_PEERS):
                acc = acc + refs[2 * k + 1][slot].astype(f32)
            refs[2 * n + k][...] = acc

    own = pl.BlockSpec((None, tm, cdim), lambda i, w_ref: (w_ref[0], w_ref[1] * nb + i, 0))
    landed = pl.BlockSpec((N_PEERS, tm, cdim), lambda i, w_ref: (0, i, 0))
    operands = []
    for g, land in zip(gs, lands):
        operands += [g, land]
    return _pcall(
        body, name=name,
        grid_spec=pltpu.PrefetchScalarGridSpec(
            num_scalar_prefetch=1, grid=(nb,), in_specs=[own, landed] * n,
            out_specs=[pl.BlockSpec((tm, cdim), lambda i, w_ref: (w_ref[1] * nb + i, 0))] * n),
        out_shape=[jax.ShapeDtypeStruct((2 * m, cdim), f32) for _ in gs],
        compiler_params=_params(1),
    )(where, *operands)


def _broadcast_copies(bufs, send_sems, recv_sems):
    v, land = bufs
    x, y, c = _position()

    def copy(slot, send_idx, recv_idx, to):
        return pltpu.make_async_remote_copy(src_ref=v, dst_ref=land.at[slot], send_sem=send_sems.at[send_idx],
                                            recv_sem=recv_sems.at[recv_idx], device_id=to, device_id_type=MESH)

    sends = [copy(0, 0, 0, (x, y, 1 - c))]
    arrivals = [copy(0, 0, 0, (x, y, 1 - c))]
    for t, (px, py) in enumerate(_other_chips(x, y)):
        for core in range(2):
            sends.append(copy(1 + 2 * t + c, 1 + 2 * t + core, 1 + 2 * t + c, (px, py, core)))
            arrivals.append(copy(1 + 2 * t + core, 1 + 2 * t + core, 1 + 2 * t + core, (px, py, core)))
    return sends, arrivals


def _sum_in_device_order(v, land, *, name):
    r, cdim = v.shape
    x, y, c = _position()
    slots, mine = [], []
    for d in range(N_DEV):
        dx, dy, dc = d // 4, (d // 2) % 2, d % 2
        fx, fy = jnp.bitwise_xor(dx, x), jnp.bitwise_xor(dy, y)
        t = jnp.where(fx == 1, jnp.where(fy == 1, 2, 0), 1)
        slots.append(jnp.where(jnp.logical_and(fx == 0, fy == 0), 0, 1 + 2 * t + dc))
        mine.append(jnp.logical_and(jnp.logical_and(fx == 0, fy == 0), dc == c))
    table = jnp.stack(slots + mine).astype(jnp.int32)

    def body(tab_ref, v_ref, *refs):
        out_ref = refs[N_DEV]
        acc = None
        for d in range(N_DEV):
            term = jnp.where(tab_ref[N_DEV + d] == 1, v_ref[...], refs[d][...])
            acc = term if acc is None else acc + term
        out_ref[...] = acc

    whole = pl.BlockSpec((r, cdim), lambda i, tab_ref: (0, 0))
    landed = [pl.BlockSpec((None, r, cdim), functools.partial(lambda i, tab_ref, d: (tab_ref[d], 0, 0), d=d))
              for d in range(N_DEV)]
    return _pcall(
        body, name=name,
        grid_spec=pltpu.PrefetchScalarGridSpec(num_scalar_prefetch=1, grid=(1,), in_specs=[whole] + landed,
                                               out_specs=whole),
        out_shape=jax.ShapeDtypeStruct((r, cdim), f32),
        compiler_params=_params(1),
    )(table, v, *[land] * N_DEV)


def _block_rows(buf, px, py, pc):
    m = buf.shape[0] // N_DEV
    return buf.at[pl.ds(pl.multiple_of((4 * px + 2 * py + pc) * m, 16), m), :]


def _gather_ici_copies(n):
    def copies(bufs, send_sems, recv_sems):
        x, y, c = _position()
        cps = []
        for k in range(n):
            rows = _block_rows(bufs[k], x, y, c)
            targets = [(x, y, 1 - c)] + [(px, py, c) for px, py in _other_chips(x, y)]
            for j, to in enumerate(targets):
                cps.append(pltpu.make_async_remote_copy(
                    src_ref=rows, dst_ref=rows, send_sem=send_sems.at[4 * k + j], recv_sem=recv_sems.at[4 * k + j],
                    device_id=to, device_id_type=MESH))
        return cps
    return copies


def _gather_d2d_copies(n):
    def copies(bufs, send_sems, recv_sems):
        x, y, c = _position()
        cps = []
        for k in range(n):
            for j, (px, py) in enumerate(_other_chips(x, y)):
                rows = _block_rows(bufs[k], px, py, c)
                cps.append(pltpu.make_async_remote_copy(
                    src_ref=rows, dst_ref=rows, send_sem=send_sems.at[3 * k + j], recv_sem=recv_sems.at[3 * k + j],
                    device_id=(x, y, 1 - c), device_id_type=MESH))
        return cps
    return copies


def _cast_halves(shards, after, *, name):
    n = len(shards)
    where = jnp.stack([2 * lax.axis_index("x") + lax.axis_index("y"), lax.axis_index("c")]).astype(jnp.int32)

    def body(w_ref, *refs):
        for k in range(n):
            refs[n + 1 + k][...] = refs[k][...].astype(refs[n + 1 + k].dtype)

    def half(s):
        return (s.shape[0] // 2, s.shape[1])

    return _pcall(
        body, name=name,
        grid_spec=pltpu.PrefetchScalarGridSpec(
            num_scalar_prefetch=1, grid=(1,),
            in_specs=[pl.BlockSpec(half(s), lambda i, w_ref: (w_ref[1], 0)) for s in shards] + [HBM_SPEC],
            out_specs=[pl.BlockSpec(half(s), lambda i, w_ref: (2 * w_ref[0] + w_ref[1], 0)) for s in shards]),
        out_shape=[jax.ShapeDtypeStruct((N_SHARD * s.shape[0], s.shape[1]), MXU_DTYPE) for s in shards],
        compiler_params=_params(1),
    )(where, *shards, after)


class _SplitGather:
    def __init__(self, shards, after, tag):
        self.tag = tag
        self.n = len(shards)
        halves = _cast_halves(shards, after, name=f"{tag}_cast")
        self.ici = _split_start(halves, _gather_ici_copies(self.n), 4 * self.n, name=f"{tag}_ici_start")
        self.token = self.ici[3]

    def forward(self, after):
        send_sems, recv_sems, thru, _ = self.ici
        landed = _split_wait(thru, send_sems, recv_sems, after, _gather_ici_copies(self.n), name=f"{self.tag}_ici_wait")
        self.d2d = _split_start(landed, _gather_d2d_copies(self.n), 3 * self.n, name=f"{self.tag}_d2d_start")
        return self.d2d[3]

    def finish(self, after):
        send_sems, recv_sems, thru, _ = self.d2d
        return _split_wait(thru, send_sems, recv_sems, after, _gather_d2d_copies(self.n), name=f"{self.tag}_d2d_wait")


class _Overlap(_NoOverlap):
    def __init__(self, late_shards, ffn2_shards, after):
        self.late = _SplitGather(late_shards, after, "ag1")
        self.ffn2 = _SplitGather(ffn2_shards, self.late.token, "ag2")
        self.reduced = None
        self.ffn1_parts = []

    def start_token(self):
        return self.ffn2.token

    def late_weights(self, w, after):
        token = self.late.forward(after)
        f1d, w_in, wo = self.late.finish([token])
        w_in = w_in.reshape(N_SHARD, D_MODEL, IN_SHARD).transpose(1, 0, 2).reshape(D_MODEL, IN_COLS)
        return dict(f1d=f1d.reshape(N_SHARD, D_FF // N_SHARD, D_MODEL), wp=make_wp(w_in), wo=wo)

    def after_attention(self, after):
        return self.ffn2.forward(after)

    def ffn2_weights(self, w, after):
        full = self.ffn2.finish(after)
        fs = D_FF // N_SHARD
        return (full[0].reshape(N_SHARD, D_MODEL, fs), full[1].reshape(N_SHARD, D_MODEL, fs),
                full[2].reshape(N_SHARD, fs, D_MODEL))

    @staticmethod
    def _send_direct(grads, tag):
        lands = [lax.empty((N_PEERS, g.shape[1] // 2, g.shape[2]), g.dtype) for g in grads]
        return _split_start(list(grads) + lands, _direct_copies(len(grads)), N_PEERS * len(grads),
                            name=f"rs_direct_{tag}_start")

    def ffn2_grads(self, grads):
        self.scatter = self._send_direct(grads, "ffn2")
        return self.scatter[3]

    def ffn1_grads(self, grads):
        tag = "ffn1" + "ab"[len(self.ffn1_parts)]
        if not self.ffn1_parts:
            started = self._send_direct(grads, tag)
        else:
            recvs = _swap_halves(grads, name=f"rs_swap_{tag}")
            ps = list(_add_halves(grads[:2], recvs[:2], name=f"rs_add_{tag}_gu"))
            ps += list(_add_halves(grads[2:], recvs[2:], name=f"rs_add_{tag}_d"))
            lands = [lax.empty(p.shape, p.dtype) for p in ps]
            started = _split_start(ps + lands, _scatter_copies(3), 9, name=f"rs_scatter_{tag}_start")
        self.ffn1_parts.append((tag, started))
        return started[3]

    def ffn1_reduced(self, after):
        sums = []
        for direct, (tag, (send_sems, recv_sems, thru, _)) in zip((True, False), self.ffn1_parts):
            plan, add = (_direct_copies, _sum_direct) if direct else (_scatter_copies, _sum_slabs)
            done = _split_wait(thru, send_sems, recv_sems, after, plan(3), name=f"rs_{tag}_wait")
            sums += list(add(done[:2], done[3:5], name=f"rs_sum_{tag}_gu"))
            sums += list(add(done[2:3], done[5:], name=f"rs_sum_{tag}_d"))
        return sums

    def mixer_grads(self, dwp, dwo, small, loss):
        packed = jnp.concatenate([_pack_small(small), jnp.broadcast_to(loss, (8, LANES))], axis=0)
        land = lax.empty((N_PEERS,) + packed.shape, packed.dtype)
        self.small = _split_start([packed, land], _broadcast_copies, N_PEERS, name="ar_small_start")
        gwin = dwp[:, :IN_COLS].reshape(D_MODEL, N_SHARD, IN_SHARD).transpose(1, 0, 2).astype(GRAD_DTYPE)
        gwo = dwo.reshape(N_SHARD, D_MODEL // N_SHARD, D_MODEL).astype(GRAD_DTYPE)
        self.scatter_mix = self._send_direct([gwin, gwo], "mix")
        return self.small[3] + self.scatter_mix[3]

    def small_summed(self, after):
        send_sems, recv_sems, thru, _ = self.small
        packed, land = _split_wait(thru, send_sems, recv_sems, after, _broadcast_copies, name="ar_small_wait")
        summed = _sum_in_device_order(packed, land, name="ar_small_sum")
        return summed[:-8], summed[-8, 0]

    def mixer_reduced(self, after):
        send_sems, recv_sems, thru, _ = self.scatter_mix
        done = _split_wait(thru, send_sems, recv_sems, after, _direct_copies(2), name="rs_direct_mix_wait")
        return [_sum_direct([done[k]], [done[2 + k]], name=f"rs_sum_{tag}")[0] for k, tag in enumerate(["w_in", "w_out"])]

    def before_ffn1_bwd(self, after):
        send_sems, recv_sems, thru, _ = self.scatter
        n = len(thru) // 2
        done = _split_wait(thru, send_sems, recv_sems, after, _direct_copies(n), name="rs_direct_ffn2_wait")
        self.reduced = list(_sum_direct(done[:n], done[n:], name="rs_sum_ffn2"))


def _adamw(gs, ws, ms, vs, *, name, tm=256):
    n = len(gs)
    r, cdim = gs[0].shape
    tm = r if tm is None else min(tm, r)
    assert r % tm == 0, (r, tm)
    c1 = 1.0 / (1.0 - ADAM_B1 ** ADAM_STEP)
    c2 = 1.0 / (1.0 - ADAM_B2 ** ADAM_STEP)

    def body(*refs):
        for k in range(n):
            g = refs[k][...]
            w = refs[n + k][...]
            m = ADAM_B1 * refs[2 * n + k][...] + (1.0 - ADAM_B1) * g
            v = ADAM_B2 * refs[3 * n + k][...] + (1.0 - ADAM_B2) * (g * g)
            refs[4 * n + k][...] = g
            refs[5 * n + k][...] = -ADAM_LR * ((m * c1) / (jnp.sqrt(v * c2) + ADAM_EPS) + ADAM_WD * w)
            refs[6 * n + k][...] = m
            refs[7 * n + k][...] = v

    flat = pl.BlockSpec((tm, cdim), lambda i: (i, 0))
    like_w = flat if ws[0].ndim == 2 else pl.BlockSpec((None, tm, cdim), lambda i: (0, i, 0))
    outs = _pcall(
        body, name=name, grid=(r // tm,), in_specs=[flat] * n + [like_w] * (3 * n), out_specs=[like_w] * (4 * n),
        out_shape=[jax.ShapeDtypeStruct(ws[0].shape, f32)] * (4 * n),
        compiler_params=_params(1),
    )(*gs, *ws, *ms, *vs)
    return outs[:n], outs[n:2 * n], outs[2 * n:3 * n], outs[3 * n:]


BIG = ["ffn1_w_gate", "ffn1_w_up", "ffn1_w_down", "ffn2_w_gate", "ffn2_w_up", "ffn2_w_down"]
SMALL = ["ln1_g", "ln1_b", "b_forget", "conv_w", "conv_b", "rg_wa", "rg_ba", "rg_wx", "rg_bx", "lru_lambda",
         "ln2_g", "ln2_b", "ln3_g", "ln3_b"]
WEIGHTS = ["ffn1_w_gate", "ffn1_w_up", "ffn1_w_down", "ln1_g", "ln1_b", "w_in", "b_forget", "conv_w", "conv_b",
           "rg_wa", "rg_ba", "rg_wx", "rg_bx", "lru_lambda", "w_out", "ln2_g", "ln2_b",
           "ffn2_w_gate", "ffn2_w_up", "ffn2_w_down", "ln3_g", "ln3_b"]


def _pack_small(parts):
    rows = []
    for n in SMALL:
        flat = parts[n].reshape(-1)
        pad = (-flat.shape[0]) % LANES
        rows.append(jnp.pad(flat, (0, pad)).reshape(-1, LANES))
    packed = jnp.concatenate(rows, axis=0)
    return jnp.pad(packed, ((0, (-packed.shape[0]) % 8), (0, 0)))


def _unpack_small(packed, shapes):
    out, r0 = {}, 0
    for n in SMALL:
        size = math.prod(shapes[n])
        nr = -(-size // LANES)
        out[n] = packed[r0:r0 + nr].reshape(-1)[:size].reshape(shapes[n])
        r0 += nr
    return out


def kernel(x, ffn1_w_gate, ffn1_w_up, ffn1_w_down, ln1_g, ln1_b, w_in, b_forget, conv_w, conv_b, rg_wa, rg_ba, rg_wx, rg_bx, lru_lambda, w_out, ln2_g, ln2_b, ffn2_w_gate, ffn2_w_up, ffn2_w_down, ln3_g, ln3_b, loss_target, m_ffn1_w_gate, m_ffn1_w_up, m_ffn1_w_down, m_ln1_g, m_ln1_b, m_w_in, m_b_forget, m_conv_w, m_conv_b, m_rg_wa, m_rg_ba, m_rg_wx, m_rg_bx, m_lru_lambda, m_w_out, m_ln2_g, m_ln2_b, m_ffn2_w_gate, m_ffn2_w_up, m_ffn2_w_down, m_ln3_g, m_ln3_b, v_ffn1_w_gate, v_ffn1_w_up, v_ffn1_w_down, v_ln1_g, v_ln1_b, v_w_in, v_b_forget, v_conv_w, v_conv_b, v_rg_wa, v_rg_ba, v_rg_wx, v_rg_bx, v_lru_lambda, v_w_out, v_ln2_g, v_ln2_b, v_ffn2_w_gate, v_ffn2_w_up, v_ffn2_w_down, v_ln3_g, v_ln3_b):
    args = dict(locals())
    w = {n: args[n] for n in WEIGHTS}
    mom = {n: args["m_" + n] for n in WEIGHTS}
    var = {n: args["v_" + n] for n in WEIGHTS}
    chip = 2 * lax.axis_index("x") + lax.axis_index("y")

    g1 = _all_gather_bf16([w[n][0] for n in BIG[:2]], name="ag_ffn1_up")
    fs = D_FF // N_SHARD
    full = dict(
        f1g=g1[0].reshape(N_SHARD, D_MODEL, fs), f1u=g1[1].reshape(N_SHARD, D_MODEL, fs),
        bfp=jnp.pad(b_forget, ((0, 0), (0, LANES - HEADS))),
        ln1_g=ln1_g, ln1_b=ln1_b, ln2_g=ln2_g, ln2_b=ln2_b, ln3_g=ln3_g, ln3_b=ln3_b,
        conv_b=conv_b, rg_wa=rg_wa[0], rg_wx=rg_wx[0], rg_ba=rg_ba[0], rg_bx=rg_bx[0], lam=lru_lambda,
    )
    cw_place = lax.dynamic_update_slice(jnp.zeros((8, LRU_W), f32), conv_w[0] * 0.5, (0, chip * (LRU_W // N_SHARD)))
    cw_full = _all_reduce_small(cw_place.reshape(-1, LANES), g1[0], name="ag_conv_w")
    full["conv_w"] = cw_full.reshape(8, LRU_W)[:CONV_K]

    hooks = _Overlap([w["ffn1_w_down"][0], w["w_in"][0], w["w_out"][0]], [w[n][0] for n in BIG[3:]], cw_full)
    loss_rep, dx, g = _local_step(x[0], loss_target[0], full, hooks)

    token1 = hooks.ffn1_grads(g["f1"][1])
    red = _join_halves(hooks.reduced + hooks.mixer_reduced([token1]), name="rs_join_rest")
    grads = dict(zip(BIG[3:] + ["w_in", "w_out"], red))

    small_sum, loss = hooks.small_summed(red)
    small_shapes = {n: w[n].shape for n in SMALL}
    small_shapes["conv_w"] = (1, CONV_K, LRU_W)
    gs_red = _unpack_small(small_sum, small_shapes)
    gs_red["conv_w"] = lax.dynamic_slice(gs_red["conv_w"], (0, 0, chip * (LRU_W // N_SHARD)),
                                         (1, CONV_K, LRU_W // N_SHARD))
    grads.update(gs_red)

    delta, new_m, new_v = {}, {}, {}

    def adamw(names, name, **kw):
        g3, d, nm, nv = _adamw([grads[n] for n in names], [w[n] for n in names], [mom[n] for n in names],
                               [var[n] for n in names], name=name, **kw)
        for i, n in enumerate(names):
            grads[n], delta[n], new_m[n], new_v[n] = g3[i], d[i], nm[i], nv[i]

    adamw(BIG[3:], "adamw_ffn2", tm=128)
    adamw(["w_in"], "adamw_w_in")
    adamw(["w_out"], "adamw_w_out")
    shard_shapes = {n: w[n].shape for n in SMALL}
    _, d, nm, nv = _adamw([_pack_small({n: grads[n] for n in SMALL})], [_pack_small({n: w[n] for n in SMALL})],
                          [_pack_small({n: mom[n] for n in SMALL})], [_pack_small({n: var[n] for n in SMALL})],
                          name="adamw_small", tm=None)
    for dst, packed in ((delta, d[0]), (new_m, nm[0]), (new_v, nv[0])):
        dst.update(_unpack_small(packed, shard_shapes))

    worked = [new_v["ffn2_w_down"], new_v["w_in"], new_v["w_out"], nv[0]]
    ga, ua, da, gb, ub, db = _join_halves(hooks.ffn1_reduced(worked), name="rs_join_ffn1")
    grads["ffn1_w_gate"] = jnp.concatenate([ga, gb], axis=1)
    grads["ffn1_w_up"] = jnp.concatenate([ua, ub], axis=1)
    grads["ffn1_w_down"] = jnp.concatenate([da, db], axis=0)
    adamw(BIG[:3], "adamw_ffn1", tm=128)

    def shaped(tree, n):
        return tree[n].reshape(w[n].shape)

    return (loss, dx[None], *[shaped(grads, n) for n in WEIGHTS], *[shaped(delta, n) for n in WEIGHTS],
            *[shaped(new_m, n) for n in WEIGHTS], *[shaped(new_v, n) for n in WEIGHTS])
```

```python
import functools
import math

import jax
import jax.numpy as jnp
from jax import lax
from jax.experimental import pallas as pl
from jax.experimental.pallas import tpu as pltpu

f32 = jnp.float32
MXU_DTYPE = jnp.bfloat16
GRAD_DTYPE = jnp.bfloat16

D_MODEL = 1024
D_FF = 4096
N_SHARD = 4
N_DEV = 8
FOX_W = 512
LRU_W = 512
HEADS = 8
HEAD_DIM = 64
CONV_K = 4
IN_COLS = 2568
IN_SHARD = IN_COLS // N_SHARD
QKV_W = 3 * FOX_W
Z_PAD = 2688
LANES = 128
LN_EPS = 1e-5
DN_ALPHA = 2.0 ** 0.25
LRU_C = 8.0
NEG_BIG = -1e30
VMEM_LIMIT = 56 * 1024 * 1024

ADAM_LR = 0.001
ADAM_B1 = 0.9
ADAM_B2 = 0.999
ADAM_EPS = 1e-08
ADAM_WD = 0.01
ADAM_STEP = 10


def _pcall(body, **kw):
    return pl.pallas_call(body, **kw)


def _params(n_grid, vmem=VMEM_LIMIT):
    return pltpu.CompilerParams(dimension_semantics=("arbitrary",) * n_grid, vmem_limit_bytes=vmem)


def _dot(a, b):
    return jnp.dot(a, b, preferred_element_type=f32)


def _dot_nt(a, b):
    return lax.dot_general(a, b, (((1,), (1,)), ((), ())), preferred_element_type=f32)


def _dot_tn(a, b):
    return lax.dot_general(a, b, (((0,), (0,)), ((), ())), preferred_element_type=f32)


def _sigmoid(x):
    return 1.0 / (1.0 + jnp.exp(-x))


def _layer_norm_stats(y):
    mu = jnp.mean(y, axis=-1, keepdims=True)
    yc = y - mu
    var = jnp.mean(yc * yc, axis=-1, keepdims=True)
    rstd = lax.rsqrt(var + LN_EPS)
    return yc * rstd, rstd


def _ln_backward(dy, xhat, rstd, gamma):
    dxhat = dy * gamma
    m1 = jnp.mean(dxhat, axis=-1, keepdims=True)
    m2 = jnp.mean(dxhat * xhat, axis=-1, keepdims=True)
    dyp = rstd * (dxhat - m1 - xhat * m2)
    return dyp, jnp.sum(dy * xhat, axis=0, keepdims=True), jnp.sum(dy, axis=0, keepdims=True)


def _ffn_fwd_loss(x, wg, wu, wd, ln_g, ln_b, target, *, name, tm=1024, tf=512):
    T = x.shape[0]
    tm = min(tm, T)
    tr = min(256, tm)
    fs = D_FF // N_SHARD
    cpf = fs // tf
    nf = D_FF // tf
    nt = T // tm

    def body(x_ref, wg_ref, wu_ref, wd_ref, g_ref, b_ref, t_ref,
             xb_ref, gact_ref, uact_ref, dyp_ref, dgam_ref, dbeta_ref, loss_ref, acc_ref):
        i = pl.program_id(0)
        f = pl.program_id(1)

        @pl.when(jnp.logical_and(i == 0, f == 0))
        def _():
            dgam_ref[...] = jnp.zeros_like(dgam_ref)
            dbeta_ref[...] = jnp.zeros_like(dbeta_ref)
            loss_ref[...] = jnp.zeros_like(loss_ref)

        @pl.when(f == 0)
        def _():
            xb_ref[...] = x_ref[...].astype(MXU_DTYPE)
            acc_ref[...] = jnp.zeros_like(acc_ref)

        xb = xb_ref[...]
        g = _dot(xb, wg_ref[...])
        u = _dot(xb, wu_ref[...])
        h = (g * _sigmoid(g)) * u
        gact_ref[...] = g.astype(gact_ref.dtype)
        uact_ref[...] = u.astype(uact_ref.dtype)
        acc_ref[...] += _dot(h.astype(MXU_DTYPE), wd_ref[...])

        @pl.when(f == nf - 1)
        def _():
            gamma = g_ref[...]

            def rows_chunk(r, carry):
                rows = pl.ds(pl.multiple_of(r * tr, tr), tr)
                xhat, rstd = _layer_norm_stats(DN_ALPHA * x_ref[rows, :] + 0.5 * acc_ref[rows, :])
                err = xhat * gamma + b_ref[...] - t_ref[rows, :]
                sq = jnp.sum(jnp.sum(err * err, axis=0, keepdims=True), axis=1, keepdims=True)
                loss_ref[...] += jnp.broadcast_to(sq * (0.5 / D_MODEL), loss_ref.shape)
                dyp, dgam, dbeta = _ln_backward(err * (1.0 / D_MODEL), xhat, rstd, gamma)
                dyp_ref[rows, :] = dyp
                dgam_ref[...] += dgam
                dbeta_ref[...] += dbeta
                return carry

            lax.fori_loop(0, tm // tr, rows_chunk, 0)

    row = lambda i, f: (i, 0)
    const = lambda i, f: (0, 0)
    tile = pl.BlockSpec((tm, tf), lambda i, f: (i, f))
    cols = pl.BlockSpec((None, D_MODEL, tf), lambda i, f: (f // cpf, 0, f % cpf))
    last = lambda i, f: (jnp.where(f == nf - 1, i, jnp.maximum(i - 1, 0)), 0)
    return _pcall(
        body, name=name, grid=(nt, nf),
        in_specs=[pl.BlockSpec((tm, D_MODEL), row), cols, cols,
                  pl.BlockSpec((None, tf, D_MODEL), lambda i, f: (f // cpf, f % cpf, 0)),
                  pl.BlockSpec((1, D_MODEL), const), pl.BlockSpec((1, D_MODEL), const),
                  pl.BlockSpec((tm, D_MODEL), last)],
        out_specs=[pl.BlockSpec((tm, D_MODEL), row), tile, tile, pl.BlockSpec((tm, D_MODEL), row),
                   pl.BlockSpec((1, D_MODEL), const), pl.BlockSpec((1, D_MODEL), const), pl.BlockSpec((1, LANES), const)],
        out_shape=[jax.ShapeDtypeStruct((T, D_MODEL), MXU_DTYPE), jax.ShapeDtypeStruct((T, D_FF), MXU_DTYPE),
                   jax.ShapeDtypeStruct((T, D_FF), MXU_DTYPE), jax.ShapeDtypeStruct((T, D_MODEL), f32),
                   jax.ShapeDtypeStruct((1, D_MODEL), f32), jax.ShapeDtypeStruct((1, D_MODEL), f32),
                   jax.ShapeDtypeStruct((1, LANES), f32)],
        scratch_shapes=[pltpu.VMEM((tm, D_MODEL), f32)],
        compiler_params=_params(2),
    )(x, wg, wu, wd, ln_g, ln_b, target)


def _ffn_up(x, wg, wu, after=None, *, name, tm=1024, tf=512):
    T = x.shape[0]
    tm = min(tm, T)
    cpf = (D_FF // N_SHARD) // tf
    nf = D_FF // tf
    extra = [] if after is None else [after]

    def body(x_ref, wg_ref, wu_ref, *refs):
        xb_ref, gact_ref, uact_ref, hact_ref = refs[len(extra):]

        @pl.when(pl.program_id(1) == 0)
        def _():
            xb_ref[...] = x_ref[...].astype(MXU_DTYPE)

        xb = xb_ref[...]
        g = _dot(xb, wg_ref[...])
        u = _dot(xb, wu_ref[...])
        gact_ref[...] = g.astype(gact_ref.dtype)
        uact_ref[...] = u.astype(uact_ref.dtype)
        hact_ref[...] = ((g * _sigmoid(g)) * u).astype(hact_ref.dtype)

    row = lambda i, f: (i, 0)
    tile = pl.BlockSpec((tm, tf), lambda i, f: (i, f))
    cols = pl.BlockSpec((None, D_MODEL, tf), lambda i, f: (f // cpf, 0, f % cpf))
    return _pcall(
        body, name=name, grid=(T // tm, nf),
        in_specs=[pl.BlockSpec((tm, D_MODEL), row), cols, cols] + [pl.BlockSpec(memory_space=pl.ANY)] * len(extra),
        out_specs=[pl.BlockSpec((tm, D_MODEL), row), tile, tile, tile],
        out_shape=[jax.ShapeDtypeStruct((T, D_MODEL), MXU_DTYPE)] + [jax.ShapeDtypeStruct((T, D_FF), MXU_DTYPE)] * 3,
        compiler_params=_params(2),
    )(x, wg, wu, *extra)


def _ffn_down_ln(x, hact, wd, ln_g, ln_b, *, name, tm=1024):
    T = x.shape[0]
    tm = min(tm, T)
    fs = D_FF // N_SHARD

    def body(x_ref, h_ref, wd_ref, g_ref, b_ref, xhat_ref, xn_ref, rstd_ref, acc_ref):
        k = pl.program_id(1)

        @pl.when(k == 0)
        def _():
            acc_ref[...] = jnp.zeros_like(acc_ref)

        acc_ref[...] += _dot(h_ref[...], wd_ref[...])

        @pl.when(k == N_SHARD - 1)
        def _():
            xhat, rstd = _layer_norm_stats(DN_ALPHA * x_ref[...] + 0.5 * acc_ref[...])
            xhat_ref[...] = xhat
            xn_ref[...] = (xhat * g_ref[...] + b_ref[...]).astype(xn_ref.dtype)
            rstd_ref[...] = jnp.broadcast_to(rstd, rstd_ref.shape)

    row = lambda i, k: (i, 0)
    vec = pl.BlockSpec((1, D_MODEL), lambda i, k: (0, 0))
    return _pcall(
        body, name=name, grid=(T // tm, N_SHARD),
        in_specs=[pl.BlockSpec((tm, D_MODEL), row), pl.BlockSpec((tm, fs), lambda i, k: (i, k)),
                  pl.BlockSpec((None, fs, D_MODEL), lambda i, k: (k, 0, 0)), vec, vec],
        out_specs=[pl.BlockSpec((tm, D_MODEL), row), pl.BlockSpec((tm, D_MODEL), row), pl.BlockSpec((tm, LANES), row)],
        out_shape=[jax.ShapeDtypeStruct((T, D_MODEL), f32), jax.ShapeDtypeStruct((T, D_MODEL), MXU_DTYPE),
                   jax.ShapeDtypeStruct((T, LANES), f32)],
        scratch_shapes=[pltpu.VMEM((tm, D_MODEL), f32)],
        compiler_params=_params(2),
    )(x, hact, wd, ln_g, ln_b)


def _ffn_bwd(dyp, xb, gact, uact, wg, wu, wd, after=None, *, name, tm=512, tf=512, part=None, dx_init=None):
    T = dyp.shape[0]
    tm = min(tm, T)
    fs = D_FF // N_SHARD
    cpf = fs // tf
    nt = T // tm
    nf = D_FF // tf if part is None else N_SHARD
    wf = fs if part is None else tf
    slab = (lambda f: f // cpf) if part is None else (lambda f: f)
    chunk = (lambda f: f % cpf) if part is None else (lambda f: part)
    extra = ([] if dx_init is None else [dx_init]) + ([] if after is None else [after])

    def body(dyp_ref, xb_ref, g_ref, u_ref, wg_ref, wu_ref, wd_ref, *refs):
        dx_hbm, dwg_ref, dwu_ref, dwd_ref, dx_sc, dwg_sc, dwu_sc, dwd_sc, sem = refs[len(extra):]
        f = pl.program_id(0)
        i = pl.program_id(1)
        rows = pl.ds(pl.multiple_of(i * tm, tm), tm)
        dyp_t = dyp_ref[...]
        dy = (0.5 * dyp_t).astype(MXU_DTYPE)

        @pl.when(i == 0)
        def _():
            dwg_sc[...] = jnp.zeros_like(dwg_sc)
            dwu_sc[...] = jnp.zeros_like(dwu_sc)
            dwd_sc[...] = jnp.zeros_like(dwd_sc)

        @pl.when(f == 0)
        def _():
            dx_sc[rows, :] = DN_ALPHA * dyp_t if dx_init is None else refs[0][...]

        g = g_ref[...].astype(f32)
        u = u_ref[...].astype(f32)
        sig = _sigmoid(g)
        silu = g * sig
        dh = _dot_nt(dy, wd_ref[...])
        dg = (dh * u * (sig * (1.0 + g * (1.0 - sig)))).astype(MXU_DTYPE)
        du = (dh * silu).astype(MXU_DTYPE)
        hb = (silu * u).astype(MXU_DTYPE)
        dx_sc[rows, :] += _dot_nt(dg, wg_ref[...]) + _dot_nt(du, wu_ref[...])
        xb_t = xb_ref[...]
        dwg_sc[...] += _dot_tn(xb_t, dg)
        dwu_sc[...] += _dot_tn(xb_t, du)
        dwd_sc[...] += _dot_tn(hb, dy)

        @pl.when(i == nt - 1)
        def _():
            dwg_ref[...] = dwg_sc[...].astype(dwg_ref.dtype)
            dwu_ref[...] = dwu_sc[...].astype(dwu_ref.dtype)
            dwd_ref[...] = dwd_sc[...].astype(dwd_ref.dtype)

        @pl.when(jnp.logical_and(f == nf - 1, i == nt - 1))
        def _():
            cp = pltpu.make_async_copy(dx_sc, dx_hbm, sem)
            cp.start()
            cp.wait()

    row = lambda f, i: (i, 0)
    return _pcall(
        body, name=name, grid=(nf, nt),
        in_specs=[
            pl.BlockSpec((tm, D_MODEL), row),
            pl.BlockSpec((tm, D_MODEL), row),
            pl.BlockSpec((tm, tf), lambda f, i: (i, slab(f) * cpf + chunk(f))),
            pl.BlockSpec((tm, tf), lambda f, i: (i, slab(f) * cpf + chunk(f))),
            pl.BlockSpec((None, D_MODEL, tf), lambda f, i: (slab(f), 0, chunk(f))),
            pl.BlockSpec((None, D_MODEL, tf), lambda f, i: (slab(f), 0, chunk(f))),
            pl.BlockSpec((None, tf, D_MODEL), lambda f, i: (slab(f), chunk(f), 0)),
        ] + ([] if dx_init is None else [pl.BlockSpec((tm, D_MODEL), row)])
        + ([] if after is None else [pl.BlockSpec(memory_space=pl.ANY)]),
        out_specs=[
            pl.BlockSpec(memory_space=pl.ANY),
            pl.BlockSpec((None, D_MODEL, tf), lambda f, i: (slab(f), 0, chunk(f) if part is None else 0)),
            pl.BlockSpec((None, D_MODEL, tf), lambda f, i: (slab(f), 0, chunk(f) if part is None else 0)),
            pl.BlockSpec((None, tf, D_MODEL), lambda f, i: (slab(f), chunk(f) if part is None else 0, 0)),
        ],
        out_shape=[
            jax.ShapeDtypeStruct((T, D_MODEL), f32),
            jax.ShapeDtypeStruct((N_SHARD, D_MODEL, wf), GRAD_DTYPE),
            jax.ShapeDtypeStruct((N_SHARD, D_MODEL, wf), GRAD_DTYPE),
            jax.ShapeDtypeStruct((N_SHARD, wf, D_MODEL), GRAD_DTYPE),
        ],
        scratch_shapes=[pltpu.VMEM((T, D_MODEL), f32), pltpu.VMEM((D_MODEL, tf), f32),
                        pltpu.VMEM((D_MODEL, tf), f32), pltpu.VMEM((tf, D_MODEL), f32),
                        pltpu.SemaphoreType.DMA],
        compiler_params=_params(2),
    )(dyp, xb, gact, uact, wg, wu, wd, *extra)


def _proj_in(xn, wp, bfp, *, name, tm=512):
    T = xn.shape[0]
    tm = min(tm, T)
    nt = T // tm

    def body(x_ref, w_ref, b_ref, qkv_ref, lxg_ref, fg_ref):
        z = _dot(x_ref[...], w_ref[...])
        qkv_ref[...] = z[:, :QKV_W].astype(qkv_ref.dtype)
        lxg_ref[...] = z[:, QKV_W:QKV_W + 2 * LRU_W]
        fg_ref[...] = z[:, QKV_W + 2 * LRU_W:] + b_ref[...]

    row = lambda i: (i, 0)
    const = lambda i: (0, 0)
    return _pcall(
        body, name=name, grid=(nt,),
        in_specs=[pl.BlockSpec((tm, D_MODEL), row), pl.BlockSpec((D_MODEL, Z_PAD), const),
                  pl.BlockSpec((1, LANES), const)],
        out_specs=[pl.BlockSpec((tm, QKV_W), row), pl.BlockSpec((tm, 2 * LRU_W), row),
                   pl.BlockSpec((tm, LANES), row)],
        out_shape=[jax.ShapeDtypeStruct((T, QKV_W), MXU_DTYPE), jax.ShapeDtypeStruct((T, 2 * LRU_W), f32),
                   jax.ShapeDtypeStruct((T, LANES), f32)],
        compiler_params=_params(1),
    )(xn, wp, bfp)


def _proj_in_bwd(dqa, dka, dva, dlxg, fgb, xn, dyp, wp, xhat, rstd, ln_g, *, name, tm=512):
    T = xn.shape[0]
    tm = min(tm, T)
    nt = T // tm

    def body(dq_ref, dk_ref, dv_ref, dl_ref, fg_ref, x_ref, dyp_ref, w_ref, xhat_ref, rstd_ref, g_ref,
             dpre_ref, dw_hbm, dgam_ref, dbeta_ref, dbf_ref, dw_sc, carry, sem):
        i = pl.program_id(0)

        @pl.when(i == 0)
        def _():
            dw_sc[...] = jnp.zeros_like(dw_sc)
            dgam_ref[...] = jnp.zeros_like(dgam_ref)
            dbeta_ref[...] = jnp.zeros_like(dbeta_ref)
            dbf_ref[...] = jnp.zeros_like(dbf_ref)
            carry[...] = jnp.zeros_like(carry)

        lane = lax.broadcasted_iota(jnp.int32, (tm, LANES), 1)
        dc = jnp.zeros((tm, LANES), f32)
        for h in range(HEADS):
            row_sum = dq_ref[:, h * LANES + AUX:h * LANES + AUX + 1]
            col_sum = dk_ref[:, h * LANES + AUX + 3:h * LANES + AUX + 4]
            dc = jnp.where(lane == h, jnp.broadcast_to(row_sum - col_sum, (tm, LANES)), dc)
        r = lax.broadcasted_iota(jnp.int32, (tm, tm), 0)
        c = lax.broadcasted_iota(jnp.int32, (tm, tm), 1)
        dls = _tri_dot(jnp.where(c >= r, 1.0, 0.0).astype(jnp.bfloat16), dc) + carry[0:1, :]
        carry[...] = jnp.broadcast_to(dls[0:1, :], carry.shape)
        dfg = dls * _sigmoid(-fg_ref[...])
        dbf_ref[...] += jnp.sum(dfg, axis=0, keepdims=True)

        low = _low_lanes((tm, LANES))

        def packed(ref):
            pairs = [jnp.where(low, ref[:, (2 * j) * LANES:(2 * j + 1) * LANES],
                               _swap_lane_halves(ref[:, (2 * j + 1) * LANES:(2 * j + 2) * LANES]))
                     for j in range(HEADS // 2)]
            return jnp.concatenate(pairs, axis=1).astype(MXU_DTYPE)

        dz = jnp.concatenate(
            [packed(dq_ref), packed(dk_ref), packed(dv_ref),
             dl_ref[...].astype(MXU_DTYPE), dfg.astype(MXU_DTYPE)], axis=1)
        dx = DN_ALPHA * dyp_ref[...] + _dot_nt(dz, w_ref[...])
        dpre, dgam, dbeta = _ln_backward(dx, xhat_ref[...], rstd_ref[:, 0:1], g_ref[...])
        dpre_ref[...] = dpre
        dgam_ref[...] += dgam
        dbeta_ref[...] += dbeta
        dw_sc[...] += _dot_tn(x_ref[...], dz)

        @pl.when(i == nt - 1)
        def _():
            dw_sc[:, :FOX_W] = dw_sc[:, :FOX_W] * (1.0 / math.sqrt(HEAD_DIM))
            cp = pltpu.make_async_copy(dw_sc, dw_hbm, sem)
            cp.start()
            cp.wait()

    row = lambda i: (nt - 1 - i, 0)
    const = lambda i: (0, 0)
    return _pcall(
        body, name=name, grid=(nt,),
        in_specs=[pl.BlockSpec((tm, HEADS * LANES), row), pl.BlockSpec((tm, HEADS * LANES), row),
                  pl.BlockSpec((tm, HEADS * LANES), row),
                  pl.BlockSpec((tm, 2 * LRU_W), row), pl.BlockSpec((tm, LANES), row),
                  pl.BlockSpec((tm, D_MODEL), row), pl.BlockSpec((tm, D_MODEL), row),
                  pl.BlockSpec((D_MODEL, Z_PAD), const),
                  pl.BlockSpec((tm, D_MODEL), row), pl.BlockSpec((tm, LANES), row), pl.BlockSpec((1, D_MODEL), const)],
        out_specs=[pl.BlockSpec((tm, D_MODEL), row), pl.BlockSpec(memory_space=pl.ANY),
                   pl.BlockSpec((1, D_MODEL), const), pl.BlockSpec((1, D_MODEL), const), pl.BlockSpec((1, LANES), const)],
        out_shape=[jax.ShapeDtypeStruct((T, D_MODEL), f32), jax.ShapeDtypeStruct((D_MODEL, Z_PAD), f32),
                   jax.ShapeDtypeStruct((1, D_MODEL), f32), jax.ShapeDtypeStruct((1, D_MODEL), f32),
                   jax.ShapeDtypeStruct((1, LANES), f32)],
        scratch_shapes=[pltpu.VMEM((D_MODEL, Z_PAD), f32), pltpu.VMEM((8, LANES), f32), pltpu.SemaphoreType.DMA],
        compiler_params=_params(1),
    )(dqa, dka, dva, dlxg, fgb, xn, dyp, wp, xhat, rstd, ln_g)


def _split3(x):
    hi = x.astype(jnp.bfloat16)
    r1 = x - hi.astype(f32)
    mid = r1.astype(jnp.bfloat16)
    lo = (r1 - mid.astype(f32)).astype(jnp.bfloat16)
    return hi, mid, lo


def _tri_dot(tri, x):
    hi, mid, lo = _split3(x)
    return _dot(tri, hi) + _dot(tri, mid) + _dot(tri, lo)


FOX_PAD = HEADS * LANES
AUX = HEAD_DIM


def _low_lanes(shape):
    return lax.broadcasted_iota(jnp.int32, shape, 1) < HEAD_DIM


def _swap_lane_halves(x):
    return pltpu.roll(x, HEAD_DIM, 1)


def _fox_prep(qkv, fgb, *, name, tm=512):
    T = fgb.shape[0]
    tm = min(tm, T)
    nt = T // tm

    def body(qkv_ref, fg_ref, qa_ref, ka_ref, va_ref, carry):
        i = pl.program_id(0)

        @pl.when(i == 0)
        def _():
            carry[...] = jnp.zeros_like(carry)

        x = fg_ref[...]
        ls = jnp.minimum(x, 0.0) - jnp.log(1.0 + jnp.exp(-jnp.abs(x)))
        r = lax.broadcasted_iota(jnp.int32, (tm, tm), 0)
        c = lax.broadcasted_iota(jnp.int32, (tm, tm), 1)
        tri = jnp.where(r >= c, 1.0, 0.0).astype(jnp.bfloat16)
        cum = _tri_dot(tri, ls) + carry[0:1, :]
        carry[...] = jnp.broadcast_to(cum[tm - 1:tm, :], carry.shape)

        lane = lax.broadcasted_iota(jnp.int32, (tm, LANES), 1)
        low = lane < HEAD_DIM
        ones_q = jnp.where(jnp.logical_and(lane >= AUX + 3, lane < AUX + 6), 1.0, 0.0)
        ones_k = jnp.where(jnp.logical_and(lane >= AUX, lane < AUX + 3), 1.0, 0.0)
        for j in range(HEADS // 2):
            pair = [qkv_ref[:, t * FOX_W + j * LANES:t * FOX_W + (j + 1) * LANES].astype(f32) for t in range(3)]
            for odd in range(2):
                h = 2 * j + odd
                q, k, v = [_swap_lane_halves(a) if odd else a for a in pair]
                hi, mid, lo = [a.astype(f32) for a in _split3(jnp.broadcast_to(cum[:, h:h + 1], (tm, LANES)))]
                aux_q = jnp.where(lane == AUX, hi, jnp.where(lane == AUX + 1, mid, jnp.where(lane == AUX + 2, lo, ones_q)))
                aux_k = jnp.where(lane == AUX + 3, -hi,
                                  jnp.where(lane == AUX + 4, -mid, jnp.where(lane == AUX + 5, -lo, ones_k)))
                blk = slice(h * LANES, (h + 1) * LANES)
                qa_ref[:, blk] = jnp.where(low, q, aux_q).astype(qa_ref.dtype)
                ka_ref[:, blk] = jnp.where(low, k, aux_k).astype(ka_ref.dtype)
                va_ref[:, blk] = jnp.where(low, v, 1.0).astype(va_ref.dtype)

    row = lambda i: (i, 0)
    return _pcall(
        body, name=name, grid=(nt,),
        in_specs=[pl.BlockSpec((tm, QKV_W), row), pl.BlockSpec((tm, LANES), row)],
        out_specs=[pl.BlockSpec((tm, FOX_PAD), row)] * 3,
        out_shape=[jax.ShapeDtypeStruct((T, FOX_PAD), MXU_DTYPE)] * 3,
        scratch_shapes=[pltpu.VMEM((8, LANES), f32)],
        compiler_params=_params(1),
    )(qkv, fgb)


def _future_keys(tq, tk):
    r = lax.broadcasted_iota(jnp.int32, (tq, tk), 0)
    c = lax.broadcasted_iota(jnp.int32, (tq, tk), 1)
    return c > r


def _causal_steps(nq, key_major):
    if key_major:
        pairs = [(qi, ki) for ki in range(nq) for qi in range(ki, nq)]
    else:
        pairs = [(qi, ki) for qi in range(nq) for ki in range(qi + 1)]
    return (jnp.asarray([p[0] for p in pairs], jnp.int32), jnp.asarray([p[1] for p in pairs], jnp.int32))


def _fox_fwd(qa, ka, va, *, name, tq=512, hps=8):
    T = qa.shape[0]
    tq = min(tq, T)
    tk = tq
    nq = T // tq
    rep = tk // LANES
    qi_tab, ki_tab = _causal_steps(nq, key_major=False)

    def body(qi_ref, ki_ref, qa_ref, ka_ref, va_ref, o_ref, lse_ref, m_sc, acc_sc):
        t = pl.program_id(1)
        qi = qi_ref[t]
        ki = ki_ref[t]

        @pl.when(ki == 0)
        def _():
            m_sc[...] = jnp.full_like(m_sc, NEG_BIG)
            acc_sc[...] = jnp.zeros_like(acc_sc)

        def tile(diagonal):
            for h in range(hps):
                blk = slice(h * LANES, (h + 1) * LANES)
                s = _dot_nt(qa_ref[:, blk], ka_ref[:, blk])
                if diagonal:
                    s = jnp.where(_future_keys(tq, tk), NEG_BIG, s)
                m_prev = m_sc[h]
                m_new = jnp.maximum(m_prev, jnp.max(s, axis=1, keepdims=True))
                p = jnp.exp(s - jnp.tile(m_new, (1, rep)))
                acc_sc[h] = jnp.exp(m_prev - m_new) * acc_sc[h] + _dot(p.astype(MXU_DTYPE), va_ref[:, blk])
                m_sc[h] = m_new

        @pl.when(ki < qi)
        def _():
            tile(False)

        @pl.when(ki == qi)
        def _():
            tile(True)
            low = _low_lanes((tq, LANES))
            outs = []
            for h in range(hps):
                acc = acc_sc[h]
                den = _swap_lane_halves(acc)
                outs.append(acc / den)
                lse_ref[h] = m_sc[h] + jnp.log(jnp.where(low, den, acc))
            for p in range(hps // 2):
                o_ref[:, p * LANES:(p + 1) * LANES] = jnp.where(low, outs[2 * p], _swap_lane_halves(outs[2 * p + 1]))

    pair = hps * LANES
    return _pcall(
        body, name=name,
        grid_spec=pltpu.PrefetchScalarGridSpec(
            num_scalar_prefetch=2, grid=(HEADS // hps, qi_tab.shape[0]),
            in_specs=[
                pl.BlockSpec((tq, pair), lambda j, t, qi_ref, ki_ref: (qi_ref[t], j)),
                pl.BlockSpec((tk, pair), lambda j, t, qi_ref, ki_ref: (ki_ref[t], j)),
                pl.BlockSpec((tk, pair), lambda j, t, qi_ref, ki_ref: (ki_ref[t], j)),
            ],
            out_specs=[pl.BlockSpec((tq, pair // 2), lambda j, t, qi_ref, ki_ref: (qi_ref[t], j)),
                       pl.BlockSpec((hps, tq, LANES), lambda j, t, qi_ref, ki_ref: (j, qi_ref[t], 0))],
            scratch_shapes=[pltpu.VMEM((hps, tq, LANES), f32)] * 2),
        out_shape=[jax.ShapeDtypeStruct((T, FOX_W), f32), jax.ShapeDtypeStruct((HEADS, T, LANES), f32)],
        compiler_params=_params(2),
    )(qi_tab, ki_tab, qa, ka, va)


def _fox_bwd(qa, ka, va, doa, lse, drep, *, name, tq=512, hps=4):
    T = qa.shape[0]
    tq = min(tq, T)
    tk = tq
    nq = T // tq
    rep = tk // LANES
    qi_tab, ki_tab = _causal_steps(nq, key_major=True)

    def body(qi_ref, ki_ref, qa_ref, ka_ref, va_ref, doa_ref, lse_ref, d_ref, dqa_ref, dka_ref, dva_ref, dk_sc, dv_sc):
        t = pl.program_id(1)
        qi = qi_ref[t]
        ki = ki_ref[t]
        rows = pl.ds(pl.multiple_of(qi * tq, tq), tq)

        @pl.when(t == 0)
        def _():
            dqa_ref[...] = jnp.zeros_like(dqa_ref)

        @pl.when(qi == ki)
        def _():
            dk_sc[...] = jnp.zeros_like(dk_sc)
            dv_sc[...] = jnp.zeros_like(dv_sc)

        def tile(diagonal):
            for h in range(hps):
                blk = slice(h * LANES, (h + 1) * LANES)
                qh, kh, doh = qa_ref[:, blk], ka_ref[:, blk], doa_ref[:, blk]
                p = jnp.exp(_dot_nt(qh, kh) - jnp.tile(lse_ref[h], (1, rep)))
                if diagonal:
                    p = jnp.where(_future_keys(tq, tk), 0.0, p)
                dp = _dot_nt(doh, va_ref[:, blk])
                ds = (p * (dp - jnp.tile(d_ref[h], (1, rep)))).astype(MXU_DTYPE)
                dv_sc[h] += _dot_tn(p.astype(MXU_DTYPE), doh)
                dk_sc[h] += _dot_tn(ds, qh)
                dqa_ref[rows, blk] += _dot(ds, kh)

        @pl.when(qi > ki)
        def _():
            tile(False)

        @pl.when(qi == ki)
        def _():
            tile(True)

        @pl.when(qi == nq - 1)
        def _():
            for h in range(hps):
                blk = slice(h * LANES, (h + 1) * LANES)
                dka_ref[:, blk] = dk_sc[h]
                dva_ref[:, blk] = dv_sc[h]

    pair = hps * LANES
    q_blk = lambda j, t, qi_ref, ki_ref: (qi_ref[t], j)
    k_blk = lambda j, t, qi_ref, ki_ref: (ki_ref[t], j)
    stat = pl.BlockSpec((hps, tq, LANES), lambda j, t, qi_ref, ki_ref: (j, qi_ref[t], 0))
    return _pcall(
        body, name=name,
        grid_spec=pltpu.PrefetchScalarGridSpec(
            num_scalar_prefetch=2, grid=(HEADS // hps, qi_tab.shape[0]),
            in_specs=[pl.BlockSpec((tq, pair), q_blk), pl.BlockSpec((tk, pair), k_blk), pl.BlockSpec((tk, pair), k_blk),
                      pl.BlockSpec((tq, pair), q_blk), stat, stat],
            out_specs=[pl.BlockSpec((T, pair), lambda j, t, qi_ref, ki_ref: (0, j)),
                       pl.BlockSpec((tk, pair), k_blk), pl.BlockSpec((tk, pair), k_blk)],
            scratch_shapes=[pltpu.VMEM((hps, tk, LANES), f32)] * 2),
        out_shape=[jax.ShapeDtypeStruct((T, FOX_PAD), f32)] * 3,
        compiler_params=_params(2),
    )(qi_tab, ki_tab, qa, ka, va, doa, lse, drep)


GELU_C = math.sqrt(2.0 / math.pi)
GELU_A = 0.044715


def _gelu(x):
    t = jnp.tanh(GELU_C * (x + GELU_A * x * x * x))
    return 0.5 * x * (1.0 + t), t


def _gelu_grad(x, t):
    return 0.5 * (1.0 + t) + 0.5 * x * (1.0 - t * t) * GELU_C * (1.0 + 3.0 * GELU_A * x * x)


EXPM1_SERIES_BELOW = 0.25


def _expm1(x, e):
    series = x * (1.0 + x * (1 / 2 + x * (1 / 6 + x * (1 / 24 + x * (1 / 120 + x * (1 / 720))))))
    return jnp.where(x > -EXPM1_SERIES_BELOW, series, e - 1.0)


def _lru_gates(u, wab_ref, bab_ref, lam_ref):
    pre = _dot(u.astype(MXU_DTYPE), wab_ref[...]) + bab_ref[...]
    r = _sigmoid(pre[:, :LRU_W])
    gi = _sigmoid(pre[:, LRU_W:])
    lam = lam_ref[...]
    sp = jnp.maximum(-lam, 0.0) + jnp.log(1.0 + jnp.exp(-jnp.abs(lam)))
    log_a = -LRU_C * r * sp
    a = jnp.exp(log_a)
    s = jnp.sqrt(-_expm1(2.0 * log_a, a * a))
    return r, gi, sp, a, s


def _lru_fwd(lxg, conv_w, conv_b, wab, bab, lam, *, name, tc=512):
    T = lxg.shape[0]
    tc = min(tc, T)
    nc = T // tc

    def body(lx_ref, lg_ref, cw_ref, cb_ref, wab_ref, bab_ref, lam_ref,
             out_ref, u_ref, hs_ref, ext, a_sc, b_sc, h_sc):
        i = pl.program_id(0)

        @pl.when(i == 0)
        def _():
            ext[0:8, :] = jnp.zeros((8, LRU_W), f32)
            h_sc[...] = jnp.zeros_like(h_sc)

        ext[8:, :] = lx_ref[...]
        u = cb_ref[...] + cw_ref[0:1, :] * ext[pl.ds(5, tc), :]
        for k in range(1, CONV_K):
            u = u + cw_ref[k:k + 1, :] * ext[pl.ds(5 + k, tc), :]
        ext[0:8, :] = ext[tc:tc + 8, :]
        u_ref[...] = u
        r, gi, sp, a, s = _lru_gates(u, wab_ref, bab_ref, lam_ref)
        a_sc[...] = a
        b_sc[...] = s * (gi * u)

        def step(t, h):
            h = a_sc[pl.ds(t, 1), :] * h + b_sc[pl.ds(t, 1), :]
            hs_ref[pl.ds(t, 1), :] = h
            return h

        h = lax.fori_loop(0, tc, step, h_sc[0:1, :], unroll=8)
        h_sc[...] = jnp.broadcast_to(h, h_sc.shape)
        gel, _ = _gelu(lg_ref[...])
        out_ref[...] = gel * hs_ref[...]

    row = lambda i: (i, 0)
    const = lambda i: (0, 0)
    return _pcall(
        body, name=name, grid=(nc,),
        in_specs=[pl.BlockSpec((tc, LRU_W), row), pl.BlockSpec((tc, LRU_W), lambda i: (i, 1)),
                  pl.BlockSpec((CONV_K, LRU_W), const), pl.BlockSpec((1, LRU_W), const),
                  pl.BlockSpec((LRU_W, 2 * LRU_W), const), pl.BlockSpec((1, 2 * LRU_W), const),
                  pl.BlockSpec((1, LRU_W), const)],
        out_specs=[pl.BlockSpec((tc, LRU_W), row)] * 3,
        out_shape=[jax.ShapeDtypeStruct((T, LRU_W), f32)] * 3,
        scratch_shapes=[pltpu.VMEM((tc + 8, LRU_W), f32), pltpu.VMEM((tc, LRU_W), f32),
                        pltpu.VMEM((tc, LRU_W), f32), pltpu.VMEM((8, LRU_W), f32)],
        compiler_params=_params(1),
    )(lxg, lxg, conv_w, conv_b, wab, bab, lam)


def _lru_bwd(dlru, lxg, u, hs, conv_w, wab, bab, lam, *, name, tc=512):
    T = lxg.shape[0]
    tc = min(tc, T)
    nc = T // tc
    bp = tc // 8

    def body(dl_ref, lx_ref, lxp_ref, lg_ref, u_ref, hs_ref, hsp_ref, cw_ref, wab_ref, bab_ref, lam_ref,
             dlxg_ref, dwab_ref, dbab_ref, dcw_ref, dcb_ref, dlam_ref,
             dh_sc, a_sc, ext, du_ext, carry):
        i = pl.program_id(0)
        first_chunk = i == nc - 1

        @pl.when(i == 0)
        def _():
            dwab_ref[...] = jnp.zeros_like(dwab_ref)
            dbab_ref[...] = jnp.zeros_like(dbab_ref)
            dcw_ref[...] = jnp.zeros_like(dcw_ref)
            dcb_ref[...] = jnp.zeros_like(dcb_ref)
            dlam_ref[...] = jnp.zeros_like(dlam_ref)
            carry[...] = jnp.zeros_like(carry)
            du_ext[tc:tc + 8, :] = jnp.zeros((8, LRU_W), f32)

        lg = lg_ref[...]
        gel, th = _gelu(lg)
        dl = dl_ref[...]
        hs = hs_ref[...]
        dlg = dl * hs * _gelu_grad(lg, th)
        u = u_ref[...]
        r, gi, sp, a, s = _lru_gates(u, wab_ref, bab_ref, lam_ref)
        a_sc[...] = a
        dh_sc[...] = dl * gel

        def step(k, c):
            t = tc - 1 - k
            dh = dh_sc[pl.ds(t, 1), :] + c
            dh_sc[pl.ds(t, 1), :] = dh
            return a_sc[pl.ds(t, 1), :] * dh

        c = lax.fori_loop(0, tc, step, carry[0:1, :], unroll=8)
        carry[...] = jnp.broadcast_to(c, carry.shape)

        ext[0:8, :] = jnp.where(first_chunk, 0.0, hsp_ref[...])
        ext[8:, :] = hs
        hprev = ext[pl.ds(7, tc), :]
        dh = dh_sc[...]
        da = dh * hprev
        giu = gi * u
        dla = da * a - (dh * giu) * (a * a / s)
        dgi = dh * s * u
        du = dh * s * gi
        dr = dla * (-LRU_C * sp)
        dlam_ref[...] += jnp.sum(dla * (-LRU_C * r), axis=0, keepdims=True) * (-_sigmoid(-lam_ref[...]))
        dpre = jnp.concatenate([dr * r * (1.0 - r), dgi * gi * (1.0 - gi)], axis=1)
        dpre_b = dpre.astype(MXU_DTYPE)
        du = du + _dot_nt(dpre_b, wab_ref[...])
        dwab_ref[...] += _dot_tn(u.astype(MXU_DTYPE), dpre_b)
        dbab_ref[...] += jnp.sum(dpre, axis=0, keepdims=True)
        dcb_ref[...] += jnp.sum(du, axis=0, keepdims=True)

        du_ext[0:tc, :] = du
        dlx = cw_ref[0:1, :] * du_ext[pl.ds(3, tc), :]
        for k in range(1, CONV_K):
            dlx = dlx + cw_ref[k:k + 1, :] * du_ext[pl.ds(3 - k, tc), :]
        du_ext[tc:tc + 8, :] = du_ext[0:8, :]
        ext[0:8, :] = jnp.where(first_chunk, 0.0, lxp_ref[...])
        ext[8:, :] = lx_ref[...]
        for k in range(CONV_K):
            dcw_ref[k:k + 1, :] += jnp.sum(du * ext[pl.ds(5 + k, tc), :], axis=0, keepdims=True)
        dlxg_ref[:, :LRU_W] = dlx.astype(dlxg_ref.dtype)
        dlxg_ref[:, LRU_W:] = dlg.astype(dlxg_ref.dtype)

    rev = lambda i: (nc - 1 - i, 0)
    prev8 = lambda i: (jnp.maximum((nc - 1 - i) * bp - 1, 0), 0)
    const = lambda i: (0, 0)
    return _pcall(
        body, name=name, grid=(nc,),
        in_specs=[
            pl.BlockSpec((tc, LRU_W), rev),
            pl.BlockSpec((tc, LRU_W), rev),
            pl.BlockSpec((8, LRU_W), prev8),
            pl.BlockSpec((tc, LRU_W), lambda i: (nc - 1 - i, 1)),
            pl.BlockSpec((tc, LRU_W), rev),
            pl.BlockSpec((tc, LRU_W), rev),
            pl.BlockSpec((8, LRU_W), prev8),
            pl.BlockSpec((CONV_K, LRU_W), const),
            pl.BlockSpec((LRU_W, 2 * LRU_W), const),
            pl.BlockSpec((1, 2 * LRU_W), const),
            pl.BlockSpec((1, LRU_W), const),
        ],
        out_specs=[
            pl.BlockSpec((tc, 2 * LRU_W), rev),
            pl.BlockSpec((LRU_W, 2 * LRU_W), const),
            pl.BlockSpec((1, 2 * LRU_W), const),
            pl.BlockSpec((8, LRU_W), const),
            pl.BlockSpec((1, LRU_W), const),
            pl.BlockSpec((1, LRU_W), const),
        ],
        out_shape=[
            jax.ShapeDtypeStruct((T, 2 * LRU_W), MXU_DTYPE),
            jax.ShapeDtypeStruct((LRU_W, 2 * LRU_W), f32),
            jax.ShapeDtypeStruct((1, 2 * LRU_W), f32),
            jax.ShapeDtypeStruct((8, LRU_W), f32),
            jax.ShapeDtypeStruct((1, LRU_W), f32),
            jax.ShapeDtypeStruct((1, LRU_W), f32),
        ],
        scratch_shapes=[pltpu.VMEM((tc, LRU_W), f32), pltpu.VMEM((tc, LRU_W), f32),
                        pltpu.VMEM((tc + 8, LRU_W), f32), pltpu.VMEM((tc + 8, LRU_W), f32),
                        pltpu.VMEM((8, LRU_W), f32)],
        compiler_params=_params(1),
    )(dlru, lxg, lxg, lxg, u, hs, hs, conv_w, wab, bab, lam)


def _mix_out(fox, lru, wo, xhat1, g1, b1, g2, b2, *, name, tm=512):
    T = fox.shape[0]
    tm = min(tm, T)
    nt = T // tm

    def body(fox_ref, lru_ref, wo_ref, xh_ref, g1_ref, b1_ref, g2_ref, b2_ref, xhat_ref, xn_ref, rstd_ref):
        mix = _dot(fox_ref[...].astype(MXU_DTYPE), wo_ref[:FOX_W, :])
        mix = mix + _dot(lru_ref[...].astype(MXU_DTYPE), wo_ref[FOX_W:, :])
        x1 = xh_ref[...] * g1_ref[...] + b1_ref[...]
        xhat, rstd = _layer_norm_stats(DN_ALPHA * x1 + mix)
        xhat_ref[...] = xhat
        xn_ref[...] = xhat * g2_ref[...] + b2_ref[...]
        rstd_ref[...] = jnp.broadcast_to(rstd, rstd_ref.shape)

    row = lambda i: (i, 0)
    const = lambda i: (0, 0)
    vec = pl.BlockSpec((1, D_MODEL), const)
    return _pcall(
        body, name=name, grid=(nt,),
        in_specs=[pl.BlockSpec((tm, FOX_W), row), pl.BlockSpec((tm, LRU_W), row),
                  pl.BlockSpec((D_MODEL, D_MODEL), const), pl.BlockSpec((tm, D_MODEL), row), vec, vec, vec, vec],
        out_specs=[pl.BlockSpec((tm, D_MODEL), row), pl.BlockSpec((tm, D_MODEL), row),
                   pl.BlockSpec((tm, LANES), row)],
        out_shape=[jax.ShapeDtypeStruct((T, D_MODEL), f32), jax.ShapeDtypeStruct((T, D_MODEL), f32),
                   jax.ShapeDtypeStruct((T, LANES), f32)],
        compiler_params=_params(1),
    )(fox, lru, wo, xhat1, g1, b1, g2, b2)


def _mix_out_bwd(dy, xhat, rstd, ln_g, fox, lru, wo, *, name, tm=512):
    T = fox.shape[0]
    tm = min(tm, T)
    nt = T // tm

    def body(dy_ref, xhat_ref, rstd_ref, g_ref, fox_ref, lru_ref, wo_ref,
             dyp_ref, dgam_ref, dbeta_ref, dlru_ref, dwo_ref, d_ref, doa_ref):
        i = pl.program_id(0)

        @pl.when(i == 0)
        def _():
            dwo_ref[...] = jnp.zeros_like(dwo_ref)
            dgam_ref[...] = jnp.zeros_like(dgam_ref)
            dbeta_ref[...] = jnp.zeros_like(dbeta_ref)

        dyp, dgam, dbeta = _ln_backward(dy_ref[...], xhat_ref[...], rstd_ref[:, 0:1], g_ref[...])
        dyp_ref[...] = dyp
        dgam_ref[...] += dgam
        dbeta_ref[...] += dbeta
        dmix = dyp.astype(MXU_DTYPE)
        dcat = _dot_nt(dmix, wo_ref[...])
        dlru_ref[...] = dcat[:, FOX_W:]
        low = _low_lanes((tm, LANES))
        for j in range(HEADS // 2):
            do2 = dcat[:, j * LANES:(j + 1) * LANES].astype(MXU_DTYPE).astype(f32)
            prod = do2 * fox_ref[:, j * LANES:(j + 1) * LANES]
            for odd in range(2):
                h = 2 * j + odd
                mine = jnp.where(low, _swap_lane_halves(prod) if odd else prod, 0.0)
                d_ref[h] = jnp.broadcast_to(jnp.sum(mine, axis=1, keepdims=True), (tm, LANES))
                doh = jnp.where(low, _swap_lane_halves(do2) if odd else do2, 0.0)
                doa_ref[:, h * LANES:(h + 1) * LANES] = doh.astype(doa_ref.dtype)
        dwo_ref[:FOX_W, :] += _dot_tn(fox_ref[...].astype(MXU_DTYPE), dmix)
        dwo_ref[FOX_W:, :] += _dot_tn(lru_ref[...].astype(MXU_DTYPE), dmix)

    row = lambda i: (i, 0)
    const = lambda i: (0, 0)
    return _pcall(
        body, name=name, grid=(nt,),
        in_specs=[pl.BlockSpec((tm, D_MODEL), row), pl.BlockSpec((tm, D_MODEL), row), pl.BlockSpec((tm, LANES), row),
                  pl.BlockSpec((1, D_MODEL), const),
                  pl.BlockSpec((tm, FOX_W), row), pl.BlockSpec((tm, LRU_W), row),
                  pl.BlockSpec((D_MODEL, D_MODEL), const)],
        out_specs=[pl.BlockSpec((tm, D_MODEL), row), pl.BlockSpec((1, D_MODEL), const), pl.BlockSpec((1, D_MODEL), const),
                   pl.BlockSpec((tm, LRU_W), row), pl.BlockSpec((D_MODEL, D_MODEL), const),
                   pl.BlockSpec((HEADS, tm, LANES), lambda i: (0, i, 0)), pl.BlockSpec((tm, HEADS * LANES), row)],
        out_shape=[jax.ShapeDtypeStruct((T, D_MODEL), f32), jax.ShapeDtypeStruct((1, D_MODEL), f32),
                   jax.ShapeDtypeStruct((1, D_MODEL), f32),
                   jax.ShapeDtypeStruct((T, LRU_W), f32), jax.ShapeDtypeStruct((D_MODEL, D_MODEL), f32),
                   jax.ShapeDtypeStruct((HEADS, T, LANES), f32), jax.ShapeDtypeStruct((T, HEADS * LANES), MXU_DTYPE)],
        compiler_params=_params(1),
    )(dy, xhat, rstd, ln_g, fox, lru, wo)


def make_wp(w_in):
    scale = jnp.concatenate([jnp.full((FOX_W,), 1.0 / math.sqrt(HEAD_DIM), w_in.dtype),
                             jnp.ones((IN_COLS - FOX_W,), w_in.dtype)])
    return jnp.pad(w_in * scale[None, :], ((0, 0), (0, Z_PAD - IN_COLS)))


def _block_diag(w):
    eye = jnp.eye(HEADS, dtype=w.dtype)
    return jnp.einsum("hij,hg->higj", w, eye).reshape(LRU_W, LRU_W)


def _block_diag_extract(m):
    m4 = m.reshape(HEADS, HEAD_DIM, HEADS, HEAD_DIM)
    return jnp.stack([m4[h, :, h, :] for h in range(HEADS)])


class _NoOverlap:
    def start_token(self):
        return None

    def late_weights(self, w, after):
        return dict(f1d=w["f1d"], wp=w["wp"], wo=w["wo"])

    def after_attention(self, after):
        return None

    def ffn2_weights(self, w, after):
        return w["f2g"], w["f2u"], w["f2d"]

    def ffn2_grads(self, grads):
        return None

    def ffn1_grads(self, grads):
        return None

    def mixer_grads(self, dwp, dwo, small, loss):
        return None

    def before_ffn1_bwd(self, after):
        return None


def _tied(a, token):
    return a if token is None else a + token[0, 0]


def _local_step(x, target, w, hooks=None):
    hooks = hooks or _NoOverlap()
    bfp = w["bfp"]
    wab = jnp.concatenate([_block_diag(w["rg_wa"]), _block_diag(w["rg_wx"])], axis=1).astype(MXU_DTYPE)
    bab = jnp.concatenate([w["rg_ba"].reshape(1, LRU_W), w["rg_bx"].reshape(1, LRU_W)], axis=1)

    xb0, g1a, u1a, h1a = _ffn_up(x, w["f1g"], w["f1u"], hooks.start_token(), name="ffn1_up")
    late = hooks.late_weights(w, [h1a])
    f1d, wp, wo = late["f1d"], late["wp"], late["wo"]
    xhat1, xn1, rstd1 = _ffn_down_ln(x, h1a, f1d, w["ln1_g"], w["ln1_b"], name="ffn1_down")
    qkv, lxg, fgb = _proj_in(xn1, wp, bfp, name="proj_in")
    qa, ka, va = _fox_prep(qkv, fgb, name="fox_prep")
    fox, lse = _fox_fwd(qa, ka, va, name="fox_fwd")
    token = hooks.after_attention([lse])
    lru, uconv, hs = _lru_fwd(lxg, w["conv_w"], _tied(w["conv_b"], token), wab, bab, w["lam"], name="lru_fwd")
    xhat2, x2, rstd2 = _mix_out(fox, lru, wo, xhat1, w["ln1_g"], w["ln1_b"], w["ln2_g"], w["ln2_b"], name="mix_out")
    f2g, f2u, f2d = hooks.ffn2_weights(w, [rstd2])
    xb2, g2a, u2a, dy3p, dln3g, dln3b, loss = _ffn_fwd_loss(x2, f2g, f2u, f2d, w["ln3_g"], w["ln3_b"], target,
                                                            name="ffn2_fwd_loss")

    dx2, df2g, df2u, df2d = _ffn_bwd(dy3p, xb2, g2a, u2a, f2g, f2u, f2d, name="ffn2_bwd")
    token = hooks.ffn2_grads([df2g, df2u, df2d])
    dy2p, dln2g, dln2b, dlru, dwo, drep, doa = _mix_out_bwd(dx2, xhat2, rstd2, _tied(w["ln2_g"], token), fox, lru, wo,
                                                            name="mix_out_bwd")
    dlxg, dwab, dbab, dcw, dcb, dlam = _lru_bwd(dlru, lxg, uconv, hs, w["conv_w"], wab, bab, w["lam"], name="lru_bwd")
    dqa, dka, dva = _fox_bwd(qa, ka, va, doa, lse, drep, name="fox_bwd")
    dy1p, dwp, dln1g, dln1b, dbf = _proj_in_bwd(dqa, dka, dva, dlxg, fgb, xn1, dy2p, wp, xhat1, rstd1, w["ln1_g"],
                                                name="proj_in_bwd")
    small = dict(
        ln1_g=dln1g, ln1_b=dln1b, ln2_g=dln2g, ln2_b=dln2b, ln3_g=dln3g, ln3_b=dln3b,
        b_forget=dbf[:, :HEADS], conv_w=dcw[:CONV_K], conv_b=dcb,
        rg_wa=_block_diag_extract(dwab[:, :LRU_W]), rg_wx=_block_diag_extract(dwab[:, LRU_W:]),
        rg_ba=dbab[:, :LRU_W].reshape(HEADS, HEAD_DIM), rg_bx=dbab[:, LRU_W:].reshape(HEADS, HEAD_DIM),
        lru_lambda=dlam,
    )
    hooks.before_ffn1_bwd([dln1b])
    token = hooks.mixer_grads(dwp, dwo, small, loss)
    dx_a, *grads_a = _ffn_bwd(dy1p, xb0, g1a, u1a, w["f1g"], w["f1u"], f1d, token, name="ffn1_bwd_a", part=0)
    token = hooks.ffn1_grads(grads_a)
    dx, *grads_b = _ffn_bwd(dy1p, xb0, g1a, u1a, w["f1g"], w["f1u"], f1d, token, name="ffn1_bwd_b", part=1,
                            dx_init=dx_a)

    grads = dict(f1=(grads_a, grads_b), f2g=df2g, f2u=df2u, f2d=df2d, wp=dwp, wo=dwo, **small)
    return loss, dx, grads


MESH = pl.DeviceIdType.MESH
HBM_SPEC = pl.BlockSpec(memory_space=pl.ANY)
VMEM_SPEC = pl.BlockSpec(memory_space=pltpu.VMEM)


def _position():
    return lax.axis_index("x"), lax.axis_index("y"), lax.axis_index("c")


def _other_chips(x, y):
    return [(1 - x, y), (x, 1 - y), (1 - x, 1 - y)]


def _all_gather_bf16(shards, *, name):
    n = len(shards)

    def body(*refs):
        ins, outs, stages = refs[:n], refs[n:2 * n], refs[2 * n:3 * n]
        send_sems, recv_sems, local_sems = refs[3 * n:]
        x, y, c = _position()
        me, sibling = (x, y, c), (x, y, 1 - c)
        chips = _other_chips(x, y)

        def rows(k, px, py, pc):
            r = shards[k].shape[0]
            m = r // 2
            return outs[k].at[pl.ds(pl.multiple_of((2 * px + py) * r + pc * m, 16), m), :]

        def copy(k, idx, block, to, src=None):
            return pltpu.make_async_remote_copy(
                src_ref=rows(k, *block) if src is None else src, dst_ref=rows(k, *block),
                send_sem=send_sems.at[7 * k + idx], recv_sem=recv_sems.at[7 * k + idx],
                device_id=to, device_id_type=MESH)

        started = []
        mine = []
        for k in range(n):
            m = shards[k].shape[0] // 2
            stages[k][...] = ins[k][pl.ds(pl.multiple_of(c * m, 16), m), :].astype(stages[k].dtype)
            cp = pltpu.make_async_copy(stages[k], rows(k, *me), local_sems.at[k])
            cp.start()
            mine.append(cp)
            first = [copy(k, 0, me, sibling, src=stages[k])]
            first += [copy(k, 1 + j, me, (*chip, c), src=stages[k]) for j, chip in enumerate(chips)]
            for cp in first:
                cp.start()
            started += first
        for k in range(n):
            for j, chip in enumerate(chips):
                copy(k, 1 + j, (*chip, c), me).wait_recv()
                fwd = copy(k, 4 + j, (*chip, c), sibling)
                fwd.start()
                started.append(fwd)
        for k in range(n):
            copy(k, 0, sibling, me).wait_recv()
            for j, chip in enumerate(chips):
                copy(k, 4 + j, (*chip, 1 - c), me).wait_recv()
        for cp in started:
            cp.wait_send()
        for cp in mine:
            cp.wait()

    return _pcall(
        body, name=name,
        in_specs=[VMEM_SPEC] * n, out_specs=[HBM_SPEC] * n,
        out_shape=[jax.ShapeDtypeStruct((N_SHARD * s.shape[0], s.shape[1]), MXU_DTYPE) for s in shards],
        scratch_shapes=[pltpu.VMEM((s.shape[0] // 2, s.shape[1]), MXU_DTYPE) for s in shards]
        + [pltpu.SemaphoreType.DMA((7 * n,)), pltpu.SemaphoreType.DMA((7 * n,)), pltpu.SemaphoreType.DMA((n,))],
        compiler_params=pltpu.CompilerParams(vmem_limit_bytes=VMEM_LIMIT),
    )(*shards)


def _swap_halves(gs, *, name):
    n = len(gs)

    def body(*refs):
        ins, outs = refs[:n], refs[n:2 * n]
        send_sems, recv_sems = refs[2 * n:]
        x, y, c = _position()
        cps = []
        for k in range(n):
            m = gs[k].shape[1] // 2
            src = ins[k].at[:, pl.ds(pl.multiple_of((1 - c) * m, 16), m), :]
            cp = pltpu.make_async_remote_copy(src_ref=src, dst_ref=outs[k], send_sem=send_sems.at[k],
                                              recv_sem=recv_sems.at[k], device_id=(x, y, 1 - c), device_id_type=MESH)
            cp.start()
            cps.append(cp)
        for cp in cps:
            cp.wait()

    return _pcall(
        body, name=name, in_specs=[HBM_SPEC] * n, out_specs=[HBM_SPEC] * n,
        out_shape=[jax.ShapeDtypeStruct((g.shape[0], g.shape[1] // 2, g.shape[2]), g.dtype) for g in gs],
        scratch_shapes=[pltpu.SemaphoreType.DMA((n,)), pltpu.SemaphoreType.DMA((n,))],
    )(*gs)


def _add_halves(gs, recvs, *, name, tm=256):
    n = len(gs)
    _, r, cdim = gs[0].shape
    m = r // 2
    tm = min(tm, m)
    nb = m // tm
    c_idx = lax.axis_index("c").astype(jnp.int32).reshape(1)

    def body(c_ref, *refs):
        for k in range(n):
            refs[2 * n + k][...] = (refs[k][...].astype(f32) + refs[n + k][...].astype(f32)).astype(refs[2 * n + k].dtype)

    mine = pl.BlockSpec((None, tm, cdim), lambda j, i, c_ref: (j, c_ref[0] * nb + i, 0))
    half = pl.BlockSpec((None, tm, cdim), lambda j, i, c_ref: (j, i, 0))
    return _pcall(
        body, name=name,
        grid_spec=pltpu.PrefetchScalarGridSpec(
            num_scalar_prefetch=1, grid=(N_SHARD, nb),
            in_specs=[mine] * n + [half] * n, out_specs=[half] * n),
        out_shape=[jax.ShapeDtypeStruct((N_SHARD, m, cdim), g.dtype) for g in gs],
        compiler_params=_params(2),
    )(c_idx, *gs, *recvs)


def _scatter_partials(ps, *, name):
    n = len(ps)

    def body(*refs):
        ins, outs = refs[:n], refs[n:2 * n]
        send_sems, recv_sems = refs[2 * n:]
        x, y, c = _position()
        me_chip = 2 * x + y
        cps = []
        for k in range(n):
            for j, (px, py) in enumerate(_other_chips(x, y)):
                cp = pltpu.make_async_remote_copy(
                    src_ref=ins[k].at[2 * px + py], dst_ref=outs[k].at[me_chip],
                    send_sem=send_sems.at[3 * k + j], recv_sem=recv_sems.at[3 * k + j],
                    device_id=(px, py, c), device_id_type=MESH)
                cp.start()
                cps.append(cp)
        for cp in cps:
            cp.wait()

    return _pcall(
        body, name=name, in_specs=[HBM_SPEC] * n, out_specs=[HBM_SPEC] * n,
        out_shape=[jax.ShapeDtypeStruct(p.shape, p.dtype) for p in ps],
        scratch_shapes=[pltpu.SemaphoreType.DMA((3 * n,)), pltpu.SemaphoreType.DMA((3 * n,))],
    )(*ps)


def _sum_slabs(ps, qs, *, name, tm=128):
    n = len(qs)
    _, m, cdim = qs[0].shape
    tm = min(tm, m)
    nb = m // tm
    assert m % tm == 0, (m, tm)
    where = jnp.stack([2 * lax.axis_index("x") + lax.axis_index("y"), lax.axis_index("c")]).astype(jnp.int32)

    def body(w_ref, *refs):
        for k in range(n):
            own, q1, q2, q3 = (refs[4 * k + t][...].astype(f32) for t in range(4))
            refs[4 * n + k][...] = ((own + q1) + q2) + q3

    def slab(flip):
        return pl.BlockSpec((None, tm, cdim), lambda i, w_ref: (jnp.bitwise_xor(w_ref[0], flip), i, 0))

    operands = []
    for p, q in zip(ps, qs):
        operands += [p, q, q, q]
    return _pcall(
        body, name=name,
        grid_spec=pltpu.PrefetchScalarGridSpec(
            num_scalar_prefetch=1, grid=(nb,),
            in_specs=[slab(0), slab(2), slab(1), slab(3)] * n,
            out_specs=[pl.BlockSpec((tm, cdim), lambda i, w_ref: (w_ref[1] * nb + i, 0))] * n),
        out_shape=[jax.ShapeDtypeStruct((2 * m, cdim), f32) for _ in qs],
        compiler_params=_params(1),
    )(where, *operands)


def _join_halves(fs, *, name):
    n = len(fs)

    def body(*refs):
        outs = refs[n:2 * n]
        send_sems, recv_sems = refs[2 * n:]
        x, y, c = _position()
        cps = []
        for k in range(n):
            m = fs[k].shape[0] // 2
            half = outs[k].at[pl.ds(pl.multiple_of(c * m, 8), m), :]
            cp = pltpu.make_async_remote_copy(src_ref=half, dst_ref=half, send_sem=send_sems.at[k],
                                              recv_sem=recv_sems.at[k], device_id=(x, y, 1 - c), device_id_type=MESH)
            cp.start()
            cps.append(cp)
        for cp in cps:
            cp.wait()

    return _pcall(
        body, name=name, in_specs=[HBM_SPEC] * n, out_specs=[HBM_SPEC] * n,
        out_shape=[jax.ShapeDtypeStruct(f.shape, f.dtype) for f in fs],
        input_output_aliases={k: k for k in range(n)},
        scratch_shapes=[pltpu.SemaphoreType.DMA((n,)), pltpu.SemaphoreType.DMA((n,))],
    )(*fs)


def _all_reduce_small(v, after=None, *, name):
    r = v.shape[0]
    extra = [] if after is None else [after]

    def body(v_ref, *refs):
        out_ref, buf, send_sems, recv_sems, local_sem = refs[len(extra):]
        x, y, c = _position()
        me, sibling = (x, y, c), (x, y, 1 - c)
        chips = _other_chips(x, y)

        def rows(px, py, pc):
            return buf.at[pl.ds(pl.multiple_of((4 * px + 2 * py + pc) * r, 8), r), :]

        def copy(k, block, to, src=None):
            return pltpu.make_async_remote_copy(
                src_ref=rows(*block) if src is None else src, dst_ref=rows(*block),
                send_sem=send_sems.at[k], recv_sem=recv_sems.at[k], device_id=to, device_id_type=MESH)

        mine = pltpu.make_async_copy(v_ref, rows(*me), local_sem)
        mine.start()
        first = [copy(0, me, sibling, src=v_ref)]
        first += [copy(1 + j, me, (*chip, c), src=v_ref) for j, chip in enumerate(chips)]
        for cp in first:
            cp.start()
        passed = [copy(4 + j, (*chip, c), sibling) for j, chip in enumerate(chips)]
        for j, chip in enumerate(chips):
            copy(1 + j, (*chip, c), me).wait_recv()
            passed[j].start()
        copy(0, sibling, me).wait_recv()
        for j, chip in enumerate(chips):
            copy(4 + j, (*chip, 1 - c), me).wait_recv()
        for cp in first + passed:
            cp.wait_send()
        mine.wait()
        acc = buf[0:r, :]
        for d in range(1, N_DEV):
            acc = acc + buf[d * r:(d + 1) * r, :]
        out_ref[...] = acc

    return _pcall(
        body, name=name, in_specs=[VMEM_SPEC] + [HBM_SPEC] * len(extra), out_specs=VMEM_SPEC,
        out_shape=jax.ShapeDtypeStruct((r, LANES), f32),
        scratch_shapes=[pltpu.VMEM((N_DEV * r, LANES), f32), pltpu.SemaphoreType.DMA((7,)),
                        pltpu.SemaphoreType.DMA((7,)), pltpu.SemaphoreType.DMA],
    )(v, *extra)


SEM_SPEC = pl.BlockSpec(memory_space=pltpu.SEMAPHORE)
HBM_ONLY = pl.BlockSpec(memory_space=pltpu.HBM)
EFFECT = pltpu.SideEffectType.DATAFLOW_SIDE_EFFECTING


def _sends(copies):
    return copies[0] if isinstance(copies, tuple) else copies


def _arrivals(copies):
    return copies[1] if isinstance(copies, tuple) else copies


def _split_start(bufs, copies_fn, n_sems, *, name):
    n = len(bufs)

    def body(*refs):
        send_sems, recv_sems = refs[n], refs[n + 1]
        thru = refs[n + 2:2 * n + 2]
        token = refs[2 * n + 2]
        for cp in _sends(copies_fn(thru, send_sems, recv_sems)):
            cp.start()
        token[...] = jnp.zeros_like(token)

    outs = _pcall(
        body, name=name,
        out_shape=(pltpu.SemaphoreType.DMA((n_sems,)), pltpu.SemaphoreType.DMA((n_sems,)),
                   *[pltpu.HBM(b.shape, b.dtype) for b in bufs], jax.ShapeDtypeStruct((8, LANES), f32)),
        in_specs=[HBM_ONLY] * n,
        out_specs=(SEM_SPEC, SEM_SPEC, *[HBM_ONLY] * n, VMEM_SPEC),
        input_output_aliases={k: 2 + k for k in range(n)},
        compiler_params=pltpu.CompilerParams(has_side_effects=EFFECT),
    )(*[pltpu.with_memory_space_constraint(b, pltpu.HBM) for b in bufs])
    return outs[0], outs[1], list(outs[2:2 + n]), outs[2 + n]


def _split_wait(thru, send_sems, recv_sems, after, copies_fn, *, name):
    n = len(thru)

    def body(*refs):
        copies = copies_fn(refs[:n], refs[n], refs[n + 1])
        for cp in _sends(copies):
            cp.wait_send()
        for cp in _arrivals(copies):
            cp.wait_recv()

    return list(_pcall(
        body, name=name,
        out_shape=tuple(pltpu.HBM(b.shape, b.dtype) for b in thru),
        in_specs=[HBM_ONLY] * n + [SEM_SPEC, SEM_SPEC] + [HBM_SPEC] * len(after),
        out_specs=tuple([HBM_ONLY] * n),
        input_output_aliases={k: k for k in range(n)},
        compiler_params=pltpu.CompilerParams(has_side_effects=EFFECT),
    )(*thru, send_sems, recv_sems, *after))


def _scatter_copies(n):
    def copies(bufs, send_sems, recv_sems):
        x, y, c = _position()
        me_chip = 2 * x + y
        cps = []
        for k in range(n):
            for j, (px, py) in enumerate(_other_chips(x, y)):
                cps.append(pltpu.make_async_remote_copy(
                    src_ref=bufs[k].at[2 * px + py], dst_ref=bufs[n + k].at[me_chip],
                    send_sem=send_sems.at[3 * k + j], recv_sem=recv_sems.at[3 * k + j],
                    device_id=(px, py, c), device_id_type=MESH))
        return cps
    return copies


N_PEERS = N_DEV - 1


def _direct_copies(n):
    def copies(bufs, send_sems, recv_sems):
        x, y, c = _position()
        me_chip = 2 * x + y
        sends, arrivals = [], []
        for k in range(n):
            m = bufs[k].shape[1] // 2
            land = bufs[n + k]

            def rows(slab, half, k=k, m=m):
                start = half * m if isinstance(half, int) else pl.multiple_of(half * m, 16)
                return bufs[k].at[slab, pl.ds(start, m), :]

            def copy(src, slot, send_idx, recv_idx, to, k=k, land=land):
                return pltpu.make_async_remote_copy(
                    src_ref=src, dst_ref=land.at[slot], send_sem=send_sems.at[N_PEERS * k + send_idx],
                    recv_sem=recv_sems.at[N_PEERS * k + recv_idx], device_id=to, device_id_type=MESH)

            sends.append(copy(rows(me_chip, 1 - c), 0, 0, 0, (x, y, 1 - c)))
            arrivals.append(copy(rows(me_chip, c), 0, 0, 0, (x, y, 1 - c)))
            for t, (px, py) in enumerate(_other_chips(x, y)):
                for core in range(2):
                    sends.append(copy(rows(2 * px + py, core), 1 + 2 * t + c, 1 + 2 * t + core, 1 + 2 * t + c,
                                      (px, py, core)))
                    arrivals.append(copy(rows(me_chip, c), 1 + 2 * t + core, 1 + 2 * t + core, 1 + 2 * t + core,
                                         (px, py, core)))
        return sends, arrivals
    return copies


def _sum_direct(gs, lands, *, name, tm=128):
    n = len(gs)
    _, m, cdim = lands[0].shape
    tm = min(tm, m)
    nb = m // tm
    assert m % tm == 0, (m, tm)
    where = jnp.stack([2 * lax.axis_index("x") + lax.axis_index("y"), lax.axis_index("c")]).astype(jnp.int32)

    def body(w_ref, *refs):
        for k in range(n):
            acc = refs[2 * k][...].astype(f32)
            for slot in range(N_PEERS):
                acc = acc + refs[2 * k + 1][slot].astype(f32)
            refs[2 * n + k][...] = acc

    own = pl.BlockSpec((None, tm, cdim), lambda i, w_ref: (w_ref[0], w_ref[1] * nb + i, 0))
    landed = pl.BlockSpec((N_PEERS, tm, cdim), lambda i, w_ref: (0, i, 0))
    operands = []
    for g, land in zip(gs, lands):
        operands += [g, land]
    return _pcall(
        body, name=name,
        grid_spec=pltpu.PrefetchScalarGridSpec(
            num_scalar_prefetch=1, grid=(nb,), in_specs=[own, landed] * n,
            out_specs=[pl.BlockSpec((tm, cdim), lambda i, w_ref: (w_ref[1] * nb + i, 0))] * n),
        out_shape=[jax.ShapeDtypeStruct((2 * m, cdim), f32) for _ in gs],
        compiler_params=_params(1),
    )(where, *operands)


def _broadcast_copies(bufs, send_sems, recv_sems):
    v, land = bufs
    x, y, c = _position()

    def copy(slot, send_idx, recv_idx, to):
        return pltpu.make_async_remote_copy(src_ref=v, dst_ref=land.at[slot], send_sem=send_sems.at[send_idx],
                                            recv_sem=recv_sems.at[recv_idx], device_id=to, device_id_type=MESH)

    sends = [copy(0, 0, 0, (x, y, 1 - c))]
    arrivals = [copy(0, 0, 0, (x, y, 1 - c))]
    for t, (px, py) in enumerate(_other_chips(x, y)):
        for core in range(2):
            sends.append(copy(1 + 2 * t + c, 1 + 2 * t + core, 1 + 2 * t + c, (px, py, core)))
            arrivals.append(copy(1 + 2 * t + core, 1 + 2 * t + core, 1 + 2 * t + core, (px, py, core)))
    return sends, arrivals


def _sum_in_device_order(v, land, *, name):
    r, cdim = v.shape
    x, y, c = _position()
    slots, mine = [], []
    for d in range(N_DEV):
        dx, dy, dc = d // 4, (d // 2) % 2, d % 2
        fx, fy = jnp.bitwise_xor(dx, x), jnp.bitwise_xor(dy, y)
        t = jnp.where(fx == 1, jnp.where(fy == 1, 2, 0), 1)
        slots.append(jnp.where(jnp.logical_and(fx == 0, fy == 0), 0, 1 + 2 * t + dc))
        mine.append(jnp.logical_and(jnp.logical_and(fx == 0, fy == 0), dc == c))
    table = jnp.stack(slots + mine).astype(jnp.int32)

    def body(tab_ref, v_ref, *refs):
        out_ref = refs[N_DEV]
        acc = None
        for d in range(N_DEV):
            term = jnp.where(tab_ref[N_DEV + d] == 1, v_ref[...], refs[d][...])
            acc = term if acc is None else acc + term
        out_ref[...] = acc

    whole = pl.BlockSpec((r, cdim), lambda i, tab_ref: (0, 0))
    landed = [pl.BlockSpec((None, r, cdim), functools.partial(lambda i, tab_ref, d: (tab_ref[d], 0, 0), d=d))
              for d in range(N_DEV)]
    return _pcall(
        body, name=name,
        grid_spec=pltpu.PrefetchScalarGridSpec(num_scalar_prefetch=1, grid=(1,), in_specs=[whole] + landed,
                                               out_specs=whole),
        out_shape=jax.ShapeDtypeStruct((r, cdim), f32),
        compiler_params=_params(1),
    )(table, v, *[land] * N_DEV)


def _block_rows(buf, px, py, pc):
    m = buf.shape[0] // N_DEV
    return buf.at[pl.ds(pl.multiple_of((4 * px + 2 * py + pc) * m, 16), m), :]


def _gather_ici_copies(n):
    def copies(bufs, send_sems, recv_sems):
        x, y, c = _position()
        cps = []
        for k in range(n):
            rows = _block_rows(bufs[k], x, y, c)
            targets = [(x, y, 1 - c)] + [(px, py, c) for px, py in _other_chips(x, y)]
            for j, to in enumerate(targets):
                cps.append(pltpu.make_async_remote_copy(
                    src_ref=rows, dst_ref=rows, send_sem=send_sems.at[4 * k + j], recv_sem=recv_sems.at[4 * k + j],
                    device_id=to, device_id_type=MESH))
        return cps
    return copies


def _gather_d2d_copies(n):
    def copies(bufs, send_sems, recv_sems):
        x, y, c = _position()
        cps = []
        for k in range(n):
            for j, (px, py) in enumerate(_other_chips(x, y)):
                rows = _block_rows(bufs[k], px, py, c)
                cps.append(pltpu.make_async_remote_copy(
                    src_ref=rows, dst_ref=rows, send_sem=send_sems.at[3 * k + j], recv_sem=recv_sems.at[3 * k + j],
                    device_id=(x, y, 1 - c), device_id_type=MESH))
        return cps
    return copies


def _cast_halves(shards, after, *, name):
    n = len(shards)
    where = jnp.stack([2 * lax.axis_index("x") + lax.axis_index("y"), lax.axis_index("c")]).astype(jnp.int32)

    def body(w_ref, *refs):
        for k in range(n):
            refs[n + 1 + k][...] = refs[k][...].astype(refs[n + 1 + k].dtype)

    def half(s):
        return (s.shape[0] // 2, s.shape[1])

    return _pcall(
        body, name=name,
        grid_spec=pltpu.PrefetchScalarGridSpec(
            num_scalar_prefetch=1, grid=(1,),
            in_specs=[pl.BlockSpec(half(s), lambda i, w_ref: (w_ref[1], 0)) for s in shards] + [HBM_SPEC],
            out_specs=[pl.BlockSpec(half(s), lambda i, w_ref: (2 * w_ref[0] + w_ref[1], 0)) for s in shards]),
        out_shape=[jax.ShapeDtypeStruct((N_SHARD * s.shape[0], s.shape[1]), MXU_DTYPE) for s in shards],
        compiler_params=_params(1),
    )(where, *shards, after)


class _SplitGather:
    def __init__(self, shards, after, tag):
        self.tag = tag
        self.n = len(shards)
        halves = _cast_halves(shards, after, name=f"{tag}_cast")
        self.ici = _split_start(halves, _gather_ici_copies(self.n), 4 * self.n, name=f"{tag}_ici_start")
        self.token = self.ici[3]

    def forward(self, after):
        send_sems, recv_sems, thru, _ = self.ici
        landed = _split_wait(thru, send_sems, recv_sems, after, _gather_ici_copies(self.n), name=f"{self.tag}_ici_wait")
        self.d2d = _split_start(landed, _gather_d2d_copies(self.n), 3 * self.n, name=f"{self.tag}_d2d_start")
        return self.d2d[3]

    def finish(self, after):
        send_sems, recv_sems, thru, _ = self.d2d
        return _split_wait(thru, send_sems, recv_sems, after, _gather_d2d_copies(self.n), name=f"{self.tag}_d2d_wait")


class _Overlap(_NoOverlap):
    def __init__(self, late_shards, ffn2_shards, after):
        self.late = _SplitGather(late_shards, after, "ag1")
        self.ffn2 = _SplitGather(ffn2_shards, self.late.token, "ag2")
        self.reduced = None
        self.ffn1_parts = []

    def start_token(self):
        return self.ffn2.token

    def late_weights(self, w, after):
        token = self.late.forward(after)
        f1d, w_in, wo = self.late.finish([token])
        w_in = w_in.reshape(N_SHARD, D_MODEL, IN_SHARD).transpose(1, 0, 2).reshape(D_MODEL, IN_COLS)
        return dict(f1d=f1d.reshape(N_SHARD, D_FF // N_SHARD, D_MODEL), wp=make_wp(w_in), wo=wo)

    def after_attention(self, after):
        return self.ffn2.forward(after)

    def ffn2_weights(self, w, after):
        full = self.ffn2.finish(after)
        fs = D_FF // N_SHARD
        return (full[0].reshape(N_SHARD, D_MODEL, fs), full[1].reshape(N_SHARD, D_MODEL, fs),
                full[2].reshape(N_SHARD, fs, D_MODEL))

    @staticmethod
    def _send_direct(grads, tag):
        lands = [lax.empty((N_PEERS, g.shape[1] // 2, g.shape[2]), g.dtype) for g in grads]
        return _split_start(list(grads) + lands, _direct_copies(len(grads)), N_PEERS * len(grads),
                            name=f"rs_direct_{tag}_start")

    def ffn2_grads(self, grads):
        self.scatter = self._send_direct(grads, "ffn2")
        return self.scatter[3]

    def ffn1_grads(self, grads):
        tag = "ffn1" + "ab"[len(self.ffn1_parts)]
        if not self.ffn1_parts:
            started = self._send_direct(grads, tag)
        else:
            recvs = _swap_halves(grads, name=f"rs_swap_{tag}")
            ps = list(_add_halves(grads[:2], recvs[:2], name=f"rs_add_{tag}_gu"))
            ps += list(_add_halves(grads[2:], recvs[2:], name=f"rs_add_{tag}_d"))
            lands = [lax.empty(p.shape, p.dtype) for p in ps]
            started = _split_start(ps + lands, _scatter_copies(3), 9, name=f"rs_scatter_{tag}_start")
        self.ffn1_parts.append((tag, started))
        return started[3]

    def ffn1_reduced(self, after):
        sums = []
        for direct, (tag, (send_sems, recv_sems, thru, _)) in zip((True, False), self.ffn1_parts):
            plan, add = (_direct_copies, _sum_direct) if direct else (_scatter_copies, _sum_slabs)
            done = _split_wait(thru, send_sems, recv_sems, after, plan(3), name=f"rs_{tag}_wait")
            sums += list(add(done[:2], done[3:5], name=f"rs_sum_{tag}_gu"))
            sums += list(add(done[2:3], done[5:], name=f"rs_sum_{tag}_d"))
        return sums

    def mixer_grads(self, dwp, dwo, small, loss):
        packed = jnp.concatenate([_pack_small(small), jnp.broadcast_to(loss, (8, LANES))], axis=0)
        land = lax.empty((N_PEERS,) + packed.shape, packed.dtype)
        self.small = _split_start([packed, land], _broadcast_copies, N_PEERS, name="ar_small_start")
        gwin = dwp[:, :IN_COLS].reshape(D_MODEL, N_SHARD, IN_SHARD).transpose(1, 0, 2).astype(GRAD_DTYPE)
        gwo = dwo.reshape(N_SHARD, D_MODEL // N_SHARD, D_MODEL).astype(GRAD_DTYPE)
        self.scatter_mix = self._send_direct([gwin, gwo], "mix")
        return self.small[3] + self.scatter_mix[3]

    def small_summed(self, after):
        send_sems, recv_sems, thru, _ = self.small
        packed, land = _split_wait(thru, send_sems, recv_sems, after, _broadcast_copies, name="ar_small_wait")
        summed = _sum_in_device_order(packed, land, name="ar_small_sum")
        return summed[:-8], summed[-8, 0]

    def mixer_reduced(self, after):
        send_sems, recv_sems, thru, _ = self.scatter_mix
        done = _split_wait(thru, send_sems, recv_sems, after, _direct_copies(2), name="rs_direct_mix_wait")
        return [_sum_direct([done[k]], [done[2 + k]], name=f"rs_sum_{tag}")[0] for k, tag in enumerate(["w_in", "w_out"])]

    def before_ffn1_bwd(self, after):
        send_sems, recv_sems, thru, _ = self.scatter
        n = len(thru) // 2
        done = _split_wait(thru, send_sems, recv_sems, after, _direct_copies(n), name="rs_direct_ffn2_wait")
        self.reduced = list(_sum_direct(done[:n], done[n:], name="rs_sum_ffn2"))


def _adamw(gs, ws, ms, vs, *, name, tm=256):
    n = len(gs)
    r, cdim = gs[0].shape
    tm = r if tm is None else min(tm, r)
    assert r % tm == 0, (r, tm)
    c1 = 1.0 / (1.0 - ADAM_B1 ** ADAM_STEP)
    c2 = 1.0 / (1.0 - ADAM_B2 ** ADAM_STEP)

    def body(*refs):
        for k in range(n):
            g = refs[k][...]
            w = refs[n + k][...]
            m = ADAM_B1 * refs[2 * n + k][...] + (1.0 - ADAM_B1) * g
            v = ADAM_B2 * refs[3 * n + k][...] + (1.0 - ADAM_B2) * (g * g)
            refs[4 * n + k][...] = g
            refs[5 * n + k][...] = -ADAM_LR * ((m * c1) / (jnp.sqrt(v * c2) + ADAM_EPS) + ADAM_WD * w)
            refs[6 * n + k][...] = m
            refs[7 * n + k][...] = v

    flat = pl.BlockSpec((tm, cdim), lambda i: (i, 0))
    like_w = flat if ws[0].ndim == 2 else pl.BlockSpec((None, tm, cdim), lambda i: (0, i, 0))
    outs = _pcall(
        body, name=name, grid=(r // tm,), in_specs=[flat] * n + [like_w] * (3 * n), out_specs=[like_w] * (4 * n),
        out_shape=[jax.ShapeDtypeStruct(ws[0].shape, f32)] * (4 * n),
        compiler_params=_params(1),
    )(*gs, *ws, *ms, *vs)
    return outs[:n], outs[n:2 * n], outs[2 * n:3 * n], outs[3 * n:]


BIG = ["ffn1_w_gate", "ffn1_w_up", "ffn1_w_down", "ffn2_w_gate", "ffn2_w_up", "ffn2_w_down"]
SMALL = ["ln1_g", "ln1_b", "b_forget", "conv_w", "conv_b", "rg_wa", "rg_ba", "rg_wx", "rg_bx", "lru_lambda",
         "ln2_g", "ln2_b", "ln3_g", "ln3_b"]
WEIGHTS = ["ffn1_w_gate", "ffn1_w_up", "ffn1_w_down", "ln1_g", "ln1_b", "w_in", "b_forget", "conv_w", "conv_b",
           "rg_wa", "rg_ba", "rg_wx", "rg_bx", "lru_lambda", "w_out", "ln2_g", "ln2_b",
           "ffn2_w_gate", "ffn2_w_up", "ffn2_w_down", "ln3_g", "ln3_b"]


def _pack_small(parts):
    rows = []
    for n in SMALL:
        flat = parts[n].reshape(-1)
        pad = (-flat.shape[0]) % LANES
        rows.append(jnp.pad(flat, (0, pad)).reshape(-1, LANES))
    packed = jnp.concatenate(rows, axis=0)
    return jnp.pad(packed, ((0, (-packed.shape[0]) % 8), (0, 0)))


def _unpack_small(packed, shapes):
    out, r0 = {}, 0
    for n in SMALL:
        size = math.prod(shapes[n])
        nr = -(-size // LANES)
        out[n] = packed[r0:r0 + nr].reshape(-1)[:size].reshape(shapes[n])
        r0 += nr
    return out


def kernel(x, ffn1_w_gate, ffn1_w_up, ffn1_w_down, ln1_g, ln1_b, w_in, b_forget, conv_w, conv_b, rg_wa, rg_ba, rg_wx, rg_bx, lru_lambda, w_out, ln2_g, ln2_b, ffn2_w_gate, ffn2_w_up, ffn2_w_down, ln3_g, ln3_b, loss_target, m_ffn1_w_gate, m_ffn1_w_up, m_ffn1_w_down, m_ln1_g, m_ln1_b, m_w_in, m_b_forget, m_conv_w, m_conv_b, m_rg_wa, m_rg_ba, m_rg_wx, m_rg_bx, m_lru_lambda, m_w_out, m_ln2_g, m_ln2_b, m_ffn2_w_gate, m_ffn2_w_up, m_ffn2_w_down, m_ln3_g, m_ln3_b, v_ffn1_w_gate, v_ffn1_w_up, v_ffn1_w_down, v_ln1_g, v_ln1_b, v_w_in, v_b_forget, v_conv_w, v_conv_b, v_rg_wa, v_rg_ba, v_rg_wx, v_rg_bx, v_lru_lambda, v_w_out, v_ln2_g, v_ln2_b, v_ffn2_w_gate, v_ffn2_w_up, v_ffn2_w_down, v_ln3_g, v_ln3_b):
    args = dict(locals())
    w = {n: args[n] for n in WEIGHTS}
    mom = {n: args["m_" + n] for n in WEIGHTS}
    var = {n: args["v_" + n] for n in WEIGHTS}
    chip = 2 * lax.axis_index("x") + lax.axis_index("y")

    g1 = _all_gather_bf16([w[n][0] for n in BIG[:2]], name="ag_ffn1_up")
    fs = D_FF // N_SHARD
    full = dict(
        f1g=g1[0].reshape(N_SHARD, D_MODEL, fs), f1u=g1[1].reshape(N_SHARD, D_MODEL, fs),
        bfp=jnp.pad(b_forget, ((0, 0), (0, LANES - HEADS))),
        ln1_g=ln1_g, ln1_b=ln1_b, ln2_g=ln2_g, ln2_b=ln2_b, ln3_g=ln3_g, ln3_b=ln3_b,
        conv_b=conv_b, rg_wa=rg_wa[0], rg_wx=rg_wx[0], rg_ba=rg_ba[0], rg_bx=rg_bx[0], lam=lru_lambda,
    )
    cw_place = lax.dynamic_update_slice(jnp.zeros((8, LRU_W), f32), conv_w[0] * 0.5, (0, chip * (LRU_W // N_SHARD)))
    cw_full = _all_reduce_small(cw_place.reshape(-1, LANES), g1[0], name="ag_conv_w")
    full["conv_w"] = cw_full.reshape(8, LRU_W)[:CONV_K]

    hooks = _Overlap([w["ffn1_w_down"][0], w["w_in"][0], w["w_out"][0]], [w[n][0] for n in BIG[3:]], cw_full)
    loss_rep, dx, g = _local_step(x[0], loss_target[0], full, hooks)

    token1 = hooks.ffn1_grads(g["f1"][1])
    red = _join_halves(hooks.reduced + hooks.mixer_reduced([token1]), name="rs_join_rest")
    grads = dict(zip(BIG[3:] + ["w_in", "w_out"], red))

    small_sum, loss = hooks.small_summed(red)
    small_shapes = {n: w[n].shape for n in SMALL}
    small_shapes["conv_w"] = (1, CONV_K, LRU_W)
    gs_red = _unpack_small(small_sum, small_shapes)
    gs_red["conv_w"] = lax.dynamic_slice(gs_red["conv_w"], (0, 0, chip * (LRU_W // N_SHARD)),
                                         (1, CONV_K, LRU_W // N_SHARD))
    grads.update(gs_red)

    delta, new_m, new_v = {}, {}, {}

    def adamw(names, name, **kw):
        g3, d, nm, nv = _adamw([grads[n] for n in names], [w[n] for n in names], [mom[n] for n in names],
                               [var[n] for n in names], name=name, **kw)
        for i, n in enumerate(names):
            grads[n], delta[n], new_m[n], new_v[n] = g3[i], d[i], nm[i], nv[i]

    adamw(BIG[3:], "adamw_ffn2", tm=128)
    adamw(["w_in"], "adamw_w_in")
    adamw(["w_out"], "adamw_w_out")
    shard_shapes = {n: w[n].shape for n in SMALL}
    _, d, nm, nv = _adamw([_pack_small({n: grads[n] for n in SMALL})], [_pack_small({n: w[n] for n in SMALL})],
                          [_pack_small({n: mom[n] for n in SMALL})], [_pack_small({n: var[n] for n in SMALL})],
                          name="adamw_small", tm=None)
    for dst, packed in ((delta, d[0]), (new_m, nm[0]), (new_v, nv[0])):
        dst.update(_unpack_small(packed, shard_shapes))

    worked = [new_v["ffn2_w_down"], new_v["w_in"], new_v["w_out"], nv[0]]
    ga, ua, da, gb, ub, db = _join_halves(hooks.ffn1_reduced(worked), name="rs_join_ffn1")
    grads["ffn1_w_gate"] = jnp.concatenate([ga, gb], axis=1)
    grads["ffn1_w_up"] = jnp.concatenate([ua, ub], axis=1)
    grads["ffn1_w_down"] = jnp.concatenate([da, db], axis=0)
    adamw(BIG[:3], "adamw_ffn1", tm=128)

    def shaped(tree, n):
        return tree[n].reshape(w[n].shape)

    return (loss, dx[None], *[shaped(grads, n) for n in WEIGHTS], *[shaped(delta, n) for n in WEIGHTS],
            *[shaped(new_m, n) for n in WEIGHTS], *[shaped(new_v, n) for n in WEIGHTS])
```

```python
import functools
import math

import jax
import jax.numpy as jnp
from jax import lax
from jax.experimental import pallas as pl
from jax.experimental.pallas import tpu as pltpu

f32 = jnp.float32
MXU_DTYPE = jnp.bfloat16
GRAD_DTYPE = jnp.bfloat16

D_MODEL = 1024
D_FF = 4096
N_SHARD = 4
N_DEV = 8
FOX_W = 512
LRU_W = 512
HEADS = 8
HEAD_DIM = 64
CONV_K = 4
IN_COLS = 2568
IN_SHARD = IN_COLS // N_SHARD
QKV_W = 3 * FOX_W
Z_PAD = 2688
LANES = 128
LN_EPS = 1e-5
DN_ALPHA = 2.0 ** 0.25
LRU_C = 8.0
NEG_BIG = -1e30
VMEM_LIMIT = 56 * 1024 * 1024

ADAM_LR = 0.001
ADAM_B1 = 0.9
ADAM_B2 = 0.999
ADAM_EPS = 1e-08
ADAM_WD = 0.01
ADAM_STEP = 10


def _pcall(body, **kw):
    return pl.pallas_call(body, **kw)


def _params(n_grid, vmem=VMEM_LIMIT):
    return pltpu.CompilerParams(dimension_semantics=("arbitrary",) * n_grid, vmem_limit_bytes=vmem)


def _dot(a, b):
    return jnp.dot(a, b, preferred_element_type=f32)


def _dot_nt(a, b):
    return lax.dot_general(a, b, (((1,), (1,)), ((), ())), preferred_element_type=f32)


def _dot_tn(a, b):
    return lax.dot_general(a, b, (((0,), (0,)), ((), ())), preferred_element_type=f32)


def _sigmoid(x):
    return 1.0 / (1.0 + jnp.exp(-x))


def _layer_norm_stats(y):
    mu = jnp.mean(y, axis=-1, keepdims=True)
    yc = y - mu
    var = jnp.mean(yc * yc, axis=-1, keepdims=True)
    rstd = lax.rsqrt(var + LN_EPS)
    return yc * rstd, rstd


def _ln_backward(dy, xhat, rstd, gamma):
    dxhat = dy * gamma
    m1 = jnp.mean(dxhat, axis=-1, keepdims=True)
    m2 = jnp.mean(dxhat * xhat, axis=-1, keepdims=True)
    dyp = rstd * (dxhat - m1 - xhat * m2)
    return dyp, jnp.sum(dy * xhat, axis=0, keepdims=True), jnp.sum(dy, axis=0, keepdims=True)


def _ffn_fwd_loss(x, wg, wu, wd, ln_g, ln_b, target, *, name, tm=1024, tf=512):
    T = x.shape[0]
    tm = min(tm, T)
    tr = min(256, tm)
    fs = D_FF // N_SHARD
    cpf = fs // tf
    nf = D_FF // tf
    nt = T // tm

    def body(x_ref, wg_ref, wu_ref, wd_ref, g_ref, b_ref, t_ref,
             xb_ref, gact_ref, uact_ref, dyp_ref, dgam_ref, dbeta_ref, loss_ref, acc_ref):
        i = pl.program_id(0)
        f = pl.program_id(1)

        @pl.when(jnp.logical_and(i == 0, f == 0))
        def _():
            dgam_ref[...] = jnp.zeros_like(dgam_ref)
            dbeta_ref[...] = jnp.zeros_like(dbeta_ref)
            loss_ref[...] = jnp.zeros_like(loss_ref)

        @pl.when(f == 0)
        def _():
            xb_ref[...] = x_ref[...].astype(MXU_DTYPE)
            acc_ref[...] = jnp.zeros_like(acc_ref)

        xb = xb_ref[...]
        g = _dot(xb, wg_ref[...])
        u = _dot(xb, wu_ref[...])
        h = (g * _sigmoid(g)) * u
        gact_ref[...] = g.astype(gact_ref.dtype)
        uact_ref[...] = u.astype(uact_ref.dtype)
        acc_ref[...] += _dot(h.astype(MXU_DTYPE), wd_ref[...])

        @pl.when(f == nf - 1)
        def _():
            gamma = g_ref[...]

            def rows_chunk(r, carry):
                rows = pl.ds(pl.multiple_of(r * tr, tr), tr)
                xhat, rstd = _layer_norm_stats(DN_ALPHA * x_ref[rows, :] + 0.5 * acc_ref[rows, :])
                err = xhat * gamma + b_ref[...] - t_ref[rows, :]
                sq = jnp.sum(jnp.sum(err * err, axis=0, keepdims=True), axis=1, keepdims=True)
                loss_ref[...] += jnp.broadcast_to(sq * (0.5 / D_MODEL), loss_ref.shape)
                dyp, dgam, dbeta = _ln_backward(err * (1.0 / D_MODEL), xhat, rstd, gamma)
                dyp_ref[rows, :] = dyp
                dgam_ref[...] += dgam
                dbeta_ref[...] += dbeta
                return carry

            lax.fori_loop(0, tm // tr, rows_chunk, 0)

    row = lambda i, f: (i, 0)
    const = lambda i, f: (0, 0)
    tile = pl.BlockSpec((tm, tf), lambda i, f: (i, f))
    cols = pl.BlockSpec((None, D_MODEL, tf), lambda i, f: (f // cpf, 0, f % cpf))
    last = lambda i, f: (jnp.where(f == nf - 1, i, jnp.maximum(i - 1, 0)), 0)
    return _pcall(
        body, name=name, grid=(nt, nf),
        in_specs=[pl.BlockSpec((tm, D_MODEL), row), cols, cols,
                  pl.BlockSpec((None, tf, D_MODEL), lambda i, f: (f // cpf, f % cpf, 0)),
                  pl.BlockSpec((1, D_MODEL), const), pl.BlockSpec((1, D_MODEL), const),
                  pl.BlockSpec((tm, D_MODEL), last)],
        out_specs=[pl.BlockSpec((tm, D_MODEL), row), tile, tile, pl.BlockSpec((tm, D_MODEL), row),
                   pl.BlockSpec((1, D_MODEL), const), pl.BlockSpec((1, D_MODEL), const), pl.BlockSpec((1, LANES), const)],
        out_shape=[jax.ShapeDtypeStruct((T, D_MODEL), MXU_DTYPE), jax.ShapeDtypeStruct((T, D_FF), MXU_DTYPE),
                   jax.ShapeDtypeStruct((T, D_FF), MXU_DTYPE), jax.ShapeDtypeStruct((T, D_MODEL), f32),
                   jax.ShapeDtypeStruct((1, D_MODEL), f32), jax.ShapeDtypeStruct((1, D_MODEL), f32),
                   jax.ShapeDtypeStruct((1, LANES), f32)],
        scratch_shapes=[pltpu.VMEM((tm, D_MODEL), f32)],
        compiler_params=_params(2),
    )(x, wg, wu, wd, ln_g, ln_b, target)


def _ffn_up(x, wg, wu, after=None, *, name, tm=1024, tf=512):
    T = x.shape[0]
    tm = min(tm, T)
    cpf = (D_FF // N_SHARD) // tf
    nf = D_FF // tf
    extra = [] if after is None else [after]

    def body(x_ref, wg_ref, wu_ref, *refs):
        xb_ref, gact_ref, uact_ref, hact_ref = refs[len(extra):]

        @pl.when(pl.program_id(1) == 0)
        def _():
            xb_ref[...] = x_ref[...].astype(MXU_DTYPE)

        xb = xb_ref[...]
        g = _dot(xb, wg_ref[...])
        u = _dot(xb, wu_ref[...])
        gact_ref[...] = g.astype(gact_ref.dtype)
        uact_ref[...] = u.astype(uact_ref.dtype)
        hact_ref[...] = ((g * _sigmoid(g)) * u).astype(hact_ref.dtype)

    row = lambda i, f: (i, 0)
    tile = pl.BlockSpec((tm, tf), lambda i, f: (i, f))
    cols = pl.BlockSpec((None, D_MODEL, tf), lambda i, f: (f // cpf, 0, f % cpf))
    return _pcall(
        body, name=name, grid=(T // tm, nf),
        in_specs=[pl.BlockSpec((tm, D_MODEL), row), cols, cols] + [pl.BlockSpec(memory_space=pl.ANY)] * len(extra),
        out_specs=[pl.BlockSpec((tm, D_MODEL), row), tile, tile, tile],
        out_shape=[jax.ShapeDtypeStruct((T, D_MODEL), MXU_DTYPE)] + [jax.ShapeDtypeStruct((T, D_FF), MXU_DTYPE)] * 3,
        compiler_params=_params(2),
    )(x, wg, wu, *extra)


def _ffn_down_ln(x, hact, wd, ln_g, ln_b, *, name, tm=1024):
    T = x.shape[0]
    tm = min(tm, T)
    fs = D_FF // N_SHARD

    def body(x_ref, h_ref, wd_ref, g_ref, b_ref, xhat_ref, xn_ref, rstd_ref, acc_ref):
        k = pl.program_id(1)

        @pl.when(k == 0)
        def _():
            acc_ref[...] = jnp.zeros_like(acc_ref)

        acc_ref[...] += _dot(h_ref[...], wd_ref[...])

        @pl.when(k == N_SHARD - 1)
        def _():
            xhat, rstd = _layer_norm_stats(DN_ALPHA * x_ref[...] + 0.5 * acc_ref[...])
            xhat_ref[...] = xhat
            xn_ref[...] = (xhat * g_ref[...] + b_ref[...]).astype(xn_ref.dtype)
            rstd_ref[...] = jnp.broadcast_to(rstd, rstd_ref.shape)

    row = lambda i, k: (i, 0)
    vec = pl.BlockSpec((1, D_MODEL), lambda i, k: (0, 0))
    return _pcall(
        body, name=name, grid=(T // tm, N_SHARD),
        in_specs=[pl.BlockSpec((tm, D_MODEL), row), pl.BlockSpec((tm, fs), lambda i, k: (i, k)),
                  pl.BlockSpec((None, fs, D_MODEL), lambda i, k: (k, 0, 0)), vec, vec],
        out_specs=[pl.BlockSpec((tm, D_MODEL), row), pl.BlockSpec((tm, D_MODEL), row), pl.BlockSpec((tm, LANES), row)],
        out_shape=[jax.ShapeDtypeStruct((T, D_MODEL), f32), jax.ShapeDtypeStruct((T, D_MODEL), MXU_DTYPE),
                   jax.ShapeDtypeStruct((T, LANES), f32)],
        scratch_shapes=[pltpu.VMEM((tm, D_MODEL), f32)],
        compiler_params=_params(2),
    )(x, hact, wd, ln_g, ln_b)


def _ffn_bwd(dyp, xb, gact, uact, wg, wu, wd, after=None, *, name, tm=512, tf=512, part=None, dx_init=None):
    T = dyp.shape[0]
    tm = min(tm, T)
    fs = D_FF // N_SHARD
    cpf = fs // tf
    nt = T // tm
    nf = D_FF // tf if part is None else N_SHARD
    wf = fs if part is None else tf
    slab = (lambda f: f // cpf) if part is None else (lambda f: f)
    chunk = (lambda f: f % cpf) if part is None else (lambda f: part)
    extra = ([] if dx_init is None else [dx_init]) + ([] if after is None else [after])

    def body(dyp_ref, xb_ref, g_ref, u_ref, wg_ref, wu_ref, wd_ref, *refs):
        dx_hbm, dwg_ref, dwu_ref, dwd_ref, dx_sc, dwg_sc, dwu_sc, dwd_sc, sem = refs[len(extra):]
        f = pl.program_id(0)
        i = pl.program_id(1)
        rows = pl.ds(pl.multiple_of(i * tm, tm), tm)
        dyp_t = dyp_ref[...]
        dy = (0.5 * dyp_t).astype(MXU_DTYPE)

        @pl.when(i == 0)
        def _():
            dwg_sc[...] = jnp.zeros_like(dwg_sc)
            dwu_sc[...] = jnp.zeros_like(dwu_sc)
            dwd_sc[...] = jnp.zeros_like(dwd_sc)

        @pl.when(f == 0)
        def _():
            dx_sc[rows, :] = DN_ALPHA * dyp_t if dx_init is None else refs[0][...]

        g = g_ref[...].astype(f32)
        u = u_ref[...].astype(f32)
        sig = _sigmoid(g)
        silu = g * sig
        dh = _dot_nt(dy, wd_ref[...])
        dg = (dh * u * (sig * (1.0 + g * (1.0 - sig)))).astype(MXU_DTYPE)
        du = (dh * silu).astype(MXU_DTYPE)
        hb = (silu * u).astype(MXU_DTYPE)
        dx_sc[rows, :] += _dot_nt(dg, wg_ref[...]) + _dot_nt(du, wu_ref[...])
        xb_t = xb_ref[...]
        dwg_sc[...] += _dot_tn(xb_t, dg)
        dwu_sc[...] += _dot_tn(xb_t, du)
        dwd_sc[...] += _dot_tn(hb, dy)

        @pl.when(i == nt - 1)
        def _():
            dwg_ref[...] = dwg_sc[...].astype(dwg_ref.dtype)
            dwu_ref[...] = dwu_sc[...].astype(dwu_ref.dtype)
            dwd_ref[...] = dwd_sc[...].astype(dwd_ref.dtype)

        @pl.when(jnp.logical_and(f == nf - 1, i == nt - 1))
        def _():
            cp = pltpu.make_async_copy(dx_sc, dx_hbm, sem)
            cp.start()
            cp.wait()

    row = lambda f, i: (i, 0)
    return _pcall(
        body, name=name, grid=(nf, nt),
        in_specs=[
            pl.BlockSpec((tm, D_MODEL), row),
            pl.BlockSpec((tm, D_MODEL), row),
            pl.BlockSpec((tm, tf), lambda f, i: (i, slab(f) * cpf + chunk(f))),
            pl.BlockSpec((tm, tf), lambda f, i: (i, slab(f) * cpf + chunk(f))),
            pl.BlockSpec((None, D_MODEL, tf), lambda f, i: (slab(f), 0, chunk(f))),
            pl.BlockSpec((None, D_MODEL, tf), lambda f, i: (slab(f), 0, chunk(f))),
            pl.BlockSpec((None, tf, D_MODEL), lambda f, i: (slab(f), chunk(f), 0)),
        ] + ([] if dx_init is None else [pl.BlockSpec((tm, D_MODEL), row)])
        + ([] if after is None else [pl.BlockSpec(memory_space=pl.ANY)]),
        out_specs=[
            pl.BlockSpec(memory_space=pl.ANY),
            pl.BlockSpec((None, D_MODEL, tf), lambda f, i: (slab(f), 0, chunk(f) if part is None else 0)),
            pl.BlockSpec((None, D_MODEL, tf), lambda f, i: (slab(f), 0, chunk(f) if part is None else 0)),
            pl.BlockSpec((None, tf, D_MODEL), lambda f, i: (slab(f), chunk(f) if part is None else 0, 0)),
        ],
        out_shape=[
            jax.ShapeDtypeStruct((T, D_MODEL), f32),
            jax.ShapeDtypeStruct((N_SHARD, D_MODEL, wf), GRAD_DTYPE),
            jax.ShapeDtypeStruct((N_SHARD, D_MODEL, wf), GRAD_DTYPE),
            jax.ShapeDtypeStruct((N_SHARD, wf, D_MODEL), GRAD_DTYPE),
        ],
        scratch_shapes=[pltpu.VMEM((T, D_MODEL), f32), pltpu.VMEM((D_MODEL, tf), f32),
                        pltpu.VMEM((D_MODEL, tf), f32), pltpu.VMEM((tf, D_MODEL), f32),
                        pltpu.SemaphoreType.DMA],
        compiler_params=_params(2),
    )(dyp, xb, gact, uact, wg, wu, wd, *extra)


def _proj_in(xn, wp, bfp, *, name, tm=512):
    T = xn.shape[0]
    tm = min(tm, T)
    nt = T // tm

    def body(x_ref, w_ref, b_ref, lxg_ref, fg_ref, qa_ref, ka_ref, va_ref, carry):
        i = pl.program_id(0)

        @pl.when(i == 0)
        def _():
            carry[...] = jnp.zeros_like(carry)

        z = _dot(x_ref[...], w_ref[...])
        lxg_ref[...] = z[:, QKV_W:QKV_W + 2 * LRU_W]
        fg = z[:, QKV_W + 2 * LRU_W:] + b_ref[...]
        fg_ref[...] = fg
        ls = jnp.minimum(fg, 0.0) - jnp.log(1.0 + jnp.exp(-jnp.abs(fg)))
        r = lax.broadcasted_iota(jnp.int32, (tm, tm), 0)
        c = lax.broadcasted_iota(jnp.int32, (tm, tm), 1)
        cum = _tri_dot(jnp.where(r >= c, 1.0, 0.0).astype(jnp.bfloat16), ls) + carry[0:1, :]
        carry[...] = jnp.broadcast_to(cum[tm - 1:tm, :], carry.shape)

        lane = lax.broadcasted_iota(jnp.int32, (tm, LANES), 1)
        low = lane < HEAD_DIM
        ones_q = jnp.where(jnp.logical_and(lane >= AUX + 3, lane < AUX + 6), 1.0, 0.0)
        ones_k = jnp.where(jnp.logical_and(lane >= AUX, lane < AUX + 3), 1.0, 0.0)
        for j in range(HEADS // 2):
            pair = [z[:, t * FOX_W + j * LANES:t * FOX_W + (j + 1) * LANES] for t in range(3)]
            for odd in range(2):
                h = 2 * j + odd
                q, k, v = [_swap_lane_halves(a) if odd else a for a in pair]
                hi, mid, lo = [a.astype(f32) for a in _split3(jnp.broadcast_to(cum[:, h:h + 1], (tm, LANES)))]
                aux_q = jnp.where(lane == AUX, hi, jnp.where(lane == AUX + 1, mid, jnp.where(lane == AUX + 2, lo, ones_q)))
                aux_k = jnp.where(lane == AUX + 3, -hi,
                                  jnp.where(lane == AUX + 4, -mid, jnp.where(lane == AUX + 5, -lo, ones_k)))
                blk = slice(h * LANES, (h + 1) * LANES)
                qa_ref[:, blk] = jnp.where(low, q, aux_q).astype(qa_ref.dtype)
                ka_ref[:, blk] = jnp.where(low, k, aux_k).astype(ka_ref.dtype)
                va_ref[:, blk] = jnp.where(low, v, 1.0).astype(va_ref.dtype)

    row = lambda i: (i, 0)
    const = lambda i: (0, 0)
    return _pcall(
        body, name=name, grid=(nt,),
        in_specs=[pl.BlockSpec((tm, D_MODEL), row), pl.BlockSpec((D_MODEL, Z_PAD), const),
                  pl.BlockSpec((1, LANES), const)],
        out_specs=[pl.BlockSpec((tm, 2 * LRU_W), row), pl.BlockSpec((tm, LANES), row)]
        + [pl.BlockSpec((tm, HEADS * LANES), row)] * 3,
        out_shape=[jax.ShapeDtypeStruct((T, 2 * LRU_W), f32), jax.ShapeDtypeStruct((T, LANES), f32)]
        + [jax.ShapeDtypeStruct((T, HEADS * LANES), MXU_DTYPE)] * 3,
        scratch_shapes=[pltpu.VMEM((8, LANES), f32)],
        compiler_params=_params(1),
    )(xn, wp, bfp)


def _proj_in_bwd(dqa, dka, dva, dlxg, fgb, xn, dyp, wp, xhat, rstd, ln_g, *, name, tm=512):
    T = xn.shape[0]
    tm = min(tm, T)
    nt = T // tm

    def body(dq_ref, dk_ref, dv_ref, dl_ref, fg_ref, x_ref, dyp_ref, w_ref, xhat_ref, rstd_ref, g_ref,
             dpre_ref, dw_hbm, dgam_ref, dbeta_ref, dbf_ref, dw_sc, carry, sem):
        i = pl.program_id(0)

        @pl.when(i == 0)
        def _():
            dw_sc[...] = jnp.zeros_like(dw_sc)
            dgam_ref[...] = jnp.zeros_like(dgam_ref)
            dbeta_ref[...] = jnp.zeros_like(dbeta_ref)
            dbf_ref[...] = jnp.zeros_like(dbf_ref)
            carry[...] = jnp.zeros_like(carry)

        lane = lax.broadcasted_iota(jnp.int32, (tm, LANES), 1)
        dc = jnp.zeros((tm, LANES), f32)
        for h in range(HEADS):
            row_sum = dq_ref[:, h * LANES + AUX:h * LANES + AUX + 1]
            col_sum = dk_ref[:, h * LANES + AUX + 3:h * LANES + AUX + 4]
            dc = jnp.where(lane == h, jnp.broadcast_to(row_sum - col_sum, (tm, LANES)), dc)
        r = lax.broadcasted_iota(jnp.int32, (tm, tm), 0)
        c = lax.broadcasted_iota(jnp.int32, (tm, tm), 1)
        dls = _tri_dot(jnp.where(c >= r, 1.0, 0.0).astype(jnp.bfloat16), dc) + carry[0:1, :]
        carry[...] = jnp.broadcast_to(dls[0:1, :], carry.shape)
        dfg = dls * _sigmoid(-fg_ref[...])
        dbf_ref[...] += jnp.sum(dfg, axis=0, keepdims=True)

        low = _low_lanes((tm, LANES))

        def packed(ref):
            pairs = [jnp.where(low, ref[:, (2 * j) * LANES:(2 * j + 1) * LANES],
                               _swap_lane_halves(ref[:, (2 * j + 1) * LANES:(2 * j + 2) * LANES]))
                     for j in range(HEADS // 2)]
            return jnp.concatenate(pairs, axis=1).astype(MXU_DTYPE)

        dz = jnp.concatenate(
            [packed(dq_ref), packed(dk_ref), packed(dv_ref),
             dl_ref[...].astype(MXU_DTYPE), dfg.astype(MXU_DTYPE)], axis=1)
        dx = DN_ALPHA * dyp_ref[...] + _dot_nt(dz, w_ref[...])
        dpre, dgam, dbeta = _ln_backward(dx, xhat_ref[...], rstd_ref[:, 0:1], g_ref[...])
        dpre_ref[...] = dpre
        dgam_ref[...] += dgam
        dbeta_ref[...] += dbeta
        dw_sc[...] += _dot_tn(x_ref[...], dz)

        @pl.when(i == nt - 1)
        def _():
            dw_sc[:, :FOX_W] = dw_sc[:, :FOX_W] * (1.0 / math.sqrt(HEAD_DIM))
            cp = pltpu.make_async_copy(dw_sc, dw_hbm, sem)
            cp.start()
            cp.wait()

    row = lambda i: (nt - 1 - i, 0)
    const = lambda i: (0, 0)
    return _pcall(
        body, name=name, grid=(nt,),
        in_specs=[pl.BlockSpec((tm, HEADS * LANES), row), pl.BlockSpec((tm, HEADS * LANES), row),
                  pl.BlockSpec((tm, HEADS * LANES), row),
                  pl.BlockSpec((tm, 2 * LRU_W), row), pl.BlockSpec((tm, LANES), row),
                  pl.BlockSpec((tm, D_MODEL), row), pl.BlockSpec((tm, D_MODEL), row),
                  pl.BlockSpec((D_MODEL, Z_PAD), const),
                  pl.BlockSpec((tm, D_MODEL), row), pl.BlockSpec((tm, LANES), row), pl.BlockSpec((1, D_MODEL), const)],
        out_specs=[pl.BlockSpec((tm, D_MODEL), row), pl.BlockSpec(memory_space=pl.ANY),
                   pl.BlockSpec((1, D_MODEL), const), pl.BlockSpec((1, D_MODEL), const), pl.BlockSpec((1, LANES), const)],
        out_shape=[jax.ShapeDtypeStruct((T, D_MODEL), f32), jax.ShapeDtypeStruct((D_MODEL, Z_PAD), f32),
                   jax.ShapeDtypeStruct((1, D_MODEL), f32), jax.ShapeDtypeStruct((1, D_MODEL), f32),
                   jax.ShapeDtypeStruct((1, LANES), f32)],
        scratch_shapes=[pltpu.VMEM((D_MODEL, Z_PAD), f32), pltpu.VMEM((8, LANES), f32), pltpu.SemaphoreType.DMA],
        compiler_params=_params(1),
    )(dqa, dka, dva, dlxg, fgb, xn, dyp, wp, xhat, rstd, ln_g)


def _split3(x):
    hi = x.astype(jnp.bfloat16)
    r1 = x - hi.astype(f32)
    mid = r1.astype(jnp.bfloat16)
    lo = (r1 - mid.astype(f32)).astype(jnp.bfloat16)
    return hi, mid, lo


def _tri_dot(tri, x):
    hi, mid, lo = _split3(x)
    return _dot(tri, hi) + _dot(tri, mid) + _dot(tri, lo)


FOX_PAD = HEADS * LANES
AUX = HEAD_DIM


def _low_lanes(shape):
    return lax.broadcasted_iota(jnp.int32, shape, 1) < HEAD_DIM


def _swap_lane_halves(x):
    return pltpu.roll(x, HEAD_DIM, 1)


def _future_keys(tq, tk):
    r = lax.broadcasted_iota(jnp.int32, (tq, tk), 0)
    c = lax.broadcasted_iota(jnp.int32, (tq, tk), 1)
    return c > r


def _causal_steps(nq, key_major):
    if key_major:
        pairs = [(qi, ki) for ki in range(nq) for qi in range(ki, nq)]
    else:
        pairs = [(qi, ki) for qi in range(nq) for ki in range(qi + 1)]
    return (jnp.asarray([p[0] for p in pairs], jnp.int32), jnp.asarray([p[1] for p in pairs], jnp.int32))


def _fox_fwd(qa, ka, va, *, name, tq=512, hps=8):
    T = qa.shape[0]
    tq = min(tq, T)
    tk = tq
    nq = T // tq
    rep = tk // LANES
    qi_tab, ki_tab = _causal_steps(nq, key_major=False)

    def body(qi_ref, ki_ref, qa_ref, ka_ref, va_ref, o_ref, lse_ref, m_sc, acc_sc):
        t = pl.program_id(1)
        qi = qi_ref[t]
        ki = ki_ref[t]

        @pl.when(ki == 0)
        def _():
            m_sc[...] = jnp.full_like(m_sc, NEG_BIG)
            acc_sc[...] = jnp.zeros_like(acc_sc)

        def tile(diagonal):
            for h in range(hps):
                blk = slice(h * LANES, (h + 1) * LANES)
                s = _dot_nt(qa_ref[:, blk], ka_ref[:, blk])
                if diagonal:
                    s = jnp.where(_future_keys(tq, tk), NEG_BIG, s)
                m_prev = m_sc[h]
                m_new = jnp.maximum(m_prev, jnp.max(s, axis=1, keepdims=True))
                p = jnp.exp(s - jnp.tile(m_new, (1, rep)))
                acc_sc[h] = jnp.exp(m_prev - m_new) * acc_sc[h] + _dot(p.astype(MXU_DTYPE), va_ref[:, blk])
                m_sc[h] = m_new

        @pl.when(ki < qi)
        def _():
            tile(False)

        @pl.when(ki == qi)
        def _():
            tile(True)
            low = _low_lanes((tq, LANES))
            outs = []
            for h in range(hps):
                acc = acc_sc[h]
                den = _swap_lane_halves(acc)
                outs.append(acc / den)
                lse_ref[h] = m_sc[h] + jnp.log(jnp.where(low, den, acc))
            for p in range(hps // 2):
                o_ref[:, p * LANES:(p + 1) * LANES] = jnp.where(low, outs[2 * p], _swap_lane_halves(outs[2 * p + 1]))

    pair = hps * LANES
    return _pcall(
        body, name=name,
        grid_spec=pltpu.PrefetchScalarGridSpec(
            num_scalar_prefetch=2, grid=(HEADS // hps, qi_tab.shape[0]),
            in_specs=[
                pl.BlockSpec((tq, pair), lambda j, t, qi_ref, ki_ref: (qi_ref[t], j)),
                pl.BlockSpec((tk, pair), lambda j, t, qi_ref, ki_ref: (ki_ref[t], j)),
                pl.BlockSpec((tk, pair), lambda j, t, qi_ref, ki_ref: (ki_ref[t], j)),
            ],
            out_specs=[pl.BlockSpec((tq, pair // 2), lambda j, t, qi_ref, ki_ref: (qi_ref[t], j)),
                       pl.BlockSpec((hps, tq, LANES), lambda j, t, qi_ref, ki_ref: (j, qi_ref[t], 0))],
            scratch_shapes=[pltpu.VMEM((hps, tq, LANES), f32)] * 2),
        out_shape=[jax.ShapeDtypeStruct((T, FOX_W), f32), jax.ShapeDtypeStruct((HEADS, T, LANES), f32)],
        compiler_params=_params(2),
    )(qi_tab, ki_tab, qa, ka, va)


def _fox_bwd(qa, ka, va, doa, lse, drep, *, name, tq=512, hps=4):
    T = qa.shape[0]
    tq = min(tq, T)
    tk = tq
    nq = T // tq
    rep = tk // LANES
    qi_tab, ki_tab = _causal_steps(nq, key_major=True)

    def body(qi_ref, ki_ref, qa_ref, ka_ref, va_ref, doa_ref, lse_ref, d_ref, dqa_ref, dka_ref, dva_ref, dk_sc, dv_sc):
        t = pl.program_id(1)
        qi = qi_ref[t]
        ki = ki_ref[t]
        rows = pl.ds(pl.multiple_of(qi * tq, tq), tq)

        @pl.when(t == 0)
        def _():
            dqa_ref[...] = jnp.zeros_like(dqa_ref)

        @pl.when(qi == ki)
        def _():
            dk_sc[...] = jnp.zeros_like(dk_sc)
            dv_sc[...] = jnp.zeros_like(dv_sc)

        def tile(diagonal):
            for h in range(hps):
                blk = slice(h * LANES, (h + 1) * LANES)
                qh, kh, doh = qa_ref[:, blk], ka_ref[:, blk], doa_ref[:, blk]
                p = jnp.exp(_dot_nt(qh, kh) - jnp.tile(lse_ref[h], (1, rep)))
                if diagonal:
                    p = jnp.where(_future_keys(tq, tk), 0.0, p)
                dp = _dot_nt(doh, va_ref[:, blk])
                ds = (p * (dp - jnp.tile(d_ref[h], (1, rep)))).astype(MXU_DTYPE)
                dv_sc[h] += _dot_tn(p.astype(MXU_DTYPE), doh)
                dk_sc[h] += _dot_tn(ds, qh)
                dqa_ref[rows, blk] += _dot(ds, kh)

        @pl.when(qi > ki)
        def _():
            tile(False)

        @pl.when(qi == ki)
        def _():
            tile(True)

        @pl.when(qi == nq - 1)
        def _():
            for h in range(hps):
                blk = slice(h * LANES, (h + 1) * LANES)
                dka_ref[:, blk] = dk_sc[h]
                dva_ref[:, blk] = dv_sc[h]

    pair = hps * LANES
    q_blk = lambda j, t, qi_ref, ki_ref: (qi_ref[t], j)
    k_blk = lambda j, t, qi_ref, ki_ref: (ki_ref[t], j)
    stat = pl.BlockSpec((hps, tq, LANES), lambda j, t, qi_ref, ki_ref: (j, qi_ref[t], 0))
    return _pcall(
        body, name=name,
        grid_spec=pltpu.PrefetchScalarGridSpec(
            num_scalar_prefetch=2, grid=(HEADS // hps, qi_tab.shape[0]),
            in_specs=[pl.BlockSpec((tq, pair), q_blk), pl.BlockSpec((tk, pair), k_blk), pl.BlockSpec((tk, pair), k_blk),
                      pl.BlockSpec((tq, pair), q_blk), stat, stat],
            out_specs=[pl.BlockSpec((T, pair), lambda j, t, qi_ref, ki_ref: (0, j)),
                       pl.BlockSpec((tk, pair), k_blk), pl.BlockSpec((tk, pair), k_blk)],
            scratch_shapes=[pltpu.VMEM((hps, tk, LANES), f32)] * 2),
        out_shape=[jax.ShapeDtypeStruct((T, FOX_PAD), f32)] * 3,
        compiler_params=_params(2),
    )(qi_tab, ki_tab, qa, ka, va, doa, lse, drep)


GELU_C = math.sqrt(2.0 / math.pi)
GELU_A = 0.044715


def _gelu(x):
    t = jnp.tanh(GELU_C * (x + GELU_A * x * x * x))
    return 0.5 * x * (1.0 + t), t


def _gelu_grad(x, t):
    return 0.5 * (1.0 + t) + 0.5 * x * (1.0 - t * t) * GELU_C * (1.0 + 3.0 * GELU_A * x * x)


EXPM1_SERIES_BELOW = 0.25


def _expm1(x, e):
    series = x * (1.0 + x * (1 / 2 + x * (1 / 6 + x * (1 / 24 + x * (1 / 120 + x * (1 / 720))))))
    return jnp.where(x > -EXPM1_SERIES_BELOW, series, e - 1.0)


def _lru_gates(u, wab_ref, bab_ref, lam_ref):
    pre = _dot(u.astype(MXU_DTYPE), wab_ref[...]) + bab_ref[...]
    r = _sigmoid(pre[:, :LRU_W])
    gi = _sigmoid(pre[:, LRU_W:])
    lam = lam_ref[...]
    sp = jnp.maximum(-lam, 0.0) + jnp.log(1.0 + jnp.exp(-jnp.abs(lam)))
    log_a = -LRU_C * r * sp
    a = jnp.exp(log_a)
    s = jnp.sqrt(-_expm1(2.0 * log_a, a * a))
    return r, gi, sp, a, s


def _lru_fwd(lxg, conv_w, conv_b, wab, bab, lam, *, name, tc=512):
    T = lxg.shape[0]
    tc = min(tc, T)
    nc = T // tc

    def body(lx_ref, lg_ref, cw_ref, cb_ref, wab_ref, bab_ref, lam_ref,
             out_ref, u_ref, hs_ref, ext, a_sc, b_sc, h_sc):
        i = pl.program_id(0)

        @pl.when(i == 0)
        def _():
            ext[0:8, :] = jnp.zeros((8, LRU_W), f32)
            h_sc[...] = jnp.zeros_like(h_sc)

        ext[8:, :] = lx_ref[...]
        u = cb_ref[...] + cw_ref[0:1, :] * ext[pl.ds(5, tc), :]
        for k in range(1, CONV_K):
            u = u + cw_ref[k:k + 1, :] * ext[pl.ds(5 + k, tc), :]
        ext[0:8, :] = ext[tc:tc + 8, :]
        u_ref[...] = u
        r, gi, sp, a, s = _lru_gates(u, wab_ref, bab_ref, lam_ref)
        a_sc[...] = a
        b_sc[...] = s * (gi * u)

        def step(t, h):
            h = a_sc[pl.ds(t, 1), :] * h + b_sc[pl.ds(t, 1), :]
            hs_ref[pl.ds(t, 1), :] = h
            return h

        h = lax.fori_loop(0, tc, step, h_sc[0:1, :], unroll=8)
        h_sc[...] = jnp.broadcast_to(h, h_sc.shape)
        gel, _ = _gelu(lg_ref[...])
        out_ref[...] = gel * hs_ref[...]

    row = lambda i: (i, 0)
    const = lambda i: (0, 0)
    return _pcall(
        body, name=name, grid=(nc,),
        in_specs=[pl.BlockSpec((tc, LRU_W), row), pl.BlockSpec((tc, LRU_W), lambda i: (i, 1)),
                  pl.BlockSpec((CONV_K, LRU_W), const), pl.BlockSpec((1, LRU_W), const),
                  pl.BlockSpec((LRU_W, 2 * LRU_W), const), pl.BlockSpec((1, 2 * LRU_W), const),
                  pl.BlockSpec((1, LRU_W), const)],
        out_specs=[pl.BlockSpec((tc, LRU_W), row)] * 3,
        out_shape=[jax.ShapeDtypeStruct((T, LRU_W), f32)] * 3,
        scratch_shapes=[pltpu.VMEM((tc + 8, LRU_W), f32), pltpu.VMEM((tc, LRU_W), f32),
                        pltpu.VMEM((tc, LRU_W), f32), pltpu.VMEM((8, LRU_W), f32)],
        compiler_params=_params(1),
    )(lxg, lxg, conv_w, conv_b, wab, bab, lam)


def _lru_bwd(dlru, lxg, u, hs, conv_w, wab, bab, lam, *, name, tc=512):
    T = lxg.shape[0]
    tc = min(tc, T)
    nc = T // tc
    bp = tc // 8

    def body(dl_ref, lx_ref, lxp_ref, lg_ref, u_ref, hs_ref, hsp_ref, cw_ref, wab_ref, bab_ref, lam_ref,
             dlxg_ref, dwab_ref, dbab_ref, dcw_ref, dcb_ref, dlam_ref,
             dh_sc, a_sc, ext, du_ext, carry):
        i = pl.program_id(0)
        first_chunk = i == nc - 1

        @pl.when(i == 0)
        def _():
            dwab_ref[...] = jnp.zeros_like(dwab_ref)
            dbab_ref[...] = jnp.zeros_like(dbab_ref)
            dcw_ref[...] = jnp.zeros_like(dcw_ref)
            dcb_ref[...] = jnp.zeros_like(dcb_ref)
            dlam_ref[...] = jnp.zeros_like(dlam_ref)
            carry[...] = jnp.zeros_like(carry)
            du_ext[tc:tc + 8, :] = jnp.zeros((8, LRU_W), f32)

        lg = lg_ref[...]
        gel, th = _gelu(lg)
        dl = dl_ref[...]
        hs = hs_ref[...]
        dlg = dl * hs * _gelu_grad(lg, th)
        u = u_ref[...]
        r, gi, sp, a, s = _lru_gates(u, wab_ref, bab_ref, lam_ref)
        a_sc[...] = a
        dh_sc[...] = dl * gel

        def step(k, c):
            t = tc - 1 - k
            dh = dh_sc[pl.ds(t, 1), :] + c
            dh_sc[pl.ds(t, 1), :] = dh
            return a_sc[pl.ds(t, 1), :] * dh

        c = lax.fori_loop(0, tc, step, carry[0:1, :], unroll=8)
        carry[...] = jnp.broadcast_to(c, carry.shape)

        ext[0:8, :] = jnp.where(first_chunk, 0.0, hsp_ref[...])
        ext[8:, :] = hs
        hprev = ext[pl.ds(7, tc), :]
        dh = dh_sc[...]
        da = dh * hprev
        giu = gi * u
        dla = da * a - (dh * giu) * (a * a / s)
        dgi = dh * s * u
        du = dh * s * gi
        dr = dla * (-LRU_C * sp)
        dlam_ref[...] += jnp.sum(dla * (-LRU_C * r), axis=0, keepdims=True) * (-_sigmoid(-lam_ref[...]))
        dpre = jnp.concatenate([dr * r * (1.0 - r), dgi * gi * (1.0 - gi)], axis=1)
        dpre_b = dpre.astype(MXU_DTYPE)
        du = du + _dot_nt(dpre_b, wab_ref[...])
        dwab_ref[...] += _dot_tn(u.astype(MXU_DTYPE), dpre_b)
        dbab_ref[...] += jnp.sum(dpre, axis=0, keepdims=True)
        dcb_ref[...] += jnp.sum(du, axis=0, keepdims=True)

        du_ext[0:tc, :] = du
        dlx = cw_ref[0:1, :] * du_ext[pl.ds(3, tc), :]
        for k in range(1, CONV_K):
            dlx = dlx + cw_ref[k:k + 1, :] * du_ext[pl.ds(3 - k, tc), :]
        du_ext[tc:tc + 8, :] = du_ext[0:8, :]
        ext[0:8, :] = jnp.where(first_chunk, 0.0, lxp_ref[...])
        ext[8:, :] = lx_ref[...]
        for k in range(CONV_K):
            dcw_ref[k:k + 1, :] += jnp.sum(du * ext[pl.ds(5 + k, tc), :], axis=0, keepdims=True)
        dlxg_ref[:, :LRU_W] = dlx.astype(dlxg_ref.dtype)
        dlxg_ref[:, LRU_W:] = dlg.astype(dlxg_ref.dtype)

    rev = lambda i: (nc - 1 - i, 0)
    prev8 = lambda i: (jnp.maximum((nc - 1 - i) * bp - 1, 0), 0)
    const = lambda i: (0, 0)
    return _pcall(
        body, name=name, grid=(nc,),
        in_specs=[
            pl.BlockSpec((tc, LRU_W), rev),
            pl.BlockSpec((tc, LRU_W), rev),
            pl.BlockSpec((8, LRU_W), prev8),
            pl.BlockSpec((tc, LRU_W), lambda i: (nc - 1 - i, 1)),
            pl.BlockSpec((tc, LRU_W), rev),
            pl.BlockSpec((tc, LRU_W), rev),
            pl.BlockSpec((8, LRU_W), prev8),
            pl.BlockSpec((CONV_K, LRU_W), const),
            pl.BlockSpec((LRU_W, 2 * LRU_W), const),
            pl.BlockSpec((1, 2 * LRU_W), const),
            pl.BlockSpec((1, LRU_W), const),
        ],
        out_specs=[
            pl.BlockSpec((tc, 2 * LRU_W), rev),
            pl.BlockSpec((LRU_W, 2 * LRU_W), const),
            pl.BlockSpec((1, 2 * LRU_W), const),
            pl.BlockSpec((8, LRU_W), const),
            pl.BlockSpec((1, LRU_W), const),
            pl.BlockSpec((1, LRU_W), const),
        ],
        out_shape=[
            jax.ShapeDtypeStruct((T, 2 * LRU_W), MXU_DTYPE),
            jax.ShapeDtypeStruct((LRU_W, 2 * LRU_W), f32),
            jax.ShapeDtypeStruct((1, 2 * LRU_W), f32),
            jax.ShapeDtypeStruct((8, LRU_W), f32),
            jax.ShapeDtypeStruct((1, LRU_W), f32),
            jax.ShapeDtypeStruct((1, LRU_W), f32),
        ],
        scratch_shapes=[pltpu.VMEM((tc, LRU_W), f32), pltpu.VMEM((tc, LRU_W), f32),
                        pltpu.VMEM((tc + 8, LRU_W), f32), pltpu.VMEM((tc + 8, LRU_W), f32),
                        pltpu.VMEM((8, LRU_W), f32)],
        compiler_params=_params(1),
    )(dlru, lxg, lxg, lxg, u, hs, hs, conv_w, wab, bab, lam)


def _mix_out(fox, lru, wo, xhat1, g1, b1, g2, b2, *, name, tm=512):
    T = fox.shape[0]
    tm = min(tm, T)
    nt = T // tm

    def body(fox_ref, lru_ref, wo_ref, xh_ref, g1_ref, b1_ref, g2_ref, b2_ref, xhat_ref, xn_ref, rstd_ref):
        mix = _dot(fox_ref[...].astype(MXU_DTYPE), wo_ref[:FOX_W, :])
        mix = mix + _dot(lru_ref[...].astype(MXU_DTYPE), wo_ref[FOX_W:, :])
        x1 = xh_ref[...] * g1_ref[...] + b1_ref[...]
        xhat, rstd = _layer_norm_stats(DN_ALPHA * x1 + mix)
        xhat_ref[...] = xhat
        xn_ref[...] = xhat * g2_ref[...] + b2_ref[...]
        rstd_ref[...] = jnp.broadcast_to(rstd, rstd_ref.shape)

    row = lambda i: (i, 0)
    const = lambda i: (0, 0)
    vec = pl.BlockSpec((1, D_MODEL), const)
    return _pcall(
        body, name=name, grid=(nt,),
        in_specs=[pl.BlockSpec((tm, FOX_W), row), pl.BlockSpec((tm, LRU_W), row),
                  pl.BlockSpec((D_MODEL, D_MODEL), const), pl.BlockSpec((tm, D_MODEL), row), vec, vec, vec, vec],
        out_specs=[pl.BlockSpec((tm, D_MODEL), row), pl.BlockSpec((tm, D_MODEL), row),
                   pl.BlockSpec((tm, LANES), row)],
        out_shape=[jax.ShapeDtypeStruct((T, D_MODEL), f32), jax.ShapeDtypeStruct((T, D_MODEL), f32),
                   jax.ShapeDtypeStruct((T, LANES), f32)],
        compiler_params=_params(1),
    )(fox, lru, wo, xhat1, g1, b1, g2, b2)


def _mix_out_bwd(dy, xhat, rstd, ln_g, fox, lru, wo, *, name, tm=512):
    T = fox.shape[0]
    tm = min(tm, T)
    nt = T // tm

    def body(dy_ref, xhat_ref, rstd_ref, g_ref, fox_ref, lru_ref, wo_ref,
             dyp_ref, dgam_ref, dbeta_ref, dlru_ref, dwo_ref, d_ref, doa_ref):
        i = pl.program_id(0)

        @pl.when(i == 0)
        def _():
            dwo_ref[...] = jnp.zeros_like(dwo_ref)
            dgam_ref[...] = jnp.zeros_like(dgam_ref)
            dbeta_ref[...] = jnp.zeros_like(dbeta_ref)

        dyp, dgam, dbeta = _ln_backward(dy_ref[...], xhat_ref[...], rstd_ref[:, 0:1], g_ref[...])
        dyp_ref[...] = dyp
        dgam_ref[...] += dgam
        dbeta_ref[...] += dbeta
        dmix = dyp.astype(MXU_DTYPE)
        dcat = _dot_nt(dmix, wo_ref[...])
        dlru_ref[...] = dcat[:, FOX_W:]
        low = _low_lanes((tm, LANES))
        for j in range(HEADS // 2):
            do2 = dcat[:, j * LANES:(j + 1) * LANES].astype(MXU_DTYPE).astype(f32)
            prod = do2 * fox_ref[:, j * LANES:(j + 1) * LANES]
            for odd in range(2):
                h = 2 * j + odd
                mine = jnp.where(low, _swap_lane_halves(prod) if odd else prod, 0.0)
                d_ref[h] = jnp.broadcast_to(jnp.sum(mine, axis=1, keepdims=True), (tm, LANES))
                doh = jnp.where(low, _swap_lane_halves(do2) if odd else do2, 0.0)
                doa_ref[:, h * LANES:(h + 1) * LANES] = doh.astype(doa_ref.dtype)
        dwo_ref[:FOX_W, :] += _dot_tn(fox_ref[...].astype(MXU_DTYPE), dmix)
        dwo_ref[FOX_W:, :] += _dot_tn(lru_ref[...].astype(MXU_DTYPE), dmix)

    row = lambda i: (i, 0)
    const = lambda i: (0, 0)
    return _pcall(
        body, name=name, grid=(nt,),
        in_specs=[pl.BlockSpec((tm, D_MODEL), row), pl.BlockSpec((tm, D_MODEL), row), pl.BlockSpec((tm, LANES), row),
                  pl.BlockSpec((1, D_MODEL), const),
                  pl.BlockSpec((tm, FOX_W), row), pl.BlockSpec((tm, LRU_W), row),
                  pl.BlockSpec((D_MODEL, D_MODEL), const)],
        out_specs=[pl.BlockSpec((tm, D_MODEL), row), pl.BlockSpec((1, D_MODEL), const), pl.BlockSpec((1, D_MODEL), const),
                   pl.BlockSpec((tm, LRU_W), row), pl.BlockSpec((D_MODEL, D_MODEL), const),
                   pl.BlockSpec((HEADS, tm, LANES), lambda i: (0, i, 0)), pl.BlockSpec((tm, HEADS * LANES), row)],
        out_shape=[jax.ShapeDtypeStruct((T, D_MODEL), f32), jax.ShapeDtypeStruct((1, D_MODEL), f32),
                   jax.ShapeDtypeStruct((1, D_MODEL), f32),
                   jax.ShapeDtypeStruct((T, LRU_W), f32), jax.ShapeDtypeStruct((D_MODEL, D_MODEL), f32),
                   jax.ShapeDtypeStruct((HEADS, T, LANES), f32), jax.ShapeDtypeStruct((T, HEADS * LANES), MXU_DTYPE)],
        compiler_params=_params(1),
    )(dy, xhat, rstd, ln_g, fox, lru, wo)


def make_wp(w_in):
    scale = jnp.concatenate([jnp.full((FOX_W,), 1.0 / math.sqrt(HEAD_DIM), w_in.dtype),
                             jnp.ones((IN_COLS - FOX_W,), w_in.dtype)])
    return jnp.pad(w_in * scale[None, :], ((0, 0), (0, Z_PAD - IN_COLS)))


def _block_diag(w):
    eye = jnp.eye(HEADS, dtype=w.dtype)
    return jnp.einsum("hij,hg->higj", w, eye).reshape(LRU_W, LRU_W)


def _block_diag_extract(m):
    m4 = m.reshape(HEADS, HEAD_DIM, HEADS, HEAD_DIM)
    return jnp.stack([m4[h, :, h, :] for h in range(HEADS)])


class _NoOverlap:
    def start_token(self):
        return None

    def late_weights(self, w, after):
        return dict(f1d=w["f1d"], wp=w["wp"], wo=w["wo"])

    def after_attention(self, after):
        return None

    def ffn2_weights(self, w, after):
        return w["f2g"], w["f2u"], w["f2d"]

    def ffn2_grads(self, grads):
        return None

    def ffn1_grads(self, grads):
        return None

    def mixer_grads(self, dwp, dwo, small, loss):
        return None

    def before_ffn1_bwd(self, after):
        return None


def _tied(a, token):
    return a if token is None else a + token[0, 0]


def _local_step(x, target, w, hooks=None):
    hooks = hooks or _NoOverlap()
    bfp = w["bfp"]
    wab = jnp.concatenate([_block_diag(w["rg_wa"]), _block_diag(w["rg_wx"])], axis=1).astype(MXU_DTYPE)
    bab = jnp.concatenate([w["rg_ba"].reshape(1, LRU_W), w["rg_bx"].reshape(1, LRU_W)], axis=1)

    xb0, g1a, u1a, h1a = _ffn_up(x, w["f1g"], w["f1u"], hooks.start_token(), name="ffn1_up")
    late = hooks.late_weights(w, [h1a])
    f1d, wp, wo = late["f1d"], late["wp"], late["wo"]
    xhat1, xn1, rstd1 = _ffn_down_ln(x, h1a, f1d, w["ln1_g"], w["ln1_b"], name="ffn1_down")
    lxg, fgb, qa, ka, va = _proj_in(xn1, wp, bfp, name="proj_in")
    fox, lse = _fox_fwd(qa, ka, va, name="fox_fwd")
    token = hooks.after_attention([lse])
    lru, uconv, hs = _lru_fwd(lxg, w["conv_w"], _tied(w["conv_b"], token), wab, bab, w["lam"], name="lru_fwd")
    xhat2, x2, rstd2 = _mix_out(fox, lru, wo, xhat1, w["ln1_g"], w["ln1_b"], w["ln2_g"], w["ln2_b"], name="mix_out")
    f2g, f2u, f2d = hooks.ffn2_weights(w, [rstd2])
    xb2, g2a, u2a, dy3p, dln3g, dln3b, loss = _ffn_fwd_loss(x2, f2g, f2u, f2d, w["ln3_g"], w["ln3_b"], target,
                                                            name="ffn2_fwd_loss")

    dx2, df2g, df2u, df2d = _ffn_bwd(dy3p, xb2, g2a, u2a, f2g, f2u, f2d, name="ffn2_bwd")
    token = hooks.ffn2_grads([df2g, df2u, df2d])
    dy2p, dln2g, dln2b, dlru, dwo, drep, doa = _mix_out_bwd(dx2, xhat2, rstd2, _tied(w["ln2_g"], token), fox, lru, wo,
                                                            name="mix_out_bwd")
    dlxg, dwab, dbab, dcw, dcb, dlam = _lru_bwd(dlru, lxg, uconv, hs, w["conv_w"], wab, bab, w["lam"], name="lru_bwd")
    dqa, dka, dva = _fox_bwd(qa, ka, va, doa, lse, drep, name="fox_bwd")
    dy1p, dwp, dln1g, dln1b, dbf = _proj_in_bwd(dqa, dka, dva, dlxg, fgb, xn1, dy2p, wp, xhat1, rstd1, w["ln1_g"],
                                                name="proj_in_bwd")
    small = dict(
        ln1_g=dln1g, ln1_b=dln1b, ln2_g=dln2g, ln2_b=dln2b, ln3_g=dln3g, ln3_b=dln3b,
        b_forget=dbf[:, :HEADS], conv_w=dcw[:CONV_K], conv_b=dcb,
        rg_wa=_block_diag_extract(dwab[:, :LRU_W]), rg_wx=_block_diag_extract(dwab[:, LRU_W:]),
        rg_ba=dbab[:, :LRU_W].reshape(HEADS, HEAD_DIM), rg_bx=dbab[:, LRU_W:].reshape(HEADS, HEAD_DIM),
        lru_lambda=dlam,
    )
    hooks.before_ffn1_bwd([dln1b])
    token = hooks.mixer_grads(dwp, dwo, small, loss)
    dx_a, *grads_a = _ffn_bwd(dy1p, xb0, g1a, u1a, w["f1g"], w["f1u"], f1d, token, name="ffn1_bwd_a", part=0)
    token = hooks.ffn1_grads(grads_a)
    dx, *grads_b = _ffn_bwd(dy1p, xb0, g1a, u1a, w["f1g"], w["f1u"], f1d, token, name="ffn1_bwd_b", part=1,
                            dx_init=dx_a)

    grads = dict(f1=(grads_a, grads_b), f2g=df2g, f2u=df2u, f2d=df2d, wp=dwp, wo=dwo, **small)
    return loss, dx, grads


MESH = pl.DeviceIdType.MESH
HBM_SPEC = pl.BlockSpec(memory_space=pl.ANY)
VMEM_SPEC = pl.BlockSpec(memory_space=pltpu.VMEM)


def _position():
    return lax.axis_index("x"), lax.axis_index("y"), lax.axis_index("c")


def _other_chips(x, y):
    return [(1 - x, y), (x, 1 - y), (1 - x, 1 - y)]


def _all_gather_bf16(shards, *, name):
    n = len(shards)

    def body(*refs):
        ins, outs, stages = refs[:n], refs[n:2 * n], refs[2 * n:3 * n]
        send_sems, recv_sems, local_sems = refs[3 * n:]
        x, y, c = _position()
        me, sibling = (x, y, c), (x, y, 1 - c)
        chips = _other_chips(x, y)

        def rows(k, px, py, pc):
            r = shards[k].shape[0]
            m = r // 2
            return outs[k].at[pl.ds(pl.multiple_of((2 * px + py) * r + pc * m, 16), m), :]

        def copy(k, idx, block, to, src=None):
            return pltpu.make_async_remote_copy(
                src_ref=rows(k, *block) if src is None else src, dst_ref=rows(k, *block),
                send_sem=send_sems.at[7 * k + idx], recv_sem=recv_sems.at[7 * k + idx],
                device_id=to, device_id_type=MESH)

        started = []
        mine = []
        for k in range(n):
            m = shards[k].shape[0] // 2
            stages[k][...] = ins[k][pl.ds(pl.multiple_of(c * m, 16), m), :].astype(stages[k].dtype)
            cp = pltpu.make_async_copy(stages[k], rows(k, *me), local_sems.at[k])
            cp.start()
            mine.append(cp)
            first = [copy(k, 0, me, sibling, src=stages[k])]
            first += [copy(k, 1 + j, me, (*chip, c), src=stages[k]) for j, chip in enumerate(chips)]
            for cp in first:
                cp.start()
            started += first
        for k in range(n):
            for j, chip in enumerate(chips):
                copy(k, 1 + j, (*chip, c), me).wait_recv()
                fwd = copy(k, 4 + j, (*chip, c), sibling)
                fwd.start()
                started.append(fwd)
        for k in range(n):
            copy(k, 0, sibling, me).wait_recv()
            for j, chip in enumerate(chips):
                copy(k, 4 + j, (*chip, 1 - c), me).wait_recv()
        for cp in started:
            cp.wait_send()
        for cp in mine:
            cp.wait()

    return _pcall(
        body, name=name,
        in_specs=[VMEM_SPEC] * n, out_specs=[HBM_SPEC] * n,
        out_shape=[jax.ShapeDtypeStruct((N_SHARD * s.shape[0], s.shape[1]), MXU_DTYPE) for s in shards],
        scratch_shapes=[pltpu.VMEM((s.shape[0] // 2, s.shape[1]), MXU_DTYPE) for s in shards]
        + [pltpu.SemaphoreType.DMA((7 * n,)), pltpu.SemaphoreType.DMA((7 * n,)), pltpu.SemaphoreType.DMA((n,))],
        compiler_params=pltpu.CompilerParams(vmem_limit_bytes=VMEM_LIMIT),
    )(*shards)


def _swap_halves(gs, *, name):
    n = len(gs)

    def body(*refs):
        ins, outs = refs[:n], refs[n:2 * n]
        send_sems, recv_sems = refs[2 * n:]
        x, y, c = _position()
        cps = []
        for k in range(n):
            m = gs[k].shape[1] // 2
            src = ins[k].at[:, pl.ds(pl.multiple_of((1 - c) * m, 16), m), :]
            cp = pltpu.make_async_remote_copy(src_ref=src, dst_ref=outs[k], send_sem=send_sems.at[k],
                                              recv_sem=recv_sems.at[k], device_id=(x, y, 1 - c), device_id_type=MESH)
            cp.start()
            cps.append(cp)
        for cp in cps:
            cp.wait()

    return _pcall(
        body, name=name, in_specs=[HBM_SPEC] * n, out_specs=[HBM_SPEC] * n,
        out_shape=[jax.ShapeDtypeStruct((g.shape[0], g.shape[1] // 2, g.shape[2]), g.dtype) for g in gs],
        scratch_shapes=[pltpu.SemaphoreType.DMA((n,)), pltpu.SemaphoreType.DMA((n,))],
    )(*gs)


def _add_halves(gs, recvs, *, name, tm=256):
    n = len(gs)
    _, r, cdim = gs[0].shape
    m = r // 2
    tm = min(tm, m)
    nb = m // tm
    c_idx = lax.axis_index("c").astype(jnp.int32).reshape(1)

    def body(c_ref, *refs):
        for k in range(n):
            refs[2 * n + k][...] = (refs[k][...].astype(f32) + refs[n + k][...].astype(f32)).astype(refs[2 * n + k].dtype)

    mine = pl.BlockSpec((None, tm, cdim), lambda j, i, c_ref: (j, c_ref[0] * nb + i, 0))
    half = pl.BlockSpec((None, tm, cdim), lambda j, i, c_ref: (j, i, 0))
    return _pcall(
        body, name=name,
        grid_spec=pltpu.PrefetchScalarGridSpec(
            num_scalar_prefetch=1, grid=(N_SHARD, nb),
            in_specs=[mine] * n + [half] * n, out_specs=[half] * n),
        out_shape=[jax.ShapeDtypeStruct((N_SHARD, m, cdim), g.dtype) for g in gs],
        compiler_params=_params(2),
    )(c_idx, *gs, *recvs)


def _scatter_partials(ps, *, name):
    n = len(ps)

    def body(*refs):
        ins, outs = refs[:n], refs[n:2 * n]
        send_sems, recv_sems = refs[2 * n:]
        x, y, c = _position()
        me_chip = 2 * x + y
        cps = []
        for k in range(n):
            for j, (px, py) in enumerate(_other_chips(x, y)):
                cp = pltpu.make_async_remote_copy(
                    src_ref=ins[k].at[2 * px + py], dst_ref=outs[k].at[me_chip],
                    send_sem=send_sems.at[3 * k + j], recv_sem=recv_sems.at[3 * k + j],
                    device_id=(px, py, c), device_id_type=MESH)
                cp.start()
                cps.append(cp)
        for cp in cps:
            cp.wait()

    return _pcall(
        body, name=name, in_specs=[HBM_SPEC] * n, out_specs=[HBM_SPEC] * n,
        out_shape=[jax.ShapeDtypeStruct(p.shape, p.dtype) for p in ps],
        scratch_shapes=[pltpu.SemaphoreType.DMA((3 * n,)), pltpu.SemaphoreType.DMA((3 * n,))],
    )(*ps)


def _sum_slabs(ps, qs, *, name, tm=128):
    n = len(qs)
    _, m, cdim = qs[0].shape
    tm = min(tm, m)
    nb = m // tm
    assert m % tm == 0, (m, tm)
    where = jnp.stack([2 * lax.axis_index("x") + lax.axis_index("y"), lax.axis_index("c")]).astype(jnp.int32)

    def body(w_ref, *refs):
        for k in range(n):
            own, q1, q2, q3 = (refs[4 * k + t][...].astype(f32) for t in range(4))
            refs[4 * n + k][...] = ((own + q1) + q2) + q3

    def slab(flip):
        return pl.BlockSpec((None, tm, cdim), lambda i, w_ref: (jnp.bitwise_xor(w_ref[0], flip), i, 0))

    operands = []
    for p, q in zip(ps, qs):
        operands += [p, q, q, q]
    return _pcall(
        body, name=name,
        grid_spec=pltpu.PrefetchScalarGridSpec(
            num_scalar_prefetch=1, grid=(nb,),
            in_specs=[slab(0), slab(2), slab(1), slab(3)] * n,
            out_specs=[pl.BlockSpec((tm, cdim), lambda i, w_ref: (w_ref[1] * nb + i, 0))] * n),
        out_shape=[jax.ShapeDtypeStruct((2 * m, cdim), f32) for _ in qs],
        compiler_params=_params(1),
    )(where, *operands)


def _join_halves(fs, *, name):
    n = len(fs)

    def body(*refs):
        outs = refs[n:2 * n]
        send_sems, recv_sems = refs[2 * n:]
        x, y, c = _position()
        cps = []
        for k in range(n):
            m = fs[k].shape[0] // 2
            half = outs[k].at[pl.ds(pl.multiple_of(c * m, 8), m), :]
            cp = pltpu.make_async_remote_copy(src_ref=half, dst_ref=half, send_sem=send_sems.at[k],
                                              recv_sem=recv_sems.at[k], device_id=(x, y, 1 - c), device_id_type=MESH)
            cp.start()
            cps.append(cp)
        for cp in cps:
            cp.wait()

    return _pcall(
        body, name=name, in_specs=[HBM_SPEC] * n, out_specs=[HBM_SPEC] * n,
        out_shape=[jax.ShapeDtypeStruct(f.shape, f.dtype) for f in fs],
        input_output_aliases={k: k for k in range(n)},
        scratch_shapes=[pltpu.SemaphoreType.DMA((n,)), pltpu.SemaphoreType.DMA((n,))],
    )(*fs)


def _all_reduce_small(v, after=None, *, name):
    r = v.shape[0]
    extra = [] if after is None else [after]

    def body(v_ref, *refs):
        out_ref, buf, send_sems, recv_sems, local_sem = refs[len(extra):]
        x, y, c = _position()
        me, sibling = (x, y, c), (x, y, 1 - c)
        chips = _other_chips(x, y)

        def rows(px, py, pc):
            return buf.at[pl.ds(pl.multiple_of((4 * px + 2 * py + pc) * r, 8), r), :]

        def copy(k, block, to, src=None):
            return pltpu.make_async_remote_copy(
                src_ref=rows(*block) if src is None else src, dst_ref=rows(*block),
                send_sem=send_sems.at[k], recv_sem=recv_sems.at[k], device_id=to, device_id_type=MESH)

        mine = pltpu.make_async_copy(v_ref, rows(*me), local_sem)
        mine.start()
        first = [copy(0, me, sibling, src=v_ref)]
        first += [copy(1 + j, me, (*chip, c), src=v_ref) for j, chip in enumerate(chips)]
        for cp in first:
            cp.start()
        passed = [copy(4 + j, (*chip, c), sibling) for j, chip in enumerate(chips)]
        for j, chip in enumerate(chips):
            copy(1 + j, (*chip, c), me).wait_recv()
            passed[j].start()
        copy(0, sibling, me).wait_recv()
        for j, chip in enumerate(chips):
            copy(4 + j, (*chip, 1 - c), me).wait_recv()
        for cp in first + passed:
            cp.wait_send()
        mine.wait()
        acc = buf[0:r, :]
        for d in range(1, N_DEV):
            acc = acc + buf[d * r:(d + 1) * r, :]
        out_ref[...] = acc

    return _pcall(
        body, name=name, in_specs=[VMEM_SPEC] + [HBM_SPEC] * len(extra), out_specs=VMEM_SPEC,
        out_shape=jax.ShapeDtypeStruct((r, LANES), f32),
        scratch_shapes=[pltpu.VMEM((N_DEV * r, LANES), f32), pltpu.SemaphoreType.DMA((7,)),
                        pltpu.SemaphoreType.DMA((7,)), pltpu.SemaphoreType.DMA],
    )(v, *extra)


SEM_SPEC = pl.BlockSpec(memory_space=pltpu.SEMAPHORE)
HBM_ONLY = pl.BlockSpec(memory_space=pltpu.HBM)
EFFECT = pltpu.SideEffectType.DATAFLOW_SIDE_EFFECTING


def _sends(copies):
    return copies[0] if isinstance(copies, tuple) else copies


def _arrivals(copies):
    return copies[1] if isinstance(copies, tuple) else copies


def _split_start(bufs, copies_fn, n_sems, *, name):
    n = len(bufs)

    def body(*refs):
        send_sems, recv_sems = refs[n], refs[n + 1]
        thru = refs[n + 2:2 * n + 2]
        token = refs[2 * n + 2]
        for cp in _sends(copies_fn(thru, send_sems, recv_sems)):
            cp.start()
        token[...] = jnp.zeros_like(token)

    outs = _pcall(
        body, name=name,
        out_shape=(pltpu.SemaphoreType.DMA((n_sems,)), pltpu.SemaphoreType.DMA((n_sems,)),
                   *[pltpu.HBM(b.shape, b.dtype) for b in bufs], jax.ShapeDtypeStruct((8, LANES), f32)),
        in_specs=[HBM_ONLY] * n,
        out_specs=(SEM_SPEC, SEM_SPEC, *[HBM_ONLY] * n, VMEM_SPEC),
        input_output_aliases={k: 2 + k for k in range(n)},
        compiler_params=pltpu.CompilerParams(has_side_effects=EFFECT),
    )(*[pltpu.with_memory_space_constraint(b, pltpu.HBM) for b in bufs])
    return outs[0], outs[1], list(outs[2:2 + n]), outs[2 + n]


def _split_wait(thru, send_sems, recv_sems, after, copies_fn, *, name):
    n = len(thru)

    def body(*refs):
        copies = copies_fn(refs[:n], refs[n], refs[n + 1])
        for cp in _sends(copies):
            cp.wait_send()
        for cp in _arrivals(copies):
            cp.wait_recv()

    return list(_pcall(
        body, name=name,
        out_shape=tuple(pltpu.HBM(b.shape, b.dtype) for b in thru),
        in_specs=[HBM_ONLY] * n + [SEM_SPEC, SEM_SPEC] + [HBM_SPEC] * len(after),
        out_specs=tuple([HBM_ONLY] * n),
        input_output_aliases={k: k for k in range(n)},
        compiler_params=pltpu.CompilerParams(has_side_effects=EFFECT),
    )(*thru, send_sems, recv_sems, *after))


def _scatter_copies(n):
    def copies(bufs, send_sems, recv_sems):
        x, y, c = _position()
        me_chip = 2 * x + y
        cps = []
        for k in range(n):
            for j, (px, py) in enumerate(_other_chips(x, y)):
                cps.append(pltpu.make_async_remote_copy(
                    src_ref=bufs[k].at[2 * px + py], dst_ref=bufs[n + k].at[me_chip],
                    send_sem=send_sems.at[3 * k + j], recv_sem=recv_sems.at[3 * k + j],
                    device_id=(px, py, c), device_id_type=MESH))
        return cps
    return copies


N_PEERS = N_DEV - 1


def _direct_copies(n):
    def copies(bufs, send_sems, recv_sems):
        x, y, c = _position()
        me_chip = 2 * x + y
        sends, arrivals = [], []
        for k in range(n):
            m = bufs[k].shape[1] // 2
            land = bufs[n + k]

            def rows(slab, half, k=k, m=m):
                start = half * m if isinstance(half, int) else pl.multiple_of(half * m, 16)
                return bufs[k].at[slab, pl.ds(start, m), :]

            def copy(src, slot, send_idx, recv_idx, to, k=k, land=land):
                return pltpu.make_async_remote_copy(
                    src_ref=src, dst_ref=land.at[slot], send_sem=send_sems.at[N_PEERS * k + send_idx],
                    recv_sem=recv_sems.at[N_PEERS * k + recv_idx], device_id=to, device_id_type=MESH)

            sends.append(copy(rows(me_chip, 1 - c), 0, 0, 0, (x, y, 1 - c)))
            arrivals.append(copy(rows(me_chip, c), 0, 0, 0, (x, y, 1 - c)))
            for t, (px, py) in enumerate(_other_chips(x, y)):
                for core in range(2):
                    sends.append(copy(rows(2 * px + py, core), 1 + 2 * t + c, 1 + 2 * t + core, 1 + 2 * t + c,
                                      (px, py, core)))
                    arrivals.append(copy(rows(me_chip, c), 1 + 2 * t + core, 1 + 2 * t + core, 1 + 2 * t + core,
                                         (px, py, core)))
        return sends, arrivals
    return copies


def _sum_direct(gs, lands, *, name, tm=128):
    n = len(gs)
    _, m, cdim = lands[0].shape
    tm = min(tm, m)
    nb = m // tm
    assert m % tm == 0, (m, tm)
    where = jnp.stack([2 * lax.axis_index("x") + lax.axis_index("y"), lax.axis_index("c")]).astype(jnp.int32)

    def body(w_ref, *refs):
        for k in range(n):
            acc = refs[2 * k][...].astype(f32)
            for slot in range(N_PEERS):
                acc = acc + refs[2 * k + 1][slot].astype(f32)
            refs[2 * n + k][...] = acc

    own = pl.BlockSpec((None, tm, cdim), lambda i, w_ref: (w_ref[0], w_ref[1] * nb + i, 0))
    landed = pl.BlockSpec((N_PEERS, tm, cdim), lambda i, w_ref: (0, i, 0))
    operands = []
    for g, land in zip(gs, lands):
        operands += [g, land]
    return _pcall(
        body, name=name,
        grid_spec=pltpu.PrefetchScalarGridSpec(
            num_scalar_prefetch=1, grid=(nb,), in_specs=[own, landed] * n,
            out_specs=[pl.BlockSpec((tm, cdim), lambda i, w_ref: (w_ref[1] * nb + i, 0))] * n),
        out_shape=[jax.ShapeDtypeStruct((2 * m, cdim), f32) for _ in gs],
        compiler_params=_params(1),
    )(where, *operands)


def _broadcast_copies(bufs, send_sems, recv_sems):
    v, land = bufs
    x, y, c = _position()

    def copy(slot, send_idx, recv_idx, to):
        return pltpu.make_async_remote_copy(src_ref=v, dst_ref=land.at[slot], send_sem=send_sems.at[send_idx],
                                            recv_sem=recv_sems.at[recv_idx], device_id=to, device_id_type=MESH)

    sends = [copy(0, 0, 0, (x, y, 1 - c))]
    arrivals = [copy(0, 0, 0, (x, y, 1 - c))]
    for t, (px, py) in enumerate(_other_chips(x, y)):
        for core in range(2):
            sends.append(copy(1 + 2 * t + c, 1 + 2 * t + core, 1 + 2 * t + c, (px, py, core)))
            arrivals.append(copy(1 + 2 * t + core, 1 + 2 * t + core, 1 + 2 * t + core, (px, py, core)))
    return sends, arrivals


def _sum_in_device_order(v, land, *, name):
    r, cdim = v.shape
    x, y, c = _position()
    slots, mine = [], []
    for d in range(N_DEV):
        dx, dy, dc = d // 4, (d // 2) % 2, d % 2
        fx, fy = jnp.bitwise_xor(dx, x), jnp.bitwise_xor(dy, y)
        t = jnp.where(fx == 1, jnp.where(fy == 1, 2, 0), 1)
        slots.append(jnp.where(jnp.logical_and(fx == 0, fy == 0), 0, 1 + 2 * t + dc))
        mine.append(jnp.logical_and(jnp.logical_and(fx == 0, fy == 0), dc == c))
    table = jnp.stack(slots + mine).astype(jnp.int32)

    def body(tab_ref, v_ref, *refs):
        out_ref = refs[N_DEV]
        acc = None
        for d in range(N_DEV):
            term = jnp.where(tab_ref[N_DEV + d] == 1, v_ref[...], refs[d][...])
            acc = term if acc is None else acc + term
        out_ref[...] = acc

    whole = pl.BlockSpec((r, cdim), lambda i, tab_ref: (0, 0))
    landed = [pl.BlockSpec((None, r, cdim), functools.partial(lambda i, tab_ref, d: (tab_ref[d], 0, 0), d=d))
              for d in range(N_DEV)]
    return _pcall(
        body, name=name,
        grid_spec=pltpu.PrefetchScalarGridSpec(num_scalar_prefetch=1, grid=(1,), in_specs=[whole] + landed,
                                               out_specs=whole),
        out_shape=jax.ShapeDtypeStruct((r, cdim), f32),
        compiler_params=_params(1),
    )(table, v, *[land] * N_DEV)


def _block_rows(buf, px, py, pc):
    m = buf.shape[0] // N_DEV
    return buf.at[pl.ds(pl.multiple_of((4 * px + 2 * py + pc) * m, 16), m), :]


def _gather_ici_copies(n):
    def copies(bufs, send_sems, recv_sems):
        x, y, c = _position()
        cps = []
        for k in range(n):
            rows = _block_rows(bufs[k], x, y, c)
            targets = [(x, y, 1 - c)] + [(px, py, c) for px, py in _other_chips(x, y)]
            for j, to in enumerate(targets):
                cps.append(pltpu.make_async_remote_copy(
                    src_ref=rows, dst_ref=rows, send_sem=send_sems.at[4 * k + j], recv_sem=recv_sems.at[4 * k + j],
                    device_id=to, device_id_type=MESH))
        return cps
    return copies


def _gather_d2d_copies(n):
    def copies(bufs, send_sems, recv_sems):
        x, y, c = _position()
        cps = []
        for k in range(n):
            for j, (px, py) in enumerate(_other_chips(x, y)):
                rows = _block_rows(bufs[k], px, py, c)
                cps.append(pltpu.make_async_remote_copy(
                    src_ref=rows, dst_ref=rows, send_sem=send_sems.at[3 * k + j], recv_sem=recv_sems.at[3 * k + j],
                    device_id=(x, y, 1 - c), device_id_type=MESH))
        return cps
    return copies


def _cast_halves(shards, after, *, name):
    n = len(shards)
    where = jnp.stack([2 * lax.axis_index("x") + lax.axis_index("y"), lax.axis_index("c")]).astype(jnp.int32)

    def body(w_ref, *refs):
        for k in range(n):
            refs[n + 1 + k][...] = refs[k][...].astype(refs[n + 1 + k].dtype)

    def half(s):
        return (s.shape[0] // 2, s.shape[1])

    return _pcall(
        body, name=name,
        grid_spec=pltpu.PrefetchScalarGridSpec(
            num_scalar_prefetch=1, grid=(1,),
            in_specs=[pl.BlockSpec(half(s), lambda i, w_ref: (w_ref[1], 0)) for s in shards] + [HBM_SPEC],
            out_specs=[pl.BlockSpec(half(s), lambda i, w_ref: (2 * w_ref[0] + w_ref[1], 0)) for s in shards]),
        out_shape=[jax.ShapeDtypeStruct((N_SHARD * s.shape[0], s.shape[1]), MXU_DTYPE) for s in shards],
        compiler_params=_params(1),
    )(where, *shards, after)


class _SplitGather:
    def __init__(self, shards, after, tag):
        self.tag = tag
        self.n = len(shards)
        halves = _cast_halves(shards, after, name=f"{tag}_cast")
        self.ici = _split_start(halves, _gather_ici_copies(self.n), 4 * self.n, name=f"{tag}_ici_start")
        self.token = self.ici[3]

    def forward(self, after):
        send_sems, recv_sems, thru, _ = self.ici
        landed = _split_wait(thru, send_sems, recv_sems, after, _gather_ici_copies(self.n), name=f"{self.tag}_ici_wait")
        self.d2d = _split_start(landed, _gather_d2d_copies(self.n), 3 * self.n, name=f"{self.tag}_d2d_start")
        return self.d2d[3]

    def finish(self, after):
        send_sems, recv_sems, thru, _ = self.d2d
        return _split_wait(thru, send_sems, recv_sems, after, _gather_d2d_copies(self.n), name=f"{self.tag}_d2d_wait")


class _Overlap(_NoOverlap):
    def __init__(self, late_shards, ffn2_shards, after):
        self.late = _SplitGather(late_shards, after, "ag1")
        self.ffn2 = _SplitGather(ffn2_shards, self.late.token, "ag2")
        self.reduced = None
        self.ffn1_parts = []

    def start_token(self):
        return self.ffn2.token

    def late_weights(self, w, after):
        token = self.late.forward(after)
        f1d, w_in, wo = self.late.finish([token])
        w_in = w_in.reshape(N_SHARD, D_MODEL, IN_SHARD).transpose(1, 0, 2).reshape(D_MODEL, IN_COLS)
        return dict(f1d=f1d.reshape(N_SHARD, D_FF // N_SHARD, D_MODEL), wp=make_wp(w_in), wo=wo)

    def after_attention(self, after):
        return self.ffn2.forward(after)

    def ffn2_weights(self, w, after):
        full = self.ffn2.finish(after)
        fs = D_FF // N_SHARD
        return (full[0].reshape(N_SHARD, D_MODEL, fs), full[1].reshape(N_SHARD, D_MODEL, fs),
                full[2].reshape(N_SHARD, fs, D_MODEL))

    @staticmethod
    def _send_direct(grads, tag):
        lands = [lax.empty((N_PEERS, g.shape[1] // 2, g.shape[2]), g.dtype) for g in grads]
        return _split_start(list(grads) + lands, _direct_copies(len(grads)), N_PEERS * len(grads),
                            name=f"rs_direct_{tag}_start")

    def ffn2_grads(self, grads):
        self.scatter = self._send_direct(grads, "ffn2")
        return self.scatter[3]

    def ffn1_grads(self, grads):
        tag = "ffn1" + "ab"[len(self.ffn1_parts)]
        if not self.ffn1_parts:
            started = self._send_direct(grads, tag)
        else:
            recvs = _swap_halves(grads, name=f"rs_swap_{tag}")
            ps = list(_add_halves(grads[:2], recvs[:2], name=f"rs_add_{tag}_gu"))
            ps += list(_add_halves(grads[2:], recvs[2:], name=f"rs_add_{tag}_d"))
            lands = [lax.empty(p.shape, p.dtype) for p in ps]
            started = _split_start(ps + lands, _scatter_copies(3), 9, name=f"rs_scatter_{tag}_start")
        self.ffn1_parts.append((tag, started))
        return started[3]

    def ffn1_reduced(self, after):
        sums = []
        for direct, (tag, (send_sems, recv_sems, thru, _)) in zip((True, False), self.ffn1_parts):
            plan, add = (_direct_copies, _sum_direct) if direct else (_scatter_copies, _sum_slabs)
            done = _split_wait(thru, send_sems, recv_sems, after, plan(3), name=f"rs_{tag}_wait")
            sums += list(add(done[:2], done[3:5], name=f"rs_sum_{tag}_gu"))
            sums += list(add(done[2:3], done[5:], name=f"rs_sum_{tag}_d"))
        return sums

    def mixer_grads(self, dwp, dwo, small, loss):
        packed = jnp.concatenate([_pack_small(small), jnp.broadcast_to(loss, (8, LANES))], axis=0)
        land = lax.empty((N_PEERS,) + packed.shape, packed.dtype)
        self.small = _split_start([packed, land], _broadcast_copies, N_PEERS, name="ar_small_start")
        gwin = dwp[:, :IN_COLS].reshape(D_MODEL, N_SHARD, IN_SHARD).transpose(1, 0, 2).astype(GRAD_DTYPE)
        gwo = dwo.reshape(N_SHARD, D_MODEL // N_SHARD, D_MODEL).astype(GRAD_DTYPE)
        self.scatter_mix = self._send_direct([gwin, gwo], "mix")
        return self.small[3] + self.scatter_mix[3]

    def small_summed(self, after):
        send_sems, recv_sems, thru, _ = self.small
        packed, land = _split_wait(thru, send_sems, recv_sems, after, _broadcast_copies, name="ar_small_wait")
        summed = _sum_in_device_order(packed, land, name="ar_small_sum")
        return summed[:-8], summed[-8, 0]

    def mixer_reduced(self, after):
        send_sems, recv_sems, thru, _ = self.scatter_mix
        done = _split_wait(thru, send_sems, recv_sems, after, _direct_copies(2), name="rs_direct_mix_wait")
        return [_sum_direct([done[k]], [done[2 + k]], name=f"rs_sum_{tag}")[0] for k, tag in enumerate(["w_in", "w_out"])]

    def before_ffn1_bwd(self, after):
        send_sems, recv_sems, thru, _ = self.scatter
        n = len(thru) // 2
        done = _split_wait(thru, send_sems, recv_sems, after, _direct_copies(n), name="rs_direct_ffn2_wait")
        self.reduced = list(_sum_direct(done[:n], done[n:], name="rs_sum_ffn2"))


def _adamw(gs, ws, ms, vs, *, name, tm=256):
    n = len(gs)
    r, cdim = gs[0].shape
    tm = r if tm is None else min(tm, r)
    assert r % tm == 0, (r, tm)
    c1 = 1.0 / (1.0 - ADAM_B1 ** ADAM_STEP)
    c2 = 1.0 / (1.0 - ADAM_B2 ** ADAM_STEP)

    def body(*refs):
        for k in range(n):
            g = refs[k][...]
            w = refs[n + k][...]
            m = ADAM_B1 * refs[2 * n + k][...] + (1.0 - ADAM_B1) * g
            v = ADAM_B2 * refs[3 * n + k][...] + (1.0 - ADAM_B2) * (g * g)
            refs[4 * n + k][...] = g
            refs[5 * n + k][...] = -ADAM_LR * ((m * c1) / (jnp.sqrt(v * c2) + ADAM_EPS) + ADAM_WD * w)
            refs[6 * n + k][...] = m
            refs[7 * n + k][...] = v

    flat = pl.BlockSpec((tm, cdim), lambda i: (i, 0))
    like_w = flat if ws[0].ndim == 2 else pl.BlockSpec((None, tm, cdim), lambda i: (0, i, 0))
    outs = _pcall(
        body, name=name, grid=(r // tm,), in_specs=[flat] * n + [like_w] * (3 * n), out_specs=[like_w] * (4 * n),
        out_shape=[jax.ShapeDtypeStruct(ws[0].shape, f32)] * (4 * n),
        compiler_params=_params(1),
    )(*gs, *ws, *ms, *vs)
    return outs[:n], outs[n:2 * n], outs[2 * n:3 * n], outs[3 * n:]


BIG = ["ffn1_w_gate", "ffn1_w_up", "ffn1_w_down", "ffn2_w_gate", "ffn2_w_up", "ffn2_w_down"]
SMALL = ["ln1_g", "ln1_b", "b_forget", "conv_w", "conv_b", "rg_wa", "rg_ba", "rg_wx", "rg_bx", "lru_lambda",
         "ln2_g", "ln2_b", "ln3_g", "ln3_b"]
WEIGHTS = ["ffn1_w_gate", "ffn1_w_up", "ffn1_w_down", "ln1_g", "ln1_b", "w_in", "b_forget", "conv_w", "conv_b",
           "rg_wa", "rg_ba", "rg_wx", "rg_bx", "lru_lambda", "w_out", "ln2_g", "ln2_b",
           "ffn2_w_gate", "ffn2_w_up", "ffn2_w_down", "ln3_g", "ln3_b"]


def _pack_small(parts):
    rows = []
    for n in SMALL:
        flat = parts[n].reshape(-1)
        pad = (-flat.shape[0]) % LANES
        rows.append(jnp.pad(flat, (0, pad)).reshape(-1, LANES))
    packed = jnp.concatenate(rows, axis=0)
    return jnp.pad(packed, ((0, (-packed.shape[0]) % 8), (0, 0)))


def _unpack_small(packed, shapes):
    out, r0 = {}, 0
    for n in SMALL:
        size = math.prod(shapes[n])
        nr = -(-size // LANES)
        out[n] = packed[r0:r0 + nr].reshape(-1)[:size].reshape(shapes[n])
        r0 += nr
    return out


def kernel(x, ffn1_w_gate, ffn1_w_up, ffn1_w_down, ln1_g, ln1_b, w_in, b_forget, conv_w, conv_b, rg_wa, rg_ba, rg_wx, rg_bx, lru_lambda, w_out, ln2_g, ln2_b, ffn2_w_gate, ffn2_w_up, ffn2_w_down, ln3_g, ln3_b, loss_target, m_ffn1_w_gate, m_ffn1_w_up, m_ffn1_w_down, m_ln1_g, m_ln1_b, m_w_in, m_b_forget, m_conv_w, m_conv_b, m_rg_wa, m_rg_ba, m_rg_wx, m_rg_bx, m_lru_lambda, m_w_out, m_ln2_g, m_ln2_b, m_ffn2_w_gate, m_ffn2_w_up, m_ffn2_w_down, m_ln3_g, m_ln3_b, v_ffn1_w_gate, v_ffn1_w_up, v_ffn1_w_down, v_ln1_g, v_ln1_b, v_w_in, v_b_forget, v_conv_w, v_conv_b, v_rg_wa, v_rg_ba, v_rg_wx, v_rg_bx, v_lru_lambda, v_w_out, v_ln2_g, v_ln2_b, v_ffn2_w_gate, v_ffn2_w_up, v_ffn2_w_down, v_ln3_g, v_ln3_b):
    args = dict(locals())
    w = {n: args[n] for n in WEIGHTS}
    mom = {n: args["m_" + n] for n in WEIGHTS}
    var = {n: args["v_" + n] for n in WEIGHTS}
    chip = 2 * lax.axis_index("x") + lax.axis_index("y")

    g1 = _all_gather_bf16([w[n][0] for n in BIG[:2]], name="ag_ffn1_up")
    fs = D_FF // N_SHARD
    full = dict(
        f1g=g1[0].reshape(N_SHARD, D_MODEL, fs), f1u=g1[1].reshape(N_SHARD, D_MODEL, fs),
        bfp=jnp.pad(b_forget, ((0, 0), (0, LANES - HEADS))),
        ln1_g=ln1_g, ln1_b=ln1_b, ln2_g=ln2_g, ln2_b=ln2_b, ln3_g=ln3_g, ln3_b=ln3_b,
        conv_b=conv_b, rg_wa=rg_wa[0], rg_wx=rg_wx[0], rg_ba=rg_ba[0], rg_bx=rg_bx[0], lam=lru_lambda,
    )
    cw_place = lax.dynamic_update_slice(jnp.zeros((8, LRU_W), f32), conv_w[0] * 0.5, (0, chip * (LRU_W // N_SHARD)))
    cw_full = _all_reduce_small(cw_place.reshape(-1, LANES), g1[0], name="ag_conv_w")
    full["conv_w"] = cw_full.reshape(8, LRU_W)[:CONV_K]

    hooks = _Overlap([w["ffn1_w_down"][0], w["w_in"][0], w["w_out"][0]], [w[n][0] for n in BIG[3:]], cw_full)
    loss_rep, dx, g = _local_step(x[0], loss_target[0], full, hooks)

    token1 = hooks.ffn1_grads(g["f1"][1])
    red = _join_halves(hooks.reduced + hooks.mixer_reduced([token1]), name="rs_join_rest")
    grads = dict(zip(BIG[3:] + ["w_in", "w_out"], red))

    small_sum, loss = hooks.small_summed(red)
    small_shapes = {n: w[n].shape for n in SMALL}
    small_shapes["conv_w"] = (1, CONV_K, LRU_W)
    gs_red = _unpack_small(small_sum, small_shapes)
    gs_red["conv_w"] = lax.dynamic_slice(gs_red["conv_w"], (0, 0, chip * (LRU_W // N_SHARD)),
                                         (1, CONV_K, LRU_W // N_SHARD))
    grads.update(gs_red)

    delta, new_m, new_v = {}, {}, {}

    def adamw(names, name, **kw):
        g3, d, nm, nv = _adamw([grads[n] for n in names], [w[n] for n in names], [mom[n] for n in names],
                               [var[n] for n in names], name=name, **kw)
        for i, n in enumerate(names):
            grads[n], delta[n], new_m[n], new_v[n] = g3[i], d[i], nm[i], nv[i]

    adamw(BIG[3:], "adamw_ffn2", tm=128)
    adamw(["w_in"], "adamw_w_in")
    adamw(["w_out"], "adamw_w_out")
    shard_shapes = {n: w[n].shape for n in SMALL}
    _, d, nm, nv = _adamw([_pack_small({n: grads[n] for n in SMALL})], [_pack_small({n: w[n] for n in SMALL})],
                          [_pack_small({n: mom[n] for n in SMALL})], [_pack_small({n: var[n] for n in SMALL})],
                          name="adamw_small", tm=None)
    for dst, packed in ((delta, d[0]), (new_m, nm[0]), (new_v, nv[0])):
        dst.update(_unpack_small(packed, shard_shapes))

    worked = [new_v["ffn2_w_down"], new_v["w_in"], new_v["w_out"], nv[0]]
    ga, ua, da, gb, ub, db = _join_halves(hooks.ffn1_reduced(worked), name="rs_join_ffn1")
    grads["ffn1_w_gate"] = jnp.concatenate([ga, gb], axis=1)
    grads["ffn1_w_up"] = jnp.concatenate([ua, ub], axis=1)
    grads["ffn1_w_down"] = jnp.concatenate([da, db], axis=0)
    adamw(BIG[:3], "adamw_ffn1", tm=128)

    def shaped(tree, n):
        return tree[n].reshape(w[n].shape)

    return (loss, dx[None], *[shaped(grads, n) for n in WEIGHTS], *[shaped(delta, n) for n in WEIGHTS],
            *[shaped(new_m, n) for n in WEIGHTS], *[shaped(new_v, n) for n in WEIGHTS])
```

```python
import functools
import math

import jax
import jax.numpy as jnp
from jax import lax
from jax.experimental import pallas as pl
from jax.experimental.pallas import tpu as pltpu

f32 = jnp.float32
MXU_DTYPE = jnp.bfloat16
GRAD_DTYPE = jnp.bfloat16

D_MODEL = 1024
D_FF = 4096
N_SHARD = 4
N_DEV = 8
FOX_W = 512
LRU_W = 512
HEADS = 8
HEAD_DIM = 64
CONV_K = 4
IN_COLS = 2568
IN_SHARD = IN_COLS // N_SHARD
QKV_W = 3 * FOX_W
Z_PAD = 2688
LANES = 128
LN_EPS = 1e-5
DN_ALPHA = 2.0 ** 0.25
LRU_C = 8.0
NEG_BIG = -1e30
VMEM_LIMIT = 56 * 1024 * 1024

ADAM_LR = 0.001
ADAM_B1 = 0.9
ADAM_B2 = 0.999
ADAM_EPS = 1e-08
ADAM_WD = 0.01
ADAM_STEP = 10


def _pcall(body, **kw):
    return pl.pallas_call(body, **kw)


def _params(n_grid, vmem=VMEM_LIMIT):
    return pltpu.CompilerParams(dimension_semantics=("arbitrary",) * n_grid, vmem_limit_bytes=vmem)


def _dot(a, b):
    return jnp.dot(a, b, preferred_element_type=f32)


def _dot_nt(a, b):
    return lax.dot_general(a, b, (((1,), (1,)), ((), ())), preferred_element_type=f32)


def _dot_tn(a, b):
    return lax.dot_general(a, b, (((0,), (0,)), ((), ())), preferred_element_type=f32)


def _sigmoid(x):
    return 1.0 / (1.0 + jnp.exp(-x))


def _layer_norm_stats(y):
    mu = jnp.mean(y, axis=-1, keepdims=True)
    yc = y - mu
    var = jnp.mean(yc * yc, axis=-1, keepdims=True)
    rstd = lax.rsqrt(var + LN_EPS)
    return yc * rstd, rstd


def _ln_backward(dy, xhat, rstd, gamma):
    dxhat = dy * gamma
    m1 = jnp.mean(dxhat, axis=-1, keepdims=True)
    m2 = jnp.mean(dxhat * xhat, axis=-1, keepdims=True)
    dyp = rstd * (dxhat - m1 - xhat * m2)
    return dyp, jnp.sum(dy * xhat, axis=0, keepdims=True), jnp.sum(dy, axis=0, keepdims=True)


def _ffn_fwd_loss(x, wg, wu, wd, ln_g, ln_b, target, *, name, tm=1024, tf=512):
    T = x.shape[0]
    tm = min(tm, T)
    tr = min(256, tm)
    fs = D_FF // N_SHARD
    cpf = fs // tf
    nf = D_FF // tf
    nt = T // tm

    def body(x_ref, wg_ref, wu_ref, wd_ref, g_ref, b_ref, t_ref,
             xb_ref, gact_ref, uact_ref, dyp_ref, dgam_ref, dbeta_ref, loss_ref, acc_ref):
        i = pl.program_id(0)
        f = pl.program_id(1)

        @pl.when(jnp.logical_and(i == 0, f == 0))
        def _():
            dgam_ref[...] = jnp.zeros_like(dgam_ref)
            dbeta_ref[...] = jnp.zeros_like(dbeta_ref)
            loss_ref[...] = jnp.zeros_like(loss_ref)

        @pl.when(f == 0)
        def _():
            xb_ref[...] = x_ref[...].astype(MXU_DTYPE)
            acc_ref[...] = jnp.zeros_like(acc_ref)

        xb = xb_ref[...]
        g = _dot(xb, wg_ref[...])
        u = _dot(xb, wu_ref[...])
        h = (g * _sigmoid(g)) * u
        gact_ref[...] = g.astype(gact_ref.dtype)
        uact_ref[...] = u.astype(uact_ref.dtype)
        acc_ref[...] += _dot(h.astype(MXU_DTYPE), wd_ref[...])

        @pl.when(f == nf - 1)
        def _():
            gamma = g_ref[...]

            def rows_chunk(r, carry):
                rows = pl.ds(pl.multiple_of(r * tr, tr), tr)
                xhat, rstd = _layer_norm_stats(DN_ALPHA * x_ref[rows, :] + 0.5 * acc_ref[rows, :])
                err = xhat * gamma + b_ref[...] - t_ref[rows, :]
                sq = jnp.sum(jnp.sum(err * err, axis=0, keepdims=True), axis=1, keepdims=True)
                loss_ref[...] += jnp.broadcast_to(sq * (0.5 / D_MODEL), loss_ref.shape)
                dyp, dgam, dbeta = _ln_backward(err * (1.0 / D_MODEL), xhat, rstd, gamma)
                dyp_ref[rows, :] = dyp
                dgam_ref[...] += dgam
                dbeta_ref[...] += dbeta
                return carry

            lax.fori_loop(0, tm // tr, rows_chunk, 0)

    row = lambda i, f: (i, 0)
    const = lambda i, f: (0, 0)
    tile = pl.BlockSpec((tm, tf), lambda i, f: (i, f))
    cols = pl.BlockSpec((None, D_MODEL, tf), lambda i, f: (f // cpf, 0, f % cpf))
    last = lambda i, f: (jnp.where(f == nf - 1, i, jnp.maximum(i - 1, 0)), 0)
    return _pcall(
        body, name=name, grid=(nt, nf),
        in_specs=[pl.BlockSpec((tm, D_MODEL), row), cols, cols,
                  pl.BlockSpec((None, tf, D_MODEL), lambda i, f: (f // cpf, f % cpf, 0)),
                  pl.BlockSpec((1, D_MODEL), const), pl.BlockSpec((1, D_MODEL), const),
                  pl.BlockSpec((tm, D_MODEL), last)],
        out_specs=[pl.BlockSpec((tm, D_MODEL), row), tile, tile, pl.BlockSpec((tm, D_MODEL), row),
                   pl.BlockSpec((1, D_MODEL), const), pl.BlockSpec((1, D_MODEL), const), pl.BlockSpec((1, LANES), const)],
        out_shape=[jax.ShapeDtypeStruct((T, D_MODEL), MXU_DTYPE), jax.ShapeDtypeStruct((T, D_FF), MXU_DTYPE),
                   jax.ShapeDtypeStruct((T, D_FF), MXU_DTYPE), jax.ShapeDtypeStruct((T, D_MODEL), f32),
                   jax.ShapeDtypeStruct((1, D_MODEL), f32), jax.ShapeDtypeStruct((1, D_MODEL), f32),
                   jax.ShapeDtypeStruct((1, LANES), f32)],
        scratch_shapes=[pltpu.VMEM((tm, D_MODEL), f32)],
        compiler_params=_params(2),
    )(x, wg, wu, wd, ln_g, ln_b, target)


def _ffn_up(x, wg, wu, after=None, *, name, tm=1024, tf=512):
    T = x.shape[0]
    tm = min(tm, T)
    cpf = (D_FF // N_SHARD) // tf
    nf = D_FF // tf
    extra = [] if after is None else [after]

    def body(x_ref, wg_ref, wu_ref, *refs):
        xb_ref, gact_ref, uact_ref, hact_ref = refs[len(extra):]

        @pl.when(pl.program_id(1) == 0)
        def _():
            xb_ref[...] = x_ref[...].astype(MXU_DTYPE)

        xb = xb_ref[...]
        g = _dot(xb, wg_ref[...])
        u = _dot(xb, wu_ref[...])
        gact_ref[...] = g.astype(gact_ref.dtype)
        uact_ref[...] = u.astype(uact_ref.dtype)
        hact_ref[...] = ((g * _sigmoid(g)) * u).astype(hact_ref.dtype)

    row = lambda i, f: (i, 0)
    tile = pl.BlockSpec((tm, tf), lambda i, f: (i, f))
    cols = pl.BlockSpec((None, D_MODEL, tf), lambda i, f: (f // cpf, 0, f % cpf))
    return _pcall(
        body, name=name, grid=(T // tm, nf),
        in_specs=[pl.BlockSpec((tm, D_MODEL), row), cols, cols] + [pl.BlockSpec(memory_space=pl.ANY)] * len(extra),
        out_specs=[pl.BlockSpec((tm, D_MODEL), row), tile, tile, tile],
        out_shape=[jax.ShapeDtypeStruct((T, D_MODEL), MXU_DTYPE)] + [jax.ShapeDtypeStruct((T, D_FF), MXU_DTYPE)] * 3,
        compiler_params=_params(2),
    )(x, wg, wu, *extra)


def _ffn_down_ln(x, hact, wd, ln_g, ln_b, *, name, tm=1024):
    T = x.shape[0]
    tm = min(tm, T)
    fs = D_FF // N_SHARD
    ks = 2
    nk = N_SHARD // ks

    def body(x_ref, h_ref, wd_ref, g_ref, b_ref, xhat_ref, xn_ref, rstd_ref, acc_ref):
        k = pl.program_id(1)

        @pl.when(k == 0)
        def _():
            acc_ref[...] = jnp.zeros_like(acc_ref)

        acc_ref[...] += _dot(h_ref[...], wd_ref[...].reshape(ks * fs, D_MODEL))

        @pl.when(k == nk - 1)
        def _():
            xhat, rstd = _layer_norm_stats(DN_ALPHA * x_ref[...] + 0.5 * acc_ref[...])
            xhat_ref[...] = xhat
            xn_ref[...] = (xhat * g_ref[...] + b_ref[...]).astype(xn_ref.dtype)
            rstd_ref[...] = jnp.broadcast_to(rstd, rstd_ref.shape)

    row = lambda i, k: (i, 0)
    vec = pl.BlockSpec((1, D_MODEL), lambda i, k: (0, 0))
    return _pcall(
        body, name=name, grid=(T // tm, nk),
        in_specs=[pl.BlockSpec((tm, D_MODEL), row), pl.BlockSpec((tm, ks * fs), lambda i, k: (i, k)),
                  pl.BlockSpec((ks, fs, D_MODEL), lambda i, k: (k, 0, 0)), vec, vec],
        out_specs=[pl.BlockSpec((tm, D_MODEL), row), pl.BlockSpec((tm, D_MODEL), row), pl.BlockSpec((tm, LANES), row)],
        out_shape=[jax.ShapeDtypeStruct((T, D_MODEL), f32), jax.ShapeDtypeStruct((T, D_MODEL), MXU_DTYPE),
                   jax.ShapeDtypeStruct((T, LANES), f32)],
        scratch_shapes=[pltpu.VMEM((tm, D_MODEL), f32)],
        compiler_params=_params(2),
    )(x, hact, wd, ln_g, ln_b)


def _ffn_bwd(dyp, xb, gact, uact, wg, wu, wd, after=None, *, name, tm=512, tf=512, part=None, dx_init=None):
    T = dyp.shape[0]
    tm = min(tm, T)
    fs = D_FF // N_SHARD
    cpf = fs // tf
    nt = T // tm
    nf = D_FF // tf if part is None else N_SHARD
    wf = fs if part is None else tf
    slab = (lambda f: f // cpf) if part is None else (lambda f: f)
    chunk = (lambda f: f % cpf) if part is None else (lambda f: part)
    extra = ([] if dx_init is None else [dx_init]) + ([] if after is None else [after])

    def body(dyp_ref, xb_ref, g_ref, u_ref, wg_ref, wu_ref, wd_ref, *refs):
        dx_hbm, dwg_ref, dwu_ref, dwd_ref, dx_sc, dwg_sc, dwu_sc, dwd_sc, sem = refs[len(extra):]
        f = pl.program_id(0)
        i = pl.program_id(1)
        rows = pl.ds(pl.multiple_of(i * tm, tm), tm)
        dyp_t = dyp_ref[...]
        dy = (0.5 * dyp_t).astype(MXU_DTYPE)

        @pl.when(i == 0)
        def _():
            dwg_sc[...] = jnp.zeros_like(dwg_sc)
            dwu_sc[...] = jnp.zeros_like(dwu_sc)
            dwd_sc[...] = jnp.zeros_like(dwd_sc)

        @pl.when(f == 0)
        def _():
            dx_sc[rows, :] = DN_ALPHA * dyp_t if dx_init is None else refs[0][...]

        g = g_ref[...].astype(f32)
        u = u_ref[...].astype(f32)
        sig = _sigmoid(g)
        silu = g * sig
        dh = _dot_nt(dy, wd_ref[...])
        dg = (dh * u * (sig * (1.0 + g * (1.0 - sig)))).astype(MXU_DTYPE)
        du = (dh * silu).astype(MXU_DTYPE)
        hb = (silu * u).astype(MXU_DTYPE)
        dx_sc[rows, :] += _dot_nt(dg, wg_ref[...]) + _dot_nt(du, wu_ref[...])
        xb_t = xb_ref[...]
        dwg_sc[...] += _dot_tn(xb_t, dg)
        dwu_sc[...] += _dot_tn(xb_t, du)
        dwd_sc[...] += _dot_tn(hb, dy)

        @pl.when(i == nt - 1)
        def _():
            dwg_ref[...] = dwg_sc[...].astype(dwg_ref.dtype)
            dwu_ref[...] = dwu_sc[...].astype(dwu_ref.dtype)
            dwd_ref[...] = dwd_sc[...].astype(dwd_ref.dtype)

        @pl.when(jnp.logical_and(f == nf - 1, i == nt - 1))
        def _():
            cp = pltpu.make_async_copy(dx_sc, dx_hbm, sem)
            cp.start()
            cp.wait()

    row = lambda f, i: (i, 0)
    return _pcall(
        body, name=name, grid=(nf, nt),
        in_specs=[
            pl.BlockSpec((tm, D_MODEL), row),
            pl.BlockSpec((tm, D_MODEL), row),
            pl.BlockSpec((tm, tf), lambda f, i: (i, slab(f) * cpf + chunk(f))),
            pl.BlockSpec((tm, tf), lambda f, i: (i, slab(f) * cpf + chunk(f))),
            pl.BlockSpec((None, D_MODEL, tf), lambda f, i: (slab(f), 0, chunk(f))),
            pl.BlockSpec((None, D_MODEL, tf), lambda f, i: (slab(f), 0, chunk(f))),
            pl.BlockSpec((None, tf, D_MODEL), lambda f, i: (slab(f), chunk(f), 0)),
        ] + ([] if dx_init is None else [pl.BlockSpec((tm, D_MODEL), row)])
        + ([] if after is None else [pl.BlockSpec(memory_space=pl.ANY)]),
        out_specs=[
            pl.BlockSpec(memory_space=pl.ANY),
            pl.BlockSpec((None, D_MODEL, tf), lambda f, i: (slab(f), 0, chunk(f) if part is None else 0)),
            pl.BlockSpec((None, D_MODEL, tf), lambda f, i: (slab(f), 0, chunk(f) if part is None else 0)),
            pl.BlockSpec((None, tf, D_MODEL), lambda f, i: (slab(f), chunk(f) if part is None else 0, 0)),
        ],
        out_shape=[
            jax.ShapeDtypeStruct((T, D_MODEL), f32),
            jax.ShapeDtypeStruct((N_SHARD, D_MODEL, wf), GRAD_DTYPE),
            jax.ShapeDtypeStruct((N_SHARD, D_MODEL, wf), GRAD_DTYPE),
            jax.ShapeDtypeStruct((N_SHARD, wf, D_MODEL), GRAD_DTYPE),
        ],
        scratch_shapes=[pltpu.VMEM((T, D_MODEL), f32), pltpu.VMEM((D_MODEL, tf), f32),
                        pltpu.VMEM((D_MODEL, tf), f32), pltpu.VMEM((tf, D_MODEL), f32),
                        pltpu.SemaphoreType.DMA],
        compiler_params=_params(2),
    )(dyp, xb, gact, uact, wg, wu, wd, *extra)


def _proj_in(xn, wp, bfp, *, name, tm=512):
    T = xn.shape[0]
    tm = min(tm, T)
    nt = T // tm

    def body(x_ref, w_ref, b_ref, lxg_ref, fg_ref, qa_ref, ka_ref, va_ref, carry):
        i = pl.program_id(0)

        @pl.when(i == 0)
        def _():
            carry[...] = jnp.zeros_like(carry)

        z = _dot(x_ref[...], w_ref[...])
        lxg_ref[...] = z[:, QKV_W:QKV_W + 2 * LRU_W]
        fg = z[:, QKV_W + 2 * LRU_W:] + b_ref[...]
        fg_ref[...] = fg
        ls = jnp.minimum(fg, 0.0) - jnp.log(1.0 + jnp.exp(-jnp.abs(fg)))
        r = lax.broadcasted_iota(jnp.int32, (tm, tm), 0)
        c = lax.broadcasted_iota(jnp.int32, (tm, tm), 1)
        cum = _tri_dot(jnp.where(r >= c, 1.0, 0.0).astype(jnp.bfloat16), ls) + carry[0:1, :]
        carry[...] = jnp.broadcast_to(cum[tm - 1:tm, :], carry.shape)

        lane = lax.broadcasted_iota(jnp.int32, (tm, LANES), 1)
        low = lane < HEAD_DIM
        ones_q = jnp.where(jnp.logical_and(lane >= AUX + 3, lane < AUX + 6), 1.0, 0.0)
        ones_k = jnp.where(jnp.logical_and(lane >= AUX, lane < AUX + 3), 1.0, 0.0)
        for j in range(HEADS // 2):
            pair = [z[:, t * FOX_W + j * LANES:t * FOX_W + (j + 1) * LANES] for t in range(3)]
            for odd in range(2):
                h = 2 * j + odd
                q, k, v = [_swap_lane_halves(a) if odd else a for a in pair]
                hi, mid, lo = [a.astype(f32) for a in _split3(jnp.broadcast_to(cum[:, h:h + 1], (tm, LANES)))]
                aux_q = jnp.where(lane == AUX, hi, jnp.where(lane == AUX + 1, mid, jnp.where(lane == AUX + 2, lo, ones_q)))
                aux_k = jnp.where(lane == AUX + 3, -hi,
                                  jnp.where(lane == AUX + 4, -mid, jnp.where(lane == AUX + 5, -lo, ones_k)))
                blk = slice(h * LANES, (h + 1) * LANES)
                qa_ref[:, blk] = jnp.where(low, q, aux_q).astype(qa_ref.dtype)
                ka_ref[:, blk] = jnp.where(low, k, aux_k).astype(ka_ref.dtype)
                va_ref[:, blk] = jnp.where(low, v, 1.0).astype(va_ref.dtype)

    row = lambda i: (i, 0)
    const = lambda i: (0, 0)
    return _pcall(
        body, name=name, grid=(nt,),
        in_specs=[pl.BlockSpec((tm, D_MODEL), row), pl.BlockSpec((D_MODEL, Z_PAD), const),
                  pl.BlockSpec((1, LANES), const)],
        out_specs=[pl.BlockSpec((tm, 2 * LRU_W), row), pl.BlockSpec((tm, LANES), row)]
        + [pl.BlockSpec((tm, HEADS * LANES), row)] * 3,
        out_shape=[jax.ShapeDtypeStruct((T, 2 * LRU_W), f32), jax.ShapeDtypeStruct((T, LANES), f32)]
        + [jax.ShapeDtypeStruct((T, HEADS * LANES), MXU_DTYPE)] * 3,
        scratch_shapes=[pltpu.VMEM((8, LANES), f32)],
        compiler_params=_params(1),
    )(xn, wp, bfp)


def _proj_in_bwd(dqa, dka, dva, dlxg, fgb, xn, dyp, wp, xhat, rstd, ln_g, *, name, tm=512):
    T = xn.shape[0]
    tm = min(tm, T)
    nt = T // tm

    def body(dq_ref, dk_ref, dv_ref, dl_ref, fg_ref, x_ref, dyp_ref, w_ref, xhat_ref, rstd_ref, g_ref,
             dpre_ref, dw_hbm, dgam_ref, dbeta_ref, dbf_ref, dw_sc, carry, sem):
        i = pl.program_id(0)

        @pl.when(i == 0)
        def _():
            dw_sc[...] = jnp.zeros_like(dw_sc)
            dgam_ref[...] = jnp.zeros_like(dgam_ref)
            dbeta_ref[...] = jnp.zeros_like(dbeta_ref)
            dbf_ref[...] = jnp.zeros_like(dbf_ref)
            carry[...] = jnp.zeros_like(carry)

        lane = lax.broadcasted_iota(jnp.int32, (tm, LANES), 1)
        dc = jnp.zeros((tm, LANES), f32)
        for h in range(HEADS):
            row_sum = dq_ref[:, h * LANES + AUX:h * LANES + AUX + 1]
            col_sum = dk_ref[:, h * LANES + AUX + 3:h * LANES + AUX + 4]
            dc = jnp.where(lane == h, jnp.broadcast_to(row_sum - col_sum, (tm, LANES)), dc)
        r = lax.broadcasted_iota(jnp.int32, (tm, tm), 0)
        c = lax.broadcasted_iota(jnp.int32, (tm, tm), 1)
        dls = _tri_dot(jnp.where(c >= r, 1.0, 0.0).astype(jnp.bfloat16), dc) + carry[0:1, :]
        carry[...] = jnp.broadcast_to(dls[0:1, :], carry.shape)
        dfg = dls * _sigmoid(-fg_ref[...])
        dbf_ref[...] += jnp.sum(dfg, axis=0, keepdims=True)

        low = _low_lanes((tm, LANES))

        def packed(ref):
            pairs = [jnp.where(low, ref[:, (2 * j) * LANES:(2 * j + 1) * LANES],
                               _swap_lane_halves(ref[:, (2 * j + 1) * LANES:(2 * j + 2) * LANES]))
                     for j in range(HEADS // 2)]
            return jnp.concatenate(pairs, axis=1).astype(MXU_DTYPE)

        dz = jnp.concatenate(
            [packed(dq_ref), packed(dk_ref), packed(dv_ref),
             dl_ref[...].astype(MXU_DTYPE), dfg.astype(MXU_DTYPE)], axis=1)
        dx = DN_ALPHA * dyp_ref[...] + _dot_nt(dz, w_ref[...])
        dpre, dgam, dbeta = _ln_backward(dx, xhat_ref[...], rstd_ref[:, 0:1], g_ref[...])
        dpre_ref[...] = dpre
        dgam_ref[...] += dgam
        dbeta_ref[...] += dbeta
        dw_sc[...] += _dot_tn(x_ref[...], dz)

        @pl.when(i == nt - 1)
        def _():
            dw_sc[:, :FOX_W] = dw_sc[:, :FOX_W] * (1.0 / math.sqrt(HEAD_DIM))
            cp = pltpu.make_async_copy(dw_sc, dw_hbm, sem)
            cp.start()
            cp.wait()

    row = lambda i: (nt - 1 - i, 0)
    const = lambda i: (0, 0)
    return _pcall(
        body, name=name, grid=(nt,),
        in_specs=[pl.BlockSpec((tm, HEADS * LANES), row), pl.BlockSpec((tm, HEADS * LANES), row),
                  pl.BlockSpec((tm, HEADS * LANES), row),
                  pl.BlockSpec((tm, 2 * LRU_W), row), pl.BlockSpec((tm, LANES), row),
                  pl.BlockSpec((tm, D_MODEL), row), pl.BlockSpec((tm, D_MODEL), row),
                  pl.BlockSpec((D_MODEL, Z_PAD), const),
                  pl.BlockSpec((tm, D_MODEL), row), pl.BlockSpec((tm, LANES), row), pl.BlockSpec((1, D_MODEL), const)],
        out_specs=[pl.BlockSpec((tm, D_MODEL), row), pl.BlockSpec(memory_space=pl.ANY),
                   pl.BlockSpec((1, D_MODEL), const), pl.BlockSpec((1, D_MODEL), const), pl.BlockSpec((1, LANES), const)],
        out_shape=[jax.ShapeDtypeStruct((T, D_MODEL), f32), jax.ShapeDtypeStruct((D_MODEL, Z_PAD), f32),
                   jax.ShapeDtypeStruct((1, D_MODEL), f32), jax.ShapeDtypeStruct((1, D_MODEL), f32),
                   jax.ShapeDtypeStruct((1, LANES), f32)],
        scratch_shapes=[pltpu.VMEM((D_MODEL, Z_PAD), f32), pltpu.VMEM((8, LANES), f32), pltpu.SemaphoreType.DMA],
        compiler_params=_params(1),
    )(dqa, dka, dva, dlxg, fgb, xn, dyp, wp, xhat, rstd, ln_g)


def _split3(x):
    hi = x.astype(jnp.bfloat16)
    r1 = x - hi.astype(f32)
    mid = r1.astype(jnp.bfloat16)
    lo = (r1 - mid.astype(f32)).astype(jnp.bfloat16)
    return hi, mid, lo


def _tri_dot(tri, x):
    hi, mid, lo = _split3(x)
    return _dot(tri, hi) + _dot(tri, mid) + _dot(tri, lo)


FOX_PAD = HEADS * LANES
AUX = HEAD_DIM


def _low_lanes(shape):
    return lax.broadcasted_iota(jnp.int32, shape, 1) < HEAD_DIM


def _swap_lane_halves(x):
    return pltpu.roll(x, HEAD_DIM, 1)


def _future_keys(tq, tk):
    r = lax.broadcasted_iota(jnp.int32, (tq, tk), 0)
    c = lax.broadcasted_iota(jnp.int32, (tq, tk), 1)
    return c > r


def _causal_steps(nq, key_major):
    if key_major:
        pairs = [(qi, ki) for ki in range(nq) for qi in range(ki, nq)]
    else:
        pairs = [(qi, ki) for qi in range(nq) for ki in range(qi + 1)]
    return (jnp.asarray([p[0] for p in pairs], jnp.int32), jnp.asarray([p[1] for p in pairs], jnp.int32))


def _fox_fwd(qa, ka, va, *, name, tq=512, hps=8):
    T = qa.shape[0]
    tq = min(tq, T)
    tk = tq
    nq = T // tq
    rep = tk // LANES
    qi_tab, ki_tab = _causal_steps(nq, key_major=False)

    def body(qi_ref, ki_ref, qa_ref, ka_ref, va_ref, o_ref, lse_ref, m_sc, acc_sc):
        t = pl.program_id(1)
        qi = qi_ref[t]
        ki = ki_ref[t]

        @pl.when(ki == 0)
        def _():
            m_sc[...] = jnp.full_like(m_sc, NEG_BIG)
            acc_sc[...] = jnp.zeros_like(acc_sc)

        def tile(diagonal):
            for h in range(hps):
                blk = slice(h * LANES, (h + 1) * LANES)
                s = _dot_nt(qa_ref[:, blk], ka_ref[:, blk])
                if diagonal:
                    s = jnp.where(_future_keys(tq, tk), NEG_BIG, s)
                m_prev = m_sc[h]
                m_new = jnp.maximum(m_prev, jnp.max(s, axis=1, keepdims=True))
                p = jnp.exp(s - jnp.tile(m_new, (1, rep)))
                acc_sc[h] = jnp.exp(m_prev - m_new) * acc_sc[h] + _dot(p.astype(MXU_DTYPE), va_ref[:, blk])
                m_sc[h] = m_new

        @pl.when(ki < qi)
        def _():
            tile(False)

        @pl.when(ki == qi)
        def _():
            tile(True)
            low = _low_lanes((tq, LANES))
            outs = []
            for h in range(hps):
                acc = acc_sc[h]
                den = _swap_lane_halves(acc)
                outs.append(acc / den)
                lse_ref[h] = m_sc[h] + jnp.log(jnp.where(low, den, acc))
            for p in range(hps // 2):
                o_ref[:, p * LANES:(p + 1) * LANES] = jnp.where(low, outs[2 * p], _swap_lane_halves(outs[2 * p + 1]))

    pair = hps * LANES
    return _pcall(
        body, name=name,
        grid_spec=pltpu.PrefetchScalarGridSpec(
            num_scalar_prefetch=2, grid=(HEADS // hps, qi_tab.shape[0]),
            in_specs=[
                pl.BlockSpec((tq, pair), lambda j, t, qi_ref, ki_ref: (qi_ref[t], j)),
                pl.BlockSpec((tk, pair), lambda j, t, qi_ref, ki_ref: (ki_ref[t], j)),
                pl.BlockSpec((tk, pair), lambda j, t, qi_ref, ki_ref: (ki_ref[t], j)),
            ],
            out_specs=[pl.BlockSpec((tq, pair // 2), lambda j, t, qi_ref, ki_ref: (qi_ref[t], j)),
                       pl.BlockSpec((hps, tq, LANES), lambda j, t, qi_ref, ki_ref: (j, qi_ref[t], 0))],
            scratch_shapes=[pltpu.VMEM((hps, tq, LANES), f32)] * 2),
        out_shape=[jax.ShapeDtypeStruct((T, FOX_W), f32), jax.ShapeDtypeStruct((HEADS, T, LANES), f32)],
        compiler_params=_params(2),
    )(qi_tab, ki_tab, qa, ka, va)


def _fox_bwd(qa, ka, va, doa, lse, drep, *, name, tq=512, hps=4):
    T = qa.shape[0]
    tq = min(tq, T)
    tk = tq
    nq = T // tq
    rep = tk // LANES
    qi_tab, ki_tab = _causal_steps(nq, key_major=True)

    def body(qi_ref, ki_ref, qa_ref, ka_ref, va_ref, doa_ref, lse_ref, d_ref, dqa_ref, dka_ref, dva_ref, dk_sc, dv_sc):
        t = pl.program_id(1)
        qi = qi_ref[t]
        ki = ki_ref[t]
        rows = pl.ds(pl.multiple_of(qi * tq, tq), tq)

        @pl.when(t == 0)
        def _():
            dqa_ref[...] = jnp.zeros_like(dqa_ref)

        @pl.when(qi == ki)
        def _():
            dk_sc[...] = jnp.zeros_like(dk_sc)
            dv_sc[...] = jnp.zeros_like(dv_sc)

        def tile(diagonal):
            for h in range(hps):
                blk = slice(h * LANES, (h + 1) * LANES)
                qh, kh, doh = qa_ref[:, blk], ka_ref[:, blk], doa_ref[:, blk]
                p = jnp.exp(_dot_nt(qh, kh) - jnp.tile(lse_ref[h], (1, rep)))
                if diagonal:
                    p = jnp.where(_future_keys(tq, tk), 0.0, p)
                dp = _dot_nt(doh, va_ref[:, blk])
                ds = (p * (dp - jnp.tile(d_ref[h], (1, rep)))).astype(MXU_DTYPE)
                dv_sc[h] += _dot_tn(p.astype(MXU_DTYPE), doh)
                dk_sc[h] += _dot_tn(ds, qh)
                dqa_ref[rows, blk] += _dot(ds, kh)

        @pl.when(qi > ki)
        def _():
            tile(False)

        @pl.when(qi == ki)
        def _():
            tile(True)

        @pl.when(qi == nq - 1)
        def _():
            for h in range(hps):
                blk = slice(h * LANES, (h + 1) * LANES)
                dka_ref[:, blk] = dk_sc[h]
                dva_ref[:, blk] = dv_sc[h]

    pair = hps * LANES
    q_blk = lambda j, t, qi_ref, ki_ref: (qi_ref[t], j)
    k_blk = lambda j, t, qi_ref, ki_ref: (ki_ref[t], j)
    stat = pl.BlockSpec((hps, tq, LANES), lambda j, t, qi_ref, ki_ref: (j, qi_ref[t], 0))
    return _pcall(
        body, name=name,
        grid_spec=pltpu.PrefetchScalarGridSpec(
            num_scalar_prefetch=2, grid=(HEADS // hps, qi_tab.shape[0]),
            in_specs=[pl.BlockSpec((tq, pair), q_blk), pl.BlockSpec((tk, pair), k_blk), pl.BlockSpec((tk, pair), k_blk),
                      pl.BlockSpec((tq, pair), q_blk), stat, stat],
            out_specs=[pl.BlockSpec((T, pair), lambda j, t, qi_ref, ki_ref: (0, j)),
                       pl.BlockSpec((tk, pair), k_blk), pl.BlockSpec((tk, pair), k_blk)],
            scratch_shapes=[pltpu.VMEM((hps, tk, LANES), f32)] * 2),
        out_shape=[jax.ShapeDtypeStruct((T, FOX_PAD), f32)] * 3,
        compiler_params=_params(2),
    )(qi_tab, ki_tab, qa, ka, va, doa, lse, drep)


GELU_C = math.sqrt(2.0 / math.pi)
GELU_A = 0.044715


def _gelu(x):
    t = jnp.tanh(GELU_C * (x + GELU_A * x * x * x))
    return 0.5 * x * (1.0 + t), t


def _gelu_grad(x, t):
    return 0.5 * (1.0 + t) + 0.5 * x * (1.0 - t * t) * GELU_C * (1.0 + 3.0 * GELU_A * x * x)


EXPM1_SERIES_BELOW = 0.25


def _expm1(x, e):
    series = x * (1.0 + x * (1 / 2 + x * (1 / 6 + x * (1 / 24 + x * (1 / 120 + x * (1 / 720))))))
    return jnp.where(x > -EXPM1_SERIES_BELOW, series, e - 1.0)


def _lru_gates(u, wab_ref, bab_ref, lam_ref):
    pre = _dot(u.astype(MXU_DTYPE), wab_ref[...]) + bab_ref[...]
    r = _sigmoid(pre[:, :LRU_W])
    gi = _sigmoid(pre[:, LRU_W:])
    lam = lam_ref[...]
    sp = jnp.maximum(-lam, 0.0) + jnp.log(1.0 + jnp.exp(-jnp.abs(lam)))
    log_a = -LRU_C * r * sp
    a = jnp.exp(log_a)
    s = jnp.sqrt(-_expm1(2.0 * log_a, a * a))
    return r, gi, sp, a, s


def _lru_fwd(lxg, conv_w, conv_b, wab, bab, lam, *, name, tc=512):
    T = lxg.shape[0]
    tc = min(tc, T)
    nc = T // tc

    def body(lx_ref, lg_ref, cw_ref, cb_ref, wab_ref, bab_ref, lam_ref,
             out_ref, u_ref, hs_ref, ext, a_sc, b_sc, h_sc):
        i = pl.program_id(0)

        @pl.when(i == 0)
        def _():
            ext[0:8, :] = jnp.zeros((8, LRU_W), f32)
            h_sc[...] = jnp.zeros_like(h_sc)

        ext[8:, :] = lx_ref[...]
        u = cb_ref[...] + cw_ref[0:1, :] * ext[pl.ds(5, tc), :]
        for k in range(1, CONV_K):
            u = u + cw_ref[k:k + 1, :] * ext[pl.ds(5 + k, tc), :]
        ext[0:8, :] = ext[tc:tc + 8, :]
        u_ref[...] = u
        r, gi, sp, a, s = _lru_gates(u, wab_ref, bab_ref, lam_ref)
        a_sc[...] = a
        b_sc[...] = s * (gi * u)

        def step(t, h):
            h = a_sc[pl.ds(t, 1), :] * h + b_sc[pl.ds(t, 1), :]
            hs_ref[pl.ds(t, 1), :] = h
            return h

        h = lax.fori_loop(0, tc, step, h_sc[0:1, :], unroll=8)
        h_sc[...] = jnp.broadcast_to(h, h_sc.shape)
        gel, _ = _gelu(lg_ref[...])
        out_ref[...] = gel * hs_ref[...]

    row = lambda i: (i, 0)
    const = lambda i: (0, 0)
    return _pcall(
        body, name=name, grid=(nc,),
        in_specs=[pl.BlockSpec((tc, LRU_W), row), pl.BlockSpec((tc, LRU_W), lambda i: (i, 1)),
                  pl.BlockSpec((CONV_K, LRU_W), const), pl.BlockSpec((1, LRU_W), const),
                  pl.BlockSpec((LRU_W, 2 * LRU_W), const), pl.BlockSpec((1, 2 * LRU_W), const),
                  pl.BlockSpec((1, LRU_W), const)],
        out_specs=[pl.BlockSpec((tc, LRU_W), row)] * 3,
        out_shape=[jax.ShapeDtypeStruct((T, LRU_W), f32)] * 3,
        scratch_shapes=[pltpu.VMEM((tc + 8, LRU_W), f32), pltpu.VMEM((tc, LRU_W), f32),
                        pltpu.VMEM((tc, LRU_W), f32), pltpu.VMEM((8, LRU_W), f32)],
        compiler_params=_params(1),
    )(lxg, lxg, conv_w, conv_b, wab, bab, lam)


def _lru_bwd(dlru, lxg, u, hs, conv_w, wab, bab, lam, *, name, tc=512):
    T = lxg.shape[0]
    tc = min(tc, T)
    nc = T // tc
    bp = tc // 8

    def body(dl_ref, lx_ref, lxp_ref, lg_ref, u_ref, hs_ref, hsp_ref, cw_ref, wab_ref, bab_ref, lam_ref,
             dlxg_ref, dwab_ref, dbab_ref, dcw_ref, dcb_ref, dlam_ref,
             dh_sc, a_sc, ext, du_ext, carry):
        i = pl.program_id(0)
        first_chunk = i == nc - 1

        @pl.when(i == 0)
        def _():
            dwab_ref[...] = jnp.zeros_like(dwab_ref)
            dbab_ref[...] = jnp.zeros_like(dbab_ref)
            dcw_ref[...] = jnp.zeros_like(dcw_ref)
            dcb_ref[...] = jnp.zeros_like(dcb_ref)
            dlam_ref[...] = jnp.zeros_like(dlam_ref)
            carry[...] = jnp.zeros_like(carry)
            du_ext[tc:tc + 8, :] = jnp.zeros((8, LRU_W), f32)

        lg = lg_ref[...]
        gel, th = _gelu(lg)
        dl = dl_ref[...]
        hs = hs_ref[...]
        dlg = dl * hs * _gelu_grad(lg, th)
        u = u_ref[...]
        r, gi, sp, a, s = _lru_gates(u, wab_ref, bab_ref, lam_ref)
        a_sc[...] = a
        dh_sc[...] = dl * gel

        def step(k, c):
            t = tc - 1 - k
            dh = dh_sc[pl.ds(t, 1), :] + c
            dh_sc[pl.ds(t, 1), :] = dh
            return a_sc[pl.ds(t, 1), :] * dh

        c = lax.fori_loop(0, tc, step, carry[0:1, :], unroll=8)
        carry[...] = jnp.broadcast_to(c, carry.shape)

        ext[0:8, :] = jnp.where(first_chunk, 0.0, hsp_ref[...])
        ext[8:, :] = hs
        hprev = ext[pl.ds(7, tc), :]
        dh = dh_sc[...]
        da = dh * hprev
        giu = gi * u
        dla = da * a - (dh * giu) * (a * a / s)
        dgi = dh * s * u
        du = dh * s * gi
        dr = dla * (-LRU_C * sp)
        dlam_ref[...] += jnp.sum(dla * (-LRU_C * r), axis=0, keepdims=True) * (-_sigmoid(-lam_ref[...]))
        dpre = jnp.concatenate([dr * r * (1.0 - r), dgi * gi * (1.0 - gi)], axis=1)
        dpre_b = dpre.astype(MXU_DTYPE)
        du = du + _dot_nt(dpre_b, wab_ref[...])
        dwab_ref[...] += _dot_tn(u.astype(MXU_DTYPE), dpre_b)
        dbab_ref[...] += jnp.sum(dpre, axis=0, keepdims=True)
        dcb_ref[...] += jnp.sum(du, axis=0, keepdims=True)

        du_ext[0:tc, :] = du
        dlx = cw_ref[0:1, :] * du_ext[pl.ds(3, tc), :]
        for k in range(1, CONV_K):
            dlx = dlx + cw_ref[k:k + 1, :] * du_ext[pl.ds(3 - k, tc), :]
        du_ext[tc:tc + 8, :] = du_ext[0:8, :]
        ext[0:8, :] = jnp.where(first_chunk, 0.0, lxp_ref[...])
        ext[8:, :] = lx_ref[...]
        for k in range(CONV_K):
            dcw_ref[k:k + 1, :] += jnp.sum(du * ext[pl.ds(5 + k, tc), :], axis=0, keepdims=True)
        dlxg_ref[:, :LRU_W] = dlx.astype(dlxg_ref.dtype)
        dlxg_ref[:, LRU_W:] = dlg.astype(dlxg_ref.dtype)

    rev = lambda i: (nc - 1 - i, 0)
    prev8 = lambda i: (jnp.maximum((nc - 1 - i) * bp - 1, 0), 0)
    const = lambda i: (0, 0)
    return _pcall(
        body, name=name, grid=(nc,),
        in_specs=[
            pl.BlockSpec((tc, LRU_W), rev),
            pl.BlockSpec((tc, LRU_W), rev),
            pl.BlockSpec((8, LRU_W), prev8),
            pl.BlockSpec((tc, LRU_W), lambda i: (nc - 1 - i, 1)),
            pl.BlockSpec((tc, LRU_W), rev),
            pl.BlockSpec((tc, LRU_W), rev),
            pl.BlockSpec((8, LRU_W), prev8),
            pl.BlockSpec((CONV_K, LRU_W), const),
            pl.BlockSpec((LRU_W, 2 * LRU_W), const),
            pl.BlockSpec((1, 2 * LRU_W), const),
            pl.BlockSpec((1, LRU_W), const),
        ],
        out_specs=[
            pl.BlockSpec((tc, 2 * LRU_W), rev),
            pl.BlockSpec((LRU_W, 2 * LRU_W), const),
            pl.BlockSpec((1, 2 * LRU_W), const),
            pl.BlockSpec((8, LRU_W), const),
            pl.BlockSpec((1, LRU_W), const),
            pl.BlockSpec((1, LRU_W), const),
        ],
        out_shape=[
            jax.ShapeDtypeStruct((T, 2 * LRU_W), MXU_DTYPE),
            jax.ShapeDtypeStruct((LRU_W, 2 * LRU_W), f32),
            jax.ShapeDtypeStruct((1, 2 * LRU_W), f32),
            jax.ShapeDtypeStruct((8, LRU_W), f32),
            jax.ShapeDtypeStruct((1, LRU_W), f32),
            jax.ShapeDtypeStruct((1, LRU_W), f32),
        ],
        scratch_shapes=[pltpu.VMEM((tc, LRU_W), f32), pltpu.VMEM((tc, LRU_W), f32),
                        pltpu.VMEM((tc + 8, LRU_W), f32), pltpu.VMEM((tc + 8, LRU_W), f32),
                        pltpu.VMEM((8, LRU_W), f32)],
        compiler_params=_params(1),
    )(dlru, lxg, lxg, lxg, u, hs, hs, conv_w, wab, bab, lam)


def _mix_out(fox, lru, wo, xhat1, g1, b1, g2, b2, *, name, tm=512):
    T = fox.shape[0]
    tm = min(tm, T)
    nt = T // tm

    def body(fox_ref, lru_ref, wo_ref, xh_ref, g1_ref, b1_ref, g2_ref, b2_ref, xhat_ref, xn_ref, rstd_ref):
        mix = _dot(fox_ref[...].astype(MXU_DTYPE), wo_ref[:FOX_W, :])
        mix = mix + _dot(lru_ref[...].astype(MXU_DTYPE), wo_ref[FOX_W:, :])
        x1 = xh_ref[...] * g1_ref[...] + b1_ref[...]
        xhat, rstd = _layer_norm_stats(DN_ALPHA * x1 + mix)
        xhat_ref[...] = xhat
        xn_ref[...] = xhat * g2_ref[...] + b2_ref[...]
        rstd_ref[...] = jnp.broadcast_to(rstd, rstd_ref.shape)

    row = lambda i: (i, 0)
    const = lambda i: (0, 0)
    vec = pl.BlockSpec((1, D_MODEL), const)
    return _pcall(
        body, name=name, grid=(nt,),
        in_specs=[pl.BlockSpec((tm, FOX_W), row), pl.BlockSpec((tm, LRU_W), row),
                  pl.BlockSpec((D_MODEL, D_MODEL), const), pl.BlockSpec((tm, D_MODEL), row), vec, vec, vec, vec],
        out_specs=[pl.BlockSpec((tm, D_MODEL), row), pl.BlockSpec((tm, D_MODEL), row),
                   pl.BlockSpec((tm, LANES), row)],
        out_shape=[jax.ShapeDtypeStruct((T, D_MODEL), f32), jax.ShapeDtypeStruct((T, D_MODEL), f32),
                   jax.ShapeDtypeStruct((T, LANES), f32)],
        compiler_params=_params(1),
    )(fox, lru, wo, xhat1, g1, b1, g2, b2)


def _mix_out_bwd(dy, xhat, rstd, ln_g, fox, lru, wo, *, name, tm=512):
    T = fox.shape[0]
    tm = min(tm, T)
    nt = T // tm

    def body(dy_ref, xhat_ref, rstd_ref, g_ref, fox_ref, lru_ref, wo_ref,
             dyp_ref, dgam_ref, dbeta_ref, dlru_ref, dwo_ref, d_ref, doa_ref):
        i = pl.program_id(0)

        @pl.when(i == 0)
        def _():
            dwo_ref[...] = jnp.zeros_like(dwo_ref)
            dgam_ref[...] = jnp.zeros_like(dgam_ref)
            dbeta_ref[...] = jnp.zeros_like(dbeta_ref)

        dyp, dgam, dbeta = _ln_backward(dy_ref[...], xhat_ref[...], rstd_ref[:, 0:1], g_ref[...])
        dyp_ref[...] = dyp
        dgam_ref[...] += dgam
        dbeta_ref[...] += dbeta
        dmix = dyp.astype(MXU_DTYPE)
        dcat = _dot_nt(dmix, wo_ref[...])
        dlru_ref[...] = dcat[:, FOX_W:]
        low = _low_lanes((tm, LANES))
        for j in range(HEADS // 2):
            do2 = dcat[:, j * LANES:(j + 1) * LANES].astype(MXU_DTYPE).astype(f32)
            prod = do2 * fox_ref[:, j * LANES:(j + 1) * LANES]
            for odd in range(2):
                h = 2 * j + odd
                mine = jnp.where(low, _swap_lane_halves(prod) if odd else prod, 0.0)
                d_ref[h] = jnp.broadcast_to(jnp.sum(mine, axis=1, keepdims=True), (tm, LANES))
                doh = jnp.where(low, _swap_lane_halves(do2) if odd else do2, 0.0)
                doa_ref[:, h * LANES:(h + 1) * LANES] = doh.astype(doa_ref.dtype)
        dwo_ref[:FOX_W, :] += _dot_tn(fox_ref[...].astype(MXU_DTYPE), dmix)
        dwo_ref[FOX_W:, :] += _dot_tn(lru_ref[...].astype(MXU_DTYPE), dmix)

    row = lambda i: (i, 0)
    const = lambda i: (0, 0)
    return _pcall(
        body, name=name, grid=(nt,),
        in_specs=[pl.BlockSpec((tm, D_MODEL), row), pl.BlockSpec((tm, D_MODEL), row), pl.BlockSpec((tm, LANES), row),
                  pl.BlockSpec((1, D_MODEL), const),
                  pl.BlockSpec((tm, FOX_W), row), pl.BlockSpec((tm, LRU_W), row),
                  pl.BlockSpec((D_MODEL, D_MODEL), const)],
        out_specs=[pl.BlockSpec((tm, D_MODEL), row), pl.BlockSpec((1, D_MODEL), const), pl.BlockSpec((1, D_MODEL), const),
                   pl.BlockSpec((tm, LRU_W), row), pl.BlockSpec((D_MODEL, D_MODEL), const),
                   pl.BlockSpec((HEADS, tm, LANES), lambda i: (0, i, 0)), pl.BlockSpec((tm, HEADS * LANES), row)],
        out_shape=[jax.ShapeDtypeStruct((T, D_MODEL), f32), jax.ShapeDtypeStruct((1, D_MODEL), f32),
                   jax.ShapeDtypeStruct((1, D_MODEL), f32),
                   jax.ShapeDtypeStruct((T, LRU_W), f32), jax.ShapeDtypeStruct((D_MODEL, D_MODEL), f32),
                   jax.ShapeDtypeStruct((HEADS, T, LANES), f32), jax.ShapeDtypeStruct((T, HEADS * LANES), MXU_DTYPE)],
        compiler_params=_params(1),
    )(dy, xhat, rstd, ln_g, fox, lru, wo)


def make_wp(w_in):
    scale = jnp.concatenate([jnp.full((FOX_W,), 1.0 / math.sqrt(HEAD_DIM), w_in.dtype),
                             jnp.ones((IN_COLS - FOX_W,), w_in.dtype)])
    return jnp.pad(w_in * scale[None, :], ((0, 0), (0, Z_PAD - IN_COLS)))


def _block_diag(w):
    eye = jnp.eye(HEADS, dtype=w.dtype)
    return jnp.einsum("hij,hg->higj", w, eye).reshape(LRU_W, LRU_W)


def _block_diag_extract(m):
    m4 = m.reshape(HEADS, HEAD_DIM, HEADS, HEAD_DIM)
    return jnp.stack([m4[h, :, h, :] for h in range(HEADS)])


class _NoOverlap:
    def start_token(self):
        return None

    def late_weights(self, w, after):
        return dict(f1d=w["f1d"], wp=w["wp"], wo=w["wo"])

    def after_attention(self, after):
        return None

    def ffn2_weights(self, w, after):
        return w["f2g"], w["f2u"], w["f2d"]

    def ffn2_grads(self, grads):
        return None

    def ffn1_grads(self, grads):
        return None

    def mixer_grads(self, dwp, dwo, small, loss):
        return None

    def before_ffn1_bwd(self, after):
        return None


def _tied(a, token):
    return a if token is None else a + token[0, 0]


def _local_step(x, target, w, hooks=None):
    hooks = hooks or _NoOverlap()
    bfp = w["bfp"]
    wab = jnp.concatenate([_block_diag(w["rg_wa"]), _block_diag(w["rg_wx"])], axis=1).astype(MXU_DTYPE)
    bab = jnp.concatenate([w["rg_ba"].reshape(1, LRU_W), w["rg_bx"].reshape(1, LRU_W)], axis=1)

    xb0, g1a, u1a, h1a = _ffn_up(x, w["f1g"], w["f1u"], hooks.start_token(), name="ffn1_up")
    late = hooks.late_weights(w, [h1a])
    f1d, wp, wo = late["f1d"], late["wp"], late["wo"]
    xhat1, xn1, rstd1 = _ffn_down_ln(x, h1a, f1d, w["ln1_g"], w["ln1_b"], name="ffn1_down")
    lxg, fgb, qa, ka, va = _proj_in(xn1, wp, bfp, name="proj_in")
    fox, lse = _fox_fwd(qa, ka, va, name="fox_fwd")
    token = hooks.after_attention([lse])
    lru, uconv, hs = _lru_fwd(lxg, w["conv_w"], _tied(w["conv_b"], token), wab, bab, w["lam"], name="lru_fwd")
    xhat2, x2, rstd2 = _mix_out(fox, lru, wo, xhat1, w["ln1_g"], w["ln1_b"], w["ln2_g"], w["ln2_b"], name="mix_out")
    f2g, f2u, f2d = hooks.ffn2_weights(w, [rstd2])
    xb2, g2a, u2a, dy3p, dln3g, dln3b, loss = _ffn_fwd_loss(x2, f2g, f2u, f2d, w["ln3_g"], w["ln3_b"], target,
                                                            name="ffn2_fwd_loss")

    dx2, df2g, df2u, df2d = _ffn_bwd(dy3p, xb2, g2a, u2a, f2g, f2u, f2d, name="ffn2_bwd")
    token = hooks.ffn2_grads([df2g, df2u, df2d])
    dy2p, dln2g, dln2b, dlru, dwo, drep, doa = _mix_out_bwd(dx2, xhat2, rstd2, _tied(w["ln2_g"], token), fox, lru, wo,
                                                            name="mix_out_bwd")
    dlxg, dwab, dbab, dcw, dcb, dlam = _lru_bwd(dlru, lxg, uconv, hs, w["conv_w"], wab, bab, w["lam"], name="lru_bwd")
    dqa, dka, dva = _fox_bwd(qa, ka, va, doa, lse, drep, name="fox_bwd")
    dy1p, dwp, dln1g, dln1b, dbf = _proj_in_bwd(dqa, dka, dva, dlxg, fgb, xn1, dy2p, wp, xhat1, rstd1, w["ln1_g"],
                                                name="proj_in_bwd")
    small = dict(
        ln1_g=dln1g, ln1_b=dln1b, ln2_g=dln2g, ln2_b=dln2b, ln3_g=dln3g, ln3_b=dln3b,
        b_forget=dbf[:, :HEADS], conv_w=dcw[:CONV_K], conv_b=dcb,
        rg_wa=_block_diag_extract(dwab[:, :LRU_W]), rg_wx=_block_diag_extract(dwab[:, LRU_W:]),
        rg_ba=dbab[:, :LRU_W].reshape(HEADS, HEAD_DIM), rg_bx=dbab[:, LRU_W:].reshape(HEADS, HEAD_DIM),
        lru_lambda=dlam,
    )
    hooks.before_ffn1_bwd([dln1b])
    token = hooks.mixer_grads(dwp, dwo, small, loss)
    dx_a, *grads_a = _ffn_bwd(dy1p, xb0, g1a, u1a, w["f1g"], w["f1u"], f1d, token, name="ffn1_bwd_a", part=0)
    token = hooks.ffn1_grads(grads_a)
    dx, *grads_b = _ffn_bwd(dy1p, xb0, g1a, u1a, w["f1g"], w["f1u"], f1d, token, name="ffn1_bwd_b", part=1,
                            dx_init=dx_a)

    grads = dict(f1=(grads_a, grads_b), f2g=df2g, f2u=df2u, f2d=df2d, wp=dwp, wo=dwo, **small)
    return loss, dx, grads


MESH = pl.DeviceIdType.MESH
HBM_SPEC = pl.BlockSpec(memory_space=pl.ANY)
VMEM_SPEC = pl.BlockSpec(memory_space=pltpu.VMEM)


def _position():
    return lax.axis_index("x"), lax.axis_index("y"), lax.axis_index("c")


def _other_chips(x, y):
    return [(1 - x, y), (x, 1 - y), (1 - x, 1 - y)]


def _all_gather_bf16(shards, *, name):
    n = len(shards)

    def body(*refs):
        ins, outs, stages = refs[:n], refs[n:2 * n], refs[2 * n:3 * n]
        send_sems, recv_sems, local_sems = refs[3 * n:]
        x, y, c = _position()
        me, sibling = (x, y, c), (x, y, 1 - c)
        chips = _other_chips(x, y)

        def rows(k, px, py, pc):
            r = shards[k].shape[0]
            m = r // 2
            return outs[k].at[pl.ds(pl.multiple_of((2 * px + py) * r + pc * m, 16), m), :]

        def copy(k, idx, block, to, src=None):
            return pltpu.make_async_remote_copy(
                src_ref=rows(k, *block) if src is None else src, dst_ref=rows(k, *block),
                send_sem=send_sems.at[7 * k + idx], recv_sem=recv_sems.at[7 * k + idx],
                device_id=to, device_id_type=MESH)

        started = []
        mine = []
        for k in range(n):
            m = shards[k].shape[0] // 2
            stages[k][...] = ins[k][pl.ds(pl.multiple_of(c * m, 16), m), :].astype(stages[k].dtype)
            cp = pltpu.make_async_copy(stages[k], rows(k, *me), local_sems.at[k])
            cp.start()
            mine.append(cp)
            first = [copy(k, 0, me, sibling, src=stages[k])]
            first += [copy(k, 1 + j, me, (*chip, c), src=stages[k]) for j, chip in enumerate(chips)]
            for cp in first:
                cp.start()
            started += first
        for k in range(n):
            for j, chip in enumerate(chips):
                copy(k, 1 + j, (*chip, c), me).wait_recv()
                fwd = copy(k, 4 + j, (*chip, c), sibling)
                fwd.start()
                started.append(fwd)
        for k in range(n):
            copy(k, 0, sibling, me).wait_recv()
            for j, chip in enumerate(chips):
                copy(k, 4 + j, (*chip, 1 - c), me).wait_recv()
        for cp in started:
            cp.wait_send()
        for cp in mine:
            cp.wait()

    return _pcall(
        body, name=name,
        in_specs=[VMEM_SPEC] * n, out_specs=[HBM_SPEC] * n,
        out_shape=[jax.ShapeDtypeStruct((N_SHARD * s.shape[0], s.shape[1]), MXU_DTYPE) for s in shards],
        scratch_shapes=[pltpu.VMEM((s.shape[0] // 2, s.shape[1]), MXU_DTYPE) for s in shards]
        + [pltpu.SemaphoreType.DMA((7 * n,)), pltpu.SemaphoreType.DMA((7 * n,)), pltpu.SemaphoreType.DMA((n,))],
        compiler_params=pltpu.CompilerParams(vmem_limit_bytes=VMEM_LIMIT),
    )(*shards)


def _swap_halves(gs, *, name):
    n = len(gs)

    def body(*refs):
        ins, outs = refs[:n], refs[n:2 * n]
        send_sems, recv_sems = refs[2 * n:]
        x, y, c = _position()
        cps = []
        for k in range(n):
            m = gs[k].shape[1] // 2
            src = ins[k].at[:, pl.ds(pl.multiple_of((1 - c) * m, 16), m), :]
            cp = pltpu.make_async_remote_copy(src_ref=src, dst_ref=outs[k], send_sem=send_sems.at[k],
                                              recv_sem=recv_sems.at[k], device_id=(x, y, 1 - c), device_id_type=MESH)
            cp.start()
            cps.append(cp)
        for cp in cps:
            cp.wait()

    return _pcall(
        body, name=name, in_specs=[HBM_SPEC] * n, out_specs=[HBM_SPEC] * n,
        out_shape=[jax.ShapeDtypeStruct((g.shape[0], g.shape[1] // 2, g.shape[2]), g.dtype) for g in gs],
        scratch_shapes=[pltpu.SemaphoreType.DMA((n,)), pltpu.SemaphoreType.DMA((n,))],
    )(*gs)


def _add_halves(gs, recvs, *, name, tm=256):
    n = len(gs)
    _, r, cdim = gs[0].shape
    m = r // 2
    tm = min(tm, m)
    nb = m // tm
    c_idx = lax.axis_index("c").astype(jnp.int32).reshape(1)

    def body(c_ref, *refs):
        for k in range(n):
            refs[2 * n + k][...] = (refs[k][...].astype(f32) + refs[n + k][...].astype(f32)).astype(refs[2 * n + k].dtype)

    mine = pl.BlockSpec((None, tm, cdim), lambda j, i, c_ref: (j, c_ref[0] * nb + i, 0))
    half = pl.BlockSpec((None, tm, cdim), lambda j, i, c_ref: (j, i, 0))
    return _pcall(
        body, name=name,
        grid_spec=pltpu.PrefetchScalarGridSpec(
            num_scalar_prefetch=1, grid=(N_SHARD, nb),
            in_specs=[mine] * n + [half] * n, out_specs=[half] * n),
        out_shape=[jax.ShapeDtypeStruct((N_SHARD, m, cdim), g.dtype) for g in gs],
        compiler_params=_params(2),
    )(c_idx, *gs, *recvs)


def _scatter_partials(ps, *, name):
    n = len(ps)

    def body(*refs):
        ins, outs = refs[:n], refs[n:2 * n]
        send_sems, recv_sems = refs[2 * n:]
        x, y, c = _position()
        me_chip = 2 * x + y
        cps = []
        for k in range(n):
            for j, (px, py) in enumerate(_other_chips(x, y)):
                cp = pltpu.make_async_remote_copy(
                    src_ref=ins[k].at[2 * px + py], dst_ref=outs[k].at[me_chip],
                    send_sem=send_sems.at[3 * k + j], recv_sem=recv_sems.at[3 * k + j],
                    device_id=(px, py, c), device_id_type=MESH)
                cp.start()
                cps.append(cp)
        for cp in cps:
            cp.wait()

    return _pcall(
        body, name=name, in_specs=[HBM_SPEC] * n, out_specs=[HBM_SPEC] * n,
        out_shape=[jax.ShapeDtypeStruct(p.shape, p.dtype) for p in ps],
        scratch_shapes=[pltpu.SemaphoreType.DMA((3 * n,)), pltpu.SemaphoreType.DMA((3 * n,))],
    )(*ps)


def _sum_slabs(ps, qs, *, name, tm=128):
    n = len(qs)
    _, m, cdim = qs[0].shape
    tm = min(tm, m)
    nb = m // tm
    assert m % tm == 0, (m, tm)
    where = jnp.stack([2 * lax.axis_index("x") + lax.axis_index("y"), lax.axis_index("c")]).astype(jnp.int32)

    def body(w_ref, *refs):
        for k in range(n):
            own, q1, q2, q3 = (refs[4 * k + t][...].astype(f32) for t in range(4))
            refs[4 * n + k][...] = ((own + q1) + q2) + q3

    def slab(flip):
        return pl.BlockSpec((None, tm, cdim), lambda i, w_ref: (jnp.bitwise_xor(w_ref[0], flip), i, 0))

    operands = []
    for p, q in zip(ps, qs):
        operands += [p, q, q, q]
    return _pcall(
        body, name=name,
        grid_spec=pltpu.PrefetchScalarGridSpec(
            num_scalar_prefetch=1, grid=(nb,),
            in_specs=[slab(0), slab(2), slab(1), slab(3)] * n,
            out_specs=[pl.BlockSpec((tm, cdim), lambda i, w_ref: (w_ref[1] * nb + i, 0))] * n),
        out_shape=[jax.ShapeDtypeStruct((2 * m, cdim), f32) for _ in qs],
        compiler_params=_params(1),
    )(where, *operands)


def _join_halves(fs, *, name):
    n = len(fs)

    def body(*refs):
        outs = refs[n:2 * n]
        send_sems, recv_sems = refs[2 * n:]
        x, y, c = _position()
        cps = []
        for k in range(n):
            m = fs[k].shape[0] // 2
            half = outs[k].at[pl.ds(pl.multiple_of(c * m, 8), m), :]
            cp = pltpu.make_async_remote_copy(src_ref=half, dst_ref=half, send_sem=send_sems.at[k],
                                              recv_sem=recv_sems.at[k], device_id=(x, y, 1 - c), device_id_type=MESH)
            cp.start()
            cps.append(cp)
        for cp in cps:
            cp.wait()

    return _pcall(
        body, name=name, in_specs=[HBM_SPEC] * n, out_specs=[HBM_SPEC] * n,
        out_shape=[jax.ShapeDtypeStruct(f.shape, f.dtype) for f in fs],
        input_output_aliases={k: k for k in range(n)},
        scratch_shapes=[pltpu.SemaphoreType.DMA((n,)), pltpu.SemaphoreType.DMA((n,))],
    )(*fs)


def _all_reduce_small(v, after=None, *, name):
    r = v.shape[0]
    extra = [] if after is None else [after]

    def body(v_ref, *refs):
        out_ref, buf, send_sems, recv_sems, local_sem = refs[len(extra):]
        x, y, c = _position()
        me, sibling = (x, y, c), (x, y, 1 - c)
        chips = _other_chips(x, y)

        def rows(px, py, pc):
            return buf.at[pl.ds(pl.multiple_of((4 * px + 2 * py + pc) * r, 8), r), :]

        def copy(k, block, to, src=None):
            return pltpu.make_async_remote_copy(
                src_ref=rows(*block) if src is None else src, dst_ref=rows(*block),
                send_sem=send_sems.at[k], recv_sem=recv_sems.at[k], device_id=to, device_id_type=MESH)

        mine = pltpu.make_async_copy(v_ref, rows(*me), local_sem)
        mine.start()
        first = [copy(0, me, sibling, src=v_ref)]
        first += [copy(1 + j, me, (*chip, c), src=v_ref) for j, chip in enumerate(chips)]
        for cp in first:
            cp.start()
        passed = [copy(4 + j, (*chip, c), sibling) for j, chip in enumerate(chips)]
        for j, chip in enumerate(chips):
            copy(1 + j, (*chip, c), me).wait_recv()
            passed[j].start()
        copy(0, sibling, me).wait_recv()
        for j, chip in enumerate(chips):
            copy(4 + j, (*chip, 1 - c), me).wait_recv()
        for cp in first + passed:
            cp.wait_send()
        mine.wait()
        acc = buf[0:r, :]
        for d in range(1, N_DEV):
            acc = acc + buf[d * r:(d + 1) * r, :]
        out_ref[...] = acc

    return _pcall(
        body, name=name, in_specs=[VMEM_SPEC] + [HBM_SPEC] * len(extra), out_specs=VMEM_SPEC,
        out_shape=jax.ShapeDtypeStruct((r, LANES), f32),
        scratch_shapes=[pltpu.VMEM((N_DEV * r, LANES), f32), pltpu.SemaphoreType.DMA((7,)),
                        pltpu.SemaphoreType.DMA((7,)), pltpu.SemaphoreType.DMA],
    )(v, *extra)


SEM_SPEC = pl.BlockSpec(memory_space=pltpu.SEMAPHORE)
HBM_ONLY = pl.BlockSpec(memory_space=pltpu.HBM)
EFFECT = pltpu.SideEffectType.DATAFLOW_SIDE_EFFECTING


def _sends(copies):
    return copies[0] if isinstance(copies, tuple) else copies


def _arrivals(copies):
    return copies[1] if isinstance(copies, tuple) else copies


def _split_start(bufs, copies_fn, n_sems, *, name):
    n = len(bufs)

    def body(*refs):
        send_sems, recv_sems = refs[n], refs[n + 1]
        thru = refs[n + 2:2 * n + 2]
        token = refs[2 * n + 2]
        for cp in _sends(copies_fn(thru, send_sems, recv_sems)):
            cp.start()
        token[...] = jnp.zeros_like(token)

    outs = _pcall(
        body, name=name,
        out_shape=(pltpu.SemaphoreType.DMA((n_sems,)), pltpu.SemaphoreType.DMA((n_sems,)),
                   *[pltpu.HBM(b.shape, b.dtype) for b in bufs], jax.ShapeDtypeStruct((8, LANES), f32)),
        in_specs=[HBM_ONLY] * n,
        out_specs=(SEM_SPEC, SEM_SPEC, *[HBM_ONLY] * n, VMEM_SPEC),
        input_output_aliases={k: 2 + k for k in range(n)},
        compiler_params=pltpu.CompilerParams(has_side_effects=EFFECT),
    )(*[pltpu.with_memory_space_constraint(b, pltpu.HBM) for b in bufs])
    return outs[0], outs[1], list(outs[2:2 + n]), outs[2 + n]


def _split_wait(thru, send_sems, recv_sems, after, copies_fn, *, name):
    n = len(thru)

    def body(*refs):
        copies = copies_fn(refs[:n], refs[n], refs[n + 1])
        for cp in _sends(copies):
            cp.wait_send()
        for cp in _arrivals(copies):
            cp.wait_recv()

    return list(_pcall(
        body, name=name,
        out_shape=tuple(pltpu.HBM(b.shape, b.dtype) for b in thru),
        in_specs=[HBM_ONLY] * n + [SEM_SPEC, SEM_SPEC] + [HBM_SPEC] * len(after),
        out_specs=tuple([HBM_ONLY] * n),
        input_output_aliases={k: k for k in range(n)},
        compiler_params=pltpu.CompilerParams(has_side_effects=EFFECT),
    )(*thru, send_sems, recv_sems, *after))


def _scatter_copies(n):
    def copies(bufs, send_sems, recv_sems):
        x, y, c = _position()
        me_chip = 2 * x + y
        cps = []
        for k in range(n):
            for j, (px, py) in enumerate(_other_chips(x, y)):
                cps.append(pltpu.make_async_remote_copy(
                    src_ref=bufs[k].at[2 * px + py], dst_ref=bufs[n + k].at[me_chip],
                    send_sem=send_sems.at[3 * k + j], recv_sem=recv_sems.at[3 * k + j],
                    device_id=(px, py, c), device_id_type=MESH))
        return cps
    return copies


N_PEERS = N_DEV - 1


def _direct_copies(n):
    def copies(bufs, send_sems, recv_sems):
        x, y, c = _position()
        me_chip = 2 * x + y
        sends, arrivals = [], []
        for k in range(n):
            m = bufs[k].shape[1] // 2
            land = bufs[n + k]

            def rows(slab, half, k=k, m=m):
                start = half * m if isinstance(half, int) else pl.multiple_of(half * m, 16)
                return bufs[k].at[slab, pl.ds(start, m), :]

            def copy(src, slot, send_idx, recv_idx, to, k=k, land=land):
                return pltpu.make_async_remote_copy(
                    src_ref=src, dst_ref=land.at[slot], send_sem=send_sems.at[N_PEERS * k + send_idx],
                    recv_sem=recv_sems.at[N_PEERS * k + recv_idx], device_id=to, device_id_type=MESH)

            sends.append(copy(rows(me_chip, 1 - c), 0, 0, 0, (x, y, 1 - c)))
            arrivals.append(copy(rows(me_chip, c), 0, 0, 0, (x, y, 1 - c)))
            for t, (px, py) in enumerate(_other_chips(x, y)):
                for core in range(2):
                    sends.append(copy(rows(2 * px + py, core), 1 + 2 * t + c, 1 + 2 * t + core, 1 + 2 * t + c,
                                      (px, py, core)))
                    arrivals.append(copy(rows(me_chip, c), 1 + 2 * t + core, 1 + 2 * t + core, 1 + 2 * t + core,
                                         (px, py, core)))
        return sends, arrivals
    return copies


def _sum_direct(gs, lands, *, name, tm=128):
    n = len(gs)
    _, m, cdim = lands[0].shape
    tm = min(tm, m)
    nb = m // tm
    assert m % tm == 0, (m, tm)
    where = jnp.stack([2 * lax.axis_index("x") + lax.axis_index("y"), lax.axis_index("c")]).astype(jnp.int32)

    def body(w_ref, *refs):
        for k in range(n):
            acc = refs[2 * k][...].astype(f32)
            for slot in range(N_PEERS):
                acc = acc + refs[2 * k + 1][slot].astype(f32)
            refs[2 * n + k][...] = acc

    own = pl.BlockSpec((None, tm, cdim), lambda i, w_ref: (w_ref[0], w_ref[1] * nb + i, 0))
    landed = pl.BlockSpec((N_PEERS, tm, cdim), lambda i, w_ref: (0, i, 0))
    operands = []
    for g, land in zip(gs, lands):
        operands += [g, land]
    return _pcall(
        body, name=name,
        grid_spec=pltpu.PrefetchScalarGridSpec(
            num_scalar_prefetch=1, grid=(nb,), in_specs=[own, landed] * n,
            out_specs=[pl.BlockSpec((tm, cdim), lambda i, w_ref: (w_ref[1] * nb + i, 0))] * n),
        out_shape=[jax.ShapeDtypeStruct((2 * m, cdim), f32) for _ in gs],
        compiler_params=_params(1),
    )(where, *operands)


def _broadcast_copies(bufs, send_sems, recv_sems):
    v, land = bufs
    x, y, c = _position()

    def copy(slot, send_idx, recv_idx, to):
        return pltpu.make_async_remote_copy(src_ref=v, dst_ref=land.at[slot], send_sem=send_sems.at[send_idx],
                                            recv_sem=recv_sems.at[recv_idx], device_id=to, device_id_type=MESH)

    sends = [copy(0, 0, 0, (x, y, 1 - c))]
    arrivals = [copy(0, 0, 0, (x, y, 1 - c))]
    for t, (px, py) in enumerate(_other_chips(x, y)):
        for core in range(2):
            sends.append(copy(1 + 2 * t + c, 1 + 2 * t + core, 1 + 2 * t + c, (px, py, core)))
            arrivals.append(copy(1 + 2 * t + core, 1 + 2 * t + core, 1 + 2 * t + core, (px, py, core)))
    return sends, arrivals


def _sum_in_device_order(v, land, *, name):
    r, cdim = v.shape
    x, y, c = _position()
    slots, mine = [], []
    for d in range(N_DEV):
        dx, dy, dc = d // 4, (d // 2) % 2, d % 2
        fx, fy = jnp.bitwise_xor(dx, x), jnp.bitwise_xor(dy, y)
        t = jnp.where(fx == 1, jnp.where(fy == 1, 2, 0), 1)
        slots.append(jnp.where(jnp.logical_and(fx == 0, fy == 0), 0, 1 + 2 * t + dc))
        mine.append(jnp.logical_and(jnp.logical_and(fx == 0, fy == 0), dc == c))
    table = jnp.stack(slots + mine).astype(jnp.int32)

    def body(tab_ref, v_ref, *refs):
        out_ref = refs[N_DEV]
        acc = None
        for d in range(N_DEV):
            term = jnp.where(tab_ref[N_DEV + d] == 1, v_ref[...], refs[d][...])
            acc = term if acc is None else acc + term
        out_ref[...] = acc

    whole = pl.BlockSpec((r, cdim), lambda i, tab_ref: (0, 0))
    landed = [pl.BlockSpec((None, r, cdim), functools.partial(lambda i, tab_ref, d: (tab_ref[d], 0, 0), d=d))
              for d in range(N_DEV)]
    return _pcall(
        body, name=name,
        grid_spec=pltpu.PrefetchScalarGridSpec(num_scalar_prefetch=1, grid=(1,), in_specs=[whole] + landed,
                                               out_specs=whole),
        out_shape=jax.ShapeDtypeStruct((r, cdim), f32),
        compiler_params=_params(1),
    )(table, v, *[land] * N_DEV)


def _block_rows(buf, px, py, pc):
    m = buf.shape[0] // N_DEV
    return buf.at[pl.ds(pl.multiple_of((4 * px + 2 * py + pc) * m, 16), m), :]


def _gather_ici_copies(n):
    def copies(bufs, send_sems, recv_sems):
        x, y, c = _position()
        cps = []
        for k in range(n):
            rows = _block_rows(bufs[k], x, y, c)
            targets = [(x, y, 1 - c)] + [(px, py, c) for px, py in _other_chips(x, y)]
            for j, to in enumerate(targets):
                cps.append(pltpu.make_async_remote_copy(
                    src_ref=rows, dst_ref=rows, send_sem=send_sems.at[4 * k + j], recv_sem=recv_sems.at[4 * k + j],
                    device_id=to, device_id_type=MESH))
        return cps
    return copies


def _gather_d2d_copies(n):
    def copies(bufs, send_sems, recv_sems):
        x, y, c = _position()
        cps = []
        for k in range(n):
            for j, (px, py) in enumerate(_other_chips(x, y)):
                rows = _block_rows(bufs[k], px, py, c)
                cps.append(pltpu.make_async_remote_copy(
                    src_ref=rows, dst_ref=rows, send_sem=send_sems.at[3 * k + j], recv_sem=recv_sems.at[3 * k + j],
                    device_id=(x, y, 1 - c), device_id_type=MESH))
        return cps
    return copies


def _cast_halves(shards, after, *, name):
    n = len(shards)
    where = jnp.stack([2 * lax.axis_index("x") + lax.axis_index("y"), lax.axis_index("c")]).astype(jnp.int32)

    def body(w_ref, *refs):
        for k in range(n):
            refs[n + 1 + k][...] = refs[k][...].astype(refs[n + 1 + k].dtype)

    def half(s):
        return (s.shape[0] // 2, s.shape[1])

    return _pcall(
        body, name=name,
        grid_spec=pltpu.PrefetchScalarGridSpec(
            num_scalar_prefetch=1, grid=(1,),
            in_specs=[pl.BlockSpec(half(s), lambda i, w_ref: (w_ref[1], 0)) for s in shards] + [HBM_SPEC],
            out_specs=[pl.BlockSpec(half(s), lambda i, w_ref: (2 * w_ref[0] + w_ref[1], 0)) for s in shards]),
        out_shape=[jax.ShapeDtypeStruct((N_SHARD * s.shape[0], s.shape[1]), MXU_DTYPE) for s in shards],
        compiler_params=_params(1),
    )(where, *shards, after)


class _SplitGather:
    def __init__(self, shards, after, tag):
        self.tag = tag
        self.n = len(shards)
        halves = _cast_halves(shards, after, name=f"{tag}_cast")
        self.ici = _split_start(halves, _gather_ici_copies(self.n), 4 * self.n, name=f"{tag}_ici_start")
        self.token = self.ici[3]

    def forward(self, after):
        send_sems, recv_sems, thru, _ = self.ici
        landed = _split_wait(thru, send_sems, recv_sems, after, _gather_ici_copies(self.n), name=f"{self.tag}_ici_wait")
        self.d2d = _split_start(landed, _gather_d2d_copies(self.n), 3 * self.n, name=f"{self.tag}_d2d_start")
        return self.d2d[3]

    def finish(self, after):
        send_sems, recv_sems, thru, _ = self.d2d
        return _split_wait(thru, send_sems, recv_sems, after, _gather_d2d_copies(self.n), name=f"{self.tag}_d2d_wait")


class _Overlap(_NoOverlap):
    def __init__(self, late_shards, ffn2_shards, after):
        self.late = _SplitGather(late_shards, after, "ag1")
        self.ffn2 = _SplitGather(ffn2_shards, self.late.token, "ag2")
        self.reduced = None
        self.ffn1_parts = []

    def start_token(self):
        return self.ffn2.token

    def late_weights(self, w, after):
        token = self.late.forward(after)
        f1d, w_in, wo = self.late.finish([token])
        w_in = w_in.reshape(N_SHARD, D_MODEL, IN_SHARD).transpose(1, 0, 2).reshape(D_MODEL, IN_COLS)
        return dict(f1d=f1d.reshape(N_SHARD, D_FF // N_SHARD, D_MODEL), wp=make_wp(w_in), wo=wo)

    def after_attention(self, after):
        return self.ffn2.forward(after)

    def ffn2_weights(self, w, after):
        full = self.ffn2.finish(after)
        fs = D_FF // N_SHARD
        return (full[0].reshape(N_SHARD, D_MODEL, fs), full[1].reshape(N_SHARD, D_MODEL, fs),
                full[2].reshape(N_SHARD, fs, D_MODEL))

    @staticmethod
    def _send_direct(grads, tag):
        lands = [lax.empty((N_PEERS, g.shape[1] // 2, g.shape[2]), g.dtype) for g in grads]
        return _split_start(list(grads) + lands, _direct_copies(len(grads)), N_PEERS * len(grads),
                            name=f"rs_direct_{tag}_start")

    def ffn2_grads(self, grads):
        self.scatter = self._send_direct(grads, "ffn2")
        return self.scatter[3]

    def ffn1_grads(self, grads):
        tag = "ffn1" + "ab"[len(self.ffn1_parts)]
        if not self.ffn1_parts:
            started = self._send_direct(grads, tag)
        else:
            recvs = _swap_halves(grads, name=f"rs_swap_{tag}")
            ps = list(_add_halves(grads[:2], recvs[:2], name=f"rs_add_{tag}_gu"))
            ps += list(_add_halves(grads[2:], recvs[2:], name=f"rs_add_{tag}_d"))
            lands = [lax.empty(p.shape, p.dtype) for p in ps]
            started = _split_start(ps + lands, _scatter_copies(3), 9, name=f"rs_scatter_{tag}_start")
        self.ffn1_parts.append((tag, started))
        return started[3]

    def ffn1_reduced(self, after):
        sums = []
        for direct, (tag, (send_sems, recv_sems, thru, _)) in zip((True, False), self.ffn1_parts):
            plan, add = (_direct_copies, _sum_direct) if direct else (_scatter_copies, _sum_slabs)
            done = _split_wait(thru, send_sems, recv_sems, after, plan(3), name=f"rs_{tag}_wait")
            sums += list(add(done[:2], done[3:5], name=f"rs_sum_{tag}_gu"))
            sums += list(add(done[2:3], done[5:], name=f"rs_sum_{tag}_d"))
        return sums

    def mixer_grads(self, dwp, dwo, small, loss):
        packed = jnp.concatenate([_pack_small(small), jnp.broadcast_to(loss, (8, LANES))], axis=0)
        land = lax.empty((N_PEERS,) + packed.shape, packed.dtype)
        self.small = _split_start([packed, land], _broadcast_copies, N_PEERS, name="ar_small_start")
        gwin = dwp[:, :IN_COLS].reshape(D_MODEL, N_SHARD, IN_SHARD).transpose(1, 0, 2).astype(GRAD_DTYPE)
        gwo = dwo.reshape(N_SHARD, D_MODEL // N_SHARD, D_MODEL).astype(GRAD_DTYPE)
        self.scatter_mix = self._send_direct([gwin, gwo], "mix")
        return self.small[3] + self.scatter_mix[3]

    def small_summed(self, after):
        send_sems, recv_sems, thru, _ = self.small
        packed, land = _split_wait(thru, send_sems, recv_sems, after, _broadcast_copies, name="ar_small_wait")
        summed = _sum_in_device_order(packed, land, name="ar_small_sum")
        return summed[:-8], summed[-8, 0]

    def mixer_reduced(self, after):
        send_sems, recv_sems, thru, _ = self.scatter_mix
        done = _split_wait(thru, send_sems, recv_sems, after, _direct_copies(2), name="rs_direct_mix_wait")
        return [_sum_direct([done[k]], [done[2 + k]], name=f"rs_sum_{tag}")[0] for k, tag in enumerate(["w_in", "w_out"])]

    def before_ffn1_bwd(self, after):
        send_sems, recv_sems, thru, _ = self.scatter
        n = len(thru) // 2
        done = _split_wait(thru, send_sems, recv_sems, after, _direct_copies(n), name="rs_direct_ffn2_wait")
        self.reduced = list(_sum_direct(done[:n], done[n:], name="rs_sum_ffn2"))


def _adamw(gs, ws, ms, vs, *, name, tm=256):
    n = len(gs)
    r, cdim = ws[0].shape[-2:]
    tm = r if tm is None else min(tm, r)
    assert r % tm == 0, (r, tm)
    nb = r // tm
    c1 = 1.0 / (1.0 - ADAM_B1 ** ADAM_STEP)
    c2 = 1.0 / (1.0 - ADAM_B2 ** ADAM_STEP)
    flat = pl.BlockSpec((tm, cdim), lambda i: (i, 0))

    g_ops, g_specs, g_where = [], [], []
    for g in gs:
        g_where.append(len(g_ops))
        if not isinstance(g, tuple):
            g_ops.append(g)
            g_specs.append(flat)
        elif g[2] == 1:
            g_ops += [g[0], g[1]]
            g_specs += [pl.BlockSpec((tm, cdim // 2), lambda i: (i, 0))] * 2
        else:
            g_ops += [g[0], g[1]]
            g_specs += [pl.BlockSpec((tm, cdim), lambda i: (jnp.minimum(i, nb // 2 - 1), 0)),
                        pl.BlockSpec((tm, cdim), lambda i: (jnp.maximum(i - nb // 2, 0), 0))]
    ng = len(g_ops)

    def gradient(refs, k):
        g, at = gs[k], g_where[k]
        if not isinstance(g, tuple):
            return refs[at][...]
        if g[2] == 1:
            return jnp.concatenate([refs[at][...], refs[at + 1][...]], axis=1)
        return jnp.where(pl.program_id(0) < nb // 2, refs[at][...], refs[at + 1][...])

    def body(*refs):
        rest = refs[ng:]
        for k in range(n):
            g = gradient(refs, k)
            w = rest[k][...]
            m = ADAM_B1 * rest[n + k][...] + (1.0 - ADAM_B1) * g
            v = ADAM_B2 * rest[2 * n + k][...] + (1.0 - ADAM_B2) * (g * g)
            rest[3 * n + k][...] = g
            rest[4 * n + k][...] = -ADAM_LR * ((m * c1) / (jnp.sqrt(v * c2) + ADAM_EPS) + ADAM_WD * w)
            rest[5 * n + k][...] = m
            rest[6 * n + k][...] = v

    like_w = flat if ws[0].ndim == 2 else pl.BlockSpec((None, tm, cdim), lambda i: (0, i, 0))
    outs = _pcall(
        body, name=name, grid=(nb,), in_specs=g_specs + [like_w] * (3 * n), out_specs=[like_w] * (4 * n),
        out_shape=[jax.ShapeDtypeStruct(ws[0].shape, f32)] * (4 * n),
        compiler_params=_params(1),
    )(*g_ops, *ws, *ms, *vs)
    return outs[:n], outs[n:2 * n], outs[2 * n:3 * n], outs[3 * n:]


BIG = ["ffn1_w_gate", "ffn1_w_up", "ffn1_w_down", "ffn2_w_gate", "ffn2_w_up", "ffn2_w_down"]
SMALL = ["ln1_g", "ln1_b", "b_forget", "conv_w", "conv_b", "rg_wa", "rg_ba", "rg_wx", "rg_bx", "lru_lambda",
         "ln2_g", "ln2_b", "ln3_g", "ln3_b"]
WEIGHTS = ["ffn1_w_gate", "ffn1_w_up", "ffn1_w_down", "ln1_g", "ln1_b", "w_in", "b_forget", "conv_w", "conv_b",
           "rg_wa", "rg_ba", "rg_wx", "rg_bx", "lru_lambda", "w_out", "ln2_g", "ln2_b",
           "ffn2_w_gate", "ffn2_w_up", "ffn2_w_down", "ln3_g", "ln3_b"]


def _pack_small(parts):
    rows = []
    for n in SMALL:
        flat = parts[n].reshape(-1)
        pad = (-flat.shape[0]) % LANES
        rows.append(jnp.pad(flat, (0, pad)).reshape(-1, LANES))
    packed = jnp.concatenate(rows, axis=0)
    return jnp.pad(packed, ((0, (-packed.shape[0]) % 8), (0, 0)))


def _unpack_small(packed, shapes):
    out, r0 = {}, 0
    for n in SMALL:
        size = math.prod(shapes[n])
        nr = -(-size // LANES)
        out[n] = packed[r0:r0 + nr].reshape(-1)[:size].reshape(shapes[n])
        r0 += nr
    return out


def kernel(x, ffn1_w_gate, ffn1_w_up, ffn1_w_down, ln1_g, ln1_b, w_in, b_forget, conv_w, conv_b, rg_wa, rg_ba, rg_wx, rg_bx, lru_lambda, w_out, ln2_g, ln2_b, ffn2_w_gate, ffn2_w_up, ffn2_w_down, ln3_g, ln3_b, loss_target, m_ffn1_w_gate, m_ffn1_w_up, m_ffn1_w_down, m_ln1_g, m_ln1_b, m_w_in, m_b_forget, m_conv_w, m_conv_b, m_rg_wa, m_rg_ba, m_rg_wx, m_rg_bx, m_lru_lambda, m_w_out, m_ln2_g, m_ln2_b, m_ffn2_w_gate, m_ffn2_w_up, m_ffn2_w_down, m_ln3_g, m_ln3_b, v_ffn1_w_gate, v_ffn1_w_up, v_ffn1_w_down, v_ln1_g, v_ln1_b, v_w_in, v_b_forget, v_conv_w, v_conv_b, v_rg_wa, v_rg_ba, v_rg_wx, v_rg_bx, v_lru_lambda, v_w_out, v_ln2_g, v_ln2_b, v_ffn2_w_gate, v_ffn2_w_up, v_ffn2_w_down, v_ln3_g, v_ln3_b):
    args = dict(locals())
    w = {n: args[n] for n in WEIGHTS}
    mom = {n: args["m_" + n] for n in WEIGHTS}
    var = {n: args["v_" + n] for n in WEIGHTS}
    chip = 2 * lax.axis_index("x") + lax.axis_index("y")

    g1 = _all_gather_bf16([w[n][0] for n in BIG[:2]], name="ag_ffn1_up")
    fs = D_FF // N_SHARD
    full = dict(
        f1g=g1[0].reshape(N_SHARD, D_MODEL, fs), f1u=g1[1].reshape(N_SHARD, D_MODEL, fs),
        bfp=jnp.pad(b_forget, ((0, 0), (0, LANES - HEADS))),
        ln1_g=ln1_g, ln1_b=ln1_b, ln2_g=ln2_g, ln2_b=ln2_b, ln3_g=ln3_g, ln3_b=ln3_b,
        conv_b=conv_b, rg_wa=rg_wa[0], rg_wx=rg_wx[0], rg_ba=rg_ba[0], rg_bx=rg_bx[0], lam=lru_lambda,
    )
    cw_place = lax.dynamic_update_slice(jnp.zeros((8, LRU_W), f32), conv_w[0] * 0.5, (0, chip * (LRU_W // N_SHARD)))
    cw_full = _all_reduce_small(cw_place.reshape(-1, LANES), g1[0], name="ag_conv_w")
    full["conv_w"] = cw_full.reshape(8, LRU_W)[:CONV_K]

    hooks = _Overlap([w["ffn1_w_down"][0], w["w_in"][0], w["w_out"][0]], [w[n][0] for n in BIG[3:]], cw_full)
    loss_rep, dx, g = _local_step(x[0], loss_target[0], full, hooks)

    token1 = hooks.ffn1_grads(g["f1"][1])
    red = _join_halves(hooks.reduced + hooks.mixer_reduced([token1]), name="rs_join_rest")
    grads = dict(zip(BIG[3:] + ["w_in", "w_out"], red))

    small_sum, loss = hooks.small_summed(red)
    small_shapes = {n: w[n].shape for n in SMALL}
    small_shapes["conv_w"] = (1, CONV_K, LRU_W)
    gs_red = _unpack_small(small_sum, small_shapes)
    gs_red["conv_w"] = lax.dynamic_slice(gs_red["conv_w"], (0, 0, chip * (LRU_W // N_SHARD)),
                                         (1, CONV_K, LRU_W // N_SHARD))
    grads.update(gs_red)

    delta, new_m, new_v = {}, {}, {}

    def adamw(names, name, **kw):
        g3, d, nm, nv = _adamw([grads[n] for n in names], [w[n] for n in names], [mom[n] for n in names],
                               [var[n] for n in names], name=name, **kw)
        for i, n in enumerate(names):
            grads[n], delta[n], new_m[n], new_v[n] = g3[i], d[i], nm[i], nv[i]

    adamw(BIG[3:], "adamw_ffn2", tm=128)
    adamw(["w_in"], "adamw_w_in")
    adamw(["w_out"], "adamw_w_out")
    shard_shapes = {n: w[n].shape for n in SMALL}
    _, d, nm, nv = _adamw([_pack_small({n: grads[n] for n in SMALL})], [_pack_small({n: w[n] for n in SMALL})],
                          [_pack_small({n: mom[n] for n in SMALL})], [_pack_small({n: var[n] for n in SMALL})],
                          name="adamw_small", tm=None)
    for dst, packed in ((delta, d[0]), (new_m, nm[0]), (new_v, nv[0])):
        dst.update(_unpack_small(packed, shard_shapes))

    worked = [new_v["ffn2_w_down"], new_v["w_in"], new_v["w_out"], nv[0]]
    ga, ua, da, gb, ub, db = _join_halves(hooks.ffn1_reduced(worked), name="rs_join_ffn1")
    grads.update(ffn1_w_gate=(ga, gb, 1), ffn1_w_up=(ua, ub, 1), ffn1_w_down=(da, db, 0))
    adamw(BIG[:3], "adamw_ffn1", tm=128)

    def shaped(tree, n):
        return tree[n].reshape(w[n].shape)

    return (loss, dx[None], *[shaped(grads, n) for n in WEIGHTS], *[shaped(delta, n) for n in WEIGHTS],
            *[shaped(new_m, n) for n in WEIGHTS], *[shaped(new_v, n) for n in WEIGHTS])
```

```python
import functools
import math

import jax
import jax.numpy as jnp
from jax import lax
from jax.experimental import pallas as pl
from jax.experimental.pallas import tpu as pltpu

f32 = jnp.float32
MXU_DTYPE = jnp.bfloat16
GRAD_DTYPE = jnp.bfloat16

D_MODEL = 1024
D_FF = 4096
N_SHARD = 4
N_DEV = 8
FOX_W = 512
LRU_W = 512
HEADS = 8
HEAD_DIM = 64
CONV_K = 4
IN_COLS = 2568
IN_SHARD = IN_COLS // N_SHARD
QKV_W = 3 * FOX_W
Z_PAD = 2688
LANES = 128
LN_EPS = 1e-5
DN_ALPHA = 2.0 ** 0.25
LRU_C = 8.0
NEG_BIG = -1e30
VMEM_LIMIT = 56 * 1024 * 1024

ADAM_LR = 0.001
ADAM_B1 = 0.9
ADAM_B2 = 0.999
ADAM_EPS = 1e-08
ADAM_WD = 0.01
ADAM_STEP = 10


def _pcall(body, **kw):
    return pl.pallas_call(body, **kw)


def _params(n_grid, vmem=VMEM_LIMIT):
    return pltpu.CompilerParams(dimension_semantics=("arbitrary",) * n_grid, vmem_limit_bytes=vmem)


def _dot(a, b):
    return jnp.dot(a, b, preferred_element_type=f32)


def _dot_nt(a, b):
    return lax.dot_general(a, b, (((1,), (1,)), ((), ())), preferred_element_type=f32)


def _dot_tn(a, b):
    return lax.dot_general(a, b, (((0,), (0,)), ((), ())), preferred_element_type=f32)


def _sigmoid(x):
    return 1.0 / (1.0 + jnp.exp(-x))


def _layer_norm_stats(y):
    mu = jnp.mean(y, axis=-1, keepdims=True)
    yc = y - mu
    var = jnp.mean(yc * yc, axis=-1, keepdims=True)
    rstd = lax.rsqrt(var + LN_EPS)
    return yc * rstd, rstd


def _ln_backward(dy, xhat, rstd, gamma):
    dxhat = dy * gamma
    m1 = jnp.mean(dxhat, axis=-1, keepdims=True)
    m2 = jnp.mean(dxhat * xhat, axis=-1, keepdims=True)
    dyp = rstd * (dxhat - m1 - xhat * m2)
    return dyp, jnp.sum(dy * xhat, axis=0, keepdims=True), jnp.sum(dy, axis=0, keepdims=True)


def _ffn_fwd_loss(x, wg, wu, wd, ln_g, ln_b, target, *, name, tm=1024, tf=512):
    T = x.shape[0]
    tm = min(tm, T)
    tr = min(256, tm)
    fs = D_FF // N_SHARD
    cpf = fs // tf
    nf = D_FF // tf
    nt = T // tm

    def body(x_ref, wg_ref, wu_ref, wd_ref, g_ref, b_ref, t_ref,
             xb_ref, gact_ref, uact_ref, dyp_ref, dgam_ref, dbeta_ref, loss_ref, acc_ref):
        i = pl.program_id(0)
        f = pl.program_id(1)

        @pl.when(jnp.logical_and(i == 0, f == 0))
        def _():
            dgam_ref[...] = jnp.zeros_like(dgam_ref)
            dbeta_ref[...] = jnp.zeros_like(dbeta_ref)
            loss_ref[...] = jnp.zeros_like(loss_ref)

        @pl.when(f == 0)
        def _():
            xb_ref[...] = x_ref[...].astype(MXU_DTYPE)
            acc_ref[...] = jnp.zeros_like(acc_ref)

        xb = xb_ref[...]
        g = _dot(xb, wg_ref[...])
        u = _dot(xb, wu_ref[...])
        h = (g * _sigmoid(g)) * u
        gact_ref[...] = g.astype(gact_ref.dtype)
        uact_ref[...] = u.astype(uact_ref.dtype)
        acc_ref[...] += _dot(h.astype(MXU_DTYPE), wd_ref[...])

        @pl.when(f == nf - 1)
        def _():
            gamma = g_ref[...]

            def rows_chunk(r, carry):
                rows = pl.ds(pl.multiple_of(r * tr, tr), tr)
                xhat, rstd = _layer_norm_stats(DN_ALPHA * x_ref[rows, :] + 0.5 * acc_ref[rows, :])
                err = xhat * gamma + b_ref[...] - t_ref[rows, :]
                sq = jnp.sum(jnp.sum(err * err, axis=0, keepdims=True), axis=1, keepdims=True)
                loss_ref[...] += jnp.broadcast_to(sq * (0.5 / D_MODEL), loss_ref.shape)
                dyp, dgam, dbeta = _ln_backward(err * (1.0 / D_MODEL), xhat, rstd, gamma)
                dyp_ref[rows, :] = dyp
                dgam_ref[...] += dgam
                dbeta_ref[...] += dbeta
                return carry

            lax.fori_loop(0, tm // tr, rows_chunk, 0)

    row = lambda i, f: (i, 0)
    const = lambda i, f: (0, 0)
    tile = pl.BlockSpec((tm, tf), lambda i, f: (i, f))
    cols = pl.BlockSpec((None, D_MODEL, tf), lambda i, f: (f // cpf, 0, f % cpf))
    last = lambda i, f: (jnp.where(f == nf - 1, i, jnp.maximum(i - 1, 0)), 0)
    return _pcall(
        body, name=name, grid=(nt, nf),
        in_specs=[pl.BlockSpec((tm, D_MODEL), row), cols, cols,
                  pl.BlockSpec((None, tf, D_MODEL), lambda i, f: (f // cpf, f % cpf, 0)),
                  pl.BlockSpec((1, D_MODEL), const), pl.BlockSpec((1, D_MODEL), const),
                  pl.BlockSpec((tm, D_MODEL), last)],
        out_specs=[pl.BlockSpec((tm, D_MODEL), row), tile, tile, pl.BlockSpec((tm, D_MODEL), row),
                   pl.BlockSpec((1, D_MODEL), const), pl.BlockSpec((1, D_MODEL), const), pl.BlockSpec((1, LANES), const)],
        out_shape=[jax.ShapeDtypeStruct((T, D_MODEL), MXU_DTYPE), jax.ShapeDtypeStruct((T, D_FF), MXU_DTYPE),
                   jax.ShapeDtypeStruct((T, D_FF), MXU_DTYPE), jax.ShapeDtypeStruct((T, D_MODEL), f32),
                   jax.ShapeDtypeStruct((1, D_MODEL), f32), jax.ShapeDtypeStruct((1, D_MODEL), f32),
                   jax.ShapeDtypeStruct((1, LANES), f32)],
        scratch_shapes=[pltpu.VMEM((tm, D_MODEL), f32)],
        compiler_params=_params(2),
    )(x, wg, wu, wd, ln_g, ln_b, target)


def _ffn_up(x, wg, wu, after=None, *, name, tm=1024, tf=512):
    T = x.shape[0]
    tm = min(tm, T)
    cpf = (D_FF // N_SHARD) // tf
    nf = D_FF // tf
    extra = [] if after is None else [after]

    def body(x_ref, wg_ref, wu_ref, *refs):
        xb_ref, gact_ref, uact_ref, hact_ref = refs[len(extra):]

        @pl.when(pl.program_id(1) == 0)
        def _():
            xb_ref[...] = x_ref[...].astype(MXU_DTYPE)

        xb = xb_ref[...]
        g = _dot(xb, wg_ref[...])
        u = _dot(xb, wu_ref[...])
        gact_ref[...] = g.astype(gact_ref.dtype)
        uact_ref[...] = u.astype(uact_ref.dtype)
        hact_ref[...] = ((g * _sigmoid(g)) * u).astype(hact_ref.dtype)

    row = lambda i, f: (i, 0)
    tile = pl.BlockSpec((tm, tf), lambda i, f: (i, f))
    cols = pl.BlockSpec((None, D_MODEL, tf), lambda i, f: (f // cpf, 0, f % cpf))
    return _pcall(
        body, name=name, grid=(T // tm, nf),
        in_specs=[pl.BlockSpec((tm, D_MODEL), row), cols, cols] + [pl.BlockSpec(memory_space=pl.ANY)] * len(extra),
        out_specs=[pl.BlockSpec((tm, D_MODEL), row), tile, tile, tile],
        out_shape=[jax.ShapeDtypeStruct((T, D_MODEL), MXU_DTYPE)] + [jax.ShapeDtypeStruct((T, D_FF), MXU_DTYPE)] * 3,
        compiler_params=_params(2),
    )(x, wg, wu, *extra)


def _ffn_down_ln(x, hact, wd, ln_g, ln_b, *, name, tm=1024):
    T = x.shape[0]
    tm = min(tm, T)
    fs = D_FF // N_SHARD
    ks = 2
    nk = N_SHARD // ks

    def body(x_ref, h_ref, wd_ref, g_ref, b_ref, xhat_ref, xn_ref, rstd_ref, acc_ref):
        k = pl.program_id(1)

        @pl.when(k == 0)
        def _():
            acc_ref[...] = jnp.zeros_like(acc_ref)

        acc_ref[...] += _dot(h_ref[...], wd_ref[...].reshape(ks * fs, D_MODEL))

        @pl.when(k == nk - 1)
        def _():
            xhat, rstd = _layer_norm_stats(DN_ALPHA * x_ref[...] + 0.5 * acc_ref[...])
            xhat_ref[...] = xhat
            xn_ref[...] = (xhat * g_ref[...] + b_ref[...]).astype(xn_ref.dtype)
            rstd_ref[...] = jnp.broadcast_to(rstd, rstd_ref.shape)

    row = lambda i, k: (i, 0)
    vec = pl.BlockSpec((1, D_MODEL), lambda i, k: (0, 0))
    return _pcall(
        body, name=name, grid=(T // tm, nk),
        in_specs=[pl.BlockSpec((tm, D_MODEL), row), pl.BlockSpec((tm, ks * fs), lambda i, k: (i, k)),
                  pl.BlockSpec((ks, fs, D_MODEL), lambda i, k: (k, 0, 0)), vec, vec],
        out_specs=[pl.BlockSpec((tm, D_MODEL), row), pl.BlockSpec((tm, D_MODEL), row), pl.BlockSpec((tm, LANES), row)],
        out_shape=[jax.ShapeDtypeStruct((T, D_MODEL), f32), jax.ShapeDtypeStruct((T, D_MODEL), MXU_DTYPE),
                   jax.ShapeDtypeStruct((T, LANES), f32)],
        scratch_shapes=[pltpu.VMEM((tm, D_MODEL), f32)],
        compiler_params=_params(2),
    )(x, hact, wd, ln_g, ln_b)


def _ffn_bwd(dyp, xb, gact, uact, wg, wu, wd, after=None, *, name, tm=512, tf=512, part=None, dx_init=None):
    T = dyp.shape[0]
    tm = min(tm, T)
    fs = D_FF // N_SHARD
    cpf = fs // tf
    nt = T // tm
    nf = D_FF // tf if part is None else N_SHARD
    wf = fs if part is None else tf
    slab = (lambda f: f // cpf) if part is None else (lambda f: f)
    chunk = (lambda f: f % cpf) if part is None else (lambda f: part)
    extra = ([] if dx_init is None else [dx_init]) + ([] if after is None else [after])

    def body(dyp_ref, xb_ref, g_ref, u_ref, wg_ref, wu_ref, wd_ref, *refs):
        dx_hbm, dwg_ref, dwu_ref, dwd_ref, dx_sc, dwg_sc, dwu_sc, dwd_sc, sem = refs[len(extra):]
        f = pl.program_id(0)
        i = pl.program_id(1)
        rows = pl.ds(pl.multiple_of(i * tm, tm), tm)
        dyp_t = dyp_ref[...]
        dy = (0.5 * dyp_t).astype(MXU_DTYPE)

        @pl.when(i == 0)
        def _():
            dwg_sc[...] = jnp.zeros_like(dwg_sc)
            dwu_sc[...] = jnp.zeros_like(dwu_sc)
            dwd_sc[...] = jnp.zeros_like(dwd_sc)

        @pl.when(f == 0)
        def _():
            dx_sc[rows, :] = DN_ALPHA * dyp_t if dx_init is None else refs[0][...]

        g = g_ref[...].astype(f32)
        u = u_ref[...].astype(f32)
        sig = _sigmoid(g)
        silu = g * sig
        dh = _dot_nt(dy, wd_ref[...])
        dg = (dh * u * (sig * (1.0 + g * (1.0 - sig)))).astype(MXU_DTYPE)
        du = (dh * silu).astype(MXU_DTYPE)
        hb = (silu * u).astype(MXU_DTYPE)
        dx_sc[rows, :] += _dot_nt(dg, wg_ref[...]) + _dot_nt(du, wu_ref[...])
        xb_t = xb_ref[...]
        dwg_sc[...] += _dot_tn(xb_t, dg)
        dwu_sc[...] += _dot_tn(xb_t, du)
        dwd_sc[...] += _dot_tn(hb, dy)

        @pl.when(i == nt - 1)
        def _():
            dwg_ref[...] = dwg_sc[...].astype(dwg_ref.dtype)
            dwu_ref[...] = dwu_sc[...].astype(dwu_ref.dtype)
            dwd_ref[...] = dwd_sc[...].astype(dwd_ref.dtype)

        @pl.when(jnp.logical_and(f == nf - 1, i == nt - 1))
        def _():
            cp = pltpu.make_async_copy(dx_sc, dx_hbm, sem)
            cp.start()
            cp.wait()

    row = lambda f, i: (i, 0)
    return _pcall(
        body, name=name, grid=(nf, nt),
        in_specs=[
            pl.BlockSpec((tm, D_MODEL), row),
            pl.BlockSpec((tm, D_MODEL), row),
            pl.BlockSpec((tm, tf), lambda f, i: (i, slab(f) * cpf + chunk(f))),
            pl.BlockSpec((tm, tf), lambda f, i: (i, slab(f) * cpf + chunk(f))),
            pl.BlockSpec((None, D_MODEL, tf), lambda f, i: (slab(f), 0, chunk(f))),
            pl.BlockSpec((None, D_MODEL, tf), lambda f, i: (slab(f), 0, chunk(f))),
            pl.BlockSpec((None, tf, D_MODEL), lambda f, i: (slab(f), chunk(f), 0)),
        ] + ([] if dx_init is None else [pl.BlockSpec((tm, D_MODEL), row)])
        + ([] if after is None else [pl.BlockSpec(memory_space=pl.ANY)]),
        out_specs=[
            pl.BlockSpec(memory_space=pl.ANY),
            pl.BlockSpec((None, D_MODEL, tf), lambda f, i: (slab(f), 0, chunk(f) if part is None else 0)),
            pl.BlockSpec((None, D_MODEL, tf), lambda f, i: (slab(f), 0, chunk(f) if part is None else 0)),
            pl.BlockSpec((None, tf, D_MODEL), lambda f, i: (slab(f), chunk(f) if part is None else 0, 0)),
        ],
        out_shape=[
            jax.ShapeDtypeStruct((T, D_MODEL), f32),
            jax.ShapeDtypeStruct((N_SHARD, D_MODEL, wf), GRAD_DTYPE),
            jax.ShapeDtypeStruct((N_SHARD, D_MODEL, wf), GRAD_DTYPE),
            jax.ShapeDtypeStruct((N_SHARD, wf, D_MODEL), GRAD_DTYPE),
        ],
        scratch_shapes=[pltpu.VMEM((T, D_MODEL), f32), pltpu.VMEM((D_MODEL, tf), f32),
                        pltpu.VMEM((D_MODEL, tf), f32), pltpu.VMEM((tf, D_MODEL), f32),
                        pltpu.SemaphoreType.DMA],
        compiler_params=_params(2),
    )(dyp, xb, gact, uact, wg, wu, wd, *extra)


def _proj_in(xn, wp, bfp, *, name, tm=512):
    T = xn.shape[0]
    tm = min(tm, T)
    nt = T // tm

    def body(x_ref, w_ref, b_ref, lxg_ref, fg_ref, qa_ref, ka_ref, va_ref, carry):
        i = pl.program_id(0)

        @pl.when(i == 0)
        def _():
            carry[...] = jnp.zeros_like(carry)

        z = _dot(x_ref[...], w_ref[...])
        lxg_ref[...] = z[:, QKV_W:QKV_W + 2 * LRU_W]
        fg = z[:, QKV_W + 2 * LRU_W:] + b_ref[...]
        fg_ref[...] = fg
        ls = jnp.minimum(fg, 0.0) - jnp.log(1.0 + jnp.exp(-jnp.abs(fg)))
        r = lax.broadcasted_iota(jnp.int32, (tm, tm), 0)
        c = lax.broadcasted_iota(jnp.int32, (tm, tm), 1)
        cum = _tri_dot(jnp.where(r >= c, 1.0, 0.0).astype(jnp.bfloat16), ls) + carry[0:1, :]
        carry[...] = jnp.broadcast_to(cum[tm - 1:tm, :], carry.shape)

        lane = lax.broadcasted_iota(jnp.int32, (tm, LANES), 1)
        low = lane < HEAD_DIM
        ones_q = jnp.where(jnp.logical_and(lane >= AUX + 3, lane < AUX + 6), 1.0, 0.0)
        ones_k = jnp.where(jnp.logical_and(lane >= AUX, lane < AUX + 3), 1.0, 0.0)
        for j in range(HEADS // 2):
            pair = [z[:, t * FOX_W + j * LANES:t * FOX_W + (j + 1) * LANES] for t in range(3)]
            for odd in range(2):
                h = 2 * j + odd
                q, k, v = [_swap_lane_halves(a) if odd else a for a in pair]
                hi, mid, lo = [a.astype(f32) for a in _split3(jnp.broadcast_to(cum[:, h:h + 1], (tm, LANES)))]
                aux_q = jnp.where(lane == AUX, hi, jnp.where(lane == AUX + 1, mid, jnp.where(lane == AUX + 2, lo, ones_q)))
                aux_k = jnp.where(lane == AUX + 3, -hi,
                                  jnp.where(lane == AUX + 4, -mid, jnp.where(lane == AUX + 5, -lo, ones_k)))
                blk = slice(h * LANES, (h + 1) * LANES)
                qa_ref[:, blk] = jnp.where(low, q, aux_q).astype(qa_ref.dtype)
                ka_ref[:, blk] = jnp.where(low, k, aux_k).astype(ka_ref.dtype)
                va_ref[:, blk] = jnp.where(low, v, 1.0).astype(va_ref.dtype)

    row = lambda i: (i, 0)
    const = lambda i: (0, 0)
    return _pcall(
        body, name=name, grid=(nt,),
        in_specs=[pl.BlockSpec((tm, D_MODEL), row), pl.BlockSpec((D_MODEL, Z_PAD), const),
                  pl.BlockSpec((1, LANES), const)],
        out_specs=[pl.BlockSpec((tm, 2 * LRU_W), row), pl.BlockSpec((tm, LANES), row)]
        + [pl.BlockSpec((tm, HEADS * LANES), row)] * 3,
        out_shape=[jax.ShapeDtypeStruct((T, 2 * LRU_W), f32), jax.ShapeDtypeStruct((T, LANES), f32)]
        + [jax.ShapeDtypeStruct((T, HEADS * LANES), MXU_DTYPE)] * 3,
        scratch_shapes=[pltpu.VMEM((8, LANES), f32)],
        compiler_params=_params(1),
    )(xn, wp, bfp)


def _proj_in_bwd(dqa, dka, dva, dlxg, fgb, xn, dyp, wp, xhat, rstd, ln_g, *, name, tm=512):
    T = xn.shape[0]
    tm = min(tm, T)
    nt = T // tm

    def body(dq_ref, dk_ref, dv_ref, dl_ref, fg_ref, x_ref, dyp_ref, w_ref, xhat_ref, rstd_ref, g_ref,
             dpre_ref, dw_hbm, dgam_ref, dbeta_ref, dbf_ref, dw_sc, carry, sem):
        i = pl.program_id(0)

        @pl.when(i == 0)
        def _():
            dw_sc[...] = jnp.zeros_like(dw_sc)
            dgam_ref[...] = jnp.zeros_like(dgam_ref)
            dbeta_ref[...] = jnp.zeros_like(dbeta_ref)
            dbf_ref[...] = jnp.zeros_like(dbf_ref)
            carry[...] = jnp.zeros_like(carry)

        lane = lax.broadcasted_iota(jnp.int32, (tm, LANES), 1)
        dc = jnp.zeros((tm, LANES), f32)
        for h in range(HEADS):
            row_sum = dq_ref[:, h * LANES + AUX:h * LANES + AUX + 1]
            col_sum = dk_ref[:, h * LANES + AUX + 3:h * LANES + AUX + 4]
            dc = jnp.where(lane == h, jnp.broadcast_to(row_sum - col_sum, (tm, LANES)), dc)
        r = lax.broadcasted_iota(jnp.int32, (tm, tm), 0)
        c = lax.broadcasted_iota(jnp.int32, (tm, tm), 1)
        dls = _tri_dot(jnp.where(c >= r, 1.0, 0.0).astype(jnp.bfloat16), dc) + carry[0:1, :]
        carry[...] = jnp.broadcast_to(dls[0:1, :], carry.shape)
        dfg = dls * _sigmoid(-fg_ref[...])
        dbf_ref[...] += jnp.sum(dfg, axis=0, keepdims=True)

        low = _low_lanes((tm, LANES))

        def packed(ref):
            pairs = [jnp.where(low, ref[:, (2 * j) * LANES:(2 * j + 1) * LANES],
                               _swap_lane_halves(ref[:, (2 * j + 1) * LANES:(2 * j + 2) * LANES]))
                     for j in range(HEADS // 2)]
            return jnp.concatenate(pairs, axis=1).astype(MXU_DTYPE)

        dz = jnp.concatenate(
            [packed(dq_ref), packed(dk_ref), packed(dv_ref),
             dl_ref[...].astype(MXU_DTYPE), dfg.astype(MXU_DTYPE)], axis=1)
        dx = DN_ALPHA * dyp_ref[...] + _dot_nt(dz, w_ref[...])
        dpre, dgam, dbeta = _ln_backward(dx, xhat_ref[...], rstd_ref[:, 0:1], g_ref[...])
        dpre_ref[...] = dpre
        dgam_ref[...] += dgam
        dbeta_ref[...] += dbeta
        dw_sc[...] += _dot_tn(x_ref[...], dz)

        @pl.when(i == nt - 1)
        def _():
            dw_sc[:, :FOX_W] = dw_sc[:, :FOX_W] * (1.0 / math.sqrt(HEAD_DIM))
            cp = pltpu.make_async_copy(dw_sc, dw_hbm, sem)
            cp.start()
            cp.wait()

    row = lambda i: (nt - 1 - i, 0)
    const = lambda i: (0, 0)
    return _pcall(
        body, name=name, grid=(nt,),
        in_specs=[pl.BlockSpec((tm, HEADS * LANES), row), pl.BlockSpec((tm, HEADS * LANES), row),
                  pl.BlockSpec((tm, HEADS * LANES), row),
                  pl.BlockSpec((tm, 2 * LRU_W), row), pl.BlockSpec((tm, LANES), row),
                  pl.BlockSpec((tm, D_MODEL), row), pl.BlockSpec((tm, D_MODEL), row),
                  pl.BlockSpec((D_MODEL, Z_PAD), const),
                  pl.BlockSpec((tm, D_MODEL), row), pl.BlockSpec((tm, LANES), row), pl.BlockSpec((1, D_MODEL), const)],
        out_specs=[pl.BlockSpec((tm, D_MODEL), row), pl.BlockSpec(memory_space=pl.ANY),
                   pl.BlockSpec((1, D_MODEL), const), pl.BlockSpec((1, D_MODEL), const), pl.BlockSpec((1, LANES), const)],
        out_shape=[jax.ShapeDtypeStruct((T, D_MODEL), f32), jax.ShapeDtypeStruct((D_MODEL, Z_PAD), f32),
                   jax.ShapeDtypeStruct((1, D_MODEL), f32), jax.ShapeDtypeStruct((1, D_MODEL), f32),
                   jax.ShapeDtypeStruct((1, LANES), f32)],
        scratch_shapes=[pltpu.VMEM((D_MODEL, Z_PAD), f32), pltpu.VMEM((8, LANES), f32), pltpu.SemaphoreType.DMA],
        compiler_params=_params(1),
    )(dqa, dka, dva, dlxg, fgb, xn, dyp, wp, xhat, rstd, ln_g)


def _split3(x):
    hi = x.astype(jnp.bfloat16)
    r1 = x - hi.astype(f32)
    mid = r1.astype(jnp.bfloat16)
    lo = (r1 - mid.astype(f32)).astype(jnp.bfloat16)
    return hi, mid, lo


def _tri_dot(tri, x):
    hi, mid, lo = _split3(x)
    return _dot(tri, hi) + _dot(tri, mid) + _dot(tri, lo)


FOX_PAD = HEADS * LANES
AUX = HEAD_DIM


def _low_lanes(shape):
    return lax.broadcasted_iota(jnp.int32, shape, 1) < HEAD_DIM


def _swap_lane_halves(x):
    return pltpu.roll(x, HEAD_DIM, 1)


def _future_keys(tq, tk):
    r = lax.broadcasted_iota(jnp.int32, (tq, tk), 0)
    c = lax.broadcasted_iota(jnp.int32, (tq, tk), 1)
    return c > r


def _causal_steps(nq, key_major):
    if key_major:
        pairs = [(qi, ki) for ki in range(nq) for qi in range(ki, nq)]
    else:
        pairs = [(qi, ki) for qi in range(nq) for ki in range(qi + 1)]
    return (jnp.asarray([p[0] for p in pairs], jnp.int32), jnp.asarray([p[1] for p in pairs], jnp.int32))


def _fox_fwd(qa, ka, va, *, name, tq=512, hps=8):
    T = qa.shape[0]
    tq = min(tq, T)
    tk = tq
    nq = T // tq
    rep = tk // LANES
    qi_tab, ki_tab = _causal_steps(nq, key_major=False)

    def body(qi_ref, ki_ref, qa_ref, ka_ref, va_ref, o_ref, lse_ref, m_sc, acc_sc):
        t = pl.program_id(1)
        qi = qi_ref[t]
        ki = ki_ref[t]

        @pl.when(ki == 0)
        def _():
            m_sc[...] = jnp.full_like(m_sc, NEG_BIG)
            acc_sc[...] = jnp.zeros_like(acc_sc)

        def tile(diagonal):
            for h in range(hps):
                blk = slice(h * LANES, (h + 1) * LANES)
                s = _dot_nt(qa_ref[:, blk], ka_ref[:, blk])
                if diagonal:
                    s = jnp.where(_future_keys(tq, tk), NEG_BIG, s)
                m_prev = m_sc[h]
                m_new = jnp.maximum(m_prev, jnp.max(s, axis=1, keepdims=True))
                p = jnp.exp(s - jnp.tile(m_new, (1, rep)))
                acc_sc[h] = jnp.exp(m_prev - m_new) * acc_sc[h] + _dot(p.astype(MXU_DTYPE), va_ref[:, blk])
                m_sc[h] = m_new

        @pl.when(ki < qi)
        def _():
            tile(False)

        @pl.when(ki == qi)
        def _():
            tile(True)
            low = _low_lanes((tq, LANES))
            outs = []
            for h in range(hps):
                acc = acc_sc[h]
                den = _swap_lane_halves(acc)
                outs.append(acc / den)
                lse_ref[h] = m_sc[h] + jnp.log(jnp.where(low, den, acc))
            for p in range(hps // 2):
                o_ref[:, p * LANES:(p + 1) * LANES] = jnp.where(low, outs[2 * p], _swap_lane_halves(outs[2 * p + 1]))

    pair = hps * LANES
    return _pcall(
        body, name=name,
        grid_spec=pltpu.PrefetchScalarGridSpec(
            num_scalar_prefetch=2, grid=(HEADS // hps, qi_tab.shape[0]),
            in_specs=[
                pl.BlockSpec((tq, pair), lambda j, t, qi_ref, ki_ref: (qi_ref[t], j)),
                pl.BlockSpec((tk, pair), lambda j, t, qi_ref, ki_ref: (ki_ref[t], j)),
                pl.BlockSpec((tk, pair), lambda j, t, qi_ref, ki_ref: (ki_ref[t], j)),
            ],
            out_specs=[pl.BlockSpec((tq, pair // 2), lambda j, t, qi_ref, ki_ref: (qi_ref[t], j)),
                       pl.BlockSpec((hps, tq, LANES), lambda j, t, qi_ref, ki_ref: (j, qi_ref[t], 0))],
            scratch_shapes=[pltpu.VMEM((hps, tq, LANES), f32)] * 2),
        out_shape=[jax.ShapeDtypeStruct((T, FOX_W), f32), jax.ShapeDtypeStruct((HEADS, T, LANES), f32)],
        compiler_params=_params(2),
    )(qi_tab, ki_tab, qa, ka, va)


def _fox_bwd(qa, ka, va, doa, lse, drep, *, name, tq=512, hps=4):
    T = qa.shape[0]
    tq = min(tq, T)
    tk = tq
    nq = T // tq
    rep = tk // LANES
    qi_tab, ki_tab = _causal_steps(nq, key_major=True)

    def body(qi_ref, ki_ref, qa_ref, ka_ref, va_ref, doa_ref, lse_ref, d_ref, dqa_ref, dka_ref, dva_ref, dk_sc, dv_sc):
        t = pl.program_id(1)
        qi = qi_ref[t]
        ki = ki_ref[t]
        rows = pl.ds(pl.multiple_of(qi * tq, tq), tq)

        @pl.when(t == 0)
        def _():
            dqa_ref[...] = jnp.zeros_like(dqa_ref)

        @pl.when(qi == ki)
        def _():
            dk_sc[...] = jnp.zeros_like(dk_sc)
            dv_sc[...] = jnp.zeros_like(dv_sc)

        def tile(diagonal):
            for h in range(hps):
                blk = slice(h * LANES, (h + 1) * LANES)
                qh, kh, doh = qa_ref[:, blk], ka_ref[:, blk], doa_ref[:, blk]
                p = jnp.exp(_dot_nt(qh, kh) - jnp.tile(lse_ref[h], (1, rep)))
                if diagonal:
                    p = jnp.where(_future_keys(tq, tk), 0.0, p)
                dp = _dot_nt(doh, va_ref[:, blk])
                ds = (p * (dp - jnp.tile(d_ref[h], (1, rep)))).astype(MXU_DTYPE)
                dv_sc[h] += _dot_tn(p.astype(MXU_DTYPE), doh)
                dk_sc[h] += _dot_tn(ds, qh)
                dqa_ref[rows, blk] += _dot(ds, kh)

        @pl.when(qi > ki)
        def _():
            tile(False)

        @pl.when(qi == ki)
        def _():
            tile(True)

        @pl.when(qi == nq - 1)
        def _():
            for h in range(hps):
                blk = slice(h * LANES, (h + 1) * LANES)
                dka_ref[:, blk] = dk_sc[h]
                dva_ref[:, blk] = dv_sc[h]

    pair = hps * LANES
    q_blk = lambda j, t, qi_ref, ki_ref: (qi_ref[t], j)
    k_blk = lambda j, t, qi_ref, ki_ref: (ki_ref[t], j)
    stat = pl.BlockSpec((hps, tq, LANES), lambda j, t, qi_ref, ki_ref: (j, qi_ref[t], 0))
    return _pcall(
        body, name=name,
        grid_spec=pltpu.PrefetchScalarGridSpec(
            num_scalar_prefetch=2, grid=(HEADS // hps, qi_tab.shape[0]),
            in_specs=[pl.BlockSpec((tq, pair), q_blk), pl.BlockSpec((tk, pair), k_blk), pl.BlockSpec((tk, pair), k_blk),
                      pl.BlockSpec((tq, pair), q_blk), stat, stat],
            out_specs=[pl.BlockSpec((T, pair), lambda j, t, qi_ref, ki_ref: (0, j)),
                       pl.BlockSpec((tk, pair), k_blk), pl.BlockSpec((tk, pair), k_blk)],
            scratch_shapes=[pltpu.VMEM((hps, tk, LANES), f32)] * 2),
        out_shape=[jax.ShapeDtypeStruct((T, FOX_PAD), f32)] * 3,
        compiler_params=_params(2),
    )(qi_tab, ki_tab, qa, ka, va, doa, lse, drep)


GELU_C = math.sqrt(2.0 / math.pi)
GELU_A = 0.044715


def _gelu(x):
    t = jnp.tanh(GELU_C * (x + GELU_A * x * x * x))
    return 0.5 * x * (1.0 + t), t


def _gelu_grad(x, t):
    return 0.5 * (1.0 + t) + 0.5 * x * (1.0 - t * t) * GELU_C * (1.0 + 3.0 * GELU_A * x * x)


EXPM1_SERIES_BELOW = 0.25


def _expm1(x, e):
    series = x * (1.0 + x * (1 / 2 + x * (1 / 6 + x * (1 / 24 + x * (1 / 120 + x * (1 / 720))))))
    return jnp.where(x > -EXPM1_SERIES_BELOW, series, e - 1.0)


def _softplus_neg(lam):
    return jnp.maximum(-lam, 0.0) + jnp.log(1.0 + jnp.exp(-jnp.abs(lam)))


def _lru_gates(u, wab_ref, bab_ref, lam_ref):
    pre = _dot(u.astype(MXU_DTYPE), wab_ref[...]) + bab_ref[...]
    r = _sigmoid(pre[:, :LRU_W])
    gi = _sigmoid(pre[:, LRU_W:])
    sp = _softplus_neg(lam_ref[...])
    log_a = -LRU_C * r * sp
    a = jnp.exp(log_a)
    s = jnp.sqrt(-_expm1(2.0 * log_a, a * a))
    return r, gi, sp, a, s


def _lru_fwd(lxg, conv_w, conv_b, wab, bab, lam, *, name, tc=512):
    T = lxg.shape[0]
    tc = min(tc, T)
    nc = T // tc

    def body(lx_ref, lg_ref, cw_ref, cb_ref, wab_ref, bab_ref, lam_ref,
             out_ref, u_ref, hs_ref, gates_ref, ext, a_sc, b_sc, h_sc):
        i = pl.program_id(0)

        @pl.when(i == 0)
        def _():
            ext[0:8, :] = jnp.zeros((8, LRU_W), f32)
            h_sc[...] = jnp.zeros_like(h_sc)

        ext[8:, :] = lx_ref[...]
        u = cb_ref[...] + cw_ref[0:1, :] * ext[pl.ds(5, tc), :]
        for k in range(1, CONV_K):
            u = u + cw_ref[k:k + 1, :] * ext[pl.ds(5 + k, tc), :]
        ext[0:8, :] = ext[tc:tc + 8, :]
        u_ref[...] = u
        r, gi, sp, a, s = _lru_gates(u, wab_ref, bab_ref, lam_ref)
        for n, gate in enumerate((r, gi, a, s)):
            gates_ref[:, n * LRU_W:(n + 1) * LRU_W] = gate
        a_sc[...] = a
        b_sc[...] = s * (gi * u)

        def step(t, h):
            h = a_sc[pl.ds(t, 1), :] * h + b_sc[pl.ds(t, 1), :]
            hs_ref[pl.ds(t, 1), :] = h
            return h

        h = lax.fori_loop(0, tc, step, h_sc[0:1, :], unroll=8)
        h_sc[...] = jnp.broadcast_to(h, h_sc.shape)
        gel, _ = _gelu(lg_ref[...])
        out_ref[...] = gel * hs_ref[...]

    row = lambda i: (i, 0)
    const = lambda i: (0, 0)
    return _pcall(
        body, name=name, grid=(nc,),
        in_specs=[pl.BlockSpec((tc, LRU_W), row), pl.BlockSpec((tc, LRU_W), lambda i: (i, 1)),
                  pl.BlockSpec((CONV_K, LRU_W), const), pl.BlockSpec((1, LRU_W), const),
                  pl.BlockSpec((LRU_W, 2 * LRU_W), const), pl.BlockSpec((1, 2 * LRU_W), const),
                  pl.BlockSpec((1, LRU_W), const)],
        out_specs=[pl.BlockSpec((tc, LRU_W), row)] * 3 + [pl.BlockSpec((tc, 4 * LRU_W), row)],
        out_shape=[jax.ShapeDtypeStruct((T, LRU_W), f32)] * 3 + [jax.ShapeDtypeStruct((T, 4 * LRU_W), f32)],
        scratch_shapes=[pltpu.VMEM((tc + 8, LRU_W), f32), pltpu.VMEM((tc, LRU_W), f32),
                        pltpu.VMEM((tc, LRU_W), f32), pltpu.VMEM((8, LRU_W), f32)],
        compiler_params=_params(1),
    )(lxg, lxg, conv_w, conv_b, wab, bab, lam)


def _lru_bwd(dlru, lxg, u, hs, gates, conv_w, wab, lam, *, name, tc=512):
    T = lxg.shape[0]
    tc = min(tc, T)
    nc = T // tc
    bp = tc // 8

    def body(dl_ref, lx_ref, lxp_ref, lg_ref, u_ref, hs_ref, hsp_ref, gates_ref, cw_ref, wab_ref, lam_ref,
             dlxg_ref, dwab_ref, dbab_ref, dcw_ref, dcb_ref, dlam_ref,
             dh_sc, a_sc, ext, du_ext, carry):
        i = pl.program_id(0)
        first_chunk = i == nc - 1

        @pl.when(i == 0)
        def _():
            dwab_ref[...] = jnp.zeros_like(dwab_ref)
            dbab_ref[...] = jnp.zeros_like(dbab_ref)
            dcw_ref[...] = jnp.zeros_like(dcw_ref)
            dcb_ref[...] = jnp.zeros_like(dcb_ref)
            dlam_ref[...] = jnp.zeros_like(dlam_ref)
            carry[...] = jnp.zeros_like(carry)
            du_ext[tc:tc + 8, :] = jnp.zeros((8, LRU_W), f32)

        lg = lg_ref[...]
        gel, th = _gelu(lg)
        dl = dl_ref[...]
        hs = hs_ref[...]
        dlg = dl * hs * _gelu_grad(lg, th)
        u = u_ref[...]
        r, gi, a, s = [gates_ref[:, n * LRU_W:(n + 1) * LRU_W] for n in range(4)]
        sp = _softplus_neg(lam_ref[...])
        a_sc[...] = a
        dh_sc[...] = dl * gel

        def step(k, c):
            t = tc - 1 - k
            dh = dh_sc[pl.ds(t, 1), :] + c
            dh_sc[pl.ds(t, 1), :] = dh
            return a_sc[pl.ds(t, 1), :] * dh

        c = lax.fori_loop(0, tc, step, carry[0:1, :], unroll=8)
        carry[...] = jnp.broadcast_to(c, carry.shape)

        ext[0:8, :] = jnp.where(first_chunk, 0.0, hsp_ref[...])
        ext[8:, :] = hs
        hprev = ext[pl.ds(7, tc), :]
        dh = dh_sc[...]
        da = dh * hprev
        giu = gi * u
        dla = da * a - (dh * giu) * (a * a / s)
        dgi = dh * s * u
        du = dh * s * gi
        dr = dla * (-LRU_C * sp)
        dlam_ref[...] += jnp.sum(dla * (-LRU_C * r), axis=0, keepdims=True) * (-_sigmoid(-lam_ref[...]))
        dpre = jnp.concatenate([dr * r * (1.0 - r), dgi * gi * (1.0 - gi)], axis=1)
        dpre_b = dpre.astype(MXU_DTYPE)
        du = du + _dot_nt(dpre_b, wab_ref[...])
        dwab_ref[...] += _dot_tn(u.astype(MXU_DTYPE), dpre_b)
        dbab_ref[...] += jnp.sum(dpre, axis=0, keepdims=True)
        dcb_ref[...] += jnp.sum(du, axis=0, keepdims=True)

        du_ext[0:tc, :] = du
        dlx = cw_ref[0:1, :] * du_ext[pl.ds(3, tc), :]
        for k in range(1, CONV_K):
            dlx = dlx + cw_ref[k:k + 1, :] * du_ext[pl.ds(3 - k, tc), :]
        du_ext[tc:tc + 8, :] = du_ext[0:8, :]
        ext[0:8, :] = jnp.where(first_chunk, 0.0, lxp_ref[...])
        ext[8:, :] = lx_ref[...]
        for k in range(CONV_K):
            dcw_ref[k:k + 1, :] += jnp.sum(du * ext[pl.ds(5 + k, tc), :], axis=0, keepdims=True)
        dlxg_ref[:, :LRU_W] = dlx.astype(dlxg_ref.dtype)
        dlxg_ref[:, LRU_W:] = dlg.astype(dlxg_ref.dtype)

    rev = lambda i: (nc - 1 - i, 0)
    prev8 = lambda i: (jnp.maximum((nc - 1 - i) * bp - 1, 0), 0)
    const = lambda i: (0, 0)
    return _pcall(
        body, name=name, grid=(nc,),
        in_specs=[
            pl.BlockSpec((tc, LRU_W), rev),
            pl.BlockSpec((tc, LRU_W), rev),
            pl.BlockSpec((8, LRU_W), prev8),
            pl.BlockSpec((tc, LRU_W), lambda i: (nc - 1 - i, 1)),
            pl.BlockSpec((tc, LRU_W), rev),
            pl.BlockSpec((tc, LRU_W), rev),
            pl.BlockSpec((8, LRU_W), prev8),
            pl.BlockSpec((tc, 4 * LRU_W), rev),
            pl.BlockSpec((CONV_K, LRU_W), const),
            pl.BlockSpec((LRU_W, 2 * LRU_W), const),
            pl.BlockSpec((1, LRU_W), const),
        ],
        out_specs=[
            pl.BlockSpec((tc, 2 * LRU_W), rev),
            pl.BlockSpec((LRU_W, 2 * LRU_W), const),
            pl.BlockSpec((1, 2 * LRU_W), const),
            pl.BlockSpec((8, LRU_W), const),
            pl.BlockSpec((1, LRU_W), const),
            pl.BlockSpec((1, LRU_W), const),
        ],
        out_shape=[
            jax.ShapeDtypeStruct((T, 2 * LRU_W), MXU_DTYPE),
            jax.ShapeDtypeStruct((LRU_W, 2 * LRU_W), f32),
            jax.ShapeDtypeStruct((1, 2 * LRU_W), f32),
            jax.ShapeDtypeStruct((8, LRU_W), f32),
            jax.ShapeDtypeStruct((1, LRU_W), f32),
            jax.ShapeDtypeStruct((1, LRU_W), f32),
        ],
        scratch_shapes=[pltpu.VMEM((tc, LRU_W), f32), pltpu.VMEM((tc, LRU_W), f32),
                        pltpu.VMEM((tc + 8, LRU_W), f32), pltpu.VMEM((tc + 8, LRU_W), f32),
                        pltpu.VMEM((8, LRU_W), f32)],
        compiler_params=_params(1),
    )(dlru, lxg, lxg, lxg, u, hs, hs, gates, conv_w, wab, lam)


def _mix_out(fox, lru, wo, xhat1, g1, b1, g2, b2, *, name, tm=512):
    T = fox.shape[0]
    tm = min(tm, T)
    nt = T // tm

    def body(fox_ref, lru_ref, wo_ref, xh_ref, g1_ref, b1_ref, g2_ref, b2_ref, xhat_ref, xn_ref, rstd_ref):
        mix = _dot(fox_ref[...].astype(MXU_DTYPE), wo_ref[:FOX_W, :])
        mix = mix + _dot(lru_ref[...].astype(MXU_DTYPE), wo_ref[FOX_W:, :])
        x1 = xh_ref[...] * g1_ref[...] + b1_ref[...]
        xhat, rstd = _layer_norm_stats(DN_ALPHA * x1 + mix)
        xhat_ref[...] = xhat
        xn_ref[...] = xhat * g2_ref[...] + b2_ref[...]
        rstd_ref[...] = jnp.broadcast_to(rstd, rstd_ref.shape)

    row = lambda i: (i, 0)
    const = lambda i: (0, 0)
    vec = pl.BlockSpec((1, D_MODEL), const)
    return _pcall(
        body, name=name, grid=(nt,),
        in_specs=[pl.BlockSpec((tm, FOX_W), row), pl.BlockSpec((tm, LRU_W), row),
                  pl.BlockSpec((D_MODEL, D_MODEL), const), pl.BlockSpec((tm, D_MODEL), row), vec, vec, vec, vec],
        out_specs=[pl.BlockSpec((tm, D_MODEL), row), pl.BlockSpec((tm, D_MODEL), row),
                   pl.BlockSpec((tm, LANES), row)],
        out_shape=[jax.ShapeDtypeStruct((T, D_MODEL), f32), jax.ShapeDtypeStruct((T, D_MODEL), f32),
                   jax.ShapeDtypeStruct((T, LANES), f32)],
        compiler_params=_params(1),
    )(fox, lru, wo, xhat1, g1, b1, g2, b2)


def _mix_out_bwd(dy, xhat, rstd, ln_g, fox, lru, wo, *, name, tm=512):
    T = fox.shape[0]
    tm = min(tm, T)
    nt = T // tm

    def body(dy_ref, xhat_ref, rstd_ref, g_ref, fox_ref, lru_ref, wo_ref,
             dyp_ref, dgam_ref, dbeta_ref, dlru_ref, dwo_ref, d_ref, doa_ref):
        i = pl.program_id(0)

        @pl.when(i == 0)
        def _():
            dwo_ref[...] = jnp.zeros_like(dwo_ref)
            dgam_ref[...] = jnp.zeros_like(dgam_ref)
            dbeta_ref[...] = jnp.zeros_like(dbeta_ref)

        dyp, dgam, dbeta = _ln_backward(dy_ref[...], xhat_ref[...], rstd_ref[:, 0:1], g_ref[...])
        dyp_ref[...] = dyp
        dgam_ref[...] += dgam
        dbeta_ref[...] += dbeta
        dmix = dyp.astype(MXU_DTYPE)
        dcat = _dot_nt(dmix, wo_ref[...])
        dlru_ref[...] = dcat[:, FOX_W:]
        low = _low_lanes((tm, LANES))
        for j in range(HEADS // 2):
            do2 = dcat[:, j * LANES:(j + 1) * LANES].astype(MXU_DTYPE).astype(f32)
            prod = do2 * fox_ref[:, j * LANES:(j + 1) * LANES]
            for odd in range(2):
                h = 2 * j + odd
                mine = jnp.where(low, _swap_lane_halves(prod) if odd else prod, 0.0)
                d_ref[h] = jnp.broadcast_to(jnp.sum(mine, axis=1, keepdims=True), (tm, LANES))
                doh = jnp.where(low, _swap_lane_halves(do2) if odd else do2, 0.0)
                doa_ref[:, h * LANES:(h + 1) * LANES] = doh.astype(doa_ref.dtype)
        dwo_ref[:FOX_W, :] += _dot_tn(fox_ref[...].astype(MXU_DTYPE), dmix)
        dwo_ref[FOX_W:, :] += _dot_tn(lru_ref[...].astype(MXU_DTYPE), dmix)

    row = lambda i: (i, 0)
    const = lambda i: (0, 0)
    return _pcall(
        body, name=name, grid=(nt,),
        in_specs=[pl.BlockSpec((tm, D_MODEL), row), pl.BlockSpec((tm, D_MODEL), row), pl.BlockSpec((tm, LANES), row),
                  pl.BlockSpec((1, D_MODEL), const),
                  pl.BlockSpec((tm, FOX_W), row), pl.BlockSpec((tm, LRU_W), row),
                  pl.BlockSpec((D_MODEL, D_MODEL), const)],
        out_specs=[pl.BlockSpec((tm, D_MODEL), row), pl.BlockSpec((1, D_MODEL), const), pl.BlockSpec((1, D_MODEL), const),
                   pl.BlockSpec((tm, LRU_W), row), pl.BlockSpec((D_MODEL, D_MODEL), const),
                   pl.BlockSpec((HEADS, tm, LANES), lambda i: (0, i, 0)), pl.BlockSpec((tm, HEADS * LANES), row)],
        out_shape=[jax.ShapeDtypeStruct((T, D_MODEL), f32), jax.ShapeDtypeStruct((1, D_MODEL), f32),
                   jax.ShapeDtypeStruct((1, D_MODEL), f32),
                   jax.ShapeDtypeStruct((T, LRU_W), f32), jax.ShapeDtypeStruct((D_MODEL, D_MODEL), f32),
                   jax.ShapeDtypeStruct((HEADS, T, LANES), f32), jax.ShapeDtypeStruct((T, HEADS * LANES), MXU_DTYPE)],
        compiler_params=_params(1),
    )(dy, xhat, rstd, ln_g, fox, lru, wo)


def make_wp(w_in):
    scale = jnp.concatenate([jnp.full((FOX_W,), 1.0 / math.sqrt(HEAD_DIM), w_in.dtype),
                             jnp.ones((IN_COLS - FOX_W,), w_in.dtype)])
    return jnp.pad(w_in * scale[None, :], ((0, 0), (0, Z_PAD - IN_COLS)))


def _block_diag(w):
    eye = jnp.eye(HEADS, dtype=w.dtype)
    return jnp.einsum("hij,hg->higj", w, eye).reshape(LRU_W, LRU_W)


def _block_diag_extract(m):
    m4 = m.reshape(HEADS, HEAD_DIM, HEADS, HEAD_DIM)
    return jnp.stack([m4[h, :, h, :] for h in range(HEADS)])


class _NoOverlap:
    def start_token(self):
        return None

    def late_weights(self, w, after):
        return dict(f1d=w["f1d"], wp=w["wp"], wo=w["wo"])

    def after_attention(self, after):
        return None

    def ffn2_weights(self, w, after):
        return w["f2g"], w["f2u"], w["f2d"]

    def ffn2_grads(self, grads):
        return None

    def ffn1_grads(self, grads):
        return None

    def mixer_grads(self, dwp, dwo, small, loss):
        return None

    def before_ffn1_bwd(self, after):
        return None


def _tied(a, token):
    return a if token is None else a + token[0, 0]


def _local_step(x, target, w, hooks=None):
    hooks = hooks or _NoOverlap()
    bfp = w["bfp"]
    wab = jnp.concatenate([_block_diag(w["rg_wa"]), _block_diag(w["rg_wx"])], axis=1).astype(MXU_DTYPE)
    bab = jnp.concatenate([w["rg_ba"].reshape(1, LRU_W), w["rg_bx"].reshape(1, LRU_W)], axis=1)

    xb0, g1a, u1a, h1a = _ffn_up(x, w["f1g"], w["f1u"], hooks.start_token(), name="ffn1_up")
    late = hooks.late_weights(w, [h1a])
    f1d, wp, wo = late["f1d"], late["wp"], late["wo"]
    xhat1, xn1, rstd1 = _ffn_down_ln(x, h1a, f1d, w["ln1_g"], w["ln1_b"], name="ffn1_down")
    lxg, fgb, qa, ka, va = _proj_in(xn1, wp, bfp, name="proj_in")
    fox, lse = _fox_fwd(qa, ka, va, name="fox_fwd")
    token = hooks.after_attention([lse])
    lru, uconv, hs, gates = _lru_fwd(lxg, w["conv_w"], _tied(w["conv_b"], token), wab, bab, w["lam"], name="lru_fwd")
    xhat2, x2, rstd2 = _mix_out(fox, lru, wo, xhat1, w["ln1_g"], w["ln1_b"], w["ln2_g"], w["ln2_b"], name="mix_out")
    f2g, f2u, f2d = hooks.ffn2_weights(w, [rstd2])
    xb2, g2a, u2a, dy3p, dln3g, dln3b, loss = _ffn_fwd_loss(x2, f2g, f2u, f2d, w["ln3_g"], w["ln3_b"], target,
                                                            name="ffn2_fwd_loss")

    dx2, df2g, df2u, df2d = _ffn_bwd(dy3p, xb2, g2a, u2a, f2g, f2u, f2d, name="ffn2_bwd")
    token = hooks.ffn2_grads([df2g, df2u, df2d])
    dy2p, dln2g, dln2b, dlru, dwo, drep, doa = _mix_out_bwd(dx2, xhat2, rstd2, _tied(w["ln2_g"], token), fox, lru, wo,
                                                            name="mix_out_bwd")
    dlxg, dwab, dbab, dcw, dcb, dlam = _lru_bwd(dlru, lxg, uconv, hs, gates, w["conv_w"], wab, w["lam"], name="lru_bwd")
    dqa, dka, dva = _fox_bwd(qa, ka, va, doa, lse, drep, name="fox_bwd")
    dy1p, dwp, dln1g, dln1b, dbf = _proj_in_bwd(dqa, dka, dva, dlxg, fgb, xn1, dy2p, wp, xhat1, rstd1, w["ln1_g"],
                                                name="proj_in_bwd")
    small = dict(
        ln1_g=dln1g, ln1_b=dln1b, ln2_g=dln2g, ln2_b=dln2b, ln3_g=dln3g, ln3_b=dln3b,
        b_forget=dbf[:, :HEADS], conv_w=dcw[:CONV_K], conv_b=dcb,
        rg_wa=_block_diag_extract(dwab[:, :LRU_W]), rg_wx=_block_diag_extract(dwab[:, LRU_W:]),
        rg_ba=dbab[:, :LRU_W].reshape(HEADS, HEAD_DIM), rg_bx=dbab[:, LRU_W:].reshape(HEADS, HEAD_DIM),
        lru_lambda=dlam,
    )
    hooks.before_ffn1_bwd([dln1b])
    token = hooks.mixer_grads(dwp, dwo, small, loss)
    dx_a, *grads_a = _ffn_bwd(dy1p, xb0, g1a, u1a, w["f1g"], w["f1u"], f1d, token, name="ffn1_bwd_a", part=0)
    token = hooks.ffn1_grads(grads_a)
    dx, *grads_b = _ffn_bwd(dy1p, xb0, g1a, u1a, w["f1g"], w["f1u"], f1d, token, name="ffn1_bwd_b", part=1,
                            dx_init=dx_a)

    grads = dict(f1=(grads_a, grads_b), f2g=df2g, f2u=df2u, f2d=df2d, wp=dwp, wo=dwo, **small)
    return loss, dx, grads


MESH = pl.DeviceIdType.MESH
HBM_SPEC = pl.BlockSpec(memory_space=pl.ANY)
VMEM_SPEC = pl.BlockSpec(memory_space=pltpu.VMEM)


def _position():
    return lax.axis_index("x"), lax.axis_index("y"), lax.axis_index("c")


def _other_chips(x, y):
    return [(1 - x, y), (x, 1 - y), (1 - x, 1 - y)]


def _all_gather_bf16(shards, *, name):
    n = len(shards)

    def body(*refs):
        ins, outs, stages = refs[:n], refs[n:2 * n], refs[2 * n:3 * n]
        send_sems, recv_sems, local_sems = refs[3 * n:]
        x, y, c = _position()
        me, sibling = (x, y, c), (x, y, 1 - c)
        chips = _other_chips(x, y)

        def rows(k, px, py, pc):
            r = shards[k].shape[0]
            m = r // 2
            return outs[k].at[pl.ds(pl.multiple_of((2 * px + py) * r + pc * m, 16), m), :]

        def copy(k, idx, block, to, src=None):
            return pltpu.make_async_remote_copy(
                src_ref=rows(k, *block) if src is None else src, dst_ref=rows(k, *block),
                send_sem=send_sems.at[7 * k + idx], recv_sem=recv_sems.at[7 * k + idx],
                device_id=to, device_id_type=MESH)

        started = []
        mine = []
        for k in range(n):
            m = shards[k].shape[0] // 2
            stages[k][...] = ins[k][pl.ds(pl.multiple_of(c * m, 16), m), :].astype(stages[k].dtype)
            cp = pltpu.make_async_copy(stages[k], rows(k, *me), local_sems.at[k])
            cp.start()
            mine.append(cp)
            first = [copy(k, 0, me, sibling, src=stages[k])]
            first += [copy(k, 1 + j, me, (*chip, c), src=stages[k]) for j, chip in enumerate(chips)]
            for cp in first:
                cp.start()
            started += first
        for k in range(n):
            for j, chip in enumerate(chips):
                copy(k, 1 + j, (*chip, c), me).wait_recv()
                fwd = copy(k, 4 + j, (*chip, c), sibling)
                fwd.start()
                started.append(fwd)
        for k in range(n):
            copy(k, 0, sibling, me).wait_recv()
            for j, chip in enumerate(chips):
                copy(k, 4 + j, (*chip, 1 - c), me).wait_recv()
        for cp in started:
            cp.wait_send()
        for cp in mine:
            cp.wait()

    return _pcall(
        body, name=name,
        in_specs=[VMEM_SPEC] * n, out_specs=[HBM_SPEC] * n,
        out_shape=[jax.ShapeDtypeStruct((N_SHARD * s.shape[0], s.shape[1]), MXU_DTYPE) for s in shards],
        scratch_shapes=[pltpu.VMEM((s.shape[0] // 2, s.shape[1]), MXU_DTYPE) for s in shards]
        + [pltpu.SemaphoreType.DMA((7 * n,)), pltpu.SemaphoreType.DMA((7 * n,)), pltpu.SemaphoreType.DMA((n,))],
        compiler_params=pltpu.CompilerParams(vmem_limit_bytes=VMEM_LIMIT),
    )(*shards)


def _swap_halves(gs, *, name):
    n = len(gs)

    def body(*refs):
        ins, outs = refs[:n], refs[n:2 * n]
        send_sems, recv_sems = refs[2 * n:]
        x, y, c = _position()
        cps = []
        for k in range(n):
            m = gs[k].shape[1] // 2
            src = ins[k].at[:, pl.ds(pl.multiple_of((1 - c) * m, 16), m), :]
            cp = pltpu.make_async_remote_copy(src_ref=src, dst_ref=outs[k], send_sem=send_sems.at[k],
                                              recv_sem=recv_sems.at[k], device_id=(x, y, 1 - c), device_id_type=MESH)
            cp.start()
            cps.append(cp)
        for cp in cps:
            cp.wait()

    return _pcall(
        body, name=name, in_specs=[HBM_SPEC] * n, out_specs=[HBM_SPEC] * n,
        out_shape=[jax.ShapeDtypeStruct((g.shape[0], g.shape[1] // 2, g.shape[2]), g.dtype) for g in gs],
        scratch_shapes=[pltpu.SemaphoreType.DMA((n,)), pltpu.SemaphoreType.DMA((n,))],
    )(*gs)


def _add_halves(gs, recvs, *, name, tm=256):
    n = len(gs)
    _, r, cdim = gs[0].shape
    m = r // 2
    tm = min(tm, m)
    nb = m // tm
    c_idx = lax.axis_index("c").astype(jnp.int32).reshape(1)

    def body(c_ref, *refs):
        for k in range(n):
            refs[2 * n + k][...] = (refs[k][...].astype(f32) + refs[n + k][...].astype(f32)).astype(refs[2 * n + k].dtype)

    mine = pl.BlockSpec((None, tm, cdim), lambda j, i, c_ref: (j, c_ref[0] * nb + i, 0))
    half = pl.BlockSpec((None, tm, cdim), lambda j, i, c_ref: (j, i, 0))
    return _pcall(
        body, name=name,
        grid_spec=pltpu.PrefetchScalarGridSpec(
            num_scalar_prefetch=1, grid=(N_SHARD, nb),
            in_specs=[mine] * n + [half] * n, out_specs=[half] * n),
        out_shape=[jax.ShapeDtypeStruct((N_SHARD, m, cdim), g.dtype) for g in gs],
        compiler_params=_params(2),
    )(c_idx, *gs, *recvs)


def _scatter_partials(ps, *, name):
    n = len(ps)

    def body(*refs):
        ins, outs = refs[:n], refs[n:2 * n]
        send_sems, recv_sems = refs[2 * n:]
        x, y, c = _position()
        me_chip = 2 * x + y
        cps = []
        for k in range(n):
            for j, (px, py) in enumerate(_other_chips(x, y)):
                cp = pltpu.make_async_remote_copy(
                    src_ref=ins[k].at[2 * px + py], dst_ref=outs[k].at[me_chip],
                    send_sem=send_sems.at[3 * k + j], recv_sem=recv_sems.at[3 * k + j],
                    device_id=(px, py, c), device_id_type=MESH)
                cp.start()
                cps.append(cp)
        for cp in cps:
            cp.wait()

    return _pcall(
        body, name=name, in_specs=[HBM_SPEC] * n, out_specs=[HBM_SPEC] * n,
        out_shape=[jax.ShapeDtypeStruct(p.shape, p.dtype) for p in ps],
        scratch_shapes=[pltpu.SemaphoreType.DMA((3 * n,)), pltpu.SemaphoreType.DMA((3 * n,))],
    )(*ps)


def _sum_slabs(ps, qs, *, name, tm=128):
    n = len(qs)
    _, m, cdim = qs[0].shape
    tm = min(tm, m)
    nb = m // tm
    assert m % tm == 0, (m, tm)
    where = jnp.stack([2 * lax.axis_index("x") + lax.axis_index("y"), lax.axis_index("c")]).astype(jnp.int32)

    def body(w_ref, *refs):
        for k in range(n):
            own, q1, q2, q3 = (refs[4 * k + t][...].astype(f32) for t in range(4))
            refs[4 * n + k][...] = ((own + q1) + q2) + q3

    def slab(flip):
        return pl.BlockSpec((None, tm, cdim), lambda i, w_ref: (jnp.bitwise_xor(w_ref[0], flip), i, 0))

    operands = []
    for p, q in zip(ps, qs):
        operands += [p, q, q, q]
    return _pcall(
        body, name=name,
        grid_spec=pltpu.PrefetchScalarGridSpec(
            num_scalar_prefetch=1, grid=(nb,),
            in_specs=[slab(0), slab(2), slab(1), slab(3)] * n,
            out_specs=[pl.BlockSpec((tm, cdim), lambda i, w_ref: (w_ref[1] * nb + i, 0))] * n),
        out_shape=[jax.ShapeDtypeStruct((2 * m, cdim), f32) for _ in qs],
        compiler_params=_params(1),
    )(where, *operands)


def _join_halves(fs, *, name):
    n = len(fs)

    def body(*refs):
        outs = refs[n:2 * n]
        send_sems, recv_sems = refs[2 * n:]
        x, y, c = _position()
        cps = []
        for k in range(n):
            m = fs[k].shape[0] // 2
            half = outs[k].at[pl.ds(pl.multiple_of(c * m, 8), m), :]
            cp = pltpu.make_async_remote_copy(src_ref=half, dst_ref=half, send_sem=send_sems.at[k],
                                              recv_sem=recv_sems.at[k], device_id=(x, y, 1 - c), device_id_type=MESH)
            cp.start()
            cps.append(cp)
        for cp in cps:
            cp.wait()

    return _pcall(
        body, name=name, in_specs=[HBM_SPEC] * n, out_specs=[HBM_SPEC] * n,
        out_shape=[jax.ShapeDtypeStruct(f.shape, f.dtype) for f in fs],
        input_output_aliases={k: k for k in range(n)},
        scratch_shapes=[pltpu.SemaphoreType.DMA((n,)), pltpu.SemaphoreType.DMA((n,))],
    )(*fs)


def _all_reduce_small(v, after=None, *, name):
    r = v.shape[0]
    extra = [] if after is None else [after]

    def body(v_ref, *refs):
        out_ref, buf, send_sems, recv_sems, local_sem = refs[len(extra):]
        x, y, c = _position()
        me, sibling = (x, y, c), (x, y, 1 - c)
        chips = _other_chips(x, y)

        def rows(px, py, pc):
            return buf.at[pl.ds(pl.multiple_of((4 * px + 2 * py + pc) * r, 8), r), :]

        def copy(k, block, to, src=None):
            return pltpu.make_async_remote_copy(
                src_ref=rows(*block) if src is None else src, dst_ref=rows(*block),
                send_sem=send_sems.at[k], recv_sem=recv_sems.at[k], device_id=to, device_id_type=MESH)

        mine = pltpu.make_async_copy(v_ref, rows(*me), local_sem)
        mine.start()
        first = [copy(0, me, sibling, src=v_ref)]
        first += [copy(1 + j, me, (*chip, c), src=v_ref) for j, chip in enumerate(chips)]
        for cp in first:
            cp.start()
        passed = [copy(4 + j, (*chip, c), sibling) for j, chip in enumerate(chips)]
        for j, chip in enumerate(chips):
            copy(1 + j, (*chip, c), me).wait_recv()
            passed[j].start()
        copy(0, sibling, me).wait_recv()
        for j, chip in enumerate(chips):
            copy(4 + j, (*chip, 1 - c), me).wait_recv()
        for cp in first + passed:
            cp.wait_send()
        mine.wait()
        acc = buf[0:r, :]
        for d in range(1, N_DEV):
            acc = acc + buf[d * r:(d + 1) * r, :]
        out_ref[...] = acc

    return _pcall(
        body, name=name, in_specs=[VMEM_SPEC] + [HBM_SPEC] * len(extra), out_specs=VMEM_SPEC,
        out_shape=jax.ShapeDtypeStruct((r, LANES), f32),
        scratch_shapes=[pltpu.VMEM((N_DEV * r, LANES), f32), pltpu.SemaphoreType.DMA((7,)),
                        pltpu.SemaphoreType.DMA((7,)), pltpu.SemaphoreType.DMA],
    )(v, *extra)


SEM_SPEC = pl.BlockSpec(memory_space=pltpu.SEMAPHORE)
HBM_ONLY = pl.BlockSpec(memory_space=pltpu.HBM)
EFFECT = pltpu.SideEffectType.DATAFLOW_SIDE_EFFECTING


def _sends(copies):
    return copies[0] if isinstance(copies, tuple) else copies


def _arrivals(copies):
    return copies[1] if isinstance(copies, tuple) else copies


def _split_start(bufs, copies_fn, n_sems, *, name):
    n = len(bufs)

    def body(*refs):
        send_sems, recv_sems = refs[n], refs[n + 1]
        thru = refs[n + 2:2 * n + 2]
        token = refs[2 * n + 2]
        for cp in _sends(copies_fn(thru, send_sems, recv_sems)):
            cp.start()
        token[...] = jnp.zeros_like(token)

    outs = _pcall(
        body, name=name,
        out_shape=(pltpu.SemaphoreType.DMA((n_sems,)), pltpu.SemaphoreType.DMA((n_sems,)),
                   *[pltpu.HBM(b.shape, b.dtype) for b in bufs], jax.ShapeDtypeStruct((8, LANES), f32)),
        in_specs=[HBM_ONLY] * n,
        out_specs=(SEM_SPEC, SEM_SPEC, *[HBM_ONLY] * n, VMEM_SPEC),
        input_output_aliases={k: 2 + k for k in range(n)},
        compiler_params=pltpu.CompilerParams(has_side_effects=EFFECT),
    )(*[pltpu.with_memory_space_constraint(b, pltpu.HBM) for b in bufs])
    return outs[0], outs[1], list(outs[2:2 + n]), outs[2 + n]


def _split_wait(thru, send_sems, recv_sems, after, copies_fn, *, name):
    n = len(thru)

    def body(*refs):
        copies = copies_fn(refs[:n], refs[n], refs[n + 1])
        for cp in _sends(copies):
            cp.wait_send()
        for cp in _arrivals(copies):
            cp.wait_recv()

    return list(_pcall(
        body, name=name,
        out_shape=tuple(pltpu.HBM(b.shape, b.dtype) for b in thru),
        in_specs=[HBM_ONLY] * n + [SEM_SPEC, SEM_SPEC] + [HBM_SPEC] * len(after),
        out_specs=tuple([HBM_ONLY] * n),
        input_output_aliases={k: k for k in range(n)},
        compiler_params=pltpu.CompilerParams(has_side_effects=EFFECT),
    )(*thru, send_sems, recv_sems, *after))


def _scatter_copies(n):
    def copies(bufs, send_sems, recv_sems):
        x, y, c = _position()
        me_chip = 2 * x + y
        cps = []
        for k in range(n):
            for j, (px, py) in enumerate(_other_chips(x, y)):
                cps.append(pltpu.make_async_remote_copy(
                    src_ref=bufs[k].at[2 * px + py], dst_ref=bufs[n + k].at[me_chip],
                    send_sem=send_sems.at[3 * k + j], recv_sem=recv_sems.at[3 * k + j],
                    device_id=(px, py, c), device_id_type=MESH))
        return cps
    return copies


N_PEERS = N_DEV - 1


def _direct_copies(n):
    def copies(bufs, send_sems, recv_sems):
        x, y, c = _position()
        me_chip = 2 * x + y
        sends, arrivals = [], []
        for k in range(n):
            m = bufs[k].shape[1] // 2
            land = bufs[n + k]

            def rows(slab, half, k=k, m=m):
                start = half * m if isinstance(half, int) else pl.multiple_of(half * m, 16)
                return bufs[k].at[slab, pl.ds(start, m), :]

            def copy(src, slot, send_idx, recv_idx, to, k=k, land=land):
                return pltpu.make_async_remote_copy(
                    src_ref=src, dst_ref=land.at[slot], send_sem=send_sems.at[N_PEERS * k + send_idx],
                    recv_sem=recv_sems.at[N_PEERS * k + recv_idx], device_id=to, device_id_type=MESH)

            sends.append(copy(rows(me_chip, 1 - c), 0, 0, 0, (x, y, 1 - c)))
            arrivals.append(copy(rows(me_chip, c), 0, 0, 0, (x, y, 1 - c)))
            for t, (px, py) in enumerate(_other_chips(x, y)):
                for core in range(2):
                    sends.append(copy(rows(2 * px + py, core), 1 + 2 * t + c, 1 + 2 * t + core, 1 + 2 * t + c,
                                      (px, py, core)))
                    arrivals.append(copy(rows(me_chip, c), 1 + 2 * t + core, 1 + 2 * t + core, 1 + 2 * t + core,
                                         (px, py, core)))
        return sends, arrivals
    return copies


def _sum_direct(gs, lands, *, name, tm=128):
    n = len(gs)
    _, m, cdim = lands[0].shape
    tm = min(tm, m)
    nb = m // tm
    assert m % tm == 0, (m, tm)
    where = jnp.stack([2 * lax.axis_index("x") + lax.axis_index("y"), lax.axis_index("c")]).astype(jnp.int32)

    def body(w_ref, *refs):
        for k in range(n):
            acc = refs[2 * k][...].astype(f32)
            for slot in range(N_PEERS):
                acc = acc + refs[2 * k + 1][slot].astype(f32)
            refs[2 * n + k][...] = acc

    own = pl.BlockSpec((None, tm, cdim), lambda i, w_ref: (w_ref[0], w_ref[1] * nb + i, 0))
    landed = pl.BlockSpec((N_PEERS, tm, cdim), lambda i, w_ref: (0, i, 0))
    operands = []
    for g, land in zip(gs, lands):
        operands += [g, land]
    return _pcall(
        body, name=name,
        grid_spec=pltpu.PrefetchScalarGridSpec(
            num_scalar_prefetch=1, grid=(nb,), in_specs=[own, landed] * n,
            out_specs=[pl.BlockSpec((tm, cdim), lambda i, w_ref: (w_ref[1] * nb + i, 0))] * n),
        out_shape=[jax.ShapeDtypeStruct((2 * m, cdim), f32) for _ in gs],
        compiler_params=_params(1),
    )(where, *operands)


def _broadcast_copies(bufs, send_sems, recv_sems):
    v, land = bufs
    x, y, c = _position()

    def copy(slot, send_idx, recv_idx, to):
        return pltpu.make_async_remote_copy(src_ref=v, dst_ref=land.at[slot], send_sem=send_sems.at[send_idx],
                                            recv_sem=recv_sems.at[recv_idx], device_id=to, device_id_type=MESH)

    sends = [copy(0, 0, 0, (x, y, 1 - c))]
    arrivals = [copy(0, 0, 0, (x, y, 1 - c))]
    for t, (px, py) in enumerate(_other_chips(x, y)):
        for core in range(2):
            sends.append(copy(1 + 2 * t + c, 1 + 2 * t + core, 1 + 2 * t + c, (px, py, core)))
            arrivals.append(copy(1 + 2 * t + core, 1 + 2 * t + core, 1 + 2 * t + core, (px, py, core)))
    return sends, arrivals


def _sum_in_device_order(v, land, *, name):
    r, cdim = v.shape
    x, y, c = _position()
    slots, mine = [], []
    for d in range(N_DEV):
        dx, dy, dc = d // 4, (d // 2) % 2, d % 2
        fx, fy = jnp.bitwise_xor(dx, x), jnp.bitwise_xor(dy, y)
        t = jnp.where(fx == 1, jnp.where(fy == 1, 2, 0), 1)
        slots.append(jnp.where(jnp.logical_and(fx == 0, fy == 0), 0, 1 + 2 * t + dc))
        mine.append(jnp.logical_and(jnp.logical_and(fx == 0, fy == 0), dc == c))
    table = jnp.stack(slots + mine).astype(jnp.int32)

    def body(tab_ref, v_ref, *refs):
        out_ref = refs[N_DEV]
        acc = None
        for d in range(N_DEV):
            term = jnp.where(tab_ref[N_DEV + d] == 1, v_ref[...], refs[d][...])
            acc = term if acc is None else acc + term
        out_ref[...] = acc

    whole = pl.BlockSpec((r, cdim), lambda i, tab_ref: (0, 0))
    landed = [pl.BlockSpec((None, r, cdim), functools.partial(lambda i, tab_ref, d: (tab_ref[d], 0, 0), d=d))
              for d in range(N_DEV)]
    return _pcall(
        body, name=name,
        grid_spec=pltpu.PrefetchScalarGridSpec(num_scalar_prefetch=1, grid=(1,), in_specs=[whole] + landed,
                                               out_specs=whole),
        out_shape=jax.ShapeDtypeStruct((r, cdim), f32),
        compiler_params=_params(1),
    )(table, v, *[land] * N_DEV)


def _block_rows(buf, px, py, pc):
    m = buf.shape[0] // N_DEV
    return buf.at[pl.ds(pl.multiple_of((4 * px + 2 * py + pc) * m, 16), m), :]


def _gather_ici_copies(n):
    def copies(bufs, send_sems, recv_sems):
        x, y, c = _position()
        cps = []
        for k in range(n):
            rows = _block_rows(bufs[k], x, y, c)
            targets = [(x, y, 1 - c)] + [(px, py, c) for px, py in _other_chips(x, y)]
            for j, to in enumerate(targets):
                cps.append(pltpu.make_async_remote_copy(
                    src_ref=rows, dst_ref=rows, send_sem=send_sems.at[4 * k + j], recv_sem=recv_sems.at[4 * k + j],
                    device_id=to, device_id_type=MESH))
        return cps
    return copies


def _gather_d2d_copies(n):
    def copies(bufs, send_sems, recv_sems):
        x, y, c = _position()
        cps = []
        for k in range(n):
            for j, (px, py) in enumerate(_other_chips(x, y)):
                rows = _block_rows(bufs[k], px, py, c)
                cps.append(pltpu.make_async_remote_copy(
                    src_ref=rows, dst_ref=rows, send_sem=send_sems.at[3 * k + j], recv_sem=recv_sems.at[3 * k + j],
                    device_id=(x, y, 1 - c), device_id_type=MESH))
        return cps
    return copies


def _cast_halves(shards, after, *, name):
    n = len(shards)
    where = jnp.stack([2 * lax.axis_index("x") + lax.axis_index("y"), lax.axis_index("c")]).astype(jnp.int32)

    def body(w_ref, *refs):
        for k in range(n):
            refs[n + 1 + k][...] = refs[k][...].astype(refs[n + 1 + k].dtype)

    def half(s):
        return (s.shape[0] // 2, s.shape[1])

    return _pcall(
        body, name=name,
        grid_spec=pltpu.PrefetchScalarGridSpec(
            num_scalar_prefetch=1, grid=(1,),
            in_specs=[pl.BlockSpec(half(s), lambda i, w_ref: (w_ref[1], 0)) for s in shards] + [HBM_SPEC],
            out_specs=[pl.BlockSpec(half(s), lambda i, w_ref: (2 * w_ref[0] + w_ref[1], 0)) for s in shards]),
        out_shape=[jax.ShapeDtypeStruct((N_SHARD * s.shape[0], s.shape[1]), MXU_DTYPE) for s in shards],
        compiler_params=_params(1),
    )(where, *shards, after)


class _SplitGather:
    def __init__(self, shards, after, tag):
        self.tag = tag
        self.n = len(shards)
        halves = _cast_halves(shards, after, name=f"{tag}_cast")
        self.ici = _split_start(halves, _gather_ici_copies(self.n), 4 * self.n, name=f"{tag}_ici_start")
        self.token = self.ici[3]

    def forward(self, after):
        send_sems, recv_sems, thru, _ = self.ici
        landed = _split_wait(thru, send_sems, recv_sems, after, _gather_ici_copies(self.n), name=f"{self.tag}_ici_wait")
        self.d2d = _split_start(landed, _gather_d2d_copies(self.n), 3 * self.n, name=f"{self.tag}_d2d_start")
        return self.d2d[3]

    def finish(self, after):
        send_sems, recv_sems, thru, _ = self.d2d
        return _split_wait(thru, send_sems, recv_sems, after, _gather_d2d_copies(self.n), name=f"{self.tag}_d2d_wait")


class _Overlap(_NoOverlap):
    def __init__(self, late_shards, ffn2_shards, after):
        self.late = _SplitGather(late_shards, after, "ag1")
        self.ffn2 = _SplitGather(ffn2_shards, self.late.token, "ag2")
        self.reduced = None
        self.ffn1_parts = []

    def start_token(self):
        return self.ffn2.token

    def late_weights(self, w, after):
        token = self.late.forward(after)
        f1d, w_in, wo = self.late.finish([token])
        w_in = w_in.reshape(N_SHARD, D_MODEL, IN_SHARD).transpose(1, 0, 2).reshape(D_MODEL, IN_COLS)
        return dict(f1d=f1d.reshape(N_SHARD, D_FF // N_SHARD, D_MODEL), wp=make_wp(w_in), wo=wo)

    def after_attention(self, after):
        return self.ffn2.forward(after)

    def ffn2_weights(self, w, after):
        full = self.ffn2.finish(after)
        fs = D_FF // N_SHARD
        return (full[0].reshape(N_SHARD, D_MODEL, fs), full[1].reshape(N_SHARD, D_MODEL, fs),
                full[2].reshape(N_SHARD, fs, D_MODEL))

    @staticmethod
    def _send_direct(grads, tag):
        lands = [lax.empty((N_PEERS, g.shape[1] // 2, g.shape[2]), g.dtype) for g in grads]
        return _split_start(list(grads) + lands, _direct_copies(len(grads)), N_PEERS * len(grads),
                            name=f"rs_direct_{tag}_start")

    def ffn2_grads(self, grads):
        self.scatter = self._send_direct(grads, "ffn2")
        return self.scatter[3]

    def ffn1_grads(self, grads):
        tag = "ffn1" + "ab"[len(self.ffn1_parts)]
        if not self.ffn1_parts:
            started = self._send_direct(grads, tag)
        else:
            recvs = _swap_halves(grads, name=f"rs_swap_{tag}")
            ps = list(_add_halves(grads[:2], recvs[:2], name=f"rs_add_{tag}_gu"))
            ps += list(_add_halves(grads[2:], recvs[2:], name=f"rs_add_{tag}_d"))
            lands = [lax.empty(p.shape, p.dtype) for p in ps]
            started = _split_start(ps + lands, _scatter_copies(3), 9, name=f"rs_scatter_{tag}_start")
        self.ffn1_parts.append((tag, started))
        return started[3]

    def ffn1_reduced(self, after):
        sums = []
        for direct, (tag, (send_sems, recv_sems, thru, _)) in zip((True, False), self.ffn1_parts):
            plan, add = (_direct_copies, _sum_direct) if direct else (_scatter_copies, _sum_slabs)
            done = _split_wait(thru, send_sems, recv_sems, after, plan(3), name=f"rs_{tag}_wait")
            sums += list(add(done[:2], done[3:5], name=f"rs_sum_{tag}_gu"))
            sums += list(add(done[2:3], done[5:], name=f"rs_sum_{tag}_d"))
        return sums

    def mixer_grads(self, dwp, dwo, small, loss):
        packed = jnp.concatenate([_pack_small(small), jnp.broadcast_to(loss, (8, LANES))], axis=0)
        land = lax.empty((N_PEERS,) + packed.shape, packed.dtype)
        self.small = _split_start([packed, land], _broadcast_copies, N_PEERS, name="ar_small_start")
        gwin = dwp[:, :IN_COLS].reshape(D_MODEL, N_SHARD, IN_SHARD).transpose(1, 0, 2).astype(GRAD_DTYPE)
        gwo = dwo.reshape(N_SHARD, D_MODEL // N_SHARD, D_MODEL).astype(GRAD_DTYPE)
        self.scatter_mix = self._send_direct([gwin, gwo], "mix")
        return self.small[3] + self.scatter_mix[3]

    def small_summed(self, after):
        send_sems, recv_sems, thru, _ = self.small
        packed, land = _split_wait(thru, send_sems, recv_sems, after, _broadcast_copies, name="ar_small_wait")
        summed = _sum_in_device_order(packed, land, name="ar_small_sum")
        return summed[:-8], summed[-8, 0]

    def mixer_reduced(self, after):
        send_sems, recv_sems, thru, _ = self.scatter_mix
        done = _split_wait(thru, send_sems, recv_sems, after, _direct_copies(2), name="rs_direct_mix_wait")
        return [_sum_direct([done[k]], [done[2 + k]], name=f"rs_sum_{tag}")[0] for k, tag in enumerate(["w_in", "w_out"])]

    def before_ffn1_bwd(self, after):
        send_sems, recv_sems, thru, _ = self.scatter
        n = len(thru) // 2
        done = _split_wait(thru, send_sems, recv_sems, after, _direct_copies(n), name="rs_direct_ffn2_wait")
        self.reduced = list(_sum_direct(done[:n], done[n:], name="rs_sum_ffn2"))


def _adamw(gs, ws, ms, vs, *, name, tm=256):
    n = len(gs)
    r, cdim = ws[0].shape[-2:]
    tm = r if tm is None else min(tm, r)
    assert r % tm == 0, (r, tm)
    nb = r // tm
    c1 = 1.0 / (1.0 - ADAM_B1 ** ADAM_STEP)
    c2 = 1.0 / (1.0 - ADAM_B2 ** ADAM_STEP)
    flat = pl.BlockSpec((tm, cdim), lambda i: (i, 0))

    g_ops, g_specs, g_where = [], [], []
    for g in gs:
        g_where.append(len(g_ops))
        if not isinstance(g, tuple):
            g_ops.append(g)
            g_specs.append(flat)
        elif g[2] == 1:
            g_ops += [g[0], g[1]]
            g_specs += [pl.BlockSpec((tm, cdim // 2), lambda i: (i, 0))] * 2
        else:
            g_ops += [g[0], g[1]]
            g_specs += [pl.BlockSpec((tm, cdim), lambda i: (jnp.minimum(i, nb // 2 - 1), 0)),
                        pl.BlockSpec((tm, cdim), lambda i: (jnp.maximum(i - nb // 2, 0), 0))]
    ng = len(g_ops)

    def gradient(refs, k):
        g, at = gs[k], g_where[k]
        if not isinstance(g, tuple):
            return refs[at][...]
        if g[2] == 1:
            return jnp.concatenate([refs[at][...], refs[at + 1][...]], axis=1)
        return jnp.where(pl.program_id(0) < nb // 2, refs[at][...], refs[at + 1][...])

    def body(*refs):
        rest = refs[ng:]
        for k in range(n):
            g = gradient(refs, k)
            w = rest[k][...]
            m = ADAM_B1 * rest[n + k][...] + (1.0 - ADAM_B1) * g
            v = ADAM_B2 * rest[2 * n + k][...] + (1.0 - ADAM_B2) * (g * g)
            rest[3 * n + k][...] = g
            rest[4 * n + k][...] = -ADAM_LR * ((m * c1) / (jnp.sqrt(v * c2) + ADAM_EPS) + ADAM_WD * w)
            rest[5 * n + k][...] = m
            rest[6 * n + k][...] = v

    like_w = flat if ws[0].ndim == 2 else pl.BlockSpec((None, tm, cdim), lambda i: (0, i, 0))
    outs = _pcall(
        body, name=name, grid=(nb,), in_specs=g_specs + [like_w] * (3 * n), out_specs=[like_w] * (4 * n),
        out_shape=[jax.ShapeDtypeStruct(ws[0].shape, f32)] * (4 * n),
        compiler_params=_params(1),
    )(*g_ops, *ws, *ms, *vs)
    return outs[:n], outs[n:2 * n], outs[2 * n:3 * n], outs[3 * n:]


BIG = ["ffn1_w_gate", "ffn1_w_up", "ffn1_w_down", "ffn2_w_gate", "ffn2_w_up", "ffn2_w_down"]
SMALL = ["ln1_g", "ln1_b", "b_forget", "conv_w", "conv_b", "rg_wa", "rg_ba", "rg_wx", "rg_bx", "lru_lambda",
         "ln2_g", "ln2_b", "ln3_g", "ln3_b"]
WEIGHTS = ["ffn1_w_gate", "ffn1_w_up", "ffn1_w_down", "ln1_g", "ln1_b", "w_in", "b_forget", "conv_w", "conv_b",
           "rg_wa", "rg_ba", "rg_wx", "rg_bx", "lru_lambda", "w_out", "ln2_g", "ln2_b",
           "ffn2_w_gate", "ffn2_w_up", "ffn2_w_down", "ln3_g", "ln3_b"]


def _pack_small(parts):
    rows = []
    for n in SMALL:
        flat = parts[n].reshape(-1)
        pad = (-flat.shape[0]) % LANES
        rows.append(jnp.pad(flat, (0, pad)).reshape(-1, LANES))
    packed = jnp.concatenate(rows, axis=0)
    return jnp.pad(packed, ((0, (-packed.shape[0]) % 8), (0, 0)))


def _unpack_small(packed, shapes):
    out, r0 = {}, 0
    for n in SMALL:
        size = math.prod(shapes[n])
        nr = -(-size // LANES)
        out[n] = packed[r0:r0 + nr].reshape(-1)[:size].reshape(shapes[n])
        r0 += nr
    return out


def kernel(x, ffn1_w_gate, ffn1_w_up, ffn1_w_down, ln1_g, ln1_b, w_in, b_forget, conv_w, conv_b, rg_wa, rg_ba, rg_wx, rg_bx, lru_lambda, w_out, ln2_g, ln2_b, ffn2_w_gate, ffn2_w_up, ffn2_w_down, ln3_g, ln3_b, loss_target, m_ffn1_w_gate, m_ffn1_w_up, m_ffn1_w_down, m_ln1_g, m_ln1_b, m_w_in, m_b_forget, m_conv_w, m_conv_b, m_rg_wa, m_rg_ba, m_rg_wx, m_rg_bx, m_lru_lambda, m_w_out, m_ln2_g, m_ln2_b, m_ffn2_w_gate, m_ffn2_w_up, m_ffn2_w_down, m_ln3_g, m_ln3_b, v_ffn1_w_gate, v_ffn1_w_up, v_ffn1_w_down, v_ln1_g, v_ln1_b, v_w_in, v_b_forget, v_conv_w, v_conv_b, v_rg_wa, v_rg_ba, v_rg_wx, v_rg_bx, v_lru_lambda, v_w_out, v_ln2_g, v_ln2_b, v_ffn2_w_gate, v_ffn2_w_up, v_ffn2_w_down, v_ln3_g, v_ln3_b):
    args = dict(locals())
    w = {n: args[n] for n in WEIGHTS}
    mom = {n: args["m_" + n] for n in WEIGHTS}
    var = {n: args["v_" + n] for n in WEIGHTS}
    chip = 2 * lax.axis_index("x") + lax.axis_index("y")

    g1 = _all_gather_bf16([w[n][0] for n in BIG[:2]], name="ag_ffn1_up")
    fs = D_FF // N_SHARD
    full = dict(
        f1g=g1[0].reshape(N_SHARD, D_MODEL, fs), f1u=g1[1].reshape(N_SHARD, D_MODEL, fs),
        bfp=jnp.pad(b_forget, ((0, 0), (0, LANES - HEADS))),
        ln1_g=ln1_g, ln1_b=ln1_b, ln2_g=ln2_g, ln2_b=ln2_b, ln3_g=ln3_g, ln3_b=ln3_b,
        conv_b=conv_b, rg_wa=rg_wa[0], rg_wx=rg_wx[0], rg_ba=rg_ba[0], rg_bx=rg_bx[0], lam=lru_lambda,
    )
    cw_place = lax.dynamic_update_slice(jnp.zeros((8, LRU_W), f32), conv_w[0] * 0.5, (0, chip * (LRU_W // N_SHARD)))
    cw_full = _all_reduce_small(cw_place.reshape(-1, LANES), g1[0], name="ag_conv_w")
    full["conv_w"] = cw_full.reshape(8, LRU_W)[:CONV_K]

    hooks = _Overlap([w["ffn1_w_down"][0], w["w_in"][0], w["w_out"][0]], [w[n][0] for n in BIG[3:]], cw_full)
    loss_rep, dx, g = _local_step(x[0], loss_target[0], full, hooks)

    token1 = hooks.ffn1_grads(g["f1"][1])
    red = _join_halves(hooks.reduced + hooks.mixer_reduced([token1]), name="rs_join_rest")
    grads = dict(zip(BIG[3:] + ["w_in", "w_out"], red))

    small_sum, loss = hooks.small_summed(red)
    small_shapes = {n: w[n].shape for n in SMALL}
    small_shapes["conv_w"] = (1, CONV_K, LRU_W)
    gs_red = _unpack_small(small_sum, small_shapes)
    gs_red["conv_w"] = lax.dynamic_slice(gs_red["conv_w"], (0, 0, chip * (LRU_W // N_SHARD)),
                                         (1, CONV_K, LRU_W // N_SHARD))
    grads.update(gs_red)

    delta, new_m, new_v = {}, {}, {}

    def adamw(names, name, **kw):
        g3, d, nm, nv = _adamw([grads[n] for n in names], [w[n] for n in names], [mom[n] for n in names],
                               [var[n] for n in names], name=name, **kw)
        for i, n in enumerate(names):
            grads[n], delta[n], new_m[n], new_v[n] = g3[i], d[i], nm[i], nv[i]

    adamw(BIG[3:], "adamw_ffn2", tm=128)
    adamw(["w_in"], "adamw_w_in")
    adamw(["w_out"], "adamw_w_out")
    shard_shapes = {n: w[n].shape for n in SMALL}
    _, d, nm, nv = _adamw([_pack_small({n: grads[n] for n in SMALL})], [_pack_small({n: w[n] for n in SMALL})],
                          [_pack_small({n: mom[n] for n in SMALL})], [_pack_small({n: var[n] for n in SMALL})],
                          name="adamw_small", tm=None)
    for dst, packed in ((delta, d[0]), (new_m, nm[0]), (new_v, nv[0])):
        dst.update(_unpack_small(packed, shard_shapes))

    worked = [new_v["ffn2_w_down"], new_v["w_in"], new_v["w_out"], nv[0]]
    ga, ua, da, gb, ub, db = _join_halves(hooks.ffn1_reduced(worked), name="rs_join_ffn1")
    grads.update(ffn1_w_gate=(ga, gb, 1), ffn1_w_up=(ua, ub, 1), ffn1_w_down=(da, db, 0))
    adamw(BIG[:3], "adamw_ffn1", tm=128)

    def shaped(tree, n):
        return tree[n].reshape(w[n].shape)

    return (loss, dx[None], *[shaped(grads, n) for n in WEIGHTS], *[shaped(delta, n) for n in WEIGHTS],
            *[shaped(new_m, n) for n in WEIGHTS], *[shaped(new_v, n) for n in WEIGHTS])
```

```python
import functools
import math

import jax
import jax.numpy as jnp
from jax import lax
from jax.experimental import pallas as pl
from jax.experimental.pallas import tpu as pltpu

f32 = jnp.float32
MXU_DTYPE = jnp.bfloat16
GRAD_DTYPE = jnp.bfloat16

D_MODEL = 1024
D_FF = 4096
N_SHARD = 4
N_DEV = 8
FOX_W = 512
LRU_W = 512
HEADS = 8
HEAD_DIM = 64
CONV_K = 4
IN_COLS = 2568
IN_SHARD = IN_COLS // N_SHARD
QKV_W = 3 * FOX_W
Z_PAD = 2688
CAST_COLS = 384
LANES = 128
LN_EPS = 1e-5
DN_ALPHA = 2.0 ** 0.25
LRU_C = 8.0
NEG_BIG = -1e30
VMEM_LIMIT = 56 * 1024 * 1024

ADAM_LR = 0.001
ADAM_B1 = 0.9
ADAM_B2 = 0.999
ADAM_EPS = 1e-08
ADAM_WD = 0.01
ADAM_STEP = 10


def _pcall(body, **kw):
    return pl.pallas_call(body, **kw)


def _params(n_grid, vmem=VMEM_LIMIT):
    return pltpu.CompilerParams(dimension_semantics=("arbitrary",) * n_grid, vmem_limit_bytes=vmem)


def _dot(a, b):
    return jnp.dot(a, b, preferred_element_type=f32)


def _dot_nt(a, b):
    return lax.dot_general(a, b, (((1,), (1,)), ((), ())), preferred_element_type=f32)


def _dot_tn(a, b):
    return lax.dot_general(a, b, (((0,), (0,)), ((), ())), preferred_element_type=f32)


def _sigmoid(x):
    return 1.0 / (1.0 + jnp.exp(-x))


def _layer_norm_stats(y):
    mu = jnp.mean(y, axis=-1, keepdims=True)
    yc = y - mu
    var = jnp.mean(yc * yc, axis=-1, keepdims=True)
    rstd = lax.rsqrt(var + LN_EPS)
    return yc * rstd, rstd


def _ln_backward(dy, xhat, rstd, gamma):
    dxhat = dy * gamma
    m1 = jnp.mean(dxhat, axis=-1, keepdims=True)
    m2 = jnp.mean(dxhat * xhat, axis=-1, keepdims=True)
    dyp = rstd * (dxhat - m1 - xhat * m2)
    return dyp, jnp.sum(dy * xhat, axis=0, keepdims=True), jnp.sum(dy, axis=0, keepdims=True)


def _ffn_fwd_loss(x, wg, wu, wd, ln_g, ln_b, target, *, name, tm=1024, tf=512):
    T = x.shape[0]
    tm = min(tm, T)
    tr = min(256, tm)
    fs = D_FF // N_SHARD
    cpf = fs // tf
    nf = D_FF // tf
    nt = T // tm

    def body(x_ref, wg_ref, wu_ref, wd_ref, g_ref, b_ref, t_ref,
             xb_ref, gact_ref, uact_ref, dyp_ref, dgam_ref, dbeta_ref, loss_ref, acc_ref):
        i = pl.program_id(0)
        f = pl.program_id(1)

        @pl.when(jnp.logical_and(i == 0, f == 0))
        def _():
            dgam_ref[...] = jnp.zeros_like(dgam_ref)
            dbeta_ref[...] = jnp.zeros_like(dbeta_ref)
            loss_ref[...] = jnp.zeros_like(loss_ref)

        @pl.when(f == 0)
        def _():
            xb_ref[...] = x_ref[...].astype(MXU_DTYPE)
            acc_ref[...] = jnp.zeros_like(acc_ref)

        xb = xb_ref[...]
        g = _dot(xb, wg_ref[...])
        u = _dot(xb, wu_ref[...])
        h = (g * _sigmoid(g)) * u
        gact_ref[...] = g.astype(gact_ref.dtype)
        uact_ref[...] = u.astype(uact_ref.dtype)
        acc_ref[...] += _dot(h.astype(MXU_DTYPE), wd_ref[...])

        @pl.when(f == nf - 1)
        def _():
            gamma = g_ref[...]

            def rows_chunk(r, carry):
                rows = pl.ds(pl.multiple_of(r * tr, tr), tr)
                xhat, rstd = _layer_norm_stats(DN_ALPHA * x_ref[rows, :] + 0.5 * acc_ref[rows, :])
                err = xhat * gamma + b_ref[...] - t_ref[rows, :]
                sq = jnp.sum(jnp.sum(err * err, axis=0, keepdims=True), axis=1, keepdims=True)
                loss_ref[...] += jnp.broadcast_to(sq * (0.5 / D_MODEL), loss_ref.shape)
                dyp, dgam, dbeta = _ln_backward(err * (1.0 / D_MODEL), xhat, rstd, gamma)
                dyp_ref[rows, :] = dyp
                dgam_ref[...] += dgam
                dbeta_ref[...] += dbeta
                return carry

            lax.fori_loop(0, tm // tr, rows_chunk, 0)

    row = lambda i, f: (i, 0)
    const = lambda i, f: (0, 0)
    tile = pl.BlockSpec((tm, tf), lambda i, f: (i, f))
    cols = pl.BlockSpec((None, D_MODEL, tf), lambda i, f: (f // cpf, 0, f % cpf))
    last = lambda i, f: (jnp.where(f == nf - 1, i, jnp.maximum(i - 1, 0)), 0)
    return _pcall(
        body, name=name, grid=(nt, nf),
        in_specs=[pl.BlockSpec((tm, D_MODEL), row), cols, cols,
                  pl.BlockSpec((None, tf, D_MODEL), lambda i, f: (f // cpf, f % cpf, 0)),
                  pl.BlockSpec((1, D_MODEL), const), pl.BlockSpec((1, D_MODEL), const),
                  pl.BlockSpec((tm, D_MODEL), last)],
        out_specs=[pl.BlockSpec((tm, D_MODEL), row), tile, tile, pl.BlockSpec((tm, D_MODEL), row),
                   pl.BlockSpec((1, D_MODEL), const), pl.BlockSpec((1, D_MODEL), const), pl.BlockSpec((1, LANES), const)],
        out_shape=[jax.ShapeDtypeStruct((T, D_MODEL), MXU_DTYPE), jax.ShapeDtypeStruct((T, D_FF), MXU_DTYPE),
                   jax.ShapeDtypeStruct((T, D_FF), MXU_DTYPE), jax.ShapeDtypeStruct((T, D_MODEL), f32),
                   jax.ShapeDtypeStruct((1, D_MODEL), f32), jax.ShapeDtypeStruct((1, D_MODEL), f32),
                   jax.ShapeDtypeStruct((1, LANES), f32)],
        scratch_shapes=[pltpu.VMEM((tm, D_MODEL), f32)],
        compiler_params=_params(2),
    )(x, wg, wu, wd, ln_g, ln_b, target)


def _ffn_up(x, wg, wu, after=None, *, name, tm=1024, tf=512):
    T = x.shape[0]
    tm = min(tm, T)
    cpf = (D_FF // N_SHARD) // tf
    nf = D_FF // tf
    extra = [] if after is None else [after]

    def body(x_ref, wg_ref, wu_ref, *refs):
        xb_ref, gact_ref, uact_ref, hact_ref = refs[len(extra):]

        @pl.when(pl.program_id(1) == 0)
        def _():
            xb_ref[...] = x_ref[...].astype(MXU_DTYPE)

        xb = xb_ref[...]
        g = _dot(xb, wg_ref[...])
        u = _dot(xb, wu_ref[...])
        gact_ref[...] = g.astype(gact_ref.dtype)
        uact_ref[...] = u.astype(uact_ref.dtype)
        hact_ref[...] = ((g * _sigmoid(g)) * u).astype(hact_ref.dtype)

    row = lambda i, f: (i, 0)
    tile = pl.BlockSpec((tm, tf), lambda i, f: (i, f))
    cols = pl.BlockSpec((None, D_MODEL, tf), lambda i, f: (f // cpf, 0, f % cpf))
    return _pcall(
        body, name=name, grid=(T // tm, nf),
        in_specs=[pl.BlockSpec((tm, D_MODEL), row), cols, cols] + [pl.BlockSpec(memory_space=pl.ANY)] * len(extra),
        out_specs=[pl.BlockSpec((tm, D_MODEL), row), tile, tile, tile],
        out_shape=[jax.ShapeDtypeStruct((T, D_MODEL), MXU_DTYPE)] + [jax.ShapeDtypeStruct((T, D_FF), MXU_DTYPE)] * 3,
        compiler_params=_params(2),
    )(x, wg, wu, *extra)


def _ffn_down_ln(x, hact, wd, ln_g, ln_b, *, name, tm=1024):
    T = x.shape[0]
    tm = min(tm, T)
    fs = D_FF // N_SHARD
    ks = 2
    nk = N_SHARD // ks

    def body(x_ref, h_ref, wd_ref, g_ref, b_ref, xhat_ref, xn_ref, rstd_ref, acc_ref):
        k = pl.program_id(1)

        @pl.when(k == 0)
        def _():
            acc_ref[...] = jnp.zeros_like(acc_ref)

        acc_ref[...] += _dot(h_ref[...], wd_ref[...].reshape(ks * fs, D_MODEL))

        @pl.when(k == nk - 1)
        def _():
            xhat, rstd = _layer_norm_stats(DN_ALPHA * x_ref[...] + 0.5 * acc_ref[...])
            xhat_ref[...] = xhat
            xn_ref[...] = (xhat * g_ref[...] + b_ref[...]).astype(xn_ref.dtype)
            rstd_ref[...] = jnp.broadcast_to(rstd, rstd_ref.shape)

    row = lambda i, k: (i, 0)
    vec = pl.BlockSpec((1, D_MODEL), lambda i, k: (0, 0))
    return _pcall(
        body, name=name, grid=(T // tm, nk),
        in_specs=[pl.BlockSpec((tm, D_MODEL), row), pl.BlockSpec((tm, ks * fs), lambda i, k: (i, k)),
                  pl.BlockSpec((ks, fs, D_MODEL), lambda i, k: (k, 0, 0)), vec, vec],
        out_specs=[pl.BlockSpec((tm, D_MODEL), row), pl.BlockSpec((tm, D_MODEL), row), pl.BlockSpec((tm, LANES), row)],
        out_shape=[jax.ShapeDtypeStruct((T, D_MODEL), f32), jax.ShapeDtypeStruct((T, D_MODEL), MXU_DTYPE),
                   jax.ShapeDtypeStruct((T, LANES), f32)],
        scratch_shapes=[pltpu.VMEM((tm, D_MODEL), f32)],
        compiler_params=_params(2),
    )(x, hact, wd, ln_g, ln_b)


def _ffn_bwd(dyp, xb, gact, uact, wg, wu, wd, after=None, *, name, tm=512, tf=512, part=None, dx_init=None):
    T = dyp.shape[0]
    tm = min(tm, T)
    fs = D_FF // N_SHARD
    cpf = fs // tf
    nt = T // tm
    nf = D_FF // tf if part is None else N_SHARD
    wf = fs if part is None else tf
    slab = (lambda f: f // cpf) if part is None else (lambda f: f)
    chunk = (lambda f: f % cpf) if part is None else (lambda f: part)
    extra = ([] if dx_init is None else [dx_init]) + ([] if after is None else [after])

    def body(dyp_ref, xb_ref, g_ref, u_ref, wg_ref, wu_ref, wd_ref, *refs):
        dx_hbm, dwg_ref, dwu_ref, dwd_ref, dx_sc, dwg_sc, dwu_sc, dwd_sc, sem = refs[len(extra):]
        f = pl.program_id(0)
        i = pl.program_id(1)
        rows = pl.ds(pl.multiple_of(i * tm, tm), tm)
        dyp_t = dyp_ref[...]
        dy = (0.5 * dyp_t).astype(MXU_DTYPE)

        @pl.when(i == 0)
        def _():
            dwg_sc[...] = jnp.zeros_like(dwg_sc)
            dwu_sc[...] = jnp.zeros_like(dwu_sc)
            dwd_sc[...] = jnp.zeros_like(dwd_sc)

        @pl.when(f == 0)
        def _():
            dx_sc[rows, :] = DN_ALPHA * dyp_t if dx_init is None else refs[0][...]

        g = g_ref[...].astype(f32)
        u = u_ref[...].astype(f32)
        sig = _sigmoid(g)
        silu = g * sig
        dh = _dot_nt(dy, wd_ref[...])
        dg = (dh * u * (sig * (1.0 + g * (1.0 - sig)))).astype(MXU_DTYPE)
        du = (dh * silu).astype(MXU_DTYPE)
        hb = (silu * u).astype(MXU_DTYPE)
        dx_sc[rows, :] += _dot_nt(dg, wg_ref[...]) + _dot_nt(du, wu_ref[...])
        xb_t = xb_ref[...]
        dwg_sc[...] += _dot_tn(xb_t, dg)
        dwu_sc[...] += _dot_tn(xb_t, du)
        dwd_sc[...] += _dot_tn(hb, dy)

        @pl.when(i == nt - 1)
        def _():
            dwg_ref[...] = dwg_sc[...].astype(dwg_ref.dtype)
            dwu_ref[...] = dwu_sc[...].astype(dwu_ref.dtype)
            dwd_ref[...] = dwd_sc[...].astype(dwd_ref.dtype)

        @pl.when(jnp.logical_and(f == nf - 1, i == nt - 1))
        def _():
            cp = pltpu.make_async_copy(dx_sc, dx_hbm, sem)
            cp.start()
            cp.wait()

    row = lambda f, i: (i, 0)
    return _pcall(
        body, name=name, grid=(nf, nt),
        in_specs=[
            pl.BlockSpec((tm, D_MODEL), row),
            pl.BlockSpec((tm, D_MODEL), row),
            pl.BlockSpec((tm, tf), lambda f, i: (i, slab(f) * cpf + chunk(f))),
            pl.BlockSpec((tm, tf), lambda f, i: (i, slab(f) * cpf + chunk(f))),
            pl.BlockSpec((None, D_MODEL, tf), lambda f, i: (slab(f), 0, chunk(f))),
            pl.BlockSpec((None, D_MODEL, tf), lambda f, i: (slab(f), 0, chunk(f))),
            pl.BlockSpec((None, tf, D_MODEL), lambda f, i: (slab(f), chunk(f), 0)),
        ] + ([] if dx_init is None else [pl.BlockSpec((tm, D_MODEL), row)])
        + ([] if after is None else [pl.BlockSpec(memory_space=pl.ANY)]),
        out_specs=[
            pl.BlockSpec(memory_space=pl.ANY),
            pl.BlockSpec((None, D_MODEL, tf), lambda f, i: (slab(f), 0, chunk(f) if part is None else 0)),
            pl.BlockSpec((None, D_MODEL, tf), lambda f, i: (slab(f), 0, chunk(f) if part is None else 0)),
            pl.BlockSpec((None, tf, D_MODEL), lambda f, i: (slab(f), chunk(f) if part is None else 0, 0)),
        ],
        out_shape=[
            jax.ShapeDtypeStruct((T, D_MODEL), f32),
            jax.ShapeDtypeStruct((N_SHARD, D_MODEL, wf), GRAD_DTYPE),
            jax.ShapeDtypeStruct((N_SHARD, D_MODEL, wf), GRAD_DTYPE),
            jax.ShapeDtypeStruct((N_SHARD, wf, D_MODEL), GRAD_DTYPE),
        ],
        scratch_shapes=[pltpu.VMEM((T, D_MODEL), f32), pltpu.VMEM((D_MODEL, tf), f32),
                        pltpu.VMEM((D_MODEL, tf), f32), pltpu.VMEM((tf, D_MODEL), f32),
                        pltpu.SemaphoreType.DMA],
        compiler_params=_params(2),
    )(dyp, xb, gact, uact, wg, wu, wd, *extra)


def _proj_in(xn, wp, bfp, *, name, tm=512):
    T = xn.shape[0]
    tm = min(tm, T)
    nt = T // tm

    def body(x_ref, w_ref, b_ref, lxg_ref, fg_ref, qa_ref, ka_ref, va_ref, carry):
        i = pl.program_id(0)

        @pl.when(i == 0)
        def _():
            carry[...] = jnp.zeros_like(carry)

        z = _dot(x_ref[...], w_ref[...])
        lxg_ref[...] = z[:, QKV_W:QKV_W + 2 * LRU_W]
        fg = z[:, QKV_W + 2 * LRU_W:] + b_ref[...]
        fg_ref[...] = fg
        ls = jnp.minimum(fg, 0.0) - jnp.log(1.0 + jnp.exp(-jnp.abs(fg)))
        r = lax.broadcasted_iota(jnp.int32, (tm, tm), 0)
        c = lax.broadcasted_iota(jnp.int32, (tm, tm), 1)
        cum = _tri_dot(jnp.where(r >= c, 1.0, 0.0).astype(jnp.bfloat16), ls) + carry[0:1, :]
        carry[...] = jnp.broadcast_to(cum[tm - 1:tm, :], carry.shape)

        lane = lax.broadcasted_iota(jnp.int32, (tm, LANES), 1)
        low = lane < HEAD_DIM
        ones_q = jnp.where(jnp.logical_and(lane >= AUX + 3, lane < AUX + 6), 1.0, 0.0)
        ones_k = jnp.where(jnp.logical_and(lane >= AUX, lane < AUX + 3), 1.0, 0.0)
        for j in range(HEADS // 2):
            pair = [z[:, t * FOX_W + j * LANES:t * FOX_W + (j + 1) * LANES] for t in range(3)]
            for odd in range(2):
                h = 2 * j + odd
                q, k, v = [_swap_lane_halves(a) if odd else a for a in pair]
                hi, mid, lo = [a.astype(f32) for a in _split3(jnp.broadcast_to(cum[:, h:h + 1], (tm, LANES)))]
                aux_q = jnp.where(lane == AUX, hi, jnp.where(lane == AUX + 1, mid, jnp.where(lane == AUX + 2, lo, ones_q)))
                aux_k = jnp.where(lane == AUX + 3, -hi,
                                  jnp.where(lane == AUX + 4, -mid, jnp.where(lane == AUX + 5, -lo, ones_k)))
                blk = slice(h * LANES, (h + 1) * LANES)
                qa_ref[:, blk] = jnp.where(low, q, aux_q).astype(qa_ref.dtype)
                ka_ref[:, blk] = jnp.where(low, k, aux_k).astype(ka_ref.dtype)
                va_ref[:, blk] = jnp.where(low, v, 1.0).astype(va_ref.dtype)

    row = lambda i: (i, 0)
    const = lambda i: (0, 0)
    return _pcall(
        body, name=name, grid=(nt,),
        in_specs=[pl.BlockSpec((tm, D_MODEL), row), pl.BlockSpec((D_MODEL, Z_PAD), const),
                  pl.BlockSpec((1, LANES), const)],
        out_specs=[pl.BlockSpec((tm, 2 * LRU_W), row), pl.BlockSpec((tm, LANES), row)]
        + [pl.BlockSpec((tm, HEADS * LANES), row)] * 3,
        out_shape=[jax.ShapeDtypeStruct((T, 2 * LRU_W), f32), jax.ShapeDtypeStruct((T, LANES), f32)]
        + [jax.ShapeDtypeStruct((T, HEADS * LANES), MXU_DTYPE)] * 3,
        scratch_shapes=[pltpu.VMEM((8, LANES), f32)],
        compiler_params=_params(1),
    )(xn, wp, bfp)


def _proj_in_bwd(dqa, dka, dva, dlxg, fgb, xn, dyp, wp, xhat, rstd, ln_g, *, name, tm=512):
    T = xn.shape[0]
    tm = min(tm, T)
    nt = T // tm

    def body(dq_ref, dk_ref, dv_ref, dl_ref, fg_ref, x_ref, dyp_ref, w_ref, xhat_ref, rstd_ref, g_ref,
             dpre_ref, dw_hbm, dgam_ref, dbeta_ref, dbf_ref, dw_sc, dw_out, carry, sem):
        i = pl.program_id(0)

        @pl.when(i == 0)
        def _():
            dw_sc[...] = jnp.zeros_like(dw_sc)
            dgam_ref[...] = jnp.zeros_like(dgam_ref)
            dbeta_ref[...] = jnp.zeros_like(dbeta_ref)
            dbf_ref[...] = jnp.zeros_like(dbf_ref)
            carry[...] = jnp.zeros_like(carry)

        lane = lax.broadcasted_iota(jnp.int32, (tm, LANES), 1)
        dc = jnp.zeros((tm, LANES), f32)
        for h in range(HEADS):
            row_sum = dq_ref[:, h * LANES + AUX:h * LANES + AUX + 1]
            col_sum = dk_ref[:, h * LANES + AUX + 3:h * LANES + AUX + 4]
            dc = jnp.where(lane == h, jnp.broadcast_to(row_sum - col_sum, (tm, LANES)), dc)
        r = lax.broadcasted_iota(jnp.int32, (tm, tm), 0)
        c = lax.broadcasted_iota(jnp.int32, (tm, tm), 1)
        dls = _tri_dot(jnp.where(c >= r, 1.0, 0.0).astype(jnp.bfloat16), dc) + carry[0:1, :]
        carry[...] = jnp.broadcast_to(dls[0:1, :], carry.shape)
        dfg = dls * _sigmoid(-fg_ref[...])
        dbf_ref[...] += jnp.sum(dfg, axis=0, keepdims=True)

        low = _low_lanes((tm, LANES))

        def packed(ref):
            pairs = [jnp.where(low, ref[:, (2 * j) * LANES:(2 * j + 1) * LANES],
                               _swap_lane_halves(ref[:, (2 * j + 1) * LANES:(2 * j + 2) * LANES]))
                     for j in range(HEADS // 2)]
            return jnp.concatenate(pairs, axis=1).astype(MXU_DTYPE)

        dz = jnp.concatenate(
            [packed(dq_ref), packed(dk_ref), packed(dv_ref),
             dl_ref[...].astype(MXU_DTYPE), dfg.astype(MXU_DTYPE)], axis=1)
        dx = DN_ALPHA * dyp_ref[...] + _dot_nt(dz, w_ref[...])
        dpre, dgam, dbeta = _ln_backward(dx, xhat_ref[...], rstd_ref[:, 0:1], g_ref[...])
        dpre_ref[...] = dpre
        dgam_ref[...] += dgam
        dbeta_ref[...] += dbeta
        dw_sc[...] += _dot_tn(x_ref[...], dz)

        @pl.when(i == nt - 1)
        def _():
            dw_sc[:, :FOX_W] = dw_sc[:, :FOX_W] * (1.0 / math.sqrt(HEAD_DIM))
            for c0 in range(0, Z_PAD, CAST_COLS):
                dw_out[:, c0:c0 + CAST_COLS] = dw_sc[:, c0:c0 + CAST_COLS].astype(dw_out.dtype)
            cp = pltpu.make_async_copy(dw_out, dw_hbm, sem)
            cp.start()
            cp.wait()

    row = lambda i: (nt - 1 - i, 0)
    const = lambda i: (0, 0)
    return _pcall(
        body, name=name, grid=(nt,),
        in_specs=[pl.BlockSpec((tm, HEADS * LANES), row), pl.BlockSpec((tm, HEADS * LANES), row),
                  pl.BlockSpec((tm, HEADS * LANES), row),
                  pl.BlockSpec((tm, 2 * LRU_W), row), pl.BlockSpec((tm, LANES), row),
                  pl.BlockSpec((tm, D_MODEL), row), pl.BlockSpec((tm, D_MODEL), row),
                  pl.BlockSpec((D_MODEL, Z_PAD), const),
                  pl.BlockSpec((tm, D_MODEL), row), pl.BlockSpec((tm, LANES), row), pl.BlockSpec((1, D_MODEL), const)],
        out_specs=[pl.BlockSpec((tm, D_MODEL), row), pl.BlockSpec(memory_space=pl.ANY),
                   pl.BlockSpec((1, D_MODEL), const), pl.BlockSpec((1, D_MODEL), const), pl.BlockSpec((1, LANES), const)],
        out_shape=[jax.ShapeDtypeStruct((T, D_MODEL), f32), jax.ShapeDtypeStruct((D_MODEL, Z_PAD), GRAD_DTYPE),
                   jax.ShapeDtypeStruct((1, D_MODEL), f32), jax.ShapeDtypeStruct((1, D_MODEL), f32),
                   jax.ShapeDtypeStruct((1, LANES), f32)],
        scratch_shapes=[pltpu.VMEM((D_MODEL, Z_PAD), f32), pltpu.VMEM((D_MODEL, Z_PAD), GRAD_DTYPE),
                        pltpu.VMEM((8, LANES), f32), pltpu.SemaphoreType.DMA],
        compiler_params=_params(1),
    )(dqa, dka, dva, dlxg, fgb, xn, dyp, wp, xhat, rstd, ln_g)


def _split3(x):
    hi = x.astype(jnp.bfloat16)
    r1 = x - hi.astype(f32)
    mid = r1.astype(jnp.bfloat16)
    lo = (r1 - mid.astype(f32)).astype(jnp.bfloat16)
    return hi, mid, lo


def _tri_dot(tri, x):
    hi, mid, lo = _split3(x)
    return _dot(tri, hi) + _dot(tri, mid) + _dot(tri, lo)


FOX_PAD = HEADS * LANES
AUX = HEAD_DIM


def _low_lanes(shape):
    return lax.broadcasted_iota(jnp.int32, shape, 1) < HEAD_DIM


def _swap_lane_halves(x):
    return pltpu.roll(x, HEAD_DIM, 1)


def _future_keys(tq, tk):
    r = lax.broadcasted_iota(jnp.int32, (tq, tk), 0)
    c = lax.broadcasted_iota(jnp.int32, (tq, tk), 1)
    return c > r


def _causal_steps(nq, key_major):
    if key_major:
        pairs = [(qi, ki) for ki in range(nq) for qi in range(ki, nq)]
    else:
        pairs = [(qi, ki) for qi in range(nq) for ki in range(qi + 1)]
    return (jnp.asarray([p[0] for p in pairs], jnp.int32), jnp.asarray([p[1] for p in pairs], jnp.int32))


def _fox_fwd(qa, ka, va, *, name, tq=512, hps=8):
    T = qa.shape[0]
    tq = min(tq, T)
    tk = tq
    nq = T // tq
    rep = tk // LANES
    qi_tab, ki_tab = _causal_steps(nq, key_major=False)

    def body(qi_ref, ki_ref, qa_ref, ka_ref, va_ref, o_ref, lse_ref, m_sc, acc_sc):
        t = pl.program_id(1)
        qi = qi_ref[t]
        ki = ki_ref[t]

        @pl.when(ki == 0)
        def _():
            m_sc[...] = jnp.full_like(m_sc, NEG_BIG)
            acc_sc[...] = jnp.zeros_like(acc_sc)

        def tile(diagonal):
            for h in range(hps):
                blk = slice(h * LANES, (h + 1) * LANES)
                s = _dot_nt(qa_ref[:, blk], ka_ref[:, blk])
                if diagonal:
                    s = jnp.where(_future_keys(tq, tk), NEG_BIG, s)
                m_prev = m_sc[h]
                m_new = jnp.maximum(m_prev, jnp.max(s, axis=1, keepdims=True))
                p = jnp.exp(s - jnp.tile(m_new, (1, rep)))
                acc_sc[h] = jnp.exp(m_prev - m_new) * acc_sc[h] + _dot(p.astype(MXU_DTYPE), va_ref[:, blk])
                m_sc[h] = m_new

        @pl.when(ki < qi)
        def _():
            tile(False)

        @pl.when(ki == qi)
        def _():
            tile(True)
            low = _low_lanes((tq, LANES))
            outs = []
            for h in range(hps):
                acc = acc_sc[h]
                den = _swap_lane_halves(acc)
                outs.append(acc / den)
                lse_ref[h] = m_sc[h] + jnp.log(jnp.where(low, den, acc))
            for p in range(hps // 2):
                o_ref[:, p * LANES:(p + 1) * LANES] = jnp.where(low, outs[2 * p], _swap_lane_halves(outs[2 * p + 1]))

    pair = hps * LANES
    return _pcall(
        body, name=name,
        grid_spec=pltpu.PrefetchScalarGridSpec(
            num_scalar_prefetch=2, grid=(HEADS // hps, qi_tab.shape[0]),
            in_specs=[
                pl.BlockSpec((tq, pair), lambda j, t, qi_ref, ki_ref: (qi_ref[t], j)),
                pl.BlockSpec((tk, pair), lambda j, t, qi_ref, ki_ref: (ki_ref[t], j)),
                pl.BlockSpec((tk, pair), lambda j, t, qi_ref, ki_ref: (ki_ref[t], j)),
            ],
            out_specs=[pl.BlockSpec((tq, pair // 2), lambda j, t, qi_ref, ki_ref: (qi_ref[t], j)),
                       pl.BlockSpec((hps, tq, LANES), lambda j, t, qi_ref, ki_ref: (j, qi_ref[t], 0))],
            scratch_shapes=[pltpu.VMEM((hps, tq, LANES), f32)] * 2),
        out_shape=[jax.ShapeDtypeStruct((T, FOX_W), f32), jax.ShapeDtypeStruct((HEADS, T, LANES), f32)],
        compiler_params=_params(2),
    )(qi_tab, ki_tab, qa, ka, va)


def _fox_bwd(qa, ka, va, doa, lse, drep, *, name, tq=512, hps=4):
    T = qa.shape[0]
    tq = min(tq, T)
    tk = tq
    nq = T // tq
    rep = tk // LANES
    qi_tab, ki_tab = _causal_steps(nq, key_major=True)

    def body(qi_ref, ki_ref, qa_ref, ka_ref, va_ref, doa_ref, lse_ref, d_ref, dqa_ref, dka_ref, dva_ref, dk_sc, dv_sc):
        t = pl.program_id(1)
        qi = qi_ref[t]
        ki = ki_ref[t]
        rows = pl.ds(pl.multiple_of(qi * tq, tq), tq)

        @pl.when(t == 0)
        def _():
            dqa_ref[...] = jnp.zeros_like(dqa_ref)

        @pl.when(qi == ki)
        def _():
            dk_sc[...] = jnp.zeros_like(dk_sc)
            dv_sc[...] = jnp.zeros_like(dv_sc)

        def tile(diagonal):
            for h in range(hps):
                blk = slice(h * LANES, (h + 1) * LANES)
                qh, kh, doh = qa_ref[:, blk], ka_ref[:, blk], doa_ref[:, blk]
                p = jnp.exp(_dot_nt(qh, kh) - jnp.tile(lse_ref[h], (1, rep)))
                if diagonal:
                    p = jnp.where(_future_keys(tq, tk), 0.0, p)
                dp = _dot_nt(doh, va_ref[:, blk])
                ds = (p * (dp - jnp.tile(d_ref[h], (1, rep)))).astype(MXU_DTYPE)
                dv_sc[h] += _dot_tn(p.astype(MXU_DTYPE), doh)
                dk_sc[h] += _dot_tn(ds, qh)
                dqa_ref[rows, blk] += _dot(ds, kh)

        @pl.when(qi > ki)
        def _():
            tile(False)

        @pl.when(qi == ki)
        def _():
            tile(True)

        @pl.when(qi == nq - 1)
        def _():
            for h in range(hps):
                blk = slice(h * LANES, (h + 1) * LANES)
                dka_ref[:, blk] = dk_sc[h]
                dva_ref[:, blk] = dv_sc[h]

    pair = hps * LANES
    q_blk = lambda j, t, qi_ref, ki_ref: (qi_ref[t], j)
    k_blk = lambda j, t, qi_ref, ki_ref: (ki_ref[t], j)
    stat = pl.BlockSpec((hps, tq, LANES), lambda j, t, qi_ref, ki_ref: (j, qi_ref[t], 0))
    return _pcall(
        body, name=name,
        grid_spec=pltpu.PrefetchScalarGridSpec(
            num_scalar_prefetch=2, grid=(HEADS // hps, qi_tab.shape[0]),
            in_specs=[pl.BlockSpec((tq, pair), q_blk), pl.BlockSpec((tk, pair), k_blk), pl.BlockSpec((tk, pair), k_blk),
                      pl.BlockSpec((tq, pair), q_blk), stat, stat],
            out_specs=[pl.BlockSpec((T, pair), lambda j, t, qi_ref, ki_ref: (0, j)),
                       pl.BlockSpec((tk, pair), k_blk), pl.BlockSpec((tk, pair), k_blk)],
            scratch_shapes=[pltpu.VMEM((hps, tk, LANES), f32)] * 2),
        out_shape=[jax.ShapeDtypeStruct((T, FOX_PAD), f32)] * 3,
        compiler_params=_params(2),
    )(qi_tab, ki_tab, qa, ka, va, doa, lse, drep)


GELU_C = math.sqrt(2.0 / math.pi)
GELU_A = 0.044715


def _gelu(x):
    t = jnp.tanh(GELU_C * (x + GELU_A * x * x * x))
    return 0.5 * x * (1.0 + t), t


def _gelu_grad(x, t):
    return 0.5 * (1.0 + t) + 0.5 * x * (1.0 - t * t) * GELU_C * (1.0 + 3.0 * GELU_A * x * x)


EXPM1_SERIES_BELOW = 0.25


def _expm1(x, e):
    series = x * (1.0 + x * (1 / 2 + x * (1 / 6 + x * (1 / 24 + x * (1 / 120 + x * (1 / 720))))))
    return jnp.where(x > -EXPM1_SERIES_BELOW, series, e - 1.0)


def _softplus_neg(lam):
    return jnp.maximum(-lam, 0.0) + jnp.log(1.0 + jnp.exp(-jnp.abs(lam)))


def _lru_gates(u, wab_ref, bab_ref, lam_ref):
    pre = _dot(u.astype(MXU_DTYPE), wab_ref[...]) + bab_ref[...]
    r = _sigmoid(pre[:, :LRU_W])
    gi = _sigmoid(pre[:, LRU_W:])
    sp = _softplus_neg(lam_ref[...])
    log_a = -LRU_C * r * sp
    a = jnp.exp(log_a)
    s = jnp.sqrt(-_expm1(2.0 * log_a, a * a))
    return r, gi, sp, a, s


def _lru_fwd(lxg, conv_w, conv_b, wab, bab, lam, *, name, tc=512):
    T = lxg.shape[0]
    tc = min(tc, T)
    nc = T // tc

    def body(lx_ref, lg_ref, cw_ref, cb_ref, wab_ref, bab_ref, lam_ref,
             out_ref, u_ref, hs_ref, gates_ref, ext, a_sc, b_sc, h_sc):
        i = pl.program_id(0)

        @pl.when(i == 0)
        def _():
            ext[0:8, :] = jnp.zeros((8, LRU_W), f32)
            h_sc[...] = jnp.zeros_like(h_sc)

        ext[8:, :] = lx_ref[...]
        u = cb_ref[...] + cw_ref[0:1, :] * ext[pl.ds(5, tc), :]
        for k in range(1, CONV_K):
            u = u + cw_ref[k:k + 1, :] * ext[pl.ds(5 + k, tc), :]
        ext[0:8, :] = ext[tc:tc + 8, :]
        u_ref[...] = u
        r, gi, sp, a, s = _lru_gates(u, wab_ref, bab_ref, lam_ref)
        for n, gate in enumerate((r, gi, a, s)):
            gates_ref[:, n * LRU_W:(n + 1) * LRU_W] = gate
        a_sc[...] = a
        b_sc[...] = s * (gi * u)

        def step(t, h):
            h = a_sc[pl.ds(t, 1), :] * h + b_sc[pl.ds(t, 1), :]
            hs_ref[pl.ds(t, 1), :] = h
            return h

        h = lax.fori_loop(0, tc, step, h_sc[0:1, :], unroll=8)
        h_sc[...] = jnp.broadcast_to(h, h_sc.shape)
        gel, _ = _gelu(lg_ref[...])
        out_ref[...] = gel * hs_ref[...]

    row = lambda i: (i, 0)
    const = lambda i: (0, 0)
    return _pcall(
        body, name=name, grid=(nc,),
        in_specs=[pl.BlockSpec((tc, LRU_W), row), pl.BlockSpec((tc, LRU_W), lambda i: (i, 1)),
                  pl.BlockSpec((CONV_K, LRU_W), const), pl.BlockSpec((1, LRU_W), const),
                  pl.BlockSpec((LRU_W, 2 * LRU_W), const), pl.BlockSpec((1, 2 * LRU_W), const),
                  pl.BlockSpec((1, LRU_W), const)],
        out_specs=[pl.BlockSpec((tc, LRU_W), row)] * 3 + [pl.BlockSpec((tc, 4 * LRU_W), row)],
        out_shape=[jax.ShapeDtypeStruct((T, LRU_W), f32)] * 3 + [jax.ShapeDtypeStruct((T, 4 * LRU_W), f32)],
        scratch_shapes=[pltpu.VMEM((tc + 8, LRU_W), f32), pltpu.VMEM((tc, LRU_W), f32),
                        pltpu.VMEM((tc, LRU_W), f32), pltpu.VMEM((8, LRU_W), f32)],
        compiler_params=_params(1),
    )(lxg, lxg, conv_w, conv_b, wab, bab, lam)


def _lru_bwd(dlru, lxg, u, hs, gates, conv_w, wab, lam, *, name, tc=512):
    T = lxg.shape[0]
    tc = min(tc, T)
    nc = T // tc
    bp = tc // 8

    def body(dl_ref, lx_ref, lxp_ref, lg_ref, u_ref, hs_ref, hsp_ref, gates_ref, cw_ref, wab_ref, lam_ref,
             dlxg_ref, dwab_ref, dbab_ref, dcw_ref, dcb_ref, dlam_ref,
             dh_sc, a_sc, ext, du_ext, carry):
        i = pl.program_id(0)
        first_chunk = i == nc - 1

        @pl.when(i == 0)
        def _():
            dwab_ref[...] = jnp.zeros_like(dwab_ref)
            dbab_ref[...] = jnp.zeros_like(dbab_ref)
            dcw_ref[...] = jnp.zeros_like(dcw_ref)
            dcb_ref[...] = jnp.zeros_like(dcb_ref)
            dlam_ref[...] = jnp.zeros_like(dlam_ref)
            carry[...] = jnp.zeros_like(carry)
            du_ext[tc:tc + 8, :] = jnp.zeros((8, LRU_W), f32)

        lg = lg_ref[...]
        gel, th = _gelu(lg)
        dl = dl_ref[...]
        hs = hs_ref[...]
        dlg = dl * hs * _gelu_grad(lg, th)
        u = u_ref[...]
        r, gi, a, s = [gates_ref[:, n * LRU_W:(n + 1) * LRU_W] for n in range(4)]
        sp = _softplus_neg(lam_ref[...])
        a_sc[...] = a
        dh_sc[...] = dl * gel

        def step(k, c):
            t = tc - 1 - k
            dh = dh_sc[pl.ds(t, 1), :] + c
            dh_sc[pl.ds(t, 1), :] = dh
            return a_sc[pl.ds(t, 1), :] * dh

        c = lax.fori_loop(0, tc, step, carry[0:1, :], unroll=8)
        carry[...] = jnp.broadcast_to(c, carry.shape)

        ext[0:8, :] = jnp.where(first_chunk, 0.0, hsp_ref[...])
        ext[8:, :] = hs
        hprev = ext[pl.ds(7, tc), :]
        dh = dh_sc[...]
        da = dh * hprev
        giu = gi * u
        dla = da * a - (dh * giu) * (a * a / s)
        dgi = dh * s * u
        du = dh * s * gi
        dr = dla * (-LRU_C * sp)
        dlam_ref[...] += jnp.sum(dla * (-LRU_C * r), axis=0, keepdims=True) * (-_sigmoid(-lam_ref[...]))
        dpre = jnp.concatenate([dr * r * (1.0 - r), dgi * gi * (1.0 - gi)], axis=1)
        dpre_b = dpre.astype(MXU_DTYPE)
        du = du + _dot_nt(dpre_b, wab_ref[...])
        dwab_ref[...] += _dot_tn(u.astype(MXU_DTYPE), dpre_b)
        dbab_ref[...] += jnp.sum(dpre, axis=0, keepdims=True)
        dcb_ref[...] += jnp.sum(du, axis=0, keepdims=True)

        du_ext[0:tc, :] = du
        dlx = cw_ref[0:1, :] * du_ext[pl.ds(3, tc), :]
        for k in range(1, CONV_K):
            dlx = dlx + cw_ref[k:k + 1, :] * du_ext[pl.ds(3 - k, tc), :]
        du_ext[tc:tc + 8, :] = du_ext[0:8, :]
        ext[0:8, :] = jnp.where(first_chunk, 0.0, lxp_ref[...])
        ext[8:, :] = lx_ref[...]
        for k in range(CONV_K):
            dcw_ref[k:k + 1, :] += jnp.sum(du * ext[pl.ds(5 + k, tc), :], axis=0, keepdims=True)
        dlxg_ref[:, :LRU_W] = dlx.astype(dlxg_ref.dtype)
        dlxg_ref[:, LRU_W:] = dlg.astype(dlxg_ref.dtype)

    rev = lambda i: (nc - 1 - i, 0)
    prev8 = lambda i: (jnp.maximum((nc - 1 - i) * bp - 1, 0), 0)
    const = lambda i: (0, 0)
    return _pcall(
        body, name=name, grid=(nc,),
        in_specs=[
            pl.BlockSpec((tc, LRU_W), rev),
            pl.BlockSpec((tc, LRU_W), rev),
            pl.BlockSpec((8, LRU_W), prev8),
            pl.BlockSpec((tc, LRU_W), lambda i: (nc - 1 - i, 1)),
            pl.BlockSpec((tc, LRU_W), rev),
            pl.BlockSpec((tc, LRU_W), rev),
            pl.BlockSpec((8, LRU_W), prev8),
            pl.BlockSpec((tc, 4 * LRU_W), rev),
            pl.BlockSpec((CONV_K, LRU_W), const),
            pl.BlockSpec((LRU_W, 2 * LRU_W), const),
            pl.BlockSpec((1, LRU_W), const),
        ],
        out_specs=[
            pl.BlockSpec((tc, 2 * LRU_W), rev),
            pl.BlockSpec((LRU_W, 2 * LRU_W), const),
            pl.BlockSpec((1, 2 * LRU_W), const),
            pl.BlockSpec((8, LRU_W), const),
            pl.BlockSpec((1, LRU_W), const),
            pl.BlockSpec((1, LRU_W), const),
        ],
        out_shape=[
            jax.ShapeDtypeStruct((T, 2 * LRU_W), MXU_DTYPE),
            jax.ShapeDtypeStruct((LRU_W, 2 * LRU_W), f32),
            jax.ShapeDtypeStruct((1, 2 * LRU_W), f32),
            jax.ShapeDtypeStruct((8, LRU_W), f32),
            jax.ShapeDtypeStruct((1, LRU_W), f32),
            jax.ShapeDtypeStruct((1, LRU_W), f32),
        ],
        scratch_shapes=[pltpu.VMEM((tc, LRU_W), f32), pltpu.VMEM((tc, LRU_W), f32),
                        pltpu.VMEM((tc + 8, LRU_W), f32), pltpu.VMEM((tc + 8, LRU_W), f32),
                        pltpu.VMEM((8, LRU_W), f32)],
        compiler_params=_params(1),
    )(dlru, lxg, lxg, lxg, u, hs, hs, gates, conv_w, wab, lam)


def _mix_out(fox, lru, wo, xhat1, g1, b1, g2, b2, *, name, tm=512):
    T = fox.shape[0]
    tm = min(tm, T)
    nt = T // tm

    def body(fox_ref, lru_ref, wo_ref, xh_ref, g1_ref, b1_ref, g2_ref, b2_ref, xhat_ref, xn_ref, rstd_ref):
        mix = _dot(fox_ref[...].astype(MXU_DTYPE), wo_ref[:FOX_W, :])
        mix = mix + _dot(lru_ref[...].astype(MXU_DTYPE), wo_ref[FOX_W:, :])
        x1 = xh_ref[...] * g1_ref[...] + b1_ref[...]
        xhat, rstd = _layer_norm_stats(DN_ALPHA * x1 + mix)
        xhat_ref[...] = xhat
        xn_ref[...] = xhat * g2_ref[...] + b2_ref[...]
        rstd_ref[...] = jnp.broadcast_to(rstd, rstd_ref.shape)

    row = lambda i: (i, 0)
    const = lambda i: (0, 0)
    vec = pl.BlockSpec((1, D_MODEL), const)
    return _pcall(
        body, name=name, grid=(nt,),
        in_specs=[pl.BlockSpec((tm, FOX_W), row), pl.BlockSpec((tm, LRU_W), row),
                  pl.BlockSpec((D_MODEL, D_MODEL), const), pl.BlockSpec((tm, D_MODEL), row), vec, vec, vec, vec],
        out_specs=[pl.BlockSpec((tm, D_MODEL), row), pl.BlockSpec((tm, D_MODEL), row),
                   pl.BlockSpec((tm, LANES), row)],
        out_shape=[jax.ShapeDtypeStruct((T, D_MODEL), f32), jax.ShapeDtypeStruct((T, D_MODEL), f32),
                   jax.ShapeDtypeStruct((T, LANES), f32)],
        compiler_params=_params(1),
    )(fox, lru, wo, xhat1, g1, b1, g2, b2)


def _mix_out_bwd(dy, xhat, rstd, ln_g, fox, lru, wo, *, name, tm=512):
    T = fox.shape[0]
    tm = min(tm, T)
    nt = T // tm

    def body(dy_ref, xhat_ref, rstd_ref, g_ref, fox_ref, lru_ref, wo_ref,
             dyp_ref, dgam_ref, dbeta_ref, dlru_ref, dwo_ref, d_ref, doa_ref):
        i = pl.program_id(0)

        @pl.when(i == 0)
        def _():
            dwo_ref[...] = jnp.zeros_like(dwo_ref)
            dgam_ref[...] = jnp.zeros_like(dgam_ref)
            dbeta_ref[...] = jnp.zeros_like(dbeta_ref)

        dyp, dgam, dbeta = _ln_backward(dy_ref[...], xhat_ref[...], rstd_ref[:, 0:1], g_ref[...])
        dyp_ref[...] = dyp
        dgam_ref[...] += dgam
        dbeta_ref[...] += dbeta
        dmix = dyp.astype(MXU_DTYPE)
        dcat = _dot_nt(dmix, wo_ref[...])
        dlru_ref[...] = dcat[:, FOX_W:]
        low = _low_lanes((tm, LANES))
        for j in range(HEADS // 2):
            do2 = dcat[:, j * LANES:(j + 1) * LANES].astype(MXU_DTYPE).astype(f32)
            prod = do2 * fox_ref[:, j * LANES:(j + 1) * LANES]
            for odd in range(2):
                h = 2 * j + odd
                mine = jnp.where(low, _swap_lane_halves(prod) if odd else prod, 0.0)
                d_ref[h] = jnp.broadcast_to(jnp.sum(mine, axis=1, keepdims=True), (tm, LANES))
                doh = jnp.where(low, _swap_lane_halves(do2) if odd else do2, 0.0)
                doa_ref[:, h * LANES:(h + 1) * LANES] = doh.astype(doa_ref.dtype)
        dwo_ref[:FOX_W, :] += _dot_tn(fox_ref[...].astype(MXU_DTYPE), dmix)
        dwo_ref[FOX_W:, :] += _dot_tn(lru_ref[...].astype(MXU_DTYPE), dmix)

    row = lambda i: (i, 0)
    const = lambda i: (0, 0)
    return _pcall(
        body, name=name, grid=(nt,),
        in_specs=[pl.BlockSpec((tm, D_MODEL), row), pl.BlockSpec((tm, D_MODEL), row), pl.BlockSpec((tm, LANES), row),
                  pl.BlockSpec((1, D_MODEL), const),
                  pl.BlockSpec((tm, FOX_W), row), pl.BlockSpec((tm, LRU_W), row),
                  pl.BlockSpec((D_MODEL, D_MODEL), const)],
        out_specs=[pl.BlockSpec((tm, D_MODEL), row), pl.BlockSpec((1, D_MODEL), const), pl.BlockSpec((1, D_MODEL), const),
                   pl.BlockSpec((tm, LRU_W), row), pl.BlockSpec((D_MODEL, D_MODEL), const),
                   pl.BlockSpec((HEADS, tm, LANES), lambda i: (0, i, 0)), pl.BlockSpec((tm, HEADS * LANES), row)],
        out_shape=[jax.ShapeDtypeStruct((T, D_MODEL), f32), jax.ShapeDtypeStruct((1, D_MODEL), f32),
                   jax.ShapeDtypeStruct((1, D_MODEL), f32),
                   jax.ShapeDtypeStruct((T, LRU_W), f32), jax.ShapeDtypeStruct((D_MODEL, D_MODEL), f32),
                   jax.ShapeDtypeStruct((HEADS, T, LANES), f32), jax.ShapeDtypeStruct((T, HEADS * LANES), MXU_DTYPE)],
        compiler_params=_params(1),
    )(dy, xhat, rstd, ln_g, fox, lru, wo)


def make_wp(w_in):
    scale = jnp.concatenate([jnp.full((FOX_W,), 1.0 / math.sqrt(HEAD_DIM), w_in.dtype),
                             jnp.ones((IN_COLS - FOX_W,), w_in.dtype)])
    return jnp.pad(w_in * scale[None, :], ((0, 0), (0, Z_PAD - IN_COLS)))


def _block_diag(w):
    eye = jnp.eye(HEADS, dtype=w.dtype)
    return jnp.einsum("hij,hg->higj", w, eye).reshape(LRU_W, LRU_W)


def _block_diag_extract(m):
    m4 = m.reshape(HEADS, HEAD_DIM, HEADS, HEAD_DIM)
    return jnp.stack([m4[h, :, h, :] for h in range(HEADS)])


class _NoOverlap:
    def start_token(self):
        return None

    def late_weights(self, w, after):
        return dict(f1d=w["f1d"], wp=w["wp"], wo=w["wo"])

    def after_attention(self, after):
        return None

    def ffn2_weights(self, w, after):
        return w["f2g"], w["f2u"], w["f2d"]

    def ffn2_grads(self, grads):
        return None

    def ffn1_grads(self, grads):
        return None

    def mixer_grads(self, dwp, dwo, small, loss):
        return None

    def before_ffn1_bwd(self, after):
        return None


def _tied(a, token):
    return a if token is None else a + token[0, 0]


def _local_step(x, target, w, hooks=None):
    hooks = hooks or _NoOverlap()
    bfp = w["bfp"]
    wab = jnp.concatenate([_block_diag(w["rg_wa"]), _block_diag(w["rg_wx"])], axis=1).astype(MXU_DTYPE)
    bab = jnp.concatenate([w["rg_ba"].reshape(1, LRU_W), w["rg_bx"].reshape(1, LRU_W)], axis=1)

    xb0, g1a, u1a, h1a = _ffn_up(x, w["f1g"], w["f1u"], hooks.start_token(), name="ffn1_up")
    late = hooks.late_weights(w, [h1a])
    f1d, wp, wo = late["f1d"], late["wp"], late["wo"]
    xhat1, xn1, rstd1 = _ffn_down_ln(x, h1a, f1d, w["ln1_g"], w["ln1_b"], name="ffn1_down")
    lxg, fgb, qa, ka, va = _proj_in(xn1, wp, bfp, name="proj_in")
    fox, lse = _fox_fwd(qa, ka, va, name="fox_fwd")
    token = hooks.after_attention([lse])
    lru, uconv, hs, gates = _lru_fwd(lxg, w["conv_w"], _tied(w["conv_b"], token), wab, bab, w["lam"], name="lru_fwd")
    xhat2, x2, rstd2 = _mix_out(fox, lru, wo, xhat1, w["ln1_g"], w["ln1_b"], w["ln2_g"], w["ln2_b"], name="mix_out")
    f2g, f2u, f2d = hooks.ffn2_weights(w, [rstd2])
    xb2, g2a, u2a, dy3p, dln3g, dln3b, loss = _ffn_fwd_loss(x2, f2g, f2u, f2d, w["ln3_g"], w["ln3_b"], target,
                                                            name="ffn2_fwd_loss")

    dx2, df2g, df2u, df2d = _ffn_bwd(dy3p, xb2, g2a, u2a, f2g, f2u, f2d, name="ffn2_bwd")
    token = hooks.ffn2_grads([df2g, df2u, df2d])
    dy2p, dln2g, dln2b, dlru, dwo, drep, doa = _mix_out_bwd(dx2, xhat2, rstd2, _tied(w["ln2_g"], token), fox, lru, wo,
                                                            name="mix_out_bwd")
    dlxg, dwab, dbab, dcw, dcb, dlam = _lru_bwd(dlru, lxg, uconv, hs, gates, w["conv_w"], wab, w["lam"], name="lru_bwd")
    dqa, dka, dva = _fox_bwd(qa, ka, va, doa, lse, drep, name="fox_bwd")
    dy1p, dwp, dln1g, dln1b, dbf = _proj_in_bwd(dqa, dka, dva, dlxg, fgb, xn1, dy2p, wp, xhat1, rstd1, w["ln1_g"],
                                                name="proj_in_bwd")
    small = dict(
        ln1_g=dln1g, ln1_b=dln1b, ln2_g=dln2g, ln2_b=dln2b, ln3_g=dln3g, ln3_b=dln3b,
        b_forget=dbf[:, :HEADS], conv_w=dcw[:CONV_K], conv_b=dcb,
        rg_wa=_block_diag_extract(dwab[:, :LRU_W]), rg_wx=_block_diag_extract(dwab[:, LRU_W:]),
        rg_ba=dbab[:, :LRU_W].reshape(HEADS, HEAD_DIM), rg_bx=dbab[:, LRU_W:].reshape(HEADS, HEAD_DIM),
        lru_lambda=dlam,
    )
    hooks.before_ffn1_bwd([dln1b])
    token = hooks.mixer_grads(dwp, dwo, small, loss)
    dx_a, *grads_a = _ffn_bwd(dy1p, xb0, g1a, u1a, w["f1g"], w["f1u"], f1d, token, name="ffn1_bwd_a", part=0)
    token = hooks.ffn1_grads(grads_a)
    dx, *grads_b = _ffn_bwd(dy1p, xb0, g1a, u1a, w["f1g"], w["f1u"], f1d, token, name="ffn1_bwd_b", part=1,
                            dx_init=dx_a)

    grads = dict(f1=(grads_a, grads_b), f2g=df2g, f2u=df2u, f2d=df2d, wp=dwp, wo=dwo, **small)
    return loss, dx, grads


MESH = pl.DeviceIdType.MESH
HBM_SPEC = pl.BlockSpec(memory_space=pl.ANY)
VMEM_SPEC = pl.BlockSpec(memory_space=pltpu.VMEM)


def _position():
    return lax.axis_index("x"), lax.axis_index("y"), lax.axis_index("c")


def _other_chips(x, y):
    return [(1 - x, y), (x, 1 - y), (1 - x, 1 - y)]


def _all_gather_bf16(shards, *, name):
    n = len(shards)

    def body(*refs):
        ins, outs, stages = refs[:n], refs[n:2 * n], refs[2 * n:3 * n]
        send_sems, recv_sems, local_sems = refs[3 * n:]
        x, y, c = _position()
        me, sibling = (x, y, c), (x, y, 1 - c)
        chips = _other_chips(x, y)

        def rows(k, px, py, pc):
            r = shards[k].shape[0]
            m = r // 2
            return outs[k].at[pl.ds(pl.multiple_of((2 * px + py) * r + pc * m, 16), m), :]

        def copy(k, idx, block, to, src=None):
            return pltpu.make_async_remote_copy(
                src_ref=rows(k, *block) if src is None else src, dst_ref=rows(k, *block),
                send_sem=send_sems.at[7 * k + idx], recv_sem=recv_sems.at[7 * k + idx],
                device_id=to, device_id_type=MESH)

        started = []
        mine = []
        for k in range(n):
            m = shards[k].shape[0] // 2
            stages[k][...] = ins[k][pl.ds(pl.multiple_of(c * m, 16), m), :].astype(stages[k].dtype)
            cp = pltpu.make_async_copy(stages[k], rows(k, *me), local_sems.at[k])
            cp.start()
            mine.append(cp)
            first = [copy(k, 0, me, sibling, src=stages[k])]
            first += [copy(k, 1 + j, me, (*chip, c), src=stages[k]) for j, chip in enumerate(chips)]
            for cp in first:
                cp.start()
            started += first
        for k in range(n):
            for j, chip in enumerate(chips):
                copy(k, 1 + j, (*chip, c), me).wait_recv()
                fwd = copy(k, 4 + j, (*chip, c), sibling)
                fwd.start()
                started.append(fwd)
        for k in range(n):
            copy(k, 0, sibling, me).wait_recv()
            for j, chip in enumerate(chips):
                copy(k, 4 + j, (*chip, 1 - c), me).wait_recv()
        for cp in started:
            cp.wait_send()
        for cp in mine:
            cp.wait()

    return _pcall(
        body, name=name,
        in_specs=[VMEM_SPEC] * n, out_specs=[HBM_SPEC] * n,
        out_shape=[jax.ShapeDtypeStruct((N_SHARD * s.shape[0], s.shape[1]), MXU_DTYPE) for s in shards],
        scratch_shapes=[pltpu.VMEM((s.shape[0] // 2, s.shape[1]), MXU_DTYPE) for s in shards]
        + [pltpu.SemaphoreType.DMA((7 * n,)), pltpu.SemaphoreType.DMA((7 * n,)), pltpu.SemaphoreType.DMA((n,))],
        compiler_params=pltpu.CompilerParams(vmem_limit_bytes=VMEM_LIMIT),
    )(*shards)


def _swap_halves(gs, *, name):
    n = len(gs)

    def body(*refs):
        ins, outs = refs[:n], refs[n:2 * n]
        send_sems, recv_sems = refs[2 * n:]
        x, y, c = _position()
        cps = []
        for k in range(n):
            m = gs[k].shape[1] // 2
            src = ins[k].at[:, pl.ds(pl.multiple_of((1 - c) * m, 16), m), :]
            cp = pltpu.make_async_remote_copy(src_ref=src, dst_ref=outs[k], send_sem=send_sems.at[k],
                                              recv_sem=recv_sems.at[k], device_id=(x, y, 1 - c), device_id_type=MESH)
            cp.start()
            cps.append(cp)
        for cp in cps:
            cp.wait()

    return _pcall(
        body, name=name, in_specs=[HBM_SPEC] * n, out_specs=[HBM_SPEC] * n,
        out_shape=[jax.ShapeDtypeStruct((g.shape[0], g.shape[1] // 2, g.shape[2]), g.dtype) for g in gs],
        scratch_shapes=[pltpu.SemaphoreType.DMA((n,)), pltpu.SemaphoreType.DMA((n,))],
    )(*gs)


def _add_halves(gs, recvs, *, name, tm=256):
    n = len(gs)
    _, r, cdim = gs[0].shape
    m = r // 2
    tm = min(tm, m)
    nb = m // tm
    c_idx = lax.axis_index("c").astype(jnp.int32).reshape(1)

    def body(c_ref, *refs):
        for k in range(n):
            refs[2 * n + k][...] = (refs[k][...].astype(f32) + refs[n + k][...].astype(f32)).astype(refs[2 * n + k].dtype)

    mine = pl.BlockSpec((None, tm, cdim), lambda j, i, c_ref: (j, c_ref[0] * nb + i, 0))
    half = pl.BlockSpec((None, tm, cdim), lambda j, i, c_ref: (j, i, 0))
    return _pcall(
        body, name=name,
        grid_spec=pltpu.PrefetchScalarGridSpec(
            num_scalar_prefetch=1, grid=(N_SHARD, nb),
            in_specs=[mine] * n + [half] * n, out_specs=[half] * n),
        out_shape=[jax.ShapeDtypeStruct((N_SHARD, m, cdim), g.dtype) for g in gs],
        compiler_params=_params(2),
    )(c_idx, *gs, *recvs)


def _scatter_partials(ps, *, name):
    n = len(ps)

    def body(*refs):
        ins, outs = refs[:n], refs[n:2 * n]
        send_sems, recv_sems = refs[2 * n:]
        x, y, c = _position()
        me_chip = 2 * x + y
        cps = []
        for k in range(n):
            for j, (px, py) in enumerate(_other_chips(x, y)):
                cp = pltpu.make_async_remote_copy(
                    src_ref=ins[k].at[2 * px + py], dst_ref=outs[k].at[me_chip],
                    send_sem=send_sems.at[3 * k + j], recv_sem=recv_sems.at[3 * k + j],
                    device_id=(px, py, c), device_id_type=MESH)
                cp.start()
                cps.append(cp)
        for cp in cps:
            cp.wait()

    return _pcall(
        body, name=name, in_specs=[HBM_SPEC] * n, out_specs=[HBM_SPEC] * n,
        out_shape=[jax.ShapeDtypeStruct(p.shape, p.dtype) for p in ps],
        scratch_shapes=[pltpu.SemaphoreType.DMA((3 * n,)), pltpu.SemaphoreType.DMA((3 * n,))],
    )(*ps)


def _sum_slabs(ps, qs, *, name, tm=128):
    n = len(qs)
    _, m, cdim = qs[0].shape
    tm = min(tm, m)
    nb = m // tm
    assert m % tm == 0, (m, tm)
    where = jnp.stack([2 * lax.axis_index("x") + lax.axis_index("y"), lax.axis_index("c")]).astype(jnp.int32)

    def body(w_ref, *refs):
        for k in range(n):
            own, q1, q2, q3 = (refs[4 * k + t][...].astype(f32) for t in range(4))
            refs[4 * n + k][...] = ((own + q1) + q2) + q3

    def slab(flip):
        return pl.BlockSpec((None, tm, cdim), lambda i, w_ref: (jnp.bitwise_xor(w_ref[0], flip), i, 0))

    operands = []
    for p, q in zip(ps, qs):
        operands += [p, q, q, q]
    return _pcall(
        body, name=name,
        grid_spec=pltpu.PrefetchScalarGridSpec(
            num_scalar_prefetch=1, grid=(nb,),
            in_specs=[slab(0), slab(2), slab(1), slab(3)] * n,
            out_specs=[pl.BlockSpec((tm, cdim), lambda i, w_ref: (w_ref[1] * nb + i, 0))] * n),
        out_shape=[jax.ShapeDtypeStruct((2 * m, cdim), f32) for _ in qs],
        compiler_params=_params(1),
    )(where, *operands)


def _join_halves(fs, *, name):
    n = len(fs)

    def body(*refs):
        outs = refs[n:2 * n]
        send_sems, recv_sems = refs[2 * n:]
        x, y, c = _position()
        cps = []
        for k in range(n):
            m = fs[k].shape[0] // 2
            half = outs[k].at[pl.ds(pl.multiple_of(c * m, 8), m), :]
            cp = pltpu.make_async_remote_copy(src_ref=half, dst_ref=half, send_sem=send_sems.at[k],
                                              recv_sem=recv_sems.at[k], device_id=(x, y, 1 - c), device_id_type=MESH)
            cp.start()
            cps.append(cp)
        for cp in cps:
            cp.wait()

    return _pcall(
        body, name=name, in_specs=[HBM_SPEC] * n, out_specs=[HBM_SPEC] * n,
        out_shape=[jax.ShapeDtypeStruct(f.shape, f.dtype) for f in fs],
        input_output_aliases={k: k for k in range(n)},
        scratch_shapes=[pltpu.SemaphoreType.DMA((n,)), pltpu.SemaphoreType.DMA((n,))],
    )(*fs)


def _all_reduce_small(v, after=None, *, name):
    r = v.shape[0]
    extra = [] if after is None else [after]

    def body(v_ref, *refs):
        out_ref, buf, send_sems, recv_sems, local_sem = refs[len(extra):]
        x, y, c = _position()
        me, sibling = (x, y, c), (x, y, 1 - c)
        chips = _other_chips(x, y)

        def rows(px, py, pc):
            return buf.at[pl.ds(pl.multiple_of((4 * px + 2 * py + pc) * r, 8), r), :]

        def copy(k, block, to, src=None):
            return pltpu.make_async_remote_copy(
                src_ref=rows(*block) if src is None else src, dst_ref=rows(*block),
                send_sem=send_sems.at[k], recv_sem=recv_sems.at[k], device_id=to, device_id_type=MESH)

        mine = pltpu.make_async_copy(v_ref, rows(*me), local_sem)
        mine.start()
        first = [copy(0, me, sibling, src=v_ref)]
        first += [copy(1 + j, me, (*chip, c), src=v_ref) for j, chip in enumerate(chips)]
        for cp in first:
            cp.start()
        passed = [copy(4 + j, (*chip, c), sibling) for j, chip in enumerate(chips)]
        for j, chip in enumerate(chips):
            copy(1 + j, (*chip, c), me).wait_recv()
            passed[j].start()
        copy(0, sibling, me).wait_recv()
        for j, chip in enumerate(chips):
            copy(4 + j, (*chip, 1 - c), me).wait_recv()
        for cp in first + passed:
            cp.wait_send()
        mine.wait()
        acc = buf[0:r, :]
        for d in range(1, N_DEV):
            acc = acc + buf[d * r:(d + 1) * r, :]
        out_ref[...] = acc

    return _pcall(
        body, name=name, in_specs=[VMEM_SPEC] + [HBM_SPEC] * len(extra), out_specs=VMEM_SPEC,
        out_shape=jax.ShapeDtypeStruct((r, LANES), f32),
        scratch_shapes=[pltpu.VMEM((N_DEV * r, LANES), f32), pltpu.SemaphoreType.DMA((7,)),
                        pltpu.SemaphoreType.DMA((7,)), pltpu.SemaphoreType.DMA],
    )(v, *extra)


SEM_SPEC = pl.BlockSpec(memory_space=pltpu.SEMAPHORE)
HBM_ONLY = pl.BlockSpec(memory_space=pltpu.HBM)
EFFECT = pltpu.SideEffectType.DATAFLOW_SIDE_EFFECTING


def _sends(copies):
    return copies[0] if isinstance(copies, tuple) else copies


def _arrivals(copies):
    return copies[1] if isinstance(copies, tuple) else copies


def _split_start(bufs, copies_fn, n_sems, *, name):
    n = len(bufs)

    def body(*refs):
        send_sems, recv_sems = refs[n], refs[n + 1]
        thru = refs[n + 2:2 * n + 2]
        token = refs[2 * n + 2]
        for cp in _sends(copies_fn(thru, send_sems, recv_sems)):
            cp.start()
        token[...] = jnp.zeros_like(token)

    outs = _pcall(
        body, name=name,
        out_shape=(pltpu.SemaphoreType.DMA((n_sems,)), pltpu.SemaphoreType.DMA((n_sems,)),
                   *[pltpu.HBM(b.shape, b.dtype) for b in bufs], jax.ShapeDtypeStruct((8, LANES), f32)),
        in_specs=[HBM_ONLY] * n,
        out_specs=(SEM_SPEC, SEM_SPEC, *[HBM_ONLY] * n, VMEM_SPEC),
        input_output_aliases={k: 2 + k for k in range(n)},
        compiler_params=pltpu.CompilerParams(has_side_effects=EFFECT),
    )(*[pltpu.with_memory_space_constraint(b, pltpu.HBM) for b in bufs])
    return outs[0], outs[1], list(outs[2:2 + n]), outs[2 + n]


def _split_wait(thru, send_sems, recv_sems, after, copies_fn, *, name):
    n = len(thru)

    def body(*refs):
        copies = copies_fn(refs[:n], refs[n], refs[n + 1])
        for cp in _sends(copies):
            cp.wait_send()
        for cp in _arrivals(copies):
            cp.wait_recv()

    return list(_pcall(
        body, name=name,
        out_shape=tuple(pltpu.HBM(b.shape, b.dtype) for b in thru),
        in_specs=[HBM_ONLY] * n + [SEM_SPEC, SEM_SPEC] + [HBM_SPEC] * len(after),
        out_specs=tuple([HBM_ONLY] * n),
        input_output_aliases={k: k for k in range(n)},
        compiler_params=pltpu.CompilerParams(has_side_effects=EFFECT),
    )(*thru, send_sems, recv_sems, *after))


def _scatter_copies(n):
    def copies(bufs, send_sems, recv_sems):
        x, y, c = _position()
        me_chip = 2 * x + y
        cps = []
        for k in range(n):
            for j, (px, py) in enumerate(_other_chips(x, y)):
                cps.append(pltpu.make_async_remote_copy(
                    src_ref=bufs[k].at[2 * px + py], dst_ref=bufs[n + k].at[me_chip],
                    send_sem=send_sems.at[3 * k + j], recv_sem=recv_sems.at[3 * k + j],
                    device_id=(px, py, c), device_id_type=MESH))
        return cps
    return copies


N_PEERS = N_DEV - 1


def _direct_copies(n):
    def copies(bufs, send_sems, recv_sems):
        x, y, c = _position()
        me_chip = 2 * x + y
        sends, arrivals = [], []
        for k in range(n):
            m = bufs[k].shape[1] // 2
            land = bufs[n + k]

            def rows(slab, half, k=k, m=m):
                start = half * m if isinstance(half, int) else pl.multiple_of(half * m, 16)
                return bufs[k].at[slab, pl.ds(start, m), :]

            def copy(src, slot, send_idx, recv_idx, to, k=k, land=land):
                return pltpu.make_async_remote_copy(
                    src_ref=src, dst_ref=land.at[slot], send_sem=send_sems.at[N_PEERS * k + send_idx],
                    recv_sem=recv_sems.at[N_PEERS * k + recv_idx], device_id=to, device_id_type=MESH)

            sends.append(copy(rows(me_chip, 1 - c), 0, 0, 0, (x, y, 1 - c)))
            arrivals.append(copy(rows(me_chip, c), 0, 0, 0, (x, y, 1 - c)))
            for t, (px, py) in enumerate(_other_chips(x, y)):
                for core in range(2):
                    sends.append(copy(rows(2 * px + py, core), 1 + 2 * t + c, 1 + 2 * t + core, 1 + 2 * t + c,
                                      (px, py, core)))
                    arrivals.append(copy(rows(me_chip, c), 1 + 2 * t + core, 1 + 2 * t + core, 1 + 2 * t + core,
                                         (px, py, core)))
        return sends, arrivals
    return copies


def _sum_direct(gs, lands, *, name, tm=128):
    n = len(gs)
    _, m, cdim = lands[0].shape
    tm = min(tm, m)
    nb = m // tm
    assert m % tm == 0, (m, tm)
    where = jnp.stack([2 * lax.axis_index("x") + lax.axis_index("y"), lax.axis_index("c")]).astype(jnp.int32)

    def body(w_ref, *refs):
        for k in range(n):
            acc = refs[2 * k][...].astype(f32)
            for slot in range(N_PEERS):
                acc = acc + refs[2 * k + 1][slot].astype(f32)
            refs[2 * n + k][...] = acc

    own = pl.BlockSpec((None, tm, cdim), lambda i, w_ref: (w_ref[0], w_ref[1] * nb + i, 0))
    landed = pl.BlockSpec((N_PEERS, tm, cdim), lambda i, w_ref: (0, i, 0))
    operands = []
    for g, land in zip(gs, lands):
        operands += [g, land]
    return _pcall(
        body, name=name,
        grid_spec=pltpu.PrefetchScalarGridSpec(
            num_scalar_prefetch=1, grid=(nb,), in_specs=[own, landed] * n,
            out_specs=[pl.BlockSpec((tm, cdim), lambda i, w_ref: (w_ref[1] * nb + i, 0))] * n),
        out_shape=[jax.ShapeDtypeStruct((2 * m, cdim), f32) for _ in gs],
        compiler_params=_params(1),
    )(where, *operands)


def _broadcast_copies(bufs, send_sems, recv_sems):
    v, land = bufs
    x, y, c = _position()

    def copy(slot, send_idx, recv_idx, to):
        return pltpu.make_async_remote_copy(src_ref=v, dst_ref=land.at[slot], send_sem=send_sems.at[send_idx],
                                            recv_sem=recv_sems.at[recv_idx], device_id=to, device_id_type=MESH)

    sends = [copy(0, 0, 0, (x, y, 1 - c))]
    arrivals = [copy(0, 0, 0, (x, y, 1 - c))]
    for t, (px, py) in enumerate(_other_chips(x, y)):
        for core in range(2):
            sends.append(copy(1 + 2 * t + c, 1 + 2 * t + core, 1 + 2 * t + c, (px, py, core)))
            arrivals.append(copy(1 + 2 * t + core, 1 + 2 * t + core, 1 + 2 * t + core, (px, py, core)))
    return sends, arrivals


def _sum_in_device_order(v, land, *, name):
    r, cdim = v.shape
    x, y, c = _position()
    slots, mine = [], []
    for d in range(N_DEV):
        dx, dy, dc = d // 4, (d // 2) % 2, d % 2
        fx, fy = jnp.bitwise_xor(dx, x), jnp.bitwise_xor(dy, y)
        t = jnp.where(fx == 1, jnp.where(fy == 1, 2, 0), 1)
        slots.append(jnp.where(jnp.logical_and(fx == 0, fy == 0), 0, 1 + 2 * t + dc))
        mine.append(jnp.logical_and(jnp.logical_and(fx == 0, fy == 0), dc == c))
    table = jnp.stack(slots + mine).astype(jnp.int32)

    def body(tab_ref, v_ref, *refs):
        out_ref = refs[N_DEV]
        acc = None
        for d in range(N_DEV):
            term = jnp.where(tab_ref[N_DEV + d] == 1, v_ref[...], refs[d][...])
            acc = term if acc is None else acc + term
        out_ref[...] = acc

    whole = pl.BlockSpec((r, cdim), lambda i, tab_ref: (0, 0))
    landed = [pl.BlockSpec((None, r, cdim), functools.partial(lambda i, tab_ref, d: (tab_ref[d], 0, 0), d=d))
              for d in range(N_DEV)]
    return _pcall(
        body, name=name,
        grid_spec=pltpu.PrefetchScalarGridSpec(num_scalar_prefetch=1, grid=(1,), in_specs=[whole] + landed,
                                               out_specs=whole),
        out_shape=jax.ShapeDtypeStruct((r, cdim), f32),
        compiler_params=_params(1),
    )(table, v, *[land] * N_DEV)


def _block_rows(buf, px, py, pc):
    m = buf.shape[0] // N_DEV
    return buf.at[pl.ds(pl.multiple_of((4 * px + 2 * py + pc) * m, 16), m), :]


def _gather_ici_copies(n):
    def copies(bufs, send_sems, recv_sems):
        x, y, c = _position()
        cps = []
        for k in range(n):
            rows = _block_rows(bufs[k], x, y, c)
            targets = [(x, y, 1 - c)] + [(px, py, c) for px, py in _other_chips(x, y)]
            for j, to in enumerate(targets):
                cps.append(pltpu.make_async_remote_copy(
                    src_ref=rows, dst_ref=rows, send_sem=send_sems.at[4 * k + j], recv_sem=recv_sems.at[4 * k + j],
                    device_id=to, device_id_type=MESH))
        return cps
    return copies


def _gather_d2d_copies(n):
    def copies(bufs, send_sems, recv_sems):
        x, y, c = _position()
        cps = []
        for k in range(n):
            for j, (px, py) in enumerate(_other_chips(x, y)):
                rows = _block_rows(bufs[k], px, py, c)
                cps.append(pltpu.make_async_remote_copy(
                    src_ref=rows, dst_ref=rows, send_sem=send_sems.at[3 * k + j], recv_sem=recv_sems.at[3 * k + j],
                    device_id=(x, y, 1 - c), device_id_type=MESH))
        return cps
    return copies


def _cast_halves(shards, after, *, name):
    n = len(shards)
    where = jnp.stack([2 * lax.axis_index("x") + lax.axis_index("y"), lax.axis_index("c")]).astype(jnp.int32)

    def body(w_ref, *refs):
        for k in range(n):
            refs[n + 1 + k][...] = refs[k][...].astype(refs[n + 1 + k].dtype)

    def half(s):
        return (s.shape[0] // 2, s.shape[1])

    return _pcall(
        body, name=name,
        grid_spec=pltpu.PrefetchScalarGridSpec(
            num_scalar_prefetch=1, grid=(1,),
            in_specs=[pl.BlockSpec(half(s), lambda i, w_ref: (w_ref[1], 0)) for s in shards] + [HBM_SPEC],
            out_specs=[pl.BlockSpec(half(s), lambda i, w_ref: (2 * w_ref[0] + w_ref[1], 0)) for s in shards]),
        out_shape=[jax.ShapeDtypeStruct((N_SHARD * s.shape[0], s.shape[1]), MXU_DTYPE) for s in shards],
        compiler_params=_params(1),
    )(where, *shards, after)


class _SplitGather:
    def __init__(self, shards, after, tag):
        self.tag = tag
        self.n = len(shards)
        halves = _cast_halves(shards, after, name=f"{tag}_cast")
        self.ici = _split_start(halves, _gather_ici_copies(self.n), 4 * self.n, name=f"{tag}_ici_start")
        self.token = self.ici[3]

    def forward(self, after):
        send_sems, recv_sems, thru, _ = self.ici
        landed = _split_wait(thru, send_sems, recv_sems, after, _gather_ici_copies(self.n), name=f"{self.tag}_ici_wait")
        self.d2d = _split_start(landed, _gather_d2d_copies(self.n), 3 * self.n, name=f"{self.tag}_d2d_start")
        return self.d2d[3]

    def finish(self, after):
        send_sems, recv_sems, thru, _ = self.d2d
        return _split_wait(thru, send_sems, recv_sems, after, _gather_d2d_copies(self.n), name=f"{self.tag}_d2d_wait")


class _Overlap(_NoOverlap):
    def __init__(self, late_shards, ffn2_shards, after):
        self.late = _SplitGather(late_shards, after, "ag1")
        self.ffn2 = _SplitGather(ffn2_shards, self.late.token, "ag2")
        self.reduced = None
        self.ffn1_parts = []

    def start_token(self):
        return self.ffn2.token

    def late_weights(self, w, after):
        token = self.late.forward(after)
        f1d, w_in, wo = self.late.finish([token])
        w_in = w_in.reshape(N_SHARD, D_MODEL, IN_SHARD).transpose(1, 0, 2).reshape(D_MODEL, IN_COLS)
        return dict(f1d=f1d.reshape(N_SHARD, D_FF // N_SHARD, D_MODEL), wp=make_wp(w_in), wo=wo)

    def after_attention(self, after):
        return self.ffn2.forward(after)

    def ffn2_weights(self, w, after):
        full = self.ffn2.finish(after)
        fs = D_FF // N_SHARD
        return (full[0].reshape(N_SHARD, D_MODEL, fs), full[1].reshape(N_SHARD, D_MODEL, fs),
                full[2].reshape(N_SHARD, fs, D_MODEL))

    @staticmethod
    def _send_direct(grads, tag):
        lands = [lax.empty((N_PEERS, g.shape[1] // 2, g.shape[2]), g.dtype) for g in grads]
        return _split_start(list(grads) + lands, _direct_copies(len(grads)), N_PEERS * len(grads),
                            name=f"rs_direct_{tag}_start")

    def ffn2_grads(self, grads):
        self.scatter = self._send_direct(grads, "ffn2")
        return self.scatter[3]

    def ffn1_grads(self, grads):
        tag = "ffn1" + "ab"[len(self.ffn1_parts)]
        if not self.ffn1_parts:
            started = self._send_direct(grads, tag)
        else:
            recvs = _swap_halves(grads, name=f"rs_swap_{tag}")
            ps = list(_add_halves(grads[:2], recvs[:2], name=f"rs_add_{tag}_gu"))
            ps += list(_add_halves(grads[2:], recvs[2:], name=f"rs_add_{tag}_d"))
            lands = [lax.empty(p.shape, p.dtype) for p in ps]
            started = _split_start(ps + lands, _scatter_copies(3), 9, name=f"rs_scatter_{tag}_start")
        self.ffn1_parts.append((tag, started))
        return started[3]

    def ffn1_reduced(self, after):
        sums = []
        for direct, (tag, (send_sems, recv_sems, thru, _)) in zip((True, False), self.ffn1_parts):
            plan, add = (_direct_copies, _sum_direct) if direct else (_scatter_copies, _sum_slabs)
            done = _split_wait(thru, send_sems, recv_sems, after, plan(3), name=f"rs_{tag}_wait")
            sums += list(add(done[:2], done[3:5], name=f"rs_sum_{tag}_gu"))
            sums += list(add(done[2:3], done[5:], name=f"rs_sum_{tag}_d"))
        return sums

    def mixer_grads(self, dwp, dwo, small, loss):
        packed = jnp.concatenate([_pack_small(small), jnp.broadcast_to(loss, (8, LANES))], axis=0)
        land = lax.empty((N_PEERS,) + packed.shape, packed.dtype)
        self.small = _split_start([packed, land], _broadcast_copies, N_PEERS, name="ar_small_start")
        gwin = dwp[:, :IN_COLS].reshape(D_MODEL, N_SHARD, IN_SHARD).transpose(1, 0, 2).astype(GRAD_DTYPE)
        gwo = dwo.reshape(N_SHARD, D_MODEL // N_SHARD, D_MODEL).astype(GRAD_DTYPE)
        self.scatter_mix = self._send_direct([gwin, gwo], "mix")
        return self.small[3] + self.scatter_mix[3]

    def small_summed(self, after):
        send_sems, recv_sems, thru, _ = self.small
        packed, land = _split_wait(thru, send_sems, recv_sems, after, _broadcast_copies, name="ar_small_wait")
        summed = _sum_in_device_order(packed, land, name="ar_small_sum")
        return summed[:-8], summed[-8, 0]

    def mixer_reduced(self, after):
        send_sems, recv_sems, thru, _ = self.scatter_mix
        done = _split_wait(thru, send_sems, recv_sems, after, _direct_copies(2), name="rs_direct_mix_wait")
        return [_sum_direct([done[k]], [done[2 + k]], name=f"rs_sum_{tag}")[0] for k, tag in enumerate(["w_in", "w_out"])]

    def before_ffn1_bwd(self, after):
        send_sems, recv_sems, thru, _ = self.scatter
        n = len(thru) // 2
        done = _split_wait(thru, send_sems, recv_sems, after, _direct_copies(n), name="rs_direct_ffn2_wait")
        self.reduced = list(_sum_direct(done[:n], done[n:], name="rs_sum_ffn2"))


def _adamw(gs, ws, ms, vs, *, name, tm=256):
    n = len(gs)
    r, cdim = ws[0].shape[-2:]
    tm = r if tm is None else min(tm, r)
    assert r % tm == 0, (r, tm)
    nb = r // tm
    c1 = 1.0 / (1.0 - ADAM_B1 ** ADAM_STEP)
    c2 = 1.0 / (1.0 - ADAM_B2 ** ADAM_STEP)
    flat = pl.BlockSpec((tm, cdim), lambda i: (i, 0))

    g_ops, g_specs, g_where = [], [], []
    for g in gs:
        g_where.append(len(g_ops))
        if not isinstance(g, tuple):
            g_ops.append(g)
            g_specs.append(flat)
        elif g[2] == 1:
            g_ops += [g[0], g[1]]
            g_specs += [pl.BlockSpec((tm, cdim // 2), lambda i: (i, 0))] * 2
        else:
            g_ops += [g[0], g[1]]
            g_specs += [pl.BlockSpec((tm, cdim), lambda i: (jnp.minimum(i, nb // 2 - 1), 0)),
                        pl.BlockSpec((tm, cdim), lambda i: (jnp.maximum(i - nb // 2, 0), 0))]
    ng = len(g_ops)

    def gradient(refs, k):
        g, at = gs[k], g_where[k]
        if not isinstance(g, tuple):
            return refs[at][...]
        if g[2] == 1:
            return jnp.concatenate([refs[at][...], refs[at + 1][...]], axis=1)
        return jnp.where(pl.program_id(0) < nb // 2, refs[at][...], refs[at + 1][...])

    def body(*refs):
        rest = refs[ng:]
        for k in range(n):
            g = gradient(refs, k)
            w = rest[k][...]
            m = ADAM_B1 * rest[n + k][...] + (1.0 - ADAM_B1) * g
            v = ADAM_B2 * rest[2 * n + k][...] + (1.0 - ADAM_B2) * (g * g)
            rest[3 * n + k][...] = g
            rest[4 * n + k][...] = -ADAM_LR * ((m * c1) / (jnp.sqrt(v * c2) + ADAM_EPS) + ADAM_WD * w)
            rest[5 * n + k][...] = m
            rest[6 * n + k][...] = v

    like_w = flat if ws[0].ndim == 2 else pl.BlockSpec((None, tm, cdim), lambda i: (0, i, 0))
    outs = _pcall(
        body, name=name, grid=(nb,), in_specs=g_specs + [like_w] * (3 * n), out_specs=[like_w] * (4 * n),
        out_shape=[jax.ShapeDtypeStruct(ws[0].shape, f32)] * (4 * n),
        compiler_params=_params(1),
    )(*g_ops, *ws, *ms, *vs)
    return outs[:n], outs[n:2 * n], outs[2 * n:3 * n], outs[3 * n:]


BIG = ["ffn1_w_gate", "ffn1_w_up", "ffn1_w_down", "ffn2_w_gate", "ffn2_w_up", "ffn2_w_down"]
SMALL = ["ln1_g", "ln1_b", "b_forget", "conv_w", "conv_b", "rg_wa", "rg_ba", "rg_wx", "rg_bx", "lru_lambda",
         "ln2_g", "ln2_b", "ln3_g", "ln3_b"]
WEIGHTS = ["ffn1_w_gate", "ffn1_w_up", "ffn1_w_down", "ln1_g", "ln1_b", "w_in", "b_forget", "conv_w", "conv_b",
           "rg_wa", "rg_ba", "rg_wx", "rg_bx", "lru_lambda", "w_out", "ln2_g", "ln2_b",
           "ffn2_w_gate", "ffn2_w_up", "ffn2_w_down", "ln3_g", "ln3_b"]


def _pack_small(parts):
    rows = []
    for n in SMALL:
        flat = parts[n].reshape(-1)
        pad = (-flat.shape[0]) % LANES
        rows.append(jnp.pad(flat, (0, pad)).reshape(-1, LANES))
    packed = jnp.concatenate(rows, axis=0)
    return jnp.pad(packed, ((0, (-packed.shape[0]) % 8), (0, 0)))


def _unpack_small(packed, shapes):
    out, r0 = {}, 0
    for n in SMALL:
        size = math.prod(shapes[n])
        nr = -(-size // LANES)
        out[n] = packed[r0:r0 + nr].reshape(-1)[:size].reshape(shapes[n])
        r0 += nr
    return out


def kernel(x, ffn1_w_gate, ffn1_w_up, ffn1_w_down, ln1_g, ln1_b, w_in, b_forget, conv_w, conv_b, rg_wa, rg_ba, rg_wx, rg_bx, lru_lambda, w_out, ln2_g, ln2_b, ffn2_w_gate, ffn2_w_up, ffn2_w_down, ln3_g, ln3_b, loss_target, m_ffn1_w_gate, m_ffn1_w_up, m_ffn1_w_down, m_ln1_g, m_ln1_b, m_w_in, m_b_forget, m_conv_w, m_conv_b, m_rg_wa, m_rg_ba, m_rg_wx, m_rg_bx, m_lru_lambda, m_w_out, m_ln2_g, m_ln2_b, m_ffn2_w_gate, m_ffn2_w_up, m_ffn2_w_down, m_ln3_g, m_ln3_b, v_ffn1_w_gate, v_ffn1_w_up, v_ffn1_w_down, v_ln1_g, v_ln1_b, v_w_in, v_b_forget, v_conv_w, v_conv_b, v_rg_wa, v_rg_ba, v_rg_wx, v_rg_bx, v_lru_lambda, v_w_out, v_ln2_g, v_ln2_b, v_ffn2_w_gate, v_ffn2_w_up, v_ffn2_w_down, v_ln3_g, v_ln3_b):
    args = dict(locals())
    w = {n: args[n] for n in WEIGHTS}
    mom = {n: args["m_" + n] for n in WEIGHTS}
    var = {n: args["v_" + n] for n in WEIGHTS}
    chip = 2 * lax.axis_index("x") + lax.axis_index("y")

    g1 = _all_gather_bf16([w[n][0] for n in BIG[:2]], name="ag_ffn1_up")
    fs = D_FF // N_SHARD
    full = dict(
        f1g=g1[0].reshape(N_SHARD, D_MODEL, fs), f1u=g1[1].reshape(N_SHARD, D_MODEL, fs),
        bfp=jnp.pad(b_forget, ((0, 0), (0, LANES - HEADS))),
        ln1_g=ln1_g, ln1_b=ln1_b, ln2_g=ln2_g, ln2_b=ln2_b, ln3_g=ln3_g, ln3_b=ln3_b,
        conv_b=conv_b, rg_wa=rg_wa[0], rg_wx=rg_wx[0], rg_ba=rg_ba[0], rg_bx=rg_bx[0], lam=lru_lambda,
    )
    cw_place = lax.dynamic_update_slice(jnp.zeros((8, LRU_W), f32), conv_w[0] * 0.5, (0, chip * (LRU_W // N_SHARD)))
    cw_full = _all_reduce_small(cw_place.reshape(-1, LANES), g1[0], name="ag_conv_w")
    full["conv_w"] = cw_full.reshape(8, LRU_W)[:CONV_K]

    hooks = _Overlap([w["ffn1_w_down"][0], w["w_in"][0], w["w_out"][0]], [w[n][0] for n in BIG[3:]], cw_full)
    loss_rep, dx, g = _local_step(x[0], loss_target[0], full, hooks)

    token1 = hooks.ffn1_grads(g["f1"][1])
    red = _join_halves(hooks.reduced + hooks.mixer_reduced([token1]), name="rs_join_rest")
    grads = dict(zip(BIG[3:] + ["w_in", "w_out"], red))

    small_sum, loss = hooks.small_summed(red)
    small_shapes = {n: w[n].shape for n in SMALL}
    small_shapes["conv_w"] = (1, CONV_K, LRU_W)
    gs_red = _unpack_small(small_sum, small_shapes)
    gs_red["conv_w"] = lax.dynamic_slice(gs_red["conv_w"], (0, 0, chip * (LRU_W // N_SHARD)),
                                         (1, CONV_K, LRU_W // N_SHARD))
    grads.update(gs_red)

    delta, new_m, new_v = {}, {}, {}

    def adamw(names, name, **kw):
        g3, d, nm, nv = _adamw([grads[n] for n in names], [w[n] for n in names], [mom[n] for n in names],
                               [var[n] for n in names], name=name, **kw)
        for i, n in enumerate(names):
            grads[n], delta[n], new_m[n], new_v[n] = g3[i], d[i], nm[i], nv[i]

    adamw(BIG[3:], "adamw_ffn2", tm=128)
    adamw(["w_in"], "adamw_w_in")
    adamw(["w_out"], "adamw_w_out")
    shard_shapes = {n: w[n].shape for n in SMALL}
    _, d, nm, nv = _adamw([_pack_small({n: grads[n] for n in SMALL})], [_pack_small({n: w[n] for n in SMALL})],
                          [_pack_small({n: mom[n] for n in SMALL})], [_pack_small({n: var[n] for n in SMALL})],
                          name="adamw_small", tm=None)
    for dst, packed in ((delta, d[0]), (new_m, nm[0]), (new_v, nv[0])):
        dst.update(_unpack_small(packed, shard_shapes))

    worked = [new_v["ffn2_w_down"], new_v["w_in"], new_v["w_out"], nv[0]]
    ga, ua, da, gb, ub, db = _join_halves(hooks.ffn1_reduced(worked), name="rs_join_ffn1")
    grads.update(ffn1_w_gate=(ga, gb, 1), ffn1_w_up=(ua, ub, 1), ffn1_w_down=(da, db, 0))
    adamw(BIG[:3], "adamw_ffn1", tm=128)

    def shaped(tree, n):
        return tree[n].reshape(w[n].shape)

    return (loss, dx[None], *[shaped(grads, n) for n in WEIGHTS], *[shaped(delta, n) for n in WEIGHTS],
            *[shaped(new_m, n) for n in WEIGHTS], *[shaped(new_v, n) for n in WEIGHTS])
```

```python
import functools
import math

import jax
import jax.numpy as jnp
from jax import lax
from jax.experimental import pallas as pl
from jax.experimental.pallas import tpu as pltpu

f32 = jnp.float32
MXU_DTYPE = jnp.bfloat16
GRAD_DTYPE = jnp.bfloat16

D_MODEL = 1024
D_FF = 4096
N_SHARD = 4
N_DEV = 8
FOX_W = 512
LRU_W = 512
HEADS = 8
HEAD_DIM = 64
CONV_K = 4
IN_COLS = 2568
IN_SHARD = IN_COLS // N_SHARD
QKV_W = 3 * FOX_W
Z_PAD = 2688
CAST_COLS = 384
LANES = 128
LN_EPS = 1e-5
DN_ALPHA = 2.0 ** 0.25
LRU_C = 8.0
NEG_BIG = -1e30
VMEM_LIMIT = 56 * 1024 * 1024

ADAM_LR = 0.001
ADAM_B1 = 0.9
ADAM_B2 = 0.999
ADAM_EPS = 1e-08
ADAM_WD = 0.01
ADAM_STEP = 10


def _pcall(body, **kw):
    return pl.pallas_call(body, **kw)


def _params(n_grid, vmem=VMEM_LIMIT):
    return pltpu.CompilerParams(dimension_semantics=("arbitrary",) * n_grid, vmem_limit_bytes=vmem)


def _dot(a, b):
    return jnp.dot(a, b, preferred_element_type=f32)


def _dot_nt(a, b):
    return lax.dot_general(a, b, (((1,), (1,)), ((), ())), preferred_element_type=f32)


def _dot_tn(a, b):
    return lax.dot_general(a, b, (((0,), (0,)), ((), ())), preferred_element_type=f32)


def _sigmoid(x):
    return 1.0 / (1.0 + jnp.exp(-x))


def _layer_norm_stats(y):
    mu = jnp.mean(y, axis=-1, keepdims=True)
    yc = y - mu
    var = jnp.mean(yc * yc, axis=-1, keepdims=True)
    rstd = lax.rsqrt(var + LN_EPS)
    return yc * rstd, rstd


def _ln_backward(dy, xhat, rstd, gamma):
    dxhat = dy * gamma
    m1 = jnp.mean(dxhat, axis=-1, keepdims=True)
    m2 = jnp.mean(dxhat * xhat, axis=-1, keepdims=True)
    dyp = rstd * (dxhat - m1 - xhat * m2)
    return dyp, jnp.sum(dy * xhat, axis=0, keepdims=True), jnp.sum(dy, axis=0, keepdims=True)


def _ffn_fwd_loss(x, wg, wu, wd, ln_g, ln_b, target, *, name, tm=1024, tf=512):
    T = x.shape[0]
    tm = min(tm, T)
    tr = min(256, tm)
    fs = D_FF // N_SHARD
    cpf = fs // tf
    nf = D_FF // tf
    nt = T // tm

    def body(x_ref, wg_ref, wu_ref, wd_ref, g_ref, b_ref, t_ref,
             xb_ref, gact_ref, uact_ref, dyp_ref, dgam_ref, dbeta_ref, loss_ref, acc_ref):
        i = pl.program_id(0)
        f = pl.program_id(1)

        @pl.when(jnp.logical_and(i == 0, f == 0))
        def _():
            dgam_ref[...] = jnp.zeros_like(dgam_ref)
            dbeta_ref[...] = jnp.zeros_like(dbeta_ref)
            loss_ref[...] = jnp.zeros_like(loss_ref)

        @pl.when(f == 0)
        def _():
            xb_ref[...] = x_ref[...].astype(MXU_DTYPE)
            acc_ref[...] = jnp.zeros_like(acc_ref)

        xb = xb_ref[...]
        g = _dot(xb, wg_ref[...])
        u = _dot(xb, wu_ref[...])
        h = (g * _sigmoid(g)) * u
        gact_ref[...] = g.astype(gact_ref.dtype)
        uact_ref[...] = u.astype(uact_ref.dtype)
        acc_ref[...] += _dot(h.astype(MXU_DTYPE), wd_ref[...])

        @pl.when(f == nf - 1)
        def _():
            gamma = g_ref[...]

            def rows_chunk(r, carry):
                rows = pl.ds(pl.multiple_of(r * tr, tr), tr)
                xhat, rstd = _layer_norm_stats(DN_ALPHA * x_ref[rows, :] + 0.5 * acc_ref[rows, :])
                err = xhat * gamma + b_ref[...] - t_ref[rows, :]
                sq = jnp.sum(jnp.sum(err * err, axis=0, keepdims=True), axis=1, keepdims=True)
                loss_ref[...] += jnp.broadcast_to(sq * (0.5 / D_MODEL), loss_ref.shape)
                dyp, dgam, dbeta = _ln_backward(err * (1.0 / D_MODEL), xhat, rstd, gamma)
                dyp_ref[rows, :] = dyp
                dgam_ref[...] += dgam
                dbeta_ref[...] += dbeta
                return carry

            lax.fori_loop(0, tm // tr, rows_chunk, 0)

    row = lambda i, f: (i, 0)
    const = lambda i, f: (0, 0)
    tile = pl.BlockSpec((tm, tf), lambda i, f: (i, f))
    cols = pl.BlockSpec((None, D_MODEL, tf), lambda i, f: (f // cpf, 0, f % cpf))
    last = lambda i, f: (jnp.where(f == nf - 1, i, jnp.maximum(i - 1, 0)), 0)
    return _pcall(
        body, name=name, grid=(nt, nf),
        in_specs=[pl.BlockSpec((tm, D_MODEL), row), cols, cols,
                  pl.BlockSpec((None, tf, D_MODEL), lambda i, f: (f // cpf, f % cpf, 0)),
                  pl.BlockSpec((1, D_MODEL), const), pl.BlockSpec((1, D_MODEL), const),
                  pl.BlockSpec((tm, D_MODEL), last)],
        out_specs=[pl.BlockSpec((tm, D_MODEL), row), tile, tile, pl.BlockSpec((tm, D_MODEL), row),
                   pl.BlockSpec((1, D_MODEL), const), pl.BlockSpec((1, D_MODEL), const), pl.BlockSpec((1, LANES), const)],
        out_shape=[jax.ShapeDtypeStruct((T, D_MODEL), MXU_DTYPE), jax.ShapeDtypeStruct((T, D_FF), MXU_DTYPE),
                   jax.ShapeDtypeStruct((T, D_FF), MXU_DTYPE), jax.ShapeDtypeStruct((T, D_MODEL), f32),
                   jax.ShapeDtypeStruct((1, D_MODEL), f32), jax.ShapeDtypeStruct((1, D_MODEL), f32),
                   jax.ShapeDtypeStruct((1, LANES), f32)],
        scratch_shapes=[pltpu.VMEM((tm, D_MODEL), f32)],
        compiler_params=_params(2),
    )(x, wg, wu, wd, ln_g, ln_b, target)


def _ffn_up(x, wg, wu, after=None, *, name, tm=1024, tf=512):
    T = x.shape[0]
    tm = min(tm, T)
    cpf = (D_FF // N_SHARD) // tf
    nf = D_FF // tf
    extra = [] if after is None else [after]

    def body(x_ref, wg_ref, wu_ref, *refs):
        xb_ref, gact_ref, uact_ref, hact_ref = refs[len(extra):]

        @pl.when(pl.program_id(1) == 0)
        def _():
            xb_ref[...] = x_ref[...].astype(MXU_DTYPE)

        xb = xb_ref[...]
        g = _dot(xb, wg_ref[...])
        u = _dot(xb, wu_ref[...])
        gact_ref[...] = g.astype(gact_ref.dtype)
        uact_ref[...] = u.astype(uact_ref.dtype)
        hact_ref[...] = ((g * _sigmoid(g)) * u).astype(hact_ref.dtype)

    row = lambda i, f: (i, 0)
    tile = pl.BlockSpec((tm, tf), lambda i, f: (i, f))
    cols = pl.BlockSpec((None, D_MODEL, tf), lambda i, f: (f // cpf, 0, f % cpf))
    return _pcall(
        body, name=name, grid=(T // tm, nf),
        in_specs=[pl.BlockSpec((tm, D_MODEL), row), cols, cols] + [pl.BlockSpec(memory_space=pl.ANY)] * len(extra),
        out_specs=[pl.BlockSpec((tm, D_MODEL), row), tile, tile, tile],
        out_shape=[jax.ShapeDtypeStruct((T, D_MODEL), MXU_DTYPE)] + [jax.ShapeDtypeStruct((T, D_FF), MXU_DTYPE)] * 3,
        compiler_params=_params(2),
    )(x, wg, wu, *extra)


def _ffn_down_ln(x, hact, wd, ln_g, ln_b, *, name, tm=1024):
    T = x.shape[0]
    tm = min(tm, T)
    fs = D_FF // N_SHARD
    ks = 2
    nk = N_SHARD // ks

    def body(x_ref, h_ref, wd_ref, g_ref, b_ref, xhat_ref, xn_ref, rstd_ref, acc_ref):
        k = pl.program_id(1)

        @pl.when(k == 0)
        def _():
            acc_ref[...] = jnp.zeros_like(acc_ref)

        acc_ref[...] += _dot(h_ref[...], wd_ref[...].reshape(ks * fs, D_MODEL))

        @pl.when(k == nk - 1)
        def _():
            xhat, rstd = _layer_norm_stats(DN_ALPHA * x_ref[...] + 0.5 * acc_ref[...])
            xhat_ref[...] = xhat
            xn_ref[...] = (xhat * g_ref[...] + b_ref[...]).astype(xn_ref.dtype)
            rstd_ref[...] = jnp.broadcast_to(rstd, rstd_ref.shape)

    row = lambda i, k: (i, 0)
    vec = pl.BlockSpec((1, D_MODEL), lambda i, k: (0, 0))
    return _pcall(
        body, name=name, grid=(T // tm, nk),
        in_specs=[pl.BlockSpec((tm, D_MODEL), row), pl.BlockSpec((tm, ks * fs), lambda i, k: (i, k)),
                  pl.BlockSpec((ks, fs, D_MODEL), lambda i, k: (k, 0, 0)), vec, vec],
        out_specs=[pl.BlockSpec((tm, D_MODEL), row), pl.BlockSpec((tm, D_MODEL), row), pl.BlockSpec((tm, LANES), row)],
        out_shape=[jax.ShapeDtypeStruct((T, D_MODEL), f32), jax.ShapeDtypeStruct((T, D_MODEL), MXU_DTYPE),
                   jax.ShapeDtypeStruct((T, LANES), f32)],
        scratch_shapes=[pltpu.VMEM((tm, D_MODEL), f32)],
        compiler_params=_params(2),
    )(x, hact, wd, ln_g, ln_b)


def _ffn_bwd(dyp, xb, gact, uact, wg, wu, wd, after=None, *, name, tm=512, tf=512, part=None, dx_init=None):
    T = dyp.shape[0]
    tm = min(tm, T)
    fs = D_FF // N_SHARD
    cpf = fs // tf
    nt = T // tm
    nf = D_FF // tf if part is None else N_SHARD
    wf = fs if part is None else tf
    slab = (lambda f: f // cpf) if part is None else (lambda f: f)
    chunk = (lambda f: f % cpf) if part is None else (lambda f: part)
    extra = ([] if dx_init is None else [dx_init]) + ([] if after is None else [after])

    def body(dyp_ref, xb_ref, g_ref, u_ref, wg_ref, wu_ref, wd_ref, *refs):
        dx_hbm, dwg_ref, dwu_ref, dwd_ref, dx_sc, dwg_sc, dwu_sc, dwd_sc, sem = refs[len(extra):]
        f = pl.program_id(0)
        i = pl.program_id(1)
        rows = pl.ds(pl.multiple_of(i * tm, tm), tm)
        dyp_t = dyp_ref[...]
        dy = (0.5 * dyp_t).astype(MXU_DTYPE)

        @pl.when(i == 0)
        def _():
            dwg_sc[...] = jnp.zeros_like(dwg_sc)
            dwu_sc[...] = jnp.zeros_like(dwu_sc)
            dwd_sc[...] = jnp.zeros_like(dwd_sc)

        @pl.when(f == 0)
        def _():
            dx_sc[rows, :] = DN_ALPHA * dyp_t if dx_init is None else refs[0][...]

        g = g_ref[...].astype(f32)
        u = u_ref[...].astype(f32)
        sig = _sigmoid(g)
        silu = g * sig
        dh = _dot_nt(dy, wd_ref[...])
        dg = (dh * u * (sig * (1.0 + g * (1.0 - sig)))).astype(MXU_DTYPE)
        du = (dh * silu).astype(MXU_DTYPE)
        hb = (silu * u).astype(MXU_DTYPE)
        dx_sc[rows, :] += _dot_nt(dg, wg_ref[...]) + _dot_nt(du, wu_ref[...])
        xb_t = xb_ref[...]
        dwg_sc[...] += _dot_tn(xb_t, dg)
        dwu_sc[...] += _dot_tn(xb_t, du)
        dwd_sc[...] += _dot_tn(hb, dy)

        @pl.when(i == nt - 1)
        def _():
            dwg_ref[...] = dwg_sc[...].astype(dwg_ref.dtype)
            dwu_ref[...] = dwu_sc[...].astype(dwu_ref.dtype)
            dwd_ref[...] = dwd_sc[...].astype(dwd_ref.dtype)

        @pl.when(jnp.logical_and(f == nf - 1, i == nt - 1))
        def _():
            cp = pltpu.make_async_copy(dx_sc, dx_hbm, sem)
            cp.start()
            cp.wait()

    row = lambda f, i: (i, 0)
    return _pcall(
        body, name=name, grid=(nf, nt),
        in_specs=[
            pl.BlockSpec((tm, D_MODEL), row),
            pl.BlockSpec((tm, D_MODEL), row),
            pl.BlockSpec((tm, tf), lambda f, i: (i, slab(f) * cpf + chunk(f))),
            pl.BlockSpec((tm, tf), lambda f, i: (i, slab(f) * cpf + chunk(f))),
            pl.BlockSpec((None, D_MODEL, tf), lambda f, i: (slab(f), 0, chunk(f))),
            pl.BlockSpec((None, D_MODEL, tf), lambda f, i: (slab(f), 0, chunk(f))),
            pl.BlockSpec((None, tf, D_MODEL), lambda f, i: (slab(f), chunk(f), 0)),
        ] + ([] if dx_init is None else [pl.BlockSpec((tm, D_MODEL), row)])
        + ([] if after is None else [pl.BlockSpec(memory_space=pl.ANY)]),
        out_specs=[
            pl.BlockSpec(memory_space=pl.ANY),
            pl.BlockSpec((None, D_MODEL, tf), lambda f, i: (slab(f), 0, chunk(f) if part is None else 0)),
            pl.BlockSpec((None, D_MODEL, tf), lambda f, i: (slab(f), 0, chunk(f) if part is None else 0)),
            pl.BlockSpec((None, tf, D_MODEL), lambda f, i: (slab(f), chunk(f) if part is None else 0, 0)),
        ],
        out_shape=[
            jax.ShapeDtypeStruct((T, D_MODEL), f32),
            jax.ShapeDtypeStruct((N_SHARD, D_MODEL, wf), GRAD_DTYPE),
            jax.ShapeDtypeStruct((N_SHARD, D_MODEL, wf), GRAD_DTYPE),
            jax.ShapeDtypeStruct((N_SHARD, wf, D_MODEL), GRAD_DTYPE),
        ],
        scratch_shapes=[pltpu.VMEM((T, D_MODEL), f32), pltpu.VMEM((D_MODEL, tf), f32),
                        pltpu.VMEM((D_MODEL, tf), f32), pltpu.VMEM((tf, D_MODEL), f32),
                        pltpu.SemaphoreType.DMA],
        compiler_params=_params(2),
    )(dyp, xb, gact, uact, wg, wu, wd, *extra)


def _proj_in(xn, wp, bfp, *, name, tm=512):
    T = xn.shape[0]
    tm = min(tm, T)
    nt = T // tm

    def body(x_ref, w_ref, b_ref, lxg_ref, fg_ref, qa_ref, ka_ref, va_ref, carry):
        i = pl.program_id(0)

        @pl.when(i == 0)
        def _():
            carry[...] = jnp.zeros_like(carry)

        z = _dot(x_ref[...], w_ref[...])
        lxg_ref[...] = z[:, QKV_W:QKV_W + 2 * LRU_W]
        fg = z[:, QKV_W + 2 * LRU_W:] + b_ref[...]
        fg_ref[...] = fg
        ls = jnp.minimum(fg, 0.0) - jnp.log(1.0 + jnp.exp(-jnp.abs(fg)))
        r = lax.broadcasted_iota(jnp.int32, (tm, tm), 0)
        c = lax.broadcasted_iota(jnp.int32, (tm, tm), 1)
        cum = _tri_dot(jnp.where(r >= c, 1.0, 0.0).astype(jnp.bfloat16), ls) + carry[0:1, :]
        carry[...] = jnp.broadcast_to(cum[tm - 1:tm, :], carry.shape)

        lane = lax.broadcasted_iota(jnp.int32, (tm, LANES), 1)
        low = lane < HEAD_DIM
        ones_q = jnp.where(jnp.logical_and(lane >= AUX + 3, lane < AUX + 6), 1.0, 0.0)
        ones_k = jnp.where(jnp.logical_and(lane >= AUX, lane < AUX + 3), 1.0, 0.0)
        for j in range(HEADS // 2):
            pair = [z[:, t * FOX_W + j * LANES:t * FOX_W + (j + 1) * LANES] for t in range(3)]
            for odd in range(2):
                h = 2 * j + odd
                q, k, v = [_swap_lane_halves(a) if odd else a for a in pair]
                hi, mid, lo = [a.astype(f32) for a in _split3(jnp.broadcast_to(cum[:, h:h + 1], (tm, LANES)))]
                aux_q = jnp.where(lane == AUX, hi, jnp.where(lane == AUX + 1, mid, jnp.where(lane == AUX + 2, lo, ones_q)))
                aux_k = jnp.where(lane == AUX + 3, -hi,
                                  jnp.where(lane == AUX + 4, -mid, jnp.where(lane == AUX + 5, -lo, ones_k)))
                blk = slice(h * LANES, (h + 1) * LANES)
                qa_ref[:, blk] = jnp.where(low, q, aux_q).astype(qa_ref.dtype)
                ka_ref[:, blk] = jnp.where(low, k, aux_k).astype(ka_ref.dtype)
                va_ref[:, blk] = jnp.where(low, v, 1.0).astype(va_ref.dtype)

    row = lambda i: (i, 0)
    const = lambda i: (0, 0)
    return _pcall(
        body, name=name, grid=(nt,),
        in_specs=[pl.BlockSpec((tm, D_MODEL), row), pl.BlockSpec((D_MODEL, Z_PAD), const),
                  pl.BlockSpec((1, LANES), const)],
        out_specs=[pl.BlockSpec((tm, 2 * LRU_W), row), pl.BlockSpec((tm, LANES), row)]
        + [pl.BlockSpec((tm, HEADS * LANES), row)] * 3,
        out_shape=[jax.ShapeDtypeStruct((T, 2 * LRU_W), f32), jax.ShapeDtypeStruct((T, LANES), f32)]
        + [jax.ShapeDtypeStruct((T, HEADS * LANES), MXU_DTYPE)] * 3,
        scratch_shapes=[pltpu.VMEM((8, LANES), f32)],
        compiler_params=_params(1),
    )(xn, wp, bfp)


def _proj_in_bwd(dqa, dka, dva, dlxg, fgb, xn, dyp, wp, xhat, rstd, ln_g, *, name, tm=512):
    T = xn.shape[0]
    tm = min(tm, T)
    nt = T // tm

    def body(dq_ref, dk_ref, dv_ref, dl_ref, fg_ref, x_ref, dyp_ref, w_ref, xhat_ref, rstd_ref, g_ref,
             dpre_ref, dw_hbm, dgam_ref, dbeta_ref, dbf_ref, dw_sc, dw_out, carry, sem):
        i = pl.program_id(0)

        @pl.when(i == 0)
        def _():
            dw_sc[...] = jnp.zeros_like(dw_sc)
            dgam_ref[...] = jnp.zeros_like(dgam_ref)
            dbeta_ref[...] = jnp.zeros_like(dbeta_ref)
            dbf_ref[...] = jnp.zeros_like(dbf_ref)
            carry[...] = jnp.zeros_like(carry)

        lane = lax.broadcasted_iota(jnp.int32, (tm, LANES), 1)
        dc = jnp.zeros((tm, LANES), f32)
        for h in range(HEADS):
            row_sum = dq_ref[:, h * LANES + AUX:h * LANES + AUX + 1]
            col_sum = dk_ref[:, h * LANES + AUX + 3:h * LANES + AUX + 4]
            dc = jnp.where(lane == h, jnp.broadcast_to(row_sum - col_sum, (tm, LANES)), dc)
        r = lax.broadcasted_iota(jnp.int32, (tm, tm), 0)
        c = lax.broadcasted_iota(jnp.int32, (tm, tm), 1)
        dls = _tri_dot(jnp.where(c >= r, 1.0, 0.0).astype(jnp.bfloat16), dc) + carry[0:1, :]
        carry[...] = jnp.broadcast_to(dls[0:1, :], carry.shape)
        dfg = dls * _sigmoid(-fg_ref[...])
        dbf_ref[...] += jnp.sum(dfg, axis=0, keepdims=True)

        low = _low_lanes((tm, LANES))

        def packed(ref):
            pairs = [jnp.where(low, ref[:, (2 * j) * LANES:(2 * j + 1) * LANES],
                               _swap_lane_halves(ref[:, (2 * j + 1) * LANES:(2 * j + 2) * LANES]))
                     for j in range(HEADS // 2)]
            return jnp.concatenate(pairs, axis=1).astype(MXU_DTYPE)

        dz = jnp.concatenate(
            [packed(dq_ref), packed(dk_ref), packed(dv_ref),
             dl_ref[...].astype(MXU_DTYPE), dfg.astype(MXU_DTYPE)], axis=1)
        dx = DN_ALPHA * dyp_ref[...] + _dot_nt(dz, w_ref[...])
        dpre, dgam, dbeta = _ln_backward(dx, xhat_ref[...], rstd_ref[:, 0:1], g_ref[...])
        dpre_ref[...] = dpre
        dgam_ref[...] += dgam
        dbeta_ref[...] += dbeta
        dw_sc[...] += _dot_tn(x_ref[...], dz)

        @pl.when(i == nt - 1)
        def _():
            dw_sc[:, :FOX_W] = dw_sc[:, :FOX_W] * (1.0 / math.sqrt(HEAD_DIM))
            for c0 in range(0, Z_PAD, CAST_COLS):
                dw_out[:, c0:c0 + CAST_COLS] = dw_sc[:, c0:c0 + CAST_COLS].astype(dw_out.dtype)
            cp = pltpu.make_async_copy(dw_out, dw_hbm, sem)
            cp.start()
            cp.wait()

    row = lambda i: (nt - 1 - i, 0)
    const = lambda i: (0, 0)
    return _pcall(
        body, name=name, grid=(nt,),
        in_specs=[pl.BlockSpec((tm, HEADS * LANES), row), pl.BlockSpec((tm, HEADS * LANES), row),
                  pl.BlockSpec((tm, HEADS * LANES), row),
                  pl.BlockSpec((tm, 2 * LRU_W), row), pl.BlockSpec((tm, LANES), row),
                  pl.BlockSpec((tm, D_MODEL), row), pl.BlockSpec((tm, D_MODEL), row),
                  pl.BlockSpec((D_MODEL, Z_PAD), const),
                  pl.BlockSpec((tm, D_MODEL), row), pl.BlockSpec((tm, LANES), row), pl.BlockSpec((1, D_MODEL), const)],
        out_specs=[pl.BlockSpec((tm, D_MODEL), row), pl.BlockSpec(memory_space=pl.ANY),
                   pl.BlockSpec((1, D_MODEL), const), pl.BlockSpec((1, D_MODEL), const), pl.BlockSpec((1, LANES), const)],
        out_shape=[jax.ShapeDtypeStruct((T, D_MODEL), f32), jax.ShapeDtypeStruct((D_MODEL, Z_PAD), GRAD_DTYPE),
                   jax.ShapeDtypeStruct((1, D_MODEL), f32), jax.ShapeDtypeStruct((1, D_MODEL), f32),
                   jax.ShapeDtypeStruct((1, LANES), f32)],
        scratch_shapes=[pltpu.VMEM((D_MODEL, Z_PAD), f32), pltpu.VMEM((D_MODEL, Z_PAD), GRAD_DTYPE),
                        pltpu.VMEM((8, LANES), f32), pltpu.SemaphoreType.DMA],
        compiler_params=_params(1),
    )(dqa, dka, dva, dlxg, fgb, xn, dyp, wp, xhat, rstd, ln_g)


def _split3(x):
    hi = x.astype(jnp.bfloat16)
    r1 = x - hi.astype(f32)
    mid = r1.astype(jnp.bfloat16)
    lo = (r1 - mid.astype(f32)).astype(jnp.bfloat16)
    return hi, mid, lo


def _tri_dot(tri, x):
    hi, mid, lo = _split3(x)
    return _dot(tri, hi) + _dot(tri, mid) + _dot(tri, lo)


FOX_PAD = HEADS * LANES
AUX = HEAD_DIM


def _low_lanes(shape):
    return lax.broadcasted_iota(jnp.int32, shape, 1) < HEAD_DIM


def _swap_lane_halves(x):
    return pltpu.roll(x, HEAD_DIM, 1)


def _future_keys(tq, tk):
    r = lax.broadcasted_iota(jnp.int32, (tq, tk), 0)
    c = lax.broadcasted_iota(jnp.int32, (tq, tk), 1)
    return c > r


def _causal_steps(nq, key_major):
    if key_major:
        pairs = [(qi, ki) for ki in range(nq) for qi in range(ki, nq)]
    else:
        pairs = [(qi, ki) for qi in range(nq) for ki in range(qi + 1)]
    return (jnp.asarray([p[0] for p in pairs], jnp.int32), jnp.asarray([p[1] for p in pairs], jnp.int32))


def _fox_fwd(qa, ka, va, *, name, tq=512, hps=8):
    T = qa.shape[0]
    tq = min(tq, T)
    tk = tq
    nq = T // tq
    rep = tk // LANES
    qi_tab, ki_tab = _causal_steps(nq, key_major=False)

    def body(qi_ref, ki_ref, qa_ref, ka_ref, va_ref, o_ref, lse_ref, m_sc, acc_sc):
        t = pl.program_id(1)
        qi = qi_ref[t]
        ki = ki_ref[t]

        @pl.when(ki == 0)
        def _():
            m_sc[...] = jnp.full_like(m_sc, NEG_BIG)
            acc_sc[...] = jnp.zeros_like(acc_sc)

        def tile(diagonal):
            for h in range(hps):
                blk = slice(h * LANES, (h + 1) * LANES)
                s = _dot_nt(qa_ref[:, blk], ka_ref[:, blk])
                if diagonal:
                    s = jnp.where(_future_keys(tq, tk), NEG_BIG, s)
                m_prev = m_sc[h]
                m_new = jnp.maximum(m_prev, jnp.max(s, axis=1, keepdims=True))
                p = jnp.exp(s - jnp.tile(m_new, (1, rep)))
                acc_sc[h] = jnp.exp(m_prev - m_new) * acc_sc[h] + _dot(p.astype(MXU_DTYPE), va_ref[:, blk])
                m_sc[h] = m_new

        @pl.when(ki < qi)
        def _():
            tile(False)

        @pl.when(ki == qi)
        def _():
            tile(True)
            low = _low_lanes((tq, LANES))
            outs = []
            for h in range(hps):
                acc = acc_sc[h]
                den = _swap_lane_halves(acc)
                outs.append(acc / den)
                lse_ref[h] = m_sc[h] + jnp.log(jnp.where(low, den, acc))
            for p in range(hps // 2):
                o_ref[:, p * LANES:(p + 1) * LANES] = jnp.where(low, outs[2 * p], _swap_lane_halves(outs[2 * p + 1]))

    pair = hps * LANES
    return _pcall(
        body, name=name,
        grid_spec=pltpu.PrefetchScalarGridSpec(
            num_scalar_prefetch=2, grid=(HEADS // hps, qi_tab.shape[0]),
            in_specs=[
                pl.BlockSpec((tq, pair), lambda j, t, qi_ref, ki_ref: (qi_ref[t], j)),
                pl.BlockSpec((tk, pair), lambda j, t, qi_ref, ki_ref: (ki_ref[t], j)),
                pl.BlockSpec((tk, pair), lambda j, t, qi_ref, ki_ref: (ki_ref[t], j)),
            ],
            out_specs=[pl.BlockSpec((tq, pair // 2), lambda j, t, qi_ref, ki_ref: (qi_ref[t], j)),
                       pl.BlockSpec((hps, tq, LANES), lambda j, t, qi_ref, ki_ref: (j, qi_ref[t], 0))],
            scratch_shapes=[pltpu.VMEM((hps, tq, LANES), f32)] * 2),
        out_shape=[jax.ShapeDtypeStruct((T, FOX_W), f32), jax.ShapeDtypeStruct((HEADS, T, LANES), f32)],
        compiler_params=_params(2),
    )(qi_tab, ki_tab, qa, ka, va)


def _fox_bwd(qa, ka, va, doa, lse, drep, *, name, tq=512, hps=8):
    T = qa.shape[0]
    tq = min(tq, T)
    tk = tq
    nq = T // tq
    rep = tk // LANES
    qi_tab, ki_tab = _causal_steps(nq, key_major=True)

    def body(qi_ref, ki_ref, qa_ref, ka_ref, va_ref, doa_ref, lse_ref, d_ref, dqa_ref, dka_ref, dva_ref, dk_sc, dv_sc):
        t = pl.program_id(1)
        qi = qi_ref[t]
        ki = ki_ref[t]
        rows = pl.ds(pl.multiple_of(qi * tq, tq), tq)

        @pl.when(t == 0)
        def _():
            dqa_ref[...] = jnp.zeros_like(dqa_ref)

        @pl.when(qi == ki)
        def _():
            dk_sc[...] = jnp.zeros_like(dk_sc)
            dv_sc[...] = jnp.zeros_like(dv_sc)

        def tile(diagonal):
            for h in range(hps):
                blk = slice(h * LANES, (h + 1) * LANES)
                qh, kh, doh = qa_ref[:, blk], ka_ref[:, blk], doa_ref[:, blk]
                p = jnp.exp(_dot_nt(qh, kh) - jnp.tile(lse_ref[h], (1, rep)))
                if diagonal:
                    p = jnp.where(_future_keys(tq, tk), 0.0, p)
                dp = _dot_nt(doh, va_ref[:, blk])
                ds = (p * (dp - jnp.tile(d_ref[h], (1, rep)))).astype(MXU_DTYPE)
                dv_sc[h] += _dot_tn(p.astype(MXU_DTYPE), doh)
                dk_sc[h] += _dot_tn(ds, qh)
                dqa_ref[rows, blk] += _dot(ds, kh)

        @pl.when(qi > ki)
        def _():
            tile(False)

        @pl.when(qi == ki)
        def _():
            tile(True)

        @pl.when(qi == nq - 1)
        def _():
            for h in range(hps):
                blk = slice(h * LANES, (h + 1) * LANES)
                dka_ref[:, blk] = dk_sc[h]
                dva_ref[:, blk] = dv_sc[h]

    pair = hps * LANES
    q_blk = lambda j, t, qi_ref, ki_ref: (qi_ref[t], j)
    k_blk = lambda j, t, qi_ref, ki_ref: (ki_ref[t], j)
    stat = pl.BlockSpec((hps, tq, LANES), lambda j, t, qi_ref, ki_ref: (j, qi_ref[t], 0))
    return _pcall(
        body, name=name,
        grid_spec=pltpu.PrefetchScalarGridSpec(
            num_scalar_prefetch=2, grid=(HEADS // hps, qi_tab.shape[0]),
            in_specs=[pl.BlockSpec((tq, pair), q_blk), pl.BlockSpec((tk, pair), k_blk), pl.BlockSpec((tk, pair), k_blk),
                      pl.BlockSpec((tq, pair), q_blk), stat, stat],
            out_specs=[pl.BlockSpec((T, pair), lambda j, t, qi_ref, ki_ref: (0, j)),
                       pl.BlockSpec((tk, pair), k_blk), pl.BlockSpec((tk, pair), k_blk)],
            scratch_shapes=[pltpu.VMEM((hps, tk, LANES), f32)] * 2),
        out_shape=[jax.ShapeDtypeStruct((T, FOX_PAD), f32)] * 3,
        compiler_params=_params(2),
    )(qi_tab, ki_tab, qa, ka, va, doa, lse, drep)


GELU_C = math.sqrt(2.0 / math.pi)
GELU_A = 0.044715


def _gelu(x):
    t = jnp.tanh(GELU_C * (x + GELU_A * x * x * x))
    return 0.5 * x * (1.0 + t), t


def _gelu_grad(x, t):
    return 0.5 * (1.0 + t) + 0.5 * x * (1.0 - t * t) * GELU_C * (1.0 + 3.0 * GELU_A * x * x)


EXPM1_SERIES_BELOW = 0.25


def _expm1(x, e):
    series = x * (1.0 + x * (1 / 2 + x * (1 / 6 + x * (1 / 24 + x * (1 / 120 + x * (1 / 720))))))
    return jnp.where(x > -EXPM1_SERIES_BELOW, series, e - 1.0)


def _softplus_neg(lam):
    return jnp.maximum(-lam, 0.0) + jnp.log(1.0 + jnp.exp(-jnp.abs(lam)))


def _lru_gates(u, wab_ref, bab_ref, lam_ref):
    pre = _dot(u.astype(MXU_DTYPE), wab_ref[...]) + bab_ref[...]
    r = _sigmoid(pre[:, :LRU_W])
    gi = _sigmoid(pre[:, LRU_W:])
    sp = _softplus_neg(lam_ref[...])
    log_a = -LRU_C * r * sp
    a = jnp.exp(log_a)
    s = jnp.sqrt(-_expm1(2.0 * log_a, a * a))
    return r, gi, sp, a, s


def _lru_fwd(lxg, conv_w, conv_b, wab, bab, lam, *, name, tc=512):
    T = lxg.shape[0]
    tc = min(tc, T)
    nc = T // tc

    def body(lx_ref, lg_ref, cw_ref, cb_ref, wab_ref, bab_ref, lam_ref,
             out_ref, u_ref, hs_ref, gates_ref, ext, a_sc, b_sc, h_sc):
        i = pl.program_id(0)

        @pl.when(i == 0)
        def _():
            ext[0:8, :] = jnp.zeros((8, LRU_W), f32)
            h_sc[...] = jnp.zeros_like(h_sc)

        ext[8:, :] = lx_ref[...]
        u = cb_ref[...] + cw_ref[0:1, :] * ext[pl.ds(5, tc), :]
        for k in range(1, CONV_K):
            u = u + cw_ref[k:k + 1, :] * ext[pl.ds(5 + k, tc), :]
        ext[0:8, :] = ext[tc:tc + 8, :]
        u_ref[...] = u
        r, gi, sp, a, s = _lru_gates(u, wab_ref, bab_ref, lam_ref)
        for n, gate in enumerate((r, gi, a, s)):
            gates_ref[:, n * LRU_W:(n + 1) * LRU_W] = gate
        a_sc[...] = a
        b_sc[...] = s * (gi * u)

        def step(t, h):
            h = a_sc[pl.ds(t, 1), :] * h + b_sc[pl.ds(t, 1), :]
            hs_ref[pl.ds(t, 1), :] = h
            return h

        h = lax.fori_loop(0, tc, step, h_sc[0:1, :], unroll=8)
        h_sc[...] = jnp.broadcast_to(h, h_sc.shape)
        gel, _ = _gelu(lg_ref[...])
        out_ref[...] = gel * hs_ref[...]

    row = lambda i: (i, 0)
    const = lambda i: (0, 0)
    return _pcall(
        body, name=name, grid=(nc,),
        in_specs=[pl.BlockSpec((tc, LRU_W), row), pl.BlockSpec((tc, LRU_W), lambda i: (i, 1)),
                  pl.BlockSpec((CONV_K, LRU_W), const), pl.BlockSpec((1, LRU_W), const),
                  pl.BlockSpec((LRU_W, 2 * LRU_W), const), pl.BlockSpec((1, 2 * LRU_W), const),
                  pl.BlockSpec((1, LRU_W), const)],
        out_specs=[pl.BlockSpec((tc, LRU_W), row)] * 3 + [pl.BlockSpec((tc, 4 * LRU_W), row)],
        out_shape=[jax.ShapeDtypeStruct((T, LRU_W), f32)] * 3 + [jax.ShapeDtypeStruct((T, 4 * LRU_W), f32)],
        scratch_shapes=[pltpu.VMEM((tc + 8, LRU_W), f32), pltpu.VMEM((tc, LRU_W), f32),
                        pltpu.VMEM((tc, LRU_W), f32), pltpu.VMEM((8, LRU_W), f32)],
        compiler_params=_params(1),
    )(lxg, lxg, conv_w, conv_b, wab, bab, lam)


def _lru_bwd(dlru, lxg, u, hs, gates, conv_w, wab, lam, *, name, tc=512):
    T = lxg.shape[0]
    tc = min(tc, T)
    nc = T // tc
    bp = tc // 8

    def body(dl_ref, lx_ref, lxp_ref, lg_ref, u_ref, hs_ref, hsp_ref, gates_ref, cw_ref, wab_ref, lam_ref,
             dlxg_ref, dwab_ref, dbab_ref, dcw_ref, dcb_ref, dlam_ref,
             dh_sc, a_sc, ext, du_ext, carry):
        i = pl.program_id(0)
        first_chunk = i == nc - 1

        @pl.when(i == 0)
        def _():
            dwab_ref[...] = jnp.zeros_like(dwab_ref)
            dbab_ref[...] = jnp.zeros_like(dbab_ref)
            dcw_ref[...] = jnp.zeros_like(dcw_ref)
            dcb_ref[...] = jnp.zeros_like(dcb_ref)
            dlam_ref[...] = jnp.zeros_like(dlam_ref)
            carry[...] = jnp.zeros_like(carry)
            du_ext[tc:tc + 8, :] = jnp.zeros((8, LRU_W), f32)

        lg = lg_ref[...]
        gel, th = _gelu(lg)
        dl = dl_ref[...]
        hs = hs_ref[...]
        dlg = dl * hs * _gelu_grad(lg, th)
        u = u_ref[...]
        r, gi, a, s = [gates_ref[:, n * LRU_W:(n + 1) * LRU_W] for n in range(4)]
        sp = _softplus_neg(lam_ref[...])
        a_sc[...] = a
        dh_sc[...] = dl * gel

        def step(k, c):
            t = tc - 1 - k
            dh = dh_sc[pl.ds(t, 1), :] + c
            dh_sc[pl.ds(t, 1), :] = dh
            return a_sc[pl.ds(t, 1), :] * dh

        c = lax.fori_loop(0, tc, step, carry[0:1, :], unroll=8)
        carry[...] = jnp.broadcast_to(c, carry.shape)

        ext[0:8, :] = jnp.where(first_chunk, 0.0, hsp_ref[...])
        ext[8:, :] = hs
        hprev = ext[pl.ds(7, tc), :]
        dh = dh_sc[...]
        da = dh * hprev
        giu = gi * u
        dla = da * a - (dh * giu) * (a * a / s)
        dgi = dh * s * u
        du = dh * s * gi
        dr = dla * (-LRU_C * sp)
        dlam_ref[...] += jnp.sum(dla * (-LRU_C * r), axis=0, keepdims=True) * (-_sigmoid(-lam_ref[...]))
        dpre = jnp.concatenate([dr * r * (1.0 - r), dgi * gi * (1.0 - gi)], axis=1)
        dpre_b = dpre.astype(MXU_DTYPE)
        du = du + _dot_nt(dpre_b, wab_ref[...])
        dwab_ref[...] += _dot_tn(u.astype(MXU_DTYPE), dpre_b)
        dbab_ref[...] += jnp.sum(dpre, axis=0, keepdims=True)
        dcb_ref[...] += jnp.sum(du, axis=0, keepdims=True)

        du_ext[0:tc, :] = du
        dlx = cw_ref[0:1, :] * du_ext[pl.ds(3, tc), :]
        for k in range(1, CONV_K):
            dlx = dlx + cw_ref[k:k + 1, :] * du_ext[pl.ds(3 - k, tc), :]
        du_ext[tc:tc + 8, :] = du_ext[0:8, :]
        ext[0:8, :] = jnp.where(first_chunk, 0.0, lxp_ref[...])
        ext[8:, :] = lx_ref[...]
        for k in range(CONV_K):
            dcw_ref[k:k + 1, :] += jnp.sum(du * ext[pl.ds(5 + k, tc), :], axis=0, keepdims=True)
        dlxg_ref[:, :LRU_W] = dlx.astype(dlxg_ref.dtype)
        dlxg_ref[:, LRU_W:] = dlg.astype(dlxg_ref.dtype)

    rev = lambda i: (nc - 1 - i, 0)
    prev8 = lambda i: (jnp.maximum((nc - 1 - i) * bp - 1, 0), 0)
    const = lambda i: (0, 0)
    return _pcall(
        body, name=name, grid=(nc,),
        in_specs=[
            pl.BlockSpec((tc, LRU_W), rev),
            pl.BlockSpec((tc, LRU_W), rev),
            pl.BlockSpec((8, LRU_W), prev8),
            pl.BlockSpec((tc, LRU_W), lambda i: (nc - 1 - i, 1)),
            pl.BlockSpec((tc, LRU_W), rev),
            pl.BlockSpec((tc, LRU_W), rev),
            pl.BlockSpec((8, LRU_W), prev8),
            pl.BlockSpec((tc, 4 * LRU_W), rev),
            pl.BlockSpec((CONV_K, LRU_W), const),
            pl.BlockSpec((LRU_W, 2 * LRU_W), const),
            pl.BlockSpec((1, LRU_W), const),
        ],
        out_specs=[
            pl.BlockSpec((tc, 2 * LRU_W), rev),
            pl.BlockSpec((LRU_W, 2 * LRU_W), const),
            pl.BlockSpec((1, 2 * LRU_W), const),
            pl.BlockSpec((8, LRU_W), const),
            pl.BlockSpec((1, LRU_W), const),
            pl.BlockSpec((1, LRU_W), const),
        ],
        out_shape=[
            jax.ShapeDtypeStruct((T, 2 * LRU_W), MXU_DTYPE),
            jax.ShapeDtypeStruct((LRU_W, 2 * LRU_W), f32),
            jax.ShapeDtypeStruct((1, 2 * LRU_W), f32),
            jax.ShapeDtypeStruct((8, LRU_W), f32),
            jax.ShapeDtypeStruct((1, LRU_W), f32),
            jax.ShapeDtypeStruct((1, LRU_W), f32),
        ],
        scratch_shapes=[pltpu.VMEM((tc, LRU_W), f32), pltpu.VMEM((tc, LRU_W), f32),
                        pltpu.VMEM((tc + 8, LRU_W), f32), pltpu.VMEM((tc + 8, LRU_W), f32),
                        pltpu.VMEM((8, LRU_W), f32)],
        compiler_params=_params(1),
    )(dlru, lxg, lxg, lxg, u, hs, hs, gates, conv_w, wab, lam)


def _mix_out(fox, lru, wo, xhat1, g1, b1, g2, b2, *, name, tm=512):
    T = fox.shape[0]
    tm = min(tm, T)
    nt = T // tm

    def body(fox_ref, lru_ref, wo_ref, xh_ref, g1_ref, b1_ref, g2_ref, b2_ref, xhat_ref, xn_ref, rstd_ref):
        mix = _dot(fox_ref[...].astype(MXU_DTYPE), wo_ref[:FOX_W, :])
        mix = mix + _dot(lru_ref[...].astype(MXU_DTYPE), wo_ref[FOX_W:, :])
        x1 = xh_ref[...] * g1_ref[...] + b1_ref[...]
        xhat, rstd = _layer_norm_stats(DN_ALPHA * x1 + mix)
        xhat_ref[...] = xhat
        xn_ref[...] = xhat * g2_ref[...] + b2_ref[...]
        rstd_ref[...] = jnp.broadcast_to(rstd, rstd_ref.shape)

    row = lambda i: (i, 0)
    const = lambda i: (0, 0)
    vec = pl.BlockSpec((1, D_MODEL), const)
    return _pcall(
        body, name=name, grid=(nt,),
        in_specs=[pl.BlockSpec((tm, FOX_W), row), pl.BlockSpec((tm, LRU_W), row),
                  pl.BlockSpec((D_MODEL, D_MODEL), const), pl.BlockSpec((tm, D_MODEL), row), vec, vec, vec, vec],
        out_specs=[pl.BlockSpec((tm, D_MODEL), row), pl.BlockSpec((tm, D_MODEL), row),
                   pl.BlockSpec((tm, LANES), row)],
        out_shape=[jax.ShapeDtypeStruct((T, D_MODEL), f32), jax.ShapeDtypeStruct((T, D_MODEL), f32),
                   jax.ShapeDtypeStruct((T, LANES), f32)],
        compiler_params=_params(1),
    )(fox, lru, wo, xhat1, g1, b1, g2, b2)


def _mix_out_bwd(dy, xhat, rstd, ln_g, fox, lru, wo, *, name, tm=512):
    T = fox.shape[0]
    tm = min(tm, T)
    nt = T // tm

    def body(dy_ref, xhat_ref, rstd_ref, g_ref, fox_ref, lru_ref, wo_ref,
             dyp_ref, dgam_ref, dbeta_ref, dlru_ref, dwo_ref, d_ref, doa_ref):
        i = pl.program_id(0)

        @pl.when(i == 0)
        def _():
            dwo_ref[...] = jnp.zeros_like(dwo_ref)
            dgam_ref[...] = jnp.zeros_like(dgam_ref)
            dbeta_ref[...] = jnp.zeros_like(dbeta_ref)

        dyp, dgam, dbeta = _ln_backward(dy_ref[...], xhat_ref[...], rstd_ref[:, 0:1], g_ref[...])
        dyp_ref[...] = dyp
        dgam_ref[...] += dgam
        dbeta_ref[...] += dbeta
        dmix = dyp.astype(MXU_DTYPE)
        dcat = _dot_nt(dmix, wo_ref[...])
        dlru_ref[...] = dcat[:, FOX_W:]
        low = _low_lanes((tm, LANES))
        for j in range(HEADS // 2):
            do2 = dcat[:, j * LANES:(j + 1) * LANES].astype(MXU_DTYPE).astype(f32)
            prod = do2 * fox_ref[:, j * LANES:(j + 1) * LANES]
            for odd in range(2):
                h = 2 * j + odd
                mine = jnp.where(low, _swap_lane_halves(prod) if odd else prod, 0.0)
                d_ref[h] = jnp.broadcast_to(jnp.sum(mine, axis=1, keepdims=True), (tm, LANES))
                doh = jnp.where(low, _swap_lane_halves(do2) if odd else do2, 0.0)
                doa_ref[:, h * LANES:(h + 1) * LANES] = doh.astype(doa_ref.dtype)
        dwo_ref[:FOX_W, :] += _dot_tn(fox_ref[...].astype(MXU_DTYPE), dmix)
        dwo_ref[FOX_W:, :] += _dot_tn(lru_ref[...].astype(MXU_DTYPE), dmix)

    row = lambda i: (i, 0)
    const = lambda i: (0, 0)
    return _pcall(
        body, name=name, grid=(nt,),
        in_specs=[pl.BlockSpec((tm, D_MODEL), row), pl.BlockSpec((tm, D_MODEL), row), pl.BlockSpec((tm, LANES), row),
                  pl.BlockSpec((1, D_MODEL), const),
                  pl.BlockSpec((tm, FOX_W), row), pl.BlockSpec((tm, LRU_W), row),
                  pl.BlockSpec((D_MODEL, D_MODEL), const)],
        out_specs=[pl.BlockSpec((tm, D_MODEL), row), pl.BlockSpec((1, D_MODEL), const), pl.BlockSpec((1, D_MODEL), const),
                   pl.BlockSpec((tm, LRU_W), row), pl.BlockSpec((D_MODEL, D_MODEL), const),
                   pl.BlockSpec((HEADS, tm, LANES), lambda i: (0, i, 0)), pl.BlockSpec((tm, HEADS * LANES), row)],
        out_shape=[jax.ShapeDtypeStruct((T, D_MODEL), f32), jax.ShapeDtypeStruct((1, D_MODEL), f32),
                   jax.ShapeDtypeStruct((1, D_MODEL), f32),
                   jax.ShapeDtypeStruct((T, LRU_W), f32), jax.ShapeDtypeStruct((D_MODEL, D_MODEL), f32),
                   jax.ShapeDtypeStruct((HEADS, T, LANES), f32), jax.ShapeDtypeStruct((T, HEADS * LANES), MXU_DTYPE)],
        compiler_params=_params(1),
    )(dy, xhat, rstd, ln_g, fox, lru, wo)


def make_wp(w_in):
    scale = jnp.concatenate([jnp.full((FOX_W,), 1.0 / math.sqrt(HEAD_DIM), w_in.dtype),
                             jnp.ones((IN_COLS - FOX_W,), w_in.dtype)])
    return jnp.pad(w_in * scale[None, :], ((0, 0), (0, Z_PAD - IN_COLS)))


def _block_diag(w):
    eye = jnp.eye(HEADS, dtype=w.dtype)
    return jnp.einsum("hij,hg->higj", w, eye).reshape(LRU_W, LRU_W)


def _block_diag_extract(m):
    m4 = m.reshape(HEADS, HEAD_DIM, HEADS, HEAD_DIM)
    return jnp.stack([m4[h, :, h, :] for h in range(HEADS)])


class _NoOverlap:
    def start_token(self):
        return None

    def late_weights(self, w, after):
        return dict(f1d=w["f1d"], wp=w["wp"], wo=w["wo"])

    def after_attention(self, after):
        return None

    def ffn2_weights(self, w, after):
        return w["f2g"], w["f2u"], w["f2d"]

    def ffn2_grads(self, grads):
        return None

    def ffn1_grads(self, grads):
        return None

    def mixer_grads(self, dwp, dwo, small, loss):
        return None

    def before_ffn1_bwd(self, after):
        return None


def _tied(a, token):
    return a if token is None else a + token[0, 0]


def _local_step(x, target, w, hooks=None):
    hooks = hooks or _NoOverlap()
    bfp = w["bfp"]
    wab = jnp.concatenate([_block_diag(w["rg_wa"]), _block_diag(w["rg_wx"])], axis=1).astype(MXU_DTYPE)
    bab = jnp.concatenate([w["rg_ba"].reshape(1, LRU_W), w["rg_bx"].reshape(1, LRU_W)], axis=1)

    xb0, g1a, u1a, h1a = _ffn_up(x, w["f1g"], w["f1u"], hooks.start_token(), name="ffn1_up")
    late = hooks.late_weights(w, [h1a])
    f1d, wp, wo = late["f1d"], late["wp"], late["wo"]
    xhat1, xn1, rstd1 = _ffn_down_ln(x, h1a, f1d, w["ln1_g"], w["ln1_b"], name="ffn1_down")
    lxg, fgb, qa, ka, va = _proj_in(xn1, wp, bfp, name="proj_in")
    fox, lse = _fox_fwd(qa, ka, va, name="fox_fwd")
    token = hooks.after_attention([lse])
    lru, uconv, hs, gates = _lru_fwd(lxg, w["conv_w"], _tied(w["conv_b"], token), wab, bab, w["lam"], name="lru_fwd")
    xhat2, x2, rstd2 = _mix_out(fox, lru, wo, xhat1, w["ln1_g"], w["ln1_b"], w["ln2_g"], w["ln2_b"], name="mix_out")
    f2g, f2u, f2d = hooks.ffn2_weights(w, [rstd2])
    xb2, g2a, u2a, dy3p, dln3g, dln3b, loss = _ffn_fwd_loss(x2, f2g, f2u, f2d, w["ln3_g"], w["ln3_b"], target,
                                                            name="ffn2_fwd_loss")

    dx2, df2g, df2u, df2d = _ffn_bwd(dy3p, xb2, g2a, u2a, f2g, f2u, f2d, name="ffn2_bwd")
    token = hooks.ffn2_grads([df2g, df2u, df2d])
    dy2p, dln2g, dln2b, dlru, dwo, drep, doa = _mix_out_bwd(dx2, xhat2, rstd2, _tied(w["ln2_g"], token), fox, lru, wo,
                                                            name="mix_out_bwd")
    dlxg, dwab, dbab, dcw, dcb, dlam = _lru_bwd(dlru, lxg, uconv, hs, gates, w["conv_w"], wab, w["lam"], name="lru_bwd")
    dqa, dka, dva = _fox_bwd(qa, ka, va, doa, lse, drep, name="fox_bwd")
    dy1p, dwp, dln1g, dln1b, dbf = _proj_in_bwd(dqa, dka, dva, dlxg, fgb, xn1, dy2p, wp, xhat1, rstd1, w["ln1_g"],
                                                name="proj_in_bwd")
    small = dict(
        ln1_g=dln1g, ln1_b=dln1b, ln2_g=dln2g, ln2_b=dln2b, ln3_g=dln3g, ln3_b=dln3b,
        b_forget=dbf[:, :HEADS], conv_w=dcw[:CONV_K], conv_b=dcb,
        rg_wa=_block_diag_extract(dwab[:, :LRU_W]), rg_wx=_block_diag_extract(dwab[:, LRU_W:]),
        rg_ba=dbab[:, :LRU_W].reshape(HEADS, HEAD_DIM), rg_bx=dbab[:, LRU_W:].reshape(HEADS, HEAD_DIM),
        lru_lambda=dlam,
    )
    hooks.before_ffn1_bwd([dln1b])
    token = hooks.mixer_grads(dwp, dwo, small, loss)
    dx_a, *grads_a = _ffn_bwd(dy1p, xb0, g1a, u1a, w["f1g"], w["f1u"], f1d, token, name="ffn1_bwd_a", part=0)
    token = hooks.ffn1_grads(grads_a)
    dx, *grads_b = _ffn_bwd(dy1p, xb0, g1a, u1a, w["f1g"], w["f1u"], f1d, token, name="ffn1_bwd_b", part=1,
                            dx_init=dx_a)

    grads = dict(f1=(grads_a, grads_b), f2g=df2g, f2u=df2u, f2d=df2d, wp=dwp, wo=dwo, **small)
    return loss, dx, grads


MESH = pl.DeviceIdType.MESH
HBM_SPEC = pl.BlockSpec(memory_space=pl.ANY)
VMEM_SPEC = pl.BlockSpec(memory_space=pltpu.VMEM)


def _position():
    return lax.axis_index("x"), lax.axis_index("y"), lax.axis_index("c")


def _other_chips(x, y):
    return [(1 - x, y), (x, 1 - y), (1 - x, 1 - y)]


def _all_gather_bf16(shards, *, name):
    n = len(shards)

    def body(*refs):
        ins, outs, stages = refs[:n], refs[n:2 * n], refs[2 * n:3 * n]
        send_sems, recv_sems, local_sems = refs[3 * n:]
        x, y, c = _position()
        me, sibling = (x, y, c), (x, y, 1 - c)
        chips = _other_chips(x, y)

        def rows(k, px, py, pc):
            r = shards[k].shape[0]
            m = r // 2
            return outs[k].at[pl.ds(pl.multiple_of((2 * px + py) * r + pc * m, 16), m), :]

        def copy(k, idx, block, to, src=None):
            return pltpu.make_async_remote_copy(
                src_ref=rows(k, *block) if src is None else src, dst_ref=rows(k, *block),
                send_sem=send_sems.at[7 * k + idx], recv_sem=recv_sems.at[7 * k + idx],
                device_id=to, device_id_type=MESH)

        started = []
        mine = []
        for k in range(n):
            m = shards[k].shape[0] // 2
            stages[k][...] = ins[k][pl.ds(pl.multiple_of(c * m, 16), m), :].astype(stages[k].dtype)
            cp = pltpu.make_async_copy(stages[k], rows(k, *me), local_sems.at[k])
            cp.start()
            mine.append(cp)
            first = [copy(k, 0, me, sibling, src=stages[k])]
            first += [copy(k, 1 + j, me, (*chip, c), src=stages[k]) for j, chip in enumerate(chips)]
            for cp in first:
                cp.start()
            started += first
        for k in range(n):
            for j, chip in enumerate(chips):
                copy(k, 1 + j, (*chip, c), me).wait_recv()
                fwd = copy(k, 4 + j, (*chip, c), sibling)
                fwd.start()
                started.append(fwd)
        for k in range(n):
            copy(k, 0, sibling, me).wait_recv()
            for j, chip in enumerate(chips):
                copy(k, 4 + j, (*chip, 1 - c), me).wait_recv()
        for cp in started:
            cp.wait_send()
        for cp in mine:
            cp.wait()

    return _pcall(
        body, name=name,
        in_specs=[VMEM_SPEC] * n, out_specs=[HBM_SPEC] * n,
        out_shape=[jax.ShapeDtypeStruct((N_SHARD * s.shape[0], s.shape[1]), MXU_DTYPE) for s in shards],
        scratch_shapes=[pltpu.VMEM((s.shape[0] // 2, s.shape[1]), MXU_DTYPE) for s in shards]
        + [pltpu.SemaphoreType.DMA((7 * n,)), pltpu.SemaphoreType.DMA((7 * n,)), pltpu.SemaphoreType.DMA((n,))],
        compiler_params=pltpu.CompilerParams(vmem_limit_bytes=VMEM_LIMIT),
    )(*shards)


def _swap_halves(gs, *, name):
    n = len(gs)

    def body(*refs):
        ins, outs = refs[:n], refs[n:2 * n]
        send_sems, recv_sems = refs[2 * n:]
        x, y, c = _position()
        cps = []
        for k in range(n):
            m = gs[k].shape[1] // 2
            src = ins[k].at[:, pl.ds(pl.multiple_of((1 - c) * m, 16), m), :]
            cp = pltpu.make_async_remote_copy(src_ref=src, dst_ref=outs[k], send_sem=send_sems.at[k],
                                              recv_sem=recv_sems.at[k], device_id=(x, y, 1 - c), device_id_type=MESH)
            cp.start()
            cps.append(cp)
        for cp in cps:
            cp.wait()

    return _pcall(
        body, name=name, in_specs=[HBM_SPEC] * n, out_specs=[HBM_SPEC] * n,
        out_shape=[jax.ShapeDtypeStruct((g.shape[0], g.shape[1] // 2, g.shape[2]), g.dtype) for g in gs],
        scratch_shapes=[pltpu.SemaphoreType.DMA((n,)), pltpu.SemaphoreType.DMA((n,))],
    )(*gs)


def _add_halves(gs, recvs, *, name, tm=256):
    n = len(gs)
    _, r, cdim = gs[0].shape
    m = r // 2
    tm = min(tm, m)
    nb = m // tm
    c_idx = lax.axis_index("c").astype(jnp.int32).reshape(1)

    def body(c_ref, *refs):
        for k in range(n):
            refs[2 * n + k][...] = (refs[k][...].astype(f32) + refs[n + k][...].astype(f32)).astype(refs[2 * n + k].dtype)

    mine = pl.BlockSpec((None, tm, cdim), lambda j, i, c_ref: (j, c_ref[0] * nb + i, 0))
    half = pl.BlockSpec((None, tm, cdim), lambda j, i, c_ref: (j, i, 0))
    return _pcall(
        body, name=name,
        grid_spec=pltpu.PrefetchScalarGridSpec(
            num_scalar_prefetch=1, grid=(N_SHARD, nb),
            in_specs=[mine] * n + [half] * n, out_specs=[half] * n),
        out_shape=[jax.ShapeDtypeStruct((N_SHARD, m, cdim), g.dtype) for g in gs],
        compiler_params=_params(2),
    )(c_idx, *gs, *recvs)


def _scatter_partials(ps, *, name):
    n = len(ps)

    def body(*refs):
        ins, outs = refs[:n], refs[n:2 * n]
        send_sems, recv_sems = refs[2 * n:]
        x, y, c = _position()
        me_chip = 2 * x + y
        cps = []
        for k in range(n):
            for j, (px, py) in enumerate(_other_chips(x, y)):
                cp = pltpu.make_async_remote_copy(
                    src_ref=ins[k].at[2 * px + py], dst_ref=outs[k].at[me_chip],
                    send_sem=send_sems.at[3 * k + j], recv_sem=recv_sems.at[3 * k + j],
                    device_id=(px, py, c), device_id_type=MESH)
                cp.start()
                cps.append(cp)
        for cp in cps:
            cp.wait()

    return _pcall(
        body, name=name, in_specs=[HBM_SPEC] * n, out_specs=[HBM_SPEC] * n,
        out_shape=[jax.ShapeDtypeStruct(p.shape, p.dtype) for p in ps],
        scratch_shapes=[pltpu.SemaphoreType.DMA((3 * n,)), pltpu.SemaphoreType.DMA((3 * n,))],
    )(*ps)


def _sum_slabs(ps, qs, *, name, tm=128):
    n = len(qs)
    _, m, cdim = qs[0].shape
    tm = min(tm, m)
    nb = m // tm
    assert m % tm == 0, (m, tm)
    where = jnp.stack([2 * lax.axis_index("x") + lax.axis_index("y"), lax.axis_index("c")]).astype(jnp.int32)

    def body(w_ref, *refs):
        for k in range(n):
            own, q1, q2, q3 = (refs[4 * k + t][...].astype(f32) for t in range(4))
            refs[4 * n + k][...] = ((own + q1) + q2) + q3

    def slab(flip):
        return pl.BlockSpec((None, tm, cdim), lambda i, w_ref: (jnp.bitwise_xor(w_ref[0], flip), i, 0))

    operands = []
    for p, q in zip(ps, qs):
        operands += [p, q, q, q]
    return _pcall(
        body, name=name,
        grid_spec=pltpu.PrefetchScalarGridSpec(
            num_scalar_prefetch=1, grid=(nb,),
            in_specs=[slab(0), slab(2), slab(1), slab(3)] * n,
            out_specs=[pl.BlockSpec((tm, cdim), lambda i, w_ref: (w_ref[1] * nb + i, 0))] * n),
        out_shape=[jax.ShapeDtypeStruct((2 * m, cdim), f32) for _ in qs],
        compiler_params=_params(1),
    )(where, *operands)


def _join_halves(fs, *, name):
    n = len(fs)

    def body(*refs):
        outs = refs[n:2 * n]
        send_sems, recv_sems = refs[2 * n:]
        x, y, c = _position()
        cps = []
        for k in range(n):
            m = fs[k].shape[0] // 2
            half = outs[k].at[pl.ds(pl.multiple_of(c * m, 8), m), :]
            cp = pltpu.make_async_remote_copy(src_ref=half, dst_ref=half, send_sem=send_sems.at[k],
                                              recv_sem=recv_sems.at[k], device_id=(x, y, 1 - c), device_id_type=MESH)
            cp.start()
            cps.append(cp)
        for cp in cps:
            cp.wait()

    return _pcall(
        body, name=name, in_specs=[HBM_SPEC] * n, out_specs=[HBM_SPEC] * n,
        out_shape=[jax.ShapeDtypeStruct(f.shape, f.dtype) for f in fs],
        input_output_aliases={k: k for k in range(n)},
        scratch_shapes=[pltpu.SemaphoreType.DMA((n,)), pltpu.SemaphoreType.DMA((n,))],
    )(*fs)


def _all_reduce_small(v, after=None, *, name):
    r = v.shape[0]
    extra = [] if after is None else [after]

    def body(v_ref, *refs):
        out_ref, buf, send_sems, recv_sems, local_sem = refs[len(extra):]
        x, y, c = _position()
        me, sibling = (x, y, c), (x, y, 1 - c)
        chips = _other_chips(x, y)

        def rows(px, py, pc):
            return buf.at[pl.ds(pl.multiple_of((4 * px + 2 * py + pc) * r, 8), r), :]

        def copy(k, block, to, src=None):
            return pltpu.make_async_remote_copy(
                src_ref=rows(*block) if src is None else src, dst_ref=rows(*block),
                send_sem=send_sems.at[k], recv_sem=recv_sems.at[k], device_id=to, device_id_type=MESH)

        mine = pltpu.make_async_copy(v_ref, rows(*me), local_sem)
        mine.start()
        first = [copy(0, me, sibling, src=v_ref)]
        first += [copy(1 + j, me, (*chip, c), src=v_ref) for j, chip in enumerate(chips)]
        for cp in first:
            cp.start()
        passed = [copy(4 + j, (*chip, c), sibling) for j, chip in enumerate(chips)]
        for j, chip in enumerate(chips):
            copy(1 + j, (*chip, c), me).wait_recv()
            passed[j].start()
        copy(0, sibling, me).wait_recv()
        for j, chip in enumerate(chips):
            copy(4 + j, (*chip, 1 - c), me).wait_recv()
        for cp in first + passed:
            cp.wait_send()
        mine.wait()
        acc = buf[0:r, :]
        for d in range(1, N_DEV):
            acc = acc + buf[d * r:(d + 1) * r, :]
        out_ref[...] = acc

    return _pcall(
        body, name=name, in_specs=[VMEM_SPEC] + [HBM_SPEC] * len(extra), out_specs=VMEM_SPEC,
        out_shape=jax.ShapeDtypeStruct((r, LANES), f32),
        scratch_shapes=[pltpu.VMEM((N_DEV * r, LANES), f32), pltpu.SemaphoreType.DMA((7,)),
                        pltpu.SemaphoreType.DMA((7,)), pltpu.SemaphoreType.DMA],
    )(v, *extra)


SEM_SPEC = pl.BlockSpec(memory_space=pltpu.SEMAPHORE)
HBM_ONLY = pl.BlockSpec(memory_space=pltpu.HBM)
EFFECT = pltpu.SideEffectType.DATAFLOW_SIDE_EFFECTING


def _sends(copies):
    return copies[0] if isinstance(copies, tuple) else copies


def _arrivals(copies):
    return copies[1] if isinstance(copies, tuple) else copies


def _split_start(bufs, copies_fn, n_sems, *, name):
    n = len(bufs)

    def body(*refs):
        send_sems, recv_sems = refs[n], refs[n + 1]
        thru = refs[n + 2:2 * n + 2]
        token = refs[2 * n + 2]
        for cp in _sends(copies_fn(thru, send_sems, recv_sems)):
            cp.start()
        token[...] = jnp.zeros_like(token)

    outs = _pcall(
        body, name=name,
        out_shape=(pltpu.SemaphoreType.DMA((n_sems,)), pltpu.SemaphoreType.DMA((n_sems,)),
                   *[pltpu.HBM(b.shape, b.dtype) for b in bufs], jax.ShapeDtypeStruct((8, LANES), f32)),
        in_specs=[HBM_ONLY] * n,
        out_specs=(SEM_SPEC, SEM_SPEC, *[HBM_ONLY] * n, VMEM_SPEC),
        input_output_aliases={k: 2 + k for k in range(n)},
        compiler_params=pltpu.CompilerParams(has_side_effects=EFFECT),
    )(*[pltpu.with_memory_space_constraint(b, pltpu.HBM) for b in bufs])
    return outs[0], outs[1], list(outs[2:2 + n]), outs[2 + n]


def _split_wait(thru, send_sems, recv_sems, after, copies_fn, *, name):
    n = len(thru)

    def body(*refs):
        copies = copies_fn(refs[:n], refs[n], refs[n + 1])
        for cp in _sends(copies):
            cp.wait_send()
        for cp in _arrivals(copies):
            cp.wait_recv()

    return list(_pcall(
        body, name=name,
        out_shape=tuple(pltpu.HBM(b.shape, b.dtype) for b in thru),
        in_specs=[HBM_ONLY] * n + [SEM_SPEC, SEM_SPEC] + [HBM_SPEC] * len(after),
        out_specs=tuple([HBM_ONLY] * n),
        input_output_aliases={k: k for k in range(n)},
        compiler_params=pltpu.CompilerParams(has_side_effects=EFFECT),
    )(*thru, send_sems, recv_sems, *after))


def _scatter_copies(n):
    def copies(bufs, send_sems, recv_sems):
        x, y, c = _position()
        me_chip = 2 * x + y
        cps = []
        for k in range(n):
            for j, (px, py) in enumerate(_other_chips(x, y)):
                cps.append(pltpu.make_async_remote_copy(
                    src_ref=bufs[k].at[2 * px + py], dst_ref=bufs[n + k].at[me_chip],
                    send_sem=send_sems.at[3 * k + j], recv_sem=recv_sems.at[3 * k + j],
                    device_id=(px, py, c), device_id_type=MESH))
        return cps
    return copies


N_PEERS = N_DEV - 1


def _direct_copies(n):
    def copies(bufs, send_sems, recv_sems):
        x, y, c = _position()
        me_chip = 2 * x + y
        sends, arrivals = [], []
        for k in range(n):
            m = bufs[k].shape[1] // 2
            land = bufs[n + k]

            def rows(slab, half, k=k, m=m):
                start = half * m if isinstance(half, int) else pl.multiple_of(half * m, 16)
                return bufs[k].at[slab, pl.ds(start, m), :]

            def copy(src, slot, send_idx, recv_idx, to, k=k, land=land):
                return pltpu.make_async_remote_copy(
                    src_ref=src, dst_ref=land.at[slot], send_sem=send_sems.at[N_PEERS * k + send_idx],
                    recv_sem=recv_sems.at[N_PEERS * k + recv_idx], device_id=to, device_id_type=MESH)

            sends.append(copy(rows(me_chip, 1 - c), 0, 0, 0, (x, y, 1 - c)))
            arrivals.append(copy(rows(me_chip, c), 0, 0, 0, (x, y, 1 - c)))
            for t, (px, py) in enumerate(_other_chips(x, y)):
                for core in range(2):
                    sends.append(copy(rows(2 * px + py, core), 1 + 2 * t + c, 1 + 2 * t + core, 1 + 2 * t + c,
                                      (px, py, core)))
                    arrivals.append(copy(rows(me_chip, c), 1 + 2 * t + core, 1 + 2 * t + core, 1 + 2 * t + core,
                                         (px, py, core)))
        return sends, arrivals
    return copies


def _sum_direct(gs, lands, *, name, tm=128):
    n = len(gs)
    _, m, cdim = lands[0].shape
    tm = min(tm, m)
    nb = m // tm
    assert m % tm == 0, (m, tm)
    where = jnp.stack([2 * lax.axis_index("x") + lax.axis_index("y"), lax.axis_index("c")]).astype(jnp.int32)

    def body(w_ref, *refs):
        for k in range(n):
            acc = refs[2 * k][...].astype(f32)
            for slot in range(N_PEERS):
                acc = acc + refs[2 * k + 1][slot].astype(f32)
            refs[2 * n + k][...] = acc

    own = pl.BlockSpec((None, tm, cdim), lambda i, w_ref: (w_ref[0], w_ref[1] * nb + i, 0))
    landed = pl.BlockSpec((N_PEERS, tm, cdim), lambda i, w_ref: (0, i, 0))
    operands = []
    for g, land in zip(gs, lands):
        operands += [g, land]
    return _pcall(
        body, name=name,
        grid_spec=pltpu.PrefetchScalarGridSpec(
            num_scalar_prefetch=1, grid=(nb,), in_specs=[own, landed] * n,
            out_specs=[pl.BlockSpec((tm, cdim), lambda i, w_ref: (w_ref[1] * nb + i, 0))] * n),
        out_shape=[jax.ShapeDtypeStruct((2 * m, cdim), f32) for _ in gs],
        compiler_params=_params(1),
    )(where, *operands)


def _broadcast_copies(bufs, send_sems, recv_sems):
    v, land = bufs
    x, y, c = _position()

    def copy(slot, send_idx, recv_idx, to):
        return pltpu.make_async_remote_copy(src_ref=v, dst_ref=land.at[slot], send_sem=send_sems.at[send_idx],
                                            recv_sem=recv_sems.at[recv_idx], device_id=to, device_id_type=MESH)

    sends = [copy(0, 0, 0, (x, y, 1 - c))]
    arrivals = [copy(0, 0, 0, (x, y, 1 - c))]
    for t, (px, py) in enumerate(_other_chips(x, y)):
        for core in range(2):
            sends.append(copy(1 + 2 * t + c, 1 + 2 * t + core, 1 + 2 * t + c, (px, py, core)))
            arrivals.append(copy(1 + 2 * t + core, 1 + 2 * t + core, 1 + 2 * t + core, (px, py, core)))
    return sends, arrivals


def _sum_in_device_order(v, land, *, name):
    r, cdim = v.shape
    x, y, c = _position()
    slots, mine = [], []
    for d in range(N_DEV):
        dx, dy, dc = d // 4, (d // 2) % 2, d % 2
        fx, fy = jnp.bitwise_xor(dx, x), jnp.bitwise_xor(dy, y)
        t = jnp.where(fx == 1, jnp.where(fy == 1, 2, 0), 1)
        slots.append(jnp.where(jnp.logical_and(fx == 0, fy == 0), 0, 1 + 2 * t + dc))
        mine.append(jnp.logical_and(jnp.logical_and(fx == 0, fy == 0), dc == c))
    table = jnp.stack(slots + mine).astype(jnp.int32)

    def body(tab_ref, v_ref, *refs):
        out_ref = refs[N_DEV]
        acc = None
        for d in range(N_DEV):
            term = jnp.where(tab_ref[N_DEV + d] == 1, v_ref[...], refs[d][...])
            acc = term if acc is None else acc + term
        out_ref[...] = acc

    whole = pl.BlockSpec((r, cdim), lambda i, tab_ref: (0, 0))
    landed = [pl.BlockSpec((None, r, cdim), functools.partial(lambda i, tab_ref, d: (tab_ref[d], 0, 0), d=d))
              for d in range(N_DEV)]
    return _pcall(
        body, name=name,
        grid_spec=pltpu.PrefetchScalarGridSpec(num_scalar_prefetch=1, grid=(1,), in_specs=[whole] + landed,
                                               out_specs=whole),
        out_shape=jax.ShapeDtypeStruct((r, cdim), f32),
        compiler_params=_params(1),
    )(table, v, *[land] * N_DEV)


def _block_rows(buf, px, py, pc):
    m = buf.shape[0] // N_DEV
    return buf.at[pl.ds(pl.multiple_of((4 * px + 2 * py + pc) * m, 16), m), :]


def _gather_ici_copies(n):
    def copies(bufs, send_sems, recv_sems):
        x, y, c = _position()
        cps = []
        for k in range(n):
            rows = _block_rows(bufs[k], x, y, c)
            targets = [(x, y, 1 - c)] + [(px, py, c) for px, py in _other_chips(x, y)]
            for j, to in enumerate(targets):
                cps.append(pltpu.make_async_remote_copy(
                    src_ref=rows, dst_ref=rows, send_sem=send_sems.at[4 * k + j], recv_sem=recv_sems.at[4 * k + j],
                    device_id=to, device_id_type=MESH))
        return cps
    return copies


def _gather_d2d_copies(n):
    def copies(bufs, send_sems, recv_sems):
        x, y, c = _position()
        cps = []
        for k in range(n):
            for j, (px, py) in enumerate(_other_chips(x, y)):
                rows = _block_rows(bufs[k], px, py, c)
                cps.append(pltpu.make_async_remote_copy(
                    src_ref=rows, dst_ref=rows, send_sem=send_sems.at[3 * k + j], recv_sem=recv_sems.at[3 * k + j],
                    device_id=(x, y, 1 - c), device_id_type=MESH))
        return cps
    return copies


def _cast_halves(shards, after, *, name):
    n = len(shards)
    where = jnp.stack([2 * lax.axis_index("x") + lax.axis_index("y"), lax.axis_index("c")]).astype(jnp.int32)

    def body(w_ref, *refs):
        for k in range(n):
            refs[n + 1 + k][...] = refs[k][...].astype(refs[n + 1 + k].dtype)

    def half(s):
        return (s.shape[0] // 2, s.shape[1])

    return _pcall(
        body, name=name,
        grid_spec=pltpu.PrefetchScalarGridSpec(
            num_scalar_prefetch=1, grid=(1,),
            in_specs=[pl.BlockSpec(half(s), lambda i, w_ref: (w_ref[1], 0)) for s in shards] + [HBM_SPEC],
            out_specs=[pl.BlockSpec(half(s), lambda i, w_ref: (2 * w_ref[0] + w_ref[1], 0)) for s in shards]),
        out_shape=[jax.ShapeDtypeStruct((N_SHARD * s.shape[0], s.shape[1]), MXU_DTYPE) for s in shards],
        compiler_params=_params(1),
    )(where, *shards, after)


class _SplitGather:
    def __init__(self, shards, after, tag):
        self.tag = tag
        self.n = len(shards)
        halves = _cast_halves(shards, after, name=f"{tag}_cast")
        self.ici = _split_start(halves, _gather_ici_copies(self.n), 4 * self.n, name=f"{tag}_ici_start")
        self.token = self.ici[3]

    def forward(self, after):
        send_sems, recv_sems, thru, _ = self.ici
        landed = _split_wait(thru, send_sems, recv_sems, after, _gather_ici_copies(self.n), name=f"{self.tag}_ici_wait")
        self.d2d = _split_start(landed, _gather_d2d_copies(self.n), 3 * self.n, name=f"{self.tag}_d2d_start")
        return self.d2d[3]

    def finish(self, after):
        send_sems, recv_sems, thru, _ = self.d2d
        return _split_wait(thru, send_sems, recv_sems, after, _gather_d2d_copies(self.n), name=f"{self.tag}_d2d_wait")


class _Overlap(_NoOverlap):
    def __init__(self, late_shards, ffn2_shards, after):
        self.late = _SplitGather(late_shards, after, "ag1")
        self.ffn2 = _SplitGather(ffn2_shards, self.late.token, "ag2")
        self.reduced = None
        self.ffn1_parts = []

    def start_token(self):
        return self.ffn2.token

    def late_weights(self, w, after):
        token = self.late.forward(after)
        f1d, w_in, wo = self.late.finish([token])
        w_in = w_in.reshape(N_SHARD, D_MODEL, IN_SHARD).transpose(1, 0, 2).reshape(D_MODEL, IN_COLS)
        return dict(f1d=f1d.reshape(N_SHARD, D_FF // N_SHARD, D_MODEL), wp=make_wp(w_in), wo=wo)

    def after_attention(self, after):
        return self.ffn2.forward(after)

    def ffn2_weights(self, w, after):
        full = self.ffn2.finish(after)
        fs = D_FF // N_SHARD
        return (full[0].reshape(N_SHARD, D_MODEL, fs), full[1].reshape(N_SHARD, D_MODEL, fs),
                full[2].reshape(N_SHARD, fs, D_MODEL))

    @staticmethod
    def _send_direct(grads, tag):
        lands = [lax.empty((N_PEERS, g.shape[1] // 2, g.shape[2]), g.dtype) for g in grads]
        return _split_start(list(grads) + lands, _direct_copies(len(grads)), N_PEERS * len(grads),
                            name=f"rs_direct_{tag}_start")

    def ffn2_grads(self, grads):
        self.scatter = self._send_direct(grads, "ffn2")
        return self.scatter[3]

    def ffn1_grads(self, grads):
        tag = "ffn1" + "ab"[len(self.ffn1_parts)]
        if not self.ffn1_parts:
            started = self._send_direct(grads, tag)
        else:
            recvs = _swap_halves(grads, name=f"rs_swap_{tag}")
            ps = list(_add_halves(grads[:2], recvs[:2], name=f"rs_add_{tag}_gu"))
            ps += list(_add_halves(grads[2:], recvs[2:], name=f"rs_add_{tag}_d"))
            lands = [lax.empty(p.shape, p.dtype) for p in ps]
            started = _split_start(ps + lands, _scatter_copies(3), 9, name=f"rs_scatter_{tag}_start")
        self.ffn1_parts.append((tag, started))
        return started[3]

    def ffn1_reduced(self, after):
        sums = []
        for direct, (tag, (send_sems, recv_sems, thru, _)) in zip((True, False), self.ffn1_parts):
            plan, add = (_direct_copies, _sum_direct) if direct else (_scatter_copies, _sum_slabs)
            done = _split_wait(thru, send_sems, recv_sems, after, plan(3), name=f"rs_{tag}_wait")
            sums += list(add(done[:2], done[3:5], name=f"rs_sum_{tag}_gu"))
            sums += list(add(done[2:3], done[5:], name=f"rs_sum_{tag}_d"))
        return sums

    def mixer_grads(self, dwp, dwo, small, loss):
        packed = jnp.concatenate([_pack_small(small), jnp.broadcast_to(loss, (8, LANES))], axis=0)
        land = lax.empty((N_PEERS,) + packed.shape, packed.dtype)
        self.small = _split_start([packed, land], _broadcast_copies, N_PEERS, name="ar_small_start")
        gwin = dwp[:, :IN_COLS].reshape(D_MODEL, N_SHARD, IN_SHARD).transpose(1, 0, 2).astype(GRAD_DTYPE)
        gwo = dwo.reshape(N_SHARD, D_MODEL // N_SHARD, D_MODEL).astype(GRAD_DTYPE)
        self.scatter_mix = self._send_direct([gwin, gwo], "mix")
        return self.small[3] + self.scatter_mix[3]

    def small_summed(self, after):
        send_sems, recv_sems, thru, _ = self.small
        packed, land = _split_wait(thru, send_sems, recv_sems, after, _broadcast_copies, name="ar_small_wait")
        summed = _sum_in_device_order(packed, land, name="ar_small_sum")
        return summed[:-8], summed[-8, 0]

    def mixer_reduced(self, after):
        send_sems, recv_sems, thru, _ = self.scatter_mix
        done = _split_wait(thru, send_sems, recv_sems, after, _direct_copies(2), name="rs_direct_mix_wait")
        return [_sum_direct([done[k]], [done[2 + k]], name=f"rs_sum_{tag}")[0] for k, tag in enumerate(["w_in", "w_out"])]

    def before_ffn1_bwd(self, after):
        send_sems, recv_sems, thru, _ = self.scatter
        n = len(thru) // 2
        done = _split_wait(thru, send_sems, recv_sems, after, _direct_copies(n), name="rs_direct_ffn2_wait")
        self.reduced = list(_sum_direct(done[:n], done[n:], name="rs_sum_ffn2"))


def _adamw(gs, ws, ms, vs, *, name, tm=256):
    n = len(gs)
    r, cdim = ws[0].shape[-2:]
    tm = r if tm is None else min(tm, r)
    assert r % tm == 0, (r, tm)
    nb = r // tm
    c1 = 1.0 / (1.0 - ADAM_B1 ** ADAM_STEP)
    c2 = 1.0 / (1.0 - ADAM_B2 ** ADAM_STEP)
    flat = pl.BlockSpec((tm, cdim), lambda i: (i, 0))

    g_ops, g_specs, g_where = [], [], []
    for g in gs:
        g_where.append(len(g_ops))
        if not isinstance(g, tuple):
            g_ops.append(g)
            g_specs.append(flat)
        elif g[2] == 1:
            g_ops += [g[0], g[1]]
            g_specs += [pl.BlockSpec((tm, cdim // 2), lambda i: (i, 0))] * 2
        else:
            g_ops += [g[0], g[1]]
            g_specs += [pl.BlockSpec((tm, cdim), lambda i: (jnp.minimum(i, nb // 2 - 1), 0)),
                        pl.BlockSpec((tm, cdim), lambda i: (jnp.maximum(i - nb // 2, 0), 0))]
    ng = len(g_ops)

    def gradient(refs, k):
        g, at = gs[k], g_where[k]
        if not isinstance(g, tuple):
            return refs[at][...]
        if g[2] == 1:
            return jnp.concatenate([refs[at][...], refs[at + 1][...]], axis=1)
        return jnp.where(pl.program_id(0) < nb // 2, refs[at][...], refs[at + 1][...])

    def body(*refs):
        rest = refs[ng:]
        for k in range(n):
            g = gradient(refs, k)
            w = rest[k][...]
            m = ADAM_B1 * rest[n + k][...] + (1.0 - ADAM_B1) * g
            v = ADAM_B2 * rest[2 * n + k][...] + (1.0 - ADAM_B2) * (g * g)
            rest[3 * n + k][...] = g
            rest[4 * n + k][...] = -ADAM_LR * ((m * c1) / (jnp.sqrt(v * c2) + ADAM_EPS) + ADAM_WD * w)
            rest[5 * n + k][...] = m
            rest[6 * n + k][...] = v

    like_w = flat if ws[0].ndim == 2 else pl.BlockSpec((None, tm, cdim), lambda i: (0, i, 0))
    outs = _pcall(
        body, name=name, grid=(nb,), in_specs=g_specs + [like_w] * (3 * n), out_specs=[like_w] * (4 * n),
        out_shape=[jax.ShapeDtypeStruct(ws[0].shape, f32)] * (4 * n),
        compiler_params=_params(1),
    )(*g_ops, *ws, *ms, *vs)
    return outs[:n], outs[n:2 * n], outs[2 * n:3 * n], outs[3 * n:]


BIG = ["ffn1_w_gate", "ffn1_w_up", "ffn1_w_down", "ffn2_w_gate", "ffn2_w_up", "ffn2_w_down"]
SMALL = ["ln1_g", "ln1_b", "b_forget", "conv_w", "conv_b", "rg_wa", "rg_ba", "rg_wx", "rg_bx", "lru_lambda",
         "ln2_g", "ln2_b", "ln3_g", "ln3_b"]
WEIGHTS = ["ffn1_w_gate", "ffn1_w_up", "ffn1_w_down", "ln1_g", "ln1_b", "w_in", "b_forget", "conv_w", "conv_b",
           "rg_wa", "rg_ba", "rg_wx", "rg_bx", "lru_lambda", "w_out", "ln2_g", "ln2_b",
           "ffn2_w_gate", "ffn2_w_up", "ffn2_w_down", "ln3_g", "ln3_b"]


def _pack_small(parts):
    rows = []
    for n in SMALL:
        flat = parts[n].reshape(-1)
        pad = (-flat.shape[0]) % LANES
        rows.append(jnp.pad(flat, (0, pad)).reshape(-1, LANES))
    packed = jnp.concatenate(rows, axis=0)
    return jnp.pad(packed, ((0, (-packed.shape[0]) % 8), (0, 0)))


def _unpack_small(packed, shapes):
    out, r0 = {}, 0
    for n in SMALL:
        size = math.prod(shapes[n])
        nr = -(-size // LANES)
        out[n] = packed[r0:r0 + nr].reshape(-1)[:size].reshape(shapes[n])
        r0 += nr
    return out


def kernel(x, ffn1_w_gate, ffn1_w_up, ffn1_w_down, ln1_g, ln1_b, w_in, b_forget, conv_w, conv_b, rg_wa, rg_ba, rg_wx, rg_bx, lru_lambda, w_out, ln2_g, ln2_b, ffn2_w_gate, ffn2_w_up, ffn2_w_down, ln3_g, ln3_b, loss_target, m_ffn1_w_gate, m_ffn1_w_up, m_ffn1_w_down, m_ln1_g, m_ln1_b, m_w_in, m_b_forget, m_conv_w, m_conv_b, m_rg_wa, m_rg_ba, m_rg_wx, m_rg_bx, m_lru_lambda, m_w_out, m_ln2_g, m_ln2_b, m_ffn2_w_gate, m_ffn2_w_up, m_ffn2_w_down, m_ln3_g, m_ln3_b, v_ffn1_w_gate, v_ffn1_w_up, v_ffn1_w_down, v_ln1_g, v_ln1_b, v_w_in, v_b_forget, v_conv_w, v_conv_b, v_rg_wa, v_rg_ba, v_rg_wx, v_rg_bx, v_lru_lambda, v_w_out, v_ln2_g, v_ln2_b, v_ffn2_w_gate, v_ffn2_w_up, v_ffn2_w_down, v_ln3_g, v_ln3_b):
    args = dict(locals())
    w = {n: args[n] for n in WEIGHTS}
    mom = {n: args["m_" + n] for n in WEIGHTS}
    var = {n: args["v_" + n] for n in WEIGHTS}
    chip = 2 * lax.axis_index("x") + lax.axis_index("y")

    g1 = _all_gather_bf16([w[n][0] for n in BIG[:2]], name="ag_ffn1_up")
    fs = D_FF // N_SHARD
    full = dict(
        f1g=g1[0].reshape(N_SHARD, D_MODEL, fs), f1u=g1[1].reshape(N_SHARD, D_MODEL, fs),
        bfp=jnp.pad(b_forget, ((0, 0), (0, LANES - HEADS))),
        ln1_g=ln1_g, ln1_b=ln1_b, ln2_g=ln2_g, ln2_b=ln2_b, ln3_g=ln3_g, ln3_b=ln3_b,
        conv_b=conv_b, rg_wa=rg_wa[0], rg_wx=rg_wx[0], rg_ba=rg_ba[0], rg_bx=rg_bx[0], lam=lru_lambda,
    )
    cw_place = lax.dynamic_update_slice(jnp.zeros((8, LRU_W), f32), conv_w[0] * 0.5, (0, chip * (LRU_W // N_SHARD)))
    cw_full = _all_reduce_small(cw_place.reshape(-1, LANES), g1[0], name="ag_conv_w")
    full["conv_w"] = cw_full.reshape(8, LRU_W)[:CONV_K]

    hooks = _Overlap([w["ffn1_w_down"][0], w["w_in"][0], w["w_out"][0]], [w[n][0] for n in BIG[3:]], cw_full)
    loss_rep, dx, g = _local_step(x[0], loss_target[0], full, hooks)

    token1 = hooks.ffn1_grads(g["f1"][1])
    red = _join_halves(hooks.reduced + hooks.mixer_reduced([token1]), name="rs_join_rest")
    grads = dict(zip(BIG[3:] + ["w_in", "w_out"], red))

    small_sum, loss = hooks.small_summed(red)
    small_shapes = {n: w[n].shape for n in SMALL}
    small_shapes["conv_w"] = (1, CONV_K, LRU_W)
    gs_red = _unpack_small(small_sum, small_shapes)
    gs_red["conv_w"] = lax.dynamic_slice(gs_red["conv_w"], (0, 0, chip * (LRU_W // N_SHARD)),
                                         (1, CONV_K, LRU_W // N_SHARD))
    grads.update(gs_red)

    delta, new_m, new_v = {}, {}, {}

    def adamw(names, name, **kw):
        g3, d, nm, nv = _adamw([grads[n] for n in names], [w[n] for n in names], [mom[n] for n in names],
                               [var[n] for n in names], name=name, **kw)
        for i, n in enumerate(names):
            grads[n], delta[n], new_m[n], new_v[n] = g3[i], d[i], nm[i], nv[i]

    adamw(BIG[3:], "adamw_ffn2", tm=128)
    adamw(["w_in"], "adamw_w_in")
    adamw(["w_out"], "adamw_w_out")
    shard_shapes = {n: w[n].shape for n in SMALL}
    _, d, nm, nv = _adamw([_pack_small({n: grads[n] for n in SMALL})], [_pack_small({n: w[n] for n in SMALL})],
                          [_pack_small({n: mom[n] for n in SMALL})], [_pack_small({n: var[n] for n in SMALL})],
                          name="adamw_small", tm=None)
    for dst, packed in ((delta, d[0]), (new_m, nm[0]), (new_v, nv[0])):
        dst.update(_unpack_small(packed, shard_shapes))

    worked = [new_v["ffn2_w_down"], new_v["w_in"], new_v["w_out"], nv[0]]
    ga, ua, da, gb, ub, db = _join_halves(hooks.ffn1_reduced(worked), name="rs_join_ffn1")
    grads.update(ffn1_w_gate=(ga, gb, 1), ffn1_w_up=(ua, ub, 1), ffn1_w_down=(da, db, 0))
    adamw(BIG[:3], "adamw_ffn1", tm=128)

    def shaped(tree, n):
        return tree[n].reshape(w[n].shape)

    return (loss, dx[None], *[shaped(grads, n) for n in WEIGHTS], *[shaped(delta, n) for n in WEIGHTS],
            *[shaped(new_m, n) for n in WEIGHTS], *[shaped(new_v, n) for n in WEIGHTS])
```

```python
import functools
import math

import jax
import jax.numpy as jnp
from jax import lax
from jax.experimental import pallas as pl
from jax.experimental.pallas import tpu as pltpu

f32 = jnp.float32
MXU_DTYPE = jnp.bfloat16
GRAD_DTYPE = jnp.bfloat16

D_MODEL = 1024
D_FF = 4096
N_SHARD = 4
N_DEV = 8
FOX_W = 512
LRU_W = 512
HEADS = 8
HEAD_DIM = 64
CONV_K = 4
IN_COLS = 2568
IN_SHARD = IN_COLS // N_SHARD
QKV_W = 3 * FOX_W
Z_PAD = 2688
CAST_COLS = 384
LANES = 128
LN_EPS = 1e-5
DN_ALPHA = 2.0 ** 0.25
LRU_C = 8.0
NEG_BIG = -1e30
VMEM_LIMIT = 56 * 1024 * 1024

ADAM_LR = 0.001
ADAM_B1 = 0.9
ADAM_B2 = 0.999
ADAM_EPS = 1e-08
ADAM_WD = 0.01
ADAM_STEP = 10


def _pcall(body, **kw):
    return pl.pallas_call(body, **kw)


def _params(n_grid, vmem=VMEM_LIMIT):
    return pltpu.CompilerParams(dimension_semantics=("arbitrary",) * n_grid, vmem_limit_bytes=vmem)


def _dot(a, b):
    return jnp.dot(a, b, preferred_element_type=f32)


def _dot_nt(a, b):
    return lax.dot_general(a, b, (((1,), (1,)), ((), ())), preferred_element_type=f32)


def _dot_tn(a, b):
    return lax.dot_general(a, b, (((0,), (0,)), ((), ())), preferred_element_type=f32)


def _sigmoid(x):
    return 1.0 / (1.0 + jnp.exp(-x))


def _layer_norm_stats(y):
    mu = jnp.mean(y, axis=-1, keepdims=True)
    yc = y - mu
    var = jnp.mean(yc * yc, axis=-1, keepdims=True)
    rstd = lax.rsqrt(var + LN_EPS)
    return yc * rstd, rstd


def _ln_backward(dy, xhat, rstd, gamma):
    dxhat = dy * gamma
    m1 = jnp.mean(dxhat, axis=-1, keepdims=True)
    m2 = jnp.mean(dxhat * xhat, axis=-1, keepdims=True)
    dyp = rstd * (dxhat - m1 - xhat * m2)
    return dyp, jnp.sum(dy * xhat, axis=0, keepdims=True), jnp.sum(dy, axis=0, keepdims=True)


def _ffn_fwd_loss(x, wg, wu, wd, ln_g, ln_b, target, *, name, tm=1024, tf=512):
    T = x.shape[0]
    tm = min(tm, T)
    tr = min(256, tm)
    fs = D_FF // N_SHARD
    cpf = fs // tf
    nf = D_FF // tf
    nt = T // tm

    def body(x_ref, wg_ref, wu_ref, wd_ref, g_ref, b_ref, t_ref,
             xb_ref, gact_ref, uact_ref, dyp_ref, dgam_ref, dbeta_ref, loss_ref, acc_ref):
        i = pl.program_id(0)
        f = pl.program_id(1)

        @pl.when(jnp.logical_and(i == 0, f == 0))
        def _():
            dgam_ref[...] = jnp.zeros_like(dgam_ref)
            dbeta_ref[...] = jnp.zeros_like(dbeta_ref)
            loss_ref[...] = jnp.zeros_like(loss_ref)

        @pl.when(f == 0)
        def _():
            xb_ref[...] = x_ref[...].astype(MXU_DTYPE)
            acc_ref[...] = jnp.zeros_like(acc_ref)

        xb = xb_ref[...]
        g = _dot(xb, wg_ref[...])
        u = _dot(xb, wu_ref[...])
        h = (g * _sigmoid(g)) * u
        gact_ref[...] = g.astype(gact_ref.dtype)
        uact_ref[...] = u.astype(uact_ref.dtype)
        acc_ref[...] += _dot(h.astype(MXU_DTYPE), wd_ref[...])

        @pl.when(f == nf - 1)
        def _():
            gamma = g_ref[...]

            def rows_chunk(r, carry):
                rows = pl.ds(pl.multiple_of(r * tr, tr), tr)
                xhat, rstd = _layer_norm_stats(DN_ALPHA * x_ref[rows, :] + 0.5 * acc_ref[rows, :])
                err = xhat * gamma + b_ref[...] - t_ref[rows, :]
                sq = jnp.sum(jnp.sum(err * err, axis=0, keepdims=True), axis=1, keepdims=True)
                loss_ref[...] += jnp.broadcast_to(sq * (0.5 / D_MODEL), loss_ref.shape)
                dyp, dgam, dbeta = _ln_backward(err * (1.0 / D_MODEL), xhat, rstd, gamma)
                dyp_ref[rows, :] = dyp
                dgam_ref[...] += dgam
                dbeta_ref[...] += dbeta
                return carry

            lax.fori_loop(0, tm // tr, rows_chunk, 0)

    row = lambda i, f: (i, 0)
    const = lambda i, f: (0, 0)
    tile = pl.BlockSpec((tm, tf), lambda i, f: (i, f))
    cols = pl.BlockSpec((None, D_MODEL, tf), lambda i, f: (f // cpf, 0, f % cpf))
    last = lambda i, f: (jnp.where(f == nf - 1, i, jnp.maximum(i - 1, 0)), 0)
    return _pcall(
        body, name=name, grid=(nt, nf),
        in_specs=[pl.BlockSpec((tm, D_MODEL), row), cols, cols,
                  pl.BlockSpec((None, tf, D_MODEL), lambda i, f: (f // cpf, f % cpf, 0)),
                  pl.BlockSpec((1, D_MODEL), const), pl.BlockSpec((1, D_MODEL), const),
                  pl.BlockSpec((tm, D_MODEL), last)],
        out_specs=[pl.BlockSpec((tm, D_MODEL), row), tile, tile, pl.BlockSpec((tm, D_MODEL), row),
                   pl.BlockSpec((1, D_MODEL), const), pl.BlockSpec((1, D_MODEL), const), pl.BlockSpec((1, LANES), const)],
        out_shape=[jax.ShapeDtypeStruct((T, D_MODEL), MXU_DTYPE), jax.ShapeDtypeStruct((T, D_FF), MXU_DTYPE),
                   jax.ShapeDtypeStruct((T, D_FF), MXU_DTYPE), jax.ShapeDtypeStruct((T, D_MODEL), f32),
                   jax.ShapeDtypeStruct((1, D_MODEL), f32), jax.ShapeDtypeStruct((1, D_MODEL), f32),
                   jax.ShapeDtypeStruct((1, LANES), f32)],
        scratch_shapes=[pltpu.VMEM((tm, D_MODEL), f32)],
        compiler_params=_params(2),
    )(x, wg, wu, wd, ln_g, ln_b, target)


def _ffn_up(x, wg, wu, after=None, *, name, tm=1024, tf=512):
    T = x.shape[0]
    tm = min(tm, T)
    cpf = (D_FF // N_SHARD) // tf
    nf = D_FF // tf
    extra = [] if after is None else [after]

    def body(x_ref, wg_ref, wu_ref, *refs):
        xb_ref, gact_ref, uact_ref, hact_ref = refs[len(extra):]

        @pl.when(pl.program_id(1) == 0)
        def _():
            xb_ref[...] = x_ref[...].astype(MXU_DTYPE)

        xb = xb_ref[...]
        g = _dot(xb, wg_ref[...])
        u = _dot(xb, wu_ref[...])
        gact_ref[...] = g.astype(gact_ref.dtype)
        uact_ref[...] = u.astype(uact_ref.dtype)
        hact_ref[...] = ((g * _sigmoid(g)) * u).astype(hact_ref.dtype)

    row = lambda i, f: (i, 0)
    tile = pl.BlockSpec((tm, tf), lambda i, f: (i, f))
    cols = pl.BlockSpec((None, D_MODEL, tf), lambda i, f: (f // cpf, 0, f % cpf))
    return _pcall(
        body, name=name, grid=(T // tm, nf),
        in_specs=[pl.BlockSpec((tm, D_MODEL), row), cols, cols] + [pl.BlockSpec(memory_space=pl.ANY)] * len(extra),
        out_specs=[pl.BlockSpec((tm, D_MODEL), row), tile, tile, tile],
        out_shape=[jax.ShapeDtypeStruct((T, D_MODEL), MXU_DTYPE)] + [jax.ShapeDtypeStruct((T, D_FF), MXU_DTYPE)] * 3,
        compiler_params=_params(2),
    )(x, wg, wu, *extra)


def _ffn_down_ln(x, hact, wd, ln_g, ln_b, *, name, tm=1024):
    T = x.shape[0]
    tm = min(tm, T)
    fs = D_FF // N_SHARD
    ks = 2
    nk = N_SHARD // ks

    def body(x_ref, h_ref, wd_ref, g_ref, b_ref, xhat_ref, xn_ref, rstd_ref, acc_ref):
        k = pl.program_id(1)

        @pl.when(k == 0)
        def _():
            acc_ref[...] = jnp.zeros_like(acc_ref)

        acc_ref[...] += _dot(h_ref[...], wd_ref[...].reshape(ks * fs, D_MODEL))

        @pl.when(k == nk - 1)
        def _():
            xhat, rstd = _layer_norm_stats(DN_ALPHA * x_ref[...] + 0.5 * acc_ref[...])
            xhat_ref[...] = xhat
            xn_ref[...] = (xhat * g_ref[...] + b_ref[...]).astype(xn_ref.dtype)
            rstd_ref[...] = jnp.broadcast_to(rstd, rstd_ref.shape)

    row = lambda i, k: (i, 0)
    vec = pl.BlockSpec((1, D_MODEL), lambda i, k: (0, 0))
    return _pcall(
        body, name=name, grid=(T // tm, nk),
        in_specs=[pl.BlockSpec((tm, D_MODEL), row), pl.BlockSpec((tm, ks * fs), lambda i, k: (i, k)),
                  pl.BlockSpec((ks, fs, D_MODEL), lambda i, k: (k, 0, 0)), vec, vec],
        out_specs=[pl.BlockSpec((tm, D_MODEL), row), pl.BlockSpec((tm, D_MODEL), row), pl.BlockSpec((tm, LANES), row)],
        out_shape=[jax.ShapeDtypeStruct((T, D_MODEL), f32), jax.ShapeDtypeStruct((T, D_MODEL), MXU_DTYPE),
                   jax.ShapeDtypeStruct((T, LANES), f32)],
        scratch_shapes=[pltpu.VMEM((tm, D_MODEL), f32)],
        compiler_params=_params(2),
    )(x, hact, wd, ln_g, ln_b)


def _ffn_bwd(dyp, xb, gact, uact, wg, wu, wd, after=None, *, name, tm=512, tf=512, part=None, dx_init=None):
    T = dyp.shape[0]
    tm = min(tm, T)
    fs = D_FF // N_SHARD
    cpf = fs // tf
    nt = T // tm
    nf = D_FF // tf if part is None else N_SHARD
    wf = fs if part is None else tf
    slab = (lambda f: f // cpf) if part is None else (lambda f: f)
    chunk = (lambda f: f % cpf) if part is None else (lambda f: part)
    extra = ([] if dx_init is None else [dx_init]) + ([] if after is None else [after])

    def body(dyp_ref, xb_ref, g_ref, u_ref, wg_ref, wu_ref, wd_ref, *refs):
        dx_hbm, dwg_ref, dwu_ref, dwd_ref, dx_sc, dwg_sc, dwu_sc, dwd_sc, sem = refs[len(extra):]
        f = pl.program_id(0)
        i = pl.program_id(1)
        rows = pl.ds(pl.multiple_of(i * tm, tm), tm)
        dyp_t = dyp_ref[...]
        dy = (0.5 * dyp_t).astype(MXU_DTYPE)

        @pl.when(i == 0)
        def _():
            dwg_sc[...] = jnp.zeros_like(dwg_sc)
            dwu_sc[...] = jnp.zeros_like(dwu_sc)
            dwd_sc[...] = jnp.zeros_like(dwd_sc)

        @pl.when(f == 0)
        def _():
            dx_sc[rows, :] = DN_ALPHA * dyp_t if dx_init is None else refs[0][...]

        g = g_ref[...].astype(f32)
        u = u_ref[...].astype(f32)
        sig = _sigmoid(g)
        silu = g * sig
        dh = _dot_nt(dy, wd_ref[...])
        dg = (dh * u * (sig * (1.0 + g * (1.0 - sig)))).astype(MXU_DTYPE)
        du = (dh * silu).astype(MXU_DTYPE)
        hb = (silu * u).astype(MXU_DTYPE)
        dx_sc[rows, :] += _dot_nt(dg, wg_ref[...]) + _dot_nt(du, wu_ref[...])
        xb_t = xb_ref[...]
        dwg_sc[...] += _dot_tn(xb_t, dg)
        dwu_sc[...] += _dot_tn(xb_t, du)
        dwd_sc[...] += _dot_tn(hb, dy)

        @pl.when(i == nt - 1)
        def _():
            dwg_ref[...] = dwg_sc[...].astype(dwg_ref.dtype)
            dwu_ref[...] = dwu_sc[...].astype(dwu_ref.dtype)
            dwd_ref[...] = dwd_sc[...].astype(dwd_ref.dtype)

        @pl.when(jnp.logical_and(f == nf - 1, i == nt - 1))
        def _():
            cp = pltpu.make_async_copy(dx_sc, dx_hbm, sem)
            cp.start()
            cp.wait()

    row = lambda f, i: (i, 0)
    return _pcall(
        body, name=name, grid=(nf, nt),
        in_specs=[
            pl.BlockSpec((tm, D_MODEL), row),
            pl.BlockSpec((tm, D_MODEL), row),
            pl.BlockSpec((tm, tf), lambda f, i: (i, slab(f) * cpf + chunk(f))),
            pl.BlockSpec((tm, tf), lambda f, i: (i, slab(f) * cpf + chunk(f))),
            pl.BlockSpec((None, D_MODEL, tf), lambda f, i: (slab(f), 0, chunk(f))),
            pl.BlockSpec((None, D_MODEL, tf), lambda f, i: (slab(f), 0, chunk(f))),
            pl.BlockSpec((None, tf, D_MODEL), lambda f, i: (slab(f), chunk(f), 0)),
        ] + ([] if dx_init is None else [pl.BlockSpec((tm, D_MODEL), row)])
        + ([] if after is None else [pl.BlockSpec(memory_space=pl.ANY)]),
        out_specs=[
            pl.BlockSpec(memory_space=pl.ANY),
            pl.BlockSpec((None, D_MODEL, tf), lambda f, i: (slab(f), 0, chunk(f) if part is None else 0)),
            pl.BlockSpec((None, D_MODEL, tf), lambda f, i: (slab(f), 0, chunk(f) if part is None else 0)),
            pl.BlockSpec((None, tf, D_MODEL), lambda f, i: (slab(f), chunk(f) if part is None else 0, 0)),
        ],
        out_shape=[
            jax.ShapeDtypeStruct((T, D_MODEL), f32),
            jax.ShapeDtypeStruct((N_SHARD, D_MODEL, wf), GRAD_DTYPE),
            jax.ShapeDtypeStruct((N_SHARD, D_MODEL, wf), GRAD_DTYPE),
            jax.ShapeDtypeStruct((N_SHARD, wf, D_MODEL), GRAD_DTYPE),
        ],
        scratch_shapes=[pltpu.VMEM((T, D_MODEL), f32), pltpu.VMEM((D_MODEL, tf), f32),
                        pltpu.VMEM((D_MODEL, tf), f32), pltpu.VMEM((tf, D_MODEL), f32),
                        pltpu.SemaphoreType.DMA],
        compiler_params=_params(2),
    )(dyp, xb, gact, uact, wg, wu, wd, *extra)


def _proj_in(xn, wp, bfp, *, name, tm=512):
    T = xn.shape[0]
    tm = min(tm, T)
    nt = T // tm

    def body(x_ref, w_ref, b_ref, lxg_ref, fg_ref, qa_ref, ka_ref, va_ref, carry):
        i = pl.program_id(0)

        @pl.when(i == 0)
        def _():
            carry[...] = jnp.zeros_like(carry)

        z = _dot(x_ref[...], w_ref[...])
        lxg_ref[...] = z[:, QKV_W:QKV_W + 2 * LRU_W]
        fg = z[:, QKV_W + 2 * LRU_W:] + b_ref[...]
        fg_ref[...] = fg
        ls = jnp.minimum(fg, 0.0) - jnp.log(1.0 + jnp.exp(-jnp.abs(fg)))
        r = lax.broadcasted_iota(jnp.int32, (tm, tm), 0)
        c = lax.broadcasted_iota(jnp.int32, (tm, tm), 1)
        cum = _tri_dot(jnp.where(r >= c, 1.0, 0.0).astype(jnp.bfloat16), ls) + carry[0:1, :]
        carry[...] = jnp.broadcast_to(cum[tm - 1:tm, :], carry.shape)

        lane = lax.broadcasted_iota(jnp.int32, (tm, LANES), 1)
        low = lane < HEAD_DIM
        ones_q = jnp.where(jnp.logical_and(lane >= AUX + 3, lane < AUX + 6), 1.0, 0.0)
        ones_k = jnp.where(jnp.logical_and(lane >= AUX, lane < AUX + 3), 1.0, 0.0)
        for j in range(HEADS // 2):
            pair = [z[:, t * FOX_W + j * LANES:t * FOX_W + (j + 1) * LANES] for t in range(3)]
            for odd in range(2):
                h = 2 * j + odd
                q, k, v = [_swap_lane_halves(a) if odd else a for a in pair]
                hi, mid, lo = [a.astype(f32) for a in _split3(jnp.broadcast_to(cum[:, h:h + 1], (tm, LANES)))]
                aux_q = jnp.where(lane == AUX, hi, jnp.where(lane == AUX + 1, mid, jnp.where(lane == AUX + 2, lo, ones_q)))
                aux_k = jnp.where(lane == AUX + 3, -hi,
                                  jnp.where(lane == AUX + 4, -mid, jnp.where(lane == AUX + 5, -lo, ones_k)))
                blk = slice(h * LANES, (h + 1) * LANES)
                qa_ref[:, blk] = jnp.where(low, q, aux_q).astype(qa_ref.dtype)
                ka_ref[:, blk] = jnp.where(low, k, aux_k).astype(ka_ref.dtype)
                va_ref[:, blk] = jnp.where(low, v, 1.0).astype(va_ref.dtype)

    row = lambda i: (i, 0)
    const = lambda i: (0, 0)
    return _pcall(
        body, name=name, grid=(nt,),
        in_specs=[pl.BlockSpec((tm, D_MODEL), row), pl.BlockSpec((D_MODEL, Z_PAD), const),
                  pl.BlockSpec((1, LANES), const)],
        out_specs=[pl.BlockSpec((tm, 2 * LRU_W), row), pl.BlockSpec((tm, LANES), row)]
        + [pl.BlockSpec((tm, HEADS * LANES), row)] * 3,
        out_shape=[jax.ShapeDtypeStruct((T, 2 * LRU_W), f32), jax.ShapeDtypeStruct((T, LANES), f32)]
        + [jax.ShapeDtypeStruct((T, HEADS * LANES), MXU_DTYPE)] * 3,
        scratch_shapes=[pltpu.VMEM((8, LANES), f32)],
        compiler_params=_params(1),
    )(xn, wp, bfp)


def _proj_in_bwd(dqa, dka, dva, dlxg, fgb, xn, dyp, wp, xhat, rstd, ln_g, *, name, tm=512):
    T = xn.shape[0]
    tm = min(tm, T)
    nt = T // tm

    def body(dq_ref, dk_ref, dv_ref, dl_ref, fg_ref, x_ref, dyp_ref, w_ref, xhat_ref, rstd_ref, g_ref,
             dpre_ref, dw_hbm, dgam_ref, dbeta_ref, dbf_ref, dw_sc, dw_out, carry, sem):
        i = pl.program_id(0)

        @pl.when(i == 0)
        def _():
            dw_sc[...] = jnp.zeros_like(dw_sc)
            dgam_ref[...] = jnp.zeros_like(dgam_ref)
            dbeta_ref[...] = jnp.zeros_like(dbeta_ref)
            dbf_ref[...] = jnp.zeros_like(dbf_ref)
            carry[...] = jnp.zeros_like(carry)

        lane = lax.broadcasted_iota(jnp.int32, (tm, LANES), 1)
        dc = jnp.zeros((tm, LANES), f32)
        for h in range(HEADS):
            row_sum = dq_ref[:, h * LANES + AUX:h * LANES + AUX + 1]
            col_sum = dk_ref[:, h * LANES + AUX + 3:h * LANES + AUX + 4]
            dc = jnp.where(lane == h, jnp.broadcast_to(row_sum - col_sum, (tm, LANES)), dc)
        r = lax.broadcasted_iota(jnp.int32, (tm, tm), 0)
        c = lax.broadcasted_iota(jnp.int32, (tm, tm), 1)
        dls = _tri_dot(jnp.where(c >= r, 1.0, 0.0).astype(jnp.bfloat16), dc) + carry[0:1, :]
        carry[...] = jnp.broadcast_to(dls[0:1, :], carry.shape)
        dfg = dls * _sigmoid(-fg_ref[...])
        dbf_ref[...] += jnp.sum(dfg, axis=0, keepdims=True)

        low = _low_lanes((tm, LANES))

        def packed(ref):
            pairs = [jnp.where(low, ref[:, (2 * j) * LANES:(2 * j + 1) * LANES],
                               _swap_lane_halves(ref[:, (2 * j + 1) * LANES:(2 * j + 2) * LANES]))
                     for j in range(HEADS // 2)]
            return jnp.concatenate(pairs, axis=1).astype(MXU_DTYPE)

        dz = jnp.concatenate(
            [packed(dq_ref), packed(dk_ref), packed(dv_ref),
             dl_ref[...].astype(MXU_DTYPE), dfg.astype(MXU_DTYPE)], axis=1)
        dx = DN_ALPHA * dyp_ref[...] + _dot_nt(dz, w_ref[...])
        dpre, dgam, dbeta = _ln_backward(dx, xhat_ref[...], rstd_ref[:, 0:1], g_ref[...])
        dpre_ref[...] = dpre
        dgam_ref[...] += dgam
        dbeta_ref[...] += dbeta
        dw_sc[...] += _dot_tn(x_ref[...], dz)

        @pl.when(i == nt - 1)
        def _():
            dw_sc[:, :FOX_W] = dw_sc[:, :FOX_W] * (1.0 / math.sqrt(HEAD_DIM))
            for c0 in range(0, Z_PAD, CAST_COLS):
                dw_out[:, c0:c0 + CAST_COLS] = dw_sc[:, c0:c0 + CAST_COLS].astype(dw_out.dtype)
            cp = pltpu.make_async_copy(dw_out, dw_hbm, sem)
            cp.start()
            cp.wait()

    row = lambda i: (nt - 1 - i, 0)
    const = lambda i: (0, 0)
    return _pcall(
        body, name=name, grid=(nt,),
        in_specs=[pl.BlockSpec((tm, HEADS * LANES), row), pl.BlockSpec((tm, HEADS * LANES), row),
                  pl.BlockSpec((tm, HEADS * LANES), row),
                  pl.BlockSpec((tm, 2 * LRU_W), row), pl.BlockSpec((tm, LANES), row),
                  pl.BlockSpec((tm, D_MODEL), row), pl.BlockSpec((tm, D_MODEL), row),
                  pl.BlockSpec((D_MODEL, Z_PAD), const),
                  pl.BlockSpec((tm, D_MODEL), row), pl.BlockSpec((tm, LANES), row), pl.BlockSpec((1, D_MODEL), const)],
        out_specs=[pl.BlockSpec((tm, D_MODEL), row), pl.BlockSpec(memory_space=pl.ANY),
                   pl.BlockSpec((1, D_MODEL), const), pl.BlockSpec((1, D_MODEL), const), pl.BlockSpec((1, LANES), const)],
        out_shape=[jax.ShapeDtypeStruct((T, D_MODEL), f32), jax.ShapeDtypeStruct((D_MODEL, Z_PAD), GRAD_DTYPE),
                   jax.ShapeDtypeStruct((1, D_MODEL), f32), jax.ShapeDtypeStruct((1, D_MODEL), f32),
                   jax.ShapeDtypeStruct((1, LANES), f32)],
        scratch_shapes=[pltpu.VMEM((D_MODEL, Z_PAD), f32), pltpu.VMEM((D_MODEL, Z_PAD), GRAD_DTYPE),
                        pltpu.VMEM((8, LANES), f32), pltpu.SemaphoreType.DMA],
        compiler_params=_params(1),
    )(dqa, dka, dva, dlxg, fgb, xn, dyp, wp, xhat, rstd, ln_g)


def _split3(x):
    hi = x.astype(jnp.bfloat16)
    r1 = x - hi.astype(f32)
    mid = r1.astype(jnp.bfloat16)
    lo = (r1 - mid.astype(f32)).astype(jnp.bfloat16)
    return hi, mid, lo


def _tri_dot(tri, x):
    hi, mid, lo = _split3(x)
    return _dot(tri, hi) + _dot(tri, mid) + _dot(tri, lo)


FOX_PAD = HEADS * LANES
AUX = HEAD_DIM


def _low_lanes(shape):
    return lax.broadcasted_iota(jnp.int32, shape, 1) < HEAD_DIM


def _swap_lane_halves(x):
    return pltpu.roll(x, HEAD_DIM, 1)


def _future_keys(tq, tk):
    r = lax.broadcasted_iota(jnp.int32, (tq, tk), 0)
    c = lax.broadcasted_iota(jnp.int32, (tq, tk), 1)
    return c > r


def _causal_steps(nq, key_major):
    if key_major:
        pairs = [(qi, ki) for ki in range(nq) for qi in range(ki, nq)]
    else:
        pairs = [(qi, ki) for qi in range(nq) for ki in range(qi + 1)]
    return (jnp.asarray([p[0] for p in pairs], jnp.int32), jnp.asarray([p[1] for p in pairs], jnp.int32))


def _fox_fwd(qa, ka, va, *, name, tq=512, hps=8):
    T = qa.shape[0]
    tq = min(tq, T)
    tk = tq
    nq = T // tq
    rep = tk // LANES
    qi_tab, ki_tab = _causal_steps(nq, key_major=False)

    def body(qi_ref, ki_ref, qa_ref, ka_ref, va_ref, o_ref, lse_ref, m_sc, acc_sc):
        t = pl.program_id(1)
        qi = qi_ref[t]
        ki = ki_ref[t]

        @pl.when(ki == 0)
        def _():
            m_sc[...] = jnp.full_like(m_sc, NEG_BIG)
            acc_sc[...] = jnp.zeros_like(acc_sc)

        def tile(diagonal):
            for h in range(hps):
                blk = slice(h * LANES, (h + 1) * LANES)
                s = _dot_nt(qa_ref[:, blk], ka_ref[:, blk])
                if diagonal:
                    s = jnp.where(_future_keys(tq, tk), NEG_BIG, s)
                m_prev = m_sc[h]
                m_new = jnp.maximum(m_prev, jnp.max(s, axis=1, keepdims=True))
                p = jnp.exp(s - jnp.tile(m_new, (1, rep)))
                acc_sc[h] = jnp.exp(m_prev - m_new) * acc_sc[h] + _dot(p.astype(MXU_DTYPE), va_ref[:, blk])
                m_sc[h] = m_new

        @pl.when(ki < qi)
        def _():
            tile(False)

        @pl.when(ki == qi)
        def _():
            tile(True)
            low = _low_lanes((tq, LANES))
            outs = []
            for h in range(hps):
                acc = acc_sc[h]
                den = _swap_lane_halves(acc)
                outs.append(acc / den)
                lse_ref[h] = m_sc[h] + jnp.log(jnp.where(low, den, acc))
            for p in range(hps // 2):
                o_ref[:, p * LANES:(p + 1) * LANES] = jnp.where(low, outs[2 * p], _swap_lane_halves(outs[2 * p + 1]))

    pair = hps * LANES
    return _pcall(
        body, name=name,
        grid_spec=pltpu.PrefetchScalarGridSpec(
            num_scalar_prefetch=2, grid=(HEADS // hps, qi_tab.shape[0]),
            in_specs=[
                pl.BlockSpec((tq, pair), lambda j, t, qi_ref, ki_ref: (qi_ref[t], j)),
                pl.BlockSpec((tk, pair), lambda j, t, qi_ref, ki_ref: (ki_ref[t], j)),
                pl.BlockSpec((tk, pair), lambda j, t, qi_ref, ki_ref: (ki_ref[t], j)),
            ],
            out_specs=[pl.BlockSpec((tq, pair // 2), lambda j, t, qi_ref, ki_ref: (qi_ref[t], j)),
                       pl.BlockSpec((hps, tq, LANES), lambda j, t, qi_ref, ki_ref: (j, qi_ref[t], 0))],
            scratch_shapes=[pltpu.VMEM((hps, tq, LANES), f32)] * 2),
        out_shape=[jax.ShapeDtypeStruct((T, FOX_W), f32), jax.ShapeDtypeStruct((HEADS, T, LANES), f32)],
        compiler_params=_params(2),
    )(qi_tab, ki_tab, qa, ka, va)


def _fox_bwd(qa, ka, va, doa, lse, drep, *, name, tq=512, hps=8):
    T = qa.shape[0]
    tq = min(tq, T)
    tk = tq
    nq = T // tq
    rep = tk // LANES
    qi_tab, ki_tab = _causal_steps(nq, key_major=True)

    def body(qi_ref, ki_ref, qa_ref, ka_ref, va_ref, doa_ref, lse_ref, d_ref, dqa_ref, dka_ref, dva_ref, dk_sc, dv_sc):
        t = pl.program_id(1)
        qi = qi_ref[t]
        ki = ki_ref[t]
        rows = pl.ds(pl.multiple_of(qi * tq, tq), tq)

        @pl.when(t == 0)
        def _():
            dqa_ref[...] = jnp.zeros_like(dqa_ref)

        @pl.when(qi == ki)
        def _():
            dk_sc[...] = jnp.zeros_like(dk_sc)
            dv_sc[...] = jnp.zeros_like(dv_sc)

        def tile(diagonal):
            for h in range(hps):
                blk = slice(h * LANES, (h + 1) * LANES)
                qh, kh, doh = qa_ref[:, blk], ka_ref[:, blk], doa_ref[:, blk]
                p = jnp.exp(_dot_nt(qh, kh) - jnp.tile(lse_ref[h], (1, rep)))
                if diagonal:
                    p = jnp.where(_future_keys(tq, tk), 0.0, p)
                dp = _dot_nt(doh, va_ref[:, blk])
                ds = (p * (dp - jnp.tile(d_ref[h], (1, rep)))).astype(MXU_DTYPE)
                dv_sc[h] += _dot_tn(p.astype(MXU_DTYPE), doh)
                dk_sc[h] += _dot_tn(ds, qh)
                dqa_ref[rows, blk] += _dot(ds, kh)

        @pl.when(qi > ki)
        def _():
            tile(False)

        @pl.when(qi == ki)
        def _():
            tile(True)

        @pl.when(qi == nq - 1)
        def _():
            for h in range(hps):
                blk = slice(h * LANES, (h + 1) * LANES)
                dka_ref[:, blk] = dk_sc[h]
                dva_ref[:, blk] = dv_sc[h]

    pair = hps * LANES
    q_blk = lambda j, t, qi_ref, ki_ref: (qi_ref[t], j)
    k_blk = lambda j, t, qi_ref, ki_ref: (ki_ref[t], j)
    stat = pl.BlockSpec((hps, tq, LANES), lambda j, t, qi_ref, ki_ref: (j, qi_ref[t], 0))
    return _pcall(
        body, name=name,
        grid_spec=pltpu.PrefetchScalarGridSpec(
            num_scalar_prefetch=2, grid=(HEADS // hps, qi_tab.shape[0]),
            in_specs=[pl.BlockSpec((tq, pair), q_blk), pl.BlockSpec((tk, pair), k_blk), pl.BlockSpec((tk, pair), k_blk),
                      pl.BlockSpec((tq, pair), q_blk), stat, stat],
            out_specs=[pl.BlockSpec((T, pair), lambda j, t, qi_ref, ki_ref: (0, j)),
                       pl.BlockSpec((tk, pair), k_blk), pl.BlockSpec((tk, pair), k_blk)],
            scratch_shapes=[pltpu.VMEM((hps, tk, LANES), f32)] * 2),
        out_shape=[jax.ShapeDtypeStruct((T, FOX_PAD), f32)] * 3,
        compiler_params=_params(2),
    )(qi_tab, ki_tab, qa, ka, va, doa, lse, drep)


GELU_C = math.sqrt(2.0 / math.pi)
GELU_A = 0.044715


def _gelu(x):
    t = jnp.tanh(GELU_C * (x + GELU_A * x * x * x))
    return 0.5 * x * (1.0 + t), t


def _gelu_grad(x, t):
    return 0.5 * (1.0 + t) + 0.5 * x * (1.0 - t * t) * GELU_C * (1.0 + 3.0 * GELU_A * x * x)


EXPM1_SERIES_BELOW = 0.25


def _expm1(x, e):
    series = x * (1.0 + x * (1 / 2 + x * (1 / 6 + x * (1 / 24 + x * (1 / 120 + x * (1 / 720))))))
    return jnp.where(x > -EXPM1_SERIES_BELOW, series, e - 1.0)


def _softplus_neg(lam):
    return jnp.maximum(-lam, 0.0) + jnp.log(1.0 + jnp.exp(-jnp.abs(lam)))


def _lru_gates(u, wab_ref, bab_ref, lam_ref):
    pre = _dot(u.astype(MXU_DTYPE), wab_ref[...]) + bab_ref[...]
    r = _sigmoid(pre[:, :LRU_W])
    gi = _sigmoid(pre[:, LRU_W:])
    sp = _softplus_neg(lam_ref[...])
    log_a = -LRU_C * r * sp
    a = jnp.exp(log_a)
    s = jnp.sqrt(-_expm1(2.0 * log_a, a * a))
    return r, gi, sp, a, s


def _lru_fwd(lxg, conv_w, conv_b, wab, bab, lam, *, name, tc=512):
    T = lxg.shape[0]
    tc = min(tc, T)
    nc = T // tc

    def body(lx_ref, lg_ref, cw_ref, cb_ref, wab_ref, bab_ref, lam_ref,
             out_ref, u_ref, hs_ref, gates_ref, ext, a_sc, b_sc, h_sc):
        i = pl.program_id(0)

        @pl.when(i == 0)
        def _():
            ext[0:8, :] = jnp.zeros((8, LRU_W), f32)
            h_sc[...] = jnp.zeros_like(h_sc)

        ext[8:, :] = lx_ref[...]
        u = cb_ref[...] + cw_ref[0:1, :] * ext[pl.ds(5, tc), :]
        for k in range(1, CONV_K):
            u = u + cw_ref[k:k + 1, :] * ext[pl.ds(5 + k, tc), :]
        ext[0:8, :] = ext[tc:tc + 8, :]
        u_ref[...] = u
        r, gi, sp, a, s = _lru_gates(u, wab_ref, bab_ref, lam_ref)
        for n, gate in enumerate((r, gi, a, s)):
            gates_ref[:, n * LRU_W:(n + 1) * LRU_W] = gate
        a_sc[...] = a
        b_sc[...] = s * (gi * u)

        def step(t, h):
            h = a_sc[pl.ds(t, 1), :] * h + b_sc[pl.ds(t, 1), :]
            hs_ref[pl.ds(t, 1), :] = h
            return h

        h = lax.fori_loop(0, tc, step, h_sc[0:1, :], unroll=8)
        h_sc[...] = jnp.broadcast_to(h, h_sc.shape)
        gel, _ = _gelu(lg_ref[...])
        out_ref[...] = gel * hs_ref[...]

    row = lambda i: (i, 0)
    const = lambda i: (0, 0)
    return _pcall(
        body, name=name, grid=(nc,),
        in_specs=[pl.BlockSpec((tc, LRU_W), row), pl.BlockSpec((tc, LRU_W), lambda i: (i, 1)),
                  pl.BlockSpec((CONV_K, LRU_W), const), pl.BlockSpec((1, LRU_W), const),
                  pl.BlockSpec((LRU_W, 2 * LRU_W), const), pl.BlockSpec((1, 2 * LRU_W), const),
                  pl.BlockSpec((1, LRU_W), const)],
        out_specs=[pl.BlockSpec((tc, LRU_W), row)] * 3 + [pl.BlockSpec((tc, 4 * LRU_W), row)],
        out_shape=[jax.ShapeDtypeStruct((T, LRU_W), f32)] * 3 + [jax.ShapeDtypeStruct((T, 4 * LRU_W), f32)],
        scratch_shapes=[pltpu.VMEM((tc + 8, LRU_W), f32), pltpu.VMEM((tc, LRU_W), f32),
                        pltpu.VMEM((tc, LRU_W), f32), pltpu.VMEM((8, LRU_W), f32)],
        compiler_params=_params(1),
    )(lxg, lxg, conv_w, conv_b, wab, bab, lam)


def _lru_bwd(dlru, lxg, u, hs, gates, conv_w, wab, lam, *, name, tc=512):
    T = lxg.shape[0]
    tc = min(tc, T)
    nc = T // tc
    bp = tc // 8

    def body(dl_ref, lx_ref, lxp_ref, lg_ref, u_ref, hs_ref, hsp_ref, gates_ref, cw_ref, wab_ref, lam_ref,
             dlxg_ref, dwab_ref, dbab_ref, dcw_ref, dcb_ref, dlam_ref,
             dh_sc, a_sc, ext, du_ext, carry):
        i = pl.program_id(0)
        first_chunk = i == nc - 1

        @pl.when(i == 0)
        def _():
            dwab_ref[...] = jnp.zeros_like(dwab_ref)
            dbab_ref[...] = jnp.zeros_like(dbab_ref)
            dcw_ref[...] = jnp.zeros_like(dcw_ref)
            dcb_ref[...] = jnp.zeros_like(dcb_ref)
            dlam_ref[...] = jnp.zeros_like(dlam_ref)
            carry[...] = jnp.zeros_like(carry)
            du_ext[tc:tc + 8, :] = jnp.zeros((8, LRU_W), f32)

        lg = lg_ref[...]
        gel, th = _gelu(lg)
        dl = dl_ref[...]
        hs = hs_ref[...]
        dlg = dl * hs * _gelu_grad(lg, th)
        u = u_ref[...]
        r, gi, a, s = [gates_ref[:, n * LRU_W:(n + 1) * LRU_W] for n in range(4)]
        sp = _softplus_neg(lam_ref[...])
        a_sc[...] = a
        dh_sc[...] = dl * gel

        def step(k, c):
            t = tc - 1 - k
            dh = dh_sc[pl.ds(t, 1), :] + c
            dh_sc[pl.ds(t, 1), :] = dh
            return a_sc[pl.ds(t, 1), :] * dh

        c = lax.fori_loop(0, tc, step, carry[0:1, :], unroll=8)
        carry[...] = jnp.broadcast_to(c, carry.shape)

        ext[0:8, :] = jnp.where(first_chunk, 0.0, hsp_ref[...])
        ext[8:, :] = hs
        hprev = ext[pl.ds(7, tc), :]
        dh = dh_sc[...]
        da = dh * hprev
        giu = gi * u
        dla = da * a - (dh * giu) * (a * a / s)
        dgi = dh * s * u
        du = dh * s * gi
        dr = dla * (-LRU_C * sp)
        dlam_ref[...] += jnp.sum(dla * (-LRU_C * r), axis=0, keepdims=True) * (-_sigmoid(-lam_ref[...]))
        dpre = jnp.concatenate([dr * r * (1.0 - r), dgi * gi * (1.0 - gi)], axis=1)
        dpre_b = dpre.astype(MXU_DTYPE)
        du = du + _dot_nt(dpre_b, wab_ref[...])
        dwab_ref[...] += _dot_tn(u.astype(MXU_DTYPE), dpre_b)
        dbab_ref[...] += jnp.sum(dpre, axis=0, keepdims=True)
        dcb_ref[...] += jnp.sum(du, axis=0, keepdims=True)

        du_ext[0:tc, :] = du
        dlx = cw_ref[0:1, :] * du_ext[pl.ds(3, tc), :]
        for k in range(1, CONV_K):
            dlx = dlx + cw_ref[k:k + 1, :] * du_ext[pl.ds(3 - k, tc), :]
        du_ext[tc:tc + 8, :] = du_ext[0:8, :]
        ext[0:8, :] = jnp.where(first_chunk, 0.0, lxp_ref[...])
        ext[8:, :] = lx_ref[...]
        for k in range(CONV_K):
            dcw_ref[k:k + 1, :] += jnp.sum(du * ext[pl.ds(5 + k, tc), :], axis=0, keepdims=True)
        dlxg_ref[:, :LRU_W] = dlx.astype(dlxg_ref.dtype)
        dlxg_ref[:, LRU_W:] = dlg.astype(dlxg_ref.dtype)

    rev = lambda i: (nc - 1 - i, 0)
    prev8 = lambda i: (jnp.maximum((nc - 1 - i) * bp - 1, 0), 0)
    const = lambda i: (0, 0)
    return _pcall(
        body, name=name, grid=(nc,),
        in_specs=[
            pl.BlockSpec((tc, LRU_W), rev),
            pl.BlockSpec((tc, LRU_W), rev),
            pl.BlockSpec((8, LRU_W), prev8),
            pl.BlockSpec((tc, LRU_W), lambda i: (nc - 1 - i, 1)),
            pl.BlockSpec((tc, LRU_W), rev),
            pl.BlockSpec((tc, LRU_W), rev),
            pl.BlockSpec((8, LRU_W), prev8),
            pl.BlockSpec((tc, 4 * LRU_W), rev),
            pl.BlockSpec((CONV_K, LRU_W), const),
            pl.BlockSpec((LRU_W, 2 * LRU_W), const),
            pl.BlockSpec((1, LRU_W), const),
        ],
        out_specs=[
            pl.BlockSpec((tc, 2 * LRU_W), rev),
            pl.BlockSpec((LRU_W, 2 * LRU_W), const),
            pl.BlockSpec((1, 2 * LRU_W), const),
            pl.BlockSpec((8, LRU_W), const),
            pl.BlockSpec((1, LRU_W), const),
            pl.BlockSpec((1, LRU_W), const),
        ],
        out_shape=[
            jax.ShapeDtypeStruct((T, 2 * LRU_W), MXU_DTYPE),
            jax.ShapeDtypeStruct((LRU_W, 2 * LRU_W), f32),
            jax.ShapeDtypeStruct((1, 2 * LRU_W), f32),
            jax.ShapeDtypeStruct((8, LRU_W), f32),
            jax.ShapeDtypeStruct((1, LRU_W), f32),
            jax.ShapeDtypeStruct((1, LRU_W), f32),
        ],
        scratch_shapes=[pltpu.VMEM((tc, LRU_W), f32), pltpu.VMEM((tc, LRU_W), f32),
                        pltpu.VMEM((tc + 8, LRU_W), f32), pltpu.VMEM((tc + 8, LRU_W), f32),
                        pltpu.VMEM((8, LRU_W), f32)],
        compiler_params=_params(1),
    )(dlru, lxg, lxg, lxg, u, hs, hs, gates, conv_w, wab, lam)


def _mix_out(fox, lru, wo, xhat1, g1, b1, g2, b2, *, name, tm=512):
    T = fox.shape[0]
    tm = min(tm, T)
    nt = T // tm

    def body(fox_ref, lru_ref, wo_ref, xh_ref, g1_ref, b1_ref, g2_ref, b2_ref, xhat_ref, xn_ref, rstd_ref):
        mix = _dot(fox_ref[...].astype(MXU_DTYPE), wo_ref[:FOX_W, :])
        mix = mix + _dot(lru_ref[...].astype(MXU_DTYPE), wo_ref[FOX_W:, :])
        x1 = xh_ref[...] * g1_ref[...] + b1_ref[...]
        xhat, rstd = _layer_norm_stats(DN_ALPHA * x1 + mix)
        xhat_ref[...] = xhat
        xn_ref[...] = xhat * g2_ref[...] + b2_ref[...]
        rstd_ref[...] = jnp.broadcast_to(rstd, rstd_ref.shape)

    row = lambda i: (i, 0)
    const = lambda i: (0, 0)
    vec = pl.BlockSpec((1, D_MODEL), const)
    return _pcall(
        body, name=name, grid=(nt,),
        in_specs=[pl.BlockSpec((tm, FOX_W), row), pl.BlockSpec((tm, LRU_W), row),
                  pl.BlockSpec((D_MODEL, D_MODEL), const), pl.BlockSpec((tm, D_MODEL), row), vec, vec, vec, vec],
        out_specs=[pl.BlockSpec((tm, D_MODEL), row), pl.BlockSpec((tm, D_MODEL), row),
                   pl.BlockSpec((tm, LANES), row)],
        out_shape=[jax.ShapeDtypeStruct((T, D_MODEL), f32), jax.ShapeDtypeStruct((T, D_MODEL), f32),
                   jax.ShapeDtypeStruct((T, LANES), f32)],
        compiler_params=_params(1),
    )(fox, lru, wo, xhat1, g1, b1, g2, b2)


def _mix_out_bwd(dy, xhat, rstd, ln_g, fox, lru, wo, *, name, tm=512):
    T = fox.shape[0]
    tm = min(tm, T)
    nt = T // tm

    def body(dy_ref, xhat_ref, rstd_ref, g_ref, fox_ref, lru_ref, wo_ref,
             dyp_ref, dgam_ref, dbeta_ref, dlru_ref, dwo_ref, d_ref, doa_ref):
        i = pl.program_id(0)

        @pl.when(i == 0)
        def _():
            dwo_ref[...] = jnp.zeros_like(dwo_ref)
            dgam_ref[...] = jnp.zeros_like(dgam_ref)
            dbeta_ref[...] = jnp.zeros_like(dbeta_ref)

        dyp, dgam, dbeta = _ln_backward(dy_ref[...], xhat_ref[...], rstd_ref[:, 0:1], g_ref[...])
        dyp_ref[...] = dyp
        dgam_ref[...] += dgam
        dbeta_ref[...] += dbeta
        dmix = dyp.astype(MXU_DTYPE)
        dcat = _dot_nt(dmix, wo_ref[...])
        dlru_ref[...] = dcat[:, FOX_W:]
        low = _low_lanes((tm, LANES))
        for j in range(HEADS // 2):
            do2 = dcat[:, j * LANES:(j + 1) * LANES].astype(MXU_DTYPE).astype(f32)
            prod = do2 * fox_ref[:, j * LANES:(j + 1) * LANES]
            for odd in range(2):
                h = 2 * j + odd
                mine = jnp.where(low, _swap_lane_halves(prod) if odd else prod, 0.0)
                d_ref[h] = jnp.broadcast_to(jnp.sum(mine, axis=1, keepdims=True), (tm, LANES))
                doh = jnp.where(low, _swap_lane_halves(do2) if odd else do2, 0.0)
                doa_ref[:, h * LANES:(h + 1) * LANES] = doh.astype(doa_ref.dtype)
        dwo_ref[:FOX_W, :] += _dot_tn(fox_ref[...].astype(MXU_DTYPE), dmix)
        dwo_ref[FOX_W:, :] += _dot_tn(lru_ref[...].astype(MXU_DTYPE), dmix)

    row = lambda i: (i, 0)
    const = lambda i: (0, 0)
    return _pcall(
        body, name=name, grid=(nt,),
        in_specs=[pl.BlockSpec((tm, D_MODEL), row), pl.BlockSpec((tm, D_MODEL), row), pl.BlockSpec((tm, LANES), row),
                  pl.BlockSpec((1, D_MODEL), const),
                  pl.BlockSpec((tm, FOX_W), row), pl.BlockSpec((tm, LRU_W), row),
                  pl.BlockSpec((D_MODEL, D_MODEL), const)],
        out_specs=[pl.BlockSpec((tm, D_MODEL), row), pl.BlockSpec((1, D_MODEL), const), pl.BlockSpec((1, D_MODEL), const),
                   pl.BlockSpec((tm, LRU_W), row), pl.BlockSpec((D_MODEL, D_MODEL), const),
                   pl.BlockSpec((HEADS, tm, LANES), lambda i: (0, i, 0)), pl.BlockSpec((tm, HEADS * LANES), row)],
        out_shape=[jax.ShapeDtypeStruct((T, D_MODEL), f32), jax.ShapeDtypeStruct((1, D_MODEL), f32),
                   jax.ShapeDtypeStruct((1, D_MODEL), f32),
                   jax.ShapeDtypeStruct((T, LRU_W), f32), jax.ShapeDtypeStruct((D_MODEL, D_MODEL), f32),
                   jax.ShapeDtypeStruct((HEADS, T, LANES), f32), jax.ShapeDtypeStruct((T, HEADS * LANES), MXU_DTYPE)],
        compiler_params=_params(1),
    )(dy, xhat, rstd, ln_g, fox, lru, wo)


def make_wp(w_in):
    scale = jnp.concatenate([jnp.full((FOX_W,), 1.0 / math.sqrt(HEAD_DIM), w_in.dtype),
                             jnp.ones((IN_COLS - FOX_W,), w_in.dtype)])
    return jnp.pad(w_in * scale[None, :], ((0, 0), (0, Z_PAD - IN_COLS)))


def _block_diag(w):
    eye = jnp.eye(HEADS, dtype=w.dtype)
    return jnp.einsum("hij,hg->higj", w, eye).reshape(LRU_W, LRU_W)


def _block_diag_extract(m):
    m4 = m.reshape(HEADS, HEAD_DIM, HEADS, HEAD_DIM)
    return jnp.stack([m4[h, :, h, :] for h in range(HEADS)])


class _NoOverlap:
    def start_token(self):
        return None

    def late_weights(self, w, after):
        return dict(f1d=w["f1d"], wp=w["wp"], wo=w["wo"])

    def after_attention(self, after):
        return None

    def ffn2_weights(self, w, after):
        return w["f2g"], w["f2u"], w["f2d"]

    def ffn2_grads(self, grads):
        return None

    def ffn1_grads(self, grads):
        return None

    def mixer_grads(self, dwp, dwo, small, loss):
        return None

    def before_ffn1_bwd(self, after):
        return None


def _tied(a, token):
    return a if token is None else a + token[0, 0]


def _local_step(x, target, w, hooks=None):
    hooks = hooks or _NoOverlap()
    bfp = w["bfp"]
    wab = jnp.concatenate([_block_diag(w["rg_wa"]), _block_diag(w["rg_wx"])], axis=1).astype(MXU_DTYPE)
    bab = jnp.concatenate([w["rg_ba"].reshape(1, LRU_W), w["rg_bx"].reshape(1, LRU_W)], axis=1)

    xb0, g1a, u1a, h1a = _ffn_up(x, w["f1g"], w["f1u"], hooks.start_token(), name="ffn1_up")
    late = hooks.late_weights(w, [h1a])
    f1d, wp, wo = late["f1d"], late["wp"], late["wo"]
    xhat1, xn1, rstd1 = _ffn_down_ln(x, h1a, f1d, w["ln1_g"], w["ln1_b"], name="ffn1_down")
    lxg, fgb, qa, ka, va = _proj_in(xn1, wp, bfp, name="proj_in")
    fox, lse = _fox_fwd(qa, ka, va, name="fox_fwd")
    token = hooks.after_attention([lse])
    lru, uconv, hs, gates = _lru_fwd(lxg, w["conv_w"], _tied(w["conv_b"], token), wab, bab, w["lam"], name="lru_fwd")
    xhat2, x2, rstd2 = _mix_out(fox, lru, wo, xhat1, w["ln1_g"], w["ln1_b"], w["ln2_g"], w["ln2_b"], name="mix_out")
    f2g, f2u, f2d = hooks.ffn2_weights(w, [rstd2])
    xb2, g2a, u2a, dy3p, dln3g, dln3b, loss = _ffn_fwd_loss(x2, f2g, f2u, f2d, w["ln3_g"], w["ln3_b"], target,
                                                            name="ffn2_fwd_loss")

    dx2, df2g, df2u, df2d = _ffn_bwd(dy3p, xb2, g2a, u2a, f2g, f2u, f2d, name="ffn2_bwd")
    token = hooks.ffn2_grads([df2g, df2u, df2d])
    dy2p, dln2g, dln2b, dlru, dwo, drep, doa = _mix_out_bwd(dx2, xhat2, rstd2, _tied(w["ln2_g"], token), fox, lru, wo,
                                                            name="mix_out_bwd")
    dlxg, dwab, dbab, dcw, dcb, dlam = _lru_bwd(dlru, lxg, uconv, hs, gates, w["conv_w"], wab, w["lam"], name="lru_bwd")
    dqa, dka, dva = _fox_bwd(qa, ka, va, doa, lse, drep, name="fox_bwd")
    dy1p, dwp, dln1g, dln1b, dbf = _proj_in_bwd(dqa, dka, dva, dlxg, fgb, xn1, dy2p, wp, xhat1, rstd1, w["ln1_g"],
                                                name="proj_in_bwd")
    small = dict(
        ln1_g=dln1g, ln1_b=dln1b, ln2_g=dln2g, ln2_b=dln2b, ln3_g=dln3g, ln3_b=dln3b,
        b_forget=dbf[:, :HEADS], conv_w=dcw[:CONV_K], conv_b=dcb,
        rg_wa=_block_diag_extract(dwab[:, :LRU_W]), rg_wx=_block_diag_extract(dwab[:, LRU_W:]),
        rg_ba=dbab[:, :LRU_W].reshape(HEADS, HEAD_DIM), rg_bx=dbab[:, LRU_W:].reshape(HEADS, HEAD_DIM),
        lru_lambda=dlam,
    )
    hooks.before_ffn1_bwd([dln1b])
    token = hooks.mixer_grads(dwp, dwo, small, loss)
    dx_a, *grads_a = _ffn_bwd(dy1p, xb0, g1a, u1a, w["f1g"], w["f1u"], f1d, token, name="ffn1_bwd_a", part=0)
    token = hooks.ffn1_grads(grads_a)
    dx, *grads_b = _ffn_bwd(dy1p, xb0, g1a, u1a, w["f1g"], w["f1u"], f1d, token, name="ffn1_bwd_b", part=1,
                            dx_init=dx_a)

    grads = dict(f1=(grads_a, grads_b), f2g=df2g, f2u=df2u, f2d=df2d, wp=dwp, wo=dwo, **small)
    return loss, dx, grads


MESH = pl.DeviceIdType.MESH
HBM_SPEC = pl.BlockSpec(memory_space=pl.ANY)
VMEM_SPEC = pl.BlockSpec(memory_space=pltpu.VMEM)


def _position():
    return lax.axis_index("x"), lax.axis_index("y"), lax.axis_index("c")


def _other_chips(x, y):
    return [(1 - x, y), (x, 1 - y), (1 - x, 1 - y)]


def _all_gather_bf16(shards, *, name):
    n = len(shards)

    def body(*refs):
        ins, outs, stages = refs[:n], refs[n:2 * n], refs[2 * n:3 * n]
        send_sems, recv_sems, local_sems = refs[3 * n:]
        x, y, c = _position()
        me, sibling = (x, y, c), (x, y, 1 - c)
        chips = _other_chips(x, y)

        def rows(k, px, py, pc):
            r = shards[k].shape[0]
            m = r // 2
            return outs[k].at[pl.ds(pl.multiple_of((2 * px + py) * r + pc * m, 16), m), :]

        def copy(k, idx, block, to, src=None):
            return pltpu.make_async_remote_copy(
                src_ref=rows(k, *block) if src is None else src, dst_ref=rows(k, *block),
                send_sem=send_sems.at[7 * k + idx], recv_sem=recv_sems.at[7 * k + idx],
                device_id=to, device_id_type=MESH)

        started = []
        mine = []
        for k in range(n):
            m = shards[k].shape[0] // 2
            stages[k][...] = ins[k][pl.ds(pl.multiple_of(c * m, 16), m), :].astype(stages[k].dtype)
            cp = pltpu.make_async_copy(stages[k], rows(k, *me), local_sems.at[k])
            cp.start()
            mine.append(cp)
            first = [copy(k, 0, me, sibling, src=stages[k])]
            first += [copy(k, 1 + j, me, (*chip, c), src=stages[k]) for j, chip in enumerate(chips)]
            for cp in first:
                cp.start()
            started += first
        for k in range(n):
            for j, chip in enumerate(chips):
                copy(k, 1 + j, (*chip, c), me).wait_recv()
                fwd = copy(k, 4 + j, (*chip, c), sibling)
                fwd.start()
                started.append(fwd)
        for k in range(n):
            copy(k, 0, sibling, me).wait_recv()
            for j, chip in enumerate(chips):
                copy(k, 4 + j, (*chip, 1 - c), me).wait_recv()
        for cp in started:
            cp.wait_send()
        for cp in mine:
            cp.wait()

    return _pcall(
        body, name=name,
        in_specs=[VMEM_SPEC] * n, out_specs=[HBM_SPEC] * n,
        out_shape=[jax.ShapeDtypeStruct((N_SHARD * s.shape[0], s.shape[1]), MXU_DTYPE) for s in shards],
        scratch_shapes=[pltpu.VMEM((s.shape[0] // 2, s.shape[1]), MXU_DTYPE) for s in shards]
        + [pltpu.SemaphoreType.DMA((7 * n,)), pltpu.SemaphoreType.DMA((7 * n,)), pltpu.SemaphoreType.DMA((n,))],
        compiler_params=pltpu.CompilerParams(vmem_limit_bytes=VMEM_LIMIT),
    )(*shards)


def _swap_halves(gs, *, name):
    n = len(gs)

    def body(*refs):
        ins, outs = refs[:n], refs[n:2 * n]
        send_sems, recv_sems = refs[2 * n:]
        x, y, c = _position()
        cps = []
        for k in range(n):
            m = gs[k].shape[1] // 2
            src = ins[k].at[:, pl.ds(pl.multiple_of((1 - c) * m, 16), m), :]
            cp = pltpu.make_async_remote_copy(src_ref=src, dst_ref=outs[k], send_sem=send_sems.at[k],
                                              recv_sem=recv_sems.at[k], device_id=(x, y, 1 - c), device_id_type=MESH)
            cp.start()
            cps.append(cp)
        for cp in cps:
            cp.wait()

    return _pcall(
        body, name=name, in_specs=[HBM_SPEC] * n, out_specs=[HBM_SPEC] * n,
        out_shape=[jax.ShapeDtypeStruct((g.shape[0], g.shape[1] // 2, g.shape[2]), g.dtype) for g in gs],
        scratch_shapes=[pltpu.SemaphoreType.DMA((n,)), pltpu.SemaphoreType.DMA((n,))],
    )(*gs)


def _add_halves(gs, recvs, *, name, tm=256):
    n = len(gs)
    _, r, cdim = gs[0].shape
    m = r // 2
    tm = min(tm, m)
    nb = m // tm
    c_idx = lax.axis_index("c").astype(jnp.int32).reshape(1)

    def body(c_ref, *refs):
        for k in range(n):
            refs[2 * n + k][...] = (refs[k][...].astype(f32) + refs[n + k][...].astype(f32)).astype(refs[2 * n + k].dtype)

    mine = pl.BlockSpec((None, tm, cdim), lambda j, i, c_ref: (j, c_ref[0] * nb + i, 0))
    half = pl.BlockSpec((None, tm, cdim), lambda j, i, c_ref: (j, i, 0))
    return _pcall(
        body, name=name,
        grid_spec=pltpu.PrefetchScalarGridSpec(
            num_scalar_prefetch=1, grid=(N_SHARD, nb),
            in_specs=[mine] * n + [half] * n, out_specs=[half] * n),
        out_shape=[jax.ShapeDtypeStruct((N_SHARD, m, cdim), g.dtype) for g in gs],
        compiler_params=_params(2),
    )(c_idx, *gs, *recvs)


def _scatter_partials(ps, *, name):
    n = len(ps)

    def body(*refs):
        ins, outs = refs[:n], refs[n:2 * n]
        send_sems, recv_sems = refs[2 * n:]
        x, y, c = _position()
        me_chip = 2 * x + y
        cps = []
        for k in range(n):
            for j, (px, py) in enumerate(_other_chips(x, y)):
                cp = pltpu.make_async_remote_copy(
                    src_ref=ins[k].at[2 * px + py], dst_ref=outs[k].at[me_chip],
                    send_sem=send_sems.at[3 * k + j], recv_sem=recv_sems.at[3 * k + j],
                    device_id=(px, py, c), device_id_type=MESH)
                cp.start()
                cps.append(cp)
        for cp in cps:
            cp.wait()

    return _pcall(
        body, name=name, in_specs=[HBM_SPEC] * n, out_specs=[HBM_SPEC] * n,
        out_shape=[jax.ShapeDtypeStruct(p.shape, p.dtype) for p in ps],
        scratch_shapes=[pltpu.SemaphoreType.DMA((3 * n,)), pltpu.SemaphoreType.DMA((3 * n,))],
    )(*ps)


def _sum_slabs(ps, qs, *, name, tm=128):
    n = len(qs)
    _, m, cdim = qs[0].shape
    tm = min(tm, m)
    nb = m // tm
    assert m % tm == 0, (m, tm)
    where = jnp.stack([2 * lax.axis_index("x") + lax.axis_index("y"), lax.axis_index("c")]).astype(jnp.int32)

    def body(w_ref, *refs):
        for k in range(n):
            own, q1, q2, q3 = (refs[4 * k + t][...].astype(f32) for t in range(4))
            refs[4 * n + k][...] = ((own + q1) + q2) + q3

    def slab(flip):
        return pl.BlockSpec((None, tm, cdim), lambda i, w_ref: (jnp.bitwise_xor(w_ref[0], flip), i, 0))

    operands = []
    for p, q in zip(ps, qs):
        operands += [p, q, q, q]
    return _pcall(
        body, name=name,
        grid_spec=pltpu.PrefetchScalarGridSpec(
            num_scalar_prefetch=1, grid=(nb,),
            in_specs=[slab(0), slab(2), slab(1), slab(3)] * n,
            out_specs=[pl.BlockSpec((tm, cdim), lambda i, w_ref: (w_ref[1] * nb + i, 0))] * n),
        out_shape=[jax.ShapeDtypeStruct((2 * m, cdim), f32) for _ in qs],
        compiler_params=_params(1),
    )(where, *operands)


def _join_halves(fs, *, name):
    n = len(fs)

    def body(*refs):
        outs = refs[n:2 * n]
        send_sems, recv_sems = refs[2 * n:]
        x, y, c = _position()
        cps = []
        for k in range(n):
            m = fs[k].shape[0] // 2
            half = outs[k].at[pl.ds(pl.multiple_of(c * m, 8), m), :]
            cp = pltpu.make_async_remote_copy(src_ref=half, dst_ref=half, send_sem=send_sems.at[k],
                                              recv_sem=recv_sems.at[k], device_id=(x, y, 1 - c), device_id_type=MESH)
            cp.start()
            cps.append(cp)
        for cp in cps:
            cp.wait()

    return _pcall(
        body, name=name, in_specs=[HBM_SPEC] * n, out_specs=[HBM_SPEC] * n,
        out_shape=[jax.ShapeDtypeStruct(f.shape, f.dtype) for f in fs],
        input_output_aliases={k: k for k in range(n)},
        scratch_shapes=[pltpu.SemaphoreType.DMA((n,)), pltpu.SemaphoreType.DMA((n,))],
    )(*fs)


def _all_reduce_small(v, after=None, *, name):
    r = v.shape[0]
    extra = [] if after is None else [after]

    def body(v_ref, *refs):
        out_ref, buf, send_sems, recv_sems, local_sem = refs[len(extra):]
        x, y, c = _position()
        me, sibling = (x, y, c), (x, y, 1 - c)
        chips = _other_chips(x, y)

        def rows(px, py, pc):
            return buf.at[pl.ds(pl.multiple_of((4 * px + 2 * py + pc) * r, 8), r), :]

        def copy(k, block, to, src=None):
            return pltpu.make_async_remote_copy(
                src_ref=rows(*block) if src is None else src, dst_ref=rows(*block),
                send_sem=send_sems.at[k], recv_sem=recv_sems.at[k], device_id=to, device_id_type=MESH)

        mine = pltpu.make_async_copy(v_ref, rows(*me), local_sem)
        mine.start()
        first = [copy(0, me, sibling, src=v_ref)]
        first += [copy(1 + j, me, (*chip, c), src=v_ref) for j, chip in enumerate(chips)]
        for cp in first:
            cp.start()
        passed = [copy(4 + j, (*chip, c), sibling) for j, chip in enumerate(chips)]
        for j, chip in enumerate(chips):
            copy(1 + j, (*chip, c), me).wait_recv()
            passed[j].start()
        copy(0, sibling, me).wait_recv()
        for j, chip in enumerate(chips):
            copy(4 + j, (*chip, 1 - c), me).wait_recv()
        for cp in first + passed:
            cp.wait_send()
        mine.wait()
        acc = buf[0:r, :]
        for d in range(1, N_DEV):
            acc = acc + buf[d * r:(d + 1) * r, :]
        out_ref[...] = acc

    return _pcall(
        body, name=name, in_specs=[VMEM_SPEC] + [HBM_SPEC] * len(extra), out_specs=VMEM_SPEC,
        out_shape=jax.ShapeDtypeStruct((r, LANES), f32),
        scratch_shapes=[pltpu.VMEM((N_DEV * r, LANES), f32), pltpu.SemaphoreType.DMA((7,)),
                        pltpu.SemaphoreType.DMA((7,)), pltpu.SemaphoreType.DMA],
    )(v, *extra)


SEM_SPEC = pl.BlockSpec(memory_space=pltpu.SEMAPHORE)
HBM_ONLY = pl.BlockSpec(memory_space=pltpu.HBM)
EFFECT = pltpu.SideEffectType.DATAFLOW_SIDE_EFFECTING


def _sends(copies):
    return copies[0] if isinstance(copies, tuple) else copies


def _arrivals(copies):
    return copies[1] if isinstance(copies, tuple) else copies


def _split_start(bufs, copies_fn, n_sems, *, name):
    n = len(bufs)

    def body(*refs):
        send_sems, recv_sems = refs[n], refs[n + 1]
        thru = refs[n + 2:2 * n + 2]
        token = refs[2 * n + 2]
        for cp in _sends(copies_fn(thru, send_sems, recv_sems)):
            cp.start()
        token[...] = jnp.zeros_like(token)

    outs = _pcall(
        body, name=name,
        out_shape=(pltpu.SemaphoreType.DMA((n_sems,)), pltpu.SemaphoreType.DMA((n_sems,)),
                   *[pltpu.HBM(b.shape, b.dtype) for b in bufs], jax.ShapeDtypeStruct((8, LANES), f32)),
        in_specs=[HBM_ONLY] * n,
        out_specs=(SEM_SPEC, SEM_SPEC, *[HBM_ONLY] * n, VMEM_SPEC),
        input_output_aliases={k: 2 + k for k in range(n)},
        compiler_params=pltpu.CompilerParams(has_side_effects=EFFECT),
    )(*[pltpu.with_memory_space_constraint(b, pltpu.HBM) for b in bufs])
    return outs[0], outs[1], list(outs[2:2 + n]), outs[2 + n]


def _split_wait(thru, send_sems, recv_sems, after, copies_fn, *, name):
    n = len(thru)

    def body(*refs):
        copies = copies_fn(refs[:n], refs[n], refs[n + 1])
        for cp in _sends(copies):
            cp.wait_send()
        for cp in _arrivals(copies):
            cp.wait_recv()

    return list(_pcall(
        body, name=name,
        out_shape=tuple(pltpu.HBM(b.shape, b.dtype) for b in thru),
        in_specs=[HBM_ONLY] * n + [SEM_SPEC, SEM_SPEC] + [HBM_SPEC] * len(after),
        out_specs=tuple([HBM_ONLY] * n),
        input_output_aliases={k: k for k in range(n)},
        compiler_params=pltpu.CompilerParams(has_side_effects=EFFECT),
    )(*thru, send_sems, recv_sems, *after))


def _scatter_copies(n):
    def copies(bufs, send_sems, recv_sems):
        x, y, c = _position()
        me_chip = 2 * x + y
        cps = []
        for k in range(n):
            for j, (px, py) in enumerate(_other_chips(x, y)):
                cps.append(pltpu.make_async_remote_copy(
                    src_ref=bufs[k].at[2 * px + py], dst_ref=bufs[n + k].at[me_chip],
                    send_sem=send_sems.at[3 * k + j], recv_sem=recv_sems.at[3 * k + j],
                    device_id=(px, py, c), device_id_type=MESH))
        return cps
    return copies


N_PEERS = N_DEV - 1


def _direct_copies(n):
    def copies(bufs, send_sems, recv_sems):
        x, y, c = _position()
        me_chip = 2 * x + y
        sends, arrivals = [], []
        for k in range(n):
            m = bufs[k].shape[1] // 2
            land = bufs[n + k]

            def rows(slab, half, k=k, m=m):
                start = half * m if isinstance(half, int) else pl.multiple_of(half * m, 16)
                return bufs[k].at[slab, pl.ds(start, m), :]

            def copy(src, slot, send_idx, recv_idx, to, k=k, land=land):
                return pltpu.make_async_remote_copy(
                    src_ref=src, dst_ref=land.at[slot], send_sem=send_sems.at[N_PEERS * k + send_idx],
                    recv_sem=recv_sems.at[N_PEERS * k + recv_idx], device_id=to, device_id_type=MESH)

            sends.append(copy(rows(me_chip, 1 - c), 0, 0, 0, (x, y, 1 - c)))
            arrivals.append(copy(rows(me_chip, c), 0, 0, 0, (x, y, 1 - c)))
            for t, (px, py) in enumerate(_other_chips(x, y)):
                for core in range(2):
                    sends.append(copy(rows(2 * px + py, core), 1 + 2 * t + c, 1 + 2 * t + core, 1 + 2 * t + c,
                                      (px, py, core)))
                    arrivals.append(copy(rows(me_chip, c), 1 + 2 * t + core, 1 + 2 * t + core, 1 + 2 * t + core,
                                         (px, py, core)))
        return sends, arrivals
    return copies


def _sum_direct(gs, lands, *, name, tm=128):
    n = len(gs)
    _, m, cdim = lands[0].shape
    tm = min(tm, m)
    nb = m // tm
    assert m % tm == 0, (m, tm)
    where = jnp.stack([2 * lax.axis_index("x") + lax.axis_index("y"), lax.axis_index("c")]).astype(jnp.int32)

    def body(w_ref, *refs):
        for k in range(n):
            acc = refs[2 * k][...].astype(f32)
            for slot in range(N_PEERS):
                acc = acc + refs[2 * k + 1][slot].astype(f32)
            refs[2 * n + k][...] = acc

    own = pl.BlockSpec((None, tm, cdim), lambda i, w_ref: (w_ref[0], w_ref[1] * nb + i, 0))
    landed = pl.BlockSpec((N_PEERS, tm, cdim), lambda i, w_ref: (0, i, 0))
    operands = []
    for g, land in zip(gs, lands):
        operands += [g, land]
    return _pcall(
        body, name=name,
        grid_spec=pltpu.PrefetchScalarGridSpec(
            num_scalar_prefetch=1, grid=(nb,), in_specs=[own, landed] * n,
            out_specs=[pl.BlockSpec((tm, cdim), lambda i, w_ref: (w_ref[1] * nb + i, 0))] * n),
        out_shape=[jax.ShapeDtypeStruct((2 * m, cdim), f32) for _ in gs],
        compiler_params=_params(1),
    )(where, *operands)


def _broadcast_copies(bufs, send_sems, recv_sems):
    v, land = bufs
    x, y, c = _position()

    def copy(slot, send_idx, recv_idx, to):
        return pltpu.make_async_remote_copy(src_ref=v, dst_ref=land.at[slot], send_sem=send_sems.at[send_idx],
                                            recv_sem=recv_sems.at[recv_idx], device_id=to, device_id_type=MESH)

    sends = [copy(0, 0, 0, (x, y, 1 - c))]
    arrivals = [copy(0, 0, 0, (x, y, 1 - c))]
    for t, (px, py) in enumerate(_other_chips(x, y)):
        for core in range(2):
            sends.append(copy(1 + 2 * t + c, 1 + 2 * t + core, 1 + 2 * t + c, (px, py, core)))
            arrivals.append(copy(1 + 2 * t + core, 1 + 2 * t + core, 1 + 2 * t + core, (px, py, core)))
    return sends, arrivals


def _sum_in_device_order(v, land, *, name):
    r, cdim = v.shape
    x, y, c = _position()
    slots, mine = [], []
    for d in range(N_DEV):
        dx, dy, dc = d // 4, (d // 2) % 2, d % 2
        fx, fy = jnp.bitwise_xor(dx, x), jnp.bitwise_xor(dy, y)
        t = jnp.where(fx == 1, jnp.where(fy == 1, 2, 0), 1)
        slots.append(jnp.where(jnp.logical_and(fx == 0, fy == 0), 0, 1 + 2 * t + dc))
        mine.append(jnp.logical_and(jnp.logical_and(fx == 0, fy == 0), dc == c))
    table = jnp.stack(slots + mine).astype(jnp.int32)

    def body(tab_ref, v_ref, *refs):
        out_ref = refs[N_DEV]
        acc = None
        for d in range(N_DEV):
            term = jnp.where(tab_ref[N_DEV + d] == 1, v_ref[...], refs[d][...])
            acc = term if acc is None else acc + term
        out_ref[...] = acc

    whole = pl.BlockSpec((r, cdim), lambda i, tab_ref: (0, 0))
    landed = [pl.BlockSpec((None, r, cdim), functools.partial(lambda i, tab_ref, d: (tab_ref[d], 0, 0), d=d))
              for d in range(N_DEV)]
    return _pcall(
        body, name=name,
        grid_spec=pltpu.PrefetchScalarGridSpec(num_scalar_prefetch=1, grid=(1,), in_specs=[whole] + landed,
                                               out_specs=whole),
        out_shape=jax.ShapeDtypeStruct((r, cdim), f32),
        compiler_params=_params(1),
    )(table, v, *[land] * N_DEV)


def _block_rows(buf, px, py, pc):
    m = buf.shape[0] // N_DEV
    return buf.at[pl.ds(pl.multiple_of((4 * px + 2 * py + pc) * m, 16), m), :]


def _gather_ici_copies(n):
    def copies(bufs, send_sems, recv_sems):
        x, y, c = _position()
        cps = []
        for k in range(n):
            rows = _block_rows(bufs[k], x, y, c)
            targets = [(x, y, 1 - c)] + [(px, py, c) for px, py in _other_chips(x, y)]
            for j, to in enumerate(targets):
                cps.append(pltpu.make_async_remote_copy(
                    src_ref=rows, dst_ref=rows, send_sem=send_sems.at[4 * k + j], recv_sem=recv_sems.at[4 * k + j],
                    device_id=to, device_id_type=MESH))
        return cps
    return copies


def _gather_d2d_copies(n):
    def copies(bufs, send_sems, recv_sems):
        x, y, c = _position()
        cps = []
        for k in range(n):
            for j, (px, py) in enumerate(_other_chips(x, y)):
                rows = _block_rows(bufs[k], px, py, c)
                cps.append(pltpu.make_async_remote_copy(
                    src_ref=rows, dst_ref=rows, send_sem=send_sems.at[3 * k + j], recv_sem=recv_sems.at[3 * k + j],
                    device_id=(x, y, 1 - c), device_id_type=MESH))
        return cps
    return copies


def _cast_halves(shards, after, *, name):
    n = len(shards)
    where = jnp.stack([2 * lax.axis_index("x") + lax.axis_index("y"), lax.axis_index("c")]).astype(jnp.int32)

    def body(w_ref, *refs):
        for k in range(n):
            refs[n + 1 + k][...] = refs[k][...].astype(refs[n + 1 + k].dtype)

    def half(s):
        return (s.shape[0] // 2, s.shape[1])

    return _pcall(
        body, name=name,
        grid_spec=pltpu.PrefetchScalarGridSpec(
            num_scalar_prefetch=1, grid=(1,),
            in_specs=[pl.BlockSpec(half(s), lambda i, w_ref: (w_ref[1], 0)) for s in shards] + [HBM_SPEC],
            out_specs=[pl.BlockSpec(half(s), lambda i, w_ref: (2 * w_ref[0] + w_ref[1], 0)) for s in shards]),
        out_shape=[jax.ShapeDtypeStruct((N_SHARD * s.shape[0], s.shape[1]), MXU_DTYPE) for s in shards],
        compiler_params=_params(1),
    )(where, *shards, after)


class _SplitGather:
    def __init__(self, shards, after, tag):
        self.tag = tag
        self.n = len(shards)
        halves = _cast_halves(shards, after, name=f"{tag}_cast")
        self.ici = _split_start(halves, _gather_ici_copies(self.n), 4 * self.n, name=f"{tag}_ici_start")
        self.token = self.ici[3]

    def forward(self, after):
        send_sems, recv_sems, thru, _ = self.ici
        landed = _split_wait(thru, send_sems, recv_sems, after, _gather_ici_copies(self.n), name=f"{self.tag}_ici_wait")
        self.d2d = _split_start(landed, _gather_d2d_copies(self.n), 3 * self.n, name=f"{self.tag}_d2d_start")
        return self.d2d[3]

    def finish(self, after):
        send_sems, recv_sems, thru, _ = self.d2d
        return _split_wait(thru, send_sems, recv_sems, after, _gather_d2d_copies(self.n), name=f"{self.tag}_d2d_wait")


class _Overlap(_NoOverlap):
    def __init__(self, late_shards, ffn2_shards, after):
        self.late = _SplitGather(late_shards, after, "ag1")
        self.ffn2 = _SplitGather(ffn2_shards, self.late.token, "ag2")
        self.reduced = None
        self.ffn1_parts = []

    def start_token(self):
        return self.ffn2.token

    def late_weights(self, w, after):
        token = self.late.forward(after)
        f1d, w_in, wo = self.late.finish([token])
        w_in = w_in.reshape(N_SHARD, D_MODEL, IN_SHARD).transpose(1, 0, 2).reshape(D_MODEL, IN_COLS)
        return dict(f1d=f1d.reshape(N_SHARD, D_FF // N_SHARD, D_MODEL), wp=make_wp(w_in), wo=wo)

    def after_attention(self, after):
        return self.ffn2.forward(after)

    def ffn2_weights(self, w, after):
        full = self.ffn2.finish(after)
        fs = D_FF // N_SHARD
        return (full[0].reshape(N_SHARD, D_MODEL, fs), full[1].reshape(N_SHARD, D_MODEL, fs),
                full[2].reshape(N_SHARD, fs, D_MODEL))

    @staticmethod
    def _send_direct(grads, tag):
        lands = [lax.empty((N_PEERS, g.shape[1] // 2, g.shape[2]), g.dtype) for g in grads]
        return _split_start(list(grads) + lands, _direct_copies(len(grads)), N_PEERS * len(grads),
                            name=f"rs_direct_{tag}_start")

    def ffn2_grads(self, grads):
        self.scatter = self._send_direct(grads, "ffn2")
        return self.scatter[3]

    def ffn1_grads(self, grads):
        tag = "ffn1" + "ab"[len(self.ffn1_parts)]
        if not self.ffn1_parts:
            started = self._send_direct(grads, tag)
        else:
            recvs = _swap_halves(grads, name=f"rs_swap_{tag}")
            ps = list(_add_halves(grads[:2], recvs[:2], name=f"rs_add_{tag}_gu"))
            ps += list(_add_halves(grads[2:], recvs[2:], name=f"rs_add_{tag}_d"))
            lands = [lax.empty(p.shape, p.dtype) for p in ps]
            started = _split_start(ps + lands, _scatter_copies(3), 9, name=f"rs_scatter_{tag}_start")
        self.ffn1_parts.append((tag, started))
        return started[3]

    def ffn1_reduced(self, after):
        sums = []
        for direct, (tag, (send_sems, recv_sems, thru, _)) in zip((True, False), self.ffn1_parts):
            plan, add = (_direct_copies, _sum_direct) if direct else (_scatter_copies, _sum_slabs)
            done = _split_wait(thru, send_sems, recv_sems, after, plan(3), name=f"rs_{tag}_wait")
            sums += list(add(done[:2], done[3:5], name=f"rs_sum_{tag}_gu"))
            sums += list(add(done[2:3], done[5:], name=f"rs_sum_{tag}_d"))
        return sums

    def mixer_grads(self, dwp, dwo, small, loss):
        packed = jnp.concatenate([_pack_small(small), jnp.broadcast_to(loss, (8, LANES))], axis=0)
        land = lax.empty((N_PEERS,) + packed.shape, packed.dtype)
        self.small = _split_start([packed, land], _broadcast_copies, N_PEERS, name="ar_small_start")
        gwin = jnp.stack([dwp[:, j * IN_SHARD:(j + 1) * IN_SHARD] for j in range(N_SHARD)]).astype(GRAD_DTYPE)
        gwo = dwo.reshape(N_SHARD, D_MODEL // N_SHARD, D_MODEL).astype(GRAD_DTYPE)
        self.scatter_mix = self._send_direct([gwin, gwo], "mix")
        return self.small[3] + self.scatter_mix[3]

    def small_summed(self, after):
        send_sems, recv_sems, thru, _ = self.small
        packed, land = _split_wait(thru, send_sems, recv_sems, after, _broadcast_copies, name="ar_small_wait")
        summed = _sum_in_device_order(packed, land, name="ar_small_sum")
        return summed[:-8], summed[-8, 0]

    def mixer_reduced(self, after):
        send_sems, recv_sems, thru, _ = self.scatter_mix
        done = _split_wait(thru, send_sems, recv_sems, after, _direct_copies(2), name="rs_direct_mix_wait")
        return [_sum_direct([done[k]], [done[2 + k]], name=f"rs_sum_{tag}")[0] for k, tag in enumerate(["w_in", "w_out"])]

    def before_ffn1_bwd(self, after):
        send_sems, recv_sems, thru, _ = self.scatter
        n = len(thru) // 2
        done = _split_wait(thru, send_sems, recv_sems, after, _direct_copies(n), name="rs_direct_ffn2_wait")
        self.reduced = list(_sum_direct(done[:n], done[n:], name="rs_sum_ffn2"))


def _adamw(gs, ws, ms, vs, *, name, tm=256):
    n = len(gs)
    r, cdim = ws[0].shape[-2:]
    tm = r if tm is None else min(tm, r)
    assert r % tm == 0, (r, tm)
    nb = r // tm
    c1 = 1.0 / (1.0 - ADAM_B1 ** ADAM_STEP)
    c2 = 1.0 / (1.0 - ADAM_B2 ** ADAM_STEP)
    flat = pl.BlockSpec((tm, cdim), lambda i: (i, 0))

    g_ops, g_specs, g_where = [], [], []
    for g in gs:
        g_where.append(len(g_ops))
        if not isinstance(g, tuple):
            g_ops.append(g)
            g_specs.append(flat)
        elif g[2] == 1:
            g_ops += [g[0], g[1]]
            g_specs += [pl.BlockSpec((tm, cdim // 2), lambda i: (i, 0))] * 2
        else:
            g_ops += [g[0], g[1]]
            g_specs += [pl.BlockSpec((tm, cdim), lambda i: (jnp.minimum(i, nb // 2 - 1), 0)),
                        pl.BlockSpec((tm, cdim), lambda i: (jnp.maximum(i - nb // 2, 0), 0))]
    ng = len(g_ops)

    def gradient(refs, k):
        g, at = gs[k], g_where[k]
        if not isinstance(g, tuple):
            return refs[at][...]
        if g[2] == 1:
            return jnp.concatenate([refs[at][...], refs[at + 1][...]], axis=1)
        return jnp.where(pl.program_id(0) < nb // 2, refs[at][...], refs[at + 1][...])

    def body(*refs):
        rest = refs[ng:]
        for k in range(n):
            g = gradient(refs, k)
            w = rest[k][...]
            m = ADAM_B1 * rest[n + k][...] + (1.0 - ADAM_B1) * g
            v = ADAM_B2 * rest[2 * n + k][...] + (1.0 - ADAM_B2) * (g * g)
            rest[3 * n + k][...] = g
            rest[4 * n + k][...] = -ADAM_LR * ((m * c1) / (jnp.sqrt(v * c2) + ADAM_EPS) + ADAM_WD * w)
            rest[5 * n + k][...] = m
            rest[6 * n + k][...] = v

    like_w = flat if ws[0].ndim == 2 else pl.BlockSpec((None, tm, cdim), lambda i: (0, i, 0))
    outs = _pcall(
        body, name=name, grid=(nb,), in_specs=g_specs + [like_w] * (3 * n), out_specs=[like_w] * (4 * n),
        out_shape=[jax.ShapeDtypeStruct(ws[0].shape, f32)] * (4 * n),
        compiler_params=_params(1),
    )(*g_ops, *ws, *ms, *vs)
    return outs[:n], outs[n:2 * n], outs[2 * n:3 * n], outs[3 * n:]


BIG = ["ffn1_w_gate", "ffn1_w_up", "ffn1_w_down", "ffn2_w_gate", "ffn2_w_up", "ffn2_w_down"]
SMALL = ["ln1_g", "ln1_b", "b_forget", "conv_w", "conv_b", "rg_wa", "rg_ba", "rg_wx", "rg_bx", "lru_lambda",
         "ln2_g", "ln2_b", "ln3_g", "ln3_b"]
WEIGHTS = ["ffn1_w_gate", "ffn1_w_up", "ffn1_w_down", "ln1_g", "ln1_b", "w_in", "b_forget", "conv_w", "conv_b",
           "rg_wa", "rg_ba", "rg_wx", "rg_bx", "lru_lambda", "w_out", "ln2_g", "ln2_b",
           "ffn2_w_gate", "ffn2_w_up", "ffn2_w_down", "ln3_g", "ln3_b"]


def _pack_small(parts):
    rows = []
    for n in SMALL:
        flat = parts[n].reshape(-1)
        pad = (-flat.shape[0]) % LANES
        rows.append(jnp.pad(flat, (0, pad)).reshape(-1, LANES))
    packed = jnp.concatenate(rows, axis=0)
    return jnp.pad(packed, ((0, (-packed.shape[0]) % 8), (0, 0)))


def _unpack_small(packed, shapes):
    out, r0 = {}, 0
    for n in SMALL:
        size = math.prod(shapes[n])
        nr = -(-size // LANES)
        out[n] = packed[r0:r0 + nr].reshape(-1)[:size].reshape(shapes[n])
        r0 += nr
    return out


def kernel(x, ffn1_w_gate, ffn1_w_up, ffn1_w_down, ln1_g, ln1_b, w_in, b_forget, conv_w, conv_b, rg_wa, rg_ba, rg_wx, rg_bx, lru_lambda, w_out, ln2_g, ln2_b, ffn2_w_gate, ffn2_w_up, ffn2_w_down, ln3_g, ln3_b, loss_target, m_ffn1_w_gate, m_ffn1_w_up, m_ffn1_w_down, m_ln1_g, m_ln1_b, m_w_in, m_b_forget, m_conv_w, m_conv_b, m_rg_wa, m_rg_ba, m_rg_wx, m_rg_bx, m_lru_lambda, m_w_out, m_ln2_g, m_ln2_b, m_ffn2_w_gate, m_ffn2_w_up, m_ffn2_w_down, m_ln3_g, m_ln3_b, v_ffn1_w_gate, v_ffn1_w_up, v_ffn1_w_down, v_ln1_g, v_ln1_b, v_w_in, v_b_forget, v_conv_w, v_conv_b, v_rg_wa, v_rg_ba, v_rg_wx, v_rg_bx, v_lru_lambda, v_w_out, v_ln2_g, v_ln2_b, v_ffn2_w_gate, v_ffn2_w_up, v_ffn2_w_down, v_ln3_g, v_ln3_b):
    args = dict(locals())
    w = {n: args[n] for n in WEIGHTS}
    mom = {n: args["m_" + n] for n in WEIGHTS}
    var = {n: args["v_" + n] for n in WEIGHTS}
    chip = 2 * lax.axis_index("x") + lax.axis_index("y")

    g1 = _all_gather_bf16([w[n][0] for n in BIG[:2]], name="ag_ffn1_up")
    fs = D_FF // N_SHARD
    full = dict(
        f1g=g1[0].reshape(N_SHARD, D_MODEL, fs), f1u=g1[1].reshape(N_SHARD, D_MODEL, fs),
        bfp=jnp.pad(b_forget, ((0, 0), (0, LANES - HEADS))),
        ln1_g=ln1_g, ln1_b=ln1_b, ln2_g=ln2_g, ln2_b=ln2_b, ln3_g=ln3_g, ln3_b=ln3_b,
        conv_b=conv_b, rg_wa=rg_wa[0], rg_wx=rg_wx[0], rg_ba=rg_ba[0], rg_bx=rg_bx[0], lam=lru_lambda,
    )
    cw_place = lax.dynamic_update_slice(jnp.zeros((8, LRU_W), f32), conv_w[0] * 0.5, (0, chip * (LRU_W // N_SHARD)))
    cw_full = _all_reduce_small(cw_place.reshape(-1, LANES), g1[0], name="ag_conv_w")
    full["conv_w"] = cw_full.reshape(8, LRU_W)[:CONV_K]

    hooks = _Overlap([w["ffn1_w_down"][0], w["w_in"][0], w["w_out"][0]], [w[n][0] for n in BIG[3:]], cw_full)
    loss_rep, dx, g = _local_step(x[0], loss_target[0], full, hooks)

    token1 = hooks.ffn1_grads(g["f1"][1])
    red = _join_halves(hooks.reduced + hooks.mixer_reduced([token1]), name="rs_join_rest")
    grads = dict(zip(BIG[3:] + ["w_in", "w_out"], red))

    small_sum, loss = hooks.small_summed(red)
    small_shapes = {n: w[n].shape for n in SMALL}
    small_shapes["conv_w"] = (1, CONV_K, LRU_W)
    gs_red = _unpack_small(small_sum, small_shapes)
    gs_red["conv_w"] = lax.dynamic_slice(gs_red["conv_w"], (0, 0, chip * (LRU_W // N_SHARD)),
                                         (1, CONV_K, LRU_W // N_SHARD))
    grads.update(gs_red)

    delta, new_m, new_v = {}, {}, {}

    def adamw(names, name, **kw):
        g3, d, nm, nv = _adamw([grads[n] for n in names], [w[n] for n in names], [mom[n] for n in names],
                               [var[n] for n in names], name=name, **kw)
        for i, n in enumerate(names):
            grads[n], delta[n], new_m[n], new_v[n] = g3[i], d[i], nm[i], nv[i]

    adamw(BIG[3:], "adamw_ffn2", tm=128)
    adamw(["w_in"], "adamw_w_in")
    adamw(["w_out"], "adamw_w_out")
    shard_shapes = {n: w[n].shape for n in SMALL}
    _, d, nm, nv = _adamw([_pack_small({n: grads[n] for n in SMALL})], [_pack_small({n: w[n] for n in SMALL})],
                          [_pack_small({n: mom[n] for n in SMALL})], [_pack_small({n: var[n] for n in SMALL})],
                          name="adamw_small", tm=None)
    for dst, packed in ((delta, d[0]), (new_m, nm[0]), (new_v, nv[0])):
        dst.update(_unpack_small(packed, shard_shapes))

    worked = [new_v["ffn2_w_down"], new_v["w_in"], new_v["w_out"], nv[0]]
    ga, ua, da, gb, ub, db = _join_halves(hooks.ffn1_reduced(worked), name="rs_join_ffn1")
    grads.update(ffn1_w_gate=(ga, gb, 1), ffn1_w_up=(ua, ub, 1), ffn1_w_down=(da, db, 0))
    adamw(BIG[:3], "adamw_ffn1", tm=128)

    def shaped(tree, n):
        return tree[n].reshape(w[n].shape)

    return (loss, dx[None], *[shaped(grads, n) for n in WEIGHTS], *[shaped(delta, n) for n in WEIGHTS],
            *[shaped(new_m, n) for n in WEIGHTS], *[shaped(new_v, n) for n in WEIGHTS])
```

```python
import functools
import math

import jax
import jax.numpy as jnp
from jax import lax
from jax.experimental import pallas as pl
from jax.experimental.pallas import tpu as pltpu

f32 = jnp.float32
MXU_DTYPE = jnp.bfloat16
GRAD_DTYPE = jnp.bfloat16

D_MODEL = 1024
D_FF = 4096
N_SHARD = 4
N_DEV = 8
FOX_W = 512
LRU_W = 512
HEADS = 8
HEAD_DIM = 64
CONV_K = 4
IN_COLS = 2568
IN_SHARD = IN_COLS // N_SHARD
QKV_W = 3 * FOX_W
Z_PAD = 2688
CAST_COLS = 384
LANES = 128
LN_EPS = 1e-5
DN_ALPHA = 2.0 ** 0.25
LRU_C = 8.0
NEG_BIG = -1e30
VMEM_LIMIT = 56 * 1024 * 1024

ADAM_LR = 0.001
ADAM_B1 = 0.9
ADAM_B2 = 0.999
ADAM_EPS = 1e-08
ADAM_WD = 0.01
ADAM_STEP = 10


def _pcall(body, **kw):
    return pl.pallas_call(body, **kw)


def _params(n_grid, vmem=VMEM_LIMIT):
    return pltpu.CompilerParams(dimension_semantics=("arbitrary",) * n_grid, vmem_limit_bytes=vmem)


def _dot(a, b):
    return jnp.dot(a, b, preferred_element_type=f32)


def _dot_nt(a, b):
    return lax.dot_general(a, b, (((1,), (1,)), ((), ())), preferred_element_type=f32)


def _dot_tn(a, b):
    return lax.dot_general(a, b, (((0,), (0,)), ((), ())), preferred_element_type=f32)


def _sigmoid(x):
    return 1.0 / (1.0 + jnp.exp(-x))


def _layer_norm_stats(y):
    mu = jnp.mean(y, axis=-1, keepdims=True)
    yc = y - mu
    var = jnp.mean(yc * yc, axis=-1, keepdims=True)
    rstd = lax.rsqrt(var + LN_EPS)
    return yc * rstd, rstd


def _ln_backward(dy, xhat, rstd, gamma):
    dxhat = dy * gamma
    m1 = jnp.mean(dxhat, axis=-1, keepdims=True)
    m2 = jnp.mean(dxhat * xhat, axis=-1, keepdims=True)
    dyp = rstd * (dxhat - m1 - xhat * m2)
    return dyp, jnp.sum(dy * xhat, axis=0, keepdims=True), jnp.sum(dy, axis=0, keepdims=True)


def _ffn_fwd_loss(x, wg, wu, wd, ln_g, ln_b, target, *, name, tm=1024, tf=512):
    T = x.shape[0]
    tm = min(tm, T)
    tr = min(256, tm)
    fs = D_FF // N_SHARD
    cpf = fs // tf
    nf = D_FF // tf
    nt = T // tm

    def body(x_ref, wg_ref, wu_ref, wd_ref, g_ref, b_ref, t_ref,
             xb_ref, gact_ref, uact_ref, dyp_ref, dgam_ref, dbeta_ref, loss_ref, acc_ref):
        i = pl.program_id(0)
        f = pl.program_id(1)

        @pl.when(jnp.logical_and(i == 0, f == 0))
        def _():
            dgam_ref[...] = jnp.zeros_like(dgam_ref)
            dbeta_ref[...] = jnp.zeros_like(dbeta_ref)
            loss_ref[...] = jnp.zeros_like(loss_ref)

        @pl.when(f == 0)
        def _():
            xb_ref[...] = x_ref[...].astype(MXU_DTYPE)
            acc_ref[...] = jnp.zeros_like(acc_ref)

        xb = xb_ref[...]
        g = _dot(xb, wg_ref[...])
        u = _dot(xb, wu_ref[...])
        h = (g * _sigmoid(g)) * u
        gact_ref[...] = g.astype(gact_ref.dtype)
        uact_ref[...] = u.astype(uact_ref.dtype)
        acc_ref[...] += _dot(h.astype(MXU_DTYPE), wd_ref[...])

        @pl.when(f == nf - 1)
        def _():
            gamma = g_ref[...]

            def rows_chunk(r, carry):
                rows = pl.ds(pl.multiple_of(r * tr, tr), tr)
                xhat, rstd = _layer_norm_stats(DN_ALPHA * x_ref[rows, :] + 0.5 * acc_ref[rows, :])
                err = xhat * gamma + b_ref[...] - t_ref[rows, :]
                sq = jnp.sum(jnp.sum(err * err, axis=0, keepdims=True), axis=1, keepdims=True)
                loss_ref[...] += jnp.broadcast_to(sq * (0.5 / D_MODEL), loss_ref.shape)
                dyp, dgam, dbeta = _ln_backward(err * (1.0 / D_MODEL), xhat, rstd, gamma)
                dyp_ref[rows, :] = dyp
                dgam_ref[...] += dgam
                dbeta_ref[...] += dbeta
                return carry

            lax.fori_loop(0, tm // tr, rows_chunk, 0)

    row = lambda i, f: (i, 0)
    const = lambda i, f: (0, 0)
    tile = pl.BlockSpec((tm, tf), lambda i, f: (i, f))
    cols = pl.BlockSpec((None, D_MODEL, tf), lambda i, f: (f // cpf, 0, f % cpf))
    last = lambda i, f: (jnp.where(f == nf - 1, i, jnp.maximum(i - 1, 0)), 0)
    return _pcall(
        body, name=name, grid=(nt, nf),
        in_specs=[pl.BlockSpec((tm, D_MODEL), row), cols, cols,
                  pl.BlockSpec((None, tf, D_MODEL), lambda i, f: (f // cpf, f % cpf, 0)),
                  pl.BlockSpec((1, D_MODEL), const), pl.BlockSpec((1, D_MODEL), const),
                  pl.BlockSpec((tm, D_MODEL), last)],
        out_specs=[pl.BlockSpec((tm, D_MODEL), row), tile, tile, pl.BlockSpec((tm, D_MODEL), row),
                   pl.BlockSpec((1, D_MODEL), const), pl.BlockSpec((1, D_MODEL), const), pl.BlockSpec((1, LANES), const)],
        out_shape=[jax.ShapeDtypeStruct((T, D_MODEL), MXU_DTYPE), jax.ShapeDtypeStruct((T, D_FF), MXU_DTYPE),
                   jax.ShapeDtypeStruct((T, D_FF), MXU_DTYPE), jax.ShapeDtypeStruct((T, D_MODEL), f32),
                   jax.ShapeDtypeStruct((1, D_MODEL), f32), jax.ShapeDtypeStruct((1, D_MODEL), f32),
                   jax.ShapeDtypeStruct((1, LANES), f32)],
        scratch_shapes=[pltpu.VMEM((tm, D_MODEL), f32)],
        compiler_params=_params(2),
    )(x, wg, wu, wd, ln_g, ln_b, target)


def _ffn_up(x, wg, wu, after=None, *, name, tm=1024, tf=512):
    T = x.shape[0]
    tm = min(tm, T)
    cpf = (D_FF // N_SHARD) // tf
    nf = D_FF // tf
    extra = [] if after is None else [after]

    def body(x_ref, wg_ref, wu_ref, *refs):
        xb_ref, gact_ref, uact_ref, hact_ref = refs[len(extra):]

        @pl.when(pl.program_id(1) == 0)
        def _():
            xb_ref[...] = x_ref[...].astype(MXU_DTYPE)

        xb = xb_ref[...]
        g = _dot(xb, wg_ref[...])
        u = _dot(xb, wu_ref[...])
        gact_ref[...] = g.astype(gact_ref.dtype)
        uact_ref[...] = u.astype(uact_ref.dtype)
        hact_ref[...] = ((g * _sigmoid(g)) * u).astype(hact_ref.dtype)

    row = lambda i, f: (i, 0)
    tile = pl.BlockSpec((tm, tf), lambda i, f: (i, f))
    cols = pl.BlockSpec((None, D_MODEL, tf), lambda i, f: (f // cpf, 0, f % cpf))
    return _pcall(
        body, name=name, grid=(T // tm, nf),
        in_specs=[pl.BlockSpec((tm, D_MODEL), row), cols, cols] + [pl.BlockSpec(memory_space=pl.ANY)] * len(extra),
        out_specs=[pl.BlockSpec((tm, D_MODEL), row), tile, tile, tile],
        out_shape=[jax.ShapeDtypeStruct((T, D_MODEL), MXU_DTYPE)] + [jax.ShapeDtypeStruct((T, D_FF), MXU_DTYPE)] * 3,
        compiler_params=_params(2),
    )(x, wg, wu, *extra)


def _ffn_down_ln(x, hact, wd, ln_g, ln_b, *, name, tm=1024):
    T = x.shape[0]
    tm = min(tm, T)
    fs = D_FF // N_SHARD
    ks = 2
    nk = N_SHARD // ks

    def body(x_ref, h_ref, wd_ref, g_ref, b_ref, xhat_ref, xn_ref, rstd_ref, acc_ref):
        k = pl.program_id(1)

        @pl.when(k == 0)
        def _():
            acc_ref[...] = jnp.zeros_like(acc_ref)

        acc_ref[...] += _dot(h_ref[...], wd_ref[...].reshape(ks * fs, D_MODEL))

        @pl.when(k == nk - 1)
        def _():
            xhat, rstd = _layer_norm_stats(DN_ALPHA * x_ref[...] + 0.5 * acc_ref[...])
            xhat_ref[...] = xhat
            xn_ref[...] = (xhat * g_ref[...] + b_ref[...]).astype(xn_ref.dtype)
            rstd_ref[...] = jnp.broadcast_to(rstd, rstd_ref.shape)

    row = lambda i, k: (i, 0)
    vec = pl.BlockSpec((1, D_MODEL), lambda i, k: (0, 0))
    return _pcall(
        body, name=name, grid=(T // tm, nk),
        in_specs=[pl.BlockSpec((tm, D_MODEL), row), pl.BlockSpec((tm, ks * fs), lambda i, k: (i, k)),
                  pl.BlockSpec((ks, fs, D_MODEL), lambda i, k: (k, 0, 0)), vec, vec],
        out_specs=[pl.BlockSpec((tm, D_MODEL), row), pl.BlockSpec((tm, D_MODEL), row), pl.BlockSpec((tm, LANES), row)],
        out_shape=[jax.ShapeDtypeStruct((T, D_MODEL), f32), jax.ShapeDtypeStruct((T, D_MODEL), MXU_DTYPE),
                   jax.ShapeDtypeStruct((T, LANES), f32)],
        scratch_shapes=[pltpu.VMEM((tm, D_MODEL), f32)],
        compiler_params=_params(2),
    )(x, hact, wd, ln_g, ln_b)


def _ffn_bwd(dyp, xb, gact, uact, wg, wu, wd, after=None, *, name, tm=512, tf=512, part=None, dx_init=None):
    T = dyp.shape[0]
    tm = min(tm, T)
    fs = D_FF // N_SHARD
    cpf = fs // tf
    nt = T // tm
    nf = D_FF // tf if part is None else N_SHARD
    wf = fs if part is None else tf
    slab = (lambda f: f // cpf) if part is None else (lambda f: f)
    chunk = (lambda f: f % cpf) if part is None else (lambda f: part)
    extra = ([] if dx_init is None else [dx_init]) + ([] if after is None else [after])

    def body(dyp_ref, xb_ref, g_ref, u_ref, wg_ref, wu_ref, wd_ref, *refs):
        dx_hbm, dwg_ref, dwu_ref, dwd_ref, dx_sc, dwg_sc, dwu_sc, dwd_sc, sem = refs[len(extra):]
        f = pl.program_id(0)
        i = pl.program_id(1)
        rows = pl.ds(pl.multiple_of(i * tm, tm), tm)
        dyp_t = dyp_ref[...]
        dy = (0.5 * dyp_t).astype(MXU_DTYPE)

        @pl.when(i == 0)
        def _():
            dwg_sc[...] = jnp.zeros_like(dwg_sc)
            dwu_sc[...] = jnp.zeros_like(dwu_sc)
            dwd_sc[...] = jnp.zeros_like(dwd_sc)

        @pl.when(f == 0)
        def _():
            dx_sc[rows, :] = DN_ALPHA * dyp_t if dx_init is None else refs[0][...]

        g = g_ref[...].astype(f32)
        u = u_ref[...].astype(f32)
        sig = _sigmoid(g)
        silu = g * sig
        dh = _dot_nt(dy, wd_ref[...])
        dg = (dh * u * (sig * (1.0 + g * (1.0 - sig)))).astype(MXU_DTYPE)
        du = (dh * silu).astype(MXU_DTYPE)
        hb = (silu * u).astype(MXU_DTYPE)
        dx_sc[rows, :] += _dot_nt(dg, wg_ref[...]) + _dot_nt(du, wu_ref[...])
        xb_t = xb_ref[...]
        dwg_sc[...] += _dot_tn(xb_t, dg)
        dwu_sc[...] += _dot_tn(xb_t, du)
        dwd_sc[...] += _dot_tn(hb, dy)

        @pl.when(i == nt - 1)
        def _():
            dwg_ref[...] = dwg_sc[...].astype(dwg_ref.dtype)
            dwu_ref[...] = dwu_sc[...].astype(dwu_ref.dtype)
            dwd_ref[...] = dwd_sc[...].astype(dwd_ref.dtype)

        @pl.when(jnp.logical_and(f == nf - 1, i == nt - 1))
        def _():
            cp = pltpu.make_async_copy(dx_sc, dx_hbm, sem)
            cp.start()
            cp.wait()

    row = lambda f, i: (i, 0)
    return _pcall(
        body, name=name, grid=(nf, nt),
        in_specs=[
            pl.BlockSpec((tm, D_MODEL), row),
            pl.BlockSpec((tm, D_MODEL), row),
            pl.BlockSpec((tm, tf), lambda f, i: (i, slab(f) * cpf + chunk(f))),
            pl.BlockSpec((tm, tf), lambda f, i: (i, slab(f) * cpf + chunk(f))),
            pl.BlockSpec((None, D_MODEL, tf), lambda f, i: (slab(f), 0, chunk(f))),
            pl.BlockSpec((None, D_MODEL, tf), lambda f, i: (slab(f), 0, chunk(f))),
            pl.BlockSpec((None, tf, D_MODEL), lambda f, i: (slab(f), chunk(f), 0)),
        ] + ([] if dx_init is None else [pl.BlockSpec((tm, D_MODEL), row)])
        + ([] if after is None else [pl.BlockSpec(memory_space=pl.ANY)]),
        out_specs=[
            pl.BlockSpec(memory_space=pl.ANY),
            pl.BlockSpec((None, D_MODEL, tf), lambda f, i: (slab(f), 0, chunk(f) if part is None else 0)),
            pl.BlockSpec((None, D_MODEL, tf), lambda f, i: (slab(f), 0, chunk(f) if part is None else 0)),
            pl.BlockSpec((None, tf, D_MODEL), lambda f, i: (slab(f), chunk(f) if part is None else 0, 0)),
        ],
        out_shape=[
            jax.ShapeDtypeStruct((T, D_MODEL), f32),
            jax.ShapeDtypeStruct((N_SHARD, D_MODEL, wf), GRAD_DTYPE),
            jax.ShapeDtypeStruct((N_SHARD, D_MODEL, wf), GRAD_DTYPE),
            jax.ShapeDtypeStruct((N_SHARD, wf, D_MODEL), GRAD_DTYPE),
        ],
        scratch_shapes=[pltpu.VMEM((T, D_MODEL), f32), pltpu.VMEM((D_MODEL, tf), f32),
                        pltpu.VMEM((D_MODEL, tf), f32), pltpu.VMEM((tf, D_MODEL), f32),
                        pltpu.SemaphoreType.DMA],
        compiler_params=_params(2),
    )(dyp, xb, gact, uact, wg, wu, wd, *extra)


def _proj_in(xn, wp, bfp, *, name, tm=512):
    T = xn.shape[0]
    tm = min(tm, T)
    nt = T // tm

    def body(x_ref, w_ref, b_ref, lxg_ref, fg_ref, qa_ref, ka_ref, va_ref, carry):
        i = pl.program_id(0)

        @pl.when(i == 0)
        def _():
            carry[...] = jnp.zeros_like(carry)

        z = _dot(x_ref[...], w_ref[...])
        lxg_ref[...] = z[:, QKV_W:QKV_W + 2 * LRU_W]
        fg = z[:, QKV_W + 2 * LRU_W:] + b_ref[...]
        fg_ref[...] = fg
        ls = jnp.minimum(fg, 0.0) - jnp.log(1.0 + jnp.exp(-jnp.abs(fg)))
        r = lax.broadcasted_iota(jnp.int32, (tm, tm), 0)
        c = lax.broadcasted_iota(jnp.int32, (tm, tm), 1)
        cum = _tri_dot(jnp.where(r >= c, 1.0, 0.0).astype(jnp.bfloat16), ls) + carry[0:1, :]
        carry[...] = jnp.broadcast_to(cum[tm - 1:tm, :], carry.shape)

        lane = lax.broadcasted_iota(jnp.int32, (tm, LANES), 1)
        low = lane < HEAD_DIM
        ones_q = jnp.where(jnp.logical_and(lane >= AUX + 3, lane < AUX + 6), 1.0, 0.0)
        ones_k = jnp.where(jnp.logical_and(lane >= AUX, lane < AUX + 3), 1.0, 0.0)
        for j in range(HEADS // 2):
            pair = [z[:, t * FOX_W + j * LANES:t * FOX_W + (j + 1) * LANES] for t in range(3)]
            for odd in range(2):
                h = 2 * j + odd
                q, k, v = [_swap_lane_halves(a) if odd else a for a in pair]
                hi, mid, lo = [a.astype(f32) for a in _split3(jnp.broadcast_to(cum[:, h:h + 1], (tm, LANES)))]
                aux_q = jnp.where(lane == AUX, hi, jnp.where(lane == AUX + 1, mid, jnp.where(lane == AUX + 2, lo, ones_q)))
                aux_k = jnp.where(lane == AUX + 3, -hi,
                                  jnp.where(lane == AUX + 4, -mid, jnp.where(lane == AUX + 5, -lo, ones_k)))
                blk = slice(h * LANES, (h + 1) * LANES)
                qa_ref[:, blk] = jnp.where(low, q, aux_q).astype(qa_ref.dtype)
                ka_ref[:, blk] = jnp.where(low, k, aux_k).astype(ka_ref.dtype)
                va_ref[:, blk] = jnp.where(low, v, 1.0).astype(va_ref.dtype)

    row = lambda i: (i, 0)
    const = lambda i: (0, 0)
    return _pcall(
        body, name=name, grid=(nt,),
        in_specs=[pl.BlockSpec((tm, D_MODEL), row), pl.BlockSpec((D_MODEL, Z_PAD), const),
                  pl.BlockSpec((1, LANES), const)],
        out_specs=[pl.BlockSpec((tm, 2 * LRU_W), row), pl.BlockSpec((tm, LANES), row)]
        + [pl.BlockSpec((tm, HEADS * LANES), row)] * 3,
        out_shape=[jax.ShapeDtypeStruct((T, 2 * LRU_W), f32), jax.ShapeDtypeStruct((T, LANES), f32)]
        + [jax.ShapeDtypeStruct((T, HEADS * LANES), MXU_DTYPE)] * 3,
        scratch_shapes=[pltpu.VMEM((8, LANES), f32)],
        compiler_params=_params(1),
    )(xn, wp, bfp)


def _proj_in_bwd(dqa, dka, dva, dlxg, fgb, xn, dyp, wp, xhat, rstd, ln_g, *, name, tm=512):
    T = xn.shape[0]
    tm = min(tm, T)
    nt = T // tm

    def body(dq_ref, dk_ref, dv_ref, dl_ref, fg_ref, x_ref, dyp_ref, w_ref, xhat_ref, rstd_ref, g_ref,
             dpre_ref, dw_hbm, dgam_ref, dbeta_ref, dbf_ref, dw_sc, dw_out, carry, sem):
        i = pl.program_id(0)

        @pl.when(i == 0)
        def _():
            dw_sc[...] = jnp.zeros_like(dw_sc)
            dgam_ref[...] = jnp.zeros_like(dgam_ref)
            dbeta_ref[...] = jnp.zeros_like(dbeta_ref)
            dbf_ref[...] = jnp.zeros_like(dbf_ref)
            carry[...] = jnp.zeros_like(carry)

        lane = lax.broadcasted_iota(jnp.int32, (tm, LANES), 1)
        dc = jnp.zeros((tm, LANES), f32)
        for h in range(HEADS):
            row_sum = dq_ref[:, h * LANES + AUX:h * LANES + AUX + 1]
            col_sum = dk_ref[:, h * LANES + AUX + 3:h * LANES + AUX + 4]
            dc = jnp.where(lane == h, jnp.broadcast_to(row_sum - col_sum, (tm, LANES)), dc)
        r = lax.broadcasted_iota(jnp.int32, (tm, tm), 0)
        c = lax.broadcasted_iota(jnp.int32, (tm, tm), 1)
        dls = _tri_dot(jnp.where(c >= r, 1.0, 0.0).astype(jnp.bfloat16), dc) + carry[0:1, :]
        carry[...] = jnp.broadcast_to(dls[0:1, :], carry.shape)
        dfg = dls * _sigmoid(-fg_ref[...])
        dbf_ref[...] += jnp.sum(dfg, axis=0, keepdims=True)

        low = _low_lanes((tm, LANES))

        def packed(ref):
            pairs = [jnp.where(low, ref[:, (2 * j) * LANES:(2 * j + 1) * LANES],
                               _swap_lane_halves(ref[:, (2 * j + 1) * LANES:(2 * j + 2) * LANES]))
                     for j in range(HEADS // 2)]
            return jnp.concatenate(pairs, axis=1).astype(MXU_DTYPE)

        dz = jnp.concatenate(
            [packed(dq_ref), packed(dk_ref), packed(dv_ref),
             dl_ref[...].astype(MXU_DTYPE), dfg.astype(MXU_DTYPE)], axis=1)
        dx = DN_ALPHA * dyp_ref[...] + _dot_nt(dz, w_ref[...])
        dpre, dgam, dbeta = _ln_backward(dx, xhat_ref[...], rstd_ref[:, 0:1], g_ref[...])
        dpre_ref[...] = dpre
        dgam_ref[...] += dgam
        dbeta_ref[...] += dbeta
        dw_sc[...] += _dot_tn(x_ref[...], dz)

        @pl.when(i == nt - 1)
        def _():
            dw_sc[:, :FOX_W] = dw_sc[:, :FOX_W] * (1.0 / math.sqrt(HEAD_DIM))
            for c0 in range(0, Z_PAD, CAST_COLS):
                dw_out[:, c0:c0 + CAST_COLS] = dw_sc[:, c0:c0 + CAST_COLS].astype(dw_out.dtype)
            cp = pltpu.make_async_copy(dw_out, dw_hbm, sem)
            cp.start()
            cp.wait()

    row = lambda i: (nt - 1 - i, 0)
    const = lambda i: (0, 0)
    return _pcall(
        body, name=name, grid=(nt,),
        in_specs=[pl.BlockSpec((tm, HEADS * LANES), row), pl.BlockSpec((tm, HEADS * LANES), row),
                  pl.BlockSpec((tm, HEADS * LANES), row),
                  pl.BlockSpec((tm, 2 * LRU_W), row), pl.BlockSpec((tm, LANES), row),
                  pl.BlockSpec((tm, D_MODEL), row), pl.BlockSpec((tm, D_MODEL), row),
                  pl.BlockSpec((D_MODEL, Z_PAD), const),
                  pl.BlockSpec((tm, D_MODEL), row), pl.BlockSpec((tm, LANES), row), pl.BlockSpec((1, D_MODEL), const)],
        out_specs=[pl.BlockSpec((tm, D_MODEL), row), pl.BlockSpec(memory_space=pl.ANY),
                   pl.BlockSpec((1, D_MODEL), const), pl.BlockSpec((1, D_MODEL), const), pl.BlockSpec((1, LANES), const)],
        out_shape=[jax.ShapeDtypeStruct((T, D_MODEL), f32), jax.ShapeDtypeStruct((D_MODEL, Z_PAD), GRAD_DTYPE),
                   jax.ShapeDtypeStruct((1, D_MODEL), f32), jax.ShapeDtypeStruct((1, D_MODEL), f32),
                   jax.ShapeDtypeStruct((1, LANES), f32)],
        scratch_shapes=[pltpu.VMEM((D_MODEL, Z_PAD), f32), pltpu.VMEM((D_MODEL, Z_PAD), GRAD_DTYPE),
                        pltpu.VMEM((8, LANES), f32), pltpu.SemaphoreType.DMA],
        compiler_params=_params(1),
    )(dqa, dka, dva, dlxg, fgb, xn, dyp, wp, xhat, rstd, ln_g)


def _split3(x):
    hi = x.astype(jnp.bfloat16)
    r1 = x - hi.astype(f32)
    mid = r1.astype(jnp.bfloat16)
    lo = (r1 - mid.astype(f32)).astype(jnp.bfloat16)
    return hi, mid, lo


def _tri_dot(tri, x):
    hi, mid, lo = _split3(x)
    return _dot(tri, hi) + _dot(tri, mid) + _dot(tri, lo)


FOX_PAD = HEADS * LANES
AUX = HEAD_DIM


def _low_lanes(shape):
    return lax.broadcasted_iota(jnp.int32, shape, 1) < HEAD_DIM


def _swap_lane_halves(x):
    return pltpu.roll(x, HEAD_DIM, 1)


def _future_keys(tq, tk):
    r = lax.broadcasted_iota(jnp.int32, (tq, tk), 0)
    c = lax.broadcasted_iota(jnp.int32, (tq, tk), 1)
    return c > r


def _causal_steps(nq, key_major):
    if key_major:
        pairs = [(qi, ki) for ki in range(nq) for qi in range(ki, nq)]
    else:
        pairs = [(qi, ki) for qi in range(nq) for ki in range(qi + 1)]
    return (jnp.asarray([p[0] for p in pairs], jnp.int32), jnp.asarray([p[1] for p in pairs], jnp.int32))


def _fox_fwd(qa, ka, va, *, name, tq=512, hps=8):
    T = qa.shape[0]
    tq = min(tq, T)
    tk = tq
    nq = T // tq
    rep = tk // LANES
    qi_tab, ki_tab = _causal_steps(nq, key_major=False)

    def body(qi_ref, ki_ref, qa_ref, ka_ref, va_ref, o_ref, lse_ref, m_sc, acc_sc):
        t = pl.program_id(1)
        qi = qi_ref[t]
        ki = ki_ref[t]

        @pl.when(ki == 0)
        def _():
            m_sc[...] = jnp.full_like(m_sc, NEG_BIG)
            acc_sc[...] = jnp.zeros_like(acc_sc)

        def tile(diagonal):
            for h in range(hps):
                blk = slice(h * LANES, (h + 1) * LANES)
                s = _dot_nt(qa_ref[:, blk], ka_ref[:, blk])
                if diagonal:
                    s = jnp.where(_future_keys(tq, tk), NEG_BIG, s)
                m_prev = m_sc[h]
                m_new = jnp.maximum(m_prev, jnp.max(s, axis=1, keepdims=True))
                p = jnp.exp(s - jnp.tile(m_new, (1, rep)))
                acc_sc[h] = jnp.exp(m_prev - m_new) * acc_sc[h] + _dot(p.astype(MXU_DTYPE), va_ref[:, blk])
                m_sc[h] = m_new

        @pl.when(ki < qi)
        def _():
            tile(False)

        @pl.when(ki == qi)
        def _():
            tile(True)
            low = _low_lanes((tq, LANES))
            outs = []
            for h in range(hps):
                acc = acc_sc[h]
                den = _swap_lane_halves(acc)
                outs.append(acc / den)
                lse_ref[h] = m_sc[h] + jnp.log(jnp.where(low, den, acc))
            for p in range(hps // 2):
                o_ref[:, p * LANES:(p + 1) * LANES] = jnp.where(low, outs[2 * p], _swap_lane_halves(outs[2 * p + 1]))

    pair = hps * LANES
    return _pcall(
        body, name=name,
        grid_spec=pltpu.PrefetchScalarGridSpec(
            num_scalar_prefetch=2, grid=(HEADS // hps, qi_tab.shape[0]),
            in_specs=[
                pl.BlockSpec((tq, pair), lambda j, t, qi_ref, ki_ref: (qi_ref[t], j)),
                pl.BlockSpec((tk, pair), lambda j, t, qi_ref, ki_ref: (ki_ref[t], j)),
                pl.BlockSpec((tk, pair), lambda j, t, qi_ref, ki_ref: (ki_ref[t], j)),
            ],
            out_specs=[pl.BlockSpec((tq, pair // 2), lambda j, t, qi_ref, ki_ref: (qi_ref[t], j)),
                       pl.BlockSpec((hps, tq, LANES), lambda j, t, qi_ref, ki_ref: (j, qi_ref[t], 0))],
            scratch_shapes=[pltpu.VMEM((hps, tq, LANES), f32)] * 2),
        out_shape=[jax.ShapeDtypeStruct((T, FOX_W), f32), jax.ShapeDtypeStruct((HEADS, T, LANES), f32)],
        compiler_params=_params(2),
    )(qi_tab, ki_tab, qa, ka, va)


def _fox_bwd(qa, ka, va, doa, lse, drep, *, name, tq=512, hps=8):
    T = qa.shape[0]
    tq = min(tq, T)
    tk = tq
    nq = T // tq
    rep = tk // LANES
    qi_tab, ki_tab = _causal_steps(nq, key_major=True)

    def body(qi_ref, ki_ref, qa_ref, ka_ref, va_ref, doa_ref, lse_ref, d_ref, dqa_ref, dka_ref, dva_ref, dk_sc, dv_sc):
        t = pl.program_id(1)
        qi = qi_ref[t]
        ki = ki_ref[t]
        rows = pl.ds(pl.multiple_of(qi * tq, tq), tq)

        @pl.when(t == 0)
        def _():
            dqa_ref[...] = jnp.zeros_like(dqa_ref)

        @pl.when(qi == ki)
        def _():
            dk_sc[...] = jnp.zeros_like(dk_sc)
            dv_sc[...] = jnp.zeros_like(dv_sc)

        def tile(diagonal):
            for h in range(hps):
                blk = slice(h * LANES, (h + 1) * LANES)
                qh, kh, doh = qa_ref[:, blk], ka_ref[:, blk], doa_ref[:, blk]
                p = jnp.exp(_dot_nt(qh, kh) - jnp.tile(lse_ref[h], (1, rep)))
                if diagonal:
                    p = jnp.where(_future_keys(tq, tk), 0.0, p)
                dp = _dot_nt(doh, va_ref[:, blk])
                ds = (p * (dp - jnp.tile(d_ref[h], (1, rep)))).astype(MXU_DTYPE)
                dv_sc[h] += _dot_tn(p.astype(MXU_DTYPE), doh)
                dk_sc[h] += _dot_tn(ds, qh)
                dqa_ref[rows, blk] += _dot(ds, kh)

        @pl.when(qi > ki)
        def _():
            tile(False)

        @pl.when(qi == ki)
        def _():
            tile(True)

        @pl.when(qi == nq - 1)
        def _():
            for h in range(hps):
                blk = slice(h * LANES, (h + 1) * LANES)
                dka_ref[:, blk] = dk_sc[h]
                dva_ref[:, blk] = dv_sc[h]

    pair = hps * LANES
    q_blk = lambda j, t, qi_ref, ki_ref: (qi_ref[t], j)
    k_blk = lambda j, t, qi_ref, ki_ref: (ki_ref[t], j)
    stat = pl.BlockSpec((hps, tq, LANES), lambda j, t, qi_ref, ki_ref: (j, qi_ref[t], 0))
    return _pcall(
        body, name=name,
        grid_spec=pltpu.PrefetchScalarGridSpec(
            num_scalar_prefetch=2, grid=(HEADS // hps, qi_tab.shape[0]),
            in_specs=[pl.BlockSpec((tq, pair), q_blk), pl.BlockSpec((tk, pair), k_blk), pl.BlockSpec((tk, pair), k_blk),
                      pl.BlockSpec((tq, pair), q_blk), stat, stat],
            out_specs=[pl.BlockSpec((T, pair), lambda j, t, qi_ref, ki_ref: (0, j)),
                       pl.BlockSpec((tk, pair), k_blk), pl.BlockSpec((tk, pair), k_blk)],
            scratch_shapes=[pltpu.VMEM((hps, tk, LANES), f32)] * 2),
        out_shape=[jax.ShapeDtypeStruct((T, FOX_PAD), f32)] * 3,
        compiler_params=_params(2),
    )(qi_tab, ki_tab, qa, ka, va, doa, lse, drep)


GELU_C = math.sqrt(2.0 / math.pi)
GELU_A = 0.044715


def _gelu(x):
    t = jnp.tanh(GELU_C * (x + GELU_A * x * x * x))
    return 0.5 * x * (1.0 + t), t


def _gelu_grad(x, t):
    return 0.5 * (1.0 + t) + 0.5 * x * (1.0 - t * t) * GELU_C * (1.0 + 3.0 * GELU_A * x * x)


EXPM1_SERIES_BELOW = 0.25


def _expm1(x, e):
    series = x * (1.0 + x * (1 / 2 + x * (1 / 6 + x * (1 / 24 + x * (1 / 120 + x * (1 / 720))))))
    return jnp.where(x > -EXPM1_SERIES_BELOW, series, e - 1.0)


def _softplus_neg(lam):
    return jnp.maximum(-lam, 0.0) + jnp.log(1.0 + jnp.exp(-jnp.abs(lam)))


def _lru_gates(u, wab_ref, bab_ref, lam_ref):
    pre = _dot(u.astype(MXU_DTYPE), wab_ref[...]) + bab_ref[...]
    r = _sigmoid(pre[:, :LRU_W])
    gi = _sigmoid(pre[:, LRU_W:])
    sp = _softplus_neg(lam_ref[...])
    log_a = -LRU_C * r * sp
    a = jnp.exp(log_a)
    s = jnp.sqrt(-_expm1(2.0 * log_a, a * a))
    return r, gi, sp, a, s


def _lru_fwd(lxg, conv_w, conv_b, wab, bab, lam, *, name, tc=512):
    T = lxg.shape[0]
    tc = min(tc, T)
    nc = T // tc

    def body(lx_ref, lg_ref, cw_ref, cb_ref, wab_ref, bab_ref, lam_ref,
             out_ref, u_ref, hs_ref, gates_ref, ext, a_sc, b_sc, h_sc):
        i = pl.program_id(0)

        @pl.when(i == 0)
        def _():
            ext[0:8, :] = jnp.zeros((8, LRU_W), f32)
            h_sc[...] = jnp.zeros_like(h_sc)

        ext[8:, :] = lx_ref[...]
        u = cb_ref[...] + cw_ref[0:1, :] * ext[pl.ds(5, tc), :]
        for k in range(1, CONV_K):
            u = u + cw_ref[k:k + 1, :] * ext[pl.ds(5 + k, tc), :]
        ext[0:8, :] = ext[tc:tc + 8, :]
        u_ref[...] = u
        r, gi, sp, a, s = _lru_gates(u, wab_ref, bab_ref, lam_ref)
        for n, gate in enumerate((r, gi, a, s)):
            gates_ref[:, n * LRU_W:(n + 1) * LRU_W] = gate
        a_sc[...] = a
        b_sc[...] = s * (gi * u)

        def step(t, h):
            h = a_sc[pl.ds(t, 1), :] * h + b_sc[pl.ds(t, 1), :]
            hs_ref[pl.ds(t, 1), :] = h
            return h

        h = lax.fori_loop(0, tc, step, h_sc[0:1, :], unroll=8)
        h_sc[...] = jnp.broadcast_to(h, h_sc.shape)
        gel, _ = _gelu(lg_ref[...])
        out_ref[...] = gel * hs_ref[...]

    row = lambda i: (i, 0)
    const = lambda i: (0, 0)
    return _pcall(
        body, name=name, grid=(nc,),
        in_specs=[pl.BlockSpec((tc, LRU_W), row), pl.BlockSpec((tc, LRU_W), lambda i: (i, 1)),
                  pl.BlockSpec((CONV_K, LRU_W), const), pl.BlockSpec((1, LRU_W), const),
                  pl.BlockSpec((LRU_W, 2 * LRU_W), const), pl.BlockSpec((1, 2 * LRU_W), const),
                  pl.BlockSpec((1, LRU_W), const)],
        out_specs=[pl.BlockSpec((tc, LRU_W), row)] * 3 + [pl.BlockSpec((tc, 4 * LRU_W), row)],
        out_shape=[jax.ShapeDtypeStruct((T, LRU_W), f32)] * 3 + [jax.ShapeDtypeStruct((T, 4 * LRU_W), f32)],
        scratch_shapes=[pltpu.VMEM((tc + 8, LRU_W), f32), pltpu.VMEM((tc, LRU_W), f32),
                        pltpu.VMEM((tc, LRU_W), f32), pltpu.VMEM((8, LRU_W), f32)],
        compiler_params=_params(1),
    )(lxg, lxg, conv_w, conv_b, wab, bab, lam)


def _lru_bwd(dlru, lxg, u, hs, gates, conv_w, wab, lam, *, name, tc=512):
    T = lxg.shape[0]
    tc = min(tc, T)
    nc = T // tc
    bp = tc // 8

    def body(dl_ref, lx_ref, lxp_ref, lg_ref, u_ref, hs_ref, hsp_ref, gates_ref, cw_ref, wab_ref, lam_ref,
             dlxg_ref, dwab_ref, dbab_ref, dcw_ref, dcb_ref, dlam_ref,
             dh_sc, a_sc, ext, du_ext, carry):
        i = pl.program_id(0)
        first_chunk = i == nc - 1

        @pl.when(i == 0)
        def _():
            dwab_ref[...] = jnp.zeros_like(dwab_ref)
            dbab_ref[...] = jnp.zeros_like(dbab_ref)
            dcw_ref[...] = jnp.zeros_like(dcw_ref)
            dcb_ref[...] = jnp.zeros_like(dcb_ref)
            dlam_ref[...] = jnp.zeros_like(dlam_ref)
            carry[...] = jnp.zeros_like(carry)
            du_ext[tc:tc + 8, :] = jnp.zeros((8, LRU_W), f32)

        lg = lg_ref[...]
        gel, th = _gelu(lg)
        dl = dl_ref[...]
        hs = hs_ref[...]
        dlg = dl * hs * _gelu_grad(lg, th)
        u = u_ref[...]
        r, gi, a, s = [gates_ref[:, n * LRU_W:(n + 1) * LRU_W] for n in range(4)]
        sp = _softplus_neg(lam_ref[...])
        a_sc[...] = a
        dh_sc[...] = dl * gel

        def step(k, c):
            t = tc - 1 - k
            dh = dh_sc[pl.ds(t, 1), :] + c
            dh_sc[pl.ds(t, 1), :] = dh
            return a_sc[pl.ds(t, 1), :] * dh

        c = lax.fori_loop(0, tc, step, carry[0:1, :], unroll=8)
        carry[...] = jnp.broadcast_to(c, carry.shape)

        ext[0:8, :] = jnp.where(first_chunk, 0.0, hsp_ref[...])
        ext[8:, :] = hs
        hprev = ext[pl.ds(7, tc), :]
        dh = dh_sc[...]
        da = dh * hprev
        giu = gi * u
        dla = da * a - (dh * giu) * (a * a / s)
        dgi = dh * s * u
        du = dh * s * gi
        dr = dla * (-LRU_C * sp)
        dlam_ref[...] += jnp.sum(dla * (-LRU_C * r), axis=0, keepdims=True) * (-_sigmoid(-lam_ref[...]))
        dpre = jnp.concatenate([dr * r * (1.0 - r), dgi * gi * (1.0 - gi)], axis=1)
        dpre_b = dpre.astype(MXU_DTYPE)
        du = du + _dot_nt(dpre_b, wab_ref[...])
        dwab_ref[...] += _dot_tn(u.astype(MXU_DTYPE), dpre_b)
        dbab_ref[...] += jnp.sum(dpre, axis=0, keepdims=True)
        dcb_ref[...] += jnp.sum(du, axis=0, keepdims=True)

        du_ext[0:tc, :] = du
        dlx = cw_ref[0:1, :] * du_ext[pl.ds(3, tc), :]
        for k in range(1, CONV_K):
            dlx = dlx + cw_ref[k:k + 1, :] * du_ext[pl.ds(3 - k, tc), :]
        du_ext[tc:tc + 8, :] = du_ext[0:8, :]
        ext[0:8, :] = jnp.where(first_chunk, 0.0, lxp_ref[...])
        ext[8:, :] = lx_ref[...]
        for k in range(CONV_K):
            dcw_ref[k:k + 1, :] += jnp.sum(du * ext[pl.ds(5 + k, tc), :], axis=0, keepdims=True)
        dlxg_ref[:, :LRU_W] = dlx.astype(dlxg_ref.dtype)
        dlxg_ref[:, LRU_W:] = dlg.astype(dlxg_ref.dtype)

    rev = lambda i: (nc - 1 - i, 0)
    prev8 = lambda i: (jnp.maximum((nc - 1 - i) * bp - 1, 0), 0)
    const = lambda i: (0, 0)
    return _pcall(
        body, name=name, grid=(nc,),
        in_specs=[
            pl.BlockSpec((tc, LRU_W), rev),
            pl.BlockSpec((tc, LRU_W), rev),
            pl.BlockSpec((8, LRU_W), prev8),
            pl.BlockSpec((tc, LRU_W), lambda i: (nc - 1 - i, 1)),
            pl.BlockSpec((tc, LRU_W), rev),
            pl.BlockSpec((tc, LRU_W), rev),
            pl.BlockSpec((8, LRU_W), prev8),
            pl.BlockSpec((tc, 4 * LRU_W), rev),
            pl.BlockSpec((CONV_K, LRU_W), const),
            pl.BlockSpec((LRU_W, 2 * LRU_W), const),
            pl.BlockSpec((1, LRU_W), const),
        ],
        out_specs=[
            pl.BlockSpec((tc, 2 * LRU_W), rev),
            pl.BlockSpec((LRU_W, 2 * LRU_W), const),
            pl.BlockSpec((1, 2 * LRU_W), const),
            pl.BlockSpec((8, LRU_W), const),
            pl.BlockSpec((1, LRU_W), const),
            pl.BlockSpec((1, LRU_W), const),
        ],
        out_shape=[
            jax.ShapeDtypeStruct((T, 2 * LRU_W), MXU_DTYPE),
            jax.ShapeDtypeStruct((LRU_W, 2 * LRU_W), f32),
            jax.ShapeDtypeStruct((1, 2 * LRU_W), f32),
            jax.ShapeDtypeStruct((8, LRU_W), f32),
            jax.ShapeDtypeStruct((1, LRU_W), f32),
            jax.ShapeDtypeStruct((1, LRU_W), f32),
        ],
        scratch_shapes=[pltpu.VMEM((tc, LRU_W), f32), pltpu.VMEM((tc, LRU_W), f32),
                        pltpu.VMEM((tc + 8, LRU_W), f32), pltpu.VMEM((tc + 8, LRU_W), f32),
                        pltpu.VMEM((8, LRU_W), f32)],
        compiler_params=_params(1),
    )(dlru, lxg, lxg, lxg, u, hs, hs, gates, conv_w, wab, lam)


def _mix_out(fox, lru, wo, xhat1, g1, b1, g2, b2, *, name, tm=512):
    T = fox.shape[0]
    tm = min(tm, T)
    nt = T // tm

    def body(fox_ref, lru_ref, wo_ref, xh_ref, g1_ref, b1_ref, g2_ref, b2_ref, xhat_ref, xn_ref, rstd_ref):
        mix = _dot(fox_ref[...].astype(MXU_DTYPE), wo_ref[:FOX_W, :])
        mix = mix + _dot(lru_ref[...].astype(MXU_DTYPE), wo_ref[FOX_W:, :])
        x1 = xh_ref[...] * g1_ref[...] + b1_ref[...]
        xhat, rstd = _layer_norm_stats(DN_ALPHA * x1 + mix)
        xhat_ref[...] = xhat
        xn_ref[...] = xhat * g2_ref[...] + b2_ref[...]
        rstd_ref[...] = jnp.broadcast_to(rstd, rstd_ref.shape)

    row = lambda i: (i, 0)
    const = lambda i: (0, 0)
    vec = pl.BlockSpec((1, D_MODEL), const)
    return _pcall(
        body, name=name, grid=(nt,),
        in_specs=[pl.BlockSpec((tm, FOX_W), row), pl.BlockSpec((tm, LRU_W), row),
                  pl.BlockSpec((D_MODEL, D_MODEL), const), pl.BlockSpec((tm, D_MODEL), row), vec, vec, vec, vec],
        out_specs=[pl.BlockSpec((tm, D_MODEL), row), pl.BlockSpec((tm, D_MODEL), row),
                   pl.BlockSpec((tm, LANES), row)],
        out_shape=[jax.ShapeDtypeStruct((T, D_MODEL), f32), jax.ShapeDtypeStruct((T, D_MODEL), f32),
                   jax.ShapeDtypeStruct((T, LANES), f32)],
        compiler_params=_params(1),
    )(fox, lru, wo, xhat1, g1, b1, g2, b2)


def _mix_out_bwd(dy, xhat, rstd, ln_g, fox, lru, wo, *, name, tm=512):
    T = fox.shape[0]
    tm = min(tm, T)
    nt = T // tm

    def body(dy_ref, xhat_ref, rstd_ref, g_ref, fox_ref, lru_ref, wo_ref,
             dyp_ref, dgam_ref, dbeta_ref, dlru_ref, dwo_ref, d_ref, doa_ref):
        i = pl.program_id(0)

        @pl.when(i == 0)
        def _():
            dwo_ref[...] = jnp.zeros_like(dwo_ref)
            dgam_ref[...] = jnp.zeros_like(dgam_ref)
            dbeta_ref[...] = jnp.zeros_like(dbeta_ref)

        dyp, dgam, dbeta = _ln_backward(dy_ref[...], xhat_ref[...], rstd_ref[:, 0:1], g_ref[...])
        dyp_ref[...] = dyp
        dgam_ref[...] += dgam
        dbeta_ref[...] += dbeta
        dmix = dyp.astype(MXU_DTYPE)
        dcat = _dot_nt(dmix, wo_ref[...])
        dlru_ref[...] = dcat[:, FOX_W:]
        low = _low_lanes((tm, LANES))
        for j in range(HEADS // 2):
            do2 = dcat[:, j * LANES:(j + 1) * LANES].astype(MXU_DTYPE).astype(f32)
            prod = do2 * fox_ref[:, j * LANES:(j + 1) * LANES]
            for odd in range(2):
                h = 2 * j + odd
                mine = jnp.where(low, _swap_lane_halves(prod) if odd else prod, 0.0)
                d_ref[h] = jnp.broadcast_to(jnp.sum(mine, axis=1, keepdims=True), (tm, LANES))
                doh = jnp.where(low, _swap_lane_halves(do2) if odd else do2, 0.0)
                doa_ref[:, h * LANES:(h + 1) * LANES] = doh.astype(doa_ref.dtype)
        dwo_ref[:FOX_W, :] += _dot_tn(fox_ref[...].astype(MXU_DTYPE), dmix)
        dwo_ref[FOX_W:, :] += _dot_tn(lru_ref[...].astype(MXU_DTYPE), dmix)

    row = lambda i: (i, 0)
    const = lambda i: (0, 0)
    return _pcall(
        body, name=name, grid=(nt,),
        in_specs=[pl.BlockSpec((tm, D_MODEL), row), pl.BlockSpec((tm, D_MODEL), row), pl.BlockSpec((tm, LANES), row),
                  pl.BlockSpec((1, D_MODEL), const),
                  pl.BlockSpec((tm, FOX_W), row), pl.BlockSpec((tm, LRU_W), row),
                  pl.BlockSpec((D_MODEL, D_MODEL), const)],
        out_specs=[pl.BlockSpec((tm, D_MODEL), row), pl.BlockSpec((1, D_MODEL), const), pl.BlockSpec((1, D_MODEL), const),
                   pl.BlockSpec((tm, LRU_W), row), pl.BlockSpec((D_MODEL, D_MODEL), const),
                   pl.BlockSpec((HEADS, tm, LANES), lambda i: (0, i, 0)), pl.BlockSpec((tm, HEADS * LANES), row)],
        out_shape=[jax.ShapeDtypeStruct((T, D_MODEL), f32), jax.ShapeDtypeStruct((1, D_MODEL), f32),
                   jax.ShapeDtypeStruct((1, D_MODEL), f32),
                   jax.ShapeDtypeStruct((T, LRU_W), f32), jax.ShapeDtypeStruct((D_MODEL, D_MODEL), f32),
                   jax.ShapeDtypeStruct((HEADS, T, LANES), f32), jax.ShapeDtypeStruct((T, HEADS * LANES), MXU_DTYPE)],
        compiler_params=_params(1),
    )(dy, xhat, rstd, ln_g, fox, lru, wo)


def make_wp(w_in):
    scale = jnp.concatenate([jnp.full((FOX_W,), 1.0 / math.sqrt(HEAD_DIM), w_in.dtype),
                             jnp.ones((IN_COLS - FOX_W,), w_in.dtype)])
    return jnp.pad(w_in * scale[None, :], ((0, 0), (0, Z_PAD - IN_COLS)))


def _block_diag(w):
    eye = jnp.eye(HEADS, dtype=w.dtype)
    return jnp.einsum("hij,hg->higj", w, eye).reshape(LRU_W, LRU_W)


def _block_diag_extract(m):
    m4 = m.reshape(HEADS, HEAD_DIM, HEADS, HEAD_DIM)
    return jnp.stack([m4[h, :, h, :] for h in range(HEADS)])


class _NoOverlap:
    def start_token(self):
        return None

    def late_weights(self, w, after):
        return dict(f1d=w["f1d"], wp=w["wp"], wo=w["wo"])

    def after_attention(self, after):
        return None

    def ffn2_weights(self, w, after):
        return w["f2g"], w["f2u"], w["f2d"]

    def ffn2_grads(self, grads):
        return None

    def ffn1_grads(self, grads):
        return None

    def mixer_grads(self, dwp, dwo, small, loss):
        return None

    def before_ffn1_bwd(self, after):
        return None


def _tied(a, token):
    return a if token is None else a + token[0, 0]


def _local_step(x, target, w, hooks=None):
    hooks = hooks or _NoOverlap()
    bfp = w["bfp"]
    wab = jnp.concatenate([_block_diag(w["rg_wa"]), _block_diag(w["rg_wx"])], axis=1).astype(MXU_DTYPE)
    bab = jnp.concatenate([w["rg_ba"].reshape(1, LRU_W), w["rg_bx"].reshape(1, LRU_W)], axis=1)

    xb0, g1a, u1a, h1a = _ffn_up(x, w["f1g"], w["f1u"], hooks.start_token(), name="ffn1_up")
    late = hooks.late_weights(w, [h1a])
    f1d, wp, wo = late["f1d"], late["wp"], late["wo"]
    xhat1, xn1, rstd1 = _ffn_down_ln(x, h1a, f1d, w["ln1_g"], w["ln1_b"], name="ffn1_down")
    lxg, fgb, qa, ka, va = _proj_in(xn1, wp, bfp, name="proj_in")
    fox, lse = _fox_fwd(qa, ka, va, name="fox_fwd")
    token = hooks.after_attention([lse])
    lru, uconv, hs, gates = _lru_fwd(lxg, w["conv_w"], _tied(w["conv_b"], token), wab, bab, w["lam"], name="lru_fwd")
    xhat2, x2, rstd2 = _mix_out(fox, lru, wo, xhat1, w["ln1_g"], w["ln1_b"], w["ln2_g"], w["ln2_b"], name="mix_out")
    f2g, f2u, f2d = hooks.ffn2_weights(w, [rstd2])
    xb2, g2a, u2a, dy3p, dln3g, dln3b, loss = _ffn_fwd_loss(x2, f2g, f2u, f2d, w["ln3_g"], w["ln3_b"], target,
                                                            name="ffn2_fwd_loss")

    dx2, df2g, df2u, df2d = _ffn_bwd(dy3p, xb2, g2a, u2a, f2g, f2u, f2d, name="ffn2_bwd")
    token = hooks.ffn2_grads([df2g, df2u, df2d])
    dy2p, dln2g, dln2b, dlru, dwo, drep, doa = _mix_out_bwd(dx2, xhat2, rstd2, _tied(w["ln2_g"], token), fox, lru, wo,
                                                            name="mix_out_bwd")
    dlxg, dwab, dbab, dcw, dcb, dlam = _lru_bwd(dlru, lxg, uconv, hs, gates, w["conv_w"], wab, w["lam"], name="lru_bwd")
    dqa, dka, dva = _fox_bwd(qa, ka, va, doa, lse, drep, name="fox_bwd")
    dy1p, dwp, dln1g, dln1b, dbf = _proj_in_bwd(dqa, dka, dva, dlxg, fgb, xn1, dy2p, wp, xhat1, rstd1, w["ln1_g"],
                                                name="proj_in_bwd")
    small = dict(
        ln1_g=dln1g, ln1_b=dln1b, ln2_g=dln2g, ln2_b=dln2b, ln3_g=dln3g, ln3_b=dln3b,
        b_forget=dbf[:, :HEADS], conv_w=dcw[:CONV_K], conv_b=dcb,
        rg_wa=_block_diag_extract(dwab[:, :LRU_W]), rg_wx=_block_diag_extract(dwab[:, LRU_W:]),
        rg_ba=dbab[:, :LRU_W].reshape(HEADS, HEAD_DIM), rg_bx=dbab[:, LRU_W:].reshape(HEADS, HEAD_DIM),
        lru_lambda=dlam,
    )
    hooks.before_ffn1_bwd([dln1b])
    token = hooks.mixer_grads(dwp, dwo, small, loss)
    dx_a, *grads_a = _ffn_bwd(dy1p, xb0, g1a, u1a, w["f1g"], w["f1u"], f1d, token, name="ffn1_bwd_a", part=0)
    token = hooks.ffn1_grads(grads_a)
    dx, *grads_b = _ffn_bwd(dy1p, xb0, g1a, u1a, w["f1g"], w["f1u"], f1d, token, name="ffn1_bwd_b", part=1,
                            dx_init=dx_a)

    grads = dict(f1=(grads_a, grads_b), f2g=df2g, f2u=df2u, f2d=df2d, wp=dwp, wo=dwo, **small)
    return loss, dx, grads


MESH = pl.DeviceIdType.MESH
HBM_SPEC = pl.BlockSpec(memory_space=pl.ANY)
VMEM_SPEC = pl.BlockSpec(memory_space=pltpu.VMEM)


def _position():
    return lax.axis_index("x"), lax.axis_index("y"), lax.axis_index("c")


def _other_chips(x, y):
    return [(1 - x, y), (x, 1 - y), (1 - x, 1 - y)]


def _all_gather_bf16(shards, *, name):
    n = len(shards)

    def body(*refs):
        ins, outs, stages = refs[:n], refs[n:2 * n], refs[2 * n:3 * n]
        send_sems, recv_sems, local_sems = refs[3 * n:]
        x, y, c = _position()
        me, sibling = (x, y, c), (x, y, 1 - c)
        chips = _other_chips(x, y)

        def rows(k, px, py, pc):
            r = shards[k].shape[0]
            m = r // 2
            return outs[k].at[pl.ds(pl.multiple_of((2 * px + py) * r + pc * m, 16), m), :]

        def copy(k, idx, block, to, src=None):
            return pltpu.make_async_remote_copy(
                src_ref=rows(k, *block) if src is None else src, dst_ref=rows(k, *block),
                send_sem=send_sems.at[7 * k + idx], recv_sem=recv_sems.at[7 * k + idx],
                device_id=to, device_id_type=MESH)

        started = []
        mine = []
        for k in range(n):
            m = shards[k].shape[0] // 2
            stages[k][...] = ins[k][pl.ds(pl.multiple_of(c * m, 16), m), :].astype(stages[k].dtype)
            cp = pltpu.make_async_copy(stages[k], rows(k, *me), local_sems.at[k])
            cp.start()
            mine.append(cp)
            first = [copy(k, 0, me, sibling, src=stages[k])]
            first += [copy(k, 1 + j, me, (*chip, c), src=stages[k]) for j, chip in enumerate(chips)]
            for cp in first:
                cp.start()
            started += first
        for k in range(n):
            for j, chip in enumerate(chips):
                copy(k, 1 + j, (*chip, c), me).wait_recv()
                fwd = copy(k, 4 + j, (*chip, c), sibling)
                fwd.start()
                started.append(fwd)
        for k in range(n):
            copy(k, 0, sibling, me).wait_recv()
            for j, chip in enumerate(chips):
                copy(k, 4 + j, (*chip, 1 - c), me).wait_recv()
        for cp in started:
            cp.wait_send()
        for cp in mine:
            cp.wait()

    return _pcall(
        body, name=name,
        in_specs=[VMEM_SPEC] * n, out_specs=[HBM_SPEC] * n,
        out_shape=[jax.ShapeDtypeStruct((N_SHARD * s.shape[0], s.shape[1]), MXU_DTYPE) for s in shards],
        scratch_shapes=[pltpu.VMEM((s.shape[0] // 2, s.shape[1]), MXU_DTYPE) for s in shards]
        + [pltpu.SemaphoreType.DMA((7 * n,)), pltpu.SemaphoreType.DMA((7 * n,)), pltpu.SemaphoreType.DMA((n,))],
        compiler_params=pltpu.CompilerParams(vmem_limit_bytes=VMEM_LIMIT),
    )(*shards)


def _swap_halves(gs, *, name):
    n = len(gs)

    def body(*refs):
        ins, outs = refs[:n], refs[n:2 * n]
        send_sems, recv_sems = refs[2 * n:]
        x, y, c = _position()
        cps = []
        for k in range(n):
            m = gs[k].shape[1] // 2
            src = ins[k].at[:, pl.ds(pl.multiple_of((1 - c) * m, 16), m), :]
            cp = pltpu.make_async_remote_copy(src_ref=src, dst_ref=outs[k], send_sem=send_sems.at[k],
                                              recv_sem=recv_sems.at[k], device_id=(x, y, 1 - c), device_id_type=MESH)
            cp.start()
            cps.append(cp)
        for cp in cps:
            cp.wait()

    return _pcall(
        body, name=name, in_specs=[HBM_SPEC] * n, out_specs=[HBM_SPEC] * n,
        out_shape=[jax.ShapeDtypeStruct((g.shape[0], g.shape[1] // 2, g.shape[2]), g.dtype) for g in gs],
        scratch_shapes=[pltpu.SemaphoreType.DMA((n,)), pltpu.SemaphoreType.DMA((n,))],
    )(*gs)


def _add_halves(gs, recvs, *, name, tm=256):
    n = len(gs)
    _, r, cdim = gs[0].shape
    m = r // 2
    tm = min(tm, m)
    nb = m // tm
    c_idx = lax.axis_index("c").astype(jnp.int32).reshape(1)

    def body(c_ref, *refs):
        for k in range(n):
            refs[2 * n + k][...] = (refs[k][...].astype(f32) + refs[n + k][...].astype(f32)).astype(refs[2 * n + k].dtype)

    mine = pl.BlockSpec((None, tm, cdim), lambda j, i, c_ref: (j, c_ref[0] * nb + i, 0))
    half = pl.BlockSpec((None, tm, cdim), lambda j, i, c_ref: (j, i, 0))
    return _pcall(
        body, name=name,
        grid_spec=pltpu.PrefetchScalarGridSpec(
            num_scalar_prefetch=1, grid=(N_SHARD, nb),
            in_specs=[mine] * n + [half] * n, out_specs=[half] * n),
        out_shape=[jax.ShapeDtypeStruct((N_SHARD, m, cdim), g.dtype) for g in gs],
        compiler_params=_params(2),
    )(c_idx, *gs, *recvs)


def _scatter_partials(ps, *, name):
    n = len(ps)

    def body(*refs):
        ins, outs = refs[:n], refs[n:2 * n]
        send_sems, recv_sems = refs[2 * n:]
        x, y, c = _position()
        me_chip = 2 * x + y
        cps = []
        for k in range(n):
            for j, (px, py) in enumerate(_other_chips(x, y)):
                cp = pltpu.make_async_remote_copy(
                    src_ref=ins[k].at[2 * px + py], dst_ref=outs[k].at[me_chip],
                    send_sem=send_sems.at[3 * k + j], recv_sem=recv_sems.at[3 * k + j],
                    device_id=(px, py, c), device_id_type=MESH)
                cp.start()
                cps.append(cp)
        for cp in cps:
            cp.wait()

    return _pcall(
        body, name=name, in_specs=[HBM_SPEC] * n, out_specs=[HBM_SPEC] * n,
        out_shape=[jax.ShapeDtypeStruct(p.shape, p.dtype) for p in ps],
        scratch_shapes=[pltpu.SemaphoreType.DMA((3 * n,)), pltpu.SemaphoreType.DMA((3 * n,))],
    )(*ps)


def _sum_slabs(ps, qs, *, name, tm=128):
    n = len(qs)
    _, m, cdim = qs[0].shape
    tm = min(tm, m)
    nb = m // tm
    assert m % tm == 0, (m, tm)
    where = jnp.stack([2 * lax.axis_index("x") + lax.axis_index("y"), lax.axis_index("c")]).astype(jnp.int32)

    def body(w_ref, *refs):
        for k in range(n):
            own, q1, q2, q3 = (refs[4 * k + t][...].astype(f32) for t in range(4))
            refs[4 * n + k][...] = ((own + q1) + q2) + q3

    def slab(flip):
        return pl.BlockSpec((None, tm, cdim), lambda i, w_ref: (jnp.bitwise_xor(w_ref[0], flip), i, 0))

    operands = []
    for p, q in zip(ps, qs):
        operands += [p, q, q, q]
    return _pcall(
        body, name=name,
        grid_spec=pltpu.PrefetchScalarGridSpec(
            num_scalar_prefetch=1, grid=(nb,),
            in_specs=[slab(0), slab(2), slab(1), slab(3)] * n,
            out_specs=[pl.BlockSpec((tm, cdim), lambda i, w_ref: (w_ref[1] * nb + i, 0))] * n),
        out_shape=[jax.ShapeDtypeStruct((2 * m, cdim), f32) for _ in qs],
        compiler_params=_params(1),
    )(where, *operands)


def _join_halves(fs, *, name):
    n = len(fs)

    def body(*refs):
        outs = refs[n:2 * n]
        send_sems, recv_sems = refs[2 * n:]
        x, y, c = _position()
        cps = []
        for k in range(n):
            m = fs[k].shape[0] // 2
            half = outs[k].at[pl.ds(pl.multiple_of(c * m, 8), m), :]
            cp = pltpu.make_async_remote_copy(src_ref=half, dst_ref=half, send_sem=send_sems.at[k],
                                              recv_sem=recv_sems.at[k], device_id=(x, y, 1 - c), device_id_type=MESH)
            cp.start()
            cps.append(cp)
        for cp in cps:
            cp.wait()

    return _pcall(
        body, name=name, in_specs=[HBM_SPEC] * n, out_specs=[HBM_SPEC] * n,
        out_shape=[jax.ShapeDtypeStruct(f.shape, f.dtype) for f in fs],
        input_output_aliases={k: k for k in range(n)},
        scratch_shapes=[pltpu.SemaphoreType.DMA((n,)), pltpu.SemaphoreType.DMA((n,))],
    )(*fs)


def _all_reduce_small(v, after=None, *, name):
    r = v.shape[0]
    extra = [] if after is None else [after]

    def body(v_ref, *refs):
        out_ref, buf, send_sems, recv_sems, local_sem = refs[len(extra):]
        x, y, c = _position()
        me, sibling = (x, y, c), (x, y, 1 - c)
        chips = _other_chips(x, y)

        def rows(px, py, pc):
            return buf.at[pl.ds(pl.multiple_of((4 * px + 2 * py + pc) * r, 8), r), :]

        def copy(k, block, to, src=None):
            return pltpu.make_async_remote_copy(
                src_ref=rows(*block) if src is None else src, dst_ref=rows(*block),
                send_sem=send_sems.at[k], recv_sem=recv_sems.at[k], device_id=to, device_id_type=MESH)

        mine = pltpu.make_async_copy(v_ref, rows(*me), local_sem)
        mine.start()
        first = [copy(0, me, sibling, src=v_ref)]
        first += [copy(1 + j, me, (*chip, c), src=v_ref) for j, chip in enumerate(chips)]
        for cp in first:
            cp.start()
        passed = [copy(4 + j, (*chip, c), sibling) for j, chip in enumerate(chips)]
        for j, chip in enumerate(chips):
            copy(1 + j, (*chip, c), me).wait_recv()
            passed[j].start()
        copy(0, sibling, me).wait_recv()
        for j, chip in enumerate(chips):
            copy(4 + j, (*chip, 1 - c), me).wait_recv()
        for cp in first + passed:
            cp.wait_send()
        mine.wait()
        acc = buf[0:r, :]
        for d in range(1, N_DEV):
            acc = acc + buf[d * r:(d + 1) * r, :]
        out_ref[...] = acc

    return _pcall(
        body, name=name, in_specs=[VMEM_SPEC] + [HBM_SPEC] * len(extra), out_specs=VMEM_SPEC,
        out_shape=jax.ShapeDtypeStruct((r, LANES), f32),
        scratch_shapes=[pltpu.VMEM((N_DEV * r, LANES), f32), pltpu.SemaphoreType.DMA((7,)),
                        pltpu.SemaphoreType.DMA((7,)), pltpu.SemaphoreType.DMA],
    )(v, *extra)


SEM_SPEC = pl.BlockSpec(memory_space=pltpu.SEMAPHORE)
HBM_ONLY = pl.BlockSpec(memory_space=pltpu.HBM)
EFFECT = pltpu.SideEffectType.DATAFLOW_SIDE_EFFECTING


def _sends(copies):
    return copies[0] if isinstance(copies, tuple) else copies


def _arrivals(copies):
    return copies[1] if isinstance(copies, tuple) else copies


def _split_start(bufs, copies_fn, n_sems, *, name):
    n = len(bufs)

    def body(*refs):
        send_sems, recv_sems = refs[n], refs[n + 1]
        thru = refs[n + 2:2 * n + 2]
        token = refs[2 * n + 2]
        for cp in _sends(copies_fn(thru, send_sems, recv_sems)):
            cp.start()
        token[...] = jnp.zeros_like(token)

    outs = _pcall(
        body, name=name,
        out_shape=(pltpu.SemaphoreType.DMA((n_sems,)), pltpu.SemaphoreType.DMA((n_sems,)),
                   *[pltpu.HBM(b.shape, b.dtype) for b in bufs], jax.ShapeDtypeStruct((8, LANES), f32)),
        in_specs=[HBM_ONLY] * n,
        out_specs=(SEM_SPEC, SEM_SPEC, *[HBM_ONLY] * n, VMEM_SPEC),
        input_output_aliases={k: 2 + k for k in range(n)},
        compiler_params=pltpu.CompilerParams(has_side_effects=EFFECT),
    )(*[pltpu.with_memory_space_constraint(b, pltpu.HBM) for b in bufs])
    return outs[0], outs[1], list(outs[2:2 + n]), outs[2 + n]


def _split_wait(thru, send_sems, recv_sems, after, copies_fn, *, name):
    n = len(thru)

    def body(*refs):
        copies = copies_fn(refs[:n], refs[n], refs[n + 1])
        for cp in _sends(copies):
            cp.wait_send()
        for cp in _arrivals(copies):
            cp.wait_recv()

    return list(_pcall(
        body, name=name,
        out_shape=tuple(pltpu.HBM(b.shape, b.dtype) for b in thru),
        in_specs=[HBM_ONLY] * n + [SEM_SPEC, SEM_SPEC] + [HBM_SPEC] * len(after),
        out_specs=tuple([HBM_ONLY] * n),
        input_output_aliases={k: k for k in range(n)},
        compiler_params=pltpu.CompilerParams(has_side_effects=EFFECT),
    )(*thru, send_sems, recv_sems, *after))


def _scatter_copies(n):
    def copies(bufs, send_sems, recv_sems):
        x, y, c = _position()
        me_chip = 2 * x + y
        cps = []
        for k in range(n):
            for j, (px, py) in enumerate(_other_chips(x, y)):
                cps.append(pltpu.make_async_remote_copy(
                    src_ref=bufs[k].at[2 * px + py], dst_ref=bufs[n + k].at[me_chip],
                    send_sem=send_sems.at[3 * k + j], recv_sem=recv_sems.at[3 * k + j],
                    device_id=(px, py, c), device_id_type=MESH))
        return cps
    return copies


N_PEERS = N_DEV - 1


def _direct_copies(n):
    def copies(bufs, send_sems, recv_sems):
        x, y, c = _position()
        me_chip = 2 * x + y
        sends, arrivals = [], []
        for k in range(n):
            m = bufs[k].shape[1] // 2
            land = bufs[n + k]

            def rows(slab, half, k=k, m=m):
                start = half * m if isinstance(half, int) else pl.multiple_of(half * m, 16)
                return bufs[k].at[slab, pl.ds(start, m), :]

            def copy(src, slot, send_idx, recv_idx, to, k=k, land=land):
                return pltpu.make_async_remote_copy(
                    src_ref=src, dst_ref=land.at[slot], send_sem=send_sems.at[N_PEERS * k + send_idx],
                    recv_sem=recv_sems.at[N_PEERS * k + recv_idx], device_id=to, device_id_type=MESH)

            sends.append(copy(rows(me_chip, 1 - c), 0, 0, 0, (x, y, 1 - c)))
            arrivals.append(copy(rows(me_chip, c), 0, 0, 0, (x, y, 1 - c)))
            for t, (px, py) in enumerate(_other_chips(x, y)):
                for core in range(2):
                    sends.append(copy(rows(2 * px + py, core), 1 + 2 * t + c, 1 + 2 * t + core, 1 + 2 * t + c,
                                      (px, py, core)))
                    arrivals.append(copy(rows(me_chip, c), 1 + 2 * t + core, 1 + 2 * t + core, 1 + 2 * t + core,
                                         (px, py, core)))
        return sends, arrivals
    return copies


def _sum_direct(gs, lands, *, name, tm=128):
    n = len(gs)
    _, m, cdim = lands[0].shape
    tm = min(tm, m)
    nb = m // tm
    assert m % tm == 0, (m, tm)
    where = jnp.stack([2 * lax.axis_index("x") + lax.axis_index("y"), lax.axis_index("c")]).astype(jnp.int32)

    def body(w_ref, *refs):
        for k in range(n):
            acc = refs[2 * k][...].astype(f32)
            for slot in range(N_PEERS):
                acc = acc + refs[2 * k + 1][slot].astype(f32)
            refs[2 * n + k][...] = acc

    own = pl.BlockSpec((None, tm, cdim), lambda i, w_ref: (w_ref[0], w_ref[1] * nb + i, 0))
    landed = pl.BlockSpec((N_PEERS, tm, cdim), lambda i, w_ref: (0, i, 0))
    operands = []
    for g, land in zip(gs, lands):
        operands += [g, land]
    return _pcall(
        body, name=name,
        grid_spec=pltpu.PrefetchScalarGridSpec(
            num_scalar_prefetch=1, grid=(nb,), in_specs=[own, landed] * n,
            out_specs=[pl.BlockSpec((tm, cdim), lambda i, w_ref: (w_ref[1] * nb + i, 0))] * n),
        out_shape=[jax.ShapeDtypeStruct((2 * m, cdim), f32) for _ in gs],
        compiler_params=_params(1),
    )(where, *operands)


def _broadcast_copies(bufs, send_sems, recv_sems):
    v, land = bufs
    x, y, c = _position()

    def copy(slot, send_idx, recv_idx, to):
        return pltpu.make_async_remote_copy(src_ref=v, dst_ref=land.at[slot], send_sem=send_sems.at[send_idx],
                                            recv_sem=recv_sems.at[recv_idx], device_id=to, device_id_type=MESH)

    sends = [copy(0, 0, 0, (x, y, 1 - c))]
    arrivals = [copy(0, 0, 0, (x, y, 1 - c))]
    for t, (px, py) in enumerate(_other_chips(x, y)):
        for core in range(2):
            sends.append(copy(1 + 2 * t + c, 1 + 2 * t + core, 1 + 2 * t + c, (px, py, core)))
            arrivals.append(copy(1 + 2 * t + core, 1 + 2 * t + core, 1 + 2 * t + core, (px, py, core)))
    return sends, arrivals


def _sum_in_device_order(v, land, *, name):
    r, cdim = v.shape
    x, y, c = _position()
    slots, mine = [], []
    for d in range(N_DEV):
        dx, dy, dc = d // 4, (d // 2) % 2, d % 2
        fx, fy = jnp.bitwise_xor(dx, x), jnp.bitwise_xor(dy, y)
        t = jnp.where(fx == 1, jnp.where(fy == 1, 2, 0), 1)
        slots.append(jnp.where(jnp.logical_and(fx == 0, fy == 0), 0, 1 + 2 * t + dc))
        mine.append(jnp.logical_and(jnp.logical_and(fx == 0, fy == 0), dc == c))
    table = jnp.stack(slots + mine).astype(jnp.int32)

    def body(tab_ref, v_ref, *refs):
        out_ref = refs[N_DEV]
        acc = None
        for d in range(N_DEV):
            term = jnp.where(tab_ref[N_DEV + d] == 1, v_ref[...], refs[d][...])
            acc = term if acc is None else acc + term
        out_ref[...] = acc

    whole = pl.BlockSpec((r, cdim), lambda i, tab_ref: (0, 0))
    landed = [pl.BlockSpec((None, r, cdim), functools.partial(lambda i, tab_ref, d: (tab_ref[d], 0, 0), d=d))
              for d in range(N_DEV)]
    return _pcall(
        body, name=name,
        grid_spec=pltpu.PrefetchScalarGridSpec(num_scalar_prefetch=1, grid=(1,), in_specs=[whole] + landed,
                                               out_specs=whole),
        out_shape=jax.ShapeDtypeStruct((r, cdim), f32),
        compiler_params=_params(1),
    )(table, v, *[land] * N_DEV)


def _block_rows(buf, px, py, pc):
    m = buf.shape[0] // N_DEV
    return buf.at[pl.ds(pl.multiple_of((4 * px + 2 * py + pc) * m, 16), m), :]


def _gather_ici_copies(n):
    def copies(bufs, send_sems, recv_sems):
        x, y, c = _position()
        cps = []
        for k in range(n):
            rows = _block_rows(bufs[k], x, y, c)
            targets = [(x, y, 1 - c)] + [(px, py, c) for px, py in _other_chips(x, y)]
            for j, to in enumerate(targets):
                cps.append(pltpu.make_async_remote_copy(
                    src_ref=rows, dst_ref=rows, send_sem=send_sems.at[4 * k + j], recv_sem=recv_sems.at[4 * k + j],
                    device_id=to, device_id_type=MESH))
        return cps
    return copies


def _gather_d2d_copies(n):
    def copies(bufs, send_sems, recv_sems):
        x, y, c = _position()
        cps = []
        for k in range(n):
            for j, (px, py) in enumerate(_other_chips(x, y)):
                rows = _block_rows(bufs[k], px, py, c)
                cps.append(pltpu.make_async_remote_copy(
                    src_ref=rows, dst_ref=rows, send_sem=send_sems.at[3 * k + j], recv_sem=recv_sems.at[3 * k + j],
                    device_id=(x, y, 1 - c), device_id_type=MESH))
        return cps
    return copies


def _cast_halves(shards, after, *, name):
    n = len(shards)
    where = jnp.stack([2 * lax.axis_index("x") + lax.axis_index("y"), lax.axis_index("c")]).astype(jnp.int32)

    def body(w_ref, *refs):
        for k in range(n):
            refs[n + 1 + k][...] = refs[k][...].astype(refs[n + 1 + k].dtype)

    def half(s):
        return (s.shape[0] // 2, s.shape[1])

    return _pcall(
        body, name=name,
        grid_spec=pltpu.PrefetchScalarGridSpec(
            num_scalar_prefetch=1, grid=(1,),
            in_specs=[pl.BlockSpec(half(s), lambda i, w_ref: (w_ref[1], 0)) for s in shards] + [HBM_SPEC],
            out_specs=[pl.BlockSpec(half(s), lambda i, w_ref: (2 * w_ref[0] + w_ref[1], 0)) for s in shards]),
        out_shape=[jax.ShapeDtypeStruct((N_SHARD * s.shape[0], s.shape[1]), MXU_DTYPE) for s in shards],
        compiler_params=_params(1),
    )(where, *shards, after)


class _SplitGather:
    def __init__(self, shards, after, tag):
        self.tag = tag
        self.n = len(shards)
        halves = _cast_halves(shards, after, name=f"{tag}_cast")
        self.ici = _split_start(halves, _gather_ici_copies(self.n), 4 * self.n, name=f"{tag}_ici_start")
        self.token = self.ici[3]

    def forward(self, after):
        send_sems, recv_sems, thru, _ = self.ici
        landed = _split_wait(thru, send_sems, recv_sems, after, _gather_ici_copies(self.n), name=f"{self.tag}_ici_wait")
        self.d2d = _split_start(landed, _gather_d2d_copies(self.n), 3 * self.n, name=f"{self.tag}_d2d_start")
        return self.d2d[3]

    def finish(self, after):
        send_sems, recv_sems, thru, _ = self.d2d
        return _split_wait(thru, send_sems, recv_sems, after, _gather_d2d_copies(self.n), name=f"{self.tag}_d2d_wait")


class _Overlap(_NoOverlap):
    def __init__(self, late_shards, ffn2_shards, after):
        self.late = _SplitGather(late_shards, after, "ag1")
        self.ffn2 = _SplitGather(ffn2_shards, self.late.token, "ag2")
        self.reduced = None
        self.ffn1_parts = []

    def start_token(self):
        return self.ffn2.token

    def late_weights(self, w, after):
        token = self.late.forward(after)
        f1d, w_in, wo = self.late.finish([token])
        w_in = w_in.reshape(N_SHARD, D_MODEL, IN_SHARD)
        w_in = jnp.concatenate([w_in[j] for j in range(N_SHARD)], axis=1)
        return dict(f1d=f1d.reshape(N_SHARD, D_FF // N_SHARD, D_MODEL), wp=make_wp(w_in), wo=wo)

    def after_attention(self, after):
        return self.ffn2.forward(after)

    def ffn2_weights(self, w, after):
        full = self.ffn2.finish(after)
        fs = D_FF // N_SHARD
        return (full[0].reshape(N_SHARD, D_MODEL, fs), full[1].reshape(N_SHARD, D_MODEL, fs),
                full[2].reshape(N_SHARD, fs, D_MODEL))

    @staticmethod
    def _send_direct(grads, tag):
        lands = [lax.empty((N_PEERS, g.shape[1] // 2, g.shape[2]), g.dtype) for g in grads]
        return _split_start(list(grads) + lands, _direct_copies(len(grads)), N_PEERS * len(grads),
                            name=f"rs_direct_{tag}_start")

    def ffn2_grads(self, grads):
        self.scatter = self._send_direct(grads, "ffn2")
        return self.scatter[3]

    def ffn1_grads(self, grads):
        tag = "ffn1" + "ab"[len(self.ffn1_parts)]
        if not self.ffn1_parts:
            started = self._send_direct(grads, tag)
        else:
            recvs = _swap_halves(grads, name=f"rs_swap_{tag}")
            ps = list(_add_halves(grads[:2], recvs[:2], name=f"rs_add_{tag}_gu"))
            ps += list(_add_halves(grads[2:], recvs[2:], name=f"rs_add_{tag}_d"))
            lands = [lax.empty(p.shape, p.dtype) for p in ps]
            started = _split_start(ps + lands, _scatter_copies(3), 9, name=f"rs_scatter_{tag}_start")
        self.ffn1_parts.append((tag, started))
        return started[3]

    def ffn1_reduced(self, after):
        sums = []
        for direct, (tag, (send_sems, recv_sems, thru, _)) in zip((True, False), self.ffn1_parts):
            plan, add = (_direct_copies, _sum_direct) if direct else (_scatter_copies, _sum_slabs)
            done = _split_wait(thru, send_sems, recv_sems, after, plan(3), name=f"rs_{tag}_wait")
            sums += list(add(done[:2], done[3:5], name=f"rs_sum_{tag}_gu"))
            sums += list(add(done[2:3], done[5:], name=f"rs_sum_{tag}_d"))
        return sums

    def mixer_grads(self, dwp, dwo, small, loss):
        packed = jnp.concatenate([_pack_small(small), jnp.broadcast_to(loss, (8, LANES))], axis=0)
        land = lax.empty((N_PEERS,) + packed.shape, packed.dtype)
        self.small = _split_start([packed, land], _broadcast_copies, N_PEERS, name="ar_small_start")
        gwin = jnp.stack([dwp[:, j * IN_SHARD:(j + 1) * IN_SHARD] for j in range(N_SHARD)]).astype(GRAD_DTYPE)
        gwo = dwo.reshape(N_SHARD, D_MODEL // N_SHARD, D_MODEL).astype(GRAD_DTYPE)
        self.scatter_mix = self._send_direct([gwin, gwo], "mix")
        return self.small[3] + self.scatter_mix[3]

    def small_summed(self, after):
        send_sems, recv_sems, thru, _ = self.small
        packed, land = _split_wait(thru, send_sems, recv_sems, after, _broadcast_copies, name="ar_small_wait")
        summed = _sum_in_device_order(packed, land, name="ar_small_sum")
        return summed[:-8], summed[-8, 0]

    def mixer_reduced(self, after):
        send_sems, recv_sems, thru, _ = self.scatter_mix
        done = _split_wait(thru, send_sems, recv_sems, after, _direct_copies(2), name="rs_direct_mix_wait")
        return [_sum_direct([done[k]], [done[2 + k]], name=f"rs_sum_{tag}")[0] for k, tag in enumerate(["w_in", "w_out"])]

    def before_ffn1_bwd(self, after):
        send_sems, recv_sems, thru, _ = self.scatter
        n = len(thru) // 2
        done = _split_wait(thru, send_sems, recv_sems, after, _direct_copies(n), name="rs_direct_ffn2_wait")
        self.reduced = list(_sum_direct(done[:n], done[n:], name="rs_sum_ffn2"))


def _adamw(gs, ws, ms, vs, *, name, tm=256):
    n = len(gs)
    r, cdim = ws[0].shape[-2:]
    tm = r if tm is None else min(tm, r)
    assert r % tm == 0, (r, tm)
    nb = r // tm
    c1 = 1.0 / (1.0 - ADAM_B1 ** ADAM_STEP)
    c2 = 1.0 / (1.0 - ADAM_B2 ** ADAM_STEP)
    flat = pl.BlockSpec((tm, cdim), lambda i: (i, 0))

    g_ops, g_specs, g_where = [], [], []
    for g in gs:
        g_where.append(len(g_ops))
        if not isinstance(g, tuple):
            g_ops.append(g)
            g_specs.append(flat)
        elif g[2] == 1:
            g_ops += [g[0], g[1]]
            g_specs += [pl.BlockSpec((tm, cdim // 2), lambda i: (i, 0))] * 2
        else:
            g_ops += [g[0], g[1]]
            g_specs += [pl.BlockSpec((tm, cdim), lambda i: (jnp.minimum(i, nb // 2 - 1), 0)),
                        pl.BlockSpec((tm, cdim), lambda i: (jnp.maximum(i - nb // 2, 0), 0))]
    ng = len(g_ops)

    def gradient(refs, k):
        g, at = gs[k], g_where[k]
        if not isinstance(g, tuple):
            return refs[at][...]
        if g[2] == 1:
            return jnp.concatenate([refs[at][...], refs[at + 1][...]], axis=1)
        return jnp.where(pl.program_id(0) < nb // 2, refs[at][...], refs[at + 1][...])

    def body(*refs):
        rest = refs[ng:]
        for k in range(n):
            g = gradient(refs, k)
            w = rest[k][...]
            m = ADAM_B1 * rest[n + k][...] + (1.0 - ADAM_B1) * g
            v = ADAM_B2 * rest[2 * n + k][...] + (1.0 - ADAM_B2) * (g * g)
            rest[3 * n + k][...] = g
            rest[4 * n + k][...] = -ADAM_LR * ((m * c1) / (jnp.sqrt(v * c2) + ADAM_EPS) + ADAM_WD * w)
            rest[5 * n + k][...] = m
            rest[6 * n + k][...] = v

    like_w = flat if ws[0].ndim == 2 else pl.BlockSpec((None, tm, cdim), lambda i: (0, i, 0))
    outs = _pcall(
        body, name=name, grid=(nb,), in_specs=g_specs + [like_w] * (3 * n), out_specs=[like_w] * (4 * n),
        out_shape=[jax.ShapeDtypeStruct(ws[0].shape, f32)] * (4 * n),
        compiler_params=_params(1),
    )(*g_ops, *ws, *ms, *vs)
    return outs[:n], outs[n:2 * n], outs[2 * n:3 * n], outs[3 * n:]


BIG = ["ffn1_w_gate", "ffn1_w_up", "ffn1_w_down", "ffn2_w_gate", "ffn2_w_up", "ffn2_w_down"]
SMALL = ["ln1_g", "ln1_b", "b_forget", "conv_w", "conv_b", "rg_wa", "rg_ba", "rg_wx", "rg_bx", "lru_lambda",
         "ln2_g", "ln2_b", "ln3_g", "ln3_b"]
WEIGHTS = ["ffn1_w_gate", "ffn1_w_up", "ffn1_w_down", "ln1_g", "ln1_b", "w_in", "b_forget", "conv_w", "conv_b",
           "rg_wa", "rg_ba", "rg_wx", "rg_bx", "lru_lambda", "w_out", "ln2_g", "ln2_b",
           "ffn2_w_gate", "ffn2_w_up", "ffn2_w_down", "ln3_g", "ln3_b"]


def _pack_small(parts):
    rows = []
    for n in SMALL:
        flat = parts[n].reshape(-1)
        pad = (-flat.shape[0]) % LANES
        rows.append(jnp.pad(flat, (0, pad)).reshape(-1, LANES))
    packed = jnp.concatenate(rows, axis=0)
    return jnp.pad(packed, ((0, (-packed.shape[0]) % 8), (0, 0)))


def _unpack_small(packed, shapes):
    out, r0 = {}, 0
    for n in SMALL:
        size = math.prod(shapes[n])
        nr = -(-size // LANES)
        out[n] = packed[r0:r0 + nr].reshape(-1)[:size].reshape(shapes[n])
        r0 += nr
    return out


def kernel(x, ffn1_w_gate, ffn1_w_up, ffn1_w_down, ln1_g, ln1_b, w_in, b_forget, conv_w, conv_b, rg_wa, rg_ba, rg_wx, rg_bx, lru_lambda, w_out, ln2_g, ln2_b, ffn2_w_gate, ffn2_w_up, ffn2_w_down, ln3_g, ln3_b, loss_target, m_ffn1_w_gate, m_ffn1_w_up, m_ffn1_w_down, m_ln1_g, m_ln1_b, m_w_in, m_b_forget, m_conv_w, m_conv_b, m_rg_wa, m_rg_ba, m_rg_wx, m_rg_bx, m_lru_lambda, m_w_out, m_ln2_g, m_ln2_b, m_ffn2_w_gate, m_ffn2_w_up, m_ffn2_w_down, m_ln3_g, m_ln3_b, v_ffn1_w_gate, v_ffn1_w_up, v_ffn1_w_down, v_ln1_g, v_ln1_b, v_w_in, v_b_forget, v_conv_w, v_conv_b, v_rg_wa, v_rg_ba, v_rg_wx, v_rg_bx, v_lru_lambda, v_w_out, v_ln2_g, v_ln2_b, v_ffn2_w_gate, v_ffn2_w_up, v_ffn2_w_down, v_ln3_g, v_ln3_b):
    args = dict(locals())
    w = {n: args[n] for n in WEIGHTS}
    mom = {n: args["m_" + n] for n in WEIGHTS}
    var = {n: args["v_" + n] for n in WEIGHTS}
    chip = 2 * lax.axis_index("x") + lax.axis_index("y")

    g1 = _all_gather_bf16([w[n][0] for n in BIG[:2]], name="ag_ffn1_up")
    fs = D_FF // N_SHARD
    full = dict(
        f1g=g1[0].reshape(N_SHARD, D_MODEL, fs), f1u=g1[1].reshape(N_SHARD, D_MODEL, fs),
        bfp=jnp.pad(b_forget, ((0, 0), (0, LANES - HEADS))),
        ln1_g=ln1_g, ln1_b=ln1_b, ln2_g=ln2_g, ln2_b=ln2_b, ln3_g=ln3_g, ln3_b=ln3_b,
        conv_b=conv_b, rg_wa=rg_wa[0], rg_wx=rg_wx[0], rg_ba=rg_ba[0], rg_bx=rg_bx[0], lam=lru_lambda,
    )
    cw_place = lax.dynamic_update_slice(jnp.zeros((8, LRU_W), f32), conv_w[0] * 0.5, (0, chip * (LRU_W // N_SHARD)))
    cw_full = _all_reduce_small(cw_place.reshape(-1, LANES), g1[0], name="ag_conv_w")
    full["conv_w"] = cw_full.reshape(8, LRU_W)[:CONV_K]

    hooks = _Overlap([w["ffn1_w_down"][0], w["w_in"][0], w["w_out"][0]], [w[n][0] for n in BIG[3:]], cw_full)
    loss_rep, dx, g = _local_step(x[0], loss_target[0], full, hooks)

    token1 = hooks.ffn1_grads(g["f1"][1])
    red = _join_halves(hooks.reduced + hooks.mixer_reduced([token1]), name="rs_join_rest")
    grads = dict(zip(BIG[3:] + ["w_in", "w_out"], red))

    small_sum, loss = hooks.small_summed(red)
    small_shapes = {n: w[n].shape for n in SMALL}
    small_shapes["conv_w"] = (1, CONV_K, LRU_W)
    gs_red = _unpack_small(small_sum, small_shapes)
    gs_red["conv_w"] = lax.dynamic_slice(gs_red["conv_w"], (0, 0, chip * (LRU_W // N_SHARD)),
                                         (1, CONV_K, LRU_W // N_SHARD))
    grads.update(gs_red)

    delta, new_m, new_v = {}, {}, {}

    def adamw(names, name, **kw):
        g3, d, nm, nv = _adamw([grads[n] for n in names], [w[n] for n in names], [mom[n] for n in names],
                               [var[n] for n in names], name=name, **kw)
        for i, n in enumerate(names):
            grads[n], delta[n], new_m[n], new_v[n] = g3[i], d[i], nm[i], nv[i]

    adamw(BIG[3:], "adamw_ffn2", tm=128)
    adamw(["w_in"], "adamw_w_in")
    adamw(["w_out"], "adamw_w_out")
    shard_shapes = {n: w[n].shape for n in SMALL}
    _, d, nm, nv = _adamw([_pack_small({n: grads[n] for n in SMALL})], [_pack_small({n: w[n] for n in SMALL})],
                          [_pack_small({n: mom[n] for n in SMALL})], [_pack_small({n: var[n] for n in SMALL})],
                          name="adamw_small", tm=None)
    for dst, packed in ((delta, d[0]), (new_m, nm[0]), (new_v, nv[0])):
        dst.update(_unpack_small(packed, shard_shapes))

    worked = [new_v["ffn2_w_down"], new_v["w_in"], new_v["w_out"], nv[0]]
    ga, ua, da, gb, ub, db = _join_halves(hooks.ffn1_reduced(worked), name="rs_join_ffn1")
    grads.update(ffn1_w_gate=(ga, gb, 1), ffn1_w_up=(ua, ub, 1), ffn1_w_down=(da, db, 0))
    adamw(BIG[:3], "adamw_ffn1", tm=128)

    def shaped(tree, n):
        return tree[n].reshape(w[n].shape)

    return (loss, dx[None], *[shaped(grads, n) for n in WEIGHTS], *[shaped(delta, n) for n in WEIGHTS],
            *[shaped(new_m, n) for n in WEIGHTS], *[shaped(new_v, n) for n in WEIGHTS])
```

```python
import functools
import math

import jax
import jax.numpy as jnp
from jax import lax
from jax.experimental import pallas as pl
from jax.experimental.pallas import tpu as pltpu

f32 = jnp.float32
MXU_DTYPE = jnp.bfloat16
GRAD_DTYPE = jnp.bfloat16

D_MODEL = 1024
D_FF = 4096
N_SHARD = 4
N_DEV = 8
FOX_W = 512
LRU_W = 512
HEADS = 8
HEAD_DIM = 64
CONV_K = 4
IN_COLS = 2568
IN_SHARD = IN_COLS // N_SHARD
QKV_W = 3 * FOX_W
Z_PAD = 2688
CAST_COLS = 384
LANES = 128
LN_EPS = 1e-5
DN_ALPHA = 2.0 ** 0.25
LRU_C = 8.0
NEG_BIG = -1e30
VMEM_LIMIT = 56 * 1024 * 1024

ADAM_LR = 0.001
ADAM_B1 = 0.9
ADAM_B2 = 0.999
ADAM_EPS = 1e-08
ADAM_WD = 0.01
ADAM_STEP = 10


def _pcall(body, **kw):
    return pl.pallas_call(body, **kw)


def _params(n_grid, vmem=VMEM_LIMIT):
    return pltpu.CompilerParams(dimension_semantics=("arbitrary",) * n_grid, vmem_limit_bytes=vmem)


def _dot(a, b):
    return jnp.dot(a, b, preferred_element_type=f32)


def _dot_nt(a, b):
    return lax.dot_general(a, b, (((1,), (1,)), ((), ())), preferred_element_type=f32)


def _dot_tn(a, b):
    return lax.dot_general(a, b, (((0,), (0,)), ((), ())), preferred_element_type=f32)


def _sigmoid(x):
    return 1.0 / (1.0 + jnp.exp(-x))


def _layer_norm_stats(y):
    mu = jnp.mean(y, axis=-1, keepdims=True)
    yc = y - mu
    var = jnp.mean(yc * yc, axis=-1, keepdims=True)
    rstd = lax.rsqrt(var + LN_EPS)
    return yc * rstd, rstd


def _ln_backward(dy, xhat, rstd, gamma):
    dxhat = dy * gamma
    m1 = jnp.mean(dxhat, axis=-1, keepdims=True)
    m2 = jnp.mean(dxhat * xhat, axis=-1, keepdims=True)
    dyp = rstd * (dxhat - m1 - xhat * m2)
    return dyp, jnp.sum(dy * xhat, axis=0, keepdims=True), jnp.sum(dy, axis=0, keepdims=True)


def _ffn_fwd_loss(x, wg, wu, wd, ln_g, ln_b, target, *, name, tm=1024, tf=512):
    T = x.shape[0]
    tm = min(tm, T)
    tr = min(256, tm)
    fs = D_FF // N_SHARD
    cpf = fs // tf
    nf = D_FF // tf
    nt = T // tm

    def body(x_ref, wg_ref, wu_ref, wd_ref, g_ref, b_ref, t_ref,
             xb_ref, gact_ref, uact_ref, dyp_ref, dgam_ref, dbeta_ref, loss_ref, acc_ref):
        i = pl.program_id(0)
        f = pl.program_id(1)

        @pl.when(jnp.logical_and(i == 0, f == 0))
        def _():
            dgam_ref[...] = jnp.zeros_like(dgam_ref)
            dbeta_ref[...] = jnp.zeros_like(dbeta_ref)
            loss_ref[...] = jnp.zeros_like(loss_ref)

        @pl.when(f == 0)
        def _():
            xb_ref[...] = x_ref[...].astype(MXU_DTYPE)
            acc_ref[...] = jnp.zeros_like(acc_ref)

        xb = xb_ref[...]
        g = _dot(xb, wg_ref[...])
        u = _dot(xb, wu_ref[...])
        h = (g * _sigmoid(g)) * u
        gact_ref[...] = g.astype(gact_ref.dtype)
        uact_ref[...] = u.astype(uact_ref.dtype)
        acc_ref[...] += _dot(h.astype(MXU_DTYPE), wd_ref[...])

        @pl.when(f == nf - 1)
        def _():
            gamma = g_ref[...]

            def rows_chunk(r, carry):
                rows = pl.ds(pl.multiple_of(r * tr, tr), tr)
                xhat, rstd = _layer_norm_stats(DN_ALPHA * x_ref[rows, :] + 0.5 * acc_ref[rows, :])
                err = xhat * gamma + b_ref[...] - t_ref[rows, :]
                sq = jnp.sum(jnp.sum(err * err, axis=0, keepdims=True), axis=1, keepdims=True)
                loss_ref[...] += jnp.broadcast_to(sq * (0.5 / D_MODEL), loss_ref.shape)
                dyp, dgam, dbeta = _ln_backward(err * (1.0 / D_MODEL), xhat, rstd, gamma)
                dyp_ref[rows, :] = dyp
                dgam_ref[...] += dgam
                dbeta_ref[...] += dbeta
                return carry

            lax.fori_loop(0, tm // tr, rows_chunk, 0)

    row = lambda i, f: (i, 0)
    const = lambda i, f: (0, 0)
    tile = pl.BlockSpec((tm, tf), lambda i, f: (i, f))
    cols = pl.BlockSpec((None, D_MODEL, tf), lambda i, f: (f // cpf, 0, f % cpf))
    last = lambda i, f: (jnp.where(f == nf - 1, i, jnp.maximum(i - 1, 0)), 0)
    return _pcall(
        body, name=name, grid=(nt, nf),
        in_specs=[pl.BlockSpec((tm, D_MODEL), row), cols, cols,
                  pl.BlockSpec((None, tf, D_MODEL), lambda i, f: (f // cpf, f % cpf, 0)),
                  pl.BlockSpec((1, D_MODEL), const), pl.BlockSpec((1, D_MODEL), const),
                  pl.BlockSpec((tm, D_MODEL), last)],
        out_specs=[pl.BlockSpec((tm, D_MODEL), row), tile, tile, pl.BlockSpec((tm, D_MODEL), row),
                   pl.BlockSpec((1, D_MODEL), const), pl.BlockSpec((1, D_MODEL), const), pl.BlockSpec((1, LANES), const)],
        out_shape=[jax.ShapeDtypeStruct((T, D_MODEL), MXU_DTYPE), jax.ShapeDtypeStruct((T, D_FF), MXU_DTYPE),
                   jax.ShapeDtypeStruct((T, D_FF), MXU_DTYPE), jax.ShapeDtypeStruct((T, D_MODEL), f32),
                   jax.ShapeDtypeStruct((1, D_MODEL), f32), jax.ShapeDtypeStruct((1, D_MODEL), f32),
                   jax.ShapeDtypeStruct((1, LANES), f32)],
        scratch_shapes=[pltpu.VMEM((tm, D_MODEL), f32)],
        compiler_params=_params(2),
    )(x, wg, wu, wd, ln_g, ln_b, target)


def _ffn_up(x, wg, wu, after=None, *, name, tm=1024, tf=512):
    T = x.shape[0]
    tm = min(tm, T)
    cpf = (D_FF // N_SHARD) // tf
    nf = D_FF // tf
    extra = [] if after is None else [after]

    def body(x_ref, wg_ref, wu_ref, *refs):
        xb_ref, gact_ref, uact_ref, hact_ref = refs[len(extra):]

        @pl.when(pl.program_id(1) == 0)
        def _():
            xb_ref[...] = x_ref[...].astype(MXU_DTYPE)

        xb = xb_ref[...]
        g = _dot(xb, wg_ref[...])
        u = _dot(xb, wu_ref[...])
        gact_ref[...] = g.astype(gact_ref.dtype)
        uact_ref[...] = u.astype(uact_ref.dtype)
        hact_ref[...] = ((g * _sigmoid(g)) * u).astype(hact_ref.dtype)

    row = lambda i, f: (i, 0)
    tile = pl.BlockSpec((tm, tf), lambda i, f: (i, f))
    cols = pl.BlockSpec((None, D_MODEL, tf), lambda i, f: (f // cpf, 0, f % cpf))
    return _pcall(
        body, name=name, grid=(T // tm, nf),
        in_specs=[pl.BlockSpec((tm, D_MODEL), row), cols, cols] + [pl.BlockSpec(memory_space=pl.ANY)] * len(extra),
        out_specs=[pl.BlockSpec((tm, D_MODEL), row), tile, tile, tile],
        out_shape=[jax.ShapeDtypeStruct((T, D_MODEL), MXU_DTYPE)] + [jax.ShapeDtypeStruct((T, D_FF), MXU_DTYPE)] * 3,
        compiler_params=_params(2),
    )(x, wg, wu, *extra)


def _ffn_up_part(x, wg, wu, chip, prev, after, *, name, tm=1024, tf=512):
    T = x.shape[0]
    tm = min(tm, T)
    cpf = (D_FF // N_SHARD) // tf
    own = prev is None
    n_shards = 1 if own else N_SHARD - 1
    rest = ([] if own else list(prev)) + ([] if after is None else [after])

    def shard(f, c_ref):
        return c_ref[0] if own else (c_ref[0] + 1 + f // cpf) % N_SHARD

    def body(c_ref, x_ref, wg_ref, wu_ref, *refs):
        outs = refs[len(rest):]
        gact_ref, uact_ref, hact_ref = outs[-3:]
        xb = x_ref[...].astype(MXU_DTYPE)
        if own:
            @pl.when(pl.program_id(1) == 0)
            def _():
                outs[0][...] = xb

        g = _dot(xb, wg_ref[...].astype(MXU_DTYPE))
        u = _dot(xb, wu_ref[...].astype(MXU_DTYPE))
        gact_ref[...] = g.astype(gact_ref.dtype)
        uact_ref[...] = u.astype(uact_ref.dtype)
        hact_ref[...] = ((g * _sigmoid(g)) * u).astype(hact_ref.dtype)

    row = pl.BlockSpec((tm, D_MODEL), lambda i, f, c_ref: (i, 0))
    tile = pl.BlockSpec((tm, tf), lambda i, f, c_ref: (i, shard(f, c_ref) * cpf + f % cpf))
    if own:
        cols = pl.BlockSpec((D_MODEL, tf), lambda i, f, c_ref: (0, f))
    else:
        cols = pl.BlockSpec((None, D_MODEL, tf), lambda i, f, c_ref: (shard(f, c_ref), 0, f % cpf))
    wide = jax.ShapeDtypeStruct((T, D_FF), MXU_DTYPE)
    return list(_pcall(
        body, name=name,
        grid_spec=pltpu.PrefetchScalarGridSpec(
            num_scalar_prefetch=1, grid=(T // tm, n_shards * cpf),
            in_specs=[row, cols, cols] + [pl.BlockSpec(memory_space=pl.ANY)] * len(rest),
            out_specs=([row] if own else []) + [tile] * 3),
        out_shape=([jax.ShapeDtypeStruct((T, D_MODEL), MXU_DTYPE)] if own else []) + [wide] * 3,
        input_output_aliases={} if own else {4 + k: k for k in range(3)},
        compiler_params=_params(2),
    )(chip, x, wg, wu, *rest))


def _ffn_down_ln(x, hact, wd, ln_g, ln_b, *, name, tm=1024):
    T = x.shape[0]
    tm = min(tm, T)
    fs = D_FF // N_SHARD
    ks = 2
    nk = N_SHARD // ks

    def body(x_ref, h_ref, wd_ref, g_ref, b_ref, xhat_ref, xn_ref, rstd_ref, acc_ref):
        k = pl.program_id(1)

        @pl.when(k == 0)
        def _():
            acc_ref[...] = jnp.zeros_like(acc_ref)

        acc_ref[...] += _dot(h_ref[...], wd_ref[...].reshape(ks * fs, D_MODEL))

        @pl.when(k == nk - 1)
        def _():
            xhat, rstd = _layer_norm_stats(DN_ALPHA * x_ref[...] + 0.5 * acc_ref[...])
            xhat_ref[...] = xhat
            xn_ref[...] = (xhat * g_ref[...] + b_ref[...]).astype(xn_ref.dtype)
            rstd_ref[...] = jnp.broadcast_to(rstd, rstd_ref.shape)

    row = lambda i, k: (i, 0)
    vec = pl.BlockSpec((1, D_MODEL), lambda i, k: (0, 0))
    return _pcall(
        body, name=name, grid=(T // tm, nk),
        in_specs=[pl.BlockSpec((tm, D_MODEL), row), pl.BlockSpec((tm, ks * fs), lambda i, k: (i, k)),
                  pl.BlockSpec((ks, fs, D_MODEL), lambda i, k: (k, 0, 0)), vec, vec],
        out_specs=[pl.BlockSpec((tm, D_MODEL), row), pl.BlockSpec((tm, D_MODEL), row), pl.BlockSpec((tm, LANES), row)],
        out_shape=[jax.ShapeDtypeStruct((T, D_MODEL), f32), jax.ShapeDtypeStruct((T, D_MODEL), MXU_DTYPE),
                   jax.ShapeDtypeStruct((T, LANES), f32)],
        scratch_shapes=[pltpu.VMEM((tm, D_MODEL), f32)],
        compiler_params=_params(2),
    )(x, hact, wd, ln_g, ln_b)


def _ffn_bwd(dyp, xb, gact, uact, wg, wu, wd, after=None, *, name, tm=512, tf=512, part=None, dx_init=None):
    T = dyp.shape[0]
    tm = min(tm, T)
    fs = D_FF // N_SHARD
    cpf = fs // tf
    nt = T // tm
    nf = D_FF // tf if part is None else N_SHARD
    wf = fs if part is None else tf
    slab = (lambda f: f // cpf) if part is None else (lambda f: f)
    chunk = (lambda f: f % cpf) if part is None else (lambda f: part)
    extra = ([] if dx_init is None else [dx_init]) + ([] if after is None else [after])

    def body(dyp_ref, xb_ref, g_ref, u_ref, wg_ref, wu_ref, wd_ref, *refs):
        dx_hbm, dwg_ref, dwu_ref, dwd_ref, dx_sc, dwg_sc, dwu_sc, dwd_sc, sem = refs[len(extra):]
        f = pl.program_id(0)
        i = pl.program_id(1)
        rows = pl.ds(pl.multiple_of(i * tm, tm), tm)
        dyp_t = dyp_ref[...]
        dy = (0.5 * dyp_t).astype(MXU_DTYPE)

        @pl.when(i == 0)
        def _():
            dwg_sc[...] = jnp.zeros_like(dwg_sc)
            dwu_sc[...] = jnp.zeros_like(dwu_sc)
            dwd_sc[...] = jnp.zeros_like(dwd_sc)

        @pl.when(f == 0)
        def _():
            dx_sc[rows, :] = DN_ALPHA * dyp_t if dx_init is None else refs[0][...]

        g = g_ref[...].astype(f32)
        u = u_ref[...].astype(f32)
        sig = _sigmoid(g)
        silu = g * sig
        dh = _dot_nt(dy, wd_ref[...])
        dg = (dh * u * (sig * (1.0 + g * (1.0 - sig)))).astype(MXU_DTYPE)
        du = (dh * silu).astype(MXU_DTYPE)
        hb = (silu * u).astype(MXU_DTYPE)
        dx_sc[rows, :] += _dot_nt(dg, wg_ref[...]) + _dot_nt(du, wu_ref[...])
        xb_t = xb_ref[...]
        dwg_sc[...] += _dot_tn(xb_t, dg)
        dwu_sc[...] += _dot_tn(xb_t, du)
        dwd_sc[...] += _dot_tn(hb, dy)

        @pl.when(i == nt - 1)
        def _():
            dwg_ref[...] = dwg_sc[...].astype(dwg_ref.dtype)
            dwu_ref[...] = dwu_sc[...].astype(dwu_ref.dtype)
            dwd_ref[...] = dwd_sc[...].astype(dwd_ref.dtype)

        @pl.when(jnp.logical_and(f == nf - 1, i == nt - 1))
        def _():
            cp = pltpu.make_async_copy(dx_sc, dx_hbm, sem)
            cp.start()
            cp.wait()

    row = lambda f, i: (i, 0)
    return _pcall(
        body, name=name, grid=(nf, nt),
        in_specs=[
            pl.BlockSpec((tm, D_MODEL), row),
            pl.BlockSpec((tm, D_MODEL), row),
            pl.BlockSpec((tm, tf), lambda f, i: (i, slab(f) * cpf + chunk(f))),
            pl.BlockSpec((tm, tf), lambda f, i: (i, slab(f) * cpf + chunk(f))),
            pl.BlockSpec((None, D_MODEL, tf), lambda f, i: (slab(f), 0, chunk(f))),
            pl.BlockSpec((None, D_MODEL, tf), lambda f, i: (slab(f), 0, chunk(f))),
            pl.BlockSpec((None, tf, D_MODEL), lambda f, i: (slab(f), chunk(f), 0)),
        ] + ([] if dx_init is None else [pl.BlockSpec((tm, D_MODEL), row)])
        + ([] if after is None else [pl.BlockSpec(memory_space=pl.ANY)]),
        out_specs=[
            pl.BlockSpec(memory_space=pl.ANY),
            pl.BlockSpec((None, D_MODEL, tf), lambda f, i: (slab(f), 0, chunk(f) if part is None else 0)),
            pl.BlockSpec((None, D_MODEL, tf), lambda f, i: (slab(f), 0, chunk(f) if part is None else 0)),
            pl.BlockSpec((None, tf, D_MODEL), lambda f, i: (slab(f), chunk(f) if part is None else 0, 0)),
        ],
        out_shape=[
            jax.ShapeDtypeStruct((T, D_MODEL), f32),
            jax.ShapeDtypeStruct((N_SHARD, D_MODEL, wf), GRAD_DTYPE),
            jax.ShapeDtypeStruct((N_SHARD, D_MODEL, wf), GRAD_DTYPE),
            jax.ShapeDtypeStruct((N_SHARD, wf, D_MODEL), GRAD_DTYPE),
        ],
        scratch_shapes=[pltpu.VMEM((T, D_MODEL), f32), pltpu.VMEM((D_MODEL, tf), f32),
                        pltpu.VMEM((D_MODEL, tf), f32), pltpu.VMEM((tf, D_MODEL), f32),
                        pltpu.SemaphoreType.DMA],
        compiler_params=_params(2),
    )(dyp, xb, gact, uact, wg, wu, wd, *extra)


def _proj_in(xn, wp, bfp, *, name, tm=512):
    T = xn.shape[0]
    tm = min(tm, T)
    nt = T // tm

    def body(x_ref, w_ref, b_ref, lxg_ref, fg_ref, qa_ref, ka_ref, va_ref, carry):
        i = pl.program_id(0)

        @pl.when(i == 0)
        def _():
            carry[...] = jnp.zeros_like(carry)

        z = _dot(x_ref[...], w_ref[...])
        lxg_ref[...] = z[:, QKV_W:QKV_W + 2 * LRU_W]
        fg = z[:, QKV_W + 2 * LRU_W:] + b_ref[...]
        fg_ref[...] = fg
        ls = jnp.minimum(fg, 0.0) - jnp.log(1.0 + jnp.exp(-jnp.abs(fg)))
        r = lax.broadcasted_iota(jnp.int32, (tm, tm), 0)
        c = lax.broadcasted_iota(jnp.int32, (tm, tm), 1)
        cum = _tri_dot(jnp.where(r >= c, 1.0, 0.0).astype(jnp.bfloat16), ls) + carry[0:1, :]
        carry[...] = jnp.broadcast_to(cum[tm - 1:tm, :], carry.shape)

        lane = lax.broadcasted_iota(jnp.int32, (tm, LANES), 1)
        low = lane < HEAD_DIM
        ones_q = jnp.where(jnp.logical_and(lane >= AUX + 3, lane < AUX + 6), 1.0, 0.0)
        ones_k = jnp.where(jnp.logical_and(lane >= AUX, lane < AUX + 3), 1.0, 0.0)
        for j in range(HEADS // 2):
            pair = [z[:, t * FOX_W + j * LANES:t * FOX_W + (j + 1) * LANES] for t in range(3)]
            for odd in range(2):
                h = 2 * j + odd
                q, k, v = [_swap_lane_halves(a) if odd else a for a in pair]
                hi, mid, lo = [a.astype(f32) for a in _split3(jnp.broadcast_to(cum[:, h:h + 1], (tm, LANES)))]
                aux_q = jnp.where(lane == AUX, hi, jnp.where(lane == AUX + 1, mid, jnp.where(lane == AUX + 2, lo, ones_q)))
                aux_k = jnp.where(lane == AUX + 3, -hi,
                                  jnp.where(lane == AUX + 4, -mid, jnp.where(lane == AUX + 5, -lo, ones_k)))
                blk = slice(h * LANES, (h + 1) * LANES)
                qa_ref[:, blk] = jnp.where(low, q, aux_q).astype(qa_ref.dtype)
                ka_ref[:, blk] = jnp.where(low, k, aux_k).astype(ka_ref.dtype)
                va_ref[:, blk] = jnp.where(low, v, 1.0).astype(va_ref.dtype)

    row = lambda i: (i, 0)
    const = lambda i: (0, 0)
    return _pcall(
        body, name=name, grid=(nt,),
        in_specs=[pl.BlockSpec((tm, D_MODEL), row), pl.BlockSpec((D_MODEL, Z_PAD), const),
                  pl.BlockSpec((1, LANES), const)],
        out_specs=[pl.BlockSpec((tm, 2 * LRU_W), row), pl.BlockSpec((tm, LANES), row)]
        + [pl.BlockSpec((tm, HEADS * LANES), row)] * 3,
        out_shape=[jax.ShapeDtypeStruct((T, 2 * LRU_W), f32), jax.ShapeDtypeStruct((T, LANES), f32)]
        + [jax.ShapeDtypeStruct((T, HEADS * LANES), MXU_DTYPE)] * 3,
        scratch_shapes=[pltpu.VMEM((8, LANES), f32)],
        compiler_params=_params(1),
    )(xn, wp, bfp)


def _proj_in_bwd(dqa, dka, dva, dlxg, fgb, xn, dyp, wp, xhat, rstd, ln_g, *, name, tm=512):
    T = xn.shape[0]
    tm = min(tm, T)
    nt = T // tm

    def body(dq_ref, dk_ref, dv_ref, dl_ref, fg_ref, x_ref, dyp_ref, w_ref, xhat_ref, rstd_ref, g_ref,
             dpre_ref, dw_hbm, dgam_ref, dbeta_ref, dbf_ref, dw_sc, dw_out, carry, sem):
        i = pl.program_id(0)

        @pl.when(i == 0)
        def _():
            dw_sc[...] = jnp.zeros_like(dw_sc)
            dgam_ref[...] = jnp.zeros_like(dgam_ref)
            dbeta_ref[...] = jnp.zeros_like(dbeta_ref)
            dbf_ref[...] = jnp.zeros_like(dbf_ref)
            carry[...] = jnp.zeros_like(carry)

        lane = lax.broadcasted_iota(jnp.int32, (tm, LANES), 1)
        dc = jnp.zeros((tm, LANES), f32)
        for h in range(HEADS):
            row_sum = dq_ref[:, h * LANES + AUX:h * LANES + AUX + 1]
            col_sum = dk_ref[:, h * LANES + AUX + 3:h * LANES + AUX + 4]
            dc = jnp.where(lane == h, jnp.broadcast_to(row_sum - col_sum, (tm, LANES)), dc)
        r = lax.broadcasted_iota(jnp.int32, (tm, tm), 0)
        c = lax.broadcasted_iota(jnp.int32, (tm, tm), 1)
        dls = _tri_dot(jnp.where(c >= r, 1.0, 0.0).astype(jnp.bfloat16), dc) + carry[0:1, :]
        carry[...] = jnp.broadcast_to(dls[0:1, :], carry.shape)
        dfg = dls * _sigmoid(-fg_ref[...])
        dbf_ref[...] += jnp.sum(dfg, axis=0, keepdims=True)

        low = _low_lanes((tm, LANES))

        def packed(ref):
            pairs = [jnp.where(low, ref[:, (2 * j) * LANES:(2 * j + 1) * LANES],
                               _swap_lane_halves(ref[:, (2 * j + 1) * LANES:(2 * j + 2) * LANES]))
                     for j in range(HEADS // 2)]
            return jnp.concatenate(pairs, axis=1).astype(MXU_DTYPE)

        dz = jnp.concatenate(
            [packed(dq_ref), packed(dk_ref), packed(dv_ref),
             dl_ref[...].astype(MXU_DTYPE), dfg.astype(MXU_DTYPE)], axis=1)
        dx = DN_ALPHA * dyp_ref[...] + _dot_nt(dz, w_ref[...])
        dpre, dgam, dbeta = _ln_backward(dx, xhat_ref[...], rstd_ref[:, 0:1], g_ref[...])
        dpre_ref[...] = dpre
        dgam_ref[...] += dgam
        dbeta_ref[...] += dbeta
        dw_sc[...] += _dot_tn(x_ref[...], dz)

        @pl.when(i == nt - 1)
        def _():
            dw_sc[:, :FOX_W] = dw_sc[:, :FOX_W] * (1.0 / math.sqrt(HEAD_DIM))
            for c0 in range(0, Z_PAD, CAST_COLS):
                dw_out[:, c0:c0 + CAST_COLS] = dw_sc[:, c0:c0 + CAST_COLS].astype(dw_out.dtype)
            cp = pltpu.make_async_copy(dw_out, dw_hbm, sem)
            cp.start()
            cp.wait()

    row = lambda i: (nt - 1 - i, 0)
    const = lambda i: (0, 0)
    return _pcall(
        body, name=name, grid=(nt,),
        in_specs=[pl.BlockSpec((tm, HEADS * LANES), row), pl.BlockSpec((tm, HEADS * LANES), row),
                  pl.BlockSpec((tm, HEADS * LANES), row),
                  pl.BlockSpec((tm, 2 * LRU_W), row), pl.BlockSpec((tm, LANES), row),
                  pl.BlockSpec((tm, D_MODEL), row), pl.BlockSpec((tm, D_MODEL), row),
                  pl.BlockSpec((D_MODEL, Z_PAD), const),
                  pl.BlockSpec((tm, D_MODEL), row), pl.BlockSpec((tm, LANES), row), pl.BlockSpec((1, D_MODEL), const)],
        out_specs=[pl.BlockSpec((tm, D_MODEL), row), pl.BlockSpec(memory_space=pl.ANY),
                   pl.BlockSpec((1, D_MODEL), const), pl.BlockSpec((1, D_MODEL), const), pl.BlockSpec((1, LANES), const)],
        out_shape=[jax.ShapeDtypeStruct((T, D_MODEL), f32), jax.ShapeDtypeStruct((D_MODEL, Z_PAD), GRAD_DTYPE),
                   jax.ShapeDtypeStruct((1, D_MODEL), f32), jax.ShapeDtypeStruct((1, D_MODEL), f32),
                   jax.ShapeDtypeStruct((1, LANES), f32)],
        scratch_shapes=[pltpu.VMEM((D_MODEL, Z_PAD), f32), pltpu.VMEM((D_MODEL, Z_PAD), GRAD_DTYPE),
                        pltpu.VMEM((8, LANES), f32), pltpu.SemaphoreType.DMA],
        compiler_params=_params(1),
    )(dqa, dka, dva, dlxg, fgb, xn, dyp, wp, xhat, rstd, ln_g)


def _split3(x):
    hi = x.astype(jnp.bfloat16)
    r1 = x - hi.astype(f32)
    mid = r1.astype(jnp.bfloat16)
    lo = (r1 - mid.astype(f32)).astype(jnp.bfloat16)
    return hi, mid, lo


def _tri_dot(tri, x):
    hi, mid, lo = _split3(x)
    return _dot(tri, hi) + _dot(tri, mid) + _dot(tri, lo)


FOX_PAD = HEADS * LANES
AUX = HEAD_DIM


def _low_lanes(shape):
    return lax.broadcasted_iota(jnp.int32, shape, 1) < HEAD_DIM


def _swap_lane_halves(x):
    return pltpu.roll(x, HEAD_DIM, 1)


def _future_keys(tq, tk):
    r = lax.broadcasted_iota(jnp.int32, (tq, tk), 0)
    c = lax.broadcasted_iota(jnp.int32, (tq, tk), 1)
    return c > r


def _causal_steps(nq, key_major):
    if key_major:
        pairs = [(qi, ki) for ki in range(nq) for qi in range(ki, nq)]
    else:
        pairs = [(qi, ki) for qi in range(nq) for ki in range(qi + 1)]
    return (jnp.asarray([p[0] for p in pairs], jnp.int32), jnp.asarray([p[1] for p in pairs], jnp.int32))


def _fox_fwd(qa, ka, va, *, name, tq=512, hps=8):
    T = qa.shape[0]
    tq = min(tq, T)
    tk = tq
    nq = T // tq
    rep = tk // LANES
    qi_tab, ki_tab = _causal_steps(nq, key_major=False)

    def body(qi_ref, ki_ref, qa_ref, ka_ref, va_ref, o_ref, lse_ref, m_sc, acc_sc):
        t = pl.program_id(1)
        qi = qi_ref[t]
        ki = ki_ref[t]

        @pl.when(ki == 0)
        def _():
            m_sc[...] = jnp.full_like(m_sc, NEG_BIG)
            acc_sc[...] = jnp.zeros_like(acc_sc)

        def tile(diagonal):
            for h in range(hps):
                blk = slice(h * LANES, (h + 1) * LANES)
                s = _dot_nt(qa_ref[:, blk], ka_ref[:, blk])
                if diagonal:
                    s = jnp.where(_future_keys(tq, tk), NEG_BIG, s)
                m_prev = m_sc[h]
                m_new = jnp.maximum(m_prev, jnp.max(s, axis=1, keepdims=True))
                p = jnp.exp(s - jnp.tile(m_new, (1, rep)))
                acc_sc[h] = jnp.exp(m_prev - m_new) * acc_sc[h] + _dot(p.astype(MXU_DTYPE), va_ref[:, blk])
                m_sc[h] = m_new

        @pl.when(ki < qi)
        def _():
            tile(False)

        @pl.when(ki == qi)
        def _():
            tile(True)
            low = _low_lanes((tq, LANES))
            outs = []
            for h in range(hps):
                acc = acc_sc[h]
                den = _swap_lane_halves(acc)
                outs.append(acc / den)
                lse_ref[h] = m_sc[h] + jnp.log(jnp.where(low, den, acc))
            for p in range(hps // 2):
                o_ref[:, p * LANES:(p + 1) * LANES] = jnp.where(low, outs[2 * p], _swap_lane_halves(outs[2 * p + 1]))

    pair = hps * LANES
    return _pcall(
        body, name=name,
        grid_spec=pltpu.PrefetchScalarGridSpec(
            num_scalar_prefetch=2, grid=(HEADS // hps, qi_tab.shape[0]),
            in_specs=[
                pl.BlockSpec((tq, pair), lambda j, t, qi_ref, ki_ref: (qi_ref[t], j)),
                pl.BlockSpec((tk, pair), lambda j, t, qi_ref, ki_ref: (ki_ref[t], j)),
                pl.BlockSpec((tk, pair), lambda j, t, qi_ref, ki_ref: (ki_ref[t], j)),
            ],
            out_specs=[pl.BlockSpec((tq, pair // 2), lambda j, t, qi_ref, ki_ref: (qi_ref[t], j)),
                       pl.BlockSpec((hps, tq, LANES), lambda j, t, qi_ref, ki_ref: (j, qi_ref[t], 0))],
            scratch_shapes=[pltpu.VMEM((hps, tq, LANES), f32)] * 2),
        out_shape=[jax.ShapeDtypeStruct((T, FOX_W), f32), jax.ShapeDtypeStruct((HEADS, T, LANES), f32)],
        compiler_params=_params(2),
    )(qi_tab, ki_tab, qa, ka, va)


def _fox_bwd(qa, ka, va, doa, lse, drep, *, name, tq=512, hps=8):
    T = qa.shape[0]
    tq = min(tq, T)
    tk = tq
    nq = T // tq
    rep = tk // LANES
    qi_tab, ki_tab = _causal_steps(nq, key_major=True)

    def body(qi_ref, ki_ref, qa_ref, ka_ref, va_ref, doa_ref, lse_ref, d_ref, dqa_ref, dka_ref, dva_ref, dk_sc, dv_sc):
        t = pl.program_id(1)
        qi = qi_ref[t]
        ki = ki_ref[t]
        rows = pl.ds(pl.multiple_of(qi * tq, tq), tq)

        @pl.when(t == 0)
        def _():
            dqa_ref[...] = jnp.zeros_like(dqa_ref)

        @pl.when(qi == ki)
        def _():
            dk_sc[...] = jnp.zeros_like(dk_sc)
            dv_sc[...] = jnp.zeros_like(dv_sc)

        def tile(diagonal):
            for h in range(hps):
                blk = slice(h * LANES, (h + 1) * LANES)
                qh, kh, doh = qa_ref[:, blk], ka_ref[:, blk], doa_ref[:, blk]
                p = jnp.exp(_dot_nt(qh, kh) - jnp.tile(lse_ref[h], (1, rep)))
                if diagonal:
                    p = jnp.where(_future_keys(tq, tk), 0.0, p)
                dp = _dot_nt(doh, va_ref[:, blk])
                ds = (p * (dp - jnp.tile(d_ref[h], (1, rep)))).astype(MXU_DTYPE)
                dv_sc[h] += _dot_tn(p.astype(MXU_DTYPE), doh)
                dk_sc[h] += _dot_tn(ds, qh)
                dqa_ref[rows, blk] += _dot(ds, kh)

        @pl.when(qi > ki)
        def _():
            tile(False)

        @pl.when(qi == ki)
        def _():
            tile(True)

        @pl.when(qi == nq - 1)
        def _():
            for h in range(hps):
                blk = slice(h * LANES, (h + 1) * LANES)
                dka_ref[:, blk] = dk_sc[h]
                dva_ref[:, blk] = dv_sc[h]

    pair = hps * LANES
    q_blk = lambda j, t, qi_ref, ki_ref: (qi_ref[t], j)
    k_blk = lambda j, t, qi_ref, ki_ref: (ki_ref[t], j)
    stat = pl.BlockSpec((hps, tq, LANES), lambda j, t, qi_ref, ki_ref: (j, qi_ref[t], 0))
    return _pcall(
        body, name=name,
        grid_spec=pltpu.PrefetchScalarGridSpec(
            num_scalar_prefetch=2, grid=(HEADS // hps, qi_tab.shape[0]),
            in_specs=[pl.BlockSpec((tq, pair), q_blk), pl.BlockSpec((tk, pair), k_blk), pl.BlockSpec((tk, pair), k_blk),
                      pl.BlockSpec((tq, pair), q_blk), stat, stat],
            out_specs=[pl.BlockSpec((T, pair), lambda j, t, qi_ref, ki_ref: (0, j)),
                       pl.BlockSpec((tk, pair), k_blk), pl.BlockSpec((tk, pair), k_blk)],
            scratch_shapes=[pltpu.VMEM((hps, tk, LANES), f32)] * 2),
        out_shape=[jax.ShapeDtypeStruct((T, FOX_PAD), f32)] * 3,
        compiler_params=_params(2),
    )(qi_tab, ki_tab, qa, ka, va, doa, lse, drep)


GELU_C = math.sqrt(2.0 / math.pi)
GELU_A = 0.044715


def _gelu(x):
    t = jnp.tanh(GELU_C * (x + GELU_A * x * x * x))
    return 0.5 * x * (1.0 + t), t


def _gelu_grad(x, t):
    return 0.5 * (1.0 + t) + 0.5 * x * (1.0 - t * t) * GELU_C * (1.0 + 3.0 * GELU_A * x * x)


EXPM1_SERIES_BELOW = 0.25


def _expm1(x, e):
    series = x * (1.0 + x * (1 / 2 + x * (1 / 6 + x * (1 / 24 + x * (1 / 120 + x * (1 / 720))))))
    return jnp.where(x > -EXPM1_SERIES_BELOW, series, e - 1.0)


def _softplus_neg(lam):
    return jnp.maximum(-lam, 0.0) + jnp.log(1.0 + jnp.exp(-jnp.abs(lam)))


def _lru_gates(u, wab_ref, bab_ref, lam_ref):
    pre = _dot(u.astype(MXU_DTYPE), wab_ref[...]) + bab_ref[...]
    r = _sigmoid(pre[:, :LRU_W])
    gi = _sigmoid(pre[:, LRU_W:])
    sp = _softplus_neg(lam_ref[...])
    log_a = -LRU_C * r * sp
    a = jnp.exp(log_a)
    s = jnp.sqrt(-_expm1(2.0 * log_a, a * a))
    return r, gi, sp, a, s


def _lru_fwd(lxg, conv_w, conv_b, wab, bab, lam, *, name, tc=512):
    T = lxg.shape[0]
    tc = min(tc, T)
    nc = T // tc

    def body(lx_ref, lg_ref, cw_ref, cb_ref, wab_ref, bab_ref, lam_ref,
             out_ref, u_ref, hs_ref, gates_ref, ext, a_sc, b_sc, h_sc):
        i = pl.program_id(0)

        @pl.when(i == 0)
        def _():
            ext[0:8, :] = jnp.zeros((8, LRU_W), f32)
            h_sc[...] = jnp.zeros_like(h_sc)

        ext[8:, :] = lx_ref[...]
        u = cb_ref[...] + cw_ref[0:1, :] * ext[pl.ds(5, tc), :]
        for k in range(1, CONV_K):
            u = u + cw_ref[k:k + 1, :] * ext[pl.ds(5 + k, tc), :]
        ext[0:8, :] = ext[tc:tc + 8, :]
        u_ref[...] = u
        r, gi, sp, a, s = _lru_gates(u, wab_ref, bab_ref, lam_ref)
        for n, gate in enumerate((r, gi, a, s)):
            gates_ref[:, n * LRU_W:(n + 1) * LRU_W] = gate
        a_sc[...] = a
        b_sc[...] = s * (gi * u)

        def step(t, h):
            h = a_sc[pl.ds(t, 1), :] * h + b_sc[pl.ds(t, 1), :]
            hs_ref[pl.ds(t, 1), :] = h
            return h

        h = lax.fori_loop(0, tc, step, h_sc[0:1, :], unroll=8)
        h_sc[...] = jnp.broadcast_to(h, h_sc.shape)
        gel, _ = _gelu(lg_ref[...])
        out_ref[...] = gel * hs_ref[...]

    row = lambda i: (i, 0)
    const = lambda i: (0, 0)
    return _pcall(
        body, name=name, grid=(nc,),
        in_specs=[pl.BlockSpec((tc, LRU_W), row), pl.BlockSpec((tc, LRU_W), lambda i: (i, 1)),
                  pl.BlockSpec((CONV_K, LRU_W), const), pl.BlockSpec((1, LRU_W), const),
                  pl.BlockSpec((LRU_W, 2 * LRU_W), const), pl.BlockSpec((1, 2 * LRU_W), const),
                  pl.BlockSpec((1, LRU_W), const)],
        out_specs=[pl.BlockSpec((tc, LRU_W), row)] * 3 + [pl.BlockSpec((tc, 4 * LRU_W), row)],
        out_shape=[jax.ShapeDtypeStruct((T, LRU_W), f32)] * 3 + [jax.ShapeDtypeStruct((T, 4 * LRU_W), f32)],
        scratch_shapes=[pltpu.VMEM((tc + 8, LRU_W), f32), pltpu.VMEM((tc, LRU_W), f32),
                        pltpu.VMEM((tc, LRU_W), f32), pltpu.VMEM((8, LRU_W), f32)],
        compiler_params=_params(1),
    )(lxg, lxg, conv_w, conv_b, wab, bab, lam)


def _lru_bwd(dlru, lxg, u, hs, gates, conv_w, wab, lam, *, name, tc=512):
    T = lxg.shape[0]
    tc = min(tc, T)
    nc = T // tc
    bp = tc // 8

    def body(dl_ref, lx_ref, lxp_ref, lg_ref, u_ref, hs_ref, hsp_ref, gates_ref, cw_ref, wab_ref, lam_ref,
             dlxg_ref, dwab_ref, dbab_ref, dcw_ref, dcb_ref, dlam_ref,
             dh_sc, a_sc, ext, du_ext, carry):
        i = pl.program_id(0)
        first_chunk = i == nc - 1

        @pl.when(i == 0)
        def _():
            dwab_ref[...] = jnp.zeros_like(dwab_ref)
            dbab_ref[...] = jnp.zeros_like(dbab_ref)
            dcw_ref[...] = jnp.zeros_like(dcw_ref)
            dcb_ref[...] = jnp.zeros_like(dcb_ref)
            dlam_ref[...] = jnp.zeros_like(dlam_ref)
            carry[...] = jnp.zeros_like(carry)
            du_ext[tc:tc + 8, :] = jnp.zeros((8, LRU_W), f32)

        lg = lg_ref[...]
        gel, th = _gelu(lg)
        dl = dl_ref[...]
        hs = hs_ref[...]
        dlg = dl * hs * _gelu_grad(lg, th)
        u = u_ref[...]
        r, gi, a, s = [gates_ref[:, n * LRU_W:(n + 1) * LRU_W] for n in range(4)]
        sp = _softplus_neg(lam_ref[...])
        a_sc[...] = a
        dh_sc[...] = dl * gel

        def step(k, c):
            t = tc - 1 - k
            dh = dh_sc[pl.ds(t, 1), :] + c
            dh_sc[pl.ds(t, 1), :] = dh
            return a_sc[pl.ds(t, 1), :] * dh

        c = lax.fori_loop(0, tc, step, carry[0:1, :], unroll=8)
        carry[...] = jnp.broadcast_to(c, carry.shape)

        ext[0:8, :] = jnp.where(first_chunk, 0.0, hsp_ref[...])
        ext[8:, :] = hs
        hprev = ext[pl.ds(7, tc), :]
        dh = dh_sc[...]
        da = dh * hprev
        giu = gi * u
        dla = da * a - (dh * giu) * (a * a / s)
        dgi = dh * s * u
        du = dh * s * gi
        dr = dla * (-LRU_C * sp)
        dlam_ref[...] += jnp.sum(dla * (-LRU_C * r), axis=0, keepdims=True) * (-_sigmoid(-lam_ref[...]))
        dpre = jnp.concatenate([dr * r * (1.0 - r), dgi * gi * (1.0 - gi)], axis=1)
        dpre_b = dpre.astype(MXU_DTYPE)
        du = du + _dot_nt(dpre_b, wab_ref[...])
        dwab_ref[...] += _dot_tn(u.astype(MXU_DTYPE), dpre_b)
        dbab_ref[...] += jnp.sum(dpre, axis=0, keepdims=True)
        dcb_ref[...] += jnp.sum(du, axis=0, keepdims=True)

        du_ext[0:tc, :] = du
        dlx = cw_ref[0:1, :] * du_ext[pl.ds(3, tc), :]
        for k in range(1, CONV_K):
            dlx = dlx + cw_ref[k:k + 1, :] * du_ext[pl.ds(3 - k, tc), :]
        du_ext[tc:tc + 8, :] = du_ext[0:8, :]
        ext[0:8, :] = jnp.where(first_chunk, 0.0, lxp_ref[...])
        ext[8:, :] = lx_ref[...]
        for k in range(CONV_K):
            dcw_ref[k:k + 1, :] += jnp.sum(du * ext[pl.ds(5 + k, tc), :], axis=0, keepdims=True)
        dlxg_ref[:, :LRU_W] = dlx.astype(dlxg_ref.dtype)
        dlxg_ref[:, LRU_W:] = dlg.astype(dlxg_ref.dtype)

    rev = lambda i: (nc - 1 - i, 0)
    prev8 = lambda i: (jnp.maximum((nc - 1 - i) * bp - 1, 0), 0)
    const = lambda i: (0, 0)
    return _pcall(
        body, name=name, grid=(nc,),
        in_specs=[
            pl.BlockSpec((tc, LRU_W), rev),
            pl.BlockSpec((tc, LRU_W), rev),
            pl.BlockSpec((8, LRU_W), prev8),
            pl.BlockSpec((tc, LRU_W), lambda i: (nc - 1 - i, 1)),
            pl.BlockSpec((tc, LRU_W), rev),
            pl.BlockSpec((tc, LRU_W), rev),
            pl.BlockSpec((8, LRU_W), prev8),
            pl.BlockSpec((tc, 4 * LRU_W), rev),
            pl.BlockSpec((CONV_K, LRU_W), const),
            pl.BlockSpec((LRU_W, 2 * LRU_W), const),
            pl.BlockSpec((1, LRU_W), const),
        ],
        out_specs=[
            pl.BlockSpec((tc, 2 * LRU_W), rev),
            pl.BlockSpec((LRU_W, 2 * LRU_W), const),
            pl.BlockSpec((1, 2 * LRU_W), const),
            pl.BlockSpec((8, LRU_W), const),
            pl.BlockSpec((1, LRU_W), const),
            pl.BlockSpec((1, LRU_W), const),
        ],
        out_shape=[
            jax.ShapeDtypeStruct((T, 2 * LRU_W), MXU_DTYPE),
            jax.ShapeDtypeStruct((LRU_W, 2 * LRU_W), f32),
            jax.ShapeDtypeStruct((1, 2 * LRU_W), f32),
            jax.ShapeDtypeStruct((8, LRU_W), f32),
            jax.ShapeDtypeStruct((1, LRU_W), f32),
            jax.ShapeDtypeStruct((1, LRU_W), f32),
        ],
        scratch_shapes=[pltpu.VMEM((tc, LRU_W), f32), pltpu.VMEM((tc, LRU_W), f32),
                        pltpu.VMEM((tc + 8, LRU_W), f32), pltpu.VMEM((tc + 8, LRU_W), f32),
                        pltpu.VMEM((8, LRU_W), f32)],
        compiler_params=_params(1),
    )(dlru, lxg, lxg, lxg, u, hs, hs, gates, conv_w, wab, lam)


def _mix_out(fox, lru, wo, xhat1, g1, b1, g2, b2, *, name, tm=512):
    T = fox.shape[0]
    tm = min(tm, T)
    nt = T // tm

    def body(fox_ref, lru_ref, wo_ref, xh_ref, g1_ref, b1_ref, g2_ref, b2_ref, xhat_ref, xn_ref, rstd_ref):
        mix = _dot(fox_ref[...].astype(MXU_DTYPE), wo_ref[:FOX_W, :])
        mix = mix + _dot(lru_ref[...].astype(MXU_DTYPE), wo_ref[FOX_W:, :])
        x1 = xh_ref[...] * g1_ref[...] + b1_ref[...]
        xhat, rstd = _layer_norm_stats(DN_ALPHA * x1 + mix)
        xhat_ref[...] = xhat
        xn_ref[...] = xhat * g2_ref[...] + b2_ref[...]
        rstd_ref[...] = jnp.broadcast_to(rstd, rstd_ref.shape)

    row = lambda i: (i, 0)
    const = lambda i: (0, 0)
    vec = pl.BlockSpec((1, D_MODEL), const)
    return _pcall(
        body, name=name, grid=(nt,),
        in_specs=[pl.BlockSpec((tm, FOX_W), row), pl.BlockSpec((tm, LRU_W), row),
                  pl.BlockSpec((D_MODEL, D_MODEL), const), pl.BlockSpec((tm, D_MODEL), row), vec, vec, vec, vec],
        out_specs=[pl.BlockSpec((tm, D_MODEL), row), pl.BlockSpec((tm, D_MODEL), row),
                   pl.BlockSpec((tm, LANES), row)],
        out_shape=[jax.ShapeDtypeStruct((T, D_MODEL), f32), jax.ShapeDtypeStruct((T, D_MODEL), f32),
                   jax.ShapeDtypeStruct((T, LANES), f32)],
        compiler_params=_params(1),
    )(fox, lru, wo, xhat1, g1, b1, g2, b2)


def _mix_out_bwd(dy, xhat, rstd, ln_g, fox, lru, wo, *, name, tm=512):
    T = fox.shape[0]
    tm = min(tm, T)
    nt = T // tm

    def body(dy_ref, xhat_ref, rstd_ref, g_ref, fox_ref, lru_ref, wo_ref,
             dyp_ref, dgam_ref, dbeta_ref, dlru_ref, dwo_ref, d_ref, doa_ref):
        i = pl.program_id(0)

        @pl.when(i == 0)
        def _():
            dwo_ref[...] = jnp.zeros_like(dwo_ref)
            dgam_ref[...] = jnp.zeros_like(dgam_ref)
            dbeta_ref[...] = jnp.zeros_like(dbeta_ref)

        dyp, dgam, dbeta = _ln_backward(dy_ref[...], xhat_ref[...], rstd_ref[:, 0:1], g_ref[...])
        dyp_ref[...] = dyp
        dgam_ref[...] += dgam
        dbeta_ref[...] += dbeta
        dmix = dyp.astype(MXU_DTYPE)
        dcat = _dot_nt(dmix, wo_ref[...])
        dlru_ref[...] = dcat[:, FOX_W:]
        low = _low_lanes((tm, LANES))
        for j in range(HEADS // 2):
            do2 = dcat[:, j * LANES:(j + 1) * LANES].astype(MXU_DTYPE).astype(f32)
            prod = do2 * fox_ref[:, j * LANES:(j + 1) * LANES]
            for odd in range(2):
                h = 2 * j + odd
                mine = jnp.where(low, _swap_lane_halves(prod) if odd else prod, 0.0)
                d_ref[h] = jnp.broadcast_to(jnp.sum(mine, axis=1, keepdims=True), (tm, LANES))
                doh = jnp.where(low, _swap_lane_halves(do2) if odd else do2, 0.0)
                doa_ref[:, h * LANES:(h + 1) * LANES] = doh.astype(doa_ref.dtype)
        dwo_ref[:FOX_W, :] += _dot_tn(fox_ref[...].astype(MXU_DTYPE), dmix)
        dwo_ref[FOX_W:, :] += _dot_tn(lru_ref[...].astype(MXU_DTYPE), dmix)

    row = lambda i: (i, 0)
    const = lambda i: (0, 0)
    return _pcall(
        body, name=name, grid=(nt,),
        in_specs=[pl.BlockSpec((tm, D_MODEL), row), pl.BlockSpec((tm, D_MODEL), row), pl.BlockSpec((tm, LANES), row),
                  pl.BlockSpec((1, D_MODEL), const),
                  pl.BlockSpec((tm, FOX_W), row), pl.BlockSpec((tm, LRU_W), row),
                  pl.BlockSpec((D_MODEL, D_MODEL), const)],
        out_specs=[pl.BlockSpec((tm, D_MODEL), row), pl.BlockSpec((1, D_MODEL), const), pl.BlockSpec((1, D_MODEL), const),
                   pl.BlockSpec((tm, LRU_W), row), pl.BlockSpec((D_MODEL, D_MODEL), const),
                   pl.BlockSpec((HEADS, tm, LANES), lambda i: (0, i, 0)), pl.BlockSpec((tm, HEADS * LANES), row)],
        out_shape=[jax.ShapeDtypeStruct((T, D_MODEL), f32), jax.ShapeDtypeStruct((1, D_MODEL), f32),
                   jax.ShapeDtypeStruct((1, D_MODEL), f32),
                   jax.ShapeDtypeStruct((T, LRU_W), f32), jax.ShapeDtypeStruct((D_MODEL, D_MODEL), f32),
                   jax.ShapeDtypeStruct((HEADS, T, LANES), f32), jax.ShapeDtypeStruct((T, HEADS * LANES), MXU_DTYPE)],
        compiler_params=_params(1),
    )(dy, xhat, rstd, ln_g, fox, lru, wo)


def make_wp(w_in):
    scale = jnp.concatenate([jnp.full((FOX_W,), 1.0 / math.sqrt(HEAD_DIM), w_in.dtype),
                             jnp.ones((IN_COLS - FOX_W,), w_in.dtype)])
    return jnp.pad(w_in * scale[None, :], ((0, 0), (0, Z_PAD - IN_COLS)))


def _block_diag(w):
    eye = jnp.eye(HEADS, dtype=w.dtype)
    return jnp.einsum("hij,hg->higj", w, eye).reshape(LRU_W, LRU_W)


def _block_diag_extract(m):
    m4 = m.reshape(HEADS, HEAD_DIM, HEADS, HEAD_DIM)
    return jnp.stack([m4[h, :, h, :] for h in range(HEADS)])


class _NoOverlap:
    def ffn1_up(self, x, w):
        return _ffn_up(x, w["f1g"], w["f1u"], name="ffn1_up")

    def late_weights(self, w, after):
        return dict(f1d=w["f1d"], wp=w["wp"], wo=w["wo"])

    def after_attention(self, after):
        return None

    def ffn2_weights(self, w, after):
        return w["f2g"], w["f2u"], w["f2d"]

    def ffn2_grads(self, grads):
        return None

    def ffn1_grads(self, grads):
        return None

    def mixer_grads(self, dwp, dwo, small, loss):
        return None

    def before_ffn1_bwd(self, after):
        return None


def _tied(a, token):
    return a if token is None else a + token[0, 0]


def _local_step(x, target, w, hooks=None):
    hooks = hooks or _NoOverlap()
    bfp = w["bfp"]
    wab = jnp.concatenate([_block_diag(w["rg_wa"]), _block_diag(w["rg_wx"])], axis=1).astype(MXU_DTYPE)
    bab = jnp.concatenate([w["rg_ba"].reshape(1, LRU_W), w["rg_bx"].reshape(1, LRU_W)], axis=1)

    xb0, g1a, u1a, h1a = hooks.ffn1_up(x, w)
    late = hooks.late_weights(w, [h1a])
    f1d, wp, wo = late["f1d"], late["wp"], late["wo"]
    xhat1, xn1, rstd1 = _ffn_down_ln(x, h1a, f1d, w["ln1_g"], w["ln1_b"], name="ffn1_down")
    lxg, fgb, qa, ka, va = _proj_in(xn1, wp, bfp, name="proj_in")
    fox, lse = _fox_fwd(qa, ka, va, name="fox_fwd")
    token = hooks.after_attention([lse])
    lru, uconv, hs, gates = _lru_fwd(lxg, w["conv_w"], _tied(w["conv_b"], token), wab, bab, w["lam"], name="lru_fwd")
    xhat2, x2, rstd2 = _mix_out(fox, lru, wo, xhat1, w["ln1_g"], w["ln1_b"], w["ln2_g"], w["ln2_b"], name="mix_out")
    f2g, f2u, f2d = hooks.ffn2_weights(w, [rstd2])
    xb2, g2a, u2a, dy3p, dln3g, dln3b, loss = _ffn_fwd_loss(x2, f2g, f2u, f2d, w["ln3_g"], w["ln3_b"], target,
                                                            name="ffn2_fwd_loss")

    dx2, df2g, df2u, df2d = _ffn_bwd(dy3p, xb2, g2a, u2a, f2g, f2u, f2d, name="ffn2_bwd")
    token = hooks.ffn2_grads([df2g, df2u, df2d])
    dy2p, dln2g, dln2b, dlru, dwo, drep, doa = _mix_out_bwd(dx2, xhat2, rstd2, _tied(w["ln2_g"], token), fox, lru, wo,
                                                            name="mix_out_bwd")
    dlxg, dwab, dbab, dcw, dcb, dlam = _lru_bwd(dlru, lxg, uconv, hs, gates, w["conv_w"], wab, w["lam"], name="lru_bwd")
    dqa, dka, dva = _fox_bwd(qa, ka, va, doa, lse, drep, name="fox_bwd")
    dy1p, dwp, dln1g, dln1b, dbf = _proj_in_bwd(dqa, dka, dva, dlxg, fgb, xn1, dy2p, wp, xhat1, rstd1, w["ln1_g"],
                                                name="proj_in_bwd")
    small = dict(
        ln1_g=dln1g, ln1_b=dln1b, ln2_g=dln2g, ln2_b=dln2b, ln3_g=dln3g, ln3_b=dln3b,
        b_forget=dbf[:, :HEADS], conv_w=dcw[:CONV_K], conv_b=dcb,
        rg_wa=_block_diag_extract(dwab[:, :LRU_W]), rg_wx=_block_diag_extract(dwab[:, LRU_W:]),
        rg_ba=dbab[:, :LRU_W].reshape(HEADS, HEAD_DIM), rg_bx=dbab[:, LRU_W:].reshape(HEADS, HEAD_DIM),
        lru_lambda=dlam,
    )
    hooks.before_ffn1_bwd([dln1b])
    token = hooks.mixer_grads(dwp, dwo, small, loss)
    dx_a, *grads_a = _ffn_bwd(dy1p, xb0, g1a, u1a, w["f1g"], w["f1u"], f1d, token, name="ffn1_bwd_a", part=0)
    token = hooks.ffn1_grads(grads_a)
    dx, *grads_b = _ffn_bwd(dy1p, xb0, g1a, u1a, w["f1g"], w["f1u"], f1d, token, name="ffn1_bwd_b", part=1,
                            dx_init=dx_a)

    grads = dict(f1=(grads_a, grads_b), f2g=df2g, f2u=df2u, f2d=df2d, wp=dwp, wo=dwo, **small)
    return loss, dx, grads


MESH = pl.DeviceIdType.MESH
HBM_SPEC = pl.BlockSpec(memory_space=pl.ANY)
VMEM_SPEC = pl.BlockSpec(memory_space=pltpu.VMEM)


def _position():
    return lax.axis_index("x"), lax.axis_index("y"), lax.axis_index("c")


def _other_chips(x, y):
    return [(1 - x, y), (x, 1 - y), (1 - x, 1 - y)]


def _all_gather_bf16(shards, *, name):
    n = len(shards)

    def body(*refs):
        ins, outs, stages = refs[:n], refs[n:2 * n], refs[2 * n:3 * n]
        send_sems, recv_sems, local_sems = refs[3 * n:]
        x, y, c = _position()
        me, sibling = (x, y, c), (x, y, 1 - c)
        chips = _other_chips(x, y)

        def rows(k, px, py, pc):
            r = shards[k].shape[0]
            m = r // 2
            return outs[k].at[pl.ds(pl.multiple_of((2 * px + py) * r + pc * m, 16), m), :]

        def copy(k, idx, block, to, src=None):
            return pltpu.make_async_remote_copy(
                src_ref=rows(k, *block) if src is None else src, dst_ref=rows(k, *block),
                send_sem=send_sems.at[7 * k + idx], recv_sem=recv_sems.at[7 * k + idx],
                device_id=to, device_id_type=MESH)

        started = []
        mine = []
        for k in range(n):
            m = shards[k].shape[0] // 2
            stages[k][...] = ins[k][pl.ds(pl.multiple_of(c * m, 16), m), :].astype(stages[k].dtype)
            cp = pltpu.make_async_copy(stages[k], rows(k, *me), local_sems.at[k])
            cp.start()
            mine.append(cp)
            first = [copy(k, 0, me, sibling, src=stages[k])]
            first += [copy(k, 1 + j, me, (*chip, c), src=stages[k]) for j, chip in enumerate(chips)]
            for cp in first:
                cp.start()
            started += first
        for k in range(n):
            for j, chip in enumerate(chips):
                copy(k, 1 + j, (*chip, c), me).wait_recv()
                fwd = copy(k, 4 + j, (*chip, c), sibling)
                fwd.start()
                started.append(fwd)
        for k in range(n):
            copy(k, 0, sibling, me).wait_recv()
            for j, chip in enumerate(chips):
                copy(k, 4 + j, (*chip, 1 - c), me).wait_recv()
        for cp in started:
            cp.wait_send()
        for cp in mine:
            cp.wait()

    return _pcall(
        body, name=name,
        in_specs=[VMEM_SPEC] * n, out_specs=[HBM_SPEC] * n,
        out_shape=[jax.ShapeDtypeStruct((N_SHARD * s.shape[0], s.shape[1]), MXU_DTYPE) for s in shards],
        scratch_shapes=[pltpu.VMEM((s.shape[0] // 2, s.shape[1]), MXU_DTYPE) for s in shards]
        + [pltpu.SemaphoreType.DMA((7 * n,)), pltpu.SemaphoreType.DMA((7 * n,)), pltpu.SemaphoreType.DMA((n,))],
        compiler_params=pltpu.CompilerParams(vmem_limit_bytes=VMEM_LIMIT),
    )(*shards)


def _swap_halves(gs, *, name):
    n = len(gs)

    def body(*refs):
        ins, outs = refs[:n], refs[n:2 * n]
        send_sems, recv_sems = refs[2 * n:]
        x, y, c = _position()
        cps = []
        for k in range(n):
            m = gs[k].shape[1] // 2
            src = ins[k].at[:, pl.ds(pl.multiple_of((1 - c) * m, 16), m), :]
            cp = pltpu.make_async_remote_copy(src_ref=src, dst_ref=outs[k], send_sem=send_sems.at[k],
                                              recv_sem=recv_sems.at[k], device_id=(x, y, 1 - c), device_id_type=MESH)
            cp.start()
            cps.append(cp)
        for cp in cps:
            cp.wait()

    return _pcall(
        body, name=name, in_specs=[HBM_SPEC] * n, out_specs=[HBM_SPEC] * n,
        out_shape=[jax.ShapeDtypeStruct((g.shape[0], g.shape[1] // 2, g.shape[2]), g.dtype) for g in gs],
        scratch_shapes=[pltpu.SemaphoreType.DMA((n,)), pltpu.SemaphoreType.DMA((n,))],
    )(*gs)


def _add_halves(gs, recvs, *, name, tm=256):
    n = len(gs)
    _, r, cdim = gs[0].shape
    m = r // 2
    tm = min(tm, m)
    nb = m // tm
    c_idx = lax.axis_index("c").astype(jnp.int32).reshape(1)

    def body(c_ref, *refs):
        for k in range(n):
            refs[2 * n + k][...] = (refs[k][...].astype(f32) + refs[n + k][...].astype(f32)).astype(refs[2 * n + k].dtype)

    mine = pl.BlockSpec((None, tm, cdim), lambda j, i, c_ref: (j, c_ref[0] * nb + i, 0))
    half = pl.BlockSpec((None, tm, cdim), lambda j, i, c_ref: (j, i, 0))
    return _pcall(
        body, name=name,
        grid_spec=pltpu.PrefetchScalarGridSpec(
            num_scalar_prefetch=1, grid=(N_SHARD, nb),
            in_specs=[mine] * n + [half] * n, out_specs=[half] * n),
        out_shape=[jax.ShapeDtypeStruct((N_SHARD, m, cdim), g.dtype) for g in gs],
        compiler_params=_params(2),
    )(c_idx, *gs, *recvs)


def _scatter_partials(ps, *, name):
    n = len(ps)

    def body(*refs):
        ins, outs = refs[:n], refs[n:2 * n]
        send_sems, recv_sems = refs[2 * n:]
        x, y, c = _position()
        me_chip = 2 * x + y
        cps = []
        for k in range(n):
            for j, (px, py) in enumerate(_other_chips(x, y)):
                cp = pltpu.make_async_remote_copy(
                    src_ref=ins[k].at[2 * px + py], dst_ref=outs[k].at[me_chip],
                    send_sem=send_sems.at[3 * k + j], recv_sem=recv_sems.at[3 * k + j],
                    device_id=(px, py, c), device_id_type=MESH)
                cp.start()
                cps.append(cp)
        for cp in cps:
            cp.wait()

    return _pcall(
        body, name=name, in_specs=[HBM_SPEC] * n, out_specs=[HBM_SPEC] * n,
        out_shape=[jax.ShapeDtypeStruct(p.shape, p.dtype) for p in ps],
        scratch_shapes=[pltpu.SemaphoreType.DMA((3 * n,)), pltpu.SemaphoreType.DMA((3 * n,))],
    )(*ps)


def _sum_slabs(ps, qs, *, name, tm=128):
    n = len(qs)
    _, m, cdim = qs[0].shape
    tm = min(tm, m)
    nb = m // tm
    assert m % tm == 0, (m, tm)
    where = jnp.stack([2 * lax.axis_index("x") + lax.axis_index("y"), lax.axis_index("c")]).astype(jnp.int32)

    def body(w_ref, *refs):
        for k in range(n):
            own, q1, q2, q3 = (refs[4 * k + t][...].astype(f32) for t in range(4))
            refs[4 * n + k][...] = ((own + q1) + q2) + q3

    def slab(flip):
        return pl.BlockSpec((None, tm, cdim), lambda i, w_ref: (jnp.bitwise_xor(w_ref[0], flip), i, 0))

    operands = []
    for p, q in zip(ps, qs):
        operands += [p, q, q, q]
    return _pcall(
        body, name=name,
        grid_spec=pltpu.PrefetchScalarGridSpec(
            num_scalar_prefetch=1, grid=(nb,),
            in_specs=[slab(0), slab(2), slab(1), slab(3)] * n,
            out_specs=[pl.BlockSpec((tm, cdim), lambda i, w_ref: (w_ref[1] * nb + i, 0))] * n),
        out_shape=[jax.ShapeDtypeStruct((2 * m, cdim), f32) for _ in qs],
        compiler_params=_params(1),
    )(where, *operands)


def _join_halves(fs, *, name):
    n = len(fs)

    def body(*refs):
        outs = refs[n:2 * n]
        send_sems, recv_sems = refs[2 * n:]
        x, y, c = _position()
        cps = []
        for k in range(n):
            m = fs[k].shape[0] // 2
            half = outs[k].at[pl.ds(pl.multiple_of(c * m, 8), m), :]
            cp = pltpu.make_async_remote_copy(src_ref=half, dst_ref=half, send_sem=send_sems.at[k],
                                              recv_sem=recv_sems.at[k], device_id=(x, y, 1 - c), device_id_type=MESH)
            cp.start()
            cps.append(cp)
        for cp in cps:
            cp.wait()

    return _pcall(
        body, name=name, in_specs=[HBM_SPEC] * n, out_specs=[HBM_SPEC] * n,
        out_shape=[jax.ShapeDtypeStruct(f.shape, f.dtype) for f in fs],
        input_output_aliases={k: k for k in range(n)},
        scratch_shapes=[pltpu.SemaphoreType.DMA((n,)), pltpu.SemaphoreType.DMA((n,))],
    )(*fs)


def _all_reduce_small(v, after=None, *, name):
    r = v.shape[0]
    extra = [] if after is None else [after]

    def body(v_ref, *refs):
        out_ref, buf, send_sems, recv_sems, local_sem = refs[len(extra):]
        x, y, c = _position()
        me, sibling = (x, y, c), (x, y, 1 - c)
        chips = _other_chips(x, y)

        def rows(px, py, pc):
            return buf.at[pl.ds(pl.multiple_of((4 * px + 2 * py + pc) * r, 8), r), :]

        def copy(k, block, to, src=None):
            return pltpu.make_async_remote_copy(
                src_ref=rows(*block) if src is None else src, dst_ref=rows(*block),
                send_sem=send_sems.at[k], recv_sem=recv_sems.at[k], device_id=to, device_id_type=MESH)

        mine = pltpu.make_async_copy(v_ref, rows(*me), local_sem)
        mine.start()
        first = [copy(0, me, sibling, src=v_ref)]
        first += [copy(1 + j, me, (*chip, c), src=v_ref) for j, chip in enumerate(chips)]
        for cp in first:
            cp.start()
        passed = [copy(4 + j, (*chip, c), sibling) for j, chip in enumerate(chips)]
        for j, chip in enumerate(chips):
            copy(1 + j, (*chip, c), me).wait_recv()
            passed[j].start()
        copy(0, sibling, me).wait_recv()
        for j, chip in enumerate(chips):
            copy(4 + j, (*chip, 1 - c), me).wait_recv()
        for cp in first + passed:
            cp.wait_send()
        mine.wait()
        acc = buf[0:r, :]
        for d in range(1, N_DEV):
            acc = acc + buf[d * r:(d + 1) * r, :]
        out_ref[...] = acc

    return _pcall(
        body, name=name, in_specs=[VMEM_SPEC] + [HBM_SPEC] * len(extra), out_specs=VMEM_SPEC,
        out_shape=jax.ShapeDtypeStruct((r, LANES), f32),
        scratch_shapes=[pltpu.VMEM((N_DEV * r, LANES), f32), pltpu.SemaphoreType.DMA((7,)),
                        pltpu.SemaphoreType.DMA((7,)), pltpu.SemaphoreType.DMA],
    )(v, *extra)


SEM_SPEC = pl.BlockSpec(memory_space=pltpu.SEMAPHORE)
HBM_ONLY = pl.BlockSpec(memory_space=pltpu.HBM)
EFFECT = pltpu.SideEffectType.DATAFLOW_SIDE_EFFECTING


def _sends(copies):
    return copies[0] if isinstance(copies, tuple) else copies


def _arrivals(copies):
    return copies[1] if isinstance(copies, tuple) else copies


def _split_start(bufs, copies_fn, n_sems, *, name):
    n = len(bufs)

    def body(*refs):
        send_sems, recv_sems = refs[n], refs[n + 1]
        thru = refs[n + 2:2 * n + 2]
        token = refs[2 * n + 2]
        for cp in _sends(copies_fn(thru, send_sems, recv_sems)):
            cp.start()
        token[...] = jnp.zeros_like(token)

    outs = _pcall(
        body, name=name,
        out_shape=(pltpu.SemaphoreType.DMA((n_sems,)), pltpu.SemaphoreType.DMA((n_sems,)),
                   *[pltpu.HBM(b.shape, b.dtype) for b in bufs], jax.ShapeDtypeStruct((8, LANES), f32)),
        in_specs=[HBM_ONLY] * n,
        out_specs=(SEM_SPEC, SEM_SPEC, *[HBM_ONLY] * n, VMEM_SPEC),
        input_output_aliases={k: 2 + k for k in range(n)},
        compiler_params=pltpu.CompilerParams(has_side_effects=EFFECT),
    )(*[pltpu.with_memory_space_constraint(b, pltpu.HBM) for b in bufs])
    return outs[0], outs[1], list(outs[2:2 + n]), outs[2 + n]


def _split_wait(thru, send_sems, recv_sems, after, copies_fn, *, name):
    n = len(thru)

    def body(*refs):
        copies = copies_fn(refs[:n], refs[n], refs[n + 1])
        for cp in _sends(copies):
            cp.wait_send()
        for cp in _arrivals(copies):
            cp.wait_recv()

    return list(_pcall(
        body, name=name,
        out_shape=tuple(pltpu.HBM(b.shape, b.dtype) for b in thru),
        in_specs=[HBM_ONLY] * n + [SEM_SPEC, SEM_SPEC] + [HBM_SPEC] * len(after),
        out_specs=tuple([HBM_ONLY] * n),
        input_output_aliases={k: k for k in range(n)},
        compiler_params=pltpu.CompilerParams(has_side_effects=EFFECT),
    )(*thru, send_sems, recv_sems, *after))


def _scatter_copies(n):
    def copies(bufs, send_sems, recv_sems):
        x, y, c = _position()
        me_chip = 2 * x + y
        cps = []
        for k in range(n):
            for j, (px, py) in enumerate(_other_chips(x, y)):
                cps.append(pltpu.make_async_remote_copy(
                    src_ref=bufs[k].at[2 * px + py], dst_ref=bufs[n + k].at[me_chip],
                    send_sem=send_sems.at[3 * k + j], recv_sem=recv_sems.at[3 * k + j],
                    device_id=(px, py, c), device_id_type=MESH))
        return cps
    return copies


N_PEERS = N_DEV - 1


def _direct_copies(n):
    def copies(bufs, send_sems, recv_sems):
        x, y, c = _position()
        me_chip = 2 * x + y
        sends, arrivals = [], []
        for k in range(n):
            m = bufs[k].shape[1] // 2
            land = bufs[n + k]

            def rows(slab, half, k=k, m=m):
                start = half * m if isinstance(half, int) else pl.multiple_of(half * m, 16)
                return bufs[k].at[slab, pl.ds(start, m), :]

            def copy(src, slot, send_idx, recv_idx, to, k=k, land=land):
                return pltpu.make_async_remote_copy(
                    src_ref=src, dst_ref=land.at[slot], send_sem=send_sems.at[N_PEERS * k + send_idx],
                    recv_sem=recv_sems.at[N_PEERS * k + recv_idx], device_id=to, device_id_type=MESH)

            sends.append(copy(rows(me_chip, 1 - c), 0, 0, 0, (x, y, 1 - c)))
            arrivals.append(copy(rows(me_chip, c), 0, 0, 0, (x, y, 1 - c)))
            for t, (px, py) in enumerate(_other_chips(x, y)):
                for core in range(2):
                    sends.append(copy(rows(2 * px + py, core), 1 + 2 * t + c, 1 + 2 * t + core, 1 + 2 * t + c,
                                      (px, py, core)))
                    arrivals.append(copy(rows(me_chip, c), 1 + 2 * t + core, 1 + 2 * t + core, 1 + 2 * t + core,
                                         (px, py, core)))
        return sends, arrivals
    return copies


def _sum_direct(gs, lands, *, name, tm=128):
    n = len(gs)
    _, m, cdim = lands[0].shape
    tm = min(tm, m)
    nb = m // tm
    assert m % tm == 0, (m, tm)
    where = jnp.stack([2 * lax.axis_index("x") + lax.axis_index("y"), lax.axis_index("c")]).astype(jnp.int32)

    def body(w_ref, *refs):
        for k in range(n):
            acc = refs[2 * k][...].astype(f32)
            for slot in range(N_PEERS):
                acc = acc + refs[2 * k + 1][slot].astype(f32)
            refs[2 * n + k][...] = acc

    own = pl.BlockSpec((None, tm, cdim), lambda i, w_ref: (w_ref[0], w_ref[1] * nb + i, 0))
    landed = pl.BlockSpec((N_PEERS, tm, cdim), lambda i, w_ref: (0, i, 0))
    operands = []
    for g, land in zip(gs, lands):
        operands += [g, land]
    return _pcall(
        body, name=name,
        grid_spec=pltpu.PrefetchScalarGridSpec(
            num_scalar_prefetch=1, grid=(nb,), in_specs=[own, landed] * n,
            out_specs=[pl.BlockSpec((tm, cdim), lambda i, w_ref: (w_ref[1] * nb + i, 0))] * n),
        out_shape=[jax.ShapeDtypeStruct((2 * m, cdim), f32) for _ in gs],
        compiler_params=_params(1),
    )(where, *operands)


def _broadcast_copies(bufs, send_sems, recv_sems):
    v, land = bufs
    x, y, c = _position()

    def copy(slot, send_idx, recv_idx, to):
        return pltpu.make_async_remote_copy(src_ref=v, dst_ref=land.at[slot], send_sem=send_sems.at[send_idx],
                                            recv_sem=recv_sems.at[recv_idx], device_id=to, device_id_type=MESH)

    sends = [copy(0, 0, 0, (x, y, 1 - c))]
    arrivals = [copy(0, 0, 0, (x, y, 1 - c))]
    for t, (px, py) in enumerate(_other_chips(x, y)):
        for core in range(2):
            sends.append(copy(1 + 2 * t + c, 1 + 2 * t + core, 1 + 2 * t + c, (px, py, core)))
            arrivals.append(copy(1 + 2 * t + core, 1 + 2 * t + core, 1 + 2 * t + core, (px, py, core)))
    return sends, arrivals


def _sum_in_device_order(v, land, *, name):
    r, cdim = v.shape
    x, y, c = _position()
    slots, mine = [], []
    for d in range(N_DEV):
        dx, dy, dc = d // 4, (d // 2) % 2, d % 2
        fx, fy = jnp.bitwise_xor(dx, x), jnp.bitwise_xor(dy, y)
        t = jnp.where(fx == 1, jnp.where(fy == 1, 2, 0), 1)
        slots.append(jnp.where(jnp.logical_and(fx == 0, fy == 0), 0, 1 + 2 * t + dc))
        mine.append(jnp.logical_and(jnp.logical_and(fx == 0, fy == 0), dc == c))
    table = jnp.stack(slots + mine).astype(jnp.int32)

    def body(tab_ref, v_ref, *refs):
        out_ref = refs[N_DEV]
        acc = None
        for d in range(N_DEV):
            term = jnp.where(tab_ref[N_DEV + d] == 1, v_ref[...], refs[d][...])
            acc = term if acc is None else acc + term
        out_ref[...] = acc

    whole = pl.BlockSpec((r, cdim), lambda i, tab_ref: (0, 0))
    landed = [pl.BlockSpec((None, r, cdim), functools.partial(lambda i, tab_ref, d: (tab_ref[d], 0, 0), d=d))
              for d in range(N_DEV)]
    return _pcall(
        body, name=name,
        grid_spec=pltpu.PrefetchScalarGridSpec(num_scalar_prefetch=1, grid=(1,), in_specs=[whole] + landed,
                                               out_specs=whole),
        out_shape=jax.ShapeDtypeStruct((r, cdim), f32),
        compiler_params=_params(1),
    )(table, v, *[land] * N_DEV)


def _block_rows(buf, px, py, pc):
    m = buf.shape[0] // N_DEV
    return buf.at[pl.ds(pl.multiple_of((4 * px + 2 * py + pc) * m, 16), m), :]


def _gather_ici_copies(n):
    def copies(bufs, send_sems, recv_sems):
        x, y, c = _position()
        cps = []
        for k in range(n):
            rows = _block_rows(bufs[k], x, y, c)
            targets = [(x, y, 1 - c)] + [(px, py, c) for px, py in _other_chips(x, y)]
            for j, to in enumerate(targets):
                cps.append(pltpu.make_async_remote_copy(
                    src_ref=rows, dst_ref=rows, send_sem=send_sems.at[4 * k + j], recv_sem=recv_sems.at[4 * k + j],
                    device_id=to, device_id_type=MESH))
        return cps
    return copies


def _gather_d2d_copies(n):
    def copies(bufs, send_sems, recv_sems):
        x, y, c = _position()
        cps = []
        for k in range(n):
            for j, (px, py) in enumerate(_other_chips(x, y)):
                rows = _block_rows(bufs[k], px, py, c)
                cps.append(pltpu.make_async_remote_copy(
                    src_ref=rows, dst_ref=rows, send_sem=send_sems.at[3 * k + j], recv_sem=recv_sems.at[3 * k + j],
                    device_id=(x, y, 1 - c), device_id_type=MESH))
        return cps
    return copies


def _cast_halves(shards, after, *, name):
    n = len(shards)
    where = jnp.stack([2 * lax.axis_index("x") + lax.axis_index("y"), lax.axis_index("c")]).astype(jnp.int32)

    def body(w_ref, *refs):
        for k in range(n):
            refs[n + 1 + k][...] = refs[k][...].astype(refs[n + 1 + k].dtype)

    def half(s):
        return (s.shape[0] // 2, s.shape[1])

    return _pcall(
        body, name=name,
        grid_spec=pltpu.PrefetchScalarGridSpec(
            num_scalar_prefetch=1, grid=(1,),
            in_specs=[pl.BlockSpec(half(s), lambda i, w_ref: (w_ref[1], 0)) for s in shards] + [HBM_SPEC],
            out_specs=[pl.BlockSpec(half(s), lambda i, w_ref: (2 * w_ref[0] + w_ref[1], 0)) for s in shards]),
        out_shape=[jax.ShapeDtypeStruct((N_SHARD * s.shape[0], s.shape[1]), MXU_DTYPE) for s in shards],
        compiler_params=_params(1),
    )(where, *shards, after)


class _SplitGather:
    def __init__(self, shards, after, tag):
        self.tag = tag
        self.n = len(shards)
        halves = _cast_halves(shards, after, name=f"{tag}_cast")
        self.ici = _split_start(halves, _gather_ici_copies(self.n), 4 * self.n, name=f"{tag}_ici_start")
        self.token = self.ici[3]

    def forward(self, after):
        send_sems, recv_sems, thru, _ = self.ici
        landed = _split_wait(thru, send_sems, recv_sems, after, _gather_ici_copies(self.n), name=f"{self.tag}_ici_wait")
        self.d2d = _split_start(landed, _gather_d2d_copies(self.n), 3 * self.n, name=f"{self.tag}_d2d_start")
        return self.d2d[3]

    def finish(self, after):
        send_sems, recv_sems, thru, _ = self.d2d
        return _split_wait(thru, send_sems, recv_sems, after, _gather_d2d_copies(self.n), name=f"{self.tag}_d2d_wait")


class _Overlap(_NoOverlap):
    def __init__(self, late_shards, ffn2_shards, after, own_part, chip):
        self.late = _SplitGather(late_shards, after, "ag1")
        self.ffn2 = _SplitGather(ffn2_shards, self.late.token, "ag2")
        self.reduced = None
        self.ffn1_parts = []
        self.own_part = own_part
        self.chip = chip

    def start_token(self):
        return self.ffn2.token

    def ffn1_up(self, x, w):
        xb = self.own_part[0]
        return [xb] + _ffn_up_part(xb, w["f1g"], w["f1u"], self.chip, self.own_part[1:], None, name="ffn1_up_rest")

    def late_weights(self, w, after):
        token = self.late.forward(after)
        f1d, w_in, wo = self.late.finish([token])
        w_in = w_in.reshape(N_SHARD, D_MODEL, IN_SHARD)
        w_in = jnp.concatenate([w_in[j] for j in range(N_SHARD)], axis=1)
        return dict(f1d=f1d.reshape(N_SHARD, D_FF // N_SHARD, D_MODEL), wp=make_wp(w_in), wo=wo)

    def after_attention(self, after):
        return self.ffn2.forward(after)

    def ffn2_weights(self, w, after):
        full = self.ffn2.finish(after)
        fs = D_FF // N_SHARD
        return (full[0].reshape(N_SHARD, D_MODEL, fs), full[1].reshape(N_SHARD, D_MODEL, fs),
                full[2].reshape(N_SHARD, fs, D_MODEL))

    @staticmethod
    def _send_direct(grads, tag):
        lands = [lax.empty((N_PEERS, g.shape[1] // 2, g.shape[2]), g.dtype) for g in grads]
        return _split_start(list(grads) + lands, _direct_copies(len(grads)), N_PEERS * len(grads),
                            name=f"rs_direct_{tag}_start")

    def ffn2_grads(self, grads):
        self.scatter = self._send_direct(grads, "ffn2")
        return self.scatter[3]

    def ffn1_grads(self, grads):
        tag = "ffn1" + "ab"[len(self.ffn1_parts)]
        if not self.ffn1_parts:
            started = self._send_direct(grads, tag)
        else:
            recvs = _swap_halves(grads, name=f"rs_swap_{tag}")
            ps = list(_add_halves(grads[:2], recvs[:2], name=f"rs_add_{tag}_gu"))
            ps += list(_add_halves(grads[2:], recvs[2:], name=f"rs_add_{tag}_d"))
            lands = [lax.empty(p.shape, p.dtype) for p in ps]
            started = _split_start(ps + lands, _scatter_copies(3), 9, name=f"rs_scatter_{tag}_start")
        self.ffn1_parts.append((tag, started))
        return started[3]

    def ffn1_reduced(self, after):
        sums = []
        for direct, (tag, (send_sems, recv_sems, thru, _)) in zip((True, False), self.ffn1_parts):
            plan, add = (_direct_copies, _sum_direct) if direct else (_scatter_copies, _sum_slabs)
            done = _split_wait(thru, send_sems, recv_sems, after, plan(3), name=f"rs_{tag}_wait")
            sums += list(add(done[:2], done[3:5], name=f"rs_sum_{tag}_gu"))
            sums += list(add(done[2:3], done[5:], name=f"rs_sum_{tag}_d"))
        return sums

    def mixer_grads(self, dwp, dwo, small, loss):
        packed = jnp.concatenate([_pack_small(small), jnp.broadcast_to(loss, (8, LANES))], axis=0)
        land = lax.empty((N_PEERS,) + packed.shape, packed.dtype)
        self.small = _split_start([packed, land], _broadcast_copies, N_PEERS, name="ar_small_start")
        gwin = jnp.stack([dwp[:, j * IN_SHARD:(j + 1) * IN_SHARD] for j in range(N_SHARD)]).astype(GRAD_DTYPE)
        gwo = dwo.reshape(N_SHARD, D_MODEL // N_SHARD, D_MODEL).astype(GRAD_DTYPE)
        self.scatter_mix = self._send_direct([gwin, gwo], "mix")
        return self.small[3] + self.scatter_mix[3]

    def small_summed(self, after):
        send_sems, recv_sems, thru, _ = self.small
        packed, land = _split_wait(thru, send_sems, recv_sems, after, _broadcast_copies, name="ar_small_wait")
        summed = _sum_in_device_order(packed, land, name="ar_small_sum")
        return summed[:-8], summed[-8, 0]

    def mixer_reduced(self, after):
        send_sems, recv_sems, thru, _ = self.scatter_mix
        done = _split_wait(thru, send_sems, recv_sems, after, _direct_copies(2), name="rs_direct_mix_wait")
        return [_sum_direct([done[k]], [done[2 + k]], name=f"rs_sum_{tag}")[0] for k, tag in enumerate(["w_in", "w_out"])]

    def before_ffn1_bwd(self, after):
        send_sems, recv_sems, thru, _ = self.scatter
        n = len(thru) // 2
        done = _split_wait(thru, send_sems, recv_sems, after, _direct_copies(n), name="rs_direct_ffn2_wait")
        self.reduced = list(_sum_direct(done[:n], done[n:], name="rs_sum_ffn2"))


def _adamw(gs, ws, ms, vs, *, name, tm=256):
    n = len(gs)
    r, cdim = ws[0].shape[-2:]
    tm = r if tm is None else min(tm, r)
    assert r % tm == 0, (r, tm)
    nb = r // tm
    c1 = 1.0 / (1.0 - ADAM_B1 ** ADAM_STEP)
    c2 = 1.0 / (1.0 - ADAM_B2 ** ADAM_STEP)
    flat = pl.BlockSpec((tm, cdim), lambda i: (i, 0))

    g_ops, g_specs, g_where = [], [], []
    for g in gs:
        g_where.append(len(g_ops))
        if not isinstance(g, tuple):
            g_ops.append(g)
            g_specs.append(flat)
        elif g[2] == 1:
            g_ops += [g[0], g[1]]
            g_specs += [pl.BlockSpec((tm, cdim // 2), lambda i: (i, 0))] * 2
        else:
            g_ops += [g[0], g[1]]
            g_specs += [pl.BlockSpec((tm, cdim), lambda i: (jnp.minimum(i, nb // 2 - 1), 0)),
                        pl.BlockSpec((tm, cdim), lambda i: (jnp.maximum(i - nb // 2, 0), 0))]
    ng = len(g_ops)

    def gradient(refs, k):
        g, at = gs[k], g_where[k]
        if not isinstance(g, tuple):
            return refs[at][...]
        if g[2] == 1:
            return jnp.concatenate([refs[at][...], refs[at + 1][...]], axis=1)
        return jnp.where(pl.program_id(0) < nb // 2, refs[at][...], refs[at + 1][...])

    def body(*refs):
        rest = refs[ng:]
        for k in range(n):
            g = gradient(refs, k)
            w = rest[k][...]
            m = ADAM_B1 * rest[n + k][...] + (1.0 - ADAM_B1) * g
            v = ADAM_B2 * rest[2 * n + k][...] + (1.0 - ADAM_B2) * (g * g)
            rest[3 * n + k][...] = g
            rest[4 * n + k][...] = -ADAM_LR * ((m * c1) / (jnp.sqrt(v * c2) + ADAM_EPS) + ADAM_WD * w)
            rest[5 * n + k][...] = m
            rest[6 * n + k][...] = v

    like_w = flat if ws[0].ndim == 2 else pl.BlockSpec((None, tm, cdim), lambda i: (0, i, 0))
    outs = _pcall(
        body, name=name, grid=(nb,), in_specs=g_specs + [like_w] * (3 * n), out_specs=[like_w] * (4 * n),
        out_shape=[jax.ShapeDtypeStruct(ws[0].shape, f32)] * (4 * n),
        compiler_params=_params(1),
    )(*g_ops, *ws, *ms, *vs)
    return outs[:n], outs[n:2 * n], outs[2 * n:3 * n], outs[3 * n:]


BIG = ["ffn1_w_gate", "ffn1_w_up", "ffn1_w_down", "ffn2_w_gate", "ffn2_w_up", "ffn2_w_down"]
SMALL = ["ln1_g", "ln1_b", "b_forget", "conv_w", "conv_b", "rg_wa", "rg_ba", "rg_wx", "rg_bx", "lru_lambda",
         "ln2_g", "ln2_b", "ln3_g", "ln3_b"]
WEIGHTS = ["ffn1_w_gate", "ffn1_w_up", "ffn1_w_down", "ln1_g", "ln1_b", "w_in", "b_forget", "conv_w", "conv_b",
           "rg_wa", "rg_ba", "rg_wx", "rg_bx", "lru_lambda", "w_out", "ln2_g", "ln2_b",
           "ffn2_w_gate", "ffn2_w_up", "ffn2_w_down", "ln3_g", "ln3_b"]


def _pack_small(parts):
    rows = []
    for n in SMALL:
        flat = parts[n].reshape(-1)
        pad = (-flat.shape[0]) % LANES
        rows.append(jnp.pad(flat, (0, pad)).reshape(-1, LANES))
    packed = jnp.concatenate(rows, axis=0)
    return jnp.pad(packed, ((0, (-packed.shape[0]) % 8), (0, 0)))


def _unpack_small(packed, shapes):
    out, r0 = {}, 0
    for n in SMALL:
        size = math.prod(shapes[n])
        nr = -(-size // LANES)
        out[n] = packed[r0:r0 + nr].reshape(-1)[:size].reshape(shapes[n])
        r0 += nr
    return out


def kernel(x, ffn1_w_gate, ffn1_w_up, ffn1_w_down, ln1_g, ln1_b, w_in, b_forget, conv_w, conv_b, rg_wa, rg_ba, rg_wx, rg_bx, lru_lambda, w_out, ln2_g, ln2_b, ffn2_w_gate, ffn2_w_up, ffn2_w_down, ln3_g, ln3_b, loss_target, m_ffn1_w_gate, m_ffn1_w_up, m_ffn1_w_down, m_ln1_g, m_ln1_b, m_w_in, m_b_forget, m_conv_w, m_conv_b, m_rg_wa, m_rg_ba, m_rg_wx, m_rg_bx, m_lru_lambda, m_w_out, m_ln2_g, m_ln2_b, m_ffn2_w_gate, m_ffn2_w_up, m_ffn2_w_down, m_ln3_g, m_ln3_b, v_ffn1_w_gate, v_ffn1_w_up, v_ffn1_w_down, v_ln1_g, v_ln1_b, v_w_in, v_b_forget, v_conv_w, v_conv_b, v_rg_wa, v_rg_ba, v_rg_wx, v_rg_bx, v_lru_lambda, v_w_out, v_ln2_g, v_ln2_b, v_ffn2_w_gate, v_ffn2_w_up, v_ffn2_w_down, v_ln3_g, v_ln3_b):
    args = dict(locals())
    w = {n: args[n] for n in WEIGHTS}
    mom = {n: args["m_" + n] for n in WEIGHTS}
    var = {n: args["v_" + n] for n in WEIGHTS}
    chip = 2 * lax.axis_index("x") + lax.axis_index("y")

    first = _SplitGather([w[n][0] for n in BIG[:2]], b_forget, "ag0")
    chip_ref = jnp.reshape(chip, (1,)).astype(jnp.int32)
    own_part = _ffn_up_part(x[0], w[BIG[0]][0], w[BIG[1]][0], chip_ref, None, first.token, name="ffn1_up_own")
    landed = first.forward([own_part[3]])
    fs = D_FF // N_SHARD
    full = dict(
        bfp=jnp.pad(b_forget, ((0, 0), (0, LANES - HEADS))),
        ln1_g=ln1_g, ln1_b=ln1_b, ln2_g=ln2_g, ln2_b=ln2_b, ln3_g=ln3_g, ln3_b=ln3_b,
        conv_b=conv_b, rg_wa=rg_wa[0], rg_wx=rg_wx[0], rg_ba=rg_ba[0], rg_bx=rg_bx[0], lam=lru_lambda,
    )
    cw_place = lax.dynamic_update_slice(jnp.zeros((8, LRU_W), f32), conv_w[0] * 0.5, (0, chip * (LRU_W // N_SHARD)))
    cw_full = _all_reduce_small(cw_place.reshape(-1, LANES), landed, name="ag_conv_w")
    full["conv_w"] = cw_full.reshape(8, LRU_W)[:CONV_K]

    hooks = _Overlap([w["ffn1_w_down"][0], w["w_in"][0], w["w_out"][0]], [w[n][0] for n in BIG[3:]], cw_full,
                     own_part, chip_ref)
    g1 = first.finish([hooks.start_token()])
    full.update(f1g=g1[0].reshape(N_SHARD, D_MODEL, fs), f1u=g1[1].reshape(N_SHARD, D_MODEL, fs))
    loss_rep, dx, g = _local_step(x[0], loss_target[0], full, hooks)

    token1 = hooks.ffn1_grads(g["f1"][1])
    red = _join_halves(hooks.reduced + hooks.mixer_reduced([token1]), name="rs_join_rest")
    grads = dict(zip(BIG[3:] + ["w_in", "w_out"], red))

    small_sum, loss = hooks.small_summed(red)
    small_shapes = {n: w[n].shape for n in SMALL}
    small_shapes["conv_w"] = (1, CONV_K, LRU_W)
    gs_red = _unpack_small(small_sum, small_shapes)
    gs_red["conv_w"] = lax.dynamic_slice(gs_red["conv_w"], (0, 0, chip * (LRU_W // N_SHARD)),
                                         (1, CONV_K, LRU_W // N_SHARD))
    grads.update(gs_red)

    delta, new_m, new_v = {}, {}, {}

    def adamw(names, name, **kw):
        g3, d, nm, nv = _adamw([grads[n] for n in names], [w[n] for n in names], [mom[n] for n in names],
                               [var[n] for n in names], name=name, **kw)
        for i, n in enumerate(names):
            grads[n], delta[n], new_m[n], new_v[n] = g3[i], d[i], nm[i], nv[i]

    adamw(BIG[3:], "adamw_ffn2", tm=128)
    adamw(["w_in"], "adamw_w_in")
    adamw(["w_out"], "adamw_w_out")
    shard_shapes = {n: w[n].shape for n in SMALL}
    _, d, nm, nv = _adamw([_pack_small({n: grads[n] for n in SMALL})], [_pack_small({n: w[n] for n in SMALL})],
                          [_pack_small({n: mom[n] for n in SMALL})], [_pack_small({n: var[n] for n in SMALL})],
                          name="adamw_small", tm=None)
    for dst, packed in ((delta, d[0]), (new_m, nm[0]), (new_v, nv[0])):
        dst.update(_unpack_small(packed, shard_shapes))

    worked = [new_v["ffn2_w_down"], new_v["w_in"], new_v["w_out"], nv[0]]
    ga, ua, da, gb, ub, db = _join_halves(hooks.ffn1_reduced(worked), name="rs_join_ffn1")
    grads.update(ffn1_w_gate=(ga, gb, 1), ffn1_w_up=(ua, ub, 1), ffn1_w_down=(da, db, 0))
    adamw(BIG[:3], "adamw_ffn1", tm=128)

    def shaped(tree, n):
        return tree[n].reshape(w[n].shape)

    return (loss, dx[None], *[shaped(grads, n) for n in WEIGHTS], *[shaped(delta, n) for n in WEIGHTS],
            *[shaped(new_m, n) for n in WEIGHTS], *[shaped(new_v, n) for n in WEIGHTS])
```

```python
import functools
import math

import jax
import jax.numpy as jnp
from jax import lax
from jax.experimental import pallas as pl
from jax.experimental.pallas import tpu as pltpu

f32 = jnp.float32
MXU_DTYPE = jnp.bfloat16
GRAD_DTYPE = jnp.bfloat16

D_MODEL = 1024
D_FF = 4096
N_SHARD = 4
N_DEV = 8
FOX_W = 512
LRU_W = 512
HEADS = 8
HEAD_DIM = 64
CONV_K = 4
IN_COLS = 2568
IN_SHARD = IN_COLS // N_SHARD
QKV_W = 3 * FOX_W
Z_PAD = 2688
CAST_COLS = 384
LANES = 128
LN_EPS = 1e-5
DN_ALPHA = 2.0 ** 0.25
LRU_C = 8.0
NEG_BIG = -1e30
VMEM_LIMIT = 56 * 1024 * 1024

ADAM_LR = 0.001
ADAM_B1 = 0.9
ADAM_B2 = 0.999
ADAM_EPS = 1e-08
ADAM_WD = 0.01
ADAM_STEP = 10


def _pcall(body, **kw):
    return pl.pallas_call(body, **kw)


def _params(n_grid, vmem=VMEM_LIMIT):
    return pltpu.CompilerParams(dimension_semantics=("arbitrary",) * n_grid, vmem_limit_bytes=vmem)


def _dot(a, b):
    return jnp.dot(a, b, preferred_element_type=f32)


def _dot_nt(a, b):
    return lax.dot_general(a, b, (((1,), (1,)), ((), ())), preferred_element_type=f32)


def _dot_tn(a, b):
    return lax.dot_general(a, b, (((0,), (0,)), ((), ())), preferred_element_type=f32)


def _sigmoid(x):
    return 1.0 / (1.0 + jnp.exp(-x))


def _layer_norm_stats(y):
    mu = jnp.mean(y, axis=-1, keepdims=True)
    yc = y - mu
    var = jnp.mean(yc * yc, axis=-1, keepdims=True)
    rstd = lax.rsqrt(var + LN_EPS)
    return yc * rstd, rstd


def _ln_backward(dy, xhat, rstd, gamma):
    dxhat = dy * gamma
    m1 = jnp.mean(dxhat, axis=-1, keepdims=True)
    m2 = jnp.mean(dxhat * xhat, axis=-1, keepdims=True)
    dyp = rstd * (dxhat - m1 - xhat * m2)
    return dyp, jnp.sum(dy * xhat, axis=0, keepdims=True), jnp.sum(dy, axis=0, keepdims=True)


def _ffn_fwd_loss(x, wg, wu, wd, ln_g, ln_b, target, *, name, tm=1024, tf=512):
    T = x.shape[0]
    tm = min(tm, T)
    tr = min(256, tm)
    fs = D_FF // N_SHARD
    cpf = fs // tf
    nf = D_FF // tf
    nt = T // tm

    def body(x_ref, wg_ref, wu_ref, wd_ref, g_ref, b_ref, t_ref,
             xb_ref, gact_ref, uact_ref, dyp_ref, dgam_ref, dbeta_ref, loss_ref, acc_ref):
        i = pl.program_id(0)
        f = pl.program_id(1)

        @pl.when(jnp.logical_and(i == 0, f == 0))
        def _():
            dgam_ref[...] = jnp.zeros_like(dgam_ref)
            dbeta_ref[...] = jnp.zeros_like(dbeta_ref)
            loss_ref[...] = jnp.zeros_like(loss_ref)

        @pl.when(f == 0)
        def _():
            xb_ref[...] = x_ref[...].astype(MXU_DTYPE)
            acc_ref[...] = jnp.zeros_like(acc_ref)

        xb = xb_ref[...]
        g = _dot(xb, wg_ref[...])
        u = _dot(xb, wu_ref[...])
        h = (g * _sigmoid(g)) * u
        gact_ref[...] = g.astype(gact_ref.dtype)
        uact_ref[...] = u.astype(uact_ref.dtype)
        acc_ref[...] += _dot(h.astype(MXU_DTYPE), wd_ref[...])

        @pl.when(f == nf - 1)
        def _():
            gamma = g_ref[...]

            def rows_chunk(r, carry):
                rows = pl.ds(pl.multiple_of(r * tr, tr), tr)
                xhat, rstd = _layer_norm_stats(DN_ALPHA * x_ref[rows, :] + 0.5 * acc_ref[rows, :])
                err = xhat * gamma + b_ref[...] - t_ref[rows, :]
                sq = jnp.sum(jnp.sum(err * err, axis=0, keepdims=True), axis=1, keepdims=True)
                loss_ref[...] += jnp.broadcast_to(sq * (0.5 / D_MODEL), loss_ref.shape)
                dyp, dgam, dbeta = _ln_backward(err * (1.0 / D_MODEL), xhat, rstd, gamma)
                dyp_ref[rows, :] = dyp
                dgam_ref[...] += dgam
                dbeta_ref[...] += dbeta
                return carry

            lax.fori_loop(0, tm // tr, rows_chunk, 0)

    row = lambda i, f: (i, 0)
    const = lambda i, f: (0, 0)
    tile = pl.BlockSpec((tm, tf), lambda i, f: (i, f))
    cols = pl.BlockSpec((None, D_MODEL, tf), lambda i, f: (f // cpf, 0, f % cpf))
    last = lambda i, f: (jnp.where(f == nf - 1, i, jnp.maximum(i - 1, 0)), 0)
    return _pcall(
        body, name=name, grid=(nt, nf),
        in_specs=[pl.BlockSpec((tm, D_MODEL), row), cols, cols,
                  pl.BlockSpec((None, tf, D_MODEL), lambda i, f: (f // cpf, f % cpf, 0)),
                  pl.BlockSpec((1, D_MODEL), const), pl.BlockSpec((1, D_MODEL), const),
                  pl.BlockSpec((tm, D_MODEL), last)],
        out_specs=[pl.BlockSpec((tm, D_MODEL), row), tile, tile, pl.BlockSpec((tm, D_MODEL), row),
                   pl.BlockSpec((1, D_MODEL), const), pl.BlockSpec((1, D_MODEL), const), pl.BlockSpec((1, LANES), const)],
        out_shape=[jax.ShapeDtypeStruct((T, D_MODEL), MXU_DTYPE), jax.ShapeDtypeStruct((T, D_FF), MXU_DTYPE),
                   jax.ShapeDtypeStruct((T, D_FF), MXU_DTYPE), jax.ShapeDtypeStruct((T, D_MODEL), f32),
                   jax.ShapeDtypeStruct((1, D_MODEL), f32), jax.ShapeDtypeStruct((1, D_MODEL), f32),
                   jax.ShapeDtypeStruct((1, LANES), f32)],
        scratch_shapes=[pltpu.VMEM((tm, D_MODEL), f32)],
        compiler_params=_params(2),
    )(x, wg, wu, wd, ln_g, ln_b, target)


def _ffn_up(x, wg, wu, after=None, *, name, tm=1024, tf=512):
    T = x.shape[0]
    tm = min(tm, T)
    cpf = (D_FF // N_SHARD) // tf
    nf = D_FF // tf
    extra = [] if after is None else [after]

    def body(x_ref, wg_ref, wu_ref, *refs):
        xb_ref, gact_ref, uact_ref, hact_ref = refs[len(extra):]

        @pl.when(pl.program_id(1) == 0)
        def _():
            xb_ref[...] = x_ref[...].astype(MXU_DTYPE)

        xb = xb_ref[...]
        g = _dot(xb, wg_ref[...])
        u = _dot(xb, wu_ref[...])
        gact_ref[...] = g.astype(gact_ref.dtype)
        uact_ref[...] = u.astype(uact_ref.dtype)
        hact_ref[...] = ((g * _sigmoid(g)) * u).astype(hact_ref.dtype)

    row = lambda i, f: (i, 0)
    tile = pl.BlockSpec((tm, tf), lambda i, f: (i, f))
    cols = pl.BlockSpec((None, D_MODEL, tf), lambda i, f: (f // cpf, 0, f % cpf))
    return _pcall(
        body, name=name, grid=(T // tm, nf),
        in_specs=[pl.BlockSpec((tm, D_MODEL), row), cols, cols] + [pl.BlockSpec(memory_space=pl.ANY)] * len(extra),
        out_specs=[pl.BlockSpec((tm, D_MODEL), row), tile, tile, tile],
        out_shape=[jax.ShapeDtypeStruct((T, D_MODEL), MXU_DTYPE)] + [jax.ShapeDtypeStruct((T, D_FF), MXU_DTYPE)] * 3,
        compiler_params=_params(2),
    )(x, wg, wu, *extra)


def _ffn_up_part(x, wg, wu, chip, prev, after, *, name, tm=1024, tf=512):
    T = x.shape[0]
    tm = min(tm, T)
    cpf = (D_FF // N_SHARD) // tf
    own = prev is None
    n_shards = 1 if own else N_SHARD - 1
    rest = ([] if own else list(prev)) + ([] if after is None else [after])

    def shard(f, c_ref):
        return c_ref[0] if own else (c_ref[0] + 1 + f // cpf) % N_SHARD

    def body(c_ref, x_ref, wg_ref, wu_ref, *refs):
        outs = refs[len(rest):]
        gact_ref, uact_ref, hact_ref = outs[-3:]
        xb = x_ref[...].astype(MXU_DTYPE)
        if own:
            @pl.when(pl.program_id(1) == 0)
            def _():
                outs[0][...] = xb

        g = _dot(xb, wg_ref[...].astype(MXU_DTYPE))
        u = _dot(xb, wu_ref[...].astype(MXU_DTYPE))
        gact_ref[...] = g.astype(gact_ref.dtype)
        uact_ref[...] = u.astype(uact_ref.dtype)
        hact_ref[...] = ((g * _sigmoid(g)) * u).astype(hact_ref.dtype)

    row = pl.BlockSpec((tm, D_MODEL), lambda i, f, c_ref: (i, 0))
    tile = pl.BlockSpec((tm, tf), lambda i, f, c_ref: (i, shard(f, c_ref) * cpf + f % cpf))
    if own:
        cols = pl.BlockSpec((D_MODEL, tf), lambda i, f, c_ref: (0, f))
    else:
        cols = pl.BlockSpec((None, D_MODEL, tf), lambda i, f, c_ref: (shard(f, c_ref), 0, f % cpf))
    wide = jax.ShapeDtypeStruct((T, D_FF), MXU_DTYPE)
    return list(_pcall(
        body, name=name,
        grid_spec=pltpu.PrefetchScalarGridSpec(
            num_scalar_prefetch=1, grid=(T // tm, n_shards * cpf),
            in_specs=[row, cols, cols] + [pl.BlockSpec(memory_space=pl.ANY)] * len(rest),
            out_specs=([row] if own else []) + [tile] * 3),
        out_shape=([jax.ShapeDtypeStruct((T, D_MODEL), MXU_DTYPE)] if own else []) + [wide] * 3,
        input_output_aliases={} if own else {4 + k: k for k in range(3)},
        compiler_params=_params(2),
    )(chip, x, wg, wu, *rest))


def _ffn_down_ln(x, hact, wd, ln_g, ln_b, *, name, tm=1024):
    T = x.shape[0]
    tm = min(tm, T)
    fs = D_FF // N_SHARD
    ks = 2
    nk = N_SHARD // ks

    def body(x_ref, h_ref, wd_ref, g_ref, b_ref, xhat_ref, xn_ref, rstd_ref, acc_ref):
        k = pl.program_id(1)

        @pl.when(k == 0)
        def _():
            acc_ref[...] = jnp.zeros_like(acc_ref)

        acc_ref[...] += _dot(h_ref[...], wd_ref[...].reshape(ks * fs, D_MODEL))

        @pl.when(k == nk - 1)
        def _():
            xhat, rstd = _layer_norm_stats(DN_ALPHA * x_ref[...] + 0.5 * acc_ref[...])
            xhat_ref[...] = xhat
            xn_ref[...] = (xhat * g_ref[...] + b_ref[...]).astype(xn_ref.dtype)
            rstd_ref[...] = jnp.broadcast_to(rstd, rstd_ref.shape)

    row = lambda i, k: (i, 0)
    vec = pl.BlockSpec((1, D_MODEL), lambda i, k: (0, 0))
    return _pcall(
        body, name=name, grid=(T // tm, nk),
        in_specs=[pl.BlockSpec((tm, D_MODEL), row), pl.BlockSpec((tm, ks * fs), lambda i, k: (i, k)),
                  pl.BlockSpec((ks, fs, D_MODEL), lambda i, k: (k, 0, 0)), vec, vec],
        out_specs=[pl.BlockSpec((tm, D_MODEL), row), pl.BlockSpec((tm, D_MODEL), row), pl.BlockSpec((tm, LANES), row)],
        out_shape=[jax.ShapeDtypeStruct((T, D_MODEL), f32), jax.ShapeDtypeStruct((T, D_MODEL), MXU_DTYPE),
                   jax.ShapeDtypeStruct((T, LANES), f32)],
        scratch_shapes=[pltpu.VMEM((tm, D_MODEL), f32)],
        compiler_params=_params(2),
    )(x, hact, wd, ln_g, ln_b)


def _ffn_bwd(dyp, xb, gact, uact, wg, wu, wd, after=None, *, name, tm=512, tf=512, part=None, dx_init=None):
    T = dyp.shape[0]
    tm = min(tm, T)
    fs = D_FF // N_SHARD
    cpf = fs // tf
    nt = T // tm
    nf = D_FF // tf if part is None else N_SHARD
    wf = fs if part is None else tf
    slab = (lambda f: f // cpf) if part is None else (lambda f: f)
    chunk = (lambda f: f % cpf) if part is None else (lambda f: part)
    extra = ([] if dx_init is None else [dx_init]) + ([] if after is None else [after])

    def body(dyp_ref, xb_ref, g_ref, u_ref, wg_ref, wu_ref, wd_ref, *refs):
        dx_hbm, dwg_ref, dwu_ref, dwd_ref, dx_sc, dwg_sc, dwu_sc, dwd_sc, sem = refs[len(extra):]
        f = pl.program_id(0)
        i = pl.program_id(1)
        rows = pl.ds(pl.multiple_of(i * tm, tm), tm)
        dyp_t = dyp_ref[...]
        dy = (0.5 * dyp_t).astype(MXU_DTYPE)

        @pl.when(i == 0)
        def _():
            dwg_sc[...] = jnp.zeros_like(dwg_sc)
            dwu_sc[...] = jnp.zeros_like(dwu_sc)
            dwd_sc[...] = jnp.zeros_like(dwd_sc)

        @pl.when(f == 0)
        def _():
            dx_sc[rows, :] = DN_ALPHA * dyp_t if dx_init is None else refs[0][...]

        g = g_ref[...].astype(f32)
        u = u_ref[...].astype(f32)
        sig = _sigmoid(g)
        silu = g * sig
        dh = _dot_nt(dy, wd_ref[...])
        dg = (dh * u * (sig * (1.0 + g * (1.0 - sig)))).astype(MXU_DTYPE)
        du = (dh * silu).astype(MXU_DTYPE)
        hb = (silu * u).astype(MXU_DTYPE)
        dx_sc[rows, :] += _dot_nt(dg, wg_ref[...]) + _dot_nt(du, wu_ref[...])
        xb_t = xb_ref[...]
        dwg_sc[...] += _dot_tn(xb_t, dg)
        dwu_sc[...] += _dot_tn(xb_t, du)
        dwd_sc[...] += _dot_tn(hb, dy)

        @pl.when(i == nt - 1)
        def _():
            dwg_ref[...] = dwg_sc[...].astype(dwg_ref.dtype)
            dwu_ref[...] = dwu_sc[...].astype(dwu_ref.dtype)
            dwd_ref[...] = dwd_sc[...].astype(dwd_ref.dtype)

        @pl.when(jnp.logical_and(f == nf - 1, i == nt - 1))
        def _():
            cp = pltpu.make_async_copy(dx_sc, dx_hbm, sem)
            cp.start()
            cp.wait()

    row = lambda f, i: (i, 0)
    return _pcall(
        body, name=name, grid=(nf, nt),
        in_specs=[
            pl.BlockSpec((tm, D_MODEL), row),
            pl.BlockSpec((tm, D_MODEL), row),
            pl.BlockSpec((tm, tf), lambda f, i: (i, slab(f) * cpf + chunk(f))),
            pl.BlockSpec((tm, tf), lambda f, i: (i, slab(f) * cpf + chunk(f))),
            pl.BlockSpec((None, D_MODEL, tf), lambda f, i: (slab(f), 0, chunk(f))),
            pl.BlockSpec((None, D_MODEL, tf), lambda f, i: (slab(f), 0, chunk(f))),
            pl.BlockSpec((None, tf, D_MODEL), lambda f, i: (slab(f), chunk(f), 0)),
        ] + ([] if dx_init is None else [pl.BlockSpec((tm, D_MODEL), row)])
        + ([] if after is None else [pl.BlockSpec(memory_space=pl.ANY)]),
        out_specs=[
            pl.BlockSpec(memory_space=pl.ANY),
            pl.BlockSpec((None, D_MODEL, tf), lambda f, i: (slab(f), 0, chunk(f) if part is None else 0)),
            pl.BlockSpec((None, D_MODEL, tf), lambda f, i: (slab(f), 0, chunk(f) if part is None else 0)),
            pl.BlockSpec((None, tf, D_MODEL), lambda f, i: (slab(f), chunk(f) if part is None else 0, 0)),
        ],
        out_shape=[
            jax.ShapeDtypeStruct((T, D_MODEL), f32),
            jax.ShapeDtypeStruct((N_SHARD, D_MODEL, wf), GRAD_DTYPE),
            jax.ShapeDtypeStruct((N_SHARD, D_MODEL, wf), GRAD_DTYPE),
            jax.ShapeDtypeStruct((N_SHARD, wf, D_MODEL), GRAD_DTYPE),
        ],
        scratch_shapes=[pltpu.VMEM((T, D_MODEL), f32), pltpu.VMEM((D_MODEL, tf), f32),
                        pltpu.VMEM((D_MODEL, tf), f32), pltpu.VMEM((tf, D_MODEL), f32),
                        pltpu.SemaphoreType.DMA],
        compiler_params=_params(2),
    )(dyp, xb, gact, uact, wg, wu, wd, *extra)


def _proj_in(xn, wp, bfp, *, name, tm=512):
    T = xn.shape[0]
    tm = min(tm, T)
    nt = T // tm

    def body(x_ref, w_ref, b_ref, lxg_ref, fg_ref, qa_ref, ka_ref, va_ref, carry):
        i = pl.program_id(0)

        @pl.when(i == 0)
        def _():
            carry[...] = jnp.zeros_like(carry)

        z = _dot(x_ref[...], w_ref[...])
        lxg_ref[...] = z[:, QKV_W:QKV_W + 2 * LRU_W]
        fg = z[:, QKV_W + 2 * LRU_W:] + b_ref[...]
        fg_ref[...] = fg
        ls = jnp.minimum(fg, 0.0) - jnp.log(1.0 + jnp.exp(-jnp.abs(fg)))
        r = lax.broadcasted_iota(jnp.int32, (tm, tm), 0)
        c = lax.broadcasted_iota(jnp.int32, (tm, tm), 1)
        cum = _tri_dot(jnp.where(r >= c, 1.0, 0.0).astype(jnp.bfloat16), ls) + carry[0:1, :]
        carry[...] = jnp.broadcast_to(cum[tm - 1:tm, :], carry.shape)

        lane = lax.broadcasted_iota(jnp.int32, (tm, LANES), 1)
        low = lane < HEAD_DIM
        ones_q = jnp.where(jnp.logical_and(lane >= AUX + 3, lane < AUX + 6), 1.0, 0.0)
        ones_k = jnp.where(jnp.logical_and(lane >= AUX, lane < AUX + 3), 1.0, 0.0)
        for j in range(HEADS // 2):
            pair = [z[:, t * FOX_W + j * LANES:t * FOX_W + (j + 1) * LANES] for t in range(3)]
            for odd in range(2):
                h = 2 * j + odd
                q, k, v = [_swap_lane_halves(a) if odd else a for a in pair]
                hi, mid, lo = [a.astype(f32) for a in _split3(jnp.broadcast_to(cum[:, h:h + 1], (tm, LANES)))]
                aux_q = jnp.where(lane == AUX, hi, jnp.where(lane == AUX + 1, mid, jnp.where(lane == AUX + 2, lo, ones_q)))
                aux_k = jnp.where(lane == AUX + 3, -hi,
                                  jnp.where(lane == AUX + 4, -mid, jnp.where(lane == AUX + 5, -lo, ones_k)))
                blk = slice(h * LANES, (h + 1) * LANES)
                qa_ref[:, blk] = jnp.where(low, q, aux_q).astype(qa_ref.dtype)
                ka_ref[:, blk] = jnp.where(low, k, aux_k).astype(ka_ref.dtype)
                va_ref[:, blk] = jnp.where(low, v, 1.0).astype(va_ref.dtype)

    row = lambda i: (i, 0)
    const = lambda i: (0, 0)
    return _pcall(
        body, name=name, grid=(nt,),
        in_specs=[pl.BlockSpec((tm, D_MODEL), row), pl.BlockSpec((D_MODEL, Z_PAD), const),
                  pl.BlockSpec((1, LANES), const)],
        out_specs=[pl.BlockSpec((tm, 2 * LRU_W), row), pl.BlockSpec((tm, LANES), row)]
        + [pl.BlockSpec((tm, HEADS * LANES), row)] * 3,
        out_shape=[jax.ShapeDtypeStruct((T, 2 * LRU_W), f32), jax.ShapeDtypeStruct((T, LANES), f32)]
        + [jax.ShapeDtypeStruct((T, HEADS * LANES), MXU_DTYPE)] * 3,
        scratch_shapes=[pltpu.VMEM((8, LANES), f32)],
        compiler_params=_params(1),
    )(xn, wp, bfp)


def _proj_in_bwd(dqa, dka, dva, dlxg, fgb, xn, dyp, wp, xhat, rstd, ln_g, *, name, tm=512):
    T = xn.shape[0]
    tm = min(tm, T)
    nt = T // tm

    def body(dq_ref, dk_ref, dv_ref, dl_ref, fg_ref, x_ref, dyp_ref, w_ref, xhat_ref, rstd_ref, g_ref,
             dpre_ref, dw_hbm, dgam_ref, dbeta_ref, dbf_ref, dw_sc, dw_out, carry, sem):
        i = pl.program_id(0)

        @pl.when(i == 0)
        def _():
            dw_sc[...] = jnp.zeros_like(dw_sc)
            dgam_ref[...] = jnp.zeros_like(dgam_ref)
            dbeta_ref[...] = jnp.zeros_like(dbeta_ref)
            dbf_ref[...] = jnp.zeros_like(dbf_ref)
            carry[...] = jnp.zeros_like(carry)

        lane = lax.broadcasted_iota(jnp.int32, (tm, LANES), 1)
        dc = jnp.zeros((tm, LANES), f32)
        for h in range(HEADS):
            row_sum = dq_ref[:, h * LANES + AUX:h * LANES + AUX + 1]
            col_sum = dk_ref[:, h * LANES + AUX + 3:h * LANES + AUX + 4]
            dc = jnp.where(lane == h, jnp.broadcast_to(row_sum - col_sum, (tm, LANES)), dc)
        r = lax.broadcasted_iota(jnp.int32, (tm, tm), 0)
        c = lax.broadcasted_iota(jnp.int32, (tm, tm), 1)
        dls = _tri_dot(jnp.where(c >= r, 1.0, 0.0).astype(jnp.bfloat16), dc) + carry[0:1, :]
        carry[...] = jnp.broadcast_to(dls[0:1, :], carry.shape)
        dfg = dls * _sigmoid(-fg_ref[...])
        dbf_ref[...] += jnp.sum(dfg, axis=0, keepdims=True)

        low = _low_lanes((tm, LANES))

        def packed(ref):
            pairs = [jnp.where(low, ref[:, (2 * j) * LANES:(2 * j + 1) * LANES],
                               _swap_lane_halves(ref[:, (2 * j + 1) * LANES:(2 * j + 2) * LANES]))
                     for j in range(HEADS // 2)]
            return jnp.concatenate(pairs, axis=1).astype(MXU_DTYPE)

        dz = jnp.concatenate(
            [packed(dq_ref), packed(dk_ref), packed(dv_ref),
             dl_ref[...].astype(MXU_DTYPE), dfg.astype(MXU_DTYPE)], axis=1)
        dx = DN_ALPHA * dyp_ref[...] + _dot_nt(dz, w_ref[...])
        dpre, dgam, dbeta = _ln_backward(dx, xhat_ref[...], rstd_ref[:, 0:1], g_ref[...])
        dpre_ref[...] = dpre
        dgam_ref[...] += dgam
        dbeta_ref[...] += dbeta
        dw_sc[...] += _dot_tn(x_ref[...], dz)

        @pl.when(i == nt - 1)
        def _():
            dw_sc[:, :FOX_W] = dw_sc[:, :FOX_W] * (1.0 / math.sqrt(HEAD_DIM))
            for c0 in range(0, Z_PAD, CAST_COLS):
                dw_out[:, c0:c0 + CAST_COLS] = dw_sc[:, c0:c0 + CAST_COLS].astype(dw_out.dtype)
            cp = pltpu.make_async_copy(dw_out, dw_hbm, sem)
            cp.start()
            cp.wait()

    row = lambda i: (nt - 1 - i, 0)
    const = lambda i: (0, 0)
    return _pcall(
        body, name=name, grid=(nt,),
        in_specs=[pl.BlockSpec((tm, HEADS * LANES), row), pl.BlockSpec((tm, HEADS * LANES), row),
                  pl.BlockSpec((tm, HEADS * LANES), row),
                  pl.BlockSpec((tm, 2 * LRU_W), row), pl.BlockSpec((tm, LANES), row),
                  pl.BlockSpec((tm, D_MODEL), row), pl.BlockSpec((tm, D_MODEL), row),
                  pl.BlockSpec((D_MODEL, Z_PAD), const),
                  pl.BlockSpec((tm, D_MODEL), row), pl.BlockSpec((tm, LANES), row), pl.BlockSpec((1, D_MODEL), const)],
        out_specs=[pl.BlockSpec((tm, D_MODEL), row), pl.BlockSpec(memory_space=pl.ANY),
                   pl.BlockSpec((1, D_MODEL), const), pl.BlockSpec((1, D_MODEL), const), pl.BlockSpec((1, LANES), const)],
        out_shape=[jax.ShapeDtypeStruct((T, D_MODEL), f32), jax.ShapeDtypeStruct((D_MODEL, Z_PAD), GRAD_DTYPE),
                   jax.ShapeDtypeStruct((1, D_MODEL), f32), jax.ShapeDtypeStruct((1, D_MODEL), f32),
                   jax.ShapeDtypeStruct((1, LANES), f32)],
        scratch_shapes=[pltpu.VMEM((D_MODEL, Z_PAD), f32), pltpu.VMEM((D_MODEL, Z_PAD), GRAD_DTYPE),
                        pltpu.VMEM((8, LANES), f32), pltpu.SemaphoreType.DMA],
        compiler_params=_params(1),
    )(dqa, dka, dva, dlxg, fgb, xn, dyp, wp, xhat, rstd, ln_g)


def _split3(x):
    hi = x.astype(jnp.bfloat16)
    r1 = x - hi.astype(f32)
    mid = r1.astype(jnp.bfloat16)
    lo = (r1 - mid.astype(f32)).astype(jnp.bfloat16)
    return hi, mid, lo


def _tri_dot(tri, x):
    hi, mid, lo = _split3(x)
    return _dot(tri, hi) + _dot(tri, mid) + _dot(tri, lo)


FOX_PAD = HEADS * LANES
AUX = HEAD_DIM


def _low_lanes(shape):
    return lax.broadcasted_iota(jnp.int32, shape, 1) < HEAD_DIM


def _swap_lane_halves(x):
    return pltpu.roll(x, HEAD_DIM, 1)


def _future_keys(tq, tk):
    r = lax.broadcasted_iota(jnp.int32, (tq, tk), 0)
    c = lax.broadcasted_iota(jnp.int32, (tq, tk), 1)
    return c > r


def _causal_steps(nq, key_major):
    if key_major:
        pairs = [(qi, ki) for ki in range(nq) for qi in range(ki, nq)]
    else:
        pairs = [(qi, ki) for qi in range(nq) for ki in range(qi + 1)]
    return (jnp.asarray([p[0] for p in pairs], jnp.int32), jnp.asarray([p[1] for p in pairs], jnp.int32))


def _fox_fwd(qa, ka, va, *, name, tq=512, hps=8):
    T = qa.shape[0]
    tq = min(tq, T)
    tk = tq
    nq = T // tq
    rep = tk // LANES
    qi_tab, ki_tab = _causal_steps(nq, key_major=False)

    def body(qi_ref, ki_ref, qa_ref, ka_ref, va_ref, o_ref, lse_ref, m_sc, acc_sc):
        t = pl.program_id(1)
        qi = qi_ref[t]
        ki = ki_ref[t]

        @pl.when(ki == 0)
        def _():
            m_sc[...] = jnp.full_like(m_sc, NEG_BIG)
            acc_sc[...] = jnp.zeros_like(acc_sc)

        def tile(diagonal):
            for h in range(hps):
                blk = slice(h * LANES, (h + 1) * LANES)
                s = _dot_nt(qa_ref[:, blk], ka_ref[:, blk])
                if diagonal:
                    s = jnp.where(_future_keys(tq, tk), NEG_BIG, s)
                m_prev = m_sc[h]
                m_new = jnp.maximum(m_prev, jnp.max(s, axis=1, keepdims=True))
                p = jnp.exp(s - jnp.tile(m_new, (1, rep)))
                acc_sc[h] = jnp.exp(m_prev - m_new) * acc_sc[h] + _dot(p.astype(MXU_DTYPE), va_ref[:, blk])
                m_sc[h] = m_new

        @pl.when(ki < qi)
        def _():
            tile(False)

        @pl.when(ki == qi)
        def _():
            tile(True)
            low = _low_lanes((tq, LANES))
            outs = []
            for h in range(hps):
                acc = acc_sc[h]
                den = _swap_lane_halves(acc)
                outs.append(acc / den)
                lse_ref[h] = m_sc[h] + jnp.log(jnp.where(low, den, acc))
            for p in range(hps // 2):
                o_ref[:, p * LANES:(p + 1) * LANES] = jnp.where(low, outs[2 * p], _swap_lane_halves(outs[2 * p + 1]))

    pair = hps * LANES
    return _pcall(
        body, name=name,
        grid_spec=pltpu.PrefetchScalarGridSpec(
            num_scalar_prefetch=2, grid=(HEADS // hps, qi_tab.shape[0]),
            in_specs=[
                pl.BlockSpec((tq, pair), lambda j, t, qi_ref, ki_ref: (qi_ref[t], j)),
                pl.BlockSpec((tk, pair), lambda j, t, qi_ref, ki_ref: (ki_ref[t], j)),
                pl.BlockSpec((tk, pair), lambda j, t, qi_ref, ki_ref: (ki_ref[t], j)),
            ],
            out_specs=[pl.BlockSpec((tq, pair // 2), lambda j, t, qi_ref, ki_ref: (qi_ref[t], j)),
                       pl.BlockSpec((hps, tq, LANES), lambda j, t, qi_ref, ki_ref: (j, qi_ref[t], 0))],
            scratch_shapes=[pltpu.VMEM((hps, tq, LANES), f32)] * 2),
        out_shape=[jax.ShapeDtypeStruct((T, FOX_W), f32), jax.ShapeDtypeStruct((HEADS, T, LANES), f32)],
        compiler_params=_params(2),
    )(qi_tab, ki_tab, qa, ka, va)


def _fox_bwd(qa, ka, va, doa, lse, drep, *, name, tq=512, hps=8):
    T = qa.shape[0]
    tq = min(tq, T)
    tk = tq
    nq = T // tq
    rep = tk // LANES
    qi_tab, ki_tab = _causal_steps(nq, key_major=True)

    def body(qi_ref, ki_ref, qa_ref, ka_ref, va_ref, doa_ref, lse_ref, d_ref, dqa_ref, dka_ref, dva_ref, dk_sc, dv_sc):
        t = pl.program_id(1)
        qi = qi_ref[t]
        ki = ki_ref[t]
        rows = pl.ds(pl.multiple_of(qi * tq, tq), tq)

        @pl.when(t == 0)
        def _():
            dqa_ref[...] = jnp.zeros_like(dqa_ref)

        @pl.when(qi == ki)
        def _():
            dk_sc[...] = jnp.zeros_like(dk_sc)
            dv_sc[...] = jnp.zeros_like(dv_sc)

        def tile(diagonal):
            for h in range(hps):
                blk = slice(h * LANES, (h + 1) * LANES)
                qh, kh, doh = qa_ref[:, blk], ka_ref[:, blk], doa_ref[:, blk]
                p = jnp.exp(_dot_nt(qh, kh) - jnp.tile(lse_ref[h], (1, rep)))
                if diagonal:
                    p = jnp.where(_future_keys(tq, tk), 0.0, p)
                dp = _dot_nt(doh, va_ref[:, blk])
                ds = (p * (dp - jnp.tile(d_ref[h], (1, rep)))).astype(MXU_DTYPE)
                dv_sc[h] += _dot_tn(p.astype(MXU_DTYPE), doh)
                dk_sc[h] += _dot_tn(ds, qh)
                dqa_ref[rows, blk] += _dot(ds, kh)

        @pl.when(qi > ki)
        def _():
            tile(False)

        @pl.when(qi == ki)
        def _():
            tile(True)

        @pl.when(qi == nq - 1)
        def _():
            for h in range(hps):
                blk = slice(h * LANES, (h + 1) * LANES)
                dka_ref[:, blk] = dk_sc[h]
                dva_ref[:, blk] = dv_sc[h]

    pair = hps * LANES
    q_blk = lambda j, t, qi_ref, ki_ref: (qi_ref[t], j)
    k_blk = lambda j, t, qi_ref, ki_ref: (ki_ref[t], j)
    stat = pl.BlockSpec((hps, tq, LANES), lambda j, t, qi_ref, ki_ref: (j, qi_ref[t], 0))
    return _pcall(
        body, name=name,
        grid_spec=pltpu.PrefetchScalarGridSpec(
            num_scalar_prefetch=2, grid=(HEADS // hps, qi_tab.shape[0]),
            in_specs=[pl.BlockSpec((tq, pair), q_blk), pl.BlockSpec((tk, pair), k_blk), pl.BlockSpec((tk, pair), k_blk),
                      pl.BlockSpec((tq, pair), q_blk), stat, stat],
            out_specs=[pl.BlockSpec((T, pair), lambda j, t, qi_ref, ki_ref: (0, j)),
                       pl.BlockSpec((tk, pair), k_blk), pl.BlockSpec((tk, pair), k_blk)],
            scratch_shapes=[pltpu.VMEM((hps, tk, LANES), f32)] * 2),
        out_shape=[jax.ShapeDtypeStruct((T, FOX_PAD), f32)] * 3,
        compiler_params=_params(2),
    )(qi_tab, ki_tab, qa, ka, va, doa, lse, drep)


GELU_C = math.sqrt(2.0 / math.pi)
GELU_A = 0.044715


def _gelu(x):
    t = jnp.tanh(GELU_C * (x + GELU_A * x * x * x))
    return 0.5 * x * (1.0 + t), t


def _gelu_grad(x, t):
    return 0.5 * (1.0 + t) + 0.5 * x * (1.0 - t * t) * GELU_C * (1.0 + 3.0 * GELU_A * x * x)


EXPM1_SERIES_BELOW = 0.25


def _expm1(x, e):
    series = x * (1.0 + x * (1 / 2 + x * (1 / 6 + x * (1 / 24 + x * (1 / 120 + x * (1 / 720))))))
    return jnp.where(x > -EXPM1_SERIES_BELOW, series, e - 1.0)


def _softplus_neg(lam):
    return jnp.maximum(-lam, 0.0) + jnp.log(1.0 + jnp.exp(-jnp.abs(lam)))


def _lru_gates(u, wab_ref, bab_ref, lam_ref):
    pre = _dot(u.astype(MXU_DTYPE), wab_ref[...]) + bab_ref[...]
    r = _sigmoid(pre[:, :LRU_W])
    gi = _sigmoid(pre[:, LRU_W:])
    sp = _softplus_neg(lam_ref[...])
    log_a = -LRU_C * r * sp
    a = jnp.exp(log_a)
    s = jnp.sqrt(-_expm1(2.0 * log_a, a * a))
    return r, gi, sp, a, s


def _lru_fwd(lxg, conv_w, conv_b, wab, bab, lam, *, name, tc=512):
    T = lxg.shape[0]
    tc = min(tc, T)
    nc = T // tc

    def body(lx_ref, lg_ref, cw_ref, cb_ref, wab_ref, bab_ref, lam_ref,
             out_ref, u_ref, hs_ref, gates_ref, ext, a_sc, b_sc, h_sc):
        i = pl.program_id(0)

        @pl.when(i == 0)
        def _():
            ext[0:8, :] = jnp.zeros((8, LRU_W), f32)
            h_sc[...] = jnp.zeros_like(h_sc)

        ext[8:, :] = lx_ref[...]
        u = cb_ref[...] + cw_ref[0:1, :] * ext[pl.ds(5, tc), :]
        for k in range(1, CONV_K):
            u = u + cw_ref[k:k + 1, :] * ext[pl.ds(5 + k, tc), :]
        ext[0:8, :] = ext[tc:tc + 8, :]
        u_ref[...] = u
        r, gi, sp, a, s = _lru_gates(u, wab_ref, bab_ref, lam_ref)
        for n, gate in enumerate((r, gi, a, s)):
            gates_ref[:, n * LRU_W:(n + 1) * LRU_W] = gate
        a_sc[...] = a
        b_sc[...] = s * (gi * u)

        def step(t, h):
            h = a_sc[pl.ds(t, 1), :] * h + b_sc[pl.ds(t, 1), :]
            hs_ref[pl.ds(t, 1), :] = h
            return h

        h = lax.fori_loop(0, tc, step, h_sc[0:1, :], unroll=8)
        h_sc[...] = jnp.broadcast_to(h, h_sc.shape)
        gel, _ = _gelu(lg_ref[...])
        out_ref[...] = gel * hs_ref[...]

    row = lambda i: (i, 0)
    const = lambda i: (0, 0)
    return _pcall(
        body, name=name, grid=(nc,),
        in_specs=[pl.BlockSpec((tc, LRU_W), row), pl.BlockSpec((tc, LRU_W), lambda i: (i, 1)),
                  pl.BlockSpec((CONV_K, LRU_W), const), pl.BlockSpec((1, LRU_W), const),
                  pl.BlockSpec((LRU_W, 2 * LRU_W), const), pl.BlockSpec((1, 2 * LRU_W), const),
                  pl.BlockSpec((1, LRU_W), const)],
        out_specs=[pl.BlockSpec((tc, LRU_W), row)] * 3 + [pl.BlockSpec((tc, 4 * LRU_W), row)],
        out_shape=[jax.ShapeDtypeStruct((T, LRU_W), f32)] * 3 + [jax.ShapeDtypeStruct((T, 4 * LRU_W), f32)],
        scratch_shapes=[pltpu.VMEM((tc + 8, LRU_W), f32), pltpu.VMEM((tc, LRU_W), f32),
                        pltpu.VMEM((tc, LRU_W), f32), pltpu.VMEM((8, LRU_W), f32)],
        compiler_params=_params(1),
    )(lxg, lxg, conv_w, conv_b, wab, bab, lam)


def _lru_bwd(dlru, lxg, u, hs, gates, conv_w, wab, lam, *, name, tc=512):
    T = lxg.shape[0]
    tc = min(tc, T)
    nc = T // tc
    bp = tc // 8

    def body(dl_ref, lx_ref, lxp_ref, lg_ref, u_ref, hs_ref, hsp_ref, gates_ref, cw_ref, wab_ref, lam_ref,
             dlxg_ref, dwab_ref, dbab_ref, dcw_ref, dcb_ref, dlam_ref,
             dh_sc, a_sc, ext, du_ext, carry):
        i = pl.program_id(0)
        first_chunk = i == nc - 1

        @pl.when(i == 0)
        def _():
            dwab_ref[...] = jnp.zeros_like(dwab_ref)
            dbab_ref[...] = jnp.zeros_like(dbab_ref)
            dcw_ref[...] = jnp.zeros_like(dcw_ref)
            dcb_ref[...] = jnp.zeros_like(dcb_ref)
            dlam_ref[...] = jnp.zeros_like(dlam_ref)
            carry[...] = jnp.zeros_like(carry)
            du_ext[tc:tc + 8, :] = jnp.zeros((8, LRU_W), f32)

        lg = lg_ref[...]
        gel, th = _gelu(lg)
        dl = dl_ref[...]
        hs = hs_ref[...]
        dlg = dl * hs * _gelu_grad(lg, th)
        u = u_ref[...]
        r, gi, a, s = [gates_ref[:, n * LRU_W:(n + 1) * LRU_W] for n in range(4)]
        sp = _softplus_neg(lam_ref[...])
        a_sc[...] = a
        dh_sc[...] = dl * gel

        def step(k, c):
            t = tc - 1 - k
            dh = dh_sc[pl.ds(t, 1), :] + c
            dh_sc[pl.ds(t, 1), :] = dh
            return a_sc[pl.ds(t, 1), :] * dh

        c = lax.fori_loop(0, tc, step, carry[0:1, :], unroll=8)
        carry[...] = jnp.broadcast_to(c, carry.shape)

        ext[0:8, :] = jnp.where(first_chunk, 0.0, hsp_ref[...])
        ext[8:, :] = hs
        hprev = ext[pl.ds(7, tc), :]
        dh = dh_sc[...]
        da = dh * hprev
        giu = gi * u
        dla = da * a - (dh * giu) * (a * a / s)
        dgi = dh * s * u
        du = dh * s * gi
        dr = dla * (-LRU_C * sp)
        dlam_ref[...] += jnp.sum(dla * (-LRU_C * r), axis=0, keepdims=True) * (-_sigmoid(-lam_ref[...]))
        dpre = jnp.concatenate([dr * r * (1.0 - r), dgi * gi * (1.0 - gi)], axis=1)
        dpre_b = dpre.astype(MXU_DTYPE)
        du = du + _dot_nt(dpre_b, wab_ref[...])
        dwab_ref[...] += _dot_tn(u.astype(MXU_DTYPE), dpre_b)
        dbab_ref[...] += jnp.sum(dpre, axis=0, keepdims=True)
        dcb_ref[...] += jnp.sum(du, axis=0, keepdims=True)

        du_ext[0:tc, :] = du
        dlx = cw_ref[0:1, :] * du_ext[pl.ds(3, tc), :]
        for k in range(1, CONV_K):
            dlx = dlx + cw_ref[k:k + 1, :] * du_ext[pl.ds(3 - k, tc), :]
        du_ext[tc:tc + 8, :] = du_ext[0:8, :]
        ext[0:8, :] = jnp.where(first_chunk, 0.0, lxp_ref[...])
        ext[8:, :] = lx_ref[...]
        for k in range(CONV_K):
            dcw_ref[k:k + 1, :] += jnp.sum(du * ext[pl.ds(5 + k, tc), :], axis=0, keepdims=True)
        dlxg_ref[:, :LRU_W] = dlx.astype(dlxg_ref.dtype)
        dlxg_ref[:, LRU_W:] = dlg.astype(dlxg_ref.dtype)

    rev = lambda i: (nc - 1 - i, 0)
    prev8 = lambda i: (jnp.maximum((nc - 1 - i) * bp - 1, 0), 0)
    const = lambda i: (0, 0)
    return _pcall(
        body, name=name, grid=(nc,),
        in_specs=[
            pl.BlockSpec((tc, LRU_W), rev),
            pl.BlockSpec((tc, LRU_W), rev),
            pl.BlockSpec((8, LRU_W), prev8),
            pl.BlockSpec((tc, LRU_W), lambda i: (nc - 1 - i, 1)),
            pl.BlockSpec((tc, LRU_W), rev),
            pl.BlockSpec((tc, LRU_W), rev),
            pl.BlockSpec((8, LRU_W), prev8),
            pl.BlockSpec((tc, 4 * LRU_W), rev),
            pl.BlockSpec((CONV_K, LRU_W), const),
            pl.BlockSpec((LRU_W, 2 * LRU_W), const),
            pl.BlockSpec((1, LRU_W), const),
        ],
        out_specs=[
            pl.BlockSpec((tc, 2 * LRU_W), rev),
            pl.BlockSpec((LRU_W, 2 * LRU_W), const),
            pl.BlockSpec((1, 2 * LRU_W), const),
            pl.BlockSpec((8, LRU_W), const),
            pl.BlockSpec((1, LRU_W), const),
            pl.BlockSpec((1, LRU_W), const),
        ],
        out_shape=[
            jax.ShapeDtypeStruct((T, 2 * LRU_W), MXU_DTYPE),
            jax.ShapeDtypeStruct((LRU_W, 2 * LRU_W), f32),
            jax.ShapeDtypeStruct((1, 2 * LRU_W), f32),
            jax.ShapeDtypeStruct((8, LRU_W), f32),
            jax.ShapeDtypeStruct((1, LRU_W), f32),
            jax.ShapeDtypeStruct((1, LRU_W), f32),
        ],
        scratch_shapes=[pltpu.VMEM((tc, LRU_W), f32), pltpu.VMEM((tc, LRU_W), f32),
                        pltpu.VMEM((tc + 8, LRU_W), f32), pltpu.VMEM((tc + 8, LRU_W), f32),
                        pltpu.VMEM((8, LRU_W), f32)],
        compiler_params=_params(1),
    )(dlru, lxg, lxg, lxg, u, hs, hs, gates, conv_w, wab, lam)


def _mix_out(fox, lru, wo, xhat1, g1, b1, g2, b2, *, name, tm=512):
    T = fox.shape[0]
    tm = min(tm, T)
    nt = T // tm

    def body(fox_ref, lru_ref, wo_ref, xh_ref, g1_ref, b1_ref, g2_ref, b2_ref, xhat_ref, xn_ref, rstd_ref):
        mix = _dot(fox_ref[...].astype(MXU_DTYPE), wo_ref[:FOX_W, :])
        mix = mix + _dot(lru_ref[...].astype(MXU_DTYPE), wo_ref[FOX_W:, :])
        x1 = xh_ref[...] * g1_ref[...] + b1_ref[...]
        xhat, rstd = _layer_norm_stats(DN_ALPHA * x1 + mix)
        xhat_ref[...] = xhat
        xn_ref[...] = xhat * g2_ref[...] + b2_ref[...]
        rstd_ref[...] = jnp.broadcast_to(rstd, rstd_ref.shape)

    row = lambda i: (i, 0)
    const = lambda i: (0, 0)
    vec = pl.BlockSpec((1, D_MODEL), const)
    return _pcall(
        body, name=name, grid=(nt,),
        in_specs=[pl.BlockSpec((tm, FOX_W), row), pl.BlockSpec((tm, LRU_W), row),
                  pl.BlockSpec((D_MODEL, D_MODEL), const), pl.BlockSpec((tm, D_MODEL), row), vec, vec, vec, vec],
        out_specs=[pl.BlockSpec((tm, D_MODEL), row), pl.BlockSpec((tm, D_MODEL), row),
                   pl.BlockSpec((tm, LANES), row)],
        out_shape=[jax.ShapeDtypeStruct((T, D_MODEL), f32), jax.ShapeDtypeStruct((T, D_MODEL), f32),
                   jax.ShapeDtypeStruct((T, LANES), f32)],
        compiler_params=_params(1),
    )(fox, lru, wo, xhat1, g1, b1, g2, b2)


def _mix_out_bwd(dy, xhat, rstd, ln_g, fox, lru, wo, *, name, tm=512):
    T = fox.shape[0]
    tm = min(tm, T)
    nt = T // tm

    def body(dy_ref, xhat_ref, rstd_ref, g_ref, fox_ref, lru_ref, wo_ref,
             dyp_ref, dgam_ref, dbeta_ref, dlru_ref, dwo_ref, d_ref, doa_ref):
        i = pl.program_id(0)

        @pl.when(i == 0)
        def _():
            dwo_ref[...] = jnp.zeros_like(dwo_ref)
            dgam_ref[...] = jnp.zeros_like(dgam_ref)
            dbeta_ref[...] = jnp.zeros_like(dbeta_ref)

        dyp, dgam, dbeta = _ln_backward(dy_ref[...], xhat_ref[...], rstd_ref[:, 0:1], g_ref[...])
        dyp_ref[...] = dyp
        dgam_ref[...] += dgam
        dbeta_ref[...] += dbeta
        dmix = dyp.astype(MXU_DTYPE)
        dcat = _dot_nt(dmix, wo_ref[...])
        dlru_ref[...] = dcat[:, FOX_W:]
        low = _low_lanes((tm, LANES))
        for j in range(HEADS // 2):
            do2 = dcat[:, j * LANES:(j + 1) * LANES].astype(MXU_DTYPE).astype(f32)
            prod = do2 * fox_ref[:, j * LANES:(j + 1) * LANES]
            for odd in range(2):
                h = 2 * j + odd
                mine = jnp.where(low, _swap_lane_halves(prod) if odd else prod, 0.0)
                d_ref[h] = jnp.broadcast_to(jnp.sum(mine, axis=1, keepdims=True), (tm, LANES))
                doh = jnp.where(low, _swap_lane_halves(do2) if odd else do2, 0.0)
                doa_ref[:, h * LANES:(h + 1) * LANES] = doh.astype(doa_ref.dtype)
        dwo_ref[:FOX_W, :] += _dot_tn(fox_ref[...].astype(MXU_DTYPE), dmix)
        dwo_ref[FOX_W:, :] += _dot_tn(lru_ref[...].astype(MXU_DTYPE), dmix)

    row = lambda i: (i, 0)
    const = lambda i: (0, 0)
    return _pcall(
        body, name=name, grid=(nt,),
        in_specs=[pl.BlockSpec((tm, D_MODEL), row), pl.BlockSpec((tm, D_MODEL), row), pl.BlockSpec((tm, LANES), row),
                  pl.BlockSpec((1, D_MODEL), const),
                  pl.BlockSpec((tm, FOX_W), row), pl.BlockSpec((tm, LRU_W), row),
                  pl.BlockSpec((D_MODEL, D_MODEL), const)],
        out_specs=[pl.BlockSpec((tm, D_MODEL), row), pl.BlockSpec((1, D_MODEL), const), pl.BlockSpec((1, D_MODEL), const),
                   pl.BlockSpec((tm, LRU_W), row), pl.BlockSpec((D_MODEL, D_MODEL), const),
                   pl.BlockSpec((HEADS, tm, LANES), lambda i: (0, i, 0)), pl.BlockSpec((tm, HEADS * LANES), row)],
        out_shape=[jax.ShapeDtypeStruct((T, D_MODEL), f32), jax.ShapeDtypeStruct((1, D_MODEL), f32),
                   jax.ShapeDtypeStruct((1, D_MODEL), f32),
                   jax.ShapeDtypeStruct((T, LRU_W), f32), jax.ShapeDtypeStruct((D_MODEL, D_MODEL), f32),
                   jax.ShapeDtypeStruct((HEADS, T, LANES), f32), jax.ShapeDtypeStruct((T, HEADS * LANES), MXU_DTYPE)],
        compiler_params=_params(1),
    )(dy, xhat, rstd, ln_g, fox, lru, wo)


def make_wp(w_in):
    scale = jnp.concatenate([jnp.full((FOX_W,), 1.0 / math.sqrt(HEAD_DIM), w_in.dtype),
                             jnp.ones((IN_COLS - FOX_W,), w_in.dtype)])
    return jnp.pad(w_in * scale[None, :], ((0, 0), (0, Z_PAD - IN_COLS)))


def _block_diag(w):
    eye = jnp.eye(HEADS, dtype=w.dtype)
    return jnp.einsum("hij,hg->higj", w, eye).reshape(LRU_W, LRU_W)


def _block_diag_extract(m):
    m4 = m.reshape(HEADS, HEAD_DIM, HEADS, HEAD_DIM)
    return jnp.stack([m4[h, :, h, :] for h in range(HEADS)])


class _NoOverlap:
    def ffn1_up(self, x, w):
        return _ffn_up(x, w["f1g"], w["f1u"], name="ffn1_up")

    def late_weights(self, w, after):
        return dict(f1d=w["f1d"], wp=w["wp"], wo=w["wo"])

    def after_attention(self, after):
        return None

    def ffn2_weights(self, w, after):
        return w["f2g"], w["f2u"], w["f2d"]

    def ffn2_grads(self, grads):
        return None

    def ffn1_grads(self, grads):
        return None

    def mixer_grads(self, dwp, dwo, small, loss):
        return None

    def before_ffn1_bwd(self, after):
        return None


def _tied(a, token):
    return a if token is None else a + token[0, 0]


def _local_step(x, target, w, hooks=None):
    hooks = hooks or _NoOverlap()
    bfp = w["bfp"]
    wab = jnp.concatenate([_block_diag(w["rg_wa"]), _block_diag(w["rg_wx"])], axis=1).astype(MXU_DTYPE)
    bab = jnp.concatenate([w["rg_ba"].reshape(1, LRU_W), w["rg_bx"].reshape(1, LRU_W)], axis=1)

    xb0, g1a, u1a, h1a = hooks.ffn1_up(x, w)
    late = hooks.late_weights(w, [h1a])
    f1d, wp, wo = late["f1d"], late["wp"], late["wo"]
    xhat1, xn1, rstd1 = _ffn_down_ln(x, h1a, f1d, w["ln1_g"], w["ln1_b"], name="ffn1_down")
    lxg, fgb, qa, ka, va = _proj_in(xn1, wp, bfp, name="proj_in")
    fox, lse = _fox_fwd(qa, ka, va, name="fox_fwd")
    token = hooks.after_attention([lse])
    lru, uconv, hs, gates = _lru_fwd(lxg, w["conv_w"], _tied(w["conv_b"], token), wab, bab, w["lam"], name="lru_fwd")
    xhat2, x2, rstd2 = _mix_out(fox, lru, wo, xhat1, w["ln1_g"], w["ln1_b"], w["ln2_g"], w["ln2_b"], name="mix_out")
    f2g, f2u, f2d = hooks.ffn2_weights(w, [rstd2])
    xb2, g2a, u2a, dy3p, dln3g, dln3b, loss = _ffn_fwd_loss(x2, f2g, f2u, f2d, w["ln3_g"], w["ln3_b"], target,
                                                            name="ffn2_fwd_loss")

    dx2, df2g, df2u, df2d = _ffn_bwd(dy3p, xb2, g2a, u2a, f2g, f2u, f2d, name="ffn2_bwd")
    token = hooks.ffn2_grads([df2g, df2u, df2d])
    dy2p, dln2g, dln2b, dlru, dwo, drep, doa = _mix_out_bwd(dx2, xhat2, rstd2, _tied(w["ln2_g"], token), fox, lru, wo,
                                                            name="mix_out_bwd")
    dlxg, dwab, dbab, dcw, dcb, dlam = _lru_bwd(dlru, lxg, uconv, hs, gates, w["conv_w"], wab, w["lam"], name="lru_bwd")
    dqa, dka, dva = _fox_bwd(qa, ka, va, doa, lse, drep, name="fox_bwd")
    dy1p, dwp, dln1g, dln1b, dbf = _proj_in_bwd(dqa, dka, dva, dlxg, fgb, xn1, dy2p, wp, xhat1, rstd1, w["ln1_g"],
                                                name="proj_in_bwd")
    small = dict(
        ln1_g=dln1g, ln1_b=dln1b, ln2_g=dln2g, ln2_b=dln2b, ln3_g=dln3g, ln3_b=dln3b,
        b_forget=dbf[:, :HEADS], conv_w=dcw[:CONV_K], conv_b=dcb,
        rg_wa=_block_diag_extract(dwab[:, :LRU_W]), rg_wx=_block_diag_extract(dwab[:, LRU_W:]),
        rg_ba=dbab[:, :LRU_W].reshape(HEADS, HEAD_DIM), rg_bx=dbab[:, LRU_W:].reshape(HEADS, HEAD_DIM),
        lru_lambda=dlam,
    )
    hooks.before_ffn1_bwd([dln1b])
    token = hooks.mixer_grads(dwp, dwo, small, loss)
    dx_a, *grads_a = _ffn_bwd(dy1p, xb0, g1a, u1a, w["f1g"], w["f1u"], f1d, token, name="ffn1_bwd_a", part=0)
    token = hooks.ffn1_grads(grads_a)
    dx, *grads_b = _ffn_bwd(dy1p, xb0, g1a, u1a, w["f1g"], w["f1u"], f1d, token, name="ffn1_bwd_b", part=1,
                            dx_init=dx_a)

    grads = dict(f1=(grads_a, grads_b), f2g=df2g, f2u=df2u, f2d=df2d, wp=dwp, wo=dwo, **small)
    return loss, dx, grads


MESH = pl.DeviceIdType.MESH
HBM_SPEC = pl.BlockSpec(memory_space=pl.ANY)
VMEM_SPEC = pl.BlockSpec(memory_space=pltpu.VMEM)


def _position():
    return lax.axis_index("x"), lax.axis_index("y"), lax.axis_index("c")


def _other_chips(x, y):
    return [(1 - x, y), (x, 1 - y), (1 - x, 1 - y)]


def _all_gather_bf16(shards, *, name):
    n = len(shards)

    def body(*refs):
        ins, outs, stages = refs[:n], refs[n:2 * n], refs[2 * n:3 * n]
        send_sems, recv_sems, local_sems = refs[3 * n:]
        x, y, c = _position()
        me, sibling = (x, y, c), (x, y, 1 - c)
        chips = _other_chips(x, y)

        def rows(k, px, py, pc):
            r = shards[k].shape[0]
            m = r // 2
            return outs[k].at[pl.ds(pl.multiple_of((2 * px + py) * r + pc * m, 16), m), :]

        def copy(k, idx, block, to, src=None):
            return pltpu.make_async_remote_copy(
                src_ref=rows(k, *block) if src is None else src, dst_ref=rows(k, *block),
                send_sem=send_sems.at[7 * k + idx], recv_sem=recv_sems.at[7 * k + idx],
                device_id=to, device_id_type=MESH)

        started = []
        mine = []
        for k in range(n):
            m = shards[k].shape[0] // 2
            stages[k][...] = ins[k][pl.ds(pl.multiple_of(c * m, 16), m), :].astype(stages[k].dtype)
            cp = pltpu.make_async_copy(stages[k], rows(k, *me), local_sems.at[k])
            cp.start()
            mine.append(cp)
            first = [copy(k, 0, me, sibling, src=stages[k])]
            first += [copy(k, 1 + j, me, (*chip, c), src=stages[k]) for j, chip in enumerate(chips)]
            for cp in first:
                cp.start()
            started += first
        for k in range(n):
            for j, chip in enumerate(chips):
                copy(k, 1 + j, (*chip, c), me).wait_recv()
                fwd = copy(k, 4 + j, (*chip, c), sibling)
                fwd.start()
                started.append(fwd)
        for k in range(n):
            copy(k, 0, sibling, me).wait_recv()
            for j, chip in enumerate(chips):
                copy(k, 4 + j, (*chip, 1 - c), me).wait_recv()
        for cp in started:
            cp.wait_send()
        for cp in mine:
            cp.wait()

    return _pcall(
        body, name=name,
        in_specs=[VMEM_SPEC] * n, out_specs=[HBM_SPEC] * n,
        out_shape=[jax.ShapeDtypeStruct((N_SHARD * s.shape[0], s.shape[1]), MXU_DTYPE) for s in shards],
        scratch_shapes=[pltpu.VMEM((s.shape[0] // 2, s.shape[1]), MXU_DTYPE) for s in shards]
        + [pltpu.SemaphoreType.DMA((7 * n,)), pltpu.SemaphoreType.DMA((7 * n,)), pltpu.SemaphoreType.DMA((n,))],
        compiler_params=pltpu.CompilerParams(vmem_limit_bytes=VMEM_LIMIT),
    )(*shards)


def _swap_halves(gs, *, name):
    n = len(gs)

    def body(*refs):
        ins, outs = refs[:n], refs[n:2 * n]
        send_sems, recv_sems = refs[2 * n:]
        x, y, c = _position()
        cps = []
        for k in range(n):
            m = gs[k].shape[1] // 2
            src = ins[k].at[:, pl.ds(pl.multiple_of((1 - c) * m, 16), m), :]
            cp = pltpu.make_async_remote_copy(src_ref=src, dst_ref=outs[k], send_sem=send_sems.at[k],
                                              recv_sem=recv_sems.at[k], device_id=(x, y, 1 - c), device_id_type=MESH)
            cp.start()
            cps.append(cp)
        for cp in cps:
            cp.wait()

    return _pcall(
        body, name=name, in_specs=[HBM_SPEC] * n, out_specs=[HBM_SPEC] * n,
        out_shape=[jax.ShapeDtypeStruct((g.shape[0], g.shape[1] // 2, g.shape[2]), g.dtype) for g in gs],
        scratch_shapes=[pltpu.SemaphoreType.DMA((n,)), pltpu.SemaphoreType.DMA((n,))],
    )(*gs)


def _add_halves(gs, recvs, *, name, tm=256):
    n = len(gs)
    _, r, cdim = gs[0].shape
    m = r // 2
    tm = min(tm, m)
    nb = m // tm
    c_idx = lax.axis_index("c").astype(jnp.int32).reshape(1)

    def body(c_ref, *refs):
        for k in range(n):
            refs[2 * n + k][...] = (refs[k][...].astype(f32) + refs[n + k][...].astype(f32)).astype(refs[2 * n + k].dtype)

    mine = pl.BlockSpec((None, tm, cdim), lambda j, i, c_ref: (j, c_ref[0] * nb + i, 0))
    half = pl.BlockSpec((None, tm, cdim), lambda j, i, c_ref: (j, i, 0))
    return _pcall(
        body, name=name,
        grid_spec=pltpu.PrefetchScalarGridSpec(
            num_scalar_prefetch=1, grid=(N_SHARD, nb),
            in_specs=[mine] * n + [half] * n, out_specs=[half] * n),
        out_shape=[jax.ShapeDtypeStruct((N_SHARD, m, cdim), g.dtype) for g in gs],
        compiler_params=_params(2),
    )(c_idx, *gs, *recvs)


def _scatter_partials(ps, *, name):
    n = len(ps)

    def body(*refs):
        ins, outs = refs[:n], refs[n:2 * n]
        send_sems, recv_sems = refs[2 * n:]
        x, y, c = _position()
        me_chip = 2 * x + y
        cps = []
        for k in range(n):
            for j, (px, py) in enumerate(_other_chips(x, y)):
                cp = pltpu.make_async_remote_copy(
                    src_ref=ins[k].at[2 * px + py], dst_ref=outs[k].at[me_chip],
                    send_sem=send_sems.at[3 * k + j], recv_sem=recv_sems.at[3 * k + j],
                    device_id=(px, py, c), device_id_type=MESH)
                cp.start()
                cps.append(cp)
        for cp in cps:
            cp.wait()

    return _pcall(
        body, name=name, in_specs=[HBM_SPEC] * n, out_specs=[HBM_SPEC] * n,
        out_shape=[jax.ShapeDtypeStruct(p.shape, p.dtype) for p in ps],
        scratch_shapes=[pltpu.SemaphoreType.DMA((3 * n,)), pltpu.SemaphoreType.DMA((3 * n,))],
    )(*ps)


def _sum_slabs(ps, qs, *, name, tm=128):
    n = len(qs)
    _, m, cdim = qs[0].shape
    tm = min(tm, m)
    nb = m // tm
    assert m % tm == 0, (m, tm)
    where = jnp.stack([2 * lax.axis_index("x") + lax.axis_index("y"), lax.axis_index("c")]).astype(jnp.int32)

    def body(w_ref, *refs):
        for k in range(n):
            own, q1, q2, q3 = (refs[4 * k + t][...].astype(f32) for t in range(4))
            refs[4 * n + k][...] = ((own + q1) + q2) + q3

    def slab(flip):
        return pl.BlockSpec((None, tm, cdim), lambda i, w_ref: (jnp.bitwise_xor(w_ref[0], flip), i, 0))

    operands = []
    for p, q in zip(ps, qs):
        operands += [p, q, q, q]
    return _pcall(
        body, name=name,
        grid_spec=pltpu.PrefetchScalarGridSpec(
            num_scalar_prefetch=1, grid=(nb,),
            in_specs=[slab(0), slab(2), slab(1), slab(3)] * n,
            out_specs=[pl.BlockSpec((tm, cdim), lambda i, w_ref: (w_ref[1] * nb + i, 0))] * n),
        out_shape=[jax.ShapeDtypeStruct((2 * m, cdim), f32) for _ in qs],
        compiler_params=_params(1),
    )(where, *operands)


def _join_halves(fs, *, name):
    n = len(fs)

    def body(*refs):
        outs = refs[n:2 * n]
        send_sems, recv_sems = refs[2 * n:]
        x, y, c = _position()
        cps = []
        for k in range(n):
            m = fs[k].shape[0] // 2
            half = outs[k].at[pl.ds(pl.multiple_of(c * m, 8), m), :]
            cp = pltpu.make_async_remote_copy(src_ref=half, dst_ref=half, send_sem=send_sems.at[k],
                                              recv_sem=recv_sems.at[k], device_id=(x, y, 1 - c), device_id_type=MESH)
            cp.start()
            cps.append(cp)
        for cp in cps:
            cp.wait()

    return _pcall(
        body, name=name, in_specs=[HBM_SPEC] * n, out_specs=[HBM_SPEC] * n,
        out_shape=[jax.ShapeDtypeStruct(f.shape, f.dtype) for f in fs],
        input_output_aliases={k: k for k in range(n)},
        scratch_shapes=[pltpu.SemaphoreType.DMA((n,)), pltpu.SemaphoreType.DMA((n,))],
    )(*fs)


def _all_reduce_small(v, after=None, *, name):
    r = v.shape[0]
    extra = [] if after is None else [after]

    def body(v_ref, *refs):
        out_ref, buf, send_sems, recv_sems, local_sem = refs[len(extra):]
        x, y, c = _position()
        me, sibling = (x, y, c), (x, y, 1 - c)
        chips = _other_chips(x, y)

        def rows(px, py, pc):
            return buf.at[pl.ds(pl.multiple_of((4 * px + 2 * py + pc) * r, 8), r), :]

        def copy(k, block, to, src=None):
            return pltpu.make_async_remote_copy(
                src_ref=rows(*block) if src is None else src, dst_ref=rows(*block),
                send_sem=send_sems.at[k], recv_sem=recv_sems.at[k], device_id=to, device_id_type=MESH)

        mine = pltpu.make_async_copy(v_ref, rows(*me), local_sem)
        mine.start()
        first = [copy(0, me, sibling, src=v_ref)]
        first += [copy(1 + j, me, (*chip, c), src=v_ref) for j, chip in enumerate(chips)]
        for cp in first:
            cp.start()
        passed = [copy(4 + j, (*chip, c), sibling) for j, chip in enumerate(chips)]
        for j, chip in enumerate(chips):
            copy(1 + j, (*chip, c), me).wait_recv()
            passed[j].start()
        copy(0, sibling, me).wait_recv()
        for j, chip in enumerate(chips):
            copy(4 + j, (*chip, 1 - c), me).wait_recv()
        for cp in first + passed:
            cp.wait_send()
        mine.wait()
        acc = buf[0:r, :]
        for d in range(1, N_DEV):
            acc = acc + buf[d * r:(d + 1) * r, :]
        out_ref[...] = acc

    return _pcall(
        body, name=name, in_specs=[VMEM_SPEC] + [HBM_SPEC] * len(extra), out_specs=VMEM_SPEC,
        out_shape=jax.ShapeDtypeStruct((r, LANES), f32),
        scratch_shapes=[pltpu.VMEM((N_DEV * r, LANES), f32), pltpu.SemaphoreType.DMA((7,)),
                        pltpu.SemaphoreType.DMA((7,)), pltpu.SemaphoreType.DMA],
    )(v, *extra)


SEM_SPEC = pl.BlockSpec(memory_space=pltpu.SEMAPHORE)
HBM_ONLY = pl.BlockSpec(memory_space=pltpu.HBM)
EFFECT = pltpu.SideEffectType.DATAFLOW_SIDE_EFFECTING


def _sends(copies):
    return copies[0] if isinstance(copies, tuple) else copies


def _arrivals(copies):
    return copies[1] if isinstance(copies, tuple) else copies


def _split_start(bufs, copies_fn, n_sems, *, name):
    n = len(bufs)

    def body(*refs):
        send_sems, recv_sems = refs[n], refs[n + 1]
        thru = refs[n + 2:2 * n + 2]
        token = refs[2 * n + 2]
        for cp in _sends(copies_fn(thru, send_sems, recv_sems)):
            cp.start()
        token[...] = jnp.zeros_like(token)

    outs = _pcall(
        body, name=name,
        out_shape=(pltpu.SemaphoreType.DMA((n_sems,)), pltpu.SemaphoreType.DMA((n_sems,)),
                   *[pltpu.HBM(b.shape, b.dtype) for b in bufs], jax.ShapeDtypeStruct((8, LANES), f32)),
        in_specs=[HBM_ONLY] * n,
        out_specs=(SEM_SPEC, SEM_SPEC, *[HBM_ONLY] * n, VMEM_SPEC),
        input_output_aliases={k: 2 + k for k in range(n)},
        compiler_params=pltpu.CompilerParams(has_side_effects=EFFECT),
    )(*[pltpu.with_memory_space_constraint(b, pltpu.HBM) for b in bufs])
    return outs[0], outs[1], list(outs[2:2 + n]), outs[2 + n]


def _split_wait(thru, send_sems, recv_sems, after, copies_fn, *, name):
    n = len(thru)

    def body(*refs):
        copies = copies_fn(refs[:n], refs[n], refs[n + 1])
        for cp in _sends(copies):
            cp.wait_send()
        for cp in _arrivals(copies):
            cp.wait_recv()

    return list(_pcall(
        body, name=name,
        out_shape=tuple(pltpu.HBM(b.shape, b.dtype) for b in thru),
        in_specs=[HBM_ONLY] * n + [SEM_SPEC, SEM_SPEC] + [HBM_SPEC] * len(after),
        out_specs=tuple([HBM_ONLY] * n),
        input_output_aliases={k: k for k in range(n)},
        compiler_params=pltpu.CompilerParams(has_side_effects=EFFECT),
    )(*thru, send_sems, recv_sems, *after))


def _scatter_copies(n):
    def copies(bufs, send_sems, recv_sems):
        x, y, c = _position()
        me_chip = 2 * x + y
        cps = []
        for k in range(n):
            for j, (px, py) in enumerate(_other_chips(x, y)):
                cps.append(pltpu.make_async_remote_copy(
                    src_ref=bufs[k].at[2 * px + py], dst_ref=bufs[n + k].at[me_chip],
                    send_sem=send_sems.at[3 * k + j], recv_sem=recv_sems.at[3 * k + j],
                    device_id=(px, py, c), device_id_type=MESH))
        return cps
    return copies


N_PEERS = N_DEV - 1


def _direct_copies(n):
    def copies(bufs, send_sems, recv_sems):
        x, y, c = _position()
        me_chip = 2 * x + y
        sends, arrivals = [], []
        for k in range(n):
            m = bufs[k].shape[1] // 2
            land = bufs[n + k]

            def rows(slab, half, k=k, m=m):
                start = half * m if isinstance(half, int) else pl.multiple_of(half * m, 16)
                return bufs[k].at[slab, pl.ds(start, m), :]

            def copy(src, slot, send_idx, recv_idx, to, k=k, land=land):
                return pltpu.make_async_remote_copy(
                    src_ref=src, dst_ref=land.at[slot], send_sem=send_sems.at[N_PEERS * k + send_idx],
                    recv_sem=recv_sems.at[N_PEERS * k + recv_idx], device_id=to, device_id_type=MESH)

            sends.append(copy(rows(me_chip, 1 - c), 0, 0, 0, (x, y, 1 - c)))
            arrivals.append(copy(rows(me_chip, c), 0, 0, 0, (x, y, 1 - c)))
            for t, (px, py) in enumerate(_other_chips(x, y)):
                for core in range(2):
                    sends.append(copy(rows(2 * px + py, core), 1 + 2 * t + c, 1 + 2 * t + core, 1 + 2 * t + c,
                                      (px, py, core)))
                    arrivals.append(copy(rows(me_chip, c), 1 + 2 * t + core, 1 + 2 * t + core, 1 + 2 * t + core,
                                         (px, py, core)))
        return sends, arrivals
    return copies


def _sum_direct(gs, lands, *, name, tm=128):
    n = len(gs)
    _, m, cdim = lands[0].shape
    tm = min(tm, m)
    nb = m // tm
    assert m % tm == 0, (m, tm)
    where = jnp.stack([2 * lax.axis_index("x") + lax.axis_index("y"), lax.axis_index("c")]).astype(jnp.int32)

    def body(w_ref, *refs):
        for k in range(n):
            acc = refs[2 * k][...].astype(f32)
            for slot in range(N_PEERS):
                acc = acc + refs[2 * k + 1][slot].astype(f32)
            refs[2 * n + k][...] = acc

    own = pl.BlockSpec((None, tm, cdim), lambda i, w_ref: (w_ref[0], w_ref[1] * nb + i, 0))
    landed = pl.BlockSpec((N_PEERS, tm, cdim), lambda i, w_ref: (0, i, 0))
    operands = []
    for g, land in zip(gs, lands):
        operands += [g, land]
    return _pcall(
        body, name=name,
        grid_spec=pltpu.PrefetchScalarGridSpec(
            num_scalar_prefetch=1, grid=(nb,), in_specs=[own, landed] * n,
            out_specs=[pl.BlockSpec((tm, cdim), lambda i, w_ref: (w_ref[1] * nb + i, 0))] * n),
        out_shape=[jax.ShapeDtypeStruct((2 * m, cdim), f32) for _ in gs],
        compiler_params=_params(1),
    )(where, *operands)


def _broadcast_copies(bufs, send_sems, recv_sems):
    v, land = bufs
    x, y, c = _position()

    def copy(slot, send_idx, recv_idx, to):
        return pltpu.make_async_remote_copy(src_ref=v, dst_ref=land.at[slot], send_sem=send_sems.at[send_idx],
                                            recv_sem=recv_sems.at[recv_idx], device_id=to, device_id_type=MESH)

    sends = [copy(0, 0, 0, (x, y, 1 - c))]
    arrivals = [copy(0, 0, 0, (x, y, 1 - c))]
    for t, (px, py) in enumerate(_other_chips(x, y)):
        for core in range(2):
            sends.append(copy(1 + 2 * t + c, 1 + 2 * t + core, 1 + 2 * t + c, (px, py, core)))
            arrivals.append(copy(1 + 2 * t + core, 1 + 2 * t + core, 1 + 2 * t + core, (px, py, core)))
    return sends, arrivals


def _sum_in_device_order(v, land, *, name):
    r, cdim = v.shape
    x, y, c = _position()
    slots, mine = [], []
    for d in range(N_DEV):
        dx, dy, dc = d // 4, (d // 2) % 2, d % 2
        fx, fy = jnp.bitwise_xor(dx, x), jnp.bitwise_xor(dy, y)
        t = jnp.where(fx == 1, jnp.where(fy == 1, 2, 0), 1)
        slots.append(jnp.where(jnp.logical_and(fx == 0, fy == 0), 0, 1 + 2 * t + dc))
        mine.append(jnp.logical_and(jnp.logical_and(fx == 0, fy == 0), dc == c))
    table = jnp.stack(slots + mine).astype(jnp.int32)

    def body(tab_ref, v_ref, *refs):
        out_ref = refs[N_DEV]
        acc = None
        for d in range(N_DEV):
            term = jnp.where(tab_ref[N_DEV + d] == 1, v_ref[...], refs[d][...])
            acc = term if acc is None else acc + term
        out_ref[...] = acc

    whole = pl.BlockSpec((r, cdim), lambda i, tab_ref: (0, 0))
    landed = [pl.BlockSpec((None, r, cdim), functools.partial(lambda i, tab_ref, d: (tab_ref[d], 0, 0), d=d))
              for d in range(N_DEV)]
    return _pcall(
        body, name=name,
        grid_spec=pltpu.PrefetchScalarGridSpec(num_scalar_prefetch=1, grid=(1,), in_specs=[whole] + landed,
                                               out_specs=whole),
        out_shape=jax.ShapeDtypeStruct((r, cdim), f32),
        compiler_params=_params(1),
    )(table, v, *[land] * N_DEV)


def _block_rows(buf, px, py, pc):
    m = buf.shape[0] // N_DEV
    return buf.at[pl.ds(pl.multiple_of((4 * px + 2 * py + pc) * m, 16), m), :]


def _gather_ici_copies(n):
    def copies(bufs, send_sems, recv_sems):
        x, y, c = _position()
        cps = []
        for k in range(n):
            rows = _block_rows(bufs[k], x, y, c)
            targets = [(x, y, 1 - c)] + [(px, py, c) for px, py in _other_chips(x, y)]
            for j, to in enumerate(targets):
                cps.append(pltpu.make_async_remote_copy(
                    src_ref=rows, dst_ref=rows, send_sem=send_sems.at[4 * k + j], recv_sem=recv_sems.at[4 * k + j],
                    device_id=to, device_id_type=MESH))
        return cps
    return copies


def _gather_d2d_copies(n):
    def copies(bufs, send_sems, recv_sems):
        x, y, c = _position()
        cps = []
        for k in range(n):
            for j, (px, py) in enumerate(_other_chips(x, y)):
                rows = _block_rows(bufs[k], px, py, c)
                cps.append(pltpu.make_async_remote_copy(
                    src_ref=rows, dst_ref=rows, send_sem=send_sems.at[3 * k + j], recv_sem=recv_sems.at[3 * k + j],
                    device_id=(x, y, 1 - c), device_id_type=MESH))
        return cps
    return copies


def _cast_halves(shards, after, *, name):
    n = len(shards)
    where = jnp.stack([2 * lax.axis_index("x") + lax.axis_index("y"), lax.axis_index("c")]).astype(jnp.int32)

    def body(w_ref, *refs):
        for k in range(n):
            refs[n + 1 + k][...] = refs[k][...].astype(refs[n + 1 + k].dtype)

    def half(s):
        return (s.shape[0] // 2, s.shape[1])

    return _pcall(
        body, name=name,
        grid_spec=pltpu.PrefetchScalarGridSpec(
            num_scalar_prefetch=1, grid=(1,),
            in_specs=[pl.BlockSpec(half(s), lambda i, w_ref: (w_ref[1], 0)) for s in shards] + [HBM_SPEC],
            out_specs=[pl.BlockSpec(half(s), lambda i, w_ref: (2 * w_ref[0] + w_ref[1], 0)) for s in shards]),
        out_shape=[jax.ShapeDtypeStruct((N_SHARD * s.shape[0], s.shape[1]), MXU_DTYPE) for s in shards],
        compiler_params=_params(1),
    )(where, *shards, after)


class _SplitGather:
    def __init__(self, shards, after, tag):
        self.tag = tag
        self.n = len(shards)
        halves = _cast_halves(shards, after, name=f"{tag}_cast")
        self.ici = _split_start(halves, _gather_ici_copies(self.n), 4 * self.n, name=f"{tag}_ici_start")
        self.token = self.ici[3]

    def forward(self, after):
        send_sems, recv_sems, thru, _ = self.ici
        landed = _split_wait(thru, send_sems, recv_sems, after, _gather_ici_copies(self.n), name=f"{self.tag}_ici_wait")
        self.d2d = _split_start(landed, _gather_d2d_copies(self.n), 3 * self.n, name=f"{self.tag}_d2d_start")
        return self.d2d[3]

    def finish(self, after):
        send_sems, recv_sems, thru, _ = self.d2d
        return _split_wait(thru, send_sems, recv_sems, after, _gather_d2d_copies(self.n), name=f"{self.tag}_d2d_wait")


class _Overlap(_NoOverlap):
    def __init__(self, late_shards, ffn2_shards, after, own_part, chip):
        self.late = _SplitGather(late_shards, after, "ag1")
        self.ffn2 = _SplitGather(ffn2_shards, self.late.token, "ag2")
        self.reduced = None
        self.ffn1_parts = []
        self.own_part = own_part
        self.chip = chip

    def start_token(self):
        return self.ffn2.token

    def ffn1_up(self, x, w):
        xb = self.own_part[0]
        return [xb] + _ffn_up_part(xb, w["f1g"], w["f1u"], self.chip, self.own_part[1:], None, name="ffn1_up_rest")

    def late_weights(self, w, after):
        token = self.late.forward(after)
        f1d, w_in, wo = self.late.finish([token])
        w_in = w_in.reshape(N_SHARD, D_MODEL, IN_SHARD)
        w_in = jnp.concatenate([w_in[j] for j in range(N_SHARD)], axis=1)
        return dict(f1d=f1d.reshape(N_SHARD, D_FF // N_SHARD, D_MODEL), wp=make_wp(w_in), wo=wo)

    def after_attention(self, after):
        return self.ffn2.forward(after)

    def ffn2_weights(self, w, after):
        full = self.ffn2.finish(after)
        fs = D_FF // N_SHARD
        return (full[0].reshape(N_SHARD, D_MODEL, fs), full[1].reshape(N_SHARD, D_MODEL, fs),
                full[2].reshape(N_SHARD, fs, D_MODEL))

    @staticmethod
    def _send_direct(grads, tag):
        lands = [lax.empty((N_PEERS, g.shape[1] // 2, g.shape[2]), g.dtype) for g in grads]
        return _split_start(list(grads) + lands, _direct_copies(len(grads)), N_PEERS * len(grads),
                            name=f"rs_direct_{tag}_start")

    def ffn2_grads(self, grads):
        self.scatter = self._send_direct(grads, "ffn2")
        return self.scatter[3]

    def ffn1_grads(self, grads):
        tag = "ffn1" + "ab"[len(self.ffn1_parts)]
        if not self.ffn1_parts:
            started = self._send_direct(grads, tag)
        else:
            recvs = _swap_halves(grads, name=f"rs_swap_{tag}")
            ps = list(_add_halves(grads[:2], recvs[:2], name=f"rs_add_{tag}_gu"))
            ps += list(_add_halves(grads[2:], recvs[2:], name=f"rs_add_{tag}_d"))
            lands = [lax.empty(p.shape, p.dtype) for p in ps]
            started = _split_start(ps + lands, _scatter_copies(3), 9, name=f"rs_scatter_{tag}_start")
        self.ffn1_parts.append((tag, started))
        return started[3]

    def ffn1_reduced(self, after):
        sums = []
        for direct, (tag, (send_sems, recv_sems, thru, _)) in zip((True, False), self.ffn1_parts):
            plan, add = (_direct_copies, _sum_direct) if direct else (_scatter_copies, _sum_slabs)
            done = _split_wait(thru, send_sems, recv_sems, after, plan(3), name=f"rs_{tag}_wait")
            sums += list(add(done[:2], done[3:5], name=f"rs_sum_{tag}_gu"))
            sums += list(add(done[2:3], done[5:], name=f"rs_sum_{tag}_d"))
        return sums

    def mixer_grads(self, dwp, dwo, small, loss):
        packed = jnp.concatenate([_pack_small(small), jnp.broadcast_to(loss, (8, LANES))], axis=0)
        land = lax.empty((N_PEERS,) + packed.shape, packed.dtype)
        self.small = _split_start([packed, land], _broadcast_copies, N_PEERS, name="ar_small_start")
        gwin = jnp.stack([dwp[:, j * IN_SHARD:(j + 1) * IN_SHARD] for j in range(N_SHARD)]).astype(GRAD_DTYPE)
        gwo = dwo.reshape(N_SHARD, D_MODEL // N_SHARD, D_MODEL).astype(GRAD_DTYPE)
        self.scatter_mix = self._send_direct([gwin, gwo], "mix")
        return self.small[3] + self.scatter_mix[3]

    def small_summed(self, after):
        send_sems, recv_sems, thru, _ = self.small
        packed, land = _split_wait(thru, send_sems, recv_sems, after, _broadcast_copies, name="ar_small_wait")
        summed = _sum_in_device_order(packed, land, name="ar_small_sum")
        return summed[:-8], summed[-8, 0]

    def mixer_reduced(self, after):
        send_sems, recv_sems, thru, _ = self.scatter_mix
        done = _split_wait(thru, send_sems, recv_sems, after, _direct_copies(2), name="rs_direct_mix_wait")
        return [_sum_direct([done[k]], [done[2 + k]], name=f"rs_sum_{tag}")[0] for k, tag in enumerate(["w_in", "w_out"])]

    def before_ffn1_bwd(self, after):
        send_sems, recv_sems, thru, _ = self.scatter
        n = len(thru) // 2
        done = _split_wait(thru, send_sems, recv_sems, after, _direct_copies(n), name="rs_direct_ffn2_wait")
        self.reduced = list(_sum_direct(done[:n], done[n:], name="rs_sum_ffn2"))


def _adamw(gs, ws, ms, vs, *, name, tm=256):
    n = len(gs)
    r, cdim = ws[0].shape[-2:]
    tm = r if tm is None else min(tm, r)
    assert r % tm == 0, (r, tm)
    nb = r // tm
    c1 = 1.0 / (1.0 - ADAM_B1 ** ADAM_STEP)
    c2 = 1.0 / (1.0 - ADAM_B2 ** ADAM_STEP)
    flat = pl.BlockSpec((tm, cdim), lambda i: (i, 0))

    g_ops, g_specs, g_where = [], [], []
    for g in gs:
        g_where.append(len(g_ops))
        if not isinstance(g, tuple):
            g_ops.append(g)
            g_specs.append(flat)
        elif g[2] == 1:
            g_ops += [g[0], g[1]]
            g_specs += [pl.BlockSpec((tm, cdim // 2), lambda i: (i, 0))] * 2
        else:
            g_ops += [g[0], g[1]]
            g_specs += [pl.BlockSpec((tm, cdim), lambda i: (jnp.minimum(i, nb // 2 - 1), 0)),
                        pl.BlockSpec((tm, cdim), lambda i: (jnp.maximum(i - nb // 2, 0), 0))]
    ng = len(g_ops)

    def gradient(refs, k):
        g, at = gs[k], g_where[k]
        if not isinstance(g, tuple):
            return refs[at][...]
        if g[2] == 1:
            return jnp.concatenate([refs[at][...], refs[at + 1][...]], axis=1)
        return jnp.where(pl.program_id(0) < nb // 2, refs[at][...], refs[at + 1][...])

    def body(*refs):
        rest = refs[ng:]
        for k in range(n):
            g = gradient(refs, k)
            w = rest[k][...]
            m = ADAM_B1 * rest[n + k][...] + (1.0 - ADAM_B1) * g
            v = ADAM_B2 * rest[2 * n + k][...] + (1.0 - ADAM_B2) * (g * g)
            rest[3 * n + k][...] = g
            rest[4 * n + k][...] = -ADAM_LR * ((m * c1) / (jnp.sqrt(v * c2) + ADAM_EPS) + ADAM_WD * w)
            rest[5 * n + k][...] = m
            rest[6 * n + k][...] = v

    like_w = flat if ws[0].ndim == 2 else pl.BlockSpec((None, tm, cdim), lambda i: (0, i, 0))
    outs = _pcall(
        body, name=name, grid=(nb,), in_specs=g_specs + [like_w] * (3 * n), out_specs=[like_w] * (4 * n),
        out_shape=[jax.ShapeDtypeStruct(ws[0].shape, f32)] * (4 * n),
        compiler_params=_params(1),
    )(*g_ops, *ws, *ms, *vs)
    return outs[:n], outs[n:2 * n], outs[2 * n:3 * n], outs[3 * n:]


BIG = ["ffn1_w_gate", "ffn1_w_up", "ffn1_w_down", "ffn2_w_gate", "ffn2_w_up", "ffn2_w_down"]
SMALL = ["ln1_g", "ln1_b", "b_forget", "conv_w", "conv_b", "rg_wa", "rg_ba", "rg_wx", "rg_bx", "lru_lambda",
         "ln2_g", "ln2_b", "ln3_g", "ln3_b"]
WEIGHTS = ["ffn1_w_gate", "ffn1_w_up", "ffn1_w_down", "ln1_g", "ln1_b", "w_in", "b_forget", "conv_w", "conv_b",
           "rg_wa", "rg_ba", "rg_wx", "rg_bx", "lru_lambda", "w_out", "ln2_g", "ln2_b",
           "ffn2_w_gate", "ffn2_w_up", "ffn2_w_down", "ln3_g", "ln3_b"]


def _pack_small(parts):
    rows = []
    for n in SMALL:
        flat = parts[n].reshape(-1)
        pad = (-flat.shape[0]) % LANES
        rows.append(jnp.pad(flat, (0, pad)).reshape(-1, LANES))
    packed = jnp.concatenate(rows, axis=0)
    return jnp.pad(packed, ((0, (-packed.shape[0]) % 8), (0, 0)))


def _unpack_small(packed, shapes):
    out, r0 = {}, 0
    for n in SMALL:
        size = math.prod(shapes[n])
        nr = -(-size // LANES)
        out[n] = packed[r0:r0 + nr].reshape(-1)[:size].reshape(shapes[n])
        r0 += nr
    return out


def kernel(x, ffn1_w_gate, ffn1_w_up, ffn1_w_down, ln1_g, ln1_b, w_in, b_forget, conv_w, conv_b, rg_wa, rg_ba, rg_wx, rg_bx, lru_lambda, w_out, ln2_g, ln2_b, ffn2_w_gate, ffn2_w_up, ffn2_w_down, ln3_g, ln3_b, loss_target, m_ffn1_w_gate, m_ffn1_w_up, m_ffn1_w_down, m_ln1_g, m_ln1_b, m_w_in, m_b_forget, m_conv_w, m_conv_b, m_rg_wa, m_rg_ba, m_rg_wx, m_rg_bx, m_lru_lambda, m_w_out, m_ln2_g, m_ln2_b, m_ffn2_w_gate, m_ffn2_w_up, m_ffn2_w_down, m_ln3_g, m_ln3_b, v_ffn1_w_gate, v_ffn1_w_up, v_ffn1_w_down, v_ln1_g, v_ln1_b, v_w_in, v_b_forget, v_conv_w, v_conv_b, v_rg_wa, v_rg_ba, v_rg_wx, v_rg_bx, v_lru_lambda, v_w_out, v_ln2_g, v_ln2_b, v_ffn2_w_gate, v_ffn2_w_up, v_ffn2_w_down, v_ln3_g, v_ln3_b):
    args = dict(locals())
    w = {n: args[n] for n in WEIGHTS}
    mom = {n: args["m_" + n] for n in WEIGHTS}
    var = {n: args["v_" + n] for n in WEIGHTS}
    chip = 2 * lax.axis_index("x") + lax.axis_index("y")

    first = _SplitGather([w[n][0] for n in BIG[:2]], b_forget, "ag0")
    chip_ref = jnp.reshape(chip, (1,)).astype(jnp.int32)
    own_part = _ffn_up_part(x[0], w[BIG[0]][0], w[BIG[1]][0], chip_ref, None, first.token, name="ffn1_up_own")
    fs = D_FF // N_SHARD
    full = dict(
        bfp=jnp.pad(b_forget, ((0, 0), (0, LANES - HEADS))),
        ln1_g=ln1_g, ln1_b=ln1_b, ln2_g=ln2_g, ln2_b=ln2_b, ln3_g=ln3_g, ln3_b=ln3_b,
        conv_b=conv_b, rg_wa=rg_wa[0], rg_wx=rg_wx[0], rg_ba=rg_ba[0], rg_bx=rg_bx[0], lam=lru_lambda,
    )
    cw_place = lax.dynamic_update_slice(jnp.zeros((8, LRU_W), f32), conv_w[0] * 0.5, (0, chip * (LRU_W // N_SHARD)))
    cw_full = _all_reduce_small(cw_place.reshape(-1, LANES), own_part[3], name="ag_conv_w")
    full["conv_w"] = cw_full.reshape(8, LRU_W)[:CONV_K]
    landed = first.forward([cw_full])

    hooks = _Overlap([w["ffn1_w_down"][0], w["w_in"][0], w["w_out"][0]], [w[n][0] for n in BIG[3:]], landed,
                     own_part, chip_ref)
    g1 = first.finish([hooks.start_token()])
    full.update(f1g=g1[0].reshape(N_SHARD, D_MODEL, fs), f1u=g1[1].reshape(N_SHARD, D_MODEL, fs))
    loss_rep, dx, g = _local_step(x[0], loss_target[0], full, hooks)

    token1 = hooks.ffn1_grads(g["f1"][1])
    red = _join_halves(hooks.reduced + hooks.mixer_reduced([token1]), name="rs_join_rest")
    grads = dict(zip(BIG[3:] + ["w_in", "w_out"], red))

    small_sum, loss = hooks.small_summed(red)
    small_shapes = {n: w[n].shape for n in SMALL}
    small_shapes["conv_w"] = (1, CONV_K, LRU_W)
    gs_red = _unpack_small(small_sum, small_shapes)
    gs_red["conv_w"] = lax.dynamic_slice(gs_red["conv_w"], (0, 0, chip * (LRU_W // N_SHARD)),
                                         (1, CONV_K, LRU_W // N_SHARD))
    grads.update(gs_red)

    delta, new_m, new_v = {}, {}, {}

    def adamw(names, name, **kw):
        g3, d, nm, nv = _adamw([grads[n] for n in names], [w[n] for n in names], [mom[n] for n in names],
                               [var[n] for n in names], name=name, **kw)
        for i, n in enumerate(names):
            grads[n], delta[n], new_m[n], new_v[n] = g3[i], d[i], nm[i], nv[i]

    adamw(BIG[3:], "adamw_ffn2", tm=128)
    adamw(["w_in"], "adamw_w_in")
    adamw(["w_out"], "adamw_w_out")
    shard_shapes = {n: w[n].shape for n in SMALL}
    _, d, nm, nv = _adamw([_pack_small({n: grads[n] for n in SMALL})], [_pack_small({n: w[n] for n in SMALL})],
                          [_pack_small({n: mom[n] for n in SMALL})], [_pack_small({n: var[n] for n in SMALL})],
                          name="adamw_small", tm=None)
    for dst, packed in ((delta, d[0]), (new_m, nm[0]), (new_v, nv[0])):
        dst.update(_unpack_small(packed, shard_shapes))

    worked = [new_v["ffn2_w_down"], new_v["w_in"], new_v["w_out"], nv[0]]
    ga, ua, da, gb, ub, db = _join_halves(hooks.ffn1_reduced(worked), name="rs_join_ffn1")
    grads.update(ffn1_w_gate=(ga, gb, 1), ffn1_w_up=(ua, ub, 1), ffn1_w_down=(da, db, 0))
    adamw(BIG[:3], "adamw_ffn1", tm=128)

    def shaped(tree, n):
        return tree[n].reshape(w[n].shape)

    return (loss, dx[None], *[shaped(grads, n) for n in WEIGHTS], *[shaped(delta, n) for n in WEIGHTS],
            *[shaped(new_m, n) for n in WEIGHTS], *[shaped(new_v, n) for n in WEIGHTS])
```
